```python
import math
import jax, jax.numpy as jnp
from jax import lax
import numpy as np

D_MODEL = 1024
BATCH = 8
SEQ = 4096
DEPTH = 1

HEAD_DIM = 64
HEADS_PER_GROUP = 8
ATTN_GROUPS = ((128, 1), (512, 4), (2048, 16))
N_GROUPS = len(ATTN_GROUPS)
N_ATTN_HEADS = N_GROUPS * HEADS_PER_GROUP
ATTN_WIDTH = HEADS_PER_GROUP * HEAD_DIM
QKV_WIDTH = N_GROUPS * 3 * ATTN_WIDTH
BLOCK = 128
POOL_WINDOWS = (2, 4, 8, 16)
POOL_GROUPS = len(POOL_WINDOWS)
POOL_WIDTH = D_MODEL // 2
PGW = POOL_WIDTH // POOL_GROUPS
NUM_BUCKETS = 32
MAX_DISTANCE = 2048
EPS = 1e-6
SPLIT_SIZES = (QKV_WIDTH, ATTN_WIDTH, POOL_WIDTH, POOL_WIDTH, D_MODEL, D_MODEL)
SPLIT_POINTS = (QKV_WIDTH,
                QKV_WIDTH + ATTN_WIDTH,
                QKV_WIDTH + ATTN_WIDTH + POOL_WIDTH,
                QKV_WIDTH + ATTN_WIDTH + 2 * POOL_WIDTH,
                QKV_WIDTH + ATTN_WIDTH + 2 * POOL_WIDTH + D_MODEL)
IN_WIDTH = QKV_WIDTH + ATTN_WIDTH + 2 * POOL_WIDTH + 2 * D_MODEL

kernel_name = "hybrid_dilated_attn_pool_gated_block"


def rmsnorm(x, g):
    xf = x.astype(jnp.float32)
    y = xf * lax.rsqrt(jnp.mean(xf * xf, axis=-1, keepdims=True) + EPS)
    return (y * g.astype(jnp.float32)).astype(x.dtype)


def t5_bucket(n):
    max_exact = NUM_BUCKETS // 2
    nf = jnp.maximum(n, 1).astype(jnp.float32)
    large = max_exact + (jnp.log(nf / max_exact) / math.log(MAX_DISTANCE / max_exact)
                         * (NUM_BUCKETS - max_exact)).astype(jnp.int32)
    large = jnp.minimum(large, NUM_BUCKETS - 1)
    return jnp.where(n < max_exact, n, large)


def to_sub(t, dil):
    B, S = t.shape[:2]
    L = S // dil
    t = t.reshape((B, L, dil) + t.shape[2:])
    t = jnp.moveaxis(t, 2, 1)
    return t.reshape((B * dil, L) + t.shape[3:])


def from_sub(t, B, dil):
    L = t.shape[1]
    t = t.reshape((B, dil, L) + t.shape[2:])
    t = jnp.moveaxis(t, 1, 2)
    return t.reshape((B, L * dil) + t.shape[3:])


def dilated_window_attention(q, k, v, dil, n_back, bias_g):
    B, S, H, Dh = q.shape
    L = S // dil
    nb = -(-L // BLOCK)
    pad = nb * BLOCK - L
    Bd = B * dil

    def sub(t):
        return jnp.pad(to_sub(t, dil), ((0, 0), (0, pad), (0, 0), (0, 0)))

    def band(t):
        tp = jnp.pad(t, ((0, 0), (BLOCK, 0), (0, 0), (0, 0)))
        prev = tp[:, :-BLOCK].reshape(Bd, nb, BLOCK, H, Dh)
        cur = t.reshape(Bd, nb, BLOCK, H, Dh)
        return jnp.concatenate([prev, cur], axis=2)

    qb = sub(q).reshape(Bd, nb, BLOCK, H, Dh)
    kb = band(sub(k))
    vb = band(sub(v))

    i = jnp.arange(BLOCK)[:, None]
    j = jnp.arange(2 * BLOCK)[None, :]
    dist = BLOCK + i - j
    band_ok = (dist >= 0) & (dist <= n_back)
    key_ok = (jnp.arange(nb)[:, None, None] * BLOCK - BLOCK + j[None]) >= 0
    mask = band_ok[None] & key_ok
    bucket = t5_bucket(jnp.clip(dist, 0, n_back) * dil)
    bias = jnp.transpose(bias_g[bucket].astype(jnp.float32), (2, 0, 1))

    logits = jnp.einsum('znqhd,znkhd->znhqk', qb, kb).astype(jnp.float32) * (HEAD_DIM ** -0.5)
    logits = jnp.where(mask[None, :, None], logits + bias[None, None], -jnp.inf)
    m = jnp.max(logits, axis=-1, keepdims=True)
    p = jnp.exp(logits - m)
    denom = jnp.sum(p, axis=-1)
    o = jnp.einsum('znhqk,znkhd->znqhd', p.astype(vb.dtype), vb).astype(jnp.float32)
    o = o / jnp.moveaxis(denom, 2, 3)[..., None]
    lse = jnp.moveaxis(m[..., 0] + jnp.log(denom), 2, 3)

    o = o.reshape(Bd, nb * BLOCK, H, Dh)[:, :L]
    lse = lse.reshape(Bd, nb * BLOCK, H)[:, :L]
    return from_sub(o, B, dil), from_sub(lse, B, dil)


def multiscale_pool(u):
    B, S, C = u.shape
    uf = u.astype(jnp.float32)
    csp = jnp.pad(jnp.cumsum(uf, axis=1), ((0, 0), (1, 0), (0, 0)))
    t = jnp.arange(S)
    outs = []
    for g, win in enumerate(POOL_WINDOWS):
        cg = csp[:, :, g * PGW:(g + 1) * PGW]
        lo = jnp.maximum(t + 1 - win, 0)
        s = cg[:, 1:] - cg[:, lo]
        cnt = jnp.minimum(t + 1, win).astype(jnp.float32)
        outs.append(s / cnt[None, :, None])
    return jnp.concatenate(outs, axis=-1) - uf


def _fwd_setup_inputs(seed: int = 0) -> dict:
    key = jax.random.key(seed)
    ks = jax.random.split(key, 14)
    f32 = jnp.float32
    nrm = lambda k, shape, s: (jax.random.normal(k, shape, f32) * s).astype(f32)
    return {
        "x": nrm(ks[0], (BATCH, SEQ, D_MODEL), 1.0),
        "c": nrm(ks[1], (BATCH, D_MODEL), 1.0),
        "norm_g": 1.0 + nrm(ks[2], (DEPTH, D_MODEL), 0.05),
        "w_ada": nrm(ks[3], (DEPTH, D_MODEL, 3 * D_MODEL), 0.5 * D_MODEL ** -0.5),
        "b_ada": nrm(ks[4], (DEPTH, 3 * D_MODEL), 0.01),
        "w_in": nrm(ks[5], (DEPTH, D_MODEL, IN_WIDTH), D_MODEL ** -0.5),
        "pool_w": nrm(ks[6], (DEPTH, POOL_GROUPS, PGW, PGW), PGW ** -0.5),
        "pool_scale": 1.0 + nrm(ks[7], (DEPTH, POOL_WIDTH), 0.1),
        "w_attn_br": nrm(ks[8], (DEPTH, ATTN_WIDTH, D_MODEL), ATTN_WIDTH ** -0.5),
        "w_pool_br": nrm(ks[9], (DEPTH, POOL_WIDTH, D_MODEL), POOL_WIDTH ** -0.5),
        "w_out": nrm(ks[10], (DEPTH, D_MODEL, D_MODEL), D_MODEL ** -0.5),
        "rel_bias": nrm(ks[11], (NUM_BUCKETS, N_ATTN_HEADS), 0.5),
        "final_g": 1.0 + nrm(ks[12], (D_MODEL,), 0.05),
    }


def _fwd_reference(x, c, norm_g, w_ada, b_ada, w_in, pool_w, pool_scale, w_attn_br, w_pool_br, w_out, rel_bias, final_g):
    B, S, D = x.shape
    for l in range(DEPTH):
        mod = c @ w_ada[l] + b_ada[l]
        shift, scale, gate = jnp.split(mod, 3, axis=-1)
        h = rmsnorm(x, norm_g[l]) * (1.0 + scale[:, None]) + shift[:, None]

        proj = h @ w_in[l]
        qkv, z_attn, u_pool, z_pool, g_attn, g_pool = jnp.split(proj, SPLIT_POINTS, axis=-1)
        qkv = qkv.reshape(B, S, N_GROUPS, 3, HEADS_PER_GROUP, HEAD_DIM)

        outs, lses = [], []
        for gi, (win, dil) in enumerate(ATTN_GROUPS):
            bias_g = rel_bias[:, gi * HEADS_PER_GROUP:(gi + 1) * HEADS_PER_GROUP]
            o, lse = dilated_window_attention(qkv[:, :, gi, 0], qkv[:, :, gi, 1], qkv[:, :, gi, 2],
                                              dil, win // dil, bias_g)
            outs.append(o)
            lses.append(lse)
        wts = jax.nn.softmax(jnp.stack(lses, axis=0), axis=0)
        attn = jnp.sum(wts[..., None] * jnp.stack(outs, axis=0), axis=0)
        attn = attn.reshape(B, S, ATTN_WIDTH).astype(x.dtype)
        y_attn = (attn * jax.nn.silu(z_attn)) @ w_attn_br[l]

        pooled = multiscale_pool(u_pool).reshape(B, S, POOL_GROUPS, PGW)
        mixed = jnp.einsum('bsgc,gce->bsge', pooled, pool_w[l].astype(jnp.float32))
        mixed = (mixed.reshape(B, S, POOL_WIDTH) * pool_scale[l]).astype(x.dtype)
        y_pool = (mixed * jax.nn.silu(z_pool)) @ w_pool_br[l]

        merged = jax.nn.sigmoid(g_attn) * y_attn + jax.nn.sigmoid(g_pool) * y_pool
        x = x + gate[:, None] * (merged @ w_out[l])
    return rmsnorm(x, final_g)


import jax as _jax
import jax.numpy as _jnp

TWIN_FORMAT = 'train_step'
FWD_PARAMS = ['x', 'c', 'norm_g', 'w_ada', 'b_ada', 'w_in', 'pool_w', 'pool_scale', 'w_attn_br', 'w_pool_br', 'w_out', 'rel_bias', 'final_g']
TWIN_WEIGHTS = ['norm_g', 'w_ada', 'b_ada', 'w_in', 'pool_w', 'pool_scale', 'w_attn_br', 'w_pool_br', 'w_out', 'rel_bias', 'final_g']
TWIN_DIFF_INPUT = 'x'
TWIN_INPUTS = ['x', 'c', 'norm_g', 'w_ada', 'b_ada', 'w_in', 'pool_w', 'pool_scale', 'w_attn_br', 'w_pool_br', 'w_out', 'rel_bias', 'final_g', 'loss_target', 'm_norm_g', 'm_w_ada', 'm_b_ada', 'm_w_in', 'm_pool_w', 'm_pool_scale', 'm_w_attn_br', 'm_w_pool_br', 'm_w_out', 'm_rel_bias', 'm_final_g', 'v_norm_g', 'v_w_ada', 'v_b_ada', 'v_w_in', 'v_pool_w', 'v_pool_scale', 'v_w_attn_br', 'v_w_pool_br', 'v_w_out', 'v_rel_bias', 'v_final_g']
TWIN_OUTPUTS = ['loss', 'grad_x', 'grad_norm_g', 'grad_w_ada', 'grad_b_ada', 'grad_w_in', 'grad_pool_w', 'grad_pool_scale', 'grad_w_attn_br', 'grad_w_pool_br', 'grad_w_out', 'grad_rel_bias', 'grad_final_g', 'delta_norm_g', 'delta_w_ada', 'delta_b_ada', 'delta_w_in', 'delta_pool_w', 'delta_pool_scale', 'delta_w_attn_br', 'delta_w_pool_br', 'delta_w_out', 'delta_rel_bias', 'delta_final_g', 'new_m_norm_g', 'new_m_w_ada', 'new_m_b_ada', 'new_m_w_in', 'new_m_pool_w', 'new_m_pool_scale', 'new_m_w_attn_br', 'new_m_w_pool_br', 'new_m_w_out', 'new_m_rel_bias', 'new_m_final_g', 'new_v_norm_g', 'new_v_w_ada', 'new_v_b_ada', 'new_v_w_in', 'new_v_pool_w', 'new_v_pool_scale', 'new_v_w_attn_br', 'new_v_w_pool_br', 'new_v_w_out', 'new_v_rel_bias', 'new_v_final_g']
TWIN_LEAF_KINDS = {'loss': 'loss', 'grad_x': 'grad_x', 'grad_norm_g': 'grad_w', 'grad_w_ada': 'grad_w', 'grad_b_ada': 'grad_w', 'grad_w_in': 'grad_w', 'grad_pool_w': 'grad_w', 'grad_pool_scale': 'grad_w', 'grad_w_attn_br': 'grad_w', 'grad_w_pool_br': 'grad_w', 'grad_w_out': 'grad_w', 'grad_rel_bias': 'grad_w', 'grad_final_g': 'grad_w', 'delta_norm_g': 'delta_w', 'delta_w_ada': 'delta_w', 'delta_b_ada': 'delta_w', 'delta_w_in': 'delta_w', 'delta_pool_w': 'delta_w', 'delta_pool_scale': 'delta_w', 'delta_w_attn_br': 'delta_w', 'delta_w_pool_br': 'delta_w', 'delta_w_out': 'delta_w', 'delta_rel_bias': 'delta_w', 'delta_final_g': 'delta_w', 'new_m_norm_g': 'new_m', 'new_m_w_ada': 'new_m', 'new_m_b_ada': 'new_m', 'new_m_w_in': 'new_m', 'new_m_pool_w': 'new_m', 'new_m_pool_scale': 'new_m', 'new_m_w_attn_br': 'new_m', 'new_m_w_pool_br': 'new_m', 'new_m_w_out': 'new_m', 'new_m_rel_bias': 'new_m', 'new_m_final_g': 'new_m', 'new_v_norm_g': 'new_v', 'new_v_w_ada': 'new_v', 'new_v_b_ada': 'new_v', 'new_v_w_in': 'new_v', 'new_v_pool_w': 'new_v', 'new_v_pool_scale': 'new_v', 'new_v_w_attn_br': 'new_v', 'new_v_w_pool_br': 'new_v', 'new_v_w_out': 'new_v', 'new_v_rel_bias': 'new_v', 'new_v_final_g': 'new_v'}


def _forward(args):
    return _fwd_reference(*[args[k] for k in FWD_PARAMS])


def _output_shape():
    out = _jax.eval_shape(lambda: _forward(_fwd_setup_inputs(0)))
    return out.shape, out.dtype

N_MICROBATCH = 1
ADAM_LR = 0.001
ADAM_B1 = 0.9
ADAM_B2 = 0.999
ADAM_EPS = 1e-08
ADAM_WD = 0.01
ADAM_STEP = 10
PER_EXAMPLE_BATCH_AXIS = {'x': 0, 'c': 0, 'loss_target': 0}
SHARED_INPUTS = []
_WEIGHT_DTYPES = {'norm_g': _jnp.float32, 'w_ada': _jnp.float32, 'b_ada': _jnp.float32, 'w_in': _jnp.float32, 'pool_w': _jnp.float32, 'pool_scale': _jnp.float32, 'w_attn_br': _jnp.float32, 'w_pool_br': _jnp.float32, 'w_out': _jnp.float32, 'rel_bias': _jnp.float32, 'final_g': _jnp.float32}
MOMENT_SCALE = {'norm_g': 6.277985e-02, 'w_ada': 7.448304e-02, 'b_ada': 7.333678e-02, 'w_in': 2.281004e-02, 'pool_w': 5.284179e-02, 'pool_scale': 6.571819e-02, 'w_attn_br': 1.868252e-02, 'w_pool_br': 3.669263e-02, 'w_out': 4.055896e-02, 'rel_bias': 1.087183e-02, 'final_g': 3.208491e+01}


def _to_microbatches(a, axis):
    t = _jnp.moveaxis(a, axis, 0)
    t = t.reshape((N_MICROBATCH, t.shape[0] // N_MICROBATCH) + t.shape[1:])
    return _jnp.moveaxis(t, 1, axis + 1)


def setup_inputs(seed: int = 0) -> dict:
    inp = _fwd_setup_inputs(seed)
    key = _jax.random.fold_in(_jax.random.key(seed), 7919)
    shape, _ = _output_shape()
    out = dict(inp)
    out["loss_target"] = _jax.random.normal(_jax.random.fold_in(key, 0), shape, _jnp.float32)
    for i, name in enumerate(TWIN_WEIGHTS):
        w = inp[name].astype(_jnp.float32)
        if MOMENT_SCALE is None:
            s = _jnp.sqrt(_jnp.mean(_jnp.square(w)) + 1e-30)
        else:
            s = MOMENT_SCALE[name]
        km, kv = _jax.random.split(_jax.random.fold_in(key, i + 1))
        out[name] = w
        out["m_" + name] = s * _jax.random.normal(km, w.shape, _jnp.float32)
        out["v_" + name] = (s * s) * _jax.random.uniform(kv, w.shape, _jnp.float32, 0.5, 1.5)
    if N_MICROBATCH > 1:
        for name, axis in PER_EXAMPLE_BATCH_AXIS.items():
            out[name] = _to_microbatches(out[name], axis)
    return {'x': out['x'], 'c': out['c'], 'norm_g': out['norm_g'], 'w_ada': out['w_ada'], 'b_ada': out['b_ada'], 'w_in': out['w_in'], 'pool_w': out['pool_w'], 'pool_scale': out['pool_scale'], 'w_attn_br': out['w_attn_br'], 'w_pool_br': out['w_pool_br'], 'w_out': out['w_out'], 'rel_bias': out['rel_bias'], 'final_g': out['final_g'], 'loss_target': out['loss_target'], 'm_norm_g': out['m_norm_g'], 'm_w_ada': out['m_w_ada'], 'm_b_ada': out['m_b_ada'], 'm_w_in': out['m_w_in'], 'm_pool_w': out['m_pool_w'], 'm_pool_scale': out['m_pool_scale'], 'm_w_attn_br': out['m_w_attn_br'], 'm_w_pool_br': out['m_w_pool_br'], 'm_w_out': out['m_w_out'], 'm_rel_bias': out['m_rel_bias'], 'm_final_g': out['m_final_g'], 'v_norm_g': out['v_norm_g'], 'v_w_ada': out['v_w_ada'], 'v_b_ada': out['v_b_ada'], 'v_w_in': out['v_w_in'], 'v_pool_w': out['v_pool_w'], 'v_pool_scale': out['v_pool_scale'], 'v_w_attn_br': out['v_w_attn_br'], 'v_w_pool_br': out['v_w_pool_br'], 'v_w_out': out['v_w_out'], 'v_rel_bias': out['v_rel_bias'], 'v_final_g': out['v_final_g']}


def _loss(weights, diff, rest, loss_target):
    with _jax.named_scope("forward"):
        args = {**rest, TWIN_DIFF_INPUT: diff, **{k: w.astype(_WEIGHT_DTYPES[k]) for k, w in weights.items()}}
        y = _forward(args)
    with _jax.named_scope("loss_head"):
        err = _jnp.square(y.astype(_jnp.float32) - loss_target)
        return 0.5 * _jnp.sum(_jnp.mean(err, axis=-1)) if err.ndim else 0.5 * err


def _adamw(w, g, m, v):
    m = ADAM_B1 * m + (1.0 - ADAM_B1) * g
    v = ADAM_B2 * v + (1.0 - ADAM_B2) * _jnp.square(g)
    m_hat = m / (1.0 - ADAM_B1 ** ADAM_STEP)
    v_hat = v / (1.0 - ADAM_B2 ** ADAM_STEP)
    delta = -ADAM_LR * (m_hat / (_jnp.sqrt(v_hat) + ADAM_EPS) + ADAM_WD * w)
    return delta, m, v


def reference(x, c, norm_g, w_ada, b_ada, w_in, pool_w, pool_scale, w_attn_br, w_pool_br, w_out, rel_bias, final_g, loss_target, m_norm_g, m_w_ada, m_b_ada, m_w_in, m_pool_w, m_pool_scale, m_w_attn_br, m_w_pool_br, m_w_out, m_rel_bias, m_final_g, v_norm_g, v_w_ada, v_b_ada, v_w_in, v_pool_w, v_pool_scale, v_w_attn_br, v_w_pool_br, v_w_out, v_rel_bias, v_final_g):
    given = dict(x=x, c=c, norm_g=norm_g, w_ada=w_ada, b_ada=b_ada, w_in=w_in, pool_w=pool_w, pool_scale=pool_scale, w_attn_br=w_attn_br, w_pool_br=w_pool_br, w_out=w_out, rel_bias=rel_bias, final_g=final_g, loss_target=loss_target, m_norm_g=m_norm_g, m_w_ada=m_w_ada, m_b_ada=m_b_ada, m_w_in=m_w_in, m_pool_w=m_pool_w, m_pool_scale=m_pool_scale, m_w_attn_br=m_w_attn_br, m_w_pool_br=m_w_pool_br, m_w_out=m_w_out, m_rel_bias=m_rel_bias, m_final_g=m_final_g, v_norm_g=v_norm_g, v_w_ada=v_w_ada, v_b_ada=v_b_ada, v_w_in=v_w_in, v_pool_w=v_pool_w, v_pool_scale=v_pool_scale, v_w_attn_br=v_w_attn_br, v_w_pool_br=v_w_pool_br, v_w_out=v_w_out, v_rel_bias=v_rel_bias, v_final_g=v_final_g)
    weights = {n: given[n] for n in TWIN_WEIGHTS}
    shared = {n: given[n] for n in SHARED_INPUTS}
    per_example = {n: given[n] for n in ['x', 'c']}
    grad_fn = _jax.value_and_grad(_loss, argnums=(0, 1))

    def one_microbatch(ex, loss_target):
        ex = dict(ex)
        diff = ex.pop(TWIN_DIFF_INPUT)
        return grad_fn(weights, diff, {**shared, **ex}, loss_target)

    if N_MICROBATCH == 1:
        loss, (grad_w, grad_x) = one_microbatch(per_example, given["loss_target"])
    else:
        def body(carry, xs):
            loss_sum, grad_sum = carry
            l_k, (gw_k, gx_k) = one_microbatch(xs[0], xs[1])
            with _jax.named_scope("update"):
                return (loss_sum + l_k, _jax.tree.map(_jnp.add, grad_sum, gw_k)), gx_k

        init = (_jnp.zeros((), _jnp.float32), _jax.tree.map(_jnp.zeros_like, weights))
        (loss, grad_w), grad_x = _jax.lax.scan(body, init, (per_example, given["loss_target"]))
    with _jax.named_scope("update"):
        delta_w, new_m, new_v = {}, {}, {}
        for n in TWIN_WEIGHTS:
            delta_w[n], new_m[n], new_v[n] = _adamw(weights[n], grad_w[n], given["m_" + n], given["v_" + n])
    return (loss, grad_x, *[grad_w[n] for n in TWIN_WEIGHTS], *[delta_w[n] for n in TWIN_WEIGHTS],
            *[new_m[n] for n in TWIN_WEIGHTS], *[new_v[n] for n in TWIN_WEIGHTS])
```

```python
import functools
import math

import numpy as np
import jax
import jax.numpy as jnp
from jax import lax
from jax.experimental import pallas as pl
from jax.experimental.pallas import tpu as pltpu

F32 = jnp.float32
BF16 = jnp.bfloat16

D = 1024
HD = 64
NH = 8
AW = NH * HD
GROUPS = ((128, 1), (512, 4), (2048, 16))
NG = len(GROUPS)
BLK = 128
GW = 3 * AW
QKV_W = NG * GW
REST_W = 3584
IN_W = QKV_W + REST_W
CB = 512
NCB = IN_W // CB
NCB_QKV = QKV_W // CB
POOL_WINDOWS = (2, 4, 8, 16)
PGW = 128
HALO = 16
NUM_BUCKETS = 32
MAX_DISTANCE = 2048
EPS = 1e-6
NEG = -1e30
N_SHARD = 4
VMEM_LIMIT = 56 * 1024 * 1024

ADAM_LR = 0.001
ADAM_B1 = 0.9
ADAM_B2 = 0.999
ADAM_EPS = 1e-08
ADAM_WD = 0.01
ADAM_STEP = 10

PK_BADA, PK_NORMG, PK_FINALG, PK_PSCALE, PK_RELB, PK_LOSS, PK_POOLW, PK_ROWS = 0, 24, 32, 40, 44, 50, 56, 568

ANY = pl.BlockSpec(memory_space=pl.ANY)
MESH = pl.DeviceIdType.MESH


def _params(*sem):
    return pltpu.CompilerParams(dimension_semantics=sem, vmem_limit_bytes=VMEM_LIMIT)


def _sds(shape, dtype=F32):
    return jax.ShapeDtypeStruct(shape, dtype)


def _dot(a, b):
    return jnp.dot(a, b, preferred_element_type=F32)


def _dot_nt(a, b):
    return lax.dot_general(a, b, (((1,), (1,)), ((), ())), preferred_element_type=F32)


def _dot_tn(a, b):
    return lax.dot_general(a, b, (((0,), (0,)), ((), ())), preferred_element_type=F32)


def _sigmoid(z):
    return 1.0 / (1.0 + jnp.exp(-z))


def _bucket_tables():
    i = np.arange(BLK)[:, None]
    j = np.arange(2 * BLK)[None, :]
    dist = BLK + i - j
    valid = (dist >= 0) & (dist <= BLK)
    tabs = []
    for _, dil in GROUPS:
        n = (np.clip(dist, 0, BLK) * dil).astype(np.int32)
        max_exact = NUM_BUCKETS // 2
        nf = np.maximum(n, 1).astype(np.float32)
        large = max_exact + (np.log(nf / np.float32(max_exact)) / np.float32(math.log(MAX_DISTANCE / max_exact))
                             * np.float32(NUM_BUCKETS - max_exact)).astype(np.int32)
        large = np.minimum(large, NUM_BUCKETS - 1)
        bucket = np.where(n < max_exact, n, large)
        tabs.append(np.where(valid, bucket, -1).astype(np.int32))
    return np.stack(tabs)


def _bias_table(rel_bias, buckets):
    def body(rb_ref, bk_ref, out_ref):
        gh = pl.program_id(0)
        bk = bk_ref[...]
        acc = jnp.full((BLK, 2 * BLK), NEG, F32)
        for b in range(NUM_BUCKETS):
            acc = jnp.where(bk == b, rb_ref[b, gh], acc)
        out_ref[...] = acc

    return pl.pallas_call(
        body, name="bias_table", grid=(NG * NH,),
        in_specs=[pl.BlockSpec(memory_space=pltpu.SMEM),
                  pl.BlockSpec((None, BLK, 2 * BLK), lambda gh: (gh // NH, 0, 0))],
        out_specs=pl.BlockSpec((None, BLK, 2 * BLK), lambda gh: (gh, 0, 0)),
        out_shape=_sds((NG * NH, BLK, 2 * BLK)),
        compiler_params=_params("arbitrary"),
    )(rel_bias, buckets)


def _bias_grad(ds_acc, buckets):
    def body(acc_ref, bk_ref, out_ref):
        bk = bk_ref[...]
        acc = acc_ref[...]
        lane = lax.broadcasted_iota(jnp.int32, (8, 128), 1)
        out = jnp.zeros((8, 128), F32)
        for b in range(NUM_BUCKETS):
            val = jnp.sum(jnp.where(bk == b, acc, 0.0))
            out = jnp.where(lane == b, val, out)
        out_ref[...] = out

    return pl.pallas_call(
        body, name="bias_grad", grid=(NG * NH,),
        in_specs=[pl.BlockSpec((None, BLK, 2 * BLK), lambda gh: (gh, 0, 0)),
                  pl.BlockSpec((None, BLK, 2 * BLK), lambda gh: (gh // NH, 0, 0))],
        out_specs=pl.BlockSpec((None, 8, 128), lambda gh: (gh, 0, 0)),
        out_shape=_sds((NG * NH, 8, 128)),
        compiler_params=_params("arbitrary"),
    )(ds_acc, buckets)


def _mod_partial(c_all, w_ada_s, b_ada_s):
    def body(c_ref, w_ref, b_ref, o_ref):
        o_ref[...] = _dot(c_ref[...].astype(BF16), w_ref[...].astype(BF16)) + b_ref[...]

    return pl.pallas_call(body, name="mod_partial", out_shape=_sds((8, w_ada_s.shape[1])),
                          compiler_params=_params())(c_all, w_ada_s, b_ada_s)


def _prenorm(x, norm_g, mod):
    S = x.shape[0]
    tm = 512

    def body(x_ref, g_ref, mod_ref, h_ref):
        xv = x_ref[...]
        r = lax.rsqrt(jnp.mean(xv * xv, axis=-1, keepdims=True) + EPS)
        n1 = xv * r * g_ref[...]
        h_ref[...] = (n1 * (1.0 + mod_ref[:, D:2 * D]) + mod_ref[:, 0:D]).astype(BF16)

    return pl.pallas_call(
        body, name="prenorm", grid=(S // tm,),
        in_specs=[pl.BlockSpec((tm, D), lambda i: (i, 0)), pl.BlockSpec((1, D), lambda i: (0, 0)),
                  pl.BlockSpec((1, 3 * D), lambda i: (0, 0))],
        out_specs=pl.BlockSpec((tm, D), lambda i: (i, 0)),
        out_shape=_sds((S, D), BF16), compiler_params=_params("parallel"),
    )(x, norm_g, mod)


def _proj(h, wg_in, j0, nj, dtype, name):
    S = h.shape[0]
    tm = 1024
    per = wg_in.shape[2] // CB

    def body(h_ref, w_ref, o_ref):
        o_ref[...] = _dot(h_ref[...], w_ref[...]).astype(dtype)

    return pl.pallas_call(
        body, name=name, grid=(nj, S // tm),
        in_specs=[pl.BlockSpec((tm, D), lambda j, m: (m, 0)),
                  pl.BlockSpec((None, D, CB), lambda j, m: ((j0 + j) // per, 0, (j0 + j) % per))],
        out_specs=pl.BlockSpec((tm, CB), lambda j, m: (m, j)),
        out_shape=_sds((S, nj * CB), dtype), compiler_params=_params("parallel", "parallel"),
    )(h, wg_in)


def _attn_fwd(qkv_g, bias_tab, g):
    S = qkv_g.shape[0]
    dil = GROUPS[g][1]
    L = S // dil
    nb = L // BLK
    view = qkv_g.reshape(L, dil * GW)

    def body(q_ref, kp_ref, kc_ref, vp_ref, vc_ref, b_ref, o_ref, l_ref):
        n = pl.program_id(1)
        col = lax.broadcasted_iota(jnp.int32, (BLK, 2 * BLK), 1)
        keep = (col >= BLK) | (n > 0)
        for h in range(NH):
            sl = slice(h * HD, (h + 1) * HD)
            kb = jnp.concatenate([kp_ref[:, sl], kc_ref[:, sl]], axis=0)
            vb = jnp.concatenate([vp_ref[:, sl], vc_ref[:, sl]], axis=0)
            s = _dot_nt(q_ref[:, sl], kb) * (HD ** -0.5) + b_ref[h]
            s = jnp.where(keep, s, NEG)
            m = jnp.max(s, axis=-1, keepdims=True)
            p = jnp.exp(s - m)
            den = jnp.sum(p, axis=-1, keepdims=True)
            o_ref[:, sl] = _dot(p.astype(BF16), vb) / den
            l_ref[:, sl] = jnp.broadcast_to(m + jnp.log(den), (BLK, HD))

    def spec(t, prev):
        if prev:
            return pl.BlockSpec((BLK, AW), lambda r, n: (jnp.maximum(n - 1, 0), r * 3 + t))
        return pl.BlockSpec((BLK, AW), lambda r, n: (n, r * 3 + t))

    out_spec = pl.BlockSpec((BLK, AW), lambda r, n: (n, r))
    o, l = pl.pallas_call(
        body, name=f"attn_fwd{g}", grid=(dil, nb),
        in_specs=[spec(0, False), spec(1, True), spec(1, False), spec(2, True), spec(2, False),
                  pl.BlockSpec((NH, BLK, 2 * BLK), lambda r, n: (g, 0, 0))],
        out_specs=[out_spec, out_spec],
        out_shape=[_sds((L, dil * AW)), _sds((L, dil * AW))],
        compiler_params=_params("parallel", "arbitrary"),
    )(view, view, view, view, view, bias_tab)
    return o.reshape(S, AW), l.reshape(S, AW)


def _attn_bwd(qkv_g, dattn, attn, lj, bias_tab, g):
    S = qkv_g.shape[0]
    dil = GROUPS[g][1]
    L = S // dil
    nb = L // BLK
    view = qkv_g.reshape(L, dil * GW)
    da_v, at_v, lj_v = (a.reshape(L, dil * AW) for a in (dattn, attn, lj))

    def body(q_ref, kp_ref, kc_ref, vp_ref, vc_ref, da_ref, at_ref, lj_ref, b_ref,
             dq_ref, dk_ref, dv_ref, ds_ref, ck_ref, cv_ref):
        r = pl.program_id(0)
        n = pl.program_id(1)

        @pl.when((r == 0) & (n == 0))
        def _():
            ds_ref[...] = jnp.zeros_like(ds_ref)

        @pl.when(n == 0)
        def _():
            ck_ref[...] = jnp.zeros_like(ck_ref)
            cv_ref[...] = jnp.zeros_like(cv_ref)

        @pl.when(n < nb)
        def _():
            col = lax.broadcasted_iota(jnp.int32, (BLK, 2 * BLK), 1)
            keep = (col >= BLK) | (n > 0)
            for h in range(NH):
                sl = slice(h * HD, (h + 1) * HD)
                q = q_ref[:, sl]
                kb = jnp.concatenate([kp_ref[:, sl], kc_ref[:, sl]], axis=0)
                vb = jnp.concatenate([vp_ref[:, sl], vc_ref[:, sl]], axis=0)
                do = da_ref[:, sl]
                delta = jnp.sum(do * at_ref[:, sl], axis=-1, keepdims=True)
                s = _dot_nt(q, kb) * (HD ** -0.5) + b_ref[h]
                s = jnp.where(keep, s, NEG)
                p = jnp.exp(s - lj_ref[:, h * HD:h * HD + 1])
                do_b = do.astype(BF16)
                dp = _dot_nt(do_b, vb)
                ds = p * (dp - delta)
                ds_ref[h] += ds
                ds_b = (ds * (HD ** -0.5)).astype(BF16)
                dq_ref[:, sl] = _dot(ds_b, kb).astype(BF16)
                dkb = _dot_tn(ds_b, q)
                dvb = _dot_tn(p.astype(BF16), do_b)
                dk_ref[:, sl] = (ck_ref[:, sl] + dkb[:BLK]).astype(BF16)
                dv_ref[:, sl] = (cv_ref[:, sl] + dvb[:BLK]).astype(BF16)
                ck_ref[:, sl] = dkb[BLK:]
                cv_ref[:, sl] = dvb[BLK:]

        @pl.when(n == nb)
        def _():
            dk_ref[...] = ck_ref[...].astype(BF16)
            dv_ref[...] = cv_ref[...].astype(BF16)

    def cur(t):
        return pl.BlockSpec((BLK, AW), lambda r, n: (jnp.minimum(n, nb - 1), r * 3 + t))

    def prev(t):
        return pl.BlockSpec((BLK, AW), lambda r, n: (jnp.clip(n - 1, 0, nb - 1), r * 3 + t))

    row = pl.BlockSpec((BLK, AW), lambda r, n: (jnp.minimum(n, nb - 1), r))
    late = pl.BlockSpec((BLK, AW), lambda r, n: (jnp.maximum(n - 1, 0), r))
    dq, dk, dv, ds_acc = pl.pallas_call(
        body, name=f"attn_bwd{g}", grid=(dil, nb + 1),
        in_specs=[cur(0), prev(1), cur(1), prev(2), cur(2), row, row, row,
                  pl.BlockSpec((NH, BLK, 2 * BLK), lambda r, n: (g, 0, 0))],
        out_specs=[row, late, late, pl.BlockSpec((NH, BLK, 2 * BLK), lambda r, n: (0, 0, 0))],
        out_shape=[_sds((L, dil * AW), BF16)] * 3 + [_sds((NH, BLK, 2 * BLK))],
        scratch_shapes=[pltpu.VMEM((BLK, AW), F32), pltpu.VMEM((BLK, AW), F32)],
        compiler_params=_params("arbitrary", "arbitrary"),
    )(view, view, view, view, view, da_v, at_v, lj_v, bias_tab)
    return dq.reshape(S, AW), dk.reshape(S, AW), dv.reshape(S, AW), ds_acc


TM_MIX = 256


def _mix_specs(tm):
    row512 = pl.BlockSpec((tm, AW), lambda i: (i, 0))
    return ([row512] * 6 + [
        pl.BlockSpec((tm, REST_W), lambda i: (i, 0)),
        pl.BlockSpec((HALO, AW), lambda i: (jnp.maximum(i * (tm // HALO) - 1, 0), 1)),
        pl.BlockSpec((AW, D), lambda i: (0, 0)), pl.BlockSpec((AW, D), lambda i: (0, 0)),
        pl.BlockSpec((4, PGW, PGW), lambda i: (0, 0, 0)), pl.BlockSpec((1, AW), lambda i: (0, 0))])


def _mix_forward(i, tm, o_refs, l_refs, rest_ref, halo_ref, wab_ref, wpb_ref, pw_ref, ps_ref):
    l0, l1, l2 = (r[...] for r in l_refs)
    mx = jnp.maximum(jnp.maximum(l0, l1), l2)
    e0, e1, e2 = jnp.exp(l0 - mx), jnp.exp(l1 - mx), jnp.exp(l2 - mx)
    den = e0 + e1 + e2
    lj = mx + jnp.log(den)
    attn = (e0 * o_refs[0][...] + e1 * o_refs[1][...] + e2 * o_refs[2][...]) / den

    z_attn = rest_ref[:, 0:AW]
    u = rest_ref[:, AW:2 * AW]
    z_pool = rest_ref[:, 2 * AW:3 * AW]
    g_attn = rest_ref[:, 3 * AW:3 * AW + D]
    g_pool = rest_ref[:, 3 * AW + D:3 * AW + 2 * D]

    sg_a = _sigmoid(z_attn)
    sil_a = z_attn * sg_a
    a_g = (attn * sil_a).astype(BF16)
    y_attn = _dot(a_g, wab_ref[...])

    halo = jnp.where(i > 0, halo_ref[...], 0.0)
    ext = jnp.concatenate([halo, u], axis=0)
    t = i * tm + lax.broadcasted_iota(jnp.int32, (tm, 1), 0)
    pooled, mixed_raw = [], []
    for gi, win in enumerate(POOL_WINDOWS):
        s = ext[:, gi * PGW:(gi + 1) * PGW]
        sh = 1
        while sh < win:
            s = s + pltpu.roll(s, sh, 0)
            sh *= 2
        cnt = jnp.minimum(t + 1, win).astype(F32)
        pg = s[HALO:] / cnt - u[:, gi * PGW:(gi + 1) * PGW]
        pooled.append(pg.astype(BF16))
        mixed_raw.append(_dot(pooled[-1], pw_ref[gi].astype(BF16)))
    mixed_raw = jnp.concatenate(mixed_raw, axis=1)
    mixed = mixed_raw * ps_ref[...]
    sg_p = _sigmoid(z_pool)
    sil_p = z_pool * sg_p
    m_g = (mixed * sil_p).astype(BF16)
    y_pool = _dot(m_g, wpb_ref[...])

    sa = _sigmoid(g_attn)
    sp = _sigmoid(g_pool)
    merged = sa * y_attn + sp * y_pool
    return dict(lj=lj, attn=attn, z_attn=z_attn, z_pool=z_pool, sg_a=sg_a, sil_a=sil_a, a_g=a_g, y_attn=y_attn,
                pooled=pooled, mixed_raw=mixed_raw, mixed=mixed, sg_p=sg_p, sil_p=sil_p, m_g=m_g, y_pool=y_pool,
                sa=sa, sp=sp, merged=merged)


def _tail(x, target, os_, ls_, rest, wab, wpb, pool_w, pool_scale, wout, mod, final_g):
    S = x.shape[0]
    tm = TM_MIX

    def body(o0, o1, o2, l0, l1, l2, rest_ref, halo_ref, wab_ref, wpb_ref, pw_ref, ps_ref,
             x_ref, t_ref, wo_ref, mod_ref, fg_ref, dx2_ref, dmo_ref, loss_ref, dfg_ref, dgate_ref):
        i = pl.program_id(0)

        @pl.when(i == 0)
        def _():
            loss_ref[...] = jnp.zeros_like(loss_ref)
            dfg_ref[...] = jnp.zeros_like(dfg_ref)
            dgate_ref[...] = jnp.zeros_like(dgate_ref)

        f = _mix_forward(i, tm, (o0, o1, o2), (l0, l1, l2), rest_ref, halo_ref, wab_ref, wpb_ref, pw_ref, ps_ref)
        mo = _dot(f["merged"].astype(BF16), wo_ref[...])
        gate = mod_ref[:, 2 * D:3 * D]
        fg = fg_ref[...]
        x2 = x_ref[...] + gate * mo
        r2 = lax.rsqrt(jnp.mean(x2 * x2, axis=-1, keepdims=True) + EPS)
        n2 = x2 * r2
        err = n2 * fg - t_ref[...]
        loss_ref[...] += 0.5 * jnp.sum(jnp.mean(err * err, axis=-1, keepdims=True))
        dy = err * (1.0 / D)
        dfg_ref[...] += jnp.sum(dy * n2, axis=0, keepdims=True)
        dn = dy * fg
        dx2 = r2 * (dn - n2 * jnp.mean(dn * n2, axis=-1, keepdims=True))
        dgate_ref[...] += jnp.sum(dx2 * mo, axis=0, keepdims=True)
        dx2_ref[...] = dx2
        dmo_ref[...] = (dx2 * gate).astype(BF16)

    row = pl.BlockSpec((tm, D), lambda i: (i, 0))
    vec = pl.BlockSpec((1, D), lambda i: (0, 0))
    return pl.pallas_call(
        body, name="tail", grid=(S // tm,),
        in_specs=_mix_specs(tm) + [row, row, pl.BlockSpec((D, D), lambda i: (0, 0)),
                                   pl.BlockSpec((1, 3 * D), lambda i: (0, 0)), vec],
        out_specs=[row, row, pl.BlockSpec((8, 128), lambda i: (0, 0)), vec, vec],
        out_shape=[_sds((S, D)), _sds((S, D), BF16), _sds((8, 128)), _sds((1, D)), _sds((1, D))],
        compiler_params=_params("arbitrary"),
    )(*os_, *ls_, rest, rest, wab, wpb, pool_w, pool_scale, x, target, wout, mod, final_g)


def _mix_bwd(dmo, os_, ls_, rest, wab, wpb, pool_w, pool_scale, wout):
    S = dmo.shape[0]
    tm = TM_MIX
    nt = S // tm
    sw = D // N_SHARD

    def body(o0, o1, o2, l0, l1, l2, rest_ref, halo_ref, wab_ref, wpb_ref, pw_ref, ps_ref, dmo_ref, wo_ref,
             dattn_ref, attn_ref, lj_ref, dpooled_ref, drest_ref, dwo_hbm, dwab_hbm, dwpb_hbm, dpw_ref, dps_ref,
             awo, awab, awpb):
        i = pl.program_id(0)

        @pl.when(i == 0)
        def _():
            awo[...] = jnp.zeros_like(awo)
            awab[...] = jnp.zeros_like(awab)
            awpb[...] = jnp.zeros_like(awpb)
            dpw_ref[...] = jnp.zeros_like(dpw_ref)
            dps_ref[...] = jnp.zeros_like(dps_ref)

        f = _mix_forward(i, tm, (o0, o1, o2), (l0, l1, l2), rest_ref, halo_ref, wab_ref, wpb_ref, pw_ref, ps_ref)
        dmo_b = dmo_ref[...]
        dmerged = _dot_nt(dmo_b, wo_ref[...])
        awo[...] += _dot_tn(f["merged"].astype(BF16), dmo_b)
        sa, sp = f["sa"], f["sp"]
        dya = (dmerged * sa).astype(BF16)
        dyp = (dmerged * sp).astype(BF16)
        dg_attn = dmerged * f["y_attn"] * sa * (1.0 - sa)
        dg_pool = dmerged * f["y_pool"] * sp * (1.0 - sp)
        dag = _dot_nt(dya, wab_ref[...])
        awab[...] += _dot_tn(f["a_g"], dya)
        dmg = _dot_nt(dyp, wpb_ref[...])
        awpb[...] += _dot_tn(f["m_g"], dyp)
        dattn_ref[...] = dag * f["sil_a"]
        attn_ref[...] = f["attn"]
        lj_ref[...] = f["lj"]
        dz_attn = dag * f["attn"] * (f["sg_a"] * (1.0 + f["z_attn"] * (1.0 - f["sg_a"])))
        dmixed = dmg * f["sil_p"]
        dz_pool = dmg * f["mixed"] * (f["sg_p"] * (1.0 + f["z_pool"] * (1.0 - f["sg_p"])))
        dps_ref[...] += jnp.sum(dmixed * f["mixed_raw"], axis=0, keepdims=True)
        dpm = (dmixed * ps_ref[...]).astype(BF16)
        for gi in range(len(POOL_WINDOWS)):
            cs = slice(gi * PGW, (gi + 1) * PGW)
            dpw_ref[gi] += _dot_tn(f["pooled"][gi], dpm[:, cs])
            dpooled_ref[:, cs] = _dot_nt(dpm[:, cs], pw_ref[gi].astype(BF16))
        drest_ref[:, 0:AW] = dz_attn.astype(BF16)
        drest_ref[:, AW:2 * AW] = jnp.zeros((tm, AW), BF16)
        drest_ref[:, 2 * AW:3 * AW] = dz_pool.astype(BF16)
        drest_ref[:, 3 * AW:3 * AW + D] = dg_attn.astype(BF16)
        drest_ref[:, 3 * AW + D:3 * AW + 2 * D] = dg_pool.astype(BF16)

        @pl.when(i == nt - 1)
        def _():
            pltpu.sync_copy(awo, dwo_hbm)
            for k in range(N_SHARD):
                pltpu.sync_copy(awab.at[:, pl.ds(k * sw, sw)], dwab_hbm.at[k])
                pltpu.sync_copy(awpb.at[:, pl.ds(k * sw, sw)], dwpb_hbm.at[k])

    row512 = pl.BlockSpec((tm, AW), lambda i: (i, 0))
    outs = pl.pallas_call(
        body, name="mix_bwd", grid=(nt,),
        in_specs=_mix_specs(tm) + [pl.BlockSpec((tm, D), lambda i: (i, 0)), pl.BlockSpec((D, D), lambda i: (0, 0))],
        out_specs=[row512, row512, row512, row512, pl.BlockSpec((tm, REST_W), lambda i: (i, 0)), ANY, ANY, ANY,
                   pl.BlockSpec((4, PGW, PGW), lambda i: (0, 0, 0)), pl.BlockSpec((1, AW), lambda i: (0, 0))],
        out_shape=[_sds((S, AW)), _sds((S, AW)), _sds((S, AW)), _sds((S, AW)), _sds((S, REST_W), BF16),
                   _sds((D, D)), _sds((N_SHARD, AW, sw)), _sds((N_SHARD, AW, sw)), _sds((4, PGW, PGW)), _sds((1, AW))],
        scratch_shapes=[pltpu.VMEM((D, D), F32), pltpu.VMEM((AW, D), F32), pltpu.VMEM((AW, D), F32)],
        compiler_params=_params("arbitrary"),
    )(*os_, *ls_, rest, rest, wab, wpb, pool_w, pool_scale, dmo, wout)
    dattn, attn, lj, dpooled, drest, dwo, dwab, dwpb, dpw, dps = outs
    return dattn, attn, lj, dpooled, drest, dwo.reshape(N_SHARD, D // N_SHARD, D), dwab, dwpb, dpw, dps


def _pool_bwd(dpooled):
    S = dpooled.shape[0]
    tm = 512
    nt = S // tm

    def body(dp_ref, nxt_ref, du_ref):
        i = pl.program_id(0)
        t = i * tm + lax.broadcasted_iota(jnp.int32, (tm + HALO, 1), 0)
        nxt = jnp.where(i < nt - 1, nxt_ref[...], 0.0)
        ext = jnp.concatenate([dp_ref[...], nxt], axis=0)
        for gi, win in enumerate(POOL_WINDOWS):
            cs = slice(gi * PGW, (gi + 1) * PGW)
            s = ext[:, cs] / jnp.minimum(t + 1, win).astype(F32)
            sh = 1
            while sh < win:
                s = s + pltpu.roll(s, tm + HALO - sh, 0)
                sh *= 2
            du_ref[:, cs] = (s[:tm] - dp_ref[:, cs]).astype(BF16)

    return pl.pallas_call(
        body, name="pool_bwd", grid=(nt,),
        in_specs=[pl.BlockSpec((tm, AW), lambda i: (i, 0)),
                  pl.BlockSpec((HALO, AW), lambda i: (jnp.minimum((i + 1) * (tm // HALO), S // HALO - 1), 0))],
        out_specs=pl.BlockSpec((tm, AW), lambda i: (i, 0)),
        out_shape=_sds((S, AW), BF16), compiler_params=_params("parallel"),
    )(dpooled, dpooled)


def _dproj_sources(dqkv, drest, du):
    srcs = []
    for a, arr in enumerate(dqkv):
        srcs.append((arr, (lambda j, a=a: j == a), (lambda j: 0)))
    srcs.append((du, (lambda j: j == NCB_QKV + 1), (lambda j: 0)))
    srcs.append((drest, (lambda j: (j >= NCB_QKV) & (j != NCB_QKV + 1)),
                 (lambda j: jnp.clip(j - NCB_QKV, 0, REST_W // CB - 1))))
    return srcs


def _dh(dqkv, drest, du, wg_in):
    S = drest.shape[0]
    tm = 512
    srcs = _dproj_sources(dqkv, drest, du)
    per = wg_in.shape[2] // CB

    def body(*refs):
        w_ref, out_ref = refs[len(srcs)], refs[len(srcs) + 1]
        j = pl.program_id(1)

        @pl.when(j == 0)
        def _():
            out_ref[...] = jnp.zeros_like(out_ref)

        for a, (_, pred, _) in enumerate(srcs):
            @pl.when(pred(j))
            def _(a=a):
                out_ref[...] += _dot_nt(refs[a][...], w_ref[...])

    in_specs = [pl.BlockSpec((tm, CB), (lambda m, j, pred=pred, colf=colf: (jnp.where(pred(j), m, 0), colf(j))))
                for (_, pred, colf) in srcs]
    in_specs.append(pl.BlockSpec((None, D, CB), lambda m, j: (j // per, 0, j % per)))
    return pl.pallas_call(
        body, name="dh", grid=(S // tm, NCB), in_specs=in_specs,
        out_specs=pl.BlockSpec((tm, D), lambda m, j: (m, 0)),
        out_shape=_sds((S, D)), compiler_params=_params("parallel", "arbitrary"),
    )(*[s[0] for s in srcs], wg_in)


def _dw_in(h_t, dqkv, drest, du):
    S = drest.shape[0]
    tk = 512
    srcs = _dproj_sources(dqkv, drest, du)
    per = IN_W // N_SHARD // CB

    def body(*refs):
        ht_ref, out_ref = refs[len(srcs)], refs[len(srcs) + 1]
        j = pl.program_id(0)
        kk = pl.program_id(1)

        @pl.when(kk == 0)
        def _():
            out_ref[...] = jnp.zeros_like(out_ref)

        for a, (_, pred, _) in enumerate(srcs):
            @pl.when(pred(j))
            def _(a=a):
                out_ref[...] += _dot(ht_ref[...], refs[a][...])

    in_specs = [pl.BlockSpec((tk, CB), (lambda j, kk, pred=pred, colf=colf: (jnp.where(pred(j), kk, 0), colf(j))))
                for (_, pred, colf) in srcs]
    in_specs.append(pl.BlockSpec((D, tk), lambda j, kk: (0, kk)))
    return pl.pallas_call(
        body, name="dw_in", grid=(NCB, S // tk), in_specs=in_specs,
        out_specs=pl.BlockSpec((None, D, CB), lambda j, kk: (j // per, 0, j % per)),
        out_shape=_sds((N_SHARD, D, IN_W // N_SHARD)), compiler_params=_params("parallel", "arbitrary"),
    )(*[s[0] for s in srcs], h_t)


def _prenorm_bwd(x, dh, dx2, norm_g, mod):
    S = x.shape[0]
    tm = 512

    def body(x_ref, dh_ref, dx2_ref, g_ref, mod_ref, gx_ref, dg_ref, dshift_ref, dscale_ref):
        i = pl.program_id(0)

        @pl.when(i == 0)
        def _():
            dg_ref[...] = jnp.zeros_like(dg_ref)
            dshift_ref[...] = jnp.zeros_like(dshift_ref)
            dscale_ref[...] = jnp.zeros_like(dscale_ref)

        xv = x_ref[...]
        dhv = dh_ref[...]
        g = g_ref[...]
        r = lax.rsqrt(jnp.mean(xv * xv, axis=-1, keepdims=True) + EPS)
        xh = xv * r
        dshift_ref[...] += jnp.sum(dhv, axis=0, keepdims=True)
        dscale_ref[...] += jnp.sum(dhv * (xh * g), axis=0, keepdims=True)
        dn1 = dhv * (1.0 + mod_ref[:, D:2 * D])
        dg_ref[...] += jnp.sum(dn1 * xh, axis=0, keepdims=True)
        dxh = dn1 * g
        gx_ref[...] = dx2_ref[...] + r * (dxh - xh * jnp.mean(dxh * xh, axis=-1, keepdims=True))

    row = pl.BlockSpec((tm, D), lambda i: (i, 0))
    vec = pl.BlockSpec((1, D), lambda i: (0, 0))
    return pl.pallas_call(
        body, name="prenorm_bwd", grid=(S // tm,),
        in_specs=[row, row, row, vec, pl.BlockSpec((1, 3 * D), lambda i: (0, 0))],
        out_specs=[row, vec, vec, vec],
        out_shape=[_sds((S, D)), _sds((1, D)), _sds((1, D)), _sds((1, D))],
        compiler_params=_params("arbitrary"),
    )(x, dh, dx2, norm_g, mod)


def _local_step(x, target, mod, wg_in, wab, wpb, wout, pool_w, pool_scale, rel_bias, norm_g, final_g):
    buckets = jnp.asarray(_bucket_tables())
    bias_tab = _bias_table(rel_bias, buckets)
    h = _prenorm(x, norm_g, mod)
    qkv = [_proj(h, wg_in, 3 * g, 3, BF16, f"proj_qkv{g}") for g in range(NG)]
    rest = _proj(h, wg_in, NCB_QKV, REST_W // CB, F32, "proj_rest")
    os_, ls_ = zip(*[_attn_fwd(qkv[g], bias_tab, g) for g in range(NG)])
    dx2, dmo, loss, dfinal_g, dgate = _tail(x, target, os_, ls_, rest, wab, wpb, pool_w, pool_scale, wout, mod, final_g)
    dattn, attn, lj, dpooled, drest, dw_out, dw_ab, dw_pb, dpool_w, dpool_scale = _mix_bwd(
        dmo, os_, ls_, rest, wab, wpb, pool_w, pool_scale, wout)
    du = _pool_bwd(dpooled)
    dqkv, ds_accs = [], []
    for g in range(NG):
        dq, dk, dv, ds_acc = _attn_bwd(qkv[g], dattn, attn, lj, bias_tab, g)
        dqkv += [dq, dk, dv]
        ds_accs.append(ds_acc)
    drel = _bias_grad(jnp.concatenate(ds_accs, axis=0), buckets)[:, 0, :NUM_BUCKETS].T
    dh = _dh(dqkv, drest, du, wg_in)
    dw_in = _dw_in(h.T, dqkv, drest, du)
    grad_x, dnorm_g, dshift, dscale = _prenorm_bwd(x, dh, dx2, norm_g, mod)
    dmod = jnp.concatenate([dshift, dscale, dgate], axis=1)
    return dict(loss=loss[0, 0], grad_x=grad_x, dmod=dmod, dnorm_g=dnorm_g, dfinal_g=dfinal_g, dpool_w=dpool_w,
                dpool_scale=dpool_scale, drel_bias=drel, dw_in=dw_in, dw_attn_br=dw_ab, dw_pool_br=dw_pb,
                dw_out=dw_out)


def _dma_sems(*shape):
    return pltpu.SemaphoreType.DMA(shape)


def _allgather8(blocks, name):
    nb = len(blocks)

    def body(*refs):
        ins, outs = refs[:nb], refs[nb:2 * nb]
        send_sems, recv_sems, local_sems = refs[2 * nb:]
        x, y, c = lax.axis_index("x"), lax.axis_index("y"), lax.axis_index("c")
        me, sibling = (x, y, c), (x, y, 1 - c)
        chips = [(1 - x, y), (x, 1 - y), (1 - x, 1 - y)]

        def copy(a, k, block, to, src=None):
            dst = outs[a].at[4 * block[0] + 2 * block[1] + block[2]]
            return pltpu.make_async_remote_copy(src_ref=dst if src is None else src, dst_ref=dst,
                                                send_sem=send_sems.at[a, k], recv_sem=recv_sems.at[a, k],
                                                device_id=to, device_id_type=MESH)

        mine = [pltpu.make_async_copy(ins[a], outs[a].at[4 * x + 2 * y + c], local_sems.at[a]) for a in range(nb)]
        for cp in mine:
            cp.start()
        first = []
        for a in range(nb):
            first.append(copy(a, 0, me, sibling, src=ins[a]))
            first += [copy(a, 1 + j, me, (*chip, c), src=ins[a]) for j, chip in enumerate(chips)]
        for cp in first:
            cp.start()
        passed = []
        for j, chip in enumerate(chips):
            for a in range(nb):
                copy(a, 1 + j, (*chip, c), me).wait_recv()
                cp = copy(a, 4 + j, (*chip, c), sibling)
                cp.start()
                passed.append(cp)
        for a in range(nb):
            copy(a, 0, sibling, me).wait_recv()
            for j, chip in enumerate(chips):
                copy(a, 4 + j, (*chip, 1 - c), me).wait_recv()
        for cp in first + passed:
            cp.wait_send()
        for cp in mine:
            cp.wait()

    return pl.pallas_call(
        body, name=name, in_specs=[ANY] * nb, out_specs=[ANY] * nb,
        out_shape=[_sds((8,) + b.shape, b.dtype) for b in blocks],
        scratch_shapes=[_dma_sems(nb, 7), _dma_sems(nb, 7), _dma_sems(nb)],
    )(*blocks)


def _sibling_halves(gs, name):
    nb = len(gs)

    def body(*refs):
        ins, outs = refs[:nb], refs[nb:2 * nb]
        send_sems, recv_sems, local_sems = refs[2 * nb:]
        x, y, c = lax.axis_index("x"), lax.axis_index("y"), lax.axis_index("c")
        cps = []
        for a in range(nb):
            r2 = ins[a].shape[1] // 2
            own = ins[a].at[:, pl.ds(c * r2, r2), :]
            other = ins[a].at[:, pl.ds((1 - c) * r2, r2), :]
            cps.append(pltpu.make_async_copy(own, outs[a].at[c], local_sems.at[a]))
            cps.append(pltpu.make_async_remote_copy(src_ref=other, dst_ref=outs[a].at[c], send_sem=send_sems.at[a],
                                                    recv_sem=recv_sems.at[a], device_id=(x, y, 1 - c),
                                                    device_id_type=MESH))
        for cp in cps:
            cp.start()
        for cp in cps:
            cp.wait()

    return pl.pallas_call(
        body, name=name, in_specs=[ANY] * nb, out_specs=[ANY] * nb,
        out_shape=[_sds((2, g.shape[0], g.shape[1] // 2, g.shape[2]), g.dtype) for g in gs],
        scratch_shapes=[_dma_sems(nb), _dma_sems(nb), _dma_sems(nb)],
    )(*gs)


def _chip_exchange(ps, name):
    nb = len(ps)

    def body(*refs):
        ins, outs = refs[:nb], refs[nb:2 * nb]
        send_sems, recv_sems, local_sems = refs[2 * nb:]
        x, y, c = lax.axis_index("x"), lax.axis_index("y"), lax.axis_index("c")
        k_me = 2 * x + y
        chips = [(1 - x, y), (x, 1 - y), (1 - x, 1 - y)]
        cps = []
        for a in range(nb):
            cps.append(pltpu.make_async_copy(ins[a].at[k_me], outs[a].at[k_me], local_sems.at[a]))
            for j, (ox, oy) in enumerate(chips):
                cps.append(pltpu.make_async_remote_copy(src_ref=ins[a].at[2 * ox + oy], dst_ref=outs[a].at[k_me],
                                                        send_sem=send_sems.at[a, j], recv_sem=recv_sems.at[a, j],
                                                        device_id=(ox, oy, c), device_id_type=MESH))
        for cp in cps:
            cp.start()
        for cp in cps:
            cp.wait()

    return pl.pallas_call(
        body, name=name, in_specs=[ANY] * nb, out_specs=[ANY] * nb,
        out_shape=[_sds(p.shape, p.dtype) for p in ps],
        scratch_shapes=[_dma_sems(nb, 3), _dma_sems(nb, 3), _dma_sems(nb)],
    )(*ps)


def _sibling_join(hs, name):
    nb = len(hs)

    def body(*refs):
        ins, outs = refs[:nb], refs[nb:2 * nb]
        send_sems, recv_sems, local_sems = refs[2 * nb:]
        x, y, c = lax.axis_index("x"), lax.axis_index("y"), lax.axis_index("c")
        cps = []
        for a in range(nb):
            r2 = ins[a].shape[0]
            dst = outs[a].at[pl.ds(c * r2, r2), :]
            cps.append(pltpu.make_async_copy(ins[a], dst, local_sems.at[a]))
            cps.append(pltpu.make_async_remote_copy(src_ref=ins[a], dst_ref=dst, send_sem=send_sems.at[a],
                                                    recv_sem=recv_sems.at[a], device_id=(x, y, 1 - c),
                                                    device_id_type=MESH))
        for cp in cps:
            cp.start()
        for cp in cps:
            cp.wait()

    return pl.pallas_call(
        body, name=name, in_specs=[ANY] * nb, out_specs=[ANY] * nb,
        out_shape=[_sds((2 * h.shape[0], h.shape[1]), h.dtype) for h in hs],
        scratch_shapes=[_dma_sems(nb), _dma_sems(nb), _dma_sems(nb)],
    )(*hs)


def _row_tile(rows, cols):
    tile = rows
    while tile * cols * 4 > (1 << 20) and tile % 16 == 0:
        tile //= 2
    return tile


def _sum_leading(a, name):
    k = a.shape[0]
    a3 = a.reshape(k, -1, a.shape[-1])
    rows, cols = a3.shape[1:]
    tr = _row_tile(rows, cols)

    def body(a_ref, o_ref):
        acc = a_ref[0]
        for s in range(1, k):
            acc = acc + a_ref[s]
        o_ref[...] = acc

    out = pl.pallas_call(
        body, name=name, grid=(rows // tr,),
        in_specs=[pl.BlockSpec((k, tr, cols), lambda i: (0, i, 0))],
        out_specs=pl.BlockSpec((tr, cols), lambda i: (i, 0)),
        out_shape=_sds((rows, cols), a.dtype), compiler_params=_params("parallel"),
    )(a3)
    return out.reshape(a.shape[1:])


def _w_ada_grad(c_all, dmod_cols):
    def body(c_ref, d_ref, o_ref):
        o_ref[...] = _dot_tn(c_ref[...].astype(BF16), d_ref[...].astype(BF16))

    return pl.pallas_call(body, name="w_ada_grad", out_shape=_sds((c_all.shape[1], dmod_cols.shape[1])),
                          compiler_params=_params())(c_all, dmod_cols)


def _adamw(w, g, m, v, name):
    rows, cols = w.shape
    tr = _row_tile(rows, cols)

    def body(w_ref, g_ref, m_ref, v_ref, d_ref, nm_ref, nv_ref):
        gv = g_ref[...]
        nm = ADAM_B1 * m_ref[...] + (1.0 - ADAM_B1) * gv
        nv = ADAM_B2 * v_ref[...] + (1.0 - ADAM_B2) * (gv * gv)
        m_hat = nm / (1.0 - ADAM_B1 ** ADAM_STEP)
        v_hat = nv / (1.0 - ADAM_B2 ** ADAM_STEP)
        d_ref[...] = -ADAM_LR * (m_hat / (jnp.sqrt(v_hat) + ADAM_EPS) + ADAM_WD * w_ref[...])
        nm_ref[...] = nm
        nv_ref[...] = nv

    spec = pl.BlockSpec((tr, cols), lambda i: (i, 0))
    return pl.pallas_call(
        body, name=name, grid=(rows // tr,), in_specs=[spec] * 4, out_specs=[spec] * 3,
        out_shape=[_sds((rows, cols))] * 3, compiler_params=_params("parallel"),
    )(w, g, m, v)


def _pack_small(b_ada, norm_g, final_g, pool_scale, rel_bias, loss_row, pool_w):
    pad = jnp.zeros((PK_POOLW - PK_LOSS - 1) * 128, F32)
    flat = jnp.concatenate([b_ada.reshape(-1), norm_g.reshape(-1), final_g.reshape(-1), pool_scale.reshape(-1),
                            rel_bias.reshape(-1), loss_row.reshape(-1), pad, pool_w.reshape(-1)])
    return flat.reshape(PK_ROWS, 128)


def _unpack_small(p):
    def take(r0, r1, shape):
        return p[r0:r1].reshape(shape)

    return dict(b_ada=take(PK_BADA, PK_NORMG, (1, 3 * D)), norm_g=take(PK_NORMG, PK_FINALG, (1, D)),
                final_g=take(PK_FINALG, PK_PSCALE, (D,)), pool_scale=take(PK_PSCALE, PK_RELB, (1, AW)),
                rel_bias=take(PK_RELB, PK_LOSS, (NUM_BUCKETS, NG * NH)), loss=p[PK_LOSS, 0],
                pool_w=take(PK_POOLW, PK_ROWS, (1, 4, PGW, PGW)))


def kernel(x, c, norm_g, w_ada, b_ada, w_in, pool_w, pool_scale, w_attn_br, w_pool_br, w_out, rel_bias, final_g, loss_target, m_norm_g, m_w_ada, m_b_ada, m_w_in, m_pool_w, m_pool_scale, m_w_attn_br, m_w_pool_br, m_w_out, m_rel_bias, m_final_g, v_norm_g, v_w_ada, v_b_ada, v_w_in, v_pool_w, v_pool_scale, v_w_attn_br, v_w_pool_br, v_w_out, v_rel_bias, v_final_g):
    ix, iy, ic = lax.axis_index("x"), lax.axis_index("y"), lax.axis_index("c")
    dev = 4 * ix + 2 * iy + ic
    chip = 2 * ix + iy

    def half(w):
        r2 = w.shape[0] // 2
        return lax.dynamic_slice_in_dim(w, ic * r2, r2, axis=0).astype(BF16)

    gathered = _allgather8([jnp.broadcast_to(c, (8, D)), half(w_in[0]), half(w_attn_br[0]), half(w_pool_br[0]),
                            half(w_out[0])], "gather_weights")
    c_all = gathered[0][:, 0, :]
    wg_in = gathered[1].reshape(N_SHARD, D, IN_W // N_SHARD)
    wab = gathered[2].reshape(N_SHARD, AW, D // N_SHARD).transpose(1, 0, 2).reshape(AW, D)
    wpb = gathered[3].reshape(N_SHARD, AW, D // N_SHARD).transpose(1, 0, 2).reshape(AW, D)
    wout = gathered[4].reshape(D, D)

    mw = 3 * D // N_SHARD
    modp = _mod_partial(c_all, w_ada[0], lax.dynamic_slice_in_dim(b_ada, chip * mw, mw, axis=1))
    mod_all = _allgather8([modp], "gather_mod")[0]
    mod_full = mod_all[::2].transpose(1, 0, 2).reshape(8, 3 * D)
    mod = lax.dynamic_slice_in_dim(mod_full, dev, 1, axis=0)

    r = _local_step(x[0], loss_target[0], mod, wg_in, wab, wpb, wout, pool_w[0], pool_scale, rel_bias, norm_g,
                    final_g.reshape(1, D))

    packed = _pack_small(r["dmod"], r["dnorm_g"], r["dfinal_g"], r["dpool_scale"], r["drel_bias"],
                         jnp.full((128,), r["loss"], F32), r["dpool_w"])
    small_all = _allgather8([packed], "gather_small")[0]
    small_sum = _sum_leading(small_all, "sum_small")
    dmod_all = small_all[:, PK_BADA:PK_NORMG, :].reshape(8, 3 * D)
    g_w_ada = _w_ada_grad(c_all, lax.dynamic_slice_in_dim(dmod_all, chip * mw, mw, axis=1))

    big = [r["dw_in"], r["dw_attn_br"], r["dw_pool_br"], r["dw_out"]]
    pair = _sibling_halves(big, "rs_sibling_halves")
    pair_sum = [_sum_leading(t, f"rs_pair_sum{a}") for a, t in enumerate(pair)]
    pieces = _chip_exchange(pair_sum, "rs_chip_exchange")
    piece_sum = [_sum_leading(u, f"rs_chip_sum{a}") for a, u in enumerate(pieces)]
    g_w_in, g_w_ab, g_w_pb, g_w_out = _sibling_join(piece_sum, "rs_sibling_join")

    small_w = _pack_small(b_ada, norm_g, final_g, pool_scale, rel_bias, jnp.zeros((128,), F32), pool_w)
    small_m = _pack_small(m_b_ada, m_norm_g, m_final_g, m_pool_scale, m_rel_bias, jnp.zeros((128,), F32), m_pool_w)
    small_v = _pack_small(v_b_ada, v_norm_g, v_final_g, v_pool_scale, v_rel_bias, jnp.ones((128,), F32), v_pool_w)
    sd, sm, sv = (_unpack_small(p) for p in _adamw(small_w, small_sum, small_m, small_v, "adamw_small"))
    sg = _unpack_small(small_sum)
    upd = {
        "w_ada": (g_w_ada,) + tuple(_adamw(w_ada[0], g_w_ada, m_w_ada[0], v_w_ada[0], "adamw_w_ada")),
        "w_in": (g_w_in,) + tuple(_adamw(w_in[0], g_w_in, m_w_in[0], v_w_in[0], "adamw_w_in")),
        "w_attn_br": (g_w_ab,) + tuple(_adamw(w_attn_br[0], g_w_ab, m_w_attn_br[0], v_w_attn_br[0], "adamw_w_ab")),
        "w_pool_br": (g_w_pb,) + tuple(_adamw(w_pool_br[0], g_w_pb, m_w_pool_br[0], v_w_pool_br[0], "adamw_w_pb")),
        "w_out": (g_w_out,) + tuple(_adamw(w_out[0], g_w_out, m_w_out[0], v_w_out[0], "adamw_w_out")),
    }
    names = ["norm_g", "w_ada", "b_ada", "w_in", "pool_w", "pool_scale", "w_attn_br", "w_pool_br", "w_out",
             "rel_bias", "final_g"]
    outs = [sg["loss"], r["grad_x"][None]]
    for kind in range(4):
        for nme in names:
            if nme in upd:
                outs.append(upd[nme][kind][None])
            else:
                outs.append((sg, sd, sm, sv)[kind][nme])
    return tuple(outs)
```

```python
import functools
import math

import numpy as np
import jax
import jax.numpy as jnp
from jax import lax
from jax.experimental import pallas as pl
from jax.experimental.pallas import tpu as pltpu

F32 = jnp.float32
BF16 = jnp.bfloat16

D = 1024
HD = 64
NH = 8
AW = NH * HD
GROUPS = ((128, 1), (512, 4), (2048, 16))
NG = len(GROUPS)
BLK = 128
GW = 3 * AW
QKV_W = NG * GW
REST_W = 3584
IN_W = QKV_W + REST_W
CB = 512
NCB = IN_W // CB
NCB_QKV = QKV_W // CB
POOL_WINDOWS = (2, 4, 8, 16)
PGW = 128
HALO = 16
NUM_BUCKETS = 32
MAX_DISTANCE = 2048
EPS = 1e-6
NEG = -1e30
N_SHARD = 4
VMEM_LIMIT = 56 * 1024 * 1024

ADAM_LR = 0.001
ADAM_B1 = 0.9
ADAM_B2 = 0.999
ADAM_EPS = 1e-08
ADAM_WD = 0.01
ADAM_STEP = 10

PK_BADA, PK_NORMG, PK_FINALG, PK_PSCALE, PK_RELB, PK_LOSS, PK_POOLW, PK_ROWS = 0, 24, 32, 40, 44, 50, 56, 568

ANY = pl.BlockSpec(memory_space=pl.ANY)
MESH = pl.DeviceIdType.MESH


def _params(*sem):
    return pltpu.CompilerParams(dimension_semantics=sem, vmem_limit_bytes=VMEM_LIMIT)


def _sds(shape, dtype=F32):
    return jax.ShapeDtypeStruct(shape, dtype)


def _dot(a, b):
    return jnp.dot(a, b, preferred_element_type=F32)


def _dot_nt(a, b):
    return lax.dot_general(a, b, (((1,), (1,)), ((), ())), preferred_element_type=F32)


def _dot_tn(a, b):
    return lax.dot_general(a, b, (((0,), (0,)), ((), ())), preferred_element_type=F32)


def _sigmoid(z):
    return 1.0 / (1.0 + jnp.exp(-z))


def _bucket_tables():
    i = np.arange(BLK)[:, None]
    j = np.arange(2 * BLK)[None, :]
    dist = BLK + i - j
    valid = (dist >= 0) & (dist <= BLK)
    tabs = []
    for _, dil in GROUPS:
        n = (np.clip(dist, 0, BLK) * dil).astype(np.int32)
        max_exact = NUM_BUCKETS // 2
        nf = np.maximum(n, 1).astype(np.float32)
        large = max_exact + (np.log(nf / np.float32(max_exact)) / np.float32(math.log(MAX_DISTANCE / max_exact))
                             * np.float32(NUM_BUCKETS - max_exact)).astype(np.int32)
        large = np.minimum(large, NUM_BUCKETS - 1)
        bucket = np.where(n < max_exact, n, large)
        tabs.append(np.where(valid, bucket, -1).astype(np.int32))
    return np.stack(tabs)


def _bias_table(rel_bias, buckets):
    def body(rb_ref, bk_ref, out_ref):
        gh = pl.program_id(0)
        bk = bk_ref[...]
        acc = jnp.full((BLK, 2 * BLK), NEG, F32)
        for b in range(NUM_BUCKETS):
            acc = jnp.where(bk == b, rb_ref[b, gh], acc)
        out_ref[...] = acc

    return pl.pallas_call(
        body, name="bias_table", grid=(NG * NH,),
        in_specs=[pl.BlockSpec(memory_space=pltpu.SMEM),
                  pl.BlockSpec((None, BLK, 2 * BLK), lambda gh: (gh // NH, 0, 0))],
        out_specs=pl.BlockSpec((None, BLK, 2 * BLK), lambda gh: (gh, 0, 0)),
        out_shape=_sds((NG * NH, BLK, 2 * BLK)),
        compiler_params=_params("arbitrary"),
    )(rel_bias, buckets)


def _bias_grad(ds_acc, buckets):
    def body(acc_ref, bk_ref, out_ref):
        bk = bk_ref[...]
        acc = acc_ref[...]
        lane = lax.broadcasted_iota(jnp.int32, (8, 128), 1)
        out = jnp.zeros((8, 128), F32)
        for b in range(NUM_BUCKETS):
            val = jnp.sum(jnp.where(bk == b, acc, 0.0))
            out = jnp.where(lane == b, val, out)
        out_ref[...] = out

    return pl.pallas_call(
        body, name="bias_grad", grid=(NG * NH,),
        in_specs=[pl.BlockSpec((None, BLK, 2 * BLK), lambda gh: (gh, 0, 0)),
                  pl.BlockSpec((None, BLK, 2 * BLK), lambda gh: (gh // NH, 0, 0))],
        out_specs=pl.BlockSpec((None, 8, 128), lambda gh: (gh, 0, 0)),
        out_shape=_sds((NG * NH, 8, 128)),
        compiler_params=_params("arbitrary"),
    )(ds_acc, buckets)


def _mod_partial(c_all, w_ada_s, b_ada_s):
    def body(c_ref, w_ref, b_ref, o_ref):
        o_ref[...] = _dot(c_ref[...].astype(BF16), w_ref[...].astype(BF16)) + b_ref[...]

    return pl.pallas_call(body, name="mod_partial", out_shape=_sds((8, w_ada_s.shape[1])),
                          compiler_params=_params())(c_all, w_ada_s, b_ada_s)


def _prenorm(x, norm_g, mod):
    S = x.shape[0]
    tm = 512

    def body(x_ref, g_ref, mod_ref, h_ref):
        xv = x_ref[...]
        r = lax.rsqrt(jnp.mean(xv * xv, axis=-1, keepdims=True) + EPS)
        n1 = xv * r * g_ref[...]
        h_ref[...] = (n1 * (1.0 + mod_ref[:, D:2 * D]) + mod_ref[:, 0:D]).astype(BF16)

    return pl.pallas_call(
        body, name="prenorm", grid=(S // tm,),
        in_specs=[pl.BlockSpec((tm, D), lambda i: (i, 0)), pl.BlockSpec((1, D), lambda i: (0, 0)),
                  pl.BlockSpec((1, 3 * D), lambda i: (0, 0))],
        out_specs=pl.BlockSpec((tm, D), lambda i: (i, 0)),
        out_shape=_sds((S, D), BF16), compiler_params=_params("parallel"),
    )(x, norm_g, mod)


def _proj(h, wg_in, j0, nj, dtype, name):
    S = h.shape[0]
    tm = 1024
    per = wg_in.shape[2] // CB

    def body(h_ref, w_ref, o_ref):
        o_ref[...] = _dot(h_ref[...], w_ref[...]).astype(dtype)

    return pl.pallas_call(
        body, name=name, grid=(S // tm, nj),
        in_specs=[pl.BlockSpec((tm, D), lambda m, j: (m, 0)),
                  pl.BlockSpec((None, D, CB), lambda m, j: ((j0 + j) // per, 0, (j0 + j) % per))],
        out_specs=pl.BlockSpec((tm, CB), lambda m, j: (m, j)),
        out_shape=_sds((S, nj * CB), dtype), compiler_params=_params("parallel", "parallel"),
    )(h, wg_in)


def _attn_fwd(qkv_g, bias_tab, g):
    S = qkv_g.shape[0]
    dil = GROUPS[g][1]
    L = S // dil
    nb = L // BLK
    view = qkv_g.reshape(L, dil * GW)

    def body(q_ref, kp_ref, kc_ref, vp_ref, vc_ref, b_ref, o_ref, l_ref):
        n = pl.program_id(1)
        col = lax.broadcasted_iota(jnp.int32, (BLK, 2 * BLK), 1)
        keep = (col >= BLK) | (n > 0)
        for h in range(NH):
            sl = slice(h * HD, (h + 1) * HD)
            kb = jnp.concatenate([kp_ref[:, sl], kc_ref[:, sl]], axis=0)
            vb = jnp.concatenate([vp_ref[:, sl], vc_ref[:, sl]], axis=0)
            s = _dot_nt(q_ref[:, sl], kb) * (HD ** -0.5) + b_ref[h]
            s = jnp.where(keep, s, NEG)
            m = jnp.max(s, axis=-1, keepdims=True)
            p = jnp.exp(s - m)
            den = jnp.sum(p, axis=-1, keepdims=True)
            o_ref[:, sl] = _dot(p.astype(BF16), vb) / den
            l_ref[:, sl] = jnp.broadcast_to(m + jnp.log(den), (BLK, HD))

    def spec(t, prev):
        if prev:
            return pl.BlockSpec((BLK, AW), lambda r, n: (jnp.maximum(n - 1, 0), r * 3 + t))
        return pl.BlockSpec((BLK, AW), lambda r, n: (n, r * 3 + t))

    out_spec = pl.BlockSpec((BLK, AW), lambda r, n: (n, r))
    o, l = pl.pallas_call(
        body, name=f"attn_fwd{g}", grid=(dil, nb),
        in_specs=[spec(0, False), spec(1, True), spec(1, False), spec(2, True), spec(2, False),
                  pl.BlockSpec((NH, BLK, 2 * BLK), lambda r, n: (g, 0, 0))],
        out_specs=[out_spec, out_spec],
        out_shape=[_sds((L, dil * AW)), _sds((L, dil * AW))],
        compiler_params=_params("parallel", "arbitrary"),
    )(view, view, view, view, view, bias_tab)
    return o.reshape(S, AW), l.reshape(S, AW)


def _attn_bwd(qkv_g, dattn, attn, lj, bias_tab, g):
    S = qkv_g.shape[0]
    dil = GROUPS[g][1]
    L = S // dil
    nb = L // BLK
    view = qkv_g.reshape(L, dil * GW)
    da_v, at_v, lj_v = (a.reshape(L, dil * AW) for a in (dattn, attn, lj))

    def body(q_ref, kp_ref, kc_ref, vp_ref, vc_ref, da_ref, at_ref, lj_ref, b_ref,
             dq_ref, dk_ref, dv_ref, ds_ref, ck_ref, cv_ref):
        r = pl.program_id(0)
        n = pl.program_id(1)

        @pl.when((r == 0) & (n == 0))
        def _():
            ds_ref[...] = jnp.zeros_like(ds_ref)

        @pl.when(n == 0)
        def _():
            ck_ref[...] = jnp.zeros_like(ck_ref)
            cv_ref[...] = jnp.zeros_like(cv_ref)

        @pl.when(n < nb)
        def _():
            col = lax.broadcasted_iota(jnp.int32, (BLK, 2 * BLK), 1)
            keep = (col >= BLK) | (n > 0)
            for h in range(NH):
                sl = slice(h * HD, (h + 1) * HD)
                q = q_ref[:, sl]
                kb = jnp.concatenate([kp_ref[:, sl], kc_ref[:, sl]], axis=0)
                vb = jnp.concatenate([vp_ref[:, sl], vc_ref[:, sl]], axis=0)
                do = da_ref[:, sl]
                delta = jnp.sum(do * at_ref[:, sl], axis=-1, keepdims=True)
                s = _dot_nt(q, kb) * (HD ** -0.5) + b_ref[h]
                s = jnp.where(keep, s, NEG)
                p = jnp.exp(s - lj_ref[:, h * HD:h * HD + 1])
                do_b = do.astype(BF16)
                dp = _dot_nt(do_b, vb)
                ds = p * (dp - delta)
                ds_ref[h] += ds
                ds_b = (ds * (HD ** -0.5)).astype(BF16)
                dq_ref[:, sl] = _dot(ds_b, kb).astype(BF16)
                dkb = _dot_tn(ds_b, q)
                dvb = _dot_tn(p.astype(BF16), do_b)
                dk_ref[:, sl] = (ck_ref[:, sl] + dkb[:BLK]).astype(BF16)
                dv_ref[:, sl] = (cv_ref[:, sl] + dvb[:BLK]).astype(BF16)
                ck_ref[:, sl] = dkb[BLK:]
                cv_ref[:, sl] = dvb[BLK:]

        @pl.when(n == nb)
        def _():
            dk_ref[...] = ck_ref[...].astype(BF16)
            dv_ref[...] = cv_ref[...].astype(BF16)

    def cur(t):
        return pl.BlockSpec((BLK, AW), lambda r, n: (jnp.minimum(n, nb - 1), r * 3 + t))

    def prev(t):
        return pl.BlockSpec((BLK, AW), lambda r, n: (jnp.clip(n - 1, 0, nb - 1), r * 3 + t))

    row = pl.BlockSpec((BLK, AW), lambda r, n: (jnp.minimum(n, nb - 1), r))
    late = pl.BlockSpec((BLK, AW), lambda r, n: (jnp.maximum(n - 1, 0), r))
    dq, dk, dv, ds_acc = pl.pallas_call(
        body, name=f"attn_bwd{g}", grid=(dil, nb + 1),
        in_specs=[cur(0), prev(1), cur(1), prev(2), cur(2), row, row, row,
                  pl.BlockSpec((NH, BLK, 2 * BLK), lambda r, n: (g, 0, 0))],
        out_specs=[row, late, late, pl.BlockSpec((NH, BLK, 2 * BLK), lambda r, n: (0, 0, 0))],
        out_shape=[_sds((L, dil * AW), BF16)] * 3 + [_sds((NH, BLK, 2 * BLK))],
        scratch_shapes=[pltpu.VMEM((BLK, AW), F32), pltpu.VMEM((BLK, AW), F32)],
        compiler_params=_params("arbitrary", "arbitrary"),
    )(view, view, view, view, view, da_v, at_v, lj_v, bias_tab)
    return dq.reshape(S, AW), dk.reshape(S, AW), dv.reshape(S, AW), ds_acc


TM_MIX = 256


def _mix_specs(tm):
    row512 = pl.BlockSpec((tm, AW), lambda i: (i, 0))
    return ([row512] * 6 + [
        pl.BlockSpec((tm, REST_W), lambda i: (i, 0)),
        pl.BlockSpec((HALO, AW), lambda i: (jnp.maximum(i * (tm // HALO) - 1, 0), 1)),
        pl.BlockSpec((AW, D), lambda i: (0, 0)), pl.BlockSpec((AW, D), lambda i: (0, 0)),
        pl.BlockSpec((4, PGW, PGW), lambda i: (0, 0, 0)), pl.BlockSpec((1, AW), lambda i: (0, 0))])


def _mix_forward(i, tm, o_refs, l_refs, rest_ref, halo_ref, wab_ref, wpb_ref, pw_ref, ps_ref):
    l0, l1, l2 = (r[...] for r in l_refs)
    mx = jnp.maximum(jnp.maximum(l0, l1), l2)
    e0, e1, e2 = jnp.exp(l0 - mx), jnp.exp(l1 - mx), jnp.exp(l2 - mx)
    den = e0 + e1 + e2
    lj = mx + jnp.log(den)
    attn = (e0 * o_refs[0][...] + e1 * o_refs[1][...] + e2 * o_refs[2][...]) / den

    z_attn = rest_ref[:, 0:AW]
    u = rest_ref[:, AW:2 * AW]
    z_pool = rest_ref[:, 2 * AW:3 * AW]
    g_attn = rest_ref[:, 3 * AW:3 * AW + D]
    g_pool = rest_ref[:, 3 * AW + D:3 * AW + 2 * D]

    sg_a = _sigmoid(z_attn)
    sil_a = z_attn * sg_a
    a_g = (attn * sil_a).astype(BF16)
    y_attn = _dot(a_g, wab_ref[...])

    halo = jnp.where(i > 0, halo_ref[...], 0.0)
    ext = jnp.concatenate([halo, u], axis=0)
    t = i * tm + lax.broadcasted_iota(jnp.int32, (tm, 1), 0)
    pooled, mixed_raw = [], []
    for gi, win in enumerate(POOL_WINDOWS):
        s = ext[:, gi * PGW:(gi + 1) * PGW]
        sh = 1
        while sh < win:
            s = s + pltpu.roll(s, sh, 0)
            sh *= 2
        cnt = jnp.minimum(t + 1, win).astype(F32)
        pg = s[HALO:] / cnt - u[:, gi * PGW:(gi + 1) * PGW]
        pooled.append(pg.astype(BF16))
        mixed_raw.append(_dot(pooled[-1], pw_ref[gi].astype(BF16)))
    mixed_raw = jnp.concatenate(mixed_raw, axis=1)
    mixed = mixed_raw * ps_ref[...]
    sg_p = _sigmoid(z_pool)
    sil_p = z_pool * sg_p
    m_g = (mixed * sil_p).astype(BF16)
    y_pool = _dot(m_g, wpb_ref[...])

    sa = _sigmoid(g_attn)
    sp = _sigmoid(g_pool)
    merged = sa * y_attn + sp * y_pool
    return dict(lj=lj, attn=attn, z_attn=z_attn, z_pool=z_pool, sg_a=sg_a, sil_a=sil_a, a_g=a_g, y_attn=y_attn,
                pooled=pooled, mixed_raw=mixed_raw, mixed=mixed, sg_p=sg_p, sil_p=sil_p, m_g=m_g, y_pool=y_pool,
                sa=sa, sp=sp, merged=merged)


def _tail(x, target, os_, ls_, rest, wab, wpb, pool_w, pool_scale, wout, mod, final_g):
    S = x.shape[0]
    tm = TM_MIX

    def body(o0, o1, o2, l0, l1, l2, rest_ref, halo_ref, wab_ref, wpb_ref, pw_ref, ps_ref,
             x_ref, t_ref, wo_ref, mod_ref, fg_ref, dx2_ref, dmo_ref, loss_ref, dfg_ref, dgate_ref):
        i = pl.program_id(0)

        @pl.when(i == 0)
        def _():
            loss_ref[...] = jnp.zeros_like(loss_ref)
            dfg_ref[...] = jnp.zeros_like(dfg_ref)
            dgate_ref[...] = jnp.zeros_like(dgate_ref)

        f = _mix_forward(i, tm, (o0, o1, o2), (l0, l1, l2), rest_ref, halo_ref, wab_ref, wpb_ref, pw_ref, ps_ref)
        mo = _dot(f["merged"].astype(BF16), wo_ref[...])
        gate = mod_ref[:, 2 * D:3 * D]
        fg = fg_ref[...]
        x2 = x_ref[...] + gate * mo
        r2 = lax.rsqrt(jnp.mean(x2 * x2, axis=-1, keepdims=True) + EPS)
        n2 = x2 * r2
        err = n2 * fg - t_ref[...]
        loss_ref[...] += 0.5 * jnp.sum(jnp.mean(err * err, axis=-1, keepdims=True))
        dy = err * (1.0 / D)
        dfg_ref[...] += jnp.sum(dy * n2, axis=0, keepdims=True)
        dn = dy * fg
        dx2 = r2 * (dn - n2 * jnp.mean(dn * n2, axis=-1, keepdims=True))
        dgate_ref[...] += jnp.sum(dx2 * mo, axis=0, keepdims=True)
        dx2_ref[...] = dx2
        dmo_ref[...] = (dx2 * gate).astype(BF16)

    row = pl.BlockSpec((tm, D), lambda i: (i, 0))
    vec = pl.BlockSpec((1, D), lambda i: (0, 0))
    return pl.pallas_call(
        body, name="tail", grid=(S // tm,),
        in_specs=_mix_specs(tm) + [row, row, pl.BlockSpec((D, D), lambda i: (0, 0)),
                                   pl.BlockSpec((1, 3 * D), lambda i: (0, 0)), vec],
        out_specs=[row, row, pl.BlockSpec((8, 128), lambda i: (0, 0)), vec, vec],
        out_shape=[_sds((S, D)), _sds((S, D), BF16), _sds((8, 128)), _sds((1, D)), _sds((1, D))],
        compiler_params=_params("arbitrary"),
    )(*os_, *ls_, rest, rest, wab, wpb, pool_w, pool_scale, x, target, wout, mod, final_g)


def _mix_bwd(dmo, os_, ls_, rest, wab, wpb, pool_w, pool_scale, wout):
    S = dmo.shape[0]
    tm = TM_MIX
    nt = S // tm
    sw = D // N_SHARD

    def body(o0, o1, o2, l0, l1, l2, rest_ref, halo_ref, wab_ref, wpb_ref, pw_ref, ps_ref, dmo_ref, wo_ref,
             dattn_ref, attn_ref, lj_ref, dpooled_ref, drest_ref, dwo_hbm, dwab_hbm, dwpb_hbm, dpw_ref, dps_ref,
             awo, awab, awpb):
        i = pl.program_id(0)

        @pl.when(i == 0)
        def _():
            awo[...] = jnp.zeros_like(awo)
            awab[...] = jnp.zeros_like(awab)
            awpb[...] = jnp.zeros_like(awpb)
            dpw_ref[...] = jnp.zeros_like(dpw_ref)
            dps_ref[...] = jnp.zeros_like(dps_ref)

        f = _mix_forward(i, tm, (o0, o1, o2), (l0, l1, l2), rest_ref, halo_ref, wab_ref, wpb_ref, pw_ref, ps_ref)
        dmo_b = dmo_ref[...]
        dmerged = _dot_nt(dmo_b, wo_ref[...])
        awo[...] += _dot_tn(f["merged"].astype(BF16), dmo_b)
        sa, sp = f["sa"], f["sp"]
        dya = (dmerged * sa).astype(BF16)
        dyp = (dmerged * sp).astype(BF16)
        dg_attn = dmerged * f["y_attn"] * sa * (1.0 - sa)
        dg_pool = dmerged * f["y_pool"] * sp * (1.0 - sp)
        dag = _dot_nt(dya, wab_ref[...])
        awab[...] += _dot_tn(f["a_g"], dya)
        dmg = _dot_nt(dyp, wpb_ref[...])
        awpb[...] += _dot_tn(f["m_g"], dyp)
        dattn_ref[...] = dag * f["sil_a"]
        attn_ref[...] = f["attn"]
        lj_ref[...] = f["lj"]
        dz_attn = dag * f["attn"] * (f["sg_a"] * (1.0 + f["z_attn"] * (1.0 - f["sg_a"])))
        dmixed = dmg * f["sil_p"]
        dz_pool = dmg * f["mixed"] * (f["sg_p"] * (1.0 + f["z_pool"] * (1.0 - f["sg_p"])))
        dps_ref[...] += jnp.sum(dmixed * f["mixed_raw"], axis=0, keepdims=True)
        dpm = (dmixed * ps_ref[...]).astype(BF16)
        for gi in range(len(POOL_WINDOWS)):
            cs = slice(gi * PGW, (gi + 1) * PGW)
            dpw_ref[gi] += _dot_tn(f["pooled"][gi], dpm[:, cs])
            dpooled_ref[:, cs] = _dot_nt(dpm[:, cs], pw_ref[gi].astype(BF16))
        drest_ref[:, 0:AW] = dz_attn.astype(BF16)
        drest_ref[:, AW:2 * AW] = jnp.zeros((tm, AW), BF16)
        drest_ref[:, 2 * AW:3 * AW] = dz_pool.astype(BF16)
        drest_ref[:, 3 * AW:3 * AW + D] = dg_attn.astype(BF16)
        drest_ref[:, 3 * AW + D:3 * AW + 2 * D] = dg_pool.astype(BF16)

        @pl.when(i == nt - 1)
        def _():
            pltpu.sync_copy(awo, dwo_hbm)
            for k in range(N_SHARD):
                pltpu.sync_copy(awab.at[:, pl.ds(k * sw, sw)], dwab_hbm.at[k])
                pltpu.sync_copy(awpb.at[:, pl.ds(k * sw, sw)], dwpb_hbm.at[k])

    row512 = pl.BlockSpec((tm, AW), lambda i: (i, 0))
    outs = pl.pallas_call(
        body, name="mix_bwd", grid=(nt,),
        in_specs=_mix_specs(tm) + [pl.BlockSpec((tm, D), lambda i: (i, 0)), pl.BlockSpec((D, D), lambda i: (0, 0))],
        out_specs=[row512, row512, row512, row512, pl.BlockSpec((tm, REST_W), lambda i: (i, 0)), ANY, ANY, ANY,
                   pl.BlockSpec((4, PGW, PGW), lambda i: (0, 0, 0)), pl.BlockSpec((1, AW), lambda i: (0, 0))],
        out_shape=[_sds((S, AW)), _sds((S, AW)), _sds((S, AW)), _sds((S, AW)), _sds((S, REST_W), BF16),
                   _sds((D, D)), _sds((N_SHARD, AW, sw)), _sds((N_SHARD, AW, sw)), _sds((4, PGW, PGW)), _sds((1, AW))],
        scratch_shapes=[pltpu.VMEM((D, D), F32), pltpu.VMEM((AW, D), F32), pltpu.VMEM((AW, D), F32)],
        compiler_params=_params("arbitrary"),
    )(*os_, *ls_, rest, rest, wab, wpb, pool_w, pool_scale, dmo, wout)
    dattn, attn, lj, dpooled, drest, dwo, dwab, dwpb, dpw, dps = outs
    return dattn, attn, lj, dpooled, drest, dwo.reshape(N_SHARD, D // N_SHARD, D), dwab, dwpb, dpw, dps


def _pool_bwd(dpooled):
    S = dpooled.shape[0]
    tm = 512
    nt = S // tm

    def body(dp_ref, nxt_ref, du_ref):
        i = pl.program_id(0)
        t = i * tm + lax.broadcasted_iota(jnp.int32, (tm + HALO, 1), 0)
        nxt = jnp.where(i < nt - 1, nxt_ref[...], 0.0)
        ext = jnp.concatenate([dp_ref[...], nxt], axis=0)
        for gi, win in enumerate(POOL_WINDOWS):
            cs = slice(gi * PGW, (gi + 1) * PGW)
            s = ext[:, cs] / jnp.minimum(t + 1, win).astype(F32)
            sh = 1
            while sh < win:
                s = s + pltpu.roll(s, tm + HALO - sh, 0)
                sh *= 2
            du_ref[:, cs] = (s[:tm] - dp_ref[:, cs]).astype(BF16)

    return pl.pallas_call(
        body, name="pool_bwd", grid=(nt,),
        in_specs=[pl.BlockSpec((tm, AW), lambda i: (i, 0)),
                  pl.BlockSpec((HALO, AW), lambda i: (jnp.minimum((i + 1) * (tm // HALO), S // HALO - 1), 0))],
        out_specs=pl.BlockSpec((tm, AW), lambda i: (i, 0)),
        out_shape=_sds((S, AW), BF16), compiler_params=_params("parallel"),
    )(dpooled, dpooled)


TB = 1024


def _dh(dproj, wg_in):
    S = dproj.shape[0]
    per = wg_in.shape[2] // TB

    def body(dp_ref, w_ref, out_ref):
        @pl.when(pl.program_id(1) == 0)
        def _():
            out_ref[...] = jnp.zeros_like(out_ref)

        out_ref[...] += _dot_nt(dp_ref[...], w_ref[...])

    return pl.pallas_call(
        body, name="dh", grid=(S // TB, IN_W // TB),
        in_specs=[pl.BlockSpec((TB, TB), lambda m, kk: (m, kk)),
                  pl.BlockSpec((None, D, TB), lambda m, kk: (kk // per, 0, kk % per))],
        out_specs=pl.BlockSpec((TB, D), lambda m, kk: (m, 0)),
        out_shape=_sds((S, D)), compiler_params=_params("parallel", "arbitrary"),
    )(dproj, wg_in)


def _dw_in(h_t, dproj):
    S = dproj.shape[0]
    per = IN_W // N_SHARD // TB

    def body(ht_ref, dp_ref, out_ref):
        @pl.when(pl.program_id(1) == 0)
        def _():
            out_ref[...] = jnp.zeros_like(out_ref)

        out_ref[...] += _dot(ht_ref[...], dp_ref[...])

    return pl.pallas_call(
        body, name="dw_in", grid=(IN_W // TB, S // TB),
        in_specs=[pl.BlockSpec((D, TB), lambda j, kk: (0, kk)), pl.BlockSpec((TB, TB), lambda j, kk: (kk, j))],
        out_specs=pl.BlockSpec((None, D, TB), lambda j, kk: (j // per, 0, j % per)),
        out_shape=_sds((N_SHARD, D, IN_W // N_SHARD)), compiler_params=_params("parallel", "arbitrary"),
    )(h_t, dproj)


def _prenorm_bwd(x, dh, dx2, norm_g, mod):
    S = x.shape[0]
    tm = 512

    def body(x_ref, dh_ref, dx2_ref, g_ref, mod_ref, gx_ref, dg_ref, dshift_ref, dscale_ref):
        i = pl.program_id(0)

        @pl.when(i == 0)
        def _():
            dg_ref[...] = jnp.zeros_like(dg_ref)
            dshift_ref[...] = jnp.zeros_like(dshift_ref)
            dscale_ref[...] = jnp.zeros_like(dscale_ref)

        xv = x_ref[...]
        dhv = dh_ref[...]
        g = g_ref[...]
        r = lax.rsqrt(jnp.mean(xv * xv, axis=-1, keepdims=True) + EPS)
        xh = xv * r
        dshift_ref[...] += jnp.sum(dhv, axis=0, keepdims=True)
        dscale_ref[...] += jnp.sum(dhv * (xh * g), axis=0, keepdims=True)
        dn1 = dhv * (1.0 + mod_ref[:, D:2 * D])
        dg_ref[...] += jnp.sum(dn1 * xh, axis=0, keepdims=True)
        dxh = dn1 * g
        gx_ref[...] = dx2_ref[...] + r * (dxh - xh * jnp.mean(dxh * xh, axis=-1, keepdims=True))

    row = pl.BlockSpec((tm, D), lambda i: (i, 0))
    vec = pl.BlockSpec((1, D), lambda i: (0, 0))
    return pl.pallas_call(
        body, name="prenorm_bwd", grid=(S // tm,),
        in_specs=[row, row, row, vec, pl.BlockSpec((1, 3 * D), lambda i: (0, 0))],
        out_specs=[row, vec, vec, vec],
        out_shape=[_sds((S, D)), _sds((1, D)), _sds((1, D)), _sds((1, D))],
        compiler_params=_params("arbitrary"),
    )(x, dh, dx2, norm_g, mod)


def _local_step(x, target, mod, wg_in, wab, wpb, wout, pool_w, pool_scale, rel_bias, norm_g, final_g):
    buckets = jnp.asarray(_bucket_tables())
    bias_tab = _bias_table(rel_bias, buckets)
    h = _prenorm(x, norm_g, mod)
    qkv = [_proj(h, wg_in, 3 * g, 3, BF16, f"proj_qkv{g}") for g in range(NG)]
    rest = _proj(h, wg_in, NCB_QKV, REST_W // CB, F32, "proj_rest")
    os_, ls_ = zip(*[_attn_fwd(qkv[g], bias_tab, g) for g in range(NG)])
    dx2, dmo, loss, dfinal_g, dgate = _tail(x, target, os_, ls_, rest, wab, wpb, pool_w, pool_scale, wout, mod, final_g)
    dattn, attn, lj, dpooled, drest, dw_out, dw_ab, dw_pb, dpool_w, dpool_scale = _mix_bwd(
        dmo, os_, ls_, rest, wab, wpb, pool_w, pool_scale, wout)
    du = _pool_bwd(dpooled)
    dqkv, ds_accs = [], []
    for g in range(NG):
        dq, dk, dv, ds_acc = _attn_bwd(qkv[g], dattn, attn, lj, bias_tab, g)
        dqkv += [dq, dk, dv]
        ds_accs.append(ds_acc)
    drel = _bias_grad(jnp.concatenate(ds_accs, axis=0), buckets)[:, 0, :NUM_BUCKETS].T
    dproj = jnp.concatenate(dqkv + [drest[:, :AW], du, drest[:, 2 * AW:]], axis=1)
    dh = _dh(dproj, wg_in)
    dw_in = _dw_in(h.T, dproj)
    grad_x, dnorm_g, dshift, dscale = _prenorm_bwd(x, dh, dx2, norm_g, mod)
    dmod = jnp.concatenate([dshift, dscale, dgate], axis=1)
    return dict(loss=loss[0, 0], grad_x=grad_x, dmod=dmod, dnorm_g=dnorm_g, dfinal_g=dfinal_g, dpool_w=dpool_w,
                dpool_scale=dpool_scale, drel_bias=drel, dw_in=dw_in, dw_attn_br=dw_ab, dw_pool_br=dw_pb,
                dw_out=dw_out)


def _dma_sems(*shape):
    return pltpu.SemaphoreType.DMA(shape)


def _allgather8(blocks, name):
    nb = len(blocks)

    def body(*refs):
        ins, outs = refs[:nb], refs[nb:2 * nb]
        send_sems, recv_sems, local_sems = refs[2 * nb:]
        x, y, c = lax.axis_index("x"), lax.axis_index("y"), lax.axis_index("c")
        me, sibling = (x, y, c), (x, y, 1 - c)
        chips = [(1 - x, y), (x, 1 - y), (1 - x, 1 - y)]

        def copy(a, k, block, to, src=None):
            dst = outs[a].at[4 * block[0] + 2 * block[1] + block[2]]
            return pltpu.make_async_remote_copy(src_ref=dst if src is None else src, dst_ref=dst,
                                                send_sem=send_sems.at[a, k], recv_sem=recv_sems.at[a, k],
                                                device_id=to, device_id_type=MESH)

        mine = [pltpu.make_async_copy(ins[a], outs[a].at[4 * x + 2 * y + c], local_sems.at[a]) for a in range(nb)]
        for cp in mine:
            cp.start()
        first = []
        for a in range(nb):
            first.append(copy(a, 0, me, sibling, src=ins[a]))
            first += [copy(a, 1 + j, me, (*chip, c), src=ins[a]) for j, chip in enumerate(chips)]
        for cp in first:
            cp.start()
        passed = []
        for j, chip in enumerate(chips):
            for a in range(nb):
                copy(a, 1 + j, (*chip, c), me).wait_recv()
                cp = copy(a, 4 + j, (*chip, c), sibling)
                cp.start()
                passed.append(cp)
        for a in range(nb):
            copy(a, 0, sibling, me).wait_recv()
            for j, chip in enumerate(chips):
                copy(a, 4 + j, (*chip, 1 - c), me).wait_recv()
        for cp in first + passed:
            cp.wait_send()
        for cp in mine:
            cp.wait()

    return pl.pallas_call(
        body, name=name, in_specs=[ANY] * nb, out_specs=[ANY] * nb,
        out_shape=[_sds((8,) + b.shape, b.dtype) for b in blocks],
        scratch_shapes=[_dma_sems(nb, 7), _dma_sems(nb, 7), _dma_sems(nb)],
    )(*blocks)


def _sibling_halves(gs, name):
    nb = len(gs)

    def body(*refs):
        ins, outs = refs[:nb], refs[nb:2 * nb]
        send_sems, recv_sems = refs[2 * nb:]
        x, y, c = lax.axis_index("x"), lax.axis_index("y"), lax.axis_index("c")
        cps = []
        for a in range(nb):
            r2 = ins[a].shape[1] // 2
            other = ins[a].at[:, pl.ds((1 - c) * r2, r2), :]
            cps.append(pltpu.make_async_remote_copy(src_ref=other, dst_ref=outs[a], send_sem=send_sems.at[a],
                                                    recv_sem=recv_sems.at[a], device_id=(x, y, 1 - c),
                                                    device_id_type=MESH))
        for cp in cps:
            cp.start()
        for cp in cps:
            cp.wait()

    return pl.pallas_call(
        body, name=name, in_specs=[ANY] * nb, out_specs=[ANY] * nb,
        out_shape=[_sds((g.shape[0], g.shape[1] // 2, g.shape[2]), g.dtype) for g in gs],
        scratch_shapes=[_dma_sems(nb), _dma_sems(nb)],
    )(*gs)


def _pair_sum(g, t, half, name):
    nsh, rows, cols = g.shape
    r2 = rows // 2
    tr = _row_tile(r2, cols)
    nt = r2 // tr

    def body(half_ref, g_ref, t_ref, p32_ref, p16_ref):
        p = g_ref[...] + t_ref[...]
        p32_ref[...] = p
        p16_ref[...] = p.astype(BF16)

    blk = pl.BlockSpec((None, tr, cols), lambda k, i, half_ref: (k, i, 0))
    return pl.pallas_call(
        body, name=name,
        grid_spec=pltpu.PrefetchScalarGridSpec(
            num_scalar_prefetch=1, grid=(nsh, nt),
            in_specs=[pl.BlockSpec((None, tr, cols), lambda k, i, half_ref: (k, half_ref[0] * nt + i, 0)), blk],
            out_specs=[blk, blk]),
        out_shape=[_sds((nsh, r2, cols)), _sds((nsh, r2, cols), BF16)],
        compiler_params=_params("parallel", "parallel"),
    )(half, g, t)


def _chip_exchange(ps, name):
    nb = len(ps)

    def body(*refs):
        ins, outs = refs[:nb], refs[nb:2 * nb]
        send_sems, recv_sems = refs[2 * nb:]
        x, y, c = lax.axis_index("x"), lax.axis_index("y"), lax.axis_index("c")
        chips = [(1 - x, y), (x, 1 - y), (1 - x, 1 - y)]
        cps = []
        for a in range(nb):
            for j, (ox, oy) in enumerate(chips):
                cps.append(pltpu.make_async_remote_copy(src_ref=ins[a].at[2 * ox + oy], dst_ref=outs[a].at[j],
                                                        send_sem=send_sems.at[a, j], recv_sem=recv_sems.at[a, j],
                                                        device_id=(ox, oy, c), device_id_type=MESH))
        for cp in cps:
            cp.start()
        for cp in cps:
            cp.wait()

    return pl.pallas_call(
        body, name=name, in_specs=[ANY] * nb, out_specs=[ANY] * nb,
        out_shape=[_sds((3,) + p.shape[1:], p.dtype) for p in ps],
        scratch_shapes=[_dma_sems(nb, 3), _dma_sems(nb, 3)],
    )(*ps)


def _chip_sum(p32, u, chip_half, name):
    r2, cols = p32.shape[1:]
    tr = _row_tile(r2, cols)
    nt = r2 // tr

    def body(ch_ref, p_ref, u_ref, o_ref):
        acc = p_ref[...]
        for j in range(3):
            acc = acc + u_ref[j].astype(F32)
        o_ref[...] = acc

    return pl.pallas_call(
        body, name=name,
        grid_spec=pltpu.PrefetchScalarGridSpec(
            num_scalar_prefetch=1, grid=(nt,),
            in_specs=[pl.BlockSpec((None, tr, cols), lambda i, ch_ref: (ch_ref[0], i, 0)),
                      pl.BlockSpec((3, tr, cols), lambda i, ch_ref: (0, i, 0))],
            out_specs=pl.BlockSpec((tr, cols), lambda i, ch_ref: (ch_ref[1] * nt + i, 0))),
        out_shape=_sds((2 * r2, cols)), compiler_params=_params("parallel"),
    )(chip_half, p32, u)


def _sibling_join(fs, name):
    nb = len(fs)

    def body(*refs):
        outs = refs[nb:2 * nb]
        send_sems, recv_sems = refs[2 * nb:]
        x, y, c = lax.axis_index("x"), lax.axis_index("y"), lax.axis_index("c")
        cps = []
        for a in range(nb):
            r2 = outs[a].shape[0] // 2
            rows = outs[a].at[pl.ds(c * r2, r2), :]
            cps.append(pltpu.make_async_remote_copy(src_ref=rows, dst_ref=rows, send_sem=send_sems.at[a],
                                                    recv_sem=recv_sems.at[a], device_id=(x, y, 1 - c),
                                                    device_id_type=MESH))
        for cp in cps:
            cp.start()
        for cp in cps:
            cp.wait()

    return pl.pallas_call(
        body, name=name, in_specs=[ANY] * nb, out_specs=[ANY] * nb,
        out_shape=[_sds(f.shape, f.dtype) for f in fs],
        input_output_aliases={a: a for a in range(nb)},
        scratch_shapes=[_dma_sems(nb), _dma_sems(nb)],
    )(*fs)


def _row_tile(rows, cols):
    tile = rows
    while tile * cols * 4 > (1 << 20) and tile % 16 == 0:
        tile //= 2
    return tile


def _sum_leading(a, name):
    k = a.shape[0]
    a3 = a.reshape(k, -1, a.shape[-1])
    rows, cols = a3.shape[1:]
    tr = _row_tile(rows, cols)

    def body(a_ref, o_ref):
        acc = a_ref[0]
        for s in range(1, k):
            acc = acc + a_ref[s]
        o_ref[...] = acc

    out = pl.pallas_call(
        body, name=name, grid=(rows // tr,),
        in_specs=[pl.BlockSpec((k, tr, cols), lambda i: (0, i, 0))],
        out_specs=pl.BlockSpec((tr, cols), lambda i: (i, 0)),
        out_shape=_sds((rows, cols), a.dtype), compiler_params=_params("parallel"),
    )(a3)
    return out.reshape(a.shape[1:])


def _w_ada_grad(c_all, dmod_cols):
    def body(c_ref, d_ref, o_ref):
        o_ref[...] = _dot_tn(c_ref[...].astype(BF16), d_ref[...].astype(BF16))

    return pl.pallas_call(body, name="w_ada_grad", out_shape=_sds((c_all.shape[1], dmod_cols.shape[1])),
                          compiler_params=_params())(c_all, dmod_cols)


def _adamw(w, g, m, v, name):
    rows, cols = w.shape
    tr = _row_tile(rows, cols)

    def body(w_ref, g_ref, m_ref, v_ref, d_ref, nm_ref, nv_ref):
        gv = g_ref[...]
        nm = ADAM_B1 * m_ref[...] + (1.0 - ADAM_B1) * gv
        nv = ADAM_B2 * v_ref[...] + (1.0 - ADAM_B2) * (gv * gv)
        m_hat = nm / (1.0 - ADAM_B1 ** ADAM_STEP)
        v_hat = nv / (1.0 - ADAM_B2 ** ADAM_STEP)
        d_ref[...] = -ADAM_LR * (m_hat / (jnp.sqrt(v_hat) + ADAM_EPS) + ADAM_WD * w_ref[...])
        nm_ref[...] = nm
        nv_ref[...] = nv

    spec = pl.BlockSpec((tr, cols), lambda i: (i, 0))
    return pl.pallas_call(
        body, name=name, grid=(rows // tr,), in_specs=[spec] * 4, out_specs=[spec] * 3,
        out_shape=[_sds((rows, cols))] * 3, compiler_params=_params("parallel"),
    )(w, g, m, v)


def _pack_small(b_ada, norm_g, final_g, pool_scale, rel_bias, loss_row, pool_w):
    pad = jnp.zeros((PK_POOLW - PK_LOSS - 1) * 128, F32)
    flat = jnp.concatenate([b_ada.reshape(-1), norm_g.reshape(-1), final_g.reshape(-1), pool_scale.reshape(-1),
                            rel_bias.reshape(-1), loss_row.reshape(-1), pad, pool_w.reshape(-1)])
    return flat.reshape(PK_ROWS, 128)


def _unpack_small(p):
    def take(r0, r1, shape):
        return p[r0:r1].reshape(shape)

    return dict(b_ada=take(PK_BADA, PK_NORMG, (1, 3 * D)), norm_g=take(PK_NORMG, PK_FINALG, (1, D)),
                final_g=take(PK_FINALG, PK_PSCALE, (D,)), pool_scale=take(PK_PSCALE, PK_RELB, (1, AW)),
                rel_bias=take(PK_RELB, PK_LOSS, (NUM_BUCKETS, NG * NH)), loss=p[PK_LOSS, 0],
                pool_w=take(PK_POOLW, PK_ROWS, (1, 4, PGW, PGW)))


def kernel(x, c, norm_g, w_ada, b_ada, w_in, pool_w, pool_scale, w_attn_br, w_pool_br, w_out, rel_bias, final_g, loss_target, m_norm_g, m_w_ada, m_b_ada, m_w_in, m_pool_w, m_pool_scale, m_w_attn_br, m_w_pool_br, m_w_out, m_rel_bias, m_final_g, v_norm_g, v_w_ada, v_b_ada, v_w_in, v_pool_w, v_pool_scale, v_w_attn_br, v_w_pool_br, v_w_out, v_rel_bias, v_final_g):
    ix, iy, ic = lax.axis_index("x"), lax.axis_index("y"), lax.axis_index("c")
    dev = 4 * ix + 2 * iy + ic
    chip = 2 * ix + iy

    def half(w):
        r2 = w.shape[0] // 2
        return lax.dynamic_slice_in_dim(w, ic * r2, r2, axis=0).astype(BF16)

    gathered = _allgather8([jnp.broadcast_to(c, (8, D)), half(w_in[0]), half(w_attn_br[0]), half(w_pool_br[0]),
                            half(w_out[0])], "gather_weights")
    c_all = gathered[0][:, 0, :]
    wg_in = gathered[1].reshape(N_SHARD, D, IN_W // N_SHARD)
    wab = gathered[2].reshape(N_SHARD, AW, D // N_SHARD).transpose(1, 0, 2).reshape(AW, D)
    wpb = gathered[3].reshape(N_SHARD, AW, D // N_SHARD).transpose(1, 0, 2).reshape(AW, D)
    wout = gathered[4].reshape(D, D)

    mw = 3 * D // N_SHARD
    modp = _mod_partial(c_all, w_ada[0], lax.dynamic_slice_in_dim(b_ada, chip * mw, mw, axis=1))
    mod_all = _allgather8([modp], "gather_mod")[0]
    mod_full = mod_all[::2].transpose(1, 0, 2).reshape(8, 3 * D)
    mod = lax.dynamic_slice_in_dim(mod_full, dev, 1, axis=0)

    r = _local_step(x[0], loss_target[0], mod, wg_in, wab, wpb, wout, pool_w[0], pool_scale, rel_bias, norm_g,
                    final_g.reshape(1, D))

    packed = _pack_small(r["dmod"], r["dnorm_g"], r["dfinal_g"], r["dpool_scale"], r["drel_bias"],
                         jnp.full((128,), r["loss"], F32), r["dpool_w"])
    small_all = _allgather8([packed], "gather_small")[0]
    small_sum = _sum_leading(small_all, "sum_small")
    dmod_all = small_all[:, PK_BADA:PK_NORMG, :].reshape(8, 3 * D)
    g_w_ada = _w_ada_grad(c_all, lax.dynamic_slice_in_dim(dmod_all, chip * mw, mw, axis=1))

    big = [r["dw_in"], r["dw_attn_br"], r["dw_pool_br"], r["dw_out"]]
    half_idx = jnp.stack([ic]).astype(jnp.int32)
    chip_half = jnp.stack([chip, ic]).astype(jnp.int32)
    from_sibling = _sibling_halves(big, "rs_sibling_halves")
    p32s, p16s = zip(*[_pair_sum(g, t, half_idx, f"rs_pair_sum{a}") for a, (g, t) in enumerate(zip(big, from_sibling))])
    pieces = _chip_exchange(list(p16s), "rs_chip_exchange")
    halves = [_chip_sum(p, u, chip_half, f"rs_chip_sum{a}") for a, (p, u) in enumerate(zip(p32s, pieces))]
    g_w_in, g_w_ab, g_w_pb, g_w_out = _sibling_join(halves, "rs_sibling_join")

    small_w = _pack_small(b_ada, norm_g, final_g, pool_scale, rel_bias, jnp.zeros((128,), F32), pool_w)
    small_m = _pack_small(m_b_ada, m_norm_g, m_final_g, m_pool_scale, m_rel_bias, jnp.zeros((128,), F32), m_pool_w)
    small_v = _pack_small(v_b_ada, v_norm_g, v_final_g, v_pool_scale, v_rel_bias, jnp.ones((128,), F32), v_pool_w)
    sd, sm, sv = (_unpack_small(p) for p in _adamw(small_w, small_sum, small_m, small_v, "adamw_small"))
    sg = _unpack_small(small_sum)
    upd = {
        "w_ada": (g_w_ada,) + tuple(_adamw(w_ada[0], g_w_ada, m_w_ada[0], v_w_ada[0], "adamw_w_ada")),
        "w_in": (g_w_in,) + tuple(_adamw(w_in[0], g_w_in, m_w_in[0], v_w_in[0], "adamw_w_in")),
        "w_attn_br": (g_w_ab,) + tuple(_adamw(w_attn_br[0], g_w_ab, m_w_attn_br[0], v_w_attn_br[0], "adamw_w_ab")),
        "w_pool_br": (g_w_pb,) + tuple(_adamw(w_pool_br[0], g_w_pb, m_w_pool_br[0], v_w_pool_br[0], "adamw_w_pb")),
        "w_out": (g_w_out,) + tuple(_adamw(w_out[0], g_w_out, m_w_out[0], v_w_out[0], "adamw_w_out")),
    }
    names = ["norm_g", "w_ada", "b_ada", "w_in", "pool_w", "pool_scale", "w_attn_br", "w_pool_br", "w_out",
             "rel_bias", "final_g"]
    outs = [sg["loss"], r["grad_x"][None]]
    for kind in range(4):
        for nme in names:
            if nme in upd:
                outs.append(upd[nme][kind][None])
            else:
                outs.append((sg, sd, sm, sv)[kind][nme])
    return tuple(outs)
```

```python
import functools
import math

import numpy as np
import jax
import jax.numpy as jnp
from jax import lax
from jax.experimental import pallas as pl
from jax.experimental.pallas import tpu as pltpu

F32 = jnp.float32
BF16 = jnp.bfloat16

D = 1024
HD = 64
NH = 8
AW = NH * HD
GROUPS = ((128, 1), (512, 4), (2048, 16))
NG = len(GROUPS)
BLK = 128
GW = 3 * AW
QKV_W = NG * GW
REST_W = 3584
IN_W = QKV_W + REST_W
CB = 512
NCB = IN_W // CB
NCB_QKV = QKV_W // CB
POOL_WINDOWS = (2, 4, 8, 16)
PGW = 128
HALO = 16
NUM_BUCKETS = 32
MAX_DISTANCE = 2048
EPS = 1e-6
NEG = -1e30
N_SHARD = 4
VMEM_LIMIT = 56 * 1024 * 1024

ADAM_LR = 0.001
ADAM_B1 = 0.9
ADAM_B2 = 0.999
ADAM_EPS = 1e-08
ADAM_WD = 0.01
ADAM_STEP = 10

PK_BADA, PK_NORMG, PK_FINALG, PK_PSCALE, PK_RELB, PK_LOSS, PK_POOLW, PK_ROWS = 0, 24, 32, 40, 44, 50, 56, 568

ANY = pl.BlockSpec(memory_space=pl.ANY)
MESH = pl.DeviceIdType.MESH


def _params(*sem):
    return pltpu.CompilerParams(dimension_semantics=sem, vmem_limit_bytes=VMEM_LIMIT)


def _sds(shape, dtype=F32):
    return jax.ShapeDtypeStruct(shape, dtype)


def _dot(a, b):
    return jnp.dot(a, b, preferred_element_type=F32)


def _dot_nt(a, b):
    return lax.dot_general(a, b, (((1,), (1,)), ((), ())), preferred_element_type=F32)


def _dot_tn(a, b):
    return lax.dot_general(a, b, (((0,), (0,)), ((), ())), preferred_element_type=F32)


def _sigmoid(z):
    return 1.0 / (1.0 + jnp.exp(-z))


def _dma_sems(*shape):
    return pltpu.SemaphoreType.DMA(shape)


class _Ride:
    def __init__(self, arrays, out_shapes, n_copies, copies):
        self.arrays, self.out_shapes, self.n_copies, self.copies = list(arrays), list(out_shapes), n_copies, copies


def _call_with_ride(body, ride, first, last, *, in_specs, out_specs, out_shape, scratch_shapes=(), **kw):
    in_specs, out_specs, out_shape, scratch_shapes = list(in_specs), list(out_specs), list(out_shape), list(scratch_shapes)
    n_in, n_out, n_sc = len(in_specs), len(out_specs), len(scratch_shapes)
    if ride is None:
        def run_plain(*operands):
            return pl.pallas_call(body, in_specs=in_specs, out_specs=out_specs, out_shape=out_shape,
                                  scratch_shapes=scratch_shapes, **kw)(*operands), []
        return run_plain
    n_ri, n_ro = len(ride.arrays), len(ride.out_shapes)

    def wrapped(*refs):
        ins, rest = refs[:n_in], refs[n_in:]
        r_ins, rest = rest[:n_ri], rest[n_ri:]
        outs, rest = rest[:n_out], rest[n_out:]
        r_outs, rest = rest[:n_ro], rest[n_ro:]
        scratch, (send_sems, recv_sems) = rest[:n_sc], rest[n_sc:]

        @pl.when(first())
        def _():
            for cp in ride.copies(r_ins, r_outs, send_sems, recv_sems):
                cp.start()

        body(*ins, *outs, *scratch)

        @pl.when(last())
        def _():
            for cp in ride.copies(r_ins, r_outs, send_sems, recv_sems):
                cp.wait()

    def run(*operands):
        res = pl.pallas_call(
            wrapped, in_specs=in_specs + [ANY] * n_ri, out_specs=out_specs + [ANY] * n_ro,
            out_shape=out_shape + ride.out_shapes,
            scratch_shapes=scratch_shapes + [_dma_sems(ride.n_copies), _dma_sems(ride.n_copies)], **kw,
        )(*operands, *ride.arrays)
        return res[:n_out], res[n_out:]
    return run


def _bucket_tables():
    i = np.arange(BLK)[:, None]
    j = np.arange(2 * BLK)[None, :]
    dist = BLK + i - j
    valid = (dist >= 0) & (dist <= BLK)
    tabs = []
    for _, dil in GROUPS:
        n = (np.clip(dist, 0, BLK) * dil).astype(np.int32)
        max_exact = NUM_BUCKETS // 2
        nf = np.maximum(n, 1).astype(np.float32)
        large = max_exact + (np.log(nf / np.float32(max_exact)) / np.float32(math.log(MAX_DISTANCE / max_exact))
                             * np.float32(NUM_BUCKETS - max_exact)).astype(np.int32)
        large = np.minimum(large, NUM_BUCKETS - 1)
        bucket = np.where(n < max_exact, n, large)
        tabs.append(np.where(valid, bucket, -1).astype(np.int32))
    return np.stack(tabs)


def _bias_table(rel_bias, buckets):
    def body(rb_ref, bk_ref, out_ref):
        gh = pl.program_id(0)
        bk = bk_ref[...]
        acc = jnp.full((BLK, 2 * BLK), NEG, F32)
        for b in range(NUM_BUCKETS):
            acc = jnp.where(bk == b, rb_ref[b, gh], acc)
        out_ref[...] = acc

    return pl.pallas_call(
        body, name="bias_table", grid=(NG * NH,),
        in_specs=[pl.BlockSpec(memory_space=pltpu.SMEM),
                  pl.BlockSpec((None, BLK, 2 * BLK), lambda gh: (gh // NH, 0, 0))],
        out_specs=pl.BlockSpec((None, BLK, 2 * BLK), lambda gh: (gh, 0, 0)),
        out_shape=_sds((NG * NH, BLK, 2 * BLK)),
        compiler_params=_params("arbitrary"),
    )(rel_bias, buckets)


def _bias_grad(ds_acc, buckets, ride):
    def body(acc_ref, bk_ref, out_ref):
        bk = bk_ref[...]
        acc = acc_ref[...]
        lane = lax.broadcasted_iota(jnp.int32, (8, 128), 1)
        out = jnp.zeros((8, 128), F32)
        for b in range(NUM_BUCKETS):
            val = jnp.sum(jnp.where(bk == b, acc, 0.0))
            out = jnp.where(lane == b, val, out)
        out_ref[...] = out

    (out,), rode = _call_with_ride(
        body, ride, lambda: pl.program_id(0) == 0, lambda: pl.program_id(0) == NG * NH - 1,
        name="bias_grad", grid=(NG * NH,),
        in_specs=[pl.BlockSpec((None, BLK, 2 * BLK), lambda gh: (gh, 0, 0)),
                  pl.BlockSpec((None, BLK, 2 * BLK), lambda gh: (gh // NH, 0, 0))],
        out_specs=[pl.BlockSpec((None, 8, 128), lambda gh: (gh, 0, 0))],
        out_shape=[_sds((NG * NH, 8, 128))],
        compiler_params=_params("arbitrary"),
    )(ds_acc, buckets)
    return out, rode


def _mod_partial(c_all, w_ada_s, b_ada_s):
    def body(c_ref, w_ref, b_ref, o_ref):
        o_ref[...] = _dot(c_ref[...].astype(BF16), w_ref[...].astype(BF16)) + b_ref[...]

    return pl.pallas_call(body, name="mod_partial", out_shape=_sds((8, w_ada_s.shape[1])),
                          compiler_params=_params())(c_all, w_ada_s, b_ada_s)


def _prenorm(x, norm_g, mod):
    S = x.shape[0]
    tm = 512

    def body(x_ref, g_ref, mod_ref, h_ref):
        xv = x_ref[...]
        r = lax.rsqrt(jnp.mean(xv * xv, axis=-1, keepdims=True) + EPS)
        n1 = xv * r * g_ref[...]
        h_ref[...] = (n1 * (1.0 + mod_ref[:, D:2 * D]) + mod_ref[:, 0:D]).astype(BF16)

    return pl.pallas_call(
        body, name="prenorm", grid=(S // tm,),
        in_specs=[pl.BlockSpec((tm, D), lambda i: (i, 0)), pl.BlockSpec((1, D), lambda i: (0, 0)),
                  pl.BlockSpec((1, 3 * D), lambda i: (0, 0))],
        out_specs=pl.BlockSpec((tm, D), lambda i: (i, 0)),
        out_shape=_sds((S, D), BF16), compiler_params=_params("parallel"),
    )(x, norm_g, mod)


def _proj(h, wg_in, j0, nj, dtype, name):
    S = h.shape[0]
    tm = 1024
    per = wg_in.shape[2] // CB

    def body(h_ref, w_ref, o_ref):
        o_ref[...] = _dot(h_ref[...], w_ref[...]).astype(dtype)

    return pl.pallas_call(
        body, name=name, grid=(S // tm, nj),
        in_specs=[pl.BlockSpec((tm, D), lambda m, j: (m, 0)),
                  pl.BlockSpec((None, D, CB), lambda m, j: ((j0 + j) // per, 0, (j0 + j) % per))],
        out_specs=pl.BlockSpec((tm, CB), lambda m, j: (m, j)),
        out_shape=_sds((S, nj * CB), dtype), compiler_params=_params("parallel", "parallel"),
    )(h, wg_in)


def _attn_fwd(qkv_g, bias_tab, g):
    S = qkv_g.shape[0]
    dil = GROUPS[g][1]
    L = S // dil
    nb = L // BLK
    view = qkv_g.reshape(L, dil * GW)

    def body(q_ref, kp_ref, kc_ref, vp_ref, vc_ref, b_ref, o_ref, l_ref):
        n = pl.program_id(1)
        col = lax.broadcasted_iota(jnp.int32, (BLK, 2 * BLK), 1)
        keep = (col >= BLK) | (n > 0)
        for h in range(NH):
            sl = slice(h * HD, (h + 1) * HD)
            kb = jnp.concatenate([kp_ref[:, sl], kc_ref[:, sl]], axis=0)
            vb = jnp.concatenate([vp_ref[:, sl], vc_ref[:, sl]], axis=0)
            s = _dot_nt(q_ref[:, sl], kb) * (HD ** -0.5) + b_ref[h]
            s = jnp.where(keep, s, NEG)
            m = jnp.max(s, axis=-1, keepdims=True)
            p = jnp.exp(s - m)
            den = jnp.sum(p, axis=-1, keepdims=True)
            o_ref[:, sl] = _dot(p.astype(BF16), vb) / den
            l_ref[:, sl] = jnp.broadcast_to(m + jnp.log(den), (BLK, HD))

    def spec(t, prev):
        if prev:
            return pl.BlockSpec((BLK, AW), lambda r, n: (jnp.maximum(n - 1, 0), r * 3 + t))
        return pl.BlockSpec((BLK, AW), lambda r, n: (n, r * 3 + t))

    out_spec = pl.BlockSpec((BLK, AW), lambda r, n: (n, r))
    o, l = pl.pallas_call(
        body, name=f"attn_fwd{g}", grid=(dil, nb),
        in_specs=[spec(0, False), spec(1, True), spec(1, False), spec(2, True), spec(2, False),
                  pl.BlockSpec((NH, BLK, 2 * BLK), lambda r, n: (g, 0, 0))],
        out_specs=[out_spec, out_spec],
        out_shape=[_sds((L, dil * AW)), _sds((L, dil * AW))],
        compiler_params=_params("parallel", "arbitrary"),
    )(view, view, view, view, view, bias_tab)
    return o.reshape(S, AW), l.reshape(S, AW)


def _attn_bwd(qkv_g, dattn, attn, lj, bias_tab, g, ride):
    S = qkv_g.shape[0]
    dil = GROUPS[g][1]
    L = S // dil
    nb = L // BLK
    view = qkv_g.reshape(L, dil * GW)
    da_v, at_v, lj_v = (a.reshape(L, dil * AW) for a in (dattn, attn, lj))

    def body(q_ref, kp_ref, kc_ref, vp_ref, vc_ref, da_ref, at_ref, lj_ref, b_ref,
             dq_ref, dk_ref, dv_ref, ds_ref, ck_ref, cv_ref):
        r = pl.program_id(0)
        n = pl.program_id(1)

        @pl.when((r == 0) & (n == 0))
        def _():
            ds_ref[...] = jnp.zeros_like(ds_ref)

        @pl.when(n == 0)
        def _():
            ck_ref[...] = jnp.zeros_like(ck_ref)
            cv_ref[...] = jnp.zeros_like(cv_ref)

        @pl.when(n < nb)
        def _():
            col = lax.broadcasted_iota(jnp.int32, (BLK, 2 * BLK), 1)
            keep = (col >= BLK) | (n > 0)
            for h in range(NH):
                sl = slice(h * HD, (h + 1) * HD)
                q = q_ref[:, sl]
                kb = jnp.concatenate([kp_ref[:, sl], kc_ref[:, sl]], axis=0)
                vb = jnp.concatenate([vp_ref[:, sl], vc_ref[:, sl]], axis=0)
                do = da_ref[:, sl]
                delta = jnp.sum(do * at_ref[:, sl], axis=-1, keepdims=True)
                s = _dot_nt(q, kb) * (HD ** -0.5) + b_ref[h]
                s = jnp.where(keep, s, NEG)
                p = jnp.exp(s - lj_ref[:, h * HD:h * HD + 1])
                do_b = do.astype(BF16)
                dp = _dot_nt(do_b, vb)
                ds = p * (dp - delta)
                ds_ref[h] += ds
                ds_b = (ds * (HD ** -0.5)).astype(BF16)
                dq_ref[:, sl] = _dot(ds_b, kb).astype(BF16)
                dkb = _dot_tn(ds_b, q)
                dvb = _dot_tn(p.astype(BF16), do_b)
                dk_ref[:, sl] = (ck_ref[:, sl] + dkb[:BLK]).astype(BF16)
                dv_ref[:, sl] = (cv_ref[:, sl] + dvb[:BLK]).astype(BF16)
                ck_ref[:, sl] = dkb[BLK:]
                cv_ref[:, sl] = dvb[BLK:]

        @pl.when(n == nb)
        def _():
            dk_ref[...] = ck_ref[...].astype(BF16)
            dv_ref[...] = cv_ref[...].astype(BF16)

    def cur(t):
        return pl.BlockSpec((BLK, AW), lambda r, n: (jnp.minimum(n, nb - 1), r * 3 + t))

    def prev(t):
        return pl.BlockSpec((BLK, AW), lambda r, n: (jnp.clip(n - 1, 0, nb - 1), r * 3 + t))

    row = pl.BlockSpec((BLK, AW), lambda r, n: (jnp.minimum(n, nb - 1), r))
    late = pl.BlockSpec((BLK, AW), lambda r, n: (jnp.maximum(n - 1, 0), r))
    (dq, dk, dv, ds_acc), rode = _call_with_ride(
        body, ride, lambda: (pl.program_id(0) == 0) & (pl.program_id(1) == 0),
        lambda: (pl.program_id(0) == dil - 1) & (pl.program_id(1) == nb),
        name=f"attn_bwd{g}", grid=(dil, nb + 1),
        in_specs=[cur(0), prev(1), cur(1), prev(2), cur(2), row, row, row,
                  pl.BlockSpec((NH, BLK, 2 * BLK), lambda r, n: (g, 0, 0))],
        out_specs=[row, late, late, pl.BlockSpec((NH, BLK, 2 * BLK), lambda r, n: (0, 0, 0))],
        out_shape=[_sds((L, dil * AW), BF16)] * 3 + [_sds((NH, BLK, 2 * BLK))],
        scratch_shapes=[pltpu.VMEM((BLK, AW), F32), pltpu.VMEM((BLK, AW), F32)],
        compiler_params=_params("arbitrary", "arbitrary"),
    )(view, view, view, view, view, da_v, at_v, lj_v, bias_tab)
    return [dq.reshape(S, AW), dk.reshape(S, AW), dv.reshape(S, AW)], ds_acc, rode


TM_MIX = 256


def _mix_specs(tm):
    row512 = pl.BlockSpec((tm, AW), lambda i: (i, 0))
    return ([row512] * 6 + [
        pl.BlockSpec((tm, REST_W), lambda i: (i, 0)),
        pl.BlockSpec((HALO, AW), lambda i: (jnp.maximum(i * (tm // HALO) - 1, 0), 1)),
        pl.BlockSpec((AW, D), lambda i: (0, 0)), pl.BlockSpec((AW, D), lambda i: (0, 0)),
        pl.BlockSpec((4, PGW, PGW), lambda i: (0, 0, 0)), pl.BlockSpec((1, AW), lambda i: (0, 0))])


def _mix_forward(i, tm, o_refs, l_refs, rest_ref, halo_ref, wab_ref, wpb_ref, pw_ref, ps_ref):
    l0, l1, l2 = (r[...] for r in l_refs)
    mx = jnp.maximum(jnp.maximum(l0, l1), l2)
    e0, e1, e2 = jnp.exp(l0 - mx), jnp.exp(l1 - mx), jnp.exp(l2 - mx)
    den = e0 + e1 + e2
    lj = mx + jnp.log(den)
    attn = (e0 * o_refs[0][...] + e1 * o_refs[1][...] + e2 * o_refs[2][...]) / den

    z_attn = rest_ref[:, 0:AW]
    u = rest_ref[:, AW:2 * AW]
    z_pool = rest_ref[:, 2 * AW:3 * AW]
    g_attn = rest_ref[:, 3 * AW:3 * AW + D]
    g_pool = rest_ref[:, 3 * AW + D:3 * AW + 2 * D]

    sg_a = _sigmoid(z_attn)
    sil_a = z_attn * sg_a
    a_g = (attn * sil_a).astype(BF16)
    y_attn = _dot(a_g, wab_ref[...])

    halo = jnp.where(i > 0, halo_ref[...], 0.0)
    ext = jnp.concatenate([halo, u], axis=0)
    t = i * tm + lax.broadcasted_iota(jnp.int32, (tm, 1), 0)
    pooled, mixed_raw = [], []
    for gi, win in enumerate(POOL_WINDOWS):
        s = ext[:, gi * PGW:(gi + 1) * PGW]
        sh = 1
        while sh < win:
            s = s + pltpu.roll(s, sh, 0)
            sh *= 2
        cnt = jnp.minimum(t + 1, win).astype(F32)
        pg = s[HALO:] / cnt - u[:, gi * PGW:(gi + 1) * PGW]
        pooled.append(pg.astype(BF16))
        mixed_raw.append(_dot(pooled[-1], pw_ref[gi].astype(BF16)))
    mixed_raw = jnp.concatenate(mixed_raw, axis=1)
    mixed = mixed_raw * ps_ref[...]
    sg_p = _sigmoid(z_pool)
    sil_p = z_pool * sg_p
    m_g = (mixed * sil_p).astype(BF16)
    y_pool = _dot(m_g, wpb_ref[...])

    sa = _sigmoid(g_attn)
    sp = _sigmoid(g_pool)
    merged = sa * y_attn + sp * y_pool
    return dict(lj=lj, attn=attn, z_attn=z_attn, z_pool=z_pool, sg_a=sg_a, sil_a=sil_a, a_g=a_g, y_attn=y_attn,
                pooled=pooled, mixed_raw=mixed_raw, mixed=mixed, sg_p=sg_p, sil_p=sil_p, m_g=m_g, y_pool=y_pool,
                sa=sa, sp=sp, merged=merged)


def _tail(x, target, os_, ls_, rest, wab, wpb, pool_w, pool_scale, wout, mod, final_g):
    S = x.shape[0]
    tm = TM_MIX

    def body(o0, o1, o2, l0, l1, l2, rest_ref, halo_ref, wab_ref, wpb_ref, pw_ref, ps_ref,
             x_ref, t_ref, wo_ref, mod_ref, fg_ref, dx2_ref, dmo_ref, loss_ref, dfg_ref, dgate_ref):
        i = pl.program_id(0)

        @pl.when(i == 0)
        def _():
            loss_ref[...] = jnp.zeros_like(loss_ref)
            dfg_ref[...] = jnp.zeros_like(dfg_ref)
            dgate_ref[...] = jnp.zeros_like(dgate_ref)

        f = _mix_forward(i, tm, (o0, o1, o2), (l0, l1, l2), rest_ref, halo_ref, wab_ref, wpb_ref, pw_ref, ps_ref)
        mo = _dot(f["merged"].astype(BF16), wo_ref[...])
        gate = mod_ref[:, 2 * D:3 * D]
        fg = fg_ref[...]
        x2 = x_ref[...] + gate * mo
        r2 = lax.rsqrt(jnp.mean(x2 * x2, axis=-1, keepdims=True) + EPS)
        n2 = x2 * r2
        err = n2 * fg - t_ref[...]
        loss_ref[...] += 0.5 * jnp.sum(jnp.mean(err * err, axis=-1, keepdims=True))
        dy = err * (1.0 / D)
        dfg_ref[...] += jnp.sum(dy * n2, axis=0, keepdims=True)
        dn = dy * fg
        dx2 = r2 * (dn - n2 * jnp.mean(dn * n2, axis=-1, keepdims=True))
        dgate_ref[...] += jnp.sum(dx2 * mo, axis=0, keepdims=True)
        dx2_ref[...] = dx2
        dmo_ref[...] = (dx2 * gate).astype(BF16)

    row = pl.BlockSpec((tm, D), lambda i: (i, 0))
    vec = pl.BlockSpec((1, D), lambda i: (0, 0))
    return pl.pallas_call(
        body, name="tail", grid=(S // tm,),
        in_specs=_mix_specs(tm) + [row, row, pl.BlockSpec((D, D), lambda i: (0, 0)),
                                   pl.BlockSpec((1, 3 * D), lambda i: (0, 0)), vec],
        out_specs=[row, row, pl.BlockSpec((8, 128), lambda i: (0, 0)), vec, vec],
        out_shape=[_sds((S, D)), _sds((S, D), BF16), _sds((8, 128)), _sds((1, D)), _sds((1, D))],
        compiler_params=_params("arbitrary"),
    )(*os_, *ls_, rest, rest, wab, wpb, pool_w, pool_scale, x, target, wout, mod, final_g)


def _mix_bwd(dmo, os_, ls_, rest, wab, wpb, pool_w, pool_scale, wout):
    S = dmo.shape[0]
    tm = TM_MIX
    nt = S // tm
    sw = D // N_SHARD

    def body(o0, o1, o2, l0, l1, l2, rest_ref, halo_ref, wab_ref, wpb_ref, pw_ref, ps_ref, dmo_ref, wo_ref,
             dattn_ref, attn_ref, lj_ref, dpooled_ref, drest_ref, dwo_hbm, dwab_hbm, dwpb_hbm, dpw_ref, dps_ref,
             awo, awab, awpb):
        i = pl.program_id(0)

        @pl.when(i == 0)
        def _():
            awo[...] = jnp.zeros_like(awo)
            awab[...] = jnp.zeros_like(awab)
            awpb[...] = jnp.zeros_like(awpb)
            dpw_ref[...] = jnp.zeros_like(dpw_ref)
            dps_ref[...] = jnp.zeros_like(dps_ref)

        f = _mix_forward(i, tm, (o0, o1, o2), (l0, l1, l2), rest_ref, halo_ref, wab_ref, wpb_ref, pw_ref, ps_ref)
        dmo_b = dmo_ref[...]
        dmerged = _dot_nt(dmo_b, wo_ref[...])
        awo[...] += _dot_tn(f["merged"].astype(BF16), dmo_b)
        sa, sp = f["sa"], f["sp"]
        dya = (dmerged * sa).astype(BF16)
        dyp = (dmerged * sp).astype(BF16)
        dg_attn = dmerged * f["y_attn"] * sa * (1.0 - sa)
        dg_pool = dmerged * f["y_pool"] * sp * (1.0 - sp)
        dag = _dot_nt(dya, wab_ref[...])
        awab[...] += _dot_tn(f["a_g"], dya)
        dmg = _dot_nt(dyp, wpb_ref[...])
        awpb[...] += _dot_tn(f["m_g"], dyp)
        dattn_ref[...] = dag * f["sil_a"]
        attn_ref[...] = f["attn"]
        lj_ref[...] = f["lj"]
        dz_attn = dag * f["attn"] * (f["sg_a"] * (1.0 + f["z_attn"] * (1.0 - f["sg_a"])))
        dmixed = dmg * f["sil_p"]
        dz_pool = dmg * f["mixed"] * (f["sg_p"] * (1.0 + f["z_pool"] * (1.0 - f["sg_p"])))
        dps_ref[...] += jnp.sum(dmixed * f["mixed_raw"], axis=0, keepdims=True)
        dpm = (dmixed * ps_ref[...]).astype(BF16)
        for gi in range(len(POOL_WINDOWS)):
            cs = slice(gi * PGW, (gi + 1) * PGW)
            dpw_ref[gi] += _dot_tn(f["pooled"][gi], dpm[:, cs])
            dpooled_ref[:, cs] = _dot_nt(dpm[:, cs], pw_ref[gi].astype(BF16))
        drest_ref[:, 0:AW] = dz_attn.astype(BF16)
        drest_ref[:, AW:2 * AW] = jnp.zeros((tm, AW), BF16)
        drest_ref[:, 2 * AW:3 * AW] = dz_pool.astype(BF16)
        drest_ref[:, 3 * AW:3 * AW + D] = dg_attn.astype(BF16)
        drest_ref[:, 3 * AW + D:3 * AW + 2 * D] = dg_pool.astype(BF16)

        @pl.when(i == nt - 1)
        def _():
            pltpu.sync_copy(awo, dwo_hbm)
            for k in range(N_SHARD):
                pltpu.sync_copy(awab.at[:, pl.ds(k * sw, sw)], dwab_hbm.at[k])
                pltpu.sync_copy(awpb.at[:, pl.ds(k * sw, sw)], dwpb_hbm.at[k])

    row512 = pl.BlockSpec((tm, AW), lambda i: (i, 0))
    outs = pl.pallas_call(
        body, name="mix_bwd", grid=(nt,),
        in_specs=_mix_specs(tm) + [pl.BlockSpec((tm, D), lambda i: (i, 0)), pl.BlockSpec((D, D), lambda i: (0, 0))],
        out_specs=[row512, row512, row512, row512, pl.BlockSpec((tm, REST_W), lambda i: (i, 0)), ANY, ANY, ANY,
                   pl.BlockSpec((4, PGW, PGW), lambda i: (0, 0, 0)), pl.BlockSpec((1, AW), lambda i: (0, 0))],
        out_shape=[_sds((S, AW)), _sds((S, AW)), _sds((S, AW)), _sds((S, AW)), _sds((S, REST_W), BF16),
                   _sds((D, D)), _sds((N_SHARD, AW, sw)), _sds((N_SHARD, AW, sw)), _sds((4, PGW, PGW)), _sds((1, AW))],
        scratch_shapes=[pltpu.VMEM((D, D), F32), pltpu.VMEM((AW, D), F32), pltpu.VMEM((AW, D), F32)],
        compiler_params=_params("arbitrary"),
    )(*os_, *ls_, rest, rest, wab, wpb, pool_w, pool_scale, dmo, wout)
    dattn, attn, lj, dpooled, drest, dwo, dwab, dwpb, dpw, dps = outs
    return dattn, attn, lj, dpooled, drest, dwo.reshape(N_SHARD, D // N_SHARD, D), dwab, dwpb, dpw, dps


def _pool_bwd(dpooled):
    S = dpooled.shape[0]
    tm = 512
    nt = S // tm

    def body(dp_ref, nxt_ref, du_ref):
        i = pl.program_id(0)
        t = i * tm + lax.broadcasted_iota(jnp.int32, (tm + HALO, 1), 0)
        nxt = jnp.where(i < nt - 1, nxt_ref[...], 0.0)
        ext = jnp.concatenate([dp_ref[...], nxt], axis=0)
        for gi, win in enumerate(POOL_WINDOWS):
            cs = slice(gi * PGW, (gi + 1) * PGW)
            s = ext[:, cs] / jnp.minimum(t + 1, win).astype(F32)
            sh = 1
            while sh < win:
                s = s + pltpu.roll(s, tm + HALO - sh, 0)
                sh *= 2
            du_ref[:, cs] = (s[:tm] - dp_ref[:, cs]).astype(BF16)

    return pl.pallas_call(
        body, name="pool_bwd", grid=(nt,),
        in_specs=[pl.BlockSpec((tm, AW), lambda i: (i, 0)),
                  pl.BlockSpec((HALO, AW), lambda i: (jnp.minimum((i + 1) * (tm // HALO), S // HALO - 1), 0))],
        out_specs=pl.BlockSpec((tm, AW), lambda i: (i, 0)),
        out_shape=_sds((S, AW), BF16), compiler_params=_params("parallel"),
    )(dpooled, dpooled)


TB = 1024


def _dh(dproj, wg_in, ride):
    S = dproj.shape[0]
    per = wg_in.shape[2] // TB
    nm, nk = S // TB, IN_W // TB

    def body(dp_ref, w_ref, out_ref):
        @pl.when(pl.program_id(1) == 0)
        def _():
            out_ref[...] = jnp.zeros_like(out_ref)

        out_ref[...] += _dot_nt(dp_ref[...], w_ref[...])

    (dh,), rode = _call_with_ride(
        body, ride, lambda: (pl.program_id(0) == 0) & (pl.program_id(1) == 0),
        lambda: (pl.program_id(0) == nm - 1) & (pl.program_id(1) == nk - 1),
        name="dh", grid=(nm, nk),
        in_specs=[pl.BlockSpec((TB, TB), lambda m, kk: (m, kk)),
                  pl.BlockSpec((None, D, TB), lambda m, kk: (kk // per, 0, kk % per))],
        out_specs=[pl.BlockSpec((TB, D), lambda m, kk: (m, 0))],
        out_shape=[_sds((S, D))], compiler_params=_params("arbitrary", "arbitrary"),
    )(dproj, wg_in)
    return dh, rode


def _dw_in(h_t, dproj):
    S = dproj.shape[0]
    per = IN_W // N_SHARD // TB

    def body(ht_ref, dp_ref, out_ref):
        @pl.when(pl.program_id(1) == 0)
        def _():
            out_ref[...] = jnp.zeros_like(out_ref)

        out_ref[...] += _dot(ht_ref[...], dp_ref[...])

    return pl.pallas_call(
        body, name="dw_in", grid=(IN_W // TB, S // TB),
        in_specs=[pl.BlockSpec((D, TB), lambda j, kk: (0, kk)), pl.BlockSpec((TB, TB), lambda j, kk: (kk, j))],
        out_specs=pl.BlockSpec((None, D, TB), lambda j, kk: (j // per, 0, j % per)),
        out_shape=_sds((N_SHARD, D, IN_W // N_SHARD)), compiler_params=_params("parallel", "arbitrary"),
    )(h_t, dproj)


def _prenorm_bwd(x, dh, dx2, norm_g, mod):
    S = x.shape[0]
    tm = 512

    def body(x_ref, dh_ref, dx2_ref, g_ref, mod_ref, gx_ref, dg_ref, dshift_ref, dscale_ref):
        i = pl.program_id(0)

        @pl.when(i == 0)
        def _():
            dg_ref[...] = jnp.zeros_like(dg_ref)
            dshift_ref[...] = jnp.zeros_like(dshift_ref)
            dscale_ref[...] = jnp.zeros_like(dscale_ref)

        xv = x_ref[...]
        dhv = dh_ref[...]
        g = g_ref[...]
        r = lax.rsqrt(jnp.mean(xv * xv, axis=-1, keepdims=True) + EPS)
        xh = xv * r
        dshift_ref[...] += jnp.sum(dhv, axis=0, keepdims=True)
        dscale_ref[...] += jnp.sum(dhv * (xh * g), axis=0, keepdims=True)
        dn1 = dhv * (1.0 + mod_ref[:, D:2 * D])
        dg_ref[...] += jnp.sum(dn1 * xh, axis=0, keepdims=True)
        dxh = dn1 * g
        gx_ref[...] = dx2_ref[...] + r * (dxh - xh * jnp.mean(dxh * xh, axis=-1, keepdims=True))

    row = pl.BlockSpec((tm, D), lambda i: (i, 0))
    vec = pl.BlockSpec((1, D), lambda i: (0, 0))
    return pl.pallas_call(
        body, name="prenorm_bwd", grid=(S // tm,),
        in_specs=[row, row, row, vec, pl.BlockSpec((1, 3 * D), lambda i: (0, 0))],
        out_specs=[row, vec, vec, vec],
        out_shape=[_sds((S, D)), _sds((1, D)), _sds((1, D)), _sds((1, D))],
        compiler_params=_params("arbitrary"),
    )(x, dh, dx2, norm_g, mod)


def _local_step(x, target, mod, wg_in, wab, wpb, wout, pool_w, pool_scale, rel_bias, norm_g, final_g, half_idx,
                chip_half):
    buckets = jnp.asarray(_bucket_tables())
    bias_tab = _bias_table(rel_bias, buckets)
    h = _prenorm(x, norm_g, mod)
    qkv = [_proj(h, wg_in, 3 * g, 3, BF16, f"proj_qkv{g}") for g in range(NG)]
    rest = _proj(h, wg_in, NCB_QKV, REST_W // CB, F32, "proj_rest")
    os_, ls_ = zip(*[_attn_fwd(qkv[g], bias_tab, g) for g in range(NG)])
    dx2, dmo, loss, dfinal_g, dgate = _tail(x, target, os_, ls_, rest, wab, wpb, pool_w, pool_scale, wout, mod, final_g)
    dattn, attn, lj, dpooled, drest, dw_out, dw_ab, dw_pb, dpool_w, dpool_scale = _mix_bwd(
        dmo, os_, ls_, rest, wab, wpb, pool_w, pool_scale, wout)
    du = _pool_bwd(dpooled)

    small = [dw_ab, dw_pb, dw_out]
    dqkv0, ds0, sib_small = _attn_bwd(qkv[0], dattn, attn, lj, bias_tab, 0, _ride_sibling_halves(small))
    p_small = [_pair_sum(g, t, half_idx, f"rs_pair_sum{a}") for a, (g, t) in enumerate(zip(small, sib_small))]
    dqkv1, ds1, u_small = _attn_bwd(qkv[1], dattn, attn, lj, bias_tab, 1,
                                    _ride_chip_exchange([p16 for _, p16 in p_small]))
    rs_ab, rs_pb, rs_out = [_chip_sum(p32, u, chip_half, f"rs_chip_sum{a}")
                            for a, ((p32, _), u) in enumerate(zip(p_small, u_small))]
    dqkv2, ds2, _ = _attn_bwd(qkv[2], dattn, attn, lj, bias_tab, 2, None)

    dproj = jnp.concatenate(dqkv0 + dqkv1 + dqkv2 + [drest[:, :AW], du, drest[:, 2 * AW:]], axis=1)
    dw_in = _dw_in(h.T, dproj)
    drel_rows, (sib_in,) = _bias_grad(jnp.concatenate([ds0, ds1, ds2], axis=0), buckets,
                                      _ride_sibling_halves([dw_in]))
    drel = drel_rows[:, 0, :NUM_BUCKETS].T
    p32_in, p16_in = _pair_sum(dw_in, sib_in, half_idx, "rs_pair_sum_in")
    dh, (u_in,) = _dh(dproj, wg_in, _ride_chip_exchange([p16_in]))
    rs_in = _chip_sum(p32_in, u_in, chip_half, "rs_chip_sum_in")

    grad_x, dnorm_g, dshift, dscale = _prenorm_bwd(x, dh, dx2, norm_g, mod)
    dmod = jnp.concatenate([dshift, dscale, dgate], axis=1)
    return dict(loss=loss[0, 0], grad_x=grad_x, dmod=dmod, dnorm_g=dnorm_g, dfinal_g=dfinal_g, dpool_w=dpool_w,
                dpool_scale=dpool_scale, drel_bias=drel, dw_in=dw_in, dw_attn_br=dw_ab, dw_pool_br=dw_pb,
                dw_out=dw_out, rs_in=rs_in, rs_attn_br=rs_ab, rs_pool_br=rs_pb, rs_out=rs_out)


def _allgather8(blocks, name):
    nb = len(blocks)

    def body(*refs):
        ins, outs = refs[:nb], refs[nb:2 * nb]
        send_sems, recv_sems, local_sems = refs[2 * nb:]
        x, y, c = lax.axis_index("x"), lax.axis_index("y"), lax.axis_index("c")
        me, sibling = (x, y, c), (x, y, 1 - c)
        chips = [(1 - x, y), (x, 1 - y), (1 - x, 1 - y)]

        def copy(a, k, block, to, src=None):
            dst = outs[a].at[4 * block[0] + 2 * block[1] + block[2]]
            return pltpu.make_async_remote_copy(src_ref=dst if src is None else src, dst_ref=dst,
                                                send_sem=send_sems.at[a, k], recv_sem=recv_sems.at[a, k],
                                                device_id=to, device_id_type=MESH)

        mine = [pltpu.make_async_copy(ins[a], outs[a].at[4 * x + 2 * y + c], local_sems.at[a]) for a in range(nb)]
        for cp in mine:
            cp.start()
        first = []
        for a in range(nb):
            first.append(copy(a, 0, me, sibling, src=ins[a]))
            first += [copy(a, 1 + j, me, (*chip, c), src=ins[a]) for j, chip in enumerate(chips)]
        for cp in first:
            cp.start()
        passed = []
        for j, chip in enumerate(chips):
            for a in range(nb):
                copy(a, 1 + j, (*chip, c), me).wait_recv()
                cp = copy(a, 4 + j, (*chip, c), sibling)
                cp.start()
                passed.append(cp)
        for a in range(nb):
            copy(a, 0, sibling, me).wait_recv()
            for j, chip in enumerate(chips):
                copy(a, 4 + j, (*chip, 1 - c), me).wait_recv()
        for cp in first + passed:
            cp.wait_send()
        for cp in mine:
            cp.wait()

    return pl.pallas_call(
        body, name=name, in_specs=[ANY] * nb, out_specs=[ANY] * nb,
        out_shape=[_sds((8,) + b.shape, b.dtype) for b in blocks],
        scratch_shapes=[_dma_sems(nb, 7), _dma_sems(nb, 7), _dma_sems(nb)],
    )(*blocks)


def _ride_sibling_halves(gs):
    def copies(ins, outs, send_sems, recv_sems):
        x, y, c = lax.axis_index("x"), lax.axis_index("y"), lax.axis_index("c")
        cps = []
        for a in range(len(gs)):
            r2 = ins[a].shape[1] // 2
            other = ins[a].at[:, pl.ds((1 - c) * r2, r2), :]
            cps.append(pltpu.make_async_remote_copy(src_ref=other, dst_ref=outs[a], send_sem=send_sems.at[a],
                                                    recv_sem=recv_sems.at[a], device_id=(x, y, 1 - c),
                                                    device_id_type=MESH))
        return cps

    return _Ride(gs, [_sds((g.shape[0], g.shape[1] // 2, g.shape[2]), g.dtype) for g in gs], len(gs), copies)


def _pair_sum(g, t, half, name):
    nsh, rows, cols = g.shape
    r2 = rows // 2
    tr = _row_tile(r2, cols)
    nt = r2 // tr

    def body(half_ref, g_ref, t_ref, p32_ref, p16_ref):
        p = g_ref[...] + t_ref[...]
        p32_ref[...] = p
        p16_ref[...] = p.astype(BF16)

    blk = pl.BlockSpec((None, tr, cols), lambda k, i, half_ref: (k, i, 0))
    return pl.pallas_call(
        body, name=name,
        grid_spec=pltpu.PrefetchScalarGridSpec(
            num_scalar_prefetch=1, grid=(nsh, nt),
            in_specs=[pl.BlockSpec((None, tr, cols), lambda k, i, half_ref: (k, half_ref[0] * nt + i, 0)), blk],
            out_specs=[blk, blk]),
        out_shape=[_sds((nsh, r2, cols)), _sds((nsh, r2, cols), BF16)],
        compiler_params=_params("parallel", "parallel"),
    )(half, g, t)


def _ride_chip_exchange(ps):
    def copies(ins, outs, send_sems, recv_sems):
        x, y, c = lax.axis_index("x"), lax.axis_index("y"), lax.axis_index("c")
        chips = [(1 - x, y), (x, 1 - y), (1 - x, 1 - y)]
        cps = []
        for a in range(len(ps)):
            for j, (ox, oy) in enumerate(chips):
                cps.append(pltpu.make_async_remote_copy(src_ref=ins[a].at[2 * ox + oy], dst_ref=outs[a].at[j],
                                                        send_sem=send_sems.at[3 * a + j],
                                                        recv_sem=recv_sems.at[3 * a + j],
                                                        device_id=(ox, oy, c), device_id_type=MESH))
        return cps

    return _Ride(ps, [_sds((3,) + p.shape[1:], p.dtype) for p in ps], 3 * len(ps), copies)


def _chip_sum(p32, u, chip_half, name):
    r2, cols = p32.shape[1:]
    tr = _row_tile(r2, cols)
    nt = r2 // tr

    def body(ch_ref, p_ref, u_ref, o_ref):
        acc = p_ref[...]
        for j in range(3):
            acc = acc + u_ref[j].astype(F32)
        o_ref[...] = acc

    return pl.pallas_call(
        body, name=name,
        grid_spec=pltpu.PrefetchScalarGridSpec(
            num_scalar_prefetch=1, grid=(nt,),
            in_specs=[pl.BlockSpec((None, tr, cols), lambda i, ch_ref: (ch_ref[0], i, 0)),
                      pl.BlockSpec((3, tr, cols), lambda i, ch_ref: (0, i, 0))],
            out_specs=pl.BlockSpec((tr, cols), lambda i, ch_ref: (ch_ref[1] * nt + i, 0))),
        out_shape=_sds((2 * r2, cols)), compiler_params=_params("parallel"),
    )(chip_half, p32, u)


def _sibling_join(fs, name):
    nb = len(fs)

    def body(*refs):
        outs = refs[nb:2 * nb]
        send_sems, recv_sems = refs[2 * nb:]
        x, y, c = lax.axis_index("x"), lax.axis_index("y"), lax.axis_index("c")
        cps = []
        for a in range(nb):
            r2 = outs[a].shape[0] // 2
            rows = outs[a].at[pl.ds(c * r2, r2), :]
            cps.append(pltpu.make_async_remote_copy(src_ref=rows, dst_ref=rows, send_sem=send_sems.at[a],
                                                    recv_sem=recv_sems.at[a], device_id=(x, y, 1 - c),
                                                    device_id_type=MESH))
        for cp in cps:
            cp.start()
        for cp in cps:
            cp.wait()

    return pl.pallas_call(
        body, name=name, in_specs=[ANY] * nb, out_specs=[ANY] * nb,
        out_shape=[_sds(f.shape, f.dtype) for f in fs],
        input_output_aliases={a: a for a in range(nb)},
        scratch_shapes=[_dma_sems(nb), _dma_sems(nb)],
    )(*fs)


def _row_tile(rows, cols):
    tile = rows
    while tile * cols * 4 > (1 << 20) and tile % 16 == 0:
        tile //= 2
    return tile


def _sum_leading(a, name):
    k = a.shape[0]
    a3 = a.reshape(k, -1, a.shape[-1])
    rows, cols = a3.shape[1:]
    tr = _row_tile(rows, cols)

    def body(a_ref, o_ref):
        acc = a_ref[0]
        for s in range(1, k):
            acc = acc + a_ref[s]
        o_ref[...] = acc

    out = pl.pallas_call(
        body, name=name, grid=(rows // tr,),
        in_specs=[pl.BlockSpec((k, tr, cols), lambda i: (0, i, 0))],
        out_specs=pl.BlockSpec((tr, cols), lambda i: (i, 0)),
        out_shape=_sds((rows, cols), a.dtype), compiler_params=_params("parallel"),
    )(a3)
    return out.reshape(a.shape[1:])


def _w_ada_grad(c_all, dmod_cols):
    def body(c_ref, d_ref, o_ref):
        o_ref[...] = _dot_tn(c_ref[...].astype(BF16), d_ref[...].astype(BF16))

    return pl.pallas_call(body, name="w_ada_grad", out_shape=_sds((c_all.shape[1], dmod_cols.shape[1])),
                          compiler_params=_params())(c_all, dmod_cols)


def _adamw(w, g, m, v, name):
    rows, cols = w.shape
    tr = _row_tile(rows, cols)

    def body(w_ref, g_ref, m_ref, v_ref, d_ref, nm_ref, nv_ref):
        gv = g_ref[...]
        nm = ADAM_B1 * m_ref[...] + (1.0 - ADAM_B1) * gv
        nv = ADAM_B2 * v_ref[...] + (1.0 - ADAM_B2) * (gv * gv)
        m_hat = nm / (1.0 - ADAM_B1 ** ADAM_STEP)
        v_hat = nv / (1.0 - ADAM_B2 ** ADAM_STEP)
        d_ref[...] = -ADAM_LR * (m_hat / (jnp.sqrt(v_hat) + ADAM_EPS) + ADAM_WD * w_ref[...])
        nm_ref[...] = nm
        nv_ref[...] = nv

    spec = pl.BlockSpec((tr, cols), lambda i: (i, 0))
    return pl.pallas_call(
        body, name=name, grid=(rows // tr,), in_specs=[spec] * 4, out_specs=[spec] * 3,
        out_shape=[_sds((rows, cols))] * 3, compiler_params=_params("parallel"),
    )(w, g, m, v)


def _pack_small(b_ada, norm_g, final_g, pool_scale, rel_bias, loss_row, pool_w):
    pad = jnp.zeros((PK_POOLW - PK_LOSS - 1) * 128, F32)
    flat = jnp.concatenate([b_ada.reshape(-1), norm_g.reshape(-1), final_g.reshape(-1), pool_scale.reshape(-1),
                            rel_bias.reshape(-1), loss_row.reshape(-1), pad, pool_w.reshape(-1)])
    return flat.reshape(PK_ROWS, 128)


def _unpack_small(p):
    def take(r0, r1, shape):
        return p[r0:r1].reshape(shape)

    return dict(b_ada=take(PK_BADA, PK_NORMG, (1, 3 * D)), norm_g=take(PK_NORMG, PK_FINALG, (1, D)),
                final_g=take(PK_FINALG, PK_PSCALE, (D,)), pool_scale=take(PK_PSCALE, PK_RELB, (1, AW)),
                rel_bias=take(PK_RELB, PK_LOSS, (NUM_BUCKETS, NG * NH)), loss=p[PK_LOSS, 0],
                pool_w=take(PK_POOLW, PK_ROWS, (1, 4, PGW, PGW)))


def kernel(x, c, norm_g, w_ada, b_ada, w_in, pool_w, pool_scale, w_attn_br, w_pool_br, w_out, rel_bias, final_g, loss_target, m_norm_g, m_w_ada, m_b_ada, m_w_in, m_pool_w, m_pool_scale, m_w_attn_br, m_w_pool_br, m_w_out, m_rel_bias, m_final_g, v_norm_g, v_w_ada, v_b_ada, v_w_in, v_pool_w, v_pool_scale, v_w_attn_br, v_w_pool_br, v_w_out, v_rel_bias, v_final_g):
    ix, iy, ic = lax.axis_index("x"), lax.axis_index("y"), lax.axis_index("c")
    dev = 4 * ix + 2 * iy + ic
    chip = 2 * ix + iy

    def half(w):
        r2 = w.shape[0] // 2
        return lax.dynamic_slice_in_dim(w, ic * r2, r2, axis=0).astype(BF16)

    gathered = _allgather8([jnp.broadcast_to(c, (8, D)), half(w_in[0]), half(w_attn_br[0]), half(w_pool_br[0]),
                            half(w_out[0])], "gather_weights")
    c_all = gathered[0][:, 0, :]
    wg_in = gathered[1].reshape(N_SHARD, D, IN_W // N_SHARD)
    wab = gathered[2].reshape(N_SHARD, AW, D // N_SHARD).transpose(1, 0, 2).reshape(AW, D)
    wpb = gathered[3].reshape(N_SHARD, AW, D // N_SHARD).transpose(1, 0, 2).reshape(AW, D)
    wout = gathered[4].reshape(D, D)

    mw = 3 * D // N_SHARD
    modp = _mod_partial(c_all, w_ada[0], lax.dynamic_slice_in_dim(b_ada, chip * mw, mw, axis=1))
    mod_all = _allgather8([modp], "gather_mod")[0]
    mod_full = mod_all[::2].transpose(1, 0, 2).reshape(8, 3 * D)
    mod = lax.dynamic_slice_in_dim(mod_full, dev, 1, axis=0)

    half_idx = jnp.stack([ic]).astype(jnp.int32)
    chip_half = jnp.stack([chip, ic]).astype(jnp.int32)
    r = _local_step(x[0], loss_target[0], mod, wg_in, wab, wpb, wout, pool_w[0], pool_scale, rel_bias, norm_g,
                    final_g.reshape(1, D), half_idx, chip_half)

    packed = _pack_small(r["dmod"], r["dnorm_g"], r["dfinal_g"], r["dpool_scale"], r["drel_bias"],
                         jnp.full((128,), r["loss"], F32), r["dpool_w"])
    small_all = _allgather8([packed], "gather_small")[0]
    small_sum = _sum_leading(small_all, "sum_small")
    dmod_all = small_all[:, PK_BADA:PK_NORMG, :].reshape(8, 3 * D)
    g_w_ada = _w_ada_grad(c_all, lax.dynamic_slice_in_dim(dmod_all, chip * mw, mw, axis=1))

    g_w_in, g_w_ab, g_w_pb, g_w_out = _sibling_join([r["rs_in"], r["rs_attn_br"], r["rs_pool_br"], r["rs_out"]],
                                                    "rs_sibling_join")

    small_w = _pack_small(b_ada, norm_g, final_g, pool_scale, rel_bias, jnp.zeros((128,), F32), pool_w)
    small_m = _pack_small(m_b_ada, m_norm_g, m_final_g, m_pool_scale, m_rel_bias, jnp.zeros((128,), F32), m_pool_w)
    small_v = _pack_small(v_b_ada, v_norm_g, v_final_g, v_pool_scale, v_rel_bias, jnp.ones((128,), F32), v_pool_w)
    sd, sm, sv = (_unpack_small(p) for p in _adamw(small_w, small_sum, small_m, small_v, "adamw_small"))
    sg = _unpack_small(small_sum)
    upd = {
        "w_ada": (g_w_ada,) + tuple(_adamw(w_ada[0], g_w_ada, m_w_ada[0], v_w_ada[0], "adamw_w_ada")),
        "w_in": (g_w_in,) + tuple(_adamw(w_in[0], g_w_in, m_w_in[0], v_w_in[0], "adamw_w_in")),
        "w_attn_br": (g_w_ab,) + tuple(_adamw(w_attn_br[0], g_w_ab, m_w_attn_br[0], v_w_attn_br[0], "adamw_w_ab")),
        "w_pool_br": (g_w_pb,) + tuple(_adamw(w_pool_br[0], g_w_pb, m_w_pool_br[0], v_w_pool_br[0], "adamw_w_pb")),
        "w_out": (g_w_out,) + tuple(_adamw(w_out[0], g_w_out, m_w_out[0], v_w_out[0], "adamw_w_out")),
    }
    names = ["norm_g", "w_ada", "b_ada", "w_in", "pool_w", "pool_scale", "w_attn_br", "w_pool_br", "w_out",
             "rel_bias", "final_g"]
    outs = [sg["loss"], r["grad_x"][None]]
    for kind in range(4):
        for nme in names:
            if nme in upd:
                outs.append(upd[nme][kind][None])
            else:
                outs.append((sg, sd, sm, sv)[kind][nme])
    return tuple(outs)
```

```python
import functools
import math

import numpy as np
import jax
import jax.numpy as jnp
from jax import lax
from jax.experimental import pallas as pl
from jax.experimental.pallas import tpu as pltpu

F32 = jnp.float32
BF16 = jnp.bfloat16

D = 1024
HD = 64
NH = 8
AW = NH * HD
GROUPS = ((128, 1), (512, 4), (2048, 16))
NG = len(GROUPS)
BLK = 128
GW = 3 * AW
QKV_W = NG * GW
REST_W = 3584
IN_W = QKV_W + REST_W
CB = 512
NCB = IN_W // CB
NCB_QKV = QKV_W // CB
POOL_WINDOWS = (2, 4, 8, 16)
PGW = 128
HALO = 16
NUM_BUCKETS = 32
MAX_DISTANCE = 2048
EPS = 1e-6
NEG = -1e30
N_SHARD = 4
VMEM_LIMIT = 56 * 1024 * 1024

ADAM_LR = 0.001
ADAM_B1 = 0.9
ADAM_B2 = 0.999
ADAM_EPS = 1e-08
ADAM_WD = 0.01
ADAM_STEP = 10

PK_BADA, PK_NORMG, PK_FINALG, PK_PSCALE, PK_RELB, PK_LOSS, PK_POOLW, PK_ROWS = 0, 24, 32, 40, 44, 50, 56, 568

ANY = pl.BlockSpec(memory_space=pl.ANY)
MESH = pl.DeviceIdType.MESH


def _params(*sem):
    return pltpu.CompilerParams(dimension_semantics=sem, vmem_limit_bytes=VMEM_LIMIT)


def _sds(shape, dtype=F32):
    return jax.ShapeDtypeStruct(shape, dtype)


def _dot(a, b):
    return jnp.dot(a, b, preferred_element_type=F32)


def _dot_nt(a, b):
    return lax.dot_general(a, b, (((1,), (1,)), ((), ())), preferred_element_type=F32)


def _dot_tn(a, b):
    return lax.dot_general(a, b, (((0,), (0,)), ((), ())), preferred_element_type=F32)


def _sigmoid(z):
    return 1.0 / (1.0 + jnp.exp(-z))


def _dma_sems(*shape):
    return pltpu.SemaphoreType.DMA(shape)


class _Ride:
    def __init__(self, arrays, out_shapes, n_copies, copies):
        self.arrays, self.out_shapes, self.n_copies, self.copies = list(arrays), list(out_shapes), n_copies, copies


def _call_with_ride(body, ride, first, last, *, in_specs, out_specs, out_shape, scratch_shapes=(), **kw):
    in_specs, out_specs, out_shape, scratch_shapes = list(in_specs), list(out_specs), list(out_shape), list(scratch_shapes)
    n_in, n_out, n_sc = len(in_specs), len(out_specs), len(scratch_shapes)
    if ride is None:
        def run_plain(*operands):
            return pl.pallas_call(body, in_specs=in_specs, out_specs=out_specs, out_shape=out_shape,
                                  scratch_shapes=scratch_shapes, **kw)(*operands), []
        return run_plain
    n_ri, n_ro = len(ride.arrays), len(ride.out_shapes)

    def wrapped(*refs):
        ins, rest = refs[:n_in], refs[n_in:]
        r_ins, rest = rest[:n_ri], rest[n_ri:]
        outs, rest = rest[:n_out], rest[n_out:]
        r_outs, rest = rest[:n_ro], rest[n_ro:]
        scratch, (send_sems, recv_sems) = rest[:n_sc], rest[n_sc:]

        @pl.when(first())
        def _():
            for cp in ride.copies(r_ins, r_outs, send_sems, recv_sems):
                cp.start()

        body(*ins, *outs, *scratch)

        @pl.when(last())
        def _():
            for cp in ride.copies(r_ins, r_outs, send_sems, recv_sems):
                cp.wait()

    def run(*operands):
        res = pl.pallas_call(
            wrapped, in_specs=in_specs + [ANY] * n_ri, out_specs=out_specs + [ANY] * n_ro,
            out_shape=out_shape + ride.out_shapes,
            scratch_shapes=scratch_shapes + [_dma_sems(ride.n_copies), _dma_sems(ride.n_copies)], **kw,
        )(*operands, *ride.arrays)
        return res[:n_out], res[n_out:]
    return run


def _bucket_tables():
    i = np.arange(BLK)[:, None]
    j = np.arange(2 * BLK)[None, :]
    dist = BLK + i - j
    valid = (dist >= 0) & (dist <= BLK)
    tabs = []
    for _, dil in GROUPS:
        n = (np.clip(dist, 0, BLK) * dil).astype(np.int32)
        max_exact = NUM_BUCKETS // 2
        nf = np.maximum(n, 1).astype(np.float32)
        large = max_exact + (np.log(nf / np.float32(max_exact)) / np.float32(math.log(MAX_DISTANCE / max_exact))
                             * np.float32(NUM_BUCKETS - max_exact)).astype(np.int32)
        large = np.minimum(large, NUM_BUCKETS - 1)
        bucket = np.where(n < max_exact, n, large)
        tabs.append(np.where(valid, bucket, -1).astype(np.int32))
    return np.stack(tabs)


def _bias_table(rel_bias, buckets):
    def body(rb_ref, bk_ref, out_ref):
        gh = pl.program_id(0)
        bk = bk_ref[...]
        acc = jnp.full((BLK, 2 * BLK), NEG, F32)
        for b in range(NUM_BUCKETS):
            acc = jnp.where(bk == b, rb_ref[b, gh], acc)
        out_ref[...] = acc

    return pl.pallas_call(
        body, name="bias_table", grid=(NG * NH,),
        in_specs=[pl.BlockSpec(memory_space=pltpu.SMEM),
                  pl.BlockSpec((None, BLK, 2 * BLK), lambda gh: (gh // NH, 0, 0))],
        out_specs=pl.BlockSpec((None, BLK, 2 * BLK), lambda gh: (gh, 0, 0)),
        out_shape=_sds((NG * NH, BLK, 2 * BLK)),
        compiler_params=_params("arbitrary"),
    )(rel_bias, buckets)


def _bias_grad(ds_acc, buckets, ride):
    def body(acc_ref, bk_ref, out_ref):
        bk = bk_ref[...]
        acc = acc_ref[...]
        lane = lax.broadcasted_iota(jnp.int32, (8, 128), 1)
        out = jnp.zeros((8, 128), F32)
        for b in range(NUM_BUCKETS):
            val = jnp.sum(jnp.where(bk == b, acc, 0.0))
            out = jnp.where(lane == b, val, out)
        out_ref[...] = out

    (out,), rode = _call_with_ride(
        body, ride, lambda: pl.program_id(0) == 0, lambda: pl.program_id(0) == NG * NH - 1,
        name="bias_grad", grid=(NG * NH,),
        in_specs=[pl.BlockSpec((None, BLK, 2 * BLK), lambda gh: (gh, 0, 0)),
                  pl.BlockSpec((None, BLK, 2 * BLK), lambda gh: (gh // NH, 0, 0))],
        out_specs=[pl.BlockSpec((None, 8, 128), lambda gh: (gh, 0, 0))],
        out_shape=[_sds((NG * NH, 8, 128))],
        compiler_params=_params("arbitrary"),
    )(ds_acc, buckets)
    return out, rode


def _mod_partial(c_all, w_ada_s, b_ada_s):
    def body(c_ref, w_ref, b_ref, o_ref):
        o_ref[...] = _dot(c_ref[...].astype(BF16), w_ref[...].astype(BF16)) + b_ref[...]

    return pl.pallas_call(body, name="mod_partial", out_shape=_sds((8, w_ada_s.shape[1])),
                          compiler_params=_params())(c_all, w_ada_s, b_ada_s)


def _prenorm(x, norm_g, mod):
    S = x.shape[0]
    tm = 512

    def body(x_ref, g_ref, mod_ref, h_ref):
        xv = x_ref[...]
        r = lax.rsqrt(jnp.mean(xv * xv, axis=-1, keepdims=True) + EPS)
        n1 = xv * r * g_ref[...]
        h_ref[...] = (n1 * (1.0 + mod_ref[:, D:2 * D]) + mod_ref[:, 0:D]).astype(BF16)

    return pl.pallas_call(
        body, name="prenorm", grid=(S // tm,),
        in_specs=[pl.BlockSpec((tm, D), lambda i: (i, 0)), pl.BlockSpec((1, D), lambda i: (0, 0)),
                  pl.BlockSpec((1, 3 * D), lambda i: (0, 0))],
        out_specs=pl.BlockSpec((tm, D), lambda i: (i, 0)),
        out_shape=_sds((S, D), BF16), compiler_params=_params("parallel"),
    )(x, norm_g, mod)


def _proj(h, wg_in, j0, nj, dtype, name):
    S = h.shape[0]
    tm = 1024
    per = wg_in.shape[2] // CB

    def body(h_ref, w_ref, o_ref):
        o_ref[...] = _dot(h_ref[...], w_ref[...]).astype(dtype)

    return pl.pallas_call(
        body, name=name, grid=(S // tm, nj),
        in_specs=[pl.BlockSpec((tm, D), lambda m, j: (m, 0)),
                  pl.BlockSpec((None, D, CB), lambda m, j: ((j0 + j) // per, 0, (j0 + j) % per))],
        out_specs=pl.BlockSpec((tm, CB), lambda m, j: (m, j)),
        out_shape=_sds((S, nj * CB), dtype), compiler_params=_params("parallel", "parallel"),
    )(h, wg_in)


HS = 4
SLAB = HS * HD


def _lane_head(rows):
    return lax.broadcasted_iota(jnp.int32, (rows, SLAB), 1) // HD


def _head_stack(a):
    head = _lane_head(a.shape[0])
    return jnp.concatenate([jnp.where(head == h, a, jnp.zeros_like(a)) for h in range(HS)], axis=0)


def _head_unstack(a):
    rows = a.shape[0] // HS
    head = _lane_head(rows)
    out = a[:rows]
    for h in range(1, HS):
        out = jnp.where(head == h, a[h * rows:(h + 1) * rows], out)
    return out
def _attn_fwd(qkv_g, bias_tab, g):
    S = qkv_g.shape[0]
    dil = GROUPS[g][1]
    L = S // dil
    nb = L // BLK
    view = qkv_g.reshape(L, dil * GW)

    def body(q_ref, kp_ref, kc_ref, vp_ref, vc_ref, b_ref, o_ref, l_ref):
        n = pl.program_id(1)
        col = lax.broadcasted_iota(jnp.int32, (HS * BLK, 2 * BLK), 1)
        keep = (col >= BLK) | (n > 0)
        for sb in range(NH // HS):
            cs = slice(sb * SLAB, (sb + 1) * SLAB)
            kb = jnp.concatenate([kp_ref[:, cs], kc_ref[:, cs]], axis=0)
            vb = jnp.concatenate([vp_ref[:, cs], vc_ref[:, cs]], axis=0)
            s = _dot_nt(_head_stack(q_ref[:, cs]), kb) * (HD ** -0.5)
            s = s + b_ref[pl.ds(sb * HS, HS)].reshape(HS * BLK, 2 * BLK)
            s = jnp.where(keep, s, NEG)
            m = jnp.max(s, axis=-1, keepdims=True)
            p = jnp.exp(s - m)
            den = jnp.sum(p, axis=-1, keepdims=True)
            o_ref[:, cs] = _head_unstack(_dot(p.astype(BF16), vb) / den)
            l_ref[:, cs] = _head_unstack(jnp.broadcast_to(m + jnp.log(den), (HS * BLK, SLAB)))

    def spec(t, prev):
        if prev:
            return pl.BlockSpec((BLK, AW), lambda r, n: (jnp.maximum(n - 1, 0), r * 3 + t))
        return pl.BlockSpec((BLK, AW), lambda r, n: (n, r * 3 + t))

    out_spec = pl.BlockSpec((BLK, AW), lambda r, n: (n, r))
    o, l = pl.pallas_call(
        body, name=f"attn_fwd{g}", grid=(dil, nb),
        in_specs=[spec(0, False), spec(1, True), spec(1, False), spec(2, True), spec(2, False),
                  pl.BlockSpec((NH, BLK, 2 * BLK), lambda r, n: (g, 0, 0))],
        out_specs=[out_spec, out_spec],
        out_shape=[_sds((L, dil * AW)), _sds((L, dil * AW))],
        compiler_params=_params("parallel", "arbitrary"),
    )(view, view, view, view, view, bias_tab)
    return o.reshape(S, AW), l.reshape(S, AW)


def _attn_bwd(qkv_g, dattn, attn, lj, bias_tab, g, ride):
    S = qkv_g.shape[0]
    dil = GROUPS[g][1]
    L = S // dil
    nb = L // BLK
    view = qkv_g.reshape(L, dil * GW)
    da_v, at_v, lj_v = (a.reshape(L, dil * AW) for a in (dattn, attn, lj))

    def body(q_ref, kp_ref, kc_ref, vp_ref, vc_ref, da_ref, at_ref, lj_ref, b_ref,
             dq_ref, dk_ref, dv_ref, ds_ref, ck_ref, cv_ref):
        r = pl.program_id(0)
        n = pl.program_id(1)

        @pl.when((r == 0) & (n == 0))
        def _():
            ds_ref[...] = jnp.zeros_like(ds_ref)

        @pl.when(n == 0)
        def _():
            ck_ref[...] = jnp.zeros_like(ck_ref)
            cv_ref[...] = jnp.zeros_like(cv_ref)

        @pl.when(n < nb)
        def _():
            col = lax.broadcasted_iota(jnp.int32, (HS * BLK, 2 * BLK), 1)
            keep = (col >= BLK) | (n > 0)
            head = _lane_head(BLK)
            for sb in range(NH // HS):
                cs = slice(sb * SLAB, (sb + 1) * SLAB)
                kb = jnp.concatenate([kp_ref[:, cs], kc_ref[:, cs]], axis=0)
                vb = jnp.concatenate([vp_ref[:, cs], vc_ref[:, cs]], axis=0)
                do = da_ref[:, cs]
                prod = do * at_ref[:, cs]
                delta = jnp.concatenate([jnp.sum(jnp.where(head == h, prod, 0.0), axis=-1, keepdims=True)
                                         for h in range(HS)], axis=0)
                lse = jnp.concatenate([lj_ref[:, (sb * HS + h) * HD:(sb * HS + h) * HD + 1] for h in range(HS)],
                                      axis=0)
                qs = _head_stack(q_ref[:, cs])
                dos = _head_stack(do.astype(BF16))
                s = _dot_nt(qs, kb) * (HD ** -0.5) + b_ref[pl.ds(sb * HS, HS)].reshape(HS * BLK, 2 * BLK)
                s = jnp.where(keep, s, NEG)
                p = jnp.exp(s - lse)
                ds = p * (_dot_nt(dos, vb) - delta)
                ds_ref[pl.ds(sb * HS, HS)] += ds.reshape(HS, BLK, 2 * BLK)
                ds_b = (ds * (HD ** -0.5)).astype(BF16)
                dq_ref[:, cs] = _head_unstack(_dot(ds_b, kb)).astype(BF16)
                dkb = _dot_tn(ds_b, qs)
                dvb = _dot_tn(p.astype(BF16), dos)
                dk_ref[:, cs] = (ck_ref[:, cs] + dkb[:BLK]).astype(BF16)
                dv_ref[:, cs] = (cv_ref[:, cs] + dvb[:BLK]).astype(BF16)
                ck_ref[:, cs] = dkb[BLK:]
                cv_ref[:, cs] = dvb[BLK:]

        @pl.when(n == nb)
        def _():
            dk_ref[...] = ck_ref[...].astype(BF16)
            dv_ref[...] = cv_ref[...].astype(BF16)

    def cur(t):
        return pl.BlockSpec((BLK, AW), lambda r, n: (jnp.minimum(n, nb - 1), r * 3 + t))

    def prev(t):
        return pl.BlockSpec((BLK, AW), lambda r, n: (jnp.clip(n - 1, 0, nb - 1), r * 3 + t))

    row = pl.BlockSpec((BLK, AW), lambda r, n: (jnp.minimum(n, nb - 1), r))
    late = pl.BlockSpec((BLK, AW), lambda r, n: (jnp.maximum(n - 1, 0), r))
    (dq, dk, dv, ds_acc), rode = _call_with_ride(
        body, ride, lambda: (pl.program_id(0) == 0) & (pl.program_id(1) == 0),
        lambda: (pl.program_id(0) == dil - 1) & (pl.program_id(1) == nb),
        name=f"attn_bwd{g}", grid=(dil, nb + 1),
        in_specs=[cur(0), prev(1), cur(1), prev(2), cur(2), row, row, row,
                  pl.BlockSpec((NH, BLK, 2 * BLK), lambda r, n: (g, 0, 0))],
        out_specs=[row, late, late, pl.BlockSpec((NH, BLK, 2 * BLK), lambda r, n: (0, 0, 0))],
        out_shape=[_sds((L, dil * AW), BF16)] * 3 + [_sds((NH, BLK, 2 * BLK))],
        scratch_shapes=[pltpu.VMEM((BLK, AW), F32), pltpu.VMEM((BLK, AW), F32)],
        compiler_params=_params("arbitrary", "arbitrary"),
    )(view, view, view, view, view, da_v, at_v, lj_v, bias_tab)
    return [dq.reshape(S, AW), dk.reshape(S, AW), dv.reshape(S, AW)], ds_acc, rode


TM_MIX = 256


def _mix_specs(tm):
    row512 = pl.BlockSpec((tm, AW), lambda i: (i, 0))
    return ([row512] * 6 + [
        pl.BlockSpec((tm, REST_W), lambda i: (i, 0)),
        pl.BlockSpec((HALO, AW), lambda i: (jnp.maximum(i * (tm // HALO) - 1, 0), 1)),
        pl.BlockSpec((AW, D), lambda i: (0, 0)), pl.BlockSpec((AW, D), lambda i: (0, 0)),
        pl.BlockSpec((4, PGW, PGW), lambda i: (0, 0, 0)), pl.BlockSpec((1, AW), lambda i: (0, 0))])


def _mix_forward(i, tm, o_refs, l_refs, rest_ref, halo_ref, wab_ref, wpb_ref, pw_ref, ps_ref):
    l0, l1, l2 = (r[...] for r in l_refs)
    mx = jnp.maximum(jnp.maximum(l0, l1), l2)
    e0, e1, e2 = jnp.exp(l0 - mx), jnp.exp(l1 - mx), jnp.exp(l2 - mx)
    den = e0 + e1 + e2
    lj = mx + jnp.log(den)
    attn = (e0 * o_refs[0][...] + e1 * o_refs[1][...] + e2 * o_refs[2][...]) / den

    z_attn = rest_ref[:, 0:AW]
    u = rest_ref[:, AW:2 * AW]
    z_pool = rest_ref[:, 2 * AW:3 * AW]
    g_attn = rest_ref[:, 3 * AW:3 * AW + D]
    g_pool = rest_ref[:, 3 * AW + D:3 * AW + 2 * D]

    sg_a = _sigmoid(z_attn)
    sil_a = z_attn * sg_a
    a_g = (attn * sil_a).astype(BF16)
    y_attn = _dot(a_g, wab_ref[...])

    halo = jnp.where(i > 0, halo_ref[...], 0.0)
    ext = jnp.concatenate([halo, u], axis=0)
    t = i * tm + lax.broadcasted_iota(jnp.int32, (tm, 1), 0)
    pooled, mixed_raw = [], []
    for gi, win in enumerate(POOL_WINDOWS):
        s = ext[:, gi * PGW:(gi + 1) * PGW]
        sh = 1
        while sh < win:
            s = s + pltpu.roll(s, sh, 0)
            sh *= 2
        cnt = jnp.minimum(t + 1, win).astype(F32)
        pg = s[HALO:] / cnt - u[:, gi * PGW:(gi + 1) * PGW]
        pooled.append(pg.astype(BF16))
        mixed_raw.append(_dot(pooled[-1], pw_ref[gi].astype(BF16)))
    mixed_raw = jnp.concatenate(mixed_raw, axis=1)
    mixed = mixed_raw * ps_ref[...]
    sg_p = _sigmoid(z_pool)
    sil_p = z_pool * sg_p
    m_g = (mixed * sil_p).astype(BF16)
    y_pool = _dot(m_g, wpb_ref[...])

    sa = _sigmoid(g_attn)
    sp = _sigmoid(g_pool)
    merged = sa * y_attn + sp * y_pool
    return dict(lj=lj, attn=attn, z_attn=z_attn, z_pool=z_pool, sg_a=sg_a, sil_a=sil_a, a_g=a_g, y_attn=y_attn,
                pooled=pooled, mixed_raw=mixed_raw, mixed=mixed, sg_p=sg_p, sil_p=sil_p, m_g=m_g, y_pool=y_pool,
                sa=sa, sp=sp, merged=merged)


def _tail(x, target, os_, ls_, rest, wab, wpb, pool_w, pool_scale, wout, mod, final_g):
    S = x.shape[0]
    tm = TM_MIX

    def body(o0, o1, o2, l0, l1, l2, rest_ref, halo_ref, wab_ref, wpb_ref, pw_ref, ps_ref,
             x_ref, t_ref, wo_ref, mod_ref, fg_ref, dx2_ref, dmo_ref, loss_ref, dfg_ref, dgate_ref):
        i = pl.program_id(0)

        @pl.when(i == 0)
        def _():
            loss_ref[...] = jnp.zeros_like(loss_ref)
            dfg_ref[...] = jnp.zeros_like(dfg_ref)
            dgate_ref[...] = jnp.zeros_like(dgate_ref)

        f = _mix_forward(i, tm, (o0, o1, o2), (l0, l1, l2), rest_ref, halo_ref, wab_ref, wpb_ref, pw_ref, ps_ref)
        mo = _dot(f["merged"].astype(BF16), wo_ref[...])
        gate = mod_ref[:, 2 * D:3 * D]
        fg = fg_ref[...]
        x2 = x_ref[...] + gate * mo
        r2 = lax.rsqrt(jnp.mean(x2 * x2, axis=-1, keepdims=True) + EPS)
        n2 = x2 * r2
        err = n2 * fg - t_ref[...]
        loss_ref[...] += 0.5 * jnp.sum(jnp.mean(err * err, axis=-1, keepdims=True))
        dy = err * (1.0 / D)
        dfg_ref[...] += jnp.sum(dy * n2, axis=0, keepdims=True)
        dn = dy * fg
        dx2 = r2 * (dn - n2 * jnp.mean(dn * n2, axis=-1, keepdims=True))
        dgate_ref[...] += jnp.sum(dx2 * mo, axis=0, keepdims=True)
        dx2_ref[...] = dx2
        dmo_ref[...] = (dx2 * gate).astype(BF16)

    row = pl.BlockSpec((tm, D), lambda i: (i, 0))
    vec = pl.BlockSpec((1, D), lambda i: (0, 0))
    return pl.pallas_call(
        body, name="tail", grid=(S // tm,),
        in_specs=_mix_specs(tm) + [row, row, pl.BlockSpec((D, D), lambda i: (0, 0)),
                                   pl.BlockSpec((1, 3 * D), lambda i: (0, 0)), vec],
        out_specs=[row, row, pl.BlockSpec((8, 128), lambda i: (0, 0)), vec, vec],
        out_shape=[_sds((S, D)), _sds((S, D), BF16), _sds((8, 128)), _sds((1, D)), _sds((1, D))],
        compiler_params=_params("arbitrary"),
    )(*os_, *ls_, rest, rest, wab, wpb, pool_w, pool_scale, x, target, wout, mod, final_g)


def _mix_bwd(dmo, os_, ls_, rest, wab, wpb, pool_w, pool_scale, wout):
    S = dmo.shape[0]
    tm = TM_MIX
    nt = S // tm
    sw = D // N_SHARD

    def body(o0, o1, o2, l0, l1, l2, rest_ref, halo_ref, wab_ref, wpb_ref, pw_ref, ps_ref, dmo_ref, wo_ref,
             dattn_ref, attn_ref, lj_ref, dpooled_ref, drest_ref, dwo_hbm, dwab_hbm, dwpb_hbm, dpw_ref, dps_ref,
             awo, awab, awpb):
        i = pl.program_id(0)

        @pl.when(i == 0)
        def _():
            awo[...] = jnp.zeros_like(awo)
            awab[...] = jnp.zeros_like(awab)
            awpb[...] = jnp.zeros_like(awpb)
            dpw_ref[...] = jnp.zeros_like(dpw_ref)
            dps_ref[...] = jnp.zeros_like(dps_ref)

        f = _mix_forward(i, tm, (o0, o1, o2), (l0, l1, l2), rest_ref, halo_ref, wab_ref, wpb_ref, pw_ref, ps_ref)
        dmo_b = dmo_ref[...]
        dmerged = _dot_nt(dmo_b, wo_ref[...])
        awo[...] += _dot_tn(f["merged"].astype(BF16), dmo_b)
        sa, sp = f["sa"], f["sp"]
        dya = (dmerged * sa).astype(BF16)
        dyp = (dmerged * sp).astype(BF16)
        dg_attn = dmerged * f["y_attn"] * sa * (1.0 - sa)
        dg_pool = dmerged * f["y_pool"] * sp * (1.0 - sp)
        dag = _dot_nt(dya, wab_ref[...])
        awab[...] += _dot_tn(f["a_g"], dya)
        dmg = _dot_nt(dyp, wpb_ref[...])
        awpb[...] += _dot_tn(f["m_g"], dyp)
        dattn_ref[...] = dag * f["sil_a"]
        attn_ref[...] = f["attn"]
        lj_ref[...] = f["lj"]
        dz_attn = dag * f["attn"] * (f["sg_a"] * (1.0 + f["z_attn"] * (1.0 - f["sg_a"])))
        dmixed = dmg * f["sil_p"]
        dz_pool = dmg * f["mixed"] * (f["sg_p"] * (1.0 + f["z_pool"] * (1.0 - f["sg_p"])))
        dps_ref[...] += jnp.sum(dmixed * f["mixed_raw"], axis=0, keepdims=True)
        dpm = (dmixed * ps_ref[...]).astype(BF16)
        for gi in range(len(POOL_WINDOWS)):
            cs = slice(gi * PGW, (gi + 1) * PGW)
            dpw_ref[gi] += _dot_tn(f["pooled"][gi], dpm[:, cs])
            dpooled_ref[:, cs] = _dot_nt(dpm[:, cs], pw_ref[gi].astype(BF16))
        drest_ref[:, 0:AW] = dz_attn.astype(BF16)
        drest_ref[:, AW:2 * AW] = jnp.zeros((tm, AW), BF16)
        drest_ref[:, 2 * AW:3 * AW] = dz_pool.astype(BF16)
        drest_ref[:, 3 * AW:3 * AW + D] = dg_attn.astype(BF16)
        drest_ref[:, 3 * AW + D:3 * AW + 2 * D] = dg_pool.astype(BF16)

        @pl.when(i == nt - 1)
        def _():
            pltpu.sync_copy(awo, dwo_hbm)
            for k in range(N_SHARD):
                pltpu.sync_copy(awab.at[:, pl.ds(k * sw, sw)], dwab_hbm.at[k])
                pltpu.sync_copy(awpb.at[:, pl.ds(k * sw, sw)], dwpb_hbm.at[k])

    row512 = pl.BlockSpec((tm, AW), lambda i: (i, 0))
    outs = pl.pallas_call(
        body, name="mix_bwd", grid=(nt,),
        in_specs=_mix_specs(tm) + [pl.BlockSpec((tm, D), lambda i: (i, 0)), pl.BlockSpec((D, D), lambda i: (0, 0))],
        out_specs=[row512, row512, row512, row512, pl.BlockSpec((tm, REST_W), lambda i: (i, 0)), ANY, ANY, ANY,
                   pl.BlockSpec((4, PGW, PGW), lambda i: (0, 0, 0)), pl.BlockSpec((1, AW), lambda i: (0, 0))],
        out_shape=[_sds((S, AW)), _sds((S, AW)), _sds((S, AW)), _sds((S, AW)), _sds((S, REST_W), BF16),
                   _sds((D, D)), _sds((N_SHARD, AW, sw)), _sds((N_SHARD, AW, sw)), _sds((4, PGW, PGW)), _sds((1, AW))],
        scratch_shapes=[pltpu.VMEM((D, D), F32), pltpu.VMEM((AW, D), F32), pltpu.VMEM((AW, D), F32)],
        compiler_params=_params("arbitrary"),
    )(*os_, *ls_, rest, rest, wab, wpb, pool_w, pool_scale, dmo, wout)
    dattn, attn, lj, dpooled, drest, dwo, dwab, dwpb, dpw, dps = outs
    return dattn, attn, lj, dpooled, drest, dwo.reshape(N_SHARD, D // N_SHARD, D), dwab, dwpb, dpw, dps


def _pool_bwd(dpooled):
    S = dpooled.shape[0]
    tm = 512
    nt = S // tm

    def body(dp_ref, nxt_ref, du_ref):
        i = pl.program_id(0)
        t = i * tm + lax.broadcasted_iota(jnp.int32, (tm + HALO, 1), 0)
        nxt = jnp.where(i < nt - 1, nxt_ref[...], 0.0)
        ext = jnp.concatenate([dp_ref[...], nxt], axis=0)
        for gi, win in enumerate(POOL_WINDOWS):
            cs = slice(gi * PGW, (gi + 1) * PGW)
            s = ext[:, cs] / jnp.minimum(t + 1, win).astype(F32)
            sh = 1
            while sh < win:
                s = s + pltpu.roll(s, tm + HALO - sh, 0)
                sh *= 2
            du_ref[:, cs] = (s[:tm] - dp_ref[:, cs]).astype(BF16)

    return pl.pallas_call(
        body, name="pool_bwd", grid=(nt,),
        in_specs=[pl.BlockSpec((tm, AW), lambda i: (i, 0)),
                  pl.BlockSpec((HALO, AW), lambda i: (jnp.minimum((i + 1) * (tm // HALO), S // HALO - 1), 0))],
        out_specs=pl.BlockSpec((tm, AW), lambda i: (i, 0)),
        out_shape=_sds((S, AW), BF16), compiler_params=_params("parallel"),
    )(dpooled, dpooled)


TB = 1024


def _dh(dproj, wg_in, ride):
    S = dproj.shape[0]
    per = wg_in.shape[2] // TB
    nm, nk = S // TB, IN_W // TB

    def body(dp_ref, w_ref, out_ref):
        @pl.when(pl.program_id(1) == 0)
        def _():
            out_ref[...] = jnp.zeros_like(out_ref)

        out_ref[...] += _dot_nt(dp_ref[...], w_ref[...])

    (dh,), rode = _call_with_ride(
        body, ride, lambda: (pl.program_id(0) == 0) & (pl.program_id(1) == 0),
        lambda: (pl.program_id(0) == nm - 1) & (pl.program_id(1) == nk - 1),
        name="dh", grid=(nm, nk),
        in_specs=[pl.BlockSpec((TB, TB), lambda m, kk: (m, kk)),
                  pl.BlockSpec((None, D, TB), lambda m, kk: (kk // per, 0, kk % per))],
        out_specs=[pl.BlockSpec((TB, D), lambda m, kk: (m, 0))],
        out_shape=[_sds((S, D))], compiler_params=_params("arbitrary", "arbitrary"),
    )(dproj, wg_in)
    return dh, rode


def _dw_in(h_t, dproj):
    S = dproj.shape[0]
    per = IN_W // N_SHARD // TB

    def body(ht_ref, dp_ref, out_ref):
        @pl.when(pl.program_id(1) == 0)
        def _():
            out_ref[...] = jnp.zeros_like(out_ref)

        out_ref[...] += _dot(ht_ref[...], dp_ref[...])

    return pl.pallas_call(
        body, name="dw_in", grid=(IN_W // TB, S // TB),
        in_specs=[pl.BlockSpec((D, TB), lambda j, kk: (0, kk)), pl.BlockSpec((TB, TB), lambda j, kk: (kk, j))],
        out_specs=pl.BlockSpec((None, D, TB), lambda j, kk: (j // per, 0, j % per)),
        out_shape=_sds((N_SHARD, D, IN_W // N_SHARD)), compiler_params=_params("parallel", "arbitrary"),
    )(h_t, dproj)


def _prenorm_bwd(x, dh, dx2, norm_g, mod):
    S = x.shape[0]
    tm = 512

    def body(x_ref, dh_ref, dx2_ref, g_ref, mod_ref, gx_ref, dg_ref, dshift_ref, dscale_ref):
        i = pl.program_id(0)

        @pl.when(i == 0)
        def _():
            dg_ref[...] = jnp.zeros_like(dg_ref)
            dshift_ref[...] = jnp.zeros_like(dshift_ref)
            dscale_ref[...] = jnp.zeros_like(dscale_ref)

        xv = x_ref[...]
        dhv = dh_ref[...]
        g = g_ref[...]
        r = lax.rsqrt(jnp.mean(xv * xv, axis=-1, keepdims=True) + EPS)
        xh = xv * r
        dshift_ref[...] += jnp.sum(dhv, axis=0, keepdims=True)
        dscale_ref[...] += jnp.sum(dhv * (xh * g), axis=0, keepdims=True)
        dn1 = dhv * (1.0 + mod_ref[:, D:2 * D])
        dg_ref[...] += jnp.sum(dn1 * xh, axis=0, keepdims=True)
        dxh = dn1 * g
        gx_ref[...] = dx2_ref[...] + r * (dxh - xh * jnp.mean(dxh * xh, axis=-1, keepdims=True))

    row = pl.BlockSpec((tm, D), lambda i: (i, 0))
    vec = pl.BlockSpec((1, D), lambda i: (0, 0))
    return pl.pallas_call(
        body, name="prenorm_bwd", grid=(S // tm,),
        in_specs=[row, row, row, vec, pl.BlockSpec((1, 3 * D), lambda i: (0, 0))],
        out_specs=[row, vec, vec, vec],
        out_shape=[_sds((S, D)), _sds((1, D)), _sds((1, D)), _sds((1, D))],
        compiler_params=_params("arbitrary"),
    )(x, dh, dx2, norm_g, mod)


def _local_step(x, target, mod, wg_in, wab, wpb, wout, pool_w, pool_scale, rel_bias, norm_g, final_g, half_idx,
                chip_half):
    buckets = jnp.asarray(_bucket_tables())
    bias_tab = _bias_table(rel_bias, buckets)
    h = _prenorm(x, norm_g, mod)
    qkv = [_proj(h, wg_in, 3 * g, 3, BF16, f"proj_qkv{g}") for g in range(NG)]
    rest = _proj(h, wg_in, NCB_QKV, REST_W // CB, F32, "proj_rest")
    os_, ls_ = zip(*[_attn_fwd(qkv[g], bias_tab, g) for g in range(NG)])
    dx2, dmo, loss, dfinal_g, dgate = _tail(x, target, os_, ls_, rest, wab, wpb, pool_w, pool_scale, wout, mod, final_g)
    dattn, attn, lj, dpooled, drest, dw_out, dw_ab, dw_pb, dpool_w, dpool_scale = _mix_bwd(
        dmo, os_, ls_, rest, wab, wpb, pool_w, pool_scale, wout)
    du = _pool_bwd(dpooled)

    small = [dw_ab, dw_pb, dw_out]
    dqkv0, ds0, sib_small = _attn_bwd(qkv[0], dattn, attn, lj, bias_tab, 0, _ride_sibling_halves(small))
    p_small = [_pair_sum(g, t, half_idx, f"rs_pair_sum{a}") for a, (g, t) in enumerate(zip(small, sib_small))]
    dqkv1, ds1, u_small = _attn_bwd(qkv[1], dattn, attn, lj, bias_tab, 1,
                                    _ride_chip_exchange([p16 for _, p16 in p_small]))
    rs_ab, rs_pb, rs_out = [_chip_sum(p32, u, chip_half, f"rs_chip_sum{a}")
                            for a, ((p32, _), u) in enumerate(zip(p_small, u_small))]
    dqkv2, ds2, _ = _attn_bwd(qkv[2], dattn, attn, lj, bias_tab, 2, None)

    dproj = jnp.concatenate(dqkv0 + dqkv1 + dqkv2 + [drest[:, :AW], du, drest[:, 2 * AW:]], axis=1)
    dw_in = _dw_in(h.T, dproj)
    drel_rows, (sib_in,) = _bias_grad(jnp.concatenate([ds0, ds1, ds2], axis=0), buckets,
                                      _ride_sibling_halves([dw_in]))
    drel = drel_rows[:, 0, :NUM_BUCKETS].T
    p32_in, p16_in = _pair_sum(dw_in, sib_in, half_idx, "rs_pair_sum_in")
    dh, (u_in,) = _dh(dproj, wg_in, _ride_chip_exchange([p16_in]))
    rs_in = _chip_sum(p32_in, u_in, chip_half, "rs_chip_sum_in")

    grad_x, dnorm_g, dshift, dscale = _prenorm_bwd(x, dh, dx2, norm_g, mod)
    dmod = jnp.concatenate([dshift, dscale, dgate], axis=1)
    return dict(loss=loss[0, 0], grad_x=grad_x, dmod=dmod, dnorm_g=dnorm_g, dfinal_g=dfinal_g, dpool_w=dpool_w,
                dpool_scale=dpool_scale, drel_bias=drel, dw_in=dw_in, dw_attn_br=dw_ab, dw_pool_br=dw_pb,
                dw_out=dw_out, rs_in=rs_in, rs_attn_br=rs_ab, rs_pool_br=rs_pb, rs_out=rs_out)


def _allgather8(blocks, name):
    nb = len(blocks)

    def body(*refs):
        ins, outs = refs[:nb], refs[nb:2 * nb]
        send_sems, recv_sems, local_sems = refs[2 * nb:]
        x, y, c = lax.axis_index("x"), lax.axis_index("y"), lax.axis_index("c")
        me, sibling = (x, y, c), (x, y, 1 - c)
        chips = [(1 - x, y), (x, 1 - y), (1 - x, 1 - y)]

        def copy(a, k, block, to, src=None):
            dst = outs[a].at[4 * block[0] + 2 * block[1] + block[2]]
            return pltpu.make_async_remote_copy(src_ref=dst if src is None else src, dst_ref=dst,
                                                send_sem=send_sems.at[a, k], recv_sem=recv_sems.at[a, k],
                                                device_id=to, device_id_type=MESH)

        mine = [pltpu.make_async_copy(ins[a], outs[a].at[4 * x + 2 * y + c], local_sems.at[a]) for a in range(nb)]
        for cp in mine:
            cp.start()
        first = []
        for a in range(nb):
            first.append(copy(a, 0, me, sibling, src=ins[a]))
            first += [copy(a, 1 + j, me, (*chip, c), src=ins[a]) for j, chip in enumerate(chips)]
        for cp in first:
            cp.start()
        passed = []
        for j, chip in enumerate(chips):
            for a in range(nb):
                copy(a, 1 + j, (*chip, c), me).wait_recv()
                cp = copy(a, 4 + j, (*chip, c), sibling)
                cp.start()
                passed.append(cp)
        for a in range(nb):
            copy(a, 0, sibling, me).wait_recv()
            for j, chip in enumerate(chips):
                copy(a, 4 + j, (*chip, 1 - c), me).wait_recv()
        for cp in first + passed:
            cp.wait_send()
        for cp in mine:
            cp.wait()

    return pl.pallas_call(
        body, name=name, in_specs=[ANY] * nb, out_specs=[ANY] * nb,
        out_shape=[_sds((8,) + b.shape, b.dtype) for b in blocks],
        scratch_shapes=[_dma_sems(nb, 7), _dma_sems(nb, 7), _dma_sems(nb)],
    )(*blocks)


def _ride_sibling_halves(gs):
    def copies(ins, outs, send_sems, recv_sems):
        x, y, c = lax.axis_index("x"), lax.axis_index("y"), lax.axis_index("c")
        cps = []
        for a in range(len(gs)):
            r2 = ins[a].shape[1] // 2
            other = ins[a].at[:, pl.ds((1 - c) * r2, r2), :]
            cps.append(pltpu.make_async_remote_copy(src_ref=other, dst_ref=outs[a], send_sem=send_sems.at[a],
                                                    recv_sem=recv_sems.at[a], device_id=(x, y, 1 - c),
                                                    device_id_type=MESH))
        return cps

    return _Ride(gs, [_sds((g.shape[0], g.shape[1] // 2, g.shape[2]), g.dtype) for g in gs], len(gs), copies)


def _pair_sum(g, t, half, name):
    nsh, rows, cols = g.shape
    r2 = rows // 2
    tr = _row_tile(r2, cols)
    nt = r2 // tr

    def body(half_ref, g_ref, t_ref, p32_ref, p16_ref):
        p = g_ref[...] + t_ref[...]
        p32_ref[...] = p
        p16_ref[...] = p.astype(BF16)

    blk = pl.BlockSpec((None, tr, cols), lambda k, i, half_ref: (k, i, 0))
    return pl.pallas_call(
        body, name=name,
        grid_spec=pltpu.PrefetchScalarGridSpec(
            num_scalar_prefetch=1, grid=(nsh, nt),
            in_specs=[pl.BlockSpec((None, tr, cols), lambda k, i, half_ref: (k, half_ref[0] * nt + i, 0)), blk],
            out_specs=[blk, blk]),
        out_shape=[_sds((nsh, r2, cols)), _sds((nsh, r2, cols), BF16)],
        compiler_params=_params("parallel", "parallel"),
    )(half, g, t)


def _ride_chip_exchange(ps):
    def copies(ins, outs, send_sems, recv_sems):
        x, y, c = lax.axis_index("x"), lax.axis_index("y"), lax.axis_index("c")
        chips = [(1 - x, y), (x, 1 - y), (1 - x, 1 - y)]
        cps = []
        for a in range(len(ps)):
            for j, (ox, oy) in enumerate(chips):
                cps.append(pltpu.make_async_remote_copy(src_ref=ins[a].at[2 * ox + oy], dst_ref=outs[a].at[j],
                                                        send_sem=send_sems.at[3 * a + j],
                                                        recv_sem=recv_sems.at[3 * a + j],
                                                        device_id=(ox, oy, c), device_id_type=MESH))
        return cps

    return _Ride(ps, [_sds((3,) + p.shape[1:], p.dtype) for p in ps], 3 * len(ps), copies)


def _chip_sum(p32, u, chip_half, name):
    r2, cols = p32.shape[1:]
    tr = _row_tile(r2, cols)
    nt = r2 // tr

    def body(ch_ref, p_ref, u_ref, o_ref):
        acc = p_ref[...]
        for j in range(3):
            acc = acc + u_ref[j].astype(F32)
        o_ref[...] = acc

    return pl.pallas_call(
        body, name=name,
        grid_spec=pltpu.PrefetchScalarGridSpec(
            num_scalar_prefetch=1, grid=(nt,),
            in_specs=[pl.BlockSpec((None, tr, cols), lambda i, ch_ref: (ch_ref[0], i, 0)),
                      pl.BlockSpec((3, tr, cols), lambda i, ch_ref: (0, i, 0))],
            out_specs=pl.BlockSpec((tr, cols), lambda i, ch_ref: (ch_ref[1] * nt + i, 0))),
        out_shape=_sds((2 * r2, cols)), compiler_params=_params("parallel"),
    )(chip_half, p32, u)


def _sibling_join(fs, name):
    nb = len(fs)

    def body(*refs):
        outs = refs[nb:2 * nb]
        send_sems, recv_sems = refs[2 * nb:]
        x, y, c = lax.axis_index("x"), lax.axis_index("y"), lax.axis_index("c")
        cps = []
        for a in range(nb):
            r2 = outs[a].shape[0] // 2
            rows = outs[a].at[pl.ds(c * r2, r2), :]
            cps.append(pltpu.make_async_remote_copy(src_ref=rows, dst_ref=rows, send_sem=send_sems.at[a],
                                                    recv_sem=recv_sems.at[a], device_id=(x, y, 1 - c),
                                                    device_id_type=MESH))
        for cp in cps:
            cp.start()
        for cp in cps:
            cp.wait()

    return pl.pallas_call(
        body, name=name, in_specs=[ANY] * nb, out_specs=[ANY] * nb,
        out_shape=[_sds(f.shape, f.dtype) for f in fs],
        input_output_aliases={a: a for a in range(nb)},
        scratch_shapes=[_dma_sems(nb), _dma_sems(nb)],
    )(*fs)


def _row_tile(rows, cols):
    tile = rows
    while tile * cols * 4 > (1 << 20) and tile % 16 == 0:
        tile //= 2
    return tile


def _sum_leading(a, name):
    k = a.shape[0]
    a3 = a.reshape(k, -1, a.shape[-1])
    rows, cols = a3.shape[1:]
    tr = _row_tile(rows, cols)

    def body(a_ref, o_ref):
        acc = a_ref[0]
        for s in range(1, k):
            acc = acc + a_ref[s]
        o_ref[...] = acc

    out = pl.pallas_call(
        body, name=name, grid=(rows // tr,),
        in_specs=[pl.BlockSpec((k, tr, cols), lambda i: (0, i, 0))],
        out_specs=pl.BlockSpec((tr, cols), lambda i: (i, 0)),
        out_shape=_sds((rows, cols), a.dtype), compiler_params=_params("parallel"),
    )(a3)
    return out.reshape(a.shape[1:])


def _w_ada_grad(c_all, dmod_cols):
    def body(c_ref, d_ref, o_ref):
        o_ref[...] = _dot_tn(c_ref[...].astype(BF16), d_ref[...].astype(BF16))

    return pl.pallas_call(body, name="w_ada_grad", out_shape=_sds((c_all.shape[1], dmod_cols.shape[1])),
                          compiler_params=_params())(c_all, dmod_cols)


def _adamw(w, g, m, v, name):
    rows, cols = w.shape
    tr = _row_tile(rows, cols)

    def body(w_ref, g_ref, m_ref, v_ref, d_ref, nm_ref, nv_ref):
        gv = g_ref[...]
        nm = ADAM_B1 * m_ref[...] + (1.0 - ADAM_B1) * gv
        nv = ADAM_B2 * v_ref[...] + (1.0 - ADAM_B2) * (gv * gv)
        m_hat = nm / (1.0 - ADAM_B1 ** ADAM_STEP)
        v_hat = nv / (1.0 - ADAM_B2 ** ADAM_STEP)
        d_ref[...] = -ADAM_LR * (m_hat / (jnp.sqrt(v_hat) + ADAM_EPS) + ADAM_WD * w_ref[...])
        nm_ref[...] = nm
        nv_ref[...] = nv

    spec = pl.BlockSpec((tr, cols), lambda i: (i, 0))
    return pl.pallas_call(
        body, name=name, grid=(rows // tr,), in_specs=[spec] * 4, out_specs=[spec] * 3,
        out_shape=[_sds((rows, cols))] * 3, compiler_params=_params("parallel"),
    )(w, g, m, v)


def _pack_small(b_ada, norm_g, final_g, pool_scale, rel_bias, loss_row, pool_w):
    pad = jnp.zeros((PK_POOLW - PK_LOSS - 1) * 128, F32)
    flat = jnp.concatenate([b_ada.reshape(-1), norm_g.reshape(-1), final_g.reshape(-1), pool_scale.reshape(-1),
                            rel_bias.reshape(-1), loss_row.reshape(-1), pad, pool_w.reshape(-1)])
    return flat.reshape(PK_ROWS, 128)


def _unpack_small(p):
    def take(r0, r1, shape):
        return p[r0:r1].reshape(shape)

    return dict(b_ada=take(PK_BADA, PK_NORMG, (1, 3 * D)), norm_g=take(PK_NORMG, PK_FINALG, (1, D)),
                final_g=take(PK_FINALG, PK_PSCALE, (D,)), pool_scale=take(PK_PSCALE, PK_RELB, (1, AW)),
                rel_bias=take(PK_RELB, PK_LOSS, (NUM_BUCKETS, NG * NH)), loss=p[PK_LOSS, 0],
                pool_w=take(PK_POOLW, PK_ROWS, (1, 4, PGW, PGW)))


def kernel(x, c, norm_g, w_ada, b_ada, w_in, pool_w, pool_scale, w_attn_br, w_pool_br, w_out, rel_bias, final_g, loss_target, m_norm_g, m_w_ada, m_b_ada, m_w_in, m_pool_w, m_pool_scale, m_w_attn_br, m_w_pool_br, m_w_out, m_rel_bias, m_final_g, v_norm_g, v_w_ada, v_b_ada, v_w_in, v_pool_w, v_pool_scale, v_w_attn_br, v_w_pool_br, v_w_out, v_rel_bias, v_final_g):
    ix, iy, ic = lax.axis_index("x"), lax.axis_index("y"), lax.axis_index("c")
    dev = 4 * ix + 2 * iy + ic
    chip = 2 * ix + iy

    def half(w):
        r2 = w.shape[0] // 2
        return lax.dynamic_slice_in_dim(w, ic * r2, r2, axis=0).astype(BF16)

    gathered = _allgather8([jnp.broadcast_to(c, (8, D)), half(w_in[0]), half(w_attn_br[0]), half(w_pool_br[0]),
                            half(w_out[0])], "gather_weights")
    c_all = gathered[0][:, 0, :]
    wg_in = gathered[1].reshape(N_SHARD, D, IN_W // N_SHARD)
    wab = gathered[2].reshape(N_SHARD, AW, D // N_SHARD).transpose(1, 0, 2).reshape(AW, D)
    wpb = gathered[3].reshape(N_SHARD, AW, D // N_SHARD).transpose(1, 0, 2).reshape(AW, D)
    wout = gathered[4].reshape(D, D)

    mw = 3 * D // N_SHARD
    modp = _mod_partial(c_all, w_ada[0], lax.dynamic_slice_in_dim(b_ada, chip * mw, mw, axis=1))
    mod_all = _allgather8([modp], "gather_mod")[0]
    mod_full = mod_all[::2].transpose(1, 0, 2).reshape(8, 3 * D)
    mod = lax.dynamic_slice_in_dim(mod_full, dev, 1, axis=0)

    half_idx = jnp.stack([ic]).astype(jnp.int32)
    chip_half = jnp.stack([chip, ic]).astype(jnp.int32)
    r = _local_step(x[0], loss_target[0], mod, wg_in, wab, wpb, wout, pool_w[0], pool_scale, rel_bias, norm_g,
                    final_g.reshape(1, D), half_idx, chip_half)

    packed = _pack_small(r["dmod"], r["dnorm_g"], r["dfinal_g"], r["dpool_scale"], r["drel_bias"],
                         jnp.full((128,), r["loss"], F32), r["dpool_w"])
    small_all = _allgather8([packed], "gather_small")[0]
    small_sum = _sum_leading(small_all, "sum_small")
    dmod_all = small_all[:, PK_BADA:PK_NORMG, :].reshape(8, 3 * D)
    g_w_ada = _w_ada_grad(c_all, lax.dynamic_slice_in_dim(dmod_all, chip * mw, mw, axis=1))

    g_w_in, g_w_ab, g_w_pb, g_w_out = _sibling_join([r["rs_in"], r["rs_attn_br"], r["rs_pool_br"], r["rs_out"]],
                                                    "rs_sibling_join")

    small_w = _pack_small(b_ada, norm_g, final_g, pool_scale, rel_bias, jnp.zeros((128,), F32), pool_w)
    small_m = _pack_small(m_b_ada, m_norm_g, m_final_g, m_pool_scale, m_rel_bias, jnp.zeros((128,), F32), m_pool_w)
    small_v = _pack_small(v_b_ada, v_norm_g, v_final_g, v_pool_scale, v_rel_bias, jnp.ones((128,), F32), v_pool_w)
    sd, sm, sv = (_unpack_small(p) for p in _adamw(small_w, small_sum, small_m, small_v, "adamw_small"))
    sg = _unpack_small(small_sum)
    upd = {
        "w_ada": (g_w_ada,) + tuple(_adamw(w_ada[0], g_w_ada, m_w_ada[0], v_w_ada[0], "adamw_w_ada")),
        "w_in": (g_w_in,) + tuple(_adamw(w_in[0], g_w_in, m_w_in[0], v_w_in[0], "adamw_w_in")),
        "w_attn_br": (g_w_ab,) + tuple(_adamw(w_attn_br[0], g_w_ab, m_w_attn_br[0], v_w_attn_br[0], "adamw_w_ab")),
        "w_pool_br": (g_w_pb,) + tuple(_adamw(w_pool_br[0], g_w_pb, m_w_pool_br[0], v_w_pool_br[0], "adamw_w_pb")),
        "w_out": (g_w_out,) + tuple(_adamw(w_out[0], g_w_out, m_w_out[0], v_w_out[0], "adamw_w_out")),
    }
    names = ["norm_g", "w_ada", "b_ada", "w_in", "pool_w", "pool_scale", "w_attn_br", "w_pool_br", "w_out",
             "rel_bias", "final_g"]
    outs = [sg["loss"], r["grad_x"][None]]
    for kind in range(4):
        for nme in names:
            if nme in upd:
                outs.append(upd[nme][kind][None])
            else:
                outs.append((sg, sd, sm, sv)[kind][nme])
    return tuple(outs)
```

```python
import functools
import math

import numpy as np
import jax
import jax.numpy as jnp
from jax import lax
from jax.experimental import pallas as pl
from jax.experimental.pallas import tpu as pltpu

F32 = jnp.float32
BF16 = jnp.bfloat16

D = 1024
HD = 64
NH = 8
AW = NH * HD
GROUPS = ((128, 1), (512, 4), (2048, 16))
NG = len(GROUPS)
BLK = 128
GW = 3 * AW
QKV_W = NG * GW
REST_W = 3584
IN_W = QKV_W + REST_W
CB = 512
NCB = IN_W // CB
NCB_QKV = QKV_W // CB
POOL_WINDOWS = (2, 4, 8, 16)
PGW = 128
HALO = 16
NUM_BUCKETS = 32
MAX_DISTANCE = 2048
EPS = 1e-6
NEG = -1e30
N_SHARD = 4
VMEM_LIMIT = 56 * 1024 * 1024

ADAM_LR = 0.001
ADAM_B1 = 0.9
ADAM_B2 = 0.999
ADAM_EPS = 1e-08
ADAM_WD = 0.01
ADAM_STEP = 10

PK_BADA, PK_NORMG, PK_FINALG, PK_PSCALE, PK_RELB, PK_LOSS, PK_POOLW, PK_ROWS = 0, 24, 32, 40, 44, 50, 56, 568

ANY = pl.BlockSpec(memory_space=pl.ANY)
MESH = pl.DeviceIdType.MESH


def _params(*sem):
    return pltpu.CompilerParams(dimension_semantics=sem, vmem_limit_bytes=VMEM_LIMIT)


def _sds(shape, dtype=F32):
    return jax.ShapeDtypeStruct(shape, dtype)


def _dot(a, b):
    return jnp.dot(a, b, preferred_element_type=F32)


def _dot_nt(a, b):
    return lax.dot_general(a, b, (((1,), (1,)), ((), ())), preferred_element_type=F32)


def _dot_tn(a, b):
    return lax.dot_general(a, b, (((0,), (0,)), ((), ())), preferred_element_type=F32)


def _sigmoid(z):
    return 1.0 / (1.0 + jnp.exp(-z))


def _dma_sems(*shape):
    return pltpu.SemaphoreType.DMA(shape)


class _Ride:
    def __init__(self, arrays, out_shapes, n_copies, copies):
        self.arrays, self.out_shapes, self.n_copies, self.copies = list(arrays), list(out_shapes), n_copies, copies


def _call_with_ride(body, ride, first, last, *, in_specs, out_specs, out_shape, scratch_shapes=(), **kw):
    in_specs, out_specs, out_shape, scratch_shapes = list(in_specs), list(out_specs), list(out_shape), list(scratch_shapes)
    n_in, n_out, n_sc = len(in_specs), len(out_specs), len(scratch_shapes)
    if ride is None:
        def run_plain(*operands):
            return pl.pallas_call(body, in_specs=in_specs, out_specs=out_specs, out_shape=out_shape,
                                  scratch_shapes=scratch_shapes, **kw)(*operands), []
        return run_plain
    n_ri, n_ro = len(ride.arrays), len(ride.out_shapes)

    def wrapped(*refs):
        ins, rest = refs[:n_in], refs[n_in:]
        r_ins, rest = rest[:n_ri], rest[n_ri:]
        outs, rest = rest[:n_out], rest[n_out:]
        r_outs, rest = rest[:n_ro], rest[n_ro:]
        scratch, (send_sems, recv_sems) = rest[:n_sc], rest[n_sc:]

        @pl.when(first())
        def _():
            for cp in ride.copies(r_ins, r_outs, send_sems, recv_sems):
                cp.start()

        body(*ins, *outs, *scratch)

        @pl.when(last())
        def _():
            for cp in ride.copies(r_ins, r_outs, send_sems, recv_sems):
                cp.wait()

    def run(*operands):
        res = pl.pallas_call(
            wrapped, in_specs=in_specs + [ANY] * n_ri, out_specs=out_specs + [ANY] * n_ro,
            out_shape=out_shape + ride.out_shapes,
            scratch_shapes=scratch_shapes + [_dma_sems(ride.n_copies), _dma_sems(ride.n_copies)], **kw,
        )(*operands, *ride.arrays)
        return res[:n_out], res[n_out:]
    return run


def _bucket_tables():
    i = np.arange(BLK)[:, None]
    j = np.arange(2 * BLK)[None, :]
    dist = BLK + i - j
    valid = (dist >= 0) & (dist <= BLK)
    tabs = []
    for _, dil in GROUPS:
        n = (np.clip(dist, 0, BLK) * dil).astype(np.int32)
        max_exact = NUM_BUCKETS // 2
        nf = np.maximum(n, 1).astype(np.float32)
        large = max_exact + (np.log(nf / np.float32(max_exact)) / np.float32(math.log(MAX_DISTANCE / max_exact))
                             * np.float32(NUM_BUCKETS - max_exact)).astype(np.int32)
        large = np.minimum(large, NUM_BUCKETS - 1)
        bucket = np.where(n < max_exact, n, large)
        tab = np.where(valid, bucket, -1).astype(np.int32)
        perm = _block_perm(dil)
        tabs.append(tab[perm][:, np.concatenate([perm, BLK + perm])])
    return np.stack(tabs)


def _bias_table(rel_bias, buckets):
    def body(rb_ref, bk_ref, out_ref):
        gh = pl.program_id(0)
        bk = bk_ref[...]
        acc = jnp.full((BLK, 2 * BLK), NEG, F32)
        for b in range(NUM_BUCKETS):
            acc = jnp.where(bk == b, rb_ref[b, gh], acc)
        out_ref[...] = acc

    return pl.pallas_call(
        body, name="bias_table", grid=(NG * NH,),
        in_specs=[pl.BlockSpec(memory_space=pltpu.SMEM),
                  pl.BlockSpec((None, BLK, 2 * BLK), lambda gh: (gh // NH, 0, 0))],
        out_specs=pl.BlockSpec((None, BLK, 2 * BLK), lambda gh: (gh, 0, 0)),
        out_shape=_sds((NG * NH, BLK, 2 * BLK)),
        compiler_params=_params("arbitrary"),
    )(rel_bias, buckets)


def _bias_grad(ds_acc, buckets, ride):
    def body(acc_ref, bk_ref, out_ref):
        bk = bk_ref[...]
        acc = acc_ref[...]
        lane = lax.broadcasted_iota(jnp.int32, (8, 128), 1)
        out = jnp.zeros((8, 128), F32)
        for b in range(NUM_BUCKETS):
            val = jnp.sum(jnp.where(bk == b, acc, 0.0))
            out = jnp.where(lane == b, val, out)
        out_ref[...] = out

    (out,), rode = _call_with_ride(
        body, ride, lambda: pl.program_id(0) == 0, lambda: pl.program_id(0) == NG * NH - 1,
        name="bias_grad", grid=(NG * NH,),
        in_specs=[pl.BlockSpec((None, BLK, 2 * BLK), lambda gh: (gh, 0, 0)),
                  pl.BlockSpec((None, BLK, 2 * BLK), lambda gh: (gh // NH, 0, 0))],
        out_specs=[pl.BlockSpec((None, 8, 128), lambda gh: (gh, 0, 0))],
        out_shape=[_sds((NG * NH, 8, 128))],
        compiler_params=_params("arbitrary"),
    )(ds_acc, buckets)
    return out, rode


def _mod_partial(c_all, w_ada_s, b_ada_s):
    def body(c_ref, w_ref, b_ref, o_ref):
        o_ref[...] = _dot(c_ref[...].astype(BF16), w_ref[...].astype(BF16)) + b_ref[...]

    return pl.pallas_call(body, name="mod_partial", out_shape=_sds((8, w_ada_s.shape[1])),
                          compiler_params=_params())(c_all, w_ada_s, b_ada_s)


def _prenorm(x, norm_g, mod):
    S = x.shape[0]
    tm = 512

    def body(x_ref, g_ref, mod_ref, h_ref):
        xv = x_ref[...]
        r = lax.rsqrt(jnp.mean(xv * xv, axis=-1, keepdims=True) + EPS)
        n1 = xv * r * g_ref[...]
        h_ref[...] = (n1 * (1.0 + mod_ref[:, D:2 * D]) + mod_ref[:, 0:D]).astype(BF16)

    return pl.pallas_call(
        body, name="prenorm", grid=(S // tm,),
        in_specs=[pl.BlockSpec((tm, D), lambda i: (i, 0)), pl.BlockSpec((1, D), lambda i: (0, 0)),
                  pl.BlockSpec((1, 3 * D), lambda i: (0, 0))],
        out_specs=pl.BlockSpec((tm, D), lambda i: (i, 0)),
        out_shape=_sds((S, D), BF16), compiler_params=_params("parallel"),
    )(x, norm_g, mod)


def _proj(h, wg_in, j0, nj, dtype, name):
    S = h.shape[0]
    tm = 1024
    per = wg_in.shape[2] // CB

    def body(h_ref, w_ref, o_ref):
        o_ref[...] = _dot(h_ref[...], w_ref[...]).astype(dtype)

    return pl.pallas_call(
        body, name=name, grid=(S // tm, nj),
        in_specs=[pl.BlockSpec((tm, D), lambda m, j: (m, 0)),
                  pl.BlockSpec((None, D, CB), lambda m, j: ((j0 + j) // per, 0, (j0 + j) % per))],
        out_specs=pl.BlockSpec((tm, CB), lambda m, j: (m, j)),
        out_shape=_sds((S, nj * CB), dtype), compiler_params=_params("parallel", "parallel"),
    )(h, wg_in)


HS = 4
SLAB = HS * HD


def _lane_head(rows):
    return lax.broadcasted_iota(jnp.int32, (rows, SLAB), 1) // HD


def _head_stack(a):
    head = _lane_head(a.shape[0])
    return jnp.concatenate([jnp.where(head == h, a, jnp.zeros_like(a)) for h in range(HS)], axis=0)


def _head_unstack(a):
    rows = a.shape[0] // HS
    head = _lane_head(rows)
    out = a[:rows]
    for h in range(1, HS):
        out = jnp.where(head == h, a[h * rows:(h + 1) * rows], out)
    return out


STAT_W = 128
VIEW = 16


def _sub_layout(dil):
    if dil == 1:
        return BLK, [None]
    return BLK * dil // VIEW, [[r + dil * u for u in range(VIEW // dil)] for r in range(dil)]


def _block_perm(dil):
    a_rows, _ = _sub_layout(dil)
    p = np.arange(BLK)
    return p if dil == 1 else (VIEW // dil) * (p % a_rows) + p // a_rows


LB = 128
N_SLAB = NH // HS


def _ld(refs, bs, s, w):
    if bs is None:
        return refs[0][:, s * w:(s + 1) * w]
    a_rows = refs[0].shape[0] // VIEW
    return jnp.concatenate([jnp.concatenate([ref[pl.ds(b, a_rows, stride=VIEW), :] for b in bs], axis=0)
                            for ref in refs], axis=1)


def _st(ref, bs, s, val):
    if bs is None:
        ref[:, s * SLAB:(s + 1) * SLAB] = val
        return
    a_rows = val.shape[0] // len(bs)
    for u, b in enumerate(bs):
        ref[:, b, :] = val[u * a_rows:(u + 1) * a_rows]


def _attn_views(dil, S):
    a_rows, subs = _sub_layout(dil)
    if dil == 1:
        def ispecs(base, w, f):
            return [pl.BlockSpec((BLK, N_SLAB * w), lambda sg, n: (f(n), base // (N_SLAB * w)))]
        return subs, S // BLK, N_SLAB, ispecs, (lambda w: (S, w)), (
            lambda f: pl.BlockSpec((BLK, AW), lambda sg, n: (f(n), 0)))

    def ispecs(base, w, f):
        return [pl.BlockSpec((a_rows * VIEW, LB), lambda sg, n, k=k: (f(n), (base + sg * w) // LB + k))
                for k in range(w // LB)]
    return subs, S // (a_rows * VIEW), 1, ispecs, (lambda w: (S // VIEW, VIEW, w)), (
        lambda f: pl.BlockSpec((a_rows, VIEW, SLAB), lambda sg, n: (f(n), 0, sg)))


def _attn_fwd(qkv_g, bias_tab, g):
    S = qkv_g.shape[0]
    subs, nbq, sps, ispecs, shape, ospec = _attn_views(GROUPS[g][1], S)
    cur, prev = (lambda n: n), (lambda n: jnp.maximum(n - 1, 0))
    in_specs = [ispecs(0, SLAB, cur), ispecs(AW, SLAB, prev), ispecs(AW, SLAB, cur), ispecs(2 * AW, SLAB, prev),
                ispecs(2 * AW, SLAB, cur)]
    nl = len(in_specs[0])

    def body(*refs):
        q, kp, kc, vp, vc = (refs[t * nl:(t + 1) * nl] for t in range(5))
        b_ref, o_ref, l_ref = refs[5 * nl:]
        n = pl.program_id(1)
        col = lax.broadcasted_iota(jnp.int32, (HS * BLK, 2 * BLK), 1)
        keep = (col >= BLK) | (n > 0)
        for s_ in range(sps):
            bias = b_ref[pl.ds(s_ * HS, HS)].reshape(HS * BLK, 2 * BLK)
            for bs in subs:
                kb = jnp.concatenate([_ld(kp, bs, s_, SLAB), _ld(kc, bs, s_, SLAB)], axis=0).astype(BF16)
                vb = jnp.concatenate([_ld(vp, bs, s_, SLAB), _ld(vc, bs, s_, SLAB)], axis=0).astype(BF16)
                s = _dot_nt(_head_stack(_ld(q, bs, s_, SLAB).astype(BF16)), kb) * (HD ** -0.5) + bias
                s = jnp.where(keep, s, NEG)
                m = jnp.max(s, axis=-1, keepdims=True)
                p = jnp.exp(s - m)
                den = jnp.sum(p, axis=-1, keepdims=True)
                _st(o_ref, bs, s_, _head_unstack(_dot(p.astype(BF16), vb) / den))
                _st(l_ref, bs, s_, _head_unstack(jnp.broadcast_to(m + jnp.log(den), (HS * BLK, SLAB))))

    out = _sds(shape(AW))
    o, l = pl.pallas_call(
        body, name=f"attn_fwd{g}", grid=(N_SLAB // sps, nbq),
        in_specs=sum(in_specs, []) + [pl.BlockSpec((sps * HS, BLK, 2 * BLK),
                                                   lambda sg, n: (g * (N_SLAB // sps) + sg, 0, 0))],
        out_specs=[ospec(cur), ospec(cur)],
        out_shape=[out, out], compiler_params=_params("parallel", "arbitrary"),
    )(*([qkv_g] * (5 * nl)), bias_tab)
    return o.reshape(S, AW), l.reshape(S, AW)


def _attn_bwd(qkv_g, dattn, stats, bias_tab, g, ride):
    S = qkv_g.shape[0]
    subs, nbq, sps, ispecs, shape, ospec = _attn_views(GROUPS[g][1], S)
    cur = lambda n: jnp.minimum(n, nbq - 1)
    prev = lambda n: jnp.clip(n - 1, 0, nbq - 1)
    late = lambda n: jnp.maximum(n - 1, 0)
    in_specs = [ispecs(0, SLAB, cur), ispecs(AW, SLAB, prev), ispecs(AW, SLAB, cur), ispecs(2 * AW, SLAB, prev),
                ispecs(2 * AW, SLAB, cur), ispecs(0, SLAB, cur), ispecs(0, STAT_W, cur)]
    nl = len(in_specs[0])

    def body(*refs):
        q, kp, kc, vp, vc, da = (refs[t * nl:(t + 1) * nl] for t in range(6))
        st_ref, b_ref, dq_ref, dk_ref, dv_ref, ds_ref, ck_ref, cv_ref = refs[6 * nl:]
        n = pl.program_id(1)

        @pl.when(n == 0)
        def _():
            ds_ref[...] = jnp.zeros_like(ds_ref)
            ck_ref[...] = jnp.zeros_like(ck_ref)
            cv_ref[...] = jnp.zeros_like(cv_ref)

        @pl.when(n < nbq)
        def _():
            col = lax.broadcasted_iota(jnp.int32, (HS * BLK, 2 * BLK), 1)
            keep = (col >= BLK) | (n > 0)
            for s_ in range(sps):
                cs = slice(s_ * SLAB, (s_ + 1) * SLAB)
                bias = b_ref[pl.ds(s_ * HS, HS)].reshape(HS * BLK, 2 * BLK)
                for i, bs in enumerate(subs):
                    st = _ld((st_ref,), bs, s_, STAT_W)
                    kb = jnp.concatenate([_ld(kp, bs, s_, SLAB), _ld(kc, bs, s_, SLAB)], axis=0).astype(BF16)
                    vb = jnp.concatenate([_ld(vp, bs, s_, SLAB), _ld(vc, bs, s_, SLAB)], axis=0).astype(BF16)
                    lse = jnp.concatenate([st[:, h:h + 1] for h in range(HS)], axis=0)
                    delta = jnp.concatenate([st[:, HS + h:HS + h + 1] for h in range(HS)], axis=0)
                    qs = _head_stack(_ld(q, bs, s_, SLAB).astype(BF16))
                    dos = _head_stack(_ld(da, bs, s_, SLAB).astype(BF16))
                    s = _dot_nt(qs, kb) * (HD ** -0.5) + bias
                    s = jnp.where(keep, s, NEG)
                    p = jnp.exp(s - lse)
                    ds = p * (_dot_nt(dos, vb) - delta)
                    ds_ref[pl.ds(s_ * HS, HS)] += ds.reshape(HS, BLK, 2 * BLK)
                    ds_b = (ds * (HD ** -0.5)).astype(BF16)
                    _st(dq_ref, bs, s_, _head_unstack(_dot(ds_b, kb)))
                    dkb = _dot_tn(ds_b, qs)
                    dvb = _dot_tn(p.astype(BF16), dos)
                    _st(dk_ref, bs, s_, ck_ref[i, :, cs] + dkb[:BLK])
                    _st(dv_ref, bs, s_, cv_ref[i, :, cs] + dvb[:BLK])
                    ck_ref[i, :, cs] = dkb[BLK:]
                    cv_ref[i, :, cs] = dvb[BLK:]

        @pl.when(n == nbq)
        def _():
            for s_ in range(sps):
                for i, bs in enumerate(subs):
                    _st(dk_ref, bs, s_, ck_ref[i, :, s_ * SLAB:(s_ + 1) * SLAB])
                    _st(dv_ref, bs, s_, cv_ref[i, :, s_ * SLAB:(s_ + 1) * SLAB])

    out = _sds(shape(AW))
    nsg = N_SLAB // sps
    (dq, dk, dv, ds_acc), rode = _call_with_ride(
        body, ride, lambda: (pl.program_id(0) == 0) & (pl.program_id(1) == 0),
        lambda: (pl.program_id(0) == nsg - 1) & (pl.program_id(1) == nbq),
        name=f"attn_bwd{g}", grid=(nsg, nbq + 1),
        in_specs=sum(in_specs, []) + [pl.BlockSpec((sps * HS, BLK, 2 * BLK), lambda sg, n: (g * nsg + sg, 0, 0))],
        out_specs=[ospec(cur), ospec(late), ospec(late),
                   pl.BlockSpec((sps * HS, BLK, 2 * BLK), lambda sg, n: (sg, 0, 0))],
        out_shape=[out] * 3 + [_sds((NH, BLK, 2 * BLK))],
        scratch_shapes=[pltpu.VMEM((len(subs), BLK, sps * SLAB), F32), pltpu.VMEM((len(subs), BLK, sps * SLAB), F32)],
        compiler_params=_params("arbitrary", "arbitrary"),
    )(*([qkv_g] * (5 * nl)), *([dattn] * nl), stats, bias_tab)
    return [dq.reshape(S, AW), dk.reshape(S, AW), dv.reshape(S, AW)], ds_acc, rode


TM_MIX = 256


def _mix_specs(tm):
    row512 = pl.BlockSpec((tm, AW), lambda i: (i, 0))
    return ([row512] * 6 + [
        pl.BlockSpec((tm, REST_W), lambda i: (i, 0)),
        pl.BlockSpec((HALO, AW), lambda i: (jnp.maximum(i * (tm // HALO) - 1, 0), 1)),
        pl.BlockSpec((AW, D), lambda i: (0, 0)), pl.BlockSpec((AW, D), lambda i: (0, 0)),
        pl.BlockSpec((4, PGW, PGW), lambda i: (0, 0, 0)), pl.BlockSpec((1, AW), lambda i: (0, 0))])


def _mix_forward(i, tm, o_refs, l_refs, rest_ref, halo_ref, wab_ref, wpb_ref, pw_ref, ps_ref):
    l0, l1, l2 = (r[...] for r in l_refs)
    mx = jnp.maximum(jnp.maximum(l0, l1), l2)
    e0, e1, e2 = jnp.exp(l0 - mx), jnp.exp(l1 - mx), jnp.exp(l2 - mx)
    den = e0 + e1 + e2
    lj = mx + jnp.log(den)
    attn = (e0 * o_refs[0][...] + e1 * o_refs[1][...] + e2 * o_refs[2][...]) / den

    z_attn = rest_ref[:, 0:AW]
    u = rest_ref[:, AW:2 * AW]
    z_pool = rest_ref[:, 2 * AW:3 * AW]
    g_attn = rest_ref[:, 3 * AW:3 * AW + D]
    g_pool = rest_ref[:, 3 * AW + D:3 * AW + 2 * D]

    sg_a = _sigmoid(z_attn)
    sil_a = z_attn * sg_a
    a_g = (attn * sil_a).astype(BF16)
    y_attn = _dot(a_g, wab_ref[...])

    halo = jnp.where(i > 0, halo_ref[...], 0.0)
    ext = jnp.concatenate([halo, u], axis=0)
    t = i * tm + lax.broadcasted_iota(jnp.int32, (tm, 1), 0)
    pooled, mixed_raw = [], []
    for gi, win in enumerate(POOL_WINDOWS):
        s = ext[:, gi * PGW:(gi + 1) * PGW]
        sh = 1
        while sh < win:
            s = s + pltpu.roll(s, sh, 0)
            sh *= 2
        cnt = jnp.minimum(t + 1, win).astype(F32)
        pg = s[HALO:] / cnt - u[:, gi * PGW:(gi + 1) * PGW]
        pooled.append(pg.astype(BF16))
        mixed_raw.append(_dot(pooled[-1], pw_ref[gi].astype(BF16)))
    mixed_raw = jnp.concatenate(mixed_raw, axis=1)
    mixed = mixed_raw * ps_ref[...]
    sg_p = _sigmoid(z_pool)
    sil_p = z_pool * sg_p
    m_g = (mixed * sil_p).astype(BF16)
    y_pool = _dot(m_g, wpb_ref[...])

    sa = _sigmoid(g_attn)
    sp = _sigmoid(g_pool)
    merged = sa * y_attn + sp * y_pool
    return dict(lj=lj, attn=attn, z_attn=z_attn, z_pool=z_pool, sg_a=sg_a, sil_a=sil_a, a_g=a_g, y_attn=y_attn,
                pooled=pooled, mixed_raw=mixed_raw, mixed=mixed, sg_p=sg_p, sil_p=sil_p, m_g=m_g, y_pool=y_pool,
                sa=sa, sp=sp, merged=merged)


def _tail(x, target, os_, ls_, rest, wab, wpb, pool_w, pool_scale, wout, mod, final_g):
    S = x.shape[0]
    tm = TM_MIX

    def body(o0, o1, o2, l0, l1, l2, rest_ref, halo_ref, wab_ref, wpb_ref, pw_ref, ps_ref,
             x_ref, t_ref, wo_ref, mod_ref, fg_ref, dx2_ref, dmo_ref, loss_ref, dfg_ref, dgate_ref):
        i = pl.program_id(0)

        @pl.when(i == 0)
        def _():
            loss_ref[...] = jnp.zeros_like(loss_ref)
            dfg_ref[...] = jnp.zeros_like(dfg_ref)
            dgate_ref[...] = jnp.zeros_like(dgate_ref)

        f = _mix_forward(i, tm, (o0, o1, o2), (l0, l1, l2), rest_ref, halo_ref, wab_ref, wpb_ref, pw_ref, ps_ref)
        mo = _dot(f["merged"].astype(BF16), wo_ref[...])
        gate = mod_ref[:, 2 * D:3 * D]
        fg = fg_ref[...]
        x2 = x_ref[...] + gate * mo
        r2 = lax.rsqrt(jnp.mean(x2 * x2, axis=-1, keepdims=True) + EPS)
        n2 = x2 * r2
        err = n2 * fg - t_ref[...]
        loss_ref[...] += 0.5 * jnp.sum(jnp.mean(err * err, axis=-1, keepdims=True))
        dy = err * (1.0 / D)
        dfg_ref[...] += jnp.sum(dy * n2, axis=0, keepdims=True)
        dn = dy * fg
        dx2 = r2 * (dn - n2 * jnp.mean(dn * n2, axis=-1, keepdims=True))
        dgate_ref[...] += jnp.sum(dx2 * mo, axis=0, keepdims=True)
        dx2_ref[...] = dx2
        dmo_ref[...] = (dx2 * gate).astype(BF16)

    row = pl.BlockSpec((tm, D), lambda i: (i, 0))
    vec = pl.BlockSpec((1, D), lambda i: (0, 0))
    return pl.pallas_call(
        body, name="tail", grid=(S // tm,),
        in_specs=_mix_specs(tm) + [row, row, pl.BlockSpec((D, D), lambda i: (0, 0)),
                                   pl.BlockSpec((1, 3 * D), lambda i: (0, 0)), vec],
        out_specs=[row, row, pl.BlockSpec((8, 128), lambda i: (0, 0)), vec, vec],
        out_shape=[_sds((S, D)), _sds((S, D), BF16), _sds((8, 128)), _sds((1, D)), _sds((1, D))],
        compiler_params=_params("arbitrary"),
    )(*os_, *ls_, rest, rest, wab, wpb, pool_w, pool_scale, x, target, wout, mod, final_g)


def _mix_bwd(dmo, os_, ls_, rest, wab, wpb, pool_w, pool_scale, wout):
    S = dmo.shape[0]
    tm = TM_MIX
    nt = S // tm
    sw = D // N_SHARD

    def body(o0, o1, o2, l0, l1, l2, rest_ref, halo_ref, wab_ref, wpb_ref, pw_ref, ps_ref, dmo_ref, wo_ref,
             dattn_ref, stats_ref, dpooled_ref, drest_ref, dwo_hbm, dwab_hbm, dwpb_hbm, dpw_ref, dps_ref,
             awo, awab, awpb):
        i = pl.program_id(0)

        @pl.when(i == 0)
        def _():
            awo[...] = jnp.zeros_like(awo)
            awab[...] = jnp.zeros_like(awab)
            awpb[...] = jnp.zeros_like(awpb)
            dpw_ref[...] = jnp.zeros_like(dpw_ref)
            dps_ref[...] = jnp.zeros_like(dps_ref)

        f = _mix_forward(i, tm, (o0, o1, o2), (l0, l1, l2), rest_ref, halo_ref, wab_ref, wpb_ref, pw_ref, ps_ref)
        dmo_b = dmo_ref[...]
        dmerged = _dot_nt(dmo_b, wo_ref[...])
        awo[...] += _dot_tn(f["merged"].astype(BF16), dmo_b)
        sa, sp = f["sa"], f["sp"]
        dya = (dmerged * sa).astype(BF16)
        dyp = (dmerged * sp).astype(BF16)
        dg_attn = dmerged * f["y_attn"] * sa * (1.0 - sa)
        dg_pool = dmerged * f["y_pool"] * sp * (1.0 - sp)
        dag = _dot_nt(dya, wab_ref[...])
        awab[...] += _dot_tn(f["a_g"], dya)
        dmg = _dot_nt(dyp, wpb_ref[...])
        awpb[...] += _dot_tn(f["m_g"], dyp)
        dattn = dag * f["sil_a"]
        dattn_ref[...] = dattn
        prod = dattn * f["attn"]
        lane = lax.broadcasted_iota(jnp.int32, (tm, STAT_W), 1)
        for sb in range(N_SLAB):
            st = jnp.zeros((tm, STAT_W), F32)
            for h in range(HS):
                hs = slice((sb * HS + h) * HD, (sb * HS + h + 1) * HD)
                st = jnp.where(lane == h, f["lj"][:, hs.start:hs.start + 1], st)
                st = jnp.where(lane == HS + h, jnp.sum(prod[:, hs], axis=-1, keepdims=True), st)
            stats_ref[:, sb * STAT_W:(sb + 1) * STAT_W] = st
        dz_attn = dag * f["attn"] * (f["sg_a"] * (1.0 + f["z_attn"] * (1.0 - f["sg_a"])))
        dmixed = dmg * f["sil_p"]
        dz_pool = dmg * f["mixed"] * (f["sg_p"] * (1.0 + f["z_pool"] * (1.0 - f["sg_p"])))
        dps_ref[...] += jnp.sum(dmixed * f["mixed_raw"], axis=0, keepdims=True)
        dpm = (dmixed * ps_ref[...]).astype(BF16)
        for gi in range(len(POOL_WINDOWS)):
            cs = slice(gi * PGW, (gi + 1) * PGW)
            dpw_ref[gi] += _dot_tn(f["pooled"][gi], dpm[:, cs])
            dpooled_ref[:, cs] = _dot_nt(dpm[:, cs], pw_ref[gi].astype(BF16))
        drest_ref[:, 0:AW] = dz_attn.astype(BF16)
        drest_ref[:, AW:2 * AW] = jnp.zeros((tm, AW), BF16)
        drest_ref[:, 2 * AW:3 * AW] = dz_pool.astype(BF16)
        drest_ref[:, 3 * AW:3 * AW + D] = dg_attn.astype(BF16)
        drest_ref[:, 3 * AW + D:3 * AW + 2 * D] = dg_pool.astype(BF16)

        @pl.when(i == nt - 1)
        def _():
            pltpu.sync_copy(awo, dwo_hbm)
            for k in range(N_SHARD):
                pltpu.sync_copy(awab.at[:, pl.ds(k * sw, sw)], dwab_hbm.at[k])
                pltpu.sync_copy(awpb.at[:, pl.ds(k * sw, sw)], dwpb_hbm.at[k])

    row512 = pl.BlockSpec((tm, AW), lambda i: (i, 0))
    outs = pl.pallas_call(
        body, name="mix_bwd", grid=(nt,),
        in_specs=_mix_specs(tm) + [pl.BlockSpec((tm, D), lambda i: (i, 0)), pl.BlockSpec((D, D), lambda i: (0, 0))],
        out_specs=[row512, pl.BlockSpec((tm, N_SLAB * STAT_W), lambda i: (i, 0)), row512,
                   pl.BlockSpec((tm, REST_W), lambda i: (i, 0)), ANY, ANY, ANY,
                   pl.BlockSpec((4, PGW, PGW), lambda i: (0, 0, 0)), pl.BlockSpec((1, AW), lambda i: (0, 0))],
        out_shape=[_sds((S, AW)), _sds((S, N_SLAB * STAT_W)), _sds((S, AW)), _sds((S, REST_W), BF16),
                   _sds((D, D)), _sds((N_SHARD, AW, sw)), _sds((N_SHARD, AW, sw)), _sds((4, PGW, PGW)), _sds((1, AW))],
        scratch_shapes=[pltpu.VMEM((D, D), F32), pltpu.VMEM((AW, D), F32), pltpu.VMEM((AW, D), F32)],
        compiler_params=_params("arbitrary"),
    )(*os_, *ls_, rest, rest, wab, wpb, pool_w, pool_scale, dmo, wout)
    dattn, stats, dpooled, drest, dwo, dwab, dwpb, dpw, dps = outs
    return dattn, stats, dpooled, drest, dwo.reshape(N_SHARD, D // N_SHARD, D), dwab, dwpb, dpw, dps


def _pool_bwd(dpooled):
    S = dpooled.shape[0]
    tm = 512
    nt = S // tm

    def body(dp_ref, nxt_ref, du_ref):
        i = pl.program_id(0)
        t = i * tm + lax.broadcasted_iota(jnp.int32, (tm + HALO, 1), 0)
        nxt = jnp.where(i < nt - 1, nxt_ref[...], 0.0)
        ext = jnp.concatenate([dp_ref[...], nxt], axis=0)
        for gi, win in enumerate(POOL_WINDOWS):
            cs = slice(gi * PGW, (gi + 1) * PGW)
            s = ext[:, cs] / jnp.minimum(t + 1, win).astype(F32)
            sh = 1
            while sh < win:
                s = s + pltpu.roll(s, tm + HALO - sh, 0)
                sh *= 2
            du_ref[:, cs] = (s[:tm] - dp_ref[:, cs]).astype(BF16)

    return pl.pallas_call(
        body, name="pool_bwd", grid=(nt,),
        in_specs=[pl.BlockSpec((tm, AW), lambda i: (i, 0)),
                  pl.BlockSpec((HALO, AW), lambda i: (jnp.minimum((i + 1) * (tm // HALO), S // HALO - 1), 0))],
        out_specs=pl.BlockSpec((tm, AW), lambda i: (i, 0)),
        out_shape=_sds((S, AW), BF16), compiler_params=_params("parallel"),
    )(dpooled, dpooled)


TB = 1024


def _dh(dproj, wg_in, ride):
    S = dproj.shape[0]
    per = wg_in.shape[2] // TB
    nm, nk = S // TB, IN_W // TB

    def body(dp_ref, w_ref, out_ref):
        @pl.when(pl.program_id(1) == 0)
        def _():
            out_ref[...] = jnp.zeros_like(out_ref)

        out_ref[...] += _dot_nt(dp_ref[...], w_ref[...])

    (dh,), rode = _call_with_ride(
        body, ride, lambda: (pl.program_id(0) == 0) & (pl.program_id(1) == 0),
        lambda: (pl.program_id(0) == nm - 1) & (pl.program_id(1) == nk - 1),
        name="dh", grid=(nm, nk),
        in_specs=[pl.BlockSpec((TB, TB), lambda m, kk: (m, kk)),
                  pl.BlockSpec((None, D, TB), lambda m, kk: (kk // per, 0, kk % per))],
        out_specs=[pl.BlockSpec((TB, D), lambda m, kk: (m, 0))],
        out_shape=[_sds((S, D))], compiler_params=_params("arbitrary", "arbitrary"),
    )(dproj, wg_in)
    return dh, rode


def _dw_in(h_t, dproj):
    S = dproj.shape[0]
    per = IN_W // N_SHARD // TB

    def body(ht_ref, dp_ref, out_ref):
        @pl.when(pl.program_id(1) == 0)
        def _():
            out_ref[...] = jnp.zeros_like(out_ref)

        out_ref[...] += _dot(ht_ref[...], dp_ref[...])

    return pl.pallas_call(
        body, name="dw_in", grid=(IN_W // TB, S // TB),
        in_specs=[pl.BlockSpec((D, TB), lambda j, kk: (0, kk)), pl.BlockSpec((TB, TB), lambda j, kk: (kk, j))],
        out_specs=pl.BlockSpec((None, D, TB), lambda j, kk: (j // per, 0, j % per)),
        out_shape=_sds((N_SHARD, D, IN_W // N_SHARD)), compiler_params=_params("parallel", "arbitrary"),
    )(h_t, dproj)


def _prenorm_bwd(x, dh, dx2, norm_g, mod):
    S = x.shape[0]
    tm = 512

    def body(x_ref, dh_ref, dx2_ref, g_ref, mod_ref, gx_ref, dg_ref, dshift_ref, dscale_ref):
        i = pl.program_id(0)

        @pl.when(i == 0)
        def _():
            dg_ref[...] = jnp.zeros_like(dg_ref)
            dshift_ref[...] = jnp.zeros_like(dshift_ref)
            dscale_ref[...] = jnp.zeros_like(dscale_ref)

        xv = x_ref[...]
        dhv = dh_ref[...]
        g = g_ref[...]
        r = lax.rsqrt(jnp.mean(xv * xv, axis=-1, keepdims=True) + EPS)
        xh = xv * r
        dshift_ref[...] += jnp.sum(dhv, axis=0, keepdims=True)
        dscale_ref[...] += jnp.sum(dhv * (xh * g), axis=0, keepdims=True)
        dn1 = dhv * (1.0 + mod_ref[:, D:2 * D])
        dg_ref[...] += jnp.sum(dn1 * xh, axis=0, keepdims=True)
        dxh = dn1 * g
        gx_ref[...] = dx2_ref[...] + r * (dxh - xh * jnp.mean(dxh * xh, axis=-1, keepdims=True))

    row = pl.BlockSpec((tm, D), lambda i: (i, 0))
    vec = pl.BlockSpec((1, D), lambda i: (0, 0))
    return pl.pallas_call(
        body, name="prenorm_bwd", grid=(S // tm,),
        in_specs=[row, row, row, vec, pl.BlockSpec((1, 3 * D), lambda i: (0, 0))],
        out_specs=[row, vec, vec, vec],
        out_shape=[_sds((S, D)), _sds((1, D)), _sds((1, D)), _sds((1, D))],
        compiler_params=_params("arbitrary"),
    )(x, dh, dx2, norm_g, mod)


def _local_step(x, target, mod, wg_in, wab, wpb, wout, pool_w, pool_scale, rel_bias, norm_g, final_g, half_idx,
                chip_half):
    buckets = jnp.asarray(_bucket_tables())
    bias_tab = _bias_table(rel_bias, buckets)
    h = _prenorm(x, norm_g, mod)
    qkv = [_proj(h, wg_in, 3 * g, 3, F32, f"proj_qkv{g}") for g in range(NG)]
    rest = _proj(h, wg_in, NCB_QKV, REST_W // CB, F32, "proj_rest")
    os_, ls_ = zip(*[_attn_fwd(qkv[g], bias_tab, g) for g in range(NG)])
    dx2, dmo, loss, dfinal_g, dgate = _tail(x, target, os_, ls_, rest, wab, wpb, pool_w, pool_scale, wout, mod, final_g)
    dattn, stats, dpooled, drest, dw_out, dw_ab, dw_pb, dpool_w, dpool_scale = _mix_bwd(
        dmo, os_, ls_, rest, wab, wpb, pool_w, pool_scale, wout)
    du = _pool_bwd(dpooled)

    small = [dw_ab, dw_pb, dw_out]
    dqkv0, ds0, sib_small = _attn_bwd(qkv[0], dattn, stats, bias_tab, 0, _ride_sibling_halves(small))
    p_small = [_pair_sum(g, t, half_idx, f"rs_pair_sum{a}") for a, (g, t) in enumerate(zip(small, sib_small))]
    dqkv1, ds1, u_small = _attn_bwd(qkv[1], dattn, stats, bias_tab, 1,
                                    _ride_chip_exchange([p16 for _, p16 in p_small]))
    rs_ab, rs_pb, rs_out = [_chip_sum(p32, u, chip_half, f"rs_chip_sum{a}")
                            for a, ((p32, _), u) in enumerate(zip(p_small, u_small))]
    dqkv2, ds2, _ = _attn_bwd(qkv[2], dattn, stats, bias_tab, 2, None)

    dproj = jnp.concatenate([a.astype(BF16) for a in dqkv0 + dqkv1 + dqkv2] + [drest[:, :AW], du, drest[:, 2 * AW:]],
                            axis=1)
    dw_in = _dw_in(h.T, dproj)
    drel_rows, (sib_in,) = _bias_grad(jnp.concatenate([ds0, ds1, ds2], axis=0), buckets,
                                      _ride_sibling_halves([dw_in]))
    drel = drel_rows[:, 0, :NUM_BUCKETS].T
    p32_in, p16_in = _pair_sum(dw_in, sib_in, half_idx, "rs_pair_sum_in")
    dh, (u_in,) = _dh(dproj, wg_in, _ride_chip_exchange([p16_in]))
    rs_in = _chip_sum(p32_in, u_in, chip_half, "rs_chip_sum_in")

    grad_x, dnorm_g, dshift, dscale = _prenorm_bwd(x, dh, dx2, norm_g, mod)
    dmod = jnp.concatenate([dshift, dscale, dgate], axis=1)
    return dict(loss=loss[0, 0], grad_x=grad_x, dmod=dmod, dnorm_g=dnorm_g, dfinal_g=dfinal_g, dpool_w=dpool_w,
                dpool_scale=dpool_scale, drel_bias=drel, dw_in=dw_in, dw_attn_br=dw_ab, dw_pool_br=dw_pb,
                dw_out=dw_out, rs_in=rs_in, rs_attn_br=rs_ab, rs_pool_br=rs_pb, rs_out=rs_out)


def _allgather8(blocks, name):
    nb = len(blocks)

    def body(*refs):
        ins, outs = refs[:nb], refs[nb:2 * nb]
        send_sems, recv_sems, local_sems = refs[2 * nb:]
        x, y, c = lax.axis_index("x"), lax.axis_index("y"), lax.axis_index("c")
        me, sibling = (x, y, c), (x, y, 1 - c)
        chips = [(1 - x, y), (x, 1 - y), (1 - x, 1 - y)]

        def copy(a, k, block, to, src=None):
            dst = outs[a].at[4 * block[0] + 2 * block[1] + block[2]]
            return pltpu.make_async_remote_copy(src_ref=dst if src is None else src, dst_ref=dst,
                                                send_sem=send_sems.at[a, k], recv_sem=recv_sems.at[a, k],
                                                device_id=to, device_id_type=MESH)

        mine = [pltpu.make_async_copy(ins[a], outs[a].at[4 * x + 2 * y + c], local_sems.at[a]) for a in range(nb)]
        for cp in mine:
            cp.start()
        first = []
        for a in range(nb):
            first.append(copy(a, 0, me, sibling, src=ins[a]))
            first += [copy(a, 1 + j, me, (*chip, c), src=ins[a]) for j, chip in enumerate(chips)]
        for cp in first:
            cp.start()
        passed = []
        for j, chip in enumerate(chips):
            for a in range(nb):
                copy(a, 1 + j, (*chip, c), me).wait_recv()
                cp = copy(a, 4 + j, (*chip, c), sibling)
                cp.start()
                passed.append(cp)
        for a in range(nb):
            copy(a, 0, sibling, me).wait_recv()
            for j, chip in enumerate(chips):
                copy(a, 4 + j, (*chip, 1 - c), me).wait_recv()
        for cp in first + passed:
            cp.wait_send()
        for cp in mine:
            cp.wait()

    return pl.pallas_call(
        body, name=name, in_specs=[ANY] * nb, out_specs=[ANY] * nb,
        out_shape=[_sds((8,) + b.shape, b.dtype) for b in blocks],
        scratch_shapes=[_dma_sems(nb, 7), _dma_sems(nb, 7), _dma_sems(nb)],
    )(*blocks)


def _ride_sibling_halves(gs):
    def copies(ins, outs, send_sems, recv_sems):
        x, y, c = lax.axis_index("x"), lax.axis_index("y"), lax.axis_index("c")
        cps = []
        for a in range(len(gs)):
            r2 = ins[a].shape[1] // 2
            other = ins[a].at[:, pl.ds((1 - c) * r2, r2), :]
            cps.append(pltpu.make_async_remote_copy(src_ref=other, dst_ref=outs[a], send_sem=send_sems.at[a],
                                                    recv_sem=recv_sems.at[a], device_id=(x, y, 1 - c),
                                                    device_id_type=MESH))
        return cps

    return _Ride(gs, [_sds((g.shape[0], g.shape[1] // 2, g.shape[2]), g.dtype) for g in gs], len(gs), copies)


def _pair_sum(g, t, half, name):
    nsh, rows, cols = g.shape
    r2 = rows // 2
    tr = _row_tile(r2, cols)
    nt = r2 // tr

    def body(half_ref, g_ref, t_ref, p32_ref, p16_ref):
        p = g_ref[...] + t_ref[...]
        p32_ref[...] = p
        p16_ref[...] = p.astype(BF16)

    blk = pl.BlockSpec((None, tr, cols), lambda k, i, half_ref: (k, i, 0))
    return pl.pallas_call(
        body, name=name,
        grid_spec=pltpu.PrefetchScalarGridSpec(
            num_scalar_prefetch=1, grid=(nsh, nt),
            in_specs=[pl.BlockSpec((None, tr, cols), lambda k, i, half_ref: (k, half_ref[0] * nt + i, 0)), blk],
            out_specs=[blk, blk]),
        out_shape=[_sds((nsh, r2, cols)), _sds((nsh, r2, cols), BF16)],
        compiler_params=_params("parallel", "parallel"),
    )(half, g, t)


def _ride_chip_exchange(ps):
    def copies(ins, outs, send_sems, recv_sems):
        x, y, c = lax.axis_index("x"), lax.axis_index("y"), lax.axis_index("c")
        chips = [(1 - x, y), (x, 1 - y), (1 - x, 1 - y)]
        cps = []
        for a in range(len(ps)):
            for j, (ox, oy) in enumerate(chips):
                cps.append(pltpu.make_async_remote_copy(src_ref=ins[a].at[2 * ox + oy], dst_ref=outs[a].at[j],
                                                        send_sem=send_sems.at[3 * a + j],
                                                        recv_sem=recv_sems.at[3 * a + j],
                                                        device_id=(ox, oy, c), device_id_type=MESH))
        return cps

    return _Ride(ps, [_sds((3,) + p.shape[1:], p.dtype) for p in ps], 3 * len(ps), copies)


def _chip_sum(p32, u, chip_half, name):
    r2, cols = p32.shape[1:]
    tr = _row_tile(r2, cols)
    nt = r2 // tr

    def body(ch_ref, p_ref, u_ref, o_ref):
        acc = p_ref[...]
        for j in range(3):
            acc = acc + u_ref[j].astype(F32)
        o_ref[...] = acc

    return pl.pallas_call(
        body, name=name,
        grid_spec=pltpu.PrefetchScalarGridSpec(
            num_scalar_prefetch=1, grid=(nt,),
            in_specs=[pl.BlockSpec((None, tr, cols), lambda i, ch_ref: (ch_ref[0], i, 0)),
                      pl.BlockSpec((3, tr, cols), lambda i, ch_ref: (0, i, 0))],
            out_specs=pl.BlockSpec((tr, cols), lambda i, ch_ref: (ch_ref[1] * nt + i, 0))),
        out_shape=_sds((2 * r2, cols)), compiler_params=_params("parallel"),
    )(chip_half, p32, u)


def _sibling_join(fs, name):
    nb = len(fs)

    def body(*refs):
        outs = refs[nb:2 * nb]
        send_sems, recv_sems = refs[2 * nb:]
        x, y, c = lax.axis_index("x"), lax.axis_index("y"), lax.axis_index("c")
        cps = []
        for a in range(nb):
            r2 = outs[a].shape[0] // 2
            rows = outs[a].at[pl.ds(c * r2, r2), :]
            cps.append(pltpu.make_async_remote_copy(src_ref=rows, dst_ref=rows, send_sem=send_sems.at[a],
                                                    recv_sem=recv_sems.at[a], device_id=(x, y, 1 - c),
                                                    device_id_type=MESH))
        for cp in cps:
            cp.start()
        for cp in cps:
            cp.wait()

    return pl.pallas_call(
        body, name=name, in_specs=[ANY] * nb, out_specs=[ANY] * nb,
        out_shape=[_sds(f.shape, f.dtype) for f in fs],
        input_output_aliases={a: a for a in range(nb)},
        scratch_shapes=[_dma_sems(nb), _dma_sems(nb)],
    )(*fs)


def _row_tile(rows, cols):
    tile = rows
    while tile * cols * 4 > (1 << 20) and tile % 16 == 0:
        tile //= 2
    return tile


def _sum_leading(a, name):
    k = a.shape[0]
    a3 = a.reshape(k, -1, a.shape[-1])
    rows, cols = a3.shape[1:]
    tr = _row_tile(rows, cols)

    def body(a_ref, o_ref):
        acc = a_ref[0]
        for s in range(1, k):
            acc = acc + a_ref[s]
        o_ref[...] = acc

    out = pl.pallas_call(
        body, name=name, grid=(rows // tr,),
        in_specs=[pl.BlockSpec((k, tr, cols), lambda i: (0, i, 0))],
        out_specs=pl.BlockSpec((tr, cols), lambda i: (i, 0)),
        out_shape=_sds((rows, cols), a.dtype), compiler_params=_params("parallel"),
    )(a3)
    return out.reshape(a.shape[1:])


def _w_ada_grad(c_all, dmod_cols):
    def body(c_ref, d_ref, o_ref):
        o_ref[...] = _dot_tn(c_ref[...].astype(BF16), d_ref[...].astype(BF16))

    return pl.pallas_call(body, name="w_ada_grad", out_shape=_sds((c_all.shape[1], dmod_cols.shape[1])),
                          compiler_params=_params())(c_all, dmod_cols)


def _adamw(w, g, m, v, name):
    rows, cols = w.shape
    tr = _row_tile(rows, cols)

    def body(w_ref, g_ref, m_ref, v_ref, d_ref, nm_ref, nv_ref):
        gv = g_ref[...]
        nm = ADAM_B1 * m_ref[...] + (1.0 - ADAM_B1) * gv
        nv = ADAM_B2 * v_ref[...] + (1.0 - ADAM_B2) * (gv * gv)
        m_hat = nm / (1.0 - ADAM_B1 ** ADAM_STEP)
        v_hat = nv / (1.0 - ADAM_B2 ** ADAM_STEP)
        d_ref[...] = -ADAM_LR * (m_hat / (jnp.sqrt(v_hat) + ADAM_EPS) + ADAM_WD * w_ref[...])
        nm_ref[...] = nm
        nv_ref[...] = nv

    spec = pl.BlockSpec((tr, cols), lambda i: (i, 0))
    return pl.pallas_call(
        body, name=name, grid=(rows // tr,), in_specs=[spec] * 4, out_specs=[spec] * 3,
        out_shape=[_sds((rows, cols))] * 3, compiler_params=_params("parallel"),
    )(w, g, m, v)


def _pack_small(b_ada, norm_g, final_g, pool_scale, rel_bias, loss_row, pool_w):
    pad = jnp.zeros((PK_POOLW - PK_LOSS - 1) * 128, F32)
    flat = jnp.concatenate([b_ada.reshape(-1), norm_g.reshape(-1), final_g.reshape(-1), pool_scale.reshape(-1),
                            rel_bias.reshape(-1), loss_row.reshape(-1), pad, pool_w.reshape(-1)])
    return flat.reshape(PK_ROWS, 128)


def _unpack_small(p):
    def take(r0, r1, shape):
        return p[r0:r1].reshape(shape)

    return dict(b_ada=take(PK_BADA, PK_NORMG, (1, 3 * D)), norm_g=take(PK_NORMG, PK_FINALG, (1, D)),
                final_g=take(PK_FINALG, PK_PSCALE, (D,)), pool_scale=take(PK_PSCALE, PK_RELB, (1, AW)),
                rel_bias=take(PK_RELB, PK_LOSS, (NUM_BUCKETS, NG * NH)), loss=p[PK_LOSS, 0],
                pool_w=take(PK_POOLW, PK_ROWS, (1, 4, PGW, PGW)))


def kernel(x, c, norm_g, w_ada, b_ada, w_in, pool_w, pool_scale, w_attn_br, w_pool_br, w_out, rel_bias, final_g, loss_target, m_norm_g, m_w_ada, m_b_ada, m_w_in, m_pool_w, m_pool_scale, m_w_attn_br, m_w_pool_br, m_w_out, m_rel_bias, m_final_g, v_norm_g, v_w_ada, v_b_ada, v_w_in, v_pool_w, v_pool_scale, v_w_attn_br, v_w_pool_br, v_w_out, v_rel_bias, v_final_g):
    ix, iy, ic = lax.axis_index("x"), lax.axis_index("y"), lax.axis_index("c")
    dev = 4 * ix + 2 * iy + ic
    chip = 2 * ix + iy

    def half(w):
        r2 = w.shape[0] // 2
        return lax.dynamic_slice_in_dim(w, ic * r2, r2, axis=0).astype(BF16)

    gathered = _allgather8([jnp.broadcast_to(c, (8, D)), half(w_in[0]), half(w_attn_br[0]), half(w_pool_br[0]),
                            half(w_out[0])], "gather_weights")
    c_all = gathered[0][:, 0, :]
    wg_in = gathered[1].reshape(N_SHARD, D, IN_W // N_SHARD)
    wab = gathered[2].reshape(N_SHARD, AW, D // N_SHARD).transpose(1, 0, 2).reshape(AW, D)
    wpb = gathered[3].reshape(N_SHARD, AW, D // N_SHARD).transpose(1, 0, 2).reshape(AW, D)
    wout = gathered[4].reshape(D, D)

    mw = 3 * D // N_SHARD
    modp = _mod_partial(c_all, w_ada[0], lax.dynamic_slice_in_dim(b_ada, chip * mw, mw, axis=1))
    mod_all = _allgather8([modp], "gather_mod")[0]
    mod_full = mod_all[::2].transpose(1, 0, 2).reshape(8, 3 * D)
    mod = lax.dynamic_slice_in_dim(mod_full, dev, 1, axis=0)

    half_idx = jnp.stack([ic]).astype(jnp.int32)
    chip_half = jnp.stack([chip, ic]).astype(jnp.int32)
    r = _local_step(x[0], loss_target[0], mod, wg_in, wab, wpb, wout, pool_w[0], pool_scale, rel_bias, norm_g,
                    final_g.reshape(1, D), half_idx, chip_half)

    packed = _pack_small(r["dmod"], r["dnorm_g"], r["dfinal_g"], r["dpool_scale"], r["drel_bias"],
                         jnp.full((128,), r["loss"], F32), r["dpool_w"])
    small_all = _allgather8([packed], "gather_small")[0]
    small_sum = _sum_leading(small_all, "sum_small")
    dmod_all = small_all[:, PK_BADA:PK_NORMG, :].reshape(8, 3 * D)
    g_w_ada = _w_ada_grad(c_all, lax.dynamic_slice_in_dim(dmod_all, chip * mw, mw, axis=1))

    g_w_in, g_w_ab, g_w_pb, g_w_out = _sibling_join([r["rs_in"], r["rs_attn_br"], r["rs_pool_br"], r["rs_out"]],
                                                    "rs_sibling_join")

    small_w = _pack_small(b_ada, norm_g, final_g, pool_scale, rel_bias, jnp.zeros((128,), F32), pool_w)
    small_m = _pack_small(m_b_ada, m_norm_g, m_final_g, m_pool_scale, m_rel_bias, jnp.zeros((128,), F32), m_pool_w)
    small_v = _pack_small(v_b_ada, v_norm_g, v_final_g, v_pool_scale, v_rel_bias, jnp.ones((128,), F32), v_pool_w)
    sd, sm, sv = (_unpack_small(p) for p in _adamw(small_w, small_sum, small_m, small_v, "adamw_small"))
    sg = _unpack_small(small_sum)
    upd = {
        "w_ada": (g_w_ada,) + tuple(_adamw(w_ada[0], g_w_ada, m_w_ada[0], v_w_ada[0], "adamw_w_ada")),
        "w_in": (g_w_in,) + tuple(_adamw(w_in[0], g_w_in, m_w_in[0], v_w_in[0], "adamw_w_in")),
        "w_attn_br": (g_w_ab,) + tuple(_adamw(w_attn_br[0], g_w_ab, m_w_attn_br[0], v_w_attn_br[0], "adamw_w_ab")),
        "w_pool_br": (g_w_pb,) + tuple(_adamw(w_pool_br[0], g_w_pb, m_w_pool_br[0], v_w_pool_br[0], "adamw_w_pb")),
        "w_out": (g_w_out,) + tuple(_adamw(w_out[0], g_w_out, m_w_out[0], v_w_out[0], "adamw_w_out")),
    }
    names = ["norm_g", "w_ada", "b_ada", "w_in", "pool_w", "pool_scale", "w_attn_br", "w_pool_br", "w_out",
             "rel_bias", "final_g"]
    outs = [sg["loss"], r["grad_x"][None]]
    for kind in range(4):
        for nme in names:
            if nme in upd:
                outs.append(upd[nme][kind][None])
            else:
                outs.append((sg, sd, sm, sv)[kind][nme])
    return tuple(outs)
```

```python
import functools
import math

import numpy as np
import jax
import jax.numpy as jnp
from jax import lax
from jax.experimental import pallas as pl
from jax.experimental.pallas import tpu as pltpu

F32 = jnp.float32
BF16 = jnp.bfloat16

D = 1024
HD = 64
NH = 8
AW = NH * HD
GROUPS = ((128, 1), (512, 4), (2048, 16))
NG = len(GROUPS)
BLK = 128
GW = 3 * AW
QKV_W = NG * GW
REST_W = 3584
IN_W = QKV_W + REST_W
CB = 512
NCB = IN_W // CB
NCB_QKV = QKV_W // CB
POOL_WINDOWS = (2, 4, 8, 16)
PGW = 128
HALO = 16
NUM_BUCKETS = 32
MAX_DISTANCE = 2048
EPS = 1e-6
NEG = -1e30
N_SHARD = 4
VMEM_LIMIT = 56 * 1024 * 1024

ADAM_LR = 0.001
ADAM_B1 = 0.9
ADAM_B2 = 0.999
ADAM_EPS = 1e-08
ADAM_WD = 0.01
ADAM_STEP = 10

PK_BADA, PK_NORMG, PK_FINALG, PK_PSCALE, PK_RELB, PK_LOSS, PK_POOLW, PK_ROWS = 0, 24, 32, 40, 44, 50, 56, 568

ANY = pl.BlockSpec(memory_space=pl.ANY)
MESH = pl.DeviceIdType.MESH


def _params(*sem):
    return pltpu.CompilerParams(dimension_semantics=sem, vmem_limit_bytes=VMEM_LIMIT)


def _sds(shape, dtype=F32):
    return jax.ShapeDtypeStruct(shape, dtype)


def _dot(a, b):
    return jnp.dot(a, b, preferred_element_type=F32)


def _dot_nt(a, b):
    return lax.dot_general(a, b, (((1,), (1,)), ((), ())), preferred_element_type=F32)


def _dot_tn(a, b):
    return lax.dot_general(a, b, (((0,), (0,)), ((), ())), preferred_element_type=F32)


def _sigmoid(z):
    return 0.5 * jnp.tanh(0.5 * z) + 0.5


def _dma_sems(*shape):
    return pltpu.SemaphoreType.DMA(shape)


class _Ride:
    def __init__(self, arrays, out_shapes, n_copies, copies):
        self.arrays, self.out_shapes, self.n_copies, self.copies = list(arrays), list(out_shapes), n_copies, copies


def _call_with_ride(body, ride, first, last, *, in_specs, out_specs, out_shape, scratch_shapes=(), **kw):
    in_specs, out_specs, out_shape, scratch_shapes = list(in_specs), list(out_specs), list(out_shape), list(scratch_shapes)
    n_in, n_out, n_sc = len(in_specs), len(out_specs), len(scratch_shapes)
    if ride is None:
        def run_plain(*operands):
            return pl.pallas_call(body, in_specs=in_specs, out_specs=out_specs, out_shape=out_shape,
                                  scratch_shapes=scratch_shapes, **kw)(*operands), []
        return run_plain
    n_ri, n_ro = len(ride.arrays), len(ride.out_shapes)

    def wrapped(*refs):
        ins, rest = refs[:n_in], refs[n_in:]
        r_ins, rest = rest[:n_ri], rest[n_ri:]
        outs, rest = rest[:n_out], rest[n_out:]
        r_outs, rest = rest[:n_ro], rest[n_ro:]
        scratch, (send_sems, recv_sems) = rest[:n_sc], rest[n_sc:]

        @pl.when(first())
        def _():
            for cp in ride.copies(r_ins, r_outs, send_sems, recv_sems):
                cp.start()

        body(*ins, *outs, *scratch)

        @pl.when(last())
        def _():
            for cp in ride.copies(r_ins, r_outs, send_sems, recv_sems):
                cp.wait()

    def run(*operands):
        res = pl.pallas_call(
            wrapped, in_specs=in_specs + [ANY] * n_ri, out_specs=out_specs + [ANY] * n_ro,
            out_shape=out_shape + ride.out_shapes,
            scratch_shapes=scratch_shapes + [_dma_sems(ride.n_copies), _dma_sems(ride.n_copies)], **kw,
        )(*operands, *ride.arrays)
        return res[:n_out], res[n_out:]
    return run


def _bucket_tables():
    i = np.arange(BLK)[:, None]
    j = np.arange(2 * BLK)[None, :]
    dist = BLK + i - j
    valid = (dist >= 0) & (dist <= BLK)
    tabs = []
    for _, dil in GROUPS:
        n = (np.clip(dist, 0, BLK) * dil).astype(np.int32)
        max_exact = NUM_BUCKETS // 2
        nf = np.maximum(n, 1).astype(np.float32)
        large = max_exact + (np.log(nf / np.float32(max_exact)) / np.float32(math.log(MAX_DISTANCE / max_exact))
                             * np.float32(NUM_BUCKETS - max_exact)).astype(np.int32)
        large = np.minimum(large, NUM_BUCKETS - 1)
        bucket = np.where(n < max_exact, n, large)
        tab = np.where(valid, bucket, -1).astype(np.int32)
        perm = _block_perm(dil)
        tabs.append(tab[perm][:, np.concatenate([perm, BLK + perm])])
    return np.stack(tabs)


def _bias_table(rel_bias, buckets):
    def body(rb_ref, bk_ref, out_ref):
        gh = pl.program_id(0)
        bk = bk_ref[...]
        acc = jnp.full((BLK, 2 * BLK), NEG, F32)
        for b in range(NUM_BUCKETS):
            acc = jnp.where(bk == b, rb_ref[b, gh], acc)
        out_ref[...] = acc

    return pl.pallas_call(
        body, name="bias_table", grid=(NG * NH,),
        in_specs=[pl.BlockSpec(memory_space=pltpu.SMEM),
                  pl.BlockSpec((None, BLK, 2 * BLK), lambda gh: (gh // NH, 0, 0))],
        out_specs=pl.BlockSpec((None, BLK, 2 * BLK), lambda gh: (gh, 0, 0)),
        out_shape=_sds((NG * NH, BLK, 2 * BLK)),
        compiler_params=_params("arbitrary"),
    )(rel_bias, buckets)


def _bias_grad(ds_acc, buckets, ride):
    def body(acc_ref, bk_ref, out_ref):
        bk = bk_ref[...]
        acc = acc_ref[...]
        lane = lax.broadcasted_iota(jnp.int32, (8, 128), 1)
        out = jnp.zeros((8, 128), F32)
        for b in range(NUM_BUCKETS):
            val = jnp.sum(jnp.where(bk == b, acc, 0.0))
            out = jnp.where(lane == b, val, out)
        out_ref[...] = out

    (out,), rode = _call_with_ride(
        body, ride, lambda: pl.program_id(0) == 0, lambda: pl.program_id(0) == NG * NH - 1,
        name="bias_grad", grid=(NG * NH,),
        in_specs=[pl.BlockSpec((None, BLK, 2 * BLK), lambda gh: (gh, 0, 0)),
                  pl.BlockSpec((None, BLK, 2 * BLK), lambda gh: (gh // NH, 0, 0))],
        out_specs=[pl.BlockSpec((None, 8, 128), lambda gh: (gh, 0, 0))],
        out_shape=[_sds((NG * NH, 8, 128))],
        compiler_params=_params("arbitrary"),
    )(ds_acc, buckets)
    return out, rode


def _mod_partial(c_all, w_ada_s, b_ada_s):
    def body(c_ref, w_ref, b_ref, o_ref):
        o_ref[...] = _dot(c_ref[...].astype(BF16), w_ref[...].astype(BF16)) + b_ref[...]

    return pl.pallas_call(body, name="mod_partial", out_shape=_sds((8, w_ada_s.shape[1])),
                          compiler_params=_params())(c_all, w_ada_s, b_ada_s)


def _prenorm(x, norm_g, mod):
    S = x.shape[0]
    tm = 512

    def body(x_ref, g_ref, mod_ref, h_ref):
        xv = x_ref[...]
        r = lax.rsqrt(jnp.mean(xv * xv, axis=-1, keepdims=True) + EPS)
        n1 = xv * r * g_ref[...]
        h_ref[...] = (n1 * (1.0 + mod_ref[:, D:2 * D]) + mod_ref[:, 0:D]).astype(BF16)

    return pl.pallas_call(
        body, name="prenorm", grid=(S // tm,),
        in_specs=[pl.BlockSpec((tm, D), lambda i: (i, 0)), pl.BlockSpec((1, D), lambda i: (0, 0)),
                  pl.BlockSpec((1, 3 * D), lambda i: (0, 0))],
        out_specs=pl.BlockSpec((tm, D), lambda i: (i, 0)),
        out_shape=_sds((S, D), BF16), compiler_params=_params("parallel"),
    )(x, norm_g, mod)


def _proj(h, wg_in, j0, nj, dtype, name):
    S = h.shape[0]
    tm = 1024
    per = wg_in.shape[2] // CB

    def body(h_ref, w_ref, o_ref):
        o_ref[...] = _dot(h_ref[...], w_ref[...]).astype(dtype)

    return pl.pallas_call(
        body, name=name, grid=(S // tm, nj),
        in_specs=[pl.BlockSpec((tm, D), lambda m, j: (m, 0)),
                  pl.BlockSpec((None, D, CB), lambda m, j: ((j0 + j) // per, 0, (j0 + j) % per))],
        out_specs=pl.BlockSpec((tm, CB), lambda m, j: (m, j)),
        out_shape=_sds((S, nj * CB), dtype), compiler_params=_params("parallel", "parallel"),
    )(h, wg_in)


HS = 4
SLAB = HS * HD


def _lane_head(rows):
    return lax.broadcasted_iota(jnp.int32, (rows, SLAB), 1) // HD


def _head_stack(a):
    head = _lane_head(a.shape[0])
    return jnp.concatenate([jnp.where(head == h, a, jnp.zeros_like(a)) for h in range(HS)], axis=0)


def _head_unstack(a):
    rows = a.shape[0] // HS
    head = _lane_head(rows)
    out = a[:rows]
    for h in range(1, HS):
        out = jnp.where(head == h, a[h * rows:(h + 1) * rows], out)
    return out


STAT_W = 128
VIEW = 16


def _sub_layout(dil):
    if dil == 1:
        return BLK, [None]
    return BLK * dil // VIEW, [[r + dil * u for u in range(VIEW // dil)] for r in range(dil)]


def _block_perm(dil):
    a_rows, _ = _sub_layout(dil)
    p = np.arange(BLK)
    return p if dil == 1 else (VIEW // dil) * (p % a_rows) + p // a_rows


LB = 128
N_SLAB = NH // HS


def _ld(refs, bs, s, w):
    if bs is None:
        return refs[0][:, s * w:(s + 1) * w]
    a_rows = refs[0].shape[0] // VIEW
    return jnp.concatenate([jnp.concatenate([ref[pl.ds(b, a_rows, stride=VIEW), :] for b in bs], axis=0)
                            for ref in refs], axis=1)


def _st(ref, bs, s, val):
    if bs is None:
        ref[:, s * SLAB:(s + 1) * SLAB] = val
        return
    a_rows = val.shape[0] // len(bs)
    for u, b in enumerate(bs):
        ref[:, b, :] = val[u * a_rows:(u + 1) * a_rows]


def _attn_views(dil, S):
    a_rows, subs = _sub_layout(dil)
    if dil == 1:
        def ispecs(base, w, f):
            return [pl.BlockSpec((BLK, N_SLAB * w), lambda sg, n: (f(n), base // (N_SLAB * w)))]
        return subs, S // BLK, N_SLAB, ispecs, (lambda w: (S, w)), (
            lambda f: pl.BlockSpec((BLK, AW), lambda sg, n: (f(n), 0)))

    def ispecs(base, w, f):
        return [pl.BlockSpec((a_rows * VIEW, LB), lambda sg, n, k=k: (f(n), (base + sg * w) // LB + k))
                for k in range(w // LB)]
    return subs, S // (a_rows * VIEW), 1, ispecs, (lambda w: (S // VIEW, VIEW, w)), (
        lambda f: pl.BlockSpec((a_rows, VIEW, SLAB), lambda sg, n: (f(n), 0, sg)))


def _attn_fwd(qkv_g, bias_tab, g):
    S = qkv_g.shape[0]
    subs, nbq, sps, ispecs, shape, ospec = _attn_views(GROUPS[g][1], S)
    cur, prev = (lambda n: n), (lambda n: jnp.maximum(n - 1, 0))
    in_specs = [ispecs(0, SLAB, cur), ispecs(AW, SLAB, prev), ispecs(AW, SLAB, cur), ispecs(2 * AW, SLAB, prev),
                ispecs(2 * AW, SLAB, cur)]
    nl = len(in_specs[0])

    def body(*refs):
        q, kp, kc, vp, vc = (refs[t * nl:(t + 1) * nl] for t in range(5))
        b_ref, o_ref, l_ref = refs[5 * nl:]
        n = pl.program_id(1)
        col = lax.broadcasted_iota(jnp.int32, (HS * BLK, 2 * BLK), 1)
        keep = (col >= BLK) | (n > 0)
        for s_ in range(sps):
            bias = b_ref[pl.ds(s_ * HS, HS)].reshape(HS * BLK, 2 * BLK)
            for bs in subs:
                kb = jnp.concatenate([_ld(kp, bs, s_, SLAB), _ld(kc, bs, s_, SLAB)], axis=0).astype(BF16)
                vb = jnp.concatenate([_ld(vp, bs, s_, SLAB), _ld(vc, bs, s_, SLAB)], axis=0).astype(BF16)
                s = _dot_nt(_head_stack(_ld(q, bs, s_, SLAB).astype(BF16)), kb) * (HD ** -0.5) + bias
                s = jnp.where(keep, s, NEG)
                m = jnp.max(s, axis=-1, keepdims=True)
                p = jnp.exp(s - m)
                den = jnp.sum(p, axis=-1, keepdims=True)
                _st(o_ref, bs, s_, _head_unstack(_dot(p.astype(BF16), vb) / den))
                _st(l_ref, bs, s_, _head_unstack(jnp.broadcast_to(m + jnp.log(den), (HS * BLK, SLAB))))

    out = _sds(shape(AW))
    o, l = pl.pallas_call(
        body, name=f"attn_fwd{g}", grid=(N_SLAB // sps, nbq),
        in_specs=sum(in_specs, []) + [pl.BlockSpec((sps * HS, BLK, 2 * BLK),
                                                   lambda sg, n: (g * (N_SLAB // sps) + sg, 0, 0))],
        out_specs=[ospec(cur), ospec(cur)],
        out_shape=[out, out], compiler_params=_params("parallel", "arbitrary"),
    )(*([qkv_g] * (5 * nl)), bias_tab)
    return o.reshape(S, AW), l.reshape(S, AW)


def _attn_bwd(qkv_g, dattn, stats, bias_tab, g, ride):
    S = qkv_g.shape[0]
    subs, nbq, sps, ispecs, shape, ospec = _attn_views(GROUPS[g][1], S)
    cur = lambda n: jnp.minimum(n, nbq - 1)
    prev = lambda n: jnp.clip(n - 1, 0, nbq - 1)
    late = lambda n: jnp.maximum(n - 1, 0)
    in_specs = [ispecs(0, SLAB, cur), ispecs(AW, SLAB, prev), ispecs(AW, SLAB, cur), ispecs(2 * AW, SLAB, prev),
                ispecs(2 * AW, SLAB, cur), ispecs(0, SLAB, cur), ispecs(0, STAT_W, cur)]
    nl = len(in_specs[0])

    def body(*refs):
        q, kp, kc, vp, vc, da = (refs[t * nl:(t + 1) * nl] for t in range(6))
        st_ref, b_ref, dq_ref, dk_ref, dv_ref, ds_ref, ck_ref, cv_ref = refs[6 * nl:]
        n = pl.program_id(1)

        @pl.when(n == 0)
        def _():
            ds_ref[...] = jnp.zeros_like(ds_ref)
            ck_ref[...] = jnp.zeros_like(ck_ref)
            cv_ref[...] = jnp.zeros_like(cv_ref)

        @pl.when(n < nbq)
        def _():
            col = lax.broadcasted_iota(jnp.int32, (HS * BLK, 2 * BLK), 1)
            keep = (col >= BLK) | (n > 0)
            for s_ in range(sps):
                cs = slice(s_ * SLAB, (s_ + 1) * SLAB)
                bias = b_ref[pl.ds(s_ * HS, HS)].reshape(HS * BLK, 2 * BLK)
                for i, bs in enumerate(subs):
                    st = _ld((st_ref,), bs, s_, STAT_W)
                    kb = jnp.concatenate([_ld(kp, bs, s_, SLAB), _ld(kc, bs, s_, SLAB)], axis=0).astype(BF16)
                    vb = jnp.concatenate([_ld(vp, bs, s_, SLAB), _ld(vc, bs, s_, SLAB)], axis=0).astype(BF16)
                    lse = jnp.concatenate([st[:, h:h + 1] for h in range(HS)], axis=0)
                    delta = jnp.concatenate([st[:, HS + h:HS + h + 1] for h in range(HS)], axis=0)
                    qs = _head_stack(_ld(q, bs, s_, SLAB).astype(BF16))
                    dos = _head_stack(_ld(da, bs, s_, SLAB).astype(BF16))
                    s = _dot_nt(qs, kb) * (HD ** -0.5) + bias
                    s = jnp.where(keep, s, NEG)
                    p = jnp.exp(s - lse)
                    ds = p * (_dot_nt(dos, vb) - delta)
                    ds_ref[pl.ds(s_ * HS, HS)] += ds.reshape(HS, BLK, 2 * BLK)
                    ds_b = (ds * (HD ** -0.5)).astype(BF16)
                    _st(dq_ref, bs, s_, _head_unstack(_dot(ds_b, kb)))
                    dkb = _dot_tn(ds_b, qs)
                    dvb = _dot_tn(p.astype(BF16), dos)
                    _st(dk_ref, bs, s_, ck_ref[i, :, cs] + dkb[:BLK])
                    _st(dv_ref, bs, s_, cv_ref[i, :, cs] + dvb[:BLK])
                    ck_ref[i, :, cs] = dkb[BLK:]
                    cv_ref[i, :, cs] = dvb[BLK:]

        @pl.when(n == nbq)
        def _():
            for s_ in range(sps):
                for i, bs in enumerate(subs):
                    _st(dk_ref, bs, s_, ck_ref[i, :, s_ * SLAB:(s_ + 1) * SLAB])
                    _st(dv_ref, bs, s_, cv_ref[i, :, s_ * SLAB:(s_ + 1) * SLAB])

    out = _sds(shape(AW))
    nsg = N_SLAB // sps
    (dq, dk, dv, ds_acc), rode = _call_with_ride(
        body, ride, lambda: (pl.program_id(0) == 0) & (pl.program_id(1) == 0),
        lambda: (pl.program_id(0) == nsg - 1) & (pl.program_id(1) == nbq),
        name=f"attn_bwd{g}", grid=(nsg, nbq + 1),
        in_specs=sum(in_specs, []) + [pl.BlockSpec((sps * HS, BLK, 2 * BLK), lambda sg, n: (g * nsg + sg, 0, 0))],
        out_specs=[ospec(cur), ospec(late), ospec(late),
                   pl.BlockSpec((sps * HS, BLK, 2 * BLK), lambda sg, n: (sg, 0, 0))],
        out_shape=[out] * 3 + [_sds((NH, BLK, 2 * BLK))],
        scratch_shapes=[pltpu.VMEM((len(subs), BLK, sps * SLAB), F32), pltpu.VMEM((len(subs), BLK, sps * SLAB), F32)],
        compiler_params=_params("arbitrary", "arbitrary"),
    )(*([qkv_g] * (5 * nl)), *([dattn] * nl), stats, bias_tab)
    return [dq.reshape(S, AW), dk.reshape(S, AW), dv.reshape(S, AW)], ds_acc, rode


TM_MIX = 256


def _mix_specs(tm):
    row512 = pl.BlockSpec((tm, AW), lambda i: (i, 0))
    return ([row512] * 6 + [
        pl.BlockSpec((tm, REST_W), lambda i: (i, 0)),
        pl.BlockSpec((HALO, AW), lambda i: (jnp.maximum(i * (tm // HALO) - 1, 0), 1)),
        pl.BlockSpec((AW, D), lambda i: (0, 0)), pl.BlockSpec((AW, D), lambda i: (0, 0)),
        pl.BlockSpec((4, PGW, PGW), lambda i: (0, 0, 0)), pl.BlockSpec((1, AW), lambda i: (0, 0))])


def _mix_forward(i, tm, o_refs, l_refs, rest_ref, halo_ref, wab_ref, wpb_ref, pw_ref, ps_ref):
    l0, l1, l2 = (r[...] for r in l_refs)
    mx = jnp.maximum(jnp.maximum(l0, l1), l2)
    e0, e1, e2 = jnp.exp(l0 - mx), jnp.exp(l1 - mx), jnp.exp(l2 - mx)
    den = e0 + e1 + e2
    lj = mx + jnp.log(den)
    attn = (e0 * o_refs[0][...] + e1 * o_refs[1][...] + e2 * o_refs[2][...]) / den

    z_attn = rest_ref[:, 0:AW]
    u = rest_ref[:, AW:2 * AW]
    z_pool = rest_ref[:, 2 * AW:3 * AW]
    g_attn = rest_ref[:, 3 * AW:3 * AW + D]
    g_pool = rest_ref[:, 3 * AW + D:3 * AW + 2 * D]

    sg_a = _sigmoid(z_attn)
    sil_a = z_attn * sg_a
    a_g = (attn * sil_a).astype(BF16)
    y_attn = _dot(a_g, wab_ref[...])

    halo = jnp.where(i > 0, halo_ref[...], 0.0)
    ext = jnp.concatenate([halo, u], axis=0)
    t = i * tm + lax.broadcasted_iota(jnp.int32, (tm, 1), 0)
    pooled, mixed_raw = [], []
    for gi, win in enumerate(POOL_WINDOWS):
        s = ext[:, gi * PGW:(gi + 1) * PGW]
        sh = 1
        while sh < win:
            s = s + pltpu.roll(s, sh, 0)
            sh *= 2
        cnt = jnp.minimum(t + 1, win).astype(F32)
        pg = s[HALO:] / cnt - u[:, gi * PGW:(gi + 1) * PGW]
        pooled.append(pg.astype(BF16))
        mixed_raw.append(_dot(pooled[-1], pw_ref[gi].astype(BF16)))
    mixed_raw = jnp.concatenate(mixed_raw, axis=1)
    mixed = mixed_raw * ps_ref[...]
    sg_p = _sigmoid(z_pool)
    sil_p = z_pool * sg_p
    m_g = (mixed * sil_p).astype(BF16)
    y_pool = _dot(m_g, wpb_ref[...])

    sa = _sigmoid(g_attn)
    sp = _sigmoid(g_pool)
    merged = sa * y_attn + sp * y_pool
    return dict(lj=lj, attn=attn, z_attn=z_attn, z_pool=z_pool, sg_a=sg_a, sil_a=sil_a, a_g=a_g, y_attn=y_attn,
                pooled=pooled, mixed_raw=mixed_raw, mixed=mixed, sg_p=sg_p, sil_p=sil_p, m_g=m_g, y_pool=y_pool,
                sa=sa, sp=sp, merged=merged)


def _tail(x, target, os_, ls_, rest, wab, wpb, pool_w, pool_scale, wout, mod, final_g):
    S = x.shape[0]
    tm = TM_MIX

    def body(o0, o1, o2, l0, l1, l2, rest_ref, halo_ref, wab_ref, wpb_ref, pw_ref, ps_ref,
             x_ref, t_ref, wo_ref, mod_ref, fg_ref, dx2_ref, dmo_ref, loss_ref, dfg_ref, dgate_ref):
        i = pl.program_id(0)

        @pl.when(i == 0)
        def _():
            loss_ref[...] = jnp.zeros_like(loss_ref)
            dfg_ref[...] = jnp.zeros_like(dfg_ref)
            dgate_ref[...] = jnp.zeros_like(dgate_ref)

        f = _mix_forward(i, tm, (o0, o1, o2), (l0, l1, l2), rest_ref, halo_ref, wab_ref, wpb_ref, pw_ref, ps_ref)
        mo = _dot(f["merged"].astype(BF16), wo_ref[...])
        gate = mod_ref[:, 2 * D:3 * D]
        fg = fg_ref[...]
        x2 = x_ref[...] + gate * mo
        r2 = lax.rsqrt(jnp.mean(x2 * x2, axis=-1, keepdims=True) + EPS)
        n2 = x2 * r2
        err = n2 * fg - t_ref[...]
        loss_ref[...] += 0.5 * jnp.sum(jnp.mean(err * err, axis=-1, keepdims=True))
        dy = err * (1.0 / D)
        dfg_ref[...] += jnp.sum(dy * n2, axis=0, keepdims=True)
        dn = dy * fg
        dx2 = r2 * (dn - n2 * jnp.mean(dn * n2, axis=-1, keepdims=True))
        dgate_ref[...] += jnp.sum(dx2 * mo, axis=0, keepdims=True)
        dx2_ref[...] = dx2
        dmo_ref[...] = (dx2 * gate).astype(BF16)

    row = pl.BlockSpec((tm, D), lambda i: (i, 0))
    vec = pl.BlockSpec((1, D), lambda i: (0, 0))
    return pl.pallas_call(
        body, name="tail", grid=(S // tm,),
        in_specs=_mix_specs(tm) + [row, row, pl.BlockSpec((D, D), lambda i: (0, 0)),
                                   pl.BlockSpec((1, 3 * D), lambda i: (0, 0)), vec],
        out_specs=[row, row, pl.BlockSpec((8, 128), lambda i: (0, 0)), vec, vec],
        out_shape=[_sds((S, D)), _sds((S, D), BF16), _sds((8, 128)), _sds((1, D)), _sds((1, D))],
        compiler_params=_params("arbitrary"),
    )(*os_, *ls_, rest, rest, wab, wpb, pool_w, pool_scale, x, target, wout, mod, final_g)


def _mix_bwd(dmo, os_, ls_, rest, wab, wpb, pool_w, pool_scale, wout):
    S = dmo.shape[0]
    tm = TM_MIX
    nt = S // tm
    sw = D // N_SHARD

    def body(o0, o1, o2, l0, l1, l2, rest_ref, halo_ref, wab_ref, wpb_ref, pw_ref, ps_ref, dmo_ref, wo_ref,
             dattn_ref, stats_ref, dpooled_ref, drest_ref, dwo_hbm, dwab_hbm, dwpb_hbm, dpw_ref, dps_ref,
             awo, awab, awpb):
        i = pl.program_id(0)

        @pl.when(i == 0)
        def _():
            awo[...] = jnp.zeros_like(awo)
            awab[...] = jnp.zeros_like(awab)
            awpb[...] = jnp.zeros_like(awpb)
            dpw_ref[...] = jnp.zeros_like(dpw_ref)
            dps_ref[...] = jnp.zeros_like(dps_ref)

        f = _mix_forward(i, tm, (o0, o1, o2), (l0, l1, l2), rest_ref, halo_ref, wab_ref, wpb_ref, pw_ref, ps_ref)
        dmo_b = dmo_ref[...]
        dmerged = _dot_nt(dmo_b, wo_ref[...])
        awo[...] += _dot_tn(f["merged"].astype(BF16), dmo_b)
        sa, sp = f["sa"], f["sp"]
        dya = (dmerged * sa).astype(BF16)
        dyp = (dmerged * sp).astype(BF16)
        dg_attn = dmerged * f["y_attn"] * sa * (1.0 - sa)
        dg_pool = dmerged * f["y_pool"] * sp * (1.0 - sp)
        dag = _dot_nt(dya, wab_ref[...])
        awab[...] += _dot_tn(f["a_g"], dya)
        dmg = _dot_nt(dyp, wpb_ref[...])
        awpb[...] += _dot_tn(f["m_g"], dyp)
        dattn = dag * f["sil_a"]
        dattn_ref[...] = dattn
        prod = dattn * f["attn"]
        lane = lax.broadcasted_iota(jnp.int32, (tm, STAT_W), 1)
        for sb in range(N_SLAB):
            st = jnp.zeros((tm, STAT_W), F32)
            for h in range(HS):
                hs = slice((sb * HS + h) * HD, (sb * HS + h + 1) * HD)
                st = jnp.where(lane == h, f["lj"][:, hs.start:hs.start + 1], st)
                st = jnp.where(lane == HS + h, jnp.sum(prod[:, hs], axis=-1, keepdims=True), st)
            stats_ref[:, sb * STAT_W:(sb + 1) * STAT_W] = st
        dz_attn = dag * f["attn"] * (f["sg_a"] * (1.0 + f["z_attn"] * (1.0 - f["sg_a"])))
        dmixed = dmg * f["sil_p"]
        dz_pool = dmg * f["mixed"] * (f["sg_p"] * (1.0 + f["z_pool"] * (1.0 - f["sg_p"])))
        dps_ref[...] += jnp.sum(dmixed * f["mixed_raw"], axis=0, keepdims=True)
        dpm = (dmixed * ps_ref[...]).astype(BF16)
        for gi in range(len(POOL_WINDOWS)):
            cs = slice(gi * PGW, (gi + 1) * PGW)
            dpw_ref[gi] += _dot_tn(f["pooled"][gi], dpm[:, cs])
            dpooled_ref[:, cs] = _dot_nt(dpm[:, cs], pw_ref[gi].astype(BF16))
        drest_ref[:, 0:AW] = dz_attn.astype(BF16)
        drest_ref[:, AW:2 * AW] = jnp.zeros((tm, AW), BF16)
        drest_ref[:, 2 * AW:3 * AW] = dz_pool.astype(BF16)
        drest_ref[:, 3 * AW:3 * AW + D] = dg_attn.astype(BF16)
        drest_ref[:, 3 * AW + D:3 * AW + 2 * D] = dg_pool.astype(BF16)

        @pl.when(i == nt - 1)
        def _():
            pltpu.sync_copy(awo, dwo_hbm)
            for k in range(N_SHARD):
                pltpu.sync_copy(awab.at[:, pl.ds(k * sw, sw)], dwab_hbm.at[k])
                pltpu.sync_copy(awpb.at[:, pl.ds(k * sw, sw)], dwpb_hbm.at[k])

    row512 = pl.BlockSpec((tm, AW), lambda i: (i, 0))
    outs = pl.pallas_call(
        body, name="mix_bwd", grid=(nt,),
        in_specs=_mix_specs(tm) + [pl.BlockSpec((tm, D), lambda i: (i, 0)), pl.BlockSpec((D, D), lambda i: (0, 0))],
        out_specs=[row512, pl.BlockSpec((tm, N_SLAB * STAT_W), lambda i: (i, 0)), row512,
                   pl.BlockSpec((tm, REST_W), lambda i: (i, 0)), ANY, ANY, ANY,
                   pl.BlockSpec((4, PGW, PGW), lambda i: (0, 0, 0)), pl.BlockSpec((1, AW), lambda i: (0, 0))],
        out_shape=[_sds((S, AW)), _sds((S, N_SLAB * STAT_W)), _sds((S, AW)), _sds((S, REST_W), BF16),
                   _sds((D, D)), _sds((N_SHARD, AW, sw)), _sds((N_SHARD, AW, sw)), _sds((4, PGW, PGW)), _sds((1, AW))],
        scratch_shapes=[pltpu.VMEM((D, D), F32), pltpu.VMEM((AW, D), F32), pltpu.VMEM((AW, D), F32)],
        compiler_params=_params("arbitrary"),
    )(*os_, *ls_, rest, rest, wab, wpb, pool_w, pool_scale, dmo, wout)
    dattn, stats, dpooled, drest, dwo, dwab, dwpb, dpw, dps = outs
    return dattn, stats, dpooled, drest, dwo.reshape(N_SHARD, D // N_SHARD, D), dwab, dwpb, dpw, dps


def _pool_bwd(dpooled):
    S = dpooled.shape[0]
    tm = 512
    nt = S // tm

    def body(dp_ref, nxt_ref, du_ref):
        i = pl.program_id(0)
        t = i * tm + lax.broadcasted_iota(jnp.int32, (tm + HALO, 1), 0)
        nxt = jnp.where(i < nt - 1, nxt_ref[...], 0.0)
        ext = jnp.concatenate([dp_ref[...], nxt], axis=0)
        for gi, win in enumerate(POOL_WINDOWS):
            cs = slice(gi * PGW, (gi + 1) * PGW)
            s = ext[:, cs] / jnp.minimum(t + 1, win).astype(F32)
            sh = 1
            while sh < win:
                s = s + pltpu.roll(s, tm + HALO - sh, 0)
                sh *= 2
            du_ref[:, cs] = (s[:tm] - dp_ref[:, cs]).astype(BF16)

    return pl.pallas_call(
        body, name="pool_bwd", grid=(nt,),
        in_specs=[pl.BlockSpec((tm, AW), lambda i: (i, 0)),
                  pl.BlockSpec((HALO, AW), lambda i: (jnp.minimum((i + 1) * (tm // HALO), S // HALO - 1), 0))],
        out_specs=pl.BlockSpec((tm, AW), lambda i: (i, 0)),
        out_shape=_sds((S, AW), BF16), compiler_params=_params("parallel"),
    )(dpooled, dpooled)


TB = 1024


def _dh(dproj, wg_in, ride):
    S = dproj.shape[0]
    per = wg_in.shape[2] // TB
    nm, nk = S // TB, IN_W // TB

    def body(dp_ref, w_ref, out_ref):
        @pl.when(pl.program_id(1) == 0)
        def _():
            out_ref[...] = jnp.zeros_like(out_ref)

        out_ref[...] += _dot_nt(dp_ref[...], w_ref[...])

    (dh,), rode = _call_with_ride(
        body, ride, lambda: (pl.program_id(0) == 0) & (pl.program_id(1) == 0),
        lambda: (pl.program_id(0) == nm - 1) & (pl.program_id(1) == nk - 1),
        name="dh", grid=(nm, nk),
        in_specs=[pl.BlockSpec((TB, TB), lambda m, kk: (m, kk)),
                  pl.BlockSpec((None, D, TB), lambda m, kk: (kk // per, 0, kk % per))],
        out_specs=[pl.BlockSpec((TB, D), lambda m, kk: (m, 0))],
        out_shape=[_sds((S, D))], compiler_params=_params("arbitrary", "arbitrary"),
    )(dproj, wg_in)
    return dh, rode


def _dw_in(h_t, dproj):
    S = dproj.shape[0]
    per = IN_W // N_SHARD // TB

    def body(ht_ref, dp_ref, out_ref):
        @pl.when(pl.program_id(1) == 0)
        def _():
            out_ref[...] = jnp.zeros_like(out_ref)

        out_ref[...] += _dot(ht_ref[...], dp_ref[...])

    return pl.pallas_call(
        body, name="dw_in", grid=(IN_W // TB, S // TB),
        in_specs=[pl.BlockSpec((D, TB), lambda j, kk: (0, kk)), pl.BlockSpec((TB, TB), lambda j, kk: (kk, j))],
        out_specs=pl.BlockSpec((None, D, TB), lambda j, kk: (j // per, 0, j % per)),
        out_shape=_sds((N_SHARD, D, IN_W // N_SHARD)), compiler_params=_params("parallel", "arbitrary"),
    )(h_t, dproj)


def _prenorm_bwd(x, dh, dx2, norm_g, mod):
    S = x.shape[0]
    tm = 512

    def body(x_ref, dh_ref, dx2_ref, g_ref, mod_ref, gx_ref, dg_ref, dshift_ref, dscale_ref):
        i = pl.program_id(0)

        @pl.when(i == 0)
        def _():
            dg_ref[...] = jnp.zeros_like(dg_ref)
            dshift_ref[...] = jnp.zeros_like(dshift_ref)
            dscale_ref[...] = jnp.zeros_like(dscale_ref)

        xv = x_ref[...]
        dhv = dh_ref[...]
        g = g_ref[...]
        r = lax.rsqrt(jnp.mean(xv * xv, axis=-1, keepdims=True) + EPS)
        xh = xv * r
        dshift_ref[...] += jnp.sum(dhv, axis=0, keepdims=True)
        dscale_ref[...] += jnp.sum(dhv * (xh * g), axis=0, keepdims=True)
        dn1 = dhv * (1.0 + mod_ref[:, D:2 * D])
        dg_ref[...] += jnp.sum(dn1 * xh, axis=0, keepdims=True)
        dxh = dn1 * g
        gx_ref[...] = dx2_ref[...] + r * (dxh - xh * jnp.mean(dxh * xh, axis=-1, keepdims=True))

    row = pl.BlockSpec((tm, D), lambda i: (i, 0))
    vec = pl.BlockSpec((1, D), lambda i: (0, 0))
    return pl.pallas_call(
        body, name="prenorm_bwd", grid=(S // tm,),
        in_specs=[row, row, row, vec, pl.BlockSpec((1, 3 * D), lambda i: (0, 0))],
        out_specs=[row, vec, vec, vec],
        out_shape=[_sds((S, D)), _sds((1, D)), _sds((1, D)), _sds((1, D))],
        compiler_params=_params("arbitrary"),
    )(x, dh, dx2, norm_g, mod)


def _local_step(x, target, mod, wg_in, wab, wpb, wout, pool_w, pool_scale, rel_bias, norm_g, final_g, half_idx,
                chip_half):
    buckets = jnp.asarray(_bucket_tables())
    bias_tab = _bias_table(rel_bias, buckets)
    h = _prenorm(x, norm_g, mod)
    qkv = [_proj(h, wg_in, 3 * g, 3, F32, f"proj_qkv{g}") for g in range(NG)]
    rest = _proj(h, wg_in, NCB_QKV, REST_W // CB, F32, "proj_rest")
    os_, ls_ = zip(*[_attn_fwd(qkv[g], bias_tab, g) for g in range(NG)])
    dx2, dmo, loss, dfinal_g, dgate = _tail(x, target, os_, ls_, rest, wab, wpb, pool_w, pool_scale, wout, mod, final_g)
    dattn, stats, dpooled, drest, dw_out, dw_ab, dw_pb, dpool_w, dpool_scale = _mix_bwd(
        dmo, os_, ls_, rest, wab, wpb, pool_w, pool_scale, wout)
    du = _pool_bwd(dpooled)

    small = [dw_ab, dw_pb, dw_out]
    dqkv0, ds0, sib_small = _attn_bwd(qkv[0], dattn, stats, bias_tab, 0, _ride_sibling_halves(small))
    p_small = [_pair_sum(g, t, half_idx, f"rs_pair_sum{a}") for a, (g, t) in enumerate(zip(small, sib_small))]
    dqkv1, ds1, u_small = _attn_bwd(qkv[1], dattn, stats, bias_tab, 1,
                                    _ride_chip_exchange([p16 for _, p16 in p_small]))
    rs_ab, rs_pb, rs_out = [_chip_sum(p32, u, chip_half, f"rs_chip_sum{a}")
                            for a, ((p32, _), u) in enumerate(zip(p_small, u_small))]
    dqkv2, ds2, _ = _attn_bwd(qkv[2], dattn, stats, bias_tab, 2, None)

    dproj = jnp.concatenate([a.astype(BF16) for a in dqkv0 + dqkv1 + dqkv2] + [drest[:, :AW], du, drest[:, 2 * AW:]],
                            axis=1)
    dw_in = _dw_in(h.T, dproj)
    drel_rows, (sib_in,) = _bias_grad(jnp.concatenate([ds0, ds1, ds2], axis=0), buckets,
                                      _ride_sibling_halves([dw_in]))
    drel = drel_rows[:, 0, :NUM_BUCKETS].T
    p32_in, p16_in = _pair_sum(dw_in, sib_in, half_idx, "rs_pair_sum_in")
    dh, (u_in,) = _dh(dproj, wg_in, _ride_chip_exchange([p16_in]))
    rs_in = _chip_sum(p32_in, u_in, chip_half, "rs_chip_sum_in")

    grad_x, dnorm_g, dshift, dscale = _prenorm_bwd(x, dh, dx2, norm_g, mod)
    dmod = jnp.concatenate([dshift, dscale, dgate], axis=1)
    return dict(loss=loss[0, 0], grad_x=grad_x, dmod=dmod, dnorm_g=dnorm_g, dfinal_g=dfinal_g, dpool_w=dpool_w,
                dpool_scale=dpool_scale, drel_bias=drel, dw_in=dw_in, dw_attn_br=dw_ab, dw_pool_br=dw_pb,
                dw_out=dw_out, rs_in=rs_in, rs_attn_br=rs_ab, rs_pool_br=rs_pb, rs_out=rs_out)


def _allgather8(blocks, name, relay=None):
    nb = len(blocks)
    relay = [False] * nb if relay is None else list(relay)

    def body(*refs):
        ins, outs = refs[:nb], refs[nb:2 * nb]
        send_sems, recv_sems, local_sems = refs[2 * nb:]
        x, y, c = lax.axis_index("x"), lax.axis_index("y"), lax.axis_index("c")
        me, sibling = (x, y, c), (x, y, 1 - c)
        here, xn, yn, dg = (x, y), (1 - x, y), (x, 1 - y), (1 - x, 1 - y)

        def slot(a, chip, core, half=None):
            ref = outs[a].at[4 * chip[0] + 2 * chip[1] + core]
            if half is None:
                return ref
            r2 = ref.shape[0] // 2
            return ref.at[pl.ds(half * r2, r2)]

        def copy(a, k, dst, to, src=None):
            return pltpu.make_async_remote_copy(src_ref=dst if src is None else src, dst_ref=dst,
                                                send_sem=send_sems.at[a, k], recv_sem=recv_sems.at[a, k],
                                                device_id=to, device_id_type=MESH)

        def start(cps):
            for cp in cps:
                cp.start()
            return cps

        mine = start([pltpu.make_async_copy(ins[a], slot(a, here, c), local_sems.at[a]) for a in range(nb)])
        sent = []
        for a in range(nb):
            own = slot(a, here, c)
            sent += [copy(a, 0, own, sibling, src=ins[a]), copy(a, 1, own, (*xn, c), src=ins[a]),
                     copy(a, 2, own, (*yn, c), src=ins[a])]
            if not relay[a]:
                sent.append(copy(a, 3, own, (*dg, c), src=ins[a]))
        start(sent)
        for a in range(nb):
            copy(a, 2, slot(a, yn, c), me).wait_recv()
            sent += start([copy(a, 6, slot(a, yn, c), sibling)]
                          + ([copy(a, 3, slot(a, yn, c, 0), (*xn, c))] if relay[a] else []))
        for a in range(nb):
            copy(a, 1, slot(a, xn, c), me).wait_recv()
            sent += start([copy(a, 5, slot(a, xn, c), sibling)]
                          + ([copy(a, 4, slot(a, xn, c, 1), (*yn, c))] if relay[a] else []))
        for a in range(nb):
            for k, half in ((3, 0), (4, 1)) if relay[a] else ((3, None),):
                copy(a, k, slot(a, dg, c, half), me).wait_recv()
                sent += start([copy(a, 4 + k, slot(a, dg, c, half), sibling)])
        for a in range(nb):
            copy(a, 0, slot(a, here, 1 - c), me).wait_recv()
            copy(a, 5, slot(a, xn, 1 - c), me).wait_recv()
            copy(a, 6, slot(a, yn, 1 - c), me).wait_recv()
            for k, half in ((7, 0), (8, 1)) if relay[a] else ((7, None),):
                copy(a, k, slot(a, dg, 1 - c, half), me).wait_recv()
        for cp in sent:
            cp.wait_send()
        for cp in mine:
            cp.wait()

    return pl.pallas_call(
        body, name=name, in_specs=[ANY] * nb, out_specs=[ANY] * nb,
        out_shape=[_sds((8,) + b.shape, b.dtype) for b in blocks],
        scratch_shapes=[_dma_sems(nb, 9), _dma_sems(nb, 9), _dma_sems(nb)],
    )(*blocks)


def _ride_sibling_halves(gs):
    def copies(ins, outs, send_sems, recv_sems):
        x, y, c = lax.axis_index("x"), lax.axis_index("y"), lax.axis_index("c")
        cps = []
        for a in range(len(gs)):
            r2 = ins[a].shape[1] // 2
            other = ins[a].at[:, pl.ds((1 - c) * r2, r2), :]
            cps.append(pltpu.make_async_remote_copy(src_ref=other, dst_ref=outs[a], send_sem=send_sems.at[a],
                                                    recv_sem=recv_sems.at[a], device_id=(x, y, 1 - c),
                                                    device_id_type=MESH))
        return cps

    return _Ride(gs, [_sds((g.shape[0], g.shape[1] // 2, g.shape[2]), g.dtype) for g in gs], len(gs), copies)


def _pair_sum(g, t, half, name):
    nsh, rows, cols = g.shape
    r2 = rows // 2
    tr = _row_tile(r2, cols)
    nt = r2 // tr

    def body(half_ref, g_ref, t_ref, p32_ref, p16_ref):
        p = g_ref[...] + t_ref[...]
        p32_ref[...] = p
        p16_ref[...] = p.astype(BF16)

    blk = pl.BlockSpec((None, tr, cols), lambda k, i, half_ref: (k, i, 0))
    return pl.pallas_call(
        body, name=name,
        grid_spec=pltpu.PrefetchScalarGridSpec(
            num_scalar_prefetch=1, grid=(nsh, nt),
            in_specs=[pl.BlockSpec((None, tr, cols), lambda k, i, half_ref: (k, half_ref[0] * nt + i, 0)), blk],
            out_specs=[blk, blk]),
        out_shape=[_sds((nsh, r2, cols)), _sds((nsh, r2, cols), BF16)],
        compiler_params=_params("parallel", "parallel"),
    )(half, g, t)


def _ride_chip_exchange(ps):
    def copies(ins, outs, send_sems, recv_sems):
        x, y, c = lax.axis_index("x"), lax.axis_index("y"), lax.axis_index("c")
        chips = [(1 - x, y), (x, 1 - y), (1 - x, 1 - y)]
        cps = []
        for a in range(len(ps)):
            for j, (ox, oy) in enumerate(chips):
                cps.append(pltpu.make_async_remote_copy(src_ref=ins[a].at[2 * ox + oy], dst_ref=outs[a].at[j],
                                                        send_sem=send_sems.at[3 * a + j],
                                                        recv_sem=recv_sems.at[3 * a + j],
                                                        device_id=(ox, oy, c), device_id_type=MESH))
        return cps

    return _Ride(ps, [_sds((3,) + p.shape[1:], p.dtype) for p in ps], 3 * len(ps), copies)


def _chip_sum(p32, u, chip_half, name):
    r2, cols = p32.shape[1:]
    tr = _row_tile(r2, cols)
    nt = r2 // tr

    def body(ch_ref, p_ref, u_ref, o_ref):
        acc = p_ref[...]
        for j in range(3):
            acc = acc + u_ref[j].astype(F32)
        o_ref[...] = acc

    return pl.pallas_call(
        body, name=name,
        grid_spec=pltpu.PrefetchScalarGridSpec(
            num_scalar_prefetch=1, grid=(nt,),
            in_specs=[pl.BlockSpec((None, tr, cols), lambda i, ch_ref: (ch_ref[0], i, 0)),
                      pl.BlockSpec((3, tr, cols), lambda i, ch_ref: (0, i, 0))],
            out_specs=pl.BlockSpec((tr, cols), lambda i, ch_ref: (ch_ref[1] * nt + i, 0))),
        out_shape=_sds((2 * r2, cols)), compiler_params=_params("parallel"),
    )(chip_half, p32, u)


def _sibling_join(fs, name):
    nb = len(fs)

    def body(*refs):
        outs = refs[nb:2 * nb]
        send_sems, recv_sems = refs[2 * nb:]
        x, y, c = lax.axis_index("x"), lax.axis_index("y"), lax.axis_index("c")
        cps = []
        for a in range(nb):
            r2 = outs[a].shape[0] // 2
            rows = outs[a].at[pl.ds(c * r2, r2), :]
            cps.append(pltpu.make_async_remote_copy(src_ref=rows, dst_ref=rows, send_sem=send_sems.at[a],
                                                    recv_sem=recv_sems.at[a], device_id=(x, y, 1 - c),
                                                    device_id_type=MESH))
        for cp in cps:
            cp.start()
        for cp in cps:
            cp.wait()

    return pl.pallas_call(
        body, name=name, in_specs=[ANY] * nb, out_specs=[ANY] * nb,
        out_shape=[_sds(f.shape, f.dtype) for f in fs],
        input_output_aliases={a: a for a in range(nb)},
        scratch_shapes=[_dma_sems(nb), _dma_sems(nb)],
    )(*fs)


def _row_tile(rows, cols):
    tile = rows
    while tile * cols * 4 > (1 << 20) and tile % 16 == 0:
        tile //= 2
    return tile


def _sum_leading(a, name):
    k = a.shape[0]
    a3 = a.reshape(k, -1, a.shape[-1])
    rows, cols = a3.shape[1:]
    tr = _row_tile(rows, cols)

    def body(a_ref, o_ref):
        acc = a_ref[0]
        for s in range(1, k):
            acc = acc + a_ref[s]
        o_ref[...] = acc

    out = pl.pallas_call(
        body, name=name, grid=(rows // tr,),
        in_specs=[pl.BlockSpec((k, tr, cols), lambda i: (0, i, 0))],
        out_specs=pl.BlockSpec((tr, cols), lambda i: (i, 0)),
        out_shape=_sds((rows, cols), a.dtype), compiler_params=_params("parallel"),
    )(a3)
    return out.reshape(a.shape[1:])


def _w_ada_grad(c_all, dmod_cols):
    def body(c_ref, d_ref, o_ref):
        o_ref[...] = _dot_tn(c_ref[...].astype(BF16), d_ref[...].astype(BF16))

    return pl.pallas_call(body, name="w_ada_grad", out_shape=_sds((c_all.shape[1], dmod_cols.shape[1])),
                          compiler_params=_params())(c_all, dmod_cols)


def _adamw(w, g, m, v, name):
    rows, cols = w.shape
    tr = _row_tile(rows, cols)

    def body(w_ref, g_ref, m_ref, v_ref, d_ref, nm_ref, nv_ref):
        gv = g_ref[...]
        nm = ADAM_B1 * m_ref[...] + (1.0 - ADAM_B1) * gv
        nv = ADAM_B2 * v_ref[...] + (1.0 - ADAM_B2) * (gv * gv)
        m_hat = nm / (1.0 - ADAM_B1 ** ADAM_STEP)
        v_hat = nv / (1.0 - ADAM_B2 ** ADAM_STEP)
        d_ref[...] = -ADAM_LR * (m_hat / (jnp.sqrt(v_hat) + ADAM_EPS) + ADAM_WD * w_ref[...])
        nm_ref[...] = nm
        nv_ref[...] = nv

    spec = pl.BlockSpec((tr, cols), lambda i: (i, 0))
    return pl.pallas_call(
        body, name=name, grid=(rows // tr,), in_specs=[spec] * 4, out_specs=[spec] * 3,
        out_shape=[_sds((rows, cols))] * 3, compiler_params=_params("parallel"),
    )(w, g, m, v)


def _pack_small(b_ada, norm_g, final_g, pool_scale, rel_bias, loss_row, pool_w):
    pad = jnp.zeros((PK_POOLW - PK_LOSS - 1) * 128, F32)
    flat = jnp.concatenate([b_ada.reshape(-1), norm_g.reshape(-1), final_g.reshape(-1), pool_scale.reshape(-1),
                            rel_bias.reshape(-1), loss_row.reshape(-1), pad, pool_w.reshape(-1)])
    return flat.reshape(PK_ROWS, 128)


def _unpack_small(p):
    def take(r0, r1, shape):
        return p[r0:r1].reshape(shape)

    return dict(b_ada=take(PK_BADA, PK_NORMG, (1, 3 * D)), norm_g=take(PK_NORMG, PK_FINALG, (1, D)),
                final_g=take(PK_FINALG, PK_PSCALE, (D,)), pool_scale=take(PK_PSCALE, PK_RELB, (1, AW)),
                rel_bias=take(PK_RELB, PK_LOSS, (NUM_BUCKETS, NG * NH)), loss=p[PK_LOSS, 0],
                pool_w=take(PK_POOLW, PK_ROWS, (1, 4, PGW, PGW)))


def kernel(x, c, norm_g, w_ada, b_ada, w_in, pool_w, pool_scale, w_attn_br, w_pool_br, w_out, rel_bias, final_g, loss_target, m_norm_g, m_w_ada, m_b_ada, m_w_in, m_pool_w, m_pool_scale, m_w_attn_br, m_w_pool_br, m_w_out, m_rel_bias, m_final_g, v_norm_g, v_w_ada, v_b_ada, v_w_in, v_pool_w, v_pool_scale, v_w_attn_br, v_w_pool_br, v_w_out, v_rel_bias, v_final_g):
    ix, iy, ic = lax.axis_index("x"), lax.axis_index("y"), lax.axis_index("c")
    dev = 4 * ix + 2 * iy + ic
    chip = 2 * ix + iy

    def half(w):
        r2 = w.shape[0] // 2
        return lax.dynamic_slice_in_dim(w, ic * r2, r2, axis=0).astype(BF16)

    gathered = _allgather8([jnp.broadcast_to(c, (8, D)), half(w_in[0]), half(w_attn_br[0]), half(w_pool_br[0]),
                            half(w_out[0])], "gather_weights", relay=[False, True, True, True, True])
    c_all = gathered[0][:, 0, :]
    wg_in = gathered[1].reshape(N_SHARD, D, IN_W // N_SHARD)
    wab = gathered[2].reshape(N_SHARD, AW, D // N_SHARD).transpose(1, 0, 2).reshape(AW, D)
    wpb = gathered[3].reshape(N_SHARD, AW, D // N_SHARD).transpose(1, 0, 2).reshape(AW, D)
    wout = gathered[4].reshape(D, D)

    mw = 3 * D // N_SHARD
    modp = _mod_partial(c_all, w_ada[0], lax.dynamic_slice_in_dim(b_ada, chip * mw, mw, axis=1))
    mod_all = _allgather8([modp], "gather_mod")[0]
    mod_full = mod_all[::2].transpose(1, 0, 2).reshape(8, 3 * D)
    mod = lax.dynamic_slice_in_dim(mod_full, dev, 1, axis=0)

    half_idx = jnp.stack([ic]).astype(jnp.int32)
    chip_half = jnp.stack([chip, ic]).astype(jnp.int32)
    r = _local_step(x[0], loss_target[0], mod, wg_in, wab, wpb, wout, pool_w[0], pool_scale, rel_bias, norm_g,
                    final_g.reshape(1, D), half_idx, chip_half)

    packed = _pack_small(r["dmod"], r["dnorm_g"], r["dfinal_g"], r["dpool_scale"], r["drel_bias"],
                         jnp.full((128,), r["loss"], F32), r["dpool_w"])
    small_all = _allgather8([packed], "gather_small")[0]
    small_sum = _sum_leading(small_all, "sum_small")
    dmod_all = small_all[:, PK_BADA:PK_NORMG, :].reshape(8, 3 * D)
    g_w_ada = _w_ada_grad(c_all, lax.dynamic_slice_in_dim(dmod_all, chip * mw, mw, axis=1))

    g_w_in, g_w_ab, g_w_pb, g_w_out = _sibling_join([r["rs_in"], r["rs_attn_br"], r["rs_pool_br"], r["rs_out"]],
                                                    "rs_sibling_join")

    small_w = _pack_small(b_ada, norm_g, final_g, pool_scale, rel_bias, jnp.zeros((128,), F32), pool_w)
    small_m = _pack_small(m_b_ada, m_norm_g, m_final_g, m_pool_scale, m_rel_bias, jnp.zeros((128,), F32), m_pool_w)
    small_v = _pack_small(v_b_ada, v_norm_g, v_final_g, v_pool_scale, v_rel_bias, jnp.ones((128,), F32), v_pool_w)
    sd, sm, sv = (_unpack_small(p) for p in _adamw(small_w, small_sum, small_m, small_v, "adamw_small"))
    sg = _unpack_small(small_sum)
    upd = {
        "w_ada": (g_w_ada,) + tuple(_adamw(w_ada[0], g_w_ada, m_w_ada[0], v_w_ada[0], "adamw_w_ada")),
        "w_in": (g_w_in,) + tuple(_adamw(w_in[0], g_w_in, m_w_in[0], v_w_in[0], "adamw_w_in")),
        "w_attn_br": (g_w_ab,) + tuple(_adamw(w_attn_br[0], g_w_ab, m_w_attn_br[0], v_w_attn_br[0], "adamw_w_ab")),
        "w_pool_br": (g_w_pb,) + tuple(_adamw(w_pool_br[0], g_w_pb, m_w_pool_br[0], v_w_pool_br[0], "adamw_w_pb")),
        "w_out": (g_w_out,) + tuple(_adamw(w_out[0], g_w_out, m_w_out[0], v_w_out[0], "adamw_w_out")),
    }
    names = ["norm_g", "w_ada", "b_ada", "w_in", "pool_w", "pool_scale", "w_attn_br", "w_pool_br", "w_out",
             "rel_bias", "final_g"]
    outs = [sg["loss"], r["grad_x"][None]]
    for kind in range(4):
        for nme in names:
            if nme in upd:
                outs.append(upd[nme][kind][None])
            else:
                outs.append((sg, sd, sm, sv)[kind][nme])
    return tuple(outs)
```

```python
import functools
import math

import numpy as np
import jax
import jax.numpy as jnp
from jax import lax
from jax.experimental import pallas as pl
from jax.experimental.pallas import tpu as pltpu

F32 = jnp.float32
BF16 = jnp.bfloat16

D = 1024
HD = 64
NH = 8
AW = NH * HD
GROUPS = ((128, 1), (512, 4), (2048, 16))
NG = len(GROUPS)
BLK = 128
GW = 3 * AW
QKV_W = NG * GW
REST_W = 3584
IN_W = QKV_W + REST_W
CB = 512
NCB = IN_W // CB
NCB_QKV = QKV_W // CB
POOL_WINDOWS = (2, 4, 8, 16)
PGW = 128
HALO = 16
NUM_BUCKETS = 32
MAX_DISTANCE = 2048
EPS = 1e-6
NEG = -1e30
N_SHARD = 4
VMEM_LIMIT = 56 * 1024 * 1024

ADAM_LR = 0.001
ADAM_B1 = 0.9
ADAM_B2 = 0.999
ADAM_EPS = 1e-08
ADAM_WD = 0.01
ADAM_STEP = 10

PK_BADA, PK_NORMG, PK_FINALG, PK_PSCALE, PK_RELB, PK_LOSS, PK_POOLW, PK_ROWS = 0, 24, 32, 40, 44, 50, 56, 568

ANY = pl.BlockSpec(memory_space=pl.ANY)
MESH = pl.DeviceIdType.MESH


def _params(*sem):
    return pltpu.CompilerParams(dimension_semantics=sem, vmem_limit_bytes=VMEM_LIMIT)


def _sds(shape, dtype=F32):
    return jax.ShapeDtypeStruct(shape, dtype)


def _dot(a, b):
    return jnp.dot(a, b, preferred_element_type=F32)


def _dot_nt(a, b):
    return lax.dot_general(a, b, (((1,), (1,)), ((), ())), preferred_element_type=F32)


def _dot_tn(a, b):
    return lax.dot_general(a, b, (((0,), (0,)), ((), ())), preferred_element_type=F32)


def _sigmoid(z):
    return 0.5 * jnp.tanh(0.5 * z) + 0.5


def _dma_sems(*shape):
    return pltpu.SemaphoreType.DMA(shape)


class _Ride:
    def __init__(self, arrays, out_shapes, n_copies, copies, in_place=False):
        self.arrays, self.out_shapes, self.n_copies, self.copies = list(arrays), list(out_shapes), n_copies, copies
        self.in_place = in_place


def _call_with_ride(body, ride, first, last, *, in_specs, out_specs, out_shape, scratch_shapes=(), **kw):
    in_specs, out_specs, out_shape, scratch_shapes = list(in_specs), list(out_specs), list(out_shape), list(scratch_shapes)
    n_in, n_out, n_sc = len(in_specs), len(out_specs), len(scratch_shapes)
    if ride is None:
        def run_plain(*operands):
            return pl.pallas_call(body, in_specs=in_specs, out_specs=out_specs, out_shape=out_shape,
                                  scratch_shapes=scratch_shapes, **kw)(*operands), []
        return run_plain
    n_ri, n_ro = len(ride.arrays), len(ride.out_shapes)

    def wrapped(*refs):
        ins, rest = refs[:n_in], refs[n_in:]
        r_ins, rest = rest[:n_ri], rest[n_ri:]
        outs, rest = rest[:n_out], rest[n_out:]
        r_outs, rest = rest[:n_ro], rest[n_ro:]
        scratch, (send_sems, recv_sems) = rest[:n_sc], rest[n_sc:]

        @pl.when(first())
        def _():
            for cp in ride.copies(r_ins, r_outs, send_sems, recv_sems):
                cp.start()

        body(*ins, *outs, *scratch)

        @pl.when(last())
        def _():
            for cp in ride.copies(r_ins, r_outs, send_sems, recv_sems):
                cp.wait()

    def run(*operands):
        res = pl.pallas_call(
            wrapped, in_specs=in_specs + [ANY] * n_ri, out_specs=out_specs + [ANY] * n_ro,
            out_shape=out_shape + ride.out_shapes,
            scratch_shapes=scratch_shapes + [_dma_sems(ride.n_copies), _dma_sems(ride.n_copies)],
            input_output_aliases={n_in + a: n_out + a for a in range(n_ri)} if ride.in_place else {}, **kw,
        )(*operands, *ride.arrays)
        return res[:n_out], res[n_out:]
    return run


def _bucket_tables():
    i = np.arange(BLK)[:, None]
    j = np.arange(2 * BLK)[None, :]
    dist = BLK + i - j
    valid = (dist >= 0) & (dist <= BLK)
    tabs = []
    for _, dil in GROUPS:
        n = (np.clip(dist, 0, BLK) * dil).astype(np.int32)
        max_exact = NUM_BUCKETS // 2
        nf = np.maximum(n, 1).astype(np.float32)
        large = max_exact + (np.log(nf / np.float32(max_exact)) / np.float32(math.log(MAX_DISTANCE / max_exact))
                             * np.float32(NUM_BUCKETS - max_exact)).astype(np.int32)
        large = np.minimum(large, NUM_BUCKETS - 1)
        bucket = np.where(n < max_exact, n, large)
        tab = np.where(valid, bucket, -1).astype(np.int32)
        perm = _block_perm(dil)
        tabs.append(tab[perm][:, np.concatenate([perm, BLK + perm])])
    return np.stack(tabs)


def _bias_table(rel_bias, buckets):
    def body(rb_ref, bk_ref, out_ref):
        gh = pl.program_id(0)
        bk = bk_ref[...]
        acc = jnp.full((BLK, 2 * BLK), NEG, F32)
        for b in range(NUM_BUCKETS):
            acc = jnp.where(bk == b, rb_ref[b, gh], acc)
        out_ref[...] = acc

    return pl.pallas_call(
        body, name="bias_table", grid=(NG * NH,),
        in_specs=[pl.BlockSpec(memory_space=pltpu.SMEM),
                  pl.BlockSpec((None, BLK, 2 * BLK), lambda gh: (gh // NH, 0, 0))],
        out_specs=pl.BlockSpec((None, BLK, 2 * BLK), lambda gh: (gh, 0, 0)),
        out_shape=_sds((NG * NH, BLK, 2 * BLK)),
        compiler_params=_params("arbitrary"),
    )(rel_bias, buckets)


def _bias_grad(ds_acc, buckets, ride):
    def body(acc_ref, bk_ref, out_ref):
        bk = bk_ref[...]
        acc = acc_ref[...]
        lane = lax.broadcasted_iota(jnp.int32, (8, 128), 1)
        out = jnp.zeros((8, 128), F32)
        for b in range(NUM_BUCKETS):
            val = jnp.sum(jnp.where(bk == b, acc, 0.0))
            out = jnp.where(lane == b, val, out)
        out_ref[...] = out

    (out,), rode = _call_with_ride(
        body, ride, lambda: pl.program_id(0) == 0, lambda: pl.program_id(0) == NG * NH - 1,
        name="bias_grad", grid=(NG * NH,),
        in_specs=[pl.BlockSpec((None, BLK, 2 * BLK), lambda gh: (gh, 0, 0)),
                  pl.BlockSpec((None, BLK, 2 * BLK), lambda gh: (gh // NH, 0, 0))],
        out_specs=[pl.BlockSpec((None, 8, 128), lambda gh: (gh, 0, 0))],
        out_shape=[_sds((NG * NH, 8, 128))],
        compiler_params=_params("arbitrary"),
    )(ds_acc, buckets)
    return out, rode


def _mod_partial(c_all, w_ada_s, b_ada_s):
    def body(c_ref, w_ref, b_ref, o_ref):
        o_ref[...] = _dot(c_ref[...].astype(BF16), w_ref[...].astype(BF16)) + b_ref[...]

    return pl.pallas_call(body, name="mod_partial", out_shape=_sds((8, w_ada_s.shape[1])),
                          compiler_params=_params())(c_all, w_ada_s, b_ada_s)


def _prenorm(x, norm_g, mod):
    S = x.shape[0]
    tm = 512

    def body(x_ref, g_ref, mod_ref, h_ref):
        xv = x_ref[...]
        r = lax.rsqrt(jnp.mean(xv * xv, axis=-1, keepdims=True) + EPS)
        n1 = xv * r * g_ref[...]
        h_ref[...] = (n1 * (1.0 + mod_ref[:, D:2 * D]) + mod_ref[:, 0:D]).astype(BF16)

    return pl.pallas_call(
        body, name="prenorm", grid=(S // tm,),
        in_specs=[pl.BlockSpec((tm, D), lambda i: (i, 0)), pl.BlockSpec((1, D), lambda i: (0, 0)),
                  pl.BlockSpec((1, 3 * D), lambda i: (0, 0))],
        out_specs=pl.BlockSpec((tm, D), lambda i: (i, 0)),
        out_shape=_sds((S, D), BF16), compiler_params=_params("parallel"),
    )(x, norm_g, mod)


def _proj(h, wg_in, j0, nj, dtype, name):
    S = h.shape[0]
    tm = 2048
    per = wg_in.shape[2] // CB

    def body(h_ref, w_ref, o_ref):
        o_ref[...] = _dot(h_ref[...], w_ref[...]).astype(dtype)

    return pl.pallas_call(
        body, name=name, grid=(S // tm, nj),
        in_specs=[pl.BlockSpec((tm, D), lambda m, j: (m, 0)),
                  pl.BlockSpec((None, D, CB), lambda m, j: ((j0 + j) // per, 0, (j0 + j) % per))],
        out_specs=pl.BlockSpec((tm, CB), lambda m, j: (m, j)),
        out_shape=_sds((S, nj * CB), dtype), compiler_params=_params("parallel", "parallel"),
    )(h, wg_in)


HS = 4
SLAB = HS * HD


def _lane_head(rows):
    return lax.broadcasted_iota(jnp.int32, (rows, SLAB), 1) // HD


def _head_stack(a):
    head = _lane_head(a.shape[0])
    return jnp.concatenate([jnp.where(head == h, a, jnp.zeros_like(a)) for h in range(HS)], axis=0)


def _head_unstack(a):
    rows = a.shape[0] // HS
    head = _lane_head(rows)
    out = a[:rows]
    for h in range(1, HS):
        out = jnp.where(head == h, a[h * rows:(h + 1) * rows], out)
    return out


STAT_W = 128
VIEW = 16


def _sub_layout(dil):
    if dil == 1:
        return BLK, [None]
    return BLK * dil // VIEW, [[r + dil * u for u in range(VIEW // dil)] for r in range(dil)]


def _block_perm(dil):
    a_rows, _ = _sub_layout(dil)
    p = np.arange(BLK)
    return p if dil == 1 else (VIEW // dil) * (p % a_rows) + p // a_rows


LB = 128
N_SLAB = NH // HS


def _ld(refs, bs, s, w):
    if bs is None:
        return refs[0][:, s * w:(s + 1) * w]
    a_rows = refs[0].shape[0] // VIEW
    return jnp.concatenate([jnp.concatenate([ref[pl.ds(b, a_rows, stride=VIEW), :] for b in bs], axis=0)
                            for ref in refs], axis=1)


def _st(ref, bs, s, val):
    if bs is None:
        ref[:, s * SLAB:(s + 1) * SLAB] = val
        return
    a_rows = val.shape[0] // len(bs)
    for u, b in enumerate(bs):
        ref[:, b, :] = val[u * a_rows:(u + 1) * a_rows]


def _attn_views(dil, S):
    a_rows, subs = _sub_layout(dil)
    if dil == 1:
        def ispecs(base, w, f):
            return [pl.BlockSpec((BLK, N_SLAB * w), lambda sg, n: (f(n), base // (N_SLAB * w)))]
        return subs, S // BLK, N_SLAB, ispecs, (lambda w: (S, w)), (
            lambda f: pl.BlockSpec((BLK, AW), lambda sg, n: (f(n), 0)))

    def ispecs(base, w, f):
        return [pl.BlockSpec((a_rows * VIEW, LB), lambda sg, n, k=k: (f(n), (base + sg * w) // LB + k))
                for k in range(w // LB)]
    return subs, S // (a_rows * VIEW), 1, ispecs, (lambda w: (S // VIEW, VIEW, w)), (
        lambda f: pl.BlockSpec((a_rows, VIEW, SLAB), lambda sg, n: (f(n), 0, sg)))


def _attn_fwd(qkv_g, bias_tab, g):
    S = qkv_g.shape[0]
    subs, nbq, sps, ispecs, shape, ospec = _attn_views(GROUPS[g][1], S)
    cur, prev = (lambda n: n), (lambda n: jnp.maximum(n - 1, 0))
    in_specs = [ispecs(0, SLAB, cur), ispecs(AW, SLAB, prev), ispecs(AW, SLAB, cur), ispecs(2 * AW, SLAB, prev),
                ispecs(2 * AW, SLAB, cur)]
    nl = len(in_specs[0])

    def body(*refs):
        q, kp, kc, vp, vc = (refs[t * nl:(t + 1) * nl] for t in range(5))
        b_ref, o_ref, l_ref = refs[5 * nl:]
        n = pl.program_id(1)
        col = lax.broadcasted_iota(jnp.int32, (HS * BLK, 2 * BLK), 1)
        keep = (col >= BLK) | (n > 0)
        for s_ in range(sps):
            bias = b_ref[pl.ds(s_ * HS, HS)].reshape(HS * BLK, 2 * BLK)
            for bs in subs:
                kb = jnp.concatenate([_ld(kp, bs, s_, SLAB), _ld(kc, bs, s_, SLAB)], axis=0).astype(BF16)
                vb = jnp.concatenate([_ld(vp, bs, s_, SLAB), _ld(vc, bs, s_, SLAB)], axis=0).astype(BF16)
                s = _dot_nt(_head_stack(_ld(q, bs, s_, SLAB).astype(BF16)), kb) * (HD ** -0.5) + bias
                s = jnp.where(keep, s, NEG)
                m = jnp.max(s, axis=-1, keepdims=True)
                p = jnp.exp(s - m)
                den = jnp.sum(p, axis=-1, keepdims=True)
                _st(o_ref, bs, s_, _head_unstack(_dot(p.astype(BF16), vb) / den))
                _st(l_ref, bs, s_, _head_unstack(jnp.broadcast_to(m + jnp.log(den), (HS * BLK, SLAB))))

    out = _sds(shape(AW))
    o, l = pl.pallas_call(
        body, name=f"attn_fwd{g}", grid=(N_SLAB // sps, nbq),
        in_specs=sum(in_specs, []) + [pl.BlockSpec((sps * HS, BLK, 2 * BLK),
                                                   lambda sg, n: (g * (N_SLAB // sps) + sg, 0, 0))],
        out_specs=[ospec(cur), ospec(cur)],
        out_shape=[out, out], compiler_params=_params("parallel", "arbitrary"),
    )(*([qkv_g] * (5 * nl)), bias_tab)
    return o.reshape(S, AW), l.reshape(S, AW)


def _attn_bwd(qkv_g, dattn, stats, bias_tab, g, ride):
    S = qkv_g.shape[0]
    subs, nbq, sps, ispecs, shape, ospec = _attn_views(GROUPS[g][1], S)
    cur = lambda n: jnp.minimum(n, nbq - 1)
    prev = lambda n: jnp.clip(n - 1, 0, nbq - 1)
    late = lambda n: jnp.maximum(n - 1, 0)
    in_specs = [ispecs(0, SLAB, cur), ispecs(AW, SLAB, prev), ispecs(AW, SLAB, cur), ispecs(2 * AW, SLAB, prev),
                ispecs(2 * AW, SLAB, cur), ispecs(0, SLAB, cur), ispecs(0, STAT_W, cur)]
    nl = len(in_specs[0])

    def body(*refs):
        q, kp, kc, vp, vc, da = (refs[t * nl:(t + 1) * nl] for t in range(6))
        st_ref, b_ref, dq_ref, dk_ref, dv_ref, ds_ref, ck_ref, cv_ref = refs[6 * nl:]
        n = pl.program_id(1)

        @pl.when(n == 0)
        def _():
            ds_ref[...] = jnp.zeros_like(ds_ref)
            ck_ref[...] = jnp.zeros_like(ck_ref)
            cv_ref[...] = jnp.zeros_like(cv_ref)

        @pl.when(n < nbq)
        def _():
            col = lax.broadcasted_iota(jnp.int32, (HS * BLK, 2 * BLK), 1)
            keep = (col >= BLK) | (n > 0)
            for s_ in range(sps):
                cs = slice(s_ * SLAB, (s_ + 1) * SLAB)
                bias = b_ref[pl.ds(s_ * HS, HS)].reshape(HS * BLK, 2 * BLK)
                for i, bs in enumerate(subs):
                    st = _ld((st_ref,), bs, s_, STAT_W)
                    kb = jnp.concatenate([_ld(kp, bs, s_, SLAB), _ld(kc, bs, s_, SLAB)], axis=0).astype(BF16)
                    vb = jnp.concatenate([_ld(vp, bs, s_, SLAB), _ld(vc, bs, s_, SLAB)], axis=0).astype(BF16)
                    lse = jnp.concatenate([st[:, h:h + 1] for h in range(HS)], axis=0)
                    delta = jnp.concatenate([st[:, HS + h:HS + h + 1] for h in range(HS)], axis=0)
                    qs = _head_stack(_ld(q, bs, s_, SLAB).astype(BF16))
                    dos = _head_stack(_ld(da, bs, s_, SLAB).astype(BF16))
                    s = _dot_nt(qs, kb) * (HD ** -0.5) + bias
                    s = jnp.where(keep, s, NEG)
                    p = jnp.exp(s - lse)
                    ds = p * (_dot_nt(dos, vb) - delta)
                    ds_ref[pl.ds(s_ * HS, HS)] += ds.reshape(HS, BLK, 2 * BLK)
                    ds_b = (ds * (HD ** -0.5)).astype(BF16)
                    _st(dq_ref, bs, s_, _head_unstack(_dot(ds_b, kb)))
                    dkb = _dot_tn(ds_b, qs)
                    dvb = _dot_tn(p.astype(BF16), dos)
                    _st(dk_ref, bs, s_, ck_ref[i, :, cs] + dkb[:BLK])
                    _st(dv_ref, bs, s_, cv_ref[i, :, cs] + dvb[:BLK])
                    ck_ref[i, :, cs] = dkb[BLK:]
                    cv_ref[i, :, cs] = dvb[BLK:]

        @pl.when(n == nbq)
        def _():
            for s_ in range(sps):
                for i, bs in enumerate(subs):
                    _st(dk_ref, bs, s_, ck_ref[i, :, s_ * SLAB:(s_ + 1) * SLAB])
                    _st(dv_ref, bs, s_, cv_ref[i, :, s_ * SLAB:(s_ + 1) * SLAB])

    out = _sds(shape(AW))
    nsg = N_SLAB // sps
    (dq, dk, dv, ds_acc), rode = _call_with_ride(
        body, ride, lambda: (pl.program_id(0) == 0) & (pl.program_id(1) == 0),
        lambda: (pl.program_id(0) == nsg - 1) & (pl.program_id(1) == nbq),
        name=f"attn_bwd{g}", grid=(nsg, nbq + 1),
        in_specs=sum(in_specs, []) + [pl.BlockSpec((sps * HS, BLK, 2 * BLK), lambda sg, n: (g * nsg + sg, 0, 0))],
        out_specs=[ospec(cur), ospec(late), ospec(late),
                   pl.BlockSpec((sps * HS, BLK, 2 * BLK), lambda sg, n: (sg, 0, 0))],
        out_shape=[out] * 3 + [_sds((NH, BLK, 2 * BLK))],
        scratch_shapes=[pltpu.VMEM((len(subs), BLK, sps * SLAB), F32), pltpu.VMEM((len(subs), BLK, sps * SLAB), F32)],
        compiler_params=_params("arbitrary", "arbitrary"),
    )(*([qkv_g] * (5 * nl)), *([dattn] * nl), stats, bias_tab)
    return [dq.reshape(S, AW), dk.reshape(S, AW), dv.reshape(S, AW)], ds_acc, rode


TM_MIX = 256


def _mix_specs(tm):
    row512 = pl.BlockSpec((tm, AW), lambda i: (i, 0))
    return ([row512] * 6 + [
        pl.BlockSpec((tm, REST_W), lambda i: (i, 0)),
        pl.BlockSpec((HALO, AW), lambda i: (jnp.maximum(i * (tm // HALO) - 1, 0), 1)),
        pl.BlockSpec((AW, D), lambda i: (0, 0)), pl.BlockSpec((AW, D), lambda i: (0, 0)),
        pl.BlockSpec((4, PGW, PGW), lambda i: (0, 0, 0)), pl.BlockSpec((1, AW), lambda i: (0, 0))])


def _mix_forward(i, tm, o_refs, l_refs, rest_ref, halo_ref, wab_ref, wpb_ref, pw_ref, ps_ref):
    l0, l1, l2 = (r[...] for r in l_refs)
    mx = jnp.maximum(jnp.maximum(l0, l1), l2)
    e0, e1, e2 = jnp.exp(l0 - mx), jnp.exp(l1 - mx), jnp.exp(l2 - mx)
    den = e0 + e1 + e2
    lj = mx + jnp.log(den)
    attn = (e0 * o_refs[0][...] + e1 * o_refs[1][...] + e2 * o_refs[2][...]) / den

    z_attn = rest_ref[:, 0:AW]
    u = rest_ref[:, AW:2 * AW]
    z_pool = rest_ref[:, 2 * AW:3 * AW]
    g_attn = rest_ref[:, 3 * AW:3 * AW + D]
    g_pool = rest_ref[:, 3 * AW + D:3 * AW + 2 * D]

    sg_a = _sigmoid(z_attn)
    sil_a = z_attn * sg_a
    a_g = (attn * sil_a).astype(BF16)
    y_attn = _dot(a_g, wab_ref[...])

    halo = jnp.where(i > 0, halo_ref[...], 0.0)
    ext = jnp.concatenate([halo, u], axis=0)
    t = i * tm + lax.broadcasted_iota(jnp.int32, (tm, 1), 0)
    pooled, mixed_raw = [], []
    for gi, win in enumerate(POOL_WINDOWS):
        s = ext[:, gi * PGW:(gi + 1) * PGW]
        sh = 1
        while sh < win:
            s = s + pltpu.roll(s, sh, 0)
            sh *= 2
        cnt = jnp.minimum(t + 1, win).astype(F32)
        pg = s[HALO:] / cnt - u[:, gi * PGW:(gi + 1) * PGW]
        pooled.append(pg.astype(BF16))
        mixed_raw.append(_dot(pooled[-1], pw_ref[gi].astype(BF16)))
    mixed_raw = jnp.concatenate(mixed_raw, axis=1)
    mixed = mixed_raw * ps_ref[...]
    sg_p = _sigmoid(z_pool)
    sil_p = z_pool * sg_p
    m_g = (mixed * sil_p).astype(BF16)
    y_pool = _dot(m_g, wpb_ref[...])

    sa = _sigmoid(g_attn)
    sp = _sigmoid(g_pool)
    merged = sa * y_attn + sp * y_pool
    return dict(lj=lj, attn=attn, z_attn=z_attn, z_pool=z_pool, sg_a=sg_a, sil_a=sil_a, a_g=a_g, y_attn=y_attn,
                pooled=pooled, mixed_raw=mixed_raw, mixed=mixed, sg_p=sg_p, sil_p=sil_p, m_g=m_g, y_pool=y_pool,
                sa=sa, sp=sp, merged=merged)


def _tail(x, target, os_, ls_, rest, wab, wpb, pool_w, pool_scale, wout, mod, final_g):
    S = x.shape[0]
    tm = TM_MIX

    def body(o0, o1, o2, l0, l1, l2, rest_ref, halo_ref, wab_ref, wpb_ref, pw_ref, ps_ref,
             x_ref, t_ref, wo_ref, mod_ref, fg_ref, dx2_ref, dmo_ref, loss_ref, dfg_ref, dgate_ref):
        i = pl.program_id(0)

        @pl.when(i == 0)
        def _():
            loss_ref[...] = jnp.zeros_like(loss_ref)
            dfg_ref[...] = jnp.zeros_like(dfg_ref)
            dgate_ref[...] = jnp.zeros_like(dgate_ref)

        f = _mix_forward(i, tm, (o0, o1, o2), (l0, l1, l2), rest_ref, halo_ref, wab_ref, wpb_ref, pw_ref, ps_ref)
        mo = _dot(f["merged"].astype(BF16), wo_ref[...])
        gate = mod_ref[:, 2 * D:3 * D]
        fg = fg_ref[...]
        x2 = x_ref[...] + gate * mo
        r2 = lax.rsqrt(jnp.mean(x2 * x2, axis=-1, keepdims=True) + EPS)
        n2 = x2 * r2
        err = n2 * fg - t_ref[...]
        loss_ref[...] += 0.5 * jnp.sum(jnp.mean(err * err, axis=-1, keepdims=True))
        dy = err * (1.0 / D)
        dfg_ref[...] += jnp.sum(dy * n2, axis=0, keepdims=True)
        dn = dy * fg
        dx2 = r2 * (dn - n2 * jnp.mean(dn * n2, axis=-1, keepdims=True))
        dgate_ref[...] += jnp.sum(dx2 * mo, axis=0, keepdims=True)
        dx2_ref[...] = dx2
        dmo_ref[...] = (dx2 * gate).astype(BF16)

    row = pl.BlockSpec((tm, D), lambda i: (i, 0))
    vec = pl.BlockSpec((1, D), lambda i: (0, 0))
    return pl.pallas_call(
        body, name="tail", grid=(S // tm,),
        in_specs=_mix_specs(tm) + [row, row, pl.BlockSpec((D, D), lambda i: (0, 0)),
                                   pl.BlockSpec((1, 3 * D), lambda i: (0, 0)), vec],
        out_specs=[row, row, pl.BlockSpec((8, 128), lambda i: (0, 0)), vec, vec],
        out_shape=[_sds((S, D)), _sds((S, D), BF16), _sds((8, 128)), _sds((1, D)), _sds((1, D))],
        compiler_params=_params("arbitrary"),
    )(*os_, *ls_, rest, rest, wab, wpb, pool_w, pool_scale, x, target, wout, mod, final_g)


def _mix_bwd(dmo, os_, ls_, rest, wab, wpb, pool_w, pool_scale, wout):
    S = dmo.shape[0]
    tm = TM_MIX
    nt = S // tm
    sw = D // N_SHARD

    def body(o0, o1, o2, l0, l1, l2, rest_ref, halo_ref, wab_ref, wpb_ref, pw_ref, ps_ref, dmo_ref, wo_ref,
             dattn_ref, stats_ref, dpooled_ref, drest_ref, dwo_hbm, dwab_hbm, dwpb_hbm, dpw_ref, dps_ref,
             awo, awab, awpb):
        i = pl.program_id(0)

        @pl.when(i == 0)
        def _():
            awo[...] = jnp.zeros_like(awo)
            awab[...] = jnp.zeros_like(awab)
            awpb[...] = jnp.zeros_like(awpb)
            dpw_ref[...] = jnp.zeros_like(dpw_ref)
            dps_ref[...] = jnp.zeros_like(dps_ref)

        f = _mix_forward(i, tm, (o0, o1, o2), (l0, l1, l2), rest_ref, halo_ref, wab_ref, wpb_ref, pw_ref, ps_ref)
        dmo_b = dmo_ref[...]
        dmerged = _dot_nt(dmo_b, wo_ref[...])
        awo[...] += _dot_tn(f["merged"].astype(BF16), dmo_b)
        sa, sp = f["sa"], f["sp"]
        dya = (dmerged * sa).astype(BF16)
        dyp = (dmerged * sp).astype(BF16)
        dg_attn = dmerged * f["y_attn"] * sa * (1.0 - sa)
        dg_pool = dmerged * f["y_pool"] * sp * (1.0 - sp)
        dag = _dot_nt(dya, wab_ref[...])
        awab[...] += _dot_tn(f["a_g"], dya)
        dmg = _dot_nt(dyp, wpb_ref[...])
        awpb[...] += _dot_tn(f["m_g"], dyp)
        dattn = dag * f["sil_a"]
        dattn_ref[...] = dattn
        prod = dattn * f["attn"]
        lane = lax.broadcasted_iota(jnp.int32, (tm, STAT_W), 1)
        for sb in range(N_SLAB):
            st = jnp.zeros((tm, STAT_W), F32)
            for h in range(HS):
                hs = slice((sb * HS + h) * HD, (sb * HS + h + 1) * HD)
                st = jnp.where(lane == h, f["lj"][:, hs.start:hs.start + 1], st)
                st = jnp.where(lane == HS + h, jnp.sum(prod[:, hs], axis=-1, keepdims=True), st)
            stats_ref[:, sb * STAT_W:(sb + 1) * STAT_W] = st
        dz_attn = dag * f["attn"] * (f["sg_a"] * (1.0 + f["z_attn"] * (1.0 - f["sg_a"])))
        dmixed = dmg * f["sil_p"]
        dz_pool = dmg * f["mixed"] * (f["sg_p"] * (1.0 + f["z_pool"] * (1.0 - f["sg_p"])))
        dps_ref[...] += jnp.sum(dmixed * f["mixed_raw"], axis=0, keepdims=True)
        dpm = (dmixed * ps_ref[...]).astype(BF16)
        for gi in range(len(POOL_WINDOWS)):
            cs = slice(gi * PGW, (gi + 1) * PGW)
            dpw_ref[gi] += _dot_tn(f["pooled"][gi], dpm[:, cs])
            dpooled_ref[:, cs] = _dot_nt(dpm[:, cs], pw_ref[gi].astype(BF16))
        drest_ref[:, 0:AW] = dz_attn.astype(BF16)
        drest_ref[:, AW:2 * AW] = jnp.zeros((tm, AW), BF16)
        drest_ref[:, 2 * AW:3 * AW] = dz_pool.astype(BF16)
        drest_ref[:, 3 * AW:3 * AW + D] = dg_attn.astype(BF16)
        drest_ref[:, 3 * AW + D:3 * AW + 2 * D] = dg_pool.astype(BF16)

        @pl.when(i == nt - 1)
        def _():
            pltpu.sync_copy(awo, dwo_hbm)
            for k in range(N_SHARD):
                pltpu.sync_copy(awab.at[:, pl.ds(k * sw, sw)], dwab_hbm.at[k])
                pltpu.sync_copy(awpb.at[:, pl.ds(k * sw, sw)], dwpb_hbm.at[k])

    row512 = pl.BlockSpec((tm, AW), lambda i: (i, 0))
    outs = pl.pallas_call(
        body, name="mix_bwd", grid=(nt,),
        in_specs=_mix_specs(tm) + [pl.BlockSpec((tm, D), lambda i: (i, 0)), pl.BlockSpec((D, D), lambda i: (0, 0))],
        out_specs=[row512, pl.BlockSpec((tm, N_SLAB * STAT_W), lambda i: (i, 0)), row512,
                   pl.BlockSpec((tm, REST_W), lambda i: (i, 0)), ANY, ANY, ANY,
                   pl.BlockSpec((4, PGW, PGW), lambda i: (0, 0, 0)), pl.BlockSpec((1, AW), lambda i: (0, 0))],
        out_shape=[_sds((S, AW)), _sds((S, N_SLAB * STAT_W)), _sds((S, AW)), _sds((S, REST_W), BF16),
                   _sds((D, D)), _sds((N_SHARD, AW, sw)), _sds((N_SHARD, AW, sw)), _sds((4, PGW, PGW)), _sds((1, AW))],
        scratch_shapes=[pltpu.VMEM((D, D), F32), pltpu.VMEM((AW, D), F32), pltpu.VMEM((AW, D), F32)],
        compiler_params=_params("arbitrary"),
    )(*os_, *ls_, rest, rest, wab, wpb, pool_w, pool_scale, dmo, wout)
    dattn, stats, dpooled, drest, dwo, dwab, dwpb, dpw, dps = outs
    return dattn, stats, dpooled, drest, dwo.reshape(N_SHARD, D // N_SHARD, D), dwab, dwpb, dpw, dps


def _pool_bwd(dpooled):
    S = dpooled.shape[0]
    tm = 512
    nt = S // tm

    def body(dp_ref, nxt_ref, du_ref):
        i = pl.program_id(0)
        t = i * tm + lax.broadcasted_iota(jnp.int32, (tm + HALO, 1), 0)
        nxt = jnp.where(i < nt - 1, nxt_ref[...], 0.0)
        ext = jnp.concatenate([dp_ref[...], nxt], axis=0)
        for gi, win in enumerate(POOL_WINDOWS):
            cs = slice(gi * PGW, (gi + 1) * PGW)
            s = ext[:, cs] / jnp.minimum(t + 1, win).astype(F32)
            sh = 1
            while sh < win:
                s = s + pltpu.roll(s, tm + HALO - sh, 0)
                sh *= 2
            du_ref[:, cs] = (s[:tm] - dp_ref[:, cs]).astype(BF16)

    return pl.pallas_call(
        body, name="pool_bwd", grid=(nt,),
        in_specs=[pl.BlockSpec((tm, AW), lambda i: (i, 0)),
                  pl.BlockSpec((HALO, AW), lambda i: (jnp.minimum((i + 1) * (tm // HALO), S // HALO - 1), 0))],
        out_specs=pl.BlockSpec((tm, AW), lambda i: (i, 0)),
        out_shape=_sds((S, AW), BF16), compiler_params=_params("parallel"),
    )(dpooled, dpooled)


TB = 1024


def _dh(dproj, wg_in, ride):
    S = dproj.shape[0]
    per = wg_in.shape[2] // TB
    nm, nk = S // TB, IN_W // TB

    def body(dp_ref, w_ref, out_ref):
        @pl.when(pl.program_id(1) == 0)
        def _():
            out_ref[...] = jnp.zeros_like(out_ref)

        out_ref[...] += _dot_nt(dp_ref[...], w_ref[...])

    (dh,), rode = _call_with_ride(
        body, ride, lambda: (pl.program_id(0) == 0) & (pl.program_id(1) == 0),
        lambda: (pl.program_id(0) == nm - 1) & (pl.program_id(1) == nk - 1),
        name="dh", grid=(nm, nk),
        in_specs=[pl.BlockSpec((TB, TB), lambda m, kk: (m, kk)),
                  pl.BlockSpec((None, D, TB), lambda m, kk: (kk // per, 0, kk % per))],
        out_specs=[pl.BlockSpec((TB, D), lambda m, kk: (m, 0))],
        out_shape=[_sds((S, D))], compiler_params=_params("arbitrary", "arbitrary"),
    )(dproj, wg_in)
    return dh, rode


def _dw_in(h, dproj):
    S = dproj.shape[0]
    per = IN_W // N_SHARD // TB

    def body(h_ref, dp_ref, out_ref):
        @pl.when(pl.program_id(1) == 0)
        def _():
            out_ref[...] = jnp.zeros_like(out_ref)

        out_ref[...] += _dot_tn(h_ref[...], dp_ref[...])

    return pl.pallas_call(
        body, name="dw_in", grid=(IN_W // TB, S // TB),
        in_specs=[pl.BlockSpec((TB, D), lambda j, kk: (kk, 0)), pl.BlockSpec((TB, TB), lambda j, kk: (kk, j))],
        out_specs=pl.BlockSpec((None, D, TB), lambda j, kk: (j // per, 0, j % per)),
        out_shape=_sds((N_SHARD, D, IN_W // N_SHARD)), compiler_params=_params("parallel", "arbitrary"),
    )(h, dproj)


def _prenorm_bwd(x, dh, dx2, norm_g, mod, ride):
    S = x.shape[0]
    tm = 512
    nt = S // tm

    def body(x_ref, dh_ref, dx2_ref, g_ref, mod_ref, gx_ref, dg_ref, dshift_ref, dscale_ref):
        i = pl.program_id(0)

        @pl.when(i == 0)
        def _():
            dg_ref[...] = jnp.zeros_like(dg_ref)
            dshift_ref[...] = jnp.zeros_like(dshift_ref)
            dscale_ref[...] = jnp.zeros_like(dscale_ref)

        xv = x_ref[...]
        dhv = dh_ref[...]
        g = g_ref[...]
        r = lax.rsqrt(jnp.mean(xv * xv, axis=-1, keepdims=True) + EPS)
        xh = xv * r
        dshift_ref[...] += jnp.sum(dhv, axis=0, keepdims=True)
        dscale_ref[...] += jnp.sum(dhv * (xh * g), axis=0, keepdims=True)
        dn1 = dhv * (1.0 + mod_ref[:, D:2 * D])
        dg_ref[...] += jnp.sum(dn1 * xh, axis=0, keepdims=True)
        dxh = dn1 * g
        gx_ref[...] = dx2_ref[...] + r * (dxh - xh * jnp.mean(dxh * xh, axis=-1, keepdims=True))

    row = pl.BlockSpec((tm, D), lambda i: (i, 0))
    vec = pl.BlockSpec((1, D), lambda i: (0, 0))
    return _call_with_ride(
        body, ride, lambda: pl.program_id(0) == 0, lambda: pl.program_id(0) == nt - 1,
        name="prenorm_bwd", grid=(nt,),
        in_specs=[row, row, row, vec, pl.BlockSpec((1, 3 * D), lambda i: (0, 0))],
        out_specs=[row, vec, vec, vec],
        out_shape=[_sds((S, D)), _sds((1, D)), _sds((1, D)), _sds((1, D))],
        compiler_params=_params("arbitrary"),
    )(x, dh, dx2, norm_g, mod)


def _local_step(x, target, mod, wg_in, wab, wpb, wout, pool_w, pool_scale, rel_bias, norm_g, final_g, half_idx,
                chip_half):
    buckets = jnp.asarray(_bucket_tables())
    bias_tab = _bias_table(rel_bias, buckets)
    h = _prenorm(x, norm_g, mod)
    qkv = [_proj(h, wg_in, 3 * g, 3, F32, f"proj_qkv{g}") for g in range(NG)]
    rest = _proj(h, wg_in, NCB_QKV, REST_W // CB, F32, "proj_rest")
    os_, ls_ = zip(*[_attn_fwd(qkv[g], bias_tab, g) for g in range(NG)])
    dx2, dmo, loss, dfinal_g, dgate = _tail(x, target, os_, ls_, rest, wab, wpb, pool_w, pool_scale, wout, mod, final_g)
    dattn, stats, dpooled, drest, dw_out, dw_ab, dw_pb, dpool_w, dpool_scale = _mix_bwd(
        dmo, os_, ls_, rest, wab, wpb, pool_w, pool_scale, wout)
    du = _pool_bwd(dpooled)

    small = [dw_ab, dw_pb, dw_out]
    dqkv0, ds0, sib_small = _attn_bwd(qkv[0], dattn, stats, bias_tab, 0, _ride_sibling_halves(small))
    p_small = [_pair_sum(g, t, half_idx, f"rs_pair_sum{a}") for a, (g, t) in enumerate(zip(small, sib_small))]
    dqkv1, ds1, u_small = _attn_bwd(qkv[1], dattn, stats, bias_tab, 1,
                                    _ride_chip_exchange([p16 for _, p16 in p_small]))
    rs_ab, rs_pb, rs_out = [_chip_sum(p32, u, chip_half, f"rs_chip_sum{a}")
                            for a, ((p32, _), u) in enumerate(zip(p_small, u_small))]
    dqkv2, ds2, _ = _attn_bwd(qkv[2], dattn, stats, bias_tab, 2, None)

    dproj = jnp.concatenate([a.astype(BF16) for a in dqkv0 + dqkv1 + dqkv2] + [drest[:, :AW], du, drest[:, 2 * AW:]],
                            axis=1)
    dw_in = _dw_in(h, dproj)
    drel_rows, (sib_in,) = _bias_grad(jnp.concatenate([ds0, ds1, ds2], axis=0), buckets,
                                      _ride_sibling_halves([dw_in]))
    drel = drel_rows[:, 0, :NUM_BUCKETS].T
    p32_in, p16_in = _pair_sum(dw_in, sib_in, half_idx, "rs_pair_sum_in")
    dh, (u_in,) = _dh(dproj, wg_in, _ride_chip_exchange([p16_in]))
    rs_in = _chip_sum(p32_in, u_in, chip_half, "rs_chip_sum_in")

    (grad_x, dnorm_g, dshift, dscale), (g_in, g_ab, g_pb, g_out) = _prenorm_bwd(
        x, dh, dx2, norm_g, mod, _ride_sibling_join([rs_in, rs_ab, rs_pb, rs_out]))
    dmod = jnp.concatenate([dshift, dscale, dgate], axis=1)
    return dict(loss=loss[0, 0], grad_x=grad_x, dmod=dmod, dnorm_g=dnorm_g, dfinal_g=dfinal_g, dpool_w=dpool_w,
                dpool_scale=dpool_scale, drel_bias=drel, dw_in=dw_in, dw_attn_br=dw_ab, dw_pool_br=dw_pb,
                dw_out=dw_out, g_w_in=g_in, g_w_attn_br=g_ab, g_w_pool_br=g_pb, g_w_out=g_out)


def _allgather8(blocks, name, relay=None):
    nb = len(blocks)
    relay = [False] * nb if relay is None else list(relay)

    def body(*refs):
        ins, outs = refs[:nb], refs[nb:2 * nb]
        send_sems, recv_sems, local_sems = refs[2 * nb:]
        x, y, c = lax.axis_index("x"), lax.axis_index("y"), lax.axis_index("c")
        me, sibling = (x, y, c), (x, y, 1 - c)
        here, xn, yn, dg = (x, y), (1 - x, y), (x, 1 - y), (1 - x, 1 - y)

        def slot(a, chip, core, half=None):
            ref = outs[a].at[4 * chip[0] + 2 * chip[1] + core]
            if half is None:
                return ref
            r2 = ref.shape[0] // 2
            return ref.at[pl.ds(half * r2, r2)]

        def copy(a, k, dst, to, src=None):
            return pltpu.make_async_remote_copy(src_ref=dst if src is None else src, dst_ref=dst,
                                                send_sem=send_sems.at[a, k], recv_sem=recv_sems.at[a, k],
                                                device_id=to, device_id_type=MESH)

        def start(cps):
            for cp in cps:
                cp.start()
            return cps

        mine = start([pltpu.make_async_copy(ins[a], slot(a, here, c), local_sems.at[a]) for a in range(nb)])
        sent = []
        for a in range(nb):
            own = slot(a, here, c)
            sent += [copy(a, 0, own, sibling, src=ins[a]), copy(a, 1, own, (*xn, c), src=ins[a]),
                     copy(a, 2, own, (*yn, c), src=ins[a])]
            if not relay[a]:
                sent.append(copy(a, 3, own, (*dg, c), src=ins[a]))
        start(sent)
        for a in range(nb):
            copy(a, 2, slot(a, yn, c), me).wait_recv()
            sent += start([copy(a, 6, slot(a, yn, c), sibling)]
                          + ([copy(a, 3, slot(a, yn, c, 0), (*xn, c))] if relay[a] else []))
        for a in range(nb):
            copy(a, 1, slot(a, xn, c), me).wait_recv()
            sent += start([copy(a, 5, slot(a, xn, c), sibling)]
                          + ([copy(a, 4, slot(a, xn, c, 1), (*yn, c))] if relay[a] else []))
        for a in range(nb):
            for k, half in ((3, 0), (4, 1)) if relay[a] else ((3, None),):
                copy(a, k, slot(a, dg, c, half), me).wait_recv()
                sent += start([copy(a, 4 + k, slot(a, dg, c, half), sibling)])
        for a in range(nb):
            copy(a, 0, slot(a, here, 1 - c), me).wait_recv()
            copy(a, 5, slot(a, xn, 1 - c), me).wait_recv()
            copy(a, 6, slot(a, yn, 1 - c), me).wait_recv()
            for k, half in ((7, 0), (8, 1)) if relay[a] else ((7, None),):
                copy(a, k, slot(a, dg, 1 - c, half), me).wait_recv()
        for cp in sent:
            cp.wait_send()
        for cp in mine:
            cp.wait()

    return pl.pallas_call(
        body, name=name, in_specs=[ANY] * nb, out_specs=[ANY] * nb,
        out_shape=[_sds((8,) + b.shape, b.dtype) for b in blocks],
        scratch_shapes=[_dma_sems(nb, 9), _dma_sems(nb, 9), _dma_sems(nb)],
    )(*blocks)


def _ride_sibling_halves(gs):
    def copies(ins, outs, send_sems, recv_sems):
        x, y, c = lax.axis_index("x"), lax.axis_index("y"), lax.axis_index("c")
        cps = []
        for a in range(len(gs)):
            r2 = ins[a].shape[1] // 2
            other = ins[a].at[:, pl.ds((1 - c) * r2, r2), :]
            cps.append(pltpu.make_async_remote_copy(src_ref=other, dst_ref=outs[a], send_sem=send_sems.at[a],
                                                    recv_sem=recv_sems.at[a], device_id=(x, y, 1 - c),
                                                    device_id_type=MESH))
        return cps

    return _Ride(gs, [_sds((g.shape[0], g.shape[1] // 2, g.shape[2]), g.dtype) for g in gs], len(gs), copies)


def _pair_sum(g, t, half, name):
    nsh, rows, cols = g.shape
    r2 = rows // 2
    tr = _row_tile(r2, cols)
    nt = r2 // tr

    def body(half_ref, g_ref, t_ref, p32_ref, p16_ref):
        p = g_ref[...] + t_ref[...]
        p32_ref[...] = p
        p16_ref[...] = p.astype(BF16)

    blk = pl.BlockSpec((None, tr, cols), lambda k, i, half_ref: (k, i, 0))
    return pl.pallas_call(
        body, name=name,
        grid_spec=pltpu.PrefetchScalarGridSpec(
            num_scalar_prefetch=1, grid=(nsh, nt),
            in_specs=[pl.BlockSpec((None, tr, cols), lambda k, i, half_ref: (k, half_ref[0] * nt + i, 0)), blk],
            out_specs=[blk, blk]),
        out_shape=[_sds((nsh, r2, cols)), _sds((nsh, r2, cols), BF16)],
        compiler_params=_params("parallel", "parallel"),
    )(half, g, t)


def _ride_chip_exchange(ps):
    def copies(ins, outs, send_sems, recv_sems):
        x, y, c = lax.axis_index("x"), lax.axis_index("y"), lax.axis_index("c")
        chips = [(1 - x, y), (x, 1 - y), (1 - x, 1 - y)]
        cps = []
        for a in range(len(ps)):
            for j, (ox, oy) in enumerate(chips):
                cps.append(pltpu.make_async_remote_copy(src_ref=ins[a].at[2 * ox + oy], dst_ref=outs[a].at[j],
                                                        send_sem=send_sems.at[3 * a + j],
                                                        recv_sem=recv_sems.at[3 * a + j],
                                                        device_id=(ox, oy, c), device_id_type=MESH))
        return cps

    return _Ride(ps, [_sds((3,) + p.shape[1:], p.dtype) for p in ps], 3 * len(ps), copies)


def _chip_sum(p32, u, chip_half, name):
    r2, cols = p32.shape[1:]
    tr = _row_tile(r2, cols)
    nt = r2 // tr

    def body(ch_ref, p_ref, u_ref, o_ref):
        acc = p_ref[...]
        for j in range(3):
            acc = acc + u_ref[j].astype(F32)
        o_ref[...] = acc

    return pl.pallas_call(
        body, name=name,
        grid_spec=pltpu.PrefetchScalarGridSpec(
            num_scalar_prefetch=1, grid=(nt,),
            in_specs=[pl.BlockSpec((None, tr, cols), lambda i, ch_ref: (ch_ref[0], i, 0)),
                      pl.BlockSpec((3, tr, cols), lambda i, ch_ref: (0, i, 0))],
            out_specs=pl.BlockSpec((tr, cols), lambda i, ch_ref: (ch_ref[1] * nt + i, 0))),
        out_shape=_sds((2 * r2, cols)), compiler_params=_params("parallel"),
    )(chip_half, p32, u)


def _ride_sibling_join(fs):
    def copies(ins, outs, send_sems, recv_sems):
        x, y, c = lax.axis_index("x"), lax.axis_index("y"), lax.axis_index("c")
        cps = []
        for a in range(len(fs)):
            r2 = outs[a].shape[0] // 2
            rows = outs[a].at[pl.ds(c * r2, r2), :]
            cps.append(pltpu.make_async_remote_copy(src_ref=rows, dst_ref=rows, send_sem=send_sems.at[a],
                                                    recv_sem=recv_sems.at[a], device_id=(x, y, 1 - c),
                                                    device_id_type=MESH))
        return cps

    return _Ride(fs, [_sds(f.shape, f.dtype) for f in fs], len(fs), copies, in_place=True)


def _row_tile(rows, cols):
    tile = rows
    while tile * cols * 4 > (1 << 20) and tile % 16 == 0:
        tile //= 2
    return tile


def _sum_leading(a, name):
    k = a.shape[0]
    a3 = a.reshape(k, -1, a.shape[-1])
    rows, cols = a3.shape[1:]
    tr = _row_tile(rows, cols)

    def body(a_ref, o_ref):
        acc = a_ref[0]
        for s in range(1, k):
            acc = acc + a_ref[s]
        o_ref[...] = acc

    out = pl.pallas_call(
        body, name=name, grid=(rows // tr,),
        in_specs=[pl.BlockSpec((k, tr, cols), lambda i: (0, i, 0))],
        out_specs=pl.BlockSpec((tr, cols), lambda i: (i, 0)),
        out_shape=_sds((rows, cols), a.dtype), compiler_params=_params("parallel"),
    )(a3)
    return out.reshape(a.shape[1:])


def _w_ada_grad(c_all, dmod_cols):
    def body(c_ref, d_ref, o_ref):
        o_ref[...] = _dot_tn(c_ref[...].astype(BF16), d_ref[...].astype(BF16))

    return pl.pallas_call(body, name="w_ada_grad", out_shape=_sds((c_all.shape[1], dmod_cols.shape[1])),
                          compiler_params=_params())(c_all, dmod_cols)


def _adamw(w, g, m, v, name):
    rows, cols = w.shape
    tr = _row_tile(rows, cols)

    def body(w_ref, g_ref, m_ref, v_ref, d_ref, nm_ref, nv_ref):
        gv = g_ref[...]
        nm = ADAM_B1 * m_ref[...] + (1.0 - ADAM_B1) * gv
        nv = ADAM_B2 * v_ref[...] + (1.0 - ADAM_B2) * (gv * gv)
        m_hat = nm / (1.0 - ADAM_B1 ** ADAM_STEP)
        v_hat = nv / (1.0 - ADAM_B2 ** ADAM_STEP)
        d_ref[...] = -ADAM_LR * (m_hat / (jnp.sqrt(v_hat) + ADAM_EPS) + ADAM_WD * w_ref[...])
        nm_ref[...] = nm
        nv_ref[...] = nv

    spec = pl.BlockSpec((tr, cols), lambda i: (i, 0))
    return pl.pallas_call(
        body, name=name, grid=(rows // tr,), in_specs=[spec] * 4, out_specs=[spec] * 3,
        out_shape=[_sds((rows, cols))] * 3, compiler_params=_params("parallel"),
    )(w, g, m, v)


def _pack_small(b_ada, norm_g, final_g, pool_scale, rel_bias, loss_row, pool_w):
    pad = jnp.zeros((PK_POOLW - PK_LOSS - 1) * 128, F32)
    flat = jnp.concatenate([b_ada.reshape(-1), norm_g.reshape(-1), final_g.reshape(-1), pool_scale.reshape(-1),
                            rel_bias.reshape(-1), loss_row.reshape(-1), pad, pool_w.reshape(-1)])
    return flat.reshape(PK_ROWS, 128)


def _unpack_small(p):
    def take(r0, r1, shape):
        return p[r0:r1].reshape(shape)

    return dict(b_ada=take(PK_BADA, PK_NORMG, (1, 3 * D)), norm_g=take(PK_NORMG, PK_FINALG, (1, D)),
                final_g=take(PK_FINALG, PK_PSCALE, (D,)), pool_scale=take(PK_PSCALE, PK_RELB, (1, AW)),
                rel_bias=take(PK_RELB, PK_LOSS, (NUM_BUCKETS, NG * NH)), loss=p[PK_LOSS, 0],
                pool_w=take(PK_POOLW, PK_ROWS, (1, 4, PGW, PGW)))


def kernel(x, c, norm_g, w_ada, b_ada, w_in, pool_w, pool_scale, w_attn_br, w_pool_br, w_out, rel_bias, final_g, loss_target, m_norm_g, m_w_ada, m_b_ada, m_w_in, m_pool_w, m_pool_scale, m_w_attn_br, m_w_pool_br, m_w_out, m_rel_bias, m_final_g, v_norm_g, v_w_ada, v_b_ada, v_w_in, v_pool_w, v_pool_scale, v_w_attn_br, v_w_pool_br, v_w_out, v_rel_bias, v_final_g):
    ix, iy, ic = lax.axis_index("x"), lax.axis_index("y"), lax.axis_index("c")
    dev = 4 * ix + 2 * iy + ic
    chip = 2 * ix + iy

    def half(w):
        r2 = w.shape[0] // 2
        return lax.dynamic_slice_in_dim(w, ic * r2, r2, axis=0).astype(BF16)

    gathered = _allgather8([jnp.broadcast_to(c, (8, D)), half(w_in[0]), half(w_attn_br[0]), half(w_pool_br[0]),
                            half(w_out[0])], "gather_weights", relay=[False, True, True, True, True])
    c_all = gathered[0][:, 0, :]
    wg_in = gathered[1].reshape(N_SHARD, D, IN_W // N_SHARD)
    wab = gathered[2].reshape(N_SHARD, AW, D // N_SHARD).transpose(1, 0, 2).reshape(AW, D)
    wpb = gathered[3].reshape(N_SHARD, AW, D // N_SHARD).transpose(1, 0, 2).reshape(AW, D)
    wout = gathered[4].reshape(D, D)

    mw = 3 * D // N_SHARD
    modp = _mod_partial(c_all, w_ada[0], lax.dynamic_slice_in_dim(b_ada, chip * mw, mw, axis=1))
    mod_all = _allgather8([modp], "gather_mod")[0]
    mod_full = mod_all[::2].transpose(1, 0, 2).reshape(8, 3 * D)
    mod = lax.dynamic_slice_in_dim(mod_full, dev, 1, axis=0)

    half_idx = jnp.stack([ic]).astype(jnp.int32)
    chip_half = jnp.stack([chip, ic]).astype(jnp.int32)
    r = _local_step(x[0], loss_target[0], mod, wg_in, wab, wpb, wout, pool_w[0], pool_scale, rel_bias, norm_g,
                    final_g.reshape(1, D), half_idx, chip_half)

    packed = _pack_small(r["dmod"], r["dnorm_g"], r["dfinal_g"], r["dpool_scale"], r["drel_bias"],
                         jnp.full((128,), r["loss"], F32), r["dpool_w"])
    small_all = _allgather8([packed], "gather_small")[0]
    small_sum = _sum_leading(small_all, "sum_small")
    dmod_all = small_all[:, PK_BADA:PK_NORMG, :].reshape(8, 3 * D)
    g_w_ada = _w_ada_grad(c_all, lax.dynamic_slice_in_dim(dmod_all, chip * mw, mw, axis=1))

    g_w_in, g_w_ab, g_w_pb, g_w_out = r["g_w_in"], r["g_w_attn_br"], r["g_w_pool_br"], r["g_w_out"]

    small_w = _pack_small(b_ada, norm_g, final_g, pool_scale, rel_bias, jnp.zeros((128,), F32), pool_w)
    small_m = _pack_small(m_b_ada, m_norm_g, m_final_g, m_pool_scale, m_rel_bias, jnp.zeros((128,), F32), m_pool_w)
    small_v = _pack_small(v_b_ada, v_norm_g, v_final_g, v_pool_scale, v_rel_bias, jnp.ones((128,), F32), v_pool_w)
    sd, sm, sv = (_unpack_small(p) for p in _adamw(small_w, small_sum, small_m, small_v, "adamw_small"))
    sg = _unpack_small(small_sum)
    upd = {
        "w_ada": (g_w_ada,) + tuple(_adamw(w_ada[0], g_w_ada, m_w_ada[0], v_w_ada[0], "adamw_w_ada")),
        "w_in": (g_w_in,) + tuple(_adamw(w_in[0], g_w_in, m_w_in[0], v_w_in[0], "adamw_w_in")),
        "w_attn_br": (g_w_ab,) + tuple(_adamw(w_attn_br[0], g_w_ab, m_w_attn_br[0], v_w_attn_br[0], "adamw_w_ab")),
        "w_pool_br": (g_w_pb,) + tuple(_adamw(w_pool_br[0], g_w_pb, m_w_pool_br[0], v_w_pool_br[0], "adamw_w_pb")),
        "w_out": (g_w_out,) + tuple(_adamw(w_out[0], g_w_out, m_w_out[0], v_w_out[0], "adamw_w_out")),
    }
    names = ["norm_g", "w_ada", "b_ada", "w_in", "pool_w", "pool_scale", "w_attn_br", "w_pool_br", "w_out",
             "rel_bias", "final_g"]
    outs = [sg["loss"], r["grad_x"][None]]
    for kind in range(4):
        for nme in names:
            if nme in upd:
                outs.append(upd[nme][kind][None])
            else:
                outs.append((sg, sd, sm, sv)[kind][nme])
    return tuple(outs)
```

```python
import functools
import math

import numpy as np
import jax
import jax.numpy as jnp
from jax import lax
from jax.experimental import pallas as pl
from jax.experimental.pallas import tpu as pltpu

F32 = jnp.float32
BF16 = jnp.bfloat16

D = 1024
HD = 64
NH = 8
AW = NH * HD
GROUPS = ((128, 1), (512, 4), (2048, 16))
NG = len(GROUPS)
BLK = 128
GW = 3 * AW
QKV_W = NG * GW
REST_W = 3584
IN_W = QKV_W + REST_W
CB = 512
NCB = IN_W // CB
NCB_QKV = QKV_W // CB
POOL_WINDOWS = (2, 4, 8, 16)
PGW = 128
HALO = 16
NUM_BUCKETS = 32
MAX_DISTANCE = 2048
EPS = 1e-6
NEG = -1e30
N_SHARD = 4
VMEM_LIMIT = 56 * 1024 * 1024

ADAM_LR = 0.001
ADAM_B1 = 0.9
ADAM_B2 = 0.999
ADAM_EPS = 1e-08
ADAM_WD = 0.01
ADAM_STEP = 10

PK_BADA, PK_NORMG, PK_FINALG, PK_PSCALE, PK_RELB, PK_LOSS, PK_POOLW, PK_ROWS = 0, 24, 32, 40, 48, 80, 88, 600

ANY = pl.BlockSpec(memory_space=pl.ANY)
MESH = pl.DeviceIdType.MESH


def _params(*sem):
    return pltpu.CompilerParams(dimension_semantics=sem, vmem_limit_bytes=VMEM_LIMIT)


def _sds(shape, dtype=F32):
    return jax.ShapeDtypeStruct(shape, dtype)


def _dot(a, b):
    return jnp.dot(a, b, preferred_element_type=F32)


def _dot_nt(a, b):
    return lax.dot_general(a, b, (((1,), (1,)), ((), ())), preferred_element_type=F32)


def _dot_tn(a, b):
    return lax.dot_general(a, b, (((0,), (0,)), ((), ())), preferred_element_type=F32)


def _sigmoid(z):
    return 0.5 * jnp.tanh(0.5 * z) + 0.5


def _dma_sems(*shape):
    return pltpu.SemaphoreType.DMA(shape)


class _Ride:
    def __init__(self, arrays, out_shapes, n_copies, copies):
        self.arrays, self.out_shapes, self.n_copies, self.copies = list(arrays), list(out_shapes), n_copies, copies


def _call_with_ride(body, ride, first, last, *, in_specs, out_specs, out_shape, scratch_shapes=(), **kw):
    in_specs, out_specs, out_shape, scratch_shapes = list(in_specs), list(out_specs), list(out_shape), list(scratch_shapes)
    n_in, n_out, n_sc = len(in_specs), len(out_specs), len(scratch_shapes)
    if ride is None:
        def run_plain(*operands):
            return pl.pallas_call(body, in_specs=in_specs, out_specs=out_specs, out_shape=out_shape,
                                  scratch_shapes=scratch_shapes, **kw)(*operands), []
        return run_plain
    n_ri, n_ro = len(ride.arrays), len(ride.out_shapes)

    def wrapped(*refs):
        ins, rest = refs[:n_in], refs[n_in:]
        r_ins, rest = rest[:n_ri], rest[n_ri:]
        outs, rest = rest[:n_out], rest[n_out:]
        r_outs, rest = rest[:n_ro], rest[n_ro:]
        scratch, (send_sems, recv_sems) = rest[:n_sc], rest[n_sc:]

        @pl.when(first())
        def _():
            for cp in ride.copies(r_ins, r_outs, send_sems, recv_sems):
                cp.start()

        body(*ins, *outs, *scratch)

        @pl.when(last())
        def _():
            for cp in ride.copies(r_ins, r_outs, send_sems, recv_sems):
                cp.wait()

    def run(*operands):
        res = pl.pallas_call(
            wrapped, in_specs=in_specs + [ANY] * n_ri, out_specs=out_specs + [ANY] * n_ro,
            out_shape=out_shape + ride.out_shapes,
            scratch_shapes=scratch_shapes + [_dma_sems(ride.n_copies), _dma_sems(ride.n_copies)], **kw,
        )(*operands, *ride.arrays)
        return res[:n_out], res[n_out:]
    return run


def _bucket_tables():
    i = np.arange(BLK)[:, None]
    j = np.arange(2 * BLK)[None, :]
    dist = BLK + i - j
    valid = (dist >= 0) & (dist <= BLK)
    tabs = []
    for _, dil in GROUPS:
        n = (np.clip(dist, 0, BLK) * dil).astype(np.int32)
        max_exact = NUM_BUCKETS // 2
        nf = np.maximum(n, 1).astype(np.float32)
        large = max_exact + (np.log(nf / np.float32(max_exact)) / np.float32(math.log(MAX_DISTANCE / max_exact))
                             * np.float32(NUM_BUCKETS - max_exact)).astype(np.int32)
        large = np.minimum(large, NUM_BUCKETS - 1)
        bucket = np.where(n < max_exact, n, large)
        tab = np.where(valid, bucket, -1).astype(np.int32)
        perm = _block_perm(dil)
        tabs.append(tab[perm][:, np.concatenate([perm, BLK + perm])])
    return np.stack(tabs)


def _bias_table(rel_bias, buckets):
    def body(rb_ref, bk_ref, out_ref):
        gh = pl.program_id(0)
        bk = bk_ref[...]
        acc = jnp.full((BLK, 2 * BLK), NEG, F32)
        for b in range(NUM_BUCKETS):
            acc = jnp.where(bk == b, rb_ref[b, gh], acc)
        out_ref[...] = acc

    return pl.pallas_call(
        body, name="bias_table", grid=(NG * NH,),
        in_specs=[pl.BlockSpec(memory_space=pltpu.SMEM),
                  pl.BlockSpec((None, BLK, 2 * BLK), lambda gh: (gh // NH, 0, 0))],
        out_specs=pl.BlockSpec((None, BLK, 2 * BLK), lambda gh: (gh, 0, 0)),
        out_shape=_sds((NG * NH, BLK, 2 * BLK)),
        compiler_params=_params("arbitrary"),
    )(rel_bias, buckets)


def _bias_grad(ds_acc, buckets, ride):
    def body(acc_ref, bk_ref, out_ref):
        bk = bk_ref[...]
        acc = acc_ref[...]
        lane = lax.broadcasted_iota(jnp.int32, (8, 128), 1)
        out = jnp.zeros((8, 128), F32)
        for b in range(NUM_BUCKETS):
            val = jnp.sum(jnp.where(bk == b, acc, 0.0))
            out = jnp.where(lane == b, val, out)
        out_ref[...] = out

    (out,), rode = _call_with_ride(
        body, ride, lambda: pl.program_id(0) == 0, lambda: pl.program_id(0) == NG * NH - 1,
        name="bias_grad", grid=(NG * NH,),
        in_specs=[pl.BlockSpec((None, BLK, 2 * BLK), lambda gh: (gh, 0, 0)),
                  pl.BlockSpec((None, BLK, 2 * BLK), lambda gh: (gh // NH, 0, 0))],
        out_specs=[pl.BlockSpec((None, 8, 128), lambda gh: (gh, 0, 0))],
        out_shape=[_sds((NG * NH, 8, 128))],
        compiler_params=_params("arbitrary"),
    )(ds_acc, buckets)
    return out, rode


def _mod_partial(c_all, w_ada_s, b_ada_s):
    def body(c_ref, w_ref, b_ref, o_ref):
        o_ref[...] = _dot(c_ref[...].astype(BF16), w_ref[...].astype(BF16)) + b_ref[...]

    return pl.pallas_call(body, name="mod_partial", out_shape=_sds((8, w_ada_s.shape[1])),
                          compiler_params=_params())(c_all, w_ada_s, b_ada_s)


def _prenorm(x, norm_g, mod):
    S = x.shape[0]
    tm = 512

    def body(x_ref, g_ref, mod_ref, h_ref):
        xv = x_ref[...]
        r = lax.rsqrt(jnp.mean(xv * xv, axis=-1, keepdims=True) + EPS)
        n1 = xv * r * g_ref[...]
        h_ref[...] = (n1 * (1.0 + mod_ref[:, D:2 * D]) + mod_ref[:, 0:D]).astype(BF16)

    return pl.pallas_call(
        body, name="prenorm", grid=(S // tm,),
        in_specs=[pl.BlockSpec((tm, D), lambda i: (i, 0)), pl.BlockSpec((1, D), lambda i: (0, 0)),
                  pl.BlockSpec((1, 3 * D), lambda i: (0, 0))],
        out_specs=pl.BlockSpec((tm, D), lambda i: (i, 0)),
        out_shape=_sds((S, D), BF16), compiler_params=_params("parallel"),
    )(x, norm_g, mod)


def _proj(h, wg_in, j0, nj, dtype, name):
    S = h.shape[0]
    tm = 2048
    per = wg_in.shape[2] // CB

    def body(h_ref, w_ref, o_ref):
        o_ref[...] = _dot(h_ref[...], w_ref[...]).astype(dtype)

    return pl.pallas_call(
        body, name=name, grid=(S // tm, nj),
        in_specs=[pl.BlockSpec((tm, D), lambda m, j: (m, 0)),
                  pl.BlockSpec((None, D, CB), lambda m, j: ((j0 + j) // per, 0, (j0 + j) % per))],
        out_specs=pl.BlockSpec((tm, CB), lambda m, j: (m, j)),
        out_shape=_sds((S, nj * CB), dtype), compiler_params=_params("parallel", "parallel"),
    )(h, wg_in)


HS = 4
SLAB = HS * HD


def _lane_head(rows):
    return lax.broadcasted_iota(jnp.int32, (rows, SLAB), 1) // HD


def _head_stack(a):
    head = _lane_head(a.shape[0])
    return jnp.concatenate([jnp.where(head == h, a, jnp.zeros_like(a)) for h in range(HS)], axis=0)


def _head_unstack(a):
    rows = a.shape[0] // HS
    head = _lane_head(rows)
    out = a[:rows]
    for h in range(1, HS):
        out = jnp.where(head == h, a[h * rows:(h + 1) * rows], out)
    return out


STAT_W = 128
VIEW = 16


def _sub_layout(dil):
    if dil == 1:
        return BLK, [None]
    return BLK * dil // VIEW, [[r + dil * u for u in range(VIEW // dil)] for r in range(dil)]


def _block_perm(dil):
    a_rows, _ = _sub_layout(dil)
    p = np.arange(BLK)
    return p if dil == 1 else (VIEW // dil) * (p % a_rows) + p // a_rows


LB = 128
N_SLAB = NH // HS


def _ld(refs, bs, s, w):
    if bs is None:
        return refs[0][:, s * w:(s + 1) * w]
    a_rows = refs[0].shape[0] // VIEW
    return jnp.concatenate([jnp.concatenate([ref[pl.ds(b, a_rows, stride=VIEW), :] for b in bs], axis=0)
                            for ref in refs], axis=1)


def _st(ref, bs, s, val):
    if bs is None:
        ref[:, s * SLAB:(s + 1) * SLAB] = val
        return
    a_rows = val.shape[0] // len(bs)
    for u, b in enumerate(bs):
        ref[:, b, :] = val[u * a_rows:(u + 1) * a_rows]


def _attn_views(dil, S):
    a_rows, subs = _sub_layout(dil)
    if dil == 1:
        def ispecs(base, w, f):
            return [pl.BlockSpec((BLK, N_SLAB * w), lambda sg, n: (f(n), base // (N_SLAB * w)))]
        return subs, S // BLK, N_SLAB, ispecs, (lambda w: (S, w)), (
            lambda f: pl.BlockSpec((BLK, AW), lambda sg, n: (f(n), 0)))

    def ispecs(base, w, f):
        return [pl.BlockSpec((a_rows * VIEW, LB), lambda sg, n, k=k: (f(n), (base + sg * w) // LB + k))
                for k in range(w // LB)]
    return subs, S // (a_rows * VIEW), 1, ispecs, (lambda w: (S // VIEW, VIEW, w)), (
        lambda f: pl.BlockSpec((a_rows, VIEW, SLAB), lambda sg, n: (f(n), 0, sg)))


def _attn_fwd(qkv_g, bias_tab, g):
    S = qkv_g.shape[0]
    subs, nbq, sps, ispecs, shape, ospec = _attn_views(GROUPS[g][1], S)
    cur, prev = (lambda n: n), (lambda n: jnp.maximum(n - 1, 0))
    in_specs = [ispecs(0, SLAB, cur), ispecs(AW, SLAB, prev), ispecs(AW, SLAB, cur), ispecs(2 * AW, SLAB, prev),
                ispecs(2 * AW, SLAB, cur)]
    nl = len(in_specs[0])

    def body(*refs):
        q, kp, kc, vp, vc = (refs[t * nl:(t + 1) * nl] for t in range(5))
        b_ref, o_ref, l_ref = refs[5 * nl:]
        n = pl.program_id(1)
        col = lax.broadcasted_iota(jnp.int32, (HS * BLK, 2 * BLK), 1)
        keep = (col >= BLK) | (n > 0)
        for s_ in range(sps):
            bias = b_ref[pl.ds(s_ * HS, HS)].reshape(HS * BLK, 2 * BLK)
            for bs in subs:
                kb = jnp.concatenate([_ld(kp, bs, s_, SLAB), _ld(kc, bs, s_, SLAB)], axis=0).astype(BF16)
                vb = jnp.concatenate([_ld(vp, bs, s_, SLAB), _ld(vc, bs, s_, SLAB)], axis=0).astype(BF16)
                s = _dot_nt(_head_stack(_ld(q, bs, s_, SLAB).astype(BF16)), kb) * (HD ** -0.5) + bias
                s = jnp.where(keep, s, NEG)
                m = jnp.max(s, axis=-1, keepdims=True)
                p = jnp.exp(s - m)
                den = jnp.sum(p, axis=-1, keepdims=True)
                _st(o_ref, bs, s_, _head_unstack(_dot(p.astype(BF16), vb) / den))
                _st(l_ref, bs, s_, _head_unstack(jnp.broadcast_to(m + jnp.log(den), (HS * BLK, SLAB))))

    out = _sds(shape(AW))
    o, l = pl.pallas_call(
        body, name=f"attn_fwd{g}", grid=(N_SLAB // sps, nbq),
        in_specs=sum(in_specs, []) + [pl.BlockSpec((sps * HS, BLK, 2 * BLK),
                                                   lambda sg, n: (g * (N_SLAB // sps) + sg, 0, 0))],
        out_specs=[ospec(cur), ospec(cur)],
        out_shape=[out, out], compiler_params=_params("parallel", "arbitrary"),
    )(*([qkv_g] * (5 * nl)), bias_tab)
    return o.reshape(S, AW), l.reshape(S, AW)


def _attn_bwd(qkv_g, dattn, stats, bias_tab, g, ride):
    S = qkv_g.shape[0]
    subs, nbq, sps, ispecs, shape, ospec = _attn_views(GROUPS[g][1], S)
    cur = lambda n: jnp.minimum(n, nbq - 1)
    prev = lambda n: jnp.clip(n - 1, 0, nbq - 1)
    late = lambda n: jnp.maximum(n - 1, 0)
    in_specs = [ispecs(0, SLAB, cur), ispecs(AW, SLAB, prev), ispecs(AW, SLAB, cur), ispecs(2 * AW, SLAB, prev),
                ispecs(2 * AW, SLAB, cur), ispecs(0, SLAB, cur), ispecs(0, STAT_W, cur)]
    nl = len(in_specs[0])

    def body(*refs):
        q, kp, kc, vp, vc, da = (refs[t * nl:(t + 1) * nl] for t in range(6))
        st_ref, b_ref, dq_ref, dk_ref, dv_ref, ds_ref, ck_ref, cv_ref = refs[6 * nl:]
        n = pl.program_id(1)

        @pl.when(n == 0)
        def _():
            ds_ref[...] = jnp.zeros_like(ds_ref)
            ck_ref[...] = jnp.zeros_like(ck_ref)
            cv_ref[...] = jnp.zeros_like(cv_ref)

        @pl.when(n < nbq)
        def _():
            col = lax.broadcasted_iota(jnp.int32, (HS * BLK, 2 * BLK), 1)
            keep = (col >= BLK) | (n > 0)
            for s_ in range(sps):
                cs = slice(s_ * SLAB, (s_ + 1) * SLAB)
                bias = b_ref[pl.ds(s_ * HS, HS)].reshape(HS * BLK, 2 * BLK)
                for i, bs in enumerate(subs):
                    st = _ld((st_ref,), bs, s_, STAT_W)
                    kb = jnp.concatenate([_ld(kp, bs, s_, SLAB), _ld(kc, bs, s_, SLAB)], axis=0).astype(BF16)
                    vb = jnp.concatenate([_ld(vp, bs, s_, SLAB), _ld(vc, bs, s_, SLAB)], axis=0).astype(BF16)
                    lse = jnp.concatenate([st[:, h:h + 1] for h in range(HS)], axis=0)
                    delta = jnp.concatenate([st[:, HS + h:HS + h + 1] for h in range(HS)], axis=0)
                    qs = _head_stack(_ld(q, bs, s_, SLAB).astype(BF16))
                    dos = _head_stack(_ld(da, bs, s_, SLAB).astype(BF16))
                    s = _dot_nt(qs, kb) * (HD ** -0.5) + bias
                    s = jnp.where(keep, s, NEG)
                    p = jnp.exp(s - lse)
                    ds = p * (_dot_nt(dos, vb) - delta)
                    ds_ref[pl.ds(s_ * HS, HS)] += ds.reshape(HS, BLK, 2 * BLK)
                    ds_b = (ds * (HD ** -0.5)).astype(BF16)
                    _st(dq_ref, bs, s_, _head_unstack(_dot(ds_b, kb)))
                    dkb = _dot_tn(ds_b, qs)
                    dvb = _dot_tn(p.astype(BF16), dos)
                    _st(dk_ref, bs, s_, ck_ref[i, :, cs] + dkb[:BLK])
                    _st(dv_ref, bs, s_, cv_ref[i, :, cs] + dvb[:BLK])
                    ck_ref[i, :, cs] = dkb[BLK:]
                    cv_ref[i, :, cs] = dvb[BLK:]

        @pl.when(n == nbq)
        def _():
            for s_ in range(sps):
                for i, bs in enumerate(subs):
                    _st(dk_ref, bs, s_, ck_ref[i, :, s_ * SLAB:(s_ + 1) * SLAB])
                    _st(dv_ref, bs, s_, cv_ref[i, :, s_ * SLAB:(s_ + 1) * SLAB])

    out = _sds(shape(AW))
    nsg = N_SLAB // sps
    (dq, dk, dv, ds_acc), rode = _call_with_ride(
        body, ride, lambda: (pl.program_id(0) == 0) & (pl.program_id(1) == 0),
        lambda: (pl.program_id(0) == nsg - 1) & (pl.program_id(1) == nbq),
        name=f"attn_bwd{g}", grid=(nsg, nbq + 1),
        in_specs=sum(in_specs, []) + [pl.BlockSpec((sps * HS, BLK, 2 * BLK), lambda sg, n: (g * nsg + sg, 0, 0))],
        out_specs=[ospec(cur), ospec(late), ospec(late),
                   pl.BlockSpec((sps * HS, BLK, 2 * BLK), lambda sg, n: (sg, 0, 0))],
        out_shape=[out] * 3 + [_sds((NH, BLK, 2 * BLK))],
        scratch_shapes=[pltpu.VMEM((len(subs), BLK, sps * SLAB), F32), pltpu.VMEM((len(subs), BLK, sps * SLAB), F32)],
        compiler_params=_params("arbitrary", "arbitrary"),
    )(*([qkv_g] * (5 * nl)), *([dattn] * nl), stats, bias_tab)
    return [dq.reshape(S, AW), dk.reshape(S, AW), dv.reshape(S, AW)], ds_acc, rode


TM_MIX = 256


def _mix_specs(tm):
    row512 = pl.BlockSpec((tm, AW), lambda i: (i, 0))
    return ([row512] * 6 + [
        pl.BlockSpec((tm, REST_W), lambda i: (i, 0)),
        pl.BlockSpec((HALO, AW), lambda i: (jnp.maximum(i * (tm // HALO) - 1, 0), 1)),
        pl.BlockSpec((AW, D), lambda i: (0, 0)), pl.BlockSpec((AW, D), lambda i: (0, 0)),
        pl.BlockSpec((4, PGW, PGW), lambda i: (0, 0, 0)), pl.BlockSpec((1, AW), lambda i: (0, 0))])


def _mix_forward(i, tm, o_refs, l_refs, rest_ref, halo_ref, wab_ref, wpb_ref, pw_ref, ps_ref):
    l0, l1, l2 = (r[...] for r in l_refs)
    mx = jnp.maximum(jnp.maximum(l0, l1), l2)
    e0, e1, e2 = jnp.exp(l0 - mx), jnp.exp(l1 - mx), jnp.exp(l2 - mx)
    den = e0 + e1 + e2
    lj = mx + jnp.log(den)
    attn = (e0 * o_refs[0][...] + e1 * o_refs[1][...] + e2 * o_refs[2][...]) / den

    z_attn = rest_ref[:, 0:AW]
    u = rest_ref[:, AW:2 * AW]
    z_pool = rest_ref[:, 2 * AW:3 * AW]
    g_attn = rest_ref[:, 3 * AW:3 * AW + D]
    g_pool = rest_ref[:, 3 * AW + D:3 * AW + 2 * D]

    sg_a = _sigmoid(z_attn)
    sil_a = z_attn * sg_a
    a_g = (attn * sil_a).astype(BF16)
    y_attn = _dot(a_g, wab_ref[...])

    halo = jnp.where(i > 0, halo_ref[...], 0.0)
    ext = jnp.concatenate([halo, u], axis=0)
    t = i * tm + lax.broadcasted_iota(jnp.int32, (tm, 1), 0)
    pooled, mixed_raw = [], []
    for gi, win in enumerate(POOL_WINDOWS):
        s = ext[:, gi * PGW:(gi + 1) * PGW]
        sh = 1
        while sh < win:
            s = s + pltpu.roll(s, sh, 0)
            sh *= 2
        cnt = jnp.minimum(t + 1, win).astype(F32)
        pg = s[HALO:] / cnt - u[:, gi * PGW:(gi + 1) * PGW]
        pooled.append(pg.astype(BF16))
        mixed_raw.append(_dot(pooled[-1], pw_ref[gi].astype(BF16)))
    mixed_raw = jnp.concatenate(mixed_raw, axis=1)
    mixed = mixed_raw * ps_ref[...]
    sg_p = _sigmoid(z_pool)
    sil_p = z_pool * sg_p
    m_g = (mixed * sil_p).astype(BF16)
    y_pool = _dot(m_g, wpb_ref[...])

    sa = _sigmoid(g_attn)
    sp = _sigmoid(g_pool)
    merged = sa * y_attn + sp * y_pool
    return dict(lj=lj, attn=attn, z_attn=z_attn, z_pool=z_pool, sg_a=sg_a, sil_a=sil_a, a_g=a_g, y_attn=y_attn,
                pooled=pooled, mixed_raw=mixed_raw, mixed=mixed, sg_p=sg_p, sil_p=sil_p, m_g=m_g, y_pool=y_pool,
                sa=sa, sp=sp, merged=merged)


def _tail(x, target, os_, ls_, rest, wab, wpb, pool_w, pool_scale, wout, mod, final_g):
    S = x.shape[0]
    tm = TM_MIX

    def body(o0, o1, o2, l0, l1, l2, rest_ref, halo_ref, wab_ref, wpb_ref, pw_ref, ps_ref,
             x_ref, t_ref, wo_ref, mod_ref, fg_ref, dx2_ref, dmo_ref, loss_ref, dfg_ref, dgate_ref):
        i = pl.program_id(0)

        @pl.when(i == 0)
        def _():
            loss_ref[...] = jnp.zeros_like(loss_ref)
            dfg_ref[...] = jnp.zeros_like(dfg_ref)
            dgate_ref[...] = jnp.zeros_like(dgate_ref)

        f = _mix_forward(i, tm, (o0, o1, o2), (l0, l1, l2), rest_ref, halo_ref, wab_ref, wpb_ref, pw_ref, ps_ref)
        mo = _dot(f["merged"].astype(BF16), wo_ref[...])
        gate = mod_ref[:, 2 * D:3 * D]
        fg = fg_ref[...]
        x2 = x_ref[...] + gate * mo
        r2 = lax.rsqrt(jnp.mean(x2 * x2, axis=-1, keepdims=True) + EPS)
        n2 = x2 * r2
        err = n2 * fg - t_ref[...]
        loss_ref[...] += 0.5 * jnp.sum(jnp.mean(err * err, axis=-1, keepdims=True))
        dy = err * (1.0 / D)
        dfg_ref[...] += jnp.sum(dy * n2, axis=0, keepdims=True)
        dn = dy * fg
        dx2 = r2 * (dn - n2 * jnp.mean(dn * n2, axis=-1, keepdims=True))
        dgate_ref[...] += jnp.sum(dx2 * mo, axis=0, keepdims=True)
        dx2_ref[...] = dx2
        dmo_ref[...] = (dx2 * gate).astype(BF16)

    row = pl.BlockSpec((tm, D), lambda i: (i, 0))
    vec = pl.BlockSpec((1, D), lambda i: (0, 0))
    return pl.pallas_call(
        body, name="tail", grid=(S // tm,),
        in_specs=_mix_specs(tm) + [row, row, pl.BlockSpec((D, D), lambda i: (0, 0)),
                                   pl.BlockSpec((1, 3 * D), lambda i: (0, 0)), vec],
        out_specs=[row, row, pl.BlockSpec((8, 128), lambda i: (0, 0)), vec, vec],
        out_shape=[_sds((S, D)), _sds((S, D), BF16), _sds((8, 128)), _sds((1, D)), _sds((1, D))],
        compiler_params=_params("arbitrary"),
    )(*os_, *ls_, rest, rest, wab, wpb, pool_w, pool_scale, x, target, wout, mod, final_g)


def _mix_bwd(dmo, os_, ls_, rest, wab, wpb, pool_w, pool_scale, wout):
    S = dmo.shape[0]
    tm = TM_MIX
    nt = S // tm
    sw = D // N_SHARD

    def body(o0, o1, o2, l0, l1, l2, rest_ref, halo_ref, wab_ref, wpb_ref, pw_ref, ps_ref, dmo_ref, wo_ref,
             dattn_ref, stats_ref, dpooled_ref, drest_ref, dwo_hbm, dwab_hbm, dwpb_hbm, dpw_ref, dps_ref,
             awo, awab, awpb):
        i = pl.program_id(0)

        @pl.when(i == 0)
        def _():
            awo[...] = jnp.zeros_like(awo)
            awab[...] = jnp.zeros_like(awab)
            awpb[...] = jnp.zeros_like(awpb)
            dpw_ref[...] = jnp.zeros_like(dpw_ref)
            dps_ref[...] = jnp.zeros_like(dps_ref)

        f = _mix_forward(i, tm, (o0, o1, o2), (l0, l1, l2), rest_ref, halo_ref, wab_ref, wpb_ref, pw_ref, ps_ref)
        dmo_b = dmo_ref[...]
        dmerged = _dot_nt(dmo_b, wo_ref[...])
        awo[...] += _dot_tn(f["merged"].astype(BF16), dmo_b)
        sa, sp = f["sa"], f["sp"]
        dya = (dmerged * sa).astype(BF16)
        dyp = (dmerged * sp).astype(BF16)
        dg_attn = dmerged * f["y_attn"] * sa * (1.0 - sa)
        dg_pool = dmerged * f["y_pool"] * sp * (1.0 - sp)
        dag = _dot_nt(dya, wab_ref[...])
        awab[...] += _dot_tn(f["a_g"], dya)
        dmg = _dot_nt(dyp, wpb_ref[...])
        awpb[...] += _dot_tn(f["m_g"], dyp)
        dattn = dag * f["sil_a"]
        dattn_ref[...] = dattn
        prod = dattn * f["attn"]
        lane = lax.broadcasted_iota(jnp.int32, (tm, STAT_W), 1)
        for sb in range(N_SLAB):
            st = jnp.zeros((tm, STAT_W), F32)
            for h in range(HS):
                hs = slice((sb * HS + h) * HD, (sb * HS + h + 1) * HD)
                st = jnp.where(lane == h, f["lj"][:, hs.start:hs.start + 1], st)
                st = jnp.where(lane == HS + h, jnp.sum(prod[:, hs], axis=-1, keepdims=True), st)
            stats_ref[:, sb * STAT_W:(sb + 1) * STAT_W] = st
        dz_attn = dag * f["attn"] * (f["sg_a"] * (1.0 + f["z_attn"] * (1.0 - f["sg_a"])))
        dmixed = dmg * f["sil_p"]
        dz_pool = dmg * f["mixed"] * (f["sg_p"] * (1.0 + f["z_pool"] * (1.0 - f["sg_p"])))
        dps_ref[...] += jnp.sum(dmixed * f["mixed_raw"], axis=0, keepdims=True)
        dpm = (dmixed * ps_ref[...]).astype(BF16)
        for gi in range(len(POOL_WINDOWS)):
            cs = slice(gi * PGW, (gi + 1) * PGW)
            dpw_ref[gi] += _dot_tn(f["pooled"][gi], dpm[:, cs])
            dpooled_ref[:, cs] = _dot_nt(dpm[:, cs], pw_ref[gi].astype(BF16))
        drest_ref[:, 0:AW] = dz_attn.astype(BF16)
        drest_ref[:, AW:2 * AW] = jnp.zeros((tm, AW), BF16)
        drest_ref[:, 2 * AW:3 * AW] = dz_pool.astype(BF16)
        drest_ref[:, 3 * AW:3 * AW + D] = dg_attn.astype(BF16)
        drest_ref[:, 3 * AW + D:3 * AW + 2 * D] = dg_pool.astype(BF16)

        @pl.when(i == nt - 1)
        def _():
            pltpu.sync_copy(awo, dwo_hbm)
            for k in range(N_SHARD):
                pltpu.sync_copy(awab.at[:, pl.ds(k * sw, sw)], dwab_hbm.at[k])
                pltpu.sync_copy(awpb.at[:, pl.ds(k * sw, sw)], dwpb_hbm.at[k])

    row512 = pl.BlockSpec((tm, AW), lambda i: (i, 0))
    outs = pl.pallas_call(
        body, name="mix_bwd", grid=(nt,),
        in_specs=_mix_specs(tm) + [pl.BlockSpec((tm, D), lambda i: (i, 0)), pl.BlockSpec((D, D), lambda i: (0, 0))],
        out_specs=[row512, pl.BlockSpec((tm, N_SLAB * STAT_W), lambda i: (i, 0)), row512,
                   pl.BlockSpec((tm, REST_W), lambda i: (i, 0)), ANY, ANY, ANY,
                   pl.BlockSpec((4, PGW, PGW), lambda i: (0, 0, 0)), pl.BlockSpec((1, AW), lambda i: (0, 0))],
        out_shape=[_sds((S, AW)), _sds((S, N_SLAB * STAT_W)), _sds((S, AW)), _sds((S, REST_W), BF16),
                   _sds((D, D)), _sds((N_SHARD, AW, sw)), _sds((N_SHARD, AW, sw)), _sds((4, PGW, PGW)), _sds((1, AW))],
        scratch_shapes=[pltpu.VMEM((D, D), F32), pltpu.VMEM((AW, D), F32), pltpu.VMEM((AW, D), F32)],
        compiler_params=_params("arbitrary"),
    )(*os_, *ls_, rest, rest, wab, wpb, pool_w, pool_scale, dmo, wout)
    dattn, stats, dpooled, drest, dwo, dwab, dwpb, dpw, dps = outs
    return dattn, stats, dpooled, drest, dwo.reshape(N_SHARD, D // N_SHARD, D), dwab, dwpb, dpw, dps


def _pool_bwd(dpooled):
    S = dpooled.shape[0]
    tm = 512
    nt = S // tm

    def body(dp_ref, nxt_ref, du_ref):
        i = pl.program_id(0)
        t = i * tm + lax.broadcasted_iota(jnp.int32, (tm + HALO, 1), 0)
        nxt = jnp.where(i < nt - 1, nxt_ref[...], 0.0)
        ext = jnp.concatenate([dp_ref[...], nxt], axis=0)
        for gi, win in enumerate(POOL_WINDOWS):
            cs = slice(gi * PGW, (gi + 1) * PGW)
            s = ext[:, cs] / jnp.minimum(t + 1, win).astype(F32)
            sh = 1
            while sh < win:
                s = s + pltpu.roll(s, tm + HALO - sh, 0)
                sh *= 2
            du_ref[:, cs] = (s[:tm] - dp_ref[:, cs]).astype(BF16)

    return pl.pallas_call(
        body, name="pool_bwd", grid=(nt,),
        in_specs=[pl.BlockSpec((tm, AW), lambda i: (i, 0)),
                  pl.BlockSpec((HALO, AW), lambda i: (jnp.minimum((i + 1) * (tm // HALO), S // HALO - 1), 0))],
        out_specs=pl.BlockSpec((tm, AW), lambda i: (i, 0)),
        out_shape=_sds((S, AW), BF16), compiler_params=_params("parallel"),
    )(dpooled, dpooled)


TB = 1024


def _dh(dproj, wg_in, ride):
    S = dproj.shape[0]
    per = wg_in.shape[2] // TB
    nm, nk = S // TB, IN_W // TB

    def body(dp_ref, w_ref, out_ref):
        @pl.when(pl.program_id(1) == 0)
        def _():
            out_ref[...] = jnp.zeros_like(out_ref)

        out_ref[...] += _dot_nt(dp_ref[...], w_ref[...])

    (dh,), rode = _call_with_ride(
        body, ride, lambda: (pl.program_id(0) == 0) & (pl.program_id(1) == 0),
        lambda: (pl.program_id(0) == nm - 1) & (pl.program_id(1) == nk - 1),
        name="dh", grid=(nm, nk),
        in_specs=[pl.BlockSpec((TB, TB), lambda m, kk: (m, kk)),
                  pl.BlockSpec((None, D, TB), lambda m, kk: (kk // per, 0, kk % per))],
        out_specs=[pl.BlockSpec((TB, D), lambda m, kk: (m, 0))],
        out_shape=[_sds((S, D))], compiler_params=_params("arbitrary", "arbitrary"),
    )(dproj, wg_in)
    return dh, rode


def _dw_in(h, dproj):
    S = dproj.shape[0]
    per = IN_W // N_SHARD // TB

    def body(h_ref, dp_ref, out_ref):
        @pl.when(pl.program_id(1) == 0)
        def _():
            out_ref[...] = jnp.zeros_like(out_ref)

        out_ref[...] += _dot_tn(h_ref[...], dp_ref[...])

    return pl.pallas_call(
        body, name="dw_in", grid=(IN_W // TB, S // TB),
        in_specs=[pl.BlockSpec((TB, D), lambda j, kk: (kk, 0)), pl.BlockSpec((TB, TB), lambda j, kk: (kk, j))],
        out_specs=pl.BlockSpec((None, D, TB), lambda j, kk: (j // per, 0, j % per)),
        out_shape=_sds((N_SHARD, D, IN_W // N_SHARD)), compiler_params=_params("parallel", "arbitrary"),
    )(h, dproj)


def _prenorm_bwd(x, dh, dx2, norm_g, mod):
    S = x.shape[0]
    tm = 512

    def body(x_ref, dh_ref, dx2_ref, g_ref, mod_ref, gx_ref, dg_ref, dshift_ref, dscale_ref):
        i = pl.program_id(0)

        @pl.when(i == 0)
        def _():
            dg_ref[...] = jnp.zeros_like(dg_ref)
            dshift_ref[...] = jnp.zeros_like(dshift_ref)
            dscale_ref[...] = jnp.zeros_like(dscale_ref)

        xv = x_ref[...]
        dhv = dh_ref[...]
        g = g_ref[...]
        r = lax.rsqrt(jnp.mean(xv * xv, axis=-1, keepdims=True) + EPS)
        xh = xv * r
        dshift_ref[...] += jnp.sum(dhv, axis=0, keepdims=True)
        dscale_ref[...] += jnp.sum(dhv * (xh * g), axis=0, keepdims=True)
        dn1 = dhv * (1.0 + mod_ref[:, D:2 * D])
        dg_ref[...] += jnp.sum(dn1 * xh, axis=0, keepdims=True)
        dxh = dn1 * g
        gx_ref[...] = dx2_ref[...] + r * (dxh - xh * jnp.mean(dxh * xh, axis=-1, keepdims=True))

    row = pl.BlockSpec((tm, D), lambda i: (i, 0))
    vec = pl.BlockSpec((1, D), lambda i: (0, 0))
    return pl.pallas_call(
        body, name="prenorm_bwd", grid=(S // tm,),
        in_specs=[row, row, row, vec, pl.BlockSpec((1, 3 * D), lambda i: (0, 0))],
        out_specs=[row, vec, vec, vec],
        out_shape=[_sds((S, D)), _sds((1, D)), _sds((1, D)), _sds((1, D))],
        compiler_params=_params("arbitrary"),
    )(x, dh, dx2, norm_g, mod)


def _local_step(x, target, mod, wg_in, wab, wpb, wout, pool_w, pool_scale, rel_bias, norm_g, final_g, half_idx,
                chip_half):
    buckets = jnp.asarray(_bucket_tables())
    bias_tab = _bias_table(rel_bias, buckets)
    h = _prenorm(x, norm_g, mod)
    qkv = [_proj(h, wg_in, 3 * g, 3, F32, f"proj_qkv{g}") for g in range(NG)]
    rest = _proj(h, wg_in, NCB_QKV, REST_W // CB, F32, "proj_rest")
    os_, ls_ = zip(*[_attn_fwd(qkv[g], bias_tab, g) for g in range(NG)])
    dx2, dmo, loss, dfinal_g, dgate = _tail(x, target, os_, ls_, rest, wab, wpb, pool_w, pool_scale, wout, mod, final_g)
    dattn, stats, dpooled, drest, dw_out, dw_ab, dw_pb, dpool_w, dpool_scale = _mix_bwd(
        dmo, os_, ls_, rest, wab, wpb, pool_w, pool_scale, wout)
    du = _pool_bwd(dpooled)

    small = [dw_ab, dw_pb, dw_out]
    dqkv0, ds0, sib_small = _attn_bwd(qkv[0], dattn, stats, bias_tab, 0, _ride_sibling_halves(small))
    p_small = [_pair_sum(g, t, half_idx, f"rs_pair_sum{a}") for a, (g, t) in enumerate(zip(small, sib_small))]
    dqkv1, ds1, u_small = _attn_bwd(qkv[1], dattn, stats, bias_tab, 1,
                                    _ride_chip_exchange([p16 for _, p16 in p_small]))
    rs_ab, rs_pb, rs_out = [_chip_sum(p32, u, chip_half, f"rs_chip_sum{a}")
                            for a, ((p32, _), u) in enumerate(zip(p_small, u_small))]
    dqkv2, ds2, _ = _attn_bwd(qkv[2], dattn, stats, bias_tab, 2, None)

    dproj = jnp.concatenate([a.astype(BF16) for a in dqkv0 + dqkv1 + dqkv2] + [drest[:, :AW], du, drest[:, 2 * AW:]],
                            axis=1)
    dw_in = _dw_in(h, dproj)
    drel_rows, (sib_in,) = _bias_grad(jnp.concatenate([ds0, ds1, ds2], axis=0), buckets,
                                      _ride_sibling_halves([dw_in]))
    drel = drel_rows[:, 0, :NUM_BUCKETS].T
    p32_in, p16_in = _pair_sum(dw_in, sib_in, half_idx, "rs_pair_sum_in")
    dh, (u_in,) = _dh(dproj, wg_in, _ride_chip_exchange([p16_in]))
    rs_in = _chip_sum(p32_in, u_in, chip_half, "rs_chip_sum_in")

    grad_x, dnorm_g, dshift, dscale = _prenorm_bwd(x, dh, dx2, norm_g, mod)
    dmod = jnp.concatenate([dshift, dscale, dgate], axis=1)
    return dict(loss=loss[0, 0], grad_x=grad_x, dmod=dmod, dnorm_g=dnorm_g, dfinal_g=dfinal_g, dpool_w=dpool_w,
                dpool_scale=dpool_scale, drel_bias=drel, dw_in=dw_in, dw_attn_br=dw_ab, dw_pool_br=dw_pb,
                dw_out=dw_out, rs_in=rs_in, rs_attn_br=rs_ab, rs_pool_br=rs_pb, rs_out=rs_out)


def _allgather8(blocks, name, relay=None):
    nb = len(blocks)
    relay = [False] * nb if relay is None else list(relay)

    def body(*refs):
        ins, outs = refs[:nb], refs[nb:2 * nb]
        send_sems, recv_sems, local_sems = refs[2 * nb:]
        x, y, c = lax.axis_index("x"), lax.axis_index("y"), lax.axis_index("c")
        me, sibling = (x, y, c), (x, y, 1 - c)
        here, xn, yn, dg = (x, y), (1 - x, y), (x, 1 - y), (1 - x, 1 - y)

        def slot(a, chip, core, half=None):
            ref = outs[a].at[4 * chip[0] + 2 * chip[1] + core]
            if half is None:
                return ref
            r2 = ref.shape[0] // 2
            return ref.at[pl.ds(half * r2, r2)]

        def copy(a, k, dst, to, src=None):
            return pltpu.make_async_remote_copy(src_ref=dst if src is None else src, dst_ref=dst,
                                                send_sem=send_sems.at[a, k], recv_sem=recv_sems.at[a, k],
                                                device_id=to, device_id_type=MESH)

        def start(cps):
            for cp in cps:
                cp.start()
            return cps

        mine = start([pltpu.make_async_copy(ins[a], slot(a, here, c), local_sems.at[a]) for a in range(nb)])
        sent = []
        for a in range(nb):
            own = slot(a, here, c)
            sent += [copy(a, 0, own, sibling, src=ins[a]), copy(a, 1, own, (*xn, c), src=ins[a]),
                     copy(a, 2, own, (*yn, c), src=ins[a])]
            if not relay[a]:
                sent.append(copy(a, 3, own, (*dg, c), src=ins[a]))
        start(sent)
        for a in range(nb):
            copy(a, 2, slot(a, yn, c), me).wait_recv()
            sent += start([copy(a, 6, slot(a, yn, c), sibling)]
                          + ([copy(a, 3, slot(a, yn, c, 0), (*xn, c))] if relay[a] else []))
        for a in range(nb):
            copy(a, 1, slot(a, xn, c), me).wait_recv()
            sent += start([copy(a, 5, slot(a, xn, c), sibling)]
                          + ([copy(a, 4, slot(a, xn, c, 1), (*yn, c))] if relay[a] else []))
        for a in range(nb):
            for k, half in ((3, 0), (4, 1)) if relay[a] else ((3, None),):
                copy(a, k, slot(a, dg, c, half), me).wait_recv()
                sent += start([copy(a, 4 + k, slot(a, dg, c, half), sibling)])
        for a in range(nb):
            copy(a, 0, slot(a, here, 1 - c), me).wait_recv()
            copy(a, 5, slot(a, xn, 1 - c), me).wait_recv()
            copy(a, 6, slot(a, yn, 1 - c), me).wait_recv()
            for k, half in ((7, 0), (8, 1)) if relay[a] else ((7, None),):
                copy(a, k, slot(a, dg, 1 - c, half), me).wait_recv()
        for cp in sent:
            cp.wait_send()
        for cp in mine:
            cp.wait()

    return pl.pallas_call(
        body, name=name, in_specs=[ANY] * nb, out_specs=[ANY] * nb,
        out_shape=[_sds((8,) + b.shape, b.dtype) for b in blocks],
        scratch_shapes=[_dma_sems(nb, 9), _dma_sems(nb, 9), _dma_sems(nb)],
    )(*blocks)


def _ride_sibling_halves(gs):
    def copies(ins, outs, send_sems, recv_sems):
        x, y, c = lax.axis_index("x"), lax.axis_index("y"), lax.axis_index("c")
        cps = []
        for a in range(len(gs)):
            r2 = ins[a].shape[1] // 2
            other = ins[a].at[:, pl.ds((1 - c) * r2, r2), :]
            cps.append(pltpu.make_async_remote_copy(src_ref=other, dst_ref=outs[a], send_sem=send_sems.at[a],
                                                    recv_sem=recv_sems.at[a], device_id=(x, y, 1 - c),
                                                    device_id_type=MESH))
        return cps

    return _Ride(gs, [_sds((g.shape[0], g.shape[1] // 2, g.shape[2]), g.dtype) for g in gs], len(gs), copies)


def _pair_sum(g, t, half, name):
    nsh, rows, cols = g.shape
    r2 = rows // 2
    tr = _row_tile(r2, cols)
    nt = r2 // tr

    def body(half_ref, g_ref, t_ref, p32_ref, p16_ref):
        p = g_ref[...] + t_ref[...]
        p32_ref[...] = p
        p16_ref[...] = p.astype(BF16)

    blk = pl.BlockSpec((None, tr, cols), lambda k, i, half_ref: (k, i, 0))
    return pl.pallas_call(
        body, name=name,
        grid_spec=pltpu.PrefetchScalarGridSpec(
            num_scalar_prefetch=1, grid=(nsh, nt),
            in_specs=[pl.BlockSpec((None, tr, cols), lambda k, i, half_ref: (k, half_ref[0] * nt + i, 0)), blk],
            out_specs=[blk, blk]),
        out_shape=[_sds((nsh, r2, cols)), _sds((nsh, r2, cols), BF16)],
        compiler_params=_params("parallel", "parallel"),
    )(half, g, t)


def _ride_chip_exchange(ps):
    def copies(ins, outs, send_sems, recv_sems):
        x, y, c = lax.axis_index("x"), lax.axis_index("y"), lax.axis_index("c")
        chips = [(1 - x, y), (x, 1 - y), (1 - x, 1 - y)]
        cps = []
        for a in range(len(ps)):
            for j, (ox, oy) in enumerate(chips):
                cps.append(pltpu.make_async_remote_copy(src_ref=ins[a].at[2 * ox + oy], dst_ref=outs[a].at[j],
                                                        send_sem=send_sems.at[3 * a + j],
                                                        recv_sem=recv_sems.at[3 * a + j],
                                                        device_id=(ox, oy, c), device_id_type=MESH))
        return cps

    return _Ride(ps, [_sds((3,) + p.shape[1:], p.dtype) for p in ps], 3 * len(ps), copies)


def _chip_sum(p32, u, chip_half, name):
    r2, cols = p32.shape[1:]
    tr = _row_tile(r2, cols)
    nt = r2 // tr

    def body(ch_ref, p_ref, u_ref, o_ref):
        acc = p_ref[...]
        for j in range(3):
            acc = acc + u_ref[j].astype(F32)
        o_ref[...] = acc

    return pl.pallas_call(
        body, name=name,
        grid_spec=pltpu.PrefetchScalarGridSpec(
            num_scalar_prefetch=1, grid=(nt,),
            in_specs=[pl.BlockSpec((None, tr, cols), lambda i, ch_ref: (ch_ref[0], i, 0)),
                      pl.BlockSpec((3, tr, cols), lambda i, ch_ref: (0, i, 0))],
            out_specs=pl.BlockSpec((tr, cols), lambda i, ch_ref: (ch_ref[1] * nt + i, 0))),
        out_shape=_sds((2 * r2, cols)), compiler_params=_params("parallel"),
    )(chip_half, p32, u)


def _sibling_join(fs, name):
    nb = len(fs)

    def body(*refs):
        outs = refs[nb:2 * nb]
        send_sems, recv_sems = refs[2 * nb:]
        x, y, c = lax.axis_index("x"), lax.axis_index("y"), lax.axis_index("c")
        cps = []
        for a in range(nb):
            r2 = outs[a].shape[0] // 2
            rows = outs[a].at[pl.ds(c * r2, r2), :]
            cps.append(pltpu.make_async_remote_copy(src_ref=rows, dst_ref=rows, send_sem=send_sems.at[a],
                                                    recv_sem=recv_sems.at[a], device_id=(x, y, 1 - c),
                                                    device_id_type=MESH))
        for cp in cps:
            cp.start()
        for cp in cps:
            cp.wait()

    return pl.pallas_call(
        body, name=name, in_specs=[ANY] * nb, out_specs=[ANY] * nb,
        out_shape=[_sds(f.shape, f.dtype) for f in fs],
        input_output_aliases={a: a for a in range(nb)},
        scratch_shapes=[_dma_sems(nb), _dma_sems(nb)],
    )(*fs)


def _row_tile(rows, cols):
    tile = rows
    while tile * cols * 4 > (1 << 20) and tile % 16 == 0:
        tile //= 2
    return tile


def _w_ada_grad(c_all, dmod_cols):
    def body(c_ref, d_ref, o_ref):
        o_ref[...] = _dot_tn(c_ref[...].astype(BF16), d_ref[...].astype(BF16))

    return pl.pallas_call(body, name="w_ada_grad", out_shape=_sds((c_all.shape[1], dmod_cols.shape[1])),
                          compiler_params=_params())(c_all, dmod_cols)


def _adam_math(w, g, m, v):
    nm = ADAM_B1 * m + (1.0 - ADAM_B1) * g
    nv = ADAM_B2 * v + (1.0 - ADAM_B2) * (g * g)
    m_hat = nm / (1.0 - ADAM_B1 ** ADAM_STEP)
    v_hat = nv / (1.0 - ADAM_B2 ** ADAM_STEP)
    return -ADAM_LR * (m_hat / (jnp.sqrt(v_hat) + ADAM_EPS) + ADAM_WD * w), nm, nv


def _adamw(w, g, m, v, name):
    rows, cols = w.shape
    tr = _row_tile(rows, cols)

    def body(w_ref, g_ref, m_ref, v_ref, d_ref, nm_ref, nv_ref):
        d_ref[...], nm_ref[...], nv_ref[...] = _adam_math(w_ref[...], g_ref[...], m_ref[...], v_ref[...])

    spec = pl.BlockSpec((tr, cols), lambda i: (i, 0))
    return pl.pallas_call(
        body, name=name, grid=(rows // tr,), in_specs=[spec] * 4, out_specs=[spec] * 3,
        out_shape=[_sds((rows, cols))] * 3, compiler_params=_params("parallel"),
    )(w, g, m, v)


def _pack_small(dmod, dnorm_g, dfinal_g, dpool_scale, drel_bias, loss, dpool_w):
    return jnp.concatenate([dmod.reshape(-1, 128), dnorm_g.reshape(-1, 128), dfinal_g.reshape(-1, 128),
                            jnp.pad(dpool_scale.reshape(-1, 128), ((0, PK_RELB - PK_PSCALE - AW // 128), (0, 0))),
                            jnp.pad(drel_bias, ((0, 0), (0, 128 - NG * NH))),
                            jnp.full((PK_POOLW - PK_LOSS, 128), loss, F32), dpool_w.reshape(-1, 128)], axis=0)


def _small_update(small_all, ws, ms, vs):
    lane_rows = [(r0, r0 + w.shape[1] // 128) for r0, w in zip((PK_BADA, PK_NORMG, PK_FINALG, PK_PSCALE), ws)]
    nw = len(ws)

    def body(all_ref, *refs):
        w_refs, m_refs, v_refs = refs[:nw], refs[nw:2 * nw], refs[2 * nw:3 * nw]
        loss_ref, outs = refs[3 * nw], refs[3 * nw + 1:]
        g = all_ref[0]
        for s in range(1, all_ref.shape[0]):
            g = g + all_ref[s]
        loss_ref[...] = jnp.broadcast_to(g[PK_LOSS:PK_LOSS + 1, :], loss_ref.shape)

        def put(p, at, gv):
            d, nm, nv = _adam_math(w_refs[p][at], gv, m_refs[p][at], v_refs[p][at])
            for o_ref, val in zip(outs[4 * p:4 * p + 4], (gv, d, nm, nv)):
                o_ref[at] = val

        for p, (r0, r1) in enumerate(lane_rows):
            for i in range(r1 - r0):
                put(p, (slice(None), slice(128 * i, 128 * (i + 1))), g[r0 + i:r0 + i + 1, :])
        put(4, (slice(None), slice(None)), g[PK_RELB:PK_LOSS, 0:NG * NH])
        put(5, (slice(None), slice(None)), g[PK_POOLW:PK_ROWS, :])

    res = pl.pallas_call(
        body, name="small_update",
        out_shape=[_sds((8, 128))] + [_sds(w.shape) for w in ws for _ in range(4)], compiler_params=_params(),
    )(small_all, *ws, *ms, *vs)
    return res[0], [res[1 + 4 * p:5 + 4 * p] for p in range(nw)]


def kernel(x, c, norm_g, w_ada, b_ada, w_in, pool_w, pool_scale, w_attn_br, w_pool_br, w_out, rel_bias, final_g, loss_target, m_norm_g, m_w_ada, m_b_ada, m_w_in, m_pool_w, m_pool_scale, m_w_attn_br, m_w_pool_br, m_w_out, m_rel_bias, m_final_g, v_norm_g, v_w_ada, v_b_ada, v_w_in, v_pool_w, v_pool_scale, v_w_attn_br, v_w_pool_br, v_w_out, v_rel_bias, v_final_g):
    ix, iy, ic = lax.axis_index("x"), lax.axis_index("y"), lax.axis_index("c")
    dev = 4 * ix + 2 * iy + ic
    chip = 2 * ix + iy

    def half(w):
        r2 = w.shape[0] // 2
        return lax.dynamic_slice_in_dim(w, ic * r2, r2, axis=0).astype(BF16)

    gathered = _allgather8([jnp.broadcast_to(c, (8, D)), half(w_in[0]), half(w_attn_br[0]), half(w_pool_br[0]),
                            half(w_out[0])], "gather_weights", relay=[False, True, True, True, True])
    c_all = gathered[0][:, 0, :]
    wg_in = gathered[1].reshape(N_SHARD, D, IN_W // N_SHARD)
    wab = gathered[2].reshape(N_SHARD, AW, D // N_SHARD).transpose(1, 0, 2).reshape(AW, D)
    wpb = gathered[3].reshape(N_SHARD, AW, D // N_SHARD).transpose(1, 0, 2).reshape(AW, D)
    wout = gathered[4].reshape(D, D)

    mw = 3 * D // N_SHARD
    modp = _mod_partial(c_all, w_ada[0], lax.dynamic_slice_in_dim(b_ada, chip * mw, mw, axis=1))
    mod_all = _allgather8([modp], "gather_mod")[0]
    mod_full = mod_all[::2].transpose(1, 0, 2).reshape(8, 3 * D)
    mod = lax.dynamic_slice_in_dim(mod_full, dev, 1, axis=0)

    half_idx = jnp.stack([ic]).astype(jnp.int32)
    chip_half = jnp.stack([chip, ic]).astype(jnp.int32)
    r = _local_step(x[0], loss_target[0], mod, wg_in, wab, wpb, wout, pool_w[0], pool_scale, rel_bias, norm_g,
                    final_g.reshape(1, D), half_idx, chip_half)

    packed = _pack_small(r["dmod"], r["dnorm_g"], r["dfinal_g"], r["dpool_scale"], r["drel_bias"], r["loss"],
                         r["dpool_w"])
    small_all = _allgather8([packed], "gather_small")[0]
    small = ["b_ada", "norm_g", "final_g", "pool_scale", "rel_bias", "pool_w"]
    shaped = lambda b, n, f, ps, rb, pw: [b, n, f.reshape(1, D), ps, rb, pw.reshape(4 * PGW, PGW)]
    loss, small_out = _small_update(small_all, shaped(b_ada, norm_g, final_g, pool_scale, rel_bias, pool_w),
                                    shaped(m_b_ada, m_norm_g, m_final_g, m_pool_scale, m_rel_bias, m_pool_w),
                                    shaped(v_b_ada, v_norm_g, v_final_g, v_pool_scale, v_rel_bias, v_pool_w))
    dmod_all = small_all[:, PK_BADA:PK_NORMG, :].reshape(8, 3 * D)
    g_w_ada = _w_ada_grad(c_all, lax.dynamic_slice_in_dim(dmod_all, chip * mw, mw, axis=1))

    g_w_in, g_w_ab, g_w_pb, g_w_out = _sibling_join([r["rs_in"], r["rs_attn_br"], r["rs_pool_br"], r["rs_out"]],
                                                    "rs_sibling_join")
    upd = dict(zip(small, small_out))
    upd["final_g"] = [a.reshape(D) for a in upd["final_g"]]
    upd["pool_w"] = [a.reshape(1, 4, PGW, PGW) for a in upd["pool_w"]]
    for nme, w, g, m, v in (("w_ada", w_ada, g_w_ada, m_w_ada, v_w_ada), ("w_in", w_in, g_w_in, m_w_in, v_w_in),
                            ("w_attn_br", w_attn_br, g_w_ab, m_w_attn_br, v_w_attn_br),
                            ("w_pool_br", w_pool_br, g_w_pb, m_w_pool_br, v_w_pool_br),
                            ("w_out", w_out, g_w_out, m_w_out, v_w_out)):
        upd[nme] = [a[None] for a in [g] + list(_adamw(w[0], g, m[0], v[0], "adamw_" + nme))]
    names = ["norm_g", "w_ada", "b_ada", "w_in", "pool_w", "pool_scale", "w_attn_br", "w_pool_br", "w_out",
             "rel_bias", "final_g"]
    return (loss[0, 0], r["grad_x"][None]) + tuple(upd[nme][kind] for kind in range(4) for nme in names)
```

```python
import functools
import math

import numpy as np
import jax
import jax.numpy as jnp
from jax import lax
from jax.experimental import pallas as pl
from jax.experimental.pallas import tpu as pltpu

F32 = jnp.float32
BF16 = jnp.bfloat16

D = 1024
HD = 64
NH = 8
AW = NH * HD
GROUPS = ((128, 1), (512, 4), (2048, 16))
NG = len(GROUPS)
BLK = 128
GW = 3 * AW
QKV_W = NG * GW
REST_W = 3584
IN_W = QKV_W + REST_W
CB = 512
NCB = IN_W // CB
NCB_QKV = QKV_W // CB
POOL_WINDOWS = (2, 4, 8, 16)
PGW = 128
HALO = 16
NUM_BUCKETS = 32
MAX_DISTANCE = 2048
EPS = 1e-6
NEG = -1e30
N_SHARD = 4
VMEM_LIMIT = 56 * 1024 * 1024

ADAM_LR = 0.001
ADAM_B1 = 0.9
ADAM_B2 = 0.999
ADAM_EPS = 1e-08
ADAM_WD = 0.01
ADAM_STEP = 10

PK_BADA, PK_NORMG, PK_FINALG, PK_PSCALE, PK_RELB, PK_LOSS, PK_POOLW, PK_ROWS = 0, 24, 32, 40, 48, 80, 88, 600

ANY = pl.BlockSpec(memory_space=pl.ANY)
MESH = pl.DeviceIdType.MESH


def _params(*sem):
    return pltpu.CompilerParams(dimension_semantics=sem, vmem_limit_bytes=VMEM_LIMIT)


def _sds(shape, dtype=F32):
    return jax.ShapeDtypeStruct(shape, dtype)


def _dot(a, b):
    return jnp.dot(a, b, preferred_element_type=F32)


def _dot_nt(a, b):
    return lax.dot_general(a, b, (((1,), (1,)), ((), ())), preferred_element_type=F32)


def _dot_tn(a, b):
    return lax.dot_general(a, b, (((0,), (0,)), ((), ())), preferred_element_type=F32)


def _sigmoid(z):
    return 0.5 * jnp.tanh(0.5 * z) + 0.5


def _dma_sems(*shape):
    return pltpu.SemaphoreType.DMA(shape)


class _Ride:
    def __init__(self, arrays, out_shapes, n_copies, copies, in_place=False):
        self.arrays, self.out_shapes, self.n_copies, self.copies = list(arrays), list(out_shapes), n_copies, copies
        self.in_place = in_place


def _call_with_ride(body, ride, first, last, *, in_specs, out_specs, out_shape, scratch_shapes=(), **kw):
    in_specs, out_specs, out_shape, scratch_shapes = list(in_specs), list(out_specs), list(out_shape), list(scratch_shapes)
    n_in, n_out, n_sc = len(in_specs), len(out_specs), len(scratch_shapes)
    if ride is None:
        def run_plain(*operands):
            return pl.pallas_call(body, in_specs=in_specs, out_specs=out_specs, out_shape=out_shape,
                                  scratch_shapes=scratch_shapes, **kw)(*operands), []
        return run_plain
    n_ri, n_ro = len(ride.arrays), len(ride.out_shapes)

    def wrapped(*refs):
        ins, rest = refs[:n_in], refs[n_in:]
        r_ins, rest = rest[:n_ri], rest[n_ri:]
        outs, rest = rest[:n_out], rest[n_out:]
        r_outs, rest = rest[:n_ro], rest[n_ro:]
        scratch, (send_sems, recv_sems) = rest[:n_sc], rest[n_sc:]

        @pl.when(first())
        def _():
            for cp in ride.copies(r_ins, r_outs, send_sems, recv_sems):
                cp.start()

        body(*ins, *outs, *scratch)

        @pl.when(last())
        def _():
            for cp in ride.copies(r_ins, r_outs, send_sems, recv_sems):
                cp.wait()

    def run(*operands):
        res = pl.pallas_call(
            wrapped, in_specs=in_specs + [ANY] * n_ri, out_specs=out_specs + [ANY] * n_ro,
            out_shape=out_shape + ride.out_shapes,
            scratch_shapes=scratch_shapes + [_dma_sems(ride.n_copies), _dma_sems(ride.n_copies)],
            input_output_aliases={n_in + a: n_out + a for a in range(n_ri)} if ride.in_place else {}, **kw,
        )(*operands, *ride.arrays)
        return res[:n_out], res[n_out:]
    return run


def _bucket_tables():
    i = np.arange(BLK)[:, None]
    j = np.arange(2 * BLK)[None, :]
    dist = BLK + i - j
    valid = (dist >= 0) & (dist <= BLK)
    tabs = []
    for _, dil in GROUPS:
        n = (np.clip(dist, 0, BLK) * dil).astype(np.int32)
        max_exact = NUM_BUCKETS // 2
        nf = np.maximum(n, 1).astype(np.float32)
        large = max_exact + (np.log(nf / np.float32(max_exact)) / np.float32(math.log(MAX_DISTANCE / max_exact))
                             * np.float32(NUM_BUCKETS - max_exact)).astype(np.int32)
        large = np.minimum(large, NUM_BUCKETS - 1)
        bucket = np.where(n < max_exact, n, large)
        tab = np.where(valid, bucket, -1).astype(np.int32)
        perm = _block_perm(dil)
        tabs.append(tab[perm][:, np.concatenate([perm, BLK + perm])])
    return np.stack(tabs)


def _bias_table(rel_bias, buckets, ride):
    def body(rb_ref, bk_ref, out_ref):
        g = pl.program_id(0)
        bk = bk_ref[...]
        for h in range(NH):
            acc = jnp.full((BLK, 2 * BLK), NEG, F32)
            for b in range(NUM_BUCKETS):
                acc = jnp.where(bk == b, rb_ref[b, g * NH + h], acc)
            out_ref[h] = acc

    (tab,), rode = _call_with_ride(
        body, ride, lambda: pl.program_id(0) == 0, lambda: pl.program_id(0) == NG - 1,
        name="bias_table", grid=(NG,),
        in_specs=[pl.BlockSpec(memory_space=pltpu.SMEM),
                  pl.BlockSpec((None, BLK, 2 * BLK), lambda g: (g, 0, 0))],
        out_specs=[pl.BlockSpec((NH, BLK, 2 * BLK), lambda g: (g, 0, 0))],
        out_shape=[_sds((NG * NH, BLK, 2 * BLK))],
        compiler_params=_params("arbitrary"),
    )(rel_bias, buckets)
    return tab, rode


def _bias_grad(ds_acc, buckets, ride):
    def body(acc_ref, bk_ref, out_ref):
        bk = bk_ref[...]
        acc = acc_ref[...]
        lane = lax.broadcasted_iota(jnp.int32, (8, 128), 1)
        out = jnp.zeros((8, 128), F32)
        for b in range(NUM_BUCKETS):
            val = jnp.sum(jnp.where(bk == b, acc, 0.0))
            out = jnp.where(lane == b, val, out)
        out_ref[...] = out

    (out,), rode = _call_with_ride(
        body, ride, lambda: pl.program_id(0) == 0, lambda: pl.program_id(0) == NG * NH - 1,
        name="bias_grad", grid=(NG * NH,),
        in_specs=[pl.BlockSpec((None, BLK, 2 * BLK), lambda gh: (gh, 0, 0)),
                  pl.BlockSpec((None, BLK, 2 * BLK), lambda gh: (gh // NH, 0, 0))],
        out_specs=[pl.BlockSpec((None, 8, 128), lambda gh: (gh, 0, 0))],
        out_shape=[_sds((NG * NH, 8, 128))],
        compiler_params=_params("arbitrary"),
    )(ds_acc, buckets)
    return out, rode


def _mod_partial(c_all, w_ada_s, b_ada_s):
    def body(c_ref, w_ref, b_ref, o_ref):
        o_ref[...] = _dot(c_ref[...].astype(BF16), w_ref[...].astype(BF16)) + b_ref[...]

    return pl.pallas_call(body, name="mod_partial", out_shape=_sds((8, w_ada_s.shape[1])),
                          compiler_params=_params())(c_all, w_ada_s, b_ada_s)


def _prenorm(x, norm_g, mod, ride):
    S = x.shape[0]
    tm = 512
    nt = S // tm

    def body(x_ref, g_ref, mod_ref, h_ref):
        xv = x_ref[...]
        r = lax.rsqrt(jnp.mean(xv * xv, axis=-1, keepdims=True) + EPS)
        n1 = xv * r * g_ref[...]
        h_ref[...] = (n1 * (1.0 + mod_ref[:, D:2 * D]) + mod_ref[:, 0:D]).astype(BF16)

    (h,), rode = _call_with_ride(
        body, ride, lambda: pl.program_id(0) == 0, lambda: pl.program_id(0) == nt - 1,
        name="prenorm", grid=(nt,),
        in_specs=[pl.BlockSpec((tm, D), lambda i: (i, 0)), pl.BlockSpec((1, D), lambda i: (0, 0)),
                  pl.BlockSpec((1, 3 * D), lambda i: (0, 0))],
        out_specs=[pl.BlockSpec((tm, D), lambda i: (i, 0))],
        out_shape=[_sds((S, D), BF16)], compiler_params=_params("arbitrary"),
    )(x, norm_g, mod)
    return h, rode


def _proj(h, wg_in, j0, nj, dtype, name):
    S = h.shape[0]
    tm = 2048
    per = wg_in.shape[2] // CB

    def body(h_ref, w_ref, o_ref):
        o_ref[...] = _dot(h_ref[...], w_ref[...]).astype(dtype)

    return pl.pallas_call(
        body, name=name, grid=(S // tm, nj),
        in_specs=[pl.BlockSpec((tm, D), lambda m, j: (m, 0)),
                  pl.BlockSpec((None, D, CB), lambda m, j: ((j0 + j) // per, 0, (j0 + j) % per))],
        out_specs=pl.BlockSpec((tm, CB), lambda m, j: (m, j)),
        out_shape=_sds((S, nj * CB), dtype), compiler_params=_params("parallel", "parallel"),
    )(h, wg_in)


HS = 4
SLAB = HS * HD


def _lane_head(rows):
    return lax.broadcasted_iota(jnp.int32, (rows, SLAB), 1) // HD


def _head_stack(a):
    head = _lane_head(a.shape[0])
    return jnp.concatenate([jnp.where(head == h, a, jnp.zeros_like(a)) for h in range(HS)], axis=0)


def _head_unstack(a):
    rows = a.shape[0] // HS
    head = _lane_head(rows)
    out = a[:rows]
    for h in range(1, HS):
        out = jnp.where(head == h, a[h * rows:(h + 1) * rows], out)
    return out


STAT_W = 128
VIEW = 16


def _sub_layout(dil):
    if dil == 1:
        return BLK, [None]
    return BLK * dil // VIEW, [[r + dil * u for u in range(VIEW // dil)] for r in range(dil)]


def _block_perm(dil):
    a_rows, _ = _sub_layout(dil)
    p = np.arange(BLK)
    return p if dil == 1 else (VIEW // dil) * (p % a_rows) + p // a_rows


LB = 128
N_SLAB = NH // HS


def _ld(refs, bs, s, w):
    if bs is None:
        return refs[0][:, s * w:(s + 1) * w]
    a_rows = refs[0].shape[0] // VIEW
    return jnp.concatenate([jnp.concatenate([ref[pl.ds(b, a_rows, stride=VIEW), :] for b in bs], axis=0)
                            for ref in refs], axis=1)


def _st(ref, bs, s, val):
    if bs is None:
        ref[:, s * SLAB:(s + 1) * SLAB] = val
        return
    a_rows = val.shape[0] // len(bs)
    for u, b in enumerate(bs):
        ref[:, b, :] = val[u * a_rows:(u + 1) * a_rows]


def _attn_views(dil, S):
    a_rows, subs = _sub_layout(dil)
    if dil == 1:
        def ispecs(base, w, f):
            return [pl.BlockSpec((BLK, N_SLAB * w), lambda sg, n: (f(n), base // (N_SLAB * w)))]
        return subs, S // BLK, N_SLAB, ispecs, (lambda w: (S, w)), (
            lambda f: pl.BlockSpec((BLK, AW), lambda sg, n: (f(n), 0)))

    def ispecs(base, w, f):
        return [pl.BlockSpec((a_rows * VIEW, LB), lambda sg, n, k=k: (f(n), (base + sg * w) // LB + k))
                for k in range(w // LB)]
    return subs, S // (a_rows * VIEW), 1, ispecs, (lambda w: (S // VIEW, VIEW, w)), (
        lambda f: pl.BlockSpec((a_rows, VIEW, SLAB), lambda sg, n: (f(n), 0, sg)))


def _attn_fwd(qkv_g, bias_tab, g):
    S = qkv_g.shape[0]
    subs, nbq, sps, ispecs, shape, ospec = _attn_views(GROUPS[g][1], S)
    cur, prev = (lambda n: n), (lambda n: jnp.maximum(n - 1, 0))
    in_specs = [ispecs(0, SLAB, cur), ispecs(AW, SLAB, prev), ispecs(AW, SLAB, cur), ispecs(2 * AW, SLAB, prev),
                ispecs(2 * AW, SLAB, cur)]
    nl = len(in_specs[0])

    def body(*refs):
        q, kp, kc, vp, vc = (refs[t * nl:(t + 1) * nl] for t in range(5))
        b_ref, o_ref, l_ref = refs[5 * nl:]
        n = pl.program_id(1)
        col = lax.broadcasted_iota(jnp.int32, (HS * BLK, 2 * BLK), 1)
        keep = (col >= BLK) | (n > 0)
        for s_ in range(sps):
            bias = b_ref[pl.ds(s_ * HS, HS)].reshape(HS * BLK, 2 * BLK)
            for bs in subs:
                kb = jnp.concatenate([_ld(kp, bs, s_, SLAB), _ld(kc, bs, s_, SLAB)], axis=0).astype(BF16)
                vb = jnp.concatenate([_ld(vp, bs, s_, SLAB), _ld(vc, bs, s_, SLAB)], axis=0).astype(BF16)
                s = _dot_nt(_head_stack(_ld(q, bs, s_, SLAB).astype(BF16)), kb) * (HD ** -0.5) + bias
                s = jnp.where(keep, s, NEG)
                m = jnp.max(s, axis=-1, keepdims=True)
                p = jnp.exp(s - m)
                den = jnp.sum(p, axis=-1, keepdims=True)
                _st(o_ref, bs, s_, _head_unstack(_dot(p.astype(BF16), vb) / den))
                _st(l_ref, bs, s_, _head_unstack(jnp.broadcast_to(m + jnp.log(den), (HS * BLK, SLAB))))

    out = _sds(shape(AW))
    o, l = pl.pallas_call(
        body, name=f"attn_fwd{g}", grid=(N_SLAB // sps, nbq),
        in_specs=sum(in_specs, []) + [pl.BlockSpec((sps * HS, BLK, 2 * BLK),
                                                   lambda sg, n: (g * (N_SLAB // sps) + sg, 0, 0))],
        out_specs=[ospec(cur), ospec(cur)],
        out_shape=[out, out], compiler_params=_params("parallel", "arbitrary"),
    )(*([qkv_g] * (5 * nl)), bias_tab)
    return o.reshape(S, AW), l.reshape(S, AW)


def _attn_bwd(qkv_g, dattn, stats, bias_tab, g, ride):
    S = qkv_g.shape[0]
    subs, nbq, sps, ispecs, shape, ospec = _attn_views(GROUPS[g][1], S)
    cur = lambda n: jnp.minimum(n, nbq - 1)
    prev = lambda n: jnp.clip(n - 1, 0, nbq - 1)
    late = lambda n: jnp.maximum(n - 1, 0)
    in_specs = [ispecs(0, SLAB, cur), ispecs(AW, SLAB, prev), ispecs(AW, SLAB, cur), ispecs(2 * AW, SLAB, prev),
                ispecs(2 * AW, SLAB, cur), ispecs(0, SLAB, cur), ispecs(0, STAT_W, cur)]
    nl = len(in_specs[0])

    def body(*refs):
        q, kp, kc, vp, vc, da = (refs[t * nl:(t + 1) * nl] for t in range(6))
        st_ref, b_ref, dq_ref, dk_ref, dv_ref, ds_ref, ck_ref, cv_ref = refs[6 * nl:]
        n = pl.program_id(1)

        @pl.when(n == 0)
        def _():
            ds_ref[...] = jnp.zeros_like(ds_ref)
            ck_ref[...] = jnp.zeros_like(ck_ref)
            cv_ref[...] = jnp.zeros_like(cv_ref)

        @pl.when(n < nbq)
        def _():
            col = lax.broadcasted_iota(jnp.int32, (HS * BLK, 2 * BLK), 1)
            keep = (col >= BLK) | (n > 0)
            for s_ in range(sps):
                cs = slice(s_ * SLAB, (s_ + 1) * SLAB)
                bias = b_ref[pl.ds(s_ * HS, HS)].reshape(HS * BLK, 2 * BLK)
                for i, bs in enumerate(subs):
                    st = _ld((st_ref,), bs, s_, STAT_W)
                    kb = jnp.concatenate([_ld(kp, bs, s_, SLAB), _ld(kc, bs, s_, SLAB)], axis=0).astype(BF16)
                    vb = jnp.concatenate([_ld(vp, bs, s_, SLAB), _ld(vc, bs, s_, SLAB)], axis=0).astype(BF16)
                    lse = jnp.concatenate([st[:, h:h + 1] for h in range(HS)], axis=0)
                    delta = jnp.concatenate([st[:, HS + h:HS + h + 1] for h in range(HS)], axis=0)
                    qs = _head_stack(_ld(q, bs, s_, SLAB).astype(BF16))
                    dos = _head_stack(_ld(da, bs, s_, SLAB).astype(BF16))
                    s = _dot_nt(qs, kb) * (HD ** -0.5) + bias
                    s = jnp.where(keep, s, NEG)
                    p = jnp.exp(s - lse)
                    ds = p * (_dot_nt(dos, vb) - delta)
                    ds_ref[pl.ds(s_ * HS, HS)] += ds.reshape(HS, BLK, 2 * BLK)
                    ds_b = (ds * (HD ** -0.5)).astype(BF16)
                    _st(dq_ref, bs, s_, _head_unstack(_dot(ds_b, kb)))
                    dkb = _dot_tn(ds_b, qs)
                    dvb = _dot_tn(p.astype(BF16), dos)
                    _st(dk_ref, bs, s_, ck_ref[i, :, cs] + dkb[:BLK])
                    _st(dv_ref, bs, s_, cv_ref[i, :, cs] + dvb[:BLK])
                    ck_ref[i, :, cs] = dkb[BLK:]
                    cv_ref[i, :, cs] = dvb[BLK:]

        @pl.when(n == nbq)
        def _():
            for s_ in range(sps):
                for i, bs in enumerate(subs):
                    _st(dk_ref, bs, s_, ck_ref[i, :, s_ * SLAB:(s_ + 1) * SLAB])
                    _st(dv_ref, bs, s_, cv_ref[i, :, s_ * SLAB:(s_ + 1) * SLAB])

    out = _sds(shape(AW))
    nsg = N_SLAB // sps
    (dq, dk, dv, ds_acc), rode = _call_with_ride(
        body, ride, lambda: (pl.program_id(0) == 0) & (pl.program_id(1) == 0),
        lambda: (pl.program_id(0) == nsg - 1) & (pl.program_id(1) == nbq),
        name=f"attn_bwd{g}", grid=(nsg, nbq + 1),
        in_specs=sum(in_specs, []) + [pl.BlockSpec((sps * HS, BLK, 2 * BLK), lambda sg, n: (g * nsg + sg, 0, 0))],
        out_specs=[ospec(cur), ospec(late), ospec(late),
                   pl.BlockSpec((sps * HS, BLK, 2 * BLK), lambda sg, n: (sg, 0, 0))],
        out_shape=[out] * 3 + [_sds((NH, BLK, 2 * BLK))],
        scratch_shapes=[pltpu.VMEM((len(subs), BLK, sps * SLAB), F32), pltpu.VMEM((len(subs), BLK, sps * SLAB), F32)],
        compiler_params=_params("arbitrary", "arbitrary"),
    )(*([qkv_g] * (5 * nl)), *([dattn] * nl), stats, bias_tab)
    return [dq.reshape(S, AW), dk.reshape(S, AW), dv.reshape(S, AW)], ds_acc, rode


TM_MIX = 256


def _mix_specs(tm):
    row512 = pl.BlockSpec((tm, AW), lambda i: (i, 0))
    return ([row512] * 6 + [
        pl.BlockSpec((tm, REST_W), lambda i: (i, 0)),
        pl.BlockSpec((HALO, AW), lambda i: (jnp.maximum(i * (tm // HALO) - 1, 0), 1)),
        pl.BlockSpec((AW, D), lambda i: (0, 0)), pl.BlockSpec((AW, D), lambda i: (0, 0)),
        pl.BlockSpec((4, PGW, PGW), lambda i: (0, 0, 0)), pl.BlockSpec((1, AW), lambda i: (0, 0))])


def _mix_forward(i, tm, o_refs, l_refs, rest_ref, halo_ref, wab_ref, wpb_ref, pw_ref, ps_ref):
    l0, l1, l2 = (r[...] for r in l_refs)
    mx = jnp.maximum(jnp.maximum(l0, l1), l2)
    e0, e1, e2 = jnp.exp(l0 - mx), jnp.exp(l1 - mx), jnp.exp(l2 - mx)
    den = e0 + e1 + e2
    lj = mx + jnp.log(den)
    attn = (e0 * o_refs[0][...] + e1 * o_refs[1][...] + e2 * o_refs[2][...]) / den

    z_attn = rest_ref[:, 0:AW]
    u = rest_ref[:, AW:2 * AW]
    z_pool = rest_ref[:, 2 * AW:3 * AW]
    g_attn = rest_ref[:, 3 * AW:3 * AW + D]
    g_pool = rest_ref[:, 3 * AW + D:3 * AW + 2 * D]

    sg_a = _sigmoid(z_attn)
    sil_a = z_attn * sg_a
    a_g = (attn * sil_a).astype(BF16)
    y_attn = _dot(a_g, wab_ref[...])

    halo = jnp.where(i > 0, halo_ref[...], 0.0)
    ext = jnp.concatenate([halo, u], axis=0)
    t = i * tm + lax.broadcasted_iota(jnp.int32, (tm, 1), 0)
    pooled, mixed_raw = [], []
    for gi, win in enumerate(POOL_WINDOWS):
        s = ext[:, gi * PGW:(gi + 1) * PGW]
        sh = 1
        while sh < win:
            s = s + pltpu.roll(s, sh, 0)
            sh *= 2
        cnt = jnp.minimum(t + 1, win).astype(F32)
        pg = s[HALO:] / cnt - u[:, gi * PGW:(gi + 1) * PGW]
        pooled.append(pg.astype(BF16))
        mixed_raw.append(_dot(pooled[-1], pw_ref[gi].astype(BF16)))
    mixed_raw = jnp.concatenate(mixed_raw, axis=1)
    mixed = mixed_raw * ps_ref[...]
    sg_p = _sigmoid(z_pool)
    sil_p = z_pool * sg_p
    m_g = (mixed * sil_p).astype(BF16)
    y_pool = _dot(m_g, wpb_ref[...])

    sa = _sigmoid(g_attn)
    sp = _sigmoid(g_pool)
    merged = sa * y_attn + sp * y_pool
    return dict(lj=lj, attn=attn, z_attn=z_attn, z_pool=z_pool, sg_a=sg_a, sil_a=sil_a, a_g=a_g, y_attn=y_attn,
                pooled=pooled, mixed_raw=mixed_raw, mixed=mixed, sg_p=sg_p, sil_p=sil_p, m_g=m_g, y_pool=y_pool,
                sa=sa, sp=sp, merged=merged)


def _tail(x, target, os_, ls_, rest, wab, wpb, pool_w, pool_scale, wout, mod, final_g):
    S = x.shape[0]
    tm = TM_MIX

    def body(o0, o1, o2, l0, l1, l2, rest_ref, halo_ref, wab_ref, wpb_ref, pw_ref, ps_ref,
             x_ref, t_ref, wo_ref, mod_ref, fg_ref, dx2_ref, dmo_ref, loss_ref, dfg_ref, dgate_ref):
        i = pl.program_id(0)

        @pl.when(i == 0)
        def _():
            loss_ref[...] = jnp.zeros_like(loss_ref)
            dfg_ref[...] = jnp.zeros_like(dfg_ref)
            dgate_ref[...] = jnp.zeros_like(dgate_ref)

        f = _mix_forward(i, tm, (o0, o1, o2), (l0, l1, l2), rest_ref, halo_ref, wab_ref, wpb_ref, pw_ref, ps_ref)
        mo = _dot(f["merged"].astype(BF16), wo_ref[...])
        gate = mod_ref[:, 2 * D:3 * D]
        fg = fg_ref[...]
        x2 = x_ref[...] + gate * mo
        r2 = lax.rsqrt(jnp.mean(x2 * x2, axis=-1, keepdims=True) + EPS)
        n2 = x2 * r2
        err = n2 * fg - t_ref[...]
        loss_ref[...] += 0.5 * jnp.sum(jnp.mean(err * err, axis=-1, keepdims=True))
        dy = err * (1.0 / D)
        dfg_ref[...] += jnp.sum(dy * n2, axis=0, keepdims=True)
        dn = dy * fg
        dx2 = r2 * (dn - n2 * jnp.mean(dn * n2, axis=-1, keepdims=True))
        dgate_ref[...] += jnp.sum(dx2 * mo, axis=0, keepdims=True)
        dx2_ref[...] = dx2
        dmo_ref[...] = (dx2 * gate).astype(BF16)

    row = pl.BlockSpec((tm, D), lambda i: (i, 0))
    vec = pl.BlockSpec((1, D), lambda i: (0, 0))
    return pl.pallas_call(
        body, name="tail", grid=(S // tm,),
        in_specs=_mix_specs(tm) + [row, row, pl.BlockSpec((D, D), lambda i: (0, 0)),
                                   pl.BlockSpec((1, 3 * D), lambda i: (0, 0)), vec],
        out_specs=[row, row, pl.BlockSpec((8, 128), lambda i: (0, 0)), vec, vec],
        out_shape=[_sds((S, D)), _sds((S, D), BF16), _sds((8, 128)), _sds((1, D)), _sds((1, D))],
        compiler_params=_params("arbitrary"),
    )(*os_, *ls_, rest, rest, wab, wpb, pool_w, pool_scale, x, target, wout, mod, final_g)


def _mix_bwd(dmo, os_, ls_, rest, wab, wpb, pool_w, pool_scale, wout):
    S = dmo.shape[0]
    tm = TM_MIX
    nt = S // tm
    sw = D // N_SHARD

    def body(o0, o1, o2, l0, l1, l2, rest_ref, halo_ref, wab_ref, wpb_ref, pw_ref, ps_ref, dmo_ref, wo_ref,
             dattn_ref, stats_ref, dpooled_ref, drest_ref, dwo_hbm, dwab_hbm, dwpb_hbm, dpw_ref, dps_ref,
             awo, awab, awpb):
        i = pl.program_id(0)

        @pl.when(i == 0)
        def _():
            awo[...] = jnp.zeros_like(awo)
            awab[...] = jnp.zeros_like(awab)
            awpb[...] = jnp.zeros_like(awpb)
            dpw_ref[...] = jnp.zeros_like(dpw_ref)
            dps_ref[...] = jnp.zeros_like(dps_ref)

        f = _mix_forward(i, tm, (o0, o1, o2), (l0, l1, l2), rest_ref, halo_ref, wab_ref, wpb_ref, pw_ref, ps_ref)
        dmo_b = dmo_ref[...]
        dmerged = _dot_nt(dmo_b, wo_ref[...])
        awo[...] += _dot_tn(f["merged"].astype(BF16), dmo_b)
        sa, sp = f["sa"], f["sp"]
        dya = (dmerged * sa).astype(BF16)
        dyp = (dmerged * sp).astype(BF16)
        dg_attn = dmerged * f["y_attn"] * sa * (1.0 - sa)
        dg_pool = dmerged * f["y_pool"] * sp * (1.0 - sp)
        dag = _dot_nt(dya, wab_ref[...])
        awab[...] += _dot_tn(f["a_g"], dya)
        dmg = _dot_nt(dyp, wpb_ref[...])
        awpb[...] += _dot_tn(f["m_g"], dyp)
        dattn = dag * f["sil_a"]
        dattn_ref[...] = dattn
        prod = dattn * f["attn"]
        lane = lax.broadcasted_iota(jnp.int32, (tm, STAT_W), 1)
        for sb in range(N_SLAB):
            st = jnp.zeros((tm, STAT_W), F32)
            for h in range(HS):
                hs = slice((sb * HS + h) * HD, (sb * HS + h + 1) * HD)
                st = jnp.where(lane == h, f["lj"][:, hs.start:hs.start + 1], st)
                st = jnp.where(lane == HS + h, jnp.sum(prod[:, hs], axis=-1, keepdims=True), st)
            stats_ref[:, sb * STAT_W:(sb + 1) * STAT_W] = st
        dz_attn = dag * f["attn"] * (f["sg_a"] * (1.0 + f["z_attn"] * (1.0 - f["sg_a"])))
        dmixed = dmg * f["sil_p"]
        dz_pool = dmg * f["mixed"] * (f["sg_p"] * (1.0 + f["z_pool"] * (1.0 - f["sg_p"])))
        dps_ref[...] += jnp.sum(dmixed * f["mixed_raw"], axis=0, keepdims=True)
        dpm = (dmixed * ps_ref[...]).astype(BF16)
        for gi in range(len(POOL_WINDOWS)):
            cs = slice(gi * PGW, (gi + 1) * PGW)
            dpw_ref[gi] += _dot_tn(f["pooled"][gi], dpm[:, cs])
            dpooled_ref[:, cs] = _dot_nt(dpm[:, cs], pw_ref[gi].astype(BF16))
        drest_ref[:, 0:AW] = dz_attn.astype(BF16)
        drest_ref[:, AW:2 * AW] = jnp.zeros((tm, AW), BF16)
        drest_ref[:, 2 * AW:3 * AW] = dz_pool.astype(BF16)
        drest_ref[:, 3 * AW:3 * AW + D] = dg_attn.astype(BF16)
        drest_ref[:, 3 * AW + D:3 * AW + 2 * D] = dg_pool.astype(BF16)

        @pl.when(i == nt - 1)
        def _():
            pltpu.sync_copy(awo, dwo_hbm)
            for k in range(N_SHARD):
                pltpu.sync_copy(awab.at[:, pl.ds(k * sw, sw)], dwab_hbm.at[k])
                pltpu.sync_copy(awpb.at[:, pl.ds(k * sw, sw)], dwpb_hbm.at[k])

    row512 = pl.BlockSpec((tm, AW), lambda i: (i, 0))
    outs = pl.pallas_call(
        body, name="mix_bwd", grid=(nt,),
        in_specs=_mix_specs(tm) + [pl.BlockSpec((tm, D), lambda i: (i, 0)), pl.BlockSpec((D, D), lambda i: (0, 0))],
        out_specs=[row512, pl.BlockSpec((tm, N_SLAB * STAT_W), lambda i: (i, 0)), row512,
                   pl.BlockSpec((tm, REST_W), lambda i: (i, 0)), ANY, ANY, ANY,
                   pl.BlockSpec((4, PGW, PGW), lambda i: (0, 0, 0)), pl.BlockSpec((1, AW), lambda i: (0, 0))],
        out_shape=[_sds((S, AW)), _sds((S, N_SLAB * STAT_W)), _sds((S, AW)), _sds((S, REST_W), BF16),
                   _sds((D, D)), _sds((N_SHARD, AW, sw)), _sds((N_SHARD, AW, sw)), _sds((4, PGW, PGW)), _sds((1, AW))],
        scratch_shapes=[pltpu.VMEM((D, D), F32), pltpu.VMEM((AW, D), F32), pltpu.VMEM((AW, D), F32)],
        compiler_params=_params("arbitrary"),
    )(*os_, *ls_, rest, rest, wab, wpb, pool_w, pool_scale, dmo, wout)
    dattn, stats, dpooled, drest, dwo, dwab, dwpb, dpw, dps = outs
    return dattn, stats, dpooled, drest, dwo.reshape(N_SHARD, D // N_SHARD, D), dwab, dwpb, dpw, dps


def _pool_bwd(dpooled):
    S = dpooled.shape[0]
    tm = 512
    nt = S // tm

    def body(dp_ref, nxt_ref, du_ref):
        i = pl.program_id(0)
        t = i * tm + lax.broadcasted_iota(jnp.int32, (tm + HALO, 1), 0)
        nxt = jnp.where(i < nt - 1, nxt_ref[...], 0.0)
        ext = jnp.concatenate([dp_ref[...], nxt], axis=0)
        for gi, win in enumerate(POOL_WINDOWS):
            cs = slice(gi * PGW, (gi + 1) * PGW)
            s = ext[:, cs] / jnp.minimum(t + 1, win).astype(F32)
            sh = 1
            while sh < win:
                s = s + pltpu.roll(s, tm + HALO - sh, 0)
                sh *= 2
            du_ref[:, cs] = (s[:tm] - dp_ref[:, cs]).astype(BF16)

    return pl.pallas_call(
        body, name="pool_bwd", grid=(nt,),
        in_specs=[pl.BlockSpec((tm, AW), lambda i: (i, 0)),
                  pl.BlockSpec((HALO, AW), lambda i: (jnp.minimum((i + 1) * (tm // HALO), S // HALO - 1), 0))],
        out_specs=pl.BlockSpec((tm, AW), lambda i: (i, 0)),
        out_shape=_sds((S, AW), BF16), compiler_params=_params("parallel"),
    )(dpooled, dpooled)


TB = 1024


def _dh(dproj, wg_in, ride):
    S = dproj.shape[0]
    per = wg_in.shape[2] // TB
    nm, nk = S // TB, IN_W // TB

    def body(dp_ref, w_ref, out_ref):
        @pl.when(pl.program_id(1) == 0)
        def _():
            out_ref[...] = jnp.zeros_like(out_ref)

        out_ref[...] += _dot_nt(dp_ref[...], w_ref[...])

    (dh,), rode = _call_with_ride(
        body, ride, lambda: (pl.program_id(0) == 0) & (pl.program_id(1) == 0),
        lambda: (pl.program_id(0) == nm - 1) & (pl.program_id(1) == nk - 1),
        name="dh", grid=(nm, nk),
        in_specs=[pl.BlockSpec((TB, TB), lambda m, kk: (m, kk)),
                  pl.BlockSpec((None, D, TB), lambda m, kk: (kk // per, 0, kk % per))],
        out_specs=[pl.BlockSpec((TB, D), lambda m, kk: (m, 0))],
        out_shape=[_sds((S, D))], compiler_params=_params("arbitrary", "arbitrary"),
    )(dproj, wg_in)
    return dh, rode


def _dw_in(h, dproj):
    S = dproj.shape[0]
    per = IN_W // N_SHARD // TB

    def body(h_ref, dp_ref, out_ref):
        @pl.when(pl.program_id(1) == 0)
        def _():
            out_ref[...] = jnp.zeros_like(out_ref)

        out_ref[...] += _dot_tn(h_ref[...], dp_ref[...])

    return pl.pallas_call(
        body, name="dw_in", grid=(IN_W // TB, S // TB),
        in_specs=[pl.BlockSpec((TB, D), lambda j, kk: (kk, 0)), pl.BlockSpec((TB, TB), lambda j, kk: (kk, j))],
        out_specs=pl.BlockSpec((None, D, TB), lambda j, kk: (j // per, 0, j % per)),
        out_shape=_sds((N_SHARD, D, IN_W // N_SHARD)), compiler_params=_params("parallel", "arbitrary"),
    )(h, dproj)


def _prenorm_bwd(x, dh, dx2, norm_g, mod):
    S = x.shape[0]
    tm = 512

    def body(x_ref, dh_ref, dx2_ref, g_ref, mod_ref, gx_ref, dg_ref, dshift_ref, dscale_ref):
        i = pl.program_id(0)

        @pl.when(i == 0)
        def _():
            dg_ref[...] = jnp.zeros_like(dg_ref)
            dshift_ref[...] = jnp.zeros_like(dshift_ref)
            dscale_ref[...] = jnp.zeros_like(dscale_ref)

        xv = x_ref[...]
        dhv = dh_ref[...]
        g = g_ref[...]
        r = lax.rsqrt(jnp.mean(xv * xv, axis=-1, keepdims=True) + EPS)
        xh = xv * r
        dshift_ref[...] += jnp.sum(dhv, axis=0, keepdims=True)
        dscale_ref[...] += jnp.sum(dhv * (xh * g), axis=0, keepdims=True)
        dn1 = dhv * (1.0 + mod_ref[:, D:2 * D])
        dg_ref[...] += jnp.sum(dn1 * xh, axis=0, keepdims=True)
        dxh = dn1 * g
        gx_ref[...] = dx2_ref[...] + r * (dxh - xh * jnp.mean(dxh * xh, axis=-1, keepdims=True))

    row = pl.BlockSpec((tm, D), lambda i: (i, 0))
    vec = pl.BlockSpec((1, D), lambda i: (0, 0))
    return pl.pallas_call(
        body, name="prenorm_bwd", grid=(S // tm,),
        in_specs=[row, row, row, vec, pl.BlockSpec((1, 3 * D), lambda i: (0, 0))],
        out_specs=[row, vec, vec, vec],
        out_shape=[_sds((S, D)), _sds((1, D)), _sds((1, D)), _sds((1, D))],
        compiler_params=_params("arbitrary"),
    )(x, dh, dx2, norm_g, mod)


def _local_step(x, target, mod, h, bias_tab, buckets, wg_in, wab, wpb, wout, pool_w, pool_scale, norm_g, final_g,
                half_idx, chip_half):
    qkv = [_proj(h, wg_in, 3 * g, 3, F32, f"proj_qkv{g}") for g in range(NG)]
    rest = _proj(h, wg_in, NCB_QKV, REST_W // CB, F32, "proj_rest")
    os_, ls_ = zip(*[_attn_fwd(qkv[g], bias_tab, g) for g in range(NG)])
    dx2, dmo, loss, dfinal_g, dgate = _tail(x, target, os_, ls_, rest, wab, wpb, pool_w, pool_scale, wout, mod, final_g)
    dattn, stats, dpooled, drest, dw_out, dw_ab, dw_pb, dpool_w, dpool_scale = _mix_bwd(
        dmo, os_, ls_, rest, wab, wpb, pool_w, pool_scale, wout)
    du = _pool_bwd(dpooled)

    small = [dw_ab, dw_pb, dw_out]
    dqkv0, ds0, sib_small = _attn_bwd(qkv[0], dattn, stats, bias_tab, 0, _ride_sibling_halves(small))
    p_small = [_pair_sum(g, t, half_idx, f"rs_pair_sum{a}") for a, (g, t) in enumerate(zip(small, sib_small))]
    dqkv1, ds1, u_small = _attn_bwd(qkv[1], dattn, stats, bias_tab, 1,
                                    _ride_chip_exchange([p16 for _, p16 in p_small]))
    rs_ab, rs_pb, rs_out = [_chip_sum(p32, u, chip_half, f"rs_chip_sum{a}")
                            for a, ((p32, _), u) in enumerate(zip(p_small, u_small))]
    dqkv2, ds2, _ = _attn_bwd(qkv[2], dattn, stats, bias_tab, 2, None)

    dproj = jnp.concatenate([a.astype(BF16) for a in dqkv0 + dqkv1 + dqkv2] + [drest[:, :AW], du, drest[:, 2 * AW:]],
                            axis=1)
    dw_in = _dw_in(h, dproj)
    drel_rows, (sib_in,) = _bias_grad(jnp.concatenate([ds0, ds1, ds2], axis=0), buckets,
                                      _ride_sibling_halves([dw_in]))
    drel = drel_rows[:, 0, :NUM_BUCKETS].T
    p32_in, p16_in = _pair_sum(dw_in, sib_in, half_idx, "rs_pair_sum_in")
    dh, (u_in,) = _dh(dproj, wg_in, _ride_chip_exchange([p16_in]))
    rs_in = _chip_sum(p32_in, u_in, chip_half, "rs_chip_sum_in")

    grad_x, dnorm_g, dshift, dscale = _prenorm_bwd(x, dh, dx2, norm_g, mod)
    dmod = jnp.concatenate([dshift, dscale, dgate], axis=1)
    return dict(loss=loss[0, 0], grad_x=grad_x, dmod=dmod, dnorm_g=dnorm_g, dfinal_g=dfinal_g, dpool_w=dpool_w,
                dpool_scale=dpool_scale, drel_bias=drel, dw_in=dw_in, dw_attn_br=dw_ab, dw_pool_br=dw_pb,
                dw_out=dw_out, rs_in=rs_in, rs_attn_br=rs_ab, rs_pool_br=rs_pb, rs_out=rs_out)


def _allgather8(blocks, name, relay=None):
    nb = len(blocks)
    relay = [False] * nb if relay is None else list(relay)

    def body(*refs):
        ins, outs = refs[:nb], refs[nb:2 * nb]
        send_sems, recv_sems, local_sems = refs[2 * nb:]
        x, y, c = lax.axis_index("x"), lax.axis_index("y"), lax.axis_index("c")
        me, sibling = (x, y, c), (x, y, 1 - c)
        here, xn, yn, dg = (x, y), (1 - x, y), (x, 1 - y), (1 - x, 1 - y)

        def slot(a, chip, core, half=None):
            ref = outs[a].at[4 * chip[0] + 2 * chip[1] + core]
            if half is None:
                return ref
            r2 = ref.shape[0] // 2
            return ref.at[pl.ds(half * r2, r2)]

        def copy(a, k, dst, to, src=None):
            return pltpu.make_async_remote_copy(src_ref=dst if src is None else src, dst_ref=dst,
                                                send_sem=send_sems.at[a, k], recv_sem=recv_sems.at[a, k],
                                                device_id=to, device_id_type=MESH)

        def start(cps):
            for cp in cps:
                cp.start()
            return cps

        mine = start([pltpu.make_async_copy(ins[a], slot(a, here, c), local_sems.at[a]) for a in range(nb)])
        sent = []
        for a in range(nb):
            own = slot(a, here, c)
            sent += [copy(a, 0, own, sibling, src=ins[a]), copy(a, 1, own, (*xn, c), src=ins[a]),
                     copy(a, 2, own, (*yn, c), src=ins[a])]
            if not relay[a]:
                sent.append(copy(a, 3, own, (*dg, c), src=ins[a]))
        start(sent)
        for a in range(nb):
            copy(a, 2, slot(a, yn, c), me).wait_recv()
            sent += start([copy(a, 6, slot(a, yn, c), sibling)]
                          + ([copy(a, 3, slot(a, yn, c, 0), (*xn, c))] if relay[a] else []))
        for a in range(nb):
            copy(a, 1, slot(a, xn, c), me).wait_recv()
            sent += start([copy(a, 5, slot(a, xn, c), sibling)]
                          + ([copy(a, 4, slot(a, xn, c, 1), (*yn, c))] if relay[a] else []))
        for a in range(nb):
            for k, half in ((3, 0), (4, 1)) if relay[a] else ((3, None),):
                copy(a, k, slot(a, dg, c, half), me).wait_recv()
                sent += start([copy(a, 4 + k, slot(a, dg, c, half), sibling)])
        for a in range(nb):
            copy(a, 0, slot(a, here, 1 - c), me).wait_recv()
            copy(a, 5, slot(a, xn, 1 - c), me).wait_recv()
            copy(a, 6, slot(a, yn, 1 - c), me).wait_recv()
            for k, half in ((7, 0), (8, 1)) if relay[a] else ((7, None),):
                copy(a, k, slot(a, dg, 1 - c, half), me).wait_recv()
        for cp in sent:
            cp.wait_send()
        for cp in mine:
            cp.wait()

    return pl.pallas_call(
        body, name=name, in_specs=[ANY] * nb, out_specs=[ANY] * nb,
        out_shape=[_sds((8,) + b.shape, b.dtype) for b in blocks],
        scratch_shapes=[_dma_sems(nb, 9), _dma_sems(nb, 9), _dma_sems(nb)],
    )(*blocks)


def _ride_gather_send(blocks):
    def copies(ins, outs, send_sems, recv_sems):
        x, y, c = lax.axis_index("x"), lax.axis_index("y"), lax.axis_index("c")
        cps = []
        for a in range(len(blocks)):
            own = outs[a].at[4 * x + 2 * y + c]
            cps.append(pltpu.make_async_copy(ins[a], own, send_sems.at[5 * a]))
            for k, to in enumerate([(x, y, 1 - c), (1 - x, y, c), (x, 1 - y, c), (1 - x, 1 - y, c)]):
                cps.append(pltpu.make_async_remote_copy(src_ref=ins[a], dst_ref=own, send_sem=send_sems.at[5 * a + 1 + k],
                                                        recv_sem=recv_sems.at[5 * a + 1 + k], device_id=to,
                                                        device_id_type=MESH))
        return cps

    return _Ride(blocks, [_sds((8,) + b.shape, b.dtype) for b in blocks], 5 * len(blocks), copies)


def _ride_gather_forward(bufs):
    def copies(ins, outs, send_sems, recv_sems):
        x, y, c = lax.axis_index("x"), lax.axis_index("y"), lax.axis_index("c")
        cps = []
        for a in range(len(bufs)):
            for j, (ox, oy) in enumerate([(1 - x, y), (x, 1 - y), (1 - x, 1 - y)]):
                blk = outs[a].at[4 * ox + 2 * oy + c]
                cps.append(pltpu.make_async_remote_copy(src_ref=blk, dst_ref=blk, send_sem=send_sems.at[3 * a + j],
                                                        recv_sem=recv_sems.at[3 * a + j], device_id=(x, y, 1 - c),
                                                        device_id_type=MESH))
        return cps

    return _Ride(bufs, [_sds(b.shape, b.dtype) for b in bufs], 3 * len(bufs), copies, in_place=True)


def _ride_sibling_halves(gs):
    def copies(ins, outs, send_sems, recv_sems):
        x, y, c = lax.axis_index("x"), lax.axis_index("y"), lax.axis_index("c")
        cps = []
        for a in range(len(gs)):
            r2 = ins[a].shape[1] // 2
            other = ins[a].at[:, pl.ds((1 - c) * r2, r2), :]
            cps.append(pltpu.make_async_remote_copy(src_ref=other, dst_ref=outs[a], send_sem=send_sems.at[a],
                                                    recv_sem=recv_sems.at[a], device_id=(x, y, 1 - c),
                                                    device_id_type=MESH))
        return cps

    return _Ride(gs, [_sds((g.shape[0], g.shape[1] // 2, g.shape[2]), g.dtype) for g in gs], len(gs), copies)


def _pair_sum(g, t, half, name):
    nsh, rows, cols = g.shape
    r2 = rows // 2
    tr = _row_tile(r2, cols)
    nt = r2 // tr

    def body(half_ref, g_ref, t_ref, p32_ref, p16_ref):
        p = g_ref[...] + t_ref[...]
        p32_ref[...] = p
        p16_ref[...] = p.astype(BF16)

    blk = pl.BlockSpec((None, tr, cols), lambda k, i, half_ref: (k, i, 0))
    return pl.pallas_call(
        body, name=name,
        grid_spec=pltpu.PrefetchScalarGridSpec(
            num_scalar_prefetch=1, grid=(nsh, nt),
            in_specs=[pl.BlockSpec((None, tr, cols), lambda k, i, half_ref: (k, half_ref[0] * nt + i, 0)), blk],
            out_specs=[blk, blk]),
        out_shape=[_sds((nsh, r2, cols)), _sds((nsh, r2, cols), BF16)],
        compiler_params=_params("parallel", "parallel"),
    )(half, g, t)


def _ride_chip_exchange(ps):
    def copies(ins, outs, send_sems, recv_sems):
        x, y, c = lax.axis_index("x"), lax.axis_index("y"), lax.axis_index("c")
        chips = [(1 - x, y), (x, 1 - y), (1 - x, 1 - y)]
        cps = []
        for a in range(len(ps)):
            for j, (ox, oy) in enumerate(chips):
                cps.append(pltpu.make_async_remote_copy(src_ref=ins[a].at[2 * ox + oy], dst_ref=outs[a].at[j],
                                                        send_sem=send_sems.at[3 * a + j],
                                                        recv_sem=recv_sems.at[3 * a + j],
                                                        device_id=(ox, oy, c), device_id_type=MESH))
        return cps

    return _Ride(ps, [_sds((3,) + p.shape[1:], p.dtype) for p in ps], 3 * len(ps), copies)


def _chip_sum(p32, u, chip_half, name):
    r2, cols = p32.shape[1:]
    tr = _row_tile(r2, cols)
    nt = r2 // tr

    def body(ch_ref, p_ref, u_ref, o_ref):
        acc = p_ref[...]
        for j in range(3):
            acc = acc + u_ref[j].astype(F32)
        o_ref[...] = acc

    return pl.pallas_call(
        body, name=name,
        grid_spec=pltpu.PrefetchScalarGridSpec(
            num_scalar_prefetch=1, grid=(nt,),
            in_specs=[pl.BlockSpec((None, tr, cols), lambda i, ch_ref: (ch_ref[0], i, 0)),
                      pl.BlockSpec((3, tr, cols), lambda i, ch_ref: (0, i, 0))],
            out_specs=pl.BlockSpec((tr, cols), lambda i, ch_ref: (ch_ref[1] * nt + i, 0))),
        out_shape=_sds((2 * r2, cols)), compiler_params=_params("parallel"),
    )(chip_half, p32, u)


def _sibling_join(fs, name):
    nb = len(fs)

    def body(*refs):
        outs = refs[nb:2 * nb]
        send_sems, recv_sems = refs[2 * nb:]
        x, y, c = lax.axis_index("x"), lax.axis_index("y"), lax.axis_index("c")
        cps = []
        for a in range(nb):
            r2 = outs[a].shape[0] // 2
            rows = outs[a].at[pl.ds(c * r2, r2), :]
            cps.append(pltpu.make_async_remote_copy(src_ref=rows, dst_ref=rows, send_sem=send_sems.at[a],
                                                    recv_sem=recv_sems.at[a], device_id=(x, y, 1 - c),
                                                    device_id_type=MESH))
        for cp in cps:
            cp.start()
        for cp in cps:
            cp.wait()

    return pl.pallas_call(
        body, name=name, in_specs=[ANY] * nb, out_specs=[ANY] * nb,
        out_shape=[_sds(f.shape, f.dtype) for f in fs],
        input_output_aliases={a: a for a in range(nb)},
        scratch_shapes=[_dma_sems(nb), _dma_sems(nb)],
    )(*fs)


def _row_tile(rows, cols):
    tile = rows
    while tile * cols * 4 > (1 << 20) and tile % 16 == 0:
        tile //= 2
    return tile


def _w_ada_grad(c_all, dmod_cols):
    def body(c_ref, d_ref, o_ref):
        o_ref[...] = _dot_tn(c_ref[...].astype(BF16), d_ref[...].astype(BF16))

    return pl.pallas_call(body, name="w_ada_grad", out_shape=_sds((c_all.shape[1], dmod_cols.shape[1])),
                          compiler_params=_params())(c_all, dmod_cols)


def _adam_math(w, g, m, v):
    nm = ADAM_B1 * m + (1.0 - ADAM_B1) * g
    nv = ADAM_B2 * v + (1.0 - ADAM_B2) * (g * g)
    m_hat = nm / (1.0 - ADAM_B1 ** ADAM_STEP)
    v_hat = nv / (1.0 - ADAM_B2 ** ADAM_STEP)
    return -ADAM_LR * (m_hat / (jnp.sqrt(v_hat) + ADAM_EPS) + ADAM_WD * w), nm, nv


def _adamw(w, g, m, v, name):
    rows, cols = w.shape
    tr = _row_tile(rows, cols)

    def body(w_ref, g_ref, m_ref, v_ref, go_ref, d_ref, nm_ref, nv_ref):
        gv = g_ref[...]
        go_ref[...] = gv
        d_ref[...], nm_ref[...], nv_ref[...] = _adam_math(w_ref[...], gv, m_ref[...], v_ref[...])

    spec = pl.BlockSpec((tr, cols), lambda i: (i, 0))
    return pl.pallas_call(
        body, name=name, grid=(rows // tr,), in_specs=[spec] * 4, out_specs=[spec] * 4,
        out_shape=[_sds((rows, cols))] * 4, compiler_params=_params("parallel"),
    )(w, g, m, v)


def _pack_small(dmod, dnorm_g, dfinal_g, dpool_scale, drel_bias, loss, dpool_w):
    return jnp.concatenate([dmod.reshape(-1, 128), dnorm_g.reshape(-1, 128), dfinal_g.reshape(-1, 128),
                            jnp.pad(dpool_scale.reshape(-1, 128), ((0, PK_RELB - PK_PSCALE - AW // 128), (0, 0))),
                            jnp.pad(drel_bias, ((0, 0), (0, 128 - NG * NH))),
                            jnp.full((PK_POOLW - PK_LOSS, 128), loss, F32), dpool_w.reshape(-1, 128)], axis=0)


def _small_update(small_all, ws, ms, vs):
    lane_rows = [(r0, r0 + w.shape[1] // 128) for r0, w in zip((PK_BADA, PK_NORMG, PK_FINALG, PK_PSCALE), ws)]
    nw = len(ws)

    def body(all_ref, *refs):
        w_refs, m_refs, v_refs = refs[:nw], refs[nw:2 * nw], refs[2 * nw:3 * nw]
        loss_ref, outs = refs[3 * nw], refs[3 * nw + 1:]
        g = all_ref[0]
        for s in range(1, all_ref.shape[0]):
            g = g + all_ref[s]
        loss_ref[...] = jnp.broadcast_to(g[PK_LOSS:PK_LOSS + 1, :], loss_ref.shape)

        def put(p, at, gv):
            d, nm, nv = _adam_math(w_refs[p][at], gv, m_refs[p][at], v_refs[p][at])
            for o_ref, val in zip(outs[4 * p:4 * p + 4], (gv, d, nm, nv)):
                o_ref[at] = val

        for p, (r0, r1) in enumerate(lane_rows):
            for i in range(r1 - r0):
                put(p, (slice(None), slice(128 * i, 128 * (i + 1))), g[r0 + i:r0 + i + 1, :])
        put(4, (slice(None), slice(None)), g[PK_RELB:PK_LOSS, 0:NG * NH])
        put(5, (slice(None), slice(None)), g[PK_POOLW:PK_ROWS, :])

    res = pl.pallas_call(
        body, name="small_update",
        out_shape=[_sds((8, 128))] + [_sds(w.shape) for w in ws for _ in range(4)], compiler_params=_params(),
    )(small_all, *ws, *ms, *vs)
    return res[0], [res[1 + 4 * p:5 + 4 * p] for p in range(nw)]


def kernel(x, c, norm_g, w_ada, b_ada, w_in, pool_w, pool_scale, w_attn_br, w_pool_br, w_out, rel_bias, final_g, loss_target, m_norm_g, m_w_ada, m_b_ada, m_w_in, m_pool_w, m_pool_scale, m_w_attn_br, m_w_pool_br, m_w_out, m_rel_bias, m_final_g, v_norm_g, v_w_ada, v_b_ada, v_w_in, v_pool_w, v_pool_scale, v_w_attn_br, v_w_pool_br, v_w_out, v_rel_bias, v_final_g):
    ix, iy, ic = lax.axis_index("x"), lax.axis_index("y"), lax.axis_index("c")
    dev = 4 * ix + 2 * iy + ic
    chip = 2 * ix + iy

    def half(w):
        r2 = w.shape[0] // 2
        return lax.dynamic_slice_in_dim(w, ic * r2, r2, axis=0).astype(BF16)

    gathered = _allgather8([jnp.broadcast_to(c, (8, D)), half(w_in[0])], "gather_weights", relay=[False, True])
    c_all = gathered[0][:, 0, :]
    wg_in = gathered[1].reshape(N_SHARD, D, IN_W // N_SHARD)
    buckets = jnp.asarray(_bucket_tables())
    bias_tab, late = _bias_table(rel_bias, buckets,
                                 _ride_gather_send([half(w_attn_br[0]), half(w_pool_br[0]), half(w_out[0])]))

    mw = 3 * D // N_SHARD
    modp = _mod_partial(c_all, w_ada[0], lax.dynamic_slice_in_dim(b_ada, chip * mw, mw, axis=1))
    mod_all = _allgather8([modp], "gather_mod")[0]
    mod_full = mod_all[::2].transpose(1, 0, 2).reshape(8, 3 * D)
    mod = lax.dynamic_slice_in_dim(mod_full, dev, 1, axis=0)
    h, late = _prenorm(x[0], norm_g, mod, _ride_gather_forward(late))
    wab = late[0].reshape(N_SHARD, AW, D // N_SHARD).transpose(1, 0, 2).reshape(AW, D)
    wpb = late[1].reshape(N_SHARD, AW, D // N_SHARD).transpose(1, 0, 2).reshape(AW, D)
    wout = late[2].reshape(D, D)

    half_idx = jnp.stack([ic]).astype(jnp.int32)
    chip_half = jnp.stack([chip, ic]).astype(jnp.int32)
    r = _local_step(x[0], loss_target[0], mod, h, bias_tab, buckets, wg_in, wab, wpb, wout, pool_w[0], pool_scale,
                    norm_g, final_g.reshape(1, D), half_idx, chip_half)

    packed = _pack_small(r["dmod"], r["dnorm_g"], r["dfinal_g"], r["dpool_scale"], r["drel_bias"], r["loss"],
                         r["dpool_w"])
    small_all = _allgather8([packed], "gather_small")[0]
    small = ["b_ada", "norm_g", "final_g", "pool_scale", "rel_bias", "pool_w"]
    shaped = lambda b, n, f, ps, rb, pw: [b, n, f.reshape(1, D), ps, rb, pw.reshape(4 * PGW, PGW)]
    loss, small_out = _small_update(small_all, shaped(b_ada, norm_g, final_g, pool_scale, rel_bias, pool_w),
                                    shaped(m_b_ada, m_norm_g, m_final_g, m_pool_scale, m_rel_bias, m_pool_w),
                                    shaped(v_b_ada, v_norm_g, v_final_g, v_pool_scale, v_rel_bias, v_pool_w))
    dmod_all = small_all[:, PK_BADA:PK_NORMG, :].reshape(8, 3 * D)
    g_w_ada = _w_ada_grad(c_all, lax.dynamic_slice_in_dim(dmod_all, chip * mw, mw, axis=1))

    g_w_in, g_w_ab, g_w_pb, g_w_out = _sibling_join([r["rs_in"], r["rs_attn_br"], r["rs_pool_br"], r["rs_out"]],
                                                    "rs_sibling_join")
    upd = dict(zip(small, small_out))
    upd["final_g"] = [a.reshape(D) for a in upd["final_g"]]
    upd["pool_w"] = [a.reshape(1, 4, PGW, PGW) for a in upd["pool_w"]]
    for nme, w, g, m, v in (("w_ada", w_ada, g_w_ada, m_w_ada, v_w_ada), ("w_in", w_in, g_w_in, m_w_in, v_w_in),
                            ("w_attn_br", w_attn_br, g_w_ab, m_w_attn_br, v_w_attn_br),
                            ("w_pool_br", w_pool_br, g_w_pb, m_w_pool_br, v_w_pool_br),
                            ("w_out", w_out, g_w_out, m_w_out, v_w_out)):
        upd[nme] = [a[None] for a in _adamw(w[0], g, m[0], v[0], "adamw_" + nme)]
    names = ["norm_g", "w_ada", "b_ada", "w_in", "pool_w", "pool_scale", "w_attn_br", "w_pool_br", "w_out",
             "rel_bias", "final_g"]
    return (loss[0, 0], r["grad_x"][None]) + tuple(upd[nme][kind] for kind in range(4) for nme in names)
```

```python
import functools
import math

import numpy as np
import jax
import jax.numpy as jnp
from jax import lax
from jax.experimental import pallas as pl
from jax.experimental.pallas import tpu as pltpu

F32 = jnp.float32
BF16 = jnp.bfloat16

D = 1024
HD = 64
NH = 8
AW = NH * HD
GROUPS = ((128, 1), (512, 4), (2048, 16))
NG = len(GROUPS)
BLK = 128
GW = 3 * AW
QKV_W = NG * GW
REST_W = 3584
IN_W = QKV_W + REST_W
CB = 512
NCB = IN_W // CB
NCB_QKV = QKV_W // CB
POOL_WINDOWS = (2, 4, 8, 16)
PGW = 128
HALO = 16
NUM_BUCKETS = 32
MAX_DISTANCE = 2048
EPS = 1e-6
NEG = -1e30
N_SHARD = 4
VMEM_LIMIT = 56 * 1024 * 1024

ADAM_LR = 0.001
ADAM_B1 = 0.9
ADAM_B2 = 0.999
ADAM_EPS = 1e-08
ADAM_WD = 0.01
ADAM_STEP = 10

PK_BADA, PK_NORMG, PK_FINALG, PK_PSCALE, PK_RELB, PK_LOSS, PK_POOLW, PK_ROWS = 0, 24, 32, 40, 48, 80, 88, 600

ANY = pl.BlockSpec(memory_space=pl.ANY)
MESH = pl.DeviceIdType.MESH


def _params(*sem):
    return pltpu.CompilerParams(dimension_semantics=sem, vmem_limit_bytes=VMEM_LIMIT)


def _sds(shape, dtype=F32):
    return jax.ShapeDtypeStruct(shape, dtype)


def _dot(a, b):
    return jnp.dot(a, b, preferred_element_type=F32)


def _dot_nt(a, b):
    return lax.dot_general(a, b, (((1,), (1,)), ((), ())), preferred_element_type=F32)


def _dot_tn(a, b):
    return lax.dot_general(a, b, (((0,), (0,)), ((), ())), preferred_element_type=F32)


def _sigmoid(z):
    return 0.5 * jnp.tanh(0.5 * z) + 0.5


def _dma_sems(*shape):
    return pltpu.SemaphoreType.DMA(shape)


class _Ride:
    def __init__(self, arrays, out_shapes, n_copies, copies, in_place=False):
        self.arrays, self.out_shapes, self.n_copies, self.copies = list(arrays), list(out_shapes), n_copies, copies
        self.in_place = in_place


def _call_with_ride(body, ride, first, last, *, in_specs, out_specs, out_shape, scratch_shapes=(), **kw):
    in_specs, out_specs, out_shape, scratch_shapes = list(in_specs), list(out_specs), list(out_shape), list(scratch_shapes)
    n_in, n_out, n_sc = len(in_specs), len(out_specs), len(scratch_shapes)
    if ride is None:
        def run_plain(*operands):
            return pl.pallas_call(body, in_specs=in_specs, out_specs=out_specs, out_shape=out_shape,
                                  scratch_shapes=scratch_shapes, **kw)(*operands), []
        return run_plain
    n_ri, n_ro = len(ride.arrays), len(ride.out_shapes)

    def wrapped(*refs):
        ins, rest = refs[:n_in], refs[n_in:]
        r_ins, rest = rest[:n_ri], rest[n_ri:]
        outs, rest = rest[:n_out], rest[n_out:]
        r_outs, rest = rest[:n_ro], rest[n_ro:]
        scratch, (send_sems, recv_sems) = rest[:n_sc], rest[n_sc:]

        @pl.when(first())
        def _():
            for cp in ride.copies(r_ins, r_outs, send_sems, recv_sems):
                cp.start()

        body(*ins, *outs, *scratch)

        @pl.when(last())
        def _():
            for cp in ride.copies(r_ins, r_outs, send_sems, recv_sems):
                cp.wait()

    def run(*operands):
        res = pl.pallas_call(
            wrapped, in_specs=in_specs + [ANY] * n_ri, out_specs=out_specs + [ANY] * n_ro,
            out_shape=out_shape + ride.out_shapes,
            scratch_shapes=scratch_shapes + [_dma_sems(ride.n_copies), _dma_sems(ride.n_copies)],
            input_output_aliases={n_in + a: n_out + a for a in range(n_ri)} if ride.in_place else {}, **kw,
        )(*operands, *ride.arrays)
        return res[:n_out], res[n_out:]
    return run


def _bucket_tables():
    i = np.arange(BLK)[:, None]
    j = np.arange(2 * BLK)[None, :]
    dist = BLK + i - j
    valid = (dist >= 0) & (dist <= BLK)
    tabs = []
    for _, dil in GROUPS:
        n = (np.clip(dist, 0, BLK) * dil).astype(np.int32)
        max_exact = NUM_BUCKETS // 2
        nf = np.maximum(n, 1).astype(np.float32)
        large = max_exact + (np.log(nf / np.float32(max_exact)) / np.float32(math.log(MAX_DISTANCE / max_exact))
                             * np.float32(NUM_BUCKETS - max_exact)).astype(np.int32)
        large = np.minimum(large, NUM_BUCKETS - 1)
        bucket = np.where(n < max_exact, n, large)
        tab = np.where(valid, bucket, -1).astype(np.int32)
        perm = _block_perm(dil)
        tabs.append(tab[perm][:, np.concatenate([perm, BLK + perm])])
    return np.stack(tabs)


def _bias_table(rel_bias, buckets, ride):
    def body(rb_ref, bk_ref, out_ref):
        g = pl.program_id(0)
        bk = bk_ref[...]
        for h in range(NH):
            acc = jnp.full((BLK, 2 * BLK), NEG, F32)
            for b in range(NUM_BUCKETS):
                acc = jnp.where(bk == b, rb_ref[b, g * NH + h], acc)
            out_ref[h] = acc

    (tab,), rode = _call_with_ride(
        body, ride, lambda: pl.program_id(0) == 0, lambda: pl.program_id(0) == NG - 1,
        name="bias_table", grid=(NG,),
        in_specs=[pl.BlockSpec(memory_space=pltpu.SMEM),
                  pl.BlockSpec((None, BLK, 2 * BLK), lambda g: (g, 0, 0))],
        out_specs=[pl.BlockSpec((NH, BLK, 2 * BLK), lambda g: (g, 0, 0))],
        out_shape=[_sds((NG * NH, BLK, 2 * BLK))],
        compiler_params=_params("arbitrary"),
    )(rel_bias, buckets)
    return tab, rode


def _bias_grad(ds_acc, buckets, ride):
    def body(acc_ref, bk_ref, out_ref):
        bk = bk_ref[...]
        acc = acc_ref[...]
        lane = lax.broadcasted_iota(jnp.int32, (8, 128), 1)
        out = jnp.zeros((8, 128), F32)
        for b in range(NUM_BUCKETS):
            val = jnp.sum(jnp.where(bk == b, acc, 0.0))
            out = jnp.where(lane == b, val, out)
        out_ref[...] = out

    (out,), rode = _call_with_ride(
        body, ride, lambda: pl.program_id(0) == 0, lambda: pl.program_id(0) == NG * NH - 1,
        name="bias_grad", grid=(NG * NH,),
        in_specs=[pl.BlockSpec((None, BLK, 2 * BLK), lambda gh: (gh, 0, 0)),
                  pl.BlockSpec((None, BLK, 2 * BLK), lambda gh: (gh // NH, 0, 0))],
        out_specs=[pl.BlockSpec((None, 8, 128), lambda gh: (gh, 0, 0))],
        out_shape=[_sds((NG * NH, 8, 128))],
        compiler_params=_params("arbitrary"),
    )(ds_acc, buckets)
    return out, rode


def _mod_partial(c_all, w_ada_s, b_ada_s):
    def body(c_ref, w_ref, b_ref, o_ref):
        o_ref[...] = _dot(c_ref[...].astype(BF16), w_ref[...].astype(BF16)) + b_ref[...]

    return pl.pallas_call(body, name="mod_partial", out_shape=_sds((8, w_ada_s.shape[1])),
                          compiler_params=_params())(c_all, w_ada_s, b_ada_s)


def _prenorm(x, norm_g, mod, ride):
    S = x.shape[0]
    tm = 512
    nt = S // tm

    def body(x_ref, g_ref, mod_ref, h_ref):
        xv = x_ref[...]
        r = lax.rsqrt(jnp.mean(xv * xv, axis=-1, keepdims=True) + EPS)
        n1 = xv * r * g_ref[...]
        h_ref[...] = (n1 * (1.0 + mod_ref[:, D:2 * D]) + mod_ref[:, 0:D]).astype(BF16)

    (h,), rode = _call_with_ride(
        body, ride, lambda: pl.program_id(0) == 0, lambda: pl.program_id(0) == nt - 1,
        name="prenorm", grid=(nt,),
        in_specs=[pl.BlockSpec((tm, D), lambda i: (i, 0)), pl.BlockSpec((1, D), lambda i: (0, 0)),
                  pl.BlockSpec((1, 3 * D), lambda i: (0, 0))],
        out_specs=[pl.BlockSpec((tm, D), lambda i: (i, 0))],
        out_shape=[_sds((S, D), BF16)], compiler_params=_params("arbitrary"),
    )(x, norm_g, mod)
    return h, rode


def _proj(h, wg_in, j0, nj, dtype, name):
    S = h.shape[0]
    tm = 2048
    per = wg_in.shape[2] // CB

    def body(h_ref, w_ref, o_ref):
        o_ref[...] = _dot(h_ref[...], w_ref[...]).astype(dtype)

    return pl.pallas_call(
        body, name=name, grid=(S // tm, nj),
        in_specs=[pl.BlockSpec((tm, D), lambda m, j: (m, 0)),
                  pl.BlockSpec((None, D, CB), lambda m, j: ((j0 + j) // per, 0, (j0 + j) % per))],
        out_specs=pl.BlockSpec((tm, CB), lambda m, j: (m, j)),
        out_shape=_sds((S, nj * CB), dtype), compiler_params=_params("parallel", "parallel"),
    )(h, wg_in)


HS = 4
SLAB = HS * HD


def _lane_head(rows):
    return lax.broadcasted_iota(jnp.int32, (rows, SLAB), 1) // HD


def _head_stack(a):
    head = _lane_head(a.shape[0])
    return jnp.concatenate([jnp.where(head == h, a, jnp.zeros_like(a)) for h in range(HS)], axis=0)


def _head_unstack(a):
    rows = a.shape[0] // HS
    head = _lane_head(rows)
    out = a[:rows]
    for h in range(1, HS):
        out = jnp.where(head == h, a[h * rows:(h + 1) * rows], out)
    return out


STAT_W = 128
VIEW = 16


def _sub_layout(dil):
    if dil == 1:
        return BLK, [None]
    return BLK * dil // VIEW, [[r + dil * u for u in range(VIEW // dil)] for r in range(dil)]


def _block_perm(dil):
    a_rows, _ = _sub_layout(dil)
    p = np.arange(BLK)
    return p if dil == 1 else (VIEW // dil) * (p % a_rows) + p // a_rows


LB = 128
N_SLAB = NH // HS


def _ld(refs, bs, s, w):
    if bs is None:
        return refs[0][:, s * w:(s + 1) * w]
    a_rows = refs[0].shape[0] // VIEW
    return jnp.concatenate([jnp.concatenate([ref[pl.ds(b, a_rows, stride=VIEW), :] for b in bs], axis=0)
                            for ref in refs], axis=1)


def _st(ref, bs, s, val):
    if bs is None:
        ref[:, s * SLAB:(s + 1) * SLAB] = val
        return
    a_rows = val.shape[0] // len(bs)
    for u, b in enumerate(bs):
        ref[:, b, :] = val[u * a_rows:(u + 1) * a_rows]


def _attn_views(dil, S):
    a_rows, subs = _sub_layout(dil)
    if dil == 1:
        def ispecs(base, w, f):
            return [pl.BlockSpec((BLK, N_SLAB * w), lambda sg, n: (f(n), base // (N_SLAB * w)))]
        return subs, S // BLK, N_SLAB, ispecs, (lambda w: (S, w)), (
            lambda f: pl.BlockSpec((BLK, AW), lambda sg, n: (f(n), 0)))

    def ispecs(base, w, f):
        return [pl.BlockSpec((a_rows * VIEW, LB), lambda sg, n, k=k: (f(n), (base + sg * w) // LB + k))
                for k in range(w // LB)]
    return subs, S // (a_rows * VIEW), 1, ispecs, (lambda w: (S // VIEW, VIEW, w)), (
        lambda f: pl.BlockSpec((a_rows, VIEW, SLAB), lambda sg, n: (f(n), 0, sg)))


def _attn_fwd(qkv_g, bias_tab, g):
    S = qkv_g.shape[0]
    subs, nbq, sps, ispecs, shape, ospec = _attn_views(GROUPS[g][1], S)
    cur, prev = (lambda n: n), (lambda n: jnp.maximum(n - 1, 0))
    in_specs = [ispecs(0, SLAB, cur), ispecs(AW, SLAB, prev), ispecs(AW, SLAB, cur), ispecs(2 * AW, SLAB, prev),
                ispecs(2 * AW, SLAB, cur)]
    nl = len(in_specs[0])

    def body(*refs):
        q, kp, kc, vp, vc = (refs[t * nl:(t + 1) * nl] for t in range(5))
        b_ref, o_ref, l_ref = refs[5 * nl:]
        n = pl.program_id(1)
        col = lax.broadcasted_iota(jnp.int32, (HS * BLK, 2 * BLK), 1)
        keep = (col >= BLK) | (n > 0)
        for s_ in range(sps):
            bias = b_ref[pl.ds(s_ * HS, HS)].reshape(HS * BLK, 2 * BLK)
            for bs in subs:
                kb = jnp.concatenate([_ld(kp, bs, s_, SLAB), _ld(kc, bs, s_, SLAB)], axis=0).astype(BF16)
                vb = jnp.concatenate([_ld(vp, bs, s_, SLAB), _ld(vc, bs, s_, SLAB)], axis=0).astype(BF16)
                s = _dot_nt(_head_stack(_ld(q, bs, s_, SLAB).astype(BF16)), kb) * (HD ** -0.5) + bias
                s = jnp.where(keep, s, NEG)
                m = jnp.max(s, axis=-1, keepdims=True)
                p = jnp.exp(s - m)
                den = jnp.sum(p, axis=-1, keepdims=True)
                _st(o_ref, bs, s_, _head_unstack(_dot(p.astype(BF16), vb) / den))
                _st(l_ref, bs, s_, _head_unstack(jnp.broadcast_to(m + jnp.log(den), (HS * BLK, SLAB))))

    out = _sds(shape(AW))
    o, l = pl.pallas_call(
        body, name=f"attn_fwd{g}", grid=(N_SLAB // sps, nbq),
        in_specs=sum(in_specs, []) + [pl.BlockSpec((sps * HS, BLK, 2 * BLK),
                                                   lambda sg, n: (g * (N_SLAB // sps) + sg, 0, 0))],
        out_specs=[ospec(cur), ospec(cur)],
        out_shape=[out, out], compiler_params=_params("parallel", "arbitrary"),
    )(*([qkv_g] * (5 * nl)), bias_tab)
    return o.reshape(S, AW), l.reshape(S, AW)


def _attn_bwd(qkv_g, dattn, stats, bias_tab, g, ride):
    S = qkv_g.shape[0]
    subs, nbq, sps, ispecs, shape, ospec = _attn_views(GROUPS[g][1], S)
    cur = lambda n: jnp.minimum(n, nbq - 1)
    prev = lambda n: jnp.clip(n - 1, 0, nbq - 1)
    late = lambda n: jnp.maximum(n - 1, 0)
    in_specs = [ispecs(0, SLAB, cur), ispecs(AW, SLAB, prev), ispecs(AW, SLAB, cur), ispecs(2 * AW, SLAB, prev),
                ispecs(2 * AW, SLAB, cur), ispecs(0, SLAB, cur), ispecs(0, STAT_W, cur)]
    nl = len(in_specs[0])

    def body(*refs):
        q, kp, kc, vp, vc, da = (refs[t * nl:(t + 1) * nl] for t in range(6))
        st_ref, b_ref, dq_ref, dk_ref, dv_ref, ds_ref, ck_ref, cv_ref = refs[6 * nl:]
        n = pl.program_id(1)

        @pl.when(n == 0)
        def _():
            ds_ref[...] = jnp.zeros_like(ds_ref)
            ck_ref[...] = jnp.zeros_like(ck_ref)
            cv_ref[...] = jnp.zeros_like(cv_ref)

        @pl.when(n < nbq)
        def _():
            col = lax.broadcasted_iota(jnp.int32, (HS * BLK, 2 * BLK), 1)
            keep = (col >= BLK) | (n > 0)
            for s_ in range(sps):
                cs = slice(s_ * SLAB, (s_ + 1) * SLAB)
                bias = b_ref[pl.ds(s_ * HS, HS)].reshape(HS * BLK, 2 * BLK)
                for i, bs in enumerate(subs):
                    st = _ld((st_ref,), bs, s_, STAT_W)
                    kb = jnp.concatenate([_ld(kp, bs, s_, SLAB), _ld(kc, bs, s_, SLAB)], axis=0).astype(BF16)
                    vb = jnp.concatenate([_ld(vp, bs, s_, SLAB), _ld(vc, bs, s_, SLAB)], axis=0).astype(BF16)
                    lse = jnp.concatenate([st[:, h:h + 1] for h in range(HS)], axis=0)
                    delta = jnp.concatenate([st[:, HS + h:HS + h + 1] for h in range(HS)], axis=0)
                    qs = _head_stack(_ld(q, bs, s_, SLAB).astype(BF16))
                    dos = _head_stack(_ld(da, bs, s_, SLAB).astype(BF16))
                    s = _dot_nt(qs, kb) * (HD ** -0.5) + bias
                    s = jnp.where(keep, s, NEG)
                    p = jnp.exp(s - lse)
                    ds = p * (_dot_nt(dos, vb) - delta)
                    ds_ref[pl.ds(s_ * HS, HS)] += ds.reshape(HS, BLK, 2 * BLK)
                    ds_b = (ds * (HD ** -0.5)).astype(BF16)
                    _st(dq_ref, bs, s_, _head_unstack(_dot(ds_b, kb)))
                    dkb = _dot_tn(ds_b, qs)
                    dvb = _dot_tn(p.astype(BF16), dos)
                    _st(dk_ref, bs, s_, ck_ref[i, :, cs] + dkb[:BLK])
                    _st(dv_ref, bs, s_, cv_ref[i, :, cs] + dvb[:BLK])
                    ck_ref[i, :, cs] = dkb[BLK:]
                    cv_ref[i, :, cs] = dvb[BLK:]

        @pl.when(n == nbq)
        def _():
            for s_ in range(sps):
                for i, bs in enumerate(subs):
                    _st(dk_ref, bs, s_, ck_ref[i, :, s_ * SLAB:(s_ + 1) * SLAB])
                    _st(dv_ref, bs, s_, cv_ref[i, :, s_ * SLAB:(s_ + 1) * SLAB])

    out = _sds(shape(AW))
    nsg = N_SLAB // sps
    (dq, dk, dv, ds_acc), rode = _call_with_ride(
        body, ride, lambda: (pl.program_id(0) == 0) & (pl.program_id(1) == 0),
        lambda: (pl.program_id(0) == nsg - 1) & (pl.program_id(1) == nbq),
        name=f"attn_bwd{g}", grid=(nsg, nbq + 1),
        in_specs=sum(in_specs, []) + [pl.BlockSpec((sps * HS, BLK, 2 * BLK), lambda sg, n: (g * nsg + sg, 0, 0))],
        out_specs=[ospec(cur), ospec(late), ospec(late),
                   pl.BlockSpec((sps * HS, BLK, 2 * BLK), lambda sg, n: (sg, 0, 0))],
        out_shape=[out] * 3 + [_sds((NH, BLK, 2 * BLK))],
        scratch_shapes=[pltpu.VMEM((len(subs), BLK, sps * SLAB), F32), pltpu.VMEM((len(subs), BLK, sps * SLAB), F32)],
        compiler_params=_params("arbitrary", "arbitrary"),
    )(*([qkv_g] * (5 * nl)), *([dattn] * nl), stats, bias_tab)
    return [dq.reshape(S, AW), dk.reshape(S, AW), dv.reshape(S, AW)], ds_acc, rode


TM_MIX = 256


def _mix_specs(tm):
    row512 = pl.BlockSpec((tm, AW), lambda i: (i, 0))
    return ([row512] * 6 + [
        pl.BlockSpec((tm, REST_W), lambda i: (i, 0)),
        pl.BlockSpec((HALO, AW), lambda i: (jnp.maximum(i * (tm // HALO) - 1, 0), 1)),
        pl.BlockSpec((AW, D), lambda i: (0, 0)), pl.BlockSpec((AW, D), lambda i: (0, 0)),
        pl.BlockSpec((4, PGW, PGW), lambda i: (0, 0, 0)), pl.BlockSpec((1, AW), lambda i: (0, 0))])


def _mix_forward(i, tm, o_refs, l_refs, rest_ref, halo_ref, wab_ref, wpb_ref, pw_ref, ps_ref):
    l0, l1, l2 = (r[...] for r in l_refs)
    mx = jnp.maximum(jnp.maximum(l0, l1), l2)
    e0, e1, e2 = jnp.exp(l0 - mx), jnp.exp(l1 - mx), jnp.exp(l2 - mx)
    den = e0 + e1 + e2
    lj = mx + jnp.log(den)
    attn = (e0 * o_refs[0][...] + e1 * o_refs[1][...] + e2 * o_refs[2][...]) / den

    z_attn = rest_ref[:, 0:AW]
    u = rest_ref[:, AW:2 * AW]
    z_pool = rest_ref[:, 2 * AW:3 * AW]
    g_attn = rest_ref[:, 3 * AW:3 * AW + D]
    g_pool = rest_ref[:, 3 * AW + D:3 * AW + 2 * D]

    sg_a = _sigmoid(z_attn)
    sil_a = z_attn * sg_a
    a_g = (attn * sil_a).astype(BF16)
    y_attn = _dot(a_g, wab_ref[...])

    halo = jnp.where(i > 0, halo_ref[...], 0.0)
    ext = jnp.concatenate([halo, u], axis=0)
    t = i * tm + lax.broadcasted_iota(jnp.int32, (tm, 1), 0)
    pooled, mixed_raw = [], []
    for gi, win in enumerate(POOL_WINDOWS):
        s = ext[:, gi * PGW:(gi + 1) * PGW]
        sh = 1
        while sh < win:
            s = s + pltpu.roll(s, sh, 0)
            sh *= 2
        cnt = jnp.minimum(t + 1, win).astype(F32)
        pg = s[HALO:] / cnt - u[:, gi * PGW:(gi + 1) * PGW]
        pooled.append(pg.astype(BF16))
        mixed_raw.append(_dot(pooled[-1], pw_ref[gi].astype(BF16)))
    mixed_raw = jnp.concatenate(mixed_raw, axis=1)
    mixed = mixed_raw * ps_ref[...]
    sg_p = _sigmoid(z_pool)
    sil_p = z_pool * sg_p
    m_g = (mixed * sil_p).astype(BF16)
    y_pool = _dot(m_g, wpb_ref[...])

    sa = _sigmoid(g_attn)
    sp = _sigmoid(g_pool)
    merged = sa * y_attn + sp * y_pool
    return dict(lj=lj, attn=attn, z_attn=z_attn, z_pool=z_pool, sg_a=sg_a, sil_a=sil_a, a_g=a_g, y_attn=y_attn,
                pooled=pooled, mixed_raw=mixed_raw, mixed=mixed, sg_p=sg_p, sil_p=sil_p, m_g=m_g, y_pool=y_pool,
                sa=sa, sp=sp, merged=merged)


def _tail(x, target, os_, ls_, rest, wab, wpb, pool_w, pool_scale, wout, mod, final_g):
    S = x.shape[0]
    tm = TM_MIX

    def body(o0, o1, o2, l0, l1, l2, rest_ref, halo_ref, wab_ref, wpb_ref, pw_ref, ps_ref,
             x_ref, t_ref, wo_ref, mod_ref, fg_ref, dx2_ref, dmo_ref, loss_ref, dfg_ref, dgate_ref):
        i = pl.program_id(0)

        @pl.when(i == 0)
        def _():
            loss_ref[...] = jnp.zeros_like(loss_ref)
            dfg_ref[...] = jnp.zeros_like(dfg_ref)
            dgate_ref[...] = jnp.zeros_like(dgate_ref)

        f = _mix_forward(i, tm, (o0, o1, o2), (l0, l1, l2), rest_ref, halo_ref, wab_ref, wpb_ref, pw_ref, ps_ref)
        mo = _dot(f["merged"].astype(BF16), wo_ref[...])
        gate = mod_ref[:, 2 * D:3 * D]
        fg = fg_ref[...]
        x2 = x_ref[...] + gate * mo
        r2 = lax.rsqrt(jnp.mean(x2 * x2, axis=-1, keepdims=True) + EPS)
        n2 = x2 * r2
        err = n2 * fg - t_ref[...]
        loss_ref[...] += 0.5 * jnp.sum(jnp.mean(err * err, axis=-1, keepdims=True))
        dy = err * (1.0 / D)
        dfg_ref[...] += jnp.sum(dy * n2, axis=0, keepdims=True)
        dn = dy * fg
        dx2 = r2 * (dn - n2 * jnp.mean(dn * n2, axis=-1, keepdims=True))
        dgate_ref[...] += jnp.sum(dx2 * mo, axis=0, keepdims=True)
        dx2_ref[...] = dx2
        dmo_ref[...] = (dx2 * gate).astype(BF16)

    row = pl.BlockSpec((tm, D), lambda i: (i, 0))
    vec = pl.BlockSpec((1, D), lambda i: (0, 0))
    return pl.pallas_call(
        body, name="tail", grid=(S // tm,),
        in_specs=_mix_specs(tm) + [row, row, pl.BlockSpec((D, D), lambda i: (0, 0)),
                                   pl.BlockSpec((1, 3 * D), lambda i: (0, 0)), vec],
        out_specs=[row, row, pl.BlockSpec((8, 128), lambda i: (0, 0)), vec, vec],
        out_shape=[_sds((S, D)), _sds((S, D), BF16), _sds((8, 128)), _sds((1, D)), _sds((1, D))],
        compiler_params=_params("arbitrary"),
    )(*os_, *ls_, rest, rest, wab, wpb, pool_w, pool_scale, x, target, wout, mod, final_g)


def _mix_bwd(dmo, os_, ls_, rest, wab, wpb, pool_w, pool_scale, wout):
    S = dmo.shape[0]
    tm = TM_MIX
    nt = S // tm
    sw = D // N_SHARD

    def body(o0, o1, o2, l0, l1, l2, rest_ref, halo_ref, wab_ref, wpb_ref, pw_ref, ps_ref, dmo_ref, wo_ref,
             dattn_ref, stats_ref, dpooled_ref, drest_ref, dwo_hbm, dwab_hbm, dwpb_hbm, dpw_ref, dps_ref,
             awo, awab, awpb):
        i = pl.program_id(0)

        @pl.when(i == 0)
        def _():
            awo[...] = jnp.zeros_like(awo)
            awab[...] = jnp.zeros_like(awab)
            awpb[...] = jnp.zeros_like(awpb)
            dpw_ref[...] = jnp.zeros_like(dpw_ref)
            dps_ref[...] = jnp.zeros_like(dps_ref)

        f = _mix_forward(i, tm, (o0, o1, o2), (l0, l1, l2), rest_ref, halo_ref, wab_ref, wpb_ref, pw_ref, ps_ref)
        dmo_b = dmo_ref[...]
        dmerged = _dot_nt(dmo_b, wo_ref[...])
        awo[...] += _dot_tn(f["merged"].astype(BF16), dmo_b)
        sa, sp = f["sa"], f["sp"]
        dya = (dmerged * sa).astype(BF16)
        dyp = (dmerged * sp).astype(BF16)
        dg_attn = dmerged * f["y_attn"] * sa * (1.0 - sa)
        dg_pool = dmerged * f["y_pool"] * sp * (1.0 - sp)
        dag = _dot_nt(dya, wab_ref[...])
        awab[...] += _dot_tn(f["a_g"], dya)
        dmg = _dot_nt(dyp, wpb_ref[...])
        awpb[...] += _dot_tn(f["m_g"], dyp)
        dattn = dag * f["sil_a"]
        dattn_ref[...] = dattn
        prod = dattn * f["attn"]
        lane = lax.broadcasted_iota(jnp.int32, (tm, STAT_W), 1)
        for sb in range(N_SLAB):
            st = jnp.zeros((tm, STAT_W), F32)
            for h in range(HS):
                hs = slice((sb * HS + h) * HD, (sb * HS + h + 1) * HD)
                st = jnp.where(lane == h, f["lj"][:, hs.start:hs.start + 1], st)
                st = jnp.where(lane == HS + h, jnp.sum(prod[:, hs], axis=-1, keepdims=True), st)
            stats_ref[:, sb * STAT_W:(sb + 1) * STAT_W] = st
        dz_attn = dag * f["attn"] * (f["sg_a"] * (1.0 + f["z_attn"] * (1.0 - f["sg_a"])))
        dmixed = dmg * f["sil_p"]
        dz_pool = dmg * f["mixed"] * (f["sg_p"] * (1.0 + f["z_pool"] * (1.0 - f["sg_p"])))
        dps_ref[...] += jnp.sum(dmixed * f["mixed_raw"], axis=0, keepdims=True)
        dpm = (dmixed * ps_ref[...]).astype(BF16)
        for gi in range(len(POOL_WINDOWS)):
            cs = slice(gi * PGW, (gi + 1) * PGW)
            dpw_ref[gi] += _dot_tn(f["pooled"][gi], dpm[:, cs])
            dpooled_ref[:, cs] = _dot_nt(dpm[:, cs], pw_ref[gi].astype(BF16))
        drest_ref[:, 0:AW] = dz_attn.astype(BF16)
        drest_ref[:, AW:2 * AW] = jnp.zeros((tm, AW), BF16)
        drest_ref[:, 2 * AW:3 * AW] = dz_pool.astype(BF16)
        drest_ref[:, 3 * AW:3 * AW + D] = dg_attn.astype(BF16)
        drest_ref[:, 3 * AW + D:3 * AW + 2 * D] = dg_pool.astype(BF16)

        @pl.when(i == nt - 1)
        def _():
            pltpu.sync_copy(awo, dwo_hbm)
            for k in range(N_SHARD):
                pltpu.sync_copy(awab.at[:, pl.ds(k * sw, sw)], dwab_hbm.at[k])
                pltpu.sync_copy(awpb.at[:, pl.ds(k * sw, sw)], dwpb_hbm.at[k])

    row512 = pl.BlockSpec((tm, AW), lambda i: (i, 0))
    outs = pl.pallas_call(
        body, name="mix_bwd", grid=(nt,),
        in_specs=_mix_specs(tm) + [pl.BlockSpec((tm, D), lambda i: (i, 0)), pl.BlockSpec((D, D), lambda i: (0, 0))],
        out_specs=[row512, pl.BlockSpec((tm, N_SLAB * STAT_W), lambda i: (i, 0)), row512,
                   pl.BlockSpec((tm, REST_W), lambda i: (i, 0)), ANY, ANY, ANY,
                   pl.BlockSpec((4, PGW, PGW), lambda i: (0, 0, 0)), pl.BlockSpec((1, AW), lambda i: (0, 0))],
        out_shape=[_sds((S, AW)), _sds((S, N_SLAB * STAT_W)), _sds((S, AW)), _sds((S, REST_W), BF16),
                   _sds((D, D)), _sds((N_SHARD, AW, sw)), _sds((N_SHARD, AW, sw)), _sds((4, PGW, PGW)), _sds((1, AW))],
        scratch_shapes=[pltpu.VMEM((D, D), F32), pltpu.VMEM((AW, D), F32), pltpu.VMEM((AW, D), F32)],
        compiler_params=_params("arbitrary"),
    )(*os_, *ls_, rest, rest, wab, wpb, pool_w, pool_scale, dmo, wout)
    dattn, stats, dpooled, drest, dwo, dwab, dwpb, dpw, dps = outs
    return dattn, stats, dpooled, drest, dwo.reshape(N_SHARD, D // N_SHARD, D), dwab, dwpb, dpw, dps


def _pool_bwd(dpooled):
    S = dpooled.shape[0]
    tm = 512
    nt = S // tm

    def body(dp_ref, nxt_ref, du_ref):
        i = pl.program_id(0)
        t = i * tm + lax.broadcasted_iota(jnp.int32, (tm + HALO, 1), 0)
        nxt = jnp.where(i < nt - 1, nxt_ref[...], 0.0)
        ext = jnp.concatenate([dp_ref[...], nxt], axis=0)
        for gi, win in enumerate(POOL_WINDOWS):
            cs = slice(gi * PGW, (gi + 1) * PGW)
            s = ext[:, cs] / jnp.minimum(t + 1, win).astype(F32)
            sh = 1
            while sh < win:
                s = s + pltpu.roll(s, tm + HALO - sh, 0)
                sh *= 2
            du_ref[:, cs] = (s[:tm] - dp_ref[:, cs]).astype(BF16)

    return pl.pallas_call(
        body, name="pool_bwd", grid=(nt,),
        in_specs=[pl.BlockSpec((tm, AW), lambda i: (i, 0)),
                  pl.BlockSpec((HALO, AW), lambda i: (jnp.minimum((i + 1) * (tm // HALO), S // HALO - 1), 0))],
        out_specs=pl.BlockSpec((tm, AW), lambda i: (i, 0)),
        out_shape=_sds((S, AW), BF16), compiler_params=_params("parallel"),
    )(dpooled, dpooled)


TB = 1024


def _dh(dproj, wg_in, ride):
    S = dproj.shape[0]
    per = wg_in.shape[2] // TB
    nm, nk = S // TB, IN_W // TB

    def body(dp_ref, w_ref, out_ref):
        @pl.when(pl.program_id(1) == 0)
        def _():
            out_ref[...] = jnp.zeros_like(out_ref)

        out_ref[...] += _dot_nt(dp_ref[...], w_ref[...])

    (dh,), rode = _call_with_ride(
        body, ride, lambda: (pl.program_id(0) == 0) & (pl.program_id(1) == 0),
        lambda: (pl.program_id(0) == nm - 1) & (pl.program_id(1) == nk - 1),
        name="dh", grid=(nm, nk),
        in_specs=[pl.BlockSpec((TB, TB), lambda m, kk: (m, kk)),
                  pl.BlockSpec((None, D, TB), lambda m, kk: (kk // per, 0, kk % per))],
        out_specs=[pl.BlockSpec((TB, D), lambda m, kk: (m, 0))],
        out_shape=[_sds((S, D))], compiler_params=_params("arbitrary", "arbitrary"),
    )(dproj, wg_in)
    return dh, rode


def _dw_in(h, dproj):
    S = dproj.shape[0]
    per = IN_W // N_SHARD // TB

    def body(h_ref, dp_ref, out_ref):
        @pl.when(pl.program_id(1) == 0)
        def _():
            out_ref[...] = jnp.zeros_like(out_ref)

        out_ref[...] += _dot_tn(h_ref[...], dp_ref[...])

    return pl.pallas_call(
        body, name="dw_in", grid=(IN_W // TB, S // TB),
        in_specs=[pl.BlockSpec((TB, D), lambda j, kk: (kk, 0)), pl.BlockSpec((TB, TB), lambda j, kk: (kk, j))],
        out_specs=pl.BlockSpec((None, D, TB), lambda j, kk: (j // per, 0, j % per)),
        out_shape=_sds((N_SHARD, D, IN_W // N_SHARD)), compiler_params=_params("parallel", "arbitrary"),
    )(h, dproj)


def _prenorm_bwd(x, dh, dx2, norm_g, mod):
    S = x.shape[0]
    tm = 512

    def body(x_ref, dh_ref, dx2_ref, g_ref, mod_ref, gx_ref, dg_ref, dshift_ref, dscale_ref):
        i = pl.program_id(0)

        @pl.when(i == 0)
        def _():
            dg_ref[...] = jnp.zeros_like(dg_ref)
            dshift_ref[...] = jnp.zeros_like(dshift_ref)
            dscale_ref[...] = jnp.zeros_like(dscale_ref)

        xv = x_ref[...]
        dhv = dh_ref[...]
        g = g_ref[...]
        r = lax.rsqrt(jnp.mean(xv * xv, axis=-1, keepdims=True) + EPS)
        xh = xv * r
        dshift_ref[...] += jnp.sum(dhv, axis=0, keepdims=True)
        dscale_ref[...] += jnp.sum(dhv * (xh * g), axis=0, keepdims=True)
        dn1 = dhv * (1.0 + mod_ref[:, D:2 * D])
        dg_ref[...] += jnp.sum(dn1 * xh, axis=0, keepdims=True)
        dxh = dn1 * g
        gx_ref[...] = dx2_ref[...] + r * (dxh - xh * jnp.mean(dxh * xh, axis=-1, keepdims=True))

    row = pl.BlockSpec((tm, D), lambda i: (i, 0))
    vec = pl.BlockSpec((1, D), lambda i: (0, 0))
    return pl.pallas_call(
        body, name="prenorm_bwd", grid=(S // tm,),
        in_specs=[row, row, row, vec, pl.BlockSpec((1, 3 * D), lambda i: (0, 0))],
        out_specs=[row, vec, vec, vec],
        out_shape=[_sds((S, D)), _sds((1, D)), _sds((1, D)), _sds((1, D))],
        compiler_params=_params("arbitrary"),
    )(x, dh, dx2, norm_g, mod)


def _local_step(x, target, mod, h, bias_tab, buckets, wg_in, wab, wpb, wout, pool_w, pool_scale, norm_g, final_g,
                half_idx, chip_half):
    qkv = [_proj(h, wg_in, 3 * g, 3, F32, f"proj_qkv{g}") for g in range(NG)]
    rest = _proj(h, wg_in, NCB_QKV, REST_W // CB, F32, "proj_rest")
    os_, ls_ = zip(*[_attn_fwd(qkv[g], bias_tab, g) for g in range(NG)])
    dx2, dmo, loss, dfinal_g, dgate = _tail(x, target, os_, ls_, rest, wab, wpb, pool_w, pool_scale, wout, mod, final_g)
    dattn, stats, dpooled, drest, dw_out, dw_ab, dw_pb, dpool_w, dpool_scale = _mix_bwd(
        dmo, os_, ls_, rest, wab, wpb, pool_w, pool_scale, wout)
    du = _pool_bwd(dpooled)

    small = [dw_ab, dw_pb, dw_out]
    dqkv0, ds0, sib_small = _attn_bwd(qkv[0], dattn, stats, bias_tab, 0, _ride_sibling_halves(small))
    p_small = [_pair_sum(g, t, half_idx, f"rs_pair_sum{a}") for a, (g, t) in enumerate(zip(small, sib_small))]
    dqkv1, ds1, u_small = _attn_bwd(qkv[1], dattn, stats, bias_tab, 1,
                                    _ride_chip_exchange([p16 for _, p16 in p_small]))
    rs_ab, rs_pb, rs_out = [_chip_sum(p32, u, chip_half, f"rs_chip_sum{a}")
                            for a, ((p32, _), u) in enumerate(zip(p_small, u_small))]
    dqkv2, ds2, _ = _attn_bwd(qkv[2], dattn, stats, bias_tab, 2, None)

    dproj = jnp.concatenate([a.astype(BF16) for a in dqkv0 + dqkv1 + dqkv2] + [drest[:, :AW], du, drest[:, 2 * AW:]],
                            axis=1)
    dw_in = _dw_in(h, dproj)
    drel_rows, (sib_in,) = _bias_grad(jnp.concatenate([ds0, ds1, ds2], axis=0), buckets,
                                      _ride_sibling_halves([dw_in]))
    drel = drel_rows[:, 0, :NUM_BUCKETS].T
    p32_in, p16_in = _pair_sum(dw_in, sib_in, half_idx, "rs_pair_sum_in")
    dh, (u_in,) = _dh(dproj, wg_in, _ride_chip_exchange([p16_in]))
    rs_in = _chip_sum(p32_in, u_in, chip_half, "rs_chip_sum_in")

    grad_x, dnorm_g, dshift, dscale = _prenorm_bwd(x, dh, dx2, norm_g, mod)
    dmod = jnp.concatenate([dshift, dscale, dgate], axis=1)
    return dict(loss=loss[0, 0], grad_x=grad_x, dmod=dmod, dnorm_g=dnorm_g, dfinal_g=dfinal_g, dpool_w=dpool_w,
                dpool_scale=dpool_scale, drel_bias=drel, dw_in=dw_in, dw_attn_br=dw_ab, dw_pool_br=dw_pb,
                dw_out=dw_out, rs_in=rs_in, rs_attn_br=rs_ab, rs_pool_br=rs_pb, rs_out=rs_out)


def _allgather8(blocks, name, relay=None):
    nb = len(blocks)
    relay = [False] * nb if relay is None else list(relay)

    def body(*refs):
        ins, outs = refs[:nb], refs[nb:2 * nb]
        send_sems, recv_sems = refs[2 * nb:]
        x, y, c = lax.axis_index("x"), lax.axis_index("y"), lax.axis_index("c")
        me, sibling = (x, y, c), (x, y, 1 - c)
        here, xn, yn, dg = (x, y), (1 - x, y), (x, 1 - y), (1 - x, 1 - y)

        def slot(a, chip, core, half=None):
            ref = outs[a].at[4 * chip[0] + 2 * chip[1] + core]
            if half is None:
                return ref
            r2 = ref.shape[0] // 2
            return ref.at[pl.ds(half * r2, r2)]

        def copy(a, k, dst, to, src=None):
            return pltpu.make_async_remote_copy(src_ref=dst if src is None else src, dst_ref=dst,
                                                send_sem=send_sems.at[a, k], recv_sem=recv_sems.at[a, k],
                                                device_id=to, device_id_type=MESH)

        def start(cps):
            for cp in cps:
                cp.start()
            return cps

        sent = []
        for a in range(nb):
            own = slot(a, here, c)
            sent += [copy(a, 0, own, sibling, src=ins[a]), copy(a, 1, own, (*xn, c), src=ins[a]),
                     copy(a, 2, own, (*yn, c), src=ins[a])]
            if not relay[a]:
                sent.append(copy(a, 3, own, (*dg, c), src=ins[a]))
        start(sent)
        for a in range(nb):
            copy(a, 2, slot(a, yn, c), me).wait_recv()
            sent += start([copy(a, 6, slot(a, yn, c), sibling)]
                          + ([copy(a, 3, slot(a, yn, c, 0), (*xn, c))] if relay[a] else []))
        for a in range(nb):
            copy(a, 1, slot(a, xn, c), me).wait_recv()
            sent += start([copy(a, 5, slot(a, xn, c), sibling)]
                          + ([copy(a, 4, slot(a, xn, c, 1), (*yn, c))] if relay[a] else []))
        for a in range(nb):
            for k, half in ((3, 0), (4, 1)) if relay[a] else ((3, None),):
                copy(a, k, slot(a, dg, c, half), me).wait_recv()
                sent += start([copy(a, 4 + k, slot(a, dg, c, half), sibling)])
        for a in range(nb):
            copy(a, 0, slot(a, here, 1 - c), me).wait_recv()
            copy(a, 5, slot(a, xn, 1 - c), me).wait_recv()
            copy(a, 6, slot(a, yn, 1 - c), me).wait_recv()
            for k, half in ((7, 0), (8, 1)) if relay[a] else ((7, None),):
                copy(a, k, slot(a, dg, 1 - c, half), me).wait_recv()
        for cp in sent:
            cp.wait_send()

    outs = pl.pallas_call(
        body, name=name, in_specs=[ANY] * nb, out_specs=[ANY] * nb,
        out_shape=[_sds((8,) + b.shape, b.dtype) for b in blocks],
        scratch_shapes=[_dma_sems(nb, 9), _dma_sems(nb, 9)],
    )(*blocks)
    return [_place_own(buf, b) for buf, b in zip(outs, blocks)]


def _place_own(buf, block):
    dev = 4 * lax.axis_index("x") + 2 * lax.axis_index("y") + lax.axis_index("c")
    return lax.dynamic_update_index_in_dim(buf, block, dev, 0)


def _ride_gather_send(blocks):
    def copies(ins, outs, send_sems, recv_sems):
        x, y, c = lax.axis_index("x"), lax.axis_index("y"), lax.axis_index("c")
        cps = []
        for a in range(len(blocks)):
            own = outs[a].at[4 * x + 2 * y + c]
            for k, to in enumerate([(x, y, 1 - c), (1 - x, y, c), (x, 1 - y, c), (1 - x, 1 - y, c)]):
                cps.append(pltpu.make_async_remote_copy(src_ref=ins[a], dst_ref=own, send_sem=send_sems.at[4 * a + k],
                                                        recv_sem=recv_sems.at[4 * a + k], device_id=to,
                                                        device_id_type=MESH))
        return cps

    return _Ride(blocks, [_sds((8,) + b.shape, b.dtype) for b in blocks], 4 * len(blocks), copies)


def _ride_gather_forward(bufs):
    def copies(ins, outs, send_sems, recv_sems):
        x, y, c = lax.axis_index("x"), lax.axis_index("y"), lax.axis_index("c")
        cps = []
        for a in range(len(bufs)):
            for j, (ox, oy) in enumerate([(1 - x, y), (x, 1 - y), (1 - x, 1 - y)]):
                blk = outs[a].at[4 * ox + 2 * oy + c]
                cps.append(pltpu.make_async_remote_copy(src_ref=blk, dst_ref=blk, send_sem=send_sems.at[3 * a + j],
                                                        recv_sem=recv_sems.at[3 * a + j], device_id=(x, y, 1 - c),
                                                        device_id_type=MESH))
        return cps

    return _Ride(bufs, [_sds(b.shape, b.dtype) for b in bufs], 3 * len(bufs), copies, in_place=True)


def _ride_sibling_halves(gs):
    def copies(ins, outs, send_sems, recv_sems):
        x, y, c = lax.axis_index("x"), lax.axis_index("y"), lax.axis_index("c")
        cps = []
        for a in range(len(gs)):
            r2 = ins[a].shape[1] // 2
            other = ins[a].at[:, pl.ds((1 - c) * r2, r2), :]
            cps.append(pltpu.make_async_remote_copy(src_ref=other, dst_ref=outs[a], send_sem=send_sems.at[a],
                                                    recv_sem=recv_sems.at[a], device_id=(x, y, 1 - c),
                                                    device_id_type=MESH))
        return cps

    return _Ride(gs, [_sds((g.shape[0], g.shape[1] // 2, g.shape[2]), g.dtype) for g in gs], len(gs), copies)


def _pair_sum(g, t, half, name):
    nsh, rows, cols = g.shape
    r2 = rows // 2
    tr = _row_tile(r2, cols)
    nt = r2 // tr

    def body(half_ref, g_ref, t_ref, p32_ref, p16_ref):
        p = g_ref[...] + t_ref[...]
        p32_ref[...] = p
        p16_ref[...] = p.astype(BF16)

    blk = pl.BlockSpec((None, tr, cols), lambda k, i, half_ref: (k, i, 0))
    return pl.pallas_call(
        body, name=name,
        grid_spec=pltpu.PrefetchScalarGridSpec(
            num_scalar_prefetch=1, grid=(nsh, nt),
            in_specs=[pl.BlockSpec((None, tr, cols), lambda k, i, half_ref: (k, half_ref[0] * nt + i, 0)), blk],
            out_specs=[blk, blk]),
        out_shape=[_sds((nsh, r2, cols)), _sds((nsh, r2, cols), BF16)],
        compiler_params=_params("parallel", "parallel"),
    )(half, g, t)


def _ride_chip_exchange(ps):
    def copies(ins, outs, send_sems, recv_sems):
        x, y, c = lax.axis_index("x"), lax.axis_index("y"), lax.axis_index("c")
        chips = [(1 - x, y), (x, 1 - y), (1 - x, 1 - y)]
        cps = []
        for a in range(len(ps)):
            for j, (ox, oy) in enumerate(chips):
                cps.append(pltpu.make_async_remote_copy(src_ref=ins[a].at[2 * ox + oy], dst_ref=outs[a].at[j],
                                                        send_sem=send_sems.at[3 * a + j],
                                                        recv_sem=recv_sems.at[3 * a + j],
                                                        device_id=(ox, oy, c), device_id_type=MESH))
        return cps

    return _Ride(ps, [_sds((3,) + p.shape[1:], p.dtype) for p in ps], 3 * len(ps), copies)


def _chip_sum(p32, u, chip_half, name):
    r2, cols = p32.shape[1:]
    tr = _row_tile(r2, cols)
    nt = r2 // tr

    def body(ch_ref, p_ref, u_ref, o_ref):
        acc = p_ref[...]
        for j in range(3):
            acc = acc + u_ref[j].astype(F32)
        o_ref[...] = acc

    return pl.pallas_call(
        body, name=name,
        grid_spec=pltpu.PrefetchScalarGridSpec(
            num_scalar_prefetch=1, grid=(nt,),
            in_specs=[pl.BlockSpec((None, tr, cols), lambda i, ch_ref: (ch_ref[0], i, 0)),
                      pl.BlockSpec((3, tr, cols), lambda i, ch_ref: (0, i, 0))],
            out_specs=pl.BlockSpec((tr, cols), lambda i, ch_ref: (ch_ref[1] * nt + i, 0))),
        out_shape=_sds((2 * r2, cols)), compiler_params=_params("parallel"),
    )(chip_half, p32, u)


def _sibling_join(fs, name):
    nb = len(fs)

    def body(*refs):
        outs = refs[nb:2 * nb]
        send_sems, recv_sems = refs[2 * nb:]
        x, y, c = lax.axis_index("x"), lax.axis_index("y"), lax.axis_index("c")
        cps = []
        for a in range(nb):
            r2 = outs[a].shape[0] // 2
            rows = outs[a].at[pl.ds(c * r2, r2), :]
            cps.append(pltpu.make_async_remote_copy(src_ref=rows, dst_ref=rows, send_sem=send_sems.at[a],
                                                    recv_sem=recv_sems.at[a], device_id=(x, y, 1 - c),
                                                    device_id_type=MESH))
        for cp in cps:
            cp.start()
        for cp in cps:
            cp.wait()

    return pl.pallas_call(
        body, name=name, in_specs=[ANY] * nb, out_specs=[ANY] * nb,
        out_shape=[_sds(f.shape, f.dtype) for f in fs],
        input_output_aliases={a: a for a in range(nb)},
        scratch_shapes=[_dma_sems(nb), _dma_sems(nb)],
    )(*fs)


def _row_tile(rows, cols):
    tile = rows
    while tile * cols * 4 > (1 << 20) and tile % 16 == 0:
        tile //= 2
    return tile


def _w_ada_grad(c_all, dmod_cols):
    def body(c_ref, d_ref, o_ref):
        o_ref[...] = _dot_tn(c_ref[...].astype(BF16), d_ref[...].astype(BF16))

    return pl.pallas_call(body, name="w_ada_grad", out_shape=_sds((c_all.shape[1], dmod_cols.shape[1])),
                          compiler_params=_params())(c_all, dmod_cols)


def _adam_math(w, g, m, v):
    nm = ADAM_B1 * m + (1.0 - ADAM_B1) * g
    nv = ADAM_B2 * v + (1.0 - ADAM_B2) * (g * g)
    m_hat = nm / (1.0 - ADAM_B1 ** ADAM_STEP)
    v_hat = nv / (1.0 - ADAM_B2 ** ADAM_STEP)
    return -ADAM_LR * (m_hat / (jnp.sqrt(v_hat) + ADAM_EPS) + ADAM_WD * w), nm, nv


def _adamw(w, g, m, v, name):
    rows, cols = w.shape
    tr = _row_tile(rows, cols)

    def body(w_ref, g_ref, m_ref, v_ref, go_ref, d_ref, nm_ref, nv_ref):
        gv = g_ref[...]
        go_ref[...] = gv
        d_ref[...], nm_ref[...], nv_ref[...] = _adam_math(w_ref[...], gv, m_ref[...], v_ref[...])

    spec = pl.BlockSpec((tr, cols), lambda i: (i, 0))
    return pl.pallas_call(
        body, name=name, grid=(rows // tr,), in_specs=[spec] * 4, out_specs=[spec] * 4,
        out_shape=[_sds((rows, cols))] * 4, compiler_params=_params("parallel"),
    )(w, g, m, v)


def _pack_small(dmod, dnorm_g, dfinal_g, dpool_scale, drel_bias, loss, dpool_w):
    return jnp.concatenate([dmod.reshape(-1, 128), dnorm_g.reshape(-1, 128), dfinal_g.reshape(-1, 128),
                            jnp.pad(dpool_scale.reshape(-1, 128), ((0, PK_RELB - PK_PSCALE - AW // 128), (0, 0))),
                            jnp.pad(drel_bias, ((0, 0), (0, 128 - NG * NH))),
                            jnp.full((PK_POOLW - PK_LOSS, 128), loss, F32), dpool_w.reshape(-1, 128)], axis=0)


def _small_update(small_all, ws, ms, vs):
    lane_rows = [(r0, r0 + w.shape[1] // 128) for r0, w in zip((PK_BADA, PK_NORMG, PK_FINALG, PK_PSCALE), ws)]
    nw = len(ws)

    def body(all_ref, *refs):
        w_refs, m_refs, v_refs = refs[:nw], refs[nw:2 * nw], refs[2 * nw:3 * nw]
        loss_ref, outs = refs[3 * nw], refs[3 * nw + 1:]
        g = all_ref[0]
        for s in range(1, all_ref.shape[0]):
            g = g + all_ref[s]
        loss_ref[...] = jnp.broadcast_to(g[PK_LOSS:PK_LOSS + 1, :], loss_ref.shape)

        def put(p, at, gv):
            d, nm, nv = _adam_math(w_refs[p][at], gv, m_refs[p][at], v_refs[p][at])
            for o_ref, val in zip(outs[4 * p:4 * p + 4], (gv, d, nm, nv)):
                o_ref[at] = val

        for p, (r0, r1) in enumerate(lane_rows):
            for i in range(r1 - r0):
                put(p, (slice(None), slice(128 * i, 128 * (i + 1))), g[r0 + i:r0 + i + 1, :])
        put(4, (slice(None), slice(None)), g[PK_RELB:PK_LOSS, 0:NG * NH])
        put(5, (slice(None), slice(None)), g[PK_POOLW:PK_ROWS, :])

    res = pl.pallas_call(
        body, name="small_update",
        out_shape=[_sds((8, 128))] + [_sds(w.shape) for w in ws for _ in range(4)], compiler_params=_params(),
    )(small_all, *ws, *ms, *vs)
    return res[0], [res[1 + 4 * p:5 + 4 * p] for p in range(nw)]


def kernel(x, c, norm_g, w_ada, b_ada, w_in, pool_w, pool_scale, w_attn_br, w_pool_br, w_out, rel_bias, final_g, loss_target, m_norm_g, m_w_ada, m_b_ada, m_w_in, m_pool_w, m_pool_scale, m_w_attn_br, m_w_pool_br, m_w_out, m_rel_bias, m_final_g, v_norm_g, v_w_ada, v_b_ada, v_w_in, v_pool_w, v_pool_scale, v_w_attn_br, v_w_pool_br, v_w_out, v_rel_bias, v_final_g):
    ix, iy, ic = lax.axis_index("x"), lax.axis_index("y"), lax.axis_index("c")
    dev = 4 * ix + 2 * iy + ic
    chip = 2 * ix + iy

    def half(w):
        r2 = w.shape[0] // 2
        return lax.dynamic_slice_in_dim(w, ic * r2, r2, axis=0).astype(BF16)

    gathered = _allgather8([jnp.broadcast_to(c, (8, D)), half(w_in[0])], "gather_weights", relay=[False, True])
    c_all = gathered[0][:, 0, :]
    wg_in = gathered[1].reshape(N_SHARD, D, IN_W // N_SHARD)
    buckets = jnp.asarray(_bucket_tables())
    own_late = [half(w_attn_br[0]), half(w_pool_br[0]), half(w_out[0])]
    bias_tab, late = _bias_table(rel_bias, buckets, _ride_gather_send(own_late))
    late = [_place_own(buf, own) for buf, own in zip(late, own_late)]

    mw = 3 * D // N_SHARD
    modp = _mod_partial(c_all, w_ada[0], lax.dynamic_slice_in_dim(b_ada, chip * mw, mw, axis=1))
    mod_all = _allgather8([modp], "gather_mod")[0]
    mod_full = mod_all[::2].transpose(1, 0, 2).reshape(8, 3 * D)
    mod = lax.dynamic_slice_in_dim(mod_full, dev, 1, axis=0)
    h, late = _prenorm(x[0], norm_g, mod, _ride_gather_forward(late))
    wab = late[0].reshape(N_SHARD, AW, D // N_SHARD).transpose(1, 0, 2).reshape(AW, D)
    wpb = late[1].reshape(N_SHARD, AW, D // N_SHARD).transpose(1, 0, 2).reshape(AW, D)
    wout = late[2].reshape(D, D)

    half_idx = jnp.stack([ic]).astype(jnp.int32)
    chip_half = jnp.stack([chip, ic]).astype(jnp.int32)
    r = _local_step(x[0], loss_target[0], mod, h, bias_tab, buckets, wg_in, wab, wpb, wout, pool_w[0], pool_scale,
                    norm_g, final_g.reshape(1, D), half_idx, chip_half)

    packed = _pack_small(r["dmod"], r["dnorm_g"], r["dfinal_g"], r["dpool_scale"], r["drel_bias"], r["loss"],
                         r["dpool_w"])
    small_all = _allgather8([packed], "gather_small")[0]
    small = ["b_ada", "norm_g", "final_g", "pool_scale", "rel_bias", "pool_w"]
    shaped = lambda b, n, f, ps, rb, pw: [b, n, f.reshape(1, D), ps, rb, pw.reshape(4 * PGW, PGW)]
    loss, small_out = _small_update(small_all, shaped(b_ada, norm_g, final_g, pool_scale, rel_bias, pool_w),
                                    shaped(m_b_ada, m_norm_g, m_final_g, m_pool_scale, m_rel_bias, m_pool_w),
                                    shaped(v_b_ada, v_norm_g, v_final_g, v_pool_scale, v_rel_bias, v_pool_w))
    dmod_all = small_all[:, PK_BADA:PK_NORMG, :].reshape(8, 3 * D)
    g_w_ada = _w_ada_grad(c_all, lax.dynamic_slice_in_dim(dmod_all, chip * mw, mw, axis=1))

    g_w_in, g_w_ab, g_w_pb, g_w_out = _sibling_join([r["rs_in"], r["rs_attn_br"], r["rs_pool_br"], r["rs_out"]],
                                                    "rs_sibling_join")
    upd = dict(zip(small, small_out))
    upd["final_g"] = [a.reshape(D) for a in upd["final_g"]]
    upd["pool_w"] = [a.reshape(1, 4, PGW, PGW) for a in upd["pool_w"]]
    for nme, w, g, m, v in (("w_ada", w_ada, g_w_ada, m_w_ada, v_w_ada), ("w_in", w_in, g_w_in, m_w_in, v_w_in),
                            ("w_attn_br", w_attn_br, g_w_ab, m_w_attn_br, v_w_attn_br),
                            ("w_pool_br", w_pool_br, g_w_pb, m_w_pool_br, v_w_pool_br),
                            ("w_out", w_out, g_w_out, m_w_out, v_w_out)):
        upd[nme] = [a[None] for a in _adamw(w[0], g, m[0], v[0], "adamw_" + nme)]
    names = ["norm_g", "w_ada", "b_ada", "w_in", "pool_w", "pool_scale", "w_attn_br", "w_pool_br", "w_out",
             "rel_bias", "final_g"]
    return (loss[0, 0], r["grad_x"][None]) + tuple(upd[nme][kind] for kind in range(4) for nme in names)
```

```python
import functools
import math

import numpy as np
import jax
import jax.numpy as jnp
from jax import lax
from jax.experimental import pallas as pl
from jax.experimental.pallas import tpu as pltpu

F32 = jnp.float32
BF16 = jnp.bfloat16

D = 1024
HD = 64
NH = 8
AW = NH * HD
GROUPS = ((128, 1), (512, 4), (2048, 16))
NG = len(GROUPS)
BLK = 128
GW = 3 * AW
QKV_W = NG * GW
REST_W = 3584
IN_W = QKV_W + REST_W
CB = 512
NCB = IN_W // CB
NCB_QKV = QKV_W // CB
POOL_WINDOWS = (2, 4, 8, 16)
PGW = 128
HALO = 16
NUM_BUCKETS = 32
MAX_DISTANCE = 2048
EPS = 1e-6
NEG = -1e30
N_SHARD = 4
VMEM_LIMIT = 56 * 1024 * 1024

ADAM_LR = 0.001
ADAM_B1 = 0.9
ADAM_B2 = 0.999
ADAM_EPS = 1e-08
ADAM_WD = 0.01
ADAM_STEP = 10

PK_BADA, PK_NORMG, PK_FINALG, PK_PSCALE, PK_RELB, PK_LOSS, PK_POOLW, PK_ROWS = 0, 24, 32, 40, 48, 80, 88, 600

ANY = pl.BlockSpec(memory_space=pl.ANY)
MESH = pl.DeviceIdType.MESH


def _params(*sem):
    return pltpu.CompilerParams(dimension_semantics=sem, vmem_limit_bytes=VMEM_LIMIT)


def _sds(shape, dtype=F32):
    return jax.ShapeDtypeStruct(shape, dtype)


def _dot(a, b):
    return jnp.dot(a, b, preferred_element_type=F32)


def _dot_nt(a, b):
    return lax.dot_general(a, b, (((1,), (1,)), ((), ())), preferred_element_type=F32)


def _dot_tn(a, b):
    return lax.dot_general(a, b, (((0,), (0,)), ((), ())), preferred_element_type=F32)


def _sigmoid(z):
    return 0.5 * jnp.tanh(0.5 * z) + 0.5


def _dma_sems(*shape):
    return pltpu.SemaphoreType.DMA(shape)


class _Ride:
    def __init__(self, arrays, out_shapes, n_copies, copies, in_place=False):
        self.arrays, self.out_shapes, self.n_copies, self.copies = list(arrays), list(out_shapes), n_copies, copies
        self.in_place = in_place


def _call_with_ride(body, ride, first, last, *, in_specs, out_specs, out_shape, scratch_shapes=(), **kw):
    in_specs, out_specs, out_shape, scratch_shapes = list(in_specs), list(out_specs), list(out_shape), list(scratch_shapes)
    n_in, n_out, n_sc = len(in_specs), len(out_specs), len(scratch_shapes)
    if ride is None:
        def run_plain(*operands):
            return pl.pallas_call(body, in_specs=in_specs, out_specs=out_specs, out_shape=out_shape,
                                  scratch_shapes=scratch_shapes, **kw)(*operands), []
        return run_plain
    n_ri, n_ro = len(ride.arrays), len(ride.out_shapes)

    def wrapped(*refs):
        ins, rest = refs[:n_in], refs[n_in:]
        r_ins, rest = rest[:n_ri], rest[n_ri:]
        outs, rest = rest[:n_out], rest[n_out:]
        r_outs, rest = rest[:n_ro], rest[n_ro:]
        scratch, (send_sems, recv_sems) = rest[:n_sc], rest[n_sc:]

        @pl.when(first())
        def _():
            for cp in ride.copies(r_ins, r_outs, send_sems, recv_sems):
                cp.start()

        body(*ins, *outs, *scratch)

        @pl.when(last())
        def _():
            for cp in ride.copies(r_ins, r_outs, send_sems, recv_sems):
                cp.wait()

    def run(*operands):
        res = pl.pallas_call(
            wrapped, in_specs=in_specs + [ANY] * n_ri, out_specs=out_specs + [ANY] * n_ro,
            out_shape=out_shape + ride.out_shapes,
            scratch_shapes=scratch_shapes + [_dma_sems(ride.n_copies), _dma_sems(ride.n_copies)],
            input_output_aliases={n_in + a: n_out + a for a in range(n_ri)} if ride.in_place else {}, **kw,
        )(*operands, *ride.arrays)
        return res[:n_out], res[n_out:]
    return run


def _bucket_tables():
    i = np.arange(BLK)[:, None]
    j = np.arange(2 * BLK)[None, :]
    dist = BLK + i - j
    valid = (dist >= 0) & (dist <= BLK)
    tabs = []
    for _, dil in GROUPS:
        n = (np.clip(dist, 0, BLK) * dil).astype(np.int32)
        max_exact = NUM_BUCKETS // 2
        nf = np.maximum(n, 1).astype(np.float32)
        large = max_exact + (np.log(nf / np.float32(max_exact)) / np.float32(math.log(MAX_DISTANCE / max_exact))
                             * np.float32(NUM_BUCKETS - max_exact)).astype(np.int32)
        large = np.minimum(large, NUM_BUCKETS - 1)
        bucket = np.where(n < max_exact, n, large)
        tab = np.where(valid, bucket, -1).astype(np.int32)
        perm = _block_perm(dil)
        tabs.append(tab[perm][:, np.concatenate([perm, BLK + perm])])
    return np.stack(tabs)


def _bias_table(rel_bias, buckets):
    def body(rb_ref, bk_ref, out_ref):
        g = pl.program_id(0)
        bk = bk_ref[...]
        for h in range(NH):
            acc = jnp.full((BLK, 2 * BLK), NEG, F32)
            for b in range(NUM_BUCKETS):
                acc = jnp.where(bk == b, rb_ref[b, g * NH + h], acc)
            out_ref[h] = acc

    return pl.pallas_call(
        body, name="bias_table", grid=(NG,),
        in_specs=[pl.BlockSpec(memory_space=pltpu.SMEM),
                  pl.BlockSpec((None, BLK, 2 * BLK), lambda g: (g, 0, 0))],
        out_specs=pl.BlockSpec((NH, BLK, 2 * BLK), lambda g: (g, 0, 0)),
        out_shape=_sds((NG * NH, BLK, 2 * BLK)),
        compiler_params=_params("arbitrary"),
    )(rel_bias, buckets)


def _bias_grad(ds_acc, buckets, ride):
    def body(acc_ref, bk_ref, out_ref):
        bk = bk_ref[...]
        acc = acc_ref[...]
        lane = lax.broadcasted_iota(jnp.int32, (8, 128), 1)
        out = jnp.zeros((8, 128), F32)
        for b in range(NUM_BUCKETS):
            val = jnp.sum(jnp.where(bk == b, acc, 0.0))
            out = jnp.where(lane == b, val, out)
        out_ref[...] = out

    (out,), rode = _call_with_ride(
        body, ride, lambda: pl.program_id(0) == 0, lambda: pl.program_id(0) == NG * NH - 1,
        name="bias_grad", grid=(NG * NH,),
        in_specs=[pl.BlockSpec((None, BLK, 2 * BLK), lambda gh: (gh, 0, 0)),
                  pl.BlockSpec((None, BLK, 2 * BLK), lambda gh: (gh // NH, 0, 0))],
        out_specs=[pl.BlockSpec((None, 8, 128), lambda gh: (gh, 0, 0))],
        out_shape=[_sds((NG * NH, 8, 128))],
        compiler_params=_params("arbitrary"),
    )(ds_acc, buckets)
    return out, rode


def _mod_partial(c_all, w_ada_s, b_ada_s):
    def body(c_ref, w_ref, b_ref, o_ref):
        o_ref[...] = _dot(c_ref[...].astype(BF16), w_ref[...].astype(BF16)) + b_ref[...]

    return pl.pallas_call(body, name="mod_partial", out_shape=_sds((8, w_ada_s.shape[1])),
                          compiler_params=_params())(c_all, w_ada_s, b_ada_s)


def _prenorm(x, norm_g, mod):
    S = x.shape[0]
    tm = 512

    def body(x_ref, g_ref, mod_ref, h_ref):
        xv = x_ref[...]
        r = lax.rsqrt(jnp.mean(xv * xv, axis=-1, keepdims=True) + EPS)
        n1 = xv * r * g_ref[...]
        h_ref[...] = (n1 * (1.0 + mod_ref[:, D:2 * D]) + mod_ref[:, 0:D]).astype(BF16)

    return pl.pallas_call(
        body, name="prenorm", grid=(S // tm,),
        in_specs=[pl.BlockSpec((tm, D), lambda i: (i, 0)), pl.BlockSpec((1, D), lambda i: (0, 0)),
                  pl.BlockSpec((1, 3 * D), lambda i: (0, 0))],
        out_specs=pl.BlockSpec((tm, D), lambda i: (i, 0)),
        out_shape=_sds((S, D), BF16), compiler_params=_params("parallel"),
    )(x, norm_g, mod)


def _proj(h, wg_in, j0, nj, dtype, name):
    S = h.shape[0]
    tm = 2048
    per = wg_in.shape[2] // CB

    def body(h_ref, w_ref, o_ref):
        o_ref[...] = _dot(h_ref[...], w_ref[...]).astype(dtype)

    return pl.pallas_call(
        body, name=name, grid=(S // tm, nj),
        in_specs=[pl.BlockSpec((tm, D), lambda m, j: (m, 0)),
                  pl.BlockSpec((None, D, CB), lambda m, j: ((j0 + j) // per, 0, (j0 + j) % per))],
        out_specs=pl.BlockSpec((tm, CB), lambda m, j: (m, j)),
        out_shape=_sds((S, nj * CB), dtype), compiler_params=_params("parallel", "parallel"),
    )(h, wg_in)


HS = 4
SLAB = HS * HD


def _lane_head(rows):
    return lax.broadcasted_iota(jnp.int32, (rows, SLAB), 1) // HD


def _head_stack(a):
    head = _lane_head(a.shape[0])
    return jnp.concatenate([jnp.where(head == h, a, jnp.zeros_like(a)) for h in range(HS)], axis=0)


def _head_unstack(a):
    rows = a.shape[0] // HS
    head = _lane_head(rows)
    out = a[:rows]
    for h in range(1, HS):
        out = jnp.where(head == h, a[h * rows:(h + 1) * rows], out)
    return out


STAT_W = 128
VIEW = 16


def _sub_layout(dil):
    if dil == 1:
        return BLK, [None]
    return BLK * dil // VIEW, [[r + dil * u for u in range(VIEW // dil)] for r in range(dil)]


def _block_perm(dil):
    a_rows, _ = _sub_layout(dil)
    p = np.arange(BLK)
    return p if dil == 1 else (VIEW // dil) * (p % a_rows) + p // a_rows


LB = 128
N_SLAB = NH // HS


def _ld(refs, bs, s, w):
    if bs is None:
        return refs[0][:, s * w:(s + 1) * w]
    a_rows = refs[0].shape[0] // VIEW
    return jnp.concatenate([jnp.concatenate([ref[pl.ds(b, a_rows, stride=VIEW), :] for b in bs], axis=0)
                            for ref in refs], axis=1)


def _st(ref, bs, s, val):
    if bs is None:
        ref[:, s * SLAB:(s + 1) * SLAB] = val
        return
    a_rows = val.shape[0] // len(bs)
    for u, b in enumerate(bs):
        ref[:, b, :] = val[u * a_rows:(u + 1) * a_rows]


def _attn_views(dil, S):
    a_rows, subs = _sub_layout(dil)
    if dil == 1:
        def ispecs(base, w, f):
            return [pl.BlockSpec((BLK, N_SLAB * w), lambda sg, n: (f(n), base // (N_SLAB * w)))]
        return subs, S // BLK, N_SLAB, ispecs, (lambda w: (S, w)), (
            lambda f: pl.BlockSpec((BLK, AW), lambda sg, n: (f(n), 0)))

    def ispecs(base, w, f):
        return [pl.BlockSpec((a_rows * VIEW, LB), lambda sg, n, k=k: (f(n), (base + sg * w) // LB + k))
                for k in range(w // LB)]
    return subs, S // (a_rows * VIEW), 1, ispecs, (lambda w: (S // VIEW, VIEW, w)), (
        lambda f: pl.BlockSpec((a_rows, VIEW, SLAB), lambda sg, n: (f(n), 0, sg)))


def _attn_fwd(qkv_g, bias_tab, g, ride):
    S = qkv_g.shape[0]
    subs, nbq, sps, ispecs, shape, ospec = _attn_views(GROUPS[g][1], S)
    cur, prev = (lambda n: n), (lambda n: jnp.maximum(n - 1, 0))
    in_specs = [ispecs(0, SLAB, cur), ispecs(AW, SLAB, prev), ispecs(AW, SLAB, cur), ispecs(2 * AW, SLAB, prev),
                ispecs(2 * AW, SLAB, cur)]
    nl = len(in_specs[0])

    def body(*refs):
        q, kp, kc, vp, vc = (refs[t * nl:(t + 1) * nl] for t in range(5))
        b_ref, o_ref, l_ref = refs[5 * nl:]
        n = pl.program_id(1)
        col = lax.broadcasted_iota(jnp.int32, (HS * BLK, 2 * BLK), 1)
        keep = (col >= BLK) | (n > 0)
        for s_ in range(sps):
            bias = b_ref[pl.ds(s_ * HS, HS)].reshape(HS * BLK, 2 * BLK)
            for bs in subs:
                kb = jnp.concatenate([_ld(kp, bs, s_, SLAB), _ld(kc, bs, s_, SLAB)], axis=0).astype(BF16)
                vb = jnp.concatenate([_ld(vp, bs, s_, SLAB), _ld(vc, bs, s_, SLAB)], axis=0).astype(BF16)
                s = _dot_nt(_head_stack(_ld(q, bs, s_, SLAB).astype(BF16)), kb) * (HD ** -0.5) + bias
                s = jnp.where(keep, s, NEG)
                m = jnp.max(s, axis=-1, keepdims=True)
                p = jnp.exp(s - m)
                den = jnp.sum(p, axis=-1, keepdims=True)
                _st(o_ref, bs, s_, _head_unstack(_dot(p.astype(BF16), vb) / den))
                _st(l_ref, bs, s_, _head_unstack(jnp.broadcast_to(m + jnp.log(den), (HS * BLK, SLAB))))

    out = _sds(shape(AW))
    nsg = N_SLAB // sps
    (o, l), rode = _call_with_ride(
        body, ride, lambda: (pl.program_id(0) == 0) & (pl.program_id(1) == 0),
        lambda: (pl.program_id(0) == nsg - 1) & (pl.program_id(1) == nbq - 1),
        name=f"attn_fwd{g}", grid=(nsg, nbq),
        in_specs=sum(in_specs, []) + [pl.BlockSpec((sps * HS, BLK, 2 * BLK), lambda sg, n: (g * nsg + sg, 0, 0))],
        out_specs=[ospec(cur), ospec(cur)],
        out_shape=[out, out], compiler_params=_params("arbitrary", "arbitrary"),
    )(*([qkv_g] * (5 * nl)), bias_tab)
    return o.reshape(S, AW), l.reshape(S, AW), rode


def _attn_bwd(qkv_g, dattn, stats, bias_tab, g, ride):
    S = qkv_g.shape[0]
    subs, nbq, sps, ispecs, shape, ospec = _attn_views(GROUPS[g][1], S)
    cur = lambda n: jnp.minimum(n, nbq - 1)
    prev = lambda n: jnp.clip(n - 1, 0, nbq - 1)
    late = lambda n: jnp.maximum(n - 1, 0)
    in_specs = [ispecs(0, SLAB, cur), ispecs(AW, SLAB, prev), ispecs(AW, SLAB, cur), ispecs(2 * AW, SLAB, prev),
                ispecs(2 * AW, SLAB, cur), ispecs(0, SLAB, cur), ispecs(0, STAT_W, cur)]
    nl = len(in_specs[0])

    def body(*refs):
        q, kp, kc, vp, vc, da = (refs[t * nl:(t + 1) * nl] for t in range(6))
        st_ref, b_ref, dq_ref, dk_ref, dv_ref, ds_ref, ck_ref, cv_ref = refs[6 * nl:]
        n = pl.program_id(1)

        @pl.when(n == 0)
        def _():
            ds_ref[...] = jnp.zeros_like(ds_ref)
            ck_ref[...] = jnp.zeros_like(ck_ref)
            cv_ref[...] = jnp.zeros_like(cv_ref)

        @pl.when(n < nbq)
        def _():
            col = lax.broadcasted_iota(jnp.int32, (HS * BLK, 2 * BLK), 1)
            keep = (col >= BLK) | (n > 0)
            for s_ in range(sps):
                cs = slice(s_ * SLAB, (s_ + 1) * SLAB)
                bias = b_ref[pl.ds(s_ * HS, HS)].reshape(HS * BLK, 2 * BLK)
                for i, bs in enumerate(subs):
                    st = _ld((st_ref,), bs, s_, STAT_W)
                    kb = jnp.concatenate([_ld(kp, bs, s_, SLAB), _ld(kc, bs, s_, SLAB)], axis=0).astype(BF16)
                    vb = jnp.concatenate([_ld(vp, bs, s_, SLAB), _ld(vc, bs, s_, SLAB)], axis=0).astype(BF16)
                    lse = jnp.concatenate([st[:, h:h + 1] for h in range(HS)], axis=0)
                    delta = jnp.concatenate([st[:, HS + h:HS + h + 1] for h in range(HS)], axis=0)
                    qs = _head_stack(_ld(q, bs, s_, SLAB).astype(BF16))
                    dos = _head_stack(_ld(da, bs, s_, SLAB).astype(BF16))
                    s = _dot_nt(qs, kb) * (HD ** -0.5) + bias
                    s = jnp.where(keep, s, NEG)
                    p = jnp.exp(s - lse)
                    ds = p * (_dot_nt(dos, vb) - delta)
                    ds_ref[pl.ds(s_ * HS, HS)] += ds.reshape(HS, BLK, 2 * BLK)
                    ds_b = (ds * (HD ** -0.5)).astype(BF16)
                    _st(dq_ref, bs, s_, _head_unstack(_dot(ds_b, kb)))
                    dkb = _dot_tn(ds_b, qs)
                    dvb = _dot_tn(p.astype(BF16), dos)
                    _st(dk_ref, bs, s_, ck_ref[i, :, cs] + dkb[:BLK])
                    _st(dv_ref, bs, s_, cv_ref[i, :, cs] + dvb[:BLK])
                    ck_ref[i, :, cs] = dkb[BLK:]
                    cv_ref[i, :, cs] = dvb[BLK:]

        @pl.when(n == nbq)
        def _():
            for s_ in range(sps):
                for i, bs in enumerate(subs):
                    _st(dk_ref, bs, s_, ck_ref[i, :, s_ * SLAB:(s_ + 1) * SLAB])
                    _st(dv_ref, bs, s_, cv_ref[i, :, s_ * SLAB:(s_ + 1) * SLAB])

    out = _sds(shape(AW))
    nsg = N_SLAB // sps
    (dq, dk, dv, ds_acc), rode = _call_with_ride(
        body, ride, lambda: (pl.program_id(0) == 0) & (pl.program_id(1) == 0),
        lambda: (pl.program_id(0) == nsg - 1) & (pl.program_id(1) == nbq),
        name=f"attn_bwd{g}", grid=(nsg, nbq + 1),
        in_specs=sum(in_specs, []) + [pl.BlockSpec((sps * HS, BLK, 2 * BLK), lambda sg, n: (g * nsg + sg, 0, 0))],
        out_specs=[ospec(cur), ospec(late), ospec(late),
                   pl.BlockSpec((sps * HS, BLK, 2 * BLK), lambda sg, n: (sg, 0, 0))],
        out_shape=[out] * 3 + [_sds((NH, BLK, 2 * BLK))],
        scratch_shapes=[pltpu.VMEM((len(subs), BLK, sps * SLAB), F32), pltpu.VMEM((len(subs), BLK, sps * SLAB), F32)],
        compiler_params=_params("arbitrary", "arbitrary"),
    )(*([qkv_g] * (5 * nl)), *([dattn] * nl), stats, bias_tab)
    return [dq.reshape(S, AW), dk.reshape(S, AW), dv.reshape(S, AW)], ds_acc, rode


TM_MIX = 256


def _mix_specs(tm):
    row512 = pl.BlockSpec((tm, AW), lambda i: (i, 0))
    return ([row512] * 6 + [
        pl.BlockSpec((tm, REST_W), lambda i: (i, 0)),
        pl.BlockSpec((HALO, AW), lambda i: (jnp.maximum(i * (tm // HALO) - 1, 0), 1)),
        pl.BlockSpec((AW, D), lambda i: (0, 0)), pl.BlockSpec((AW, D), lambda i: (0, 0)),
        pl.BlockSpec((4, PGW, PGW), lambda i: (0, 0, 0)), pl.BlockSpec((1, AW), lambda i: (0, 0))])


def _mix_forward(i, tm, o_refs, l_refs, rest_ref, halo_ref, wab_ref, wpb_ref, pw_ref, ps_ref):
    l0, l1, l2 = (r[...] for r in l_refs)
    mx = jnp.maximum(jnp.maximum(l0, l1), l2)
    e0, e1, e2 = jnp.exp(l0 - mx), jnp.exp(l1 - mx), jnp.exp(l2 - mx)
    den = e0 + e1 + e2
    lj = mx + jnp.log(den)
    attn = (e0 * o_refs[0][...] + e1 * o_refs[1][...] + e2 * o_refs[2][...]) / den

    z_attn = rest_ref[:, 0:AW]
    u = rest_ref[:, AW:2 * AW]
    z_pool = rest_ref[:, 2 * AW:3 * AW]
    g_attn = rest_ref[:, 3 * AW:3 * AW + D]
    g_pool = rest_ref[:, 3 * AW + D:3 * AW + 2 * D]

    sg_a = _sigmoid(z_attn)
    sil_a = z_attn * sg_a
    a_g = (attn * sil_a).astype(BF16)
    y_attn = _dot(a_g, wab_ref[...])

    halo = jnp.where(i > 0, halo_ref[...], 0.0)
    ext = jnp.concatenate([halo, u], axis=0)
    t = i * tm + lax.broadcasted_iota(jnp.int32, (tm, 1), 0)
    pooled, mixed_raw = [], []
    for gi, win in enumerate(POOL_WINDOWS):
        s = ext[:, gi * PGW:(gi + 1) * PGW]
        sh = 1
        while sh < win:
            s = s + pltpu.roll(s, sh, 0)
            sh *= 2
        cnt = jnp.minimum(t + 1, win).astype(F32)
        pg = s[HALO:] / cnt - u[:, gi * PGW:(gi + 1) * PGW]
        pooled.append(pg.astype(BF16))
        mixed_raw.append(_dot(pooled[-1], pw_ref[gi].astype(BF16)))
    mixed_raw = jnp.concatenate(mixed_raw, axis=1)
    mixed = mixed_raw * ps_ref[...]
    sg_p = _sigmoid(z_pool)
    sil_p = z_pool * sg_p
    m_g = (mixed * sil_p).astype(BF16)
    y_pool = _dot(m_g, wpb_ref[...])

    sa = _sigmoid(g_attn)
    sp = _sigmoid(g_pool)
    merged = sa * y_attn + sp * y_pool
    return dict(lj=lj, attn=attn, z_attn=z_attn, z_pool=z_pool, sg_a=sg_a, sil_a=sil_a, a_g=a_g, y_attn=y_attn,
                pooled=pooled, mixed_raw=mixed_raw, mixed=mixed, sg_p=sg_p, sil_p=sil_p, m_g=m_g, y_pool=y_pool,
                sa=sa, sp=sp, merged=merged)


def _tail(x, target, os_, ls_, rest, wab, wpb, pool_w, pool_scale, wout, mod, final_g):
    S = x.shape[0]
    tm = TM_MIX

    def body(o0, o1, o2, l0, l1, l2, rest_ref, halo_ref, wab_ref, wpb_ref, pw_ref, ps_ref,
             x_ref, t_ref, wo_ref, mod_ref, fg_ref, dx2_ref, dmo_ref, loss_ref, dfg_ref, dgate_ref):
        i = pl.program_id(0)

        @pl.when(i == 0)
        def _():
            loss_ref[...] = jnp.zeros_like(loss_ref)
            dfg_ref[...] = jnp.zeros_like(dfg_ref)
            dgate_ref[...] = jnp.zeros_like(dgate_ref)

        f = _mix_forward(i, tm, (o0, o1, o2), (l0, l1, l2), rest_ref, halo_ref, wab_ref, wpb_ref, pw_ref, ps_ref)
        mo = _dot(f["merged"].astype(BF16), wo_ref[...])
        gate = mod_ref[:, 2 * D:3 * D]
        fg = fg_ref[...]
        x2 = x_ref[...] + gate * mo
        r2 = lax.rsqrt(jnp.mean(x2 * x2, axis=-1, keepdims=True) + EPS)
        n2 = x2 * r2
        err = n2 * fg - t_ref[...]
        loss_ref[...] += 0.5 * jnp.sum(jnp.mean(err * err, axis=-1, keepdims=True))
        dy = err * (1.0 / D)
        dfg_ref[...] += jnp.sum(dy * n2, axis=0, keepdims=True)
        dn = dy * fg
        dx2 = r2 * (dn - n2 * jnp.mean(dn * n2, axis=-1, keepdims=True))
        dgate_ref[...] += jnp.sum(dx2 * mo, axis=0, keepdims=True)
        dx2_ref[...] = dx2
        dmo_ref[...] = (dx2 * gate).astype(BF16)

    row = pl.BlockSpec((tm, D), lambda i: (i, 0))
    vec = pl.BlockSpec((1, D), lambda i: (0, 0))
    return pl.pallas_call(
        body, name="tail", grid=(S // tm,),
        in_specs=_mix_specs(tm) + [row, row, pl.BlockSpec((D, D), lambda i: (0, 0)),
                                   pl.BlockSpec((1, 3 * D), lambda i: (0, 0)), vec],
        out_specs=[row, row, pl.BlockSpec((8, 128), lambda i: (0, 0)), vec, vec],
        out_shape=[_sds((S, D)), _sds((S, D), BF16), _sds((8, 128)), _sds((1, D)), _sds((1, D))],
        compiler_params=_params("arbitrary"),
    )(*os_, *ls_, rest, rest, wab, wpb, pool_w, pool_scale, x, target, wout, mod, final_g)


def _mix_bwd(dmo, os_, ls_, rest, wab, wpb, pool_w, pool_scale, wout):
    S = dmo.shape[0]
    tm = TM_MIX
    nt = S // tm
    sw = D // N_SHARD

    def body(o0, o1, o2, l0, l1, l2, rest_ref, halo_ref, wab_ref, wpb_ref, pw_ref, ps_ref, dmo_ref, wo_ref,
             dattn_ref, stats_ref, dpooled_ref, drest_ref, dwo_hbm, dwab_hbm, dwpb_hbm, dpw_ref, dps_ref,
             awo, awab, awpb):
        i = pl.program_id(0)

        @pl.when(i == 0)
        def _():
            awo[...] = jnp.zeros_like(awo)
            awab[...] = jnp.zeros_like(awab)
            awpb[...] = jnp.zeros_like(awpb)
            dpw_ref[...] = jnp.zeros_like(dpw_ref)
            dps_ref[...] = jnp.zeros_like(dps_ref)

        f = _mix_forward(i, tm, (o0, o1, o2), (l0, l1, l2), rest_ref, halo_ref, wab_ref, wpb_ref, pw_ref, ps_ref)
        dmo_b = dmo_ref[...]
        dmerged = _dot_nt(dmo_b, wo_ref[...])
        awo[...] += _dot_tn(f["merged"].astype(BF16), dmo_b)
        sa, sp = f["sa"], f["sp"]
        dya = (dmerged * sa).astype(BF16)
        dyp = (dmerged * sp).astype(BF16)
        dg_attn = dmerged * f["y_attn"] * sa * (1.0 - sa)
        dg_pool = dmerged * f["y_pool"] * sp * (1.0 - sp)
        dag = _dot_nt(dya, wab_ref[...])
        awab[...] += _dot_tn(f["a_g"], dya)
        dmg = _dot_nt(dyp, wpb_ref[...])
        awpb[...] += _dot_tn(f["m_g"], dyp)
        dattn = dag * f["sil_a"]
        dattn_ref[...] = dattn
        prod = dattn * f["attn"]
        lane = lax.broadcasted_iota(jnp.int32, (tm, STAT_W), 1)
        for sb in range(N_SLAB):
            st = jnp.zeros((tm, STAT_W), F32)
            for h in range(HS):
                hs = slice((sb * HS + h) * HD, (sb * HS + h + 1) * HD)
                st = jnp.where(lane == h, f["lj"][:, hs.start:hs.start + 1], st)
                st = jnp.where(lane == HS + h, jnp.sum(prod[:, hs], axis=-1, keepdims=True), st)
            stats_ref[:, sb * STAT_W:(sb + 1) * STAT_W] = st
        dz_attn = dag * f["attn"] * (f["sg_a"] * (1.0 + f["z_attn"] * (1.0 - f["sg_a"])))
        dmixed = dmg * f["sil_p"]
        dz_pool = dmg * f["mixed"] * (f["sg_p"] * (1.0 + f["z_pool"] * (1.0 - f["sg_p"])))
        dps_ref[...] += jnp.sum(dmixed * f["mixed_raw"], axis=0, keepdims=True)
        dpm = (dmixed * ps_ref[...]).astype(BF16)
        for gi in range(len(POOL_WINDOWS)):
            cs = slice(gi * PGW, (gi + 1) * PGW)
            dpw_ref[gi] += _dot_tn(f["pooled"][gi], dpm[:, cs])
            dpooled_ref[:, cs] = _dot_nt(dpm[:, cs], pw_ref[gi].astype(BF16))
        drest_ref[:, 0:AW] = dz_attn.astype(BF16)
        drest_ref[:, AW:2 * AW] = jnp.zeros((tm, AW), BF16)
        drest_ref[:, 2 * AW:3 * AW] = dz_pool.astype(BF16)
        drest_ref[:, 3 * AW:3 * AW + D] = dg_attn.astype(BF16)
        drest_ref[:, 3 * AW + D:3 * AW + 2 * D] = dg_pool.astype(BF16)

        @pl.when(i == nt - 1)
        def _():
            pltpu.sync_copy(awo, dwo_hbm)
            for k in range(N_SHARD):
                pltpu.sync_copy(awab.at[:, pl.ds(k * sw, sw)], dwab_hbm.at[k])
                pltpu.sync_copy(awpb.at[:, pl.ds(k * sw, sw)], dwpb_hbm.at[k])

    row512 = pl.BlockSpec((tm, AW), lambda i: (i, 0))
    outs = pl.pallas_call(
        body, name="mix_bwd", grid=(nt,),
        in_specs=_mix_specs(tm) + [pl.BlockSpec((tm, D), lambda i: (i, 0)), pl.BlockSpec((D, D), lambda i: (0, 0))],
        out_specs=[row512, pl.BlockSpec((tm, N_SLAB * STAT_W), lambda i: (i, 0)), row512,
                   pl.BlockSpec((tm, REST_W), lambda i: (i, 0)), ANY, ANY, ANY,
                   pl.BlockSpec((4, PGW, PGW), lambda i: (0, 0, 0)), pl.BlockSpec((1, AW), lambda i: (0, 0))],
        out_shape=[_sds((S, AW)), _sds((S, N_SLAB * STAT_W)), _sds((S, AW)), _sds((S, REST_W), BF16),
                   _sds((D, D)), _sds((N_SHARD, AW, sw)), _sds((N_SHARD, AW, sw)), _sds((4, PGW, PGW)), _sds((1, AW))],
        scratch_shapes=[pltpu.VMEM((D, D), F32), pltpu.VMEM((AW, D), F32), pltpu.VMEM((AW, D), F32)],
        compiler_params=_params("arbitrary"),
    )(*os_, *ls_, rest, rest, wab, wpb, pool_w, pool_scale, dmo, wout)
    dattn, stats, dpooled, drest, dwo, dwab, dwpb, dpw, dps = outs
    return dattn, stats, dpooled, drest, dwo.reshape(N_SHARD, D // N_SHARD, D), dwab, dwpb, dpw, dps


def _pool_bwd(dpooled):
    S = dpooled.shape[0]
    tm = 512
    nt = S // tm

    def body(dp_ref, nxt_ref, du_ref):
        i = pl.program_id(0)
        t = i * tm + lax.broadcasted_iota(jnp.int32, (tm + HALO, 1), 0)
        nxt = jnp.where(i < nt - 1, nxt_ref[...], 0.0)
        ext = jnp.concatenate([dp_ref[...], nxt], axis=0)
        for gi, win in enumerate(POOL_WINDOWS):
            cs = slice(gi * PGW, (gi + 1) * PGW)
            s = ext[:, cs] / jnp.minimum(t + 1, win).astype(F32)
            sh = 1
            while sh < win:
                s = s + pltpu.roll(s, tm + HALO - sh, 0)
                sh *= 2
            du_ref[:, cs] = (s[:tm] - dp_ref[:, cs]).astype(BF16)

    return pl.pallas_call(
        body, name="pool_bwd", grid=(nt,),
        in_specs=[pl.BlockSpec((tm, AW), lambda i: (i, 0)),
                  pl.BlockSpec((HALO, AW), lambda i: (jnp.minimum((i + 1) * (tm // HALO), S // HALO - 1), 0))],
        out_specs=pl.BlockSpec((tm, AW), lambda i: (i, 0)),
        out_shape=_sds((S, AW), BF16), compiler_params=_params("parallel"),
    )(dpooled, dpooled)


TB = 1024


def _dh(dproj, wg_in, ride):
    S = dproj.shape[0]
    per = wg_in.shape[2] // TB
    nm, nk = S // TB, IN_W // TB

    def body(dp_ref, w_ref, out_ref):
        @pl.when(pl.program_id(1) == 0)
        def _():
            out_ref[...] = jnp.zeros_like(out_ref)

        out_ref[...] += _dot_nt(dp_ref[...], w_ref[...])

    (dh,), rode = _call_with_ride(
        body, ride, lambda: (pl.program_id(0) == 0) & (pl.program_id(1) == 0),
        lambda: (pl.program_id(0) == nm - 1) & (pl.program_id(1) == nk - 1),
        name="dh", grid=(nm, nk),
        in_specs=[pl.BlockSpec((TB, TB), lambda m, kk: (m, kk)),
                  pl.BlockSpec((None, D, TB), lambda m, kk: (kk // per, 0, kk % per))],
        out_specs=[pl.BlockSpec((TB, D), lambda m, kk: (m, 0))],
        out_shape=[_sds((S, D))], compiler_params=_params("arbitrary", "arbitrary"),
    )(dproj, wg_in)
    return dh, rode


def _dw_in(h, dproj):
    S = dproj.shape[0]
    per = IN_W // N_SHARD // TB

    def body(h_ref, dp_ref, out_ref):
        @pl.when(pl.program_id(1) == 0)
        def _():
            out_ref[...] = jnp.zeros_like(out_ref)

        out_ref[...] += _dot_tn(h_ref[...], dp_ref[...])

    return pl.pallas_call(
        body, name="dw_in", grid=(IN_W // TB, S // TB),
        in_specs=[pl.BlockSpec((TB, D), lambda j, kk: (kk, 0)), pl.BlockSpec((TB, TB), lambda j, kk: (kk, j))],
        out_specs=pl.BlockSpec((None, D, TB), lambda j, kk: (j // per, 0, j % per)),
        out_shape=_sds((N_SHARD, D, IN_W // N_SHARD)), compiler_params=_params("parallel", "arbitrary"),
    )(h, dproj)


def _prenorm_bwd(x, dh, dx2, norm_g, mod):
    S = x.shape[0]
    tm = 512

    def body(x_ref, dh_ref, dx2_ref, g_ref, mod_ref, gx_ref, dg_ref, dshift_ref, dscale_ref):
        i = pl.program_id(0)

        @pl.when(i == 0)
        def _():
            dg_ref[...] = jnp.zeros_like(dg_ref)
            dshift_ref[...] = jnp.zeros_like(dshift_ref)
            dscale_ref[...] = jnp.zeros_like(dscale_ref)

        xv = x_ref[...]
        dhv = dh_ref[...]
        g = g_ref[...]
        r = lax.rsqrt(jnp.mean(xv * xv, axis=-1, keepdims=True) + EPS)
        xh = xv * r
        dshift_ref[...] += jnp.sum(dhv, axis=0, keepdims=True)
        dscale_ref[...] += jnp.sum(dhv * (xh * g), axis=0, keepdims=True)
        dn1 = dhv * (1.0 + mod_ref[:, D:2 * D])
        dg_ref[...] += jnp.sum(dn1 * xh, axis=0, keepdims=True)
        dxh = dn1 * g
        gx_ref[...] = dx2_ref[...] + r * (dxh - xh * jnp.mean(dxh * xh, axis=-1, keepdims=True))

    row = pl.BlockSpec((tm, D), lambda i: (i, 0))
    vec = pl.BlockSpec((1, D), lambda i: (0, 0))
    return pl.pallas_call(
        body, name="prenorm_bwd", grid=(S // tm,),
        in_specs=[row, row, row, vec, pl.BlockSpec((1, 3 * D), lambda i: (0, 0))],
        out_specs=[row, vec, vec, vec],
        out_shape=[_sds((S, D)), _sds((1, D)), _sds((1, D)), _sds((1, D))],
        compiler_params=_params("arbitrary"),
    )(x, dh, dx2, norm_g, mod)


def _local_step(x, target, mod, wg_in, own_late, pool_w, pool_scale, rel_bias, norm_g, final_g, half_idx, chip_half):
    buckets = jnp.asarray(_bucket_tables())
    bias_tab = _bias_table(rel_bias, buckets)
    h = _prenorm(x, norm_g, mod)
    qkv = [_proj(h, wg_in, 3 * g, 3, F32, f"proj_qkv{g}") for g in range(NG)]
    rest = _proj(h, wg_in, NCB_QKV, REST_W // CB, F32, "proj_rest")
    o0, l0, late = _attn_fwd(qkv[0], bias_tab, 0, _ride_gather_send(own_late))
    o1, l1, late = _attn_fwd(qkv[1], bias_tab, 1, _ride_gather_forward([_place_own(b, o) for b, o in zip(late, own_late)]))
    o2, l2, _ = _attn_fwd(qkv[2], bias_tab, 2, None)
    os_, ls_ = (o0, o1, o2), (l0, l1, l2)
    wab = late[0].reshape(N_SHARD, AW, D // N_SHARD).transpose(1, 0, 2).reshape(AW, D)
    wpb = late[1].reshape(N_SHARD, AW, D // N_SHARD).transpose(1, 0, 2).reshape(AW, D)
    wout = late[2].reshape(D, D)
    dx2, dmo, loss, dfinal_g, dgate = _tail(x, target, os_, ls_, rest, wab, wpb, pool_w, pool_scale, wout, mod, final_g)
    dattn, stats, dpooled, drest, dw_out, dw_ab, dw_pb, dpool_w, dpool_scale = _mix_bwd(
        dmo, os_, ls_, rest, wab, wpb, pool_w, pool_scale, wout)
    du = _pool_bwd(dpooled)

    small = [dw_ab, dw_pb, dw_out]
    dqkv0, ds0, sib_small = _attn_bwd(qkv[0], dattn, stats, bias_tab, 0, _ride_sibling_halves(small))
    p_small = [_pair_sum(g, t, half_idx, f"rs_pair_sum{a}") for a, (g, t) in enumerate(zip(small, sib_small))]
    dqkv1, ds1, u_small = _attn_bwd(qkv[1], dattn, stats, bias_tab, 1,
                                    _ride_chip_exchange([p16 for _, p16 in p_small]))
    rs_ab, rs_pb, rs_out = [_chip_sum(p32, u, chip_half, f"rs_chip_sum{a}")
                            for a, ((p32, _), u) in enumerate(zip(p_small, u_small))]
    dqkv2, ds2, _ = _attn_bwd(qkv[2], dattn, stats, bias_tab, 2, None)

    dproj = jnp.concatenate([a.astype(BF16) for a in dqkv0 + dqkv1 + dqkv2] + [drest[:, :AW], du, drest[:, 2 * AW:]],
                            axis=1)
    dw_in = _dw_in(h, dproj)
    drel_rows, (sib_in,) = _bias_grad(jnp.concatenate([ds0, ds1, ds2], axis=0), buckets,
                                      _ride_sibling_halves([dw_in]))
    drel = drel_rows[:, 0, :NUM_BUCKETS].T
    p32_in, p16_in = _pair_sum(dw_in, sib_in, half_idx, "rs_pair_sum_in")
    dh, (u_in,) = _dh(dproj, wg_in, _ride_chip_exchange([p16_in]))
    rs_in = _chip_sum(p32_in, u_in, chip_half, "rs_chip_sum_in")

    grad_x, dnorm_g, dshift, dscale = _prenorm_bwd(x, dh, dx2, norm_g, mod)
    dmod = jnp.concatenate([dshift, dscale, dgate], axis=1)
    return dict(loss=loss[0, 0], grad_x=grad_x, dmod=dmod, dnorm_g=dnorm_g, dfinal_g=dfinal_g, dpool_w=dpool_w,
                dpool_scale=dpool_scale, drel_bias=drel, dw_in=dw_in, dw_attn_br=dw_ab, dw_pool_br=dw_pb,
                dw_out=dw_out, rs_in=rs_in, rs_attn_br=rs_ab, rs_pool_br=rs_pb, rs_out=rs_out)


def _allgather8(blocks, name, relay=None):
    nb = len(blocks)
    relay = [False] * nb if relay is None else list(relay)

    def body(*refs):
        ins, outs = refs[:nb], refs[nb:2 * nb]
        send_sems, recv_sems = refs[2 * nb:]
        x, y, c = lax.axis_index("x"), lax.axis_index("y"), lax.axis_index("c")
        me, sibling = (x, y, c), (x, y, 1 - c)
        here, xn, yn, dg = (x, y), (1 - x, y), (x, 1 - y), (1 - x, 1 - y)

        def slot(a, chip, core, half=None):
            ref = outs[a].at[4 * chip[0] + 2 * chip[1] + core]
            if half is None:
                return ref
            r2 = ref.shape[0] // 2
            return ref.at[pl.ds(half * r2, r2)]

        def copy(a, k, dst, to, src=None):
            return pltpu.make_async_remote_copy(src_ref=dst if src is None else src, dst_ref=dst,
                                                send_sem=send_sems.at[a, k], recv_sem=recv_sems.at[a, k],
                                                device_id=to, device_id_type=MESH)

        def start(cps):
            for cp in cps:
                cp.start()
            return cps

        sent = []
        for a in range(nb):
            own = slot(a, here, c)
            sent += [copy(a, 0, own, sibling, src=ins[a]), copy(a, 1, own, (*xn, c), src=ins[a]),
                     copy(a, 2, own, (*yn, c), src=ins[a])]
            if not relay[a]:
                sent.append(copy(a, 3, own, (*dg, c), src=ins[a]))
        start(sent)
        for a in range(nb):
            copy(a, 2, slot(a, yn, c), me).wait_recv()
            sent += start([copy(a, 6, slot(a, yn, c), sibling)]
                          + ([copy(a, 3, slot(a, yn, c, 0), (*xn, c))] if relay[a] else []))
        for a in range(nb):
            copy(a, 1, slot(a, xn, c), me).wait_recv()
            sent += start([copy(a, 5, slot(a, xn, c), sibling)]
                          + ([copy(a, 4, slot(a, xn, c, 1), (*yn, c))] if relay[a] else []))
        for a in range(nb):
            for k, half in ((3, 0), (4, 1)) if relay[a] else ((3, None),):
                copy(a, k, slot(a, dg, c, half), me).wait_recv()
                sent += start([copy(a, 4 + k, slot(a, dg, c, half), sibling)])
        for a in range(nb):
            copy(a, 0, slot(a, here, 1 - c), me).wait_recv()
            copy(a, 5, slot(a, xn, 1 - c), me).wait_recv()
            copy(a, 6, slot(a, yn, 1 - c), me).wait_recv()
            for k, half in ((7, 0), (8, 1)) if relay[a] else ((7, None),):
                copy(a, k, slot(a, dg, 1 - c, half), me).wait_recv()
        for cp in sent:
            cp.wait_send()

    outs = pl.pallas_call(
        body, name=name, in_specs=[ANY] * nb, out_specs=[ANY] * nb,
        out_shape=[_sds((8,) + b.shape, b.dtype) for b in blocks],
        scratch_shapes=[_dma_sems(nb, 9), _dma_sems(nb, 9)],
    )(*blocks)
    return [_place_own(buf, b) for buf, b in zip(outs, blocks)]


def _place_own(buf, block):
    dev = 4 * lax.axis_index("x") + 2 * lax.axis_index("y") + lax.axis_index("c")
    return lax.dynamic_update_index_in_dim(buf, block, dev, 0)


def _ride_gather_send(blocks):
    def copies(ins, outs, send_sems, recv_sems):
        x, y, c = lax.axis_index("x"), lax.axis_index("y"), lax.axis_index("c")
        cps = []
        for a in range(len(blocks)):
            own = outs[a].at[4 * x + 2 * y + c]
            for k, to in enumerate([(x, y, 1 - c), (1 - x, y, c), (x, 1 - y, c), (1 - x, 1 - y, c)]):
                cps.append(pltpu.make_async_remote_copy(src_ref=ins[a], dst_ref=own, send_sem=send_sems.at[4 * a + k],
                                                        recv_sem=recv_sems.at[4 * a + k], device_id=to,
                                                        device_id_type=MESH))
        return cps

    return _Ride(blocks, [_sds((8,) + b.shape, b.dtype) for b in blocks], 4 * len(blocks), copies)


def _ride_gather_forward(bufs):
    def copies(ins, outs, send_sems, recv_sems):
        x, y, c = lax.axis_index("x"), lax.axis_index("y"), lax.axis_index("c")
        cps = []
        for a in range(len(bufs)):
            for j, (ox, oy) in enumerate([(1 - x, y), (x, 1 - y), (1 - x, 1 - y)]):
                blk = outs[a].at[4 * ox + 2 * oy + c]
                cps.append(pltpu.make_async_remote_copy(src_ref=blk, dst_ref=blk, send_sem=send_sems.at[3 * a + j],
                                                        recv_sem=recv_sems.at[3 * a + j], device_id=(x, y, 1 - c),
                                                        device_id_type=MESH))
        return cps

    return _Ride(bufs, [_sds(b.shape, b.dtype) for b in bufs], 3 * len(bufs), copies, in_place=True)


def _ride_sibling_halves(gs):
    def copies(ins, outs, send_sems, recv_sems):
        x, y, c = lax.axis_index("x"), lax.axis_index("y"), lax.axis_index("c")
        cps = []
        for a in range(len(gs)):
            r2 = ins[a].shape[1] // 2
            other = ins[a].at[:, pl.ds((1 - c) * r2, r2), :]
            cps.append(pltpu.make_async_remote_copy(src_ref=other, dst_ref=outs[a], send_sem=send_sems.at[a],
                                                    recv_sem=recv_sems.at[a], device_id=(x, y, 1 - c),
                                                    device_id_type=MESH))
        return cps

    return _Ride(gs, [_sds((g.shape[0], g.shape[1] // 2, g.shape[2]), g.dtype) for g in gs], len(gs), copies)


def _pair_sum(g, t, half, name):
    nsh, rows, cols = g.shape
    r2 = rows // 2
    tr = _row_tile(r2, cols)
    nt = r2 // tr

    def body(half_ref, g_ref, t_ref, p32_ref, p16_ref):
        p = g_ref[...] + t_ref[...]
        p32_ref[...] = p
        p16_ref[...] = p.astype(BF16)

    blk = pl.BlockSpec((None, tr, cols), lambda k, i, half_ref: (k, i, 0))
    return pl.pallas_call(
        body, name=name,
        grid_spec=pltpu.PrefetchScalarGridSpec(
            num_scalar_prefetch=1, grid=(nsh, nt),
            in_specs=[pl.BlockSpec((None, tr, cols), lambda k, i, half_ref: (k, half_ref[0] * nt + i, 0)), blk],
            out_specs=[blk, blk]),
        out_shape=[_sds((nsh, r2, cols)), _sds((nsh, r2, cols), BF16)],
        compiler_params=_params("parallel", "parallel"),
    )(half, g, t)


def _ride_chip_exchange(ps):
    def copies(ins, outs, send_sems, recv_sems):
        x, y, c = lax.axis_index("x"), lax.axis_index("y"), lax.axis_index("c")
        chips = [(1 - x, y), (x, 1 - y), (1 - x, 1 - y)]
        cps = []
        for a in range(len(ps)):
            for j, (ox, oy) in enumerate(chips):
                cps.append(pltpu.make_async_remote_copy(src_ref=ins[a].at[2 * ox + oy], dst_ref=outs[a].at[j],
                                                        send_sem=send_sems.at[3 * a + j],
                                                        recv_sem=recv_sems.at[3 * a + j],
                                                        device_id=(ox, oy, c), device_id_type=MESH))
        return cps

    return _Ride(ps, [_sds((3,) + p.shape[1:], p.dtype) for p in ps], 3 * len(ps), copies)


def _chip_sum(p32, u, chip_half, name):
    r2, cols = p32.shape[1:]
    tr = _row_tile(r2, cols)
    nt = r2 // tr

    def body(ch_ref, p_ref, u_ref, o_ref):
        acc = p_ref[...]
        for j in range(3):
            acc = acc + u_ref[j].astype(F32)
        o_ref[...] = acc

    return pl.pallas_call(
        body, name=name,
        grid_spec=pltpu.PrefetchScalarGridSpec(
            num_scalar_prefetch=1, grid=(nt,),
            in_specs=[pl.BlockSpec((None, tr, cols), lambda i, ch_ref: (ch_ref[0], i, 0)),
                      pl.BlockSpec((3, tr, cols), lambda i, ch_ref: (0, i, 0))],
            out_specs=pl.BlockSpec((tr, cols), lambda i, ch_ref: (ch_ref[1] * nt + i, 0))),
        out_shape=_sds((2 * r2, cols)), compiler_params=_params("parallel"),
    )(chip_half, p32, u)


def _sibling_join(fs, name):
    nb = len(fs)

    def body(*refs):
        outs = refs[nb:2 * nb]
        send_sems, recv_sems = refs[2 * nb:]
        x, y, c = lax.axis_index("x"), lax.axis_index("y"), lax.axis_index("c")
        cps = []
        for a in range(nb):
            r2 = outs[a].shape[0] // 2
            rows = outs[a].at[pl.ds(c * r2, r2), :]
            cps.append(pltpu.make_async_remote_copy(src_ref=rows, dst_ref=rows, send_sem=send_sems.at[a],
                                                    recv_sem=recv_sems.at[a], device_id=(x, y, 1 - c),
                                                    device_id_type=MESH))
        for cp in cps:
            cp.start()
        for cp in cps:
            cp.wait()

    return pl.pallas_call(
        body, name=name, in_specs=[ANY] * nb, out_specs=[ANY] * nb,
        out_shape=[_sds(f.shape, f.dtype) for f in fs],
        input_output_aliases={a: a for a in range(nb)},
        scratch_shapes=[_dma_sems(nb), _dma_sems(nb)],
    )(*fs)


def _row_tile(rows, cols):
    tile = rows
    while tile * cols * 4 > (1 << 20) and tile % 16 == 0:
        tile //= 2
    return tile


def _w_ada_grad(c_all, dmod_cols):
    def body(c_ref, d_ref, o_ref):
        o_ref[...] = _dot_tn(c_ref[...].astype(BF16), d_ref[...].astype(BF16))

    return pl.pallas_call(body, name="w_ada_grad", out_shape=_sds((c_all.shape[1], dmod_cols.shape[1])),
                          compiler_params=_params())(c_all, dmod_cols)


def _adam_math(w, g, m, v):
    nm = ADAM_B1 * m + (1.0 - ADAM_B1) * g
    nv = ADAM_B2 * v + (1.0 - ADAM_B2) * (g * g)
    m_hat = nm / (1.0 - ADAM_B1 ** ADAM_STEP)
    v_hat = nv / (1.0 - ADAM_B2 ** ADAM_STEP)
    return -ADAM_LR * (m_hat / (jnp.sqrt(v_hat) + ADAM_EPS) + ADAM_WD * w), nm, nv


def _adamw(w, g, m, v, name):
    rows, cols = w.shape
    tr = _row_tile(rows, cols)

    def body(w_ref, g_ref, m_ref, v_ref, go_ref, d_ref, nm_ref, nv_ref):
        gv = g_ref[...]
        go_ref[...] = gv
        d_ref[...], nm_ref[...], nv_ref[...] = _adam_math(w_ref[...], gv, m_ref[...], v_ref[...])

    spec = pl.BlockSpec((tr, cols), lambda i: (i, 0))
    return pl.pallas_call(
        body, name=name, grid=(rows // tr,), in_specs=[spec] * 4, out_specs=[spec] * 4,
        out_shape=[_sds((rows, cols))] * 4, compiler_params=_params("parallel"),
    )(w, g, m, v)


def _pack_small(dmod, dnorm_g, dfinal_g, dpool_scale, drel_bias, loss, dpool_w):
    return jnp.concatenate([dmod.reshape(-1, 128), dnorm_g.reshape(-1, 128), dfinal_g.reshape(-1, 128),
                            jnp.pad(dpool_scale.reshape(-1, 128), ((0, PK_RELB - PK_PSCALE - AW // 128), (0, 0))),
                            jnp.pad(drel_bias, ((0, 0), (0, 128 - NG * NH))),
                            jnp.full((PK_POOLW - PK_LOSS, 128), loss, F32), dpool_w.reshape(-1, 128)], axis=0)


def _small_update(small_all, ws, ms, vs):
    lane_rows = [(r0, r0 + w.shape[1] // 128) for r0, w in zip((PK_BADA, PK_NORMG, PK_FINALG, PK_PSCALE), ws)]
    nw = len(ws)

    def body(all_ref, *refs):
        w_refs, m_refs, v_refs = refs[:nw], refs[nw:2 * nw], refs[2 * nw:3 * nw]
        loss_ref, outs = refs[3 * nw], refs[3 * nw + 1:]
        g = all_ref[0]
        for s in range(1, all_ref.shape[0]):
            g = g + all_ref[s]
        loss_ref[...] = jnp.broadcast_to(g[PK_LOSS:PK_LOSS + 1, :], loss_ref.shape)

        def put(p, at, gv):
            d, nm, nv = _adam_math(w_refs[p][at], gv, m_refs[p][at], v_refs[p][at])
            for o_ref, val in zip(outs[4 * p:4 * p + 4], (gv, d, nm, nv)):
                o_ref[at] = val

        for p, (r0, r1) in enumerate(lane_rows):
            for i in range(r1 - r0):
                put(p, (slice(None), slice(128 * i, 128 * (i + 1))), g[r0 + i:r0 + i + 1, :])
        put(4, (slice(None), slice(None)), g[PK_RELB:PK_LOSS, 0:NG * NH])
        put(5, (slice(None), slice(None)), g[PK_POOLW:PK_ROWS, :])

    res = pl.pallas_call(
        body, name="small_update",
        out_shape=[_sds((8, 128))] + [_sds(w.shape) for w in ws for _ in range(4)], compiler_params=_params(),
    )(small_all, *ws, *ms, *vs)
    return res[0], [res[1 + 4 * p:5 + 4 * p] for p in range(nw)]


def kernel(x, c, norm_g, w_ada, b_ada, w_in, pool_w, pool_scale, w_attn_br, w_pool_br, w_out, rel_bias, final_g, loss_target, m_norm_g, m_w_ada, m_b_ada, m_w_in, m_pool_w, m_pool_scale, m_w_attn_br, m_w_pool_br, m_w_out, m_rel_bias, m_final_g, v_norm_g, v_w_ada, v_b_ada, v_w_in, v_pool_w, v_pool_scale, v_w_attn_br, v_w_pool_br, v_w_out, v_rel_bias, v_final_g):
    ix, iy, ic = lax.axis_index("x"), lax.axis_index("y"), lax.axis_index("c")
    dev = 4 * ix + 2 * iy + ic
    chip = 2 * ix + iy

    def half(w):
        r2 = w.shape[0] // 2
        return lax.dynamic_slice_in_dim(w, ic * r2, r2, axis=0).astype(BF16)

    gathered = _allgather8([jnp.broadcast_to(c, (8, D)), half(w_in[0])], "gather_weights", relay=[False, True])
    c_all = gathered[0][:, 0, :]
    wg_in = gathered[1].reshape(N_SHARD, D, IN_W // N_SHARD)

    mw = 3 * D // N_SHARD
    modp = _mod_partial(c_all, w_ada[0], lax.dynamic_slice_in_dim(b_ada, chip * mw, mw, axis=1))
    mod_all = _allgather8([modp], "gather_mod")[0]
    mod_full = mod_all[::2].transpose(1, 0, 2).reshape(8, 3 * D)
    mod = lax.dynamic_slice_in_dim(mod_full, dev, 1, axis=0)

    half_idx = jnp.stack([ic]).astype(jnp.int32)
    chip_half = jnp.stack([chip, ic]).astype(jnp.int32)
    r = _local_step(x[0], loss_target[0], mod, wg_in, [half(w_attn_br[0]), half(w_pool_br[0]), half(w_out[0])],
                    pool_w[0], pool_scale, rel_bias, norm_g, final_g.reshape(1, D), half_idx, chip_half)

    packed = _pack_small(r["dmod"], r["dnorm_g"], r["dfinal_g"], r["dpool_scale"], r["drel_bias"], r["loss"],
                         r["dpool_w"])
    small_all = _allgather8([packed], "gather_small")[0]
    small = ["b_ada", "norm_g", "final_g", "pool_scale", "rel_bias", "pool_w"]
    shaped = lambda b, n, f, ps, rb, pw: [b, n, f.reshape(1, D), ps, rb, pw.reshape(4 * PGW, PGW)]
    loss, small_out = _small_update(small_all, shaped(b_ada, norm_g, final_g, pool_scale, rel_bias, pool_w),
                                    shaped(m_b_ada, m_norm_g, m_final_g, m_pool_scale, m_rel_bias, m_pool_w),
                                    shaped(v_b_ada, v_norm_g, v_final_g, v_pool_scale, v_rel_bias, v_pool_w))
    dmod_all = small_all[:, PK_BADA:PK_NORMG, :].reshape(8, 3 * D)
    g_w_ada = _w_ada_grad(c_all, lax.dynamic_slice_in_dim(dmod_all, chip * mw, mw, axis=1))

    g_w_in, g_w_ab, g_w_pb, g_w_out = _sibling_join([r["rs_in"], r["rs_attn_br"], r["rs_pool_br"], r["rs_out"]],
                                                    "rs_sibling_join")
    upd = dict(zip(small, small_out))
    upd["final_g"] = [a.reshape(D) for a in upd["final_g"]]
    upd["pool_w"] = [a.reshape(1, 4, PGW, PGW) for a in upd["pool_w"]]
    for nme, w, g, m, v in (("w_ada", w_ada, g_w_ada, m_w_ada, v_w_ada), ("w_in", w_in, g_w_in, m_w_in, v_w_in),
                            ("w_attn_br", w_attn_br, g_w_ab, m_w_attn_br, v_w_attn_br),
                            ("w_pool_br", w_pool_br, g_w_pb, m_w_pool_br, v_w_pool_br),
                            ("w_out", w_out, g_w_out, m_w_out, v_w_out)):
        upd[nme] = [a[None] for a in _adamw(w[0], g, m[0], v[0], "adamw_" + nme)]
    names = ["norm_g", "w_ada", "b_ada", "w_in", "pool_w", "pool_scale", "w_attn_br", "w_pool_br", "w_out",
             "rel_bias", "final_g"]
    return (loss[0, 0], r["grad_x"][None]) + tuple(upd[nme][kind] for kind in range(4) for nme in names)
```

```python
import functools
import math

import numpy as np
import jax
import jax.numpy as jnp
from jax import lax
from jax.experimental import pallas as pl
from jax.experimental.pallas import tpu as pltpu

F32 = jnp.float32
BF16 = jnp.bfloat16

D = 1024
HD = 64
NH = 8
AW = NH * HD
GROUPS = ((128, 1), (512, 4), (2048, 16))
NG = len(GROUPS)
BLK = 128
GW = 3 * AW
QKV_W = NG * GW
REST_W = 3584
IN_W = QKV_W + REST_W
CB = 512
NCB = IN_W // CB
NCB_QKV = QKV_W // CB
POOL_WINDOWS = (2, 4, 8, 16)
PGW = 128
HALO = 16
NUM_BUCKETS = 32
MAX_DISTANCE = 2048
EPS = 1e-6
NEG = -1e30
N_SHARD = 4
VMEM_LIMIT = 56 * 1024 * 1024

ADAM_LR = 0.001
ADAM_B1 = 0.9
ADAM_B2 = 0.999
ADAM_EPS = 1e-08
ADAM_WD = 0.01
ADAM_STEP = 10

PK_BADA, PK_NORMG, PK_FINALG, PK_PSCALE, PK_RELB, PK_LOSS, PK_POOLW, PK_ROWS = 0, 24, 32, 40, 48, 80, 88, 600

ANY = pl.BlockSpec(memory_space=pl.ANY)
MESH = pl.DeviceIdType.MESH


def _params(*sem):
    return pltpu.CompilerParams(dimension_semantics=sem, vmem_limit_bytes=VMEM_LIMIT)


def _sds(shape, dtype=F32):
    return jax.ShapeDtypeStruct(shape, dtype)


def _dot(a, b):
    return jnp.dot(a, b, preferred_element_type=F32)


def _dot_nt(a, b):
    return lax.dot_general(a, b, (((1,), (1,)), ((), ())), preferred_element_type=F32)


def _dot_tn(a, b):
    return lax.dot_general(a, b, (((0,), (0,)), ((), ())), preferred_element_type=F32)


def _sigmoid(z):
    return 0.5 * jnp.tanh(0.5 * z) + 0.5


def _dma_sems(*shape):
    return pltpu.SemaphoreType.DMA(shape)


class _Ride:
    def __init__(self, arrays, out_shapes, n_copies, copies, in_place=False):
        self.arrays, self.out_shapes, self.n_copies, self.copies = list(arrays), list(out_shapes), n_copies, copies
        self.in_place = in_place


def _call_with_ride(body, ride, first, last, *, in_specs, out_specs, out_shape, scratch_shapes=(), **kw):
    in_specs, out_specs, out_shape, scratch_shapes = list(in_specs), list(out_specs), list(out_shape), list(scratch_shapes)
    n_in, n_out, n_sc = len(in_specs), len(out_specs), len(scratch_shapes)
    if ride is None:
        def run_plain(*operands):
            return pl.pallas_call(body, in_specs=in_specs, out_specs=out_specs, out_shape=out_shape,
                                  scratch_shapes=scratch_shapes, **kw)(*operands), []
        return run_plain
    n_ri, n_ro = len(ride.arrays), len(ride.out_shapes)

    def wrapped(*refs):
        ins, rest = refs[:n_in], refs[n_in:]
        r_ins, rest = rest[:n_ri], rest[n_ri:]
        outs, rest = rest[:n_out], rest[n_out:]
        r_outs, rest = rest[:n_ro], rest[n_ro:]
        scratch, (send_sems, recv_sems) = rest[:n_sc], rest[n_sc:]

        @pl.when(first())
        def _():
            for cp in ride.copies(r_ins, r_outs, send_sems, recv_sems):
                cp.start()

        body(*ins, *outs, *scratch)

        @pl.when(last())
        def _():
            for cp in ride.copies(r_ins, r_outs, send_sems, recv_sems):
                cp.wait()

    def run(*operands):
        res = pl.pallas_call(
            wrapped, in_specs=in_specs + [ANY] * n_ri, out_specs=out_specs + [ANY] * n_ro,
            out_shape=out_shape + ride.out_shapes,
            scratch_shapes=scratch_shapes + [_dma_sems(ride.n_copies), _dma_sems(ride.n_copies)],
            input_output_aliases={n_in + a: n_out + a for a in range(n_ri)} if ride.in_place else {}, **kw,
        )(*operands, *ride.arrays)
        return res[:n_out], res[n_out:]
    return run


def _bucket_tables():
    i = np.arange(BLK)[:, None]
    j = np.arange(2 * BLK)[None, :]
    dist = BLK + i - j
    valid = (dist >= 0) & (dist <= BLK)
    tabs = []
    for _, dil in GROUPS:
        n = (np.clip(dist, 0, BLK) * dil).astype(np.int32)
        max_exact = NUM_BUCKETS // 2
        nf = np.maximum(n, 1).astype(np.float32)
        large = max_exact + (np.log(nf / np.float32(max_exact)) / np.float32(math.log(MAX_DISTANCE / max_exact))
                             * np.float32(NUM_BUCKETS - max_exact)).astype(np.int32)
        large = np.minimum(large, NUM_BUCKETS - 1)
        bucket = np.where(n < max_exact, n, large)
        tab = np.where(valid, bucket, -1).astype(np.int32)
        perm = _block_perm(dil)
        tabs.append(tab[perm][:, np.concatenate([perm, BLK + perm])])
    return np.stack(tabs)


def _bias_table(rel_bias, buckets):
    def body(rb_ref, bk_ref, out_ref):
        g = pl.program_id(0)
        bk = bk_ref[...]
        for h in range(NH):
            acc = jnp.full((BLK, 2 * BLK), NEG, F32)
            for b in range(NUM_BUCKETS):
                acc = jnp.where(bk == b, rb_ref[b, g * NH + h], acc)
            out_ref[h] = acc

    return pl.pallas_call(
        body, name="bias_table", grid=(NG,),
        in_specs=[pl.BlockSpec(memory_space=pltpu.SMEM),
                  pl.BlockSpec((None, BLK, 2 * BLK), lambda g: (g, 0, 0))],
        out_specs=pl.BlockSpec((NH, BLK, 2 * BLK), lambda g: (g, 0, 0)),
        out_shape=_sds((NG * NH, BLK, 2 * BLK)),
        compiler_params=_params("arbitrary"),
    )(rel_bias, buckets)


def _bias_grad(ds_acc, buckets, ride):
    def body(acc_ref, bk_ref, out_ref):
        bk = bk_ref[...]
        acc = acc_ref[...]
        lane = lax.broadcasted_iota(jnp.int32, (8, 128), 1)
        out = jnp.zeros((8, 128), F32)
        for b in range(NUM_BUCKETS):
            val = jnp.sum(jnp.where(bk == b, acc, 0.0))
            out = jnp.where(lane == b, val, out)
        out_ref[...] = out

    (out,), rode = _call_with_ride(
        body, ride, lambda: pl.program_id(0) == 0, lambda: pl.program_id(0) == NG * NH - 1,
        name="bias_grad", grid=(NG * NH,),
        in_specs=[pl.BlockSpec((None, BLK, 2 * BLK), lambda gh: (gh, 0, 0)),
                  pl.BlockSpec((None, BLK, 2 * BLK), lambda gh: (gh // NH, 0, 0))],
        out_specs=[pl.BlockSpec((None, 8, 128), lambda gh: (gh, 0, 0))],
        out_shape=[_sds((NG * NH, 8, 128))],
        compiler_params=_params("arbitrary"),
    )(ds_acc, buckets)
    return out, rode


def _mod_partial(c_all, w_ada_s, b_ada_s):
    def body(c_ref, w_ref, b_ref, o_ref):
        o_ref[...] = _dot(c_ref[...].astype(BF16), w_ref[...].astype(BF16)) + b_ref[...]

    return pl.pallas_call(body, name="mod_partial", out_shape=_sds((8, w_ada_s.shape[1])),
                          compiler_params=_params())(c_all, w_ada_s, b_ada_s)


def _prenorm(x, norm_g, mod):
    S = x.shape[0]
    tm = 512

    def body(x_ref, g_ref, mod_ref, h_ref):
        xv = x_ref[...]
        r = lax.rsqrt(jnp.mean(xv * xv, axis=-1, keepdims=True) + EPS)
        n1 = xv * r * g_ref[...]
        h_ref[...] = (n1 * (1.0 + mod_ref[:, D:2 * D]) + mod_ref[:, 0:D]).astype(BF16)

    return pl.pallas_call(
        body, name="prenorm", grid=(S // tm,),
        in_specs=[pl.BlockSpec((tm, D), lambda i: (i, 0)), pl.BlockSpec((1, D), lambda i: (0, 0)),
                  pl.BlockSpec((1, 3 * D), lambda i: (0, 0))],
        out_specs=pl.BlockSpec((tm, D), lambda i: (i, 0)),
        out_shape=_sds((S, D), BF16), compiler_params=_params("parallel"),
    )(x, norm_g, mod)


def _proj(h, wg_in, j0, nj, dtype, name):
    S = h.shape[0]
    tm = 2048
    per = wg_in.shape[2] // CB

    def body(h_ref, w_ref, o_ref):
        o_ref[...] = _dot(h_ref[...], w_ref[...]).astype(dtype)

    return pl.pallas_call(
        body, name=name, grid=(S // tm, nj),
        in_specs=[pl.BlockSpec((tm, D), lambda m, j: (m, 0)),
                  pl.BlockSpec((None, D, CB), lambda m, j: ((j0 + j) // per, 0, (j0 + j) % per))],
        out_specs=pl.BlockSpec((tm, CB), lambda m, j: (m, j)),
        out_shape=_sds((S, nj * CB), dtype), compiler_params=_params("parallel", "parallel"),
    )(h, wg_in)


HS = 4
SLAB = HS * HD


def _lane_head(rows):
    return lax.broadcasted_iota(jnp.int32, (rows, SLAB), 1) // HD


def _head_stack(a):
    head = _lane_head(a.shape[0])
    return jnp.concatenate([jnp.where(head == h, a, jnp.zeros_like(a)) for h in range(HS)], axis=0)


def _head_unstack(a):
    rows = a.shape[0] // HS
    head = _lane_head(rows)
    out = a[:rows]
    for h in range(1, HS):
        out = jnp.where(head == h, a[h * rows:(h + 1) * rows], out)
    return out


STAT_W = 128
VIEW = 16


def _sub_layout(dil):
    if dil == 1:
        return BLK, [None]
    return BLK * dil // VIEW, [[r + dil * u for u in range(VIEW // dil)] for r in range(dil)]


def _block_perm(dil):
    a_rows, _ = _sub_layout(dil)
    p = np.arange(BLK)
    return p if dil == 1 else (VIEW // dil) * (p % a_rows) + p // a_rows


LB = 128
N_SLAB = NH // HS


def _ld(refs, bs, s, w):
    if bs is None:
        return refs[0][:, s * w:(s + 1) * w]
    a_rows = refs[0].shape[0] // VIEW
    return jnp.concatenate([jnp.concatenate([ref[pl.ds(b, a_rows, stride=VIEW), :] for b in bs], axis=0)
                            for ref in refs], axis=1)


def _st(ref, bs, s, val):
    if bs is None:
        ref[:, s * SLAB:(s + 1) * SLAB] = val
        return
    a_rows = val.shape[0] // len(bs)
    for u, b in enumerate(bs):
        ref[:, b, :] = val[u * a_rows:(u + 1) * a_rows]


def _attn_views(dil, S):
    a_rows, subs = _sub_layout(dil)
    if dil == 1:
        def ispecs(base, w, f):
            return [pl.BlockSpec((BLK, N_SLAB * w), lambda sg, n: (f(n), base // (N_SLAB * w)))]
        return subs, S // BLK, N_SLAB, ispecs, (lambda w: (S, w)), (
            lambda f: pl.BlockSpec((BLK, AW), lambda sg, n: (f(n), 0)))

    def ispecs(base, w, f):
        return [pl.BlockSpec((a_rows * VIEW, LB), lambda sg, n, k=k: (f(n), (base + sg * w) // LB + k))
                for k in range(w // LB)]
    return subs, S // (a_rows * VIEW), 1, ispecs, (lambda w: (S // VIEW, VIEW, w)), (
        lambda f: pl.BlockSpec((a_rows, VIEW, SLAB), lambda sg, n: (f(n), 0, sg)))


def _slab_axis(ride):
    return "parallel" if ride is None else "arbitrary"


def _attn_fwd(qkv_g, bias_tab, g, ride):
    S = qkv_g.shape[0]
    subs, nbq, sps, ispecs, shape, ospec = _attn_views(GROUPS[g][1], S)
    cur, prev = (lambda n: n), (lambda n: jnp.maximum(n - 1, 0))
    in_specs = [ispecs(0, SLAB, cur), ispecs(AW, SLAB, prev), ispecs(AW, SLAB, cur), ispecs(2 * AW, SLAB, prev),
                ispecs(2 * AW, SLAB, cur)]
    nl = len(in_specs[0])

    def body(*refs):
        q, kp, kc, vp, vc = (refs[t * nl:(t + 1) * nl] for t in range(5))
        b_ref, o_ref, l_ref = refs[5 * nl:]
        n = pl.program_id(1)
        col = lax.broadcasted_iota(jnp.int32, (HS * BLK, 2 * BLK), 1)
        keep = (col >= BLK) | (n > 0)
        for s_ in range(sps):
            bias = b_ref[pl.ds(s_ * HS, HS)].reshape(HS * BLK, 2 * BLK)
            for bs in subs:
                kb = jnp.concatenate([_ld(kp, bs, s_, SLAB), _ld(kc, bs, s_, SLAB)], axis=0).astype(BF16)
                vb = jnp.concatenate([_ld(vp, bs, s_, SLAB), _ld(vc, bs, s_, SLAB)], axis=0).astype(BF16)
                s = _dot_nt(_head_stack(_ld(q, bs, s_, SLAB).astype(BF16)), kb) * (HD ** -0.5) + bias
                s = jnp.where(keep, s, NEG)
                m = jnp.max(s, axis=-1, keepdims=True)
                p = jnp.exp(s - m)
                den = jnp.sum(p, axis=-1, keepdims=True)
                _st(o_ref, bs, s_, _head_unstack(_dot(p.astype(BF16), vb) / den))
                _st(l_ref, bs, s_, _head_unstack(jnp.broadcast_to(m + jnp.log(den), (HS * BLK, SLAB))))

    out = _sds(shape(AW))
    nsg = N_SLAB // sps
    (o, l), rode = _call_with_ride(
        body, ride, lambda: (pl.program_id(0) == 0) & (pl.program_id(1) == 0),
        lambda: (pl.program_id(0) == nsg - 1) & (pl.program_id(1) == nbq - 1),
        name=f"attn_fwd{g}", grid=(nsg, nbq),
        in_specs=sum(in_specs, []) + [pl.BlockSpec((sps * HS, BLK, 2 * BLK), lambda sg, n: (g * nsg + sg, 0, 0))],
        out_specs=[ospec(cur), ospec(cur)],
        out_shape=[out, out], compiler_params=_params(_slab_axis(ride), "arbitrary"),
    )(*([qkv_g] * (5 * nl)), bias_tab)
    return o.reshape(S, AW), l.reshape(S, AW), rode


def _attn_bwd(qkv_g, dattn, stats, bias_tab, g, ride):
    S = qkv_g.shape[0]
    subs, nbq, sps, ispecs, shape, ospec = _attn_views(GROUPS[g][1], S)
    cur = lambda n: jnp.minimum(n, nbq - 1)
    prev = lambda n: jnp.clip(n - 1, 0, nbq - 1)
    late = lambda n: jnp.maximum(n - 1, 0)
    in_specs = [ispecs(0, SLAB, cur), ispecs(AW, SLAB, prev), ispecs(AW, SLAB, cur), ispecs(2 * AW, SLAB, prev),
                ispecs(2 * AW, SLAB, cur), ispecs(0, SLAB, cur), ispecs(0, STAT_W, cur)]
    nl = len(in_specs[0])

    def body(*refs):
        q, kp, kc, vp, vc, da = (refs[t * nl:(t + 1) * nl] for t in range(6))
        st_ref, b_ref, dq_ref, dk_ref, dv_ref, ds_ref, ck_ref, cv_ref = refs[6 * nl:]
        n = pl.program_id(1)

        @pl.when(n == 0)
        def _():
            ds_ref[...] = jnp.zeros_like(ds_ref)
            ck_ref[...] = jnp.zeros_like(ck_ref)
            cv_ref[...] = jnp.zeros_like(cv_ref)

        @pl.when(n < nbq)
        def _():
            col = lax.broadcasted_iota(jnp.int32, (HS * BLK, 2 * BLK), 1)
            keep = (col >= BLK) | (n > 0)
            for s_ in range(sps):
                cs = slice(s_ * SLAB, (s_ + 1) * SLAB)
                bias = b_ref[pl.ds(s_ * HS, HS)].reshape(HS * BLK, 2 * BLK)
                for i, bs in enumerate(subs):
                    st = _ld((st_ref,), bs, s_, STAT_W)
                    kb = jnp.concatenate([_ld(kp, bs, s_, SLAB), _ld(kc, bs, s_, SLAB)], axis=0).astype(BF16)
                    vb = jnp.concatenate([_ld(vp, bs, s_, SLAB), _ld(vc, bs, s_, SLAB)], axis=0).astype(BF16)
                    lse = jnp.concatenate([st[:, h:h + 1] for h in range(HS)], axis=0)
                    delta = jnp.concatenate([st[:, HS + h:HS + h + 1] for h in range(HS)], axis=0)
                    qs = _head_stack(_ld(q, bs, s_, SLAB).astype(BF16))
                    dos = _head_stack(_ld(da, bs, s_, SLAB).astype(BF16))
                    s = _dot_nt(qs, kb) * (HD ** -0.5) + bias
                    s = jnp.where(keep, s, NEG)
                    p = jnp.exp(s - lse)
                    ds = p * (_dot_nt(dos, vb) - delta)
                    ds_ref[pl.ds(s_ * HS, HS)] += ds.reshape(HS, BLK, 2 * BLK)
                    ds_b = (ds * (HD ** -0.5)).astype(BF16)
                    _st(dq_ref, bs, s_, _head_unstack(_dot(ds_b, kb)))
                    dkb = _dot_tn(ds_b, qs)
                    dvb = _dot_tn(p.astype(BF16), dos)
                    _st(dk_ref, bs, s_, ck_ref[i, :, cs] + dkb[:BLK])
                    _st(dv_ref, bs, s_, cv_ref[i, :, cs] + dvb[:BLK])
                    ck_ref[i, :, cs] = dkb[BLK:]
                    cv_ref[i, :, cs] = dvb[BLK:]

        @pl.when(n == nbq)
        def _():
            for s_ in range(sps):
                for i, bs in enumerate(subs):
                    _st(dk_ref, bs, s_, ck_ref[i, :, s_ * SLAB:(s_ + 1) * SLAB])
                    _st(dv_ref, bs, s_, cv_ref[i, :, s_ * SLAB:(s_ + 1) * SLAB])

    out = _sds(shape(AW))
    nsg = N_SLAB // sps
    (dq, dk, dv, ds_acc), rode = _call_with_ride(
        body, ride, lambda: (pl.program_id(0) == 0) & (pl.program_id(1) == 0),
        lambda: (pl.program_id(0) == nsg - 1) & (pl.program_id(1) == nbq),
        name=f"attn_bwd{g}", grid=(nsg, nbq + 1),
        in_specs=sum(in_specs, []) + [pl.BlockSpec((sps * HS, BLK, 2 * BLK), lambda sg, n: (g * nsg + sg, 0, 0))],
        out_specs=[ospec(cur), ospec(late), ospec(late),
                   pl.BlockSpec((sps * HS, BLK, 2 * BLK), lambda sg, n: (sg, 0, 0))],
        out_shape=[out] * 3 + [_sds((NH, BLK, 2 * BLK))],
        scratch_shapes=[pltpu.VMEM((len(subs), BLK, sps * SLAB), F32), pltpu.VMEM((len(subs), BLK, sps * SLAB), F32)],
        compiler_params=_params(_slab_axis(ride), "arbitrary"),
    )(*([qkv_g] * (5 * nl)), *([dattn] * nl), stats, bias_tab)
    return [dq.reshape(S, AW), dk.reshape(S, AW), dv.reshape(S, AW)], ds_acc, rode


TM_MIX = 256


def _mix_specs(tm):
    row512 = pl.BlockSpec((tm, AW), lambda i: (i, 0))
    return ([row512] * 6 + [
        pl.BlockSpec((tm, REST_W), lambda i: (i, 0)),
        pl.BlockSpec((HALO, AW), lambda i: (jnp.maximum(i * (tm // HALO) - 1, 0), 1)),
        pl.BlockSpec((AW, D), lambda i: (0, 0)), pl.BlockSpec((AW, D), lambda i: (0, 0)),
        pl.BlockSpec((4, PGW, PGW), lambda i: (0, 0, 0)), pl.BlockSpec((1, AW), lambda i: (0, 0))])


def _mix_forward(i, tm, o_refs, l_refs, rest_ref, halo_ref, wab_ref, wpb_ref, pw_ref, ps_ref):
    l0, l1, l2 = (r[...] for r in l_refs)
    mx = jnp.maximum(jnp.maximum(l0, l1), l2)
    e0, e1, e2 = jnp.exp(l0 - mx), jnp.exp(l1 - mx), jnp.exp(l2 - mx)
    den = e0 + e1 + e2
    lj = mx + jnp.log(den)
    attn = (e0 * o_refs[0][...] + e1 * o_refs[1][...] + e2 * o_refs[2][...]) / den

    z_attn = rest_ref[:, 0:AW]
    u = rest_ref[:, AW:2 * AW]
    z_pool = rest_ref[:, 2 * AW:3 * AW]
    g_attn = rest_ref[:, 3 * AW:3 * AW + D]
    g_pool = rest_ref[:, 3 * AW + D:3 * AW + 2 * D]

    sg_a = _sigmoid(z_attn)
    sil_a = z_attn * sg_a
    a_g = (attn * sil_a).astype(BF16)
    y_attn = _dot(a_g, wab_ref[...])

    halo = jnp.where(i > 0, halo_ref[...], 0.0)
    ext = jnp.concatenate([halo, u], axis=0)
    t = i * tm + lax.broadcasted_iota(jnp.int32, (tm, 1), 0)
    pooled, mixed_raw = [], []
    for gi, win in enumerate(POOL_WINDOWS):
        s = ext[:, gi * PGW:(gi + 1) * PGW]
        sh = 1
        while sh < win:
            s = s + pltpu.roll(s, sh, 0)
            sh *= 2
        cnt = jnp.minimum(t + 1, win).astype(F32)
        pg = s[HALO:] / cnt - u[:, gi * PGW:(gi + 1) * PGW]
        pooled.append(pg.astype(BF16))
        mixed_raw.append(_dot(pooled[-1], pw_ref[gi].astype(BF16)))
    mixed_raw = jnp.concatenate(mixed_raw, axis=1)
    mixed = mixed_raw * ps_ref[...]
    sg_p = _sigmoid(z_pool)
    sil_p = z_pool * sg_p
    m_g = (mixed * sil_p).astype(BF16)
    y_pool = _dot(m_g, wpb_ref[...])

    sa = _sigmoid(g_attn)
    sp = _sigmoid(g_pool)
    merged = sa * y_attn + sp * y_pool
    return dict(lj=lj, attn=attn, z_attn=z_attn, z_pool=z_pool, sg_a=sg_a, sil_a=sil_a, a_g=a_g, y_attn=y_attn,
                pooled=pooled, mixed_raw=mixed_raw, mixed=mixed, sg_p=sg_p, sil_p=sil_p, m_g=m_g, y_pool=y_pool,
                sa=sa, sp=sp, merged=merged)


def _tail(x, target, os_, ls_, rest, wab, wpb, pool_w, pool_scale, wout, mod, final_g):
    S = x.shape[0]
    tm = TM_MIX

    def body(o0, o1, o2, l0, l1, l2, rest_ref, halo_ref, wab_ref, wpb_ref, pw_ref, ps_ref,
             x_ref, t_ref, wo_ref, mod_ref, fg_ref, dx2_ref, dmo_ref, loss_ref, dfg_ref, dgate_ref):
        i = pl.program_id(0)

        @pl.when(i == 0)
        def _():
            loss_ref[...] = jnp.zeros_like(loss_ref)
            dfg_ref[...] = jnp.zeros_like(dfg_ref)
            dgate_ref[...] = jnp.zeros_like(dgate_ref)

        f = _mix_forward(i, tm, (o0, o1, o2), (l0, l1, l2), rest_ref, halo_ref, wab_ref, wpb_ref, pw_ref, ps_ref)
        mo = _dot(f["merged"].astype(BF16), wo_ref[...])
        gate = mod_ref[:, 2 * D:3 * D]
        fg = fg_ref[...]
        x2 = x_ref[...] + gate * mo
        r2 = lax.rsqrt(jnp.mean(x2 * x2, axis=-1, keepdims=True) + EPS)
        n2 = x2 * r2
        err = n2 * fg - t_ref[...]
        loss_ref[...] += 0.5 * jnp.sum(jnp.mean(err * err, axis=-1, keepdims=True))
        dy = err * (1.0 / D)
        dfg_ref[...] += jnp.sum(dy * n2, axis=0, keepdims=True)
        dn = dy * fg
        dx2 = r2 * (dn - n2 * jnp.mean(dn * n2, axis=-1, keepdims=True))
        dgate_ref[...] += jnp.sum(dx2 * mo, axis=0, keepdims=True)
        dx2_ref[...] = dx2
        dmo_ref[...] = (dx2 * gate).astype(BF16)

    row = pl.BlockSpec((tm, D), lambda i: (i, 0))
    vec = pl.BlockSpec((1, D), lambda i: (0, 0))
    return pl.pallas_call(
        body, name="tail", grid=(S // tm,),
        in_specs=_mix_specs(tm) + [row, row, pl.BlockSpec((D, D), lambda i: (0, 0)),
                                   pl.BlockSpec((1, 3 * D), lambda i: (0, 0)), vec],
        out_specs=[row, row, pl.BlockSpec((8, 128), lambda i: (0, 0)), vec, vec],
        out_shape=[_sds((S, D)), _sds((S, D), BF16), _sds((8, 128)), _sds((1, D)), _sds((1, D))],
        compiler_params=_params("arbitrary"),
    )(*os_, *ls_, rest, rest, wab, wpb, pool_w, pool_scale, x, target, wout, mod, final_g)


def _mix_bwd(dmo, os_, ls_, rest, wab, wpb, pool_w, pool_scale, wout):
    S = dmo.shape[0]
    tm = TM_MIX
    nt = S // tm
    sw = D // N_SHARD

    def body(o0, o1, o2, l0, l1, l2, rest_ref, halo_ref, wab_ref, wpb_ref, pw_ref, ps_ref, dmo_ref, wo_ref,
             dattn_ref, stats_ref, dpooled_ref, drest_ref, dwo_hbm, dwab_hbm, dwpb_hbm, dpw_ref, dps_ref,
             awo, awab, awpb):
        i = pl.program_id(0)

        @pl.when(i == 0)
        def _():
            awo[...] = jnp.zeros_like(awo)
            awab[...] = jnp.zeros_like(awab)
            awpb[...] = jnp.zeros_like(awpb)
            dpw_ref[...] = jnp.zeros_like(dpw_ref)
            dps_ref[...] = jnp.zeros_like(dps_ref)

        f = _mix_forward(i, tm, (o0, o1, o2), (l0, l1, l2), rest_ref, halo_ref, wab_ref, wpb_ref, pw_ref, ps_ref)
        dmo_b = dmo_ref[...]
        dmerged = _dot_nt(dmo_b, wo_ref[...])
        awo[...] += _dot_tn(f["merged"].astype(BF16), dmo_b)
        sa, sp = f["sa"], f["sp"]
        dya = (dmerged * sa).astype(BF16)
        dyp = (dmerged * sp).astype(BF16)
        dg_attn = dmerged * f["y_attn"] * sa * (1.0 - sa)
        dg_pool = dmerged * f["y_pool"] * sp * (1.0 - sp)
        dag = _dot_nt(dya, wab_ref[...])
        awab[...] += _dot_tn(f["a_g"], dya)
        dmg = _dot_nt(dyp, wpb_ref[...])
        awpb[...] += _dot_tn(f["m_g"], dyp)
        dattn = dag * f["sil_a"]
        dattn_ref[...] = dattn
        prod = dattn * f["attn"]
        lane = lax.broadcasted_iota(jnp.int32, (tm, STAT_W), 1)
        for sb in range(N_SLAB):
            st = jnp.zeros((tm, STAT_W), F32)
            for h in range(HS):
                hs = slice((sb * HS + h) * HD, (sb * HS + h + 1) * HD)
                st = jnp.where(lane == h, f["lj"][:, hs.start:hs.start + 1], st)
                st = jnp.where(lane == HS + h, jnp.sum(prod[:, hs], axis=-1, keepdims=True), st)
            stats_ref[:, sb * STAT_W:(sb + 1) * STAT_W] = st
        dz_attn = dag * f["attn"] * (f["sg_a"] * (1.0 + f["z_attn"] * (1.0 - f["sg_a"])))
        dmixed = dmg * f["sil_p"]
        dz_pool = dmg * f["mixed"] * (f["sg_p"] * (1.0 + f["z_pool"] * (1.0 - f["sg_p"])))
        dps_ref[...] += jnp.sum(dmixed * f["mixed_raw"], axis=0, keepdims=True)
        dpm = (dmixed * ps_ref[...]).astype(BF16)
        for gi in range(len(POOL_WINDOWS)):
            cs = slice(gi * PGW, (gi + 1) * PGW)
            dpw_ref[gi] += _dot_tn(f["pooled"][gi], dpm[:, cs])
            dpooled_ref[:, cs] = _dot_nt(dpm[:, cs], pw_ref[gi].astype(BF16))
        drest_ref[:, 0:AW] = dz_attn.astype(BF16)
        drest_ref[:, AW:2 * AW] = jnp.zeros((tm, AW), BF16)
        drest_ref[:, 2 * AW:3 * AW] = dz_pool.astype(BF16)
        drest_ref[:, 3 * AW:3 * AW + D] = dg_attn.astype(BF16)
        drest_ref[:, 3 * AW + D:3 * AW + 2 * D] = dg_pool.astype(BF16)

        @pl.when(i == nt - 1)
        def _():
            pltpu.sync_copy(awo, dwo_hbm)
            for k in range(N_SHARD):
                pltpu.sync_copy(awab.at[:, pl.ds(k * sw, sw)], dwab_hbm.at[k])
                pltpu.sync_copy(awpb.at[:, pl.ds(k * sw, sw)], dwpb_hbm.at[k])

    row512 = pl.BlockSpec((tm, AW), lambda i: (i, 0))
    outs = pl.pallas_call(
        body, name="mix_bwd", grid=(nt,),
        in_specs=_mix_specs(tm) + [pl.BlockSpec((tm, D), lambda i: (i, 0)), pl.BlockSpec((D, D), lambda i: (0, 0))],
        out_specs=[row512, pl.BlockSpec((tm, N_SLAB * STAT_W), lambda i: (i, 0)), row512,
                   pl.BlockSpec((tm, REST_W), lambda i: (i, 0)), ANY, ANY, ANY,
                   pl.BlockSpec((4, PGW, PGW), lambda i: (0, 0, 0)), pl.BlockSpec((1, AW), lambda i: (0, 0))],
        out_shape=[_sds((S, AW)), _sds((S, N_SLAB * STAT_W)), _sds((S, AW)), _sds((S, REST_W), BF16),
                   _sds((D, D)), _sds((N_SHARD, AW, sw)), _sds((N_SHARD, AW, sw)), _sds((4, PGW, PGW)), _sds((1, AW))],
        scratch_shapes=[pltpu.VMEM((D, D), F32), pltpu.VMEM((AW, D), F32), pltpu.VMEM((AW, D), F32)],
        compiler_params=_params("arbitrary"),
    )(*os_, *ls_, rest, rest, wab, wpb, pool_w, pool_scale, dmo, wout)
    dattn, stats, dpooled, drest, dwo, dwab, dwpb, dpw, dps = outs
    return dattn, stats, dpooled, drest, dwo.reshape(N_SHARD, D // N_SHARD, D), dwab, dwpb, dpw, dps


def _pool_bwd(dpooled):
    S = dpooled.shape[0]
    tm = 512
    nt = S // tm

    def body(dp_ref, nxt_ref, du_ref):
        i = pl.program_id(0)
        t = i * tm + lax.broadcasted_iota(jnp.int32, (tm + HALO, 1), 0)
        nxt = jnp.where(i < nt - 1, nxt_ref[...], 0.0)
        ext = jnp.concatenate([dp_ref[...], nxt], axis=0)
        for gi, win in enumerate(POOL_WINDOWS):
            cs = slice(gi * PGW, (gi + 1) * PGW)
            s = ext[:, cs] / jnp.minimum(t + 1, win).astype(F32)
            sh = 1
            while sh < win:
                s = s + pltpu.roll(s, tm + HALO - sh, 0)
                sh *= 2
            du_ref[:, cs] = (s[:tm] - dp_ref[:, cs]).astype(BF16)

    return pl.pallas_call(
        body, name="pool_bwd", grid=(nt,),
        in_specs=[pl.BlockSpec((tm, AW), lambda i: (i, 0)),
                  pl.BlockSpec((HALO, AW), lambda i: (jnp.minimum((i + 1) * (tm // HALO), S // HALO - 1), 0))],
        out_specs=pl.BlockSpec((tm, AW), lambda i: (i, 0)),
        out_shape=_sds((S, AW), BF16), compiler_params=_params("parallel"),
    )(dpooled, dpooled)


TB = 1024


def _dh(dproj, wg_in, ride):
    S = dproj.shape[0]
    per = wg_in.shape[2] // TB
    nm, nk = S // TB, IN_W // TB

    def body(dp_ref, w_ref, out_ref):
        @pl.when(pl.program_id(1) == 0)
        def _():
            out_ref[...] = jnp.zeros_like(out_ref)

        out_ref[...] += _dot_nt(dp_ref[...], w_ref[...])

    (dh,), rode = _call_with_ride(
        body, ride, lambda: (pl.program_id(0) == 0) & (pl.program_id(1) == 0),
        lambda: (pl.program_id(0) == nm - 1) & (pl.program_id(1) == nk - 1),
        name="dh", grid=(nm, nk),
        in_specs=[pl.BlockSpec((TB, TB), lambda m, kk: (m, kk)),
                  pl.BlockSpec((None, D, TB), lambda m, kk: (kk // per, 0, kk % per))],
        out_specs=[pl.BlockSpec((TB, D), lambda m, kk: (m, 0))],
        out_shape=[_sds((S, D))], compiler_params=_params("arbitrary", "arbitrary"),
    )(dproj, wg_in)
    return dh, rode


def _dw_in(h, dproj):
    S = dproj.shape[0]
    per = IN_W // N_SHARD // TB

    def body(h_ref, dp_ref, out_ref):
        @pl.when(pl.program_id(1) == 0)
        def _():
            out_ref[...] = jnp.zeros_like(out_ref)

        out_ref[...] += _dot_tn(h_ref[...], dp_ref[...])

    return pl.pallas_call(
        body, name="dw_in", grid=(IN_W // TB, S // TB),
        in_specs=[pl.BlockSpec((TB, D), lambda j, kk: (kk, 0)), pl.BlockSpec((TB, TB), lambda j, kk: (kk, j))],
        out_specs=pl.BlockSpec((None, D, TB), lambda j, kk: (j // per, 0, j % per)),
        out_shape=_sds((N_SHARD, D, IN_W // N_SHARD)), compiler_params=_params("parallel", "arbitrary"),
    )(h, dproj)


def _prenorm_bwd(x, dh, dx2, norm_g, mod):
    S = x.shape[0]
    tm = 512

    def body(x_ref, dh_ref, dx2_ref, g_ref, mod_ref, gx_ref, dg_ref, dshift_ref, dscale_ref):
        i = pl.program_id(0)

        @pl.when(i == 0)
        def _():
            dg_ref[...] = jnp.zeros_like(dg_ref)
            dshift_ref[...] = jnp.zeros_like(dshift_ref)
            dscale_ref[...] = jnp.zeros_like(dscale_ref)

        xv = x_ref[...]
        dhv = dh_ref[...]
        g = g_ref[...]
        r = lax.rsqrt(jnp.mean(xv * xv, axis=-1, keepdims=True) + EPS)
        xh = xv * r
        dshift_ref[...] += jnp.sum(dhv, axis=0, keepdims=True)
        dscale_ref[...] += jnp.sum(dhv * (xh * g), axis=0, keepdims=True)
        dn1 = dhv * (1.0 + mod_ref[:, D:2 * D])
        dg_ref[...] += jnp.sum(dn1 * xh, axis=0, keepdims=True)
        dxh = dn1 * g
        gx_ref[...] = dx2_ref[...] + r * (dxh - xh * jnp.mean(dxh * xh, axis=-1, keepdims=True))

    row = pl.BlockSpec((tm, D), lambda i: (i, 0))
    vec = pl.BlockSpec((1, D), lambda i: (0, 0))
    return pl.pallas_call(
        body, name="prenorm_bwd", grid=(S // tm,),
        in_specs=[row, row, row, vec, pl.BlockSpec((1, 3 * D), lambda i: (0, 0))],
        out_specs=[row, vec, vec, vec],
        out_shape=[_sds((S, D)), _sds((1, D)), _sds((1, D)), _sds((1, D))],
        compiler_params=_params("arbitrary"),
    )(x, dh, dx2, norm_g, mod)


def _local_step(x, target, mod, wg_in, own_late, pool_w, pool_scale, rel_bias, norm_g, final_g, half_idx, chip_half):
    buckets = jnp.asarray(_bucket_tables())
    bias_tab = _bias_table(rel_bias, buckets)
    h = _prenorm(x, norm_g, mod)
    qkv = [_proj(h, wg_in, 3 * g, 3, F32, f"proj_qkv{g}") for g in range(NG)]
    rest = _proj(h, wg_in, NCB_QKV, REST_W // CB, F32, "proj_rest")
    o0, l0, late = _attn_fwd(qkv[0], bias_tab, 0, _ride_gather_send(own_late))
    o1, l1, late = _attn_fwd(qkv[1], bias_tab, 1, _ride_gather_forward([_place_own(b, o) for b, o in zip(late, own_late)]))
    o2, l2, _ = _attn_fwd(qkv[2], bias_tab, 2, None)
    os_, ls_ = (o0, o1, o2), (l0, l1, l2)
    wab = late[0].reshape(N_SHARD, AW, D // N_SHARD).transpose(1, 0, 2).reshape(AW, D)
    wpb = late[1].reshape(N_SHARD, AW, D // N_SHARD).transpose(1, 0, 2).reshape(AW, D)
    wout = late[2].reshape(D, D)
    dx2, dmo, loss, dfinal_g, dgate = _tail(x, target, os_, ls_, rest, wab, wpb, pool_w, pool_scale, wout, mod, final_g)
    dattn, stats, dpooled, drest, dw_out, dw_ab, dw_pb, dpool_w, dpool_scale = _mix_bwd(
        dmo, os_, ls_, rest, wab, wpb, pool_w, pool_scale, wout)
    du = _pool_bwd(dpooled)

    small = [dw_ab, dw_pb, dw_out]
    dqkv0, ds0, sib_small = _attn_bwd(qkv[0], dattn, stats, bias_tab, 0, _ride_sibling_halves(small))
    p_small = [_pair_sum(g, t, half_idx, f"rs_pair_sum{a}") for a, (g, t) in enumerate(zip(small, sib_small))]
    dqkv1, ds1, u_small = _attn_bwd(qkv[1], dattn, stats, bias_tab, 1,
                                    _ride_chip_exchange([p16 for _, p16 in p_small]))
    rs_ab, rs_pb, rs_out = [_chip_sum(p32, u, chip_half, f"rs_chip_sum{a}")
                            for a, ((p32, _), u) in enumerate(zip(p_small, u_small))]
    dqkv2, ds2, _ = _attn_bwd(qkv[2], dattn, stats, bias_tab, 2, None)

    dproj = jnp.concatenate([a.astype(BF16) for a in dqkv0 + dqkv1 + dqkv2] + [drest[:, :AW], du, drest[:, 2 * AW:]],
                            axis=1)
    dw_in = _dw_in(h, dproj)
    drel_rows, (sib_in,) = _bias_grad(jnp.concatenate([ds0, ds1, ds2], axis=0), buckets,
                                      _ride_sibling_halves([dw_in]))
    drel = drel_rows[:, 0, :NUM_BUCKETS].T
    p32_in, p16_in = _pair_sum(dw_in, sib_in, half_idx, "rs_pair_sum_in")
    dh, (u_in,) = _dh(dproj, wg_in, _ride_chip_exchange([p16_in]))
    rs_in = _chip_sum(p32_in, u_in, chip_half, "rs_chip_sum_in")

    grad_x, dnorm_g, dshift, dscale = _prenorm_bwd(x, dh, dx2, norm_g, mod)
    dmod = jnp.concatenate([dshift, dscale, dgate], axis=1)
    return dict(loss=loss[0, 0], grad_x=grad_x, dmod=dmod, dnorm_g=dnorm_g, dfinal_g=dfinal_g, dpool_w=dpool_w,
                dpool_scale=dpool_scale, drel_bias=drel, dw_in=dw_in, dw_attn_br=dw_ab, dw_pool_br=dw_pb,
                dw_out=dw_out, rs_in=rs_in, rs_attn_br=rs_ab, rs_pool_br=rs_pb, rs_out=rs_out)


def _allgather8(blocks, name, relay=None):
    nb = len(blocks)
    relay = [False] * nb if relay is None else list(relay)

    def body(*refs):
        ins, outs = refs[:nb], refs[nb:2 * nb]
        send_sems, recv_sems = refs[2 * nb:]
        x, y, c = lax.axis_index("x"), lax.axis_index("y"), lax.axis_index("c")
        me, sibling = (x, y, c), (x, y, 1 - c)
        here, xn, yn, dg = (x, y), (1 - x, y), (x, 1 - y), (1 - x, 1 - y)

        def slot(a, chip, core, half=None):
            ref = outs[a].at[4 * chip[0] + 2 * chip[1] + core]
            if half is None:
                return ref
            r2 = ref.shape[0] // 2
            return ref.at[pl.ds(half * r2, r2)]

        def copy(a, k, dst, to, src=None):
            return pltpu.make_async_remote_copy(src_ref=dst if src is None else src, dst_ref=dst,
                                                send_sem=send_sems.at[a, k], recv_sem=recv_sems.at[a, k],
                                                device_id=to, device_id_type=MESH)

        def start(cps):
            for cp in cps:
                cp.start()
            return cps

        sent = []
        for a in range(nb):
            own = slot(a, here, c)
            sent += [copy(a, 0, own, sibling, src=ins[a]), copy(a, 1, own, (*xn, c), src=ins[a]),
                     copy(a, 2, own, (*yn, c), src=ins[a])]
            if not relay[a]:
                sent.append(copy(a, 3, own, (*dg, c), src=ins[a]))
        start(sent)
        for a in range(nb):
            copy(a, 2, slot(a, yn, c), me).wait_recv()
            sent += start([copy(a, 6, slot(a, yn, c), sibling)]
                          + ([copy(a, 3, slot(a, yn, c, 0), (*xn, c))] if relay[a] else []))
        for a in range(nb):
            copy(a, 1, slot(a, xn, c), me).wait_recv()
            sent += start([copy(a, 5, slot(a, xn, c), sibling)]
                          + ([copy(a, 4, slot(a, xn, c, 1), (*yn, c))] if relay[a] else []))
        for a in range(nb):
            for k, half in ((3, 0), (4, 1)) if relay[a] else ((3, None),):
                copy(a, k, slot(a, dg, c, half), me).wait_recv()
                sent += start([copy(a, 4 + k, slot(a, dg, c, half), sibling)])
        for a in range(nb):
            copy(a, 0, slot(a, here, 1 - c), me).wait_recv()
            copy(a, 5, slot(a, xn, 1 - c), me).wait_recv()
            copy(a, 6, slot(a, yn, 1 - c), me).wait_recv()
            for k, half in ((7, 0), (8, 1)) if relay[a] else ((7, None),):
                copy(a, k, slot(a, dg, 1 - c, half), me).wait_recv()
        for cp in sent:
            cp.wait_send()

    outs = pl.pallas_call(
        body, name=name, in_specs=[ANY] * nb, out_specs=[ANY] * nb,
        out_shape=[_sds((8,) + b.shape, b.dtype) for b in blocks],
        scratch_shapes=[_dma_sems(nb, 9), _dma_sems(nb, 9)],
    )(*blocks)
    return [_place_own(buf, b) for buf, b in zip(outs, blocks)]


def _place_own(buf, block):
    dev = 4 * lax.axis_index("x") + 2 * lax.axis_index("y") + lax.axis_index("c")
    return lax.dynamic_update_index_in_dim(buf, block, dev, 0)


def _ride_gather_send(blocks):
    def copies(ins, outs, send_sems, recv_sems):
        x, y, c = lax.axis_index("x"), lax.axis_index("y"), lax.axis_index("c")
        cps = []
        for a in range(len(blocks)):
            own = outs[a].at[4 * x + 2 * y + c]
            for k, to in enumerate([(x, y, 1 - c), (1 - x, y, c), (x, 1 - y, c), (1 - x, 1 - y, c)]):
                cps.append(pltpu.make_async_remote_copy(src_ref=ins[a], dst_ref=own, send_sem=send_sems.at[4 * a + k],
                                                        recv_sem=recv_sems.at[4 * a + k], device_id=to,
                                                        device_id_type=MESH))
        return cps

    return _Ride(blocks, [_sds((8,) + b.shape, b.dtype) for b in blocks], 4 * len(blocks), copies)


def _ride_gather_forward(bufs):
    def copies(ins, outs, send_sems, recv_sems):
        x, y, c = lax.axis_index("x"), lax.axis_index("y"), lax.axis_index("c")
        cps = []
        for a in range(len(bufs)):
            for j, (ox, oy) in enumerate([(1 - x, y), (x, 1 - y), (1 - x, 1 - y)]):
                blk = outs[a].at[4 * ox + 2 * oy + c]
                cps.append(pltpu.make_async_remote_copy(src_ref=blk, dst_ref=blk, send_sem=send_sems.at[3 * a + j],
                                                        recv_sem=recv_sems.at[3 * a + j], device_id=(x, y, 1 - c),
                                                        device_id_type=MESH))
        return cps

    return _Ride(bufs, [_sds(b.shape, b.dtype) for b in bufs], 3 * len(bufs), copies, in_place=True)


def _ride_sibling_halves(gs):
    def copies(ins, outs, send_sems, recv_sems):
        x, y, c = lax.axis_index("x"), lax.axis_index("y"), lax.axis_index("c")
        cps = []
        for a in range(len(gs)):
            r2 = ins[a].shape[1] // 2
            other = ins[a].at[:, pl.ds((1 - c) * r2, r2), :]
            cps.append(pltpu.make_async_remote_copy(src_ref=other, dst_ref=outs[a], send_sem=send_sems.at[a],
                                                    recv_sem=recv_sems.at[a], device_id=(x, y, 1 - c),
                                                    device_id_type=MESH))
        return cps

    return _Ride(gs, [_sds((g.shape[0], g.shape[1] // 2, g.shape[2]), g.dtype) for g in gs], len(gs), copies)


def _pair_sum(g, t, half, name):
    nsh, rows, cols = g.shape
    r2 = rows // 2
    tr = _row_tile(r2, cols)
    nt = r2 // tr

    def body(half_ref, g_ref, t_ref, p32_ref, p16_ref):
        p = g_ref[...] + t_ref[...]
        p32_ref[...] = p
        p16_ref[...] = p.astype(BF16)

    blk = pl.BlockSpec((None, tr, cols), lambda k, i, half_ref: (k, i, 0))
    return pl.pallas_call(
        body, name=name,
        grid_spec=pltpu.PrefetchScalarGridSpec(
            num_scalar_prefetch=1, grid=(nsh, nt),
            in_specs=[pl.BlockSpec((None, tr, cols), lambda k, i, half_ref: (k, half_ref[0] * nt + i, 0)), blk],
            out_specs=[blk, blk]),
        out_shape=[_sds((nsh, r2, cols)), _sds((nsh, r2, cols), BF16)],
        compiler_params=_params("parallel", "parallel"),
    )(half, g, t)


def _ride_chip_exchange(ps):
    def copies(ins, outs, send_sems, recv_sems):
        x, y, c = lax.axis_index("x"), lax.axis_index("y"), lax.axis_index("c")
        chips = [(1 - x, y), (x, 1 - y), (1 - x, 1 - y)]
        cps = []
        for a in range(len(ps)):
            for j, (ox, oy) in enumerate(chips):
                cps.append(pltpu.make_async_remote_copy(src_ref=ins[a].at[2 * ox + oy], dst_ref=outs[a].at[j],
                                                        send_sem=send_sems.at[3 * a + j],
                                                        recv_sem=recv_sems.at[3 * a + j],
                                                        device_id=(ox, oy, c), device_id_type=MESH))
        return cps

    return _Ride(ps, [_sds((3,) + p.shape[1:], p.dtype) for p in ps], 3 * len(ps), copies)


def _chip_sum(p32, u, chip_half, name):
    r2, cols = p32.shape[1:]
    tr = _row_tile(r2, cols)
    nt = r2 // tr

    def body(ch_ref, p_ref, u_ref, o_ref):
        acc = p_ref[...]
        for j in range(3):
            acc = acc + u_ref[j].astype(F32)
        o_ref[...] = acc

    return pl.pallas_call(
        body, name=name,
        grid_spec=pltpu.PrefetchScalarGridSpec(
            num_scalar_prefetch=1, grid=(nt,),
            in_specs=[pl.BlockSpec((None, tr, cols), lambda i, ch_ref: (ch_ref[0], i, 0)),
                      pl.BlockSpec((3, tr, cols), lambda i, ch_ref: (0, i, 0))],
            out_specs=pl.BlockSpec((tr, cols), lambda i, ch_ref: (ch_ref[1] * nt + i, 0))),
        out_shape=_sds((2 * r2, cols)), compiler_params=_params("parallel"),
    )(chip_half, p32, u)


def _sibling_join(fs, name):
    nb = len(fs)

    def body(*refs):
        outs = refs[nb:2 * nb]
        send_sems, recv_sems = refs[2 * nb:]
        x, y, c = lax.axis_index("x"), lax.axis_index("y"), lax.axis_index("c")
        cps = []
        for a in range(nb):
            r2 = outs[a].shape[0] // 2
            rows = outs[a].at[pl.ds(c * r2, r2), :]
            cps.append(pltpu.make_async_remote_copy(src_ref=rows, dst_ref=rows, send_sem=send_sems.at[a],
                                                    recv_sem=recv_sems.at[a], device_id=(x, y, 1 - c),
                                                    device_id_type=MESH))
        for cp in cps:
            cp.start()
        for cp in cps:
            cp.wait()

    return pl.pallas_call(
        body, name=name, in_specs=[ANY] * nb, out_specs=[ANY] * nb,
        out_shape=[_sds(f.shape, f.dtype) for f in fs],
        input_output_aliases={a: a for a in range(nb)},
        scratch_shapes=[_dma_sems(nb), _dma_sems(nb)],
    )(*fs)


def _row_tile(rows, cols):
    tile = rows
    while tile * cols * 4 > (1 << 20) and tile % 16 == 0:
        tile //= 2
    return tile


def _w_ada_grad(c_all, dmod_cols):
    def body(c_ref, d_ref, o_ref):
        o_ref[...] = _dot_tn(c_ref[...].astype(BF16), d_ref[...].astype(BF16))

    return pl.pallas_call(body, name="w_ada_grad", out_shape=_sds((c_all.shape[1], dmod_cols.shape[1])),
                          compiler_params=_params())(c_all, dmod_cols)


def _adam_math(w, g, m, v):
    nm = ADAM_B1 * m + (1.0 - ADAM_B1) * g
    nv = ADAM_B2 * v + (1.0 - ADAM_B2) * (g * g)
    m_hat = nm / (1.0 - ADAM_B1 ** ADAM_STEP)
    v_hat = nv / (1.0 - ADAM_B2 ** ADAM_STEP)
    return -ADAM_LR * (m_hat / (jnp.sqrt(v_hat) + ADAM_EPS) + ADAM_WD * w), nm, nv


def _adamw(w, g, m, v, name):
    rows, cols = w.shape
    tr = _row_tile(rows, cols)

    def body(w_ref, g_ref, m_ref, v_ref, go_ref, d_ref, nm_ref, nv_ref):
        gv = g_ref[...]
        go_ref[...] = gv
        d_ref[...], nm_ref[...], nv_ref[...] = _adam_math(w_ref[...], gv, m_ref[...], v_ref[...])

    spec = pl.BlockSpec((tr, cols), lambda i: (i, 0))
    return pl.pallas_call(
        body, name=name, grid=(rows // tr,), in_specs=[spec] * 4, out_specs=[spec] * 4,
        out_shape=[_sds((rows, cols))] * 4, compiler_params=_params("parallel"),
    )(w, g, m, v)


def _pack_small(dmod, dnorm_g, dfinal_g, dpool_scale, drel_bias, loss, dpool_w):
    return jnp.concatenate([dmod.reshape(-1, 128), dnorm_g.reshape(-1, 128), dfinal_g.reshape(-1, 128),
                            jnp.pad(dpool_scale.reshape(-1, 128), ((0, PK_RELB - PK_PSCALE - AW // 128), (0, 0))),
                            jnp.pad(drel_bias, ((0, 0), (0, 128 - NG * NH))),
                            jnp.full((PK_POOLW - PK_LOSS, 128), loss, F32), dpool_w.reshape(-1, 128)], axis=0)


def _small_update(small_all, ws, ms, vs):
    lane_rows = [(r0, r0 + w.shape[1] // 128) for r0, w in zip((PK_BADA, PK_NORMG, PK_FINALG, PK_PSCALE), ws)]
    nw = len(ws)

    def body(all_ref, *refs):
        w_refs, m_refs, v_refs = refs[:nw], refs[nw:2 * nw], refs[2 * nw:3 * nw]
        loss_ref, outs = refs[3 * nw], refs[3 * nw + 1:]
        g = all_ref[0]
        for s in range(1, all_ref.shape[0]):
            g = g + all_ref[s]
        loss_ref[...] = jnp.broadcast_to(g[PK_LOSS:PK_LOSS + 1, :], loss_ref.shape)

        def put(p, at, gv):
            d, nm, nv = _adam_math(w_refs[p][at], gv, m_refs[p][at], v_refs[p][at])
            for o_ref, val in zip(outs[4 * p:4 * p + 4], (gv, d, nm, nv)):
                o_ref[at] = val

        for p, (r0, r1) in enumerate(lane_rows):
            for i in range(r1 - r0):
                put(p, (slice(None), slice(128 * i, 128 * (i + 1))), g[r0 + i:r0 + i + 1, :])
        put(4, (slice(None), slice(None)), g[PK_RELB:PK_LOSS, 0:NG * NH])
        put(5, (slice(None), slice(None)), g[PK_POOLW:PK_ROWS, :])

    res = pl.pallas_call(
        body, name="small_update",
        out_shape=[_sds((8, 128))] + [_sds(w.shape) for w in ws for _ in range(4)], compiler_params=_params(),
    )(small_all, *ws, *ms, *vs)
    return res[0], [res[1 + 4 * p:5 + 4 * p] for p in range(nw)]


def kernel(x, c, norm_g, w_ada, b_ada, w_in, pool_w, pool_scale, w_attn_br, w_pool_br, w_out, rel_bias, final_g, loss_target, m_norm_g, m_w_ada, m_b_ada, m_w_in, m_pool_w, m_pool_scale, m_w_attn_br, m_w_pool_br, m_w_out, m_rel_bias, m_final_g, v_norm_g, v_w_ada, v_b_ada, v_w_in, v_pool_w, v_pool_scale, v_w_attn_br, v_w_pool_br, v_w_out, v_rel_bias, v_final_g):
    ix, iy, ic = lax.axis_index("x"), lax.axis_index("y"), lax.axis_index("c")
    dev = 4 * ix + 2 * iy + ic
    chip = 2 * ix + iy

    def half(w):
        r2 = w.shape[0] // 2
        return lax.dynamic_slice_in_dim(w, ic * r2, r2, axis=0).astype(BF16)

    gathered = _allgather8([jnp.broadcast_to(c, (8, D)), half(w_in[0])], "gather_weights", relay=[False, True])
    c_all = gathered[0][:, 0, :]
    wg_in = gathered[1].reshape(N_SHARD, D, IN_W // N_SHARD)

    mw = 3 * D // N_SHARD
    modp = _mod_partial(c_all, w_ada[0], lax.dynamic_slice_in_dim(b_ada, chip * mw, mw, axis=1))
    mod_all = _allgather8([modp], "gather_mod")[0]
    mod_full = mod_all[::2].transpose(1, 0, 2).reshape(8, 3 * D)
    mod = lax.dynamic_slice_in_dim(mod_full, dev, 1, axis=0)

    half_idx = jnp.stack([ic]).astype(jnp.int32)
    chip_half = jnp.stack([chip, ic]).astype(jnp.int32)
    r = _local_step(x[0], loss_target[0], mod, wg_in, [half(w_attn_br[0]), half(w_pool_br[0]), half(w_out[0])],
                    pool_w[0], pool_scale, rel_bias, norm_g, final_g.reshape(1, D), half_idx, chip_half)

    packed = _pack_small(r["dmod"], r["dnorm_g"], r["dfinal_g"], r["dpool_scale"], r["drel_bias"], r["loss"],
                         r["dpool_w"])
    small_all = _allgather8([packed], "gather_small")[0]
    small = ["b_ada", "norm_g", "final_g", "pool_scale", "rel_bias", "pool_w"]
    shaped = lambda b, n, f, ps, rb, pw: [b, n, f.reshape(1, D), ps, rb, pw.reshape(4 * PGW, PGW)]
    loss, small_out = _small_update(small_all, shaped(b_ada, norm_g, final_g, pool_scale, rel_bias, pool_w),
                                    shaped(m_b_ada, m_norm_g, m_final_g, m_pool_scale, m_rel_bias, m_pool_w),
                                    shaped(v_b_ada, v_norm_g, v_final_g, v_pool_scale, v_rel_bias, v_pool_w))
    dmod_all = small_all[:, PK_BADA:PK_NORMG, :].reshape(8, 3 * D)
    g_w_ada = _w_ada_grad(c_all, lax.dynamic_slice_in_dim(dmod_all, chip * mw, mw, axis=1))

    g_w_in, g_w_ab, g_w_pb, g_w_out = _sibling_join([r["rs_in"], r["rs_attn_br"], r["rs_pool_br"], r["rs_out"]],
                                                    "rs_sibling_join")
    upd = dict(zip(small, small_out))
    upd["final_g"] = [a.reshape(D) for a in upd["final_g"]]
    upd["pool_w"] = [a.reshape(1, 4, PGW, PGW) for a in upd["pool_w"]]
    for nme, w, g, m, v in (("w_ada", w_ada, g_w_ada, m_w_ada, v_w_ada), ("w_in", w_in, g_w_in, m_w_in, v_w_in),
                            ("w_attn_br", w_attn_br, g_w_ab, m_w_attn_br, v_w_attn_br),
                            ("w_pool_br", w_pool_br, g_w_pb, m_w_pool_br, v_w_pool_br),
                            ("w_out", w_out, g_w_out, m_w_out, v_w_out)):
        upd[nme] = [a[None] for a in _adamw(w[0], g, m[0], v[0], "adamw_" + nme)]
    names = ["norm_g", "w_ada", "b_ada", "w_in", "pool_w", "pool_scale", "w_attn_br", "w_pool_br", "w_out",
             "rel_bias", "final_g"]
    return (loss[0, 0], r["grad_x"][None]) + tuple(upd[nme][kind] for kind in range(4) for nme in names)
```

```python
import functools
import math

import numpy as np
import jax
import jax.numpy as jnp
from jax import lax
from jax.experimental import pallas as pl
from jax.experimental.pallas import tpu as pltpu

F32 = jnp.float32
BF16 = jnp.bfloat16

D = 1024
HD = 64
NH = 8
AW = NH * HD
GROUPS = ((128, 1), (512, 4), (2048, 16))
NG = len(GROUPS)
BLK = 128
GW = 3 * AW
QKV_W = NG * GW
REST_W = 3584
IN_W = QKV_W + REST_W
CB = 512
NCB = IN_W // CB
NCB_QKV = QKV_W // CB
POOL_WINDOWS = (2, 4, 8, 16)
PGW = 128
HALO = 16
NUM_BUCKETS = 32
MAX_DISTANCE = 2048
EPS = 1e-6
NEG = -1e30
N_SHARD = 4
VMEM_LIMIT = 56 * 1024 * 1024

ADAM_LR = 0.001
ADAM_B1 = 0.9
ADAM_B2 = 0.999
ADAM_EPS = 1e-08
ADAM_WD = 0.01
ADAM_STEP = 10

PK_BADA, PK_NORMG, PK_FINALG, PK_PSCALE, PK_RELB, PK_LOSS, PK_POOLW, PK_ROWS = 0, 24, 32, 40, 48, 80, 88, 600

ANY = pl.BlockSpec(memory_space=pl.ANY)
MESH = pl.DeviceIdType.MESH


def _params(*sem):
    return pltpu.CompilerParams(dimension_semantics=sem, vmem_limit_bytes=VMEM_LIMIT)


def _sds(shape, dtype=F32):
    return jax.ShapeDtypeStruct(shape, dtype)


def _dot(a, b):
    return jnp.dot(a, b, preferred_element_type=F32)


def _dot_nt(a, b):
    return lax.dot_general(a, b, (((1,), (1,)), ((), ())), preferred_element_type=F32)


def _dot_tn(a, b):
    return lax.dot_general(a, b, (((0,), (0,)), ((), ())), preferred_element_type=F32)


def _sigmoid(z):
    return 0.5 * jnp.tanh(0.5 * z) + 0.5


def _dma_sems(*shape):
    return pltpu.SemaphoreType.DMA(shape)


class _Ride:
    def __init__(self, arrays, out_shapes, n_copies, copies):
        self.arrays, self.out_shapes, self.n_copies, self.copies = list(arrays), list(out_shapes), n_copies, copies


def _call_with_ride(body, ride, first, last, *, in_specs, out_specs, out_shape, scratch_shapes=(), **kw):
    in_specs, out_specs, out_shape, scratch_shapes = list(in_specs), list(out_specs), list(out_shape), list(scratch_shapes)
    n_in, n_out, n_sc = len(in_specs), len(out_specs), len(scratch_shapes)
    if ride is None:
        def run_plain(*operands):
            return pl.pallas_call(body, in_specs=in_specs, out_specs=out_specs, out_shape=out_shape,
                                  scratch_shapes=scratch_shapes, **kw)(*operands), []
        return run_plain
    n_ri, n_ro = len(ride.arrays), len(ride.out_shapes)

    def wrapped(*refs):
        ins, rest = refs[:n_in], refs[n_in:]
        r_ins, rest = rest[:n_ri], rest[n_ri:]
        outs, rest = rest[:n_out], rest[n_out:]
        r_outs, rest = rest[:n_ro], rest[n_ro:]
        scratch, (send_sems, recv_sems) = rest[:n_sc], rest[n_sc:]

        @pl.when(first())
        def _():
            for cp in ride.copies(r_ins, r_outs, send_sems, recv_sems):
                cp.start()

        body(*ins, *outs, *scratch)

        @pl.when(last())
        def _():
            for cp in ride.copies(r_ins, r_outs, send_sems, recv_sems):
                cp.wait()

    def run(*operands):
        res = pl.pallas_call(
            wrapped, in_specs=in_specs + [ANY] * n_ri, out_specs=out_specs + [ANY] * n_ro,
            out_shape=out_shape + ride.out_shapes,
            scratch_shapes=scratch_shapes + [_dma_sems(ride.n_copies), _dma_sems(ride.n_copies)], **kw,
        )(*operands, *ride.arrays)
        return res[:n_out], res[n_out:]
    return run


def _bucket_tables():
    i = np.arange(BLK)[:, None]
    j = np.arange(2 * BLK)[None, :]
    dist = BLK + i - j
    valid = (dist >= 0) & (dist <= BLK)
    tabs = []
    for _, dil in GROUPS:
        n = (np.clip(dist, 0, BLK) * dil).astype(np.int32)
        max_exact = NUM_BUCKETS // 2
        nf = np.maximum(n, 1).astype(np.float32)
        large = max_exact + (np.log(nf / np.float32(max_exact)) / np.float32(math.log(MAX_DISTANCE / max_exact))
                             * np.float32(NUM_BUCKETS - max_exact)).astype(np.int32)
        large = np.minimum(large, NUM_BUCKETS - 1)
        bucket = np.where(n < max_exact, n, large)
        tab = np.where(valid, bucket, -1).astype(np.int32)
        perm = _block_perm(dil)
        tabs.append(tab[perm][:, np.concatenate([perm, BLK + perm])])
    return np.stack(tabs)


def _bias_table(rel_bias, buckets):
    def body(rb_ref, bk_ref, out_ref):
        g = pl.program_id(0)
        bk = bk_ref[...]
        for h in range(NH):
            acc = jnp.full((BLK, 2 * BLK), NEG, F32)
            for b in range(NUM_BUCKETS):
                acc = jnp.where(bk == b, rb_ref[b, g * NH + h], acc)
            out_ref[h] = acc

    return pl.pallas_call(
        body, name="bias_table", grid=(NG,),
        in_specs=[pl.BlockSpec(memory_space=pltpu.SMEM),
                  pl.BlockSpec((None, BLK, 2 * BLK), lambda g: (g, 0, 0))],
        out_specs=pl.BlockSpec((NH, BLK, 2 * BLK), lambda g: (g, 0, 0)),
        out_shape=_sds((NG * NH, BLK, 2 * BLK)),
        compiler_params=_params("arbitrary"),
    )(rel_bias, buckets)


def _bias_grad(ds_acc, buckets, ride):
    def body(acc_ref, bk_ref, out_ref):
        bk = bk_ref[...]
        acc = acc_ref[...]
        lane = lax.broadcasted_iota(jnp.int32, (8, 128), 1)
        out = jnp.zeros((8, 128), F32)
        for b in range(NUM_BUCKETS):
            val = jnp.sum(jnp.where(bk == b, acc, 0.0))
            out = jnp.where(lane == b, val, out)
        out_ref[...] = out

    (out,), rode = _call_with_ride(
        body, ride, lambda: pl.program_id(0) == 0, lambda: pl.program_id(0) == NG * NH - 1,
        name="bias_grad", grid=(NG * NH,),
        in_specs=[pl.BlockSpec((None, BLK, 2 * BLK), lambda gh: (gh, 0, 0)),
                  pl.BlockSpec((None, BLK, 2 * BLK), lambda gh: (gh // NH, 0, 0))],
        out_specs=[pl.BlockSpec((None, 8, 128), lambda gh: (gh, 0, 0))],
        out_shape=[_sds((NG * NH, 8, 128))],
        compiler_params=_params("arbitrary"),
    )(ds_acc, buckets)
    return out, rode


def _mod_partial(c_all, w_ada_s, b_ada_s):
    def body(c_ref, w_ref, b_ref, o_ref):
        o_ref[...] = _dot(c_ref[...].astype(BF16), w_ref[...].astype(BF16)) + b_ref[...]

    return pl.pallas_call(body, name="mod_partial", out_shape=_sds((8, w_ada_s.shape[1])),
                          compiler_params=_params())(c_all, w_ada_s, b_ada_s)


def _prenorm(x, norm_g, mod):
    S = x.shape[0]
    tm = 512

    def body(x_ref, g_ref, mod_ref, h_ref):
        xv = x_ref[...]
        r = lax.rsqrt(jnp.mean(xv * xv, axis=-1, keepdims=True) + EPS)
        n1 = xv * r * g_ref[...]
        h_ref[...] = (n1 * (1.0 + mod_ref[:, D:2 * D]) + mod_ref[:, 0:D]).astype(BF16)

    return pl.pallas_call(
        body, name="prenorm", grid=(S // tm,),
        in_specs=[pl.BlockSpec((tm, D), lambda i: (i, 0)), pl.BlockSpec((1, D), lambda i: (0, 0)),
                  pl.BlockSpec((1, 3 * D), lambda i: (0, 0))],
        out_specs=pl.BlockSpec((tm, D), lambda i: (i, 0)),
        out_shape=_sds((S, D), BF16), compiler_params=_params("parallel"),
    )(x, norm_g, mod)


def _proj(h, wg_in, j0, nj, dtype, name):
    S = h.shape[0]
    tm = 2048
    per = wg_in.shape[2] // CB

    def body(h_ref, w_ref, o_ref):
        o_ref[...] = _dot(h_ref[...], w_ref[...]).astype(dtype)

    return pl.pallas_call(
        body, name=name, grid=(S // tm, nj),
        in_specs=[pl.BlockSpec((tm, D), lambda m, j: (m, 0)),
                  pl.BlockSpec((None, D, CB), lambda m, j: ((j0 + j) // per, 0, (j0 + j) % per))],
        out_specs=pl.BlockSpec((tm, CB), lambda m, j: (m, j)),
        out_shape=_sds((S, nj * CB), dtype), compiler_params=_params("parallel", "parallel"),
    )(h, wg_in)


HS = 4
SLAB = HS * HD


def _lane_head(rows):
    return lax.broadcasted_iota(jnp.int32, (rows, SLAB), 1) // HD


def _head_stack(a):
    head = _lane_head(a.shape[0])
    return jnp.concatenate([jnp.where(head == h, a, jnp.zeros_like(a)) for h in range(HS)], axis=0)


def _head_unstack(a):
    rows = a.shape[0] // HS
    head = _lane_head(rows)
    out = a[:rows]
    for h in range(1, HS):
        out = jnp.where(head == h, a[h * rows:(h + 1) * rows], out)
    return out


STAT_W = 128
VIEW = 16


def _sub_layout(dil):
    if dil == 1:
        return BLK, [None]
    return BLK * dil // VIEW, [[r + dil * u for u in range(VIEW // dil)] for r in range(dil)]


def _block_perm(dil):
    a_rows, _ = _sub_layout(dil)
    p = np.arange(BLK)
    return p if dil == 1 else (VIEW // dil) * (p % a_rows) + p // a_rows


LB = 128
N_SLAB = NH // HS


def _ld(refs, bs, s, w):
    if bs is None:
        return refs[0][:, s * w:(s + 1) * w]
    a_rows = refs[0].shape[0] // VIEW
    return jnp.concatenate([jnp.concatenate([ref[pl.ds(b, a_rows, stride=VIEW), :] for b in bs], axis=0)
                            for ref in refs], axis=1)


def _st(ref, bs, s, val):
    if bs is None:
        ref[:, s * SLAB:(s + 1) * SLAB] = val
        return
    a_rows = val.shape[0] // len(bs)
    for u, b in enumerate(bs):
        ref[:, b, :] = val[u * a_rows:(u + 1) * a_rows]


def _attn_views(dil, S):
    a_rows, subs = _sub_layout(dil)
    if dil == 1:
        def ispecs(base, w, f):
            return [pl.BlockSpec((BLK, N_SLAB * w), lambda sg, n: (f(n), base // (N_SLAB * w)))]
        return subs, S // BLK, N_SLAB, ispecs, (lambda w: (S, w)), (
            lambda f: pl.BlockSpec((BLK, AW), lambda sg, n: (f(n), 0)))

    def ispecs(base, w, f):
        return [pl.BlockSpec((a_rows * VIEW, LB), lambda sg, n, k=k: (f(n), (base + sg * w) // LB + k))
                for k in range(w // LB)]
    return subs, S // (a_rows * VIEW), 1, ispecs, (lambda w: (S // VIEW, VIEW, w)), (
        lambda f: pl.BlockSpec((a_rows, VIEW, SLAB), lambda sg, n: (f(n), 0, sg)))


def _attn_fwd(qkv_g, bias_tab, g):
    S = qkv_g.shape[0]
    subs, nbq, sps, ispecs, shape, ospec = _attn_views(GROUPS[g][1], S)
    cur, prev = (lambda n: n), (lambda n: jnp.maximum(n - 1, 0))
    in_specs = [ispecs(0, SLAB, cur), ispecs(AW, SLAB, prev), ispecs(AW, SLAB, cur), ispecs(2 * AW, SLAB, prev),
                ispecs(2 * AW, SLAB, cur)]
    nl = len(in_specs[0])

    def body(*refs):
        q, kp, kc, vp, vc = (refs[t * nl:(t + 1) * nl] for t in range(5))
        b_ref, o_ref, l_ref = refs[5 * nl:]
        n = pl.program_id(1)
        col = lax.broadcasted_iota(jnp.int32, (HS * BLK, 2 * BLK), 1)
        keep = (col >= BLK) | (n > 0)
        for s_ in range(sps):
            bias = b_ref[pl.ds(s_ * HS, HS)].reshape(HS * BLK, 2 * BLK)
            for bs in subs:
                kb = jnp.concatenate([_ld(kp, bs, s_, SLAB), _ld(kc, bs, s_, SLAB)], axis=0).astype(BF16)
                vb = jnp.concatenate([_ld(vp, bs, s_, SLAB), _ld(vc, bs, s_, SLAB)], axis=0).astype(BF16)
                s = _dot_nt(_head_stack(_ld(q, bs, s_, SLAB).astype(BF16)), kb) * (HD ** -0.5) + bias
                s = jnp.where(keep, s, NEG)
                m = jnp.max(s, axis=-1, keepdims=True)
                p = jnp.exp(s - m)
                den = jnp.sum(p, axis=-1, keepdims=True)
                _st(o_ref, bs, s_, _head_unstack(_dot(p.astype(BF16), vb) / den))
                _st(l_ref, bs, s_, _head_unstack(jnp.broadcast_to(m + jnp.log(den), (HS * BLK, SLAB))))

    out = _sds(shape(AW))
    nsg = N_SLAB // sps
    o, l = pl.pallas_call(
        body, name=f"attn_fwd{g}", grid=(nsg, nbq),
        in_specs=sum(in_specs, []) + [pl.BlockSpec((sps * HS, BLK, 2 * BLK), lambda sg, n: (g * nsg + sg, 0, 0))],
        out_specs=[ospec(cur), ospec(cur)],
        out_shape=[out, out], compiler_params=_params("parallel", "arbitrary"),
    )(*([qkv_g] * (5 * nl)), bias_tab)
    return o.reshape(S, AW), l.reshape(S, AW)


def _attn_bwd(qkv_g, dattn, stats, bias_tab, g, ride):
    S = qkv_g.shape[0]
    subs, nbq, sps, ispecs, shape, ospec = _attn_views(GROUPS[g][1], S)
    cur = lambda n: jnp.minimum(n, nbq - 1)
    prev = lambda n: jnp.clip(n - 1, 0, nbq - 1)
    late = lambda n: jnp.maximum(n - 1, 0)
    in_specs = [ispecs(0, SLAB, cur), ispecs(AW, SLAB, prev), ispecs(AW, SLAB, cur), ispecs(2 * AW, SLAB, prev),
                ispecs(2 * AW, SLAB, cur), ispecs(0, SLAB, cur), ispecs(0, STAT_W, cur)]
    nl = len(in_specs[0])

    def body(*refs):
        q, kp, kc, vp, vc, da = (refs[t * nl:(t + 1) * nl] for t in range(6))
        st_ref, b_ref, dq_ref, dk_ref, dv_ref, ds_ref, ck_ref, cv_ref = refs[6 * nl:]
        n = pl.program_id(1)

        @pl.when(n == 0)
        def _():
            ds_ref[...] = jnp.zeros_like(ds_ref)
            ck_ref[...] = jnp.zeros_like(ck_ref)
            cv_ref[...] = jnp.zeros_like(cv_ref)

        @pl.when(n < nbq)
        def _():
            col = lax.broadcasted_iota(jnp.int32, (HS * BLK, 2 * BLK), 1)
            keep = (col >= BLK) | (n > 0)
            for s_ in range(sps):
                cs = slice(s_ * SLAB, (s_ + 1) * SLAB)
                bias = b_ref[pl.ds(s_ * HS, HS)].reshape(HS * BLK, 2 * BLK)
                for i, bs in enumerate(subs):
                    st = _ld((st_ref,), bs, s_, STAT_W)
                    kb = jnp.concatenate([_ld(kp, bs, s_, SLAB), _ld(kc, bs, s_, SLAB)], axis=0).astype(BF16)
                    vb = jnp.concatenate([_ld(vp, bs, s_, SLAB), _ld(vc, bs, s_, SLAB)], axis=0).astype(BF16)
                    lse = jnp.concatenate([st[:, h:h + 1] for h in range(HS)], axis=0)
                    delta = jnp.concatenate([st[:, HS + h:HS + h + 1] for h in range(HS)], axis=0)
                    qs = _head_stack(_ld(q, bs, s_, SLAB).astype(BF16))
                    dos = _head_stack(_ld(da, bs, s_, SLAB).astype(BF16))
                    s = _dot_nt(qs, kb) * (HD ** -0.5) + bias
                    s = jnp.where(keep, s, NEG)
                    p = jnp.exp(s - lse)
                    ds = p * (_dot_nt(dos, vb) - delta)
                    ds_ref[pl.ds(s_ * HS, HS)] += ds.reshape(HS, BLK, 2 * BLK)
                    ds_b = (ds * (HD ** -0.5)).astype(BF16)
                    _st(dq_ref, bs, s_, _head_unstack(_dot(ds_b, kb)))
                    dkb = _dot_tn(ds_b, qs)
                    dvb = _dot_tn(p.astype(BF16), dos)
                    _st(dk_ref, bs, s_, ck_ref[i, :, cs] + dkb[:BLK])
                    _st(dv_ref, bs, s_, cv_ref[i, :, cs] + dvb[:BLK])
                    ck_ref[i, :, cs] = dkb[BLK:]
                    cv_ref[i, :, cs] = dvb[BLK:]

        @pl.when(n == nbq)
        def _():
            for s_ in range(sps):
                for i, bs in enumerate(subs):
                    _st(dk_ref, bs, s_, ck_ref[i, :, s_ * SLAB:(s_ + 1) * SLAB])
                    _st(dv_ref, bs, s_, cv_ref[i, :, s_ * SLAB:(s_ + 1) * SLAB])

    out = _sds(shape(AW))
    nsg = N_SLAB // sps
    (dq, dk, dv, ds_acc), rode = _call_with_ride(
        body, ride, lambda: (pl.program_id(0) == 0) & (pl.program_id(1) == 0),
        lambda: (pl.program_id(0) == nsg - 1) & (pl.program_id(1) == nbq),
        name=f"attn_bwd{g}", grid=(nsg, nbq + 1),
        in_specs=sum(in_specs, []) + [pl.BlockSpec((sps * HS, BLK, 2 * BLK), lambda sg, n: (g * nsg + sg, 0, 0))],
        out_specs=[ospec(cur), ospec(late), ospec(late),
                   pl.BlockSpec((sps * HS, BLK, 2 * BLK), lambda sg, n: (sg, 0, 0))],
        out_shape=[out] * 3 + [_sds((NH, BLK, 2 * BLK))],
        scratch_shapes=[pltpu.VMEM((len(subs), BLK, sps * SLAB), F32), pltpu.VMEM((len(subs), BLK, sps * SLAB), F32)],
        compiler_params=_params("arbitrary", "arbitrary"),
    )(*([qkv_g] * (5 * nl)), *([dattn] * nl), stats, bias_tab)
    return [dq.reshape(S, AW), dk.reshape(S, AW), dv.reshape(S, AW)], ds_acc, rode


TM_MIX = 256


def _mix_specs(tm):
    row512 = pl.BlockSpec((tm, AW), lambda i: (i, 0))
    return ([row512] * 6 + [
        pl.BlockSpec((tm, REST_W), lambda i: (i, 0)),
        pl.BlockSpec((HALO, AW), lambda i: (jnp.maximum(i * (tm // HALO) - 1, 0), 1)),
        pl.BlockSpec((AW, D), lambda i: (0, 0)), pl.BlockSpec((AW, D), lambda i: (0, 0)),
        pl.BlockSpec((4, PGW, PGW), lambda i: (0, 0, 0)), pl.BlockSpec((1, AW), lambda i: (0, 0))])


def _mix_forward(i, tm, o_refs, l_refs, rest_ref, halo_ref, wab_ref, wpb_ref, pw_ref, ps_ref):
    l0, l1, l2 = (r[...] for r in l_refs)
    mx = jnp.maximum(jnp.maximum(l0, l1), l2)
    e0, e1, e2 = jnp.exp(l0 - mx), jnp.exp(l1 - mx), jnp.exp(l2 - mx)
    den = e0 + e1 + e2
    lj = mx + jnp.log(den)
    attn = (e0 * o_refs[0][...] + e1 * o_refs[1][...] + e2 * o_refs[2][...]) / den

    z_attn = rest_ref[:, 0:AW]
    u = rest_ref[:, AW:2 * AW]
    z_pool = rest_ref[:, 2 * AW:3 * AW]
    g_attn = rest_ref[:, 3 * AW:3 * AW + D]
    g_pool = rest_ref[:, 3 * AW + D:3 * AW + 2 * D]

    sg_a = _sigmoid(z_attn)
    sil_a = z_attn * sg_a
    a_g = (attn * sil_a).astype(BF16)
    y_attn = _dot(a_g, wab_ref[...])

    halo = jnp.where(i > 0, halo_ref[...], 0.0)
    ext = jnp.concatenate([halo, u], axis=0)
    t = i * tm + lax.broadcasted_iota(jnp.int32, (tm, 1), 0)
    pooled, mixed_raw = [], []
    for gi, win in enumerate(POOL_WINDOWS):
        s = ext[:, gi * PGW:(gi + 1) * PGW]
        sh = 1
        while sh < win:
            s = s + pltpu.roll(s, sh, 0)
            sh *= 2
        cnt = jnp.minimum(t + 1, win).astype(F32)
        pg = s[HALO:] / cnt - u[:, gi * PGW:(gi + 1) * PGW]
        pooled.append(pg.astype(BF16))
        mixed_raw.append(_dot(pooled[-1], pw_ref[gi].astype(BF16)))
    mixed_raw = jnp.concatenate(mixed_raw, axis=1)
    mixed = mixed_raw * ps_ref[...]
    sg_p = _sigmoid(z_pool)
    sil_p = z_pool * sg_p
    m_g = (mixed * sil_p).astype(BF16)
    y_pool = _dot(m_g, wpb_ref[...])

    sa = _sigmoid(g_attn)
    sp = _sigmoid(g_pool)
    merged = sa * y_attn + sp * y_pool
    return dict(lj=lj, attn=attn, z_attn=z_attn, z_pool=z_pool, sg_a=sg_a, sil_a=sil_a, a_g=a_g, y_attn=y_attn,
                pooled=pooled, mixed_raw=mixed_raw, mixed=mixed, sg_p=sg_p, sil_p=sil_p, m_g=m_g, y_pool=y_pool,
                sa=sa, sp=sp, merged=merged)


def _mix_step(x, target, os_, ls_, rest, wab, wpb, pool_w, pool_scale, wout, mod, final_g):
    S = x.shape[0]
    tm = TM_MIX
    nt = S // tm
    sw = D // N_SHARD

    def body(o0, o1, o2, l0, l1, l2, rest_ref, halo_ref, wab_ref, wpb_ref, pw_ref, ps_ref,
             x_ref, t_ref, wo_ref, mod_ref, fg_ref, dx2_ref, loss_ref, dfg_ref, dgate_ref,
             dattn_ref, stats_ref, dpooled_ref, drest_ref, dwo_hbm, dwab_hbm, dwpb_hbm, dpw_ref, dps_ref,
             awo, awab, awpb):
        i = pl.program_id(0)

        @pl.when(i == 0)
        def _():
            for ref in (loss_ref, dfg_ref, dgate_ref, awo, awab, awpb, dpw_ref, dps_ref):
                ref[...] = jnp.zeros_like(ref)

        f = _mix_forward(i, tm, (o0, o1, o2), (l0, l1, l2), rest_ref, halo_ref, wab_ref, wpb_ref, pw_ref, ps_ref)
        mo = _dot(f["merged"].astype(BF16), wo_ref[...])
        gate = mod_ref[:, 2 * D:3 * D]
        fg = fg_ref[...]
        x2 = x_ref[...] + gate * mo
        r2 = lax.rsqrt(jnp.mean(x2 * x2, axis=-1, keepdims=True) + EPS)
        n2 = x2 * r2
        err = n2 * fg - t_ref[...]
        loss_ref[...] += 0.5 * jnp.sum(jnp.mean(err * err, axis=-1, keepdims=True))
        dy = err * (1.0 / D)
        dfg_ref[...] += jnp.sum(dy * n2, axis=0, keepdims=True)
        dn = dy * fg
        dx2 = r2 * (dn - n2 * jnp.mean(dn * n2, axis=-1, keepdims=True))
        dgate_ref[...] += jnp.sum(dx2 * mo, axis=0, keepdims=True)
        dx2_ref[...] = dx2

        dmo_b = (dx2 * gate).astype(BF16)
        dmerged = _dot_nt(dmo_b, wo_ref[...])
        awo[...] += _dot_tn(f["merged"].astype(BF16), dmo_b)
        sa, sp = f["sa"], f["sp"]
        dya = (dmerged * sa).astype(BF16)
        dyp = (dmerged * sp).astype(BF16)
        dg_attn = dmerged * f["y_attn"] * sa * (1.0 - sa)
        dg_pool = dmerged * f["y_pool"] * sp * (1.0 - sp)
        dag = _dot_nt(dya, wab_ref[...])
        awab[...] += _dot_tn(f["a_g"], dya)
        dmg = _dot_nt(dyp, wpb_ref[...])
        awpb[...] += _dot_tn(f["m_g"], dyp)
        dattn = dag * f["sil_a"]
        dattn_ref[...] = dattn
        prod = dattn * f["attn"]
        lane = lax.broadcasted_iota(jnp.int32, (tm, STAT_W), 1)
        for sb in range(N_SLAB):
            st = jnp.zeros((tm, STAT_W), F32)
            for h in range(HS):
                hs = slice((sb * HS + h) * HD, (sb * HS + h + 1) * HD)
                st = jnp.where(lane == h, f["lj"][:, hs.start:hs.start + 1], st)
                st = jnp.where(lane == HS + h, jnp.sum(prod[:, hs], axis=-1, keepdims=True), st)
            stats_ref[:, sb * STAT_W:(sb + 1) * STAT_W] = st
        dz_attn = dag * f["attn"] * (f["sg_a"] * (1.0 + f["z_attn"] * (1.0 - f["sg_a"])))
        dmixed = dmg * f["sil_p"]
        dz_pool = dmg * f["mixed"] * (f["sg_p"] * (1.0 + f["z_pool"] * (1.0 - f["sg_p"])))
        dps_ref[...] += jnp.sum(dmixed * f["mixed_raw"], axis=0, keepdims=True)
        dpm = (dmixed * ps_ref[...]).astype(BF16)
        for gi in range(len(POOL_WINDOWS)):
            cs = slice(gi * PGW, (gi + 1) * PGW)
            dpw_ref[gi] += _dot_tn(f["pooled"][gi], dpm[:, cs])
            dpooled_ref[:, cs] = _dot_nt(dpm[:, cs], pw_ref[gi].astype(BF16))
        drest_ref[:, 0:AW] = dz_attn.astype(BF16)
        drest_ref[:, AW:2 * AW] = jnp.zeros((tm, AW), BF16)
        drest_ref[:, 2 * AW:3 * AW] = dz_pool.astype(BF16)
        drest_ref[:, 3 * AW:3 * AW + D] = dg_attn.astype(BF16)
        drest_ref[:, 3 * AW + D:3 * AW + 2 * D] = dg_pool.astype(BF16)

        @pl.when(i == nt - 1)
        def _():
            pltpu.sync_copy(awo, dwo_hbm)
            for k in range(N_SHARD):
                pltpu.sync_copy(awab.at[:, pl.ds(k * sw, sw)], dwab_hbm.at[k])
                pltpu.sync_copy(awpb.at[:, pl.ds(k * sw, sw)], dwpb_hbm.at[k])

    row = pl.BlockSpec((tm, D), lambda i: (i, 0))
    vec = pl.BlockSpec((1, D), lambda i: (0, 0))
    row512 = pl.BlockSpec((tm, AW), lambda i: (i, 0))
    outs = pl.pallas_call(
        body, name="mix_step", grid=(nt,),
        in_specs=_mix_specs(tm) + [row, row, pl.BlockSpec((D, D), lambda i: (0, 0)),
                                   pl.BlockSpec((1, 3 * D), lambda i: (0, 0)), vec],
        out_specs=[row, pl.BlockSpec((8, 128), lambda i: (0, 0)), vec, vec,
                   row512, pl.BlockSpec((tm, N_SLAB * STAT_W), lambda i: (i, 0)), row512,
                   pl.BlockSpec((tm, REST_W), lambda i: (i, 0)), ANY, ANY, ANY,
                   pl.BlockSpec((4, PGW, PGW), lambda i: (0, 0, 0)), pl.BlockSpec((1, AW), lambda i: (0, 0))],
        out_shape=[_sds((S, D)), _sds((8, 128)), _sds((1, D)), _sds((1, D)),
                   _sds((S, AW)), _sds((S, N_SLAB * STAT_W)), _sds((S, AW)), _sds((S, REST_W), BF16),
                   _sds((D, D)), _sds((N_SHARD, AW, sw)), _sds((N_SHARD, AW, sw)), _sds((4, PGW, PGW)), _sds((1, AW))],
        scratch_shapes=[pltpu.VMEM((D, D), F32), pltpu.VMEM((AW, D), F32), pltpu.VMEM((AW, D), F32)],
        compiler_params=_params("arbitrary"),
    )(*os_, *ls_, rest, rest, wab, wpb, pool_w, pool_scale, x, target, wout, mod, final_g)
    dx2, loss, dfg, dgate, dattn, stats, dpooled, drest, dwo, dwab, dwpb, dpw, dps = outs
    return (dx2, loss, dfg, dgate, dattn, stats, dpooled, drest, dwo.reshape(N_SHARD, D // N_SHARD, D), dwab, dwpb,
            dpw, dps)


def _pool_bwd(dpooled):
    S = dpooled.shape[0]
    tm = 512
    nt = S // tm

    def body(dp_ref, nxt_ref, du_ref):
        i = pl.program_id(0)
        t = i * tm + lax.broadcasted_iota(jnp.int32, (tm + HALO, 1), 0)
        nxt = jnp.where(i < nt - 1, nxt_ref[...], 0.0)
        ext = jnp.concatenate([dp_ref[...], nxt], axis=0)
        for gi, win in enumerate(POOL_WINDOWS):
            cs = slice(gi * PGW, (gi + 1) * PGW)
            s = ext[:, cs] / jnp.minimum(t + 1, win).astype(F32)
            sh = 1
            while sh < win:
                s = s + pltpu.roll(s, tm + HALO - sh, 0)
                sh *= 2
            du_ref[:, cs] = (s[:tm] - dp_ref[:, cs]).astype(BF16)

    return pl.pallas_call(
        body, name="pool_bwd", grid=(nt,),
        in_specs=[pl.BlockSpec((tm, AW), lambda i: (i, 0)),
                  pl.BlockSpec((HALO, AW), lambda i: (jnp.minimum((i + 1) * (tm // HALO), S // HALO - 1), 0))],
        out_specs=pl.BlockSpec((tm, AW), lambda i: (i, 0)),
        out_shape=_sds((S, AW), BF16), compiler_params=_params("parallel"),
    )(dpooled, dpooled)


TB = 1024


def _dh(dproj, wg_in, ride):
    S = dproj.shape[0]
    per = wg_in.shape[2] // TB
    nm, nk = S // TB, IN_W // TB

    def body(dp_ref, w_ref, out_ref):
        @pl.when(pl.program_id(1) == 0)
        def _():
            out_ref[...] = jnp.zeros_like(out_ref)

        out_ref[...] += _dot_nt(dp_ref[...], w_ref[...])

    (dh,), rode = _call_with_ride(
        body, ride, lambda: (pl.program_id(0) == 0) & (pl.program_id(1) == 0),
        lambda: (pl.program_id(0) == nm - 1) & (pl.program_id(1) == nk - 1),
        name="dh", grid=(nm, nk),
        in_specs=[pl.BlockSpec((TB, TB), lambda m, kk: (m, kk)),
                  pl.BlockSpec((None, D, TB), lambda m, kk: (kk // per, 0, kk % per))],
        out_specs=[pl.BlockSpec((TB, D), lambda m, kk: (m, 0))],
        out_shape=[_sds((S, D))], compiler_params=_params("arbitrary", "arbitrary"),
    )(dproj, wg_in)
    return dh, rode


def _dw_in(h, dproj):
    S = dproj.shape[0]
    per = IN_W // N_SHARD // TB

    def body(h_ref, dp_ref, out_ref):
        @pl.when(pl.program_id(1) == 0)
        def _():
            out_ref[...] = jnp.zeros_like(out_ref)

        out_ref[...] += _dot_tn(h_ref[...], dp_ref[...])

    return pl.pallas_call(
        body, name="dw_in", grid=(IN_W // TB, S // TB),
        in_specs=[pl.BlockSpec((TB, D), lambda j, kk: (kk, 0)), pl.BlockSpec((TB, TB), lambda j, kk: (kk, j))],
        out_specs=pl.BlockSpec((None, D, TB), lambda j, kk: (j // per, 0, j % per)),
        out_shape=_sds((N_SHARD, D, IN_W // N_SHARD)), compiler_params=_params("parallel", "arbitrary"),
    )(h, dproj)


def _prenorm_bwd(x, dh, dx2, norm_g, mod):
    S = x.shape[0]
    tm = 512

    def body(x_ref, dh_ref, dx2_ref, g_ref, mod_ref, gx_ref, dg_ref, dshift_ref, dscale_ref):
        i = pl.program_id(0)

        @pl.when(i == 0)
        def _():
            dg_ref[...] = jnp.zeros_like(dg_ref)
            dshift_ref[...] = jnp.zeros_like(dshift_ref)
            dscale_ref[...] = jnp.zeros_like(dscale_ref)

        xv = x_ref[...]
        dhv = dh_ref[...]
        g = g_ref[...]
        r = lax.rsqrt(jnp.mean(xv * xv, axis=-1, keepdims=True) + EPS)
        xh = xv * r
        dshift_ref[...] += jnp.sum(dhv, axis=0, keepdims=True)
        dscale_ref[...] += jnp.sum(dhv * (xh * g), axis=0, keepdims=True)
        dn1 = dhv * (1.0 + mod_ref[:, D:2 * D])
        dg_ref[...] += jnp.sum(dn1 * xh, axis=0, keepdims=True)
        dxh = dn1 * g
        gx_ref[...] = dx2_ref[...] + r * (dxh - xh * jnp.mean(dxh * xh, axis=-1, keepdims=True))

    row = pl.BlockSpec((tm, D), lambda i: (i, 0))
    vec = pl.BlockSpec((1, D), lambda i: (0, 0))
    return pl.pallas_call(
        body, name="prenorm_bwd", grid=(S // tm,),
        in_specs=[row, row, row, vec, pl.BlockSpec((1, 3 * D), lambda i: (0, 0))],
        out_specs=[row, vec, vec, vec],
        out_shape=[_sds((S, D)), _sds((1, D)), _sds((1, D)), _sds((1, D))],
        compiler_params=_params("arbitrary"),
    )(x, dh, dx2, norm_g, mod)


def _local_step(x, target, mod, wg_in, wab, wpb, wout, pool_w, pool_scale, rel_bias, norm_g, final_g, half_idx,
                chip_half):
    buckets = jnp.asarray(_bucket_tables())
    bias_tab = _bias_table(rel_bias, buckets)
    h = _prenorm(x, norm_g, mod)
    qkv = [_proj(h, wg_in, 3 * g, 3, F32, f"proj_qkv{g}") for g in range(NG)]
    rest = _proj(h, wg_in, NCB_QKV, REST_W // CB, F32, "proj_rest")
    os_, ls_ = zip(*[_attn_fwd(qkv[g], bias_tab, g) for g in range(NG)])
    (dx2, loss, dfinal_g, dgate, dattn, stats, dpooled, drest, dw_out, dw_ab, dw_pb, dpool_w,
     dpool_scale) = _mix_step(x, target, os_, ls_, rest, wab, wpb, pool_w, pool_scale, wout, mod, final_g)
    du = _pool_bwd(dpooled)

    small = [dw_ab, dw_pb, dw_out]
    dqkv0, ds0, sib_small = _attn_bwd(qkv[0], dattn, stats, bias_tab, 0, _ride_sibling_halves(small))
    p_small = [_pair_sum(g, t, half_idx, f"rs_pair_sum{a}") for a, (g, t) in enumerate(zip(small, sib_small))]
    dqkv1, ds1, u_small = _attn_bwd(qkv[1], dattn, stats, bias_tab, 1,
                                    _ride_chip_exchange([p16 for _, p16 in p_small]))
    rs_ab, rs_pb, rs_out = [_chip_sum(p32, u, chip_half, f"rs_chip_sum{a}")
                            for a, ((p32, _), u) in enumerate(zip(p_small, u_small))]
    dqkv2, ds2, _ = _attn_bwd(qkv[2], dattn, stats, bias_tab, 2, None)

    dproj = jnp.concatenate([a.astype(BF16) for a in dqkv0 + dqkv1 + dqkv2] + [drest[:, :AW], du, drest[:, 2 * AW:]],
                            axis=1)
    dw_in = _dw_in(h, dproj)
    drel_rows, (sib_in,) = _bias_grad(jnp.concatenate([ds0, ds1, ds2], axis=0), buckets,
                                      _ride_sibling_halves([dw_in]))
    drel = drel_rows[:, 0, :NUM_BUCKETS].T
    p32_in, p16_in = _pair_sum(dw_in, sib_in, half_idx, "rs_pair_sum_in")
    dh, (u_in,) = _dh(dproj, wg_in, _ride_chip_exchange([p16_in]))
    rs_in = _chip_sum(p32_in, u_in, chip_half, "rs_chip_sum_in")

    grad_x, dnorm_g, dshift, dscale = _prenorm_bwd(x, dh, dx2, norm_g, mod)
    dmod = jnp.concatenate([dshift, dscale, dgate], axis=1)
    return dict(loss=loss[0, 0], grad_x=grad_x, dmod=dmod, dnorm_g=dnorm_g, dfinal_g=dfinal_g, dpool_w=dpool_w,
                dpool_scale=dpool_scale, drel_bias=drel, dw_in=dw_in, dw_attn_br=dw_ab, dw_pool_br=dw_pb,
                dw_out=dw_out, rs_in=rs_in, rs_attn_br=rs_ab, rs_pool_br=rs_pb, rs_out=rs_out)


def _allgather8(blocks, name, relay=None):
    nb = len(blocks)
    relay = [False] * nb if relay is None else list(relay)

    def body(*refs):
        ins, outs = refs[:nb], refs[nb:2 * nb]
        send_sems, recv_sems = refs[2 * nb:]
        x, y, c = lax.axis_index("x"), lax.axis_index("y"), lax.axis_index("c")
        me, sibling = (x, y, c), (x, y, 1 - c)
        here, xn, yn, dg = (x, y), (1 - x, y), (x, 1 - y), (1 - x, 1 - y)

        def slot(a, chip, core, half=None):
            ref = outs[a].at[4 * chip[0] + 2 * chip[1] + core]
            if half is None:
                return ref
            r2 = ref.shape[0] // 2
            return ref.at[pl.ds(half * r2, r2)]

        def copy(a, k, dst, to, src=None):
            return pltpu.make_async_remote_copy(src_ref=dst if src is None else src, dst_ref=dst,
                                                send_sem=send_sems.at[a, k], recv_sem=recv_sems.at[a, k],
                                                device_id=to, device_id_type=MESH)

        def start(cps):
            for cp in cps:
                cp.start()
            return cps

        sent = []
        for a in range(nb):
            own = slot(a, here, c)
            sent += [copy(a, 0, own, sibling, src=ins[a]), copy(a, 1, own, (*xn, c), src=ins[a]),
                     copy(a, 2, own, (*yn, c), src=ins[a])]
            if not relay[a]:
                sent.append(copy(a, 3, own, (*dg, c), src=ins[a]))
        start(sent)
        for a in range(nb):
            copy(a, 2, slot(a, yn, c), me).wait_recv()
            sent += start([copy(a, 6, slot(a, yn, c), sibling)]
                          + ([copy(a, 3, slot(a, yn, c, 0), (*xn, c))] if relay[a] else []))
        for a in range(nb):
            copy(a, 1, slot(a, xn, c), me).wait_recv()
            sent += start([copy(a, 5, slot(a, xn, c), sibling)]
                          + ([copy(a, 4, slot(a, xn, c, 1), (*yn, c))] if relay[a] else []))
        for a in range(nb):
            for k, half in ((3, 0), (4, 1)) if relay[a] else ((3, None),):
                copy(a, k, slot(a, dg, c, half), me).wait_recv()
                sent += start([copy(a, 4 + k, slot(a, dg, c, half), sibling)])
        for a in range(nb):
            copy(a, 0, slot(a, here, 1 - c), me).wait_recv()
            copy(a, 5, slot(a, xn, 1 - c), me).wait_recv()
            copy(a, 6, slot(a, yn, 1 - c), me).wait_recv()
            for k, half in ((7, 0), (8, 1)) if relay[a] else ((7, None),):
                copy(a, k, slot(a, dg, 1 - c, half), me).wait_recv()
        for cp in sent:
            cp.wait_send()

    outs = pl.pallas_call(
        body, name=name, in_specs=[ANY] * nb, out_specs=[ANY] * nb,
        out_shape=[_sds((8,) + b.shape, b.dtype) for b in blocks],
        scratch_shapes=[_dma_sems(nb, 9), _dma_sems(nb, 9)],
    )(*blocks)
    return [_place_own(buf, b) for buf, b in zip(outs, blocks)]


def _place_own(buf, block):
    dev = 4 * lax.axis_index("x") + 2 * lax.axis_index("y") + lax.axis_index("c")
    return lax.dynamic_update_index_in_dim(buf, block, dev, 0)


def _ride_sibling_halves(gs):
    def copies(ins, outs, send_sems, recv_sems):
        x, y, c = lax.axis_index("x"), lax.axis_index("y"), lax.axis_index("c")
        cps = []
        for a in range(len(gs)):
            r2 = ins[a].shape[1] // 2
            other = ins[a].at[:, pl.ds((1 - c) * r2, r2), :]
            cps.append(pltpu.make_async_remote_copy(src_ref=other, dst_ref=outs[a], send_sem=send_sems.at[a],
                                                    recv_sem=recv_sems.at[a], device_id=(x, y, 1 - c),
                                                    device_id_type=MESH))
        return cps

    return _Ride(gs, [_sds((g.shape[0], g.shape[1] // 2, g.shape[2]), g.dtype) for g in gs], len(gs), copies)


def _pair_sum(g, t, half, name):
    nsh, rows, cols = g.shape
    r2 = rows // 2
    tr = _row_tile(r2, cols)
    nt = r2 // tr

    def body(half_ref, g_ref, t_ref, p32_ref, p16_ref):
        p = g_ref[...] + t_ref[...]
        p32_ref[...] = p
        p16_ref[...] = p.astype(BF16)

    blk = pl.BlockSpec((None, tr, cols), lambda k, i, half_ref: (k, i, 0))
    return pl.pallas_call(
        body, name=name,
        grid_spec=pltpu.PrefetchScalarGridSpec(
            num_scalar_prefetch=1, grid=(nsh, nt),
            in_specs=[pl.BlockSpec((None, tr, cols), lambda k, i, half_ref: (k, half_ref[0] * nt + i, 0)), blk],
            out_specs=[blk, blk]),
        out_shape=[_sds((nsh, r2, cols)), _sds((nsh, r2, cols), BF16)],
        compiler_params=_params("parallel", "parallel"),
    )(half, g, t)


def _ride_chip_exchange(ps):
    def copies(ins, outs, send_sems, recv_sems):
        x, y, c = lax.axis_index("x"), lax.axis_index("y"), lax.axis_index("c")
        chips = [(1 - x, y), (x, 1 - y), (1 - x, 1 - y)]
        cps = []
        for a in range(len(ps)):
            for j, (ox, oy) in enumerate(chips):
                cps.append(pltpu.make_async_remote_copy(src_ref=ins[a].at[2 * ox + oy], dst_ref=outs[a].at[j],
                                                        send_sem=send_sems.at[3 * a + j],
                                                        recv_sem=recv_sems.at[3 * a + j],
                                                        device_id=(ox, oy, c), device_id_type=MESH))
        return cps

    return _Ride(ps, [_sds((3,) + p.shape[1:], p.dtype) for p in ps], 3 * len(ps), copies)


def _chip_sum(p32, u, chip_half, name):
    r2, cols = p32.shape[1:]
    tr = _row_tile(r2, cols)
    nt = r2 // tr

    def body(ch_ref, p_ref, u_ref, o_ref):
        acc = p_ref[...]
        for j in range(3):
            acc = acc + u_ref[j].astype(F32)
        o_ref[...] = acc

    return pl.pallas_call(
        body, name=name,
        grid_spec=pltpu.PrefetchScalarGridSpec(
            num_scalar_prefetch=1, grid=(nt,),
            in_specs=[pl.BlockSpec((None, tr, cols), lambda i, ch_ref: (ch_ref[0], i, 0)),
                      pl.BlockSpec((3, tr, cols), lambda i, ch_ref: (0, i, 0))],
            out_specs=pl.BlockSpec((tr, cols), lambda i, ch_ref: (ch_ref[1] * nt + i, 0))),
        out_shape=_sds((2 * r2, cols)), compiler_params=_params("parallel"),
    )(chip_half, p32, u)


def _sibling_join(fs, name):
    nb = len(fs)

    def body(*refs):
        outs = refs[nb:2 * nb]
        send_sems, recv_sems = refs[2 * nb:]
        x, y, c = lax.axis_index("x"), lax.axis_index("y"), lax.axis_index("c")
        cps = []
        for a in range(nb):
            r2 = outs[a].shape[0] // 2
            rows = outs[a].at[pl.ds(c * r2, r2), :]
            cps.append(pltpu.make_async_remote_copy(src_ref=rows, dst_ref=rows, send_sem=send_sems.at[a],
                                                    recv_sem=recv_sems.at[a], device_id=(x, y, 1 - c),
                                                    device_id_type=MESH))
        for cp in cps:
            cp.start()
        for cp in cps:
            cp.wait()

    return pl.pallas_call(
        body, name=name, in_specs=[ANY] * nb, out_specs=[ANY] * nb,
        out_shape=[_sds(f.shape, f.dtype) for f in fs],
        input_output_aliases={a: a for a in range(nb)},
        scratch_shapes=[_dma_sems(nb), _dma_sems(nb)],
    )(*fs)


def _row_tile(rows, cols):
    tile = rows
    while tile * cols * 4 > (1 << 20) and tile % 16 == 0:
        tile //= 2
    return tile


def _w_ada_grad(c_all, dmod_cols):
    def body(c_ref, d_ref, o_ref):
        o_ref[...] = _dot_tn(c_ref[...].astype(BF16), d_ref[...].astype(BF16))

    return pl.pallas_call(body, name="w_ada_grad", out_shape=_sds((c_all.shape[1], dmod_cols.shape[1])),
                          compiler_params=_params())(c_all, dmod_cols)


def _adam_math(w, g, m, v):
    nm = ADAM_B1 * m + (1.0 - ADAM_B1) * g
    nv = ADAM_B2 * v + (1.0 - ADAM_B2) * (g * g)
    m_hat = nm / (1.0 - ADAM_B1 ** ADAM_STEP)
    v_hat = nv / (1.0 - ADAM_B2 ** ADAM_STEP)
    return -ADAM_LR * (m_hat / (jnp.sqrt(v_hat) + ADAM_EPS) + ADAM_WD * w), nm, nv


def _adamw(w, g, m, v, name):
    rows, cols = w.shape
    tr = _row_tile(rows, cols)

    def body(w_ref, g_ref, m_ref, v_ref, go_ref, d_ref, nm_ref, nv_ref):
        gv = g_ref[...]
        go_ref[...] = gv
        d_ref[...], nm_ref[...], nv_ref[...] = _adam_math(w_ref[...], gv, m_ref[...], v_ref[...])

    spec = pl.BlockSpec((tr, cols), lambda i: (i, 0))
    return pl.pallas_call(
        body, name=name, grid=(rows // tr,), in_specs=[spec] * 4, out_specs=[spec] * 4,
        out_shape=[_sds((rows, cols))] * 4, compiler_params=_params("parallel"),
    )(w, g, m, v)


def _pack_small(dmod, dnorm_g, dfinal_g, dpool_scale, drel_bias, loss, dpool_w):
    return jnp.concatenate([dmod.reshape(-1, 128), dnorm_g.reshape(-1, 128), dfinal_g.reshape(-1, 128),
                            jnp.pad(dpool_scale.reshape(-1, 128), ((0, PK_RELB - PK_PSCALE - AW // 128), (0, 0))),
                            jnp.pad(drel_bias, ((0, 0), (0, 128 - NG * NH))),
                            jnp.full((PK_POOLW - PK_LOSS, 128), loss, F32), dpool_w.reshape(-1, 128)], axis=0)


def _small_update(small_all, ws, ms, vs):
    lane_rows = [(r0, r0 + w.shape[1] // 128) for r0, w in zip((PK_BADA, PK_NORMG, PK_FINALG, PK_PSCALE), ws)]
    nw = len(ws)

    def body(all_ref, *refs):
        w_refs, m_refs, v_refs = refs[:nw], refs[nw:2 * nw], refs[2 * nw:3 * nw]
        loss_ref, outs = refs[3 * nw], refs[3 * nw + 1:]
        g = all_ref[0]
        for s in range(1, all_ref.shape[0]):
            g = g + all_ref[s]
        loss_ref[...] = jnp.broadcast_to(g[PK_LOSS:PK_LOSS + 1, :], loss_ref.shape)

        def put(p, at, gv):
            d, nm, nv = _adam_math(w_refs[p][at], gv, m_refs[p][at], v_refs[p][at])
            for o_ref, val in zip(outs[4 * p:4 * p + 4], (gv, d, nm, nv)):
                o_ref[at] = val

        for p, (r0, r1) in enumerate(lane_rows):
            for i in range(r1 - r0):
                put(p, (slice(None), slice(128 * i, 128 * (i + 1))), g[r0 + i:r0 + i + 1, :])
        put(4, (slice(None), slice(None)), g[PK_RELB:PK_LOSS, 0:NG * NH])
        put(5, (slice(None), slice(None)), g[PK_POOLW:PK_ROWS, :])

    res = pl.pallas_call(
        body, name="small_update",
        out_shape=[_sds((8, 128))] + [_sds(w.shape) for w in ws for _ in range(4)], compiler_params=_params(),
    )(small_all, *ws, *ms, *vs)
    return res[0], [res[1 + 4 * p:5 + 4 * p] for p in range(nw)]


def kernel(x, c, norm_g, w_ada, b_ada, w_in, pool_w, pool_scale, w_attn_br, w_pool_br, w_out, rel_bias, final_g, loss_target, m_norm_g, m_w_ada, m_b_ada, m_w_in, m_pool_w, m_pool_scale, m_w_attn_br, m_w_pool_br, m_w_out, m_rel_bias, m_final_g, v_norm_g, v_w_ada, v_b_ada, v_w_in, v_pool_w, v_pool_scale, v_w_attn_br, v_w_pool_br, v_w_out, v_rel_bias, v_final_g):
    ix, iy, ic = lax.axis_index("x"), lax.axis_index("y"), lax.axis_index("c")
    dev = 4 * ix + 2 * iy + ic
    chip = 2 * ix + iy

    def half(w):
        r2 = w.shape[0] // 2
        return lax.dynamic_slice_in_dim(w, ic * r2, r2, axis=0).astype(BF16)

    gathered = _allgather8([jnp.broadcast_to(c, (8, D)), half(w_in[0]), half(w_attn_br[0]), half(w_pool_br[0]),
                            half(w_out[0])], "gather_weights", relay=[False, True, True, True, True])
    c_all = gathered[0][:, 0, :]
    wg_in = gathered[1].reshape(N_SHARD, D, IN_W // N_SHARD)
    wab = gathered[2].reshape(N_SHARD, AW, D // N_SHARD).transpose(1, 0, 2).reshape(AW, D)
    wpb = gathered[3].reshape(N_SHARD, AW, D // N_SHARD).transpose(1, 0, 2).reshape(AW, D)
    wout = gathered[4].reshape(D, D)

    mw = 3 * D // N_SHARD
    modp = _mod_partial(c_all, w_ada[0], lax.dynamic_slice_in_dim(b_ada, chip * mw, mw, axis=1))
    mod_all = _allgather8([modp], "gather_mod")[0]
    mod_full = mod_all[::2].transpose(1, 0, 2).reshape(8, 3 * D)
    mod = lax.dynamic_slice_in_dim(mod_full, dev, 1, axis=0)

    half_idx = jnp.stack([ic]).astype(jnp.int32)
    chip_half = jnp.stack([chip, ic]).astype(jnp.int32)
    r = _local_step(x[0], loss_target[0], mod, wg_in, wab, wpb, wout, pool_w[0], pool_scale, rel_bias, norm_g,
                    final_g.reshape(1, D), half_idx, chip_half)

    packed = _pack_small(r["dmod"], r["dnorm_g"], r["dfinal_g"], r["dpool_scale"], r["drel_bias"], r["loss"],
                         r["dpool_w"])
    small_all = _allgather8([packed], "gather_small")[0]
    small = ["b_ada", "norm_g", "final_g", "pool_scale", "rel_bias", "pool_w"]
    shaped = lambda b, n, f, ps, rb, pw: [b, n, f.reshape(1, D), ps, rb, pw.reshape(4 * PGW, PGW)]
    loss, small_out = _small_update(small_all, shaped(b_ada, norm_g, final_g, pool_scale, rel_bias, pool_w),
                                    shaped(m_b_ada, m_norm_g, m_final_g, m_pool_scale, m_rel_bias, m_pool_w),
                                    shaped(v_b_ada, v_norm_g, v_final_g, v_pool_scale, v_rel_bias, v_pool_w))
    dmod_all = small_all[:, PK_BADA:PK_NORMG, :].reshape(8, 3 * D)
    g_w_ada = _w_ada_grad(c_all, lax.dynamic_slice_in_dim(dmod_all, chip * mw, mw, axis=1))

    g_w_in, g_w_ab, g_w_pb, g_w_out = _sibling_join([r["rs_in"], r["rs_attn_br"], r["rs_pool_br"], r["rs_out"]],
                                                    "rs_sibling_join")
    upd = dict(zip(small, small_out))
    upd["final_g"] = [a.reshape(D) for a in upd["final_g"]]
    upd["pool_w"] = [a.reshape(1, 4, PGW, PGW) for a in upd["pool_w"]]
    for nme, w, g, m, v in (("w_ada", w_ada, g_w_ada, m_w_ada, v_w_ada), ("w_in", w_in, g_w_in, m_w_in, v_w_in),
                            ("w_attn_br", w_attn_br, g_w_ab, m_w_attn_br, v_w_attn_br),
                            ("w_pool_br", w_pool_br, g_w_pb, m_w_pool_br, v_w_pool_br),
                            ("w_out", w_out, g_w_out, m_w_out, v_w_out)):
        upd[nme] = [a[None] for a in _adamw(w[0], g, m[0], v[0], "adamw_" + nme)]
    names = ["norm_g", "w_ada", "b_ada", "w_in", "pool_w", "pool_scale", "w_attn_br", "w_pool_br", "w_out",
             "rel_bias", "final_g"]
    return (loss[0, 0], r["grad_x"][None]) + tuple(upd[nme][kind] for kind in range(4) for nme in names)
```

```python
import functools
import math

import numpy as np
import jax
import jax.numpy as jnp
from jax import lax
from jax.experimental import pallas as pl
from jax.experimental.pallas import tpu as pltpu

F32 = jnp.float32
BF16 = jnp.bfloat16

D = 1024
HD = 64
NH = 8
AW = NH * HD
GROUPS = ((128, 1), (512, 4), (2048, 16))
NG = len(GROUPS)
BLK = 128
GW = 3 * AW
QKV_W = NG * GW
REST_W = 3584
IN_W = QKV_W + REST_W
CB = 512
NCB = IN_W // CB
NCB_QKV = QKV_W // CB
POOL_WINDOWS = (2, 4, 8, 16)
PGW = 128
HALO = 16
NUM_BUCKETS = 32
MAX_DISTANCE = 2048
EPS = 1e-6
NEG = -1e30
N_SHARD = 4
VMEM_LIMIT = 56 * 1024 * 1024

ADAM_LR = 0.001
ADAM_B1 = 0.9
ADAM_B2 = 0.999
ADAM_EPS = 1e-08
ADAM_WD = 0.01
ADAM_STEP = 10

PK_BADA, PK_NORMG, PK_FINALG, PK_PSCALE, PK_RELB, PK_LOSS, PK_POOLW, PK_ROWS = 0, 24, 32, 40, 48, 80, 88, 600

ANY = pl.BlockSpec(memory_space=pl.ANY)
MESH = pl.DeviceIdType.MESH


def _params(*sem):
    return pltpu.CompilerParams(dimension_semantics=sem, vmem_limit_bytes=VMEM_LIMIT)


def _sds(shape, dtype=F32):
    return jax.ShapeDtypeStruct(shape, dtype)


def _dot(a, b):
    return jnp.dot(a, b, preferred_element_type=F32)


def _dot_nt(a, b):
    return lax.dot_general(a, b, (((1,), (1,)), ((), ())), preferred_element_type=F32)


def _dot_tn(a, b):
    return lax.dot_general(a, b, (((0,), (0,)), ((), ())), preferred_element_type=F32)


def _sigmoid(z):
    return 0.5 * jnp.tanh(0.5 * z) + 0.5


def _dma_sems(*shape):
    return pltpu.SemaphoreType.DMA(shape)


class _Ride:
    def __init__(self, arrays, out_shapes, n_copies, copies):
        self.arrays, self.out_shapes, self.n_copies, self.copies = list(arrays), list(out_shapes), n_copies, copies


def _call_with_ride(body, ride, first, last, *, in_specs, out_specs, out_shape, scratch_shapes=(), **kw):
    in_specs, out_specs, out_shape, scratch_shapes = list(in_specs), list(out_specs), list(out_shape), list(scratch_shapes)
    n_in, n_out, n_sc = len(in_specs), len(out_specs), len(scratch_shapes)
    if ride is None:
        def run_plain(*operands):
            return pl.pallas_call(body, in_specs=in_specs, out_specs=out_specs, out_shape=out_shape,
                                  scratch_shapes=scratch_shapes, **kw)(*operands), []
        return run_plain
    n_ri, n_ro = len(ride.arrays), len(ride.out_shapes)

    def wrapped(*refs):
        ins, rest = refs[:n_in], refs[n_in:]
        r_ins, rest = rest[:n_ri], rest[n_ri:]
        outs, rest = rest[:n_out], rest[n_out:]
        r_outs, rest = rest[:n_ro], rest[n_ro:]
        scratch, (send_sems, recv_sems) = rest[:n_sc], rest[n_sc:]

        @pl.when(first())
        def _():
            for cp in ride.copies(r_ins, r_outs, send_sems, recv_sems):
                cp.start()

        body(*ins, *outs, *scratch)

        @pl.when(last())
        def _():
            for cp in ride.copies(r_ins, r_outs, send_sems, recv_sems):
                cp.wait()

    def run(*operands):
        res = pl.pallas_call(
            wrapped, in_specs=in_specs + [ANY] * n_ri, out_specs=out_specs + [ANY] * n_ro,
            out_shape=out_shape + ride.out_shapes,
            scratch_shapes=scratch_shapes + [_dma_sems(ride.n_copies), _dma_sems(ride.n_copies)], **kw,
        )(*operands, *ride.arrays)
        return res[:n_out], res[n_out:]
    return run


def _bucket_tables():
    i = np.arange(BLK)[:, None]
    j = np.arange(2 * BLK)[None, :]
    dist = BLK + i - j
    valid = (dist >= 0) & (dist <= BLK)
    tabs = []
    for _, dil in GROUPS:
        n = (np.clip(dist, 0, BLK) * dil).astype(np.int32)
        max_exact = NUM_BUCKETS // 2
        nf = np.maximum(n, 1).astype(np.float32)
        large = max_exact + (np.log(nf / np.float32(max_exact)) / np.float32(math.log(MAX_DISTANCE / max_exact))
                             * np.float32(NUM_BUCKETS - max_exact)).astype(np.int32)
        large = np.minimum(large, NUM_BUCKETS - 1)
        bucket = np.where(n < max_exact, n, large)
        tab = np.where(valid, bucket, -1).astype(np.int32)
        perm = _block_perm(dil)
        tabs.append(tab[perm][:, np.concatenate([perm, BLK + perm])])
    return np.stack(tabs)


def _bias_table(rel_bias, buckets):
    def body(rb_ref, bk_ref, out_ref):
        g = pl.program_id(0)
        bk = bk_ref[...]
        for h in range(NH):
            acc = jnp.full((BLK, 2 * BLK), NEG, F32)
            for b in range(NUM_BUCKETS):
                acc = jnp.where(bk == b, rb_ref[b, g * NH + h], acc)
            out_ref[h] = acc

    return pl.pallas_call(
        body, name="bias_table", grid=(NG,),
        in_specs=[pl.BlockSpec(memory_space=pltpu.SMEM),
                  pl.BlockSpec((None, BLK, 2 * BLK), lambda g: (g, 0, 0))],
        out_specs=pl.BlockSpec((NH, BLK, 2 * BLK), lambda g: (g, 0, 0)),
        out_shape=_sds((NG * NH, BLK, 2 * BLK)),
        compiler_params=_params("arbitrary"),
    )(rel_bias, buckets)


def _bias_grad(ds_acc, buckets, ride):
    def body(acc_ref, bk_ref, out_ref):
        bk = bk_ref[...]
        acc = acc_ref[...]
        lane = lax.broadcasted_iota(jnp.int32, (8, 128), 1)
        out = jnp.zeros((8, 128), F32)
        for b in range(NUM_BUCKETS):
            val = jnp.sum(jnp.where(bk == b, acc, 0.0))
            out = jnp.where(lane == b, val, out)
        out_ref[...] = out

    (out,), rode = _call_with_ride(
        body, ride, lambda: pl.program_id(0) == 0, lambda: pl.program_id(0) == NG * NH - 1,
        name="bias_grad", grid=(NG * NH,),
        in_specs=[pl.BlockSpec((None, BLK, 2 * BLK), lambda gh: (gh, 0, 0)),
                  pl.BlockSpec((None, BLK, 2 * BLK), lambda gh: (gh // NH, 0, 0))],
        out_specs=[pl.BlockSpec((None, 8, 128), lambda gh: (gh, 0, 0))],
        out_shape=[_sds((NG * NH, 8, 128))],
        compiler_params=_params("arbitrary"),
    )(ds_acc, buckets)
    return out, rode


def _mod_partial(c_all, w_ada_s, b_ada_s):
    def body(c_ref, w_ref, b_ref, o_ref):
        o_ref[...] = _dot(c_ref[...].astype(BF16), w_ref[...].astype(BF16)) + b_ref[...]

    return pl.pallas_call(body, name="mod_partial", out_shape=_sds((8, w_ada_s.shape[1])),
                          compiler_params=_params())(c_all, w_ada_s, b_ada_s)


def _prenorm(x, norm_g, mod):
    S = x.shape[0]
    tm = 512

    def body(x_ref, g_ref, mod_ref, h_ref):
        xv = x_ref[...]
        r = lax.rsqrt(jnp.mean(xv * xv, axis=-1, keepdims=True) + EPS)
        n1 = xv * r * g_ref[...]
        h_ref[...] = (n1 * (1.0 + mod_ref[:, D:2 * D]) + mod_ref[:, 0:D]).astype(BF16)

    return pl.pallas_call(
        body, name="prenorm", grid=(S // tm,),
        in_specs=[pl.BlockSpec((tm, D), lambda i: (i, 0)), pl.BlockSpec((1, D), lambda i: (0, 0)),
                  pl.BlockSpec((1, 3 * D), lambda i: (0, 0))],
        out_specs=pl.BlockSpec((tm, D), lambda i: (i, 0)),
        out_shape=_sds((S, D), BF16), compiler_params=_params("parallel"),
    )(x, norm_g, mod)


def _proj(h, wg_in, j0, nj, dtype, name):
    S = h.shape[0]
    tm = 2048
    per = wg_in.shape[2] // CB

    def body(h_ref, w_ref, o_ref):
        o_ref[...] = _dot(h_ref[...], w_ref[...]).astype(dtype)

    return pl.pallas_call(
        body, name=name, grid=(S // tm, nj),
        in_specs=[pl.BlockSpec((tm, D), lambda m, j: (m, 0)),
                  pl.BlockSpec((None, D, CB), lambda m, j: ((j0 + j) // per, 0, (j0 + j) % per))],
        out_specs=pl.BlockSpec((tm, CB), lambda m, j: (m, j)),
        out_shape=_sds((S, nj * CB), dtype), compiler_params=_params("parallel", "parallel"),
    )(h, wg_in)


HS = 4
SLAB = HS * HD


def _lane_head(rows):
    return lax.broadcasted_iota(jnp.int32, (rows, SLAB), 1) // HD


def _head_stack(a):
    head = _lane_head(a.shape[0])
    return jnp.concatenate([jnp.where(head == h, a, jnp.zeros_like(a)) for h in range(HS)], axis=0)


def _head_unstack(a):
    rows = a.shape[0] // HS
    head = _lane_head(rows)
    out = a[:rows]
    for h in range(1, HS):
        out = jnp.where(head == h, a[h * rows:(h + 1) * rows], out)
    return out


STAT_W = 128
VIEW = 16


def _sub_layout(dil):
    if dil == 1:
        return BLK, [None]
    return BLK * dil // VIEW, [[r + dil * u for u in range(VIEW // dil)] for r in range(dil)]


def _block_perm(dil):
    a_rows, _ = _sub_layout(dil)
    p = np.arange(BLK)
    return p if dil == 1 else (VIEW // dil) * (p % a_rows) + p // a_rows


LB = 128
N_SLAB = NH // HS


def _ld(refs, bs, s, w):
    if bs is None:
        return refs[0][:, s * w:(s + 1) * w]
    a_rows = refs[0].shape[0] // VIEW
    return jnp.concatenate([jnp.concatenate([ref[pl.ds(b, a_rows, stride=VIEW), :] for b in bs], axis=0)
                            for ref in refs], axis=1)


def _st(ref, bs, s, val):
    if bs is None:
        ref[:, s * SLAB:(s + 1) * SLAB] = val
        return
    a_rows = val.shape[0] // len(bs)
    for u, b in enumerate(bs):
        ref[:, b, :] = val[u * a_rows:(u + 1) * a_rows]


def _attn_views(dil, S):
    a_rows, subs = _sub_layout(dil)
    if dil == 1:
        def ispecs(base, w, f):
            return [pl.BlockSpec((BLK, N_SLAB * w), lambda sg, n: (f(n), base // (N_SLAB * w)))]
        return subs, S // BLK, N_SLAB, ispecs, (lambda w: (S, w)), (
            lambda f: pl.BlockSpec((BLK, AW), lambda sg, n: (f(n), 0)))

    def ispecs(base, w, f):
        return [pl.BlockSpec((a_rows * VIEW, LB), lambda sg, n, k=k: (f(n), (base + sg * w) // LB + k))
                for k in range(w // LB)]
    return subs, S // (a_rows * VIEW), 1, ispecs, (lambda w: (S // VIEW, VIEW, w)), (
        lambda f: pl.BlockSpec((a_rows, VIEW, SLAB), lambda sg, n: (f(n), 0, sg)))


def _attn_fwd(qkv_g, bias_tab, g):
    S = qkv_g.shape[0]
    subs, nbq, sps, ispecs, shape, ospec = _attn_views(GROUPS[g][1], S)
    cur, prev = (lambda n: n), (lambda n: jnp.maximum(n - 1, 0))
    in_specs = [ispecs(0, SLAB, cur), ispecs(AW, SLAB, prev), ispecs(AW, SLAB, cur), ispecs(2 * AW, SLAB, prev),
                ispecs(2 * AW, SLAB, cur)]
    nl = len(in_specs[0])

    def body(*refs):
        q, kp, kc, vp, vc = (refs[t * nl:(t + 1) * nl] for t in range(5))
        b_ref, o_ref, l_ref = refs[5 * nl:]
        n = pl.program_id(1)
        col = lax.broadcasted_iota(jnp.int32, (HS * BLK, 2 * BLK), 1)
        keep = (col >= BLK) | (n > 0)
        for s_ in range(sps):
            bias = b_ref[pl.ds(s_ * HS, HS)].reshape(HS * BLK, 2 * BLK)
            for bs in subs:
                kb = jnp.concatenate([_ld(kp, bs, s_, SLAB), _ld(kc, bs, s_, SLAB)], axis=0).astype(BF16)
                vb = jnp.concatenate([_ld(vp, bs, s_, SLAB), _ld(vc, bs, s_, SLAB)], axis=0).astype(BF16)
                s = _dot_nt(_head_stack(_ld(q, bs, s_, SLAB).astype(BF16)), kb) * (HD ** -0.5) + bias
                s = jnp.where(keep, s, NEG)
                m = jnp.max(s, axis=-1, keepdims=True)
                p = jnp.exp(s - m)
                den = jnp.sum(p, axis=-1, keepdims=True)
                _st(o_ref, bs, s_, _head_unstack(_dot(p.astype(BF16), vb) / den))
                _st(l_ref, bs, s_, _head_unstack(jnp.broadcast_to(m + jnp.log(den), (HS * BLK, SLAB))))

    out = _sds(shape(AW))
    nsg = N_SLAB // sps
    o, l = pl.pallas_call(
        body, name=f"attn_fwd{g}", grid=(nsg, nbq),
        in_specs=sum(in_specs, []) + [pl.BlockSpec((sps * HS, BLK, 2 * BLK), lambda sg, n: (g * nsg + sg, 0, 0))],
        out_specs=[ospec(cur), ospec(cur)],
        out_shape=[out, out], compiler_params=_params("parallel", "arbitrary"),
    )(*([qkv_g] * (5 * nl)), bias_tab)
    return o.reshape(S, AW), l.reshape(S, AW)


def _attn_bwd(qkv_g, dattn, stats, bias_tab, g, ride):
    S = qkv_g.shape[0]
    subs, nbq, sps, ispecs, shape, ospec = _attn_views(GROUPS[g][1], S)
    cur = lambda n: jnp.minimum(n, nbq - 1)
    prev = lambda n: jnp.clip(n - 1, 0, nbq - 1)
    late = lambda n: jnp.maximum(n - 1, 0)
    in_specs = [ispecs(0, SLAB, cur), ispecs(AW, SLAB, prev), ispecs(AW, SLAB, cur), ispecs(2 * AW, SLAB, prev),
                ispecs(2 * AW, SLAB, cur), ispecs(0, SLAB, cur), ispecs(0, STAT_W, cur)]
    nl = len(in_specs[0])

    def body(*refs):
        q, kp, kc, vp, vc, da = (refs[t * nl:(t + 1) * nl] for t in range(6))
        st_ref, b_ref, dq_ref, dk_ref, dv_ref, ds_ref, ck_ref, cv_ref = refs[6 * nl:]
        n = pl.program_id(1)

        @pl.when(n == 0)
        def _():
            ds_ref[...] = jnp.zeros_like(ds_ref)
            ck_ref[...] = jnp.zeros_like(ck_ref)
            cv_ref[...] = jnp.zeros_like(cv_ref)

        @pl.when(n < nbq)
        def _():
            col = lax.broadcasted_iota(jnp.int32, (HS * BLK, 2 * BLK), 1)
            keep = (col >= BLK) | (n > 0)
            for s_ in range(sps):
                cs = slice(s_ * SLAB, (s_ + 1) * SLAB)
                bias = b_ref[pl.ds(s_ * HS, HS)].reshape(HS * BLK, 2 * BLK)
                for i, bs in enumerate(subs):
                    st = _ld((st_ref,), bs, s_, STAT_W)
                    kb = jnp.concatenate([_ld(kp, bs, s_, SLAB), _ld(kc, bs, s_, SLAB)], axis=0).astype(BF16)
                    vb = jnp.concatenate([_ld(vp, bs, s_, SLAB), _ld(vc, bs, s_, SLAB)], axis=0).astype(BF16)
                    lse = jnp.concatenate([st[:, h:h + 1] for h in range(HS)], axis=0)
                    delta = jnp.concatenate([st[:, HS + h:HS + h + 1] for h in range(HS)], axis=0)
                    qs = _head_stack(_ld(q, bs, s_, SLAB).astype(BF16))
                    dos = _head_stack(_ld(da, bs, s_, SLAB).astype(BF16))
                    s = _dot_nt(qs, kb) * (HD ** -0.5) + bias
                    s = jnp.where(keep, s, NEG)
                    p = jnp.exp(s - lse)
                    ds = p * (_dot_nt(dos, vb) - delta)
                    ds_ref[pl.ds(s_ * HS, HS)] += ds.reshape(HS, BLK, 2 * BLK)
                    ds_b = (ds * (HD ** -0.5)).astype(BF16)
                    _st(dq_ref, bs, s_, _head_unstack(_dot(ds_b, kb)))
                    dkb = _dot_tn(ds_b, qs)
                    dvb = _dot_tn(p.astype(BF16), dos)
                    _st(dk_ref, bs, s_, ck_ref[i, :, cs] + dkb[:BLK])
                    _st(dv_ref, bs, s_, cv_ref[i, :, cs] + dvb[:BLK])
                    ck_ref[i, :, cs] = dkb[BLK:]
                    cv_ref[i, :, cs] = dvb[BLK:]

        @pl.when(n == nbq)
        def _():
            for s_ in range(sps):
                for i, bs in enumerate(subs):
                    _st(dk_ref, bs, s_, ck_ref[i, :, s_ * SLAB:(s_ + 1) * SLAB])
                    _st(dv_ref, bs, s_, cv_ref[i, :, s_ * SLAB:(s_ + 1) * SLAB])

    out = _sds(shape(AW))
    nsg = N_SLAB // sps
    (dq, dk, dv, ds_acc), rode = _call_with_ride(
        body, ride, lambda: (pl.program_id(0) == 0) & (pl.program_id(1) == 0),
        lambda: (pl.program_id(0) == nsg - 1) & (pl.program_id(1) == nbq),
        name=f"attn_bwd{g}", grid=(nsg, nbq + 1),
        in_specs=sum(in_specs, []) + [pl.BlockSpec((sps * HS, BLK, 2 * BLK), lambda sg, n: (g * nsg + sg, 0, 0))],
        out_specs=[ospec(cur), ospec(late), ospec(late),
                   pl.BlockSpec((sps * HS, BLK, 2 * BLK), lambda sg, n: (sg, 0, 0))],
        out_shape=[out] * 3 + [_sds((NH, BLK, 2 * BLK))],
        scratch_shapes=[pltpu.VMEM((len(subs), BLK, sps * SLAB), F32), pltpu.VMEM((len(subs), BLK, sps * SLAB), F32)],
        compiler_params=_params("arbitrary", "arbitrary"),
    )(*([qkv_g] * (5 * nl)), *([dattn] * nl), stats, bias_tab)
    return [dq.reshape(S, AW), dk.reshape(S, AW), dv.reshape(S, AW)], ds_acc, rode


TM_MIX = 256


def _mix_specs(tm):
    row512 = pl.BlockSpec((tm, AW), lambda i: (i, 0))
    return ([row512] * 6 + [
        pl.BlockSpec((tm, REST_W), lambda i: (i, 0)),
        pl.BlockSpec((HALO, AW), lambda i: (jnp.maximum(i * (tm // HALO) - 1, 0), 1)),
        pl.BlockSpec((AW, D), lambda i: (0, 0)), pl.BlockSpec((AW, D), lambda i: (0, 0)),
        pl.BlockSpec((4, PGW, PGW), lambda i: (0, 0, 0)), pl.BlockSpec((1, AW), lambda i: (0, 0))])


def _mix_forward(i, tm, o_refs, l_refs, rest_ref, halo_ref, wab_ref, wpb_ref, pw_ref, ps_ref):
    l0, l1, l2 = (r[...] for r in l_refs)
    mx = jnp.maximum(jnp.maximum(l0, l1), l2)
    e0, e1, e2 = jnp.exp(l0 - mx), jnp.exp(l1 - mx), jnp.exp(l2 - mx)
    den = e0 + e1 + e2
    lj = mx + jnp.log(den)
    attn = (e0 * o_refs[0][...] + e1 * o_refs[1][...] + e2 * o_refs[2][...]) / den

    z_attn = rest_ref[:, 0:AW]
    u = rest_ref[:, AW:2 * AW]
    z_pool = rest_ref[:, 2 * AW:3 * AW]
    g_attn = rest_ref[:, 3 * AW:3 * AW + D]
    g_pool = rest_ref[:, 3 * AW + D:3 * AW + 2 * D]

    sg_a = _sigmoid(z_attn)
    sil_a = z_attn * sg_a
    a_g = (attn * sil_a).astype(BF16)
    y_attn = _dot(a_g, wab_ref[...])

    halo = jnp.where(i > 0, halo_ref[...], 0.0)
    ext = jnp.concatenate([halo, u], axis=0)
    t = i * tm + lax.broadcasted_iota(jnp.int32, (tm, 1), 0)
    pooled, mixed_raw = [], []
    for gi, win in enumerate(POOL_WINDOWS):
        s = ext[:, gi * PGW:(gi + 1) * PGW]
        sh = 1
        while sh < win:
            s = s + pltpu.roll(s, sh, 0)
            sh *= 2
        cnt = jnp.minimum(t + 1, win).astype(F32)
        pg = s[HALO:] / cnt - u[:, gi * PGW:(gi + 1) * PGW]
        pooled.append(pg.astype(BF16))
        mixed_raw.append(_dot(pooled[-1], pw_ref[gi].astype(BF16)))
    mixed_raw = jnp.concatenate(mixed_raw, axis=1)
    mixed = mixed_raw * ps_ref[...]
    sg_p = _sigmoid(z_pool)
    sil_p = z_pool * sg_p
    m_g = (mixed * sil_p).astype(BF16)
    y_pool = _dot(m_g, wpb_ref[...])

    sa = _sigmoid(g_attn)
    sp = _sigmoid(g_pool)
    merged = sa * y_attn + sp * y_pool
    return dict(lj=lj, attn=attn, z_attn=z_attn, z_pool=z_pool, sg_a=sg_a, sil_a=sil_a, a_g=a_g, y_attn=y_attn,
                pooled=pooled, mixed_raw=mixed_raw, mixed=mixed, sg_p=sg_p, sil_p=sil_p, m_g=m_g, y_pool=y_pool,
                sa=sa, sp=sp, merged=merged)


def _mix_step(x, target, os_, ls_, rest, wab, wpb, pool_w, pool_scale, wout, mod, final_g):
    S = x.shape[0]
    tm = TM_MIX
    nt = S // tm
    sw = D // N_SHARD

    def body(o0, o1, o2, l0, l1, l2, rest_ref, halo_ref, wab_ref, wpb_ref, pw_ref, ps_ref,
             x_ref, t_ref, wo_ref, mod_ref, fg_ref, dx2_ref, loss_ref, dfg_ref, dgate_ref,
             dattn_ref, stats_ref, dpooled_ref, drest_ref, dwo_hbm, dwab_hbm, dwpb_hbm, dpw_ref, dps_ref,
             awo, awab, awpb):
        i = pl.program_id(0)

        @pl.when(i == 0)
        def _():
            for ref in (loss_ref, dfg_ref, dgate_ref, awo, awab, awpb, dpw_ref, dps_ref):
                ref[...] = jnp.zeros_like(ref)

        f = _mix_forward(i, tm, (o0, o1, o2), (l0, l1, l2), rest_ref, halo_ref, wab_ref, wpb_ref, pw_ref, ps_ref)
        mo = _dot(f["merged"].astype(BF16), wo_ref[...])
        gate = mod_ref[:, 2 * D:3 * D]
        fg = fg_ref[...]
        x2 = x_ref[...] + gate * mo
        r2 = lax.rsqrt(jnp.mean(x2 * x2, axis=-1, keepdims=True) + EPS)
        n2 = x2 * r2
        err = n2 * fg - t_ref[...]
        loss_ref[...] += 0.5 * jnp.sum(jnp.mean(err * err, axis=-1, keepdims=True))
        dy = err * (1.0 / D)
        dfg_ref[...] += jnp.sum(dy * n2, axis=0, keepdims=True)
        dn = dy * fg
        dx2 = r2 * (dn - n2 * jnp.mean(dn * n2, axis=-1, keepdims=True))
        dgate_ref[...] += jnp.sum(dx2 * mo, axis=0, keepdims=True)
        dx2_ref[...] = dx2

        dmo_b = (dx2 * gate).astype(BF16)
        dmerged = _dot_nt(dmo_b, wo_ref[...])
        awo[...] += _dot_tn(f["merged"].astype(BF16), dmo_b)
        sa, sp = f["sa"], f["sp"]
        dya = (dmerged * sa).astype(BF16)
        dyp = (dmerged * sp).astype(BF16)
        dg_attn = dmerged * f["y_attn"] * sa * (1.0 - sa)
        dg_pool = dmerged * f["y_pool"] * sp * (1.0 - sp)
        dag = _dot_nt(dya, wab_ref[...])
        awab[...] += _dot_tn(f["a_g"], dya)
        dmg = _dot_nt(dyp, wpb_ref[...])
        awpb[...] += _dot_tn(f["m_g"], dyp)
        dattn = dag * f["sil_a"]
        dattn_ref[...] = dattn
        prod = dattn * f["attn"]
        lane = lax.broadcasted_iota(jnp.int32, (tm, STAT_W), 1)
        for sb in range(N_SLAB):
            st = jnp.zeros((tm, STAT_W), F32)
            for h in range(HS):
                hs = slice((sb * HS + h) * HD, (sb * HS + h + 1) * HD)
                st = jnp.where(lane == h, f["lj"][:, hs.start:hs.start + 1], st)
                st = jnp.where(lane == HS + h, jnp.sum(prod[:, hs], axis=-1, keepdims=True), st)
            stats_ref[:, sb * STAT_W:(sb + 1) * STAT_W] = st
        dz_attn = dag * f["attn"] * (f["sg_a"] * (1.0 + f["z_attn"] * (1.0 - f["sg_a"])))
        dmixed = dmg * f["sil_p"]
        dz_pool = dmg * f["mixed"] * (f["sg_p"] * (1.0 + f["z_pool"] * (1.0 - f["sg_p"])))
        dps_ref[...] += jnp.sum(dmixed * f["mixed_raw"], axis=0, keepdims=True)
        dpm = (dmixed * ps_ref[...]).astype(BF16)
        for gi in range(len(POOL_WINDOWS)):
            cs = slice(gi * PGW, (gi + 1) * PGW)
            dpw_ref[gi] += _dot_tn(f["pooled"][gi], dpm[:, cs])
            dpooled_ref[:, cs] = _dot_nt(dpm[:, cs], pw_ref[gi].astype(BF16))
        drest_ref[:, 0:AW] = dz_attn.astype(BF16)
        drest_ref[:, AW:2 * AW] = jnp.zeros((tm, AW), BF16)
        drest_ref[:, 2 * AW:3 * AW] = dz_pool.astype(BF16)
        drest_ref[:, 3 * AW:3 * AW + D] = dg_attn.astype(BF16)
        drest_ref[:, 3 * AW + D:3 * AW + 2 * D] = dg_pool.astype(BF16)

        @pl.when(i == nt - 1)
        def _():
            pltpu.sync_copy(awo, dwo_hbm)
            for k in range(N_SHARD):
                pltpu.sync_copy(awab.at[:, pl.ds(k * sw, sw)], dwab_hbm.at[k])
                pltpu.sync_copy(awpb.at[:, pl.ds(k * sw, sw)], dwpb_hbm.at[k])

    row = pl.BlockSpec((tm, D), lambda i: (i, 0))
    vec = pl.BlockSpec((1, D), lambda i: (0, 0))
    row512 = pl.BlockSpec((tm, AW), lambda i: (i, 0))
    outs = pl.pallas_call(
        body, name="mix_step", grid=(nt,),
        in_specs=_mix_specs(tm) + [row, row, pl.BlockSpec((D, D), lambda i: (0, 0)),
                                   pl.BlockSpec((1, 3 * D), lambda i: (0, 0)), vec],
        out_specs=[row, pl.BlockSpec((8, 128), lambda i: (0, 0)), vec, vec,
                   row512, pl.BlockSpec((tm, N_SLAB * STAT_W), lambda i: (i, 0)), row512,
                   pl.BlockSpec((tm, REST_W), lambda i: (i, 0)), ANY, ANY, ANY,
                   pl.BlockSpec((4, PGW, PGW), lambda i: (0, 0, 0)), pl.BlockSpec((1, AW), lambda i: (0, 0))],
        out_shape=[_sds((S, D)), _sds((8, 128)), _sds((1, D)), _sds((1, D)),
                   _sds((S, AW)), _sds((S, N_SLAB * STAT_W)), _sds((S, AW)), _sds((S, REST_W), BF16),
                   _sds((D, D)), _sds((N_SHARD, AW, sw)), _sds((N_SHARD, AW, sw)), _sds((4, PGW, PGW)), _sds((1, AW))],
        scratch_shapes=[pltpu.VMEM((D, D), F32), pltpu.VMEM((AW, D), F32), pltpu.VMEM((AW, D), F32)],
        compiler_params=_params("arbitrary"),
    )(*os_, *ls_, rest, rest, wab, wpb, pool_w, pool_scale, x, target, wout, mod, final_g)
    dx2, loss, dfg, dgate, dattn, stats, dpooled, drest, dwo, dwab, dwpb, dpw, dps = outs
    return (dx2, loss, dfg, dgate, dattn, stats, dpooled, drest, dwo.reshape(N_SHARD, D // N_SHARD, D), dwab, dwpb,
            dpw, dps)


def _pool_bwd(dpooled):
    S = dpooled.shape[0]
    tm = 512
    nt = S // tm

    def body(dp_ref, nxt_ref, du_ref):
        i = pl.program_id(0)
        t = i * tm + lax.broadcasted_iota(jnp.int32, (tm + HALO, 1), 0)
        nxt = jnp.where(i < nt - 1, nxt_ref[...], 0.0)
        ext = jnp.concatenate([dp_ref[...], nxt], axis=0)
        for gi, win in enumerate(POOL_WINDOWS):
            cs = slice(gi * PGW, (gi + 1) * PGW)
            s = ext[:, cs] / jnp.minimum(t + 1, win).astype(F32)
            sh = 1
            while sh < win:
                s = s + pltpu.roll(s, tm + HALO - sh, 0)
                sh *= 2
            du_ref[:, cs] = (s[:tm] - dp_ref[:, cs]).astype(BF16)

    return pl.pallas_call(
        body, name="pool_bwd", grid=(nt,),
        in_specs=[pl.BlockSpec((tm, AW), lambda i: (i, 0)),
                  pl.BlockSpec((HALO, AW), lambda i: (jnp.minimum((i + 1) * (tm // HALO), S // HALO - 1), 0))],
        out_specs=pl.BlockSpec((tm, AW), lambda i: (i, 0)),
        out_shape=_sds((S, AW), BF16), compiler_params=_params("parallel"),
    )(dpooled, dpooled)


TB = 1024


def _dh(dproj, wg_in, ride):
    S = dproj.shape[0]
    per = wg_in.shape[2] // TB
    nm, nk = S // TB, IN_W // TB

    def body(dp_ref, w_ref, out_ref):
        @pl.when(pl.program_id(1) == 0)
        def _():
            out_ref[...] = jnp.zeros_like(out_ref)

        out_ref[...] += _dot_nt(dp_ref[...], w_ref[...])

    (dh,), rode = _call_with_ride(
        body, ride, lambda: (pl.program_id(0) == 0) & (pl.program_id(1) == 0),
        lambda: (pl.program_id(0) == nm - 1) & (pl.program_id(1) == nk - 1),
        name="dh", grid=(nm, nk),
        in_specs=[pl.BlockSpec((TB, TB), lambda m, kk: (m, kk)),
                  pl.BlockSpec((None, D, TB), lambda m, kk: (kk // per, 0, kk % per))],
        out_specs=[pl.BlockSpec((TB, D), lambda m, kk: (m, 0))],
        out_shape=[_sds((S, D))], compiler_params=_params("arbitrary", "arbitrary"),
    )(dproj, wg_in)
    return dh, rode


def _dw_in(h, dproj):
    S = dproj.shape[0]
    per = IN_W // N_SHARD // TB

    def body(h_ref, dp_ref, out_ref):
        @pl.when(pl.program_id(1) == 0)
        def _():
            out_ref[...] = jnp.zeros_like(out_ref)

        out_ref[...] += _dot_tn(h_ref[...], dp_ref[...])

    return pl.pallas_call(
        body, name="dw_in", grid=(IN_W // TB, S // TB),
        in_specs=[pl.BlockSpec((TB, D), lambda j, kk: (kk, 0)), pl.BlockSpec((TB, TB), lambda j, kk: (kk, j))],
        out_specs=pl.BlockSpec((None, D, TB), lambda j, kk: (j // per, 0, j % per)),
        out_shape=_sds((N_SHARD, D, IN_W // N_SHARD)), compiler_params=_params("parallel", "arbitrary"),
    )(h, dproj)


def _prenorm_bwd(x, dh, dx2, norm_g, mod):
    S = x.shape[0]
    tm = 512

    def body(x_ref, dh_ref, dx2_ref, g_ref, mod_ref, gx_ref, dg_ref, dshift_ref, dscale_ref):
        i = pl.program_id(0)

        @pl.when(i == 0)
        def _():
            dg_ref[...] = jnp.zeros_like(dg_ref)
            dshift_ref[...] = jnp.zeros_like(dshift_ref)
            dscale_ref[...] = jnp.zeros_like(dscale_ref)

        xv = x_ref[...]
        dhv = dh_ref[...]
        g = g_ref[...]
        r = lax.rsqrt(jnp.mean(xv * xv, axis=-1, keepdims=True) + EPS)
        xh = xv * r
        dshift_ref[...] += jnp.sum(dhv, axis=0, keepdims=True)
        dscale_ref[...] += jnp.sum(dhv * (xh * g), axis=0, keepdims=True)
        dn1 = dhv * (1.0 + mod_ref[:, D:2 * D])
        dg_ref[...] += jnp.sum(dn1 * xh, axis=0, keepdims=True)
        dxh = dn1 * g
        gx_ref[...] = dx2_ref[...] + r * (dxh - xh * jnp.mean(dxh * xh, axis=-1, keepdims=True))

    row = pl.BlockSpec((tm, D), lambda i: (i, 0))
    vec = pl.BlockSpec((1, D), lambda i: (0, 0))
    return pl.pallas_call(
        body, name="prenorm_bwd", grid=(S // tm,),
        in_specs=[row, row, row, vec, pl.BlockSpec((1, 3 * D), lambda i: (0, 0))],
        out_specs=[row, vec, vec, vec],
        out_shape=[_sds((S, D)), _sds((1, D)), _sds((1, D)), _sds((1, D))],
        compiler_params=_params("arbitrary"),
    )(x, dh, dx2, norm_g, mod)


def _local_step(x, target, mod, wg_in, wab, wpb, wout, pool_w, pool_scale, rel_bias, norm_g, final_g, chip_half):
    buckets = jnp.asarray(_bucket_tables())
    bias_tab = _bias_table(rel_bias, buckets)
    h = _prenorm(x, norm_g, mod)
    qkv = [_proj(h, wg_in, 3 * g, 3, F32, f"proj_qkv{g}") for g in range(NG)]
    rest = _proj(h, wg_in, NCB_QKV, REST_W // CB, F32, "proj_rest")
    os_, ls_ = zip(*[_attn_fwd(qkv[g], bias_tab, g) for g in range(NG)])
    (dx2, loss, dfinal_g, dgate, dattn, stats, dpooled, drest, dw_out, dw_ab, dw_pb, dpool_w,
     dpool_scale) = _mix_step(x, target, os_, ls_, rest, wab, wpb, pool_w, pool_scale, wout, mod, final_g)
    du = _pool_bwd(dpooled)

    small = [dw_ab, dw_pb, dw_out]
    dqkv0, ds0, sib_small = _attn_bwd(qkv[0], dattn, stats, bias_tab, 0, _ride_sibling_halves(small))
    p_small = [_pair_sum(g, t, chip_half, f"rs_pair_sum{a}") for a, (g, t) in enumerate(zip(small, sib_small))]
    dqkv1, ds1, u_small = _attn_bwd(qkv[1], dattn, stats, bias_tab, 1,
                                    _ride_chip_exchange([p16 for _, p16 in p_small]))
    rs_ab, rs_pb, rs_out = [_chip_sum(p32, u, chip_half, f"rs_chip_sum{a}")
                            for a, ((p32, _), u) in enumerate(zip(p_small, u_small))]
    dqkv2, ds2, _ = _attn_bwd(qkv[2], dattn, stats, bias_tab, 2, None)

    dproj = jnp.concatenate([a.astype(BF16) for a in dqkv0 + dqkv1 + dqkv2] + [drest[:, :AW], du, drest[:, 2 * AW:]],
                            axis=1)
    dw_in = _dw_in(h, dproj)
    drel_rows, (sib_in,) = _bias_grad(jnp.concatenate([ds0, ds1, ds2], axis=0), buckets,
                                      _ride_sibling_halves([dw_in]))
    drel = drel_rows[:, 0, :NUM_BUCKETS].T
    p32_in, p16_in = _pair_sum(dw_in, sib_in, chip_half, "rs_pair_sum_in")
    dh, (u_in,) = _dh(dproj, wg_in, _ride_chip_exchange([p16_in]))
    rs_in = _chip_sum(p32_in, u_in, chip_half, "rs_chip_sum_in")

    grad_x, dnorm_g, dshift, dscale = _prenorm_bwd(x, dh, dx2, norm_g, mod)
    dmod = jnp.concatenate([dshift, dscale, dgate], axis=1)
    return dict(loss=loss[0, 0], grad_x=grad_x, dmod=dmod, dnorm_g=dnorm_g, dfinal_g=dfinal_g, dpool_w=dpool_w,
                dpool_scale=dpool_scale, drel_bias=drel, dw_in=dw_in, dw_attn_br=dw_ab, dw_pool_br=dw_pb,
                dw_out=dw_out, rs_in=rs_in, rs_attn_br=rs_ab, rs_pool_br=rs_pb, rs_out=rs_out)


def _allgather8(blocks, name, relay=None):
    nb = len(blocks)
    relay = [False] * nb if relay is None else list(relay)

    def body(*refs):
        ins, outs = refs[:nb], refs[nb:2 * nb]
        send_sems, recv_sems = refs[2 * nb:]
        x, y, c = lax.axis_index("x"), lax.axis_index("y"), lax.axis_index("c")
        me, sibling = (x, y, c), (x, y, 1 - c)
        here, xn, yn, dg = (x, y), (1 - x, y), (x, 1 - y), (1 - x, 1 - y)

        def slot(a, chip, core, half=None):
            ref = outs[a].at[4 * chip[0] + 2 * chip[1] + core]
            if half is None:
                return ref
            r2 = ref.shape[0] // 2
            return ref.at[pl.ds(half * r2, r2)]

        def copy(a, k, dst, to, src=None):
            return pltpu.make_async_remote_copy(src_ref=dst if src is None else src, dst_ref=dst,
                                                send_sem=send_sems.at[a, k], recv_sem=recv_sems.at[a, k],
                                                device_id=to, device_id_type=MESH)

        def start(cps):
            for cp in cps:
                cp.start()
            return cps

        sent = []
        for a in range(nb):
            own = slot(a, here, c)
            sent += [copy(a, 0, own, sibling, src=ins[a]), copy(a, 1, own, (*xn, c), src=ins[a]),
                     copy(a, 2, own, (*yn, c), src=ins[a])]
            if not relay[a]:
                sent.append(copy(a, 3, own, (*dg, c), src=ins[a]))
        start(sent)
        for a in range(nb):
            copy(a, 2, slot(a, yn, c), me).wait_recv()
            sent += start([copy(a, 6, slot(a, yn, c), sibling)]
                          + ([copy(a, 3, slot(a, yn, c, 0), (*xn, c))] if relay[a] else []))
        for a in range(nb):
            copy(a, 1, slot(a, xn, c), me).wait_recv()
            sent += start([copy(a, 5, slot(a, xn, c), sibling)]
                          + ([copy(a, 4, slot(a, xn, c, 1), (*yn, c))] if relay[a] else []))
        for a in range(nb):
            for k, half in ((3, 0), (4, 1)) if relay[a] else ((3, None),):
                copy(a, k, slot(a, dg, c, half), me).wait_recv()
                sent += start([copy(a, 4 + k, slot(a, dg, c, half), sibling)])
        for a in range(nb):
            copy(a, 0, slot(a, here, 1 - c), me).wait_recv()
            copy(a, 5, slot(a, xn, 1 - c), me).wait_recv()
            copy(a, 6, slot(a, yn, 1 - c), me).wait_recv()
            for k, half in ((7, 0), (8, 1)) if relay[a] else ((7, None),):
                copy(a, k, slot(a, dg, 1 - c, half), me).wait_recv()
        for cp in sent:
            cp.wait_send()

    outs = pl.pallas_call(
        body, name=name, in_specs=[ANY] * nb, out_specs=[ANY] * nb,
        out_shape=[_sds((8,) + b.shape, b.dtype) for b in blocks],
        scratch_shapes=[_dma_sems(nb, 9), _dma_sems(nb, 9)],
    )(*blocks)
    return [_place_own(buf, b) for buf, b in zip(outs, blocks)]


def _place_own(buf, block):
    dev = 4 * lax.axis_index("x") + 2 * lax.axis_index("y") + lax.axis_index("c")
    return lax.dynamic_update_index_in_dim(buf, block, dev, 0)


def _ride_sibling_halves(gs):
    def copies(ins, outs, send_sems, recv_sems):
        x, y, c = lax.axis_index("x"), lax.axis_index("y"), lax.axis_index("c")
        cps = []
        for a in range(len(gs)):
            r2 = ins[a].shape[1] // 2
            other = ins[a].at[:, pl.ds((1 - c) * r2, r2), :]
            cps.append(pltpu.make_async_remote_copy(src_ref=other, dst_ref=outs[a], send_sem=send_sems.at[a],
                                                    recv_sem=recv_sems.at[a], device_id=(x, y, 1 - c),
                                                    device_id_type=MESH))
        return cps

    return _Ride(gs, [_sds((g.shape[0], g.shape[1] // 2, g.shape[2]), g.dtype) for g in gs], len(gs), copies)


def _pair_sum(g, t, chip_half, name):
    nsh, rows, cols = g.shape
    r2 = rows // 2
    tr = _row_tile(r2, cols)
    nt = r2 // tr

    def body(ch_ref, g_ref, t_ref, p32_ref, p16_ref):
        p = g_ref[...] + t_ref[...]
        p16_ref[...] = p.astype(BF16)

        @pl.when(pl.program_id(1) == ch_ref[0])
        def _():
            p32_ref[...] = p

    blk = pl.BlockSpec((None, tr, cols), lambda i, k, ch_ref: (k, i, 0))
    return pl.pallas_call(
        body, name=name,
        grid_spec=pltpu.PrefetchScalarGridSpec(
            num_scalar_prefetch=1, grid=(nt, nsh),
            in_specs=[pl.BlockSpec((None, tr, cols), lambda i, k, ch_ref: (k, ch_ref[1] * nt + i, 0)), blk],
            out_specs=[pl.BlockSpec((tr, cols), lambda i, k, ch_ref: (i, 0)), blk]),
        out_shape=[_sds((r2, cols)), _sds((nsh, r2, cols), BF16)],
        compiler_params=_params("parallel", "arbitrary"),
    )(chip_half, g, t)


def _ride_chip_exchange(ps):
    def copies(ins, outs, send_sems, recv_sems):
        x, y, c = lax.axis_index("x"), lax.axis_index("y"), lax.axis_index("c")
        chips = [(1 - x, y), (x, 1 - y), (1 - x, 1 - y)]
        cps = []
        for a in range(len(ps)):
            for j, (ox, oy) in enumerate(chips):
                cps.append(pltpu.make_async_remote_copy(src_ref=ins[a].at[2 * ox + oy], dst_ref=outs[a].at[j],
                                                        send_sem=send_sems.at[3 * a + j],
                                                        recv_sem=recv_sems.at[3 * a + j],
                                                        device_id=(ox, oy, c), device_id_type=MESH))
        return cps

    return _Ride(ps, [_sds((3,) + p.shape[1:], p.dtype) for p in ps], 3 * len(ps), copies)


def _chip_sum(p32, u, chip_half, name):
    r2, cols = p32.shape
    tr = _row_tile(r2, cols)
    nt = r2 // tr

    def body(ch_ref, p_ref, u_ref, o_ref):
        acc = p_ref[...]
        for j in range(3):
            acc = acc + u_ref[j].astype(F32)
        o_ref[...] = acc

    return pl.pallas_call(
        body, name=name,
        grid_spec=pltpu.PrefetchScalarGridSpec(
            num_scalar_prefetch=1, grid=(nt,),
            in_specs=[pl.BlockSpec((tr, cols), lambda i, ch_ref: (i, 0)),
                      pl.BlockSpec((3, tr, cols), lambda i, ch_ref: (0, i, 0))],
            out_specs=pl.BlockSpec((tr, cols), lambda i, ch_ref: (ch_ref[1] * nt + i, 0))),
        out_shape=_sds((2 * r2, cols)), compiler_params=_params("parallel"),
    )(chip_half, p32, u)


def _sibling_join(fs, name):
    nb = len(fs)

    def body(*refs):
        outs = refs[nb:2 * nb]
        send_sems, recv_sems = refs[2 * nb:]
        x, y, c = lax.axis_index("x"), lax.axis_index("y"), lax.axis_index("c")
        cps = []
        for a in range(nb):
            r2 = outs[a].shape[0] // 2
            rows = outs[a].at[pl.ds(c * r2, r2), :]
            cps.append(pltpu.make_async_remote_copy(src_ref=rows, dst_ref=rows, send_sem=send_sems.at[a],
                                                    recv_sem=recv_sems.at[a], device_id=(x, y, 1 - c),
                                                    device_id_type=MESH))
        for cp in cps:
            cp.start()
        for cp in cps:
            cp.wait()

    return pl.pallas_call(
        body, name=name, in_specs=[ANY] * nb, out_specs=[ANY] * nb,
        out_shape=[_sds(f.shape, f.dtype) for f in fs],
        input_output_aliases={a: a for a in range(nb)},
        scratch_shapes=[_dma_sems(nb), _dma_sems(nb)],
    )(*fs)


def _row_tile(rows, cols):
    tile = rows
    while tile * cols * 4 > (1 << 20) and tile % 16 == 0:
        tile //= 2
    return tile


def _w_ada_grad(c_all, dmod_cols):
    def body(c_ref, d_ref, o_ref):
        o_ref[...] = _dot_tn(c_ref[...].astype(BF16), d_ref[...].astype(BF16))

    return pl.pallas_call(body, name="w_ada_grad", out_shape=_sds((c_all.shape[1], dmod_cols.shape[1])),
                          compiler_params=_params())(c_all, dmod_cols)


def _adam_math(w, g, m, v):
    nm = ADAM_B1 * m + (1.0 - ADAM_B1) * g
    nv = ADAM_B2 * v + (1.0 - ADAM_B2) * (g * g)
    m_hat = nm / (1.0 - ADAM_B1 ** ADAM_STEP)
    v_hat = nv / (1.0 - ADAM_B2 ** ADAM_STEP)
    return -ADAM_LR * (m_hat / (jnp.sqrt(v_hat) + ADAM_EPS) + ADAM_WD * w), nm, nv


def _adamw(w, g, m, v, name):
    rows, cols = w.shape
    tr = _row_tile(rows, cols)

    def body(w_ref, g_ref, m_ref, v_ref, go_ref, d_ref, nm_ref, nv_ref):
        gv = g_ref[...]
        go_ref[...] = gv
        d_ref[...], nm_ref[...], nv_ref[...] = _adam_math(w_ref[...], gv, m_ref[...], v_ref[...])

    spec = pl.BlockSpec((tr, cols), lambda i: (i, 0))
    return pl.pallas_call(
        body, name=name, grid=(rows // tr,), in_specs=[spec] * 4, out_specs=[spec] * 4,
        out_shape=[_sds((rows, cols))] * 4, compiler_params=_params("parallel"),
    )(w, g, m, v)


def _pack_small(dmod, dnorm_g, dfinal_g, dpool_scale, drel_bias, loss, dpool_w):
    return jnp.concatenate([dmod.reshape(-1, 128), dnorm_g.reshape(-1, 128), dfinal_g.reshape(-1, 128),
                            jnp.pad(dpool_scale.reshape(-1, 128), ((0, PK_RELB - PK_PSCALE - AW // 128), (0, 0))),
                            jnp.pad(drel_bias, ((0, 0), (0, 128 - NG * NH))),
                            jnp.full((PK_POOLW - PK_LOSS, 128), loss, F32), dpool_w.reshape(-1, 128)], axis=0)


def _small_update(small_all, ws, ms, vs):
    lane_rows = [(r0, r0 + w.shape[1] // 128) for r0, w in zip((PK_BADA, PK_NORMG, PK_FINALG, PK_PSCALE), ws)]
    nw = len(ws)

    def body(all_ref, *refs):
        w_refs, m_refs, v_refs = refs[:nw], refs[nw:2 * nw], refs[2 * nw:3 * nw]
        loss_ref, outs = refs[3 * nw], refs[3 * nw + 1:]
        g = all_ref[0]
        for s in range(1, all_ref.shape[0]):
            g = g + all_ref[s]
        loss_ref[...] = jnp.broadcast_to(g[PK_LOSS:PK_LOSS + 1, :], loss_ref.shape)

        def put(p, at, gv):
            d, nm, nv = _adam_math(w_refs[p][at], gv, m_refs[p][at], v_refs[p][at])
            for o_ref, val in zip(outs[4 * p:4 * p + 4], (gv, d, nm, nv)):
                o_ref[at] = val

        for p, (r0, r1) in enumerate(lane_rows):
            for i in range(r1 - r0):
                put(p, (slice(None), slice(128 * i, 128 * (i + 1))), g[r0 + i:r0 + i + 1, :])
        put(4, (slice(None), slice(None)), g[PK_RELB:PK_LOSS, 0:NG * NH])
        put(5, (slice(None), slice(None)), g[PK_POOLW:PK_ROWS, :])

    res = pl.pallas_call(
        body, name="small_update",
        out_shape=[_sds((8, 128))] + [_sds(w.shape) for w in ws for _ in range(4)], compiler_params=_params(),
    )(small_all, *ws, *ms, *vs)
    return res[0], [res[1 + 4 * p:5 + 4 * p] for p in range(nw)]


def kernel(x, c, norm_g, w_ada, b_ada, w_in, pool_w, pool_scale, w_attn_br, w_pool_br, w_out, rel_bias, final_g, loss_target, m_norm_g, m_w_ada, m_b_ada, m_w_in, m_pool_w, m_pool_scale, m_w_attn_br, m_w_pool_br, m_w_out, m_rel_bias, m_final_g, v_norm_g, v_w_ada, v_b_ada, v_w_in, v_pool_w, v_pool_scale, v_w_attn_br, v_w_pool_br, v_w_out, v_rel_bias, v_final_g):
    ix, iy, ic = lax.axis_index("x"), lax.axis_index("y"), lax.axis_index("c")
    dev = 4 * ix + 2 * iy + ic
    chip = 2 * ix + iy

    def half(w):
        r2 = w.shape[0] // 2
        return lax.dynamic_slice_in_dim(w, ic * r2, r2, axis=0).astype(BF16)

    gathered = _allgather8([jnp.broadcast_to(c, (8, D)), half(w_in[0]), half(w_attn_br[0]), half(w_pool_br[0]),
                            half(w_out[0])], "gather_weights", relay=[False, True, True, True, True])
    c_all = gathered[0][:, 0, :]
    wg_in = gathered[1].reshape(N_SHARD, D, IN_W // N_SHARD)
    wab = gathered[2].reshape(N_SHARD, AW, D // N_SHARD).transpose(1, 0, 2).reshape(AW, D)
    wpb = gathered[3].reshape(N_SHARD, AW, D // N_SHARD).transpose(1, 0, 2).reshape(AW, D)
    wout = gathered[4].reshape(D, D)

    mw = 3 * D // N_SHARD
    modp = _mod_partial(c_all, w_ada[0], lax.dynamic_slice_in_dim(b_ada, chip * mw, mw, axis=1))
    mod_all = _allgather8([modp], "gather_mod")[0]
    mod_full = mod_all[::2].transpose(1, 0, 2).reshape(8, 3 * D)
    mod = lax.dynamic_slice_in_dim(mod_full, dev, 1, axis=0)

    chip_half = jnp.stack([chip, ic]).astype(jnp.int32)
    r = _local_step(x[0], loss_target[0], mod, wg_in, wab, wpb, wout, pool_w[0], pool_scale, rel_bias, norm_g,
                    final_g.reshape(1, D), chip_half)

    packed = _pack_small(r["dmod"], r["dnorm_g"], r["dfinal_g"], r["dpool_scale"], r["drel_bias"], r["loss"],
                         r["dpool_w"])
    small_all = _allgather8([packed], "gather_small")[0]
    small = ["b_ada", "norm_g", "final_g", "pool_scale", "rel_bias", "pool_w"]
    shaped = lambda b, n, f, ps, rb, pw: [b, n, f.reshape(1, D), ps, rb, pw.reshape(4 * PGW, PGW)]
    loss, small_out = _small_update(small_all, shaped(b_ada, norm_g, final_g, pool_scale, rel_bias, pool_w),
                                    shaped(m_b_ada, m_norm_g, m_final_g, m_pool_scale, m_rel_bias, m_pool_w),
                                    shaped(v_b_ada, v_norm_g, v_final_g, v_pool_scale, v_rel_bias, v_pool_w))
    dmod_all = small_all[:, PK_BADA:PK_NORMG, :].reshape(8, 3 * D)
    g_w_ada = _w_ada_grad(c_all, lax.dynamic_slice_in_dim(dmod_all, chip * mw, mw, axis=1))

    g_w_in, g_w_ab, g_w_pb, g_w_out = _sibling_join([r["rs_in"], r["rs_attn_br"], r["rs_pool_br"], r["rs_out"]],
                                                    "rs_sibling_join")
    upd = dict(zip(small, small_out))
    upd["final_g"] = [a.reshape(D) for a in upd["final_g"]]
    upd["pool_w"] = [a.reshape(1, 4, PGW, PGW) for a in upd["pool_w"]]
    for nme, w, g, m, v in (("w_ada", w_ada, g_w_ada, m_w_ada, v_w_ada), ("w_in", w_in, g_w_in, m_w_in, v_w_in),
                            ("w_attn_br", w_attn_br, g_w_ab, m_w_attn_br, v_w_attn_br),
                            ("w_pool_br", w_pool_br, g_w_pb, m_w_pool_br, v_w_pool_br),
                            ("w_out", w_out, g_w_out, m_w_out, v_w_out)):
        upd[nme] = [a[None] for a in _adamw(w[0], g, m[0], v[0], "adamw_" + nme)]
    names = ["norm_g", "w_ada", "b_ada", "w_in", "pool_w", "pool_scale", "w_attn_br", "w_pool_br", "w_out",
             "rel_bias", "final_g"]
    return (loss[0, 0], r["grad_x"][None]) + tuple(upd[nme][kind] for kind in range(4) for nme in names)
```

```python
import functools
import math

import numpy as np
import jax
import jax.numpy as jnp
from jax import lax
from jax.experimental import pallas as pl
from jax.experimental.pallas import tpu as pltpu

F32 = jnp.float32
BF16 = jnp.bfloat16

D = 1024
HD = 64
NH = 8
AW = NH * HD
GROUPS = ((128, 1), (512, 4), (2048, 16))
NG = len(GROUPS)
BLK = 128
GW = 3 * AW
QKV_W = NG * GW
REST_W = 3584
IN_W = QKV_W + REST_W
CB = 512
NCB = IN_W // CB
NCB_QKV = QKV_W // CB
POOL_WINDOWS = (2, 4, 8, 16)
PGW = 128
HALO = 16
NUM_BUCKETS = 32
MAX_DISTANCE = 2048
EPS = 1e-6
NEG = -1e30
N_SHARD = 4
VMEM_LIMIT = 56 * 1024 * 1024

ADAM_LR = 0.001
ADAM_B1 = 0.9
ADAM_B2 = 0.999
ADAM_EPS = 1e-08
ADAM_WD = 0.01
ADAM_STEP = 10

PK_BADA, PK_NORMG, PK_FINALG, PK_PSCALE, PK_RELB, PK_LOSS, PK_POOLW, PK_ROWS = 0, 24, 32, 40, 48, 80, 88, 600

ANY = pl.BlockSpec(memory_space=pl.ANY)
MESH = pl.DeviceIdType.MESH


def _params(*sem):
    return pltpu.CompilerParams(dimension_semantics=sem, vmem_limit_bytes=VMEM_LIMIT)


def _sds(shape, dtype=F32):
    return jax.ShapeDtypeStruct(shape, dtype)


def _dot(a, b):
    return jnp.dot(a, b, preferred_element_type=F32)


def _dot_nt(a, b):
    return lax.dot_general(a, b, (((1,), (1,)), ((), ())), preferred_element_type=F32)


def _dot_tn(a, b):
    return lax.dot_general(a, b, (((0,), (0,)), ((), ())), preferred_element_type=F32)


def _sigmoid(z):
    return 0.5 * jnp.tanh(0.5 * z) + 0.5


def _dma_sems(*shape):
    return pltpu.SemaphoreType.DMA(shape)


class _Ride:
    def __init__(self, arrays, out_shapes, n_copies, copies):
        self.arrays, self.out_shapes, self.n_copies, self.copies = list(arrays), list(out_shapes), n_copies, copies


def _call_with_ride(body, ride, first, last, *, in_specs, out_specs, out_shape, scratch_shapes=(), **kw):
    in_specs, out_specs, out_shape, scratch_shapes = list(in_specs), list(out_specs), list(out_shape), list(scratch_shapes)
    n_in, n_out, n_sc = len(in_specs), len(out_specs), len(scratch_shapes)
    if ride is None:
        def run_plain(*operands):
            return pl.pallas_call(body, in_specs=in_specs, out_specs=out_specs, out_shape=out_shape,
                                  scratch_shapes=scratch_shapes, **kw)(*operands), []
        return run_plain
    n_ri, n_ro = len(ride.arrays), len(ride.out_shapes)

    def wrapped(*refs):
        ins, rest = refs[:n_in], refs[n_in:]
        r_ins, rest = rest[:n_ri], rest[n_ri:]
        outs, rest = rest[:n_out], rest[n_out:]
        r_outs, rest = rest[:n_ro], rest[n_ro:]
        scratch, (send_sems, recv_sems) = rest[:n_sc], rest[n_sc:]

        @pl.when(first())
        def _():
            for cp in ride.copies(r_ins, r_outs, send_sems, recv_sems):
                cp.start()

        body(*ins, *outs, *scratch)

        @pl.when(last())
        def _():
            for cp in ride.copies(r_ins, r_outs, send_sems, recv_sems):
                cp.wait()

    def run(*operands):
        res = pl.pallas_call(
            wrapped, in_specs=in_specs + [ANY] * n_ri, out_specs=out_specs + [ANY] * n_ro,
            out_shape=out_shape + ride.out_shapes,
            scratch_shapes=scratch_shapes + [_dma_sems(ride.n_copies), _dma_sems(ride.n_copies)], **kw,
        )(*operands, *ride.arrays)
        return res[:n_out], res[n_out:]
    return run


def _bucket_tables():
    i = np.arange(BLK)[:, None]
    j = np.arange(2 * BLK)[None, :]
    dist = BLK + i - j
    valid = (dist >= 0) & (dist <= BLK)
    tabs = []
    for _, dil in GROUPS:
        n = (np.clip(dist, 0, BLK) * dil).astype(np.int32)
        max_exact = NUM_BUCKETS // 2
        nf = np.maximum(n, 1).astype(np.float32)
        large = max_exact + (np.log(nf / np.float32(max_exact)) / np.float32(math.log(MAX_DISTANCE / max_exact))
                             * np.float32(NUM_BUCKETS - max_exact)).astype(np.int32)
        large = np.minimum(large, NUM_BUCKETS - 1)
        bucket = np.where(n < max_exact, n, large)
        tab = np.where(valid, bucket, -1).astype(np.int32)
        perm = _block_perm(dil)
        tabs.append(tab[perm][:, np.concatenate([perm, BLK + perm])])
    return np.stack(tabs)


def _bias_table(rel_bias, buckets):
    def body(rb_ref, bk_ref, out_ref):
        g = pl.program_id(0)
        bk = bk_ref[...]
        for h in range(NH):
            acc = jnp.full((BLK, 2 * BLK), NEG, F32)
            for b in range(NUM_BUCKETS):
                acc = jnp.where(bk == b, rb_ref[b, g * NH + h], acc)
            out_ref[h] = acc

    return pl.pallas_call(
        body, name="bias_table", grid=(NG,),
        in_specs=[pl.BlockSpec(memory_space=pltpu.SMEM),
                  pl.BlockSpec((None, BLK, 2 * BLK), lambda g: (g, 0, 0))],
        out_specs=pl.BlockSpec((NH, BLK, 2 * BLK), lambda g: (g, 0, 0)),
        out_shape=_sds((NG * NH, BLK, 2 * BLK)),
        compiler_params=_params("arbitrary"),
    )(rel_bias, buckets)


def _bias_grad(ds_acc, buckets, ride):
    def body(acc_ref, bk_ref, out_ref):
        bk = bk_ref[...]
        acc = acc_ref[...]
        lane = lax.broadcasted_iota(jnp.int32, (8, 128), 1)
        out = jnp.zeros((8, 128), F32)
        for b in range(NUM_BUCKETS):
            val = jnp.sum(jnp.where(bk == b, acc, 0.0))
            out = jnp.where(lane == b, val, out)
        out_ref[...] = out

    (out,), rode = _call_with_ride(
        body, ride, lambda: pl.program_id(0) == 0, lambda: pl.program_id(0) == NG * NH - 1,
        name="bias_grad", grid=(NG * NH,),
        in_specs=[pl.BlockSpec((None, BLK, 2 * BLK), lambda gh: (gh, 0, 0)),
                  pl.BlockSpec((None, BLK, 2 * BLK), lambda gh: (gh // NH, 0, 0))],
        out_specs=[pl.BlockSpec((None, 8, 128), lambda gh: (gh, 0, 0))],
        out_shape=[_sds((NG * NH, 8, 128))],
        compiler_params=_params("arbitrary"),
    )(ds_acc, buckets)
    return out, rode


def _mod_partial(c_all, w_ada_s, b_ada_s):
    def body(c_ref, w_ref, b_ref, o_ref):
        o_ref[...] = _dot(c_ref[...].astype(BF16), w_ref[...].astype(BF16)) + b_ref[...]

    return pl.pallas_call(body, name="mod_partial", out_shape=_sds((8, w_ada_s.shape[1])),
                          compiler_params=_params())(c_all, w_ada_s, b_ada_s)


def _prenorm(x, norm_g, mod):
    S = x.shape[0]
    tm = 512

    def body(x_ref, g_ref, mod_ref, h_ref):
        xv = x_ref[...]
        r = lax.rsqrt(jnp.mean(xv * xv, axis=-1, keepdims=True) + EPS)
        n1 = xv * r * g_ref[...]
        h_ref[...] = (n1 * (1.0 + mod_ref[:, D:2 * D]) + mod_ref[:, 0:D]).astype(BF16)

    return pl.pallas_call(
        body, name="prenorm", grid=(S // tm,),
        in_specs=[pl.BlockSpec((tm, D), lambda i: (i, 0)), pl.BlockSpec((1, D), lambda i: (0, 0)),
                  pl.BlockSpec((1, 3 * D), lambda i: (0, 0))],
        out_specs=pl.BlockSpec((tm, D), lambda i: (i, 0)),
        out_shape=_sds((S, D), BF16), compiler_params=_params("parallel"),
    )(x, norm_g, mod)


def _proj(h, wg_in, j0, nj, dtype, name):
    S = h.shape[0]
    tm = 2048
    per = wg_in.shape[2] // CB

    def body(h_ref, w_ref, o_ref):
        o_ref[...] = _dot(h_ref[...], w_ref[...]).astype(dtype)

    return pl.pallas_call(
        body, name=name, grid=(S // tm, nj),
        in_specs=[pl.BlockSpec((tm, D), lambda m, j: (m, 0)),
                  pl.BlockSpec((None, D, CB), lambda m, j: ((j0 + j) // per, 0, (j0 + j) % per))],
        out_specs=pl.BlockSpec((tm, CB), lambda m, j: (m, j)),
        out_shape=_sds((S, nj * CB), dtype), compiler_params=_params("parallel", "parallel"),
    )(h, wg_in)


HS = 4
SLAB = HS * HD


def _lane_head(rows):
    return lax.broadcasted_iota(jnp.int32, (rows, SLAB), 1) // HD


def _head_stack(a):
    head = _lane_head(a.shape[0])
    return jnp.concatenate([jnp.where(head == h, a, jnp.zeros_like(a)) for h in range(HS)], axis=0)


def _head_unstack(a):
    rows = a.shape[0] // HS
    head = _lane_head(rows)
    out = a[:rows]
    for h in range(1, HS):
        out = jnp.where(head == h, a[h * rows:(h + 1) * rows], out)
    return out


STAT_W = 128
VIEW = 16


def _sub_layout(dil):
    if dil == 1:
        return BLK, [None]
    return BLK * dil // VIEW, [[r + dil * u for u in range(VIEW // dil)] for r in range(dil)]


def _block_perm(dil):
    a_rows, _ = _sub_layout(dil)
    p = np.arange(BLK)
    return p if dil == 1 else (VIEW // dil) * (p % a_rows) + p // a_rows


LB = 128
N_SLAB = NH // HS


def _ld(refs, bs, s, w):
    if bs is None:
        return refs[0][:, s * w:(s + 1) * w]
    a_rows = refs[0].shape[0] // VIEW
    return jnp.concatenate([jnp.concatenate([ref[pl.ds(b, a_rows, stride=VIEW), :] for b in bs], axis=0)
                            for ref in refs], axis=1)


def _st(ref, bs, s, val):
    if bs is None:
        ref[:, s * SLAB:(s + 1) * SLAB] = val
        return
    a_rows = val.shape[0] // len(bs)
    for u, b in enumerate(bs):
        ref[:, b, :] = val[u * a_rows:(u + 1) * a_rows]


def _attn_views(dil, S):
    a_rows, subs = _sub_layout(dil)
    if dil == 1:
        def ispecs(base, w, f):
            return [pl.BlockSpec((BLK, N_SLAB * w), lambda sg, n: (f(n), base // (N_SLAB * w)))]
        return subs, S // BLK, N_SLAB, ispecs, (lambda w: (S, w)), (
            lambda f: pl.BlockSpec((BLK, AW), lambda sg, n: (f(n), 0)))

    def ispecs(base, w, f):
        return [pl.BlockSpec((a_rows * VIEW, LB), lambda sg, n, k=k: (f(n), (base + sg * w) // LB + k))
                for k in range(w // LB)]
    return subs, S // (a_rows * VIEW), 1, ispecs, (lambda w: (S // VIEW, VIEW, w)), (
        lambda f: pl.BlockSpec((a_rows, VIEW, SLAB), lambda sg, n: (f(n), 0, sg)))


def _attn_fwd(qkv_g, bias_tab, g):
    S = qkv_g.shape[0]
    subs, nbq, sps, ispecs, shape, ospec = _attn_views(GROUPS[g][1], S)
    cur = lambda n: n
    in_specs = [ispecs(0, SLAB, cur), ispecs(AW, SLAB, cur), ispecs(2 * AW, SLAB, cur)]
    nl = len(in_specs[0])

    def body(*refs):
        q, k, v = (refs[t * nl:(t + 1) * nl] for t in range(3))
        b_ref, o_ref, l_ref, kprev, vprev = refs[3 * nl:]
        n = pl.program_id(1)

        @pl.when(n == 0)
        def _():
            kprev[...] = jnp.zeros_like(kprev)
            vprev[...] = jnp.zeros_like(vprev)

        col = lax.broadcasted_iota(jnp.int32, (HS * BLK, 2 * BLK), 1)
        keep = (col >= BLK) | (n > 0)
        for s_ in range(sps):
            cs = slice(s_ * SLAB, (s_ + 1) * SLAB)
            bias = b_ref[pl.ds(s_ * HS, HS)].reshape(HS * BLK, 2 * BLK)
            for i, bs in enumerate(subs):
                kc, vc = _ld(k, bs, s_, SLAB).astype(BF16), _ld(v, bs, s_, SLAB).astype(BF16)
                kb = jnp.concatenate([kprev[i, :, cs], kc], axis=0)
                vb = jnp.concatenate([vprev[i, :, cs], vc], axis=0)
                kprev[i, :, cs], vprev[i, :, cs] = kc, vc
                s = _dot_nt(_head_stack(_ld(q, bs, s_, SLAB).astype(BF16)), kb) * (HD ** -0.5) + bias
                s = jnp.where(keep, s, NEG)
                m = jnp.max(s, axis=-1, keepdims=True)
                p = jnp.exp(s - m)
                den = jnp.sum(p, axis=-1, keepdims=True)
                _st(o_ref, bs, s_, _head_unstack(_dot(p.astype(BF16), vb) / den))
                _st(l_ref, bs, s_, _head_unstack(jnp.broadcast_to(m + jnp.log(den), (HS * BLK, SLAB))))

    out = _sds(shape(AW))
    nsg = N_SLAB // sps
    o, l = pl.pallas_call(
        body, name=f"attn_fwd{g}", grid=(nsg, nbq),
        in_specs=sum(in_specs, []) + [pl.BlockSpec((sps * HS, BLK, 2 * BLK), lambda sg, n: (g * nsg + sg, 0, 0))],
        out_specs=[ospec(cur), ospec(cur)],
        out_shape=[out, out],
        scratch_shapes=[pltpu.VMEM((len(subs), BLK, sps * SLAB), BF16)] * 2,
        compiler_params=_params("parallel", "arbitrary"),
    )(*([qkv_g] * (3 * nl)), bias_tab)
    return o.reshape(S, AW), l.reshape(S, AW)


def _attn_bwd(qkv_g, dattn, stats, bias_tab, g, ride):
    S = qkv_g.shape[0]
    subs, nbq, sps, ispecs, shape, ospec = _attn_views(GROUPS[g][1], S)
    cur = lambda n: jnp.minimum(n, nbq - 1)
    late = lambda n: jnp.maximum(n - 1, 0)
    in_specs = [ispecs(0, SLAB, cur), ispecs(AW, SLAB, cur), ispecs(2 * AW, SLAB, cur), ispecs(0, SLAB, cur),
                ispecs(0, STAT_W, cur)]
    nl = len(in_specs[0])

    def body(*refs):
        q, k, v, da = (refs[t * nl:(t + 1) * nl] for t in range(4))
        st_ref, b_ref, dq_ref, dk_ref, dv_ref, ds_ref, ck_ref, cv_ref, kprev, vprev = refs[4 * nl:]
        n = pl.program_id(1)

        @pl.when(n == 0)
        def _():
            for ref in (ds_ref, ck_ref, cv_ref, kprev, vprev):
                ref[...] = jnp.zeros_like(ref)

        @pl.when(n < nbq)
        def _():
            col = lax.broadcasted_iota(jnp.int32, (HS * BLK, 2 * BLK), 1)
            keep = (col >= BLK) | (n > 0)
            for s_ in range(sps):
                cs = slice(s_ * SLAB, (s_ + 1) * SLAB)
                bias = b_ref[pl.ds(s_ * HS, HS)].reshape(HS * BLK, 2 * BLK)
                for i, bs in enumerate(subs):
                    st = _ld((st_ref,), bs, s_, STAT_W)
                    kc, vc = _ld(k, bs, s_, SLAB).astype(BF16), _ld(v, bs, s_, SLAB).astype(BF16)
                    kb = jnp.concatenate([kprev[i, :, cs], kc], axis=0)
                    vb = jnp.concatenate([vprev[i, :, cs], vc], axis=0)
                    kprev[i, :, cs], vprev[i, :, cs] = kc, vc
                    lse =jnp.concatenate([st[:, h:h + 1] for h in range(HS)], axis=0)
                    delta = jnp.concatenate([st[:, HS + h:HS + h + 1] for h in range(HS)], axis=0)
                    qs = _head_stack(_ld(q, bs, s_, SLAB).astype(BF16))
                    dos = _head_stack(_ld(da, bs, s_, SLAB).astype(BF16))
                    s = _dot_nt(qs, kb) * (HD ** -0.5) + bias
                    s = jnp.where(keep, s, NEG)
                    p = jnp.exp(s - lse)
                    ds = p * (_dot_nt(dos, vb) - delta)
                    ds_ref[pl.ds(s_ * HS, HS)] += ds.reshape(HS, BLK, 2 * BLK)
                    ds_b = (ds * (HD ** -0.5)).astype(BF16)
                    _st(dq_ref, bs, s_, _head_unstack(_dot(ds_b, kb)))
                    dkb = _dot_tn(ds_b, qs)
                    dvb = _dot_tn(p.astype(BF16), dos)
                    _st(dk_ref, bs, s_, ck_ref[i, :, cs] + dkb[:BLK])
                    _st(dv_ref, bs, s_, cv_ref[i, :, cs] + dvb[:BLK])
                    ck_ref[i, :, cs] = dkb[BLK:]
                    cv_ref[i, :, cs] = dvb[BLK:]

        @pl.when(n == nbq)
        def _():
            for s_ in range(sps):
                for i, bs in enumerate(subs):
                    _st(dk_ref, bs, s_, ck_ref[i, :, s_ * SLAB:(s_ + 1) * SLAB])
                    _st(dv_ref, bs, s_, cv_ref[i, :, s_ * SLAB:(s_ + 1) * SLAB])

    out = _sds(shape(AW))
    nsg = N_SLAB // sps
    (dq, dk, dv, ds_acc), rode = _call_with_ride(
        body, ride, lambda: (pl.program_id(0) == 0) & (pl.program_id(1) == 0),
        lambda: (pl.program_id(0) == nsg - 1) & (pl.program_id(1) == nbq),
        name=f"attn_bwd{g}", grid=(nsg, nbq + 1),
        in_specs=sum(in_specs, []) + [pl.BlockSpec((sps * HS, BLK, 2 * BLK), lambda sg, n: (g * nsg + sg, 0, 0))],
        out_specs=[ospec(cur), ospec(late), ospec(late),
                   pl.BlockSpec((sps * HS, BLK, 2 * BLK), lambda sg, n: (sg, 0, 0))],
        out_shape=[out] * 3 + [_sds((NH, BLK, 2 * BLK))],
        scratch_shapes=[pltpu.VMEM((len(subs), BLK, sps * SLAB), F32)] * 2
        + [pltpu.VMEM((len(subs), BLK, sps * SLAB), BF16)] * 2,
        compiler_params=_params("arbitrary", "arbitrary"),
    )(*([qkv_g] * (3 * nl)), *([dattn] * nl), stats, bias_tab)
    return [dq.reshape(S, AW), dk.reshape(S, AW), dv.reshape(S, AW)], ds_acc, rode


TM_MIX = 256


def _mix_specs(tm):
    row512 = pl.BlockSpec((tm, AW), lambda i: (i, 0))
    return ([row512] * 6 + [
        pl.BlockSpec((tm, REST_W), lambda i: (i, 0)),
        pl.BlockSpec((HALO, AW), lambda i: (jnp.maximum(i * (tm // HALO) - 1, 0), 1)),
        pl.BlockSpec((AW, D), lambda i: (0, 0)), pl.BlockSpec((AW, D), lambda i: (0, 0)),
        pl.BlockSpec((4, PGW, PGW), lambda i: (0, 0, 0)), pl.BlockSpec((1, AW), lambda i: (0, 0))])


def _mix_forward(i, tm, o_refs, l_refs, rest_ref, halo_ref, wab_ref, wpb_ref, pw_ref, ps_ref):
    l0, l1, l2 = (r[...] for r in l_refs)
    mx = jnp.maximum(jnp.maximum(l0, l1), l2)
    e0, e1, e2 = jnp.exp(l0 - mx), jnp.exp(l1 - mx), jnp.exp(l2 - mx)
    den = e0 + e1 + e2
    lj = mx + jnp.log(den)
    attn = (e0 * o_refs[0][...] + e1 * o_refs[1][...] + e2 * o_refs[2][...]) / den

    z_attn = rest_ref[:, 0:AW]
    u = rest_ref[:, AW:2 * AW]
    z_pool = rest_ref[:, 2 * AW:3 * AW]
    g_attn = rest_ref[:, 3 * AW:3 * AW + D]
    g_pool = rest_ref[:, 3 * AW + D:3 * AW + 2 * D]

    sg_a = _sigmoid(z_attn)
    sil_a = z_attn * sg_a
    a_g = (attn * sil_a).astype(BF16)
    y_attn = _dot(a_g, wab_ref[...])

    halo = jnp.where(i > 0, halo_ref[...], 0.0)
    ext = jnp.concatenate([halo, u], axis=0)
    t = i * tm + lax.broadcasted_iota(jnp.int32, (tm, 1), 0)
    pooled, mixed_raw = [], []
    for gi, win in enumerate(POOL_WINDOWS):
        s = ext[:, gi * PGW:(gi + 1) * PGW]
        sh = 1
        while sh < win:
            s = s + pltpu.roll(s, sh, 0)
            sh *= 2
        cnt = jnp.minimum(t + 1, win).astype(F32)
        pg = s[HALO:] / cnt - u[:, gi * PGW:(gi + 1) * PGW]
        pooled.append(pg.astype(BF16))
        mixed_raw.append(_dot(pooled[-1], pw_ref[gi].astype(BF16)))
    mixed_raw = jnp.concatenate(mixed_raw, axis=1)
    mixed = mixed_raw * ps_ref[...]
    sg_p = _sigmoid(z_pool)
    sil_p = z_pool * sg_p
    m_g = (mixed * sil_p).astype(BF16)
    y_pool = _dot(m_g, wpb_ref[...])

    sa = _sigmoid(g_attn)
    sp = _sigmoid(g_pool)
    merged = sa * y_attn + sp * y_pool
    return dict(lj=lj, attn=attn, z_attn=z_attn, z_pool=z_pool, sg_a=sg_a, sil_a=sil_a, a_g=a_g, y_attn=y_attn,
                pooled=pooled, mixed_raw=mixed_raw, mixed=mixed, sg_p=sg_p, sil_p=sil_p, m_g=m_g, y_pool=y_pool,
                sa=sa, sp=sp, merged=merged)


def _mix_step(x, target, os_, ls_, rest, wab, wpb, pool_w, pool_scale, wout, mod, final_g):
    S = x.shape[0]
    tm = TM_MIX
    nt = S // tm
    sw = D // N_SHARD

    def body(o0, o1, o2, l0, l1, l2, rest_ref, halo_ref, wab_ref, wpb_ref, pw_ref, ps_ref,
             x_ref, t_ref, wo_ref, mod_ref, fg_ref, dx2_ref, loss_ref, dfg_ref, dgate_ref,
             dattn_ref, stats_ref, dpooled_ref, drest_ref, dwo_hbm, dwab_hbm, dwpb_hbm, dpw_ref, dps_ref,
             awo, awab, awpb):
        i = pl.program_id(0)

        @pl.when(i == 0)
        def _():
            for ref in (loss_ref, dfg_ref, dgate_ref, awo, awab, awpb, dpw_ref, dps_ref):
                ref[...] = jnp.zeros_like(ref)

        f = _mix_forward(i, tm, (o0, o1, o2), (l0, l1, l2), rest_ref, halo_ref, wab_ref, wpb_ref, pw_ref, ps_ref)
        mo = _dot(f["merged"].astype(BF16), wo_ref[...])
        gate = mod_ref[:, 2 * D:3 * D]
        fg = fg_ref[...]
        x2 = x_ref[...] + gate * mo
        r2 = lax.rsqrt(jnp.mean(x2 * x2, axis=-1, keepdims=True) + EPS)
        n2 = x2 * r2
        err = n2 * fg - t_ref[...]
        loss_ref[...] += 0.5 * jnp.sum(jnp.mean(err * err, axis=-1, keepdims=True))
        dy = err * (1.0 / D)
        dfg_ref[...] += jnp.sum(dy * n2, axis=0, keepdims=True)
        dn = dy * fg
        dx2 = r2 * (dn - n2 * jnp.mean(dn * n2, axis=-1, keepdims=True))
        dgate_ref[...] += jnp.sum(dx2 * mo, axis=0, keepdims=True)
        dx2_ref[...] = dx2

        dmo_b = (dx2 * gate).astype(BF16)
        dmerged = _dot_nt(dmo_b, wo_ref[...])
        awo[...] += _dot_tn(f["merged"].astype(BF16), dmo_b)
        sa, sp = f["sa"], f["sp"]
        dya = (dmerged * sa).astype(BF16)
        dyp = (dmerged * sp).astype(BF16)
        dg_attn = dmerged * f["y_attn"] * sa * (1.0 - sa)
        dg_pool = dmerged * f["y_pool"] * sp * (1.0 - sp)
        dag = _dot_nt(dya, wab_ref[...])
        awab[...] += _dot_tn(f["a_g"], dya)
        dmg = _dot_nt(dyp, wpb_ref[...])
        awpb[...] += _dot_tn(f["m_g"], dyp)
        dattn = dag * f["sil_a"]
        dattn_ref[...] = dattn
        prod = dattn * f["attn"]
        lane = lax.broadcasted_iota(jnp.int32, (tm, STAT_W), 1)
        for sb in range(N_SLAB):
            st = jnp.zeros((tm, STAT_W), F32)
            for h in range(HS):
                hs = slice((sb * HS + h) * HD, (sb * HS + h + 1) * HD)
                st = jnp.where(lane == h, f["lj"][:, hs.start:hs.start + 1], st)
                st = jnp.where(lane == HS + h, jnp.sum(prod[:, hs], axis=-1, keepdims=True), st)
            stats_ref[:, sb * STAT_W:(sb + 1) * STAT_W] = st
        dz_attn = dag * f["attn"] * (f["sg_a"] * (1.0 + f["z_attn"] * (1.0 - f["sg_a"])))
        dmixed = dmg * f["sil_p"]
        dz_pool = dmg * f["mixed"] * (f["sg_p"] * (1.0 + f["z_pool"] * (1.0 - f["sg_p"])))
        dps_ref[...] += jnp.sum(dmixed * f["mixed_raw"], axis=0, keepdims=True)
        dpm = (dmixed * ps_ref[...]).astype(BF16)
        for gi in range(len(POOL_WINDOWS)):
            cs = slice(gi * PGW, (gi + 1) * PGW)
            dpw_ref[gi] += _dot_tn(f["pooled"][gi], dpm[:, cs])
            dpooled_ref[:, cs] = _dot_nt(dpm[:, cs], pw_ref[gi].astype(BF16))
        drest_ref[:, 0:AW] = dz_attn.astype(BF16)
        drest_ref[:, AW:2 * AW] = jnp.zeros((tm, AW), BF16)
        drest_ref[:, 2 * AW:3 * AW] = dz_pool.astype(BF16)
        drest_ref[:, 3 * AW:3 * AW + D] = dg_attn.astype(BF16)
        drest_ref[:, 3 * AW + D:3 * AW + 2 * D] = dg_pool.astype(BF16)

        @pl.when(i == nt - 1)
        def _():
            pltpu.sync_copy(awo, dwo_hbm)
            for k in range(N_SHARD):
                pltpu.sync_copy(awab.at[:, pl.ds(k * sw, sw)], dwab_hbm.at[k])
                pltpu.sync_copy(awpb.at[:, pl.ds(k * sw, sw)], dwpb_hbm.at[k])

    row = pl.BlockSpec((tm, D), lambda i: (i, 0))
    vec = pl.BlockSpec((1, D), lambda i: (0, 0))
    row512 = pl.BlockSpec((tm, AW), lambda i: (i, 0))
    outs = pl.pallas_call(
        body, name="mix_step", grid=(nt,),
        in_specs=_mix_specs(tm) + [row, row, pl.BlockSpec((D, D), lambda i: (0, 0)),
                                   pl.BlockSpec((1, 3 * D), lambda i: (0, 0)), vec],
        out_specs=[row, pl.BlockSpec((8, 128), lambda i: (0, 0)), vec, vec,
                   row512, pl.BlockSpec((tm, N_SLAB * STAT_W), lambda i: (i, 0)), row512,
                   pl.BlockSpec((tm, REST_W), lambda i: (i, 0)), ANY, ANY, ANY,
                   pl.BlockSpec((4, PGW, PGW), lambda i: (0, 0, 0)), pl.BlockSpec((1, AW), lambda i: (0, 0))],
        out_shape=[_sds((S, D)), _sds((8, 128)), _sds((1, D)), _sds((1, D)),
                   _sds((S, AW)), _sds((S, N_SLAB * STAT_W)), _sds((S, AW)), _sds((S, REST_W), BF16),
                   _sds((D, D)), _sds((N_SHARD, AW, sw)), _sds((N_SHARD, AW, sw)), _sds((4, PGW, PGW)), _sds((1, AW))],
        scratch_shapes=[pltpu.VMEM((D, D), F32), pltpu.VMEM((AW, D), F32), pltpu.VMEM((AW, D), F32)],
        compiler_params=_params("arbitrary"),
    )(*os_, *ls_, rest, rest, wab, wpb, pool_w, pool_scale, x, target, wout, mod, final_g)
    dx2, loss, dfg, dgate, dattn, stats, dpooled, drest, dwo, dwab, dwpb, dpw, dps = outs
    return (dx2, loss, dfg, dgate, dattn, stats, dpooled, drest, dwo.reshape(N_SHARD, D // N_SHARD, D), dwab, dwpb,
            dpw, dps)


def _pool_bwd(dpooled):
    S = dpooled.shape[0]
    tm = 512
    nt = S // tm

    def body(dp_ref, nxt_ref, du_ref):
        i = pl.program_id(0)
        t = i * tm + lax.broadcasted_iota(jnp.int32, (tm + HALO, 1), 0)
        nxt = jnp.where(i < nt - 1, nxt_ref[...], 0.0)
        ext = jnp.concatenate([dp_ref[...], nxt], axis=0)
        for gi, win in enumerate(POOL_WINDOWS):
            cs = slice(gi * PGW, (gi + 1) * PGW)
            s = ext[:, cs] / jnp.minimum(t + 1, win).astype(F32)
            sh = 1
            while sh < win:
                s = s + pltpu.roll(s, tm + HALO - sh, 0)
                sh *= 2
            du_ref[:, cs] = (s[:tm] - dp_ref[:, cs]).astype(BF16)

    return pl.pallas_call(
        body, name="pool_bwd", grid=(nt,),
        in_specs=[pl.BlockSpec((tm, AW), lambda i: (i, 0)),
                  pl.BlockSpec((HALO, AW), lambda i: (jnp.minimum((i + 1) * (tm // HALO), S // HALO - 1), 0))],
        out_specs=pl.BlockSpec((tm, AW), lambda i: (i, 0)),
        out_shape=_sds((S, AW), BF16), compiler_params=_params("parallel"),
    )(dpooled, dpooled)


TB = 1024


def _dh(dproj, wg_in, ride):
    S = dproj.shape[0]
    per = wg_in.shape[2] // TB
    nm, nk = S // TB, IN_W // TB

    def body(dp_ref, w_ref, out_ref):
        @pl.when(pl.program_id(1) == 0)
        def _():
            out_ref[...] = jnp.zeros_like(out_ref)

        out_ref[...] += _dot_nt(dp_ref[...], w_ref[...])

    (dh,), rode = _call_with_ride(
        body, ride, lambda: (pl.program_id(0) == 0) & (pl.program_id(1) == 0),
        lambda: (pl.program_id(0) == nm - 1) & (pl.program_id(1) == nk - 1),
        name="dh", grid=(nm, nk),
        in_specs=[pl.BlockSpec((TB, TB), lambda m, kk: (m, kk)),
                  pl.BlockSpec((None, D, TB), lambda m, kk: (kk // per, 0, kk % per))],
        out_specs=[pl.BlockSpec((TB, D), lambda m, kk: (m, 0))],
        out_shape=[_sds((S, D))], compiler_params=_params("arbitrary", "arbitrary"),
    )(dproj, wg_in)
    return dh, rode


def _dw_in(h, dproj):
    S = dproj.shape[0]
    per = IN_W // N_SHARD // TB

    def body(h_ref, dp_ref, out_ref):
        @pl.when(pl.program_id(1) == 0)
        def _():
            out_ref[...] = jnp.zeros_like(out_ref)

        out_ref[...] += _dot_tn(h_ref[...], dp_ref[...])

    return pl.pallas_call(
        body, name="dw_in", grid=(IN_W // TB, S // TB),
        in_specs=[pl.BlockSpec((TB, D), lambda j, kk: (kk, 0)), pl.BlockSpec((TB, TB), lambda j, kk: (kk, j))],
        out_specs=pl.BlockSpec((None, D, TB), lambda j, kk: (j // per, 0, j % per)),
        out_shape=_sds((N_SHARD, D, IN_W // N_SHARD)), compiler_params=_params("parallel", "arbitrary"),
    )(h, dproj)


def _prenorm_bwd(x, dh, dx2, norm_g, mod):
    S = x.shape[0]
    tm = 512

    def body(x_ref, dh_ref, dx2_ref, g_ref, mod_ref, gx_ref, dg_ref, dshift_ref, dscale_ref):
        i = pl.program_id(0)

        @pl.when(i == 0)
        def _():
            dg_ref[...] = jnp.zeros_like(dg_ref)
            dshift_ref[...] = jnp.zeros_like(dshift_ref)
            dscale_ref[...] = jnp.zeros_like(dscale_ref)

        xv = x_ref[...]
        dhv = dh_ref[...]
        g = g_ref[...]
        r = lax.rsqrt(jnp.mean(xv * xv, axis=-1, keepdims=True) + EPS)
        xh = xv * r
        dshift_ref[...] += jnp.sum(dhv, axis=0, keepdims=True)
        dscale_ref[...] += jnp.sum(dhv * (xh * g), axis=0, keepdims=True)
        dn1 = dhv * (1.0 + mod_ref[:, D:2 * D])
        dg_ref[...] += jnp.sum(dn1 * xh, axis=0, keepdims=True)
        dxh = dn1 * g
        gx_ref[...] = dx2_ref[...] + r * (dxh - xh * jnp.mean(dxh * xh, axis=-1, keepdims=True))

    row = pl.BlockSpec((tm, D), lambda i: (i, 0))
    vec = pl.BlockSpec((1, D), lambda i: (0, 0))
    return pl.pallas_call(
        body, name="prenorm_bwd", grid=(S // tm,),
        in_specs=[row, row, row, vec, pl.BlockSpec((1, 3 * D), lambda i: (0, 0))],
        out_specs=[row, vec, vec, vec],
        out_shape=[_sds((S, D)), _sds((1, D)), _sds((1, D)), _sds((1, D))],
        compiler_params=_params("arbitrary"),
    )(x, dh, dx2, norm_g, mod)


def _local_step(x, target, mod, wg_in, wab, wpb, wout, pool_w, pool_scale, rel_bias, norm_g, final_g, chip_half):
    buckets = jnp.asarray(_bucket_tables())
    bias_tab = _bias_table(rel_bias, buckets)
    h = _prenorm(x, norm_g, mod)
    qkv = [_proj(h, wg_in, 3 * g, 3, F32, f"proj_qkv{g}") for g in range(NG)]
    rest = _proj(h, wg_in, NCB_QKV, REST_W // CB, F32, "proj_rest")
    os_, ls_ = zip(*[_attn_fwd(qkv[g], bias_tab, g) for g in range(NG)])
    (dx2, loss, dfinal_g, dgate, dattn, stats, dpooled, drest, dw_out, dw_ab, dw_pb, dpool_w,
     dpool_scale) = _mix_step(x, target, os_, ls_, rest, wab, wpb, pool_w, pool_scale, wout, mod, final_g)
    du = _pool_bwd(dpooled)

    small = [dw_ab, dw_pb, dw_out]
    dqkv0, ds0, sib_small = _attn_bwd(qkv[0], dattn, stats, bias_tab, 0, _ride_sibling_halves(small))
    p_small = [_pair_sum(g, t, chip_half, f"rs_pair_sum{a}") for a, (g, t) in enumerate(zip(small, sib_small))]
    dqkv1, ds1, u_small = _attn_bwd(qkv[1], dattn, stats, bias_tab, 1,
                                    _ride_chip_exchange([p16 for _, p16 in p_small]))
    rs_ab, rs_pb, rs_out = [_chip_sum(p32, u, chip_half, f"rs_chip_sum{a}")
                            for a, ((p32, _), u) in enumerate(zip(p_small, u_small))]
    dqkv2, ds2, _ = _attn_bwd(qkv[2], dattn, stats, bias_tab, 2, None)

    dproj = jnp.concatenate([a.astype(BF16) for a in dqkv0 + dqkv1 + dqkv2] + [drest[:, :AW], du, drest[:, 2 * AW:]],
                            axis=1)
    dw_in = _dw_in(h, dproj)
    drel_rows, (sib_in,) = _bias_grad(jnp.concatenate([ds0, ds1, ds2], axis=0), buckets,
                                      _ride_sibling_halves([dw_in]))
    drel = drel_rows[:, 0, :NUM_BUCKETS].T
    p32_in, p16_in = _pair_sum(dw_in, sib_in, chip_half, "rs_pair_sum_in")
    dh, (u_in,) = _dh(dproj, wg_in, _ride_chip_exchange([p16_in]))
    rs_in = _chip_sum(p32_in, u_in, chip_half, "rs_chip_sum_in")

    grad_x, dnorm_g, dshift, dscale = _prenorm_bwd(x, dh, dx2, norm_g, mod)
    dmod = jnp.concatenate([dshift, dscale, dgate], axis=1)
    return dict(loss=loss[0, 0], grad_x=grad_x, dmod=dmod, dnorm_g=dnorm_g, dfinal_g=dfinal_g, dpool_w=dpool_w,
                dpool_scale=dpool_scale, drel_bias=drel, dw_in=dw_in, dw_attn_br=dw_ab, dw_pool_br=dw_pb,
                dw_out=dw_out, rs_in=rs_in, rs_attn_br=rs_ab, rs_pool_br=rs_pb, rs_out=rs_out)


def _allgather8(blocks, name, relay=None):
    nb = len(blocks)
    relay = [False] * nb if relay is None else list(relay)

    def body(*refs):
        ins, outs = refs[:nb], refs[nb:2 * nb]
        send_sems, recv_sems = refs[2 * nb:]
        x, y, c = lax.axis_index("x"), lax.axis_index("y"), lax.axis_index("c")
        me, sibling = (x, y, c), (x, y, 1 - c)
        here, xn, yn, dg = (x, y), (1 - x, y), (x, 1 - y), (1 - x, 1 - y)

        def slot(a, chip, core, half=None):
            ref = outs[a].at[4 * chip[0] + 2 * chip[1] + core]
            if half is None:
                return ref
            r2 = ref.shape[0] // 2
            return ref.at[pl.ds(half * r2, r2)]

        def copy(a, k, dst, to, src=None):
            return pltpu.make_async_remote_copy(src_ref=dst if src is None else src, dst_ref=dst,
                                                send_sem=send_sems.at[a, k], recv_sem=recv_sems.at[a, k],
                                                device_id=to, device_id_type=MESH)

        def start(cps):
            for cp in cps:
                cp.start()
            return cps

        sent = []
        for a in range(nb):
            own = slot(a, here, c)
            sent += [copy(a, 0, own, sibling, src=ins[a]), copy(a, 1, own, (*xn, c), src=ins[a]),
                     copy(a, 2, own, (*yn, c), src=ins[a])]
            if not relay[a]:
                sent.append(copy(a, 3, own, (*dg, c), src=ins[a]))
        start(sent)
        for a in range(nb):
            copy(a, 2, slot(a, yn, c), me).wait_recv()
            sent += start([copy(a, 6, slot(a, yn, c), sibling)]
                          + ([copy(a, 3, slot(a, yn, c, 0), (*xn, c))] if relay[a] else []))
        for a in range(nb):
            copy(a, 1, slot(a, xn, c), me).wait_recv()
            sent += start([copy(a, 5, slot(a, xn, c), sibling)]
                          + ([copy(a, 4, slot(a, xn, c, 1), (*yn, c))] if relay[a] else []))
        for a in range(nb):
            for k, half in ((3, 0), (4, 1)) if relay[a] else ((3, None),):
                copy(a, k, slot(a, dg, c, half), me).wait_recv()
                sent += start([copy(a, 4 + k, slot(a, dg, c, half), sibling)])
        for a in range(nb):
            copy(a, 0, slot(a, here, 1 - c), me).wait_recv()
            copy(a, 5, slot(a, xn, 1 - c), me).wait_recv()
            copy(a, 6, slot(a, yn, 1 - c), me).wait_recv()
            for k, half in ((7, 0), (8, 1)) if relay[a] else ((7, None),):
                copy(a, k, slot(a, dg, 1 - c, half), me).wait_recv()
        for cp in sent:
            cp.wait_send()

    outs = pl.pallas_call(
        body, name=name, in_specs=[ANY] * nb, out_specs=[ANY] * nb,
        out_shape=[_sds((8,) + b.shape, b.dtype) for b in blocks],
        scratch_shapes=[_dma_sems(nb, 9), _dma_sems(nb, 9)],
    )(*blocks)
    return [_place_own(buf, b) for buf, b in zip(outs, blocks)]


def _place_own(buf, block):
    dev = 4 * lax.axis_index("x") + 2 * lax.axis_index("y") + lax.axis_index("c")
    return lax.dynamic_update_index_in_dim(buf, block, dev, 0)


def _ride_sibling_halves(gs):
    def copies(ins, outs, send_sems, recv_sems):
        x, y, c = lax.axis_index("x"), lax.axis_index("y"), lax.axis_index("c")
        cps = []
        for a in range(len(gs)):
            r2 = ins[a].shape[1] // 2
            other = ins[a].at[:, pl.ds((1 - c) * r2, r2), :]
            cps.append(pltpu.make_async_remote_copy(src_ref=other, dst_ref=outs[a], send_sem=send_sems.at[a],
                                                    recv_sem=recv_sems.at[a], device_id=(x, y, 1 - c),
                                                    device_id_type=MESH))
        return cps

    return _Ride(gs, [_sds((g.shape[0], g.shape[1] // 2, g.shape[2]), g.dtype) for g in gs], len(gs), copies)


def _pair_sum(g, t, chip_half, name):
    nsh, rows, cols = g.shape
    r2 = rows // 2
    tr = _row_tile(r2, cols)
    nt = r2 // tr

    def body(ch_ref, g_ref, t_ref, p32_ref, p16_ref):
        p = g_ref[...] + t_ref[...]
        p16_ref[...] = p.astype(BF16)

        @pl.when(pl.program_id(1) == ch_ref[0])
        def _():
            p32_ref[...] = p

    blk = pl.BlockSpec((None, tr, cols), lambda i, k, ch_ref: (k, i, 0))
    return pl.pallas_call(
        body, name=name,
        grid_spec=pltpu.PrefetchScalarGridSpec(
            num_scalar_prefetch=1, grid=(nt, nsh),
            in_specs=[pl.BlockSpec((None, tr, cols), lambda i, k, ch_ref: (k, ch_ref[1] * nt + i, 0)), blk],
            out_specs=[pl.BlockSpec((tr, cols), lambda i, k, ch_ref: (i, 0)), blk]),
        out_shape=[_sds((r2, cols)), _sds((nsh, r2, cols), BF16)],
        compiler_params=_params("parallel", "arbitrary"),
    )(chip_half, g, t)


def _ride_chip_exchange(ps):
    def copies(ins, outs, send_sems, recv_sems):
        x, y, c = lax.axis_index("x"), lax.axis_index("y"), lax.axis_index("c")
        chips = [(1 - x, y), (x, 1 - y), (1 - x, 1 - y)]
        cps = []
        for a in range(len(ps)):
            for j, (ox, oy) in enumerate(chips):
                cps.append(pltpu.make_async_remote_copy(src_ref=ins[a].at[2 * ox + oy], dst_ref=outs[a].at[j],
                                                        send_sem=send_sems.at[3 * a + j],
                                                        recv_sem=recv_sems.at[3 * a + j],
                                                        device_id=(ox, oy, c), device_id_type=MESH))
        return cps

    return _Ride(ps, [_sds((3,) + p.shape[1:], p.dtype) for p in ps], 3 * len(ps), copies)


def _chip_sum(p32, u, chip_half, name):
    r2, cols = p32.shape
    tr = _row_tile(r2, cols)
    nt = r2 // tr

    def body(ch_ref, p_ref, u_ref, o_ref):
        acc = p_ref[...]
        for j in range(3):
            acc = acc + u_ref[j].astype(F32)
        o_ref[...] = acc

    return pl.pallas_call(
        body, name=name,
        grid_spec=pltpu.PrefetchScalarGridSpec(
            num_scalar_prefetch=1, grid=(nt,),
            in_specs=[pl.BlockSpec((tr, cols), lambda i, ch_ref: (i, 0)),
                      pl.BlockSpec((3, tr, cols), lambda i, ch_ref: (0, i, 0))],
            out_specs=pl.BlockSpec((tr, cols), lambda i, ch_ref: (ch_ref[1] * nt + i, 0))),
        out_shape=_sds((2 * r2, cols)), compiler_params=_params("parallel"),
    )(chip_half, p32, u)


def _sibling_join(fs, name):
    nb = len(fs)

    def body(*refs):
        outs = refs[nb:2 * nb]
        send_sems, recv_sems = refs[2 * nb:]
        x, y, c = lax.axis_index("x"), lax.axis_index("y"), lax.axis_index("c")
        cps = []
        for a in range(nb):
            r2 = outs[a].shape[0] // 2
            rows = outs[a].at[pl.ds(c * r2, r2), :]
            cps.append(pltpu.make_async_remote_copy(src_ref=rows, dst_ref=rows, send_sem=send_sems.at[a],
                                                    recv_sem=recv_sems.at[a], device_id=(x, y, 1 - c),
                                                    device_id_type=MESH))
        for cp in cps:
            cp.start()
        for cp in cps:
            cp.wait()

    return pl.pallas_call(
        body, name=name, in_specs=[ANY] * nb, out_specs=[ANY] * nb,
        out_shape=[_sds(f.shape, f.dtype) for f in fs],
        input_output_aliases={a: a for a in range(nb)},
        scratch_shapes=[_dma_sems(nb), _dma_sems(nb)],
    )(*fs)


def _row_tile(rows, cols):
    tile = rows
    while tile * cols * 4 > (1 << 20) and tile % 16 == 0:
        tile //= 2
    return tile


def _w_ada_grad(c_all, dmod_cols):
    def body(c_ref, d_ref, o_ref):
        o_ref[...] = _dot_tn(c_ref[...].astype(BF16), d_ref[...].astype(BF16))

    return pl.pallas_call(body, name="w_ada_grad", out_shape=_sds((c_all.shape[1], dmod_cols.shape[1])),
                          compiler_params=_params())(c_all, dmod_cols)


def _adam_math(w, g, m, v):
    nm = ADAM_B1 * m + (1.0 - ADAM_B1) * g
    nv = ADAM_B2 * v + (1.0 - ADAM_B2) * (g * g)
    m_hat = nm / (1.0 - ADAM_B1 ** ADAM_STEP)
    v_hat = nv / (1.0 - ADAM_B2 ** ADAM_STEP)
    return -ADAM_LR * (m_hat / (jnp.sqrt(v_hat) + ADAM_EPS) + ADAM_WD * w), nm, nv


def _adamw(w, g, m, v, name):
    rows, cols = w.shape
    tr = _row_tile(rows, cols)

    def body(w_ref, g_ref, m_ref, v_ref, go_ref, d_ref, nm_ref, nv_ref):
        gv = g_ref[...]
        go_ref[...] = gv
        d_ref[...], nm_ref[...], nv_ref[...] = _adam_math(w_ref[...], gv, m_ref[...], v_ref[...])

    spec = pl.BlockSpec((tr, cols), lambda i: (i, 0))
    return pl.pallas_call(
        body, name=name, grid=(rows // tr,), in_specs=[spec] * 4, out_specs=[spec] * 4,
        out_shape=[_sds((rows, cols))] * 4, compiler_params=_params("parallel"),
    )(w, g, m, v)


def _pack_small(dmod, dnorm_g, dfinal_g, dpool_scale, drel_bias, loss, dpool_w):
    return jnp.concatenate([dmod.reshape(-1, 128), dnorm_g.reshape(-1, 128), dfinal_g.reshape(-1, 128),
                            jnp.pad(dpool_scale.reshape(-1, 128), ((0, PK_RELB - PK_PSCALE - AW // 128), (0, 0))),
                            jnp.pad(drel_bias, ((0, 0), (0, 128 - NG * NH))),
                            jnp.full((PK_POOLW - PK_LOSS, 128), loss, F32), dpool_w.reshape(-1, 128)], axis=0)


def _small_update(small_all, ws, ms, vs):
    lane_rows = [(r0, r0 + w.shape[1] // 128) for r0, w in zip((PK_BADA, PK_NORMG, PK_FINALG, PK_PSCALE), ws)]
    nw = len(ws)

    def body(all_ref, *refs):
        w_refs, m_refs, v_refs = refs[:nw], refs[nw:2 * nw], refs[2 * nw:3 * nw]
        loss_ref, outs = refs[3 * nw], refs[3 * nw + 1:]
        g = all_ref[0]
        for s in range(1, all_ref.shape[0]):
            g = g + all_ref[s]
        loss_ref[...] = jnp.broadcast_to(g[PK_LOSS:PK_LOSS + 1, :], loss_ref.shape)

        def put(p, at, gv):
            d, nm, nv = _adam_math(w_refs[p][at], gv, m_refs[p][at], v_refs[p][at])
            for o_ref, val in zip(outs[4 * p:4 * p + 4], (gv, d, nm, nv)):
                o_ref[at] = val

        for p, (r0, r1) in enumerate(lane_rows):
            for i in range(r1 - r0):
                put(p, (slice(None), slice(128 * i, 128 * (i + 1))), g[r0 + i:r0 + i + 1, :])
        put(4, (slice(None), slice(None)), g[PK_RELB:PK_LOSS, 0:NG * NH])
        put(5, (slice(None), slice(None)), g[PK_POOLW:PK_ROWS, :])

    res = pl.pallas_call(
        body, name="small_update",
        out_shape=[_sds((8, 128))] + [_sds(w.shape) for w in ws for _ in range(4)], compiler_params=_params(),
    )(small_all, *ws, *ms, *vs)
    return res[0], [res[1 + 4 * p:5 + 4 * p] for p in range(nw)]


def kernel(x, c, norm_g, w_ada, b_ada, w_in, pool_w, pool_scale, w_attn_br, w_pool_br, w_out, rel_bias, final_g, loss_target, m_norm_g, m_w_ada, m_b_ada, m_w_in, m_pool_w, m_pool_scale, m_w_attn_br, m_w_pool_br, m_w_out, m_rel_bias, m_final_g, v_norm_g, v_w_ada, v_b_ada, v_w_in, v_pool_w, v_pool_scale, v_w_attn_br, v_w_pool_br, v_w_out, v_rel_bias, v_final_g):
    ix, iy, ic = lax.axis_index("x"), lax.axis_index("y"), lax.axis_index("c")
    dev = 4 * ix + 2 * iy + ic
    chip = 2 * ix + iy

    def half(w):
        r2 = w.shape[0] // 2
        return lax.dynamic_slice_in_dim(w, ic * r2, r2, axis=0).astype(BF16)

    gathered = _allgather8([jnp.broadcast_to(c, (8, D)), half(w_in[0]), half(w_attn_br[0]), half(w_pool_br[0]),
                            half(w_out[0])], "gather_weights", relay=[False, True, True, True, True])
    c_all = gathered[0][:, 0, :]
    wg_in = gathered[1].reshape(N_SHARD, D, IN_W // N_SHARD)
    wab = gathered[2].reshape(N_SHARD, AW, D // N_SHARD).transpose(1, 0, 2).reshape(AW, D)
    wpb = gathered[3].reshape(N_SHARD, AW, D // N_SHARD).transpose(1, 0, 2).reshape(AW, D)
    wout = gathered[4].reshape(D, D)

    mw = 3 * D // N_SHARD
    modp = _mod_partial(c_all, w_ada[0], lax.dynamic_slice_in_dim(b_ada, chip * mw, mw, axis=1))
    mod_all = _allgather8([modp], "gather_mod")[0]
    mod_full = mod_all[::2].transpose(1, 0, 2).reshape(8, 3 * D)
    mod = lax.dynamic_slice_in_dim(mod_full, dev, 1, axis=0)

    chip_half = jnp.stack([chip, ic]).astype(jnp.int32)
    r = _local_step(x[0], loss_target[0], mod, wg_in, wab, wpb, wout, pool_w[0], pool_scale, rel_bias, norm_g,
                    final_g.reshape(1, D), chip_half)

    packed = _pack_small(r["dmod"], r["dnorm_g"], r["dfinal_g"], r["dpool_scale"], r["drel_bias"], r["loss"],
                         r["dpool_w"])
    small_all = _allgather8([packed], "gather_small")[0]
    small = ["b_ada", "norm_g", "final_g", "pool_scale", "rel_bias", "pool_w"]
    shaped = lambda b, n, f, ps, rb, pw: [b, n, f.reshape(1, D), ps, rb, pw.reshape(4 * PGW, PGW)]
    loss, small_out = _small_update(small_all, shaped(b_ada, norm_g, final_g, pool_scale, rel_bias, pool_w),
                                    shaped(m_b_ada, m_norm_g, m_final_g, m_pool_scale, m_rel_bias, m_pool_w),
                                    shaped(v_b_ada, v_norm_g, v_final_g, v_pool_scale, v_rel_bias, v_pool_w))
    dmod_all = small_all[:, PK_BADA:PK_NORMG, :].reshape(8, 3 * D)
    g_w_ada = _w_ada_grad(c_all, lax.dynamic_slice_in_dim(dmod_all, chip * mw, mw, axis=1))

    g_w_in, g_w_ab, g_w_pb, g_w_out = _sibling_join([r["rs_in"], r["rs_attn_br"], r["rs_pool_br"], r["rs_out"]],
                                                    "rs_sibling_join")
    upd = dict(zip(small, small_out))
    upd["final_g"] = [a.reshape(D) for a in upd["final_g"]]
    upd["pool_w"] = [a.reshape(1, 4, PGW, PGW) for a in upd["pool_w"]]
    for nme, w, g, m, v in (("w_ada", w_ada, g_w_ada, m_w_ada, v_w_ada), ("w_in", w_in, g_w_in, m_w_in, v_w_in),
                            ("w_attn_br", w_attn_br, g_w_ab, m_w_attn_br, v_w_attn_br),
                            ("w_pool_br", w_pool_br, g_w_pb, m_w_pool_br, v_w_pool_br),
                            ("w_out", w_out, g_w_out, m_w_out, v_w_out)):
        upd[nme] = [a[None] for a in _adamw(w[0], g, m[0], v[0], "adamw_" + nme)]
    names = ["norm_g", "w_ada", "b_ada", "w_in", "pool_w", "pool_scale", "w_attn_br", "w_pool_br", "w_out",
             "rel_bias", "final_g"]
    return (loss[0, 0], r["grad_x"][None]) + tuple(upd[nme][kind] for kind in range(4) for nme in names)
```

```python
import functools
import math

import numpy as np
import jax
import jax.numpy as jnp
from jax import lax
from jax.experimental import pallas as pl
from jax.experimental.pallas import tpu as pltpu

F32 = jnp.float32
BF16 = jnp.bfloat16

D = 1024
HD = 64
NH = 8
AW = NH * HD
GROUPS = ((128, 1), (512, 4), (2048, 16))
NG = len(GROUPS)
BLK = 128
GW = 3 * AW
QKV_W = NG * GW
REST_W = 3584
IN_W = QKV_W + REST_W
CB = 512
NCB = IN_W // CB
NCB_QKV = QKV_W // CB
POOL_WINDOWS = (2, 4, 8, 16)
PGW = 128
HALO = 16
NUM_BUCKETS = 32
MAX_DISTANCE = 2048
EPS = 1e-6
NEG = -1e30
N_SHARD = 4
VMEM_LIMIT = 56 * 1024 * 1024

ADAM_LR = 0.001
ADAM_B1 = 0.9
ADAM_B2 = 0.999
ADAM_EPS = 1e-08
ADAM_WD = 0.01
ADAM_STEP = 10

PK_BADA, PK_NORMG, PK_FINALG, PK_PSCALE, PK_RELB, PK_LOSS, PK_POOLW, PK_ROWS = 0, 24, 32, 40, 48, 80, 88, 600

ANY = pl.BlockSpec(memory_space=pl.ANY)
MESH = pl.DeviceIdType.MESH


def _params(*sem):
    return pltpu.CompilerParams(dimension_semantics=sem, vmem_limit_bytes=VMEM_LIMIT)


def _sds(shape, dtype=F32):
    return jax.ShapeDtypeStruct(shape, dtype)


def _dot(a, b):
    return jnp.dot(a, b, preferred_element_type=F32)


def _dot_nt(a, b):
    return lax.dot_general(a, b, (((1,), (1,)), ((), ())), preferred_element_type=F32)


def _dot_tn(a, b):
    return lax.dot_general(a, b, (((0,), (0,)), ((), ())), preferred_element_type=F32)


def _sigmoid(z):
    return 0.5 * jnp.tanh(0.5 * z) + 0.5


def _dma_sems(*shape):
    return pltpu.SemaphoreType.DMA(shape)


class _Ride:
    def __init__(self, arrays, out_shapes, n_copies, copies):
        self.arrays, self.out_shapes, self.n_copies, self.copies = list(arrays), list(out_shapes), n_copies, copies


def _call_with_ride(body, ride, first, last, *, in_specs, out_specs, out_shape, scratch_shapes=(), **kw):
    in_specs, out_specs, out_shape, scratch_shapes = list(in_specs), list(out_specs), list(out_shape), list(scratch_shapes)
    n_in, n_out, n_sc = len(in_specs), len(out_specs), len(scratch_shapes)
    if ride is None:
        def run_plain(*operands):
            return pl.pallas_call(body, in_specs=in_specs, out_specs=out_specs, out_shape=out_shape,
                                  scratch_shapes=scratch_shapes, **kw)(*operands), []
        return run_plain
    n_ri, n_ro = len(ride.arrays), len(ride.out_shapes)

    def wrapped(*refs):
        ins, rest = refs[:n_in], refs[n_in:]
        r_ins, rest = rest[:n_ri], rest[n_ri:]
        outs, rest = rest[:n_out], rest[n_out:]
        r_outs, rest = rest[:n_ro], rest[n_ro:]
        scratch, (send_sems, recv_sems) = rest[:n_sc], rest[n_sc:]

        @pl.when(first())
        def _():
            for cp in ride.copies(r_ins, r_outs, send_sems, recv_sems):
                cp.start()

        body(*ins, *outs, *scratch)

        @pl.when(last())
        def _():
            for cp in ride.copies(r_ins, r_outs, send_sems, recv_sems):
                cp.wait()

    def run(*operands):
        res = pl.pallas_call(
            wrapped, in_specs=in_specs + [ANY] * n_ri, out_specs=out_specs + [ANY] * n_ro,
            out_shape=out_shape + ride.out_shapes,
            scratch_shapes=scratch_shapes + [_dma_sems(ride.n_copies), _dma_sems(ride.n_copies)], **kw,
        )(*operands, *ride.arrays)
        return res[:n_out], res[n_out:]
    return run


def _bucket_tables():
    i = np.arange(BLK)[:, None]
    j = np.arange(2 * BLK)[None, :]
    dist = BLK + i - j
    valid = (dist >= 0) & (dist <= BLK)
    tabs = []
    for _, dil in GROUPS:
        n = (np.clip(dist, 0, BLK) * dil).astype(np.int32)
        max_exact = NUM_BUCKETS // 2
        nf = np.maximum(n, 1).astype(np.float32)
        large = max_exact + (np.log(nf / np.float32(max_exact)) / np.float32(math.log(MAX_DISTANCE / max_exact))
                             * np.float32(NUM_BUCKETS - max_exact)).astype(np.int32)
        large = np.minimum(large, NUM_BUCKETS - 1)
        bucket = np.where(n < max_exact, n, large)
        tab = np.where(valid, bucket, -1).astype(np.int32)
        perm = _block_perm(dil)
        tabs.append(tab[perm][:, np.concatenate([perm, BLK + perm])])
    return np.stack(tabs)


def _bias_table(rel_bias, buckets):
    def body(rb_ref, bk_ref, out_ref):
        g = pl.program_id(0)
        bk = bk_ref[...]
        for h in range(NH):
            acc = jnp.full((BLK, 2 * BLK), NEG, F32)
            for b in range(NUM_BUCKETS):
                acc = jnp.where(bk == b, rb_ref[b, g * NH + h], acc)
            out_ref[h] = acc

    return pl.pallas_call(
        body, name="bias_table", grid=(NG,),
        in_specs=[pl.BlockSpec(memory_space=pltpu.SMEM),
                  pl.BlockSpec((None, BLK, 2 * BLK), lambda g: (g, 0, 0))],
        out_specs=pl.BlockSpec((NH, BLK, 2 * BLK), lambda g: (g, 0, 0)),
        out_shape=_sds((NG * NH, BLK, 2 * BLK)),
        compiler_params=_params("arbitrary"),
    )(rel_bias, buckets)


def _bias_grad(ds_acc, buckets, ride):
    def body(acc_ref, bk_ref, out_ref):
        bk = bk_ref[...]
        acc = acc_ref[...]
        lane = lax.broadcasted_iota(jnp.int32, (8, 128), 1)
        out = jnp.zeros((8, 128), F32)
        for b in range(NUM_BUCKETS):
            val = jnp.sum(jnp.where(bk == b, acc, 0.0))
            out = jnp.where(lane == b, val, out)
        out_ref[...] = out

    (out,), rode = _call_with_ride(
        body, ride, lambda: pl.program_id(0) == 0, lambda: pl.program_id(0) == NG * NH - 1,
        name="bias_grad", grid=(NG * NH,),
        in_specs=[pl.BlockSpec((None, BLK, 2 * BLK), lambda gh: (gh, 0, 0)),
                  pl.BlockSpec((None, BLK, 2 * BLK), lambda gh: (gh // NH, 0, 0))],
        out_specs=[pl.BlockSpec((None, 8, 128), lambda gh: (gh, 0, 0))],
        out_shape=[_sds((NG * NH, 8, 128))],
        compiler_params=_params("arbitrary"),
    )(ds_acc, buckets)
    return out, rode


def _mod_partial(c_all, w_ada_s, b_ada_s):
    def body(c_ref, w_ref, b_ref, o_ref):
        o_ref[...] = _dot(c_ref[...].astype(BF16), w_ref[...].astype(BF16)) + b_ref[...]

    return pl.pallas_call(body, name="mod_partial", out_shape=_sds((8, w_ada_s.shape[1])),
                          compiler_params=_params())(c_all, w_ada_s, b_ada_s)


def _prenorm(x, norm_g, mod):
    S = x.shape[0]
    tm = 512

    def body(x_ref, g_ref, mod_ref, h_ref):
        xv = x_ref[...]
        r = lax.rsqrt(jnp.mean(xv * xv, axis=-1, keepdims=True) + EPS)
        n1 = xv * r * g_ref[...]
        h_ref[...] = (n1 * (1.0 + mod_ref[:, D:2 * D]) + mod_ref[:, 0:D]).astype(BF16)

    return pl.pallas_call(
        body, name="prenorm", grid=(S // tm,),
        in_specs=[pl.BlockSpec((tm, D), lambda i: (i, 0)), pl.BlockSpec((1, D), lambda i: (0, 0)),
                  pl.BlockSpec((1, 3 * D), lambda i: (0, 0))],
        out_specs=pl.BlockSpec((tm, D), lambda i: (i, 0)),
        out_shape=_sds((S, D), BF16), compiler_params=_params("parallel"),
    )(x, norm_g, mod)


def _proj(h, wg_in, j0, nj, dtype, name):
    S = h.shape[0]
    tm = 2048
    per = wg_in.shape[2] // CB

    def body(h_ref, w_ref, o_ref):
        o_ref[...] = _dot(h_ref[...], w_ref[...]).astype(dtype)

    return pl.pallas_call(
        body, name=name, grid=(S // tm, nj),
        in_specs=[pl.BlockSpec((tm, D), lambda m, j: (m, 0)),
                  pl.BlockSpec((None, D, CB), lambda m, j: ((j0 + j) // per, 0, (j0 + j) % per))],
        out_specs=pl.BlockSpec((tm, CB), lambda m, j: (m, j)),
        out_shape=_sds((S, nj * CB), dtype), compiler_params=_params("parallel", "parallel"),
    )(h, wg_in)


HS = 4
SLAB = HS * HD


def _lane_head(rows):
    return lax.broadcasted_iota(jnp.int32, (rows, SLAB), 1) // HD


def _head_stack(a):
    head = _lane_head(a.shape[0])
    return jnp.concatenate([jnp.where(head == h, a, jnp.zeros_like(a)) for h in range(HS)], axis=0)


def _head_unstack(a):
    rows = a.shape[0] // HS
    head = _lane_head(rows)
    out = a[:rows]
    for h in range(1, HS):
        out = jnp.where(head == h, a[h * rows:(h + 1) * rows], out)
    return out


STAT_W = 128
VIEW = 16


def _sub_layout(dil):
    if dil == 1:
        return BLK, [None]
    return BLK * dil // VIEW, [[r + dil * u for u in range(VIEW // dil)] for r in range(dil)]


def _block_perm(dil):
    a_rows, _ = _sub_layout(dil)
    p = np.arange(BLK)
    return p if dil == 1 else (VIEW // dil) * (p % a_rows) + p // a_rows


LB = 128
N_SLAB = NH // HS


def _ld(refs, bs, s, w):
    if bs is None:
        return refs[0][:, s * w:(s + 1) * w]
    a_rows = refs[0].shape[0] // VIEW
    return jnp.concatenate([jnp.concatenate([ref[pl.ds(b, a_rows, stride=VIEW), :] for b in bs], axis=0)
                            for ref in refs], axis=1)


def _st(ref, bs, s, val):
    if bs is None:
        ref[:, s * SLAB:(s + 1) * SLAB] = val.astype(ref.dtype)
        return
    a_rows = val.shape[0] // len(bs)
    for u, b in enumerate(bs):
        ref[:, b, :] = val[u * a_rows:(u + 1) * a_rows]


def _attn_views(dil, S):
    a_rows, subs = _sub_layout(dil)
    if dil == 1:
        def ispecs(base, w, f):
            return [pl.BlockSpec((BLK, N_SLAB * w), lambda sg, n: (f(n), base // (N_SLAB * w)))]
        return subs, S // BLK, N_SLAB, ispecs, (lambda w: (S, w)), (
            lambda f: pl.BlockSpec((BLK, AW), lambda sg, n: (f(n), 0)))

    def ispecs(base, w, f):
        return [pl.BlockSpec((a_rows * VIEW, LB), lambda sg, n, k=k: (f(n), (base + sg * w) // LB + k))
                for k in range(w // LB)]
    return subs, S // (a_rows * VIEW), 1, ispecs, (lambda w: (S // VIEW, VIEW, w)), (
        lambda f: pl.BlockSpec((a_rows, VIEW, SLAB), lambda sg, n: (f(n), 0, sg)))


def _attn_fwd(qkv_g, bias_tab, g):
    S = qkv_g.shape[0]
    subs, nbq, sps, ispecs, shape, ospec = _attn_views(GROUPS[g][1], S)
    cur = lambda n: n
    in_specs = [ispecs(0, SLAB, cur), ispecs(AW, SLAB, cur), ispecs(2 * AW, SLAB, cur)]
    nl = len(in_specs[0])

    def body(*refs):
        q, k, v = (refs[t * nl:(t + 1) * nl] for t in range(3))
        b_ref, o_ref, l_ref, kprev, vprev = refs[3 * nl:]
        n = pl.program_id(1)

        @pl.when(n == 0)
        def _():
            kprev[...] = jnp.zeros_like(kprev)
            vprev[...] = jnp.zeros_like(vprev)

        col = lax.broadcasted_iota(jnp.int32, (HS * BLK, 2 * BLK), 1)
        keep = (col >= BLK) | (n > 0)
        for s_ in range(sps):
            cs = slice(s_ * SLAB, (s_ + 1) * SLAB)
            bias = b_ref[pl.ds(s_ * HS, HS)].reshape(HS * BLK, 2 * BLK)
            for i, bs in enumerate(subs):
                kc, vc = _ld(k, bs, s_, SLAB).astype(BF16), _ld(v, bs, s_, SLAB).astype(BF16)
                kb = jnp.concatenate([kprev[i, :, cs], kc], axis=0)
                vb = jnp.concatenate([vprev[i, :, cs], vc], axis=0)
                kprev[i, :, cs], vprev[i, :, cs] = kc, vc
                s = _dot_nt(_head_stack(_ld(q, bs, s_, SLAB).astype(BF16)), kb) * (HD ** -0.5) + bias
                s = jnp.where(keep, s, NEG)
                m = jnp.max(s, axis=-1, keepdims=True)
                p = jnp.exp(s - m)
                den = jnp.sum(p, axis=-1, keepdims=True)
                _st(o_ref, bs, s_, _head_unstack(_dot(p.astype(BF16), vb) / den))
                _st(l_ref, bs, s_, _head_unstack(jnp.broadcast_to(m + jnp.log(den), (HS * BLK, SLAB))))

    out = _sds(shape(AW))
    nsg = N_SLAB // sps
    o, l = pl.pallas_call(
        body, name=f"attn_fwd{g}", grid=(nsg, nbq),
        in_specs=sum(in_specs, []) + [pl.BlockSpec((sps * HS, BLK, 2 * BLK), lambda sg, n: (g * nsg + sg, 0, 0))],
        out_specs=[ospec(cur), ospec(cur)],
        out_shape=[out, out],
        scratch_shapes=[pltpu.VMEM((len(subs), BLK, sps * SLAB), BF16)] * 2,
        compiler_params=_params("parallel", "arbitrary"),
    )(*([qkv_g] * (3 * nl)), bias_tab)
    return o.reshape(S, AW), l.reshape(S, AW)


def _attn_bwd(qkv_g, dattn, stats, bias_tab, g, ride):
    S = qkv_g.shape[0]
    subs, nbq, sps, ispecs, shape, ospec = _attn_views(GROUPS[g][1], S)
    cur = lambda n: jnp.minimum(n, nbq - 1)
    late = lambda n: jnp.maximum(n - 1, 0)
    in_specs = [ispecs(0, SLAB, cur), ispecs(AW, SLAB, cur), ispecs(2 * AW, SLAB, cur), ispecs(0, SLAB, cur),
                ispecs(0, STAT_W, cur)]
    nl = len(in_specs[0])

    def body(*refs):
        q, k, v, da = (refs[t * nl:(t + 1) * nl] for t in range(4))
        st_ref, b_ref, dq_ref, dk_ref, dv_ref, ds_ref, ck_ref, cv_ref, kprev, vprev = refs[4 * nl:]
        n = pl.program_id(1)

        @pl.when(n == 0)
        def _():
            for ref in (ds_ref, ck_ref, cv_ref, kprev, vprev):
                ref[...] = jnp.zeros_like(ref)

        @pl.when(n < nbq)
        def _():
            col = lax.broadcasted_iota(jnp.int32, (HS * BLK, 2 * BLK), 1)
            keep = (col >= BLK) | (n > 0)
            for s_ in range(sps):
                cs = slice(s_ * SLAB, (s_ + 1) * SLAB)
                bias = b_ref[pl.ds(s_ * HS, HS)].reshape(HS * BLK, 2 * BLK)
                for i, bs in enumerate(subs):
                    st = _ld((st_ref,), bs, s_, STAT_W)
                    kc, vc = _ld(k, bs, s_, SLAB).astype(BF16), _ld(v, bs, s_, SLAB).astype(BF16)
                    kb = jnp.concatenate([kprev[i, :, cs], kc], axis=0)
                    vb = jnp.concatenate([vprev[i, :, cs], vc], axis=0)
                    kprev[i, :, cs], vprev[i, :, cs] = kc, vc
                    lse =jnp.concatenate([st[:, h:h + 1] for h in range(HS)], axis=0)
                    delta = jnp.concatenate([st[:, HS + h:HS + h + 1] for h in range(HS)], axis=0)
                    qs = _head_stack(_ld(q, bs, s_, SLAB).astype(BF16))
                    dos = _head_stack(_ld(da, bs, s_, SLAB).astype(BF16))
                    s = _dot_nt(qs, kb) * (HD ** -0.5) + bias
                    s = jnp.where(keep, s, NEG)
                    p = jnp.exp(s - lse)
                    ds = p * (_dot_nt(dos, vb) - delta)
                    ds_ref[pl.ds(s_ * HS, HS)] += ds.reshape(HS, BLK, 2 * BLK)
                    ds_b = (ds * (HD ** -0.5)).astype(BF16)
                    _st(dq_ref, bs, s_, _head_unstack(_dot(ds_b, kb)))
                    dkb = _dot_tn(ds_b, qs)
                    dvb = _dot_tn(p.astype(BF16), dos)
                    _st(dk_ref, bs, s_, ck_ref[i, :, cs] + dkb[:BLK])
                    _st(dv_ref, bs, s_, cv_ref[i, :, cs] + dvb[:BLK])
                    ck_ref[i, :, cs] = dkb[BLK:]
                    cv_ref[i, :, cs] = dvb[BLK:]

        @pl.when(n == nbq)
        def _():
            for s_ in range(sps):
                for i, bs in enumerate(subs):
                    _st(dk_ref, bs, s_, ck_ref[i, :, s_ * SLAB:(s_ + 1) * SLAB])
                    _st(dv_ref, bs, s_, cv_ref[i, :, s_ * SLAB:(s_ + 1) * SLAB])

    out = _sds(shape(AW), BF16 if GROUPS[g][1] == 1 else F32)
    nsg = N_SLAB // sps
    (dq, dk, dv, ds_acc), rode = _call_with_ride(
        body, ride, lambda: (pl.program_id(0) == 0) & (pl.program_id(1) == 0),
        lambda: (pl.program_id(0) == nsg - 1) & (pl.program_id(1) == nbq),
        name=f"attn_bwd{g}", grid=(nsg, nbq + 1),
        in_specs=sum(in_specs, []) + [pl.BlockSpec((sps * HS, BLK, 2 * BLK), lambda sg, n: (g * nsg + sg, 0, 0))],
        out_specs=[ospec(cur), ospec(late), ospec(late),
                   pl.BlockSpec((sps * HS, BLK, 2 * BLK), lambda sg, n: (sg, 0, 0))],
        out_shape=[out] * 3 + [_sds((NH, BLK, 2 * BLK))],
        scratch_shapes=[pltpu.VMEM((len(subs), BLK, sps * SLAB), F32)] * 2
        + [pltpu.VMEM((len(subs), BLK, sps * SLAB), BF16)] * 2,
        compiler_params=_params("arbitrary", "arbitrary"),
    )(*([qkv_g] * (3 * nl)), *([dattn] * nl), stats, bias_tab)
    return [dq.reshape(S, AW), dk.reshape(S, AW), dv.reshape(S, AW)], ds_acc, rode


TM_MIX = 256


def _mix_specs(tm):
    row512 = pl.BlockSpec((tm, AW), lambda i: (i, 0))
    return ([row512] * 6 + [
        pl.BlockSpec((tm, REST_W), lambda i: (i, 0)),
        pl.BlockSpec((HALO, AW), lambda i: (jnp.maximum(i * (tm // HALO) - 1, 0), 1)),
        pl.BlockSpec((AW, D), lambda i: (0, 0)), pl.BlockSpec((AW, D), lambda i: (0, 0)),
        pl.BlockSpec((4, PGW, PGW), lambda i: (0, 0, 0)), pl.BlockSpec((1, AW), lambda i: (0, 0))])


def _mix_forward(i, tm, o_refs, l_refs, rest_ref, halo_ref, wab_ref, wpb_ref, pw_ref, ps_ref):
    l0, l1, l2 = (r[...] for r in l_refs)
    mx = jnp.maximum(jnp.maximum(l0, l1), l2)
    e0, e1, e2 = jnp.exp(l0 - mx), jnp.exp(l1 - mx), jnp.exp(l2 - mx)
    den = e0 + e1 + e2
    lj = mx + jnp.log(den)
    attn = (e0 * o_refs[0][...] + e1 * o_refs[1][...] + e2 * o_refs[2][...]) / den

    z_attn = rest_ref[:, 0:AW]
    u = rest_ref[:, AW:2 * AW]
    z_pool = rest_ref[:, 2 * AW:3 * AW]
    g_attn = rest_ref[:, 3 * AW:3 * AW + D]
    g_pool = rest_ref[:, 3 * AW + D:3 * AW + 2 * D]

    sg_a = _sigmoid(z_attn)
    sil_a = z_attn * sg_a
    a_g = (attn * sil_a).astype(BF16)
    y_attn = _dot(a_g, wab_ref[...])

    halo = jnp.where(i > 0, halo_ref[...], 0.0)
    ext = jnp.concatenate([halo, u], axis=0)
    t = i * tm + lax.broadcasted_iota(jnp.int32, (tm, 1), 0)
    pooled, mixed_raw = [], []
    for gi, win in enumerate(POOL_WINDOWS):
        s = ext[:, gi * PGW:(gi + 1) * PGW]
        sh = 1
        while sh < win:
            s = s + pltpu.roll(s, sh, 0)
            sh *= 2
        cnt = jnp.minimum(t + 1, win).astype(F32)
        pg = s[HALO:] / cnt - u[:, gi * PGW:(gi + 1) * PGW]
        pooled.append(pg.astype(BF16))
        mixed_raw.append(_dot(pooled[-1], pw_ref[gi].astype(BF16)))
    mixed_raw = jnp.concatenate(mixed_raw, axis=1)
    mixed = mixed_raw * ps_ref[...]
    sg_p = _sigmoid(z_pool)
    sil_p = z_pool * sg_p
    m_g = (mixed * sil_p).astype(BF16)
    y_pool = _dot(m_g, wpb_ref[...])

    sa = _sigmoid(g_attn)
    sp = _sigmoid(g_pool)
    merged = sa * y_attn + sp * y_pool
    return dict(lj=lj, attn=attn, z_attn=z_attn, z_pool=z_pool, sg_a=sg_a, sil_a=sil_a, a_g=a_g, y_attn=y_attn,
                pooled=pooled, mixed_raw=mixed_raw, mixed=mixed, sg_p=sg_p, sil_p=sil_p, m_g=m_g, y_pool=y_pool,
                sa=sa, sp=sp, merged=merged)


def _mix_step(x, target, os_, ls_, rest, wab, wpb, pool_w, pool_scale, wout, mod, final_g):
    S = x.shape[0]
    tm = TM_MIX
    nt = S // tm
    sw = D // N_SHARD

    def body(o0, o1, o2, l0, l1, l2, rest_ref, halo_ref, wab_ref, wpb_ref, pw_ref, ps_ref,
             x_ref, t_ref, wo_ref, mod_ref, fg_ref, dx2_ref, loss_ref, dfg_ref, dgate_ref,
             dattn_ref, stats_ref, dpooled_ref, drest_ref, dwo_hbm, dwab_hbm, dwpb_hbm, dpw_ref, dps_ref,
             awo, awab, awpb):
        i = pl.program_id(0)

        @pl.when(i == 0)
        def _():
            for ref in (loss_ref, dfg_ref, dgate_ref, awo, awab, awpb, dpw_ref, dps_ref):
                ref[...] = jnp.zeros_like(ref)

        f = _mix_forward(i, tm, (o0, o1, o2), (l0, l1, l2), rest_ref, halo_ref, wab_ref, wpb_ref, pw_ref, ps_ref)
        mo = _dot(f["merged"].astype(BF16), wo_ref[...])
        gate = mod_ref[:, 2 * D:3 * D]
        fg = fg_ref[...]
        x2 = x_ref[...] + gate * mo
        r2 = lax.rsqrt(jnp.mean(x2 * x2, axis=-1, keepdims=True) + EPS)
        n2 = x2 * r2
        err = n2 * fg - t_ref[...]
        loss_ref[...] += 0.5 * jnp.sum(jnp.mean(err * err, axis=-1, keepdims=True))
        dy = err * (1.0 / D)
        dfg_ref[...] += jnp.sum(dy * n2, axis=0, keepdims=True)
        dn = dy * fg
        dx2 = r2 * (dn - n2 * jnp.mean(dn * n2, axis=-1, keepdims=True))
        dgate_ref[...] += jnp.sum(dx2 * mo, axis=0, keepdims=True)
        dx2_ref[...] = dx2

        dmo_b = (dx2 * gate).astype(BF16)
        dmerged = _dot_nt(dmo_b, wo_ref[...])
        awo[...] += _dot_tn(f["merged"].astype(BF16), dmo_b)
        sa, sp = f["sa"], f["sp"]
        dya = (dmerged * sa).astype(BF16)
        dyp = (dmerged * sp).astype(BF16)
        dg_attn = dmerged * f["y_attn"] * sa * (1.0 - sa)
        dg_pool = dmerged * f["y_pool"] * sp * (1.0 - sp)
        dag = _dot_nt(dya, wab_ref[...])
        awab[...] += _dot_tn(f["a_g"], dya)
        dmg = _dot_nt(dyp, wpb_ref[...])
        awpb[...] += _dot_tn(f["m_g"], dyp)
        dattn = dag * f["sil_a"]
        dattn_ref[...] = dattn
        prod = dattn * f["attn"]
        lane = lax.broadcasted_iota(jnp.int32, (tm, STAT_W), 1)
        for sb in range(N_SLAB):
            st = jnp.zeros((tm, STAT_W), F32)
            for h in range(HS):
                hs = slice((sb * HS + h) * HD, (sb * HS + h + 1) * HD)
                st = jnp.where(lane == h, f["lj"][:, hs.start:hs.start + 1], st)
                st = jnp.where(lane == HS + h, jnp.sum(prod[:, hs], axis=-1, keepdims=True), st)
            stats_ref[:, sb * STAT_W:(sb + 1) * STAT_W] = st
        dz_attn = dag * f["attn"] * (f["sg_a"] * (1.0 + f["z_attn"] * (1.0 - f["sg_a"])))
        dmixed = dmg * f["sil_p"]
        dz_pool = dmg * f["mixed"] * (f["sg_p"] * (1.0 + f["z_pool"] * (1.0 - f["sg_p"])))
        dps_ref[...] += jnp.sum(dmixed * f["mixed_raw"], axis=0, keepdims=True)
        dpm = (dmixed * ps_ref[...]).astype(BF16)
        for gi in range(len(POOL_WINDOWS)):
            cs = slice(gi * PGW, (gi + 1) * PGW)
            dpw_ref[gi] += _dot_tn(f["pooled"][gi], dpm[:, cs])
            dpooled_ref[:, cs] = _dot_nt(dpm[:, cs], pw_ref[gi].astype(BF16))
        drest_ref[:, 0:AW] = dz_attn.astype(BF16)
        drest_ref[:, AW:2 * AW] = jnp.zeros((tm, AW), BF16)
        drest_ref[:, 2 * AW:3 * AW] = dz_pool.astype(BF16)
        drest_ref[:, 3 * AW:3 * AW + D] = dg_attn.astype(BF16)
        drest_ref[:, 3 * AW + D:3 * AW + 2 * D] = dg_pool.astype(BF16)

        @pl.when(i == nt - 1)
        def _():
            pltpu.sync_copy(awo, dwo_hbm)
            for k in range(N_SHARD):
                pltpu.sync_copy(awab.at[:, pl.ds(k * sw, sw)], dwab_hbm.at[k])
                pltpu.sync_copy(awpb.at[:, pl.ds(k * sw, sw)], dwpb_hbm.at[k])

    row = pl.BlockSpec((tm, D), lambda i: (i, 0))
    vec = pl.BlockSpec((1, D), lambda i: (0, 0))
    row512 = pl.BlockSpec((tm, AW), lambda i: (i, 0))
    outs = pl.pallas_call(
        body, name="mix_step", grid=(nt,),
        in_specs=_mix_specs(tm) + [row, row, pl.BlockSpec((D, D), lambda i: (0, 0)),
                                   pl.BlockSpec((1, 3 * D), lambda i: (0, 0)), vec],
        out_specs=[row, pl.BlockSpec((8, 128), lambda i: (0, 0)), vec, vec,
                   row512, pl.BlockSpec((tm, N_SLAB * STAT_W), lambda i: (i, 0)), row512,
                   pl.BlockSpec((tm, REST_W), lambda i: (i, 0)), ANY, ANY, ANY,
                   pl.BlockSpec((4, PGW, PGW), lambda i: (0, 0, 0)), pl.BlockSpec((1, AW), lambda i: (0, 0))],
        out_shape=[_sds((S, D)), _sds((8, 128)), _sds((1, D)), _sds((1, D)),
                   _sds((S, AW)), _sds((S, N_SLAB * STAT_W)), _sds((S, AW)), _sds((S, REST_W), BF16),
                   _sds((D, D)), _sds((N_SHARD, AW, sw)), _sds((N_SHARD, AW, sw)), _sds((4, PGW, PGW)), _sds((1, AW))],
        scratch_shapes=[pltpu.VMEM((D, D), F32), pltpu.VMEM((AW, D), F32), pltpu.VMEM((AW, D), F32)],
        compiler_params=_params("arbitrary"),
    )(*os_, *ls_, rest, rest, wab, wpb, pool_w, pool_scale, x, target, wout, mod, final_g)
    dx2, loss, dfg, dgate, dattn, stats, dpooled, drest, dwo, dwab, dwpb, dpw, dps = outs
    return (dx2, loss, dfg, dgate, dattn, stats, dpooled, drest, dwo.reshape(N_SHARD, D // N_SHARD, D), dwab, dwpb,
            dpw, dps)


def _pool_bwd(dpooled):
    S = dpooled.shape[0]
    tm = 512
    nt = S // tm

    def body(dp_ref, nxt_ref, du_ref):
        i = pl.program_id(0)
        t = i * tm + lax.broadcasted_iota(jnp.int32, (tm + HALO, 1), 0)
        nxt = jnp.where(i < nt - 1, nxt_ref[...], 0.0)
        ext = jnp.concatenate([dp_ref[...], nxt], axis=0)
        for gi, win in enumerate(POOL_WINDOWS):
            cs = slice(gi * PGW, (gi + 1) * PGW)
            s = ext[:, cs] / jnp.minimum(t + 1, win).astype(F32)
            sh = 1
            while sh < win:
                s = s + pltpu.roll(s, tm + HALO - sh, 0)
                sh *= 2
            du_ref[:, cs] = (s[:tm] - dp_ref[:, cs]).astype(BF16)

    return pl.pallas_call(
        body, name="pool_bwd", grid=(nt,),
        in_specs=[pl.BlockSpec((tm, AW), lambda i: (i, 0)),
                  pl.BlockSpec((HALO, AW), lambda i: (jnp.minimum((i + 1) * (tm // HALO), S // HALO - 1), 0))],
        out_specs=pl.BlockSpec((tm, AW), lambda i: (i, 0)),
        out_shape=_sds((S, AW), BF16), compiler_params=_params("parallel"),
    )(dpooled, dpooled)


TB = 1024


def _dh(dproj, wg_in, ride):
    S = dproj.shape[0]
    per = wg_in.shape[2] // TB
    nm, nk = S // TB, IN_W // TB

    def body(dp_ref, w_ref, out_ref):
        @pl.when(pl.program_id(1) == 0)
        def _():
            out_ref[...] = jnp.zeros_like(out_ref)

        out_ref[...] += _dot_nt(dp_ref[...], w_ref[...])

    (dh,), rode = _call_with_ride(
        body, ride, lambda: (pl.program_id(0) == 0) & (pl.program_id(1) == 0),
        lambda: (pl.program_id(0) == nm - 1) & (pl.program_id(1) == nk - 1),
        name="dh", grid=(nm, nk),
        in_specs=[pl.BlockSpec((TB, TB), lambda m, kk: (m, kk)),
                  pl.BlockSpec((None, D, TB), lambda m, kk: (kk // per, 0, kk % per))],
        out_specs=[pl.BlockSpec((TB, D), lambda m, kk: (m, 0))],
        out_shape=[_sds((S, D))], compiler_params=_params("arbitrary", "arbitrary"),
    )(dproj, wg_in)
    return dh, rode


def _dw_in(h, dproj):
    S = dproj.shape[0]
    per = IN_W // N_SHARD // TB

    def body(h_ref, dp_ref, out_ref):
        @pl.when(pl.program_id(1) == 0)
        def _():
            out_ref[...] = jnp.zeros_like(out_ref)

        out_ref[...] += _dot_tn(h_ref[...], dp_ref[...])

    return pl.pallas_call(
        body, name="dw_in", grid=(IN_W // TB, S // TB),
        in_specs=[pl.BlockSpec((TB, D), lambda j, kk: (kk, 0)), pl.BlockSpec((TB, TB), lambda j, kk: (kk, j))],
        out_specs=pl.BlockSpec((None, D, TB), lambda j, kk: (j // per, 0, j % per)),
        out_shape=_sds((N_SHARD, D, IN_W // N_SHARD)), compiler_params=_params("parallel", "arbitrary"),
    )(h, dproj)


def _prenorm_bwd(x, dh, dx2, norm_g, mod):
    S = x.shape[0]
    tm = 512

    def body(x_ref, dh_ref, dx2_ref, g_ref, mod_ref, gx_ref, dg_ref, dshift_ref, dscale_ref):
        i = pl.program_id(0)

        @pl.when(i == 0)
        def _():
            dg_ref[...] = jnp.zeros_like(dg_ref)
            dshift_ref[...] = jnp.zeros_like(dshift_ref)
            dscale_ref[...] = jnp.zeros_like(dscale_ref)

        xv = x_ref[...]
        dhv = dh_ref[...]
        g = g_ref[...]
        r = lax.rsqrt(jnp.mean(xv * xv, axis=-1, keepdims=True) + EPS)
        xh = xv * r
        dshift_ref[...] += jnp.sum(dhv, axis=0, keepdims=True)
        dscale_ref[...] += jnp.sum(dhv * (xh * g), axis=0, keepdims=True)
        dn1 = dhv * (1.0 + mod_ref[:, D:2 * D])
        dg_ref[...] += jnp.sum(dn1 * xh, axis=0, keepdims=True)
        dxh = dn1 * g
        gx_ref[...] = dx2_ref[...] + r * (dxh - xh * jnp.mean(dxh * xh, axis=-1, keepdims=True))

    row = pl.BlockSpec((tm, D), lambda i: (i, 0))
    vec = pl.BlockSpec((1, D), lambda i: (0, 0))
    return pl.pallas_call(
        body, name="prenorm_bwd", grid=(S // tm,),
        in_specs=[row, row, row, vec, pl.BlockSpec((1, 3 * D), lambda i: (0, 0))],
        out_specs=[row, vec, vec, vec],
        out_shape=[_sds((S, D)), _sds((1, D)), _sds((1, D)), _sds((1, D))],
        compiler_params=_params("arbitrary"),
    )(x, dh, dx2, norm_g, mod)


def _local_step(x, target, mod, wg_in, wab, wpb, wout, pool_w, pool_scale, rel_bias, norm_g, final_g, chip_half):
    buckets = jnp.asarray(_bucket_tables())
    bias_tab = _bias_table(rel_bias, buckets)
    h = _prenorm(x, norm_g, mod)
    qkv = [_proj(h, wg_in, 3 * g, 3, BF16 if GROUPS[g][1] == 1 else F32, f"proj_qkv{g}") for g in range(NG)]
    rest = _proj(h, wg_in, NCB_QKV, REST_W // CB, F32, "proj_rest")
    os_, ls_ = zip(*[_attn_fwd(qkv[g], bias_tab, g) for g in range(NG)])
    (dx2, loss, dfinal_g, dgate, dattn, stats, dpooled, drest, dw_out, dw_ab, dw_pb, dpool_w,
     dpool_scale) = _mix_step(x, target, os_, ls_, rest, wab, wpb, pool_w, pool_scale, wout, mod, final_g)
    du = _pool_bwd(dpooled)

    small = [dw_ab, dw_pb, dw_out]
    dqkv0, ds0, sib_small = _attn_bwd(qkv[0], dattn, stats, bias_tab, 0, _ride_sibling_halves(small))
    p_small = _pair_sum_small(small, sib_small, chip_half)
    dqkv1, ds1, u_small = _attn_bwd(qkv[1], dattn, stats, bias_tab, 1,
                                    _ride_chip_exchange([p16 for _, p16 in p_small]))
    rs_ab, rs_pb, rs_out = _chip_sum_small([p32 for p32, _ in p_small], u_small, chip_half)
    dqkv2, ds2, _ = _attn_bwd(qkv[2], dattn, stats, bias_tab, 2, None)

    dproj = jnp.concatenate([a.astype(BF16) for a in dqkv0 + dqkv1 + dqkv2] + [drest[:, :AW], du, drest[:, 2 * AW:]],
                            axis=1)
    dw_in = _dw_in(h, dproj)
    drel_rows, (sib_in,) = _bias_grad(jnp.concatenate([ds0, ds1, ds2], axis=0), buckets,
                                      _ride_sibling_halves([dw_in]))
    drel = drel_rows[:, 0, :NUM_BUCKETS].T
    p32_in, p16_in = _pair_sum(dw_in, sib_in, chip_half, "rs_pair_sum_in")
    dh, (u_in,) = _dh(dproj, wg_in, _ride_chip_exchange([p16_in]))
    rs_in = _chip_sum(p32_in, u_in, chip_half, "rs_chip_sum_in")

    grad_x, dnorm_g, dshift, dscale = _prenorm_bwd(x, dh, dx2, norm_g, mod)
    dmod = jnp.concatenate([dshift, dscale, dgate], axis=1)
    return dict(loss=loss[0, 0], grad_x=grad_x, dmod=dmod, dnorm_g=dnorm_g, dfinal_g=dfinal_g, dpool_w=dpool_w,
                dpool_scale=dpool_scale, drel_bias=drel, dw_in=dw_in, dw_attn_br=dw_ab, dw_pool_br=dw_pb,
                dw_out=dw_out, rs_in=rs_in, rs_attn_br=rs_ab, rs_pool_br=rs_pb, rs_out=rs_out)


def _allgather8(blocks, name, relay=None):
    nb = len(blocks)
    relay = [False] * nb if relay is None else list(relay)

    def body(*refs):
        ins, outs = refs[:nb], refs[nb:2 * nb]
        send_sems, recv_sems = refs[2 * nb:]
        x, y, c = lax.axis_index("x"), lax.axis_index("y"), lax.axis_index("c")
        me, sibling = (x, y, c), (x, y, 1 - c)
        here, xn, yn, dg = (x, y), (1 - x, y), (x, 1 - y), (1 - x, 1 - y)

        def slot(a, chip, core, half=None):
            ref = outs[a].at[4 * chip[0] + 2 * chip[1] + core]
            if half is None:
                return ref
            r2 = ref.shape[0] // 2
            return ref.at[pl.ds(half * r2, r2)]

        def copy(a, k, dst, to, src=None):
            return pltpu.make_async_remote_copy(src_ref=dst if src is None else src, dst_ref=dst,
                                                send_sem=send_sems.at[a, k], recv_sem=recv_sems.at[a, k],
                                                device_id=to, device_id_type=MESH)

        def start(cps):
            for cp in cps:
                cp.start()
            return cps

        sent = []
        for a in range(nb):
            own = slot(a, here, c)
            sent += [copy(a, 0, own, sibling, src=ins[a]), copy(a, 1, own, (*xn, c), src=ins[a]),
                     copy(a, 2, own, (*yn, c), src=ins[a])]
            if not relay[a]:
                sent.append(copy(a, 3, own, (*dg, c), src=ins[a]))
        start(sent)
        for a in range(nb):
            copy(a, 2, slot(a, yn, c), me).wait_recv()
            sent += start([copy(a, 6, slot(a, yn, c), sibling)]
                          + ([copy(a, 3, slot(a, yn, c, 0), (*xn, c))] if relay[a] else []))
        for a in range(nb):
            copy(a, 1, slot(a, xn, c), me).wait_recv()
            sent += start([copy(a, 5, slot(a, xn, c), sibling)]
                          + ([copy(a, 4, slot(a, xn, c, 1), (*yn, c))] if relay[a] else []))
        for a in range(nb):
            for k, half in ((3, 0), (4, 1)) if relay[a] else ((3, None),):
                copy(a, k, slot(a, dg, c, half), me).wait_recv()
                sent += start([copy(a, 4 + k, slot(a, dg, c, half), sibling)])
        for a in range(nb):
            copy(a, 0, slot(a, here, 1 - c), me).wait_recv()
            copy(a, 5, slot(a, xn, 1 - c), me).wait_recv()
            copy(a, 6, slot(a, yn, 1 - c), me).wait_recv()
            for k, half in ((7, 0), (8, 1)) if relay[a] else ((7, None),):
                copy(a, k, slot(a, dg, 1 - c, half), me).wait_recv()
        for cp in sent:
            cp.wait_send()

    outs = pl.pallas_call(
        body, name=name, in_specs=[ANY] * nb, out_specs=[ANY] * nb,
        out_shape=[_sds((8,) + b.shape, b.dtype) for b in blocks],
        scratch_shapes=[_dma_sems(nb, 9), _dma_sems(nb, 9)],
    )(*blocks)
    return [_place_own(buf, b) for buf, b in zip(outs, blocks)]


def _place_own(buf, block):
    dev = 4 * lax.axis_index("x") + 2 * lax.axis_index("y") + lax.axis_index("c")
    return lax.dynamic_update_index_in_dim(buf, block, dev, 0)


def _ride_sibling_halves(gs):
    def copies(ins, outs, send_sems, recv_sems):
        x, y, c = lax.axis_index("x"), lax.axis_index("y"), lax.axis_index("c")
        cps = []
        for a in range(len(gs)):
            r2 = ins[a].shape[1] // 2
            other = ins[a].at[:, pl.ds((1 - c) * r2, r2), :]
            cps.append(pltpu.make_async_remote_copy(src_ref=other, dst_ref=outs[a], send_sem=send_sems.at[a],
                                                    recv_sem=recv_sems.at[a], device_id=(x, y, 1 - c),
                                                    device_id_type=MESH))
        return cps

    return _Ride(gs, [_sds((g.shape[0], g.shape[1] // 2, g.shape[2]), g.dtype) for g in gs], len(gs), copies)


def _pair_sum(g, t, chip_half, name):
    nsh, rows, cols = g.shape
    r2 = rows // 2
    tr = _row_tile(r2, cols)
    nt = r2 // tr

    def body(ch_ref, g_ref, t_ref, p32_ref, p16_ref):
        p = g_ref[...] + t_ref[...]
        p16_ref[...] = p.astype(BF16)

        @pl.when(pl.program_id(1) == ch_ref[0])
        def _():
            p32_ref[...] = p

    blk = pl.BlockSpec((None, tr, cols), lambda i, k, ch_ref: (k, i, 0))
    return pl.pallas_call(
        body, name=name,
        grid_spec=pltpu.PrefetchScalarGridSpec(
            num_scalar_prefetch=1, grid=(nt, nsh),
            in_specs=[pl.BlockSpec((None, tr, cols), lambda i, k, ch_ref: (k, ch_ref[1] * nt + i, 0)), blk],
            out_specs=[pl.BlockSpec((tr, cols), lambda i, k, ch_ref: (i, 0)), blk]),
        out_shape=[_sds((r2, cols)), _sds((nsh, r2, cols), BF16)],
        compiler_params=_params("parallel", "arbitrary"),
    )(chip_half, g, t)


def _pair_sum_small(gs, ts, chip_half):
    na = len(gs)

    def body(ch_ref, *refs):
        g_refs, t_refs, outs = refs[:na], refs[na:2 * na], refs[2 * na:]
        for a in range(na):
            r2 = t_refs[a].shape[1]
            own = pl.ds(pl.multiple_of(ch_ref[1] * r2, 8), r2)
            outs[2 * a + 1][...] = (g_refs[a][:, own, :] + t_refs[a][...]).astype(BF16)
            outs[2 * a][...] = g_refs[a][ch_ref[0], own, :] + t_refs[a][ch_ref[0]]

    res = pl.pallas_call(
        body, name="rs_pair_sum_small",
        in_specs=[pl.BlockSpec(memory_space=pltpu.SMEM)] + [pl.BlockSpec(memory_space=pltpu.VMEM)] * (2 * na),
        out_shape=[s for t in ts for s in (_sds(t.shape[1:]), _sds(t.shape, BF16))], compiler_params=_params(),
    )(chip_half, *gs, *ts)
    return [(res[2 * a], res[2 * a + 1]) for a in range(na)]


def _chip_sum_small(p32s, us, chip_half):
    na = len(p32s)

    def body(ch_ref, *refs):
        p_refs, u_refs, outs = refs[:na], refs[na:2 * na], refs[2 * na:]
        for a in range(na):
            r2 = p_refs[a].shape[0]
            acc = p_refs[a][...]
            for j in range(3):
                acc = acc + u_refs[a][j].astype(F32)
            outs[a][pl.ds(pl.multiple_of(ch_ref[1] * r2, 8), r2), :] = acc

    return pl.pallas_call(
        body, name="rs_chip_sum_small",
        in_specs=[pl.BlockSpec(memory_space=pltpu.SMEM)] + [pl.BlockSpec(memory_space=pltpu.VMEM)] * (2 * na),
        out_shape=[_sds((2 * p.shape[0], p.shape[1])) for p in p32s], compiler_params=_params(),
    )(chip_half, *p32s, *us)


def _ride_chip_exchange(ps):
    def copies(ins, outs, send_sems, recv_sems):
        x, y, c = lax.axis_index("x"), lax.axis_index("y"), lax.axis_index("c")
        chips = [(1 - x, y), (x, 1 - y), (1 - x, 1 - y)]
        cps = []
        for a in range(len(ps)):
            for j, (ox, oy) in enumerate(chips):
                cps.append(pltpu.make_async_remote_copy(src_ref=ins[a].at[2 * ox + oy], dst_ref=outs[a].at[j],
                                                        send_sem=send_sems.at[3 * a + j],
                                                        recv_sem=recv_sems.at[3 * a + j],
                                                        device_id=(ox, oy, c), device_id_type=MESH))
        return cps

    return _Ride(ps, [_sds((3,) + p.shape[1:], p.dtype) for p in ps], 3 * len(ps), copies)


def _chip_sum(p32, u, chip_half, name):
    r2, cols = p32.shape
    tr = _row_tile(r2, cols)
    nt = r2 // tr

    def body(ch_ref, p_ref, u_ref, o_ref):
        acc = p_ref[...]
        for j in range(3):
            acc = acc + u_ref[j].astype(F32)
        o_ref[...] = acc

    return pl.pallas_call(
        body, name=name,
        grid_spec=pltpu.PrefetchScalarGridSpec(
            num_scalar_prefetch=1, grid=(nt,),
            in_specs=[pl.BlockSpec((tr, cols), lambda i, ch_ref: (i, 0)),
                      pl.BlockSpec((3, tr, cols), lambda i, ch_ref: (0, i, 0))],
            out_specs=pl.BlockSpec((tr, cols), lambda i, ch_ref: (ch_ref[1] * nt + i, 0))),
        out_shape=_sds((2 * r2, cols)), compiler_params=_params("parallel"),
    )(chip_half, p32, u)


def _sibling_join(fs, name):
    nb = len(fs)

    def body(*refs):
        outs = refs[nb:2 * nb]
        send_sems, recv_sems = refs[2 * nb:]
        x, y, c = lax.axis_index("x"), lax.axis_index("y"), lax.axis_index("c")
        cps = []
        for a in range(nb):
            r2 = outs[a].shape[0] // 2
            rows = outs[a].at[pl.ds(c * r2, r2), :]
            cps.append(pltpu.make_async_remote_copy(src_ref=rows, dst_ref=rows, send_sem=send_sems.at[a],
                                                    recv_sem=recv_sems.at[a], device_id=(x, y, 1 - c),
                                                    device_id_type=MESH))
        for cp in cps:
            cp.start()
        for cp in cps:
            cp.wait()

    return pl.pallas_call(
        body, name=name, in_specs=[ANY] * nb, out_specs=[ANY] * nb,
        out_shape=[_sds(f.shape, f.dtype) for f in fs],
        input_output_aliases={a: a for a in range(nb)},
        scratch_shapes=[_dma_sems(nb), _dma_sems(nb)],
    )(*fs)


def _row_tile(rows, cols):
    tile = rows
    while tile * cols * 4 > (1 << 20) and tile % 16 == 0:
        tile //= 2
    return tile


def _w_ada_grad(c_all, dmod_cols):
    def body(c_ref, d_ref, o_ref):
        o_ref[...] = _dot_tn(c_ref[...].astype(BF16), d_ref[...].astype(BF16))

    return pl.pallas_call(body, name="w_ada_grad", out_shape=_sds((c_all.shape[1], dmod_cols.shape[1])),
                          compiler_params=_params())(c_all, dmod_cols)


def _adam_math(w, g, m, v):
    nm = ADAM_B1 * m + (1.0 - ADAM_B1) * g
    nv = ADAM_B2 * v + (1.0 - ADAM_B2) * (g * g)
    m_hat = nm / (1.0 - ADAM_B1 ** ADAM_STEP)
    v_hat = nv / (1.0 - ADAM_B2 ** ADAM_STEP)
    return -ADAM_LR * (m_hat / (jnp.sqrt(v_hat) + ADAM_EPS) + ADAM_WD * w), nm, nv


def _adamw(w, g, m, v, name):
    rows, cols = w.shape
    tr = _row_tile(rows, cols)

    def body(w_ref, g_ref, m_ref, v_ref, go_ref, d_ref, nm_ref, nv_ref):
        gv = g_ref[...]
        go_ref[...] = gv
        d_ref[...], nm_ref[...], nv_ref[...] = _adam_math(w_ref[...], gv, m_ref[...], v_ref[...])

    spec = pl.BlockSpec((tr, cols), lambda i: (i, 0))
    return pl.pallas_call(
        body, name=name, grid=(rows // tr,), in_specs=[spec] * 4, out_specs=[spec] * 4,
        out_shape=[_sds((rows, cols))] * 4, compiler_params=_params("parallel"),
    )(w, g, m, v)


def _pack_small(dmod, dnorm_g, dfinal_g, dpool_scale, drel_bias, loss, dpool_w):
    return jnp.concatenate([dmod.reshape(-1, 128), dnorm_g.reshape(-1, 128), dfinal_g.reshape(-1, 128),
                            jnp.pad(dpool_scale.reshape(-1, 128), ((0, PK_RELB - PK_PSCALE - AW // 128), (0, 0))),
                            jnp.pad(drel_bias, ((0, 0), (0, 128 - NG * NH))),
                            jnp.full((PK_POOLW - PK_LOSS, 128), loss, F32), dpool_w.reshape(-1, 128)], axis=0)


def _small_update(small_all, ws, ms, vs):
    lane_rows = [(r0, r0 + w.shape[1] // 128) for r0, w in zip((PK_BADA, PK_NORMG, PK_FINALG, PK_PSCALE), ws)]
    nw = len(ws)

    def body(all_ref, *refs):
        w_refs, m_refs, v_refs = refs[:nw], refs[nw:2 * nw], refs[2 * nw:3 * nw]
        loss_ref, outs = refs[3 * nw], refs[3 * nw + 1:]
        g = all_ref[0]
        for s in range(1, all_ref.shape[0]):
            g = g + all_ref[s]
        loss_ref[...] = jnp.broadcast_to(g[PK_LOSS:PK_LOSS + 1, :], loss_ref.shape)

        def put(p, at, gv):
            d, nm, nv = _adam_math(w_refs[p][at], gv, m_refs[p][at], v_refs[p][at])
            for o_ref, val in zip(outs[4 * p:4 * p + 4], (gv, d, nm, nv)):
                o_ref[at] = val

        for p, (r0, r1) in enumerate(lane_rows):
            for i in range(r1 - r0):
                put(p, (slice(None), slice(128 * i, 128 * (i + 1))), g[r0 + i:r0 + i + 1, :])
        put(4, (slice(None), slice(None)), g[PK_RELB:PK_LOSS, 0:NG * NH])
        put(5, (slice(None), slice(None)), g[PK_POOLW:PK_ROWS, :])

    res = pl.pallas_call(
        body, name="small_update",
        out_shape=[_sds((8, 128))] + [_sds(w.shape) for w in ws for _ in range(4)], compiler_params=_params(),
    )(small_all, *ws, *ms, *vs)
    return res[0], [res[1 + 4 * p:5 + 4 * p] for p in range(nw)]


def kernel(x, c, norm_g, w_ada, b_ada, w_in, pool_w, pool_scale, w_attn_br, w_pool_br, w_out, rel_bias, final_g, loss_target, m_norm_g, m_w_ada, m_b_ada, m_w_in, m_pool_w, m_pool_scale, m_w_attn_br, m_w_pool_br, m_w_out, m_rel_bias, m_final_g, v_norm_g, v_w_ada, v_b_ada, v_w_in, v_pool_w, v_pool_scale, v_w_attn_br, v_w_pool_br, v_w_out, v_rel_bias, v_final_g):
    ix, iy, ic = lax.axis_index("x"), lax.axis_index("y"), lax.axis_index("c")
    dev = 4 * ix + 2 * iy + ic
    chip = 2 * ix + iy

    def half(w):
        r2 = w.shape[0] // 2
        return lax.dynamic_slice_in_dim(w, ic * r2, r2, axis=0).astype(BF16)

    gathered = _allgather8([jnp.broadcast_to(c, (8, D)), half(w_in[0]), half(w_attn_br[0]), half(w_pool_br[0]),
                            half(w_out[0])], "gather_weights", relay=[False, True, True, True, True])
    c_all = gathered[0][:, 0, :]
    wg_in = gathered[1].reshape(N_SHARD, D, IN_W // N_SHARD)
    wab = gathered[2].reshape(N_SHARD, AW, D // N_SHARD).transpose(1, 0, 2).reshape(AW, D)
    wpb = gathered[3].reshape(N_SHARD, AW, D // N_SHARD).transpose(1, 0, 2).reshape(AW, D)
    wout = gathered[4].reshape(D, D)

    mw = 3 * D // N_SHARD
    modp = _mod_partial(c_all, w_ada[0], lax.dynamic_slice_in_dim(b_ada, chip * mw, mw, axis=1))
    mod_all = _allgather8([modp], "gather_mod")[0]
    mod_full = mod_all[::2].transpose(1, 0, 2).reshape(8, 3 * D)
    mod = lax.dynamic_slice_in_dim(mod_full, dev, 1, axis=0)

    chip_half = jnp.stack([chip, ic]).astype(jnp.int32)
    r = _local_step(x[0], loss_target[0], mod, wg_in, wab, wpb, wout, pool_w[0], pool_scale, rel_bias, norm_g,
                    final_g.reshape(1, D), chip_half)

    packed = _pack_small(r["dmod"], r["dnorm_g"], r["dfinal_g"], r["dpool_scale"], r["drel_bias"], r["loss"],
                         r["dpool_w"])
    small_all = _allgather8([packed], "gather_small")[0]
    small = ["b_ada", "norm_g", "final_g", "pool_scale", "rel_bias", "pool_w"]
    shaped = lambda b, n, f, ps, rb, pw: [b, n, f.reshape(1, D), ps, rb, pw.reshape(4 * PGW, PGW)]
    loss, small_out = _small_update(small_all, shaped(b_ada, norm_g, final_g, pool_scale, rel_bias, pool_w),
                                    shaped(m_b_ada, m_norm_g, m_final_g, m_pool_scale, m_rel_bias, m_pool_w),
                                    shaped(v_b_ada, v_norm_g, v_final_g, v_pool_scale, v_rel_bias, v_pool_w))
    dmod_all = small_all[:, PK_BADA:PK_NORMG, :].reshape(8, 3 * D)
    g_w_ada = _w_ada_grad(c_all, lax.dynamic_slice_in_dim(dmod_all, chip * mw, mw, axis=1))

    g_w_in, g_w_ab, g_w_pb, g_w_out = _sibling_join([r["rs_in"], r["rs_attn_br"], r["rs_pool_br"], r["rs_out"]],
                                                    "rs_sibling_join")
    upd = dict(zip(small, small_out))
    upd["final_g"] = [a.reshape(D) for a in upd["final_g"]]
    upd["pool_w"] = [a.reshape(1, 4, PGW, PGW) for a in upd["pool_w"]]
    for nme, w, g, m, v in (("w_ada", w_ada, g_w_ada, m_w_ada, v_w_ada), ("w_in", w_in, g_w_in, m_w_in, v_w_in),
                            ("w_attn_br", w_attn_br, g_w_ab, m_w_attn_br, v_w_attn_br),
                            ("w_pool_br", w_pool_br, g_w_pb, m_w_pool_br, v_w_pool_br),
                            ("w_out", w_out, g_w_out, m_w_out, v_w_out)):
        upd[nme] = [a[None] for a in _adamw(w[0], g, m[0], v[0], "adamw_" + nme)]
    names = ["norm_g", "w_ada", "b_ada", "w_in", "pool_w", "pool_scale", "w_attn_br", "w_pool_br", "w_out",
             "rel_bias", "final_g"]
    return (loss[0, 0], r["grad_x"][None]) + tuple(upd[nme][kind] for kind in range(4) for nme in names)
```

```python
import functools
import math

import numpy as np
import jax
import jax.numpy as jnp
from jax import lax
from jax.experimental import pallas as pl
from jax.experimental.pallas import tpu as pltpu

F32 = jnp.float32
BF16 = jnp.bfloat16

D = 1024
HD = 64
NH = 8
AW = NH * HD
GROUPS = ((128, 1), (512, 4), (2048, 16))
NG = len(GROUPS)
BLK = 128
GW = 3 * AW
QKV_W = NG * GW
REST_W = 3584
IN_W = QKV_W + REST_W
CB = 512
NCB = IN_W // CB
NCB_QKV = QKV_W // CB
POOL_WINDOWS = (2, 4, 8, 16)
PGW = 128
HALO = 16
NUM_BUCKETS = 32
MAX_DISTANCE = 2048
EPS = 1e-6
NEG = -1e30
N_SHARD = 4
VMEM_LIMIT = 56 * 1024 * 1024

ADAM_LR = 0.001
ADAM_B1 = 0.9
ADAM_B2 = 0.999
ADAM_EPS = 1e-08
ADAM_WD = 0.01
ADAM_STEP = 10

PK_BADA, PK_NORMG, PK_FINALG, PK_PSCALE, PK_RELB, PK_LOSS, PK_POOLW, PK_ROWS = 0, 24, 32, 40, 48, 80, 88, 600

ANY = pl.BlockSpec(memory_space=pl.ANY)
MESH = pl.DeviceIdType.MESH


def _params(*sem):
    return pltpu.CompilerParams(dimension_semantics=sem, vmem_limit_bytes=VMEM_LIMIT)


def _sds(shape, dtype=F32):
    return jax.ShapeDtypeStruct(shape, dtype)


def _dot(a, b):
    return jnp.dot(a, b, preferred_element_type=F32)


def _dot_nt(a, b):
    return lax.dot_general(a, b, (((1,), (1,)), ((), ())), preferred_element_type=F32)


def _dot_tn(a, b):
    return lax.dot_general(a, b, (((0,), (0,)), ((), ())), preferred_element_type=F32)


def _sigmoid(z):
    return 0.5 * jnp.tanh(0.5 * z) + 0.5


def _dma_sems(*shape):
    return pltpu.SemaphoreType.DMA(shape)


class _Ride:
    def __init__(self, arrays, out_shapes, n_copies, copies):
        self.arrays, self.out_shapes, self.n_copies, self.copies = list(arrays), list(out_shapes), n_copies, copies


def _call_with_ride(body, ride, first, last, *, in_specs, out_specs, out_shape, scratch_shapes=(), **kw):
    in_specs, out_specs, out_shape, scratch_shapes = list(in_specs), list(out_specs), list(out_shape), list(scratch_shapes)
    n_in, n_out, n_sc = len(in_specs), len(out_specs), len(scratch_shapes)
    if ride is None:
        def run_plain(*operands):
            return pl.pallas_call(body, in_specs=in_specs, out_specs=out_specs, out_shape=out_shape,
                                  scratch_shapes=scratch_shapes, **kw)(*operands), []
        return run_plain
    n_ri, n_ro = len(ride.arrays), len(ride.out_shapes)

    def wrapped(*refs):
        ins, rest = refs[:n_in], refs[n_in:]
        r_ins, rest = rest[:n_ri], rest[n_ri:]
        outs, rest = rest[:n_out], rest[n_out:]
        r_outs, rest = rest[:n_ro], rest[n_ro:]
        scratch, (send_sems, recv_sems) = rest[:n_sc], rest[n_sc:]

        @pl.when(first())
        def _():
            for cp in ride.copies(r_ins, r_outs, send_sems, recv_sems):
                cp.start()

        body(*ins, *outs, *scratch)

        @pl.when(last())
        def _():
            for cp in ride.copies(r_ins, r_outs, send_sems, recv_sems):
                cp.wait()

    def run(*operands):
        res = pl.pallas_call(
            wrapped, in_specs=in_specs + [ANY] * n_ri, out_specs=out_specs + [ANY] * n_ro,
            out_shape=out_shape + ride.out_shapes,
            scratch_shapes=scratch_shapes + [_dma_sems(ride.n_copies), _dma_sems(ride.n_copies)], **kw,
        )(*operands, *ride.arrays)
        return res[:n_out], res[n_out:]
    return run


def _bucket_tables():
    i = np.arange(BLK)[:, None]
    j = np.arange(2 * BLK)[None, :]
    dist = BLK + i - j
    valid = (dist >= 0) & (dist <= BLK)
    tabs = []
    for _, dil in GROUPS:
        n = (np.clip(dist, 0, BLK) * dil).astype(np.int32)
        max_exact = NUM_BUCKETS // 2
        nf = np.maximum(n, 1).astype(np.float32)
        large = max_exact + (np.log(nf / np.float32(max_exact)) / np.float32(math.log(MAX_DISTANCE / max_exact))
                             * np.float32(NUM_BUCKETS - max_exact)).astype(np.int32)
        large = np.minimum(large, NUM_BUCKETS - 1)
        bucket = np.where(n < max_exact, n, large)
        tab = np.where(valid, bucket, -1).astype(np.int32)
        perm = _block_perm(dil)
        tabs.append(tab[perm][:, np.concatenate([perm, BLK + perm])])
    return np.stack(tabs)


def _bias_table(rel_bias, buckets):
    def body(rb_ref, bk_ref, out_ref):
        g = pl.program_id(0)
        bk = bk_ref[...]
        for h in range(NH):
            acc = jnp.full((BLK, 2 * BLK), NEG, F32)
            for b in range(NUM_BUCKETS):
                acc = jnp.where(bk == b, rb_ref[b, g * NH + h], acc)
            out_ref[h] = acc

    return pl.pallas_call(
        body, name="bias_table", grid=(NG,),
        in_specs=[pl.BlockSpec(memory_space=pltpu.SMEM),
                  pl.BlockSpec((None, BLK, 2 * BLK), lambda g: (g, 0, 0))],
        out_specs=pl.BlockSpec((NH, BLK, 2 * BLK), lambda g: (g, 0, 0)),
        out_shape=_sds((NG * NH, BLK, 2 * BLK)),
        compiler_params=_params("arbitrary"),
    )(rel_bias, buckets)


def _bias_grad(ds_acc, buckets, ride):
    def body(acc_ref, bk_ref, out_ref):
        bk = bk_ref[...]
        acc = acc_ref[...]
        lane = lax.broadcasted_iota(jnp.int32, (8, 128), 1)
        out = jnp.zeros((8, 128), F32)
        for b in range(NUM_BUCKETS):
            val = jnp.sum(jnp.where(bk == b, acc, 0.0))
            out = jnp.where(lane == b, val, out)
        out_ref[...] = out

    (out,), rode = _call_with_ride(
        body, ride, lambda: pl.program_id(0) == 0, lambda: pl.program_id(0) == NG * NH - 1,
        name="bias_grad", grid=(NG * NH,),
        in_specs=[pl.BlockSpec((None, BLK, 2 * BLK), lambda gh: (gh, 0, 0)),
                  pl.BlockSpec((None, BLK, 2 * BLK), lambda gh: (gh // NH, 0, 0))],
        out_specs=[pl.BlockSpec((None, 8, 128), lambda gh: (gh, 0, 0))],
        out_shape=[_sds((NG * NH, 8, 128))],
        compiler_params=_params("arbitrary"),
    )(ds_acc, buckets)
    return out, rode


def _mod_partial(c_all, w_ada_s, b_ada_s):
    def body(c_ref, w_ref, b_ref, o_ref):
        o_ref[...] = _dot(c_ref[...].astype(BF16), w_ref[...].astype(BF16)) + b_ref[...]

    return pl.pallas_call(body, name="mod_partial", out_shape=_sds((8, w_ada_s.shape[1])),
                          compiler_params=_params())(c_all, w_ada_s, b_ada_s)


def _prenorm(x, norm_g, mod):
    S = x.shape[0]
    tm = 512

    def body(x_ref, g_ref, mod_ref, h_ref):
        xv = x_ref[...]
        r = lax.rsqrt(jnp.mean(xv * xv, axis=-1, keepdims=True) + EPS)
        n1 = xv * r * g_ref[...]
        h_ref[...] = (n1 * (1.0 + mod_ref[:, D:2 * D]) + mod_ref[:, 0:D]).astype(BF16)

    return pl.pallas_call(
        body, name="prenorm", grid=(S // tm,),
        in_specs=[pl.BlockSpec((tm, D), lambda i: (i, 0)), pl.BlockSpec((1, D), lambda i: (0, 0)),
                  pl.BlockSpec((1, 3 * D), lambda i: (0, 0))],
        out_specs=pl.BlockSpec((tm, D), lambda i: (i, 0)),
        out_shape=_sds((S, D), BF16), compiler_params=_params("parallel"),
    )(x, norm_g, mod)


def _proj(h, wg_in, j0, nj, dtype, name):
    S = h.shape[0]
    tm = 2048
    per = wg_in.shape[2] // CB

    def body(h_ref, w_ref, o_ref):
        o_ref[...] = _dot(h_ref[...], w_ref[...]).astype(dtype)

    return pl.pallas_call(
        body, name=name, grid=(S // tm, nj),
        in_specs=[pl.BlockSpec((tm, D), lambda m, j: (m, 0)),
                  pl.BlockSpec((None, D, CB), lambda m, j: ((j0 + j) // per, 0, (j0 + j) % per))],
        out_specs=pl.BlockSpec((tm, CB), lambda m, j: (m, j)),
        out_shape=_sds((S, nj * CB), dtype), compiler_params=_params("parallel", "parallel"),
    )(h, wg_in)


HS = 4
SLAB = HS * HD


def _lane_head(rows):
    return lax.broadcasted_iota(jnp.int32, (rows, SLAB), 1) // HD


def _head_stack(a):
    head = _lane_head(a.shape[0])
    return jnp.concatenate([jnp.where(head == h, a, jnp.zeros_like(a)) for h in range(HS)], axis=0)


def _head_unstack(a):
    rows = a.shape[0] // HS
    head = _lane_head(rows)
    out = a[:rows]
    for h in range(1, HS):
        out = jnp.where(head == h, a[h * rows:(h + 1) * rows], out)
    return out


STAT_W = 128
VIEW = 16


def _sub_layout(dil):
    if dil == 1:
        return BLK, [None]
    return BLK * dil // VIEW, [[r + dil * u for u in range(VIEW // dil)] for r in range(dil)]


def _block_perm(dil):
    a_rows, _ = _sub_layout(dil)
    p = np.arange(BLK)
    return p if dil == 1 else (VIEW // dil) * (p % a_rows) + p // a_rows


LB = 128
N_SLAB = NH // HS


def _ld(refs, bs, s, w):
    if bs is None:
        return refs[0][:, s * w:(s + 1) * w]
    a_rows = refs[0].shape[0] // VIEW
    return jnp.concatenate([jnp.concatenate([ref[pl.ds(b, a_rows, stride=VIEW), :] for b in bs], axis=0)
                            for ref in refs], axis=1)


def _st(ref, bs, s, val):
    if bs is None:
        ref[:, s * SLAB:(s + 1) * SLAB] = val.astype(ref.dtype)
        return
    a_rows = val.shape[0] // len(bs)
    for u, b in enumerate(bs):
        ref[:, b, :] = val[u * a_rows:(u + 1) * a_rows]


def _attn_views(dil, S):
    a_rows, subs = _sub_layout(dil)
    if dil == 1:
        def ispecs(base, w, f):
            return [pl.BlockSpec((BLK, N_SLAB * w), lambda sg, n: (f(n), base // (N_SLAB * w)))]
        return subs, S // BLK, N_SLAB, ispecs, (lambda w: (S, w)), (
            lambda f: pl.BlockSpec((BLK, AW), lambda sg, n: (f(n), 0)))

    def ispecs(base, w, f):
        return [pl.BlockSpec((a_rows * VIEW, LB), lambda sg, n, k=k: (f(n), (base + sg * w) // LB + k))
                for k in range(w // LB)]
    return subs, S // (a_rows * VIEW), 1, ispecs, (lambda w: (S // VIEW, VIEW, w)), (
        lambda f: pl.BlockSpec((a_rows, VIEW, SLAB), lambda sg, n: (f(n), 0, sg)))


def _attn_fwd(qkv_g, bias_tab, g):
    S = qkv_g.shape[0]
    subs, nbq, sps, ispecs, shape, ospec = _attn_views(GROUPS[g][1], S)
    cur = lambda n: n
    in_specs = [ispecs(0, SLAB, cur), ispecs(AW, SLAB, cur), ispecs(2 * AW, SLAB, cur)]
    nl = len(in_specs[0])

    def body(*refs):
        q, k, v = (refs[t * nl:(t + 1) * nl] for t in range(3))
        b_ref, o_ref, l_ref, kprev, vprev = refs[3 * nl:]
        n = pl.program_id(1)

        @pl.when(n == 0)
        def _():
            kprev[...] = jnp.zeros_like(kprev)
            vprev[...] = jnp.zeros_like(vprev)

        col = lax.broadcasted_iota(jnp.int32, (HS * BLK, 2 * BLK), 1)
        keep = (col >= BLK) | (n > 0)
        for s_ in range(sps):
            cs = slice(s_ * SLAB, (s_ + 1) * SLAB)
            bias = b_ref[pl.ds(s_ * HS, HS)].reshape(HS * BLK, 2 * BLK)
            for i, bs in enumerate(subs):
                kc, vc = _ld(k, bs, s_, SLAB).astype(BF16), _ld(v, bs, s_, SLAB).astype(BF16)
                kb = jnp.concatenate([kprev[i, :, cs], kc], axis=0)
                vb = jnp.concatenate([vprev[i, :, cs], vc], axis=0)
                kprev[i, :, cs], vprev[i, :, cs] = kc, vc
                s = _dot_nt(_head_stack(_ld(q, bs, s_, SLAB).astype(BF16)), kb) * (HD ** -0.5) + bias
                s = jnp.where(keep, s, NEG)
                m = jnp.max(s, axis=-1, keepdims=True)
                p = jnp.exp(s - m)
                den = jnp.sum(p, axis=-1, keepdims=True)
                _st(o_ref, bs, s_, _head_unstack(_dot(p.astype(BF16), vb) / den))
                _st(l_ref, bs, s_, _head_unstack(jnp.broadcast_to(m + jnp.log(den), (HS * BLK, SLAB))))

    out = _sds(shape(AW))
    nsg = N_SLAB // sps
    o, l = pl.pallas_call(
        body, name=f"attn_fwd{g}", grid=(nsg, nbq),
        in_specs=sum(in_specs, []) + [pl.BlockSpec((sps * HS, BLK, 2 * BLK), lambda sg, n: (g * nsg + sg, 0, 0))],
        out_specs=[ospec(cur), ospec(cur)],
        out_shape=[out, out],
        scratch_shapes=[pltpu.VMEM((len(subs), BLK, sps * SLAB), BF16)] * 2,
        compiler_params=_params("parallel", "arbitrary"),
    )(*([qkv_g] * (3 * nl)), bias_tab)
    return o.reshape(S, AW), l.reshape(S, AW)


def _attn_bwd(qkv_g, dattn, stats, bias_tab, g, ride):
    S = qkv_g.shape[0]
    subs, nbq, sps, ispecs, shape, ospec = _attn_views(GROUPS[g][1], S)
    cur = lambda n: jnp.minimum(n, nbq - 1)
    late = lambda n: jnp.maximum(n - 1, 0)
    in_specs = [ispecs(0, SLAB, cur), ispecs(AW, SLAB, cur), ispecs(2 * AW, SLAB, cur), ispecs(0, SLAB, cur),
                ispecs(0, STAT_W, cur)]
    nl = len(in_specs[0])

    def body(*refs):
        q, k, v, da = (refs[t * nl:(t + 1) * nl] for t in range(4))
        st_ref, b_ref, dq_ref, dk_ref, dv_ref, ds_ref, ck_ref, cv_ref, kprev, vprev = refs[4 * nl:]
        n = pl.program_id(1)

        @pl.when(n == 0)
        def _():
            for ref in (ds_ref, ck_ref, cv_ref, kprev, vprev):
                ref[...] = jnp.zeros_like(ref)

        @pl.when(n < nbq)
        def _():
            col = lax.broadcasted_iota(jnp.int32, (HS * BLK, 2 * BLK), 1)
            keep = (col >= BLK) | (n > 0)
            for s_ in range(sps):
                cs = slice(s_ * SLAB, (s_ + 1) * SLAB)
                bias = b_ref[pl.ds(s_ * HS, HS)].reshape(HS * BLK, 2 * BLK)
                for i, bs in enumerate(subs):
                    st = _ld((st_ref,), bs, s_, STAT_W)
                    kc, vc = _ld(k, bs, s_, SLAB).astype(BF16), _ld(v, bs, s_, SLAB).astype(BF16)
                    kb = jnp.concatenate([kprev[i, :, cs], kc], axis=0)
                    vb = jnp.concatenate([vprev[i, :, cs], vc], axis=0)
                    kprev[i, :, cs], vprev[i, :, cs] = kc, vc
                    lse =jnp.concatenate([st[:, h:h + 1] for h in range(HS)], axis=0)
                    delta = jnp.concatenate([st[:, HS + h:HS + h + 1] for h in range(HS)], axis=0)
                    qs = _head_stack(_ld(q, bs, s_, SLAB).astype(BF16))
                    dos = _head_stack(_ld(da, bs, s_, SLAB).astype(BF16))
                    s = _dot_nt(qs, kb) * (HD ** -0.5) + bias
                    s = jnp.where(keep, s, NEG)
                    p = jnp.exp(s - lse)
                    ds = p * (_dot_nt(dos, vb) - delta)
                    ds_ref[pl.ds(s_ * HS, HS)] += ds.reshape(HS, BLK, 2 * BLK)
                    ds_b = (ds * (HD ** -0.5)).astype(BF16)
                    _st(dq_ref, bs, s_, _head_unstack(_dot(ds_b, kb)))
                    dkb = _dot_tn(ds_b, qs)
                    dvb = _dot_tn(p.astype(BF16), dos)
                    _st(dk_ref, bs, s_, ck_ref[i, :, cs] + dkb[:BLK])
                    _st(dv_ref, bs, s_, cv_ref[i, :, cs] + dvb[:BLK])
                    ck_ref[i, :, cs] = dkb[BLK:]
                    cv_ref[i, :, cs] = dvb[BLK:]

        @pl.when(n == nbq)
        def _():
            for s_ in range(sps):
                for i, bs in enumerate(subs):
                    _st(dk_ref, bs, s_, ck_ref[i, :, s_ * SLAB:(s_ + 1) * SLAB])
                    _st(dv_ref, bs, s_, cv_ref[i, :, s_ * SLAB:(s_ + 1) * SLAB])

    out = _sds(shape(AW), BF16 if GROUPS[g][1] == 1 else F32)
    nsg = N_SLAB // sps
    (dq, dk, dv, ds_acc), rode = _call_with_ride(
        body, ride, lambda: (pl.program_id(0) == 0) & (pl.program_id(1) == 0),
        lambda: (pl.program_id(0) == nsg - 1) & (pl.program_id(1) == nbq),
        name=f"attn_bwd{g}", grid=(nsg, nbq + 1),
        in_specs=sum(in_specs, []) + [pl.BlockSpec((sps * HS, BLK, 2 * BLK), lambda sg, n: (g * nsg + sg, 0, 0))],
        out_specs=[ospec(cur), ospec(late), ospec(late),
                   pl.BlockSpec((sps * HS, BLK, 2 * BLK), lambda sg, n: (sg, 0, 0))],
        out_shape=[out] * 3 + [_sds((NH, BLK, 2 * BLK))],
        scratch_shapes=[pltpu.VMEM((len(subs), BLK, sps * SLAB), F32)] * 2
        + [pltpu.VMEM((len(subs), BLK, sps * SLAB), BF16)] * 2,
        compiler_params=_params("arbitrary", "arbitrary"),
    )(*([qkv_g] * (3 * nl)), *([dattn] * nl), stats, bias_tab)
    return [dq.reshape(S, AW), dk.reshape(S, AW), dv.reshape(S, AW)], ds_acc, rode


TM_MIX = 256


def _mix_specs(tm):
    row512 = pl.BlockSpec((tm, AW), lambda i: (i, 0))
    return ([row512] * 6 + [
        pl.BlockSpec((tm, REST_W), lambda i: (i, 0)),
        pl.BlockSpec((HALO, AW), lambda i: (jnp.maximum(i * (tm // HALO) - 1, 0), 1)),
        pl.BlockSpec((AW, D), lambda i: (0, 0)), pl.BlockSpec((AW, D), lambda i: (0, 0)),
        pl.BlockSpec((4, PGW, PGW), lambda i: (0, 0, 0)), pl.BlockSpec((1, AW), lambda i: (0, 0))])


def _mix_forward(i, tm, o_refs, l_refs, rest_ref, halo_ref, wab_ref, wpb_ref, pw_ref, ps_ref):
    l0, l1, l2 = (r[...] for r in l_refs)
    mx = jnp.maximum(jnp.maximum(l0, l1), l2)
    e0, e1, e2 = jnp.exp(l0 - mx), jnp.exp(l1 - mx), jnp.exp(l2 - mx)
    den = e0 + e1 + e2
    lj = mx + jnp.log(den)
    attn = (e0 * o_refs[0][...] + e1 * o_refs[1][...] + e2 * o_refs[2][...]) / den

    z_attn = rest_ref[:, 0:AW]
    u = rest_ref[:, AW:2 * AW]
    z_pool = rest_ref[:, 2 * AW:3 * AW]
    g_attn = rest_ref[:, 3 * AW:3 * AW + D]
    g_pool = rest_ref[:, 3 * AW + D:3 * AW + 2 * D]

    sg_a = _sigmoid(z_attn)
    sil_a = z_attn * sg_a
    a_g = (attn * sil_a).astype(BF16)
    y_attn = _dot(a_g, wab_ref[...])

    halo = jnp.where(i > 0, halo_ref[...], 0.0)
    ext = jnp.concatenate([halo, u], axis=0)
    t = i * tm + lax.broadcasted_iota(jnp.int32, (tm, 1), 0)
    pooled, mixed_raw = [], []
    for gi, win in enumerate(POOL_WINDOWS):
        s = ext[:, gi * PGW:(gi + 1) * PGW]
        sh = 1
        while sh < win:
            s = s + pltpu.roll(s, sh, 0)
            sh *= 2
        cnt = jnp.minimum(t + 1, win).astype(F32)
        pg = s[HALO:] / cnt - u[:, gi * PGW:(gi + 1) * PGW]
        pooled.append(pg.astype(BF16))
        mixed_raw.append(_dot(pooled[-1], pw_ref[gi].astype(BF16)))
    mixed_raw = jnp.concatenate(mixed_raw, axis=1)
    mixed = mixed_raw * ps_ref[...]
    sg_p = _sigmoid(z_pool)
    sil_p = z_pool * sg_p
    m_g = (mixed * sil_p).astype(BF16)
    y_pool = _dot(m_g, wpb_ref[...])

    sa = _sigmoid(g_attn)
    sp = _sigmoid(g_pool)
    merged = sa * y_attn + sp * y_pool
    return dict(lj=lj, attn=attn, z_attn=z_attn, z_pool=z_pool, sg_a=sg_a, sil_a=sil_a, a_g=a_g, y_attn=y_attn,
                pooled=pooled, mixed_raw=mixed_raw, mixed=mixed, sg_p=sg_p, sil_p=sil_p, m_g=m_g, y_pool=y_pool,
                sa=sa, sp=sp, merged=merged)


def _mix_step(x, target, os_, ls_, rest, wab, wpb, pool_w, pool_scale, wout, mod, final_g):
    S = x.shape[0]
    tm = TM_MIX
    nt = S // tm
    sw = D // N_SHARD

    def body(o0, o1, o2, l0, l1, l2, rest_ref, halo_ref, wab_ref, wpb_ref, pw_ref, ps_ref,
             x_ref, t_ref, wo_ref, mod_ref, fg_ref, dx2_ref, loss_ref, dfg_ref, dgate_ref,
             dattn_ref, stats_ref, dpooled_ref, dproj_hbm, dwo_hbm, dwab_hbm, dwpb_hbm, dpw_ref, dps_ref,
             awo, awab, awpb, stage, stage_sem):
        i = pl.program_id(0)
        slot = i % 2

        def staged(step, sl):
            return pltpu.make_async_copy(stage.at[sl], dproj_hbm.at[pl.ds(step * tm, tm), pl.ds(QKV_W, REST_W)],
                                         stage_sem.at[sl])

        @pl.when(i == 0)
        def _():
            for ref in (loss_ref, dfg_ref, dgate_ref, awo, awab, awpb, dpw_ref, dps_ref):
                ref[...] = jnp.zeros_like(ref)

        f = _mix_forward(i, tm, (o0, o1, o2), (l0, l1, l2), rest_ref, halo_ref, wab_ref, wpb_ref, pw_ref, ps_ref)
        mo = _dot(f["merged"].astype(BF16), wo_ref[...])
        gate = mod_ref[:, 2 * D:3 * D]
        fg = fg_ref[...]
        x2 = x_ref[...] + gate * mo
        r2 = lax.rsqrt(jnp.mean(x2 * x2, axis=-1, keepdims=True) + EPS)
        n2 = x2 * r2
        err = n2 * fg - t_ref[...]
        loss_ref[...] += 0.5 * jnp.sum(jnp.mean(err * err, axis=-1, keepdims=True))
        dy = err * (1.0 / D)
        dfg_ref[...] += jnp.sum(dy * n2, axis=0, keepdims=True)
        dn = dy * fg
        dx2 = r2 * (dn - n2 * jnp.mean(dn * n2, axis=-1, keepdims=True))
        dgate_ref[...] += jnp.sum(dx2 * mo, axis=0, keepdims=True)
        dx2_ref[...] = dx2

        dmo_b = (dx2 * gate).astype(BF16)
        dmerged = _dot_nt(dmo_b, wo_ref[...])
        awo[...] += _dot_tn(f["merged"].astype(BF16), dmo_b)
        sa, sp = f["sa"], f["sp"]
        dya = (dmerged * sa).astype(BF16)
        dyp = (dmerged * sp).astype(BF16)
        dg_attn = dmerged * f["y_attn"] * sa * (1.0 - sa)
        dg_pool = dmerged * f["y_pool"] * sp * (1.0 - sp)
        dag = _dot_nt(dya, wab_ref[...])
        awab[...] += _dot_tn(f["a_g"], dya)
        dmg = _dot_nt(dyp, wpb_ref[...])
        awpb[...] += _dot_tn(f["m_g"], dyp)
        dattn = dag * f["sil_a"]
        dattn_ref[...] = dattn
        prod = dattn * f["attn"]
        lane = lax.broadcasted_iota(jnp.int32, (tm, STAT_W), 1)
        for sb in range(N_SLAB):
            st = jnp.zeros((tm, STAT_W), F32)
            for h in range(HS):
                hs = slice((sb * HS + h) * HD, (sb * HS + h + 1) * HD)
                st = jnp.where(lane == h, f["lj"][:, hs.start:hs.start + 1], st)
                st = jnp.where(lane == HS + h, jnp.sum(prod[:, hs], axis=-1, keepdims=True), st)
            stats_ref[:, sb * STAT_W:(sb + 1) * STAT_W] = st
        dz_attn = dag * f["attn"] * (f["sg_a"] * (1.0 + f["z_attn"] * (1.0 - f["sg_a"])))
        dmixed = dmg * f["sil_p"]
        dz_pool = dmg * f["mixed"] * (f["sg_p"] * (1.0 + f["z_pool"] * (1.0 - f["sg_p"])))
        dps_ref[...] += jnp.sum(dmixed * f["mixed_raw"], axis=0, keepdims=True)
        dpm = (dmixed * ps_ref[...]).astype(BF16)
        for gi in range(len(POOL_WINDOWS)):
            cs = slice(gi * PGW, (gi + 1) * PGW)
            dpw_ref[gi] += _dot_tn(f["pooled"][gi], dpm[:, cs])
            dpooled_ref[:, cs] = _dot_nt(dpm[:, cs], pw_ref[gi].astype(BF16))
        @pl.when(i >= 2)
        def _():
            staged(i - 2, slot).wait()

        stage[slot, :, 0:AW] = dz_attn.astype(BF16)
        stage[slot, :, AW:2 * AW] = jnp.zeros((tm, AW), BF16)
        stage[slot, :, 2 * AW:3 * AW] = dz_pool.astype(BF16)
        stage[slot, :, 3 * AW:3 * AW + D] = dg_attn.astype(BF16)
        stage[slot, :, 3 * AW + D:3 * AW + 2 * D] = dg_pool.astype(BF16)
        staged(i, slot).start()

        @pl.when(i == nt - 1)
        def _():
            staged(i - 1, 1 - slot).wait()
            staged(i, slot).wait()
            pltpu.sync_copy(awo, dwo_hbm)
            for k in range(N_SHARD):
                pltpu.sync_copy(awab.at[:, pl.ds(k * sw, sw)], dwab_hbm.at[k])
                pltpu.sync_copy(awpb.at[:, pl.ds(k * sw, sw)], dwpb_hbm.at[k])

    row = pl.BlockSpec((tm, D), lambda i: (i, 0))
    vec = pl.BlockSpec((1, D), lambda i: (0, 0))
    row512 = pl.BlockSpec((tm, AW), lambda i: (i, 0))
    outs = pl.pallas_call(
        body, name="mix_step", grid=(nt,),
        in_specs=_mix_specs(tm) + [row, row, pl.BlockSpec((D, D), lambda i: (0, 0)),
                                   pl.BlockSpec((1, 3 * D), lambda i: (0, 0)), vec],
        out_specs=[row, pl.BlockSpec((8, 128), lambda i: (0, 0)), vec, vec,
                   row512, pl.BlockSpec((tm, N_SLAB * STAT_W), lambda i: (i, 0)), row512, ANY, ANY, ANY, ANY,
                   pl.BlockSpec((4, PGW, PGW), lambda i: (0, 0, 0)), pl.BlockSpec((1, AW), lambda i: (0, 0))],
        out_shape=[_sds((S, D)), _sds((8, 128)), _sds((1, D)), _sds((1, D)),
                   _sds((S, AW)), _sds((S, N_SLAB * STAT_W)), _sds((S, AW)), _sds((S, IN_W), BF16),
                   _sds((D, D)), _sds((N_SHARD, AW, sw)), _sds((N_SHARD, AW, sw)), _sds((4, PGW, PGW)), _sds((1, AW))],
        scratch_shapes=[pltpu.VMEM((D, D), F32), pltpu.VMEM((AW, D), F32), pltpu.VMEM((AW, D), F32),
                        pltpu.VMEM((2, tm, REST_W), BF16), _dma_sems(2)],
        compiler_params=_params("arbitrary"),
    )(*os_, *ls_, rest, rest, wab, wpb, pool_w, pool_scale, x, target, wout, mod, final_g)
    dx2, loss, dfg, dgate, dattn, stats, dpooled, dproj, dwo, dwab, dwpb, dpw, dps = outs
    return (dx2, loss, dfg, dgate, dattn, stats, dpooled, dproj, dwo.reshape(N_SHARD, D // N_SHARD, D), dwab, dwpb,
            dpw, dps)


def _pool_bwd(dpooled):
    S = dpooled.shape[0]
    tm = 512
    nt = S // tm

    def body(dp_ref, nxt_ref, du_ref):
        i = pl.program_id(0)
        t = i * tm + lax.broadcasted_iota(jnp.int32, (tm + HALO, 1), 0)
        nxt = jnp.where(i < nt - 1, nxt_ref[...], 0.0)
        ext = jnp.concatenate([dp_ref[...], nxt], axis=0)
        for gi, win in enumerate(POOL_WINDOWS):
            cs = slice(gi * PGW, (gi + 1) * PGW)
            s = ext[:, cs] / jnp.minimum(t + 1, win).astype(F32)
            sh = 1
            while sh < win:
                s = s + pltpu.roll(s, tm + HALO - sh, 0)
                sh *= 2
            du_ref[:, cs] = (s[:tm] - dp_ref[:, cs]).astype(BF16)

    return pl.pallas_call(
        body, name="pool_bwd", grid=(nt,),
        in_specs=[pl.BlockSpec((tm, AW), lambda i: (i, 0)),
                  pl.BlockSpec((HALO, AW), lambda i: (jnp.minimum((i + 1) * (tm // HALO), S // HALO - 1), 0))],
        out_specs=pl.BlockSpec((tm, AW), lambda i: (i, 0)),
        out_shape=_sds((S, AW), BF16), compiler_params=_params("parallel"),
    )(dpooled, dpooled)


TB = 1024


def _dh(dproj, wg_in, ride):
    S = dproj.shape[0]
    per = wg_in.shape[2] // TB
    nm, nk = S // TB, IN_W // TB

    def body(dp_ref, w_ref, out_ref):
        @pl.when(pl.program_id(1) == 0)
        def _():
            out_ref[...] = jnp.zeros_like(out_ref)

        out_ref[...] += _dot_nt(dp_ref[...], w_ref[...])

    (dh,), rode = _call_with_ride(
        body, ride, lambda: (pl.program_id(0) == 0) & (pl.program_id(1) == 0),
        lambda: (pl.program_id(0) == nm - 1) & (pl.program_id(1) == nk - 1),
        name="dh", grid=(nm, nk),
        in_specs=[pl.BlockSpec((TB, TB), lambda m, kk: (m, kk)),
                  pl.BlockSpec((None, D, TB), lambda m, kk: (kk // per, 0, kk % per))],
        out_specs=[pl.BlockSpec((TB, D), lambda m, kk: (m, 0))],
        out_shape=[_sds((S, D))], compiler_params=_params("arbitrary", "arbitrary"),
    )(dproj, wg_in)
    return dh, rode


def _dw_in(h, dproj):
    S = dproj.shape[0]
    per = IN_W // N_SHARD // TB

    def body(h_ref, dp_ref, out_ref):
        @pl.when(pl.program_id(1) == 0)
        def _():
            out_ref[...] = jnp.zeros_like(out_ref)

        out_ref[...] += _dot_tn(h_ref[...], dp_ref[...])

    return pl.pallas_call(
        body, name="dw_in", grid=(IN_W // TB, S // TB),
        in_specs=[pl.BlockSpec((TB, D), lambda j, kk: (kk, 0)), pl.BlockSpec((TB, TB), lambda j, kk: (kk, j))],
        out_specs=pl.BlockSpec((None, D, TB), lambda j, kk: (j // per, 0, j % per)),
        out_shape=_sds((N_SHARD, D, IN_W // N_SHARD)), compiler_params=_params("parallel", "arbitrary"),
    )(h, dproj)


def _prenorm_bwd(x, dh, dx2, norm_g, mod):
    S = x.shape[0]
    tm = 512

    def body(x_ref, dh_ref, dx2_ref, g_ref, mod_ref, gx_ref, dg_ref, dshift_ref, dscale_ref):
        i = pl.program_id(0)

        @pl.when(i == 0)
        def _():
            dg_ref[...] = jnp.zeros_like(dg_ref)
            dshift_ref[...] = jnp.zeros_like(dshift_ref)
            dscale_ref[...] = jnp.zeros_like(dscale_ref)

        xv = x_ref[...]
        dhv = dh_ref[...]
        g = g_ref[...]
        r = lax.rsqrt(jnp.mean(xv * xv, axis=-1, keepdims=True) + EPS)
        xh = xv * r
        dshift_ref[...] += jnp.sum(dhv, axis=0, keepdims=True)
        dscale_ref[...] += jnp.sum(dhv * (xh * g), axis=0, keepdims=True)
        dn1 = dhv * (1.0 + mod_ref[:, D:2 * D])
        dg_ref[...] += jnp.sum(dn1 * xh, axis=0, keepdims=True)
        dxh = dn1 * g
        gx_ref[...] = dx2_ref[...] + r * (dxh - xh * jnp.mean(dxh * xh, axis=-1, keepdims=True))

    row = pl.BlockSpec((tm, D), lambda i: (i, 0))
    vec = pl.BlockSpec((1, D), lambda i: (0, 0))
    return pl.pallas_call(
        body, name="prenorm_bwd", grid=(S // tm,),
        in_specs=[row, row, row, vec, pl.BlockSpec((1, 3 * D), lambda i: (0, 0))],
        out_specs=[row, vec, vec, vec],
        out_shape=[_sds((S, D)), _sds((1, D)), _sds((1, D)), _sds((1, D))],
        compiler_params=_params("arbitrary"),
    )(x, dh, dx2, norm_g, mod)


def _local_step(x, target, mod, wg_in, wab, wpb, wout, pool_w, pool_scale, rel_bias, norm_g, final_g, chip_half):
    buckets = jnp.asarray(_bucket_tables())
    bias_tab = _bias_table(rel_bias, buckets)
    h = _prenorm(x, norm_g, mod)
    qkv = [_proj(h, wg_in, 3 * g, 3, BF16 if GROUPS[g][1] == 1 else F32, f"proj_qkv{g}") for g in range(NG)]
    rest = _proj(h, wg_in, NCB_QKV, REST_W // CB, F32, "proj_rest")
    os_, ls_ = zip(*[_attn_fwd(qkv[g], bias_tab, g) for g in range(NG)])
    (dx2, loss, dfinal_g, dgate, dattn, stats, dpooled, dproj, dw_out, dw_ab, dw_pb, dpool_w,
     dpool_scale) = _mix_step(x, target, os_, ls_, rest, wab, wpb, pool_w, pool_scale, wout, mod, final_g)
    du = _pool_bwd(dpooled)

    small = [dw_ab, dw_pb, dw_out]
    dqkv0, ds0, sib_small = _attn_bwd(qkv[0], dattn, stats, bias_tab, 0, _ride_sibling_halves(small))
    p_small = _pair_sum_small(small, sib_small, chip_half)
    dqkv1, ds1, u_small = _attn_bwd(qkv[1], dattn, stats, bias_tab, 1,
                                    _ride_chip_exchange([p16 for _, p16 in p_small]))
    rs_ab, rs_pb, rs_out = _chip_sum_small([p32 for p32, _ in p_small], u_small, chip_half)
    dqkv2, ds2, _ = _attn_bwd(qkv[2], dattn, stats, bias_tab, 2, None)

    for j, piece in enumerate(dqkv0 + dqkv1 + dqkv2):
        dproj = lax.dynamic_update_slice(dproj, piece.astype(BF16), (0, j * AW))
    dproj = lax.dynamic_update_slice(dproj, du, (0, QKV_W + AW))
    dw_in = _dw_in(h, dproj)
    drel_rows, (sib_in,) = _bias_grad(jnp.concatenate([ds0, ds1, ds2], axis=0), buckets,
                                      _ride_sibling_halves([dw_in]))
    drel = drel_rows[:, 0, :NUM_BUCKETS].T
    p32_in, p16_in = _pair_sum(dw_in, sib_in, chip_half, "rs_pair_sum_in")
    dh, (u_in,) = _dh(dproj, wg_in, _ride_chip_exchange([p16_in]))
    rs_in = _chip_sum(p32_in, u_in, chip_half, "rs_chip_sum_in")

    grad_x, dnorm_g, dshift, dscale = _prenorm_bwd(x, dh, dx2, norm_g, mod)
    dmod = jnp.concatenate([dshift, dscale, dgate], axis=1)
    return dict(loss=loss[0, 0], grad_x=grad_x, dmod=dmod, dnorm_g=dnorm_g, dfinal_g=dfinal_g, dpool_w=dpool_w,
                dpool_scale=dpool_scale, drel_bias=drel, dw_in=dw_in, dw_attn_br=dw_ab, dw_pool_br=dw_pb,
                dw_out=dw_out, rs_in=rs_in, rs_attn_br=rs_ab, rs_pool_br=rs_pb, rs_out=rs_out)


def _allgather8(blocks, name, relay=None):
    nb = len(blocks)
    relay = [False] * nb if relay is None else list(relay)

    def body(*refs):
        ins, outs = refs[:nb], refs[nb:2 * nb]
        send_sems, recv_sems = refs[2 * nb:]
        x, y, c = lax.axis_index("x"), lax.axis_index("y"), lax.axis_index("c")
        me, sibling = (x, y, c), (x, y, 1 - c)
        here, xn, yn, dg = (x, y), (1 - x, y), (x, 1 - y), (1 - x, 1 - y)

        def slot(a, chip, core, half=None):
            ref = outs[a].at[4 * chip[0] + 2 * chip[1] + core]
            if half is None:
                return ref
            r2 = ref.shape[0] // 2
            return ref.at[pl.ds(half * r2, r2)]

        def copy(a, k, dst, to, src=None):
            return pltpu.make_async_remote_copy(src_ref=dst if src is None else src, dst_ref=dst,
                                                send_sem=send_sems.at[a, k], recv_sem=recv_sems.at[a, k],
                                                device_id=to, device_id_type=MESH)

        def start(cps):
            for cp in cps:
                cp.start()
            return cps

        sent = []
        for a in range(nb):
            own = slot(a, here, c)
            sent += [copy(a, 0, own, sibling, src=ins[a]), copy(a, 1, own, (*xn, c), src=ins[a]),
                     copy(a, 2, own, (*yn, c), src=ins[a])]
            if not relay[a]:
                sent.append(copy(a, 3, own, (*dg, c), src=ins[a]))
        start(sent)
        for a in range(nb):
            copy(a, 2, slot(a, yn, c), me).wait_recv()
            sent += start([copy(a, 6, slot(a, yn, c), sibling)]
                          + ([copy(a, 3, slot(a, yn, c, 0), (*xn, c))] if relay[a] else []))
        for a in range(nb):
            copy(a, 1, slot(a, xn, c), me).wait_recv()
            sent += start([copy(a, 5, slot(a, xn, c), sibling)]
                          + ([copy(a, 4, slot(a, xn, c, 1), (*yn, c))] if relay[a] else []))
        for a in range(nb):
            for k, half in ((3, 0), (4, 1)) if relay[a] else ((3, None),):
                copy(a, k, slot(a, dg, c, half), me).wait_recv()
                sent += start([copy(a, 4 + k, slot(a, dg, c, half), sibling)])
        for a in range(nb):
            copy(a, 0, slot(a, here, 1 - c), me).wait_recv()
            copy(a, 5, slot(a, xn, 1 - c), me).wait_recv()
            copy(a, 6, slot(a, yn, 1 - c), me).wait_recv()
            for k, half in ((7, 0), (8, 1)) if relay[a] else ((7, None),):
                copy(a, k, slot(a, dg, 1 - c, half), me).wait_recv()
        for cp in sent:
            cp.wait_send()

    outs = pl.pallas_call(
        body, name=name, in_specs=[ANY] * nb, out_specs=[ANY] * nb,
        out_shape=[_sds((8,) + b.shape, b.dtype) for b in blocks],
        scratch_shapes=[_dma_sems(nb, 9), _dma_sems(nb, 9)],
    )(*blocks)
    return [_place_own(buf, b) for buf, b in zip(outs, blocks)]


def _place_own(buf, block):
    dev = 4 * lax.axis_index("x") + 2 * lax.axis_index("y") + lax.axis_index("c")
    return lax.dynamic_update_index_in_dim(buf, block, dev, 0)


def _ride_sibling_halves(gs):
    def copies(ins, outs, send_sems, recv_sems):
        x, y, c = lax.axis_index("x"), lax.axis_index("y"), lax.axis_index("c")
        cps = []
        for a in range(len(gs)):
            r2 = ins[a].shape[1] // 2
            other = ins[a].at[:, pl.ds((1 - c) * r2, r2), :]
            cps.append(pltpu.make_async_remote_copy(src_ref=other, dst_ref=outs[a], send_sem=send_sems.at[a],
                                                    recv_sem=recv_sems.at[a], device_id=(x, y, 1 - c),
                                                    device_id_type=MESH))
        return cps

    return _Ride(gs, [_sds((g.shape[0], g.shape[1] // 2, g.shape[2]), g.dtype) for g in gs], len(gs), copies)


def _pair_sum(g, t, chip_half, name):
    nsh, rows, cols = g.shape
    r2 = rows // 2
    tr = _row_tile(r2, cols)
    nt = r2 // tr

    def body(ch_ref, g_ref, t_ref, p32_ref, p16_ref):
        p = g_ref[...] + t_ref[...]
        p16_ref[...] = p.astype(BF16)

        @pl.when(pl.program_id(1) == ch_ref[0])
        def _():
            p32_ref[...] = p

    blk = pl.BlockSpec((None, tr, cols), lambda i, k, ch_ref: (k, i, 0))
    return pl.pallas_call(
        body, name=name,
        grid_spec=pltpu.PrefetchScalarGridSpec(
            num_scalar_prefetch=1, grid=(nt, nsh),
            in_specs=[pl.BlockSpec((None, tr, cols), lambda i, k, ch_ref: (k, ch_ref[1] * nt + i, 0)), blk],
            out_specs=[pl.BlockSpec((tr, cols), lambda i, k, ch_ref: (i, 0)), blk]),
        out_shape=[_sds((r2, cols)), _sds((nsh, r2, cols), BF16)],
        compiler_params=_params("parallel", "arbitrary"),
    )(chip_half, g, t)


def _pair_sum_small(gs, ts, chip_half):
    na = len(gs)

    def body(ch_ref, *refs):
        g_refs, t_refs, outs = refs[:na], refs[na:2 * na], refs[2 * na:]
        for a in range(na):
            r2 = t_refs[a].shape[1]
            own = pl.ds(pl.multiple_of(ch_ref[1] * r2, 8), r2)
            outs[2 * a + 1][...] = (g_refs[a][:, own, :] + t_refs[a][...]).astype(BF16)
            outs[2 * a][...] = g_refs[a][ch_ref[0], own, :] + t_refs[a][ch_ref[0]]

    res = pl.pallas_call(
        body, name="rs_pair_sum_small",
        in_specs=[pl.BlockSpec(memory_space=pltpu.SMEM)] + [pl.BlockSpec(memory_space=pltpu.VMEM)] * (2 * na),
        out_shape=[s for t in ts for s in (_sds(t.shape[1:]), _sds(t.shape, BF16))], compiler_params=_params(),
    )(chip_half, *gs, *ts)
    return [(res[2 * a], res[2 * a + 1]) for a in range(na)]


def _chip_sum_small(p32s, us, chip_half):
    na = len(p32s)

    def body(ch_ref, *refs):
        p_refs, u_refs, outs = refs[:na], refs[na:2 * na], refs[2 * na:]
        for a in range(na):
            r2 = p_refs[a].shape[0]
            acc = p_refs[a][...]
            for j in range(3):
                acc = acc + u_refs[a][j].astype(F32)
            outs[a][pl.ds(pl.multiple_of(ch_ref[1] * r2, 8), r2), :] = acc

    return pl.pallas_call(
        body, name="rs_chip_sum_small",
        in_specs=[pl.BlockSpec(memory_space=pltpu.SMEM)] + [pl.BlockSpec(memory_space=pltpu.VMEM)] * (2 * na),
        out_shape=[_sds((2 * p.shape[0], p.shape[1])) for p in p32s], compiler_params=_params(),
    )(chip_half, *p32s, *us)


def _ride_chip_exchange(ps):
    def copies(ins, outs, send_sems, recv_sems):
        x, y, c = lax.axis_index("x"), lax.axis_index("y"), lax.axis_index("c")
        chips = [(1 - x, y), (x, 1 - y), (1 - x, 1 - y)]
        cps = []
        for a in range(len(ps)):
            for j, (ox, oy) in enumerate(chips):
                cps.append(pltpu.make_async_remote_copy(src_ref=ins[a].at[2 * ox + oy], dst_ref=outs[a].at[j],
                                                        send_sem=send_sems.at[3 * a + j],
                                                        recv_sem=recv_sems.at[3 * a + j],
                                                        device_id=(ox, oy, c), device_id_type=MESH))
        return cps

    return _Ride(ps, [_sds((3,) + p.shape[1:], p.dtype) for p in ps], 3 * len(ps), copies)


def _chip_sum(p32, u, chip_half, name):
    r2, cols = p32.shape
    tr = _row_tile(r2, cols)
    nt = r2 // tr

    def body(ch_ref, p_ref, u_ref, o_ref):
        acc = p_ref[...]
        for j in range(3):
            acc = acc + u_ref[j].astype(F32)
        o_ref[...] = acc

    return pl.pallas_call(
        body, name=name,
        grid_spec=pltpu.PrefetchScalarGridSpec(
            num_scalar_prefetch=1, grid=(nt,),
            in_specs=[pl.BlockSpec((tr, cols), lambda i, ch_ref: (i, 0)),
                      pl.BlockSpec((3, tr, cols), lambda i, ch_ref: (0, i, 0))],
            out_specs=pl.BlockSpec((tr, cols), lambda i, ch_ref: (ch_ref[1] * nt + i, 0))),
        out_shape=_sds((2 * r2, cols)), compiler_params=_params("parallel"),
    )(chip_half, p32, u)


def _sibling_join(fs, name):
    nb = len(fs)

    def body(*refs):
        outs = refs[nb:2 * nb]
        send_sems, recv_sems = refs[2 * nb:]
        x, y, c = lax.axis_index("x"), lax.axis_index("y"), lax.axis_index("c")
        cps = []
        for a in range(nb):
            r2 = outs[a].shape[0] // 2
            rows = outs[a].at[pl.ds(c * r2, r2), :]
            cps.append(pltpu.make_async_remote_copy(src_ref=rows, dst_ref=rows, send_sem=send_sems.at[a],
                                                    recv_sem=recv_sems.at[a], device_id=(x, y, 1 - c),
                                                    device_id_type=MESH))
        for cp in cps:
            cp.start()
        for cp in cps:
            cp.wait()

    return pl.pallas_call(
        body, name=name, in_specs=[ANY] * nb, out_specs=[ANY] * nb,
        out_shape=[_sds(f.shape, f.dtype) for f in fs],
        input_output_aliases={a: a for a in range(nb)},
        scratch_shapes=[_dma_sems(nb), _dma_sems(nb)],
    )(*fs)


def _row_tile(rows, cols):
    tile = rows
    while tile * cols * 4 > (1 << 20) and tile % 16 == 0:
        tile //= 2
    return tile


def _w_ada_grad(c_all, dmod_cols):
    def body(c_ref, d_ref, o_ref):
        o_ref[...] = _dot_tn(c_ref[...].astype(BF16), d_ref[...].astype(BF16))

    return pl.pallas_call(body, name="w_ada_grad", out_shape=_sds((c_all.shape[1], dmod_cols.shape[1])),
                          compiler_params=_params())(c_all, dmod_cols)


def _adam_math(w, g, m, v):
    nm = ADAM_B1 * m + (1.0 - ADAM_B1) * g
    nv = ADAM_B2 * v + (1.0 - ADAM_B2) * (g * g)
    m_hat = nm / (1.0 - ADAM_B1 ** ADAM_STEP)
    v_hat = nv / (1.0 - ADAM_B2 ** ADAM_STEP)
    return -ADAM_LR * (m_hat / (jnp.sqrt(v_hat) + ADAM_EPS) + ADAM_WD * w), nm, nv


def _adamw(w, g, m, v, name):
    rows, cols = w.shape
    tr = _row_tile(rows, cols)

    def body(w_ref, g_ref, m_ref, v_ref, go_ref, d_ref, nm_ref, nv_ref):
        gv = g_ref[...]
        go_ref[...] = gv
        d_ref[...], nm_ref[...], nv_ref[...] = _adam_math(w_ref[...], gv, m_ref[...], v_ref[...])

    spec = pl.BlockSpec((tr, cols), lambda i: (i, 0))
    return pl.pallas_call(
        body, name=name, grid=(rows // tr,), in_specs=[spec] * 4, out_specs=[spec] * 4,
        out_shape=[_sds((rows, cols))] * 4, compiler_params=_params("parallel"),
    )(w, g, m, v)


def _pack_small(dmod, dnorm_g, dfinal_g, dpool_scale, drel_bias, loss, dpool_w):
    return jnp.concatenate([dmod.reshape(-1, 128), dnorm_g.reshape(-1, 128), dfinal_g.reshape(-1, 128),
                            jnp.pad(dpool_scale.reshape(-1, 128), ((0, PK_RELB - PK_PSCALE - AW // 128), (0, 0))),
                            jnp.pad(drel_bias, ((0, 0), (0, 128 - NG * NH))),
                            jnp.full((PK_POOLW - PK_LOSS, 128), loss, F32), dpool_w.reshape(-1, 128)], axis=0)


def _small_update(small_all, ws, ms, vs):
    lane_rows = [(r0, r0 + w.shape[1] // 128) for r0, w in zip((PK_BADA, PK_NORMG, PK_FINALG, PK_PSCALE), ws)]
    nw = len(ws)

    def body(all_ref, *refs):
        w_refs, m_refs, v_refs = refs[:nw], refs[nw:2 * nw], refs[2 * nw:3 * nw]
        loss_ref, outs = refs[3 * nw], refs[3 * nw + 1:]
        g = all_ref[0]
        for s in range(1, all_ref.shape[0]):
            g = g + all_ref[s]
        loss_ref[...] = jnp.broadcast_to(g[PK_LOSS:PK_LOSS + 1, :], loss_ref.shape)

        def put(p, at, gv):
            d, nm, nv = _adam_math(w_refs[p][at], gv, m_refs[p][at], v_refs[p][at])
            for o_ref, val in zip(outs[4 * p:4 * p + 4], (gv, d, nm, nv)):
                o_ref[at] = val

        for p, (r0, r1) in enumerate(lane_rows):
            for i in range(r1 - r0):
                put(p, (slice(None), slice(128 * i, 128 * (i + 1))), g[r0 + i:r0 + i + 1, :])
        put(4, (slice(None), slice(None)), g[PK_RELB:PK_LOSS, 0:NG * NH])
        put(5, (slice(None), slice(None)), g[PK_POOLW:PK_ROWS, :])

    res = pl.pallas_call(
        body, name="small_update",
        out_shape=[_sds((8, 128))] + [_sds(w.shape) for w in ws for _ in range(4)], compiler_params=_params(),
    )(small_all, *ws, *ms, *vs)
    return res[0], [res[1 + 4 * p:5 + 4 * p] for p in range(nw)]


def kernel(x, c, norm_g, w_ada, b_ada, w_in, pool_w, pool_scale, w_attn_br, w_pool_br, w_out, rel_bias, final_g, loss_target, m_norm_g, m_w_ada, m_b_ada, m_w_in, m_pool_w, m_pool_scale, m_w_attn_br, m_w_pool_br, m_w_out, m_rel_bias, m_final_g, v_norm_g, v_w_ada, v_b_ada, v_w_in, v_pool_w, v_pool_scale, v_w_attn_br, v_w_pool_br, v_w_out, v_rel_bias, v_final_g):
    ix, iy, ic = lax.axis_index("x"), lax.axis_index("y"), lax.axis_index("c")
    dev = 4 * ix + 2 * iy + ic
    chip = 2 * ix + iy

    def half(w):
        r2 = w.shape[0] // 2
        return lax.dynamic_slice_in_dim(w, ic * r2, r2, axis=0).astype(BF16)

    gathered = _allgather8([jnp.broadcast_to(c, (8, D)), half(w_in[0]), half(w_attn_br[0]), half(w_pool_br[0]),
                            half(w_out[0])], "gather_weights", relay=[False, True, True, True, True])
    c_all = gathered[0][:, 0, :]
    wg_in = gathered[1].reshape(N_SHARD, D, IN_W // N_SHARD)
    wab = gathered[2].reshape(N_SHARD, AW, D // N_SHARD).transpose(1, 0, 2).reshape(AW, D)
    wpb = gathered[3].reshape(N_SHARD, AW, D // N_SHARD).transpose(1, 0, 2).reshape(AW, D)
    wout = gathered[4].reshape(D, D)

    mw = 3 * D // N_SHARD
    modp = _mod_partial(c_all, w_ada[0], lax.dynamic_slice_in_dim(b_ada, chip * mw, mw, axis=1))
    mod_all = _allgather8([modp], "gather_mod")[0]
    mod_full = mod_all[::2].transpose(1, 0, 2).reshape(8, 3 * D)
    mod = lax.dynamic_slice_in_dim(mod_full, dev, 1, axis=0)

    chip_half = jnp.stack([chip, ic]).astype(jnp.int32)
    r = _local_step(x[0], loss_target[0], mod, wg_in, wab, wpb, wout, pool_w[0], pool_scale, rel_bias, norm_g,
                    final_g.reshape(1, D), chip_half)

    packed = _pack_small(r["dmod"], r["dnorm_g"], r["dfinal_g"], r["dpool_scale"], r["drel_bias"], r["loss"],
                         r["dpool_w"])
    small_all = _allgather8([packed], "gather_small")[0]
    small = ["b_ada", "norm_g", "final_g", "pool_scale", "rel_bias", "pool_w"]
    shaped = lambda b, n, f, ps, rb, pw: [b, n, f.reshape(1, D), ps, rb, pw.reshape(4 * PGW, PGW)]
    loss, small_out = _small_update(small_all, shaped(b_ada, norm_g, final_g, pool_scale, rel_bias, pool_w),
                                    shaped(m_b_ada, m_norm_g, m_final_g, m_pool_scale, m_rel_bias, m_pool_w),
                                    shaped(v_b_ada, v_norm_g, v_final_g, v_pool_scale, v_rel_bias, v_pool_w))
    dmod_all = small_all[:, PK_BADA:PK_NORMG, :].reshape(8, 3 * D)
    g_w_ada = _w_ada_grad(c_all, lax.dynamic_slice_in_dim(dmod_all, chip * mw, mw, axis=1))

    g_w_in, g_w_ab, g_w_pb, g_w_out = _sibling_join([r["rs_in"], r["rs_attn_br"], r["rs_pool_br"], r["rs_out"]],
                                                    "rs_sibling_join")
    upd = dict(zip(small, small_out))
    upd["final_g"] = [a.reshape(D) for a in upd["final_g"]]
    upd["pool_w"] = [a.reshape(1, 4, PGW, PGW) for a in upd["pool_w"]]
    for nme, w, g, m, v in (("w_ada", w_ada, g_w_ada, m_w_ada, v_w_ada), ("w_in", w_in, g_w_in, m_w_in, v_w_in),
                            ("w_attn_br", w_attn_br, g_w_ab, m_w_attn_br, v_w_attn_br),
                            ("w_pool_br", w_pool_br, g_w_pb, m_w_pool_br, v_w_pool_br),
                            ("w_out", w_out, g_w_out, m_w_out, v_w_out)):
        upd[nme] = [a[None] for a in _adamw(w[0], g, m[0], v[0], "adamw_" + nme)]
    names = ["norm_g", "w_ada", "b_ada", "w_in", "pool_w", "pool_scale", "w_attn_br", "w_pool_br", "w_out",
             "rel_bias", "final_g"]
    return (loss[0, 0], r["grad_x"][None]) + tuple(upd[nme][kind] for kind in range(4) for nme in names)
```

```python
import functools
import math

import numpy as np
import jax
import jax.numpy as jnp
from jax import lax
from jax.experimental import pallas as pl
from jax.experimental.pallas import tpu as pltpu

F32 = jnp.float32
BF16 = jnp.bfloat16

D = 1024
HD = 64
NH = 8
AW = NH * HD
GROUPS = ((128, 1), (512, 4), (2048, 16))
NG = len(GROUPS)
BLK = 128
GW = 3 * AW
QKV_W = NG * GW
REST_W = 3584
IN_W = QKV_W + REST_W
CB = 512
NCB = IN_W // CB
NCB_QKV = QKV_W // CB
POOL_WINDOWS = (2, 4, 8, 16)
PGW = 128
HALO = 16
NUM_BUCKETS = 32
MAX_DISTANCE = 2048
EPS = 1e-6
NEG = -1e30
N_SHARD = 4
VMEM_LIMIT = 56 * 1024 * 1024

ADAM_LR = 0.001
ADAM_B1 = 0.9
ADAM_B2 = 0.999
ADAM_EPS = 1e-08
ADAM_WD = 0.01
ADAM_STEP = 10

PK_BADA, PK_NORMG, PK_FINALG, PK_PSCALE, PK_RELB, PK_LOSS, PK_POOLW, PK_ROWS = 0, 24, 32, 40, 48, 80, 88, 600

ANY = pl.BlockSpec(memory_space=pl.ANY)
MESH = pl.DeviceIdType.MESH


def _params(*sem):
    return pltpu.CompilerParams(dimension_semantics=sem, vmem_limit_bytes=VMEM_LIMIT)


def _sds(shape, dtype=F32):
    return jax.ShapeDtypeStruct(shape, dtype)


def _dot(a, b):
    return jnp.dot(a, b, preferred_element_type=F32)


def _dot_nt(a, b):
    return lax.dot_general(a, b, (((1,), (1,)), ((), ())), preferred_element_type=F32)


def _dot_tn(a, b):
    return lax.dot_general(a, b, (((0,), (0,)), ((), ())), preferred_element_type=F32)


def _sigmoid(z):
    return 0.5 * jnp.tanh(0.5 * z) + 0.5


def _dma_sems(*shape):
    return pltpu.SemaphoreType.DMA(shape)


class _Ride:
    def __init__(self, arrays, out_shapes, n_copies, copies):
        self.arrays, self.out_shapes, self.n_copies, self.copies = list(arrays), list(out_shapes), n_copies, copies


def _call_with_ride(body, ride, first, last, *, in_specs, out_specs, out_shape, scratch_shapes=(), **kw):
    in_specs, out_specs, out_shape, scratch_shapes = list(in_specs), list(out_specs), list(out_shape), list(scratch_shapes)
    n_in, n_out, n_sc = len(in_specs), len(out_specs), len(scratch_shapes)
    if ride is None:
        def run_plain(*operands):
            return pl.pallas_call(body, in_specs=in_specs, out_specs=out_specs, out_shape=out_shape,
                                  scratch_shapes=scratch_shapes, **kw)(*operands), []
        return run_plain
    n_ri, n_ro = len(ride.arrays), len(ride.out_shapes)

    def wrapped(*refs):
        ins, rest = refs[:n_in], refs[n_in:]
        r_ins, rest = rest[:n_ri], rest[n_ri:]
        outs, rest = rest[:n_out], rest[n_out:]
        r_outs, rest = rest[:n_ro], rest[n_ro:]
        scratch, (send_sems, recv_sems) = rest[:n_sc], rest[n_sc:]

        @pl.when(first())
        def _():
            for cp in ride.copies(r_ins, r_outs, send_sems, recv_sems):
                cp.start()

        body(*ins, *outs, *scratch)

        @pl.when(last())
        def _():
            for cp in ride.copies(r_ins, r_outs, send_sems, recv_sems):
                cp.wait()

    def run(*operands):
        res = pl.pallas_call(
            wrapped, in_specs=in_specs + [ANY] * n_ri, out_specs=out_specs + [ANY] * n_ro,
            out_shape=out_shape + ride.out_shapes,
            scratch_shapes=scratch_shapes + [_dma_sems(ride.n_copies), _dma_sems(ride.n_copies)], **kw,
        )(*operands, *ride.arrays)
        return res[:n_out], res[n_out:]
    return run


def _bucket_tables():
    i = np.arange(BLK)[:, None]
    j = np.arange(2 * BLK)[None, :]
    dist = BLK + i - j
    valid = (dist >= 0) & (dist <= BLK)
    tabs = []
    for _, dil in GROUPS:
        n = (np.clip(dist, 0, BLK) * dil).astype(np.int32)
        max_exact = NUM_BUCKETS // 2
        nf = np.maximum(n, 1).astype(np.float32)
        large = max_exact + (np.log(nf / np.float32(max_exact)) / np.float32(math.log(MAX_DISTANCE / max_exact))
                             * np.float32(NUM_BUCKETS - max_exact)).astype(np.int32)
        large = np.minimum(large, NUM_BUCKETS - 1)
        bucket = np.where(n < max_exact, n, large)
        tab = np.where(valid, bucket, -1).astype(np.int32)
        perm = _block_perm(dil)
        tabs.append(tab[perm][:, np.concatenate([perm, BLK + perm])])
    return np.stack(tabs)


def _bias_table(rel_bias, buckets):
    def body(rb_ref, bk_ref, out_ref):
        g = pl.program_id(0)
        bk = bk_ref[...]
        for h in range(NH):
            acc = jnp.full((BLK, 2 * BLK), NEG, F32)
            for b in range(NUM_BUCKETS):
                acc = jnp.where(bk == b, rb_ref[b, g * NH + h], acc)
            out_ref[h] = acc

    return pl.pallas_call(
        body, name="bias_table", grid=(NG,),
        in_specs=[pl.BlockSpec(memory_space=pltpu.SMEM),
                  pl.BlockSpec((None, BLK, 2 * BLK), lambda g: (g, 0, 0))],
        out_specs=pl.BlockSpec((NH, BLK, 2 * BLK), lambda g: (g, 0, 0)),
        out_shape=_sds((NG * NH, BLK, 2 * BLK)),
        compiler_params=_params("arbitrary"),
    )(rel_bias, buckets)


def _bias_grad(ds_acc, buckets, ride):
    def body(acc_ref, bk_ref, out_ref):
        bk = bk_ref[...]
        acc = acc_ref[...]
        lane = lax.broadcasted_iota(jnp.int32, (8, 128), 1)
        out = jnp.zeros((8, 128), F32)
        for b in range(NUM_BUCKETS):
            val = jnp.sum(jnp.where(bk == b, acc, 0.0))
            out = jnp.where(lane == b, val, out)
        out_ref[...] = out

    (out,), rode = _call_with_ride(
        body, ride, lambda: pl.program_id(0) == 0, lambda: pl.program_id(0) == NG * NH - 1,
        name="bias_grad", grid=(NG * NH,),
        in_specs=[pl.BlockSpec((None, BLK, 2 * BLK), lambda gh: (gh, 0, 0)),
                  pl.BlockSpec((None, BLK, 2 * BLK), lambda gh: (gh // NH, 0, 0))],
        out_specs=[pl.BlockSpec((None, 8, 128), lambda gh: (gh, 0, 0))],
        out_shape=[_sds((NG * NH, 8, 128))],
        compiler_params=_params("arbitrary"),
    )(ds_acc, buckets)
    return out, rode


def _mod_partial(c_all, w_ada_s, b_ada_s):
    def body(c_ref, w_ref, b_ref, o_ref):
        o_ref[...] = _dot(c_ref[...].astype(BF16), w_ref[...].astype(BF16)) + b_ref[...]

    return pl.pallas_call(body, name="mod_partial", out_shape=_sds((8, w_ada_s.shape[1])),
                          compiler_params=_params())(c_all, w_ada_s, b_ada_s)


def _prenorm(x, norm_g, mod):
    S = x.shape[0]
    tm = 512

    def body(x_ref, g_ref, mod_ref, h_ref):
        xv = x_ref[...]
        r = lax.rsqrt(jnp.mean(xv * xv, axis=-1, keepdims=True) + EPS)
        n1 = xv * r * g_ref[...]
        h_ref[...] = (n1 * (1.0 + mod_ref[:, D:2 * D]) + mod_ref[:, 0:D]).astype(BF16)

    return pl.pallas_call(
        body, name="prenorm", grid=(S // tm,),
        in_specs=[pl.BlockSpec((tm, D), lambda i: (i, 0)), pl.BlockSpec((1, D), lambda i: (0, 0)),
                  pl.BlockSpec((1, 3 * D), lambda i: (0, 0))],
        out_specs=pl.BlockSpec((tm, D), lambda i: (i, 0)),
        out_shape=_sds((S, D), BF16), compiler_params=_params("parallel"),
    )(x, norm_g, mod)


def _proj(h, wg_in, j0, nj, dtype, name):
    S = h.shape[0]
    tm = 2048
    per = wg_in.shape[2] // CB

    def body(h_ref, w_ref, o_ref):
        o_ref[...] = _dot(h_ref[...], w_ref[...]).astype(dtype)

    return pl.pallas_call(
        body, name=name, grid=(S // tm, nj),
        in_specs=[pl.BlockSpec((tm, D), lambda m, j: (m, 0)),
                  pl.BlockSpec((None, D, CB), lambda m, j: ((j0 + j) // per, 0, (j0 + j) % per))],
        out_specs=pl.BlockSpec((tm, CB), lambda m, j: (m, j)),
        out_shape=_sds((S, nj * CB), dtype), compiler_params=_params("parallel", "parallel"),
    )(h, wg_in)


HS = 4
SLAB = HS * HD


def _lane_head(rows):
    return lax.broadcasted_iota(jnp.int32, (rows, SLAB), 1) // HD


def _head_stack(a):
    head = _lane_head(a.shape[0])
    return jnp.concatenate([jnp.where(head == h, a, jnp.zeros_like(a)) for h in range(HS)], axis=0)


def _head_unstack(a):
    rows = a.shape[0] // HS
    head = _lane_head(rows)
    out = a[:rows]
    for h in range(1, HS):
        out = jnp.where(head == h, a[h * rows:(h + 1) * rows], out)
    return out


STAT_W = 128
VIEW = 16


def _sub_layout(dil):
    if dil == 1:
        return BLK, [None]
    return BLK * dil // VIEW, [[r + dil * u for u in range(VIEW // dil)] for r in range(dil)]


def _block_perm(dil):
    a_rows, _ = _sub_layout(dil)
    p = np.arange(BLK)
    return p if dil == 1 else (VIEW // dil) * (p % a_rows) + p // a_rows


LB = 128
N_SLAB = NH // HS


def _ld(refs, bs, s, w):
    if bs is None:
        return refs[0][:, s * w:(s + 1) * w]
    a_rows = refs[0].shape[0] // VIEW
    return jnp.concatenate([jnp.concatenate([ref[pl.ds(b, a_rows, stride=VIEW), :] for b in bs], axis=0)
                            for ref in refs], axis=1)


def _st(ref, bs, s, val):
    if bs is None:
        ref[:, s * SLAB:(s + 1) * SLAB] = val.astype(ref.dtype)
        return
    a_rows = val.shape[0] // len(bs)
    for u, b in enumerate(bs):
        ref[:, b, :] = val[u * a_rows:(u + 1) * a_rows]


def _attn_views(dil, S):
    a_rows, subs = _sub_layout(dil)
    if dil == 1:
        def ispecs(base, w, f):
            return [pl.BlockSpec((BLK, N_SLAB * w), lambda sg, n: (f(n), base // (N_SLAB * w)))]
        return subs, S // BLK, N_SLAB, ispecs, (lambda w: (S, w)), (
            lambda f: pl.BlockSpec((BLK, AW), lambda sg, n: (f(n), 0)))

    def ispecs(base, w, f):
        return [pl.BlockSpec((a_rows * VIEW, LB), lambda sg, n, k=k: (f(n), (base + sg * w) // LB + k))
                for k in range(w // LB)]
    return subs, S // (a_rows * VIEW), 1, ispecs, (lambda w: (S // VIEW, VIEW, w)), (
        lambda f: pl.BlockSpec((a_rows, VIEW, SLAB), lambda sg, n: (f(n), 0, sg)))


def _attn_fwd(qkv_g, bias_tab, g):
    S = qkv_g.shape[0]
    subs, nbq, sps, ispecs, shape, ospec = _attn_views(GROUPS[g][1], S)
    cur = lambda n: n
    in_specs = [ispecs(0, SLAB, cur), ispecs(AW, SLAB, cur), ispecs(2 * AW, SLAB, cur)]
    nl = len(in_specs[0])

    def body(*refs):
        q, k, v = (refs[t * nl:(t + 1) * nl] for t in range(3))
        b_ref, o_ref, l_ref, kprev, vprev = refs[3 * nl:]
        n = pl.program_id(1)

        @pl.when(n == 0)
        def _():
            kprev[...] = jnp.zeros_like(kprev)
            vprev[...] = jnp.zeros_like(vprev)

        col = lax.broadcasted_iota(jnp.int32, (HS * BLK, 2 * BLK), 1)
        keep = (col >= BLK) | (n > 0)
        for s_ in range(sps):
            cs = slice(s_ * SLAB, (s_ + 1) * SLAB)
            bias = b_ref[pl.ds(s_ * HS, HS)].reshape(HS * BLK, 2 * BLK)
            for i, bs in enumerate(subs):
                kc, vc = _ld(k, bs, s_, SLAB).astype(BF16), _ld(v, bs, s_, SLAB).astype(BF16)
                kb = jnp.concatenate([kprev[i, :, cs], kc], axis=0)
                vb = jnp.concatenate([vprev[i, :, cs], vc], axis=0)
                kprev[i, :, cs], vprev[i, :, cs] = kc, vc
                s = _dot_nt(_head_stack(_ld(q, bs, s_, SLAB).astype(BF16)), kb) * (HD ** -0.5) + bias
                s = jnp.where(keep, s, NEG)
                m = jnp.max(s, axis=-1, keepdims=True)
                p = jnp.exp(s - m)
                den = jnp.sum(p, axis=-1, keepdims=True)
                _st(o_ref, bs, s_, _head_unstack(_dot(p.astype(BF16), vb) / den))
                _st(l_ref, bs, s_, _head_unstack(jnp.broadcast_to(m + jnp.log(den), (HS * BLK, SLAB))))

    out = _sds(shape(AW))
    nsg = N_SLAB // sps
    o, l = pl.pallas_call(
        body, name=f"attn_fwd{g}", grid=(nsg, nbq),
        in_specs=sum(in_specs, []) + [pl.BlockSpec((sps * HS, BLK, 2 * BLK), lambda sg, n: (g * nsg + sg, 0, 0))],
        out_specs=[ospec(cur), ospec(cur)],
        out_shape=[out, out],
        scratch_shapes=[pltpu.VMEM((len(subs), BLK, sps * SLAB), BF16)] * 2,
        compiler_params=_params("parallel", "arbitrary"),
    )(*([qkv_g] * (3 * nl)), bias_tab)
    return o.reshape(S, AW), l.reshape(S, AW)


def _attn_bwd(qkv_g, dattn, stats, bias_tab, g, ride):
    S = qkv_g.shape[0]
    subs, nbq, sps, ispecs, shape, ospec = _attn_views(GROUPS[g][1], S)
    cur = lambda n: jnp.minimum(n, nbq - 1)
    late = lambda n: jnp.maximum(n - 1, 0)
    in_specs = [ispecs(0, SLAB, cur), ispecs(AW, SLAB, cur), ispecs(2 * AW, SLAB, cur), ispecs(0, SLAB, cur),
                ispecs(0, STAT_W, cur)]
    nl = len(in_specs[0])

    def body(*refs):
        q, k, v, da = (refs[t * nl:(t + 1) * nl] for t in range(4))
        st_ref, b_ref, dq_ref, dk_ref, dv_ref, ds_ref, ck_ref, cv_ref, kprev, vprev = refs[4 * nl:]
        n = pl.program_id(1)

        @pl.when(n == 0)
        def _():
            for ref in (ds_ref, ck_ref, cv_ref, kprev, vprev):
                ref[...] = jnp.zeros_like(ref)

        @pl.when(n < nbq)
        def _():
            col = lax.broadcasted_iota(jnp.int32, (HS * BLK, 2 * BLK), 1)
            keep = (col >= BLK) | (n > 0)
            for s_ in range(sps):
                cs = slice(s_ * SLAB, (s_ + 1) * SLAB)
                bias = b_ref[pl.ds(s_ * HS, HS)].reshape(HS * BLK, 2 * BLK)
                for i, bs in enumerate(subs):
                    st = _ld((st_ref,), bs, s_, STAT_W)
                    kc, vc = _ld(k, bs, s_, SLAB).astype(BF16), _ld(v, bs, s_, SLAB).astype(BF16)
                    kb = jnp.concatenate([kprev[i, :, cs], kc], axis=0)
                    vb = jnp.concatenate([vprev[i, :, cs], vc], axis=0)
                    kprev[i, :, cs], vprev[i, :, cs] = kc, vc
                    lse =jnp.concatenate([st[:, h:h + 1] for h in range(HS)], axis=0)
                    delta = jnp.concatenate([st[:, HS + h:HS + h + 1] for h in range(HS)], axis=0)
                    qs = _head_stack(_ld(q, bs, s_, SLAB).astype(BF16))
                    dos = _head_stack(_ld(da, bs, s_, SLAB).astype(BF16))
                    s = _dot_nt(qs, kb) * (HD ** -0.5) + bias
                    s = jnp.where(keep, s, NEG)
                    p = jnp.exp(s - lse)
                    ds = p * (_dot_nt(dos, vb) - delta)
                    ds_ref[pl.ds(s_ * HS, HS)] += ds.reshape(HS, BLK, 2 * BLK)
                    ds_b = (ds * (HD ** -0.5)).astype(BF16)
                    _st(dq_ref, bs, s_, _head_unstack(_dot(ds_b, kb)))
                    dkb = _dot_tn(ds_b, qs)
                    dvb = _dot_tn(p.astype(BF16), dos)
                    _st(dk_ref, bs, s_, ck_ref[i, :, cs] + dkb[:BLK])
                    _st(dv_ref, bs, s_, cv_ref[i, :, cs] + dvb[:BLK])
                    ck_ref[i, :, cs] = dkb[BLK:]
                    cv_ref[i, :, cs] = dvb[BLK:]

        @pl.when(n == nbq)
        def _():
            for s_ in range(sps):
                for i, bs in enumerate(subs):
                    _st(dk_ref, bs, s_, ck_ref[i, :, s_ * SLAB:(s_ + 1) * SLAB])
                    _st(dv_ref, bs, s_, cv_ref[i, :, s_ * SLAB:(s_ + 1) * SLAB])

    out = _sds(shape(AW), BF16 if GROUPS[g][1] == 1 else F32)
    nsg = N_SLAB // sps
    (dq, dk, dv, ds_acc), rode = _call_with_ride(
        body, ride, lambda: (pl.program_id(0) == 0) & (pl.program_id(1) == 0),
        lambda: (pl.program_id(0) == nsg - 1) & (pl.program_id(1) == nbq),
        name=f"attn_bwd{g}", grid=(nsg, nbq + 1),
        in_specs=sum(in_specs, []) + [pl.BlockSpec((sps * HS, BLK, 2 * BLK), lambda sg, n: (g * nsg + sg, 0, 0))],
        out_specs=[ospec(cur), ospec(late), ospec(late),
                   pl.BlockSpec((sps * HS, BLK, 2 * BLK), lambda sg, n: (sg, 0, 0))],
        out_shape=[out] * 3 + [_sds((NH, BLK, 2 * BLK))],
        scratch_shapes=[pltpu.VMEM((len(subs), BLK, sps * SLAB), F32)] * 2
        + [pltpu.VMEM((len(subs), BLK, sps * SLAB), BF16)] * 2,
        compiler_params=_params("arbitrary", "arbitrary"),
    )(*([qkv_g] * (3 * nl)), *([dattn] * nl), stats, bias_tab)
    return [dq.reshape(S, AW), dk.reshape(S, AW), dv.reshape(S, AW)], ds_acc, rode


TM_MIX = 256


def _mix_specs(tm):
    row512 = pl.BlockSpec((tm, AW), lambda i: (i, 0))
    return ([row512] * 6 + [
        pl.BlockSpec((tm, REST_W), lambda i: (i, 0)),
        pl.BlockSpec((HALO, AW), lambda i: (jnp.maximum(i * (tm // HALO) - 1, 0), 1)),
        pl.BlockSpec((AW, D), lambda i: (0, 0)), pl.BlockSpec((AW, D), lambda i: (0, 0)),
        pl.BlockSpec((4, PGW, PGW), lambda i: (0, 0, 0)), pl.BlockSpec((1, AW), lambda i: (0, 0))])


def _mix_forward(i, tm, o_refs, l_refs, rest_ref, halo_ref, wab_ref, wpb_ref, pw_ref, ps_ref):
    l0, l1, l2 = (r[...] for r in l_refs)
    mx = jnp.maximum(jnp.maximum(l0, l1), l2)
    e0, e1, e2 = jnp.exp(l0 - mx), jnp.exp(l1 - mx), jnp.exp(l2 - mx)
    den = e0 + e1 + e2
    lj = mx + jnp.log(den)
    attn = (e0 * o_refs[0][...] + e1 * o_refs[1][...] + e2 * o_refs[2][...]) / den

    z_attn = rest_ref[:, 0:AW]
    u = rest_ref[:, AW:2 * AW]
    z_pool = rest_ref[:, 2 * AW:3 * AW]
    g_attn = rest_ref[:, 3 * AW:3 * AW + D]
    g_pool = rest_ref[:, 3 * AW + D:3 * AW + 2 * D]

    sg_a = _sigmoid(z_attn)
    sil_a = z_attn * sg_a
    a_g = (attn * sil_a).astype(BF16)
    y_attn = _dot(a_g, wab_ref[...])

    halo = jnp.where(i > 0, halo_ref[...], 0.0)
    ext = jnp.concatenate([halo, u], axis=0)
    t = i * tm + lax.broadcasted_iota(jnp.int32, (tm, 1), 0)
    pooled, mixed_raw = [], []
    for gi, win in enumerate(POOL_WINDOWS):
        s = ext[:, gi * PGW:(gi + 1) * PGW]
        sh = 1
        while sh < win:
            s = s + pltpu.roll(s, sh, 0)
            sh *= 2
        cnt = jnp.minimum(t + 1, win).astype(F32)
        pg = s[HALO:] / cnt - u[:, gi * PGW:(gi + 1) * PGW]
        pooled.append(pg.astype(BF16))
        mixed_raw.append(_dot(pooled[-1], pw_ref[gi].astype(BF16)))
    mixed_raw = jnp.concatenate(mixed_raw, axis=1)
    mixed = mixed_raw * ps_ref[...]
    sg_p = _sigmoid(z_pool)
    sil_p = z_pool * sg_p
    m_g = (mixed * sil_p).astype(BF16)
    y_pool = _dot(m_g, wpb_ref[...])

    sa = _sigmoid(g_attn)
    sp = _sigmoid(g_pool)
    merged = sa * y_attn + sp * y_pool
    return dict(lj=lj, attn=attn, z_attn=z_attn, z_pool=z_pool, sg_a=sg_a, sil_a=sil_a, a_g=a_g, y_attn=y_attn,
                pooled=pooled, mixed_raw=mixed_raw, mixed=mixed, sg_p=sg_p, sil_p=sil_p, m_g=m_g, y_pool=y_pool,
                sa=sa, sp=sp, merged=merged)


def _mix_step(x, target, os_, ls_, rest, wab, wpb, pool_w, pool_scale, wout, mod, final_g):
    S = x.shape[0]
    tm = TM_MIX
    nt = S // tm
    sw = D // N_SHARD

    def body(o0, o1, o2, l0, l1, l2, rest_ref, halo_ref, wab_ref, wpb_ref, pw_ref, ps_ref,
             x_ref, t_ref, wo_ref, mod_ref, fg_ref, dx2_ref, loss_ref, dfg_ref, dgate_ref,
             dattn_ref, stats_ref, dpooled_ref, dproj_hbm, dwo_hbm, dwab_hbm, dwpb_hbm, dpw_ref, dps_ref,
             awo, awab, awpb, stage, stage_sem):
        i = pl.program_id(0)
        slot = i % 2

        def staged(step, sl):
            return pltpu.make_async_copy(stage.at[sl], dproj_hbm.at[pl.ds(step * tm, tm), pl.ds(QKV_W, REST_W)],
                                         stage_sem.at[sl])

        @pl.when(i == 0)
        def _():
            for ref in (loss_ref, dfg_ref, dgate_ref, awo, awab, awpb, dpw_ref, dps_ref):
                ref[...] = jnp.zeros_like(ref)

        f = _mix_forward(i, tm, (o0, o1, o2), (l0, l1, l2), rest_ref, halo_ref, wab_ref, wpb_ref, pw_ref, ps_ref)
        mo = _dot(f["merged"].astype(BF16), wo_ref[...])
        gate = mod_ref[:, 2 * D:3 * D]
        fg = fg_ref[...]
        x2 = x_ref[...] + gate * mo
        r2 = lax.rsqrt(jnp.mean(x2 * x2, axis=-1, keepdims=True) + EPS)
        n2 = x2 * r2
        err = n2 * fg - t_ref[...]
        loss_ref[...] += 0.5 * jnp.sum(jnp.mean(err * err, axis=-1, keepdims=True))
        dy = err * (1.0 / D)
        dfg_ref[...] += jnp.sum(dy * n2, axis=0, keepdims=True)
        dn = dy * fg
        dx2 = r2 * (dn - n2 * jnp.mean(dn * n2, axis=-1, keepdims=True))
        dgate_ref[...] += jnp.sum(dx2 * mo, axis=0, keepdims=True)
        dx2_ref[...] = dx2

        dmo_b = (dx2 * gate).astype(BF16)
        dmerged = _dot_nt(dmo_b, wo_ref[...])
        awo[...] += _dot_tn(f["merged"].astype(BF16), dmo_b)
        sa, sp = f["sa"], f["sp"]
        dya = (dmerged * sa).astype(BF16)
        dyp = (dmerged * sp).astype(BF16)
        dg_attn = dmerged * f["y_attn"] * sa * (1.0 - sa)
        dg_pool = dmerged * f["y_pool"] * sp * (1.0 - sp)
        dag = _dot_nt(dya, wab_ref[...])
        awab[...] += _dot_tn(f["a_g"], dya)
        dmg = _dot_nt(dyp, wpb_ref[...])
        awpb[...] += _dot_tn(f["m_g"], dyp)
        dattn = dag * f["sil_a"]
        dattn_ref[...] = dattn
        prod = dattn * f["attn"]
        lane = lax.broadcasted_iota(jnp.int32, (tm, STAT_W), 1)
        for sb in range(N_SLAB):
            st = jnp.zeros((tm, STAT_W), F32)
            for h in range(HS):
                hs = slice((sb * HS + h) * HD, (sb * HS + h + 1) * HD)
                st = jnp.where(lane == h, f["lj"][:, hs.start:hs.start + 1], st)
                st = jnp.where(lane == HS + h, jnp.sum(prod[:, hs], axis=-1, keepdims=True), st)
            stats_ref[:, sb * STAT_W:(sb + 1) * STAT_W] = st
        dz_attn = dag * f["attn"] * (f["sg_a"] * (1.0 + f["z_attn"] * (1.0 - f["sg_a"])))
        dmixed = dmg * f["sil_p"]
        dz_pool = dmg * f["mixed"] * (f["sg_p"] * (1.0 + f["z_pool"] * (1.0 - f["sg_p"])))
        dps_ref[...] += jnp.sum(dmixed * f["mixed_raw"], axis=0, keepdims=True)
        dpm = (dmixed * ps_ref[...]).astype(BF16)
        for gi in range(len(POOL_WINDOWS)):
            cs = slice(gi * PGW, (gi + 1) * PGW)
            dpw_ref[gi] += _dot_tn(f["pooled"][gi], dpm[:, cs])
            dpooled_ref[:, cs] = _dot_nt(dpm[:, cs], pw_ref[gi].astype(BF16))
        @pl.when(i >= 2)
        def _():
            staged(i - 2, slot).wait()

        stage[slot, :, 0:AW] = dz_attn.astype(BF16)
        stage[slot, :, AW:2 * AW] = jnp.zeros((tm, AW), BF16)
        stage[slot, :, 2 * AW:3 * AW] = dz_pool.astype(BF16)
        stage[slot, :, 3 * AW:3 * AW + D] = dg_attn.astype(BF16)
        stage[slot, :, 3 * AW + D:3 * AW + 2 * D] = dg_pool.astype(BF16)
        staged(i, slot).start()

        @pl.when(i == nt - 1)
        def _():
            staged(i - 1, 1 - slot).wait()
            staged(i, slot).wait()
            pltpu.sync_copy(awo, dwo_hbm)
            for k in range(N_SHARD):
                pltpu.sync_copy(awab.at[:, pl.ds(k * sw, sw)], dwab_hbm.at[k])
                pltpu.sync_copy(awpb.at[:, pl.ds(k * sw, sw)], dwpb_hbm.at[k])

    row = pl.BlockSpec((tm, D), lambda i: (i, 0))
    vec = pl.BlockSpec((1, D), lambda i: (0, 0))
    row512 = pl.BlockSpec((tm, AW), lambda i: (i, 0))
    outs = pl.pallas_call(
        body, name="mix_step", grid=(nt,),
        in_specs=_mix_specs(tm) + [row, row, pl.BlockSpec((D, D), lambda i: (0, 0)),
                                   pl.BlockSpec((1, 3 * D), lambda i: (0, 0)), vec],
        out_specs=[row, pl.BlockSpec((8, 128), lambda i: (0, 0)), vec, vec,
                   row512, pl.BlockSpec((tm, N_SLAB * STAT_W), lambda i: (i, 0)), row512, ANY, ANY, ANY, ANY,
                   pl.BlockSpec((4, PGW, PGW), lambda i: (0, 0, 0)), pl.BlockSpec((1, AW), lambda i: (0, 0))],
        out_shape=[_sds((S, D)), _sds((8, 128)), _sds((1, D)), _sds((1, D)),
                   _sds((S, AW)), _sds((S, N_SLAB * STAT_W)), _sds((S, AW)), _sds((S, IN_W), BF16),
                   _sds((D, D)), _sds((N_SHARD, AW, sw)), _sds((N_SHARD, AW, sw)), _sds((4, PGW, PGW)), _sds((1, AW))],
        scratch_shapes=[pltpu.VMEM((D, D), F32), pltpu.VMEM((AW, D), F32), pltpu.VMEM((AW, D), F32),
                        pltpu.VMEM((2, tm, REST_W), BF16), _dma_sems(2)],
        compiler_params=_params("arbitrary"),
    )(*os_, *ls_, rest, rest, wab, wpb, pool_w, pool_scale, x, target, wout, mod, final_g)
    dx2, loss, dfg, dgate, dattn, stats, dpooled, dproj, dwo, dwab, dwpb, dpw, dps = outs
    return (dx2, loss, dfg, dgate, dattn, stats, dpooled, dproj, dwo.reshape(N_SHARD, D // N_SHARD, D), dwab, dwpb,
            dpw, dps)


def _pool_bwd(dpooled):
    S = dpooled.shape[0]
    tm = 512
    nt = S // tm

    def body(dp_ref, nxt_ref, du_ref):
        i = pl.program_id(0)
        t = i * tm + lax.broadcasted_iota(jnp.int32, (tm + HALO, 1), 0)
        nxt = jnp.where(i < nt - 1, nxt_ref[...], 0.0)
        ext = jnp.concatenate([dp_ref[...], nxt], axis=0)
        for gi, win in enumerate(POOL_WINDOWS):
            cs = slice(gi * PGW, (gi + 1) * PGW)
            s = ext[:, cs] / jnp.minimum(t + 1, win).astype(F32)
            sh = 1
            while sh < win:
                s = s + pltpu.roll(s, tm + HALO - sh, 0)
                sh *= 2
            du_ref[:, cs] = (s[:tm] - dp_ref[:, cs]).astype(BF16)

    return pl.pallas_call(
        body, name="pool_bwd", grid=(nt,),
        in_specs=[pl.BlockSpec((tm, AW), lambda i: (i, 0)),
                  pl.BlockSpec((HALO, AW), lambda i: (jnp.minimum((i + 1) * (tm // HALO), S // HALO - 1), 0))],
        out_specs=pl.BlockSpec((tm, AW), lambda i: (i, 0)),
        out_shape=_sds((S, AW), BF16), compiler_params=_params("parallel"),
    )(dpooled, dpooled)


TB = 1024


def _dh(dproj, wg_in, ride):
    S = dproj.shape[0]
    per = wg_in.shape[2] // TB
    nm, nk = S // TB, IN_W // TB

    def body(dp_ref, w_ref, out_ref):
        @pl.when(pl.program_id(1) == 0)
        def _():
            out_ref[...] = jnp.zeros_like(out_ref)

        out_ref[...] += _dot_nt(dp_ref[...], w_ref[...])

    (dh,), rode = _call_with_ride(
        body, ride, lambda: (pl.program_id(0) == 0) & (pl.program_id(1) == 0),
        lambda: (pl.program_id(0) == nm - 1) & (pl.program_id(1) == nk - 1),
        name="dh", grid=(nm, nk),
        in_specs=[pl.BlockSpec((TB, TB), lambda m, kk: (m, kk)),
                  pl.BlockSpec((None, D, TB), lambda m, kk: (kk // per, 0, kk % per))],
        out_specs=[pl.BlockSpec((TB, D), lambda m, kk: (m, 0))],
        out_shape=[_sds((S, D))], compiler_params=_params("arbitrary", "arbitrary"),
    )(dproj, wg_in)
    return dh, rode


def _dw_in(h, dproj):
    S = dproj.shape[0]
    per = IN_W // N_SHARD // TB
    nk = S // TB

    def body(h_ref, dp_ref, out_ref, out16_ref):
        kk = pl.program_id(1)

        @pl.when(kk == 0)
        def _():
            out_ref[...] = jnp.zeros_like(out_ref)

        out_ref[...] += _dot_tn(h_ref[...], dp_ref[...])

        @pl.when(kk == nk - 1)
        def _():
            out16_ref[...] = out_ref[...].astype(BF16)

    out_spec = pl.BlockSpec((None, D, TB), lambda j, kk: (j // per, 0, j % per))
    return pl.pallas_call(
        body, name="dw_in", grid=(IN_W // TB, nk),
        in_specs=[pl.BlockSpec((TB, D), lambda j, kk: (kk, 0)), pl.BlockSpec((TB, TB), lambda j, kk: (kk, j))],
        out_specs=[out_spec, out_spec],
        out_shape=[_sds((N_SHARD, D, IN_W // N_SHARD)), _sds((N_SHARD, D, IN_W // N_SHARD), BF16)],
        compiler_params=_params("parallel", "arbitrary"),
    )(h, dproj)


def _prenorm_bwd(x, dh, dx2, norm_g, mod):
    S = x.shape[0]
    tm = 512

    def body(x_ref, dh_ref, dx2_ref, g_ref, mod_ref, gx_ref, dg_ref, dshift_ref, dscale_ref):
        i = pl.program_id(0)

        @pl.when(i == 0)
        def _():
            dg_ref[...] = jnp.zeros_like(dg_ref)
            dshift_ref[...] = jnp.zeros_like(dshift_ref)
            dscale_ref[...] = jnp.zeros_like(dscale_ref)

        xv = x_ref[...]
        dhv = dh_ref[...]
        g = g_ref[...]
        r = lax.rsqrt(jnp.mean(xv * xv, axis=-1, keepdims=True) + EPS)
        xh = xv * r
        dshift_ref[...] += jnp.sum(dhv, axis=0, keepdims=True)
        dscale_ref[...] += jnp.sum(dhv * (xh * g), axis=0, keepdims=True)
        dn1 = dhv * (1.0 + mod_ref[:, D:2 * D])
        dg_ref[...] += jnp.sum(dn1 * xh, axis=0, keepdims=True)
        dxh = dn1 * g
        gx_ref[...] = dx2_ref[...] + r * (dxh - xh * jnp.mean(dxh * xh, axis=-1, keepdims=True))

    row = pl.BlockSpec((tm, D), lambda i: (i, 0))
    vec = pl.BlockSpec((1, D), lambda i: (0, 0))
    return pl.pallas_call(
        body, name="prenorm_bwd", grid=(S // tm,),
        in_specs=[row, row, row, vec, pl.BlockSpec((1, 3 * D), lambda i: (0, 0))],
        out_specs=[row, vec, vec, vec],
        out_shape=[_sds((S, D)), _sds((1, D)), _sds((1, D)), _sds((1, D))],
        compiler_params=_params("arbitrary"),
    )(x, dh, dx2, norm_g, mod)


def _local_step(x, target, mod, wg_in, wab, wpb, wout, pool_w, pool_scale, rel_bias, norm_g, final_g, chip_half):
    buckets = jnp.asarray(_bucket_tables())
    bias_tab = _bias_table(rel_bias, buckets)
    h = _prenorm(x, norm_g, mod)
    qkv = [_proj(h, wg_in, 3 * g, 3, BF16 if GROUPS[g][1] == 1 else F32, f"proj_qkv{g}") for g in range(NG)]
    rest = _proj(h, wg_in, NCB_QKV, REST_W // CB, F32, "proj_rest")
    os_, ls_ = zip(*[_attn_fwd(qkv[g], bias_tab, g) for g in range(NG)])
    (dx2, loss, dfinal_g, dgate, dattn, stats, dpooled, dproj, dw_out, dw_ab, dw_pb, dpool_w,
     dpool_scale) = _mix_step(x, target, os_, ls_, rest, wab, wpb, pool_w, pool_scale, wout, mod, final_g)
    du = _pool_bwd(dpooled)

    small = [dw_ab, dw_pb, dw_out]
    dqkv0, ds0, sib_small = _attn_bwd(qkv[0], dattn, stats, bias_tab, 0, _ride_sibling_halves(small))
    p_small = _pair_sum_small(small, sib_small, chip_half)
    dqkv1, ds1, u_small = _attn_bwd(qkv[1], dattn, stats, bias_tab, 1,
                                    _ride_chip_exchange([p16 for _, p16 in p_small]))
    rs_ab, rs_pb, rs_out = _chip_sum_small([p32 for p32, _ in p_small], u_small, chip_half)
    dqkv2, ds2, _ = _attn_bwd(qkv[2], dattn, stats, bias_tab, 2, None)

    for j, piece in enumerate(dqkv0 + dqkv1 + dqkv2):
        dproj = lax.dynamic_update_slice(dproj, piece.astype(BF16), (0, j * AW))
    dproj = lax.dynamic_update_slice(dproj, du, (0, QKV_W + AW))
    dw_in, dw_in16 = _dw_in(h, dproj)
    drel_rows, (sib_in,) = _bias_grad(jnp.concatenate([ds0, ds1, ds2], axis=0), buckets,
                                      _ride_sibling_halves([dw_in16]))
    drel = drel_rows[:, 0, :NUM_BUCKETS].T
    p32_in, p16_in = _pair_sum(dw_in, sib_in, chip_half, "rs_pair_sum_in")
    dh, (u_in,) = _dh(dproj, wg_in, _ride_chip_exchange([p16_in]))
    rs_in = _chip_sum(p32_in, u_in, chip_half, "rs_chip_sum_in")

    grad_x, dnorm_g, dshift, dscale = _prenorm_bwd(x, dh, dx2, norm_g, mod)
    dmod = jnp.concatenate([dshift, dscale, dgate], axis=1)
    return dict(loss=loss[0, 0], grad_x=grad_x, dmod=dmod, dnorm_g=dnorm_g, dfinal_g=dfinal_g, dpool_w=dpool_w,
                dpool_scale=dpool_scale, drel_bias=drel, dw_in=dw_in, dw_attn_br=dw_ab, dw_pool_br=dw_pb,
                dw_out=dw_out, rs_in=rs_in, rs_attn_br=rs_ab, rs_pool_br=rs_pb, rs_out=rs_out)


def _allgather8(blocks, name, relay=None):
    nb = len(blocks)
    relay = [False] * nb if relay is None else list(relay)

    def body(*refs):
        ins, outs = refs[:nb], refs[nb:2 * nb]
        send_sems, recv_sems = refs[2 * nb:]
        x, y, c = lax.axis_index("x"), lax.axis_index("y"), lax.axis_index("c")
        me, sibling = (x, y, c), (x, y, 1 - c)
        here, xn, yn, dg = (x, y), (1 - x, y), (x, 1 - y), (1 - x, 1 - y)

        def slot(a, chip, core, half=None):
            ref = outs[a].at[4 * chip[0] + 2 * chip[1] + core]
            if half is None:
                return ref
            r2 = ref.shape[0] // 2
            return ref.at[pl.ds(half * r2, r2)]

        def copy(a, k, dst, to, src=None):
            return pltpu.make_async_remote_copy(src_ref=dst if src is None else src, dst_ref=dst,
                                                send_sem=send_sems.at[a, k], recv_sem=recv_sems.at[a, k],
                                                device_id=to, device_id_type=MESH)

        def start(cps):
            for cp in cps:
                cp.start()
            return cps

        sent = []
        for a in range(nb):
            own = slot(a, here, c)
            sent += [copy(a, 0, own, sibling, src=ins[a]), copy(a, 1, own, (*xn, c), src=ins[a]),
                     copy(a, 2, own, (*yn, c), src=ins[a])]
            if not relay[a]:
                sent.append(copy(a, 3, own, (*dg, c), src=ins[a]))
        start(sent)
        for a in range(nb):
            copy(a, 2, slot(a, yn, c), me).wait_recv()
            sent += start([copy(a, 6, slot(a, yn, c), sibling)]
                          + ([copy(a, 3, slot(a, yn, c, 0), (*xn, c))] if relay[a] else []))
        for a in range(nb):
            copy(a, 1, slot(a, xn, c), me).wait_recv()
            sent += start([copy(a, 5, slot(a, xn, c), sibling)]
                          + ([copy(a, 4, slot(a, xn, c, 1), (*yn, c))] if relay[a] else []))
        for a in range(nb):
            for k, half in ((3, 0), (4, 1)) if relay[a] else ((3, None),):
                copy(a, k, slot(a, dg, c, half), me).wait_recv()
                sent += start([copy(a, 4 + k, slot(a, dg, c, half), sibling)])
        for a in range(nb):
            copy(a, 0, slot(a, here, 1 - c), me).wait_recv()
            copy(a, 5, slot(a, xn, 1 - c), me).wait_recv()
            copy(a, 6, slot(a, yn, 1 - c), me).wait_recv()
            for k, half in ((7, 0), (8, 1)) if relay[a] else ((7, None),):
                copy(a, k, slot(a, dg, 1 - c, half), me).wait_recv()
        for cp in sent:
            cp.wait_send()

    outs = pl.pallas_call(
        body, name=name, in_specs=[ANY] * nb, out_specs=[ANY] * nb,
        out_shape=[_sds((8,) + b.shape, b.dtype) for b in blocks],
        scratch_shapes=[_dma_sems(nb, 9), _dma_sems(nb, 9)],
    )(*blocks)
    return [_place_own(buf, b) for buf, b in zip(outs, blocks)]


def _place_own(buf, block):
    dev = 4 * lax.axis_index("x") + 2 * lax.axis_index("y") + lax.axis_index("c")
    return lax.dynamic_update_index_in_dim(buf, block, dev, 0)


def _ride_sibling_halves(gs):
    def copies(ins, outs, send_sems, recv_sems):
        x, y, c = lax.axis_index("x"), lax.axis_index("y"), lax.axis_index("c")
        cps = []
        for a in range(len(gs)):
            r2 = ins[a].shape[1] // 2
            other = ins[a].at[:, pl.ds((1 - c) * r2, r2), :]
            cps.append(pltpu.make_async_remote_copy(src_ref=other, dst_ref=outs[a], send_sem=send_sems.at[a],
                                                    recv_sem=recv_sems.at[a], device_id=(x, y, 1 - c),
                                                    device_id_type=MESH))
        return cps

    return _Ride(gs, [_sds((g.shape[0], g.shape[1] // 2, g.shape[2]), g.dtype) for g in gs], len(gs), copies)


def _pair_sum(g, t, chip_half, name):
    nsh, rows, cols = g.shape
    r2 = rows // 2
    tr = _row_tile(r2, cols)
    nt = r2 // tr

    def body(ch_ref, g_ref, t_ref, p32_ref, p16_ref):
        p = g_ref[...] + t_ref[...].astype(F32)
        p16_ref[...] = p.astype(BF16)

        @pl.when(pl.program_id(1) == ch_ref[0])
        def _():
            p32_ref[...] = p

    blk = pl.BlockSpec((None, tr, cols), lambda i, k, ch_ref: (k, i, 0))
    return pl.pallas_call(
        body, name=name,
        grid_spec=pltpu.PrefetchScalarGridSpec(
            num_scalar_prefetch=1, grid=(nt, nsh),
            in_specs=[pl.BlockSpec((None, tr, cols), lambda i, k, ch_ref: (k, ch_ref[1] * nt + i, 0)), blk],
            out_specs=[pl.BlockSpec((tr, cols), lambda i, k, ch_ref: (i, 0)), blk]),
        out_shape=[_sds((r2, cols)), _sds((nsh, r2, cols), BF16)],
        compiler_params=_params("parallel", "arbitrary"),
    )(chip_half, g, t)


def _pair_sum_small(gs, ts, chip_half):
    na = len(gs)

    def body(ch_ref, *refs):
        g_refs, t_refs, outs = refs[:na], refs[na:2 * na], refs[2 * na:]
        for a in range(na):
            r2 = t_refs[a].shape[1]
            own = pl.ds(pl.multiple_of(ch_ref[1] * r2, 8), r2)
            outs[2 * a + 1][...] = (g_refs[a][:, own, :] + t_refs[a][...]).astype(BF16)
            outs[2 * a][...] = g_refs[a][ch_ref[0], own, :] + t_refs[a][ch_ref[0]]

    res = pl.pallas_call(
        body, name="rs_pair_sum_small",
        in_specs=[pl.BlockSpec(memory_space=pltpu.SMEM)] + [pl.BlockSpec(memory_space=pltpu.VMEM)] * (2 * na),
        out_shape=[s for t in ts for s in (_sds(t.shape[1:]), _sds(t.shape, BF16))], compiler_params=_params(),
    )(chip_half, *gs, *ts)
    return [(res[2 * a], res[2 * a + 1]) for a in range(na)]


def _chip_sum_small(p32s, us, chip_half):
    na = len(p32s)

    def body(ch_ref, *refs):
        p_refs, u_refs, outs = refs[:na], refs[na:2 * na], refs[2 * na:]
        for a in range(na):
            r2 = p_refs[a].shape[0]
            acc = p_refs[a][...]
            for j in range(3):
                acc = acc + u_refs[a][j].astype(F32)
            outs[a][pl.ds(pl.multiple_of(ch_ref[1] * r2, 8), r2), :] = acc

    return pl.pallas_call(
        body, name="rs_chip_sum_small",
        in_specs=[pl.BlockSpec(memory_space=pltpu.SMEM)] + [pl.BlockSpec(memory_space=pltpu.VMEM)] * (2 * na),
        out_shape=[_sds((2 * p.shape[0], p.shape[1])) for p in p32s], compiler_params=_params(),
    )(chip_half, *p32s, *us)


def _ride_chip_exchange(ps):
    def copies(ins, outs, send_sems, recv_sems):
        x, y, c = lax.axis_index("x"), lax.axis_index("y"), lax.axis_index("c")
        chips = [(1 - x, y), (x, 1 - y), (1 - x, 1 - y)]
        cps = []
        for a in range(len(ps)):
            for j, (ox, oy) in enumerate(chips):
                cps.append(pltpu.make_async_remote_copy(src_ref=ins[a].at[2 * ox + oy], dst_ref=outs[a].at[j],
                                                        send_sem=send_sems.at[3 * a + j],
                                                        recv_sem=recv_sems.at[3 * a + j],
                                                        device_id=(ox, oy, c), device_id_type=MESH))
        return cps

    return _Ride(ps, [_sds((3,) + p.shape[1:], p.dtype) for p in ps], 3 * len(ps), copies)


def _chip_sum(p32, u, chip_half, name):
    r2, cols = p32.shape
    tr = _row_tile(r2, cols)
    nt = r2 // tr

    def body(ch_ref, p_ref, u_ref, o_ref):
        acc = p_ref[...]
        for j in range(3):
            acc = acc + u_ref[j].astype(F32)
        o_ref[...] = acc

    return pl.pallas_call(
        body, name=name,
        grid_spec=pltpu.PrefetchScalarGridSpec(
            num_scalar_prefetch=1, grid=(nt,),
            in_specs=[pl.BlockSpec((tr, cols), lambda i, ch_ref: (i, 0)),
                      pl.BlockSpec((3, tr, cols), lambda i, ch_ref: (0, i, 0))],
            out_specs=pl.BlockSpec((tr, cols), lambda i, ch_ref: (ch_ref[1] * nt + i, 0))),
        out_shape=_sds((2 * r2, cols)), compiler_params=_params("parallel"),
    )(chip_half, p32, u)


def _sibling_join(fs, name):
    nb = len(fs)

    def body(*refs):
        outs = refs[nb:2 * nb]
        send_sems, recv_sems = refs[2 * nb:]
        x, y, c = lax.axis_index("x"), lax.axis_index("y"), lax.axis_index("c")
        cps = []
        for a in range(nb):
            r2 = outs[a].shape[0] // 2
            rows = outs[a].at[pl.ds(c * r2, r2), :]
            cps.append(pltpu.make_async_remote_copy(src_ref=rows, dst_ref=rows, send_sem=send_sems.at[a],
                                                    recv_sem=recv_sems.at[a], device_id=(x, y, 1 - c),
                                                    device_id_type=MESH))
        for cp in cps:
            cp.start()
        for cp in cps:
            cp.wait()

    return pl.pallas_call(
        body, name=name, in_specs=[ANY] * nb, out_specs=[ANY] * nb,
        out_shape=[_sds(f.shape, f.dtype) for f in fs],
        input_output_aliases={a: a for a in range(nb)},
        scratch_shapes=[_dma_sems(nb), _dma_sems(nb)],
    )(*fs)


def _row_tile(rows, cols):
    tile = rows
    while tile * cols * 4 > (1 << 20) and tile % 16 == 0:
        tile //= 2
    return tile


def _w_ada_grad(c_all, dmod_cols):
    def body(c_ref, d_ref, o_ref):
        o_ref[...] = _dot_tn(c_ref[...].astype(BF16), d_ref[...].astype(BF16))

    return pl.pallas_call(body, name="w_ada_grad", out_shape=_sds((c_all.shape[1], dmod_cols.shape[1])),
                          compiler_params=_params())(c_all, dmod_cols)


def _adam_math(w, g, m, v):
    nm = ADAM_B1 * m + (1.0 - ADAM_B1) * g
    nv = ADAM_B2 * v + (1.0 - ADAM_B2) * (g * g)
    m_hat = nm / (1.0 - ADAM_B1 ** ADAM_STEP)
    v_hat = nv / (1.0 - ADAM_B2 ** ADAM_STEP)
    return -ADAM_LR * (m_hat / (jnp.sqrt(v_hat) + ADAM_EPS) + ADAM_WD * w), nm, nv


def _adamw(w, g, m, v, name):
    rows, cols = w.shape
    tr = _row_tile(rows, cols)

    def body(w_ref, g_ref, m_ref, v_ref, go_ref, d_ref, nm_ref, nv_ref):
        gv = g_ref[...]
        go_ref[...] = gv
        d_ref[...], nm_ref[...], nv_ref[...] = _adam_math(w_ref[...], gv, m_ref[...], v_ref[...])

    spec = pl.BlockSpec((tr, cols), lambda i: (i, 0))
    return pl.pallas_call(
        body, name=name, grid=(rows // tr,), in_specs=[spec] * 4, out_specs=[spec] * 4,
        out_shape=[_sds((rows, cols))] * 4, compiler_params=_params("parallel"),
    )(w, g, m, v)


def _pack_small(dmod, dnorm_g, dfinal_g, dpool_scale, drel_bias, loss, dpool_w):
    return jnp.concatenate([dmod.reshape(-1, 128), dnorm_g.reshape(-1, 128), dfinal_g.reshape(-1, 128),
                            jnp.pad(dpool_scale.reshape(-1, 128), ((0, PK_RELB - PK_PSCALE - AW // 128), (0, 0))),
                            jnp.pad(drel_bias, ((0, 0), (0, 128 - NG * NH))),
                            jnp.full((PK_POOLW - PK_LOSS, 128), loss, F32), dpool_w.reshape(-1, 128)], axis=0)


def _small_update(small_all, ws, ms, vs):
    lane_rows = [(r0, r0 + w.shape[1] // 128) for r0, w in zip((PK_BADA, PK_NORMG, PK_FINALG, PK_PSCALE), ws)]
    nw = len(ws)

    def body(all_ref, *refs):
        w_refs, m_refs, v_refs = refs[:nw], refs[nw:2 * nw], refs[2 * nw:3 * nw]
        loss_ref, outs = refs[3 * nw], refs[3 * nw + 1:]
        g = all_ref[0]
        for s in range(1, all_ref.shape[0]):
            g = g + all_ref[s]
        loss_ref[...] = jnp.broadcast_to(g[PK_LOSS:PK_LOSS + 1, :], loss_ref.shape)

        def put(p, at, gv):
            d, nm, nv = _adam_math(w_refs[p][at], gv, m_refs[p][at], v_refs[p][at])
            for o_ref, val in zip(outs[4 * p:4 * p + 4], (gv, d, nm, nv)):
                o_ref[at] = val

        for p, (r0, r1) in enumerate(lane_rows):
            for i in range(r1 - r0):
                put(p, (slice(None), slice(128 * i, 128 * (i + 1))), g[r0 + i:r0 + i + 1, :])
        put(4, (slice(None), slice(None)), g[PK_RELB:PK_LOSS, 0:NG * NH])
        put(5, (slice(None), slice(None)), g[PK_POOLW:PK_ROWS, :])

    res = pl.pallas_call(
        body, name="small_update",
        out_shape=[_sds((8, 128))] + [_sds(w.shape) for w in ws for _ in range(4)], compiler_params=_params(),
    )(small_all, *ws, *ms, *vs)
    return res[0], [res[1 + 4 * p:5 + 4 * p] for p in range(nw)]


def kernel(x, c, norm_g, w_ada, b_ada, w_in, pool_w, pool_scale, w_attn_br, w_pool_br, w_out, rel_bias, final_g, loss_target, m_norm_g, m_w_ada, m_b_ada, m_w_in, m_pool_w, m_pool_scale, m_w_attn_br, m_w_pool_br, m_w_out, m_rel_bias, m_final_g, v_norm_g, v_w_ada, v_b_ada, v_w_in, v_pool_w, v_pool_scale, v_w_attn_br, v_w_pool_br, v_w_out, v_rel_bias, v_final_g):
    ix, iy, ic = lax.axis_index("x"), lax.axis_index("y"), lax.axis_index("c")
    dev = 4 * ix + 2 * iy + ic
    chip = 2 * ix + iy

    def half(w):
        r2 = w.shape[0] // 2
        return lax.dynamic_slice_in_dim(w, ic * r2, r2, axis=0).astype(BF16)

    gathered = _allgather8([jnp.broadcast_to(c, (8, D)), half(w_in[0]), half(w_attn_br[0]), half(w_pool_br[0]),
                            half(w_out[0])], "gather_weights", relay=[False, True, True, True, True])
    c_all = gathered[0][:, 0, :]
    wg_in = gathered[1].reshape(N_SHARD, D, IN_W // N_SHARD)
    wab = gathered[2].reshape(N_SHARD, AW, D // N_SHARD).transpose(1, 0, 2).reshape(AW, D)
    wpb = gathered[3].reshape(N_SHARD, AW, D // N_SHARD).transpose(1, 0, 2).reshape(AW, D)
    wout = gathered[4].reshape(D, D)

    mw = 3 * D // N_SHARD
    modp = _mod_partial(c_all, w_ada[0], lax.dynamic_slice_in_dim(b_ada, chip * mw, mw, axis=1))
    mod_all = _allgather8([modp], "gather_mod")[0]
    mod_full = mod_all[::2].transpose(1, 0, 2).reshape(8, 3 * D)
    mod = lax.dynamic_slice_in_dim(mod_full, dev, 1, axis=0)

    chip_half = jnp.stack([chip, ic]).astype(jnp.int32)
    r = _local_step(x[0], loss_target[0], mod, wg_in, wab, wpb, wout, pool_w[0], pool_scale, rel_bias, norm_g,
                    final_g.reshape(1, D), chip_half)

    packed = _pack_small(r["dmod"], r["dnorm_g"], r["dfinal_g"], r["dpool_scale"], r["drel_bias"], r["loss"],
                         r["dpool_w"])
    small_all = _allgather8([packed], "gather_small")[0]
    small = ["b_ada", "norm_g", "final_g", "pool_scale", "rel_bias", "pool_w"]
    shaped = lambda b, n, f, ps, rb, pw: [b, n, f.reshape(1, D), ps, rb, pw.reshape(4 * PGW, PGW)]
    loss, small_out = _small_update(small_all, shaped(b_ada, norm_g, final_g, pool_scale, rel_bias, pool_w),
                                    shaped(m_b_ada, m_norm_g, m_final_g, m_pool_scale, m_rel_bias, m_pool_w),
                                    shaped(v_b_ada, v_norm_g, v_final_g, v_pool_scale, v_rel_bias, v_pool_w))
    dmod_all = small_all[:, PK_BADA:PK_NORMG, :].reshape(8, 3 * D)
    g_w_ada = _w_ada_grad(c_all, lax.dynamic_slice_in_dim(dmod_all, chip * mw, mw, axis=1))

    g_w_in, g_w_ab, g_w_pb, g_w_out = _sibling_join([r["rs_in"], r["rs_attn_br"], r["rs_pool_br"], r["rs_out"]],
                                                    "rs_sibling_join")
    upd = dict(zip(small, small_out))
    upd["final_g"] = [a.reshape(D) for a in upd["final_g"]]
    upd["pool_w"] = [a.reshape(1, 4, PGW, PGW) for a in upd["pool_w"]]
    for nme, w, g, m, v in (("w_ada", w_ada, g_w_ada, m_w_ada, v_w_ada), ("w_in", w_in, g_w_in, m_w_in, v_w_in),
                            ("w_attn_br", w_attn_br, g_w_ab, m_w_attn_br, v_w_attn_br),
                            ("w_pool_br", w_pool_br, g_w_pb, m_w_pool_br, v_w_pool_br),
                            ("w_out", w_out, g_w_out, m_w_out, v_w_out)):
        upd[nme] = [a[None] for a in _adamw(w[0], g, m[0], v[0], "adamw_" + nme)]
    names = ["norm_g", "w_ada", "b_ada", "w_in", "pool_w", "pool_scale", "w_attn_br", "w_pool_br", "w_out",
             "rel_bias", "final_g"]
    return (loss[0, 0], r["grad_x"][None]) + tuple(upd[nme][kind] for kind in range(4) for nme in names)
```

```python
import functools
import math

import numpy as np
import jax
import jax.numpy as jnp
from jax import lax
from jax.experimental import pallas as pl
from jax.experimental.pallas import tpu as pltpu

F32 = jnp.float32
BF16 = jnp.bfloat16

D = 1024
HD = 64
NH = 8
AW = NH * HD
GROUPS = ((128, 1), (512, 4), (2048, 16))
NG = len(GROUPS)
BLK = 128
GW = 3 * AW
QKV_W = NG * GW
REST_W = 3584
IN_W = QKV_W + REST_W
CB = 512
NCB = IN_W // CB
NCB_QKV = QKV_W // CB
POOL_WINDOWS = (2, 4, 8, 16)
PGW = 128
HALO = 16
NUM_BUCKETS = 32
MAX_DISTANCE = 2048
EPS = 1e-6
NEG = -1e30
N_SHARD = 4
VMEM_LIMIT = 56 * 1024 * 1024

ADAM_LR = 0.001
ADAM_B1 = 0.9
ADAM_B2 = 0.999
ADAM_EPS = 1e-08
ADAM_WD = 0.01
ADAM_STEP = 10

PK_BADA, PK_NORMG, PK_FINALG, PK_PSCALE, PK_RELB, PK_LOSS, PK_POOLW, PK_ROWS = 0, 24, 32, 40, 48, 80, 88, 600

ANY = pl.BlockSpec(memory_space=pl.ANY)
MESH = pl.DeviceIdType.MESH


def _params(*sem):
    return pltpu.CompilerParams(dimension_semantics=sem, vmem_limit_bytes=VMEM_LIMIT)


def _sds(shape, dtype=F32):
    return jax.ShapeDtypeStruct(shape, dtype)


def _dot(a, b):
    return jnp.dot(a, b, preferred_element_type=F32)


def _dot_nt(a, b):
    return lax.dot_general(a, b, (((1,), (1,)), ((), ())), preferred_element_type=F32)


def _dot_tn(a, b):
    return lax.dot_general(a, b, (((0,), (0,)), ((), ())), preferred_element_type=F32)


def _sigmoid(z):
    return 0.5 * jnp.tanh(0.5 * z) + 0.5


def _dma_sems(*shape):
    return pltpu.SemaphoreType.DMA(shape)


class _Ride:
    def __init__(self, arrays, out_shapes, n_copies, copies):
        self.arrays, self.out_shapes, self.n_copies, self.copies = list(arrays), list(out_shapes), n_copies, copies


def _call_with_ride(body, ride, first, last, *, in_specs, out_specs, out_shape, scratch_shapes=(), **kw):
    in_specs, out_specs, out_shape, scratch_shapes = list(in_specs), list(out_specs), list(out_shape), list(scratch_shapes)
    n_in, n_out, n_sc = len(in_specs), len(out_specs), len(scratch_shapes)
    if ride is None:
        def run_plain(*operands):
            return pl.pallas_call(body, in_specs=in_specs, out_specs=out_specs, out_shape=out_shape,
                                  scratch_shapes=scratch_shapes, **kw)(*operands), []
        return run_plain
    n_ri, n_ro = len(ride.arrays), len(ride.out_shapes)

    def wrapped(*refs):
        ins, rest = refs[:n_in], refs[n_in:]
        r_ins, rest = rest[:n_ri], rest[n_ri:]
        outs, rest = rest[:n_out], rest[n_out:]
        r_outs, rest = rest[:n_ro], rest[n_ro:]
        scratch, (send_sems, recv_sems) = rest[:n_sc], rest[n_sc:]

        @pl.when(first())
        def _():
            for cp in ride.copies(r_ins, r_outs, send_sems, recv_sems):
                cp.start()

        body(*ins, *outs, *scratch)

        @pl.when(last())
        def _():
            for cp in ride.copies(r_ins, r_outs, send_sems, recv_sems):
                cp.wait()

    def run(*operands):
        res = pl.pallas_call(
            wrapped, in_specs=in_specs + [ANY] * n_ri, out_specs=out_specs + [ANY] * n_ro,
            out_shape=out_shape + ride.out_shapes,
            scratch_shapes=scratch_shapes + [_dma_sems(ride.n_copies), _dma_sems(ride.n_copies)], **kw,
        )(*operands, *ride.arrays)
        return res[:n_out], res[n_out:]
    return run


def _bucket_tables():
    i = np.arange(BLK)[:, None]
    j = np.arange(2 * BLK)[None, :]
    dist = BLK + i - j
    valid = (dist >= 0) & (dist <= BLK)
    tabs = []
    for _, dil in GROUPS:
        n = (np.clip(dist, 0, BLK) * dil).astype(np.int32)
        max_exact = NUM_BUCKETS // 2
        nf = np.maximum(n, 1).astype(np.float32)
        large = max_exact + (np.log(nf / np.float32(max_exact)) / np.float32(math.log(MAX_DISTANCE / max_exact))
                             * np.float32(NUM_BUCKETS - max_exact)).astype(np.int32)
        large = np.minimum(large, NUM_BUCKETS - 1)
        bucket = np.where(n < max_exact, n, large)
        tab = np.where(valid, bucket, -1).astype(np.int32)
        perm = _block_perm(dil)
        tabs.append(tab[perm][:, np.concatenate([perm, BLK + perm])])
    return np.stack(tabs)


def _bias_table(rel_bias, buckets):
    def body(rb_ref, bk_ref, out_ref):
        g = pl.program_id(0)
        bk = bk_ref[...]
        for h in range(NH):
            acc = jnp.full((BLK, 2 * BLK), NEG, F32)
            for b in range(NUM_BUCKETS):
                acc = jnp.where(bk == b, rb_ref[b, g * NH + h], acc)
            out_ref[h] = acc

    return pl.pallas_call(
        body, name="bias_table", grid=(NG,),
        in_specs=[pl.BlockSpec(memory_space=pltpu.SMEM),
                  pl.BlockSpec((None, BLK, 2 * BLK), lambda g: (g, 0, 0))],
        out_specs=pl.BlockSpec((NH, BLK, 2 * BLK), lambda g: (g, 0, 0)),
        out_shape=_sds((NG * NH, BLK, 2 * BLK)),
        compiler_params=_params("arbitrary"),
    )(rel_bias, buckets)


def _bias_grad(ds_acc, buckets, ride):
    def body(acc_ref, bk_ref, out_ref):
        bk = bk_ref[...]
        acc = acc_ref[...]
        lane = lax.broadcasted_iota(jnp.int32, (8, 128), 1)
        out = jnp.zeros((8, 128), F32)
        for b in range(NUM_BUCKETS):
            val = jnp.sum(jnp.where(bk == b, acc, 0.0))
            out = jnp.where(lane == b, val, out)
        out_ref[...] = out

    (out,), rode = _call_with_ride(
        body, ride, lambda: pl.program_id(0) == 0, lambda: pl.program_id(0) == NG * NH - 1,
        name="bias_grad", grid=(NG * NH,),
        in_specs=[pl.BlockSpec((None, BLK, 2 * BLK), lambda gh: (gh, 0, 0)),
                  pl.BlockSpec((None, BLK, 2 * BLK), lambda gh: (gh // NH, 0, 0))],
        out_specs=[pl.BlockSpec((None, 8, 128), lambda gh: (gh, 0, 0))],
        out_shape=[_sds((NG * NH, 8, 128))],
        compiler_params=_params("arbitrary"),
    )(ds_acc, buckets)
    return out, rode


def _mod_partial(c_all, w_ada_s, b_ada_s):
    def body(c_ref, w_ref, b_ref, o_ref):
        o_ref[...] = _dot(c_ref[...].astype(BF16), w_ref[...].astype(BF16)) + b_ref[...]

    return pl.pallas_call(body, name="mod_partial", out_shape=_sds((8, w_ada_s.shape[1])),
                          compiler_params=_params())(c_all, w_ada_s, b_ada_s)


def _prenorm(x, norm_g, mod):
    S = x.shape[0]
    tm = 512

    def body(x_ref, g_ref, mod_ref, h_ref):
        xv = x_ref[...]
        r = lax.rsqrt(jnp.mean(xv * xv, axis=-1, keepdims=True) + EPS)
        n1 = xv * r * g_ref[...]
        h_ref[...] = (n1 * (1.0 + mod_ref[:, D:2 * D]) + mod_ref[:, 0:D]).astype(BF16)

    return pl.pallas_call(
        body, name="prenorm", grid=(S // tm,),
        in_specs=[pl.BlockSpec((tm, D), lambda i: (i, 0)), pl.BlockSpec((1, D), lambda i: (0, 0)),
                  pl.BlockSpec((1, 3 * D), lambda i: (0, 0))],
        out_specs=pl.BlockSpec((tm, D), lambda i: (i, 0)),
        out_shape=_sds((S, D), BF16), compiler_params=_params("parallel"),
    )(x, norm_g, mod)


def _proj(h, wg_in, j0, nj, dtype, name):
    S = h.shape[0]
    tm = 2048
    per = wg_in.shape[2] // CB

    def body(h_ref, w_ref, o_ref):
        o_ref[...] = _dot(h_ref[...], w_ref[...]).astype(dtype)

    return pl.pallas_call(
        body, name=name, grid=(S // tm, nj),
        in_specs=[pl.BlockSpec((tm, D), lambda m, j: (m, 0)),
                  pl.BlockSpec((None, D, CB), lambda m, j: ((j0 + j) // per, 0, (j0 + j) % per))],
        out_specs=pl.BlockSpec((tm, CB), lambda m, j: (m, j)),
        out_shape=_sds((S, nj * CB), dtype), compiler_params=_params("parallel", "parallel"),
    )(h, wg_in)


HS = 4
SLAB = HS * HD


def _lane_head(rows):
    return lax.broadcasted_iota(jnp.int32, (rows, SLAB), 1) // HD


def _head_stack(a):
    head = _lane_head(a.shape[0])
    return jnp.concatenate([jnp.where(head == h, a, jnp.zeros_like(a)) for h in range(HS)], axis=0)


def _head_unstack(a):
    rows = a.shape[0] // HS
    head = _lane_head(rows)
    out = a[:rows]
    for h in range(1, HS):
        out = jnp.where(head == h, a[h * rows:(h + 1) * rows], out)
    return out


STAT_W = 128
VIEW = 16


def _sub_layout(dil):
    if dil == 1:
        return BLK, [None]
    return BLK * dil // VIEW, [[r + dil * u for u in range(VIEW // dil)] for r in range(dil)]


def _block_perm(dil):
    a_rows, _ = _sub_layout(dil)
    p = np.arange(BLK)
    return p if dil == 1 else (VIEW // dil) * (p % a_rows) + p // a_rows


LB = 128
N_SLAB = NH // HS


def _ld(refs, bs, s, w):
    if bs is None:
        return refs[0][:, s * w:(s + 1) * w]
    a_rows = refs[0].shape[0] // VIEW
    return jnp.concatenate([jnp.concatenate([ref[pl.ds(b, a_rows, stride=VIEW), :] for b in bs], axis=0)
                            for ref in refs], axis=1)


def _st(ref, bs, s, val):
    if bs is None:
        ref[:, s * SLAB:(s + 1) * SLAB] = val.astype(ref.dtype)
        return
    a_rows = val.shape[0] // len(bs)
    for u, b in enumerate(bs):
        ref[:, b, :] = val[u * a_rows:(u + 1) * a_rows]


def _attn_views(dil, S):
    a_rows, subs = _sub_layout(dil)
    if dil == 1:
        def ispecs(base, w, f):
            return [pl.BlockSpec((BLK, N_SLAB * w), lambda sg, n: (f(n), base // (N_SLAB * w)))]
        return subs, S // BLK, N_SLAB, ispecs, (lambda w: (S, w)), (
            lambda f: pl.BlockSpec((BLK, AW), lambda sg, n: (f(n), 0)))

    def ispecs(base, w, f):
        return [pl.BlockSpec((a_rows * VIEW, LB), lambda sg, n, k=k: (f(n), (base + sg * w) // LB + k))
                for k in range(w // LB)]
    return subs, S // (a_rows * VIEW), 1, ispecs, (lambda w: (S // VIEW, VIEW, w)), (
        lambda f: pl.BlockSpec((a_rows, VIEW, SLAB), lambda sg, n: (f(n), 0, sg)))


def _attn_fwd(qkv_g, bias_tab, g):
    S = qkv_g.shape[0]
    subs, nbq, sps, ispecs, shape, ospec = _attn_views(GROUPS[g][1], S)
    cur = lambda n: n
    in_specs = [ispecs(0, SLAB, cur), ispecs(AW, SLAB, cur), ispecs(2 * AW, SLAB, cur)]
    nl = len(in_specs[0])

    def body(*refs):
        q, k, v = (refs[t * nl:(t + 1) * nl] for t in range(3))
        b_ref, o_ref, l_ref, kprev, vprev = refs[3 * nl:]
        n = pl.program_id(1)

        @pl.when(n == 0)
        def _():
            kprev[...] = jnp.zeros_like(kprev)
            vprev[...] = jnp.zeros_like(vprev)

        col = lax.broadcasted_iota(jnp.int32, (HS * BLK, 2 * BLK), 1)
        keep = (col >= BLK) | (n > 0)
        for s_ in range(sps):
            cs = slice(s_ * SLAB, (s_ + 1) * SLAB)
            bias = b_ref[pl.ds(s_ * HS, HS)].reshape(HS * BLK, 2 * BLK)
            for i, bs in enumerate(subs):
                kc, vc = _ld(k, bs, s_, SLAB).astype(BF16), _ld(v, bs, s_, SLAB).astype(BF16)
                kb = jnp.concatenate([kprev[i, :, cs], kc], axis=0)
                vb = jnp.concatenate([vprev[i, :, cs], vc], axis=0)
                kprev[i, :, cs], vprev[i, :, cs] = kc, vc
                s = _dot_nt(_head_stack(_ld(q, bs, s_, SLAB).astype(BF16)), kb) * (HD ** -0.5) + bias
                s = jnp.where(keep, s, NEG)
                m = jnp.max(s, axis=-1, keepdims=True)
                p = jnp.exp(s - m)
                den = jnp.sum(p, axis=-1, keepdims=True)
                _st(o_ref, bs, s_, _head_unstack(_dot(p.astype(BF16), vb) / den))
                _st(l_ref, bs, s_, _head_unstack(jnp.broadcast_to(m + jnp.log(den), (HS * BLK, SLAB))))

    out = _sds(shape(AW))
    nsg = N_SLAB // sps
    o, l = pl.pallas_call(
        body, name=f"attn_fwd{g}", grid=(nsg, nbq),
        in_specs=sum(in_specs, []) + [pl.BlockSpec((sps * HS, BLK, 2 * BLK), lambda sg, n: (g * nsg + sg, 0, 0))],
        out_specs=[ospec(cur), ospec(cur)],
        out_shape=[out, out],
        scratch_shapes=[pltpu.VMEM((len(subs), BLK, sps * SLAB), BF16)] * 2,
        compiler_params=_params("parallel", "arbitrary"),
    )(*([qkv_g] * (3 * nl)), bias_tab)
    return o.reshape(S, AW), l.reshape(S, AW)


def _attn_bwd(qkv_g, dattn, stats, bias_tab, g, ride):
    S = qkv_g.shape[0]
    subs, nbq, sps, ispecs, shape, ospec = _attn_views(GROUPS[g][1], S)
    cur = lambda n: jnp.minimum(n, nbq - 1)
    late = lambda n: jnp.maximum(n - 1, 0)
    in_specs = [ispecs(0, SLAB, cur), ispecs(AW, SLAB, cur), ispecs(2 * AW, SLAB, cur), ispecs(0, SLAB, cur),
                ispecs(0, STAT_W, cur)]
    nl = len(in_specs[0])

    def body(*refs):
        q, k, v, da = (refs[t * nl:(t + 1) * nl] for t in range(4))
        st_ref, b_ref, dq_ref, dk_ref, dv_ref, ds_ref, ck_ref, cv_ref, kprev, vprev = refs[4 * nl:]
        n = pl.program_id(1)

        @pl.when(n == 0)
        def _():
            for ref in (ds_ref, ck_ref, cv_ref, kprev, vprev):
                ref[...] = jnp.zeros_like(ref)

        @pl.when(n < nbq)
        def _():
            col = lax.broadcasted_iota(jnp.int32, (HS * BLK, 2 * BLK), 1)
            keep = (col >= BLK) | (n > 0)
            for s_ in range(sps):
                cs = slice(s_ * SLAB, (s_ + 1) * SLAB)
                bias = b_ref[pl.ds(s_ * HS, HS)].reshape(HS * BLK, 2 * BLK)
                for i, bs in enumerate(subs):
                    st = _ld((st_ref,), bs, s_, STAT_W)
                    kc, vc = _ld(k, bs, s_, SLAB).astype(BF16), _ld(v, bs, s_, SLAB).astype(BF16)
                    kb = jnp.concatenate([kprev[i, :, cs], kc], axis=0)
                    vb = jnp.concatenate([vprev[i, :, cs], vc], axis=0)
                    kprev[i, :, cs], vprev[i, :, cs] = kc, vc
                    lse =jnp.concatenate([st[:, h:h + 1] for h in range(HS)], axis=0)
                    delta = jnp.concatenate([st[:, HS + h:HS + h + 1] for h in range(HS)], axis=0)
                    qs = _head_stack(_ld(q, bs, s_, SLAB).astype(BF16))
                    dos = _head_stack(_ld(da, bs, s_, SLAB).astype(BF16))
                    s = _dot_nt(qs, kb) * (HD ** -0.5) + bias
                    s = jnp.where(keep, s, NEG)
                    p = jnp.exp(s - lse)
                    ds = p * (_dot_nt(dos, vb) - delta)
                    ds_ref[pl.ds(s_ * HS, HS)] += ds.reshape(HS, BLK, 2 * BLK)
                    ds_b = (ds * (HD ** -0.5)).astype(BF16)
                    _st(dq_ref, bs, s_, _head_unstack(_dot(ds_b, kb)))
                    dkb = _dot_tn(ds_b, qs)
                    dvb = _dot_tn(p.astype(BF16), dos)
                    _st(dk_ref, bs, s_, ck_ref[i, :, cs] + dkb[:BLK])
                    _st(dv_ref, bs, s_, cv_ref[i, :, cs] + dvb[:BLK])
                    ck_ref[i, :, cs] = dkb[BLK:]
                    cv_ref[i, :, cs] = dvb[BLK:]

        @pl.when(n == nbq)
        def _():
            for s_ in range(sps):
                for i, bs in enumerate(subs):
                    _st(dk_ref, bs, s_, ck_ref[i, :, s_ * SLAB:(s_ + 1) * SLAB])
                    _st(dv_ref, bs, s_, cv_ref[i, :, s_ * SLAB:(s_ + 1) * SLAB])

    out = _sds(shape(AW), BF16 if GROUPS[g][1] == 1 else F32)
    nsg = N_SLAB // sps
    (dq, dk, dv, ds_acc), rode = _call_with_ride(
        body, ride, lambda: (pl.program_id(0) == 0) & (pl.program_id(1) == 0),
        lambda: (pl.program_id(0) == nsg - 1) & (pl.program_id(1) == nbq),
        name=f"attn_bwd{g}", grid=(nsg, nbq + 1),
        in_specs=sum(in_specs, []) + [pl.BlockSpec((sps * HS, BLK, 2 * BLK), lambda sg, n: (g * nsg + sg, 0, 0))],
        out_specs=[ospec(cur), ospec(late), ospec(late),
                   pl.BlockSpec((sps * HS, BLK, 2 * BLK), lambda sg, n: (sg, 0, 0))],
        out_shape=[out] * 3 + [_sds((NH, BLK, 2 * BLK))],
        scratch_shapes=[pltpu.VMEM((len(subs), BLK, sps * SLAB), F32)] * 2
        + [pltpu.VMEM((len(subs), BLK, sps * SLAB), BF16)] * 2,
        compiler_params=_params("arbitrary", "arbitrary"),
    )(*([qkv_g] * (3 * nl)), *([dattn] * nl), stats, bias_tab)
    return [dq.reshape(S, AW), dk.reshape(S, AW), dv.reshape(S, AW)], ds_acc, rode


TM_MIX = 256


def _mix_specs(tm):
    row512 = pl.BlockSpec((tm, AW), lambda i: (i, 0))
    return ([row512] * 6 + [
        pl.BlockSpec((tm, REST_W), lambda i: (i, 0)),
        pl.BlockSpec((HALO, AW), lambda i: (jnp.maximum(i * (tm // HALO) - 1, 0), 1)),
        pl.BlockSpec((AW, D), lambda i: (0, 0)), pl.BlockSpec((AW, D), lambda i: (0, 0)),
        pl.BlockSpec((4, PGW, PGW), lambda i: (0, 0, 0)), pl.BlockSpec((1, AW), lambda i: (0, 0))])


def _mix_forward(i, tm, o_refs, l_refs, rest_ref, halo_ref, wab_ref, wpb_ref, pw_ref, ps_ref):
    l0, l1, l2 = (r[...] for r in l_refs)
    mx = jnp.maximum(jnp.maximum(l0, l1), l2)
    e0, e1, e2 = jnp.exp(l0 - mx), jnp.exp(l1 - mx), jnp.exp(l2 - mx)
    den = e0 + e1 + e2
    lj = mx + jnp.log(den)
    attn = (e0 * o_refs[0][...] + e1 * o_refs[1][...] + e2 * o_refs[2][...]) / den

    z_attn = rest_ref[:, 0:AW]
    u = rest_ref[:, AW:2 * AW]
    z_pool = rest_ref[:, 2 * AW:3 * AW]
    g_attn = rest_ref[:, 3 * AW:3 * AW + D]
    g_pool = rest_ref[:, 3 * AW + D:3 * AW + 2 * D]

    sg_a = _sigmoid(z_attn)
    sil_a = z_attn * sg_a
    a_g = (attn * sil_a).astype(BF16)
    y_attn = _dot(a_g, wab_ref[...])

    halo = jnp.where(i > 0, halo_ref[...], 0.0)
    ext = jnp.concatenate([halo, u], axis=0)
    t = i * tm + lax.broadcasted_iota(jnp.int32, (tm, 1), 0)
    pooled, mixed_raw = [], []
    for gi, win in enumerate(POOL_WINDOWS):
        s = ext[:, gi * PGW:(gi + 1) * PGW]
        sh = 1
        while sh < win:
            s = s + pltpu.roll(s, sh, 0)
            sh *= 2
        cnt = jnp.minimum(t + 1, win).astype(F32)
        pg = s[HALO:] / cnt - u[:, gi * PGW:(gi + 1) * PGW]
        pooled.append(pg.astype(BF16))
        mixed_raw.append(_dot(pooled[-1], pw_ref[gi].astype(BF16)))
    mixed_raw = jnp.concatenate(mixed_raw, axis=1)
    mixed = mixed_raw * ps_ref[...]
    sg_p = _sigmoid(z_pool)
    sil_p = z_pool * sg_p
    m_g = (mixed * sil_p).astype(BF16)
    y_pool = _dot(m_g, wpb_ref[...])

    sa = _sigmoid(g_attn)
    sp = _sigmoid(g_pool)
    merged = sa * y_attn + sp * y_pool
    return dict(lj=lj, attn=attn, z_attn=z_attn, z_pool=z_pool, sg_a=sg_a, sil_a=sil_a, a_g=a_g, y_attn=y_attn,
                pooled=pooled, mixed_raw=mixed_raw, mixed=mixed, sg_p=sg_p, sil_p=sil_p, m_g=m_g, y_pool=y_pool,
                sa=sa, sp=sp, merged=merged)


def _mix_step(x, target, os_, ls_, rest, wab, wpb, pool_w, pool_scale, wout, mod, final_g):
    S = x.shape[0]
    tm = TM_MIX
    nt = S // tm
    sw = D // N_SHARD

    def body(o0, o1, o2, l0, l1, l2, rest_ref, halo_ref, wab_ref, wpb_ref, pw_ref, ps_ref,
             x_ref, t_ref, wo_ref, mod_ref, fg_ref, dx2_ref, loss_ref, dfg_ref, dgate_ref,
             dattn_ref, stats_ref, dpooled_ref, dproj_hbm, dwo_hbm, dwab_hbm, dwpb_hbm, dpw_ref, dps_ref,
             awo, awab, awpb, stage, stage_sem):
        i = pl.program_id(0)
        slot = i % 2

        def staged(step, sl):
            return pltpu.make_async_copy(stage.at[sl], dproj_hbm.at[pl.ds(step * tm, tm), pl.ds(QKV_W, REST_W)],
                                         stage_sem.at[sl])

        @pl.when(i == 0)
        def _():
            for ref in (loss_ref, dfg_ref, dgate_ref, awo, awab, awpb, dpw_ref, dps_ref):
                ref[...] = jnp.zeros_like(ref)

        f = _mix_forward(i, tm, (o0, o1, o2), (l0, l1, l2), rest_ref, halo_ref, wab_ref, wpb_ref, pw_ref, ps_ref)
        mo = _dot(f["merged"].astype(BF16), wo_ref[...])
        gate = mod_ref[:, 2 * D:3 * D]
        fg = fg_ref[...]
        x2 = x_ref[...] + gate * mo
        r2 = lax.rsqrt(jnp.mean(x2 * x2, axis=-1, keepdims=True) + EPS)
        n2 = x2 * r2
        err = n2 * fg - t_ref[...]
        loss_ref[...] += 0.5 * jnp.sum(jnp.mean(err * err, axis=-1, keepdims=True))
        dy = err * (1.0 / D)
        dfg_ref[...] += jnp.sum(dy * n2, axis=0, keepdims=True)
        dn = dy * fg
        dx2 = r2 * (dn - n2 * jnp.mean(dn * n2, axis=-1, keepdims=True))
        dgate_ref[...] += jnp.sum(dx2 * mo, axis=0, keepdims=True)
        dx2_ref[...] = dx2

        dmo_b = (dx2 * gate).astype(BF16)
        dmerged = _dot_nt(dmo_b, wo_ref[...])
        awo[...] += _dot_tn(f["merged"].astype(BF16), dmo_b)
        sa, sp = f["sa"], f["sp"]
        dya = (dmerged * sa).astype(BF16)
        dyp = (dmerged * sp).astype(BF16)
        dg_attn = dmerged * f["y_attn"] * sa * (1.0 - sa)
        dg_pool = dmerged * f["y_pool"] * sp * (1.0 - sp)
        dag = _dot_nt(dya, wab_ref[...])
        awab[...] += _dot_tn(f["a_g"], dya)
        dmg = _dot_nt(dyp, wpb_ref[...])
        awpb[...] += _dot_tn(f["m_g"], dyp)
        dattn = dag * f["sil_a"]
        dattn_ref[...] = dattn
        prod = dattn * f["attn"]
        lane = lax.broadcasted_iota(jnp.int32, (tm, STAT_W), 1)
        for sb in range(N_SLAB):
            st = jnp.zeros((tm, STAT_W), F32)
            for h in range(HS):
                hs = slice((sb * HS + h) * HD, (sb * HS + h + 1) * HD)
                st = jnp.where(lane == h, f["lj"][:, hs.start:hs.start + 1], st)
                st = jnp.where(lane == HS + h, jnp.sum(prod[:, hs], axis=-1, keepdims=True), st)
            stats_ref[:, sb * STAT_W:(sb + 1) * STAT_W] = st
        dz_attn = dag * f["attn"] * (f["sg_a"] * (1.0 + f["z_attn"] * (1.0 - f["sg_a"])))
        dmixed = dmg * f["sil_p"]
        dz_pool = dmg * f["mixed"] * (f["sg_p"] * (1.0 + f["z_pool"] * (1.0 - f["sg_p"])))
        dps_ref[...] += jnp.sum(dmixed * f["mixed_raw"], axis=0, keepdims=True)
        dpm = (dmixed * ps_ref[...]).astype(BF16)
        for gi in range(len(POOL_WINDOWS)):
            cs = slice(gi * PGW, (gi + 1) * PGW)
            dpw_ref[gi] += _dot_tn(f["pooled"][gi], dpm[:, cs])
            dpooled_ref[:, cs] = _dot_nt(dpm[:, cs], pw_ref[gi].astype(BF16))
        @pl.when(i >= 2)
        def _():
            staged(i - 2, slot).wait()

        stage[slot, :, 0:AW] = dz_attn.astype(BF16)
        stage[slot, :, AW:2 * AW] = jnp.zeros((tm, AW), BF16)
        stage[slot, :, 2 * AW:3 * AW] = dz_pool.astype(BF16)
        stage[slot, :, 3 * AW:3 * AW + D] = dg_attn.astype(BF16)
        stage[slot, :, 3 * AW + D:3 * AW + 2 * D] = dg_pool.astype(BF16)
        staged(i, slot).start()

        @pl.when(i == nt - 1)
        def _():
            staged(i - 1, 1 - slot).wait()
            staged(i, slot).wait()
            pltpu.sync_copy(awo, dwo_hbm)
            for k in range(N_SHARD):
                pltpu.sync_copy(awab.at[:, pl.ds(k * sw, sw)], dwab_hbm.at[k])
                pltpu.sync_copy(awpb.at[:, pl.ds(k * sw, sw)], dwpb_hbm.at[k])

    row = pl.BlockSpec((tm, D), lambda i: (i, 0))
    vec = pl.BlockSpec((1, D), lambda i: (0, 0))
    row512 = pl.BlockSpec((tm, AW), lambda i: (i, 0))
    outs = pl.pallas_call(
        body, name="mix_step", grid=(nt,),
        in_specs=_mix_specs(tm) + [row, row, pl.BlockSpec((D, D), lambda i: (0, 0)),
                                   pl.BlockSpec((1, 3 * D), lambda i: (0, 0)), vec],
        out_specs=[row, pl.BlockSpec((8, 128), lambda i: (0, 0)), vec, vec,
                   row512, pl.BlockSpec((tm, N_SLAB * STAT_W), lambda i: (i, 0)), row512, ANY, ANY, ANY, ANY,
                   pl.BlockSpec((4, PGW, PGW), lambda i: (0, 0, 0)), pl.BlockSpec((1, AW), lambda i: (0, 0))],
        out_shape=[_sds((S, D)), _sds((8, 128)), _sds((1, D)), _sds((1, D)),
                   _sds((S, AW)), _sds((S, N_SLAB * STAT_W)), _sds((S, AW)), _sds((S, IN_W), BF16),
                   _sds((D, D)), _sds((N_SHARD, AW, sw)), _sds((N_SHARD, AW, sw)), _sds((4, PGW, PGW)), _sds((1, AW))],
        scratch_shapes=[pltpu.VMEM((D, D), F32), pltpu.VMEM((AW, D), F32), pltpu.VMEM((AW, D), F32),
                        pltpu.VMEM((2, tm, REST_W), BF16), _dma_sems(2)],
        compiler_params=_params("arbitrary"),
    )(*os_, *ls_, rest, rest, wab, wpb, pool_w, pool_scale, x, target, wout, mod, final_g)
    dx2, loss, dfg, dgate, dattn, stats, dpooled, dproj, dwo, dwab, dwpb, dpw, dps = outs
    return (dx2, loss, dfg, dgate, dattn, stats, dpooled, dproj, dwo.reshape(N_SHARD, D // N_SHARD, D), dwab, dwpb,
            dpw, dps)


def _pool_bwd(dpooled):
    S = dpooled.shape[0]
    tm = 512
    nt = S // tm

    def body(dp_ref, nxt_ref, du_ref):
        i = pl.program_id(0)
        t = i * tm + lax.broadcasted_iota(jnp.int32, (tm + HALO, 1), 0)
        nxt = jnp.where(i < nt - 1, nxt_ref[...], 0.0)
        ext = jnp.concatenate([dp_ref[...], nxt], axis=0)
        for gi, win in enumerate(POOL_WINDOWS):
            cs = slice(gi * PGW, (gi + 1) * PGW)
            s = ext[:, cs] / jnp.minimum(t + 1, win).astype(F32)
            sh = 1
            while sh < win:
                s = s + pltpu.roll(s, tm + HALO - sh, 0)
                sh *= 2
            du_ref[:, cs] = (s[:tm] - dp_ref[:, cs]).astype(BF16)

    return pl.pallas_call(
        body, name="pool_bwd", grid=(nt,),
        in_specs=[pl.BlockSpec((tm, AW), lambda i: (i, 0)),
                  pl.BlockSpec((HALO, AW), lambda i: (jnp.minimum((i + 1) * (tm // HALO), S // HALO - 1), 0))],
        out_specs=pl.BlockSpec((tm, AW), lambda i: (i, 0)),
        out_shape=_sds((S, AW), BF16), compiler_params=_params("parallel"),
    )(dpooled, dpooled)


TB = 1024


def _dh(dproj, wg_in, ride):
    S = dproj.shape[0]
    per = wg_in.shape[2] // TB
    nm, nk = S // TB, IN_W // TB

    def body(dp_ref, w_ref, out_ref):
        @pl.when(pl.program_id(1) == 0)
        def _():
            out_ref[...] = jnp.zeros_like(out_ref)

        out_ref[...] += _dot_nt(dp_ref[...], w_ref[...])

    (dh,), rode = _call_with_ride(
        body, ride, lambda: (pl.program_id(0) == 0) & (pl.program_id(1) == 0),
        lambda: (pl.program_id(0) == nm - 1) & (pl.program_id(1) == nk - 1),
        name="dh", grid=(nm, nk),
        in_specs=[pl.BlockSpec((TB, TB), lambda m, kk: (m, kk)),
                  pl.BlockSpec((None, D, TB), lambda m, kk: (kk // per, 0, kk % per))],
        out_specs=[pl.BlockSpec((TB, D), lambda m, kk: (m, 0))],
        out_shape=[_sds((S, D))], compiler_params=_params("arbitrary", "arbitrary"),
    )(dproj, wg_in)
    return dh, rode


def _dw_in(h, dproj):
    S = dproj.shape[0]
    per = IN_W // N_SHARD // TB

    def body(h_ref, dp_ref, out_ref):
        out_ref[...] = _dot_tn(h_ref[...], dp_ref[...])

    return pl.pallas_call(
        body, name="dw_in", grid=(IN_W // TB,),
        in_specs=[pl.BlockSpec((S, D), lambda j: (0, 0)), pl.BlockSpec((S, TB), lambda j: (0, j))],
        out_specs=pl.BlockSpec((None, D, TB), lambda j: (j // per, 0, j % per)),
        out_shape=_sds((N_SHARD, D, IN_W // N_SHARD)), compiler_params=_params("parallel"),
    )(h, dproj)


def _prenorm_bwd(x, dh, dx2, norm_g, mod):
    S = x.shape[0]
    tm = 512

    def body(x_ref, dh_ref, dx2_ref, g_ref, mod_ref, gx_ref, dg_ref, dshift_ref, dscale_ref):
        i = pl.program_id(0)

        @pl.when(i == 0)
        def _():
            dg_ref[...] = jnp.zeros_like(dg_ref)
            dshift_ref[...] = jnp.zeros_like(dshift_ref)
            dscale_ref[...] = jnp.zeros_like(dscale_ref)

        xv = x_ref[...]
        dhv = dh_ref[...]
        g = g_ref[...]
        r = lax.rsqrt(jnp.mean(xv * xv, axis=-1, keepdims=True) + EPS)
        xh = xv * r
        dshift_ref[...] += jnp.sum(dhv, axis=0, keepdims=True)
        dscale_ref[...] += jnp.sum(dhv * (xh * g), axis=0, keepdims=True)
        dn1 = dhv * (1.0 + mod_ref[:, D:2 * D])
        dg_ref[...] += jnp.sum(dn1 * xh, axis=0, keepdims=True)
        dxh = dn1 * g
        gx_ref[...] = dx2_ref[...] + r * (dxh - xh * jnp.mean(dxh * xh, axis=-1, keepdims=True))

    row = pl.BlockSpec((tm, D), lambda i: (i, 0))
    vec = pl.BlockSpec((1, D), lambda i: (0, 0))
    return pl.pallas_call(
        body, name="prenorm_bwd", grid=(S // tm,),
        in_specs=[row, row, row, vec, pl.BlockSpec((1, 3 * D), lambda i: (0, 0))],
        out_specs=[row, vec, vec, vec],
        out_shape=[_sds((S, D)), _sds((1, D)), _sds((1, D)), _sds((1, D))],
        compiler_params=_params("arbitrary"),
    )(x, dh, dx2, norm_g, mod)


def _local_step(x, target, mod, wg_in, wab, wpb, wout, pool_w, pool_scale, rel_bias, norm_g, final_g, chip_half):
    buckets = jnp.asarray(_bucket_tables())
    bias_tab = _bias_table(rel_bias, buckets)
    h = _prenorm(x, norm_g, mod)
    qkv = [_proj(h, wg_in, 3 * g, 3, BF16 if GROUPS[g][1] == 1 else F32, f"proj_qkv{g}") for g in range(NG)]
    rest = _proj(h, wg_in, NCB_QKV, REST_W // CB, F32, "proj_rest")
    os_, ls_ = zip(*[_attn_fwd(qkv[g], bias_tab, g) for g in range(NG)])
    (dx2, loss, dfinal_g, dgate, dattn, stats, dpooled, dproj, dw_out, dw_ab, dw_pb, dpool_w,
     dpool_scale) = _mix_step(x, target, os_, ls_, rest, wab, wpb, pool_w, pool_scale, wout, mod, final_g)
    du = _pool_bwd(dpooled)

    small = [dw_ab, dw_pb, dw_out]
    dqkv0, ds0, sib_small = _attn_bwd(qkv[0], dattn, stats, bias_tab, 0, _ride_sibling_halves(small))
    p_small = _pair_sum_small(small, sib_small, chip_half)
    dqkv1, ds1, u_small = _attn_bwd(qkv[1], dattn, stats, bias_tab, 1,
                                    _ride_chip_exchange([p16 for _, p16 in p_small]))
    rs_ab, rs_pb, rs_out = _chip_sum_small([p32 for p32, _ in p_small], u_small, chip_half)
    dqkv2, ds2, _ = _attn_bwd(qkv[2], dattn, stats, bias_tab, 2, None)

    for j, piece in enumerate(dqkv0 + dqkv1 + dqkv2):
        dproj = lax.dynamic_update_slice(dproj, piece.astype(BF16), (0, j * AW))
    dproj = lax.dynamic_update_slice(dproj, du, (0, QKV_W + AW))
    dw_in = _dw_in(h, dproj)
    drel_rows, (sib_in,) = _bias_grad(jnp.concatenate([ds0, ds1, ds2], axis=0), buckets,
                                      _ride_sibling_halves([dw_in]))
    drel = drel_rows[:, 0, :NUM_BUCKETS].T
    p32_in, p16_in = _pair_sum(dw_in, sib_in, chip_half, "rs_pair_sum_in")
    dh, (u_in,) = _dh(dproj, wg_in, _ride_chip_exchange([p16_in]))
    rs_in = _chip_sum(p32_in, u_in, chip_half, "rs_chip_sum_in")

    grad_x, dnorm_g, dshift, dscale = _prenorm_bwd(x, dh, dx2, norm_g, mod)
    dmod = jnp.concatenate([dshift, dscale, dgate], axis=1)
    return dict(loss=loss[0, 0], grad_x=grad_x, dmod=dmod, dnorm_g=dnorm_g, dfinal_g=dfinal_g, dpool_w=dpool_w,
                dpool_scale=dpool_scale, drel_bias=drel, dw_in=dw_in, dw_attn_br=dw_ab, dw_pool_br=dw_pb,
                dw_out=dw_out, rs_in=rs_in, rs_attn_br=rs_ab, rs_pool_br=rs_pb, rs_out=rs_out)


def _allgather8(blocks, name, relay=None):
    nb = len(blocks)
    relay = [False] * nb if relay is None else list(relay)

    def body(*refs):
        ins, outs = refs[:nb], refs[nb:2 * nb]
        send_sems, recv_sems = refs[2 * nb:]
        x, y, c = lax.axis_index("x"), lax.axis_index("y"), lax.axis_index("c")
        me, sibling = (x, y, c), (x, y, 1 - c)
        here, xn, yn, dg = (x, y), (1 - x, y), (x, 1 - y), (1 - x, 1 - y)

        def slot(a, chip, core, half=None):
            ref = outs[a].at[4 * chip[0] + 2 * chip[1] + core]
            if half is None:
                return ref
            r2 = ref.shape[0] // 2
            return ref.at[pl.ds(half * r2, r2)]

        def copy(a, k, dst, to, src=None):
            return pltpu.make_async_remote_copy(src_ref=dst if src is None else src, dst_ref=dst,
                                                send_sem=send_sems.at[a, k], recv_sem=recv_sems.at[a, k],
                                                device_id=to, device_id_type=MESH)

        def start(cps):
            for cp in cps:
                cp.start()
            return cps

        sent = []
        for a in range(nb):
            own = slot(a, here, c)
            sent += [copy(a, 0, own, sibling, src=ins[a]), copy(a, 1, own, (*xn, c), src=ins[a]),
                     copy(a, 2, own, (*yn, c), src=ins[a])]
            if not relay[a]:
                sent.append(copy(a, 3, own, (*dg, c), src=ins[a]))
        start(sent)
        for a in range(nb):
            copy(a, 2, slot(a, yn, c), me).wait_recv()
            sent += start([copy(a, 6, slot(a, yn, c), sibling)]
                          + ([copy(a, 3, slot(a, yn, c, 0), (*xn, c))] if relay[a] else []))
        for a in range(nb):
            copy(a, 1, slot(a, xn, c), me).wait_recv()
            sent += start([copy(a, 5, slot(a, xn, c), sibling)]
                          + ([copy(a, 4, slot(a, xn, c, 1), (*yn, c))] if relay[a] else []))
        for a in range(nb):
            for k, half in ((3, 0), (4, 1)) if relay[a] else ((3, None),):
                copy(a, k, slot(a, dg, c, half), me).wait_recv()
                sent += start([copy(a, 4 + k, slot(a, dg, c, half), sibling)])
        for a in range(nb):
            copy(a, 0, slot(a, here, 1 - c), me).wait_recv()
            copy(a, 5, slot(a, xn, 1 - c), me).wait_recv()
            copy(a, 6, slot(a, yn, 1 - c), me).wait_recv()
            for k, half in ((7, 0), (8, 1)) if relay[a] else ((7, None),):
                copy(a, k, slot(a, dg, 1 - c, half), me).wait_recv()
        for cp in sent:
            cp.wait_send()

    outs = pl.pallas_call(
        body, name=name, in_specs=[ANY] * nb, out_specs=[ANY] * nb,
        out_shape=[_sds((8,) + b.shape, b.dtype) for b in blocks],
        scratch_shapes=[_dma_sems(nb, 9), _dma_sems(nb, 9)],
    )(*blocks)
    return [_place_own(buf, b) for buf, b in zip(outs, blocks)]


def _place_own(buf, block):
    dev = 4 * lax.axis_index("x") + 2 * lax.axis_index("y") + lax.axis_index("c")
    return lax.dynamic_update_index_in_dim(buf, block, dev, 0)


def _ride_sibling_halves(gs):
    def copies(ins, outs, send_sems, recv_sems):
        x, y, c = lax.axis_index("x"), lax.axis_index("y"), lax.axis_index("c")
        cps = []
        for a in range(len(gs)):
            r2 = ins[a].shape[1] // 2
            other = ins[a].at[:, pl.ds((1 - c) * r2, r2), :]
            cps.append(pltpu.make_async_remote_copy(src_ref=other, dst_ref=outs[a], send_sem=send_sems.at[a],
                                                    recv_sem=recv_sems.at[a], device_id=(x, y, 1 - c),
                                                    device_id_type=MESH))
        return cps

    return _Ride(gs, [_sds((g.shape[0], g.shape[1] // 2, g.shape[2]), g.dtype) for g in gs], len(gs), copies)


def _pair_sum(g, t, chip_half, name):
    nsh, rows, cols = g.shape
    r2 = rows // 2
    tr = _row_tile(r2, cols)
    nt = r2 // tr

    def body(ch_ref, g_ref, t_ref, p32_ref, p16_ref):
        p = g_ref[...] + t_ref[...]
        p16_ref[...] = p.astype(BF16)

        @pl.when(pl.program_id(1) == ch_ref[0])
        def _():
            p32_ref[...] = p

    blk = pl.BlockSpec((None, tr, cols), lambda i, k, ch_ref: (k, i, 0))
    return pl.pallas_call(
        body, name=name,
        grid_spec=pltpu.PrefetchScalarGridSpec(
            num_scalar_prefetch=1, grid=(nt, nsh),
            in_specs=[pl.BlockSpec((None, tr, cols), lambda i, k, ch_ref: (k, ch_ref[1] * nt + i, 0)), blk],
            out_specs=[pl.BlockSpec((tr, cols), lambda i, k, ch_ref: (i, 0)), blk]),
        out_shape=[_sds((r2, cols)), _sds((nsh, r2, cols), BF16)],
        compiler_params=_params("parallel", "arbitrary"),
    )(chip_half, g, t)


def _pair_sum_small(gs, ts, chip_half):
    na = len(gs)

    def body(ch_ref, *refs):
        g_refs, t_refs, outs = refs[:na], refs[na:2 * na], refs[2 * na:]
        for a in range(na):
            r2 = t_refs[a].shape[1]
            own = pl.ds(pl.multiple_of(ch_ref[1] * r2, 8), r2)
            outs[2 * a + 1][...] = (g_refs[a][:, own, :] + t_refs[a][...]).astype(BF16)
            outs[2 * a][...] = g_refs[a][ch_ref[0], own, :] + t_refs[a][ch_ref[0]]

    res = pl.pallas_call(
        body, name="rs_pair_sum_small",
        in_specs=[pl.BlockSpec(memory_space=pltpu.SMEM)] + [pl.BlockSpec(memory_space=pltpu.VMEM)] * (2 * na),
        out_shape=[s for t in ts for s in (_sds(t.shape[1:]), _sds(t.shape, BF16))], compiler_params=_params(),
    )(chip_half, *gs, *ts)
    return [(res[2 * a], res[2 * a + 1]) for a in range(na)]


def _chip_sum_small(p32s, us, chip_half):
    na = len(p32s)

    def body(ch_ref, *refs):
        p_refs, u_refs, outs = refs[:na], refs[na:2 * na], refs[2 * na:]
        for a in range(na):
            r2 = p_refs[a].shape[0]
            acc = p_refs[a][...]
            for j in range(3):
                acc = acc + u_refs[a][j].astype(F32)
            outs[a][pl.ds(pl.multiple_of(ch_ref[1] * r2, 8), r2), :] = acc

    return pl.pallas_call(
        body, name="rs_chip_sum_small",
        in_specs=[pl.BlockSpec(memory_space=pltpu.SMEM)] + [pl.BlockSpec(memory_space=pltpu.VMEM)] * (2 * na),
        out_shape=[_sds((2 * p.shape[0], p.shape[1])) for p in p32s], compiler_params=_params(),
    )(chip_half, *p32s, *us)


def _ride_chip_exchange(ps):
    def copies(ins, outs, send_sems, recv_sems):
        x, y, c = lax.axis_index("x"), lax.axis_index("y"), lax.axis_index("c")
        chips = [(1 - x, y), (x, 1 - y), (1 - x, 1 - y)]
        cps = []
        for a in range(len(ps)):
            for j, (ox, oy) in enumerate(chips):
                cps.append(pltpu.make_async_remote_copy(src_ref=ins[a].at[2 * ox + oy], dst_ref=outs[a].at[j],
                                                        send_sem=send_sems.at[3 * a + j],
                                                        recv_sem=recv_sems.at[3 * a + j],
                                                        device_id=(ox, oy, c), device_id_type=MESH))
        return cps

    return _Ride(ps, [_sds((3,) + p.shape[1:], p.dtype) for p in ps], 3 * len(ps), copies)


def _chip_sum(p32, u, chip_half, name):
    r2, cols = p32.shape
    tr = _row_tile(r2, cols)
    nt = r2 // tr

    def body(ch_ref, p_ref, u_ref, o_ref):
        acc = p_ref[...]
        for j in range(3):
            acc = acc + u_ref[j].astype(F32)
        o_ref[...] = acc

    return pl.pallas_call(
        body, name=name,
        grid_spec=pltpu.PrefetchScalarGridSpec(
            num_scalar_prefetch=1, grid=(nt,),
            in_specs=[pl.BlockSpec((tr, cols), lambda i, ch_ref: (i, 0)),
                      pl.BlockSpec((3, tr, cols), lambda i, ch_ref: (0, i, 0))],
            out_specs=pl.BlockSpec((tr, cols), lambda i, ch_ref: (ch_ref[1] * nt + i, 0))),
        out_shape=_sds((2 * r2, cols)), compiler_params=_params("parallel"),
    )(chip_half, p32, u)


def _sibling_join(fs, name):
    nb = len(fs)

    def body(*refs):
        outs = refs[nb:2 * nb]
        send_sems, recv_sems = refs[2 * nb:]
        x, y, c = lax.axis_index("x"), lax.axis_index("y"), lax.axis_index("c")
        cps = []
        for a in range(nb):
            r2 = outs[a].shape[0] // 2
            rows = outs[a].at[pl.ds(c * r2, r2), :]
            cps.append(pltpu.make_async_remote_copy(src_ref=rows, dst_ref=rows, send_sem=send_sems.at[a],
                                                    recv_sem=recv_sems.at[a], device_id=(x, y, 1 - c),
                                                    device_id_type=MESH))
        for cp in cps:
            cp.start()
        for cp in cps:
            cp.wait()

    return pl.pallas_call(
        body, name=name, in_specs=[ANY] * nb, out_specs=[ANY] * nb,
        out_shape=[_sds(f.shape, f.dtype) for f in fs],
        input_output_aliases={a: a for a in range(nb)},
        scratch_shapes=[_dma_sems(nb), _dma_sems(nb)],
    )(*fs)


def _row_tile(rows, cols):
    tile = rows
    while tile * cols * 4 > (1 << 20) and tile % 16 == 0:
        tile //= 2
    return tile


def _w_ada_grad(c_all, dmod_cols):
    def body(c_ref, d_ref, o_ref):
        o_ref[...] = _dot_tn(c_ref[...].astype(BF16), d_ref[...].astype(BF16))

    return pl.pallas_call(body, name="w_ada_grad", out_shape=_sds((c_all.shape[1], dmod_cols.shape[1])),
                          compiler_params=_params())(c_all, dmod_cols)


def _adam_math(w, g, m, v):
    nm = ADAM_B1 * m + (1.0 - ADAM_B1) * g
    nv = ADAM_B2 * v + (1.0 - ADAM_B2) * (g * g)
    m_hat = nm / (1.0 - ADAM_B1 ** ADAM_STEP)
    v_hat = nv / (1.0 - ADAM_B2 ** ADAM_STEP)
    return -ADAM_LR * (m_hat / (jnp.sqrt(v_hat) + ADAM_EPS) + ADAM_WD * w), nm, nv


def _adamw(w, g, m, v, name):
    rows, cols = w.shape
    tr = _row_tile(rows, cols)

    def body(w_ref, g_ref, m_ref, v_ref, go_ref, d_ref, nm_ref, nv_ref):
        gv = g_ref[...]
        go_ref[...] = gv
        d_ref[...], nm_ref[...], nv_ref[...] = _adam_math(w_ref[...], gv, m_ref[...], v_ref[...])

    spec = pl.BlockSpec((tr, cols), lambda i: (i, 0))
    return pl.pallas_call(
        body, name=name, grid=(rows // tr,), in_specs=[spec] * 4, out_specs=[spec] * 4,
        out_shape=[_sds((rows, cols))] * 4, compiler_params=_params("parallel"),
    )(w, g, m, v)


def _pack_small(dmod, dnorm_g, dfinal_g, dpool_scale, drel_bias, loss, dpool_w):
    return jnp.concatenate([dmod.reshape(-1, 128), dnorm_g.reshape(-1, 128), dfinal_g.reshape(-1, 128),
                            jnp.pad(dpool_scale.reshape(-1, 128), ((0, PK_RELB - PK_PSCALE - AW // 128), (0, 0))),
                            jnp.pad(drel_bias, ((0, 0), (0, 128 - NG * NH))),
                            jnp.full((PK_POOLW - PK_LOSS, 128), loss, F32), dpool_w.reshape(-1, 128)], axis=0)


def _small_update(small_all, ws, ms, vs):
    lane_rows = [(r0, r0 + w.shape[1] // 128) for r0, w in zip((PK_BADA, PK_NORMG, PK_FINALG, PK_PSCALE), ws)]
    nw = len(ws)

    def body(all_ref, *refs):
        w_refs, m_refs, v_refs = refs[:nw], refs[nw:2 * nw], refs[2 * nw:3 * nw]
        loss_ref, outs = refs[3 * nw], refs[3 * nw + 1:]
        g = all_ref[0]
        for s in range(1, all_ref.shape[0]):
            g = g + all_ref[s]
        loss_ref[...] = jnp.broadcast_to(g[PK_LOSS:PK_LOSS + 1, :], loss_ref.shape)

        def put(p, at, gv):
            d, nm, nv = _adam_math(w_refs[p][at], gv, m_refs[p][at], v_refs[p][at])
            for o_ref, val in zip(outs[4 * p:4 * p + 4], (gv, d, nm, nv)):
                o_ref[at] = val

        for p, (r0, r1) in enumerate(lane_rows):
            for i in range(r1 - r0):
                put(p, (slice(None), slice(128 * i, 128 * (i + 1))), g[r0 + i:r0 + i + 1, :])
        put(4, (slice(None), slice(None)), g[PK_RELB:PK_LOSS, 0:NG * NH])
        put(5, (slice(None), slice(None)), g[PK_POOLW:PK_ROWS, :])

    res = pl.pallas_call(
        body, name="small_update",
        out_shape=[_sds((8, 128))] + [_sds(w.shape) for w in ws for _ in range(4)], compiler_params=_params(),
    )(small_all, *ws, *ms, *vs)
    return res[0], [res[1 + 4 * p:5 + 4 * p] for p in range(nw)]


def kernel(x, c, norm_g, w_ada, b_ada, w_in, pool_w, pool_scale, w_attn_br, w_pool_br, w_out, rel_bias, final_g, loss_target, m_norm_g, m_w_ada, m_b_ada, m_w_in, m_pool_w, m_pool_scale, m_w_attn_br, m_w_pool_br, m_w_out, m_rel_bias, m_final_g, v_norm_g, v_w_ada, v_b_ada, v_w_in, v_pool_w, v_pool_scale, v_w_attn_br, v_w_pool_br, v_w_out, v_rel_bias, v_final_g):
    ix, iy, ic = lax.axis_index("x"), lax.axis_index("y"), lax.axis_index("c")
    dev = 4 * ix + 2 * iy + ic
    chip = 2 * ix + iy

    def half(w):
        r2 = w.shape[0] // 2
        return lax.dynamic_slice_in_dim(w, ic * r2, r2, axis=0).astype(BF16)

    gathered = _allgather8([jnp.broadcast_to(c, (8, D)), half(w_in[0]), half(w_attn_br[0]), half(w_pool_br[0]),
                            half(w_out[0])], "gather_weights", relay=[False, True, True, True, True])
    c_all = gathered[0][:, 0, :]
    wg_in = gathered[1].reshape(N_SHARD, D, IN_W // N_SHARD)
    wab = gathered[2].reshape(N_SHARD, AW, D // N_SHARD).transpose(1, 0, 2).reshape(AW, D)
    wpb = gathered[3].reshape(N_SHARD, AW, D // N_SHARD).transpose(1, 0, 2).reshape(AW, D)
    wout = gathered[4].reshape(D, D)

    mw = 3 * D // N_SHARD
    modp = _mod_partial(c_all, w_ada[0], lax.dynamic_slice_in_dim(b_ada, chip * mw, mw, axis=1))
    mod_all = _allgather8([modp], "gather_mod")[0]
    mod_full = mod_all[::2].transpose(1, 0, 2).reshape(8, 3 * D)
    mod = lax.dynamic_slice_in_dim(mod_full, dev, 1, axis=0)

    chip_half = jnp.stack([chip, ic]).astype(jnp.int32)
    r = _local_step(x[0], loss_target[0], mod, wg_in, wab, wpb, wout, pool_w[0], pool_scale, rel_bias, norm_g,
                    final_g.reshape(1, D), chip_half)

    packed = _pack_small(r["dmod"], r["dnorm_g"], r["dfinal_g"], r["dpool_scale"], r["drel_bias"], r["loss"],
                         r["dpool_w"])
    small_all = _allgather8([packed], "gather_small")[0]
    small = ["b_ada", "norm_g", "final_g", "pool_scale", "rel_bias", "pool_w"]
    shaped = lambda b, n, f, ps, rb, pw: [b, n, f.reshape(1, D), ps, rb, pw.reshape(4 * PGW, PGW)]
    loss, small_out = _small_update(small_all, shaped(b_ada, norm_g, final_g, pool_scale, rel_bias, pool_w),
                                    shaped(m_b_ada, m_norm_g, m_final_g, m_pool_scale, m_rel_bias, m_pool_w),
                                    shaped(v_b_ada, v_norm_g, v_final_g, v_pool_scale, v_rel_bias, v_pool_w))
    dmod_all = small_all[:, PK_BADA:PK_NORMG, :].reshape(8, 3 * D)
    g_w_ada = _w_ada_grad(c_all, lax.dynamic_slice_in_dim(dmod_all, chip * mw, mw, axis=1))

    g_w_in, g_w_ab, g_w_pb, g_w_out = _sibling_join([r["rs_in"], r["rs_attn_br"], r["rs_pool_br"], r["rs_out"]],
                                                    "rs_sibling_join")
    upd = dict(zip(small, small_out))
    upd["final_g"] = [a.reshape(D) for a in upd["final_g"]]
    upd["pool_w"] = [a.reshape(1, 4, PGW, PGW) for a in upd["pool_w"]]
    for nme, w, g, m, v in (("w_ada", w_ada, g_w_ada, m_w_ada, v_w_ada), ("w_in", w_in, g_w_in, m_w_in, v_w_in),
                            ("w_attn_br", w_attn_br, g_w_ab, m_w_attn_br, v_w_attn_br),
                            ("w_pool_br", w_pool_br, g_w_pb, m_w_pool_br, v_w_pool_br),
                            ("w_out", w_out, g_w_out, m_w_out, v_w_out)):
        upd[nme] = [a[None] for a in _adamw(w[0], g, m[0], v[0], "adamw_" + nme)]
    names = ["norm_g", "w_ada", "b_ada", "w_in", "pool_w", "pool_scale", "w_attn_br", "w_pool_br", "w_out",
             "rel_bias", "final_g"]
    return (loss[0, 0], r["grad_x"][None]) + tuple(upd[nme][kind] for kind in range(4) for nme in names)
```

```python
import math

import numpy as np
import jax
import jax.numpy as jnp
from jax import lax
from jax.experimental import pallas as pl
from jax.experimental.pallas import tpu as pltpu

F32 = jnp.float32
BF16 = jnp.bfloat16

D = 1024
HD = 64
NH = 8
AW = NH * HD
GROUPS = ((128, 1), (512, 4), (2048, 16))
NG = len(GROUPS)
BLK = 128
GW = 3 * AW
QKV_W = NG * GW
REST_W = 3584
IN_W = QKV_W + REST_W
CB = 512
NCB_QKV = QKV_W // CB
POOL_WINDOWS = (2, 4, 8, 16)
PGW = 128
HALO = 16
NUM_BUCKETS = 32
MAX_DISTANCE = 2048
EPS = 1e-6
NEG = -1e30
N_SHARD = 4
VMEM_LIMIT = 56 * 1024 * 1024

ADAM_LR = 0.001
ADAM_B1 = 0.9
ADAM_B2 = 0.999
ADAM_EPS = 1e-08
ADAM_WD = 0.01
ADAM_STEP = 10

PK_BADA, PK_NORMG, PK_FINALG, PK_PSCALE, PK_RELB, PK_LOSS, PK_POOLW, PK_ROWS = 0, 24, 32, 40, 48, 80, 88, 600

ANY = pl.BlockSpec(memory_space=pl.ANY)
MESH = pl.DeviceIdType.MESH


def _params(*sem):
    return pltpu.CompilerParams(dimension_semantics=sem, vmem_limit_bytes=VMEM_LIMIT)


def _sds(shape, dtype=F32):
    return jax.ShapeDtypeStruct(shape, dtype)


def _dot(a, b):
    return jnp.dot(a, b, preferred_element_type=F32)


def _dot_nt(a, b):
    return lax.dot_general(a, b, (((1,), (1,)), ((), ())), preferred_element_type=F32)


def _dot_tn(a, b):
    return lax.dot_general(a, b, (((0,), (0,)), ((), ())), preferred_element_type=F32)


def _sigmoid(z):
    return 0.5 * jnp.tanh(0.5 * z) + 0.5


def _dma_sems(*shape):
    return pltpu.SemaphoreType.DMA(shape)


class _Ride:
    def __init__(self, arrays, out_shapes, n_copies, copies, relays=None, landed_only=()):
        self.arrays, self.out_shapes, self.n_copies, self.copies = list(arrays), list(out_shapes), n_copies, copies
        self.relays, self.landed_only = relays, tuple(landed_only)


def _call_with_ride(body, ride, first, last, *, in_specs, out_specs, out_shape, scratch_shapes=(), mid=None, **kw):
    in_specs, out_specs, out_shape, scratch_shapes = list(in_specs), list(out_specs), list(out_shape), list(scratch_shapes)
    n_in, n_out, n_sc = len(in_specs), len(out_specs), len(scratch_shapes)
    if ride is None:
        def run_plain(*operands):
            return pl.pallas_call(body, in_specs=in_specs, out_specs=out_specs, out_shape=out_shape,
                                  scratch_shapes=scratch_shapes, **kw)(*operands), []
        return run_plain
    n_ri, n_ro = len(ride.arrays), len(ride.out_shapes)

    def wrapped(*refs):
        ins, rest = refs[:n_in], refs[n_in:]
        r_ins, rest = rest[:n_ri], rest[n_ri:]
        outs, rest = rest[:n_out], rest[n_out:]
        r_outs, rest = rest[:n_ro], rest[n_ro:]
        scratch, (send_sems, recv_sems) = rest[:n_sc], rest[n_sc:]

        @pl.when(first())
        def _():
            for cp in ride.copies(r_ins, r_outs, send_sems, recv_sems):
                cp.start()

        if ride.relays is not None:
            @pl.when(mid())
            def _():
                for landed, onward in ride.relays(r_ins, r_outs, send_sems, recv_sems):
                    landed.wait_recv()
                    onward.start()

        body(*ins, *outs, *scratch)

        @pl.when(last())
        def _():
            for k, cp in enumerate(ride.copies(r_ins, r_outs, send_sems, recv_sems)):
                if k in ride.landed_only:
                    cp.wait_send()
                else:
                    cp.wait()
            if ride.relays is not None:
                for _, onward in ride.relays(r_ins, r_outs, send_sems, recv_sems):
                    onward.wait()

    def run(*operands):
        res = pl.pallas_call(
            wrapped, in_specs=in_specs + [ANY] * n_ri, out_specs=out_specs + [ANY] * n_ro,
            out_shape=out_shape + ride.out_shapes,
            scratch_shapes=scratch_shapes + [_dma_sems(ride.n_copies), _dma_sems(ride.n_copies)], **kw,
        )(*operands, *ride.arrays)
        return res[:n_out], res[n_out:]
    return run


def _bucket_tables():
    i = np.arange(BLK)[:, None]
    j = np.arange(2 * BLK)[None, :]
    dist = BLK + i - j
    valid = (dist >= 0) & (dist <= BLK)
    tabs = []
    for _, dil in GROUPS:
        n = (np.clip(dist, 0, BLK) * dil).astype(np.int32)
        max_exact = NUM_BUCKETS // 2
        nf = np.maximum(n, 1).astype(np.float32)
        large = max_exact + (np.log(nf / np.float32(max_exact)) / np.float32(math.log(MAX_DISTANCE / max_exact))
                             * np.float32(NUM_BUCKETS - max_exact)).astype(np.int32)
        large = np.minimum(large, NUM_BUCKETS - 1)
        bucket = np.where(n < max_exact, n, large)
        tab = np.where(valid, bucket, -1).astype(np.int32)
        perm = _block_perm(dil)
        tabs.append(tab[perm][:, np.concatenate([perm, BLK + perm])])
    return np.stack(tabs)


def _bias_table(rel_bias, buckets):
    def body(rb_ref, bk_ref, out_ref):
        g = pl.program_id(0)
        bk = bk_ref[...]
        for h in range(NH):
            acc = jnp.full((BLK, 2 * BLK), NEG, F32)
            for b in range(NUM_BUCKETS):
                acc = jnp.where(bk == b, rb_ref[b, g * NH + h], acc)
            out_ref[h] = acc

    return pl.pallas_call(
        body, name="bias_table", grid=(NG,),
        in_specs=[pl.BlockSpec(memory_space=pltpu.SMEM),
                  pl.BlockSpec((None, BLK, 2 * BLK), lambda g: (g, 0, 0))],
        out_specs=pl.BlockSpec((NH, BLK, 2 * BLK), lambda g: (g, 0, 0)),
        out_shape=_sds((NG * NH, BLK, 2 * BLK)),
        compiler_params=_params("arbitrary"),
    )(rel_bias, buckets)


def _bias_grad(ds_acc, buckets, ride):
    def body(acc_ref, bk_ref, out_ref):
        bk = bk_ref[...]
        acc = acc_ref[...]
        lane = lax.broadcasted_iota(jnp.int32, (8, 128), 1)
        out = jnp.zeros((8, 128), F32)
        for b in range(NUM_BUCKETS):
            val = jnp.sum(jnp.where(bk == b, acc, 0.0))
            out = jnp.where(lane == b, val, out)
        out_ref[...] = out

    (out,), rode = _call_with_ride(
        body, ride, lambda: pl.program_id(0) == 0, lambda: pl.program_id(0) == NG * NH - 1,
        name="bias_grad", grid=(NG * NH,),
        in_specs=[pl.BlockSpec((None, BLK, 2 * BLK), lambda gh: (gh, 0, 0)),
                  pl.BlockSpec((None, BLK, 2 * BLK), lambda gh: (gh // NH, 0, 0))],
        out_specs=[pl.BlockSpec((None, 8, 128), lambda gh: (gh, 0, 0))],
        out_shape=[_sds((NG * NH, 8, 128))],
        compiler_params=_params("arbitrary"),
    )(ds_acc, buckets)
    return out, rode


def _mod_partial(c_all, w_ada_s, b_ada_s):
    def body(c_ref, w_ref, b_ref, o_ref):
        o_ref[...] = _dot(c_ref[...].astype(BF16), w_ref[...].astype(BF16)) + b_ref[...]

    return pl.pallas_call(body, name="mod_partial", out_shape=_sds((8, w_ada_s.shape[1])),
                          compiler_params=_params())(c_all, w_ada_s, b_ada_s)


def _prenorm(x, norm_g, mod):
    S = x.shape[0]
    tm = 512

    def body(x_ref, g_ref, mod_ref, h_ref):
        xv = x_ref[...]
        r = lax.rsqrt(jnp.mean(xv * xv, axis=-1, keepdims=True) + EPS)
        n1 = xv * r * g_ref[...]
        h_ref[...] = (n1 * (1.0 + mod_ref[:, D:2 * D]) + mod_ref[:, 0:D]).astype(BF16)

    return pl.pallas_call(
        body, name="prenorm", grid=(S // tm,),
        in_specs=[pl.BlockSpec((tm, D), lambda i: (i, 0)), pl.BlockSpec((1, D), lambda i: (0, 0)),
                  pl.BlockSpec((1, 3 * D), lambda i: (0, 0))],
        out_specs=pl.BlockSpec((tm, D), lambda i: (i, 0)),
        out_shape=_sds((S, D), BF16), compiler_params=_params("parallel"),
    )(x, norm_g, mod)


def _proj(h, wg_in, j0, nj, dtype, name):
    S = h.shape[0]
    tm = 2048
    per = wg_in.shape[2] // CB

    def body(h_ref, w_ref, o_ref):
        o_ref[...] = _dot(h_ref[...], w_ref[...]).astype(dtype)

    return pl.pallas_call(
        body, name=name, grid=(S // tm, nj),
        in_specs=[pl.BlockSpec((tm, D), lambda m, j: (m, 0)),
                  pl.BlockSpec((None, D, CB), lambda m, j: ((j0 + j) // per, 0, (j0 + j) % per))],
        out_specs=pl.BlockSpec((tm, CB), lambda m, j: (m, j)),
        out_shape=_sds((S, nj * CB), dtype), compiler_params=_params("parallel", "parallel"),
    )(h, wg_in)


HS = 4
SLAB = HS * HD


def _lane_head(rows):
    return lax.broadcasted_iota(jnp.int32, (rows, SLAB), 1) // HD


def _head_stack(a):
    head = _lane_head(a.shape[0])
    return jnp.concatenate([jnp.where(head == h, a, jnp.zeros_like(a)) for h in range(HS)], axis=0)


def _head_unstack(a):
    rows = a.shape[0] // HS
    head = _lane_head(rows)
    out = a[:rows]
    for h in range(1, HS):
        out = jnp.where(head == h, a[h * rows:(h + 1) * rows], out)
    return out


STAT_W = 128
VIEW = 16


def _sub_layout(dil):
    if dil == 1:
        return BLK, [None]
    return BLK * dil // VIEW, [[r + dil * u for u in range(VIEW // dil)] for r in range(dil)]


def _block_perm(dil):
    a_rows, _ = _sub_layout(dil)
    p = np.arange(BLK)
    return p if dil == 1 else (VIEW // dil) * (p % a_rows) + p // a_rows


LB = 128
N_SLAB = NH // HS


def _ld(refs, bs, s, w):
    if bs is None:
        return refs[0][:, s * w:(s + 1) * w]
    a_rows = refs[0].shape[0] // VIEW
    return jnp.concatenate([jnp.concatenate([ref[pl.ds(b, a_rows, stride=VIEW), :] for b in bs], axis=0)
                            for ref in refs], axis=1)


def _st(ref, bs, s, val):
    if bs is None:
        ref[:, s * SLAB:(s + 1) * SLAB] = val.astype(ref.dtype)
        return
    a_rows = val.shape[0] // len(bs)
    for u, b in enumerate(bs):
        ref[:, b, :] = val[u * a_rows:(u + 1) * a_rows]


def _attn_views(dil, S):
    a_rows, subs = _sub_layout(dil)
    if dil == 1:
        def ispecs(base, w, f):
            return [pl.BlockSpec((BLK, N_SLAB * w), lambda sg, n: (f(n), base // (N_SLAB * w)))]
        return subs, S // BLK, N_SLAB, ispecs, (lambda w: (S, w)), (
            lambda f: pl.BlockSpec((BLK, AW), lambda sg, n: (f(n), 0)))

    def ispecs(base, w, f):
        return [pl.BlockSpec((a_rows * VIEW, LB), lambda sg, n, k=k: (f(n), (base + sg * w) // LB + k))
                for k in range(w // LB)]
    return subs, S // (a_rows * VIEW), 1, ispecs, (lambda w: (S // VIEW, VIEW, w)), (
        lambda f: pl.BlockSpec((a_rows, VIEW, SLAB), lambda sg, n: (f(n), 0, sg)))


def _attn_fwd(qkv_g, bias_tab, g):
    S = qkv_g.shape[0]
    subs, nbq, sps, ispecs, shape, ospec = _attn_views(GROUPS[g][1], S)
    cur = lambda n: n
    in_specs = [ispecs(0, SLAB, cur), ispecs(AW, SLAB, cur), ispecs(2 * AW, SLAB, cur)]
    nl = len(in_specs[0])

    def body(*refs):
        q, k, v = (refs[t * nl:(t + 1) * nl] for t in range(3))
        b_ref, o_ref, l_ref, kprev, vprev = refs[3 * nl:]
        n = pl.program_id(1)

        @pl.when(n == 0)
        def _():
            kprev[...] = jnp.zeros_like(kprev)
            vprev[...] = jnp.zeros_like(vprev)

        col = lax.broadcasted_iota(jnp.int32, (HS * BLK, 2 * BLK), 1)
        keep = (col >= BLK) | (n > 0)
        for s_ in range(sps):
            cs = slice(s_ * SLAB, (s_ + 1) * SLAB)
            bias = b_ref[pl.ds(s_ * HS, HS)].reshape(HS * BLK, 2 * BLK)
            for i, bs in enumerate(subs):
                kc, vc = _ld(k, bs, s_, SLAB).astype(BF16), _ld(v, bs, s_, SLAB).astype(BF16)
                kb = jnp.concatenate([kprev[i, :, cs], kc], axis=0)
                vb = jnp.concatenate([vprev[i, :, cs], vc], axis=0)
                kprev[i, :, cs], vprev[i, :, cs] = kc, vc
                s = _dot_nt(_head_stack(_ld(q, bs, s_, SLAB).astype(BF16)), kb) * (HD ** -0.5) + bias
                s = jnp.where(keep, s, NEG)
                m = jnp.max(s, axis=-1, keepdims=True)
                p = jnp.exp(s - m)
                den = jnp.sum(p, axis=-1, keepdims=True)
                _st(o_ref, bs, s_, _head_unstack(_dot(p.astype(BF16), vb) / den))
                _st(l_ref, bs, s_, _head_unstack(jnp.broadcast_to(m + jnp.log(den), (HS * BLK, SLAB))))

    out = _sds(shape(AW))
    nsg = N_SLAB // sps
    o, l = pl.pallas_call(
        body, name=f"attn_fwd{g}", grid=(nsg, nbq),
        in_specs=sum(in_specs, []) + [pl.BlockSpec((sps * HS, BLK, 2 * BLK), lambda sg, n: (g * nsg + sg, 0, 0))],
        out_specs=[ospec(cur), ospec(cur)],
        out_shape=[out, out],
        scratch_shapes=[pltpu.VMEM((len(subs), BLK, sps * SLAB), BF16)] * 2,
        compiler_params=_params("parallel", "arbitrary"),
    )(*([qkv_g] * (3 * nl)), bias_tab)
    return o.reshape(S, AW), l.reshape(S, AW)


def _attn_bwd(qkv_g, dattn, stats, bias_tab, g, ride):
    S = qkv_g.shape[0]
    subs, nbq, sps, ispecs, shape, ospec = _attn_views(GROUPS[g][1], S)
    cur = lambda n: jnp.minimum(n, nbq - 1)
    late = lambda n: jnp.maximum(n - 1, 0)
    in_specs = [ispecs(0, SLAB, cur), ispecs(AW, SLAB, cur), ispecs(2 * AW, SLAB, cur), ispecs(0, SLAB, cur),
                ispecs(0, STAT_W, cur)]
    nl = len(in_specs[0])

    def body(*refs):
        q, k, v, da = (refs[t * nl:(t + 1) * nl] for t in range(4))
        st_ref, b_ref, dq_ref, dk_ref, dv_ref, ds_ref, ck_ref, cv_ref, kprev, vprev = refs[4 * nl:]
        n = pl.program_id(1)

        @pl.when(n == 0)
        def _():
            for ref in (ds_ref, ck_ref, cv_ref, kprev, vprev):
                ref[...] = jnp.zeros_like(ref)

        @pl.when(n < nbq)
        def _():
            col = lax.broadcasted_iota(jnp.int32, (HS * BLK, 2 * BLK), 1)
            keep = (col >= BLK) | (n > 0)
            for s_ in range(sps):
                cs = slice(s_ * SLAB, (s_ + 1) * SLAB)
                bias = b_ref[pl.ds(s_ * HS, HS)].reshape(HS * BLK, 2 * BLK)
                for i, bs in enumerate(subs):
                    st = _ld((st_ref,), bs, s_, STAT_W)
                    kc, vc = _ld(k, bs, s_, SLAB).astype(BF16), _ld(v, bs, s_, SLAB).astype(BF16)
                    kb = jnp.concatenate([kprev[i, :, cs], kc], axis=0)
                    vb = jnp.concatenate([vprev[i, :, cs], vc], axis=0)
                    kprev[i, :, cs], vprev[i, :, cs] = kc, vc
                    lse =jnp.concatenate([st[:, h:h + 1] for h in range(HS)], axis=0)
                    delta = jnp.concatenate([st[:, HS + h:HS + h + 1] for h in range(HS)], axis=0)
                    qs = _head_stack(_ld(q, bs, s_, SLAB).astype(BF16))
                    dos = _head_stack(_ld(da, bs, s_, SLAB).astype(BF16))
                    s = _dot_nt(qs, kb) * (HD ** -0.5) + bias
                    s = jnp.where(keep, s, NEG)
                    p = jnp.exp(s - lse)
                    ds = p * (_dot_nt(dos, vb) - delta)
                    ds_ref[pl.ds(s_ * HS, HS)] += ds.reshape(HS, BLK, 2 * BLK)
                    ds_b = (ds * (HD ** -0.5)).astype(BF16)
                    _st(dq_ref, bs, s_, _head_unstack(_dot(ds_b, kb)))
                    dkb = _dot_tn(ds_b, qs)
                    dvb = _dot_tn(p.astype(BF16), dos)
                    _st(dk_ref, bs, s_, ck_ref[i, :, cs] + dkb[:BLK])
                    _st(dv_ref, bs, s_, cv_ref[i, :, cs] + dvb[:BLK])
                    ck_ref[i, :, cs] = dkb[BLK:]
                    cv_ref[i, :, cs] = dvb[BLK:]

        @pl.when(n == nbq)
        def _():
            for s_ in range(sps):
                for i, bs in enumerate(subs):
                    _st(dk_ref, bs, s_, ck_ref[i, :, s_ * SLAB:(s_ + 1) * SLAB])
                    _st(dv_ref, bs, s_, cv_ref[i, :, s_ * SLAB:(s_ + 1) * SLAB])

    out = _sds(shape(AW), BF16 if GROUPS[g][1] == 1 else F32)
    nsg = N_SLAB // sps
    (dq, dk, dv, ds_acc), rode = _call_with_ride(
        body, ride, lambda: (pl.program_id(0) == 0) & (pl.program_id(1) == 0),
        lambda: (pl.program_id(0) == nsg - 1) & (pl.program_id(1) == nbq),
        name=f"attn_bwd{g}", grid=(nsg, nbq + 1),
        in_specs=sum(in_specs, []) + [pl.BlockSpec((sps * HS, BLK, 2 * BLK), lambda sg, n: (g * nsg + sg, 0, 0))],
        out_specs=[ospec(cur), ospec(late), ospec(late),
                   pl.BlockSpec((sps * HS, BLK, 2 * BLK), lambda sg, n: (sg, 0, 0))],
        out_shape=[out] * 3 + [_sds((NH, BLK, 2 * BLK))],
        scratch_shapes=[pltpu.VMEM((len(subs), BLK, sps * SLAB), F32)] * 2
        + [pltpu.VMEM((len(subs), BLK, sps * SLAB), BF16)] * 2,
        compiler_params=_params("arbitrary", "arbitrary"),
    )(*([qkv_g] * (3 * nl)), *([dattn] * nl), stats, bias_tab)
    return [dq.reshape(S, AW), dk.reshape(S, AW), dv.reshape(S, AW)], ds_acc, rode


TM_MIX = 256


def _mix_specs(tm):
    row512 = pl.BlockSpec((tm, AW), lambda i: (i, 0))
    return ([row512] * 6 + [
        pl.BlockSpec((tm, REST_W), lambda i: (i, 0)),
        pl.BlockSpec((HALO, AW), lambda i: (jnp.maximum(i * (tm // HALO) - 1, 0), 1)),
        pl.BlockSpec((AW, D), lambda i: (0, 0)), pl.BlockSpec((AW, D), lambda i: (0, 0)),
        pl.BlockSpec((4, PGW, PGW), lambda i: (0, 0, 0)), pl.BlockSpec((1, AW), lambda i: (0, 0))])


def _mix_forward(i, tm, o_refs, l_refs, rest_ref, halo_ref, wab_ref, wpb_ref, pw_ref, ps_ref):
    l0, l1, l2 = (r[...] for r in l_refs)
    mx = jnp.maximum(jnp.maximum(l0, l1), l2)
    e0, e1, e2 = jnp.exp(l0 - mx), jnp.exp(l1 - mx), jnp.exp(l2 - mx)
    den = e0 + e1 + e2
    lj = mx + jnp.log(den)
    attn = (e0 * o_refs[0][...] + e1 * o_refs[1][...] + e2 * o_refs[2][...]) / den

    z_attn = rest_ref[:, 0:AW]
    u = rest_ref[:, AW:2 * AW]
    z_pool = rest_ref[:, 2 * AW:3 * AW]
    g_attn = rest_ref[:, 3 * AW:3 * AW + D]
    g_pool = rest_ref[:, 3 * AW + D:3 * AW + 2 * D]

    sg_a = _sigmoid(z_attn)
    sil_a = z_attn * sg_a
    a_g = (attn * sil_a).astype(BF16)
    y_attn = _dot(a_g, wab_ref[...])

    halo = jnp.where(i > 0, halo_ref[...], 0.0)
    ext = jnp.concatenate([halo, u], axis=0)
    t = i * tm + lax.broadcasted_iota(jnp.int32, (tm, 1), 0)
    pooled, mixed_raw = [], []
    for gi, win in enumerate(POOL_WINDOWS):
        s = ext[:, gi * PGW:(gi + 1) * PGW]
        sh = 1
        while sh < win:
            s = s + pltpu.roll(s, sh, 0)
            sh *= 2
        cnt = jnp.minimum(t + 1, win).astype(F32)
        pg = s[HALO:] / cnt - u[:, gi * PGW:(gi + 1) * PGW]
        pooled.append(pg.astype(BF16))
        mixed_raw.append(_dot(pooled[-1], pw_ref[gi].astype(BF16)))
    mixed_raw = jnp.concatenate(mixed_raw, axis=1)
    mixed = mixed_raw * ps_ref[...]
    sg_p = _sigmoid(z_pool)
    sil_p = z_pool * sg_p
    m_g = (mixed * sil_p).astype(BF16)
    y_pool = _dot(m_g, wpb_ref[...])

    sa = _sigmoid(g_attn)
    sp = _sigmoid(g_pool)
    merged = sa * y_attn + sp * y_pool
    return dict(lj=lj, attn=attn, z_attn=z_attn, z_pool=z_pool, sg_a=sg_a, sil_a=sil_a, a_g=a_g, y_attn=y_attn,
                pooled=pooled, mixed_raw=mixed_raw, mixed=mixed, sg_p=sg_p, sil_p=sil_p, m_g=m_g, y_pool=y_pool,
                sa=sa, sp=sp, merged=merged)


def _mix_step(x, target, os_, ls_, rest, wab, wpb, pool_w, pool_scale, wout, mod, final_g):
    S = x.shape[0]
    tm = TM_MIX
    nt = S // tm
    sw = D // N_SHARD

    def body(o0, o1, o2, l0, l1, l2, rest_ref, halo_ref, wab_ref, wpb_ref, pw_ref, ps_ref,
             x_ref, t_ref, wo_ref, mod_ref, fg_ref, dx2_ref, loss_ref, dfg_ref, dgate_ref,
             dattn_ref, stats_ref, dpooled_ref, dproj_hbm, dwo_hbm, dwab_hbm, dwpb_hbm, dpw_ref, dps_ref,
             awo, awab, awpb, stage, stage_sem):
        i = pl.program_id(0)
        slot = i % 2

        def staged(step, sl):
            return pltpu.make_async_copy(stage.at[sl], dproj_hbm.at[pl.ds(step * tm, tm), pl.ds(QKV_W, REST_W)],
                                         stage_sem.at[sl])

        @pl.when(i == 0)
        def _():
            for ref in (loss_ref, dfg_ref, dgate_ref, awo, awab, awpb, dpw_ref, dps_ref):
                ref[...] = jnp.zeros_like(ref)

        f = _mix_forward(i, tm, (o0, o1, o2), (l0, l1, l2), rest_ref, halo_ref, wab_ref, wpb_ref, pw_ref, ps_ref)
        mo = _dot(f["merged"].astype(BF16), wo_ref[...])
        gate = mod_ref[:, 2 * D:3 * D]
        fg = fg_ref[...]
        x2 = x_ref[...] + gate * mo
        r2 = lax.rsqrt(jnp.mean(x2 * x2, axis=-1, keepdims=True) + EPS)
        n2 = x2 * r2
        err = n2 * fg - t_ref[...]
        loss_ref[...] += 0.5 * jnp.sum(jnp.mean(err * err, axis=-1, keepdims=True))
        dy = err * (1.0 / D)
        dfg_ref[...] += jnp.sum(dy * n2, axis=0, keepdims=True)
        dn = dy * fg
        dx2 = r2 * (dn - n2 * jnp.mean(dn * n2, axis=-1, keepdims=True))
        dgate_ref[...] += jnp.sum(dx2 * mo, axis=0, keepdims=True)
        dx2_ref[...] = dx2

        dmo_b = (dx2 * gate).astype(BF16)
        dmerged = _dot_nt(dmo_b, wo_ref[...])
        awo[...] += _dot_tn(f["merged"].astype(BF16), dmo_b)
        sa, sp = f["sa"], f["sp"]
        dya = (dmerged * sa).astype(BF16)
        dyp = (dmerged * sp).astype(BF16)
        dg_attn = dmerged * f["y_attn"] * sa * (1.0 - sa)
        dg_pool = dmerged * f["y_pool"] * sp * (1.0 - sp)
        dag = _dot_nt(dya, wab_ref[...])
        awab[...] += _dot_tn(f["a_g"], dya)
        dmg = _dot_nt(dyp, wpb_ref[...])
        awpb[...] += _dot_tn(f["m_g"], dyp)
        dattn = dag * f["sil_a"]
        dattn_ref[...] = dattn
        prod = dattn * f["attn"]
        lane = lax.broadcasted_iota(jnp.int32, (tm, STAT_W), 1)
        for sb in range(N_SLAB):
            st = jnp.zeros((tm, STAT_W), F32)
            for h in range(HS):
                hs = slice((sb * HS + h) * HD, (sb * HS + h + 1) * HD)
                st = jnp.where(lane == h, f["lj"][:, hs.start:hs.start + 1], st)
                st = jnp.where(lane == HS + h, jnp.sum(prod[:, hs], axis=-1, keepdims=True), st)
            stats_ref[:, sb * STAT_W:(sb + 1) * STAT_W] = st
        dz_attn = dag * f["attn"] * (f["sg_a"] * (1.0 + f["z_attn"] * (1.0 - f["sg_a"])))
        dmixed = dmg * f["sil_p"]
        dz_pool = dmg * f["mixed"] * (f["sg_p"] * (1.0 + f["z_pool"] * (1.0 - f["sg_p"])))
        dps_ref[...] += jnp.sum(dmixed * f["mixed_raw"], axis=0, keepdims=True)
        dpm = (dmixed * ps_ref[...]).astype(BF16)
        for gi in range(len(POOL_WINDOWS)):
            cs = slice(gi * PGW, (gi + 1) * PGW)
            dpw_ref[gi] += _dot_tn(f["pooled"][gi], dpm[:, cs])
            dpooled_ref[:, cs] = _dot_nt(dpm[:, cs], pw_ref[gi].astype(BF16))
        @pl.when(i >= 2)
        def _():
            staged(i - 2, slot).wait()

        stage[slot, :, 0:AW] = dz_attn.astype(BF16)
        stage[slot, :, AW:2 * AW] = jnp.zeros((tm, AW), BF16)
        stage[slot, :, 2 * AW:3 * AW] = dz_pool.astype(BF16)
        stage[slot, :, 3 * AW:3 * AW + D] = dg_attn.astype(BF16)
        stage[slot, :, 3 * AW + D:3 * AW + 2 * D] = dg_pool.astype(BF16)
        staged(i, slot).start()

        @pl.when(i == nt - 1)
        def _():
            staged(i - 1, 1 - slot).wait()
            staged(i, slot).wait()
            pltpu.sync_copy(awo, dwo_hbm)
            for k in range(N_SHARD):
                pltpu.sync_copy(awab.at[:, pl.ds(k * sw, sw)], dwab_hbm.at[k])
                pltpu.sync_copy(awpb.at[:, pl.ds(k * sw, sw)], dwpb_hbm.at[k])

    row = pl.BlockSpec((tm, D), lambda i: (i, 0))
    vec = pl.BlockSpec((1, D), lambda i: (0, 0))
    row512 = pl.BlockSpec((tm, AW), lambda i: (i, 0))
    outs = pl.pallas_call(
        body, name="mix_step", grid=(nt,),
        in_specs=_mix_specs(tm) + [row, row, pl.BlockSpec((D, D), lambda i: (0, 0)),
                                   pl.BlockSpec((1, 3 * D), lambda i: (0, 0)), vec],
        out_specs=[row, pl.BlockSpec((8, 128), lambda i: (0, 0)), vec, vec,
                   row512, pl.BlockSpec((tm, N_SLAB * STAT_W), lambda i: (i, 0)), row512, ANY, ANY, ANY, ANY,
                   pl.BlockSpec((4, PGW, PGW), lambda i: (0, 0, 0)), pl.BlockSpec((1, AW), lambda i: (0, 0))],
        out_shape=[_sds((S, D)), _sds((8, 128)), _sds((1, D)), _sds((1, D)),
                   _sds((S, AW)), _sds((S, N_SLAB * STAT_W)), _sds((S, AW)), _sds((S, IN_W), BF16),
                   _sds((D, D)), _sds((N_SHARD, AW, sw)), _sds((N_SHARD, AW, sw)), _sds((4, PGW, PGW)), _sds((1, AW))],
        scratch_shapes=[pltpu.VMEM((D, D), F32), pltpu.VMEM((AW, D), F32), pltpu.VMEM((AW, D), F32),
                        pltpu.VMEM((2, tm, REST_W), BF16), _dma_sems(2)],
        compiler_params=_params("arbitrary"),
    )(*os_, *ls_, rest, rest, wab, wpb, pool_w, pool_scale, x, target, wout, mod, final_g)
    dx2, loss, dfg, dgate, dattn, stats, dpooled, dproj, dwo, dwab, dwpb, dpw, dps = outs
    return (dx2, loss, dfg, dgate, dattn, stats, dpooled, dproj, dwo.reshape(N_SHARD, D // N_SHARD, D), dwab, dwpb,
            dpw, dps)


def _pool_bwd(dpooled):
    S = dpooled.shape[0]
    tm = 512
    nt = S // tm

    def body(dp_ref, nxt_ref, du_ref):
        i = pl.program_id(0)
        t = i * tm + lax.broadcasted_iota(jnp.int32, (tm + HALO, 1), 0)
        nxt = jnp.where(i < nt - 1, nxt_ref[...], 0.0)
        ext = jnp.concatenate([dp_ref[...], nxt], axis=0)
        for gi, win in enumerate(POOL_WINDOWS):
            cs = slice(gi * PGW, (gi + 1) * PGW)
            s = ext[:, cs] / jnp.minimum(t + 1, win).astype(F32)
            sh = 1
            while sh < win:
                s = s + pltpu.roll(s, tm + HALO - sh, 0)
                sh *= 2
            du_ref[:, cs] = (s[:tm] - dp_ref[:, cs]).astype(BF16)

    return pl.pallas_call(
        body, name="pool_bwd", grid=(nt,),
        in_specs=[pl.BlockSpec((tm, AW), lambda i: (i, 0)),
                  pl.BlockSpec((HALO, AW), lambda i: (jnp.minimum((i + 1) * (tm // HALO), S // HALO - 1), 0))],
        out_specs=pl.BlockSpec((tm, AW), lambda i: (i, 0)),
        out_shape=_sds((S, AW), BF16), compiler_params=_params("parallel"),
    )(dpooled, dpooled)


TB = 1024


def _dh(dproj, wg_in, ride):
    S = dproj.shape[0]
    sw = wg_in.shape[2]
    spk = 2
    nm, nk = S // TB, N_SHARD // spk

    def body(dp_ref, w_ref, out_ref):
        part = _dot_nt(dp_ref[:, 0:sw], w_ref[0])
        for s in range(1, spk):
            part = part + _dot_nt(dp_ref[:, s * sw:(s + 1) * sw], w_ref[s])

        @pl.when(pl.program_id(1) == 0)
        def _():
            out_ref[...] = part

        @pl.when(pl.program_id(1) > 0)
        def _():
            out_ref[...] += part

    (dh,), rode = _call_with_ride(
        body, ride, lambda: (pl.program_id(0) == 0) & (pl.program_id(1) == 0),
        lambda: (pl.program_id(0) == nm - 1) & (pl.program_id(1) == nk - 1),
        mid=lambda: (pl.program_id(0) == nm - 1) & (pl.program_id(1) == 0),
        name="dh", grid=(nm, nk),
        in_specs=[pl.BlockSpec((TB, spk * sw), lambda m, kk: (m, kk)),
                  pl.BlockSpec((spk, D, sw), lambda m, kk: (kk, 0, 0))],
        out_specs=[pl.BlockSpec((TB, D), lambda m, kk: (m, 0))],
        out_shape=[_sds((S, D))], compiler_params=_params("arbitrary", "arbitrary"),
    )(dproj, wg_in)
    return dh, rode


def _dw_in(h, dproj):
    S = dproj.shape[0]
    per = IN_W // N_SHARD // TB

    def body(h_ref, dp_ref, out_ref):
        out_ref[...] = _dot_tn(h_ref[...], dp_ref[...])

    return pl.pallas_call(
        body, name="dw_in", grid=(IN_W // TB,),
        in_specs=[pl.BlockSpec((S, D), lambda j: (0, 0)), pl.BlockSpec((S, TB), lambda j: (0, j))],
        out_specs=pl.BlockSpec((None, D, TB), lambda j: (j // per, 0, j % per)),
        out_shape=_sds((N_SHARD, D, IN_W // N_SHARD)), compiler_params=_params("parallel"),
    )(h, dproj)


def _prenorm_bwd(x, dh, dx2, norm_g, mod):
    S = x.shape[0]
    tm = 512

    def body(x_ref, dh_ref, dx2_ref, g_ref, mod_ref, gx_ref, dg_ref, dshift_ref, dscale_ref):
        i = pl.program_id(0)

        @pl.when(i == 0)
        def _():
            dg_ref[...] = jnp.zeros_like(dg_ref)
            dshift_ref[...] = jnp.zeros_like(dshift_ref)
            dscale_ref[...] = jnp.zeros_like(dscale_ref)

        xv = x_ref[...]
        dhv = dh_ref[...]
        g = g_ref[...]
        r = lax.rsqrt(jnp.mean(xv * xv, axis=-1, keepdims=True) + EPS)
        xh = xv * r
        dshift_ref[...] += jnp.sum(dhv, axis=0, keepdims=True)
        dscale_ref[...] += jnp.sum(dhv * (xh * g), axis=0, keepdims=True)
        dn1 = dhv * (1.0 + mod_ref[:, D:2 * D])
        dg_ref[...] += jnp.sum(dn1 * xh, axis=0, keepdims=True)
        dxh = dn1 * g
        gx_ref[...] = dx2_ref[...] + r * (dxh - xh * jnp.mean(dxh * xh, axis=-1, keepdims=True))

    row = pl.BlockSpec((tm, D), lambda i: (i, 0))
    vec = pl.BlockSpec((1, D), lambda i: (0, 0))
    return pl.pallas_call(
        body, name="prenorm_bwd", grid=(S // tm,),
        in_specs=[row, row, row, vec, pl.BlockSpec((1, 3 * D), lambda i: (0, 0))],
        out_specs=[row, vec, vec, vec],
        out_shape=[_sds((S, D)), _sds((1, D)), _sds((1, D)), _sds((1, D))],
        compiler_params=_params("arbitrary"),
    )(x, dh, dx2, norm_g, mod)


def _local_step(x, target, mod, wg_in, wab, wpb, wout, pool_w, pool_scale, rel_bias, norm_g, final_g, chip_half):
    buckets = jnp.asarray(_bucket_tables())
    bias_tab = _bias_table(rel_bias, buckets)
    h = _prenorm(x, norm_g, mod)
    qkv = [_proj(h, wg_in, 3 * g, 3, BF16 if GROUPS[g][1] == 1 else F32, f"proj_qkv{g}") for g in range(NG)]
    rest = _proj(h, wg_in, NCB_QKV, REST_W // CB, F32, "proj_rest")
    os_, ls_ = zip(*[_attn_fwd(qkv[g], bias_tab, g) for g in range(NG)])
    (dx2, loss, dfinal_g, dgate, dattn, stats, dpooled, dproj, dw_out, dw_ab, dw_pb, dpool_w,
     dpool_scale) = _mix_step(x, target, os_, ls_, rest, wab, wpb, pool_w, pool_scale, wout, mod, final_g)
    du = _pool_bwd(dpooled)

    small = [dw_ab, dw_pb, dw_out]
    dqkv0, ds0, sib_small = _attn_bwd(qkv[0], dattn, stats, bias_tab, 0, _ride_sibling_halves(small))
    p_small = _pair_sum_small(small, sib_small, chip_half)
    dqkv1, ds1, u_small = _attn_bwd(qkv[1], dattn, stats, bias_tab, 1,
                                    _ride_chip_exchange([p16 for _, p16 in p_small]))
    rs_ab, rs_pb, rs_out = _chip_sum_small([p32 for p32, _ in p_small], u_small, chip_half)
    dqkv2, ds2, _ = _attn_bwd(qkv[2], dattn, stats, bias_tab, 2, None)

    for j, piece in enumerate(dqkv0 + dqkv1 + dqkv2):
        dproj = lax.dynamic_update_slice(dproj, piece.astype(BF16), (0, j * AW))
    dproj = lax.dynamic_update_slice(dproj, du, (0, QKV_W + AW))
    dw_in = _dw_in(h, dproj)
    drel_rows, (sib_in,) = _bias_grad(jnp.concatenate([ds0, ds1, ds2], axis=0), buckets,
                                      _ride_sibling_halves([dw_in]))
    drel = drel_rows[:, 0, :NUM_BUCKETS].T
    p32_in, p16_in = _pair_sum(dw_in, sib_in, chip_half, "rs_pair_sum_in")
    dh, (u_in, _) = _dh(dproj, wg_in, _ride_chip_exchange_relayed([p16_in]))
    rs_in = _chip_sum(p32_in, u_in, chip_half, "rs_chip_sum_in")

    grad_x, dnorm_g, dshift, dscale = _prenorm_bwd(x, dh, dx2, norm_g, mod)
    dmod = jnp.concatenate([dshift, dscale, dgate], axis=1)
    return dict(loss=loss[0, 0], grad_x=grad_x, dmod=dmod, dnorm_g=dnorm_g, dfinal_g=dfinal_g, dpool_w=dpool_w,
                dpool_scale=dpool_scale, drel_bias=drel, dw_in=dw_in, dw_attn_br=dw_ab, dw_pool_br=dw_pb,
                dw_out=dw_out, rs_in=rs_in, rs_attn_br=rs_ab, rs_pool_br=rs_pb, rs_out=rs_out)


def _allgather8(blocks, name, relay=None):
    nb = len(blocks)
    relay = [False] * nb if relay is None else list(relay)

    def body(*refs):
        ins, outs = refs[:nb], refs[nb:2 * nb]
        send_sems, recv_sems = refs[2 * nb:]
        x, y, c = lax.axis_index("x"), lax.axis_index("y"), lax.axis_index("c")
        me, sibling = (x, y, c), (x, y, 1 - c)
        here, xn, yn, dg = (x, y), (1 - x, y), (x, 1 - y), (1 - x, 1 - y)

        def slot(a, chip, core, half=None):
            ref = outs[a].at[4 * chip[0] + 2 * chip[1] + core]
            if half is None:
                return ref
            r2 = ref.shape[0] // 2
            return ref.at[pl.ds(half * r2, r2)]

        def copy(a, k, dst, to, src=None):
            return pltpu.make_async_remote_copy(src_ref=dst if src is None else src, dst_ref=dst,
                                                send_sem=send_sems.at[a, k], recv_sem=recv_sems.at[a, k],
                                                device_id=to, device_id_type=MESH)

        def start(cps):
            for cp in cps:
                cp.start()
            return cps

        sent = []
        for a in range(nb):
            own = slot(a, here, c)
            sent += [copy(a, 0, own, sibling, src=ins[a]), copy(a, 1, own, (*xn, c), src=ins[a]),
                     copy(a, 2, own, (*yn, c), src=ins[a])]
            if not relay[a]:
                sent.append(copy(a, 3, own, (*dg, c), src=ins[a]))
        start(sent)
        for a in range(nb):
            copy(a, 2, slot(a, yn, c), me).wait_recv()
            sent += start([copy(a, 6, slot(a, yn, c), sibling)]
                          + ([copy(a, 3, slot(a, yn, c, 0), (*xn, c))] if relay[a] else []))
        for a in range(nb):
            copy(a, 1, slot(a, xn, c), me).wait_recv()
            sent += start([copy(a, 5, slot(a, xn, c), sibling)]
                          + ([copy(a, 4, slot(a, xn, c, 1), (*yn, c))] if relay[a] else []))
        for a in range(nb):
            for k, half in ((3, 0), (4, 1)) if relay[a] else ((3, None),):
                copy(a, k, slot(a, dg, c, half), me).wait_recv()
                sent += start([copy(a, 4 + k, slot(a, dg, c, half), sibling)])
        for a in range(nb):
            copy(a, 0, slot(a, here, 1 - c), me).wait_recv()
            copy(a, 5, slot(a, xn, 1 - c), me).wait_recv()
            copy(a, 6, slot(a, yn, 1 - c), me).wait_recv()
            for k, half in ((7, 0), (8, 1)) if relay[a] else ((7, None),):
                copy(a, k, slot(a, dg, 1 - c, half), me).wait_recv()
        for cp in sent:
            cp.wait_send()

    outs = pl.pallas_call(
        body, name=name, in_specs=[ANY] * nb, out_specs=[ANY] * nb,
        out_shape=[_sds((8,) + b.shape, b.dtype) for b in blocks],
        scratch_shapes=[_dma_sems(nb, 9), _dma_sems(nb, 9)],
    )(*blocks)
    return [_place_own(buf, b) for buf, b in zip(outs, blocks)]


def _place_own(buf, block):
    dev = 4 * lax.axis_index("x") + 2 * lax.axis_index("y") + lax.axis_index("c")
    return lax.dynamic_update_index_in_dim(buf, block, dev, 0)


def _ride_sibling_halves(gs):
    def copies(ins, outs, send_sems, recv_sems):
        x, y, c = lax.axis_index("x"), lax.axis_index("y"), lax.axis_index("c")
        cps = []
        for a in range(len(gs)):
            r2 = ins[a].shape[1] // 2
            other = ins[a].at[:, pl.ds((1 - c) * r2, r2), :]
            cps.append(pltpu.make_async_remote_copy(src_ref=other, dst_ref=outs[a], send_sem=send_sems.at[a],
                                                    recv_sem=recv_sems.at[a], device_id=(x, y, 1 - c),
                                                    device_id_type=MESH))
        return cps

    return _Ride(gs, [_sds((g.shape[0], g.shape[1] // 2, g.shape[2]), g.dtype) for g in gs], len(gs), copies)


def _pair_sum(g, t, chip_half, name):
    nsh, rows, cols = g.shape
    r2 = rows // 2
    tr = _row_tile(r2, cols)
    nt = r2 // tr

    def body(ch_ref, g_ref, t_ref, p32_ref, p16_ref):
        p = g_ref[...] + t_ref[...]
        p16_ref[...] = p.astype(BF16)

        @pl.when(pl.program_id(1) == ch_ref[0])
        def _():
            p32_ref[...] = p

    blk = pl.BlockSpec((None, tr, cols), lambda i, k, ch_ref: (k, i, 0))
    return pl.pallas_call(
        body, name=name,
        grid_spec=pltpu.PrefetchScalarGridSpec(
            num_scalar_prefetch=1, grid=(nt, nsh),
            in_specs=[pl.BlockSpec((None, tr, cols), lambda i, k, ch_ref: (k, ch_ref[1] * nt + i, 0)), blk],
            out_specs=[pl.BlockSpec((tr, cols), lambda i, k, ch_ref: (i, 0)), blk]),
        out_shape=[_sds((r2, cols)), _sds((nsh, r2, cols), BF16)],
        compiler_params=_params("parallel", "arbitrary"),
    )(chip_half, g, t)


def _pair_sum_small(gs, ts, chip_half):
    na = len(gs)

    def body(ch_ref, *refs):
        g_refs, t_refs, outs = refs[:na], refs[na:2 * na], refs[2 * na:]
        for a in range(na):
            r2 = t_refs[a].shape[1]
            own = pl.ds(pl.multiple_of(ch_ref[1] * r2, 8), r2)
            outs[2 * a + 1][...] = (g_refs[a][:, own, :] + t_refs[a][...]).astype(BF16)
            outs[2 * a][...] = g_refs[a][ch_ref[0], own, :] + t_refs[a][ch_ref[0]]

    res = pl.pallas_call(
        body, name="rs_pair_sum_small",
        in_specs=[pl.BlockSpec(memory_space=pltpu.SMEM)] + [pl.BlockSpec(memory_space=pltpu.VMEM)] * (2 * na),
        out_shape=[s for t in ts for s in (_sds(t.shape[1:]), _sds(t.shape, BF16))], compiler_params=_params(),
    )(chip_half, *gs, *ts)
    return [(res[2 * a], res[2 * a + 1]) for a in range(na)]


def _chip_sum_small(p32s, us, chip_half):
    na = len(p32s)

    def body(ch_ref, *refs):
        p_refs, u_refs, outs = refs[:na], refs[na:2 * na], refs[2 * na:]
        for a in range(na):
            r2 = p_refs[a].shape[0]
            acc = p_refs[a][...]
            for j in range(3):
                acc = acc + u_refs[a][j].astype(F32)
            outs[a][pl.ds(pl.multiple_of(ch_ref[1] * r2, 8), r2), :] = acc

    return pl.pallas_call(
        body, name="rs_chip_sum_small",
        in_specs=[pl.BlockSpec(memory_space=pltpu.SMEM)] + [pl.BlockSpec(memory_space=pltpu.VMEM)] * (2 * na),
        out_shape=[_sds((2 * p.shape[0], p.shape[1])) for p in p32s], compiler_params=_params(),
    )(chip_half, *p32s, *us)


def _ride_chip_exchange(ps):
    def copies(ins, outs, send_sems, recv_sems):
        x, y, c = lax.axis_index("x"), lax.axis_index("y"), lax.axis_index("c")
        chips = [(1 - x, y), (x, 1 - y), (1 - x, 1 - y)]
        cps = []
        for a in range(len(ps)):
            for j, (ox, oy) in enumerate(chips):
                cps.append(pltpu.make_async_remote_copy(src_ref=ins[a].at[2 * ox + oy], dst_ref=outs[a].at[j],
                                                        send_sem=send_sems.at[3 * a + j],
                                                        recv_sem=recv_sems.at[3 * a + j],
                                                        device_id=(ox, oy, c), device_id_type=MESH))
        return cps

    return _Ride(ps, [_sds((3,) + p.shape[1:], p.dtype) for p in ps], 3 * len(ps), copies)


def _ride_chip_exchange_relayed(ps):
    na = len(ps)

    def descriptors(ins, outs, send_sems, recv_sems):
        x, y, c = lax.axis_index("x"), lax.axis_index("y"), lax.axis_index("c")
        xn, yn, dg = (1 - x, y), (x, 1 - y), (1 - x, 1 - y)
        first, relays = [], []
        for a in range(na):
            u, relay = outs[a], outs[na + a]
            h2 = relay.shape[1]

            def copy(k, src, dst, chip):
                return pltpu.make_async_remote_copy(src_ref=src, dst_ref=dst, send_sem=send_sems.at[6 * a + k],
                                                    recv_sem=recv_sems.at[6 * a + k], device_id=(*chip, c),
                                                    device_id_type=MESH)

            diag = ins[a].at[2 * dg[0] + dg[1]]
            first += [copy(0, ins[a].at[2 * xn[0] + xn[1]], u.at[0], xn), copy(1, ins[a].at[2 * yn[0] + yn[1]], u.at[1], yn),
                      copy(2, diag.at[pl.ds(0, h2)], relay.at[0], xn), copy(3, diag.at[pl.ds(h2, h2)], relay.at[1], yn)]
            relays += [(copy(2, relay.at[0], relay.at[0], xn), copy(4, relay.at[0], u.at[2, pl.ds(0, h2)], yn)),
                       (copy(3, relay.at[1], relay.at[1], yn), copy(5, relay.at[1], u.at[2, pl.ds(h2, h2)], xn))]
        return first, relays

    return _Ride(ps, [_sds((3,) + p.shape[1:], p.dtype) for p in ps]
                 + [_sds((2, p.shape[1] // 2) + p.shape[2:], p.dtype) for p in ps], 6 * na,
                 lambda *refs: descriptors(*refs)[0], relays=lambda *refs: descriptors(*refs)[1],
                 landed_only=[4 * a + k for a in range(na) for k in (2, 3)])


def _chip_sum(p32, u, chip_half, name):
    r2, cols = p32.shape
    tr = _row_tile(r2, cols)
    nt = r2 // tr

    def body(ch_ref, p_ref, u_ref, o_ref):
        acc = p_ref[...]
        for j in range(3):
            acc = acc + u_ref[j].astype(F32)
        o_ref[...] = acc

    return pl.pallas_call(
        body, name=name,
        grid_spec=pltpu.PrefetchScalarGridSpec(
            num_scalar_prefetch=1, grid=(nt,),
            in_specs=[pl.BlockSpec((tr, cols), lambda i, ch_ref: (i, 0)),
                      pl.BlockSpec((3, tr, cols), lambda i, ch_ref: (0, i, 0))],
            out_specs=pl.BlockSpec((tr, cols), lambda i, ch_ref: (ch_ref[1] * nt + i, 0))),
        out_shape=_sds((2 * r2, cols)), compiler_params=_params("parallel"),
    )(chip_half, p32, u)


def _sibling_join(fs, name):
    nb = len(fs)

    def body(*refs):
        outs = refs[nb:2 * nb]
        send_sems, recv_sems = refs[2 * nb:]
        x, y, c = lax.axis_index("x"), lax.axis_index("y"), lax.axis_index("c")
        cps = []
        for a in range(nb):
            r2 = outs[a].shape[0] // 2
            rows = outs[a].at[pl.ds(c * r2, r2), :]
            cps.append(pltpu.make_async_remote_copy(src_ref=rows, dst_ref=rows, send_sem=send_sems.at[a],
                                                    recv_sem=recv_sems.at[a], device_id=(x, y, 1 - c),
                                                    device_id_type=MESH))
        for cp in cps:
            cp.start()
        for cp in cps:
            cp.wait()

    return pl.pallas_call(
        body, name=name, in_specs=[ANY] * nb, out_specs=[ANY] * nb,
        out_shape=[_sds(f.shape, f.dtype) for f in fs],
        input_output_aliases={a: a for a in range(nb)},
        scratch_shapes=[_dma_sems(nb), _dma_sems(nb)],
    )(*fs)


def _row_tile(rows, cols):
    tile = rows
    while tile * cols * 4 > (1 << 20) and tile % 16 == 0:
        tile //= 2
    return tile


def _w_ada_grad(c_all, dmod_cols):
    def body(c_ref, d_ref, o_ref):
        o_ref[...] = _dot_tn(c_ref[...].astype(BF16), d_ref[...].astype(BF16))

    return pl.pallas_call(body, name="w_ada_grad", out_shape=_sds((c_all.shape[1], dmod_cols.shape[1])),
                          compiler_params=_params())(c_all, dmod_cols)


def _adam_math(w, g, m, v):
    nm = ADAM_B1 * m + (1.0 - ADAM_B1) * g
    nv = ADAM_B2 * v + (1.0 - ADAM_B2) * (g * g)
    m_hat = nm / (1.0 - ADAM_B1 ** ADAM_STEP)
    v_hat = nv / (1.0 - ADAM_B2 ** ADAM_STEP)
    return -ADAM_LR * (m_hat / (jnp.sqrt(v_hat) + ADAM_EPS) + ADAM_WD * w), nm, nv


def _adamw(w, g, m, v, name):
    rows, cols = w.shape
    tr = _row_tile(rows, cols)

    def body(w_ref, g_ref, m_ref, v_ref, go_ref, d_ref, nm_ref, nv_ref):
        gv = g_ref[...]
        go_ref[...] = gv
        d_ref[...], nm_ref[...], nv_ref[...] = _adam_math(w_ref[...], gv, m_ref[...], v_ref[...])

    spec = pl.BlockSpec((tr, cols), lambda i: (i, 0))
    return pl.pallas_call(
        body, name=name, grid=(rows // tr,), in_specs=[spec] * 4, out_specs=[spec] * 4,
        out_shape=[_sds((rows, cols))] * 4, compiler_params=_params("parallel"),
    )(w, g, m, v)


def _pack_small(dmod, dnorm_g, dfinal_g, dpool_scale, drel_bias, loss, dpool_w):
    return jnp.concatenate([dmod.reshape(-1, 128), dnorm_g.reshape(-1, 128), dfinal_g.reshape(-1, 128),
                            jnp.pad(dpool_scale.reshape(-1, 128), ((0, PK_RELB - PK_PSCALE - AW // 128), (0, 0))),
                            jnp.pad(drel_bias, ((0, 0), (0, 128 - NG * NH))),
                            jnp.full((PK_POOLW - PK_LOSS, 128), loss, F32), dpool_w.reshape(-1, 128)], axis=0)


def _small_update(small_all, ws, ms, vs):
    lane_rows = [(r0, r0 + w.shape[1] // 128) for r0, w in zip((PK_BADA, PK_NORMG, PK_FINALG, PK_PSCALE), ws)]
    nw = len(ws)

    def body(all_ref, *refs):
        w_refs, m_refs, v_refs = refs[:nw], refs[nw:2 * nw], refs[2 * nw:3 * nw]
        loss_ref, outs = refs[3 * nw], refs[3 * nw + 1:]
        g = all_ref[0]
        for s in range(1, all_ref.shape[0]):
            g = g + all_ref[s]
        loss_ref[...] = jnp.broadcast_to(g[PK_LOSS:PK_LOSS + 1, :], loss_ref.shape)

        def put(p, at, gv):
            d, nm, nv = _adam_math(w_refs[p][at], gv, m_refs[p][at], v_refs[p][at])
            for o_ref, val in zip(outs[4 * p:4 * p + 4], (gv, d, nm, nv)):
                o_ref[at] = val

        for p, (r0, r1) in enumerate(lane_rows):
            for i in range(r1 - r0):
                put(p, (slice(None), slice(128 * i, 128 * (i + 1))), g[r0 + i:r0 + i + 1, :])
        put(4, (slice(None), slice(None)), g[PK_RELB:PK_LOSS, 0:NG * NH])
        put(5, (slice(None), slice(None)), g[PK_POOLW:PK_ROWS, :])

    res = pl.pallas_call(
        body, name="small_update",
        out_shape=[_sds((8, 128))] + [_sds(w.shape) for w in ws for _ in range(4)], compiler_params=_params(),
    )(small_all, *ws, *ms, *vs)
    return res[0], [res[1 + 4 * p:5 + 4 * p] for p in range(nw)]


def kernel(x, c, norm_g, w_ada, b_ada, w_in, pool_w, pool_scale, w_attn_br, w_pool_br, w_out, rel_bias, final_g, loss_target, m_norm_g, m_w_ada, m_b_ada, m_w_in, m_pool_w, m_pool_scale, m_w_attn_br, m_w_pool_br, m_w_out, m_rel_bias, m_final_g, v_norm_g, v_w_ada, v_b_ada, v_w_in, v_pool_w, v_pool_scale, v_w_attn_br, v_w_pool_br, v_w_out, v_rel_bias, v_final_g):
    ix, iy, ic = lax.axis_index("x"), lax.axis_index("y"), lax.axis_index("c")
    dev = 4 * ix + 2 * iy + ic
    chip = 2 * ix + iy

    def half(w):
        r2 = w.shape[0] // 2
        return lax.dynamic_slice_in_dim(w, ic * r2, r2, axis=0).astype(BF16)

    gathered = _allgather8([jnp.broadcast_to(c, (8, D)), half(w_in[0]), half(w_attn_br[0]), half(w_pool_br[0]),
                            half(w_out[0])], "gather_weights", relay=[False, True, True, True, True])
    c_all = gathered[0][:, 0, :]
    wg_in = gathered[1].reshape(N_SHARD, D, IN_W // N_SHARD)
    wab = gathered[2].reshape(N_SHARD, AW, D // N_SHARD).transpose(1, 0, 2).reshape(AW, D)
    wpb = gathered[3].reshape(N_SHARD, AW, D // N_SHARD).transpose(1, 0, 2).reshape(AW, D)
    wout = gathered[4].reshape(D, D)

    mw = 3 * D // N_SHARD
    modp = _mod_partial(c_all, w_ada[0], lax.dynamic_slice_in_dim(b_ada, chip * mw, mw, axis=1))
    mod_all = _allgather8([modp], "gather_mod")[0]
    mod_full = mod_all[::2].transpose(1, 0, 2).reshape(8, 3 * D)
    mod = lax.dynamic_slice_in_dim(mod_full, dev, 1, axis=0)

    chip_half = jnp.stack([chip, ic]).astype(jnp.int32)
    r = _local_step(x[0], loss_target[0], mod, wg_in, wab, wpb, wout, pool_w[0], pool_scale, rel_bias, norm_g,
                    final_g.reshape(1, D), chip_half)

    packed = _pack_small(r["dmod"], r["dnorm_g"], r["dfinal_g"], r["dpool_scale"], r["drel_bias"], r["loss"],
                         r["dpool_w"])
    small_all = _allgather8([packed], "gather_small")[0]
    small = ["b_ada", "norm_g", "final_g", "pool_scale", "rel_bias", "pool_w"]
    shaped = lambda b, n, f, ps, rb, pw: [b, n, f.reshape(1, D), ps, rb, pw.reshape(4 * PGW, PGW)]
    loss, small_out = _small_update(small_all, shaped(b_ada, norm_g, final_g, pool_scale, rel_bias, pool_w),
                                    shaped(m_b_ada, m_norm_g, m_final_g, m_pool_scale, m_rel_bias, m_pool_w),
                                    shaped(v_b_ada, v_norm_g, v_final_g, v_pool_scale, v_rel_bias, v_pool_w))
    dmod_all = small_all[:, PK_BADA:PK_NORMG, :].reshape(8, 3 * D)
    g_w_ada = _w_ada_grad(c_all, lax.dynamic_slice_in_dim(dmod_all, chip * mw, mw, axis=1))

    g_w_in, g_w_ab, g_w_pb, g_w_out = _sibling_join([r["rs_in"], r["rs_attn_br"], r["rs_pool_br"], r["rs_out"]],
                                                    "rs_sibling_join")
    upd = dict(zip(small, small_out))
    upd["final_g"] = [a.reshape(D) for a in upd["final_g"]]
    upd["pool_w"] = [a.reshape(1, 4, PGW, PGW) for a in upd["pool_w"]]
    for nme, w, g, m, v in (("w_ada", w_ada, g_w_ada, m_w_ada, v_w_ada), ("w_in", w_in, g_w_in, m_w_in, v_w_in),
                            ("w_attn_br", w_attn_br, g_w_ab, m_w_attn_br, v_w_attn_br),
                            ("w_pool_br", w_pool_br, g_w_pb, m_w_pool_br, v_w_pool_br),
                            ("w_out", w_out, g_w_out, m_w_out, v_w_out)):
        upd[nme] = [a[None] for a in _adamw(w[0], g, m[0], v[0], "adamw_" + nme)]
    names = ["norm_g", "w_ada", "b_ada", "w_in", "pool_w", "pool_scale", "w_attn_br", "w_pool_br", "w_out",
             "rel_bias", "final_g"]
    return (loss[0, 0], r["grad_x"][None]) + tuple(upd[nme][kind] for kind in range(4) for nme in names)
```

```python
import math

import numpy as np
import jax
import jax.numpy as jnp
from jax import lax
from jax.experimental import pallas as pl
from jax.experimental.pallas import tpu as pltpu

F32 = jnp.float32
BF16 = jnp.bfloat16

D = 1024
HD = 64
NH = 8
AW = NH * HD
GROUPS = ((128, 1), (512, 4), (2048, 16))
NG = len(GROUPS)
BLK = 128
GW = 3 * AW
QKV_W = NG * GW
REST_W = 3584
IN_W = QKV_W + REST_W
CB = 512
NCB_QKV = QKV_W // CB
POOL_WINDOWS = (2, 4, 8, 16)
PGW = 128
HALO = 16
NUM_BUCKETS = 32
MAX_DISTANCE = 2048
EPS = 1e-6
NEG = -1e30
N_SHARD = 4
VMEM_LIMIT = 56 * 1024 * 1024

ADAM_LR = 0.001
ADAM_B1 = 0.9
ADAM_B2 = 0.999
ADAM_EPS = 1e-08
ADAM_WD = 0.01
ADAM_STEP = 10

PK_BADA, PK_NORMG, PK_FINALG, PK_PSCALE, PK_RELB, PK_LOSS, PK_POOLW, PK_ROWS = 0, 24, 32, 40, 48, 80, 88, 600

ANY = pl.BlockSpec(memory_space=pl.ANY)
MESH = pl.DeviceIdType.MESH


def _params(*sem):
    return pltpu.CompilerParams(dimension_semantics=sem, vmem_limit_bytes=VMEM_LIMIT)


def _sds(shape, dtype=F32):
    return jax.ShapeDtypeStruct(shape, dtype)


def _dot(a, b):
    return jnp.dot(a, b, preferred_element_type=F32)


def _dot_nt(a, b):
    return lax.dot_general(a, b, (((1,), (1,)), ((), ())), preferred_element_type=F32)


def _dot_tn(a, b):
    return lax.dot_general(a, b, (((0,), (0,)), ((), ())), preferred_element_type=F32)


def _sigmoid(z):
    return 0.5 * jnp.tanh(0.5 * z) + 0.5


def _dma_sems(*shape):
    return pltpu.SemaphoreType.DMA(shape)


class _Ride:
    def __init__(self, arrays, out_shapes, n_copies, copies, relays=None, landed_only=()):
        self.arrays, self.out_shapes, self.n_copies, self.copies = list(arrays), list(out_shapes), n_copies, copies
        self.relays, self.landed_only = relays, tuple(landed_only)


def _call_with_ride(body, ride, first, last, *, in_specs, out_specs, out_shape, scratch_shapes=(), mid=None, **kw):
    in_specs, out_specs, out_shape, scratch_shapes = list(in_specs), list(out_specs), list(out_shape), list(scratch_shapes)
    n_in, n_out, n_sc = len(in_specs), len(out_specs), len(scratch_shapes)
    if ride is None:
        def run_plain(*operands):
            return pl.pallas_call(body, in_specs=in_specs, out_specs=out_specs, out_shape=out_shape,
                                  scratch_shapes=scratch_shapes, **kw)(*operands), []
        return run_plain
    n_ri, n_ro = len(ride.arrays), len(ride.out_shapes)

    def wrapped(*refs):
        ins, rest = refs[:n_in], refs[n_in:]
        r_ins, rest = rest[:n_ri], rest[n_ri:]
        outs, rest = rest[:n_out], rest[n_out:]
        r_outs, rest = rest[:n_ro], rest[n_ro:]
        scratch, (send_sems, recv_sems) = rest[:n_sc], rest[n_sc:]

        @pl.when(first())
        def _():
            for cp in ride.copies(r_ins, r_outs, send_sems, recv_sems):
                cp.start()

        if ride.relays is not None:
            @pl.when(mid())
            def _():
                for landed, onward in ride.relays(r_ins, r_outs, send_sems, recv_sems):
                    landed.wait_recv()
                    onward.start()

        body(*ins, *outs, *scratch)

        @pl.when(last())
        def _():
            for k, cp in enumerate(ride.copies(r_ins, r_outs, send_sems, recv_sems)):
                if k in ride.landed_only:
                    cp.wait_send()
                else:
                    cp.wait()
            if ride.relays is not None:
                for _, onward in ride.relays(r_ins, r_outs, send_sems, recv_sems):
                    onward.wait()

    def run(*operands):
        res = pl.pallas_call(
            wrapped, in_specs=in_specs + [ANY] * n_ri, out_specs=out_specs + [ANY] * n_ro,
            out_shape=out_shape + ride.out_shapes,
            scratch_shapes=scratch_shapes + [_dma_sems(ride.n_copies), _dma_sems(ride.n_copies)], **kw,
        )(*operands, *ride.arrays)
        return res[:n_out], res[n_out:]
    return run


def _bucket_tables():
    i = np.arange(BLK)[:, None]
    j = np.arange(2 * BLK)[None, :]
    dist = BLK + i - j
    valid = (dist >= 0) & (dist <= BLK)
    tabs = []
    for _, dil in GROUPS:
        n = (np.clip(dist, 0, BLK) * dil).astype(np.int32)
        max_exact = NUM_BUCKETS // 2
        nf = np.maximum(n, 1).astype(np.float32)
        large = max_exact + (np.log(nf / np.float32(max_exact)) / np.float32(math.log(MAX_DISTANCE / max_exact))
                             * np.float32(NUM_BUCKETS - max_exact)).astype(np.int32)
        large = np.minimum(large, NUM_BUCKETS - 1)
        bucket = np.where(n < max_exact, n, large)
        tab = np.where(valid, bucket, -1).astype(np.int32)
        perm = _block_perm(dil)
        tabs.append(tab[perm][:, np.concatenate([perm, BLK + perm])])
    return np.stack(tabs)


def _bias_table(rel_bias, buckets):
    def body(rb_ref, bk_ref, out_ref):
        g = pl.program_id(0)
        bk = bk_ref[...]
        for h in range(NH):
            acc = jnp.full((BLK, 2 * BLK), NEG, F32)
            for b in range(NUM_BUCKETS):
                acc = jnp.where(bk == b, rb_ref[b, g * NH + h], acc)
            out_ref[h] = acc

    return pl.pallas_call(
        body, name="bias_table", grid=(NG,),
        in_specs=[pl.BlockSpec(memory_space=pltpu.SMEM),
                  pl.BlockSpec((None, BLK, 2 * BLK), lambda g: (g, 0, 0))],
        out_specs=pl.BlockSpec((NH, BLK, 2 * BLK), lambda g: (g, 0, 0)),
        out_shape=_sds((NG * NH, BLK, 2 * BLK)),
        compiler_params=_params("arbitrary"),
    )(rel_bias, buckets)


def _bias_grad(ds_acc, buckets, ride):
    def body(acc_ref, bk_ref, out_ref):
        bk = bk_ref[...]
        acc = acc_ref[...]
        lane = lax.broadcasted_iota(jnp.int32, (8, 128), 1)
        out = jnp.zeros((8, 128), F32)
        for b in range(NUM_BUCKETS):
            val = jnp.sum(jnp.where(bk == b, acc, 0.0))
            out = jnp.where(lane == b, val, out)
        out_ref[...] = out

    (out,), rode = _call_with_ride(
        body, ride, lambda: pl.program_id(0) == 0, lambda: pl.program_id(0) == NG * NH - 1,
        name="bias_grad", grid=(NG * NH,),
        in_specs=[pl.BlockSpec((None, BLK, 2 * BLK), lambda gh: (gh, 0, 0)),
                  pl.BlockSpec((None, BLK, 2 * BLK), lambda gh: (gh // NH, 0, 0))],
        out_specs=[pl.BlockSpec((None, 8, 128), lambda gh: (gh, 0, 0))],
        out_shape=[_sds((NG * NH, 8, 128))],
        compiler_params=_params("arbitrary"),
    )(ds_acc, buckets)
    return out, rode


def _mod_partial(c_all, w_ada_s, b_ada_s):
    def body(c_ref, w_ref, b_ref, o_ref):
        o_ref[...] = _dot(c_ref[...].astype(BF16), w_ref[...].astype(BF16)) + b_ref[...]

    return pl.pallas_call(body, name="mod_partial", out_shape=_sds((8, w_ada_s.shape[1])),
                          compiler_params=_params())(c_all, w_ada_s, b_ada_s)


def _prenorm(x, norm_g, mod):
    S = x.shape[0]
    tm = 512

    def body(x_ref, g_ref, mod_ref, h_ref):
        xv = x_ref[...]
        r = lax.rsqrt(jnp.mean(xv * xv, axis=-1, keepdims=True) + EPS)
        n1 = xv * r * g_ref[...]
        h_ref[...] = (n1 * (1.0 + mod_ref[:, D:2 * D]) + mod_ref[:, 0:D]).astype(BF16)

    return pl.pallas_call(
        body, name="prenorm", grid=(S // tm,),
        in_specs=[pl.BlockSpec((tm, D), lambda i: (i, 0)), pl.BlockSpec((1, D), lambda i: (0, 0)),
                  pl.BlockSpec((1, 3 * D), lambda i: (0, 0))],
        out_specs=pl.BlockSpec((tm, D), lambda i: (i, 0)),
        out_shape=_sds((S, D), BF16), compiler_params=_params("parallel"),
    )(x, norm_g, mod)


def _proj(h, wg_in, j0, nj, dtype, name):
    S = h.shape[0]
    tm = 2048
    per = wg_in.shape[2] // CB

    def body(h_ref, w_ref, o_ref):
        o_ref[...] = _dot(h_ref[...], w_ref[...]).astype(dtype)

    return pl.pallas_call(
        body, name=name, grid=(S // tm, nj),
        in_specs=[pl.BlockSpec((tm, D), lambda m, j: (m, 0)),
                  pl.BlockSpec((None, D, CB), lambda m, j: ((j0 + j) // per, 0, (j0 + j) % per))],
        out_specs=pl.BlockSpec((tm, CB), lambda m, j: (m, j)),
        out_shape=_sds((S, nj * CB), dtype), compiler_params=_params("parallel", "parallel"),
    )(h, wg_in)


HS = 4
SLAB = HS * HD


def _lane_head(rows):
    return lax.broadcasted_iota(jnp.int32, (rows, SLAB), 1) // HD


def _head_stack(a):
    head = _lane_head(a.shape[0])
    return jnp.concatenate([jnp.where(head == h, a, jnp.zeros_like(a)) for h in range(HS)], axis=0)


def _head_unstack(a):
    rows = a.shape[0] // HS
    head = _lane_head(rows)
    out = a[:rows]
    for h in range(1, HS):
        out = jnp.where(head == h, a[h * rows:(h + 1) * rows], out)
    return out


STAT_W = 128
VIEW = 16


def _sub_layout(dil):
    if dil == 1:
        return BLK, [None]
    return BLK * dil // VIEW, [[r + dil * u for u in range(VIEW // dil)] for r in range(dil)]


def _block_perm(dil):
    a_rows, _ = _sub_layout(dil)
    p = np.arange(BLK)
    return p if dil == 1 else (VIEW // dil) * (p % a_rows) + p // a_rows


LB = 128
N_SLAB = NH // HS


def _ld(refs, bs, s, w):
    if bs is None:
        return refs[0][:, s * w:(s + 1) * w]
    a_rows = refs[0].shape[0] // VIEW
    return jnp.concatenate([jnp.concatenate([ref[pl.ds(b, a_rows, stride=VIEW), :] for b in bs], axis=0)
                            for ref in refs], axis=1)


def _st(ref, bs, s, val):
    if bs is None:
        ref[:, s * SLAB:(s + 1) * SLAB] = val.astype(ref.dtype)
        return
    a_rows = val.shape[0] // len(bs)
    for u, b in enumerate(bs):
        ref[:, b, :] = val[u * a_rows:(u + 1) * a_rows]


def _attn_views(dil, S):
    a_rows, subs = _sub_layout(dil)
    if dil == 1:
        def ispecs(base, w, f):
            return [pl.BlockSpec((BLK, N_SLAB * w), lambda sg, n: (f(n), base // (N_SLAB * w)))]
        return subs, S // BLK, N_SLAB, ispecs, (lambda w: (S, w)), (
            lambda f: pl.BlockSpec((BLK, AW), lambda sg, n: (f(n), 0)))

    def ispecs(base, w, f):
        return [pl.BlockSpec((a_rows * VIEW, LB), lambda sg, n, k=k: (f(n), (base + sg * w) // LB + k))
                for k in range(w // LB)]
    return subs, S // (a_rows * VIEW), 1, ispecs, (lambda w: (S // VIEW, VIEW, w)), (
        lambda f: pl.BlockSpec((a_rows, VIEW, SLAB), lambda sg, n: (f(n), 0, sg)))


def _attn_fwd(qkv_g, bias_tab, g):
    S = qkv_g.shape[0]
    subs, nbq, sps, ispecs, shape, ospec = _attn_views(GROUPS[g][1], S)
    cur = lambda n: n
    in_specs = [ispecs(0, SLAB, cur), ispecs(AW, SLAB, cur), ispecs(2 * AW, SLAB, cur)]
    nl = len(in_specs[0])

    def body(*refs):
        q, k, v = (refs[t * nl:(t + 1) * nl] for t in range(3))
        b_ref, o_ref, l_ref, kprev, vprev = refs[3 * nl:]
        n = pl.program_id(1)

        @pl.when(n == 0)
        def _():
            kprev[...] = jnp.zeros_like(kprev)
            vprev[...] = jnp.zeros_like(vprev)

        col = lax.broadcasted_iota(jnp.int32, (HS * BLK, 2 * BLK), 1)
        keep = (col >= BLK) | (n > 0)
        for s_ in range(sps):
            cs = slice(s_ * SLAB, (s_ + 1) * SLAB)
            bias = b_ref[pl.ds(s_ * HS, HS)].reshape(HS * BLK, 2 * BLK)
            for i, bs in enumerate(subs):
                kc, vc = _ld(k, bs, s_, SLAB).astype(BF16), _ld(v, bs, s_, SLAB).astype(BF16)
                kb = jnp.concatenate([kprev[i, :, cs], kc], axis=0)
                vb = jnp.concatenate([vprev[i, :, cs], vc], axis=0)
                kprev[i, :, cs], vprev[i, :, cs] = kc, vc
                s = _dot_nt(_head_stack(_ld(q, bs, s_, SLAB).astype(BF16)), kb) * (HD ** -0.5) + bias
                s = jnp.where(keep, s, NEG)
                m = jnp.max(s, axis=-1, keepdims=True)
                p = jnp.exp(s - m)
                den = jnp.sum(p, axis=-1, keepdims=True)
                _st(o_ref, bs, s_, _head_unstack(_dot(p.astype(BF16), vb) / den))
                _st(l_ref, bs, s_, _head_unstack(jnp.broadcast_to(m + jnp.log(den), (HS * BLK, SLAB))))

    out = _sds(shape(AW))
    nsg = N_SLAB // sps
    o, l = pl.pallas_call(
        body, name=f"attn_fwd{g}", grid=(nsg, nbq),
        in_specs=sum(in_specs, []) + [pl.BlockSpec((sps * HS, BLK, 2 * BLK), lambda sg, n: (g * nsg + sg, 0, 0))],
        out_specs=[ospec(cur), ospec(cur)],
        out_shape=[out, out],
        scratch_shapes=[pltpu.VMEM((len(subs), BLK, sps * SLAB), BF16)] * 2,
        compiler_params=_params("parallel", "arbitrary"),
    )(*([qkv_g] * (3 * nl)), bias_tab)
    return o.reshape(S, AW), l.reshape(S, AW)


def _attn_bwd(qkv_g, dattn, stats, bias_tab, g, ride):
    S = qkv_g.shape[0]
    subs, nbq, sps, ispecs, shape, ospec = _attn_views(GROUPS[g][1], S)
    cur = lambda n: jnp.minimum(n, nbq - 1)
    late = lambda n: jnp.maximum(n - 1, 0)
    in_specs = [ispecs(0, SLAB, cur), ispecs(AW, SLAB, cur), ispecs(2 * AW, SLAB, cur), ispecs(0, SLAB, cur),
                ispecs(0, STAT_W, cur)]
    nl = len(in_specs[0])

    def body(*refs):
        q, k, v, da = (refs[t * nl:(t + 1) * nl] for t in range(4))
        st_ref, b_ref, dq_ref, dk_ref, dv_ref, ds_ref, ck_ref, cv_ref, kprev, vprev = refs[4 * nl:]
        n = pl.program_id(1)

        @pl.when(n == 0)
        def _():
            for ref in (ds_ref, ck_ref, cv_ref, kprev, vprev):
                ref[...] = jnp.zeros_like(ref)

        @pl.when(n < nbq)
        def _():
            col = lax.broadcasted_iota(jnp.int32, (HS * BLK, 2 * BLK), 1)
            keep = (col >= BLK) | (n > 0)
            for s_ in range(sps):
                cs = slice(s_ * SLAB, (s_ + 1) * SLAB)
                bias = b_ref[pl.ds(s_ * HS, HS)].reshape(HS * BLK, 2 * BLK)
                for i, bs in enumerate(subs):
                    st = _ld((st_ref,), bs, s_, STAT_W)
                    kc, vc = _ld(k, bs, s_, SLAB).astype(BF16), _ld(v, bs, s_, SLAB).astype(BF16)
                    kb = jnp.concatenate([kprev[i, :, cs], kc], axis=0)
                    vb = jnp.concatenate([vprev[i, :, cs], vc], axis=0)
                    kprev[i, :, cs], vprev[i, :, cs] = kc, vc
                    lse =jnp.concatenate([st[:, h:h + 1] for h in range(HS)], axis=0)
                    delta = jnp.concatenate([st[:, HS + h:HS + h + 1] for h in range(HS)], axis=0)
                    qs = _head_stack(_ld(q, bs, s_, SLAB).astype(BF16))
                    dos = _head_stack(_ld(da, bs, s_, SLAB).astype(BF16))
                    s = _dot_nt(qs, kb) * (HD ** -0.5) + bias
                    s = jnp.where(keep, s, NEG)
                    p = jnp.exp(s - lse)
                    ds = p * (_dot_nt(dos, vb) - delta)
                    ds_ref[pl.ds(s_ * HS, HS)] += ds.reshape(HS, BLK, 2 * BLK)
                    ds_b = (ds * (HD ** -0.5)).astype(BF16)
                    _st(dq_ref, bs, s_, _head_unstack(_dot(ds_b, kb)))
                    dkb = _dot_tn(ds_b, qs)
                    dvb = _dot_tn(p.astype(BF16), dos)
                    _st(dk_ref, bs, s_, ck_ref[i, :, cs] + dkb[:BLK])
                    _st(dv_ref, bs, s_, cv_ref[i, :, cs] + dvb[:BLK])
                    ck_ref[i, :, cs] = dkb[BLK:]
                    cv_ref[i, :, cs] = dvb[BLK:]

        @pl.when(n == nbq)
        def _():
            for s_ in range(sps):
                for i, bs in enumerate(subs):
                    _st(dk_ref, bs, s_, ck_ref[i, :, s_ * SLAB:(s_ + 1) * SLAB])
                    _st(dv_ref, bs, s_, cv_ref[i, :, s_ * SLAB:(s_ + 1) * SLAB])

    out = _sds(shape(AW), BF16 if GROUPS[g][1] == 1 else F32)
    nsg = N_SLAB // sps
    (dq, dk, dv, ds_acc), rode = _call_with_ride(
        body, ride, lambda: (pl.program_id(0) == 0) & (pl.program_id(1) == 0),
        lambda: (pl.program_id(0) == nsg - 1) & (pl.program_id(1) == nbq),
        name=f"attn_bwd{g}", grid=(nsg, nbq + 1),
        in_specs=sum(in_specs, []) + [pl.BlockSpec((sps * HS, BLK, 2 * BLK), lambda sg, n: (g * nsg + sg, 0, 0))],
        out_specs=[ospec(cur), ospec(late), ospec(late),
                   pl.BlockSpec((sps * HS, BLK, 2 * BLK), lambda sg, n: (sg, 0, 0))],
        out_shape=[out] * 3 + [_sds((NH, BLK, 2 * BLK))],
        scratch_shapes=[pltpu.VMEM((len(subs), BLK, sps * SLAB), F32)] * 2
        + [pltpu.VMEM((len(subs), BLK, sps * SLAB), BF16)] * 2,
        compiler_params=_params("arbitrary", "arbitrary"),
    )(*([qkv_g] * (3 * nl)), *([dattn] * nl), stats, bias_tab)
    return [dq.reshape(S, AW), dk.reshape(S, AW), dv.reshape(S, AW)], ds_acc, rode


TM_MIX = 256


def _mix_specs(tm):
    row512 = pl.BlockSpec((tm, AW), lambda i: (i, 0))
    return ([row512] * 6 + [
        pl.BlockSpec((tm, REST_W), lambda i: (i, 0)),
        pl.BlockSpec((HALO, AW), lambda i: (jnp.maximum(i * (tm // HALO) - 1, 0), 1)),
        pl.BlockSpec((AW, D), lambda i: (0, 0)), pl.BlockSpec((AW, D), lambda i: (0, 0)),
        pl.BlockSpec((4, PGW, PGW), lambda i: (0, 0, 0)), pl.BlockSpec((1, AW), lambda i: (0, 0))])


def _mix_forward(i, tm, o_refs, l_refs, rest_ref, halo_ref, wab_ref, wpb_ref, pw_ref, ps_ref):
    l0, l1, l2 = (r[...] for r in l_refs)
    mx = jnp.maximum(jnp.maximum(l0, l1), l2)
    e0, e1, e2 = jnp.exp(l0 - mx), jnp.exp(l1 - mx), jnp.exp(l2 - mx)
    den = e0 + e1 + e2
    lj = mx + jnp.log(den)
    attn = (e0 * o_refs[0][...] + e1 * o_refs[1][...] + e2 * o_refs[2][...]) / den

    z_attn = rest_ref[:, 0:AW]
    u = rest_ref[:, AW:2 * AW]
    z_pool = rest_ref[:, 2 * AW:3 * AW]
    g_attn = rest_ref[:, 3 * AW:3 * AW + D]
    g_pool = rest_ref[:, 3 * AW + D:3 * AW + 2 * D]

    sg_a = _sigmoid(z_attn)
    sil_a = z_attn * sg_a
    a_g = (attn * sil_a).astype(BF16)
    y_attn = _dot(a_g, wab_ref[...])

    halo = jnp.where(i > 0, halo_ref[...], 0.0)
    ext = jnp.concatenate([halo, u], axis=0)
    t = i * tm + lax.broadcasted_iota(jnp.int32, (tm, 1), 0)
    pooled, mixed_raw = [], []
    for gi, win in enumerate(POOL_WINDOWS):
        s = ext[:, gi * PGW:(gi + 1) * PGW]
        sh = 1
        while sh < win:
            s = s + pltpu.roll(s, sh, 0)
            sh *= 2
        cnt = jnp.minimum(t + 1, win).astype(F32)
        pg = s[HALO:] / cnt - u[:, gi * PGW:(gi + 1) * PGW]
        pooled.append(pg.astype(BF16))
        mixed_raw.append(_dot(pooled[-1], pw_ref[gi].astype(BF16)))
    mixed_raw = jnp.concatenate(mixed_raw, axis=1)
    mixed = mixed_raw * ps_ref[...]
    sg_p = _sigmoid(z_pool)
    sil_p = z_pool * sg_p
    m_g = (mixed * sil_p).astype(BF16)
    y_pool = _dot(m_g, wpb_ref[...])

    sa = _sigmoid(g_attn)
    sp = _sigmoid(g_pool)
    merged = sa * y_attn + sp * y_pool
    return dict(lj=lj, attn=attn, z_attn=z_attn, z_pool=z_pool, sg_a=sg_a, sil_a=sil_a, a_g=a_g, y_attn=y_attn,
                pooled=pooled, mixed_raw=mixed_raw, mixed=mixed, sg_p=sg_p, sil_p=sil_p, m_g=m_g, y_pool=y_pool,
                sa=sa, sp=sp, merged=merged)


def _mix_step(x, target, os_, ls_, rest, wab, wpb, pool_w, pool_scale, wout, mod, final_g):
    S = x.shape[0]
    tm = TM_MIX
    nt = S // tm
    sw = D // N_SHARD

    def body(o0, o1, o2, l0, l1, l2, rest_ref, halo_ref, wab_ref, wpb_ref, pw_ref, ps_ref,
             x_ref, t_ref, wo_ref, mod_ref, fg_ref, dx2_ref, loss_ref, dfg_ref, dgate_ref,
             dattn_ref, stats_ref, dpooled_ref, dproj_hbm, dwo_hbm, dwab_hbm, dwpb_hbm, dpw_ref, dps_ref,
             awo, awab, awpb, stage, stage_sem):
        i = pl.program_id(0)
        slot = i % 2

        def staged(step, sl):
            return pltpu.make_async_copy(stage.at[sl], dproj_hbm.at[pl.ds(step * tm, tm), pl.ds(QKV_W, REST_W)],
                                         stage_sem.at[sl])

        @pl.when(i == 0)
        def _():
            for ref in (loss_ref, dfg_ref, dgate_ref, awo, awab, awpb, dpw_ref, dps_ref):
                ref[...] = jnp.zeros_like(ref)

        f = _mix_forward(i, tm, (o0, o1, o2), (l0, l1, l2), rest_ref, halo_ref, wab_ref, wpb_ref, pw_ref, ps_ref)
        mo = _dot(f["merged"].astype(BF16), wo_ref[...])
        gate = mod_ref[:, 2 * D:3 * D]
        fg = fg_ref[...]
        x2 = x_ref[...] + gate * mo
        r2 = lax.rsqrt(jnp.mean(x2 * x2, axis=-1, keepdims=True) + EPS)
        n2 = x2 * r2
        err = n2 * fg - t_ref[...]
        loss_ref[...] += 0.5 * jnp.sum(jnp.mean(err * err, axis=-1, keepdims=True))
        dy = err * (1.0 / D)
        dfg_ref[...] += jnp.sum(dy * n2, axis=0, keepdims=True)
        dn = dy * fg
        dx2 = r2 * (dn - n2 * jnp.mean(dn * n2, axis=-1, keepdims=True))
        dgate_ref[...] += jnp.sum(dx2 * mo, axis=0, keepdims=True)
        dx2_ref[...] = dx2

        dmo_b = (dx2 * gate).astype(BF16)
        dmerged = _dot_nt(dmo_b, wo_ref[...])
        awo[...] += _dot_tn(f["merged"].astype(BF16), dmo_b)
        sa, sp = f["sa"], f["sp"]
        dya = (dmerged * sa).astype(BF16)
        dyp = (dmerged * sp).astype(BF16)
        dg_attn = dmerged * f["y_attn"] * sa * (1.0 - sa)
        dg_pool = dmerged * f["y_pool"] * sp * (1.0 - sp)
        dag = _dot_nt(dya, wab_ref[...])
        awab[...] += _dot_tn(f["a_g"], dya)
        dmg = _dot_nt(dyp, wpb_ref[...])
        awpb[...] += _dot_tn(f["m_g"], dyp)
        dattn = dag * f["sil_a"]
        dattn_ref[...] = dattn
        prod = dattn * f["attn"]
        lane = lax.broadcasted_iota(jnp.int32, (tm, STAT_W), 1)
        for sb in range(N_SLAB):
            st = jnp.zeros((tm, STAT_W), F32)
            for h in range(HS):
                hs = slice((sb * HS + h) * HD, (sb * HS + h + 1) * HD)
                st = jnp.where(lane == h, f["lj"][:, hs.start:hs.start + 1], st)
                st = jnp.where(lane == HS + h, jnp.sum(prod[:, hs], axis=-1, keepdims=True), st)
            stats_ref[:, sb * STAT_W:(sb + 1) * STAT_W] = st
        dz_attn = dag * f["attn"] * (f["sg_a"] * (1.0 + f["z_attn"] * (1.0 - f["sg_a"])))
        dmixed = dmg * f["sil_p"]
        dz_pool = dmg * f["mixed"] * (f["sg_p"] * (1.0 + f["z_pool"] * (1.0 - f["sg_p"])))
        dps_ref[...] += jnp.sum(dmixed * f["mixed_raw"], axis=0, keepdims=True)
        dpm = (dmixed * ps_ref[...]).astype(BF16)
        for gi in range(len(POOL_WINDOWS)):
            cs = slice(gi * PGW, (gi + 1) * PGW)
            dpw_ref[gi] += _dot_tn(f["pooled"][gi], dpm[:, cs])
            dpooled_ref[:, cs] = _dot_nt(dpm[:, cs], pw_ref[gi].astype(BF16))
        @pl.when(i >= 2)
        def _():
            staged(i - 2, slot).wait()

        stage[slot, :, 0:AW] = dz_attn.astype(BF16)
        stage[slot, :, AW:2 * AW] = jnp.zeros((tm, AW), BF16)
        stage[slot, :, 2 * AW:3 * AW] = dz_pool.astype(BF16)
        stage[slot, :, 3 * AW:3 * AW + D] = dg_attn.astype(BF16)
        stage[slot, :, 3 * AW + D:3 * AW + 2 * D] = dg_pool.astype(BF16)
        staged(i, slot).start()

        @pl.when(i == nt - 1)
        def _():
            staged(i - 1, 1 - slot).wait()
            staged(i, slot).wait()
            pltpu.sync_copy(awo, dwo_hbm)
            for k in range(N_SHARD):
                pltpu.sync_copy(awab.at[:, pl.ds(k * sw, sw)], dwab_hbm.at[k])
                pltpu.sync_copy(awpb.at[:, pl.ds(k * sw, sw)], dwpb_hbm.at[k])

    row = pl.BlockSpec((tm, D), lambda i: (i, 0))
    vec = pl.BlockSpec((1, D), lambda i: (0, 0))
    row512 = pl.BlockSpec((tm, AW), lambda i: (i, 0))
    outs = pl.pallas_call(
        body, name="mix_step", grid=(nt,),
        in_specs=_mix_specs(tm) + [row, row, pl.BlockSpec((D, D), lambda i: (0, 0)),
                                   pl.BlockSpec((1, 3 * D), lambda i: (0, 0)), vec],
        out_specs=[row, pl.BlockSpec((8, 128), lambda i: (0, 0)), vec, vec,
                   row512, pl.BlockSpec((tm, N_SLAB * STAT_W), lambda i: (i, 0)), row512, ANY, ANY, ANY, ANY,
                   pl.BlockSpec((4, PGW, PGW), lambda i: (0, 0, 0)), pl.BlockSpec((1, AW), lambda i: (0, 0))],
        out_shape=[_sds((S, D)), _sds((8, 128)), _sds((1, D)), _sds((1, D)),
                   _sds((S, AW)), _sds((S, N_SLAB * STAT_W)), _sds((S, AW)), _sds((S, IN_W), BF16),
                   _sds((D, D)), _sds((N_SHARD, AW, sw)), _sds((N_SHARD, AW, sw)), _sds((4, PGW, PGW)), _sds((1, AW))],
        scratch_shapes=[pltpu.VMEM((D, D), F32), pltpu.VMEM((AW, D), F32), pltpu.VMEM((AW, D), F32),
                        pltpu.VMEM((2, tm, REST_W), BF16), _dma_sems(2)],
        compiler_params=_params("arbitrary"),
    )(*os_, *ls_, rest, rest, wab, wpb, pool_w, pool_scale, x, target, wout, mod, final_g)
    dx2, loss, dfg, dgate, dattn, stats, dpooled, dproj, dwo, dwab, dwpb, dpw, dps = outs
    return (dx2, loss, dfg, dgate, dattn, stats, dpooled, dproj, dwo.reshape(N_SHARD, D // N_SHARD, D), dwab, dwpb,
            dpw, dps)


def _pool_bwd(dpooled):
    S = dpooled.shape[0]
    tm = 512
    nt = S // tm

    def body(dp_ref, nxt_ref, du_ref):
        i = pl.program_id(0)
        t = i * tm + lax.broadcasted_iota(jnp.int32, (tm + HALO, 1), 0)
        nxt = jnp.where(i < nt - 1, nxt_ref[...], 0.0)
        ext = jnp.concatenate([dp_ref[...], nxt], axis=0)
        for gi, win in enumerate(POOL_WINDOWS):
            cs = slice(gi * PGW, (gi + 1) * PGW)
            s = ext[:, cs] / jnp.minimum(t + 1, win).astype(F32)
            sh = 1
            while sh < win:
                s = s + pltpu.roll(s, tm + HALO - sh, 0)
                sh *= 2
            du_ref[:, cs] = (s[:tm] - dp_ref[:, cs]).astype(BF16)

    return pl.pallas_call(
        body, name="pool_bwd", grid=(nt,),
        in_specs=[pl.BlockSpec((tm, AW), lambda i: (i, 0)),
                  pl.BlockSpec((HALO, AW), lambda i: (jnp.minimum((i + 1) * (tm // HALO), S // HALO - 1), 0))],
        out_specs=pl.BlockSpec((tm, AW), lambda i: (i, 0)),
        out_shape=_sds((S, AW), BF16), compiler_params=_params("parallel"),
    )(dpooled, dpooled)


TB = 1024


def _dh(dproj, wg_in, ride):
    S = dproj.shape[0]
    sw = wg_in.shape[2]
    spk = 2
    nm, nk = S // TB, N_SHARD // spk

    def body(dp_ref, w_ref, out_ref):
        part = _dot_nt(dp_ref[:, 0:sw], w_ref[0])
        for s in range(1, spk):
            part = part + _dot_nt(dp_ref[:, s * sw:(s + 1) * sw], w_ref[s])

        @pl.when(pl.program_id(1) == 0)
        def _():
            out_ref[...] = part

        @pl.when(pl.program_id(1) > 0)
        def _():
            out_ref[...] += part

    (dh,), rode = _call_with_ride(
        body, ride, lambda: (pl.program_id(0) == 0) & (pl.program_id(1) == 0),
        lambda: (pl.program_id(0) == nm - 1) & (pl.program_id(1) == nk - 1),
        mid=lambda: (pl.program_id(0) == nm // 2) & (pl.program_id(1) == 0),
        name="dh", grid=(nm, nk),
        in_specs=[pl.BlockSpec((TB, spk * sw), lambda m, kk: (m, kk)),
                  pl.BlockSpec((spk, D, sw), lambda m, kk: (kk, 0, 0))],
        out_specs=[pl.BlockSpec((TB, D), lambda m, kk: (m, 0))],
        out_shape=[_sds((S, D))], compiler_params=_params("arbitrary", "arbitrary"),
    )(dproj, wg_in)
    return dh, rode


def _dw_in(h, dproj):
    S = dproj.shape[0]
    per = IN_W // N_SHARD // TB

    def body(h_ref, dp_ref, out_ref):
        out_ref[...] = _dot_tn(h_ref[...], dp_ref[...])

    return pl.pallas_call(
        body, name="dw_in", grid=(IN_W // TB,),
        in_specs=[pl.BlockSpec((S, D), lambda j: (0, 0)), pl.BlockSpec((S, TB), lambda j: (0, j))],
        out_specs=pl.BlockSpec((None, D, TB), lambda j: (j // per, 0, j % per)),
        out_shape=_sds((N_SHARD, D, IN_W // N_SHARD)), compiler_params=_params("parallel"),
    )(h, dproj)


def _prenorm_bwd(x, dh, dx2, norm_g, mod):
    S = x.shape[0]
    tm = 512

    def body(x_ref, dh_ref, dx2_ref, g_ref, mod_ref, gx_ref, dg_ref, dshift_ref, dscale_ref):
        i = pl.program_id(0)

        @pl.when(i == 0)
        def _():
            dg_ref[...] = jnp.zeros_like(dg_ref)
            dshift_ref[...] = jnp.zeros_like(dshift_ref)
            dscale_ref[...] = jnp.zeros_like(dscale_ref)

        xv = x_ref[...]
        dhv = dh_ref[...]
        g = g_ref[...]
        r = lax.rsqrt(jnp.mean(xv * xv, axis=-1, keepdims=True) + EPS)
        xh = xv * r
        dshift_ref[...] += jnp.sum(dhv, axis=0, keepdims=True)
        dscale_ref[...] += jnp.sum(dhv * (xh * g), axis=0, keepdims=True)
        dn1 = dhv * (1.0 + mod_ref[:, D:2 * D])
        dg_ref[...] += jnp.sum(dn1 * xh, axis=0, keepdims=True)
        dxh = dn1 * g
        gx_ref[...] = dx2_ref[...] + r * (dxh - xh * jnp.mean(dxh * xh, axis=-1, keepdims=True))

    row = pl.BlockSpec((tm, D), lambda i: (i, 0))
    vec = pl.BlockSpec((1, D), lambda i: (0, 0))
    return pl.pallas_call(
        body, name="prenorm_bwd", grid=(S // tm,),
        in_specs=[row, row, row, vec, pl.BlockSpec((1, 3 * D), lambda i: (0, 0))],
        out_specs=[row, vec, vec, vec],
        out_shape=[_sds((S, D)), _sds((1, D)), _sds((1, D)), _sds((1, D))],
        compiler_params=_params("arbitrary"),
    )(x, dh, dx2, norm_g, mod)


def _local_step(x, target, mod, wg_in, wab, wpb, wout, pool_w, pool_scale, rel_bias, norm_g, final_g, chip_half):
    buckets = jnp.asarray(_bucket_tables())
    bias_tab = _bias_table(rel_bias, buckets)
    h = _prenorm(x, norm_g, mod)
    qkv = [_proj(h, wg_in, 3 * g, 3, BF16 if GROUPS[g][1] == 1 else F32, f"proj_qkv{g}") for g in range(NG)]
    rest = _proj(h, wg_in, NCB_QKV, REST_W // CB, F32, "proj_rest")
    os_, ls_ = zip(*[_attn_fwd(qkv[g], bias_tab, g) for g in range(NG)])
    (dx2, loss, dfinal_g, dgate, dattn, stats, dpooled, dproj, dw_out, dw_ab, dw_pb, dpool_w,
     dpool_scale) = _mix_step(x, target, os_, ls_, rest, wab, wpb, pool_w, pool_scale, wout, mod, final_g)
    du = _pool_bwd(dpooled)

    small = [dw_ab, dw_pb, dw_out]
    dqkv0, ds0, sib_small = _attn_bwd(qkv[0], dattn, stats, bias_tab, 0, _ride_sibling_halves(small))
    p_small = _pair_sum_small(small, sib_small, chip_half)
    dqkv1, ds1, u_small = _attn_bwd(qkv[1], dattn, stats, bias_tab, 1,
                                    _ride_chip_exchange([p16 for _, p16 in p_small]))
    rs_ab, rs_pb, rs_out = _chip_sum_small([p32 for p32, _ in p_small], u_small, chip_half)
    dqkv2, ds2, _ = _attn_bwd(qkv[2], dattn, stats, bias_tab, 2, None)

    for j, piece in enumerate(dqkv0 + dqkv1 + dqkv2):
        dproj = lax.dynamic_update_slice(dproj, piece.astype(BF16), (0, j * AW))
    dproj = lax.dynamic_update_slice(dproj, du, (0, QKV_W + AW))
    dw_in = _dw_in(h, dproj)
    drel_rows, (sib_in,) = _bias_grad(jnp.concatenate([ds0, ds1, ds2], axis=0), buckets,
                                      _ride_sibling_halves([dw_in]))
    drel = drel_rows[:, 0, :NUM_BUCKETS].T
    p32_in, p16_in = _pair_sum(dw_in, sib_in, chip_half, "rs_pair_sum_in")
    dh, (u_in, _) = _dh(dproj, wg_in, _ride_chip_exchange_relayed([p16_in]))
    rs_in = _chip_sum(p32_in, u_in, chip_half, "rs_chip_sum_in")

    grad_x, dnorm_g, dshift, dscale = _prenorm_bwd(x, dh, dx2, norm_g, mod)
    dmod = jnp.concatenate([dshift, dscale, dgate], axis=1)
    return dict(loss=loss[0, 0], grad_x=grad_x, dmod=dmod, dnorm_g=dnorm_g, dfinal_g=dfinal_g, dpool_w=dpool_w,
                dpool_scale=dpool_scale, drel_bias=drel, dw_in=dw_in, dw_attn_br=dw_ab, dw_pool_br=dw_pb,
                dw_out=dw_out, rs_in=rs_in, rs_attn_br=rs_ab, rs_pool_br=rs_pb, rs_out=rs_out)


def _allgather8(blocks, name, relay=None):
    nb = len(blocks)
    relay = [False] * nb if relay is None else list(relay)

    def body(*refs):
        ins, outs = refs[:nb], refs[nb:2 * nb]
        send_sems, recv_sems = refs[2 * nb:]
        x, y, c = lax.axis_index("x"), lax.axis_index("y"), lax.axis_index("c")
        me, sibling = (x, y, c), (x, y, 1 - c)
        here, xn, yn, dg = (x, y), (1 - x, y), (x, 1 - y), (1 - x, 1 - y)

        def slot(a, chip, core, half=None):
            ref = outs[a].at[4 * chip[0] + 2 * chip[1] + core]
            if half is None:
                return ref
            r2 = ref.shape[0] // 2
            return ref.at[pl.ds(half * r2, r2)]

        def copy(a, k, dst, to, src=None):
            return pltpu.make_async_remote_copy(src_ref=dst if src is None else src, dst_ref=dst,
                                                send_sem=send_sems.at[a, k], recv_sem=recv_sems.at[a, k],
                                                device_id=to, device_id_type=MESH)

        def start(cps):
            for cp in cps:
                cp.start()
            return cps

        sent = []
        for a in range(nb):
            own = slot(a, here, c)
            sent += [copy(a, 0, own, sibling, src=ins[a]), copy(a, 1, own, (*xn, c), src=ins[a]),
                     copy(a, 2, own, (*yn, c), src=ins[a])]
            if not relay[a]:
                sent.append(copy(a, 3, own, (*dg, c), src=ins[a]))
        start(sent)
        for a in range(nb):
            copy(a, 2, slot(a, yn, c), me).wait_recv()
            sent += start([copy(a, 6, slot(a, yn, c), sibling)]
                          + ([copy(a, 3, slot(a, yn, c, 0), (*xn, c))] if relay[a] else []))
        for a in range(nb):
            copy(a, 1, slot(a, xn, c), me).wait_recv()
            sent += start([copy(a, 5, slot(a, xn, c), sibling)]
                          + ([copy(a, 4, slot(a, xn, c, 1), (*yn, c))] if relay[a] else []))
        for a in range(nb):
            for k, half in ((3, 0), (4, 1)) if relay[a] else ((3, None),):
                copy(a, k, slot(a, dg, c, half), me).wait_recv()
                sent += start([copy(a, 4 + k, slot(a, dg, c, half), sibling)])
        for a in range(nb):
            copy(a, 0, slot(a, here, 1 - c), me).wait_recv()
            copy(a, 5, slot(a, xn, 1 - c), me).wait_recv()
            copy(a, 6, slot(a, yn, 1 - c), me).wait_recv()
            for k, half in ((7, 0), (8, 1)) if relay[a] else ((7, None),):
                copy(a, k, slot(a, dg, 1 - c, half), me).wait_recv()
        for cp in sent:
            cp.wait_send()

    outs = pl.pallas_call(
        body, name=name, in_specs=[ANY] * nb, out_specs=[ANY] * nb,
        out_shape=[_sds((8,) + b.shape, b.dtype) for b in blocks],
        scratch_shapes=[_dma_sems(nb, 9), _dma_sems(nb, 9)],
    )(*blocks)
    return [_place_own(buf, b) for buf, b in zip(outs, blocks)]


def _place_own(buf, block):
    dev = 4 * lax.axis_index("x") + 2 * lax.axis_index("y") + lax.axis_index("c")
    return lax.dynamic_update_index_in_dim(buf, block, dev, 0)


def _ride_sibling_halves(gs):
    def copies(ins, outs, send_sems, recv_sems):
        x, y, c = lax.axis_index("x"), lax.axis_index("y"), lax.axis_index("c")
        cps = []
        for a in range(len(gs)):
            r2 = ins[a].shape[1] // 2
            other = ins[a].at[:, pl.ds((1 - c) * r2, r2), :]
            cps.append(pltpu.make_async_remote_copy(src_ref=other, dst_ref=outs[a], send_sem=send_sems.at[a],
                                                    recv_sem=recv_sems.at[a], device_id=(x, y, 1 - c),
                                                    device_id_type=MESH))
        return cps

    return _Ride(gs, [_sds((g.shape[0], g.shape[1] // 2, g.shape[2]), g.dtype) for g in gs], len(gs), copies)


def _pair_sum(g, t, chip_half, name):
    nsh, rows, cols = g.shape
    r2 = rows // 2
    tr = _row_tile(r2, cols)
    nt = r2 // tr

    def body(ch_ref, g_ref, t_ref, p32_ref, p16_ref):
        p = g_ref[...] + t_ref[...]
        p16_ref[...] = p.astype(BF16)

        @pl.when(pl.program_id(1) == ch_ref[0])
        def _():
            p32_ref[...] = p

    blk = pl.BlockSpec((None, tr, cols), lambda i, k, ch_ref: (k, i, 0))
    return pl.pallas_call(
        body, name=name,
        grid_spec=pltpu.PrefetchScalarGridSpec(
            num_scalar_prefetch=1, grid=(nt, nsh),
            in_specs=[pl.BlockSpec((None, tr, cols), lambda i, k, ch_ref: (k, ch_ref[1] * nt + i, 0)), blk],
            out_specs=[pl.BlockSpec((tr, cols), lambda i, k, ch_ref: (i, 0)), blk]),
        out_shape=[_sds((r2, cols)), _sds((nsh, r2, cols), BF16)],
        compiler_params=_params("parallel", "arbitrary"),
    )(chip_half, g, t)


def _pair_sum_small(gs, ts, chip_half):
    na = len(gs)

    def body(ch_ref, *refs):
        g_refs, t_refs, outs = refs[:na], refs[na:2 * na], refs[2 * na:]
        for a in range(na):
            r2 = t_refs[a].shape[1]
            own = pl.ds(pl.multiple_of(ch_ref[1] * r2, 8), r2)
            outs[2 * a + 1][...] = (g_refs[a][:, own, :] + t_refs[a][...]).astype(BF16)
            outs[2 * a][...] = g_refs[a][ch_ref[0], own, :] + t_refs[a][ch_ref[0]]

    res = pl.pallas_call(
        body, name="rs_pair_sum_small",
        in_specs=[pl.BlockSpec(memory_space=pltpu.SMEM)] + [pl.BlockSpec(memory_space=pltpu.VMEM)] * (2 * na),
        out_shape=[s for t in ts for s in (_sds(t.shape[1:]), _sds(t.shape, BF16))], compiler_params=_params(),
    )(chip_half, *gs, *ts)
    return [(res[2 * a], res[2 * a + 1]) for a in range(na)]


def _chip_sum_small(p32s, us, chip_half):
    na = len(p32s)

    def body(ch_ref, *refs):
        p_refs, u_refs, outs = refs[:na], refs[na:2 * na], refs[2 * na:]
        for a in range(na):
            r2 = p_refs[a].shape[0]
            acc = p_refs[a][...]
            for j in range(3):
                acc = acc + u_refs[a][j].astype(F32)
            outs[a][pl.ds(pl.multiple_of(ch_ref[1] * r2, 8), r2), :] = acc

    return pl.pallas_call(
        body, name="rs_chip_sum_small",
        in_specs=[pl.BlockSpec(memory_space=pltpu.SMEM)] + [pl.BlockSpec(memory_space=pltpu.VMEM)] * (2 * na),
        out_shape=[_sds((2 * p.shape[0], p.shape[1])) for p in p32s], compiler_params=_params(),
    )(chip_half, *p32s, *us)


def _ride_chip_exchange(ps):
    def copies(ins, outs, send_sems, recv_sems):
        x, y, c = lax.axis_index("x"), lax.axis_index("y"), lax.axis_index("c")
        chips = [(1 - x, y), (x, 1 - y), (1 - x, 1 - y)]
        cps = []
        for a in range(len(ps)):
            for j, (ox, oy) in enumerate(chips):
                cps.append(pltpu.make_async_remote_copy(src_ref=ins[a].at[2 * ox + oy], dst_ref=outs[a].at[j],
                                                        send_sem=send_sems.at[3 * a + j],
                                                        recv_sem=recv_sems.at[3 * a + j],
                                                        device_id=(ox, oy, c), device_id_type=MESH))
        return cps

    return _Ride(ps, [_sds((3,) + p.shape[1:], p.dtype) for p in ps], 3 * len(ps), copies)


def _ride_chip_exchange_relayed(ps):
    na = len(ps)

    def descriptors(ins, outs, send_sems, recv_sems):
        x, y, c = lax.axis_index("x"), lax.axis_index("y"), lax.axis_index("c")
        xn, yn, dg = (1 - x, y), (x, 1 - y), (1 - x, 1 - y)
        first, relays = [], []
        for a in range(na):
            u, relay = outs[a], outs[na + a]
            h2 = relay.shape[1]

            def copy(k, src, dst, chip):
                return pltpu.make_async_remote_copy(src_ref=src, dst_ref=dst, send_sem=send_sems.at[6 * a + k],
                                                    recv_sem=recv_sems.at[6 * a + k], device_id=(*chip, c),
                                                    device_id_type=MESH)

            diag = ins[a].at[2 * dg[0] + dg[1]]
            first += [copy(0, ins[a].at[2 * xn[0] + xn[1]], u.at[0], xn), copy(1, ins[a].at[2 * yn[0] + yn[1]], u.at[1], yn),
                      copy(2, diag.at[pl.ds(0, h2)], relay.at[0], xn), copy(3, diag.at[pl.ds(h2, h2)], relay.at[1], yn)]
            relays += [(copy(2, relay.at[0], relay.at[0], xn), copy(4, relay.at[0], u.at[2, pl.ds(0, h2)], yn)),
                       (copy(3, relay.at[1], relay.at[1], yn), copy(5, relay.at[1], u.at[2, pl.ds(h2, h2)], xn))]
        return first, relays

    return _Ride(ps, [_sds((3,) + p.shape[1:], p.dtype) for p in ps]
                 + [_sds((2, p.shape[1] // 2) + p.shape[2:], p.dtype) for p in ps], 6 * na,
                 lambda *refs: descriptors(*refs)[0], relays=lambda *refs: descriptors(*refs)[1],
                 landed_only=[4 * a + k for a in range(na) for k in (2, 3)])


def _chip_sum(p32, u, chip_half, name):
    r2, cols = p32.shape
    tr = _row_tile(r2, cols)
    nt = r2 // tr

    def body(ch_ref, p_ref, u_ref, o_ref):
        acc = p_ref[...]
        for j in range(3):
            acc = acc + u_ref[j].astype(F32)
        o_ref[...] = acc

    return pl.pallas_call(
        body, name=name,
        grid_spec=pltpu.PrefetchScalarGridSpec(
            num_scalar_prefetch=1, grid=(nt,),
            in_specs=[pl.BlockSpec((tr, cols), lambda i, ch_ref: (i, 0)),
                      pl.BlockSpec((3, tr, cols), lambda i, ch_ref: (0, i, 0))],
            out_specs=pl.BlockSpec((tr, cols), lambda i, ch_ref: (ch_ref[1] * nt + i, 0))),
        out_shape=_sds((2 * r2, cols)), compiler_params=_params("parallel"),
    )(chip_half, p32, u)


def _sibling_join(fs, name):
    nb = len(fs)

    def body(*refs):
        outs = refs[nb:2 * nb]
        send_sems, recv_sems = refs[2 * nb:]
        x, y, c = lax.axis_index("x"), lax.axis_index("y"), lax.axis_index("c")
        cps = []
        for a in range(nb):
            r2 = outs[a].shape[0] // 2
            rows = outs[a].at[pl.ds(c * r2, r2), :]
            cps.append(pltpu.make_async_remote_copy(src_ref=rows, dst_ref=rows, send_sem=send_sems.at[a],
                                                    recv_sem=recv_sems.at[a], device_id=(x, y, 1 - c),
                                                    device_id_type=MESH))
        for cp in cps:
            cp.start()
        for cp in cps:
            cp.wait()

    return pl.pallas_call(
        body, name=name, in_specs=[ANY] * nb, out_specs=[ANY] * nb,
        out_shape=[_sds(f.shape, f.dtype) for f in fs],
        input_output_aliases={a: a for a in range(nb)},
        scratch_shapes=[_dma_sems(nb), _dma_sems(nb)],
    )(*fs)


def _row_tile(rows, cols):
    tile = rows
    while tile * cols * 4 > (1 << 20) and tile % 16 == 0:
        tile //= 2
    return tile


def _w_ada_grad(c_all, dmod_cols):
    def body(c_ref, d_ref, o_ref):
        o_ref[...] = _dot_tn(c_ref[...].astype(BF16), d_ref[...].astype(BF16))

    return pl.pallas_call(body, name="w_ada_grad", out_shape=_sds((c_all.shape[1], dmod_cols.shape[1])),
                          compiler_params=_params())(c_all, dmod_cols)


def _adam_math(w, g, m, v):
    nm = ADAM_B1 * m + (1.0 - ADAM_B1) * g
    nv = ADAM_B2 * v + (1.0 - ADAM_B2) * (g * g)
    m_hat = nm / (1.0 - ADAM_B1 ** ADAM_STEP)
    v_hat = nv / (1.0 - ADAM_B2 ** ADAM_STEP)
    return -ADAM_LR * (m_hat / (jnp.sqrt(v_hat) + ADAM_EPS) + ADAM_WD * w), nm, nv


def _adamw(w, g, m, v, name):
    rows, cols = w.shape
    tr = _row_tile(rows, cols)

    def body(w_ref, g_ref, m_ref, v_ref, go_ref, d_ref, nm_ref, nv_ref):
        gv = g_ref[...]
        go_ref[...] = gv
        d_ref[...], nm_ref[...], nv_ref[...] = _adam_math(w_ref[...], gv, m_ref[...], v_ref[...])

    spec = pl.BlockSpec((tr, cols), lambda i: (i, 0))
    return pl.pallas_call(
        body, name=name, grid=(rows // tr,), in_specs=[spec] * 4, out_specs=[spec] * 4,
        out_shape=[_sds((rows, cols))] * 4, compiler_params=_params("parallel"),
    )(w, g, m, v)


def _pack_small(dmod, dnorm_g, dfinal_g, dpool_scale, drel_bias, loss, dpool_w):
    return jnp.concatenate([dmod.reshape(-1, 128), dnorm_g.reshape(-1, 128), dfinal_g.reshape(-1, 128),
                            jnp.pad(dpool_scale.reshape(-1, 128), ((0, PK_RELB - PK_PSCALE - AW // 128), (0, 0))),
                            jnp.pad(drel_bias, ((0, 0), (0, 128 - NG * NH))),
                            jnp.full((PK_POOLW - PK_LOSS, 128), loss, F32), dpool_w.reshape(-1, 128)], axis=0)


def _small_update(small_all, ws, ms, vs):
    lane_rows = [(r0, r0 + w.shape[1] // 128) for r0, w in zip((PK_BADA, PK_NORMG, PK_FINALG, PK_PSCALE), ws)]
    nw = len(ws)

    def body(all_ref, *refs):
        w_refs, m_refs, v_refs = refs[:nw], refs[nw:2 * nw], refs[2 * nw:3 * nw]
        loss_ref, outs = refs[3 * nw], refs[3 * nw + 1:]
        g = all_ref[0]
        for s in range(1, all_ref.shape[0]):
            g = g + all_ref[s]
        loss_ref[...] = jnp.broadcast_to(g[PK_LOSS:PK_LOSS + 1, :], loss_ref.shape)

        def put(p, at, gv):
            d, nm, nv = _adam_math(w_refs[p][at], gv, m_refs[p][at], v_refs[p][at])
            for o_ref, val in zip(outs[4 * p:4 * p + 4], (gv, d, nm, nv)):
                o_ref[at] = val

        for p, (r0, r1) in enumerate(lane_rows):
            for i in range(r1 - r0):
                put(p, (slice(None), slice(128 * i, 128 * (i + 1))), g[r0 + i:r0 + i + 1, :])
        put(4, (slice(None), slice(None)), g[PK_RELB:PK_LOSS, 0:NG * NH])
        put(5, (slice(None), slice(None)), g[PK_POOLW:PK_ROWS, :])

    res = pl.pallas_call(
        body, name="small_update",
        out_shape=[_sds((8, 128))] + [_sds(w.shape) for w in ws for _ in range(4)], compiler_params=_params(),
    )(small_all, *ws, *ms, *vs)
    return res[0], [res[1 + 4 * p:5 + 4 * p] for p in range(nw)]


def kernel(x, c, norm_g, w_ada, b_ada, w_in, pool_w, pool_scale, w_attn_br, w_pool_br, w_out, rel_bias, final_g, loss_target, m_norm_g, m_w_ada, m_b_ada, m_w_in, m_pool_w, m_pool_scale, m_w_attn_br, m_w_pool_br, m_w_out, m_rel_bias, m_final_g, v_norm_g, v_w_ada, v_b_ada, v_w_in, v_pool_w, v_pool_scale, v_w_attn_br, v_w_pool_br, v_w_out, v_rel_bias, v_final_g):
    ix, iy, ic = lax.axis_index("x"), lax.axis_index("y"), lax.axis_index("c")
    dev = 4 * ix + 2 * iy + ic
    chip = 2 * ix + iy

    def half(w):
        r2 = w.shape[0] // 2
        return lax.dynamic_slice_in_dim(w, ic * r2, r2, axis=0).astype(BF16)

    gathered = _allgather8([jnp.broadcast_to(c, (8, D)), half(w_in[0]), half(w_attn_br[0]), half(w_pool_br[0]),
                            half(w_out[0])], "gather_weights", relay=[False, True, True, True, True])
    c_all = gathered[0][:, 0, :]
    wg_in = gathered[1].reshape(N_SHARD, D, IN_W // N_SHARD)
    wab = gathered[2].reshape(N_SHARD, AW, D // N_SHARD).transpose(1, 0, 2).reshape(AW, D)
    wpb = gathered[3].reshape(N_SHARD, AW, D // N_SHARD).transpose(1, 0, 2).reshape(AW, D)
    wout = gathered[4].reshape(D, D)

    mw = 3 * D // N_SHARD
    modp = _mod_partial(c_all, w_ada[0], lax.dynamic_slice_in_dim(b_ada, chip * mw, mw, axis=1))
    mod_all = _allgather8([modp], "gather_mod")[0]
    mod_full = mod_all[::2].transpose(1, 0, 2).reshape(8, 3 * D)
    mod = lax.dynamic_slice_in_dim(mod_full, dev, 1, axis=0)

    chip_half = jnp.stack([chip, ic]).astype(jnp.int32)
    r = _local_step(x[0], loss_target[0], mod, wg_in, wab, wpb, wout, pool_w[0], pool_scale, rel_bias, norm_g,
                    final_g.reshape(1, D), chip_half)

    packed = _pack_small(r["dmod"], r["dnorm_g"], r["dfinal_g"], r["dpool_scale"], r["drel_bias"], r["loss"],
                         r["dpool_w"])
    small_all = _allgather8([packed], "gather_small")[0]
    small = ["b_ada", "norm_g", "final_g", "pool_scale", "rel_bias", "pool_w"]
    shaped = lambda b, n, f, ps, rb, pw: [b, n, f.reshape(1, D), ps, rb, pw.reshape(4 * PGW, PGW)]
    loss, small_out = _small_update(small_all, shaped(b_ada, norm_g, final_g, pool_scale, rel_bias, pool_w),
                                    shaped(m_b_ada, m_norm_g, m_final_g, m_pool_scale, m_rel_bias, m_pool_w),
                                    shaped(v_b_ada, v_norm_g, v_final_g, v_pool_scale, v_rel_bias, v_pool_w))
    dmod_all = small_all[:, PK_BADA:PK_NORMG, :].reshape(8, 3 * D)
    g_w_ada = _w_ada_grad(c_all, lax.dynamic_slice_in_dim(dmod_all, chip * mw, mw, axis=1))

    g_w_in, g_w_ab, g_w_pb, g_w_out = _sibling_join([r["rs_in"], r["rs_attn_br"], r["rs_pool_br"], r["rs_out"]],
                                                    "rs_sibling_join")
    upd = dict(zip(small, small_out))
    upd["final_g"] = [a.reshape(D) for a in upd["final_g"]]
    upd["pool_w"] = [a.reshape(1, 4, PGW, PGW) for a in upd["pool_w"]]
    for nme, w, g, m, v in (("w_ada", w_ada, g_w_ada, m_w_ada, v_w_ada), ("w_in", w_in, g_w_in, m_w_in, v_w_in),
                            ("w_attn_br", w_attn_br, g_w_ab, m_w_attn_br, v_w_attn_br),
                            ("w_pool_br", w_pool_br, g_w_pb, m_w_pool_br, v_w_pool_br),
                            ("w_out", w_out, g_w_out, m_w_out, v_w_out)):
        upd[nme] = [a[None] for a in _adamw(w[0], g, m[0], v[0], "adamw_" + nme)]
    names = ["norm_g", "w_ada", "b_ada", "w_in", "pool_w", "pool_scale", "w_attn_br", "w_pool_br", "w_out",
             "rel_bias", "final_g"]
    return (loss[0, 0], r["grad_x"][None]) + tuple(upd[nme][kind] for kind in range(4) for nme in names)
```

```python
import math

import numpy as np
import jax
import jax.numpy as jnp
from jax import lax
from jax.experimental import pallas as pl
from jax.experimental.pallas import tpu as pltpu

F32 = jnp.float32
BF16 = jnp.bfloat16

D = 1024
HD = 64
NH = 8
AW = NH * HD
GROUPS = ((128, 1), (512, 4), (2048, 16))
NG = len(GROUPS)
BLK = 128
GW = 3 * AW
QKV_W = NG * GW
REST_W = 3584
IN_W = QKV_W + REST_W
CB = 512
NCB_QKV = QKV_W // CB
POOL_WINDOWS = (2, 4, 8, 16)
PGW = 128
HALO = 16
NUM_BUCKETS = 32
MAX_DISTANCE = 2048
EPS = 1e-6
NEG = -1e30
N_SHARD = 4
VMEM_LIMIT = 56 * 1024 * 1024

ADAM_LR = 0.001
ADAM_B1 = 0.9
ADAM_B2 = 0.999
ADAM_EPS = 1e-08
ADAM_WD = 0.01
ADAM_STEP = 10

PK_BADA, PK_NORMG, PK_FINALG, PK_PSCALE, PK_RELB, PK_LOSS, PK_POOLW, PK_ROWS = 0, 24, 32, 40, 48, 80, 88, 600

ANY = pl.BlockSpec(memory_space=pl.ANY)
MESH = pl.DeviceIdType.MESH


def _params(*sem):
    return pltpu.CompilerParams(dimension_semantics=sem, vmem_limit_bytes=VMEM_LIMIT)


def _sds(shape, dtype=F32):
    return jax.ShapeDtypeStruct(shape, dtype)


def _dot(a, b):
    return jnp.dot(a, b, preferred_element_type=F32)


def _dot_nt(a, b):
    return lax.dot_general(a, b, (((1,), (1,)), ((), ())), preferred_element_type=F32)


def _dot_tn(a, b):
    return lax.dot_general(a, b, (((0,), (0,)), ((), ())), preferred_element_type=F32)


def _sigmoid(z):
    return 0.5 * jnp.tanh(0.5 * z) + 0.5


def _dma_sems(*shape):
    return pltpu.SemaphoreType.DMA(shape)


class _Ride:
    def __init__(self, arrays, out_shapes, n_copies, copies):
        self.arrays, self.out_shapes, self.n_copies, self.copies = list(arrays), list(out_shapes), n_copies, copies


def _call_with_ride(body, ride, first, last, *, in_specs, out_specs, out_shape, scratch_shapes=(), **kw):
    in_specs, out_specs, out_shape, scratch_shapes = list(in_specs), list(out_specs), list(out_shape), list(scratch_shapes)
    n_in, n_out, n_sc = len(in_specs), len(out_specs), len(scratch_shapes)
    if ride is None:
        def run_plain(*operands):
            return pl.pallas_call(body, in_specs=in_specs, out_specs=out_specs, out_shape=out_shape,
                                  scratch_shapes=scratch_shapes, **kw)(*operands), []
        return run_plain
    n_ri, n_ro = len(ride.arrays), len(ride.out_shapes)

    def wrapped(*refs):
        ins, rest = refs[:n_in], refs[n_in:]
        r_ins, rest = rest[:n_ri], rest[n_ri:]
        outs, rest = rest[:n_out], rest[n_out:]
        r_outs, rest = rest[:n_ro], rest[n_ro:]
        scratch, (send_sems, recv_sems) = rest[:n_sc], rest[n_sc:]

        @pl.when(first())
        def _():
            for cp in ride.copies(r_ins, r_outs, send_sems, recv_sems):
                cp.start()

        body(*ins, *outs, *scratch)

        @pl.when(last())
        def _():
            for cp in ride.copies(r_ins, r_outs, send_sems, recv_sems):
                cp.wait()

    def run(*operands):
        res = pl.pallas_call(
            wrapped, in_specs=in_specs + [ANY] * n_ri, out_specs=out_specs + [ANY] * n_ro,
            out_shape=out_shape + ride.out_shapes,
            scratch_shapes=scratch_shapes + [_dma_sems(ride.n_copies), _dma_sems(ride.n_copies)], **kw,
        )(*operands, *ride.arrays)
        return res[:n_out], res[n_out:]
    return run


def _bucket_tables():
    i = np.arange(BLK)[:, None]
    j = np.arange(2 * BLK)[None, :]
    dist = BLK + i - j
    valid = (dist >= 0) & (dist <= BLK)
    tabs = []
    for _, dil in GROUPS:
        n = (np.clip(dist, 0, BLK) * dil).astype(np.int32)
        max_exact = NUM_BUCKETS // 2
        nf = np.maximum(n, 1).astype(np.float32)
        large = max_exact + (np.log(nf / np.float32(max_exact)) / np.float32(math.log(MAX_DISTANCE / max_exact))
                             * np.float32(NUM_BUCKETS - max_exact)).astype(np.int32)
        large = np.minimum(large, NUM_BUCKETS - 1)
        bucket = np.where(n < max_exact, n, large)
        tab = np.where(valid, bucket, -1).astype(np.int32)
        perm = _block_perm(dil)
        tabs.append(tab[perm][:, np.concatenate([perm, BLK + perm])])
    return np.stack(tabs)


def _bias_table(rel_bias, buckets):
    def body(rb_ref, bk_ref, out_ref):
        g = pl.program_id(0)
        bk = bk_ref[...]
        for h in range(NH):
            acc = jnp.full((BLK, 2 * BLK), NEG, F32)
            for b in range(NUM_BUCKETS):
                acc = jnp.where(bk == b, rb_ref[b, g * NH + h], acc)
            out_ref[h] = acc

    return pl.pallas_call(
        body, name="bias_table", grid=(NG,),
        in_specs=[pl.BlockSpec(memory_space=pltpu.SMEM),
                  pl.BlockSpec((None, BLK, 2 * BLK), lambda g: (g, 0, 0))],
        out_specs=pl.BlockSpec((NH, BLK, 2 * BLK), lambda g: (g, 0, 0)),
        out_shape=_sds((NG * NH, BLK, 2 * BLK)),
        compiler_params=_params("arbitrary"),
    )(rel_bias, buckets)


def _bias_grad(ds_acc, buckets, ride):
    def body(acc_ref, bk_ref, out_ref):
        bk = bk_ref[...]
        acc = acc_ref[...]
        lane = lax.broadcasted_iota(jnp.int32, (8, 128), 1)
        out = jnp.zeros((8, 128), F32)
        for b in range(NUM_BUCKETS):
            val = jnp.sum(jnp.where(bk == b, acc, 0.0))
            out = jnp.where(lane == b, val, out)
        out_ref[...] = out

    (out,), rode = _call_with_ride(
        body, ride, lambda: pl.program_id(0) == 0, lambda: pl.program_id(0) == NG * NH - 1,
        name="bias_grad", grid=(NG * NH,),
        in_specs=[pl.BlockSpec((None, BLK, 2 * BLK), lambda gh: (gh, 0, 0)),
                  pl.BlockSpec((None, BLK, 2 * BLK), lambda gh: (gh // NH, 0, 0))],
        out_specs=[pl.BlockSpec((None, 8, 128), lambda gh: (gh, 0, 0))],
        out_shape=[_sds((NG * NH, 8, 128))],
        compiler_params=_params("arbitrary"),
    )(ds_acc, buckets)
    return out, rode


def _mod_partial(c_all, w_ada_s, b_ada_s):
    def body(c_ref, w_ref, b_ref, o_ref):
        o_ref[...] = _dot(c_ref[...].astype(BF16), w_ref[...].astype(BF16)) + b_ref[...]

    return pl.pallas_call(body, name="mod_partial", out_shape=_sds((8, w_ada_s.shape[1])),
                          compiler_params=_params())(c_all, w_ada_s, b_ada_s)


def _prenorm(x, norm_g, mod):
    S = x.shape[0]
    tm = 512

    def body(x_ref, g_ref, mod_ref, h_ref):
        xv = x_ref[...]
        r = lax.rsqrt(jnp.mean(xv * xv, axis=-1, keepdims=True) + EPS)
        n1 = xv * r * g_ref[...]
        h_ref[...] = (n1 * (1.0 + mod_ref[:, D:2 * D]) + mod_ref[:, 0:D]).astype(BF16)

    return pl.pallas_call(
        body, name="prenorm", grid=(S // tm,),
        in_specs=[pl.BlockSpec((tm, D), lambda i: (i, 0)), pl.BlockSpec((1, D), lambda i: (0, 0)),
                  pl.BlockSpec((1, 3 * D), lambda i: (0, 0))],
        out_specs=pl.BlockSpec((tm, D), lambda i: (i, 0)),
        out_shape=_sds((S, D), BF16), compiler_params=_params("parallel"),
    )(x, norm_g, mod)


def _proj(h, wg_in, j0, nj, dtype, name):
    S = h.shape[0]
    tm = 2048
    per = wg_in.shape[2] // CB

    def body(h_ref, w_ref, o_ref):
        o_ref[...] = _dot(h_ref[...], w_ref[...]).astype(dtype)

    return pl.pallas_call(
        body, name=name, grid=(S // tm, nj),
        in_specs=[pl.BlockSpec((tm, D), lambda m, j: (m, 0)),
                  pl.BlockSpec((None, D, CB), lambda m, j: ((j0 + j) // per, 0, (j0 + j) % per))],
        out_specs=pl.BlockSpec((tm, CB), lambda m, j: (m, j)),
        out_shape=_sds((S, nj * CB), dtype), compiler_params=_params("parallel", "parallel"),
    )(h, wg_in)


HS = 4
SLAB = HS * HD


def _lane_head(rows):
    return lax.broadcasted_iota(jnp.int32, (rows, SLAB), 1) // HD


def _head_stack(a):
    head = _lane_head(a.shape[0])
    return jnp.concatenate([jnp.where(head == h, a, jnp.zeros_like(a)) for h in range(HS)], axis=0)


def _head_unstack(a):
    rows = a.shape[0] // HS
    head = _lane_head(rows)
    out = a[:rows]
    for h in range(1, HS):
        out = jnp.where(head == h, a[h * rows:(h + 1) * rows], out)
    return out


STAT_W = 128
VIEW = 16


def _sub_layout(dil):
    if dil == 1:
        return BLK, [None]
    return BLK * dil // VIEW, [[r + dil * u for u in range(VIEW // dil)] for r in range(dil)]


def _block_perm(dil):
    a_rows, _ = _sub_layout(dil)
    p = np.arange(BLK)
    return p if dil == 1 else (VIEW // dil) * (p % a_rows) + p // a_rows


LB = 128
N_SLAB = NH // HS


RBS = 2


def _ld(refs, bs, s, w, rb=0):
    if bs is None:
        return refs[0][rb * BLK:(rb + 1) * BLK, s * w:(s + 1) * w]
    a_rows = refs[0].shape[0] // VIEW
    return jnp.concatenate([jnp.concatenate([ref[pl.ds(b, a_rows, stride=VIEW), :] for b in bs], axis=0)
                            for ref in refs], axis=1)


def _st(ref, bs, s, val, rb=0):
    if bs is None:
        ref[rb * BLK:(rb + 1) * BLK, s * SLAB:(s + 1) * SLAB] = val.astype(ref.dtype)
        return
    a_rows = val.shape[0] // len(bs)
    for u, b in enumerate(bs):
        ref[:, b, :] = val[u * a_rows:(u + 1) * a_rows]


def _attn_views(dil, S):
    a_rows, subs = _sub_layout(dil)
    if dil == 1:
        def ispecs(base, w, f):
            return [pl.BlockSpec((RBS * BLK, N_SLAB * w), lambda sg, n: (f(n), base // (N_SLAB * w)))]
        return subs, S // (RBS * BLK), N_SLAB, RBS, ispecs, (lambda w: (S, w)), (
            lambda f: pl.BlockSpec((RBS * BLK, AW), lambda sg, n: (f(n), 0)))

    def ispecs(base, w, f):
        return [pl.BlockSpec((a_rows * VIEW, LB), lambda sg, n, k=k: (f(n), (base + sg * w) // LB + k))
                for k in range(w // LB)]
    return subs, S // (a_rows * VIEW), 1, 1, ispecs, (lambda w: (S // VIEW, VIEW, w)), (
        lambda f: pl.BlockSpec((a_rows, VIEW, SLAB), lambda sg, n: (f(n), 0, sg)))


def _attn_fwd(qkv_g, bias_tab, g):
    S = qkv_g.shape[0]
    subs, nbq, sps, rbs, ispecs, shape, ospec = _attn_views(GROUPS[g][1], S)
    cur = lambda n: n
    in_specs = [ispecs(0, SLAB, cur), ispecs(AW, SLAB, cur), ispecs(2 * AW, SLAB, cur)]
    nl = len(in_specs[0])

    def body(*refs):
        q, k, v = (refs[t * nl:(t + 1) * nl] for t in range(3))
        b_ref, o_ref, l_ref, kprev, vprev = refs[3 * nl:]
        n = pl.program_id(1)

        @pl.when(n == 0)
        def _():
            kprev[...] = jnp.zeros_like(kprev)
            vprev[...] = jnp.zeros_like(vprev)

        col = lax.broadcasted_iota(jnp.int32, (HS * BLK, 2 * BLK), 1)
        first = (col >= BLK) | (n > 0)
        for s_, rb, (i, bs) in ((s_, rb, sub) for s_ in range(sps) for rb in range(rbs) for sub in enumerate(subs)):
            cs = slice(s_ * SLAB, (s_ + 1) * SLAB)
            kc, vc = _ld(k, bs, s_, SLAB, rb).astype(BF16), _ld(v, bs, s_, SLAB, rb).astype(BF16)
            kb = jnp.concatenate([kprev[i, :, cs], kc], axis=0)
            vb = jnp.concatenate([vprev[i, :, cs], vc], axis=0)
            kprev[i, :, cs], vprev[i, :, cs] = kc, vc
            s = _dot_nt(_head_stack(_ld(q, bs, s_, SLAB, rb).astype(BF16)), kb) * (HD ** -0.5)
            s = s + b_ref[pl.ds(s_ * HS, HS)].reshape(HS * BLK, 2 * BLK)
            if rb == 0:
                s = jnp.where(first, s, NEG)
            m = jnp.max(s, axis=-1, keepdims=True)
            p = jnp.exp(s - m)
            den = jnp.sum(p, axis=-1, keepdims=True)
            _st(o_ref, bs, s_, _head_unstack(_dot(p.astype(BF16), vb) / den), rb)
            _st(l_ref, bs, s_, _head_unstack(jnp.broadcast_to(m + jnp.log(den), (HS * BLK, SLAB))), rb)

    out = _sds(shape(AW))
    nsg = N_SLAB // sps
    o, l = pl.pallas_call(
        body, name=f"attn_fwd{g}", grid=(nsg, nbq),
        in_specs=sum(in_specs, []) + [pl.BlockSpec((sps * HS, BLK, 2 * BLK), lambda sg, n: (g * nsg + sg, 0, 0))],
        out_specs=[ospec(cur), ospec(cur)],
        out_shape=[out, out],
        scratch_shapes=[pltpu.VMEM((len(subs), BLK, sps * SLAB), BF16)] * 2,
        compiler_params=_params("parallel", "arbitrary"),
    )(*([qkv_g] * (3 * nl)), bias_tab)
    return o.reshape(S, AW), l.reshape(S, AW)


def _attn_bwd(qkv_g, dattn, stats, bias_tab, g, ride):
    S = qkv_g.shape[0]
    subs, nbq, sps, rbs, ispecs, shape, ospec = _attn_views(GROUPS[g][1], S)
    cur = lambda n: jnp.minimum(n, nbq - 1)
    late = lambda n: jnp.maximum(n - 1, 0)
    in_specs = [ispecs(0, SLAB, cur), ispecs(AW, SLAB, cur), ispecs(2 * AW, SLAB, cur), ispecs(0, SLAB, cur),
                ispecs(0, STAT_W, cur)]
    nl = len(in_specs[0])

    def body(*refs):
        q, k, v, da = (refs[t * nl:(t + 1) * nl] for t in range(4))
        st_ref, b_ref, dq_ref, dk_ref, dv_ref, ds_ref, ck_ref, cv_ref, kprev, vprev, *held = refs[4 * nl:]
        n = pl.program_id(1)

        @pl.when(n == 0)
        def _():
            for ref in (ds_ref, ck_ref, cv_ref, kprev, vprev, *held):
                ref[...] = jnp.zeros_like(ref)

        def finish(ref, t, bs, s_, rb, val):
            if rbs == 1:
                _st(ref, bs, s_, val)
            elif rb == 0:
                _st(ref, bs, s_, held[t][:, s_ * SLAB:(s_ + 1) * SLAB], 0)
                _st(ref, bs, s_, val, 1)
            else:
                held[t][:, s_ * SLAB:(s_ + 1) * SLAB] = val

        @pl.when(n < nbq)
        def _():
            col = lax.broadcasted_iota(jnp.int32, (HS * BLK, 2 * BLK), 1)
            first = (col >= BLK) | (n > 0)
            for s_, rb, (i, bs) in ((s_, rb, sub) for s_ in range(sps) for rb in range(rbs) for sub in enumerate(subs)):
                cs = slice(s_ * SLAB, (s_ + 1) * SLAB)
                st = _ld((st_ref,), bs, s_, STAT_W, rb)
                kc, vc = _ld(k, bs, s_, SLAB, rb).astype(BF16), _ld(v, bs, s_, SLAB, rb).astype(BF16)
                kb = jnp.concatenate([kprev[i, :, cs], kc], axis=0)
                vb = jnp.concatenate([vprev[i, :, cs], vc], axis=0)
                kprev[i, :, cs], vprev[i, :, cs] = kc, vc
                lse = jnp.concatenate([st[:, h:h + 1] for h in range(HS)], axis=0)
                delta = jnp.concatenate([st[:, HS + h:HS + h + 1] for h in range(HS)], axis=0)
                qs = _head_stack(_ld(q, bs, s_, SLAB, rb).astype(BF16))
                dos = _head_stack(_ld(da, bs, s_, SLAB, rb).astype(BF16))
                s = _dot_nt(qs, kb) * (HD ** -0.5) + b_ref[pl.ds(s_ * HS, HS)].reshape(HS * BLK, 2 * BLK)
                if rb == 0:
                    s = jnp.where(first, s, NEG)
                p = jnp.exp(s - lse)
                ds = p * (_dot_nt(dos, vb) - delta)
                ds_ref[pl.ds(s_ * HS, HS)] += ds.reshape(HS, BLK, 2 * BLK)
                ds_b = (ds * (HD ** -0.5)).astype(BF16)
                _st(dq_ref, bs, s_, _head_unstack(_dot(ds_b, kb)), rb)
                dkb = _dot_tn(ds_b, qs)
                dvb = _dot_tn(p.astype(BF16), dos)
                finish(dk_ref, 0, bs, s_, rb, ck_ref[i, :, cs] + dkb[:BLK])
                finish(dv_ref, 1, bs, s_, rb, cv_ref[i, :, cs] + dvb[:BLK])
                ck_ref[i, :, cs] = dkb[BLK:]
                cv_ref[i, :, cs] = dvb[BLK:]

        @pl.when(n == nbq)
        def _():
            for s_ in range(sps):
                for i, bs in enumerate(subs):
                    finish(dk_ref, 0, bs, s_, 0, ck_ref[i, :, s_ * SLAB:(s_ + 1) * SLAB])
                    finish(dv_ref, 1, bs, s_, 0, cv_ref[i, :, s_ * SLAB:(s_ + 1) * SLAB])

    out = _sds(shape(AW), BF16 if GROUPS[g][1] == 1 else F32)
    nsg = N_SLAB // sps
    (dq, dk, dv, ds_acc), rode = _call_with_ride(
        body, ride, lambda: (pl.program_id(0) == 0) & (pl.program_id(1) == 0),
        lambda: (pl.program_id(0) == nsg - 1) & (pl.program_id(1) == nbq),
        name=f"attn_bwd{g}", grid=(nsg, nbq + 1),
        in_specs=sum(in_specs, []) + [pl.BlockSpec((sps * HS, BLK, 2 * BLK), lambda sg, n: (g * nsg + sg, 0, 0))],
        out_specs=[ospec(cur), ospec(late), ospec(late),
                   pl.BlockSpec((sps * HS, BLK, 2 * BLK), lambda sg, n: (sg, 0, 0))],
        out_shape=[out] * 3 + [_sds((NH, BLK, 2 * BLK))],
        scratch_shapes=[pltpu.VMEM((len(subs), BLK, sps * SLAB), F32)] * 2
        + [pltpu.VMEM((len(subs), BLK, sps * SLAB), BF16)] * 2 + [pltpu.VMEM((BLK, sps * SLAB), F32)] * (2 * (rbs - 1)),
        compiler_params=_params("arbitrary", "arbitrary"),
    )(*([qkv_g] * (3 * nl)), *([dattn] * nl), stats, bias_tab)
    return [dq.reshape(S, AW), dk.reshape(S, AW), dv.reshape(S, AW)], ds_acc, rode


TM_MIX = 256


def _mix_specs(tm):
    row512 = pl.BlockSpec((tm, AW), lambda i: (i, 0))
    return ([row512] * 6 + [
        pl.BlockSpec((tm, REST_W), lambda i: (i, 0)),
        pl.BlockSpec((HALO, AW), lambda i: (jnp.maximum(i * (tm // HALO) - 1, 0), 1)),
        pl.BlockSpec((AW, D), lambda i: (0, 0)), pl.BlockSpec((AW, D), lambda i: (0, 0)),
        pl.BlockSpec((4, PGW, PGW), lambda i: (0, 0, 0)), pl.BlockSpec((1, AW), lambda i: (0, 0))])


def _mix_forward(i, tm, o_refs, l_refs, rest_ref, halo_ref, wab_ref, wpb_ref, pw_ref, ps_ref):
    l0, l1, l2 = (r[...] for r in l_refs)
    mx = jnp.maximum(jnp.maximum(l0, l1), l2)
    e0, e1, e2 = jnp.exp(l0 - mx), jnp.exp(l1 - mx), jnp.exp(l2 - mx)
    den = e0 + e1 + e2
    lj = mx + jnp.log(den)
    attn = (e0 * o_refs[0][...] + e1 * o_refs[1][...] + e2 * o_refs[2][...]) / den

    z_attn = rest_ref[:, 0:AW]
    u = rest_ref[:, AW:2 * AW]
    z_pool = rest_ref[:, 2 * AW:3 * AW]
    g_attn = rest_ref[:, 3 * AW:3 * AW + D]
    g_pool = rest_ref[:, 3 * AW + D:3 * AW + 2 * D]

    sg_a = _sigmoid(z_attn)
    sil_a = z_attn * sg_a
    a_g = (attn * sil_a).astype(BF16)
    y_attn = _dot(a_g, wab_ref[...])

    halo = jnp.where(i > 0, halo_ref[...], 0.0)
    ext = jnp.concatenate([halo, u], axis=0)
    t = i * tm + lax.broadcasted_iota(jnp.int32, (tm, 1), 0)
    pooled, mixed_raw = [], []
    for gi, win in enumerate(POOL_WINDOWS):
        s = ext[:, gi * PGW:(gi + 1) * PGW]
        sh = 1
        while sh < win:
            s = s + pltpu.roll(s, sh, 0)
            sh *= 2
        cnt = jnp.minimum(t + 1, win).astype(F32)
        pg = s[HALO:] / cnt - u[:, gi * PGW:(gi + 1) * PGW]
        pooled.append(pg.astype(BF16))
        mixed_raw.append(_dot(pooled[-1], pw_ref[gi].astype(BF16)))
    mixed_raw = jnp.concatenate(mixed_raw, axis=1)
    mixed = mixed_raw * ps_ref[...]
    sg_p = _sigmoid(z_pool)
    sil_p = z_pool * sg_p
    m_g = (mixed * sil_p).astype(BF16)
    y_pool = _dot(m_g, wpb_ref[...])

    sa = _sigmoid(g_attn)
    sp = _sigmoid(g_pool)
    merged = sa * y_attn + sp * y_pool
    return dict(lj=lj, attn=attn, z_attn=z_attn, z_pool=z_pool, sg_a=sg_a, sil_a=sil_a, a_g=a_g, y_attn=y_attn,
                pooled=pooled, mixed_raw=mixed_raw, mixed=mixed, sg_p=sg_p, sil_p=sil_p, m_g=m_g, y_pool=y_pool,
                sa=sa, sp=sp, merged=merged)


def _mix_step(x, target, os_, ls_, rest, wab, wpb, pool_w, pool_scale, wout, mod, final_g):
    S = x.shape[0]
    tm = TM_MIX
    nt = S // tm
    sw = D // N_SHARD

    def body(o0, o1, o2, l0, l1, l2, rest_ref, halo_ref, wab_ref, wpb_ref, pw_ref, ps_ref,
             x_ref, t_ref, wo_ref, mod_ref, fg_ref, dx2_ref, loss_ref, dfg_ref, dgate_ref,
             dattn_ref, stats_ref, dpooled_ref, dproj_hbm, dwo_hbm, dwab_hbm, dwpb_hbm, dpw_ref, dps_ref,
             awo, awab, awpb, stage, stage_sem):
        i = pl.program_id(0)
        slot = i % 2

        def staged(step, sl):
            return pltpu.make_async_copy(stage.at[sl], dproj_hbm.at[pl.ds(step * tm, tm), pl.ds(QKV_W, REST_W)],
                                         stage_sem.at[sl])

        @pl.when(i == 0)
        def _():
            for ref in (loss_ref, dfg_ref, dgate_ref, awo, awab, awpb, dpw_ref, dps_ref):
                ref[...] = jnp.zeros_like(ref)

        f = _mix_forward(i, tm, (o0, o1, o2), (l0, l1, l2), rest_ref, halo_ref, wab_ref, wpb_ref, pw_ref, ps_ref)
        mo = _dot(f["merged"].astype(BF16), wo_ref[...])
        gate = mod_ref[:, 2 * D:3 * D]
        fg = fg_ref[...]
        x2 = x_ref[...] + gate * mo
        r2 = lax.rsqrt(jnp.mean(x2 * x2, axis=-1, keepdims=True) + EPS)
        n2 = x2 * r2
        err = n2 * fg - t_ref[...]
        loss_ref[...] += 0.5 * jnp.sum(jnp.mean(err * err, axis=-1, keepdims=True))
        dy = err * (1.0 / D)
        dfg_ref[...] += jnp.sum(dy * n2, axis=0, keepdims=True)
        dn = dy * fg
        dx2 = r2 * (dn - n2 * jnp.mean(dn * n2, axis=-1, keepdims=True))
        dgate_ref[...] += jnp.sum(dx2 * mo, axis=0, keepdims=True)
        dx2_ref[...] = dx2

        dmo_b = (dx2 * gate).astype(BF16)
        dmerged = _dot_nt(dmo_b, wo_ref[...])
        awo[...] += _dot_tn(f["merged"].astype(BF16), dmo_b)
        sa, sp = f["sa"], f["sp"]
        dya = (dmerged * sa).astype(BF16)
        dyp = (dmerged * sp).astype(BF16)
        dg_attn = dmerged * f["y_attn"] * sa * (1.0 - sa)
        dg_pool = dmerged * f["y_pool"] * sp * (1.0 - sp)
        dag = _dot_nt(dya, wab_ref[...])
        awab[...] += _dot_tn(f["a_g"], dya)
        dmg = _dot_nt(dyp, wpb_ref[...])
        awpb[...] += _dot_tn(f["m_g"], dyp)
        dattn = dag * f["sil_a"]
        dattn_ref[...] = dattn
        prod = dattn * f["attn"]
        lane = lax.broadcasted_iota(jnp.int32, (tm, STAT_W), 1)
        for sb in range(N_SLAB):
            st = jnp.zeros((tm, STAT_W), F32)
            for h in range(HS):
                hs = slice((sb * HS + h) * HD, (sb * HS + h + 1) * HD)
                st = jnp.where(lane == h, f["lj"][:, hs.start:hs.start + 1], st)
                st = jnp.where(lane == HS + h, jnp.sum(prod[:, hs], axis=-1, keepdims=True), st)
            stats_ref[:, sb * STAT_W:(sb + 1) * STAT_W] = st
        dz_attn = dag * f["attn"] * (f["sg_a"] * (1.0 + f["z_attn"] * (1.0 - f["sg_a"])))
        dmixed = dmg * f["sil_p"]
        dz_pool = dmg * f["mixed"] * (f["sg_p"] * (1.0 + f["z_pool"] * (1.0 - f["sg_p"])))
        dps_ref[...] += jnp.sum(dmixed * f["mixed_raw"], axis=0, keepdims=True)
        dpm = (dmixed * ps_ref[...]).astype(BF16)
        for gi in range(len(POOL_WINDOWS)):
            cs = slice(gi * PGW, (gi + 1) * PGW)
            dpw_ref[gi] += _dot_tn(f["pooled"][gi], dpm[:, cs])
            dpooled_ref[:, cs] = _dot_nt(dpm[:, cs], pw_ref[gi].astype(BF16))
        @pl.when(i >= 2)
        def _():
            staged(i - 2, slot).wait()

        stage[slot, :, 0:AW] = dz_attn.astype(BF16)
        stage[slot, :, AW:2 * AW] = jnp.zeros((tm, AW), BF16)
        stage[slot, :, 2 * AW:3 * AW] = dz_pool.astype(BF16)
        stage[slot, :, 3 * AW:3 * AW + D] = dg_attn.astype(BF16)
        stage[slot, :, 3 * AW + D:3 * AW + 2 * D] = dg_pool.astype(BF16)
        staged(i, slot).start()

        @pl.when(i == nt - 1)
        def _():
            staged(i - 1, 1 - slot).wait()
            staged(i, slot).wait()
            pltpu.sync_copy(awo, dwo_hbm)
            for k in range(N_SHARD):
                pltpu.sync_copy(awab.at[:, pl.ds(k * sw, sw)], dwab_hbm.at[k])
                pltpu.sync_copy(awpb.at[:, pl.ds(k * sw, sw)], dwpb_hbm.at[k])

    row = pl.BlockSpec((tm, D), lambda i: (i, 0))
    vec = pl.BlockSpec((1, D), lambda i: (0, 0))
    row512 = pl.BlockSpec((tm, AW), lambda i: (i, 0))
    outs = pl.pallas_call(
        body, name="mix_step", grid=(nt,),
        in_specs=_mix_specs(tm) + [row, row, pl.BlockSpec((D, D), lambda i: (0, 0)),
                                   pl.BlockSpec((1, 3 * D), lambda i: (0, 0)), vec],
        out_specs=[row, pl.BlockSpec((8, 128), lambda i: (0, 0)), vec, vec,
                   row512, pl.BlockSpec((tm, N_SLAB * STAT_W), lambda i: (i, 0)), row512, ANY, ANY, ANY, ANY,
                   pl.BlockSpec((4, PGW, PGW), lambda i: (0, 0, 0)), pl.BlockSpec((1, AW), lambda i: (0, 0))],
        out_shape=[_sds((S, D)), _sds((8, 128)), _sds((1, D)), _sds((1, D)),
                   _sds((S, AW)), _sds((S, N_SLAB * STAT_W)), _sds((S, AW)), _sds((S, IN_W), BF16),
                   _sds((D, D)), _sds((N_SHARD, AW, sw)), _sds((N_SHARD, AW, sw)), _sds((4, PGW, PGW)), _sds((1, AW))],
        scratch_shapes=[pltpu.VMEM((D, D), F32), pltpu.VMEM((AW, D), F32), pltpu.VMEM((AW, D), F32),
                        pltpu.VMEM((2, tm, REST_W), BF16), _dma_sems(2)],
        compiler_params=_params("arbitrary"),
    )(*os_, *ls_, rest, rest, wab, wpb, pool_w, pool_scale, x, target, wout, mod, final_g)
    dx2, loss, dfg, dgate, dattn, stats, dpooled, dproj, dwo, dwab, dwpb, dpw, dps = outs
    return (dx2, loss, dfg, dgate, dattn, stats, dpooled, dproj, dwo.reshape(N_SHARD, D // N_SHARD, D), dwab, dwpb,
            dpw, dps)


def _pool_bwd(dpooled):
    S = dpooled.shape[0]
    tm = 512
    nt = S // tm

    def body(dp_ref, nxt_ref, du_ref):
        i = pl.program_id(0)
        t = i * tm + lax.broadcasted_iota(jnp.int32, (tm + HALO, 1), 0)
        nxt = jnp.where(i < nt - 1, nxt_ref[...], 0.0)
        ext = jnp.concatenate([dp_ref[...], nxt], axis=0)
        for gi, win in enumerate(POOL_WINDOWS):
            cs = slice(gi * PGW, (gi + 1) * PGW)
            s = ext[:, cs] / jnp.minimum(t + 1, win).astype(F32)
            sh = 1
            while sh < win:
                s = s + pltpu.roll(s, tm + HALO - sh, 0)
                sh *= 2
            du_ref[:, cs] = (s[:tm] - dp_ref[:, cs]).astype(BF16)

    return pl.pallas_call(
        body, name="pool_bwd", grid=(nt,),
        in_specs=[pl.BlockSpec((tm, AW), lambda i: (i, 0)),
                  pl.BlockSpec((HALO, AW), lambda i: (jnp.minimum((i + 1) * (tm // HALO), S // HALO - 1), 0))],
        out_specs=pl.BlockSpec((tm, AW), lambda i: (i, 0)),
        out_shape=_sds((S, AW), BF16), compiler_params=_params("parallel"),
    )(dpooled, dpooled)


TB = 1024


def _dh(dproj, wg_in, ride):
    S = dproj.shape[0]
    per = wg_in.shape[2] // TB
    nm, nk = S // TB, IN_W // TB

    def body(dp_ref, w_ref, out_ref):
        @pl.when(pl.program_id(1) == 0)
        def _():
            out_ref[...] = jnp.zeros_like(out_ref)

        out_ref[...] += _dot_nt(dp_ref[...], w_ref[...])

    (dh,), rode = _call_with_ride(
        body, ride, lambda: (pl.program_id(0) == 0) & (pl.program_id(1) == 0),
        lambda: (pl.program_id(0) == nm - 1) & (pl.program_id(1) == nk - 1),
        name="dh", grid=(nm, nk),
        in_specs=[pl.BlockSpec((TB, TB), lambda m, kk: (m, kk)),
                  pl.BlockSpec((None, D, TB), lambda m, kk: (kk // per, 0, kk % per))],
        out_specs=[pl.BlockSpec((TB, D), lambda m, kk: (m, 0))],
        out_shape=[_sds((S, D))], compiler_params=_params("arbitrary", "arbitrary"),
    )(dproj, wg_in)
    return dh, rode


def _dw_in(h, dproj):
    S = dproj.shape[0]
    per = IN_W // N_SHARD // TB

    def body(h_ref, dp_ref, out_ref):
        out_ref[...] = _dot_tn(h_ref[...], dp_ref[...])

    return pl.pallas_call(
        body, name="dw_in", grid=(IN_W // TB,),
        in_specs=[pl.BlockSpec((S, D), lambda j: (0, 0)), pl.BlockSpec((S, TB), lambda j: (0, j))],
        out_specs=pl.BlockSpec((None, D, TB), lambda j: (j // per, 0, j % per)),
        out_shape=_sds((N_SHARD, D, IN_W // N_SHARD)), compiler_params=_params("parallel"),
    )(h, dproj)


def _prenorm_bwd(x, dh, dx2, norm_g, mod):
    S = x.shape[0]
    tm = 512

    def body(x_ref, dh_ref, dx2_ref, g_ref, mod_ref, gx_ref, dg_ref, dshift_ref, dscale_ref):
        i = pl.program_id(0)

        @pl.when(i == 0)
        def _():
            dg_ref[...] = jnp.zeros_like(dg_ref)
            dshift_ref[...] = jnp.zeros_like(dshift_ref)
            dscale_ref[...] = jnp.zeros_like(dscale_ref)

        xv = x_ref[...]
        dhv = dh_ref[...]
        g = g_ref[...]
        r = lax.rsqrt(jnp.mean(xv * xv, axis=-1, keepdims=True) + EPS)
        xh = xv * r
        dshift_ref[...] += jnp.sum(dhv, axis=0, keepdims=True)
        dscale_ref[...] += jnp.sum(dhv * (xh * g), axis=0, keepdims=True)
        dn1 = dhv * (1.0 + mod_ref[:, D:2 * D])
        dg_ref[...] += jnp.sum(dn1 * xh, axis=0, keepdims=True)
        dxh = dn1 * g
        gx_ref[...] = dx2_ref[...] + r * (dxh - xh * jnp.mean(dxh * xh, axis=-1, keepdims=True))

    row = pl.BlockSpec((tm, D), lambda i: (i, 0))
    vec = pl.BlockSpec((1, D), lambda i: (0, 0))
    return pl.pallas_call(
        body, name="prenorm_bwd", grid=(S // tm,),
        in_specs=[row, row, row, vec, pl.BlockSpec((1, 3 * D), lambda i: (0, 0))],
        out_specs=[row, vec, vec, vec],
        out_shape=[_sds((S, D)), _sds((1, D)), _sds((1, D)), _sds((1, D))],
        compiler_params=_params("arbitrary"),
    )(x, dh, dx2, norm_g, mod)


def _local_step(x, target, mod, wg_in, wab, wpb, wout, pool_w, pool_scale, rel_bias, norm_g, final_g, chip_half):
    buckets = jnp.asarray(_bucket_tables())
    bias_tab = _bias_table(rel_bias, buckets)
    h = _prenorm(x, norm_g, mod)
    qkv = [_proj(h, wg_in, 3 * g, 3, BF16 if GROUPS[g][1] == 1 else F32, f"proj_qkv{g}") for g in range(NG)]
    rest = _proj(h, wg_in, NCB_QKV, REST_W // CB, F32, "proj_rest")
    os_, ls_ = zip(*[_attn_fwd(qkv[g], bias_tab, g) for g in range(NG)])
    (dx2, loss, dfinal_g, dgate, dattn, stats, dpooled, dproj, dw_out, dw_ab, dw_pb, dpool_w,
     dpool_scale) = _mix_step(x, target, os_, ls_, rest, wab, wpb, pool_w, pool_scale, wout, mod, final_g)
    du = _pool_bwd(dpooled)

    small = [dw_ab, dw_pb, dw_out]
    dqkv0, ds0, sib_small = _attn_bwd(qkv[0], dattn, stats, bias_tab, 0, _ride_sibling_halves(small))
    p_small = _pair_sum_small(small, sib_small, chip_half)
    dqkv1, ds1, u_small = _attn_bwd(qkv[1], dattn, stats, bias_tab, 1,
                                    _ride_chip_exchange([p16 for _, p16 in p_small]))
    rs_ab, rs_pb, rs_out = _chip_sum_small([p32 for p32, _ in p_small], u_small, chip_half)
    dqkv2, ds2, _ = _attn_bwd(qkv[2], dattn, stats, bias_tab, 2, None)

    for j, piece in enumerate(dqkv0 + dqkv1 + dqkv2):
        dproj = lax.dynamic_update_slice(dproj, piece.astype(BF16), (0, j * AW))
    dproj = lax.dynamic_update_slice(dproj, du, (0, QKV_W + AW))
    dw_in = _dw_in(h, dproj)
    drel_rows, (sib_in,) = _bias_grad(jnp.concatenate([ds0, ds1, ds2], axis=0), buckets,
                                      _ride_sibling_halves([dw_in]))
    drel = drel_rows[:, 0, :NUM_BUCKETS].T
    p32_in, p16_in = _pair_sum(dw_in, sib_in, chip_half, "rs_pair_sum_in")
    dh, (u_in,) = _dh(dproj, wg_in, _ride_chip_exchange([p16_in]))
    rs_in = _chip_sum(p32_in, u_in, chip_half, "rs_chip_sum_in")

    grad_x, dnorm_g, dshift, dscale = _prenorm_bwd(x, dh, dx2, norm_g, mod)
    dmod = jnp.concatenate([dshift, dscale, dgate], axis=1)
    return dict(loss=loss[0, 0], grad_x=grad_x, dmod=dmod, dnorm_g=dnorm_g, dfinal_g=dfinal_g, dpool_w=dpool_w,
                dpool_scale=dpool_scale, drel_bias=drel, dw_in=dw_in, dw_attn_br=dw_ab, dw_pool_br=dw_pb,
                dw_out=dw_out, rs_in=rs_in, rs_attn_br=rs_ab, rs_pool_br=rs_pb, rs_out=rs_out)


def _allgather8(blocks, name, relay=None):
    nb = len(blocks)
    relay = [False] * nb if relay is None else list(relay)

    def body(*refs):
        ins, outs = refs[:nb], refs[nb:2 * nb]
        send_sems, recv_sems = refs[2 * nb:]
        x, y, c = lax.axis_index("x"), lax.axis_index("y"), lax.axis_index("c")
        me, sibling = (x, y, c), (x, y, 1 - c)
        here, xn, yn, dg = (x, y), (1 - x, y), (x, 1 - y), (1 - x, 1 - y)

        def slot(a, chip, core, half=None):
            ref = outs[a].at[4 * chip[0] + 2 * chip[1] + core]
            if half is None:
                return ref
            r2 = ref.shape[0] // 2
            return ref.at[pl.ds(half * r2, r2)]

        def copy(a, k, dst, to, src=None):
            return pltpu.make_async_remote_copy(src_ref=dst if src is None else src, dst_ref=dst,
                                                send_sem=send_sems.at[a, k], recv_sem=recv_sems.at[a, k],
                                                device_id=to, device_id_type=MESH)

        def start(cps):
            for cp in cps:
                cp.start()
            return cps

        sent = []
        for a in range(nb):
            own = slot(a, here, c)
            sent += [copy(a, 0, own, sibling, src=ins[a]), copy(a, 1, own, (*xn, c), src=ins[a]),
                     copy(a, 2, own, (*yn, c), src=ins[a])]
            if not relay[a]:
                sent.append(copy(a, 3, own, (*dg, c), src=ins[a]))
        start(sent)
        for a in range(nb):
            copy(a, 2, slot(a, yn, c), me).wait_recv()
            sent += start([copy(a, 6, slot(a, yn, c), sibling)]
                          + ([copy(a, 3, slot(a, yn, c, 0), (*xn, c))] if relay[a] else []))
        for a in range(nb):
            copy(a, 1, slot(a, xn, c), me).wait_recv()
            sent += start([copy(a, 5, slot(a, xn, c), sibling)]
                          + ([copy(a, 4, slot(a, xn, c, 1), (*yn, c))] if relay[a] else []))
        for a in range(nb):
            for k, half in ((3, 0), (4, 1)) if relay[a] else ((3, None),):
                copy(a, k, slot(a, dg, c, half), me).wait_recv()
                sent += start([copy(a, 4 + k, slot(a, dg, c, half), sibling)])
        for a in range(nb):
            copy(a, 0, slot(a, here, 1 - c), me).wait_recv()
            copy(a, 5, slot(a, xn, 1 - c), me).wait_recv()
            copy(a, 6, slot(a, yn, 1 - c), me).wait_recv()
            for k, half in ((7, 0), (8, 1)) if relay[a] else ((7, None),):
                copy(a, k, slot(a, dg, 1 - c, half), me).wait_recv()
        for cp in sent:
            cp.wait_send()

    outs = pl.pallas_call(
        body, name=name, in_specs=[ANY] * nb, out_specs=[ANY] * nb,
        out_shape=[_sds((8,) + b.shape, b.dtype) for b in blocks],
        scratch_shapes=[_dma_sems(nb, 9), _dma_sems(nb, 9)],
    )(*blocks)
    return [_place_own(buf, b) for buf, b in zip(outs, blocks)]


def _place_own(buf, block):
    dev = 4 * lax.axis_index("x") + 2 * lax.axis_index("y") + lax.axis_index("c")
    return lax.dynamic_update_index_in_dim(buf, block, dev, 0)


def _ride_sibling_halves(gs):
    def copies(ins, outs, send_sems, recv_sems):
        x, y, c = lax.axis_index("x"), lax.axis_index("y"), lax.axis_index("c")
        cps = []
        for a in range(len(gs)):
            r2 = ins[a].shape[1] // 2
            other = ins[a].at[:, pl.ds((1 - c) * r2, r2), :]
            cps.append(pltpu.make_async_remote_copy(src_ref=other, dst_ref=outs[a], send_sem=send_sems.at[a],
                                                    recv_sem=recv_sems.at[a], device_id=(x, y, 1 - c),
                                                    device_id_type=MESH))
        return cps

    return _Ride(gs, [_sds((g.shape[0], g.shape[1] // 2, g.shape[2]), g.dtype) for g in gs], len(gs), copies)


def _pair_sum(g, t, chip_half, name):
    nsh, rows, cols = g.shape
    r2 = rows // 2
    tr = _row_tile(r2, cols)
    nt = r2 // tr

    def body(ch_ref, g_ref, t_ref, p32_ref, p16_ref):
        p = g_ref[...] + t_ref[...]
        p16_ref[...] = p.astype(BF16)

        @pl.when(pl.program_id(1) == ch_ref[0])
        def _():
            p32_ref[...] = p

    blk = pl.BlockSpec((None, tr, cols), lambda i, k, ch_ref: (k, i, 0))
    return pl.pallas_call(
        body, name=name,
        grid_spec=pltpu.PrefetchScalarGridSpec(
            num_scalar_prefetch=1, grid=(nt, nsh),
            in_specs=[pl.BlockSpec((None, tr, cols), lambda i, k, ch_ref: (k, ch_ref[1] * nt + i, 0)), blk],
            out_specs=[pl.BlockSpec((tr, cols), lambda i, k, ch_ref: (i, 0)), blk]),
        out_shape=[_sds((r2, cols)), _sds((nsh, r2, cols), BF16)],
        compiler_params=_params("parallel", "arbitrary"),
    )(chip_half, g, t)


def _pair_sum_small(gs, ts, chip_half):
    na = len(gs)

    def body(ch_ref, *refs):
        g_refs, t_refs, outs = refs[:na], refs[na:2 * na], refs[2 * na:]
        for a in range(na):
            r2 = t_refs[a].shape[1]
            own = pl.ds(pl.multiple_of(ch_ref[1] * r2, 8), r2)
            outs[2 * a + 1][...] = (g_refs[a][:, own, :] + t_refs[a][...]).astype(BF16)
            outs[2 * a][...] = g_refs[a][ch_ref[0], own, :] + t_refs[a][ch_ref[0]]

    res = pl.pallas_call(
        body, name="rs_pair_sum_small",
        in_specs=[pl.BlockSpec(memory_space=pltpu.SMEM)] + [pl.BlockSpec(memory_space=pltpu.VMEM)] * (2 * na),
        out_shape=[s for t in ts for s in (_sds(t.shape[1:]), _sds(t.shape, BF16))], compiler_params=_params(),
    )(chip_half, *gs, *ts)
    return [(res[2 * a], res[2 * a + 1]) for a in range(na)]


def _chip_sum_small(p32s, us, chip_half):
    na = len(p32s)

    def body(ch_ref, *refs):
        p_refs, u_refs, outs = refs[:na], refs[na:2 * na], refs[2 * na:]
        for a in range(na):
            r2 = p_refs[a].shape[0]
            acc = p_refs[a][...]
            for j in range(3):
                acc = acc + u_refs[a][j].astype(F32)
            outs[a][pl.ds(pl.multiple_of(ch_ref[1] * r2, 8), r2), :] = acc

    return pl.pallas_call(
        body, name="rs_chip_sum_small",
        in_specs=[pl.BlockSpec(memory_space=pltpu.SMEM)] + [pl.BlockSpec(memory_space=pltpu.VMEM)] * (2 * na),
        out_shape=[_sds((2 * p.shape[0], p.shape[1])) for p in p32s], compiler_params=_params(),
    )(chip_half, *p32s, *us)


def _ride_chip_exchange(ps):
    def copies(ins, outs, send_sems, recv_sems):
        x, y, c = lax.axis_index("x"), lax.axis_index("y"), lax.axis_index("c")
        chips = [(1 - x, y), (x, 1 - y), (1 - x, 1 - y)]
        cps = []
        for a in range(len(ps)):
            for j, (ox, oy) in enumerate(chips):
                cps.append(pltpu.make_async_remote_copy(src_ref=ins[a].at[2 * ox + oy], dst_ref=outs[a].at[j],
                                                        send_sem=send_sems.at[3 * a + j],
                                                        recv_sem=recv_sems.at[3 * a + j],
                                                        device_id=(ox, oy, c), device_id_type=MESH))
        return cps

    return _Ride(ps, [_sds((3,) + p.shape[1:], p.dtype) for p in ps], 3 * len(ps), copies)


def _chip_sum(p32, u, chip_half, name):
    r2, cols = p32.shape
    tr = _row_tile(r2, cols)
    nt = r2 // tr

    def body(ch_ref, p_ref, u_ref, o_ref):
        acc = p_ref[...]
        for j in range(3):
            acc = acc + u_ref[j].astype(F32)
        o_ref[...] = acc

    return pl.pallas_call(
        body, name=name,
        grid_spec=pltpu.PrefetchScalarGridSpec(
            num_scalar_prefetch=1, grid=(nt,),
            in_specs=[pl.BlockSpec((tr, cols), lambda i, ch_ref: (i, 0)),
                      pl.BlockSpec((3, tr, cols), lambda i, ch_ref: (0, i, 0))],
            out_specs=pl.BlockSpec((tr, cols), lambda i, ch_ref: (ch_ref[1] * nt + i, 0))),
        out_shape=_sds((2 * r2, cols)), compiler_params=_params("parallel"),
    )(chip_half, p32, u)


def _sibling_join(fs, name):
    nb = len(fs)

    def body(*refs):
        outs = refs[nb:2 * nb]
        send_sems, recv_sems = refs[2 * nb:]
        x, y, c = lax.axis_index("x"), lax.axis_index("y"), lax.axis_index("c")
        cps = []
        for a in range(nb):
            r2 = outs[a].shape[0] // 2
            rows = outs[a].at[pl.ds(c * r2, r2), :]
            cps.append(pltpu.make_async_remote_copy(src_ref=rows, dst_ref=rows, send_sem=send_sems.at[a],
                                                    recv_sem=recv_sems.at[a], device_id=(x, y, 1 - c),
                                                    device_id_type=MESH))
        for cp in cps:
            cp.start()
        for cp in cps:
            cp.wait()

    return pl.pallas_call(
        body, name=name, in_specs=[ANY] * nb, out_specs=[ANY] * nb,
        out_shape=[_sds(f.shape, f.dtype) for f in fs],
        input_output_aliases={a: a for a in range(nb)},
        scratch_shapes=[_dma_sems(nb), _dma_sems(nb)],
    )(*fs)


def _row_tile(rows, cols):
    tile = rows
    while tile * cols * 4 > (1 << 20) and tile % 16 == 0:
        tile //= 2
    return tile


def _w_ada_grad(c_all, dmod_cols):
    def body(c_ref, d_ref, o_ref):
        o_ref[...] = _dot_tn(c_ref[...].astype(BF16), d_ref[...].astype(BF16))

    return pl.pallas_call(body, name="w_ada_grad", out_shape=_sds((c_all.shape[1], dmod_cols.shape[1])),
                          compiler_params=_params())(c_all, dmod_cols)


def _adam_math(w, g, m, v):
    nm = ADAM_B1 * m + (1.0 - ADAM_B1) * g
    nv = ADAM_B2 * v + (1.0 - ADAM_B2) * (g * g)
    m_hat = nm / (1.0 - ADAM_B1 ** ADAM_STEP)
    v_hat = nv / (1.0 - ADAM_B2 ** ADAM_STEP)
    return -ADAM_LR * (m_hat / (jnp.sqrt(v_hat) + ADAM_EPS) + ADAM_WD * w), nm, nv


def _adamw(w, g, m, v, name):
    rows, cols = w.shape
    tr = _row_tile(rows, cols)

    def body(w_ref, g_ref, m_ref, v_ref, go_ref, d_ref, nm_ref, nv_ref):
        gv = g_ref[...]
        go_ref[...] = gv
        d_ref[...], nm_ref[...], nv_ref[...] = _adam_math(w_ref[...], gv, m_ref[...], v_ref[...])

    spec = pl.BlockSpec((tr, cols), lambda i: (i, 0))
    return pl.pallas_call(
        body, name=name, grid=(rows // tr,), in_specs=[spec] * 4, out_specs=[spec] * 4,
        out_shape=[_sds((rows, cols))] * 4, compiler_params=_params("parallel"),
    )(w, g, m, v)


def _pack_small(dmod, dnorm_g, dfinal_g, dpool_scale, drel_bias, loss, dpool_w):
    return jnp.concatenate([dmod.reshape(-1, 128), dnorm_g.reshape(-1, 128), dfinal_g.reshape(-1, 128),
                            jnp.pad(dpool_scale.reshape(-1, 128), ((0, PK_RELB - PK_PSCALE - AW // 128), (0, 0))),
                            jnp.pad(drel_bias, ((0, 0), (0, 128 - NG * NH))),
                            jnp.full((PK_POOLW - PK_LOSS, 128), loss, F32), dpool_w.reshape(-1, 128)], axis=0)


def _small_update(small_all, ws, ms, vs):
    lane_rows = [(r0, r0 + w.shape[1] // 128) for r0, w in zip((PK_BADA, PK_NORMG, PK_FINALG, PK_PSCALE), ws)]
    nw = len(ws)

    def body(all_ref, *refs):
        w_refs, m_refs, v_refs = refs[:nw], refs[nw:2 * nw], refs[2 * nw:3 * nw]
        loss_ref, outs = refs[3 * nw], refs[3 * nw + 1:]
        g = all_ref[0]
        for s in range(1, all_ref.shape[0]):
            g = g + all_ref[s]
        loss_ref[...] = jnp.broadcast_to(g[PK_LOSS:PK_LOSS + 1, :], loss_ref.shape)

        def put(p, at, gv):
            d, nm, nv = _adam_math(w_refs[p][at], gv, m_refs[p][at], v_refs[p][at])
            for o_ref, val in zip(outs[4 * p:4 * p + 4], (gv, d, nm, nv)):
                o_ref[at] = val

        for p, (r0, r1) in enumerate(lane_rows):
            for i in range(r1 - r0):
                put(p, (slice(None), slice(128 * i, 128 * (i + 1))), g[r0 + i:r0 + i + 1, :])
        put(4, (slice(None), slice(None)), g[PK_RELB:PK_LOSS, 0:NG * NH])
        put(5, (slice(None), slice(None)), g[PK_POOLW:PK_ROWS, :])

    res = pl.pallas_call(
        body, name="small_update",
        out_shape=[_sds((8, 128))] + [_sds(w.shape) for w in ws for _ in range(4)], compiler_params=_params(),
    )(small_all, *ws, *ms, *vs)
    return res[0], [res[1 + 4 * p:5 + 4 * p] for p in range(nw)]


def kernel(x, c, norm_g, w_ada, b_ada, w_in, pool_w, pool_scale, w_attn_br, w_pool_br, w_out, rel_bias, final_g, loss_target, m_norm_g, m_w_ada, m_b_ada, m_w_in, m_pool_w, m_pool_scale, m_w_attn_br, m_w_pool_br, m_w_out, m_rel_bias, m_final_g, v_norm_g, v_w_ada, v_b_ada, v_w_in, v_pool_w, v_pool_scale, v_w_attn_br, v_w_pool_br, v_w_out, v_rel_bias, v_final_g):
    ix, iy, ic = lax.axis_index("x"), lax.axis_index("y"), lax.axis_index("c")
    dev = 4 * ix + 2 * iy + ic
    chip = 2 * ix + iy

    def half(w):
        r2 = w.shape[0] // 2
        return lax.dynamic_slice_in_dim(w, ic * r2, r2, axis=0).astype(BF16)

    gathered = _allgather8([jnp.broadcast_to(c, (8, D)), half(w_in[0]), half(w_attn_br[0]), half(w_pool_br[0]),
                            half(w_out[0])], "gather_weights", relay=[False, True, True, True, True])
    c_all = gathered[0][:, 0, :]
    wg_in = gathered[1].reshape(N_SHARD, D, IN_W // N_SHARD)
    wab = gathered[2].reshape(N_SHARD, AW, D // N_SHARD).transpose(1, 0, 2).reshape(AW, D)
    wpb = gathered[3].reshape(N_SHARD, AW, D // N_SHARD).transpose(1, 0, 2).reshape(AW, D)
    wout = gathered[4].reshape(D, D)

    mw = 3 * D // N_SHARD
    modp = _mod_partial(c_all, w_ada[0], lax.dynamic_slice_in_dim(b_ada, chip * mw, mw, axis=1))
    mod_all = _allgather8([modp], "gather_mod")[0]
    mod_full = mod_all[::2].transpose(1, 0, 2).reshape(8, 3 * D)
    mod = lax.dynamic_slice_in_dim(mod_full, dev, 1, axis=0)

    chip_half = jnp.stack([chip, ic]).astype(jnp.int32)
    r = _local_step(x[0], loss_target[0], mod, wg_in, wab, wpb, wout, pool_w[0], pool_scale, rel_bias, norm_g,
                    final_g.reshape(1, D), chip_half)

    packed = _pack_small(r["dmod"], r["dnorm_g"], r["dfinal_g"], r["dpool_scale"], r["drel_bias"], r["loss"],
                         r["dpool_w"])
    small_all = _allgather8([packed], "gather_small")[0]
    small = ["b_ada", "norm_g", "final_g", "pool_scale", "rel_bias", "pool_w"]
    shaped = lambda b, n, f, ps, rb, pw: [b, n, f.reshape(1, D), ps, rb, pw.reshape(4 * PGW, PGW)]
    loss, small_out = _small_update(small_all, shaped(b_ada, norm_g, final_g, pool_scale, rel_bias, pool_w),
                                    shaped(m_b_ada, m_norm_g, m_final_g, m_pool_scale, m_rel_bias, m_pool_w),
                                    shaped(v_b_ada, v_norm_g, v_final_g, v_pool_scale, v_rel_bias, v_pool_w))
    dmod_all = small_all[:, PK_BADA:PK_NORMG, :].reshape(8, 3 * D)
    g_w_ada = _w_ada_grad(c_all, lax.dynamic_slice_in_dim(dmod_all, chip * mw, mw, axis=1))

    g_w_in, g_w_ab, g_w_pb, g_w_out = _sibling_join([r["rs_in"], r["rs_attn_br"], r["rs_pool_br"], r["rs_out"]],
                                                    "rs_sibling_join")
    upd = dict(zip(small, small_out))
    upd["final_g"] = [a.reshape(D) for a in upd["final_g"]]
    upd["pool_w"] = [a.reshape(1, 4, PGW, PGW) for a in upd["pool_w"]]
    for nme, w, g, m, v in (("w_ada", w_ada, g_w_ada, m_w_ada, v_w_ada), ("w_in", w_in, g_w_in, m_w_in, v_w_in),
                            ("w_attn_br", w_attn_br, g_w_ab, m_w_attn_br, v_w_attn_br),
                            ("w_pool_br", w_pool_br, g_w_pb, m_w_pool_br, v_w_pool_br),
                            ("w_out", w_out, g_w_out, m_w_out, v_w_out)):
        upd[nme] = [a[None] for a in _adamw(w[0], g, m[0], v[0], "adamw_" + nme)]
    names = ["norm_g", "w_ada", "b_ada", "w_in", "pool_w", "pool_scale", "w_attn_br", "w_pool_br", "w_out",
             "rel_bias", "final_g"]
    return (loss[0, 0], r["grad_x"][None]) + tuple(upd[nme][kind] for kind in range(4) for nme in names)
```

```python
import math

import numpy as np
import jax
import jax.numpy as jnp
from jax import lax
from jax.experimental import pallas as pl
from jax.experimental.pallas import tpu as pltpu

F32 = jnp.float32
BF16 = jnp.bfloat16

D = 1024
HD = 64
NH = 8
AW = NH * HD
GROUPS = ((128, 1), (512, 4), (2048, 16))
NG = len(GROUPS)
BLK = 128
GW = 3 * AW
QKV_W = NG * GW
REST_W = 3584
IN_W = QKV_W + REST_W
CB = 512
NCB_QKV = QKV_W // CB
POOL_WINDOWS = (2, 4, 8, 16)
PGW = 128
HALO = 16
NUM_BUCKETS = 32
MAX_DISTANCE = 2048
EPS = 1e-6
NEG = -1e30
N_SHARD = 4
VMEM_LIMIT = 56 * 1024 * 1024

ADAM_LR = 0.001
ADAM_B1 = 0.9
ADAM_B2 = 0.999
ADAM_EPS = 1e-08
ADAM_WD = 0.01
ADAM_STEP = 10

PK_BADA, PK_NORMG, PK_FINALG, PK_PSCALE, PK_RELB, PK_LOSS, PK_POOLW, PK_ROWS = 0, 24, 32, 40, 48, 80, 88, 600

ANY = pl.BlockSpec(memory_space=pl.ANY)
MESH = pl.DeviceIdType.MESH


def _params(*sem):
    return pltpu.CompilerParams(dimension_semantics=sem, vmem_limit_bytes=VMEM_LIMIT)


def _sds(shape, dtype=F32):
    return jax.ShapeDtypeStruct(shape, dtype)


def _dot(a, b):
    return jnp.dot(a, b, preferred_element_type=F32)


def _dot_nt(a, b):
    return lax.dot_general(a, b, (((1,), (1,)), ((), ())), preferred_element_type=F32)


def _dot_tn(a, b):
    return lax.dot_general(a, b, (((0,), (0,)), ((), ())), preferred_element_type=F32)


def _sigmoid(z):
    return 0.5 * jnp.tanh(0.5 * z) + 0.5


def _dma_sems(*shape):
    return pltpu.SemaphoreType.DMA(shape)


class _Ride:
    def __init__(self, arrays, out_shapes, n_copies, copies):
        self.arrays, self.out_shapes, self.n_copies, self.copies = list(arrays), list(out_shapes), n_copies, copies


def _call_with_ride(body, ride, first, last, *, in_specs, out_specs, out_shape, scratch_shapes=(), **kw):
    in_specs, out_specs, out_shape, scratch_shapes = list(in_specs), list(out_specs), list(out_shape), list(scratch_shapes)
    n_in, n_out, n_sc = len(in_specs), len(out_specs), len(scratch_shapes)
    if ride is None:
        def run_plain(*operands):
            return pl.pallas_call(body, in_specs=in_specs, out_specs=out_specs, out_shape=out_shape,
                                  scratch_shapes=scratch_shapes, **kw)(*operands), []
        return run_plain
    n_ri, n_ro = len(ride.arrays), len(ride.out_shapes)

    def wrapped(*refs):
        ins, rest = refs[:n_in], refs[n_in:]
        r_ins, rest = rest[:n_ri], rest[n_ri:]
        outs, rest = rest[:n_out], rest[n_out:]
        r_outs, rest = rest[:n_ro], rest[n_ro:]
        scratch, (send_sems, recv_sems) = rest[:n_sc], rest[n_sc:]

        @pl.when(first())
        def _():
            for cp in ride.copies(r_ins, r_outs, send_sems, recv_sems):
                cp.start()

        body(*ins, *outs, *scratch)

        @pl.when(last())
        def _():
            for cp in ride.copies(r_ins, r_outs, send_sems, recv_sems):
                cp.wait()

    def run(*operands):
        res = pl.pallas_call(
            wrapped, in_specs=in_specs + [ANY] * n_ri, out_specs=out_specs + [ANY] * n_ro,
            out_shape=out_shape + ride.out_shapes,
            scratch_shapes=scratch_shapes + [_dma_sems(ride.n_copies), _dma_sems(ride.n_copies)], **kw,
        )(*operands, *ride.arrays)
        return res[:n_out], res[n_out:]
    return run


def _bucket_tables():
    i = np.arange(BLK)[:, None]
    j = np.arange(2 * BLK)[None, :]
    dist = BLK + i - j
    valid = (dist >= 0) & (dist <= BLK)
    tabs = []
    for _, dil in GROUPS:
        n = (np.clip(dist, 0, BLK) * dil).astype(np.int32)
        max_exact = NUM_BUCKETS // 2
        nf = np.maximum(n, 1).astype(np.float32)
        large = max_exact + (np.log(nf / np.float32(max_exact)) / np.float32(math.log(MAX_DISTANCE / max_exact))
                             * np.float32(NUM_BUCKETS - max_exact)).astype(np.int32)
        large = np.minimum(large, NUM_BUCKETS - 1)
        bucket = np.where(n < max_exact, n, large)
        tab = np.where(valid, bucket, -1).astype(np.int32)
        perm = _block_perm(dil)
        tabs.append(tab[perm][:, np.concatenate([perm, BLK + perm])])
    return np.stack(tabs)


def _bias_table(rel_bias, buckets):
    def body(rb_ref, bk_ref, out_ref):
        g = pl.program_id(0)
        bk = bk_ref[...]
        for h in range(NH):
            acc = jnp.full((BLK, 2 * BLK), NEG, F32)
            for b in range(NUM_BUCKETS):
                acc = jnp.where(bk == b, rb_ref[b, g * NH + h], acc)
            out_ref[h] = acc

    return pl.pallas_call(
        body, name="bias_table", grid=(NG,),
        in_specs=[pl.BlockSpec(memory_space=pltpu.SMEM),
                  pl.BlockSpec((None, BLK, 2 * BLK), lambda g: (g, 0, 0))],
        out_specs=pl.BlockSpec((NH, BLK, 2 * BLK), lambda g: (g, 0, 0)),
        out_shape=_sds((NG * NH, BLK, 2 * BLK)),
        compiler_params=_params("arbitrary"),
    )(rel_bias, buckets)


def _bias_grad(ds_acc, buckets, ride):
    def body(acc_ref, bk_ref, out_ref):
        bk = bk_ref[...]
        acc = acc_ref[...]
        lane = lax.broadcasted_iota(jnp.int32, (8, 128), 1)
        out = jnp.zeros((8, 128), F32)
        for b in range(NUM_BUCKETS):
            val = jnp.sum(jnp.where(bk == b, acc, 0.0))
            out = jnp.where(lane == b, val, out)
        out_ref[...] = out

    (out,), rode = _call_with_ride(
        body, ride, lambda: pl.program_id(0) == 0, lambda: pl.program_id(0) == NG * NH - 1,
        name="bias_grad", grid=(NG * NH,),
        in_specs=[pl.BlockSpec((None, BLK, 2 * BLK), lambda gh: (gh, 0, 0)),
                  pl.BlockSpec((None, BLK, 2 * BLK), lambda gh: (gh // NH, 0, 0))],
        out_specs=[pl.BlockSpec((None, 8, 128), lambda gh: (gh, 0, 0))],
        out_shape=[_sds((NG * NH, 8, 128))],
        compiler_params=_params("arbitrary"),
    )(ds_acc, buckets)
    return out, rode


def _mod_partial(c_all, w_ada_s, b_ada_s):
    def body(c_ref, w_ref, b_ref, o_ref):
        o_ref[...] = _dot(c_ref[...].astype(BF16), w_ref[...].astype(BF16)) + b_ref[...]

    return pl.pallas_call(body, name="mod_partial", out_shape=_sds((8, w_ada_s.shape[1])),
                          compiler_params=_params())(c_all, w_ada_s, b_ada_s)


def _prenorm(x, norm_g, mod):
    S = x.shape[0]
    tm = 512

    def body(x_ref, g_ref, mod_ref, h_ref):
        xv = x_ref[...]
        r = lax.rsqrt(jnp.mean(xv * xv, axis=-1, keepdims=True) + EPS)
        n1 = xv * r * g_ref[...]
        h_ref[...] = (n1 * (1.0 + mod_ref[:, D:2 * D]) + mod_ref[:, 0:D]).astype(BF16)

    return pl.pallas_call(
        body, name="prenorm", grid=(S // tm,),
        in_specs=[pl.BlockSpec((tm, D), lambda i: (i, 0)), pl.BlockSpec((1, D), lambda i: (0, 0)),
                  pl.BlockSpec((1, 3 * D), lambda i: (0, 0))],
        out_specs=pl.BlockSpec((tm, D), lambda i: (i, 0)),
        out_shape=_sds((S, D), BF16), compiler_params=_params("parallel"),
    )(x, norm_g, mod)


def _proj(h, wg_in, j0, nj, dtype, name):
    S = h.shape[0]
    tm = 2048
    per = wg_in.shape[2] // CB

    def body(h_ref, w_ref, o_ref):
        o_ref[...] = _dot(h_ref[...], w_ref[...]).astype(dtype)

    return pl.pallas_call(
        body, name=name, grid=(S // tm, nj),
        in_specs=[pl.BlockSpec((tm, D), lambda m, j: (m, 0)),
                  pl.BlockSpec((None, D, CB), lambda m, j: ((j0 + j) // per, 0, (j0 + j) % per))],
        out_specs=pl.BlockSpec((tm, CB), lambda m, j: (m, j)),
        out_shape=_sds((S, nj * CB), dtype), compiler_params=_params("parallel", "parallel"),
    )(h, wg_in)


HS = 4
SLAB = HS * HD


def _lane_head(rows):
    return lax.broadcasted_iota(jnp.int32, (rows, SLAB), 1) // HD


def _head_stack(a):
    head = _lane_head(a.shape[0])
    return jnp.concatenate([jnp.where(head == h, a, jnp.zeros_like(a)) for h in range(HS)], axis=0)


def _head_unstack(a):
    rows = a.shape[0] // HS
    head = _lane_head(rows)
    out = a[:rows]
    for h in range(1, HS):
        out = jnp.where(head == h, a[h * rows:(h + 1) * rows], out)
    return out


STAT_W = 128
VIEW = 16


def _sub_layout(dil):
    if dil == 1:
        return BLK, [None]
    return BLK * dil // VIEW, [[r + dil * u for u in range(VIEW // dil)] for r in range(dil)]


def _block_perm(dil):
    a_rows, _ = _sub_layout(dil)
    p = np.arange(BLK)
    return p if dil == 1 else (VIEW // dil) * (p % a_rows) + p // a_rows


LB = 128
N_SLAB = NH // HS


RBS = 4


def _ld(refs, bs, s, w, rb=0):
    if bs is None:
        return refs[0][rb * BLK:(rb + 1) * BLK, s * w:(s + 1) * w]
    a_rows = refs[0].shape[0] // VIEW
    return jnp.concatenate([jnp.concatenate([ref[pl.ds(b, a_rows, stride=VIEW), :] for b in bs], axis=0)
                            for ref in refs], axis=1)


def _st(ref, bs, s, val, rb=0):
    if bs is None:
        ref[rb * BLK:(rb + 1) * BLK, s * SLAB:(s + 1) * SLAB] = val.astype(ref.dtype)
        return
    a_rows = val.shape[0] // len(bs)
    for u, b in enumerate(bs):
        ref[:, b, :] = val[u * a_rows:(u + 1) * a_rows]


def _attn_views(dil, S):
    a_rows, subs = _sub_layout(dil)
    if dil == 1:
        def ispecs(base, w, f):
            return [pl.BlockSpec((RBS * BLK, N_SLAB * w), lambda sg, n: (f(n), base // (N_SLAB * w)))]
        return subs, S // (RBS * BLK), N_SLAB, RBS, ispecs, (lambda w: (S, w)), (
            lambda f: pl.BlockSpec((RBS * BLK, AW), lambda sg, n: (f(n), 0)))

    def ispecs(base, w, f):
        return [pl.BlockSpec((a_rows * VIEW, LB), lambda sg, n, k=k: (f(n), (base + sg * w) // LB + k))
                for k in range(w // LB)]
    return subs, S // (a_rows * VIEW), 1, 1, ispecs, (lambda w: (S // VIEW, VIEW, w)), (
        lambda f: pl.BlockSpec((a_rows, VIEW, SLAB), lambda sg, n: (f(n), 0, sg)))


def _attn_fwd(qkv_g, bias_tab, g):
    S = qkv_g.shape[0]
    subs, nbq, sps, rbs, ispecs, shape, ospec = _attn_views(GROUPS[g][1], S)
    cur = lambda n: n
    in_specs = [ispecs(0, SLAB, cur), ispecs(AW, SLAB, cur), ispecs(2 * AW, SLAB, cur)]
    nl = len(in_specs[0])

    def body(*refs):
        q, k, v = (refs[t * nl:(t + 1) * nl] for t in range(3))
        b_ref, o_ref, l_ref, kprev, vprev = refs[3 * nl:]
        n = pl.program_id(1)

        @pl.when(n == 0)
        def _():
            kprev[...] = jnp.zeros_like(kprev)
            vprev[...] = jnp.zeros_like(vprev)

        col = lax.broadcasted_iota(jnp.int32, (HS * BLK, 2 * BLK), 1)
        first = (col >= BLK) | (n > 0)
        for s_, rb, (i, bs) in ((s_, rb, sub) for s_ in range(sps) for rb in range(rbs) for sub in enumerate(subs)):
            cs = slice(s_ * SLAB, (s_ + 1) * SLAB)
            kc, vc = _ld(k, bs, s_, SLAB, rb).astype(BF16), _ld(v, bs, s_, SLAB, rb).astype(BF16)
            kb = jnp.concatenate([kprev[i, :, cs], kc], axis=0)
            vb = jnp.concatenate([vprev[i, :, cs], vc], axis=0)
            kprev[i, :, cs], vprev[i, :, cs] = kc, vc
            s = _dot_nt(_head_stack(_ld(q, bs, s_, SLAB, rb).astype(BF16)), kb) * (HD ** -0.5)
            s = s + b_ref[pl.ds(s_ * HS, HS)].reshape(HS * BLK, 2 * BLK)
            if rb == 0:
                s = jnp.where(first, s, NEG)
            m = jnp.max(s, axis=-1, keepdims=True)
            p = jnp.exp(s - m)
            den = jnp.sum(p, axis=-1, keepdims=True)
            _st(o_ref, bs, s_, _head_unstack(_dot(p.astype(BF16), vb) / den), rb)
            _st(l_ref, bs, s_, _head_unstack(jnp.broadcast_to(m + jnp.log(den), (HS * BLK, SLAB))), rb)

    out = _sds(shape(AW))
    nsg = N_SLAB // sps
    o, l = pl.pallas_call(
        body, name=f"attn_fwd{g}", grid=(nsg, nbq),
        in_specs=sum(in_specs, []) + [pl.BlockSpec((sps * HS, BLK, 2 * BLK), lambda sg, n: (g * nsg + sg, 0, 0))],
        out_specs=[ospec(cur), ospec(cur)],
        out_shape=[out, out],
        scratch_shapes=[pltpu.VMEM((len(subs), BLK, sps * SLAB), BF16)] * 2,
        compiler_params=_params("parallel", "arbitrary"),
    )(*([qkv_g] * (3 * nl)), bias_tab)
    return o.reshape(S, AW), l.reshape(S, AW)


def _attn_bwd(qkv_g, dattn, stats, bias_tab, g, ride):
    S = qkv_g.shape[0]
    subs, nbq, sps, rbs, ispecs, shape, ospec = _attn_views(GROUPS[g][1], S)
    cur = lambda n: jnp.minimum(n, nbq - 1)
    late = lambda n: jnp.maximum(n - 1, 0)
    in_specs = [ispecs(0, SLAB, cur), ispecs(AW, SLAB, cur), ispecs(2 * AW, SLAB, cur), ispecs(0, SLAB, cur),
                ispecs(0, STAT_W, cur)]
    nl = len(in_specs[0])

    def body(*refs):
        q, k, v, da = (refs[t * nl:(t + 1) * nl] for t in range(4))
        st_ref, b_ref, dq_ref, dk_ref, dv_ref, ds_ref, ck_ref, cv_ref, kprev, vprev, *held = refs[4 * nl:]
        n = pl.program_id(1)

        @pl.when(n == 0)
        def _():
            for ref in (ds_ref, ck_ref, cv_ref, kprev, vprev, *held):
                ref[...] = jnp.zeros_like(ref)

        def finish(ref, t, bs, s_, rb, val):
            cs = slice(s_ * SLAB, (s_ + 1) * SLAB)
            if rbs == 1:
                _st(ref, bs, s_, val)
            elif rb == 0:
                for j in range(rbs - 1):
                    _st(ref, bs, s_, held[t][j * BLK:(j + 1) * BLK, cs], j)
                _st(ref, bs, s_, val, rbs - 1)
            else:
                held[t][(rb - 1) * BLK:rb * BLK, cs] = val

        @pl.when(n < nbq)
        def _():
            col = lax.broadcasted_iota(jnp.int32, (HS * BLK, 2 * BLK), 1)
            first = (col >= BLK) | (n > 0)
            for s_, rb, (i, bs) in ((s_, rb, sub) for s_ in range(sps) for rb in range(rbs) for sub in enumerate(subs)):
                cs = slice(s_ * SLAB, (s_ + 1) * SLAB)
                st = _ld((st_ref,), bs, s_, STAT_W, rb)
                kc, vc = _ld(k, bs, s_, SLAB, rb).astype(BF16), _ld(v, bs, s_, SLAB, rb).astype(BF16)
                kb = jnp.concatenate([kprev[i, :, cs], kc], axis=0)
                vb = jnp.concatenate([vprev[i, :, cs], vc], axis=0)
                kprev[i, :, cs], vprev[i, :, cs] = kc, vc
                lse = jnp.concatenate([st[:, h:h + 1] for h in range(HS)], axis=0)
                delta = jnp.concatenate([st[:, HS + h:HS + h + 1] for h in range(HS)], axis=0)
                qs = _head_stack(_ld(q, bs, s_, SLAB, rb).astype(BF16))
                dos = _head_stack(_ld(da, bs, s_, SLAB, rb).astype(BF16))
                s = _dot_nt(qs, kb) * (HD ** -0.5) + b_ref[pl.ds(s_ * HS, HS)].reshape(HS * BLK, 2 * BLK)
                if rb == 0:
                    s = jnp.where(first, s, NEG)
                p = jnp.exp(s - lse)
                ds = p * (_dot_nt(dos, vb) - delta)
                ds_ref[pl.ds(s_ * HS, HS)] += ds.reshape(HS, BLK, 2 * BLK)
                ds_b = (ds * (HD ** -0.5)).astype(BF16)
                _st(dq_ref, bs, s_, _head_unstack(_dot(ds_b, kb)), rb)
                dkb = _dot_tn(ds_b, qs)
                dvb = _dot_tn(p.astype(BF16), dos)
                finish(dk_ref, 0, bs, s_, rb, ck_ref[i, :, cs] + dkb[:BLK])
                finish(dv_ref, 1, bs, s_, rb, cv_ref[i, :, cs] + dvb[:BLK])
                ck_ref[i, :, cs] = dkb[BLK:]
                cv_ref[i, :, cs] = dvb[BLK:]

        @pl.when(n == nbq)
        def _():
            for s_ in range(sps):
                for i, bs in enumerate(subs):
                    finish(dk_ref, 0, bs, s_, 0, ck_ref[i, :, s_ * SLAB:(s_ + 1) * SLAB])
                    finish(dv_ref, 1, bs, s_, 0, cv_ref[i, :, s_ * SLAB:(s_ + 1) * SLAB])

    out = _sds(shape(AW), BF16 if GROUPS[g][1] == 1 else F32)
    nsg = N_SLAB // sps
    (dq, dk, dv, ds_acc), rode = _call_with_ride(
        body, ride, lambda: (pl.program_id(0) == 0) & (pl.program_id(1) == 0),
        lambda: (pl.program_id(0) == nsg - 1) & (pl.program_id(1) == nbq),
        name=f"attn_bwd{g}", grid=(nsg, nbq + 1),
        in_specs=sum(in_specs, []) + [pl.BlockSpec((sps * HS, BLK, 2 * BLK), lambda sg, n: (g * nsg + sg, 0, 0))],
        out_specs=[ospec(cur), ospec(late), ospec(late),
                   pl.BlockSpec((sps * HS, BLK, 2 * BLK), lambda sg, n: (sg, 0, 0))],
        out_shape=[out] * 3 + [_sds((NH, BLK, 2 * BLK))],
        scratch_shapes=[pltpu.VMEM((len(subs), BLK, sps * SLAB), F32)] * 2
        + [pltpu.VMEM((len(subs), BLK, sps * SLAB), BF16)] * 2 + [pltpu.VMEM(((rbs - 1) * BLK, sps * SLAB), F32)] * (2 if rbs > 1 else 0),
        compiler_params=_params("arbitrary", "arbitrary"),
    )(*([qkv_g] * (3 * nl)), *([dattn] * nl), stats, bias_tab)
    return [dq.reshape(S, AW), dk.reshape(S, AW), dv.reshape(S, AW)], ds_acc, rode


TM_MIX = 256


def _mix_specs(tm):
    row512 = pl.BlockSpec((tm, AW), lambda i: (i, 0))
    return ([row512] * 6 + [
        pl.BlockSpec((tm, REST_W), lambda i: (i, 0)),
        pl.BlockSpec((HALO, AW), lambda i: (jnp.maximum(i * (tm // HALO) - 1, 0), 1)),
        pl.BlockSpec((AW, D), lambda i: (0, 0)), pl.BlockSpec((AW, D), lambda i: (0, 0)),
        pl.BlockSpec((4, PGW, PGW), lambda i: (0, 0, 0)), pl.BlockSpec((1, AW), lambda i: (0, 0))])


def _mix_forward(i, tm, o_refs, l_refs, rest_ref, halo_ref, wab_ref, wpb_ref, pw_ref, ps_ref):
    l0, l1, l2 = (r[...] for r in l_refs)
    mx = jnp.maximum(jnp.maximum(l0, l1), l2)
    e0, e1, e2 = jnp.exp(l0 - mx), jnp.exp(l1 - mx), jnp.exp(l2 - mx)
    den = e0 + e1 + e2
    lj = mx + jnp.log(den)
    attn = (e0 * o_refs[0][...] + e1 * o_refs[1][...] + e2 * o_refs[2][...]) / den

    z_attn = rest_ref[:, 0:AW]
    u = rest_ref[:, AW:2 * AW]
    z_pool = rest_ref[:, 2 * AW:3 * AW]
    g_attn = rest_ref[:, 3 * AW:3 * AW + D]
    g_pool = rest_ref[:, 3 * AW + D:3 * AW + 2 * D]

    sg_a = _sigmoid(z_attn)
    sil_a = z_attn * sg_a
    a_g = (attn * sil_a).astype(BF16)
    y_attn = _dot(a_g, wab_ref[...])

    halo = jnp.where(i > 0, halo_ref[...], 0.0)
    ext = jnp.concatenate([halo, u], axis=0)
    t = i * tm + lax.broadcasted_iota(jnp.int32, (tm, 1), 0)
    pooled, mixed_raw = [], []
    for gi, win in enumerate(POOL_WINDOWS):
        s = ext[:, gi * PGW:(gi + 1) * PGW]
        sh = 1
        while sh < win:
            s = s + pltpu.roll(s, sh, 0)
            sh *= 2
        cnt = jnp.minimum(t + 1, win).astype(F32)
        pg = s[HALO:] / cnt - u[:, gi * PGW:(gi + 1) * PGW]
        pooled.append(pg.astype(BF16))
        mixed_raw.append(_dot(pooled[-1], pw_ref[gi].astype(BF16)))
    mixed_raw = jnp.concatenate(mixed_raw, axis=1)
    mixed = mixed_raw * ps_ref[...]
    sg_p = _sigmoid(z_pool)
    sil_p = z_pool * sg_p
    m_g = (mixed * sil_p).astype(BF16)
    y_pool = _dot(m_g, wpb_ref[...])

    sa = _sigmoid(g_attn)
    sp = _sigmoid(g_pool)
    merged = sa * y_attn + sp * y_pool
    return dict(lj=lj, attn=attn, z_attn=z_attn, z_pool=z_pool, sg_a=sg_a, sil_a=sil_a, a_g=a_g, y_attn=y_attn,
                pooled=pooled, mixed_raw=mixed_raw, mixed=mixed, sg_p=sg_p, sil_p=sil_p, m_g=m_g, y_pool=y_pool,
                sa=sa, sp=sp, merged=merged)


def _mix_step(x, target, os_, ls_, rest, wab, wpb, pool_w, pool_scale, wout, mod, final_g):
    S = x.shape[0]
    tm = TM_MIX
    nt = S // tm
    sw = D // N_SHARD

    def body(o0, o1, o2, l0, l1, l2, rest_ref, halo_ref, wab_ref, wpb_ref, pw_ref, ps_ref,
             x_ref, t_ref, wo_ref, mod_ref, fg_ref, dx2_ref, loss_ref, dfg_ref, dgate_ref,
             dattn_ref, stats_ref, dpooled_ref, dproj_hbm, dwo_hbm, dwab_hbm, dwpb_hbm, dpw_ref, dps_ref,
             awo, awab, awpb, stage, stage_sem):
        i = pl.program_id(0)
        slot = i % 2

        def staged(step, sl):
            return pltpu.make_async_copy(stage.at[sl], dproj_hbm.at[pl.ds(step * tm, tm), pl.ds(QKV_W, REST_W)],
                                         stage_sem.at[sl])

        @pl.when(i == 0)
        def _():
            for ref in (loss_ref, dfg_ref, dgate_ref, awo, awab, awpb, dpw_ref, dps_ref):
                ref[...] = jnp.zeros_like(ref)

        f = _mix_forward(i, tm, (o0, o1, o2), (l0, l1, l2), rest_ref, halo_ref, wab_ref, wpb_ref, pw_ref, ps_ref)
        mo = _dot(f["merged"].astype(BF16), wo_ref[...])
        gate = mod_ref[:, 2 * D:3 * D]
        fg = fg_ref[...]
        x2 = x_ref[...] + gate * mo
        r2 = lax.rsqrt(jnp.mean(x2 * x2, axis=-1, keepdims=True) + EPS)
        n2 = x2 * r2
        err = n2 * fg - t_ref[...]
        loss_ref[...] += 0.5 * jnp.sum(jnp.mean(err * err, axis=-1, keepdims=True))
        dy = err * (1.0 / D)
        dfg_ref[...] += jnp.sum(dy * n2, axis=0, keepdims=True)
        dn = dy * fg
        dx2 = r2 * (dn - n2 * jnp.mean(dn * n2, axis=-1, keepdims=True))
        dgate_ref[...] += jnp.sum(dx2 * mo, axis=0, keepdims=True)
        dx2_ref[...] = dx2

        dmo_b = (dx2 * gate).astype(BF16)
        dmerged = _dot_nt(dmo_b, wo_ref[...])
        awo[...] += _dot_tn(f["merged"].astype(BF16), dmo_b)
        sa, sp = f["sa"], f["sp"]
        dya = (dmerged * sa).astype(BF16)
        dyp = (dmerged * sp).astype(BF16)
        dg_attn = dmerged * f["y_attn"] * sa * (1.0 - sa)
        dg_pool = dmerged * f["y_pool"] * sp * (1.0 - sp)
        dag = _dot_nt(dya, wab_ref[...])
        awab[...] += _dot_tn(f["a_g"], dya)
        dmg = _dot_nt(dyp, wpb_ref[...])
        awpb[...] += _dot_tn(f["m_g"], dyp)
        dattn = dag * f["sil_a"]
        dattn_ref[...] = dattn
        prod = dattn * f["attn"]
        lane = lax.broadcasted_iota(jnp.int32, (tm, STAT_W), 1)
        for sb in range(N_SLAB):
            st = jnp.zeros((tm, STAT_W), F32)
            for h in range(HS):
                hs = slice((sb * HS + h) * HD, (sb * HS + h + 1) * HD)
                st = jnp.where(lane == h, f["lj"][:, hs.start:hs.start + 1], st)
                st = jnp.where(lane == HS + h, jnp.sum(prod[:, hs], axis=-1, keepdims=True), st)
            stats_ref[:, sb * STAT_W:(sb + 1) * STAT_W] = st
        dz_attn = dag * f["attn"] * (f["sg_a"] * (1.0 + f["z_attn"] * (1.0 - f["sg_a"])))
        dmixed = dmg * f["sil_p"]
        dz_pool = dmg * f["mixed"] * (f["sg_p"] * (1.0 + f["z_pool"] * (1.0 - f["sg_p"])))
        dps_ref[...] += jnp.sum(dmixed * f["mixed_raw"], axis=0, keepdims=True)
        dpm = (dmixed * ps_ref[...]).astype(BF16)
        for gi in range(len(POOL_WINDOWS)):
            cs = slice(gi * PGW, (gi + 1) * PGW)
            dpw_ref[gi] += _dot_tn(f["pooled"][gi], dpm[:, cs])
            dpooled_ref[:, cs] = _dot_nt(dpm[:, cs], pw_ref[gi].astype(BF16))
        @pl.when(i >= 2)
        def _():
            staged(i - 2, slot).wait()

        stage[slot, :, 0:AW] = dz_attn.astype(BF16)
        stage[slot, :, AW:2 * AW] = jnp.zeros((tm, AW), BF16)
        stage[slot, :, 2 * AW:3 * AW] = dz_pool.astype(BF16)
        stage[slot, :, 3 * AW:3 * AW + D] = dg_attn.astype(BF16)
        stage[slot, :, 3 * AW + D:3 * AW + 2 * D] = dg_pool.astype(BF16)
        staged(i, slot).start()

        @pl.when(i == nt - 1)
        def _():
            staged(i - 1, 1 - slot).wait()
            staged(i, slot).wait()
            pltpu.sync_copy(awo, dwo_hbm)
            for k in range(N_SHARD):
                pltpu.sync_copy(awab.at[:, pl.ds(k * sw, sw)], dwab_hbm.at[k])
                pltpu.sync_copy(awpb.at[:, pl.ds(k * sw, sw)], dwpb_hbm.at[k])

    row = pl.BlockSpec((tm, D), lambda i: (i, 0))
    vec = pl.BlockSpec((1, D), lambda i: (0, 0))
    row512 = pl.BlockSpec((tm, AW), lambda i: (i, 0))
    outs = pl.pallas_call(
        body, name="mix_step", grid=(nt,),
        in_specs=_mix_specs(tm) + [row, row, pl.BlockSpec((D, D), lambda i: (0, 0)),
                                   pl.BlockSpec((1, 3 * D), lambda i: (0, 0)), vec],
        out_specs=[row, pl.BlockSpec((8, 128), lambda i: (0, 0)), vec, vec,
                   row512, pl.BlockSpec((tm, N_SLAB * STAT_W), lambda i: (i, 0)), row512, ANY, ANY, ANY, ANY,
                   pl.BlockSpec((4, PGW, PGW), lambda i: (0, 0, 0)), pl.BlockSpec((1, AW), lambda i: (0, 0))],
        out_shape=[_sds((S, D)), _sds((8, 128)), _sds((1, D)), _sds((1, D)),
                   _sds((S, AW)), _sds((S, N_SLAB * STAT_W)), _sds((S, AW)), _sds((S, IN_W), BF16),
                   _sds((D, D)), _sds((N_SHARD, AW, sw)), _sds((N_SHARD, AW, sw)), _sds((4, PGW, PGW)), _sds((1, AW))],
        scratch_shapes=[pltpu.VMEM((D, D), F32), pltpu.VMEM((AW, D), F32), pltpu.VMEM((AW, D), F32),
                        pltpu.VMEM((2, tm, REST_W), BF16), _dma_sems(2)],
        compiler_params=_params("arbitrary"),
    )(*os_, *ls_, rest, rest, wab, wpb, pool_w, pool_scale, x, target, wout, mod, final_g)
    dx2, loss, dfg, dgate, dattn, stats, dpooled, dproj, dwo, dwab, dwpb, dpw, dps = outs
    return (dx2, loss, dfg, dgate, dattn, stats, dpooled, dproj, dwo.reshape(N_SHARD, D // N_SHARD, D), dwab, dwpb,
            dpw, dps)


def _pool_bwd(dpooled):
    S = dpooled.shape[0]
    tm = 512
    nt = S // tm

    def body(dp_ref, nxt_ref, du_ref):
        i = pl.program_id(0)
        t = i * tm + lax.broadcasted_iota(jnp.int32, (tm + HALO, 1), 0)
        nxt = jnp.where(i < nt - 1, nxt_ref[...], 0.0)
        ext = jnp.concatenate([dp_ref[...], nxt], axis=0)
        for gi, win in enumerate(POOL_WINDOWS):
            cs = slice(gi * PGW, (gi + 1) * PGW)
            s = ext[:, cs] / jnp.minimum(t + 1, win).astype(F32)
            sh = 1
            while sh < win:
                s = s + pltpu.roll(s, tm + HALO - sh, 0)
                sh *= 2
            du_ref[:, cs] = (s[:tm] - dp_ref[:, cs]).astype(BF16)

    return pl.pallas_call(
        body, name="pool_bwd", grid=(nt,),
        in_specs=[pl.BlockSpec((tm, AW), lambda i: (i, 0)),
                  pl.BlockSpec((HALO, AW), lambda i: (jnp.minimum((i + 1) * (tm // HALO), S // HALO - 1), 0))],
        out_specs=pl.BlockSpec((tm, AW), lambda i: (i, 0)),
        out_shape=_sds((S, AW), BF16), compiler_params=_params("parallel"),
    )(dpooled, dpooled)


TB = 1024


def _dh(dproj, wg_in, ride):
    S = dproj.shape[0]
    per = wg_in.shape[2] // TB
    nm, nk = S // TB, IN_W // TB

    def body(dp_ref, w_ref, out_ref):
        @pl.when(pl.program_id(1) == 0)
        def _():
            out_ref[...] = jnp.zeros_like(out_ref)

        out_ref[...] += _dot_nt(dp_ref[...], w_ref[...])

    (dh,), rode = _call_with_ride(
        body, ride, lambda: (pl.program_id(0) == 0) & (pl.program_id(1) == 0),
        lambda: (pl.program_id(0) == nm - 1) & (pl.program_id(1) == nk - 1),
        name="dh", grid=(nm, nk),
        in_specs=[pl.BlockSpec((TB, TB), lambda m, kk: (m, kk)),
                  pl.BlockSpec((None, D, TB), lambda m, kk: (kk // per, 0, kk % per))],
        out_specs=[pl.BlockSpec((TB, D), lambda m, kk: (m, 0))],
        out_shape=[_sds((S, D))], compiler_params=_params("arbitrary", "arbitrary"),
    )(dproj, wg_in)
    return dh, rode


def _dw_in(h, dproj):
    S = dproj.shape[0]
    per = IN_W // N_SHARD // TB

    def body(h_ref, dp_ref, out_ref):
        out_ref[...] = _dot_tn(h_ref[...], dp_ref[...])

    return pl.pallas_call(
        body, name="dw_in", grid=(IN_W // TB,),
        in_specs=[pl.BlockSpec((S, D), lambda j: (0, 0)), pl.BlockSpec((S, TB), lambda j: (0, j))],
        out_specs=pl.BlockSpec((None, D, TB), lambda j: (j // per, 0, j % per)),
        out_shape=_sds((N_SHARD, D, IN_W // N_SHARD)), compiler_params=_params("parallel"),
    )(h, dproj)


def _prenorm_bwd(x, dh, dx2, norm_g, mod):
    S = x.shape[0]
    tm = 512

    def body(x_ref, dh_ref, dx2_ref, g_ref, mod_ref, gx_ref, dg_ref, dshift_ref, dscale_ref):
        i = pl.program_id(0)

        @pl.when(i == 0)
        def _():
            dg_ref[...] = jnp.zeros_like(dg_ref)
            dshift_ref[...] = jnp.zeros_like(dshift_ref)
            dscale_ref[...] = jnp.zeros_like(dscale_ref)

        xv = x_ref[...]
        dhv = dh_ref[...]
        g = g_ref[...]
        r = lax.rsqrt(jnp.mean(xv * xv, axis=-1, keepdims=True) + EPS)
        xh = xv * r
        dshift_ref[...] += jnp.sum(dhv, axis=0, keepdims=True)
        dscale_ref[...] += jnp.sum(dhv * (xh * g), axis=0, keepdims=True)
        dn1 = dhv * (1.0 + mod_ref[:, D:2 * D])
        dg_ref[...] += jnp.sum(dn1 * xh, axis=0, keepdims=True)
        dxh = dn1 * g
        gx_ref[...] = dx2_ref[...] + r * (dxh - xh * jnp.mean(dxh * xh, axis=-1, keepdims=True))

    row = pl.BlockSpec((tm, D), lambda i: (i, 0))
    vec = pl.BlockSpec((1, D), lambda i: (0, 0))
    return pl.pallas_call(
        body, name="prenorm_bwd", grid=(S // tm,),
        in_specs=[row, row, row, vec, pl.BlockSpec((1, 3 * D), lambda i: (0, 0))],
        out_specs=[row, vec, vec, vec],
        out_shape=[_sds((S, D)), _sds((1, D)), _sds((1, D)), _sds((1, D))],
        compiler_params=_params("arbitrary"),
    )(x, dh, dx2, norm_g, mod)


def _local_step(x, target, mod, wg_in, wab, wpb, wout, pool_w, pool_scale, rel_bias, norm_g, final_g, chip_half):
    buckets = jnp.asarray(_bucket_tables())
    bias_tab = _bias_table(rel_bias, buckets)
    h = _prenorm(x, norm_g, mod)
    qkv = [_proj(h, wg_in, 3 * g, 3, BF16 if GROUPS[g][1] == 1 else F32, f"proj_qkv{g}") for g in range(NG)]
    rest = _proj(h, wg_in, NCB_QKV, REST_W // CB, F32, "proj_rest")
    os_, ls_ = zip(*[_attn_fwd(qkv[g], bias_tab, g) for g in range(NG)])
    (dx2, loss, dfinal_g, dgate, dattn, stats, dpooled, dproj, dw_out, dw_ab, dw_pb, dpool_w,
     dpool_scale) = _mix_step(x, target, os_, ls_, rest, wab, wpb, pool_w, pool_scale, wout, mod, final_g)
    du = _pool_bwd(dpooled)

    small = [dw_ab, dw_pb, dw_out]
    dqkv0, ds0, sib_small = _attn_bwd(qkv[0], dattn, stats, bias_tab, 0, _ride_sibling_halves(small))
    p_small = _pair_sum_small(small, sib_small, chip_half)
    dqkv1, ds1, u_small = _attn_bwd(qkv[1], dattn, stats, bias_tab, 1,
                                    _ride_chip_exchange([p16 for _, p16 in p_small]))
    rs_ab, rs_pb, rs_out = _chip_sum_small([p32 for p32, _ in p_small], u_small, chip_half)
    dqkv2, ds2, _ = _attn_bwd(qkv[2], dattn, stats, bias_tab, 2, None)

    for j, piece in enumerate(dqkv0 + dqkv1 + dqkv2):
        dproj = lax.dynamic_update_slice(dproj, piece.astype(BF16), (0, j * AW))
    dproj = lax.dynamic_update_slice(dproj, du, (0, QKV_W + AW))
    dw_in = _dw_in(h, dproj)
    drel_rows, (sib_in,) = _bias_grad(jnp.concatenate([ds0, ds1, ds2], axis=0), buckets,
                                      _ride_sibling_halves([dw_in]))
    drel = drel_rows[:, 0, :NUM_BUCKETS].T
    p32_in, p16_in = _pair_sum(dw_in, sib_in, chip_half, "rs_pair_sum_in")
    dh, (u_in,) = _dh(dproj, wg_in, _ride_chip_exchange([p16_in]))
    rs_in = _chip_sum(p32_in, u_in, chip_half, "rs_chip_sum_in")

    grad_x, dnorm_g, dshift, dscale = _prenorm_bwd(x, dh, dx2, norm_g, mod)
    dmod = jnp.concatenate([dshift, dscale, dgate], axis=1)
    return dict(loss=loss[0, 0], grad_x=grad_x, dmod=dmod, dnorm_g=dnorm_g, dfinal_g=dfinal_g, dpool_w=dpool_w,
                dpool_scale=dpool_scale, drel_bias=drel, dw_in=dw_in, dw_attn_br=dw_ab, dw_pool_br=dw_pb,
                dw_out=dw_out, rs_in=rs_in, rs_attn_br=rs_ab, rs_pool_br=rs_pb, rs_out=rs_out)


def _allgather8(blocks, name, relay=None):
    nb = len(blocks)
    relay = [False] * nb if relay is None else list(relay)

    def body(*refs):
        ins, outs = refs[:nb], refs[nb:2 * nb]
        send_sems, recv_sems = refs[2 * nb:]
        x, y, c = lax.axis_index("x"), lax.axis_index("y"), lax.axis_index("c")
        me, sibling = (x, y, c), (x, y, 1 - c)
        here, xn, yn, dg = (x, y), (1 - x, y), (x, 1 - y), (1 - x, 1 - y)

        def slot(a, chip, core, half=None):
            ref = outs[a].at[4 * chip[0] + 2 * chip[1] + core]
            if half is None:
                return ref
            r2 = ref.shape[0] // 2
            return ref.at[pl.ds(half * r2, r2)]

        def copy(a, k, dst, to, src=None):
            return pltpu.make_async_remote_copy(src_ref=dst if src is None else src, dst_ref=dst,
                                                send_sem=send_sems.at[a, k], recv_sem=recv_sems.at[a, k],
                                                device_id=to, device_id_type=MESH)

        def start(cps):
            for cp in cps:
                cp.start()
            return cps

        sent = []
        for a in range(nb):
            own = slot(a, here, c)
            sent += [copy(a, 0, own, sibling, src=ins[a]), copy(a, 1, own, (*xn, c), src=ins[a]),
                     copy(a, 2, own, (*yn, c), src=ins[a])]
            if not relay[a]:
                sent.append(copy(a, 3, own, (*dg, c), src=ins[a]))
        start(sent)
        for a in range(nb):
            copy(a, 2, slot(a, yn, c), me).wait_recv()
            sent += start([copy(a, 6, slot(a, yn, c), sibling)]
                          + ([copy(a, 3, slot(a, yn, c, 0), (*xn, c))] if relay[a] else []))
        for a in range(nb):
            copy(a, 1, slot(a, xn, c), me).wait_recv()
            sent += start([copy(a, 5, slot(a, xn, c), sibling)]
                          + ([copy(a, 4, slot(a, xn, c, 1), (*yn, c))] if relay[a] else []))
        for a in range(nb):
            for k, half in ((3, 0), (4, 1)) if relay[a] else ((3, None),):
                copy(a, k, slot(a, dg, c, half), me).wait_recv()
                sent += start([copy(a, 4 + k, slot(a, dg, c, half), sibling)])
        for a in range(nb):
            copy(a, 0, slot(a, here, 1 - c), me).wait_recv()
            copy(a, 5, slot(a, xn, 1 - c), me).wait_recv()
            copy(a, 6, slot(a, yn, 1 - c), me).wait_recv()
            for k, half in ((7, 0), (8, 1)) if relay[a] else ((7, None),):
                copy(a, k, slot(a, dg, 1 - c, half), me).wait_recv()
        for cp in sent:
            cp.wait_send()

    outs = pl.pallas_call(
        body, name=name, in_specs=[ANY] * nb, out_specs=[ANY] * nb,
        out_shape=[_sds((8,) + b.shape, b.dtype) for b in blocks],
        scratch_shapes=[_dma_sems(nb, 9), _dma_sems(nb, 9)],
    )(*blocks)
    return [_place_own(buf, b) for buf, b in zip(outs, blocks)]


def _place_own(buf, block):
    dev = 4 * lax.axis_index("x") + 2 * lax.axis_index("y") + lax.axis_index("c")
    return lax.dynamic_update_index_in_dim(buf, block, dev, 0)


def _ride_sibling_halves(gs):
    def copies(ins, outs, send_sems, recv_sems):
        x, y, c = lax.axis_index("x"), lax.axis_index("y"), lax.axis_index("c")
        cps = []
        for a in range(len(gs)):
            r2 = ins[a].shape[1] // 2
            other = ins[a].at[:, pl.ds((1 - c) * r2, r2), :]
            cps.append(pltpu.make_async_remote_copy(src_ref=other, dst_ref=outs[a], send_sem=send_sems.at[a],
                                                    recv_sem=recv_sems.at[a], device_id=(x, y, 1 - c),
                                                    device_id_type=MESH))
        return cps

    return _Ride(gs, [_sds((g.shape[0], g.shape[1] // 2, g.shape[2]), g.dtype) for g in gs], len(gs), copies)


def _pair_sum(g, t, chip_half, name):
    nsh, rows, cols = g.shape
    r2 = rows // 2
    tr = _row_tile(r2, cols)
    nt = r2 // tr

    def body(ch_ref, g_ref, t_ref, p32_ref, p16_ref):
        p = g_ref[...] + t_ref[...]
        p16_ref[...] = p.astype(BF16)

        @pl.when(pl.program_id(1) == ch_ref[0])
        def _():
            p32_ref[...] = p

    blk = pl.BlockSpec((None, tr, cols), lambda i, k, ch_ref: (k, i, 0))
    return pl.pallas_call(
        body, name=name,
        grid_spec=pltpu.PrefetchScalarGridSpec(
            num_scalar_prefetch=1, grid=(nt, nsh),
            in_specs=[pl.BlockSpec((None, tr, cols), lambda i, k, ch_ref: (k, ch_ref[1] * nt + i, 0)), blk],
            out_specs=[pl.BlockSpec((tr, cols), lambda i, k, ch_ref: (i, 0)), blk]),
        out_shape=[_sds((r2, cols)), _sds((nsh, r2, cols), BF16)],
        compiler_params=_params("parallel", "arbitrary"),
    )(chip_half, g, t)


def _pair_sum_small(gs, ts, chip_half):
    na = len(gs)

    def body(ch_ref, *refs):
        g_refs, t_refs, outs = refs[:na], refs[na:2 * na], refs[2 * na:]
        for a in range(na):
            r2 = t_refs[a].shape[1]
            own = pl.ds(pl.multiple_of(ch_ref[1] * r2, 8), r2)
            outs[2 * a + 1][...] = (g_refs[a][:, own, :] + t_refs[a][...]).astype(BF16)
            outs[2 * a][...] = g_refs[a][ch_ref[0], own, :] + t_refs[a][ch_ref[0]]

    res = pl.pallas_call(
        body, name="rs_pair_sum_small",
        in_specs=[pl.BlockSpec(memory_space=pltpu.SMEM)] + [pl.BlockSpec(memory_space=pltpu.VMEM)] * (2 * na),
        out_shape=[s for t in ts for s in (_sds(t.shape[1:]), _sds(t.shape, BF16))], compiler_params=_params(),
    )(chip_half, *gs, *ts)
    return [(res[2 * a], res[2 * a + 1]) for a in range(na)]


def _chip_sum_small(p32s, us, chip_half):
    na = len(p32s)

    def body(ch_ref, *refs):
        p_refs, u_refs, outs = refs[:na], refs[na:2 * na], refs[2 * na:]
        for a in range(na):
            r2 = p_refs[a].shape[0]
            acc = p_refs[a][...]
            for j in range(3):
                acc = acc + u_refs[a][j].astype(F32)
            outs[a][pl.ds(pl.multiple_of(ch_ref[1] * r2, 8), r2), :] = acc

    return pl.pallas_call(
        body, name="rs_chip_sum_small",
        in_specs=[pl.BlockSpec(memory_space=pltpu.SMEM)] + [pl.BlockSpec(memory_space=pltpu.VMEM)] * (2 * na),
        out_shape=[_sds((2 * p.shape[0], p.shape[1])) for p in p32s], compiler_params=_params(),
    )(chip_half, *p32s, *us)


def _ride_chip_exchange(ps):
    def copies(ins, outs, send_sems, recv_sems):
        x, y, c = lax.axis_index("x"), lax.axis_index("y"), lax.axis_index("c")
        chips = [(1 - x, y), (x, 1 - y), (1 - x, 1 - y)]
        cps = []
        for a in range(len(ps)):
            for j, (ox, oy) in enumerate(chips):
                cps.append(pltpu.make_async_remote_copy(src_ref=ins[a].at[2 * ox + oy], dst_ref=outs[a].at[j],
                                                        send_sem=send_sems.at[3 * a + j],
                                                        recv_sem=recv_sems.at[3 * a + j],
                                                        device_id=(ox, oy, c), device_id_type=MESH))
        return cps

    return _Ride(ps, [_sds((3,) + p.shape[1:], p.dtype) for p in ps], 3 * len(ps), copies)


def _chip_sum(p32, u, chip_half, name):
    r2, cols = p32.shape
    tr = _row_tile(r2, cols)
    nt = r2 // tr

    def body(ch_ref, p_ref, u_ref, o_ref):
        acc = p_ref[...]
        for j in range(3):
            acc = acc + u_ref[j].astype(F32)
        o_ref[...] = acc

    return pl.pallas_call(
        body, name=name,
        grid_spec=pltpu.PrefetchScalarGridSpec(
            num_scalar_prefetch=1, grid=(nt,),
            in_specs=[pl.BlockSpec((tr, cols), lambda i, ch_ref: (i, 0)),
                      pl.BlockSpec((3, tr, cols), lambda i, ch_ref: (0, i, 0))],
            out_specs=pl.BlockSpec((tr, cols), lambda i, ch_ref: (ch_ref[1] * nt + i, 0))),
        out_shape=_sds((2 * r2, cols)), compiler_params=_params("parallel"),
    )(chip_half, p32, u)


def _sibling_join(fs, name):
    nb = len(fs)

    def body(*refs):
        outs = refs[nb:2 * nb]
        send_sems, recv_sems = refs[2 * nb:]
        x, y, c = lax.axis_index("x"), lax.axis_index("y"), lax.axis_index("c")
        cps = []
        for a in range(nb):
            r2 = outs[a].shape[0] // 2
            rows = outs[a].at[pl.ds(c * r2, r2), :]
            cps.append(pltpu.make_async_remote_copy(src_ref=rows, dst_ref=rows, send_sem=send_sems.at[a],
                                                    recv_sem=recv_sems.at[a], device_id=(x, y, 1 - c),
                                                    device_id_type=MESH))
        for cp in cps:
            cp.start()
        for cp in cps:
            cp.wait()

    return pl.pallas_call(
        body, name=name, in_specs=[ANY] * nb, out_specs=[ANY] * nb,
        out_shape=[_sds(f.shape, f.dtype) for f in fs],
        input_output_aliases={a: a for a in range(nb)},
        scratch_shapes=[_dma_sems(nb), _dma_sems(nb)],
    )(*fs)


def _row_tile(rows, cols):
    tile = rows
    while tile * cols * 4 > (1 << 20) and tile % 16 == 0:
        tile //= 2
    return tile


def _w_ada_grad(c_all, dmod_cols):
    def body(c_ref, d_ref, o_ref):
        o_ref[...] = _dot_tn(c_ref[...].astype(BF16), d_ref[...].astype(BF16))

    return pl.pallas_call(body, name="w_ada_grad", out_shape=_sds((c_all.shape[1], dmod_cols.shape[1])),
                          compiler_params=_params())(c_all, dmod_cols)


def _adam_math(w, g, m, v):
    nm = ADAM_B1 * m + (1.0 - ADAM_B1) * g
    nv = ADAM_B2 * v + (1.0 - ADAM_B2) * (g * g)
    m_hat = nm / (1.0 - ADAM_B1 ** ADAM_STEP)
    v_hat = nv / (1.0 - ADAM_B2 ** ADAM_STEP)
    return -ADAM_LR * (m_hat / (jnp.sqrt(v_hat) + ADAM_EPS) + ADAM_WD * w), nm, nv


def _adamw(w, g, m, v, name):
    rows, cols = w.shape
    tr = _row_tile(rows, cols)

    def body(w_ref, g_ref, m_ref, v_ref, go_ref, d_ref, nm_ref, nv_ref):
        gv = g_ref[...]
        go_ref[...] = gv
        d_ref[...], nm_ref[...], nv_ref[...] = _adam_math(w_ref[...], gv, m_ref[...], v_ref[...])

    spec = pl.BlockSpec((tr, cols), lambda i: (i, 0))
    return pl.pallas_call(
        body, name=name, grid=(rows // tr,), in_specs=[spec] * 4, out_specs=[spec] * 4,
        out_shape=[_sds((rows, cols))] * 4, compiler_params=_params("parallel"),
    )(w, g, m, v)


def _pack_small(dmod, dnorm_g, dfinal_g, dpool_scale, drel_bias, loss, dpool_w):
    return jnp.concatenate([dmod.reshape(-1, 128), dnorm_g.reshape(-1, 128), dfinal_g.reshape(-1, 128),
                            jnp.pad(dpool_scale.reshape(-1, 128), ((0, PK_RELB - PK_PSCALE - AW // 128), (0, 0))),
                            jnp.pad(drel_bias, ((0, 0), (0, 128 - NG * NH))),
                            jnp.full((PK_POOLW - PK_LOSS, 128), loss, F32), dpool_w.reshape(-1, 128)], axis=0)


def _small_update(small_all, ws, ms, vs):
    lane_rows = [(r0, r0 + w.shape[1] // 128) for r0, w in zip((PK_BADA, PK_NORMG, PK_FINALG, PK_PSCALE), ws)]
    nw = len(ws)

    def body(all_ref, *refs):
        w_refs, m_refs, v_refs = refs[:nw], refs[nw:2 * nw], refs[2 * nw:3 * nw]
        loss_ref, outs = refs[3 * nw], refs[3 * nw + 1:]
        g = all_ref[0]
        for s in range(1, all_ref.shape[0]):
            g = g + all_ref[s]
        loss_ref[...] = jnp.broadcast_to(g[PK_LOSS:PK_LOSS + 1, :], loss_ref.shape)

        def put(p, at, gv):
            d, nm, nv = _adam_math(w_refs[p][at], gv, m_refs[p][at], v_refs[p][at])
            for o_ref, val in zip(outs[4 * p:4 * p + 4], (gv, d, nm, nv)):
                o_ref[at] = val

        for p, (r0, r1) in enumerate(lane_rows):
            for i in range(r1 - r0):
                put(p, (slice(None), slice(128 * i, 128 * (i + 1))), g[r0 + i:r0 + i + 1, :])
        put(4, (slice(None), slice(None)), g[PK_RELB:PK_LOSS, 0:NG * NH])
        put(5, (slice(None), slice(None)), g[PK_POOLW:PK_ROWS, :])

    res = pl.pallas_call(
        body, name="small_update",
        out_shape=[_sds((8, 128))] + [_sds(w.shape) for w in ws for _ in range(4)], compiler_params=_params(),
    )(small_all, *ws, *ms, *vs)
    return res[0], [res[1 + 4 * p:5 + 4 * p] for p in range(nw)]


def kernel(x, c, norm_g, w_ada, b_ada, w_in, pool_w, pool_scale, w_attn_br, w_pool_br, w_out, rel_bias, final_g, loss_target, m_norm_g, m_w_ada, m_b_ada, m_w_in, m_pool_w, m_pool_scale, m_w_attn_br, m_w_pool_br, m_w_out, m_rel_bias, m_final_g, v_norm_g, v_w_ada, v_b_ada, v_w_in, v_pool_w, v_pool_scale, v_w_attn_br, v_w_pool_br, v_w_out, v_rel_bias, v_final_g):
    ix, iy, ic = lax.axis_index("x"), lax.axis_index("y"), lax.axis_index("c")
    dev = 4 * ix + 2 * iy + ic
    chip = 2 * ix + iy

    def half(w):
        r2 = w.shape[0] // 2
        return lax.dynamic_slice_in_dim(w, ic * r2, r2, axis=0).astype(BF16)

    gathered = _allgather8([jnp.broadcast_to(c, (8, D)), half(w_in[0]), half(w_attn_br[0]), half(w_pool_br[0]),
                            half(w_out[0])], "gather_weights", relay=[False, True, True, True, True])
    c_all = gathered[0][:, 0, :]
    wg_in = gathered[1].reshape(N_SHARD, D, IN_W // N_SHARD)
    wab = gathered[2].reshape(N_SHARD, AW, D // N_SHARD).transpose(1, 0, 2).reshape(AW, D)
    wpb = gathered[3].reshape(N_SHARD, AW, D // N_SHARD).transpose(1, 0, 2).reshape(AW, D)
    wout = gathered[4].reshape(D, D)

    mw = 3 * D // N_SHARD
    modp = _mod_partial(c_all, w_ada[0], lax.dynamic_slice_in_dim(b_ada, chip * mw, mw, axis=1))
    mod_all = _allgather8([modp], "gather_mod")[0]
    mod_full = mod_all[::2].transpose(1, 0, 2).reshape(8, 3 * D)
    mod = lax.dynamic_slice_in_dim(mod_full, dev, 1, axis=0)

    chip_half = jnp.stack([chip, ic]).astype(jnp.int32)
    r = _local_step(x[0], loss_target[0], mod, wg_in, wab, wpb, wout, pool_w[0], pool_scale, rel_bias, norm_g,
                    final_g.reshape(1, D), chip_half)

    packed = _pack_small(r["dmod"], r["dnorm_g"], r["dfinal_g"], r["dpool_scale"], r["drel_bias"], r["loss"],
                         r["dpool_w"])
    small_all = _allgather8([packed], "gather_small")[0]
    small = ["b_ada", "norm_g", "final_g", "pool_scale", "rel_bias", "pool_w"]
    shaped = lambda b, n, f, ps, rb, pw: [b, n, f.reshape(1, D), ps, rb, pw.reshape(4 * PGW, PGW)]
    loss, small_out = _small_update(small_all, shaped(b_ada, norm_g, final_g, pool_scale, rel_bias, pool_w),
                                    shaped(m_b_ada, m_norm_g, m_final_g, m_pool_scale, m_rel_bias, m_pool_w),
                                    shaped(v_b_ada, v_norm_g, v_final_g, v_pool_scale, v_rel_bias, v_pool_w))
    dmod_all = small_all[:, PK_BADA:PK_NORMG, :].reshape(8, 3 * D)
    g_w_ada = _w_ada_grad(c_all, lax.dynamic_slice_in_dim(dmod_all, chip * mw, mw, axis=1))

    g_w_in, g_w_ab, g_w_pb, g_w_out = _sibling_join([r["rs_in"], r["rs_attn_br"], r["rs_pool_br"], r["rs_out"]],
                                                    "rs_sibling_join")
    upd = dict(zip(small, small_out))
    upd["final_g"] = [a.reshape(D) for a in upd["final_g"]]
    upd["pool_w"] = [a.reshape(1, 4, PGW, PGW) for a in upd["pool_w"]]
    for nme, w, g, m, v in (("w_ada", w_ada, g_w_ada, m_w_ada, v_w_ada), ("w_in", w_in, g_w_in, m_w_in, v_w_in),
                            ("w_attn_br", w_attn_br, g_w_ab, m_w_attn_br, v_w_attn_br),
                            ("w_pool_br", w_pool_br, g_w_pb, m_w_pool_br, v_w_pool_br),
                            ("w_out", w_out, g_w_out, m_w_out, v_w_out)):
        upd[nme] = [a[None] for a in _adamw(w[0], g, m[0], v[0], "adamw_" + nme)]
    names = ["norm_g", "w_ada", "b_ada", "w_in", "pool_w", "pool_scale", "w_attn_br", "w_pool_br", "w_out",
             "rel_bias", "final_g"]
    return (loss[0, 0], r["grad_x"][None]) + tuple(upd[nme][kind] for kind in range(4) for nme in names)
```

```python
import math

import numpy as np
import jax
import jax.numpy as jnp
from jax import lax
from jax.experimental import pallas as pl
from jax.experimental.pallas import tpu as pltpu

F32 = jnp.float32
BF16 = jnp.bfloat16

D = 1024
HD = 64
NH = 8
AW = NH * HD
GROUPS = ((128, 1), (512, 4), (2048, 16))
NG = len(GROUPS)
BLK = 128
GW = 3 * AW
QKV_W = NG * GW
REST_W = 3584
IN_W = QKV_W + REST_W
CB = 512
NCB_QKV = QKV_W // CB
POOL_WINDOWS = (2, 4, 8, 16)
PGW = 128
HALO = 16
NUM_BUCKETS = 32
MAX_DISTANCE = 2048
EPS = 1e-6
NEG = -1e30
N_SHARD = 4
VMEM_LIMIT = 56 * 1024 * 1024

ADAM_LR = 0.001
ADAM_B1 = 0.9
ADAM_B2 = 0.999
ADAM_EPS = 1e-08
ADAM_WD = 0.01
ADAM_STEP = 10

PK_BADA, PK_NORMG, PK_FINALG, PK_PSCALE, PK_RELB, PK_LOSS, PK_POOLW, PK_ROWS = 0, 24, 32, 40, 48, 80, 88, 600

ANY = pl.BlockSpec(memory_space=pl.ANY)
MESH = pl.DeviceIdType.MESH


def _params(*sem):
    return pltpu.CompilerParams(dimension_semantics=sem, vmem_limit_bytes=VMEM_LIMIT)


def _sds(shape, dtype=F32):
    return jax.ShapeDtypeStruct(shape, dtype)


def _dot(a, b):
    return jnp.dot(a, b, preferred_element_type=F32)


def _dot_nt(a, b):
    return lax.dot_general(a, b, (((1,), (1,)), ((), ())), preferred_element_type=F32)


def _dot_tn(a, b):
    return lax.dot_general(a, b, (((0,), (0,)), ((), ())), preferred_element_type=F32)


def _sigmoid(z):
    return 0.5 * jnp.tanh(0.5 * z) + 0.5


def _dma_sems(*shape):
    return pltpu.SemaphoreType.DMA(shape)


class _Ride:
    def __init__(self, arrays, out_shapes, n_copies, copies):
        self.arrays, self.out_shapes, self.n_copies, self.copies = list(arrays), list(out_shapes), n_copies, copies


def _call_with_ride(body, ride, first, last, *, in_specs, out_specs, out_shape, scratch_shapes=(), **kw):
    in_specs, out_specs, out_shape, scratch_shapes = list(in_specs), list(out_specs), list(out_shape), list(scratch_shapes)
    n_in, n_out, n_sc = len(in_specs), len(out_specs), len(scratch_shapes)
    if ride is None:
        def run_plain(*operands):
            return pl.pallas_call(body, in_specs=in_specs, out_specs=out_specs, out_shape=out_shape,
                                  scratch_shapes=scratch_shapes, **kw)(*operands), []
        return run_plain
    n_ri, n_ro = len(ride.arrays), len(ride.out_shapes)

    def wrapped(*refs):
        ins, rest = refs[:n_in], refs[n_in:]
        r_ins, rest = rest[:n_ri], rest[n_ri:]
        outs, rest = rest[:n_out], rest[n_out:]
        r_outs, rest = rest[:n_ro], rest[n_ro:]
        scratch, (send_sems, recv_sems) = rest[:n_sc], rest[n_sc:]

        @pl.when(first())
        def _():
            for cp in ride.copies(r_ins, r_outs, send_sems, recv_sems):
                cp.start()

        body(*ins, *outs, *scratch)

        @pl.when(last())
        def _():
            for cp in ride.copies(r_ins, r_outs, send_sems, recv_sems):
                cp.wait()

    def run(*operands):
        res = pl.pallas_call(
            wrapped, in_specs=in_specs + [ANY] * n_ri, out_specs=out_specs + [ANY] * n_ro,
            out_shape=out_shape + ride.out_shapes,
            scratch_shapes=scratch_shapes + [_dma_sems(ride.n_copies), _dma_sems(ride.n_copies)], **kw,
        )(*operands, *ride.arrays)
        return res[:n_out], res[n_out:]
    return run


def _bucket_tables():
    i = np.arange(BLK)[:, None]
    j = np.arange(2 * BLK)[None, :]
    dist = BLK + i - j
    valid = (dist >= 0) & (dist <= BLK)
    tabs = []
    for _, dil in GROUPS:
        n = (np.clip(dist, 0, BLK) * dil).astype(np.int32)
        max_exact = NUM_BUCKETS // 2
        nf = np.maximum(n, 1).astype(np.float32)
        large = max_exact + (np.log(nf / np.float32(max_exact)) / np.float32(math.log(MAX_DISTANCE / max_exact))
                             * np.float32(NUM_BUCKETS - max_exact)).astype(np.int32)
        large = np.minimum(large, NUM_BUCKETS - 1)
        bucket = np.where(n < max_exact, n, large)
        tab = np.where(valid, bucket, -1).astype(np.int32)
        perm = _block_perm(dil)
        tabs.append(tab[perm][:, np.concatenate([perm, BLK + perm])])
    return np.stack(tabs)


def _bias_table(rel_bias, buckets):
    def body(rb_ref, bk_ref, out_ref):
        g = pl.program_id(0)
        bk = bk_ref[...]
        for h in range(NH):
            acc = jnp.full((BLK, 2 * BLK), NEG, F32)
            for b in range(NUM_BUCKETS):
                acc = jnp.where(bk == b, rb_ref[b, g * NH + h], acc)
            out_ref[h] = acc

    return pl.pallas_call(
        body, name="bias_table", grid=(NG,),
        in_specs=[pl.BlockSpec(memory_space=pltpu.SMEM),
                  pl.BlockSpec((None, BLK, 2 * BLK), lambda g: (g, 0, 0))],
        out_specs=pl.BlockSpec((NH, BLK, 2 * BLK), lambda g: (g, 0, 0)),
        out_shape=_sds((NG * NH, BLK, 2 * BLK)),
        compiler_params=_params("arbitrary"),
    )(rel_bias, buckets)


def _bias_grad(ds_acc, buckets, ride):
    def body(acc_ref, bk_ref, out_ref):
        bk = bk_ref[...]
        acc = acc_ref[...]
        lane = lax.broadcasted_iota(jnp.int32, (8, 128), 1)
        out = jnp.zeros((8, 128), F32)
        for b in range(NUM_BUCKETS):
            val = jnp.sum(jnp.where(bk == b, acc, 0.0))
            out = jnp.where(lane == b, val, out)
        out_ref[...] = out

    (out,), rode = _call_with_ride(
        body, ride, lambda: pl.program_id(0) == 0, lambda: pl.program_id(0) == NG * NH - 1,
        name="bias_grad", grid=(NG * NH,),
        in_specs=[pl.BlockSpec((None, BLK, 2 * BLK), lambda gh: (gh, 0, 0)),
                  pl.BlockSpec((None, BLK, 2 * BLK), lambda gh: (gh // NH, 0, 0))],
        out_specs=[pl.BlockSpec((None, 8, 128), lambda gh: (gh, 0, 0))],
        out_shape=[_sds((NG * NH, 8, 128))],
        compiler_params=_params("arbitrary"),
    )(ds_acc, buckets)
    return out, rode


def _mod_partial(c_all, w_ada_s, b_ada_s):
    def body(c_ref, w_ref, b_ref, o_ref):
        o_ref[...] = _dot(c_ref[...].astype(BF16), w_ref[...].astype(BF16)) + b_ref[...]

    return pl.pallas_call(body, name="mod_partial", out_shape=_sds((8, w_ada_s.shape[1])),
                          compiler_params=_params())(c_all, w_ada_s, b_ada_s)


def _prenorm(x, norm_g, mod):
    S = x.shape[0]
    tm = 512

    def body(x_ref, g_ref, mod_ref, h_ref):
        xv = x_ref[...]
        r = lax.rsqrt(jnp.mean(xv * xv, axis=-1, keepdims=True) + EPS)
        n1 = xv * r * g_ref[...]
        h_ref[...] = (n1 * (1.0 + mod_ref[:, D:2 * D]) + mod_ref[:, 0:D]).astype(BF16)

    return pl.pallas_call(
        body, name="prenorm", grid=(S // tm,),
        in_specs=[pl.BlockSpec((tm, D), lambda i: (i, 0)), pl.BlockSpec((1, D), lambda i: (0, 0)),
                  pl.BlockSpec((1, 3 * D), lambda i: (0, 0))],
        out_specs=pl.BlockSpec((tm, D), lambda i: (i, 0)),
        out_shape=_sds((S, D), BF16), compiler_params=_params("parallel"),
    )(x, norm_g, mod)


def _proj(h, wg_in, j0, nj, dtype, name):
    S = h.shape[0]
    tm = 2048
    per = wg_in.shape[2] // CB

    def body(h_ref, w_ref, o_ref):
        o_ref[...] = _dot(h_ref[...], w_ref[...]).astype(dtype)

    return pl.pallas_call(
        body, name=name, grid=(S // tm, nj),
        in_specs=[pl.BlockSpec((tm, D), lambda m, j: (m, 0)),
                  pl.BlockSpec((None, D, CB), lambda m, j: ((j0 + j) // per, 0, (j0 + j) % per))],
        out_specs=pl.BlockSpec((tm, CB), lambda m, j: (m, j)),
        out_shape=_sds((S, nj * CB), dtype), compiler_params=_params("parallel", "parallel"),
    )(h, wg_in)


HS = 4
SLAB = HS * HD


def _lane_head(rows):
    return lax.broadcasted_iota(jnp.int32, (rows, SLAB), 1) // HD


def _head_stack(a):
    head = _lane_head(a.shape[0])
    return jnp.concatenate([jnp.where(head == h, a, jnp.zeros_like(a)) for h in range(HS)], axis=0)


def _head_unstack(a):
    rows = a.shape[0] // HS
    head = _lane_head(rows)
    out = a[:rows]
    for h in range(1, HS):
        out = jnp.where(head == h, a[h * rows:(h + 1) * rows], out)
    return out


STAT_W = 128
VIEW = 16


def _sub_layout(dil):
    if dil == 1:
        return BLK, [None]
    return BLK * dil // VIEW, [[r + dil * u for u in range(VIEW // dil)] for r in range(dil)]


def _block_perm(dil):
    a_rows, _ = _sub_layout(dil)
    p = np.arange(BLK)
    return p if dil == 1 else (VIEW // dil) * (p % a_rows) + p // a_rows


LB = 128
N_SLAB = NH // HS


RBS = 4


def _ld(refs, bs, s, w, rb=0):
    if bs is None:
        return refs[0][rb * BLK:(rb + 1) * BLK, s * w:(s + 1) * w]
    a_rows = refs[0].shape[0] // VIEW
    return jnp.concatenate([jnp.concatenate([ref[pl.ds(b, a_rows, stride=VIEW), :] for b in bs], axis=0)
                            for ref in refs[s * (w // LB):(s + 1) * (w // LB)]], axis=1)


def _st(ref, bs, s, val, rb=0):
    if bs is None:
        ref[rb * BLK:(rb + 1) * BLK, s * SLAB:(s + 1) * SLAB] = val.astype(ref.dtype)
        return
    a_rows = val.shape[0] // len(bs)
    for u, b in enumerate(bs):
        ref[:, b, s * SLAB:(s + 1) * SLAB] = val[u * a_rows:(u + 1) * a_rows]


def _attn_views(dil, S):
    a_rows, subs = _sub_layout(dil)
    if dil == 1:
        def ispecs(base, w, f):
            return [pl.BlockSpec((RBS * BLK, N_SLAB * w), lambda sg, n: (f(n), base // (N_SLAB * w)))]
        return subs, S // (RBS * BLK), N_SLAB, RBS, ispecs, (lambda w: (S, w)), (
            lambda f: pl.BlockSpec((RBS * BLK, AW), lambda sg, n: (f(n), 0)))

    sps = N_SLAB if dil < VIEW else 1

    def ispecs(base, w, f):
        return [pl.BlockSpec((a_rows * VIEW, LB), lambda sg, n, k=k: (f(n), (base + sg * sps * w) // LB + k))
                for k in range(sps * w // LB)]
    return subs, S // (a_rows * VIEW), sps, 1, ispecs, (lambda w: (S // VIEW, VIEW, w)), (
        lambda f: pl.BlockSpec((a_rows, VIEW, sps * SLAB), lambda sg, n: (f(n), 0, sg)))


def _attn_fwd(qkv_g, bias_tab, g):
    S = qkv_g.shape[0]
    subs, nbq, sps, rbs, ispecs, shape, ospec = _attn_views(GROUPS[g][1], S)
    cur = lambda n: n
    in_specs = [ispecs(0, SLAB, cur), ispecs(AW, SLAB, cur), ispecs(2 * AW, SLAB, cur)]
    nl = len(in_specs[0])

    def body(*refs):
        q, k, v = (refs[t * nl:(t + 1) * nl] for t in range(3))
        b_ref, o_ref, l_ref, kprev, vprev = refs[3 * nl:]
        n = pl.program_id(1)

        @pl.when(n == 0)
        def _():
            kprev[...] = jnp.zeros_like(kprev)
            vprev[...] = jnp.zeros_like(vprev)

        col = lax.broadcasted_iota(jnp.int32, (HS * BLK, 2 * BLK), 1)
        first = (col >= BLK) | (n > 0)
        for s_, rb, (i, bs) in ((s_, rb, sub) for s_ in range(sps) for rb in range(rbs) for sub in enumerate(subs)):
            cs = slice(s_ * SLAB, (s_ + 1) * SLAB)
            kc, vc = _ld(k, bs, s_, SLAB, rb).astype(BF16), _ld(v, bs, s_, SLAB, rb).astype(BF16)
            kb = jnp.concatenate([kprev[i, :, cs], kc], axis=0)
            vb = jnp.concatenate([vprev[i, :, cs], vc], axis=0)
            kprev[i, :, cs], vprev[i, :, cs] = kc, vc
            s = _dot_nt(_head_stack(_ld(q, bs, s_, SLAB, rb).astype(BF16)), kb) * (HD ** -0.5)
            s = s + b_ref[pl.ds(s_ * HS, HS)].reshape(HS * BLK, 2 * BLK)
            if rb == 0:
                s = jnp.where(first, s, NEG)
            m = jnp.max(s, axis=-1, keepdims=True)
            p = jnp.exp(s - m)
            den = jnp.sum(p, axis=-1, keepdims=True)
            _st(o_ref, bs, s_, _head_unstack(_dot(p.astype(BF16), vb) / den), rb)
            _st(l_ref, bs, s_, _head_unstack(jnp.broadcast_to(m + jnp.log(den), (HS * BLK, SLAB))), rb)

    out = _sds(shape(AW))
    nsg = N_SLAB // sps
    o, l = pl.pallas_call(
        body, name=f"attn_fwd{g}", grid=(nsg, nbq),
        in_specs=sum(in_specs, []) + [pl.BlockSpec((sps * HS, BLK, 2 * BLK), lambda sg, n: (g * nsg + sg, 0, 0))],
        out_specs=[ospec(cur), ospec(cur)],
        out_shape=[out, out],
        scratch_shapes=[pltpu.VMEM((len(subs), BLK, sps * SLAB), BF16)] * 2,
        compiler_params=_params("parallel", "arbitrary"),
    )(*([qkv_g] * (3 * nl)), bias_tab)
    return o.reshape(S, AW), l.reshape(S, AW)


def _attn_bwd(qkv_g, dattn, stats, bias_tab, g, ride):
    S = qkv_g.shape[0]
    subs, nbq, sps, rbs, ispecs, shape, ospec = _attn_views(GROUPS[g][1], S)
    cur = lambda n: jnp.minimum(n, nbq - 1)
    late = lambda n: jnp.maximum(n - 1, 0)
    in_specs = [ispecs(0, SLAB, cur), ispecs(AW, SLAB, cur), ispecs(2 * AW, SLAB, cur), ispecs(0, SLAB, cur),
                ispecs(0, STAT_W, cur)]
    nl = len(in_specs[0])

    def body(*refs):
        q, k, v, da = (refs[t * nl:(t + 1) * nl] for t in range(4))
        nst = len(in_specs[4])
        st_refs = refs[4 * nl:4 * nl + nst]
        b_ref, dq_ref, dk_ref, dv_ref, ds_ref, ck_ref, cv_ref, kprev, vprev, *held = refs[4 * nl + nst:]
        n = pl.program_id(1)

        @pl.when(n == 0)
        def _():
            for ref in (ds_ref, ck_ref, cv_ref, kprev, vprev, *held):
                ref[...] = jnp.zeros_like(ref)

        def finish(ref, t, bs, s_, rb, val):
            cs = slice(s_ * SLAB, (s_ + 1) * SLAB)
            if rbs == 1:
                _st(ref, bs, s_, val)
            elif rb == 0:
                for j in range(rbs - 1):
                    _st(ref, bs, s_, held[t][j * BLK:(j + 1) * BLK, cs], j)
                _st(ref, bs, s_, val, rbs - 1)
            else:
                held[t][(rb - 1) * BLK:rb * BLK, cs] = val

        @pl.when(n < nbq)
        def _():
            col = lax.broadcasted_iota(jnp.int32, (HS * BLK, 2 * BLK), 1)
            first = (col >= BLK) | (n > 0)
            for s_, rb, (i, bs) in ((s_, rb, sub) for s_ in range(sps) for rb in range(rbs) for sub in enumerate(subs)):
                cs = slice(s_ * SLAB, (s_ + 1) * SLAB)
                st = _ld(st_refs, bs, s_, STAT_W, rb)
                kc, vc = _ld(k, bs, s_, SLAB, rb).astype(BF16), _ld(v, bs, s_, SLAB, rb).astype(BF16)
                kb = jnp.concatenate([kprev[i, :, cs], kc], axis=0)
                vb = jnp.concatenate([vprev[i, :, cs], vc], axis=0)
                kprev[i, :, cs], vprev[i, :, cs] = kc, vc
                lse = jnp.concatenate([st[:, h:h + 1] for h in range(HS)], axis=0)
                delta = jnp.concatenate([st[:, HS + h:HS + h + 1] for h in range(HS)], axis=0)
                qs = _head_stack(_ld(q, bs, s_, SLAB, rb).astype(BF16))
                dos = _head_stack(_ld(da, bs, s_, SLAB, rb).astype(BF16))
                s = _dot_nt(qs, kb) * (HD ** -0.5) + b_ref[pl.ds(s_ * HS, HS)].reshape(HS * BLK, 2 * BLK)
                if rb == 0:
                    s = jnp.where(first, s, NEG)
                p = jnp.exp(s - lse)
                ds = p * (_dot_nt(dos, vb) - delta)
                ds_ref[pl.ds(s_ * HS, HS)] += ds.reshape(HS, BLK, 2 * BLK)
                ds_b = (ds * (HD ** -0.5)).astype(BF16)
                _st(dq_ref, bs, s_, _head_unstack(_dot(ds_b, kb)), rb)
                dkb = _dot_tn(ds_b, qs)
                dvb = _dot_tn(p.astype(BF16), dos)
                finish(dk_ref, 0, bs, s_, rb, ck_ref[i, :, cs] + dkb[:BLK])
                finish(dv_ref, 1, bs, s_, rb, cv_ref[i, :, cs] + dvb[:BLK])
                ck_ref[i, :, cs] = dkb[BLK:]
                cv_ref[i, :, cs] = dvb[BLK:]

        @pl.when(n == nbq)
        def _():
            for s_ in range(sps):
                for i, bs in enumerate(subs):
                    finish(dk_ref, 0, bs, s_, 0, ck_ref[i, :, s_ * SLAB:(s_ + 1) * SLAB])
                    finish(dv_ref, 1, bs, s_, 0, cv_ref[i, :, s_ * SLAB:(s_ + 1) * SLAB])

    out = _sds(shape(AW), BF16 if GROUPS[g][1] == 1 else F32)
    nsg = N_SLAB // sps
    (dq, dk, dv, ds_acc), rode = _call_with_ride(
        body, ride, lambda: (pl.program_id(0) == 0) & (pl.program_id(1) == 0),
        lambda: (pl.program_id(0) == nsg - 1) & (pl.program_id(1) == nbq),
        name=f"attn_bwd{g}", grid=(nsg, nbq + 1),
        in_specs=sum(in_specs, []) + [pl.BlockSpec((sps * HS, BLK, 2 * BLK), lambda sg, n: (g * nsg + sg, 0, 0))],
        out_specs=[ospec(cur), ospec(late), ospec(late),
                   pl.BlockSpec((sps * HS, BLK, 2 * BLK), lambda sg, n: (sg, 0, 0))],
        out_shape=[out] * 3 + [_sds((NH, BLK, 2 * BLK))],
        scratch_shapes=[pltpu.VMEM((len(subs), BLK, sps * SLAB), F32)] * 2
        + [pltpu.VMEM((len(subs), BLK, sps * SLAB), BF16)] * 2 + [pltpu.VMEM(((rbs - 1) * BLK, sps * SLAB), F32)] * (2 if rbs > 1 else 0),
        compiler_params=_params("arbitrary", "arbitrary"),
    )(*([qkv_g] * (3 * nl)), *([dattn] * nl), *([stats] * len(in_specs[4])), bias_tab)
    return [dq.reshape(S, AW), dk.reshape(S, AW), dv.reshape(S, AW)], ds_acc, rode


TM_MIX = 256


def _mix_specs(tm):
    row512 = pl.BlockSpec((tm, AW), lambda i: (i, 0))
    return ([row512] * 6 + [
        pl.BlockSpec((tm, REST_W), lambda i: (i, 0)),
        pl.BlockSpec((HALO, AW), lambda i: (jnp.maximum(i * (tm // HALO) - 1, 0), 1)),
        pl.BlockSpec((AW, D), lambda i: (0, 0)), pl.BlockSpec((AW, D), lambda i: (0, 0)),
        pl.BlockSpec((4, PGW, PGW), lambda i: (0, 0, 0)), pl.BlockSpec((1, AW), lambda i: (0, 0))])


def _mix_forward(i, tm, o_refs, l_refs, rest_ref, halo_ref, wab_ref, wpb_ref, pw_ref, ps_ref):
    l0, l1, l2 = (r[...] for r in l_refs)
    mx = jnp.maximum(jnp.maximum(l0, l1), l2)
    e0, e1, e2 = jnp.exp(l0 - mx), jnp.exp(l1 - mx), jnp.exp(l2 - mx)
    den = e0 + e1 + e2
    lj = mx + jnp.log(den)
    attn = (e0 * o_refs[0][...] + e1 * o_refs[1][...] + e2 * o_refs[2][...]) / den

    z_attn = rest_ref[:, 0:AW]
    u = rest_ref[:, AW:2 * AW]
    z_pool = rest_ref[:, 2 * AW:3 * AW]
    g_attn = rest_ref[:, 3 * AW:3 * AW + D]
    g_pool = rest_ref[:, 3 * AW + D:3 * AW + 2 * D]

    sg_a = _sigmoid(z_attn)
    sil_a = z_attn * sg_a
    a_g = (attn * sil_a).astype(BF16)
    y_attn = _dot(a_g, wab_ref[...])

    halo = jnp.where(i > 0, halo_ref[...], 0.0)
    ext = jnp.concatenate([halo, u], axis=0)
    t = i * tm + lax.broadcasted_iota(jnp.int32, (tm, 1), 0)
    pooled, mixed_raw = [], []
    for gi, win in enumerate(POOL_WINDOWS):
        s = ext[:, gi * PGW:(gi + 1) * PGW]
        sh = 1
        while sh < win:
            s = s + pltpu.roll(s, sh, 0)
            sh *= 2
        cnt = jnp.minimum(t + 1, win).astype(F32)
        pg = s[HALO:] / cnt - u[:, gi * PGW:(gi + 1) * PGW]
        pooled.append(pg.astype(BF16))
        mixed_raw.append(_dot(pooled[-1], pw_ref[gi].astype(BF16)))
    mixed_raw = jnp.concatenate(mixed_raw, axis=1)
    mixed = mixed_raw * ps_ref[...]
    sg_p = _sigmoid(z_pool)
    sil_p = z_pool * sg_p
    m_g = (mixed * sil_p).astype(BF16)
    y_pool = _dot(m_g, wpb_ref[...])

    sa = _sigmoid(g_attn)
    sp = _sigmoid(g_pool)
    merged = sa * y_attn + sp * y_pool
    return dict(lj=lj, attn=attn, z_attn=z_attn, z_pool=z_pool, sg_a=sg_a, sil_a=sil_a, a_g=a_g, y_attn=y_attn,
                pooled=pooled, mixed_raw=mixed_raw, mixed=mixed, sg_p=sg_p, sil_p=sil_p, m_g=m_g, y_pool=y_pool,
                sa=sa, sp=sp, merged=merged)


def _mix_step(x, target, os_, ls_, rest, wab, wpb, pool_w, pool_scale, wout, mod, final_g):
    S = x.shape[0]
    tm = TM_MIX
    nt = S // tm
    sw = D // N_SHARD

    def body(o0, o1, o2, l0, l1, l2, rest_ref, halo_ref, wab_ref, wpb_ref, pw_ref, ps_ref,
             x_ref, t_ref, wo_ref, mod_ref, fg_ref, dx2_ref, loss_ref, dfg_ref, dgate_ref,
             dattn_ref, stats_ref, dpooled_ref, dproj_hbm, dwo_hbm, dwab_hbm, dwpb_hbm, dpw_ref, dps_ref,
             awo, awab, awpb, stage, stage_sem):
        i = pl.program_id(0)
        slot = i % 2

        def staged(step, sl):
            return pltpu.make_async_copy(stage.at[sl], dproj_hbm.at[pl.ds(step * tm, tm), pl.ds(QKV_W, REST_W)],
                                         stage_sem.at[sl])

        @pl.when(i == 0)
        def _():
            for ref in (loss_ref, dfg_ref, dgate_ref, awo, awab, awpb, dpw_ref, dps_ref):
                ref[...] = jnp.zeros_like(ref)

        f = _mix_forward(i, tm, (o0, o1, o2), (l0, l1, l2), rest_ref, halo_ref, wab_ref, wpb_ref, pw_ref, ps_ref)
        mo = _dot(f["merged"].astype(BF16), wo_ref[...])
        gate = mod_ref[:, 2 * D:3 * D]
        fg = fg_ref[...]
        x2 = x_ref[...] + gate * mo
        r2 = lax.rsqrt(jnp.mean(x2 * x2, axis=-1, keepdims=True) + EPS)
        n2 = x2 * r2
        err = n2 * fg - t_ref[...]
        loss_ref[...] += 0.5 * jnp.sum(jnp.mean(err * err, axis=-1, keepdims=True))
        dy = err * (1.0 / D)
        dfg_ref[...] += jnp.sum(dy * n2, axis=0, keepdims=True)
        dn = dy * fg
        dx2 = r2 * (dn - n2 * jnp.mean(dn * n2, axis=-1, keepdims=True))
        dgate_ref[...] += jnp.sum(dx2 * mo, axis=0, keepdims=True)
        dx2_ref[...] = dx2

        dmo_b = (dx2 * gate).astype(BF16)
        dmerged = _dot_nt(dmo_b, wo_ref[...])
        awo[...] += _dot_tn(f["merged"].astype(BF16), dmo_b)
        sa, sp = f["sa"], f["sp"]
        dya = (dmerged * sa).astype(BF16)
        dyp = (dmerged * sp).astype(BF16)
        dg_attn = dmerged * f["y_attn"] * sa * (1.0 - sa)
        dg_pool = dmerged * f["y_pool"] * sp * (1.0 - sp)
        dag = _dot_nt(dya, wab_ref[...])
        awab[...] += _dot_tn(f["a_g"], dya)
        dmg = _dot_nt(dyp, wpb_ref[...])
        awpb[...] += _dot_tn(f["m_g"], dyp)
        dattn = dag * f["sil_a"]
        dattn_ref[...] = dattn
        prod = dattn * f["attn"]
        lane = lax.broadcasted_iota(jnp.int32, (tm, STAT_W), 1)
        for sb in range(N_SLAB):
            st = jnp.zeros((tm, STAT_W), F32)
            for h in range(HS):
                hs = slice((sb * HS + h) * HD, (sb * HS + h + 1) * HD)
                st = jnp.where(lane == h, f["lj"][:, hs.start:hs.start + 1], st)
                st = jnp.where(lane == HS + h, jnp.sum(prod[:, hs], axis=-1, keepdims=True), st)
            stats_ref[:, sb * STAT_W:(sb + 1) * STAT_W] = st
        dz_attn = dag * f["attn"] * (f["sg_a"] * (1.0 + f["z_attn"] * (1.0 - f["sg_a"])))
        dmixed = dmg * f["sil_p"]
        dz_pool = dmg * f["mixed"] * (f["sg_p"] * (1.0 + f["z_pool"] * (1.0 - f["sg_p"])))
        dps_ref[...] += jnp.sum(dmixed * f["mixed_raw"], axis=0, keepdims=True)
        dpm = (dmixed * ps_ref[...]).astype(BF16)
        for gi in range(len(POOL_WINDOWS)):
            cs = slice(gi * PGW, (gi + 1) * PGW)
            dpw_ref[gi] += _dot_tn(f["pooled"][gi], dpm[:, cs])
            dpooled_ref[:, cs] = _dot_nt(dpm[:, cs], pw_ref[gi].astype(BF16))
        @pl.when(i >= 2)
        def _():
            staged(i - 2, slot).wait()

        stage[slot, :, 0:AW] = dz_attn.astype(BF16)
        stage[slot, :, AW:2 * AW] = jnp.zeros((tm, AW), BF16)
        stage[slot, :, 2 * AW:3 * AW] = dz_pool.astype(BF16)
        stage[slot, :, 3 * AW:3 * AW + D] = dg_attn.astype(BF16)
        stage[slot, :, 3 * AW + D:3 * AW + 2 * D] = dg_pool.astype(BF16)
        staged(i, slot).start()

        @pl.when(i == nt - 1)
        def _():
            staged(i - 1, 1 - slot).wait()
            staged(i, slot).wait()
            pltpu.sync_copy(awo, dwo_hbm)
            for k in range(N_SHARD):
                pltpu.sync_copy(awab.at[:, pl.ds(k * sw, sw)], dwab_hbm.at[k])
                pltpu.sync_copy(awpb.at[:, pl.ds(k * sw, sw)], dwpb_hbm.at[k])

    row = pl.BlockSpec((tm, D), lambda i: (i, 0))
    vec = pl.BlockSpec((1, D), lambda i: (0, 0))
    row512 = pl.BlockSpec((tm, AW), lambda i: (i, 0))
    outs = pl.pallas_call(
        body, name="mix_step", grid=(nt,),
        in_specs=_mix_specs(tm) + [row, row, pl.BlockSpec((D, D), lambda i: (0, 0)),
                                   pl.BlockSpec((1, 3 * D), lambda i: (0, 0)), vec],
        out_specs=[row, pl.BlockSpec((8, 128), lambda i: (0, 0)), vec, vec,
                   row512, pl.BlockSpec((tm, N_SLAB * STAT_W), lambda i: (i, 0)), row512, ANY, ANY, ANY, ANY,
                   pl.BlockSpec((4, PGW, PGW), lambda i: (0, 0, 0)), pl.BlockSpec((1, AW), lambda i: (0, 0))],
        out_shape=[_sds((S, D)), _sds((8, 128)), _sds((1, D)), _sds((1, D)),
                   _sds((S, AW)), _sds((S, N_SLAB * STAT_W)), _sds((S, AW)), _sds((S, IN_W), BF16),
                   _sds((D, D)), _sds((N_SHARD, AW, sw)), _sds((N_SHARD, AW, sw)), _sds((4, PGW, PGW)), _sds((1, AW))],
        scratch_shapes=[pltpu.VMEM((D, D), F32), pltpu.VMEM((AW, D), F32), pltpu.VMEM((AW, D), F32),
                        pltpu.VMEM((2, tm, REST_W), BF16), _dma_sems(2)],
        compiler_params=_params("arbitrary"),
    )(*os_, *ls_, rest, rest, wab, wpb, pool_w, pool_scale, x, target, wout, mod, final_g)
    dx2, loss, dfg, dgate, dattn, stats, dpooled, dproj, dwo, dwab, dwpb, dpw, dps = outs
    return (dx2, loss, dfg, dgate, dattn, stats, dpooled, dproj, dwo.reshape(N_SHARD, D // N_SHARD, D), dwab, dwpb,
            dpw, dps)


def _pool_bwd(dpooled):
    S = dpooled.shape[0]
    tm = 512
    nt = S // tm

    def body(dp_ref, nxt_ref, du_ref):
        i = pl.program_id(0)
        t = i * tm + lax.broadcasted_iota(jnp.int32, (tm + HALO, 1), 0)
        nxt = jnp.where(i < nt - 1, nxt_ref[...], 0.0)
        ext = jnp.concatenate([dp_ref[...], nxt], axis=0)
        for gi, win in enumerate(POOL_WINDOWS):
            cs = slice(gi * PGW, (gi + 1) * PGW)
            s = ext[:, cs] / jnp.minimum(t + 1, win).astype(F32)
            sh = 1
            while sh < win:
                s = s + pltpu.roll(s, tm + HALO - sh, 0)
                sh *= 2
            du_ref[:, cs] = (s[:tm] - dp_ref[:, cs]).astype(BF16)

    return pl.pallas_call(
        body, name="pool_bwd", grid=(nt,),
        in_specs=[pl.BlockSpec((tm, AW), lambda i: (i, 0)),
                  pl.BlockSpec((HALO, AW), lambda i: (jnp.minimum((i + 1) * (tm // HALO), S // HALO - 1), 0))],
        out_specs=pl.BlockSpec((tm, AW), lambda i: (i, 0)),
        out_shape=_sds((S, AW), BF16), compiler_params=_params("parallel"),
    )(dpooled, dpooled)


TB = 1024


def _dh(dproj, wg_in, ride):
    S = dproj.shape[0]
    per = wg_in.shape[2] // TB
    nm, nk = S // TB, IN_W // TB

    def body(dp_ref, w_ref, out_ref):
        @pl.when(pl.program_id(1) == 0)
        def _():
            out_ref[...] = jnp.zeros_like(out_ref)

        out_ref[...] += _dot_nt(dp_ref[...], w_ref[...])

    (dh,), rode = _call_with_ride(
        body, ride, lambda: (pl.program_id(0) == 0) & (pl.program_id(1) == 0),
        lambda: (pl.program_id(0) == nm - 1) & (pl.program_id(1) == nk - 1),
        name="dh", grid=(nm, nk),
        in_specs=[pl.BlockSpec((TB, TB), lambda m, kk: (m, kk)),
                  pl.BlockSpec((None, D, TB), lambda m, kk: (kk // per, 0, kk % per))],
        out_specs=[pl.BlockSpec((TB, D), lambda m, kk: (m, 0))],
        out_shape=[_sds((S, D))], compiler_params=_params("arbitrary", "arbitrary"),
    )(dproj, wg_in)
    return dh, rode


def _dw_in(h, dproj):
    S = dproj.shape[0]
    per = IN_W // N_SHARD // TB

    def body(h_ref, dp_ref, out_ref):
        out_ref[...] = _dot_tn(h_ref[...], dp_ref[...])

    return pl.pallas_call(
        body, name="dw_in", grid=(IN_W // TB,),
        in_specs=[pl.BlockSpec((S, D), lambda j: (0, 0)), pl.BlockSpec((S, TB), lambda j: (0, j))],
        out_specs=pl.BlockSpec((None, D, TB), lambda j: (j // per, 0, j % per)),
        out_shape=_sds((N_SHARD, D, IN_W // N_SHARD)), compiler_params=_params("parallel"),
    )(h, dproj)


def _prenorm_bwd(x, dh, dx2, norm_g, mod):
    S = x.shape[0]
    tm = 512

    def body(x_ref, dh_ref, dx2_ref, g_ref, mod_ref, gx_ref, dg_ref, dshift_ref, dscale_ref):
        i = pl.program_id(0)

        @pl.when(i == 0)
        def _():
            dg_ref[...] = jnp.zeros_like(dg_ref)
            dshift_ref[...] = jnp.zeros_like(dshift_ref)
            dscale_ref[...] = jnp.zeros_like(dscale_ref)

        xv = x_ref[...]
        dhv = dh_ref[...]
        g = g_ref[...]
        r = lax.rsqrt(jnp.mean(xv * xv, axis=-1, keepdims=True) + EPS)
        xh = xv * r
        dshift_ref[...] += jnp.sum(dhv, axis=0, keepdims=True)
        dscale_ref[...] += jnp.sum(dhv * (xh * g), axis=0, keepdims=True)
        dn1 = dhv * (1.0 + mod_ref[:, D:2 * D])
        dg_ref[...] += jnp.sum(dn1 * xh, axis=0, keepdims=True)
        dxh = dn1 * g
        gx_ref[...] = dx2_ref[...] + r * (dxh - xh * jnp.mean(dxh * xh, axis=-1, keepdims=True))

    row = pl.BlockSpec((tm, D), lambda i: (i, 0))
    vec = pl.BlockSpec((1, D), lambda i: (0, 0))
    return pl.pallas_call(
        body, name="prenorm_bwd", grid=(S // tm,),
        in_specs=[row, row, row, vec, pl.BlockSpec((1, 3 * D), lambda i: (0, 0))],
        out_specs=[row, vec, vec, vec],
        out_shape=[_sds((S, D)), _sds((1, D)), _sds((1, D)), _sds((1, D))],
        compiler_params=_params("arbitrary"),
    )(x, dh, dx2, norm_g, mod)


def _local_step(x, target, mod, wg_in, wab, wpb, wout, pool_w, pool_scale, rel_bias, norm_g, final_g, chip_half):
    buckets = jnp.asarray(_bucket_tables())
    bias_tab = _bias_table(rel_bias, buckets)
    h = _prenorm(x, norm_g, mod)
    qkv = [_proj(h, wg_in, 3 * g, 3, BF16 if GROUPS[g][1] == 1 else F32, f"proj_qkv{g}") for g in range(NG)]
    rest = _proj(h, wg_in, NCB_QKV, REST_W // CB, F32, "proj_rest")
    os_, ls_ = zip(*[_attn_fwd(qkv[g], bias_tab, g) for g in range(NG)])
    (dx2, loss, dfinal_g, dgate, dattn, stats, dpooled, dproj, dw_out, dw_ab, dw_pb, dpool_w,
     dpool_scale) = _mix_step(x, target, os_, ls_, rest, wab, wpb, pool_w, pool_scale, wout, mod, final_g)
    du = _pool_bwd(dpooled)

    small = [dw_ab, dw_pb, dw_out]
    dqkv0, ds0, sib_small = _attn_bwd(qkv[0], dattn, stats, bias_tab, 0, _ride_sibling_halves(small))
    p_small = _pair_sum_small(small, sib_small, chip_half)
    dqkv1, ds1, u_small = _attn_bwd(qkv[1], dattn, stats, bias_tab, 1,
                                    _ride_chip_exchange([p16 for _, p16 in p_small]))
    rs_ab, rs_pb, rs_out = _chip_sum_small([p32 for p32, _ in p_small], u_small, chip_half)
    dqkv2, ds2, _ = _attn_bwd(qkv[2], dattn, stats, bias_tab, 2, None)

    for j, piece in enumerate(dqkv0 + dqkv1 + dqkv2):
        dproj = lax.dynamic_update_slice(dproj, piece.astype(BF16), (0, j * AW))
    dproj = lax.dynamic_update_slice(dproj, du, (0, QKV_W + AW))
    dw_in = _dw_in(h, dproj)
    drel_rows, (sib_in,) = _bias_grad(jnp.concatenate([ds0, ds1, ds2], axis=0), buckets,
                                      _ride_sibling_halves([dw_in]))
    drel = drel_rows[:, 0, :NUM_BUCKETS].T
    p32_in, p16_in = _pair_sum(dw_in, sib_in, chip_half, "rs_pair_sum_in")
    dh, (u_in,) = _dh(dproj, wg_in, _ride_chip_exchange([p16_in]))
    rs_in = _chip_sum(p32_in, u_in, chip_half, "rs_chip_sum_in")

    grad_x, dnorm_g, dshift, dscale = _prenorm_bwd(x, dh, dx2, norm_g, mod)
    dmod = jnp.concatenate([dshift, dscale, dgate], axis=1)
    return dict(loss=loss[0, 0], grad_x=grad_x, dmod=dmod, dnorm_g=dnorm_g, dfinal_g=dfinal_g, dpool_w=dpool_w,
                dpool_scale=dpool_scale, drel_bias=drel, dw_in=dw_in, dw_attn_br=dw_ab, dw_pool_br=dw_pb,
                dw_out=dw_out, rs_in=rs_in, rs_attn_br=rs_ab, rs_pool_br=rs_pb, rs_out=rs_out)


def _allgather8(blocks, name, relay=None):
    nb = len(blocks)
    relay = [False] * nb if relay is None else list(relay)

    def body(*refs):
        ins, outs = refs[:nb], refs[nb:2 * nb]
        send_sems, recv_sems = refs[2 * nb:]
        x, y, c = lax.axis_index("x"), lax.axis_index("y"), lax.axis_index("c")
        me, sibling = (x, y, c), (x, y, 1 - c)
        here, xn, yn, dg = (x, y), (1 - x, y), (x, 1 - y), (1 - x, 1 - y)

        def slot(a, chip, core, half=None):
            ref = outs[a].at[4 * chip[0] + 2 * chip[1] + core]
            if half is None:
                return ref
            r2 = ref.shape[0] // 2
            return ref.at[pl.ds(half * r2, r2)]

        def copy(a, k, dst, to, src=None):
            return pltpu.make_async_remote_copy(src_ref=dst if src is None else src, dst_ref=dst,
                                                send_sem=send_sems.at[a, k], recv_sem=recv_sems.at[a, k],
                                                device_id=to, device_id_type=MESH)

        def start(cps):
            for cp in cps:
                cp.start()
            return cps

        sent = []
        for a in range(nb):
            own = slot(a, here, c)
            sent += [copy(a, 0, own, sibling, src=ins[a]), copy(a, 1, own, (*xn, c), src=ins[a]),
                     copy(a, 2, own, (*yn, c), src=ins[a])]
            if not relay[a]:
                sent.append(copy(a, 3, own, (*dg, c), src=ins[a]))
        start(sent)
        for a in range(nb):
            copy(a, 2, slot(a, yn, c), me).wait_recv()
            sent += start([copy(a, 6, slot(a, yn, c), sibling)]
                          + ([copy(a, 3, slot(a, yn, c, 0), (*xn, c))] if relay[a] else []))
        for a in range(nb):
            copy(a, 1, slot(a, xn, c), me).wait_recv()
            sent += start([copy(a, 5, slot(a, xn, c), sibling)]
                          + ([copy(a, 4, slot(a, xn, c, 1), (*yn, c))] if relay[a] else []))
        for a in range(nb):
            for k, half in ((3, 0), (4, 1)) if relay[a] else ((3, None),):
                copy(a, k, slot(a, dg, c, half), me).wait_recv()
                sent += start([copy(a, 4 + k, slot(a, dg, c, half), sibling)])
        for a in range(nb):
            copy(a, 0, slot(a, here, 1 - c), me).wait_recv()
            copy(a, 5, slot(a, xn, 1 - c), me).wait_recv()
            copy(a, 6, slot(a, yn, 1 - c), me).wait_recv()
            for k, half in ((7, 0), (8, 1)) if relay[a] else ((7, None),):
                copy(a, k, slot(a, dg, 1 - c, half), me).wait_recv()
        for cp in sent:
            cp.wait_send()

    outs = pl.pallas_call(
        body, name=name, in_specs=[ANY] * nb, out_specs=[ANY] * nb,
        out_shape=[_sds((8,) + b.shape, b.dtype) for b in blocks],
        scratch_shapes=[_dma_sems(nb, 9), _dma_sems(nb, 9)],
    )(*blocks)
    return [_place_own(buf, b) for buf, b in zip(outs, blocks)]


def _place_own(buf, block):
    dev = 4 * lax.axis_index("x") + 2 * lax.axis_index("y") + lax.axis_index("c")
    return lax.dynamic_update_index_in_dim(buf, block, dev, 0)


def _ride_sibling_halves(gs):
    def copies(ins, outs, send_sems, recv_sems):
        x, y, c = lax.axis_index("x"), lax.axis_index("y"), lax.axis_index("c")
        cps = []
        for a in range(len(gs)):
            r2 = ins[a].shape[1] // 2
            other = ins[a].at[:, pl.ds((1 - c) * r2, r2), :]
            cps.append(pltpu.make_async_remote_copy(src_ref=other, dst_ref=outs[a], send_sem=send_sems.at[a],
                                                    recv_sem=recv_sems.at[a], device_id=(x, y, 1 - c),
                                                    device_id_type=MESH))
        return cps

    return _Ride(gs, [_sds((g.shape[0], g.shape[1] // 2, g.shape[2]), g.dtype) for g in gs], len(gs), copies)


def _pair_sum(g, t, chip_half, name):
    nsh, rows, cols = g.shape
    r2 = rows // 2
    tr = _row_tile(r2, cols)
    nt = r2 // tr

    def body(ch_ref, g_ref, t_ref, p32_ref, p16_ref):
        p = g_ref[...] + t_ref[...]
        p16_ref[...] = p.astype(BF16)

        @pl.when(pl.program_id(1) == ch_ref[0])
        def _():
            p32_ref[...] = p

    blk = pl.BlockSpec((None, tr, cols), lambda i, k, ch_ref: (k, i, 0))
    return pl.pallas_call(
        body, name=name,
        grid_spec=pltpu.PrefetchScalarGridSpec(
            num_scalar_prefetch=1, grid=(nt, nsh),
            in_specs=[pl.BlockSpec((None, tr, cols), lambda i, k, ch_ref: (k, ch_ref[1] * nt + i, 0)), blk],
            out_specs=[pl.BlockSpec((tr, cols), lambda i, k, ch_ref: (i, 0)), blk]),
        out_shape=[_sds((r2, cols)), _sds((nsh, r2, cols), BF16)],
        compiler_params=_params("parallel", "arbitrary"),
    )(chip_half, g, t)


def _pair_sum_small(gs, ts, chip_half):
    na = len(gs)

    def body(ch_ref, *refs):
        g_refs, t_refs, outs = refs[:na], refs[na:2 * na], refs[2 * na:]
        for a in range(na):
            r2 = t_refs[a].shape[1]
            own = pl.ds(pl.multiple_of(ch_ref[1] * r2, 8), r2)
            outs[2 * a + 1][...] = (g_refs[a][:, own, :] + t_refs[a][...]).astype(BF16)
            outs[2 * a][...] = g_refs[a][ch_ref[0], own, :] + t_refs[a][ch_ref[0]]

    res = pl.pallas_call(
        body, name="rs_pair_sum_small",
        in_specs=[pl.BlockSpec(memory_space=pltpu.SMEM)] + [pl.BlockSpec(memory_space=pltpu.VMEM)] * (2 * na),
        out_shape=[s for t in ts for s in (_sds(t.shape[1:]), _sds(t.shape, BF16))], compiler_params=_params(),
    )(chip_half, *gs, *ts)
    return [(res[2 * a], res[2 * a + 1]) for a in range(na)]


def _chip_sum_small(p32s, us, chip_half):
    na = len(p32s)

    def body(ch_ref, *refs):
        p_refs, u_refs, outs = refs[:na], refs[na:2 * na], refs[2 * na:]
        for a in range(na):
            r2 = p_refs[a].shape[0]
            acc = p_refs[a][...]
            for j in range(3):
                acc = acc + u_refs[a][j].astype(F32)
            outs[a][pl.ds(pl.multiple_of(ch_ref[1] * r2, 8), r2), :] = acc

    return pl.pallas_call(
        body, name="rs_chip_sum_small",
        in_specs=[pl.BlockSpec(memory_space=pltpu.SMEM)] + [pl.BlockSpec(memory_space=pltpu.VMEM)] * (2 * na),
        out_shape=[_sds((2 * p.shape[0], p.shape[1])) for p in p32s], compiler_params=_params(),
    )(chip_half, *p32s, *us)


def _ride_chip_exchange(ps):
    def copies(ins, outs, send_sems, recv_sems):
        x, y, c = lax.axis_index("x"), lax.axis_index("y"), lax.axis_index("c")
        chips = [(1 - x, y), (x, 1 - y), (1 - x, 1 - y)]
        cps = []
        for a in range(len(ps)):
            for j, (ox, oy) in enumerate(chips):
                cps.append(pltpu.make_async_remote_copy(src_ref=ins[a].at[2 * ox + oy], dst_ref=outs[a].at[j],
                                                        send_sem=send_sems.at[3 * a + j],
                                                        recv_sem=recv_sems.at[3 * a + j],
                                                        device_id=(ox, oy, c), device_id_type=MESH))
        return cps

    return _Ride(ps, [_sds((3,) + p.shape[1:], p.dtype) for p in ps], 3 * len(ps), copies)


def _chip_sum(p32, u, chip_half, name):
    r2, cols = p32.shape
    tr = _row_tile(r2, cols)
    nt = r2 // tr

    def body(ch_ref, p_ref, u_ref, o_ref):
        acc = p_ref[...]
        for j in range(3):
            acc = acc + u_ref[j].astype(F32)
        o_ref[...] = acc

    return pl.pallas_call(
        body, name=name,
        grid_spec=pltpu.PrefetchScalarGridSpec(
            num_scalar_prefetch=1, grid=(nt,),
            in_specs=[pl.BlockSpec((tr, cols), lambda i, ch_ref: (i, 0)),
                      pl.BlockSpec((3, tr, cols), lambda i, ch_ref: (0, i, 0))],
            out_specs=pl.BlockSpec((tr, cols), lambda i, ch_ref: (ch_ref[1] * nt + i, 0))),
        out_shape=_sds((2 * r2, cols)), compiler_params=_params("parallel"),
    )(chip_half, p32, u)


def _sibling_join(fs, name):
    nb = len(fs)

    def body(*refs):
        outs = refs[nb:2 * nb]
        send_sems, recv_sems = refs[2 * nb:]
        x, y, c = lax.axis_index("x"), lax.axis_index("y"), lax.axis_index("c")
        cps = []
        for a in range(nb):
            r2 = outs[a].shape[0] // 2
            rows = outs[a].at[pl.ds(c * r2, r2), :]
            cps.append(pltpu.make_async_remote_copy(src_ref=rows, dst_ref=rows, send_sem=send_sems.at[a],
                                                    recv_sem=recv_sems.at[a], device_id=(x, y, 1 - c),
                                                    device_id_type=MESH))
        for cp in cps:
            cp.start()
        for cp in cps:
            cp.wait()

    return pl.pallas_call(
        body, name=name, in_specs=[ANY] * nb, out_specs=[ANY] * nb,
        out_shape=[_sds(f.shape, f.dtype) for f in fs],
        input_output_aliases={a: a for a in range(nb)},
        scratch_shapes=[_dma_sems(nb), _dma_sems(nb)],
    )(*fs)


def _row_tile(rows, cols):
    tile = rows
    while tile * cols * 4 > (1 << 20) and tile % 16 == 0:
        tile //= 2
    return tile


def _w_ada_grad(c_all, dmod_cols):
    def body(c_ref, d_ref, o_ref):
        o_ref[...] = _dot_tn(c_ref[...].astype(BF16), d_ref[...].astype(BF16))

    return pl.pallas_call(body, name="w_ada_grad", out_shape=_sds((c_all.shape[1], dmod_cols.shape[1])),
                          compiler_params=_params())(c_all, dmod_cols)


def _adam_math(w, g, m, v):
    nm = ADAM_B1 * m + (1.0 - ADAM_B1) * g
    nv = ADAM_B2 * v + (1.0 - ADAM_B2) * (g * g)
    m_hat = nm / (1.0 - ADAM_B1 ** ADAM_STEP)
    v_hat = nv / (1.0 - ADAM_B2 ** ADAM_STEP)
    return -ADAM_LR * (m_hat / (jnp.sqrt(v_hat) + ADAM_EPS) + ADAM_WD * w), nm, nv


def _adamw(w, g, m, v, name):
    rows, cols = w.shape
    tr = _row_tile(rows, cols)

    def body(w_ref, g_ref, m_ref, v_ref, go_ref, d_ref, nm_ref, nv_ref):
        gv = g_ref[...]
        go_ref[...] = gv
        d_ref[...], nm_ref[...], nv_ref[...] = _adam_math(w_ref[...], gv, m_ref[...], v_ref[...])

    spec = pl.BlockSpec((tr, cols), lambda i: (i, 0))
    return pl.pallas_call(
        body, name=name, grid=(rows // tr,), in_specs=[spec] * 4, out_specs=[spec] * 4,
        out_shape=[_sds((rows, cols))] * 4, compiler_params=_params("parallel"),
    )(w, g, m, v)


def _pack_small(dmod, dnorm_g, dfinal_g, dpool_scale, drel_bias, loss, dpool_w):
    return jnp.concatenate([dmod.reshape(-1, 128), dnorm_g.reshape(-1, 128), dfinal_g.reshape(-1, 128),
                            jnp.pad(dpool_scale.reshape(-1, 128), ((0, PK_RELB - PK_PSCALE - AW // 128), (0, 0))),
                            jnp.pad(drel_bias, ((0, 0), (0, 128 - NG * NH))),
                            jnp.full((PK_POOLW - PK_LOSS, 128), loss, F32), dpool_w.reshape(-1, 128)], axis=0)


def _small_update(small_all, ws, ms, vs):
    lane_rows = [(r0, r0 + w.shape[1] // 128) for r0, w in zip((PK_BADA, PK_NORMG, PK_FINALG, PK_PSCALE), ws)]
    nw = len(ws)

    def body(all_ref, *refs):
        w_refs, m_refs, v_refs = refs[:nw], refs[nw:2 * nw], refs[2 * nw:3 * nw]
        loss_ref, outs = refs[3 * nw], refs[3 * nw + 1:]
        g = all_ref[0]
        for s in range(1, all_ref.shape[0]):
            g = g + all_ref[s]
        loss_ref[...] = jnp.broadcast_to(g[PK_LOSS:PK_LOSS + 1, :], loss_ref.shape)

        def put(p, at, gv):
            d, nm, nv = _adam_math(w_refs[p][at], gv, m_refs[p][at], v_refs[p][at])
            for o_ref, val in zip(outs[4 * p:4 * p + 4], (gv, d, nm, nv)):
                o_ref[at] = val

        for p, (r0, r1) in enumerate(lane_rows):
            for i in range(r1 - r0):
                put(p, (slice(None), slice(128 * i, 128 * (i + 1))), g[r0 + i:r0 + i + 1, :])
        put(4, (slice(None), slice(None)), g[PK_RELB:PK_LOSS, 0:NG * NH])
        put(5, (slice(None), slice(None)), g[PK_POOLW:PK_ROWS, :])

    res = pl.pallas_call(
        body, name="small_update",
        out_shape=[_sds((8, 128))] + [_sds(w.shape) for w in ws for _ in range(4)], compiler_params=_params(),
    )(small_all, *ws, *ms, *vs)
    return res[0], [res[1 + 4 * p:5 + 4 * p] for p in range(nw)]


def kernel(x, c, norm_g, w_ada, b_ada, w_in, pool_w, pool_scale, w_attn_br, w_pool_br, w_out, rel_bias, final_g, loss_target, m_norm_g, m_w_ada, m_b_ada, m_w_in, m_pool_w, m_pool_scale, m_w_attn_br, m_w_pool_br, m_w_out, m_rel_bias, m_final_g, v_norm_g, v_w_ada, v_b_ada, v_w_in, v_pool_w, v_pool_scale, v_w_attn_br, v_w_pool_br, v_w_out, v_rel_bias, v_final_g):
    ix, iy, ic = lax.axis_index("x"), lax.axis_index("y"), lax.axis_index("c")
    dev = 4 * ix + 2 * iy + ic
    chip = 2 * ix + iy

    def half(w):
        r2 = w.shape[0] // 2
        return lax.dynamic_slice_in_dim(w, ic * r2, r2, axis=0).astype(BF16)

    gathered = _allgather8([jnp.broadcast_to(c, (8, D)), half(w_in[0]), half(w_attn_br[0]), half(w_pool_br[0]),
                            half(w_out[0])], "gather_weights", relay=[False, True, True, True, True])
    c_all = gathered[0][:, 0, :]
    wg_in = gathered[1].reshape(N_SHARD, D, IN_W // N_SHARD)
    wab = gathered[2].reshape(N_SHARD, AW, D // N_SHARD).transpose(1, 0, 2).reshape(AW, D)
    wpb = gathered[3].reshape(N_SHARD, AW, D // N_SHARD).transpose(1, 0, 2).reshape(AW, D)
    wout = gathered[4].reshape(D, D)

    mw = 3 * D // N_SHARD
    modp = _mod_partial(c_all, w_ada[0], lax.dynamic_slice_in_dim(b_ada, chip * mw, mw, axis=1))
    mod_all = _allgather8([modp], "gather_mod")[0]
    mod_full = mod_all[::2].transpose(1, 0, 2).reshape(8, 3 * D)
    mod = lax.dynamic_slice_in_dim(mod_full, dev, 1, axis=0)

    chip_half = jnp.stack([chip, ic]).astype(jnp.int32)
    r = _local_step(x[0], loss_target[0], mod, wg_in, wab, wpb, wout, pool_w[0], pool_scale, rel_bias, norm_g,
                    final_g.reshape(1, D), chip_half)

    packed = _pack_small(r["dmod"], r["dnorm_g"], r["dfinal_g"], r["dpool_scale"], r["drel_bias"], r["loss"],
                         r["dpool_w"])
    small_all = _allgather8([packed], "gather_small")[0]
    small = ["b_ada", "norm_g", "final_g", "pool_scale", "rel_bias", "pool_w"]
    shaped = lambda b, n, f, ps, rb, pw: [b, n, f.reshape(1, D), ps, rb, pw.reshape(4 * PGW, PGW)]
    loss, small_out = _small_update(small_all, shaped(b_ada, norm_g, final_g, pool_scale, rel_bias, pool_w),
                                    shaped(m_b_ada, m_norm_g, m_final_g, m_pool_scale, m_rel_bias, m_pool_w),
                                    shaped(v_b_ada, v_norm_g, v_final_g, v_pool_scale, v_rel_bias, v_pool_w))
    dmod_all = small_all[:, PK_BADA:PK_NORMG, :].reshape(8, 3 * D)
    g_w_ada = _w_ada_grad(c_all, lax.dynamic_slice_in_dim(dmod_all, chip * mw, mw, axis=1))

    g_w_in, g_w_ab, g_w_pb, g_w_out = _sibling_join([r["rs_in"], r["rs_attn_br"], r["rs_pool_br"], r["rs_out"]],
                                                    "rs_sibling_join")
    upd = dict(zip(small, small_out))
    upd["final_g"] = [a.reshape(D) for a in upd["final_g"]]
    upd["pool_w"] = [a.reshape(1, 4, PGW, PGW) for a in upd["pool_w"]]
    for nme, w, g, m, v in (("w_ada", w_ada, g_w_ada, m_w_ada, v_w_ada), ("w_in", w_in, g_w_in, m_w_in, v_w_in),
                            ("w_attn_br", w_attn_br, g_w_ab, m_w_attn_br, v_w_attn_br),
                            ("w_pool_br", w_pool_br, g_w_pb, m_w_pool_br, v_w_pool_br),
                            ("w_out", w_out, g_w_out, m_w_out, v_w_out)):
        upd[nme] = [a[None] for a in _adamw(w[0], g, m[0], v[0], "adamw_" + nme)]
    names = ["norm_g", "w_ada", "b_ada", "w_in", "pool_w", "pool_scale", "w_attn_br", "w_pool_br", "w_out",
             "rel_bias", "final_g"]
    return (loss[0, 0], r["grad_x"][None]) + tuple(upd[nme][kind] for kind in range(4) for nme in names)
```

```python
import math

import numpy as np
import jax
import jax.numpy as jnp
from jax import lax
from jax.experimental import pallas as pl
from jax.experimental.pallas import tpu as pltpu

F32 = jnp.float32
BF16 = jnp.bfloat16

D = 1024
HD = 64
NH = 8
AW = NH * HD
GROUPS = ((128, 1), (512, 4), (2048, 16))
NG = len(GROUPS)
BLK = 128
GW = 3 * AW
QKV_W = NG * GW
REST_W = 3584
IN_W = QKV_W + REST_W
CB = 512
NCB_QKV = QKV_W // CB
POOL_WINDOWS = (2, 4, 8, 16)
PGW = 128
HALO = 16
NUM_BUCKETS = 32
MAX_DISTANCE = 2048
EPS = 1e-6
NEG = -1e30
N_SHARD = 4
VMEM_LIMIT = 56 * 1024 * 1024

ADAM_LR = 0.001
ADAM_B1 = 0.9
ADAM_B2 = 0.999
ADAM_EPS = 1e-08
ADAM_WD = 0.01
ADAM_STEP = 10

PK_BADA, PK_NORMG, PK_FINALG, PK_PSCALE, PK_RELB, PK_LOSS, PK_POOLW, PK_ROWS = 0, 24, 32, 40, 48, 80, 88, 600

ANY = pl.BlockSpec(memory_space=pl.ANY)
MESH = pl.DeviceIdType.MESH


def _params(*sem):
    return pltpu.CompilerParams(dimension_semantics=sem, vmem_limit_bytes=VMEM_LIMIT)


def _sds(shape, dtype=F32):
    return jax.ShapeDtypeStruct(shape, dtype)


def _dot(a, b):
    return jnp.dot(a, b, preferred_element_type=F32)


def _dot_nt(a, b):
    return lax.dot_general(a, b, (((1,), (1,)), ((), ())), preferred_element_type=F32)


def _dot_tn(a, b):
    return lax.dot_general(a, b, (((0,), (0,)), ((), ())), preferred_element_type=F32)


def _sigmoid(z):
    return 0.5 * jnp.tanh(0.5 * z) + 0.5


def _dma_sems(*shape):
    return pltpu.SemaphoreType.DMA(shape)


class _Ride:
    def __init__(self, arrays, out_shapes, n_copies, copies):
        self.arrays, self.out_shapes, self.n_copies, self.copies = list(arrays), list(out_shapes), n_copies, copies


def _call_with_ride(body, ride, first, last, *, in_specs, out_specs, out_shape, scratch_shapes=(), **kw):
    in_specs, out_specs, out_shape, scratch_shapes = list(in_specs), list(out_specs), list(out_shape), list(scratch_shapes)
    n_in, n_out, n_sc = len(in_specs), len(out_specs), len(scratch_shapes)
    if ride is None:
        def run_plain(*operands):
            return pl.pallas_call(body, in_specs=in_specs, out_specs=out_specs, out_shape=out_shape,
                                  scratch_shapes=scratch_shapes, **kw)(*operands), []
        return run_plain
    n_ri, n_ro = len(ride.arrays), len(ride.out_shapes)

    def wrapped(*refs):
        ins, rest = refs[:n_in], refs[n_in:]
        r_ins, rest = rest[:n_ri], rest[n_ri:]
        outs, rest = rest[:n_out], rest[n_out:]
        r_outs, rest = rest[:n_ro], rest[n_ro:]
        scratch, (send_sems, recv_sems) = rest[:n_sc], rest[n_sc:]

        @pl.when(first())
        def _():
            for cp in ride.copies(r_ins, r_outs, send_sems, recv_sems):
                cp.start()

        body(*ins, *outs, *scratch)

        @pl.when(last())
        def _():
            for cp in ride.copies(r_ins, r_outs, send_sems, recv_sems):
                cp.wait()

    def run(*operands):
        res = pl.pallas_call(
            wrapped, in_specs=in_specs + [ANY] * n_ri, out_specs=out_specs + [ANY] * n_ro,
            out_shape=out_shape + ride.out_shapes,
            scratch_shapes=scratch_shapes + [_dma_sems(ride.n_copies), _dma_sems(ride.n_copies)], **kw,
        )(*operands, *ride.arrays)
        return res[:n_out], res[n_out:]
    return run


def _bucket_tables():
    i = np.arange(BLK)[:, None]
    j = np.arange(2 * BLK)[None, :]
    dist = BLK + i - j
    valid = (dist >= 0) & (dist <= BLK)
    tabs = []
    for _, dil in GROUPS:
        n = (np.clip(dist, 0, BLK) * dil).astype(np.int32)
        max_exact = NUM_BUCKETS // 2
        nf = np.maximum(n, 1).astype(np.float32)
        large = max_exact + (np.log(nf / np.float32(max_exact)) / np.float32(math.log(MAX_DISTANCE / max_exact))
                             * np.float32(NUM_BUCKETS - max_exact)).astype(np.int32)
        large = np.minimum(large, NUM_BUCKETS - 1)
        bucket = np.where(n < max_exact, n, large)
        tab = np.where(valid, bucket, -1).astype(np.int32)
        perm = _block_perm(dil)
        tabs.append(tab[perm][:, np.concatenate([perm, BLK + perm])])
    return np.stack(tabs)


def _bias_table(rel_bias, buckets):
    def body(rb_ref, bk_ref, out_ref):
        g = pl.program_id(0)
        bk = bk_ref[...]
        for h in range(NH):
            acc = jnp.full((BLK, 2 * BLK), NEG, F32)
            for b in range(NUM_BUCKETS):
                acc = jnp.where(bk == b, rb_ref[b, g * NH + h], acc)
            out_ref[h] = acc

    return pl.pallas_call(
        body, name="bias_table", grid=(NG,),
        in_specs=[pl.BlockSpec(memory_space=pltpu.SMEM),
                  pl.BlockSpec((None, BLK, 2 * BLK), lambda g: (g, 0, 0))],
        out_specs=pl.BlockSpec((NH, BLK, 2 * BLK), lambda g: (g, 0, 0)),
        out_shape=_sds((NG * NH, BLK, 2 * BLK)),
        compiler_params=_params("arbitrary"),
    )(rel_bias, buckets)


def _bias_grad(ds_acc, buckets, ride):
    def body(acc_ref, bk_ref, out_ref):
        bk = bk_ref[...]
        acc = acc_ref[...]
        lane = lax.broadcasted_iota(jnp.int32, (8, 128), 1)
        out = jnp.zeros((8, 128), F32)
        for b in range(NUM_BUCKETS):
            val = jnp.sum(jnp.where(bk == b, acc, 0.0))
            out = jnp.where(lane == b, val, out)
        out_ref[...] = out

    (out,), rode = _call_with_ride(
        body, ride, lambda: pl.program_id(0) == 0, lambda: pl.program_id(0) == NG * NH - 1,
        name="bias_grad", grid=(NG * NH,),
        in_specs=[pl.BlockSpec((None, BLK, 2 * BLK), lambda gh: (gh, 0, 0)),
                  pl.BlockSpec((None, BLK, 2 * BLK), lambda gh: (gh // NH, 0, 0))],
        out_specs=[pl.BlockSpec((None, 8, 128), lambda gh: (gh, 0, 0))],
        out_shape=[_sds((NG * NH, 8, 128))],
        compiler_params=_params("arbitrary"),
    )(ds_acc, buckets)
    return out, rode


def _mod_partial(c_all, w_ada_s, b_ada_s):
    def body(c_ref, w_ref, b_ref, o_ref):
        o_ref[...] = _dot(c_ref[...].astype(BF16), w_ref[...].astype(BF16)) + b_ref[...]

    return pl.pallas_call(body, name="mod_partial", out_shape=_sds((8, w_ada_s.shape[1])),
                          compiler_params=_params())(c_all, w_ada_s, b_ada_s)


def _prenorm(x, norm_g, mod):
    S = x.shape[0]
    tm = 512

    def body(x_ref, g_ref, mod_ref, h_ref):
        xv = x_ref[...]
        r = lax.rsqrt(jnp.mean(xv * xv, axis=-1, keepdims=True) + EPS)
        n1 = xv * r * g_ref[...]
        h_ref[...] = (n1 * (1.0 + mod_ref[:, D:2 * D]) + mod_ref[:, 0:D]).astype(BF16)

    return pl.pallas_call(
        body, name="prenorm", grid=(S // tm,),
        in_specs=[pl.BlockSpec((tm, D), lambda i: (i, 0)), pl.BlockSpec((1, D), lambda i: (0, 0)),
                  pl.BlockSpec((1, 3 * D), lambda i: (0, 0))],
        out_specs=pl.BlockSpec((tm, D), lambda i: (i, 0)),
        out_shape=_sds((S, D), BF16), compiler_params=_params("parallel"),
    )(x, norm_g, mod)


def _proj(h, wg_in, j0, nj, dtype, name):
    S = h.shape[0]
    tm = 2048
    per = wg_in.shape[2] // CB

    def body(h_ref, w_ref, o_ref):
        o_ref[...] = _dot(h_ref[...], w_ref[...]).astype(dtype)

    return pl.pallas_call(
        body, name=name, grid=(S // tm, nj),
        in_specs=[pl.BlockSpec((tm, D), lambda m, j: (m, 0)),
                  pl.BlockSpec((None, D, CB), lambda m, j: ((j0 + j) // per, 0, (j0 + j) % per))],
        out_specs=pl.BlockSpec((tm, CB), lambda m, j: (m, j)),
        out_shape=_sds((S, nj * CB), dtype), compiler_params=_params("parallel", "parallel"),
    )(h, wg_in)


HS = 4
SLAB = HS * HD


def _lane_head(rows):
    return lax.broadcasted_iota(jnp.int32, (rows, SLAB), 1) // HD


def _head_stack(a):
    head = _lane_head(a.shape[0])
    return jnp.concatenate([jnp.where(head == h, a, jnp.zeros_like(a)) for h in range(HS)], axis=0)


def _head_unstack(a):
    rows = a.shape[0] // HS
    head = _lane_head(rows)
    out = a[:rows]
    for h in range(1, HS):
        out = jnp.where(head == h, a[h * rows:(h + 1) * rows], out)
    return out


STAT_W = 128
VIEW = 16


def _sub_layout(dil):
    if dil == 1:
        return BLK, [None]
    return BLK * dil // VIEW, [[r + dil * u for u in range(VIEW // dil)] for r in range(dil)]


def _block_perm(dil):
    a_rows, _ = _sub_layout(dil)
    p = np.arange(BLK)
    return p if dil == 1 else (VIEW // dil) * (p % a_rows) + p // a_rows


LB = 128
N_SLAB = NH // HS


RBS = 8


def _ld(refs, bs, s, w, rb=0):
    if bs is None:
        return refs[0][rb * BLK:(rb + 1) * BLK, s * w:(s + 1) * w]
    a_rows = refs[0].shape[0] // VIEW
    return jnp.concatenate([jnp.concatenate([ref[pl.ds(b, a_rows, stride=VIEW), :] for b in bs], axis=0)
                            for ref in refs[s * (w // LB):(s + 1) * (w // LB)]], axis=1)


def _st(ref, bs, s, val, rb=0):
    if bs is None:
        ref[rb * BLK:(rb + 1) * BLK, s * SLAB:(s + 1) * SLAB] = val.astype(ref.dtype)
        return
    a_rows = val.shape[0] // len(bs)
    for u, b in enumerate(bs):
        ref[:, b, s * SLAB:(s + 1) * SLAB] = val[u * a_rows:(u + 1) * a_rows]


def _attn_views(dil, S):
    a_rows, subs = _sub_layout(dil)
    if dil == 1:
        def ispecs(base, w, f):
            return [pl.BlockSpec((RBS * BLK, N_SLAB * w), lambda sg, n: (f(n), base // (N_SLAB * w)))]
        return subs, S // (RBS * BLK), N_SLAB, RBS, ispecs, (lambda w: (S, w)), (
            lambda f: pl.BlockSpec((RBS * BLK, AW), lambda sg, n: (f(n), 0)))

    sps = N_SLAB if dil < VIEW else 1

    def ispecs(base, w, f):
        return [pl.BlockSpec((a_rows * VIEW, LB), lambda sg, n, k=k: (f(n), (base + sg * sps * w) // LB + k))
                for k in range(sps * w // LB)]
    return subs, S // (a_rows * VIEW), sps, 1, ispecs, (lambda w: (S // VIEW, VIEW, w)), (
        lambda f: pl.BlockSpec((a_rows, VIEW, sps * SLAB), lambda sg, n: (f(n), 0, sg)))


def _attn_fwd(qkv_g, bias_tab, g):
    S = qkv_g.shape[0]
    subs, nbq, sps, rbs, ispecs, shape, ospec = _attn_views(GROUPS[g][1], S)
    cur = lambda n: n
    in_specs = [ispecs(0, SLAB, cur), ispecs(AW, SLAB, cur), ispecs(2 * AW, SLAB, cur)]
    nl = len(in_specs[0])

    def body(*refs):
        q, k, v = (refs[t * nl:(t + 1) * nl] for t in range(3))
        b_ref, o_ref, l_ref, kprev, vprev = refs[3 * nl:]
        n = pl.program_id(1)

        @pl.when(n == 0)
        def _():
            kprev[...] = jnp.zeros_like(kprev)
            vprev[...] = jnp.zeros_like(vprev)

        col = lax.broadcasted_iota(jnp.int32, (HS * BLK, 2 * BLK), 1)
        first = (col >= BLK) | (n > 0)
        for s_, rb, (i, bs) in ((s_, rb, sub) for s_ in range(sps) for rb in range(rbs) for sub in enumerate(subs)):
            cs = slice(s_ * SLAB, (s_ + 1) * SLAB)
            kc, vc = _ld(k, bs, s_, SLAB, rb).astype(BF16), _ld(v, bs, s_, SLAB, rb).astype(BF16)
            kb = jnp.concatenate([kprev[i, :, cs], kc], axis=0)
            vb = jnp.concatenate([vprev[i, :, cs], vc], axis=0)
            kprev[i, :, cs], vprev[i, :, cs] = kc, vc
            s = _dot_nt(_head_stack(_ld(q, bs, s_, SLAB, rb).astype(BF16)), kb) * (HD ** -0.5)
            s = s + b_ref[pl.ds(s_ * HS, HS)].reshape(HS * BLK, 2 * BLK)
            if rb == 0:
                s = jnp.where(first, s, NEG)
            m = jnp.max(s, axis=-1, keepdims=True)
            p = jnp.exp(s - m)
            den = jnp.sum(p, axis=-1, keepdims=True)
            _st(o_ref, bs, s_, _head_unstack(_dot(p.astype(BF16), vb) / den), rb)
            _st(l_ref, bs, s_, _head_unstack(jnp.broadcast_to(m + jnp.log(den), (HS * BLK, SLAB))), rb)

    out = _sds(shape(AW))
    nsg = N_SLAB // sps
    o, l = pl.pallas_call(
        body, name=f"attn_fwd{g}", grid=(nsg, nbq),
        in_specs=sum(in_specs, []) + [pl.BlockSpec((sps * HS, BLK, 2 * BLK), lambda sg, n: (g * nsg + sg, 0, 0))],
        out_specs=[ospec(cur), ospec(cur)],
        out_shape=[out, out],
        scratch_shapes=[pltpu.VMEM((len(subs), BLK, sps * SLAB), BF16)] * 2,
        compiler_params=_params("parallel", "arbitrary"),
    )(*([qkv_g] * (3 * nl)), bias_tab)
    return o.reshape(S, AW), l.reshape(S, AW)


def _attn_bwd(qkv_g, dattn, stats, bias_tab, g, ride):
    S = qkv_g.shape[0]
    subs, nbq, sps, rbs, ispecs, shape, ospec = _attn_views(GROUPS[g][1], S)
    cur = lambda n: jnp.minimum(n, nbq - 1)
    late = lambda n: jnp.maximum(n - 1, 0)
    in_specs = [ispecs(0, SLAB, cur), ispecs(AW, SLAB, cur), ispecs(2 * AW, SLAB, cur), ispecs(0, SLAB, cur),
                ispecs(0, STAT_W, cur)]
    nl = len(in_specs[0])

    def body(*refs):
        q, k, v, da = (refs[t * nl:(t + 1) * nl] for t in range(4))
        nst = len(in_specs[4])
        st_refs = refs[4 * nl:4 * nl + nst]
        b_ref, dq_ref, dk_ref, dv_ref, ds_ref, ck_ref, cv_ref, kprev, vprev, *held = refs[4 * nl + nst:]
        n = pl.program_id(1)

        @pl.when(n == 0)
        def _():
            for ref in (ds_ref, ck_ref, cv_ref, kprev, vprev, *held):
                ref[...] = jnp.zeros_like(ref)

        def finish(ref, t, bs, s_, rb, val):
            cs = slice(s_ * SLAB, (s_ + 1) * SLAB)
            if rbs == 1:
                _st(ref, bs, s_, val)
            elif rb == 0:
                for j in range(rbs - 1):
                    _st(ref, bs, s_, held[t][j * BLK:(j + 1) * BLK, cs], j)
                _st(ref, bs, s_, val, rbs - 1)
            else:
                held[t][(rb - 1) * BLK:rb * BLK, cs] = val

        @pl.when(n < nbq)
        def _():
            col = lax.broadcasted_iota(jnp.int32, (HS * BLK, 2 * BLK), 1)
            first = (col >= BLK) | (n > 0)
            for s_, rb, (i, bs) in ((s_, rb, sub) for s_ in range(sps) for rb in range(rbs) for sub in enumerate(subs)):
                cs = slice(s_ * SLAB, (s_ + 1) * SLAB)
                st = _ld(st_refs, bs, s_, STAT_W, rb)
                kc, vc = _ld(k, bs, s_, SLAB, rb).astype(BF16), _ld(v, bs, s_, SLAB, rb).astype(BF16)
                kb = jnp.concatenate([kprev[i, :, cs], kc], axis=0)
                vb = jnp.concatenate([vprev[i, :, cs], vc], axis=0)
                kprev[i, :, cs], vprev[i, :, cs] = kc, vc
                lse = jnp.concatenate([st[:, h:h + 1] for h in range(HS)], axis=0)
                delta = jnp.concatenate([st[:, HS + h:HS + h + 1] for h in range(HS)], axis=0)
                qs = _head_stack(_ld(q, bs, s_, SLAB, rb).astype(BF16))
                dos = _head_stack(_ld(da, bs, s_, SLAB, rb).astype(BF16))
                s = _dot_nt(qs, kb) * (HD ** -0.5) + b_ref[pl.ds(s_ * HS, HS)].reshape(HS * BLK, 2 * BLK)
                if rb == 0:
                    s = jnp.where(first, s, NEG)
                p = jnp.exp(s - lse)
                ds = p * (_dot_nt(dos, vb) - delta)
                ds_ref[pl.ds(s_ * HS, HS)] += ds.reshape(HS, BLK, 2 * BLK)
                ds_b = (ds * (HD ** -0.5)).astype(BF16)
                _st(dq_ref, bs, s_, _head_unstack(_dot(ds_b, kb)), rb)
                dkb = _dot_tn(ds_b, qs)
                dvb = _dot_tn(p.astype(BF16), dos)
                finish(dk_ref, 0, bs, s_, rb, ck_ref[i, :, cs] + dkb[:BLK])
                finish(dv_ref, 1, bs, s_, rb, cv_ref[i, :, cs] + dvb[:BLK])
                ck_ref[i, :, cs] = dkb[BLK:]
                cv_ref[i, :, cs] = dvb[BLK:]

        @pl.when(n == nbq)
        def _():
            for s_ in range(sps):
                for i, bs in enumerate(subs):
                    finish(dk_ref, 0, bs, s_, 0, ck_ref[i, :, s_ * SLAB:(s_ + 1) * SLAB])
                    finish(dv_ref, 1, bs, s_, 0, cv_ref[i, :, s_ * SLAB:(s_ + 1) * SLAB])

    out = _sds(shape(AW), BF16 if GROUPS[g][1] == 1 else F32)
    nsg = N_SLAB // sps
    (dq, dk, dv, ds_acc), rode = _call_with_ride(
        body, ride, lambda: (pl.program_id(0) == 0) & (pl.program_id(1) == 0),
        lambda: (pl.program_id(0) == nsg - 1) & (pl.program_id(1) == nbq),
        name=f"attn_bwd{g}", grid=(nsg, nbq + 1),
        in_specs=sum(in_specs, []) + [pl.BlockSpec((sps * HS, BLK, 2 * BLK), lambda sg, n: (g * nsg + sg, 0, 0))],
        out_specs=[ospec(cur), ospec(late), ospec(late),
                   pl.BlockSpec((sps * HS, BLK, 2 * BLK), lambda sg, n: (sg, 0, 0))],
        out_shape=[out] * 3 + [_sds((NH, BLK, 2 * BLK))],
        scratch_shapes=[pltpu.VMEM((len(subs), BLK, sps * SLAB), F32)] * 2
        + [pltpu.VMEM((len(subs), BLK, sps * SLAB), BF16)] * 2 + [pltpu.VMEM(((rbs - 1) * BLK, sps * SLAB), F32)] * (2 if rbs > 1 else 0),
        compiler_params=_params("arbitrary", "arbitrary"),
    )(*([qkv_g] * (3 * nl)), *([dattn] * nl), *([stats] * len(in_specs[4])), bias_tab)
    return [dq.reshape(S, AW), dk.reshape(S, AW), dv.reshape(S, AW)], ds_acc, rode


TM_MIX = 256


def _mix_specs(tm):
    row512 = pl.BlockSpec((tm, AW), lambda i: (i, 0))
    return ([row512] * 6 + [
        pl.BlockSpec((tm, REST_W), lambda i: (i, 0)),
        pl.BlockSpec((HALO, AW), lambda i: (jnp.maximum(i * (tm // HALO) - 1, 0), 1)),
        pl.BlockSpec((AW, D), lambda i: (0, 0)), pl.BlockSpec((AW, D), lambda i: (0, 0)),
        pl.BlockSpec((4, PGW, PGW), lambda i: (0, 0, 0)), pl.BlockSpec((1, AW), lambda i: (0, 0))])


def _mix_forward(i, tm, o_refs, l_refs, rest_ref, halo_ref, wab_ref, wpb_ref, pw_ref, ps_ref):
    l0, l1, l2 = (r[...] for r in l_refs)
    mx = jnp.maximum(jnp.maximum(l0, l1), l2)
    e0, e1, e2 = jnp.exp(l0 - mx), jnp.exp(l1 - mx), jnp.exp(l2 - mx)
    den = e0 + e1 + e2
    lj = mx + jnp.log(den)
    attn = (e0 * o_refs[0][...] + e1 * o_refs[1][...] + e2 * o_refs[2][...]) / den

    z_attn = rest_ref[:, 0:AW]
    u = rest_ref[:, AW:2 * AW]
    z_pool = rest_ref[:, 2 * AW:3 * AW]
    g_attn = rest_ref[:, 3 * AW:3 * AW + D]
    g_pool = rest_ref[:, 3 * AW + D:3 * AW + 2 * D]

    sg_a = _sigmoid(z_attn)
    sil_a = z_attn * sg_a
    a_g = (attn * sil_a).astype(BF16)
    y_attn = _dot(a_g, wab_ref[...])

    halo = jnp.where(i > 0, halo_ref[...], 0.0)
    ext = jnp.concatenate([halo, u], axis=0)
    t = i * tm + lax.broadcasted_iota(jnp.int32, (tm, 1), 0)
    pooled, mixed_raw = [], []
    for gi, win in enumerate(POOL_WINDOWS):
        s = ext[:, gi * PGW:(gi + 1) * PGW]
        sh = 1
        while sh < win:
            s = s + pltpu.roll(s, sh, 0)
            sh *= 2
        cnt = jnp.minimum(t + 1, win).astype(F32)
        pg = s[HALO:] / cnt - u[:, gi * PGW:(gi + 1) * PGW]
        pooled.append(pg.astype(BF16))
        mixed_raw.append(_dot(pooled[-1], pw_ref[gi].astype(BF16)))
    mixed_raw = jnp.concatenate(mixed_raw, axis=1)
    mixed = mixed_raw * ps_ref[...]
    sg_p = _sigmoid(z_pool)
    sil_p = z_pool * sg_p
    m_g = (mixed * sil_p).astype(BF16)
    y_pool = _dot(m_g, wpb_ref[...])

    sa = _sigmoid(g_attn)
    sp = _sigmoid(g_pool)
    merged = sa * y_attn + sp * y_pool
    return dict(lj=lj, attn=attn, z_attn=z_attn, z_pool=z_pool, sg_a=sg_a, sil_a=sil_a, a_g=a_g, y_attn=y_attn,
                pooled=pooled, mixed_raw=mixed_raw, mixed=mixed, sg_p=sg_p, sil_p=sil_p, m_g=m_g, y_pool=y_pool,
                sa=sa, sp=sp, merged=merged)


def _mix_step(x, target, os_, ls_, rest, wab, wpb, pool_w, pool_scale, wout, mod, final_g):
    S = x.shape[0]
    tm = TM_MIX
    nt = S // tm
    sw = D // N_SHARD

    def body(o0, o1, o2, l0, l1, l2, rest_ref, halo_ref, wab_ref, wpb_ref, pw_ref, ps_ref,
             x_ref, t_ref, wo_ref, mod_ref, fg_ref, dx2_ref, loss_ref, dfg_ref, dgate_ref,
             dattn_ref, stats_ref, dpooled_ref, dproj_hbm, dwo_hbm, dwab_hbm, dwpb_hbm, dpw_ref, dps_ref,
             awo, awab, awpb, stage, stage_sem):
        i = pl.program_id(0)
        slot = i % 2

        def staged(step, sl):
            return pltpu.make_async_copy(stage.at[sl], dproj_hbm.at[pl.ds(step * tm, tm), pl.ds(QKV_W, REST_W)],
                                         stage_sem.at[sl])

        @pl.when(i == 0)
        def _():
            for ref in (loss_ref, dfg_ref, dgate_ref, awo, awab, awpb, dpw_ref, dps_ref):
                ref[...] = jnp.zeros_like(ref)

        f = _mix_forward(i, tm, (o0, o1, o2), (l0, l1, l2), rest_ref, halo_ref, wab_ref, wpb_ref, pw_ref, ps_ref)
        mo = _dot(f["merged"].astype(BF16), wo_ref[...])
        gate = mod_ref[:, 2 * D:3 * D]
        fg = fg_ref[...]
        x2 = x_ref[...] + gate * mo
        r2 = lax.rsqrt(jnp.mean(x2 * x2, axis=-1, keepdims=True) + EPS)
        n2 = x2 * r2
        err = n2 * fg - t_ref[...]
        loss_ref[...] += 0.5 * jnp.sum(jnp.mean(err * err, axis=-1, keepdims=True))
        dy = err * (1.0 / D)
        dfg_ref[...] += jnp.sum(dy * n2, axis=0, keepdims=True)
        dn = dy * fg
        dx2 = r2 * (dn - n2 * jnp.mean(dn * n2, axis=-1, keepdims=True))
        dgate_ref[...] += jnp.sum(dx2 * mo, axis=0, keepdims=True)
        dx2_ref[...] = dx2

        dmo_b = (dx2 * gate).astype(BF16)
        dmerged = _dot_nt(dmo_b, wo_ref[...])
        awo[...] += _dot_tn(f["merged"].astype(BF16), dmo_b)
        sa, sp = f["sa"], f["sp"]
        dya = (dmerged * sa).astype(BF16)
        dyp = (dmerged * sp).astype(BF16)
        dg_attn = dmerged * f["y_attn"] * sa * (1.0 - sa)
        dg_pool = dmerged * f["y_pool"] * sp * (1.0 - sp)
        dag = _dot_nt(dya, wab_ref[...])
        awab[...] += _dot_tn(f["a_g"], dya)
        dmg = _dot_nt(dyp, wpb_ref[...])
        awpb[...] += _dot_tn(f["m_g"], dyp)
        dattn = dag * f["sil_a"]
        dattn_ref[...] = dattn
        prod = dattn * f["attn"]
        lane = lax.broadcasted_iota(jnp.int32, (tm, STAT_W), 1)
        for sb in range(N_SLAB):
            st = jnp.zeros((tm, STAT_W), F32)
            for h in range(HS):
                hs = slice((sb * HS + h) * HD, (sb * HS + h + 1) * HD)
                st = jnp.where(lane == h, f["lj"][:, hs.start:hs.start + 1], st)
                st = jnp.where(lane == HS + h, jnp.sum(prod[:, hs], axis=-1, keepdims=True), st)
            stats_ref[:, sb * STAT_W:(sb + 1) * STAT_W] = st
        dz_attn = dag * f["attn"] * (f["sg_a"] * (1.0 + f["z_attn"] * (1.0 - f["sg_a"])))
        dmixed = dmg * f["sil_p"]
        dz_pool = dmg * f["mixed"] * (f["sg_p"] * (1.0 + f["z_pool"] * (1.0 - f["sg_p"])))
        dps_ref[...] += jnp.sum(dmixed * f["mixed_raw"], axis=0, keepdims=True)
        dpm = (dmixed * ps_ref[...]).astype(BF16)
        for gi in range(len(POOL_WINDOWS)):
            cs = slice(gi * PGW, (gi + 1) * PGW)
            dpw_ref[gi] += _dot_tn(f["pooled"][gi], dpm[:, cs])
            dpooled_ref[:, cs] = _dot_nt(dpm[:, cs], pw_ref[gi].astype(BF16))
        @pl.when(i >= 2)
        def _():
            staged(i - 2, slot).wait()

        stage[slot, :, 0:AW] = dz_attn.astype(BF16)
        stage[slot, :, AW:2 * AW] = jnp.zeros((tm, AW), BF16)
        stage[slot, :, 2 * AW:3 * AW] = dz_pool.astype(BF16)
        stage[slot, :, 3 * AW:3 * AW + D] = dg_attn.astype(BF16)
        stage[slot, :, 3 * AW + D:3 * AW + 2 * D] = dg_pool.astype(BF16)
        staged(i, slot).start()

        @pl.when(i == nt - 1)
        def _():
            staged(i - 1, 1 - slot).wait()
            staged(i, slot).wait()
            pltpu.sync_copy(awo, dwo_hbm)
            for k in range(N_SHARD):
                pltpu.sync_copy(awab.at[:, pl.ds(k * sw, sw)], dwab_hbm.at[k])
                pltpu.sync_copy(awpb.at[:, pl.ds(k * sw, sw)], dwpb_hbm.at[k])

    row = pl.BlockSpec((tm, D), lambda i: (i, 0))
    vec = pl.BlockSpec((1, D), lambda i: (0, 0))
    row512 = pl.BlockSpec((tm, AW), lambda i: (i, 0))
    outs = pl.pallas_call(
        body, name="mix_step", grid=(nt,),
        in_specs=_mix_specs(tm) + [row, row, pl.BlockSpec((D, D), lambda i: (0, 0)),
                                   pl.BlockSpec((1, 3 * D), lambda i: (0, 0)), vec],
        out_specs=[row, pl.BlockSpec((8, 128), lambda i: (0, 0)), vec, vec,
                   row512, pl.BlockSpec((tm, N_SLAB * STAT_W), lambda i: (i, 0)), row512, ANY, ANY, ANY, ANY,
                   pl.BlockSpec((4, PGW, PGW), lambda i: (0, 0, 0)), pl.BlockSpec((1, AW), lambda i: (0, 0))],
        out_shape=[_sds((S, D)), _sds((8, 128)), _sds((1, D)), _sds((1, D)),
                   _sds((S, AW)), _sds((S, N_SLAB * STAT_W)), _sds((S, AW)), _sds((S, IN_W), BF16),
                   _sds((D, D)), _sds((N_SHARD, AW, sw)), _sds((N_SHARD, AW, sw)), _sds((4, PGW, PGW)), _sds((1, AW))],
        scratch_shapes=[pltpu.VMEM((D, D), F32), pltpu.VMEM((AW, D), F32), pltpu.VMEM((AW, D), F32),
                        pltpu.VMEM((2, tm, REST_W), BF16), _dma_sems(2)],
        compiler_params=_params("arbitrary"),
    )(*os_, *ls_, rest, rest, wab, wpb, pool_w, pool_scale, x, target, wout, mod, final_g)
    dx2, loss, dfg, dgate, dattn, stats, dpooled, dproj, dwo, dwab, dwpb, dpw, dps = outs
    return (dx2, loss, dfg, dgate, dattn, stats, dpooled, dproj, dwo.reshape(N_SHARD, D // N_SHARD, D), dwab, dwpb,
            dpw, dps)


def _pool_bwd(dpooled):
    S = dpooled.shape[0]
    tm = 512
    nt = S // tm

    def body(dp_ref, nxt_ref, du_ref):
        i = pl.program_id(0)
        t = i * tm + lax.broadcasted_iota(jnp.int32, (tm + HALO, 1), 0)
        nxt = jnp.where(i < nt - 1, nxt_ref[...], 0.0)
        ext = jnp.concatenate([dp_ref[...], nxt], axis=0)
        for gi, win in enumerate(POOL_WINDOWS):
            cs = slice(gi * PGW, (gi + 1) * PGW)
            s = ext[:, cs] / jnp.minimum(t + 1, win).astype(F32)
            sh = 1
            while sh < win:
                s = s + pltpu.roll(s, tm + HALO - sh, 0)
                sh *= 2
            du_ref[:, cs] = (s[:tm] - dp_ref[:, cs]).astype(BF16)

    return pl.pallas_call(
        body, name="pool_bwd", grid=(nt,),
        in_specs=[pl.BlockSpec((tm, AW), lambda i: (i, 0)),
                  pl.BlockSpec((HALO, AW), lambda i: (jnp.minimum((i + 1) * (tm // HALO), S // HALO - 1), 0))],
        out_specs=pl.BlockSpec((tm, AW), lambda i: (i, 0)),
        out_shape=_sds((S, AW), BF16), compiler_params=_params("parallel"),
    )(dpooled, dpooled)


TB = 1024


def _dh(dproj, wg_in, ride):
    S = dproj.shape[0]
    per = wg_in.shape[2] // TB
    nm, nk = S // TB, IN_W // TB

    def body(dp_ref, w_ref, out_ref):
        @pl.when(pl.program_id(1) == 0)
        def _():
            out_ref[...] = jnp.zeros_like(out_ref)

        out_ref[...] += _dot_nt(dp_ref[...], w_ref[...])

    (dh,), rode = _call_with_ride(
        body, ride, lambda: (pl.program_id(0) == 0) & (pl.program_id(1) == 0),
        lambda: (pl.program_id(0) == nm - 1) & (pl.program_id(1) == nk - 1),
        name="dh", grid=(nm, nk),
        in_specs=[pl.BlockSpec((TB, TB), lambda m, kk: (m, kk)),
                  pl.BlockSpec((None, D, TB), lambda m, kk: (kk // per, 0, kk % per))],
        out_specs=[pl.BlockSpec((TB, D), lambda m, kk: (m, 0))],
        out_shape=[_sds((S, D))], compiler_params=_params("arbitrary", "arbitrary"),
    )(dproj, wg_in)
    return dh, rode


def _dw_in(h, dproj):
    S = dproj.shape[0]
    per = IN_W // N_SHARD // TB

    def body(h_ref, dp_ref, out_ref):
        out_ref[...] = _dot_tn(h_ref[...], dp_ref[...])

    return pl.pallas_call(
        body, name="dw_in", grid=(IN_W // TB,),
        in_specs=[pl.BlockSpec((S, D), lambda j: (0, 0)), pl.BlockSpec((S, TB), lambda j: (0, j))],
        out_specs=pl.BlockSpec((None, D, TB), lambda j: (j // per, 0, j % per)),
        out_shape=_sds((N_SHARD, D, IN_W // N_SHARD)), compiler_params=_params("parallel"),
    )(h, dproj)


def _prenorm_bwd(x, dh, dx2, norm_g, mod):
    S = x.shape[0]
    tm = 512

    def body(x_ref, dh_ref, dx2_ref, g_ref, mod_ref, gx_ref, dg_ref, dshift_ref, dscale_ref):
        i = pl.program_id(0)

        @pl.when(i == 0)
        def _():
            dg_ref[...] = jnp.zeros_like(dg_ref)
            dshift_ref[...] = jnp.zeros_like(dshift_ref)
            dscale_ref[...] = jnp.zeros_like(dscale_ref)

        xv = x_ref[...]
        dhv = dh_ref[...]
        g = g_ref[...]
        r = lax.rsqrt(jnp.mean(xv * xv, axis=-1, keepdims=True) + EPS)
        xh = xv * r
        dshift_ref[...] += jnp.sum(dhv, axis=0, keepdims=True)
        dscale_ref[...] += jnp.sum(dhv * (xh * g), axis=0, keepdims=True)
        dn1 = dhv * (1.0 + mod_ref[:, D:2 * D])
        dg_ref[...] += jnp.sum(dn1 * xh, axis=0, keepdims=True)
        dxh = dn1 * g
        gx_ref[...] = dx2_ref[...] + r * (dxh - xh * jnp.mean(dxh * xh, axis=-1, keepdims=True))

    row = pl.BlockSpec((tm, D), lambda i: (i, 0))
    vec = pl.BlockSpec((1, D), lambda i: (0, 0))
    return pl.pallas_call(
        body, name="prenorm_bwd", grid=(S // tm,),
        in_specs=[row, row, row, vec, pl.BlockSpec((1, 3 * D), lambda i: (0, 0))],
        out_specs=[row, vec, vec, vec],
        out_shape=[_sds((S, D)), _sds((1, D)), _sds((1, D)), _sds((1, D))],
        compiler_params=_params("arbitrary"),
    )(x, dh, dx2, norm_g, mod)


def _local_step(x, target, mod, wg_in, wab, wpb, wout, pool_w, pool_scale, rel_bias, norm_g, final_g, chip_half):
    buckets = jnp.asarray(_bucket_tables())
    bias_tab = _bias_table(rel_bias, buckets)
    h = _prenorm(x, norm_g, mod)
    qkv = [_proj(h, wg_in, 3 * g, 3, BF16 if GROUPS[g][1] == 1 else F32, f"proj_qkv{g}") for g in range(NG)]
    rest = _proj(h, wg_in, NCB_QKV, REST_W // CB, F32, "proj_rest")
    os_, ls_ = zip(*[_attn_fwd(qkv[g], bias_tab, g) for g in range(NG)])
    (dx2, loss, dfinal_g, dgate, dattn, stats, dpooled, dproj, dw_out, dw_ab, dw_pb, dpool_w,
     dpool_scale) = _mix_step(x, target, os_, ls_, rest, wab, wpb, pool_w, pool_scale, wout, mod, final_g)
    du = _pool_bwd(dpooled)

    small = [dw_ab, dw_pb, dw_out]
    dqkv0, ds0, sib_small = _attn_bwd(qkv[0], dattn, stats, bias_tab, 0, _ride_sibling_halves(small))
    p_small = _pair_sum_small(small, sib_small, chip_half)
    dqkv1, ds1, u_small = _attn_bwd(qkv[1], dattn, stats, bias_tab, 1,
                                    _ride_chip_exchange([p16 for _, p16 in p_small]))
    rs_ab, rs_pb, rs_out = _chip_sum_small([p32 for p32, _ in p_small], u_small, chip_half)
    dqkv2, ds2, _ = _attn_bwd(qkv[2], dattn, stats, bias_tab, 2, None)

    for j, piece in enumerate(dqkv0 + dqkv1 + dqkv2):
        dproj = lax.dynamic_update_slice(dproj, piece.astype(BF16), (0, j * AW))
    dproj = lax.dynamic_update_slice(dproj, du, (0, QKV_W + AW))
    dw_in = _dw_in(h, dproj)
    drel_rows, (sib_in,) = _bias_grad(jnp.concatenate([ds0, ds1, ds2], axis=0), buckets,
                                      _ride_sibling_halves([dw_in]))
    drel = drel_rows[:, 0, :NUM_BUCKETS].T
    p32_in, p16_in = _pair_sum(dw_in, sib_in, chip_half, "rs_pair_sum_in")
    dh, (u_in,) = _dh(dproj, wg_in, _ride_chip_exchange([p16_in]))
    rs_in = _chip_sum(p32_in, u_in, chip_half, "rs_chip_sum_in")

    grad_x, dnorm_g, dshift, dscale = _prenorm_bwd(x, dh, dx2, norm_g, mod)
    dmod = jnp.concatenate([dshift, dscale, dgate], axis=1)
    return dict(loss=loss[0, 0], grad_x=grad_x, dmod=dmod, dnorm_g=dnorm_g, dfinal_g=dfinal_g, dpool_w=dpool_w,
                dpool_scale=dpool_scale, drel_bias=drel, dw_in=dw_in, dw_attn_br=dw_ab, dw_pool_br=dw_pb,
                dw_out=dw_out, rs_in=rs_in, rs_attn_br=rs_ab, rs_pool_br=rs_pb, rs_out=rs_out)


def _allgather8(blocks, name, relay=None):
    nb = len(blocks)
    relay = [False] * nb if relay is None else list(relay)

    def body(*refs):
        ins, outs = refs[:nb], refs[nb:2 * nb]
        send_sems, recv_sems = refs[2 * nb:]
        x, y, c = lax.axis_index("x"), lax.axis_index("y"), lax.axis_index("c")
        me, sibling = (x, y, c), (x, y, 1 - c)
        here, xn, yn, dg = (x, y), (1 - x, y), (x, 1 - y), (1 - x, 1 - y)

        def slot(a, chip, core, half=None):
            ref = outs[a].at[4 * chip[0] + 2 * chip[1] + core]
            if half is None:
                return ref
            r2 = ref.shape[0] // 2
            return ref.at[pl.ds(half * r2, r2)]

        def copy(a, k, dst, to, src=None):
            return pltpu.make_async_remote_copy(src_ref=dst if src is None else src, dst_ref=dst,
                                                send_sem=send_sems.at[a, k], recv_sem=recv_sems.at[a, k],
                                                device_id=to, device_id_type=MESH)

        def start(cps):
            for cp in cps:
                cp.start()
            return cps

        sent = []
        for a in range(nb):
            own = slot(a, here, c)
            sent += [copy(a, 0, own, sibling, src=ins[a]), copy(a, 1, own, (*xn, c), src=ins[a]),
                     copy(a, 2, own, (*yn, c), src=ins[a])]
            if not relay[a]:
                sent.append(copy(a, 3, own, (*dg, c), src=ins[a]))
        start(sent)
        for a in range(nb):
            copy(a, 2, slot(a, yn, c), me).wait_recv()
            sent += start([copy(a, 6, slot(a, yn, c), sibling)]
                          + ([copy(a, 3, slot(a, yn, c, 0), (*xn, c))] if relay[a] else []))
        for a in range(nb):
            copy(a, 1, slot(a, xn, c), me).wait_recv()
            sent += start([copy(a, 5, slot(a, xn, c), sibling)]
                          + ([copy(a, 4, slot(a, xn, c, 1), (*yn, c))] if relay[a] else []))
        for a in range(nb):
            for k, half in ((3, 0), (4, 1)) if relay[a] else ((3, None),):
                copy(a, k, slot(a, dg, c, half), me).wait_recv()
                sent += start([copy(a, 4 + k, slot(a, dg, c, half), sibling)])
        for a in range(nb):
            copy(a, 0, slot(a, here, 1 - c), me).wait_recv()
            copy(a, 5, slot(a, xn, 1 - c), me).wait_recv()
            copy(a, 6, slot(a, yn, 1 - c), me).wait_recv()
            for k, half in ((7, 0), (8, 1)) if relay[a] else ((7, None),):
                copy(a, k, slot(a, dg, 1 - c, half), me).wait_recv()
        for cp in sent:
            cp.wait_send()

    outs = pl.pallas_call(
        body, name=name, in_specs=[ANY] * nb, out_specs=[ANY] * nb,
        out_shape=[_sds((8,) + b.shape, b.dtype) for b in blocks],
        scratch_shapes=[_dma_sems(nb, 9), _dma_sems(nb, 9)],
    )(*blocks)
    return [_place_own(buf, b) for buf, b in zip(outs, blocks)]


def _place_own(buf, block):
    dev = 4 * lax.axis_index("x") + 2 * lax.axis_index("y") + lax.axis_index("c")
    return lax.dynamic_update_index_in_dim(buf, block, dev, 0)


def _ride_sibling_halves(gs):
    def copies(ins, outs, send_sems, recv_sems):
        x, y, c = lax.axis_index("x"), lax.axis_index("y"), lax.axis_index("c")
        cps = []
        for a in range(len(gs)):
            r2 = ins[a].shape[1] // 2
            other = ins[a].at[:, pl.ds((1 - c) * r2, r2), :]
            cps.append(pltpu.make_async_remote_copy(src_ref=other, dst_ref=outs[a], send_sem=send_sems.at[a],
                                                    recv_sem=recv_sems.at[a], device_id=(x, y, 1 - c),
                                                    device_id_type=MESH))
        return cps

    return _Ride(gs, [_sds((g.shape[0], g.shape[1] // 2, g.shape[2]), g.dtype) for g in gs], len(gs), copies)


def _pair_sum(g, t, chip_half, name):
    nsh, rows, cols = g.shape
    r2 = rows // 2
    tr = _row_tile(r2, cols)
    nt = r2 // tr

    def body(ch_ref, g_ref, t_ref, p32_ref, p16_ref):
        p = g_ref[...] + t_ref[...]
        p16_ref[...] = p.astype(BF16)

        @pl.when(pl.program_id(1) == ch_ref[0])
        def _():
            p32_ref[...] = p

    blk = pl.BlockSpec((None, tr, cols), lambda i, k, ch_ref: (k, i, 0))
    return pl.pallas_call(
        body, name=name,
        grid_spec=pltpu.PrefetchScalarGridSpec(
            num_scalar_prefetch=1, grid=(nt, nsh),
            in_specs=[pl.BlockSpec((None, tr, cols), lambda i, k, ch_ref: (k, ch_ref[1] * nt + i, 0)), blk],
            out_specs=[pl.BlockSpec((tr, cols), lambda i, k, ch_ref: (i, 0)), blk]),
        out_shape=[_sds((r2, cols)), _sds((nsh, r2, cols), BF16)],
        compiler_params=_params("parallel", "arbitrary"),
    )(chip_half, g, t)


def _pair_sum_small(gs, ts, chip_half):
    na = len(gs)

    def body(ch_ref, *refs):
        g_refs, t_refs, outs = refs[:na], refs[na:2 * na], refs[2 * na:]
        for a in range(na):
            r2 = t_refs[a].shape[1]
            own = pl.ds(pl.multiple_of(ch_ref[1] * r2, 8), r2)
            outs[2 * a + 1][...] = (g_refs[a][:, own, :] + t_refs[a][...]).astype(BF16)
            outs[2 * a][...] = g_refs[a][ch_ref[0], own, :] + t_refs[a][ch_ref[0]]

    res = pl.pallas_call(
        body, name="rs_pair_sum_small",
        in_specs=[pl.BlockSpec(memory_space=pltpu.SMEM)] + [pl.BlockSpec(memory_space=pltpu.VMEM)] * (2 * na),
        out_shape=[s for t in ts for s in (_sds(t.shape[1:]), _sds(t.shape, BF16))], compiler_params=_params(),
    )(chip_half, *gs, *ts)
    return [(res[2 * a], res[2 * a + 1]) for a in range(na)]


def _chip_sum_small(p32s, us, chip_half):
    na = len(p32s)

    def body(ch_ref, *refs):
        p_refs, u_refs, outs = refs[:na], refs[na:2 * na], refs[2 * na:]
        for a in range(na):
            r2 = p_refs[a].shape[0]
            acc = p_refs[a][...]
            for j in range(3):
                acc = acc + u_refs[a][j].astype(F32)
            outs[a][pl.ds(pl.multiple_of(ch_ref[1] * r2, 8), r2), :] = acc

    return pl.pallas_call(
        body, name="rs_chip_sum_small",
        in_specs=[pl.BlockSpec(memory_space=pltpu.SMEM)] + [pl.BlockSpec(memory_space=pltpu.VMEM)] * (2 * na),
        out_shape=[_sds((2 * p.shape[0], p.shape[1])) for p in p32s], compiler_params=_params(),
    )(chip_half, *p32s, *us)


def _ride_chip_exchange(ps):
    def copies(ins, outs, send_sems, recv_sems):
        x, y, c = lax.axis_index("x"), lax.axis_index("y"), lax.axis_index("c")
        chips = [(1 - x, y), (x, 1 - y), (1 - x, 1 - y)]
        cps = []
        for a in range(len(ps)):
            for j, (ox, oy) in enumerate(chips):
                cps.append(pltpu.make_async_remote_copy(src_ref=ins[a].at[2 * ox + oy], dst_ref=outs[a].at[j],
                                                        send_sem=send_sems.at[3 * a + j],
                                                        recv_sem=recv_sems.at[3 * a + j],
                                                        device_id=(ox, oy, c), device_id_type=MESH))
        return cps

    return _Ride(ps, [_sds((3,) + p.shape[1:], p.dtype) for p in ps], 3 * len(ps), copies)


def _chip_sum(p32, u, chip_half, name):
    r2, cols = p32.shape
    tr = _row_tile(r2, cols)
    nt = r2 // tr

    def body(ch_ref, p_ref, u_ref, o_ref):
        acc = p_ref[...]
        for j in range(3):
            acc = acc + u_ref[j].astype(F32)
        o_ref[...] = acc

    return pl.pallas_call(
        body, name=name,
        grid_spec=pltpu.PrefetchScalarGridSpec(
            num_scalar_prefetch=1, grid=(nt,),
            in_specs=[pl.BlockSpec((tr, cols), lambda i, ch_ref: (i, 0)),
                      pl.BlockSpec((3, tr, cols), lambda i, ch_ref: (0, i, 0))],
            out_specs=pl.BlockSpec((tr, cols), lambda i, ch_ref: (ch_ref[1] * nt + i, 0))),
        out_shape=_sds((2 * r2, cols)), compiler_params=_params("parallel"),
    )(chip_half, p32, u)


def _sibling_join(fs, name):
    nb = len(fs)

    def body(*refs):
        outs = refs[nb:2 * nb]
        send_sems, recv_sems = refs[2 * nb:]
        x, y, c = lax.axis_index("x"), lax.axis_index("y"), lax.axis_index("c")
        cps = []
        for a in range(nb):
            r2 = outs[a].shape[0] // 2
            rows = outs[a].at[pl.ds(c * r2, r2), :]
            cps.append(pltpu.make_async_remote_copy(src_ref=rows, dst_ref=rows, send_sem=send_sems.at[a],
                                                    recv_sem=recv_sems.at[a], device_id=(x, y, 1 - c),
                                                    device_id_type=MESH))
        for cp in cps:
            cp.start()
        for cp in cps:
            cp.wait()

    return pl.pallas_call(
        body, name=name, in_specs=[ANY] * nb, out_specs=[ANY] * nb,
        out_shape=[_sds(f.shape, f.dtype) for f in fs],
        input_output_aliases={a: a for a in range(nb)},
        scratch_shapes=[_dma_sems(nb), _dma_sems(nb)],
    )(*fs)


def _row_tile(rows, cols):
    tile = rows
    while tile * cols * 4 > (1 << 20) and tile % 16 == 0:
        tile //= 2
    return tile


def _w_ada_grad(c_all, dmod_cols):
    def body(c_ref, d_ref, o_ref):
        o_ref[...] = _dot_tn(c_ref[...].astype(BF16), d_ref[...].astype(BF16))

    return pl.pallas_call(body, name="w_ada_grad", out_shape=_sds((c_all.shape[1], dmod_cols.shape[1])),
                          compiler_params=_params())(c_all, dmod_cols)


def _adam_math(w, g, m, v):
    nm = ADAM_B1 * m + (1.0 - ADAM_B1) * g
    nv = ADAM_B2 * v + (1.0 - ADAM_B2) * (g * g)
    m_hat = nm / (1.0 - ADAM_B1 ** ADAM_STEP)
    v_hat = nv / (1.0 - ADAM_B2 ** ADAM_STEP)
    return -ADAM_LR * (m_hat / (jnp.sqrt(v_hat) + ADAM_EPS) + ADAM_WD * w), nm, nv


def _adamw(w, g, m, v, name):
    rows, cols = w.shape
    tr = _row_tile(rows, cols)

    def body(w_ref, g_ref, m_ref, v_ref, go_ref, d_ref, nm_ref, nv_ref):
        gv = g_ref[...]
        go_ref[...] = gv
        d_ref[...], nm_ref[...], nv_ref[...] = _adam_math(w_ref[...], gv, m_ref[...], v_ref[...])

    spec = pl.BlockSpec((tr, cols), lambda i: (i, 0))
    return pl.pallas_call(
        body, name=name, grid=(rows // tr,), in_specs=[spec] * 4, out_specs=[spec] * 4,
        out_shape=[_sds((rows, cols))] * 4, compiler_params=_params("parallel"),
    )(w, g, m, v)


def _pack_small(dmod, dnorm_g, dfinal_g, dpool_scale, drel_bias, loss, dpool_w):
    return jnp.concatenate([dmod.reshape(-1, 128), dnorm_g.reshape(-1, 128), dfinal_g.reshape(-1, 128),
                            jnp.pad(dpool_scale.reshape(-1, 128), ((0, PK_RELB - PK_PSCALE - AW // 128), (0, 0))),
                            jnp.pad(drel_bias, ((0, 0), (0, 128 - NG * NH))),
                            jnp.full((PK_POOLW - PK_LOSS, 128), loss, F32), dpool_w.reshape(-1, 128)], axis=0)


def _small_update(small_all, ws, ms, vs):
    lane_rows = [(r0, r0 + w.shape[1] // 128) for r0, w in zip((PK_BADA, PK_NORMG, PK_FINALG, PK_PSCALE), ws)]
    nw = len(ws)

    def body(all_ref, *refs):
        w_refs, m_refs, v_refs = refs[:nw], refs[nw:2 * nw], refs[2 * nw:3 * nw]
        loss_ref, outs = refs[3 * nw], refs[3 * nw + 1:]
        g = all_ref[0]
        for s in range(1, all_ref.shape[0]):
            g = g + all_ref[s]
        loss_ref[...] = jnp.broadcast_to(g[PK_LOSS:PK_LOSS + 1, :], loss_ref.shape)

        def put(p, at, gv):
            d, nm, nv = _adam_math(w_refs[p][at], gv, m_refs[p][at], v_refs[p][at])
            for o_ref, val in zip(outs[4 * p:4 * p + 4], (gv, d, nm, nv)):
                o_ref[at] = val

        for p, (r0, r1) in enumerate(lane_rows):
            for i in range(r1 - r0):
                put(p, (slice(None), slice(128 * i, 128 * (i + 1))), g[r0 + i:r0 + i + 1, :])
        put(4, (slice(None), slice(None)), g[PK_RELB:PK_LOSS, 0:NG * NH])
        put(5, (slice(None), slice(None)), g[PK_POOLW:PK_ROWS, :])

    res = pl.pallas_call(
        body, name="small_update",
        out_shape=[_sds((8, 128))] + [_sds(w.shape) for w in ws for _ in range(4)], compiler_params=_params(),
    )(small_all, *ws, *ms, *vs)
    return res[0], [res[1 + 4 * p:5 + 4 * p] for p in range(nw)]


def kernel(x, c, norm_g, w_ada, b_ada, w_in, pool_w, pool_scale, w_attn_br, w_pool_br, w_out, rel_bias, final_g, loss_target, m_norm_g, m_w_ada, m_b_ada, m_w_in, m_pool_w, m_pool_scale, m_w_attn_br, m_w_pool_br, m_w_out, m_rel_bias, m_final_g, v_norm_g, v_w_ada, v_b_ada, v_w_in, v_pool_w, v_pool_scale, v_w_attn_br, v_w_pool_br, v_w_out, v_rel_bias, v_final_g):
    ix, iy, ic = lax.axis_index("x"), lax.axis_index("y"), lax.axis_index("c")
    dev = 4 * ix + 2 * iy + ic
    chip = 2 * ix + iy

    def half(w):
        r2 = w.shape[0] // 2
        return lax.dynamic_slice_in_dim(w, ic * r2, r2, axis=0).astype(BF16)

    gathered = _allgather8([jnp.broadcast_to(c, (8, D)), half(w_in[0]), half(w_attn_br[0]), half(w_pool_br[0]),
                            half(w_out[0])], "gather_weights", relay=[False, True, True, True, True])
    c_all = gathered[0][:, 0, :]
    wg_in = gathered[1].reshape(N_SHARD, D, IN_W // N_SHARD)
    wab = gathered[2].reshape(N_SHARD, AW, D // N_SHARD).transpose(1, 0, 2).reshape(AW, D)
    wpb = gathered[3].reshape(N_SHARD, AW, D // N_SHARD).transpose(1, 0, 2).reshape(AW, D)
    wout = gathered[4].reshape(D, D)

    mw = 3 * D // N_SHARD
    modp = _mod_partial(c_all, w_ada[0], lax.dynamic_slice_in_dim(b_ada, chip * mw, mw, axis=1))
    mod_all = _allgather8([modp], "gather_mod")[0]
    mod_full = mod_all[::2].transpose(1, 0, 2).reshape(8, 3 * D)
    mod = lax.dynamic_slice_in_dim(mod_full, dev, 1, axis=0)

    chip_half = jnp.stack([chip, ic]).astype(jnp.int32)
    r = _local_step(x[0], loss_target[0], mod, wg_in, wab, wpb, wout, pool_w[0], pool_scale, rel_bias, norm_g,
                    final_g.reshape(1, D), chip_half)

    packed = _pack_small(r["dmod"], r["dnorm_g"], r["dfinal_g"], r["dpool_scale"], r["drel_bias"], r["loss"],
                         r["dpool_w"])
    small_all = _allgather8([packed], "gather_small")[0]
    small = ["b_ada", "norm_g", "final_g", "pool_scale", "rel_bias", "pool_w"]
    shaped = lambda b, n, f, ps, rb, pw: [b, n, f.reshape(1, D), ps, rb, pw.reshape(4 * PGW, PGW)]
    loss, small_out = _small_update(small_all, shaped(b_ada, norm_g, final_g, pool_scale, rel_bias, pool_w),
                                    shaped(m_b_ada, m_norm_g, m_final_g, m_pool_scale, m_rel_bias, m_pool_w),
                                    shaped(v_b_ada, v_norm_g, v_final_g, v_pool_scale, v_rel_bias, v_pool_w))
    dmod_all = small_all[:, PK_BADA:PK_NORMG, :].reshape(8, 3 * D)
    g_w_ada = _w_ada_grad(c_all, lax.dynamic_slice_in_dim(dmod_all, chip * mw, mw, axis=1))

    g_w_in, g_w_ab, g_w_pb, g_w_out = _sibling_join([r["rs_in"], r["rs_attn_br"], r["rs_pool_br"], r["rs_out"]],
                                                    "rs_sibling_join")
    upd = dict(zip(small, small_out))
    upd["final_g"] = [a.reshape(D) for a in upd["final_g"]]
    upd["pool_w"] = [a.reshape(1, 4, PGW, PGW) for a in upd["pool_w"]]
    for nme, w, g, m, v in (("w_ada", w_ada, g_w_ada, m_w_ada, v_w_ada), ("w_in", w_in, g_w_in, m_w_in, v_w_in),
                            ("w_attn_br", w_attn_br, g_w_ab, m_w_attn_br, v_w_attn_br),
                            ("w_pool_br", w_pool_br, g_w_pb, m_w_pool_br, v_w_pool_br),
                            ("w_out", w_out, g_w_out, m_w_out, v_w_out)):
        upd[nme] = [a[None] for a in _adamw(w[0], g, m[0], v[0], "adamw_" + nme)]
    names = ["norm_g", "w_ada", "b_ada", "w_in", "pool_w", "pool_scale", "w_attn_br", "w_pool_br", "w_out",
             "rel_bias", "final_g"]
    return (loss[0, 0], r["grad_x"][None]) + tuple(upd[nme][kind] for kind in range(4) for nme in names)
```

```python
import math

import numpy as np
import jax
import jax.numpy as jnp
from jax import lax
from jax.experimental import pallas as pl
from jax.experimental.pallas import tpu as pltpu

F32 = jnp.float32
BF16 = jnp.bfloat16

D = 1024
HD = 64
NH = 8
AW = NH * HD
GROUPS = ((128, 1), (512, 4), (2048, 16))
NG = len(GROUPS)
BLK = 128
GW = 3 * AW
QKV_W = NG * GW
REST_W = 3584
IN_W = QKV_W + REST_W
CB = 512
NCB_QKV = QKV_W // CB
POOL_WINDOWS = (2, 4, 8, 16)
PGW = 128
HALO = 16
NUM_BUCKETS = 32
MAX_DISTANCE = 2048
EPS = 1e-6
NEG = -1e30
N_SHARD = 4
VMEM_LIMIT = 56 * 1024 * 1024

ADAM_LR = 0.001
ADAM_B1 = 0.9
ADAM_B2 = 0.999
ADAM_EPS = 1e-08
ADAM_WD = 0.01
ADAM_STEP = 10

PK_BADA, PK_NORMG, PK_FINALG, PK_PSCALE, PK_RELB, PK_LOSS, PK_POOLW, PK_ROWS = 0, 24, 32, 40, 48, 80, 88, 600

ANY = pl.BlockSpec(memory_space=pl.ANY)
MESH = pl.DeviceIdType.MESH


def _params(*sem):
    return pltpu.CompilerParams(dimension_semantics=sem, vmem_limit_bytes=VMEM_LIMIT)


def _sds(shape, dtype=F32):
    return jax.ShapeDtypeStruct(shape, dtype)


def _dot(a, b):
    return jnp.dot(a, b, preferred_element_type=F32)


def _dot_nt(a, b):
    return lax.dot_general(a, b, (((1,), (1,)), ((), ())), preferred_element_type=F32)


def _dot_tn(a, b):
    return lax.dot_general(a, b, (((0,), (0,)), ((), ())), preferred_element_type=F32)


def _sigmoid(z):
    return 0.5 * jnp.tanh(0.5 * z) + 0.5


def _dma_sems(*shape):
    return pltpu.SemaphoreType.DMA(shape)


class _Ride:
    def __init__(self, arrays, out_shapes, n_copies, copies):
        self.arrays, self.out_shapes, self.n_copies, self.copies = list(arrays), list(out_shapes), n_copies, copies


def _call_with_ride(body, ride, first, last, *, in_specs, out_specs, out_shape, scratch_shapes=(), **kw):
    in_specs, out_specs, out_shape, scratch_shapes = list(in_specs), list(out_specs), list(out_shape), list(scratch_shapes)
    n_in, n_out, n_sc = len(in_specs), len(out_specs), len(scratch_shapes)
    if ride is None:
        def run_plain(*operands):
            return pl.pallas_call(body, in_specs=in_specs, out_specs=out_specs, out_shape=out_shape,
                                  scratch_shapes=scratch_shapes, **kw)(*operands), []
        return run_plain
    n_ri, n_ro = len(ride.arrays), len(ride.out_shapes)

    def wrapped(*refs):
        ins, rest = refs[:n_in], refs[n_in:]
        r_ins, rest = rest[:n_ri], rest[n_ri:]
        outs, rest = rest[:n_out], rest[n_out:]
        r_outs, rest = rest[:n_ro], rest[n_ro:]
        scratch, (send_sems, recv_sems) = rest[:n_sc], rest[n_sc:]

        @pl.when(first())
        def _():
            for cp in ride.copies(r_ins, r_outs, send_sems, recv_sems):
                cp.start()

        body(*ins, *outs, *scratch)

        @pl.when(last())
        def _():
            for cp in ride.copies(r_ins, r_outs, send_sems, recv_sems):
                cp.wait()

    def run(*operands):
        res = pl.pallas_call(
            wrapped, in_specs=in_specs + [ANY] * n_ri, out_specs=out_specs + [ANY] * n_ro,
            out_shape=out_shape + ride.out_shapes,
            scratch_shapes=scratch_shapes + [_dma_sems(ride.n_copies), _dma_sems(ride.n_copies)], **kw,
        )(*operands, *ride.arrays)
        return res[:n_out], res[n_out:]
    return run


def _bucket_tables():
    i = np.arange(BLK)[:, None]
    j = np.arange(2 * BLK)[None, :]
    dist = BLK + i - j
    valid = (dist >= 0) & (dist <= BLK)
    tabs = []
    for _, dil in GROUPS:
        n = (np.clip(dist, 0, BLK) * dil).astype(np.int32)
        max_exact = NUM_BUCKETS // 2
        nf = np.maximum(n, 1).astype(np.float32)
        large = max_exact + (np.log(nf / np.float32(max_exact)) / np.float32(math.log(MAX_DISTANCE / max_exact))
                             * np.float32(NUM_BUCKETS - max_exact)).astype(np.int32)
        large = np.minimum(large, NUM_BUCKETS - 1)
        bucket = np.where(n < max_exact, n, large)
        tab = np.where(valid, bucket, -1).astype(np.int32)
        perm = _block_perm(dil)
        tabs.append(tab[perm][:, np.concatenate([perm, BLK + perm])])
    return np.stack(tabs)


def _bias_table(rel_bias, buckets):
    def body(rb_ref, bk_ref, out_ref):
        g = pl.program_id(0)
        bk = bk_ref[...]
        for h in range(NH):
            acc = jnp.full((BLK, 2 * BLK), NEG, F32)
            for b in range(NUM_BUCKETS):
                acc = jnp.where(bk == b, rb_ref[b, g * NH + h], acc)
            out_ref[h] = acc

    return pl.pallas_call(
        body, name="bias_table", grid=(NG,),
        in_specs=[pl.BlockSpec(memory_space=pltpu.SMEM),
                  pl.BlockSpec((None, BLK, 2 * BLK), lambda g: (g, 0, 0))],
        out_specs=pl.BlockSpec((NH, BLK, 2 * BLK), lambda g: (g, 0, 0)),
        out_shape=_sds((NG * NH, BLK, 2 * BLK)),
        compiler_params=_params("arbitrary"),
    )(rel_bias, buckets)


def _bias_grad(ds_acc, buckets, ride):
    def body(acc_ref, bk_ref, out_ref):
        bk = bk_ref[...]
        acc = acc_ref[...]
        lane = lax.broadcasted_iota(jnp.int32, (8, 128), 1)
        out = jnp.zeros((8, 128), F32)
        for b in range(NUM_BUCKETS):
            val = jnp.sum(jnp.where(bk == b, acc, 0.0))
            out = jnp.where(lane == b, val, out)
        out_ref[...] = out

    (out,), rode = _call_with_ride(
        body, ride, lambda: pl.program_id(0) == 0, lambda: pl.program_id(0) == NG * NH - 1,
        name="bias_grad", grid=(NG * NH,),
        in_specs=[pl.BlockSpec((None, BLK, 2 * BLK), lambda gh: (gh, 0, 0)),
                  pl.BlockSpec((None, BLK, 2 * BLK), lambda gh: (gh // NH, 0, 0))],
        out_specs=[pl.BlockSpec((None, 8, 128), lambda gh: (gh, 0, 0))],
        out_shape=[_sds((NG * NH, 8, 128))],
        compiler_params=_params("arbitrary"),
    )(ds_acc, buckets)
    return out, rode


def _mod_partial(c_all, w_ada_s, b_ada_s):
    def body(c_ref, w_ref, b_ref, o_ref):
        o_ref[...] = _dot(c_ref[...].astype(BF16), w_ref[...].astype(BF16)) + b_ref[...]

    return pl.pallas_call(body, name="mod_partial", out_shape=_sds((8, w_ada_s.shape[1])),
                          compiler_params=_params())(c_all, w_ada_s, b_ada_s)


def _prenorm(x, norm_g, mod):
    S = x.shape[0]
    tm = 1024

    def body(x_ref, g_ref, mod_ref, h_ref):
        xv = x_ref[...]
        r = lax.rsqrt(jnp.mean(xv * xv, axis=-1, keepdims=True) + EPS)
        n1 = xv * r * g_ref[...]
        h_ref[...] = (n1 * (1.0 + mod_ref[:, D:2 * D]) + mod_ref[:, 0:D]).astype(BF16)

    return pl.pallas_call(
        body, name="prenorm", grid=(S // tm,),
        in_specs=[pl.BlockSpec((tm, D), lambda i: (i, 0)), pl.BlockSpec((1, D), lambda i: (0, 0)),
                  pl.BlockSpec((1, 3 * D), lambda i: (0, 0))],
        out_specs=pl.BlockSpec((tm, D), lambda i: (i, 0)),
        out_shape=_sds((S, D), BF16), compiler_params=_params("parallel"),
    )(x, norm_g, mod)


def _proj(h, wg_in, j0, nj, dtype, name):
    S = h.shape[0]
    tm = 2048
    per = wg_in.shape[2] // CB

    def body(h_ref, w_ref, o_ref):
        o_ref[...] = _dot(h_ref[...], w_ref[...]).astype(dtype)

    return pl.pallas_call(
        body, name=name, grid=(S // tm, nj),
        in_specs=[pl.BlockSpec((tm, D), lambda m, j: (m, 0)),
                  pl.BlockSpec((None, D, CB), lambda m, j: ((j0 + j) // per, 0, (j0 + j) % per))],
        out_specs=pl.BlockSpec((tm, CB), lambda m, j: (m, j)),
        out_shape=_sds((S, nj * CB), dtype), compiler_params=_params("parallel", "parallel"),
    )(h, wg_in)


HS = 4
SLAB = HS * HD


def _lane_head(rows):
    return lax.broadcasted_iota(jnp.int32, (rows, SLAB), 1) // HD


def _head_stack(a):
    head = _lane_head(a.shape[0])
    return jnp.concatenate([jnp.where(head == h, a, jnp.zeros_like(a)) for h in range(HS)], axis=0)


def _head_unstack(a):
    rows = a.shape[0] // HS
    head = _lane_head(rows)
    out = a[:rows]
    for h in range(1, HS):
        out = jnp.where(head == h, a[h * rows:(h + 1) * rows], out)
    return out


STAT_W = 128
VIEW = 16


def _sub_layout(dil):
    if dil == 1:
        return BLK, [None]
    return BLK * dil // VIEW, [[r + dil * u for u in range(VIEW // dil)] for r in range(dil)]


def _block_perm(dil):
    a_rows, _ = _sub_layout(dil)
    p = np.arange(BLK)
    return p if dil == 1 else (VIEW // dil) * (p % a_rows) + p // a_rows


LB = 128
N_SLAB = NH // HS


RBS = 4


def _ld(refs, bs, s, w, rb=0):
    if bs is None:
        return refs[0][rb * BLK:(rb + 1) * BLK, s * w:(s + 1) * w]
    a_rows = refs[0].shape[0] // VIEW
    return jnp.concatenate([jnp.concatenate([ref[pl.ds(b, a_rows, stride=VIEW), :] for b in bs], axis=0)
                            for ref in refs[s * (w // LB):(s + 1) * (w // LB)]], axis=1)


def _st(ref, bs, s, val, rb=0):
    if bs is None:
        ref[rb * BLK:(rb + 1) * BLK, s * SLAB:(s + 1) * SLAB] = val.astype(ref.dtype)
        return
    a_rows = val.shape[0] // len(bs)
    for u, b in enumerate(bs):
        ref[:, b, s * SLAB:(s + 1) * SLAB] = val[u * a_rows:(u + 1) * a_rows]


def _attn_views(dil, S):
    a_rows, subs = _sub_layout(dil)
    if dil == 1:
        def ispecs(base, w, f):
            return [pl.BlockSpec((RBS * BLK, N_SLAB * w), lambda sg, n: (f(n), base // (N_SLAB * w)))]
        return subs, S // (RBS * BLK), N_SLAB, RBS, ispecs, (lambda w: (S, w)), (
            lambda f: pl.BlockSpec((RBS * BLK, AW), lambda sg, n: (f(n), 0)))

    sps = N_SLAB if dil < VIEW else 1

    def ispecs(base, w, f):
        return [pl.BlockSpec((a_rows * VIEW, LB), lambda sg, n, k=k: (f(n), (base + sg * sps * w) // LB + k))
                for k in range(sps * w // LB)]
    return subs, S // (a_rows * VIEW), sps, 1, ispecs, (lambda w: (S // VIEW, VIEW, w)), (
        lambda f: pl.BlockSpec((a_rows, VIEW, sps * SLAB), lambda sg, n: (f(n), 0, sg)))


def _attn_fwd(qkv_g, bias_tab, g):
    S = qkv_g.shape[0]
    subs, nbq, sps, rbs, ispecs, shape, ospec = _attn_views(GROUPS[g][1], S)
    cur = lambda n: n
    in_specs = [ispecs(0, SLAB, cur), ispecs(AW, SLAB, cur), ispecs(2 * AW, SLAB, cur)]
    nl = len(in_specs[0])

    def body(*refs):
        q, k, v = (refs[t * nl:(t + 1) * nl] for t in range(3))
        b_ref, o_ref, l_ref, kprev, vprev = refs[3 * nl:]
        n = pl.program_id(1)

        @pl.when(n == 0)
        def _():
            kprev[...] = jnp.zeros_like(kprev)
            vprev[...] = jnp.zeros_like(vprev)

        col = lax.broadcasted_iota(jnp.int32, (HS * BLK, 2 * BLK), 1)
        first = (col >= BLK) | (n > 0)
        for s_, rb, (i, bs) in ((s_, rb, sub) for s_ in range(sps) for rb in range(rbs) for sub in enumerate(subs)):
            cs = slice(s_ * SLAB, (s_ + 1) * SLAB)
            kc, vc = _ld(k, bs, s_, SLAB, rb).astype(BF16), _ld(v, bs, s_, SLAB, rb).astype(BF16)
            kb = jnp.concatenate([kprev[i, :, cs], kc], axis=0)
            vb = jnp.concatenate([vprev[i, :, cs], vc], axis=0)
            kprev[i, :, cs], vprev[i, :, cs] = kc, vc
            s = _dot_nt(_head_stack(_ld(q, bs, s_, SLAB, rb).astype(BF16)), kb) * (HD ** -0.5)
            s = s + b_ref[pl.ds(s_ * HS, HS)].reshape(HS * BLK, 2 * BLK)
            if rb == 0:
                s = jnp.where(first, s, NEG)
            m = jnp.max(s, axis=-1, keepdims=True)
            p = jnp.exp(s - m)
            den = jnp.sum(p, axis=-1, keepdims=True)
            _st(o_ref, bs, s_, _head_unstack(_dot(p.astype(BF16), vb) / den), rb)
            _st(l_ref, bs, s_, _head_unstack(jnp.broadcast_to(m + jnp.log(den), (HS * BLK, SLAB))), rb)

    out = _sds(shape(AW))
    nsg = N_SLAB // sps
    o, l = pl.pallas_call(
        body, name=f"attn_fwd{g}", grid=(nsg, nbq),
        in_specs=sum(in_specs, []) + [pl.BlockSpec((sps * HS, BLK, 2 * BLK), lambda sg, n: (g * nsg + sg, 0, 0))],
        out_specs=[ospec(cur), ospec(cur)],
        out_shape=[out, out],
        scratch_shapes=[pltpu.VMEM((len(subs), BLK, sps * SLAB), BF16)] * 2,
        compiler_params=_params("parallel", "arbitrary"),
    )(*([qkv_g] * (3 * nl)), bias_tab)
    return o.reshape(S, AW), l.reshape(S, AW)


def _attn_bwd(qkv_g, dattn, stats, bias_tab, g, ride):
    S = qkv_g.shape[0]
    subs, nbq, sps, rbs, ispecs, shape, ospec = _attn_views(GROUPS[g][1], S)
    cur = lambda n: jnp.minimum(n, nbq - 1)
    late = lambda n: jnp.maximum(n - 1, 0)
    in_specs = [ispecs(0, SLAB, cur), ispecs(AW, SLAB, cur), ispecs(2 * AW, SLAB, cur), ispecs(0, SLAB, cur),
                ispecs(0, STAT_W, cur)]
    nl = len(in_specs[0])

    def body(*refs):
        q, k, v, da = (refs[t * nl:(t + 1) * nl] for t in range(4))
        nst = len(in_specs[4])
        st_refs = refs[4 * nl:4 * nl + nst]
        b_ref, dq_ref, dk_ref, dv_ref, ds_ref, ck_ref, cv_ref, kprev, vprev, *held = refs[4 * nl + nst:]
        n = pl.program_id(1)

        @pl.when(n == 0)
        def _():
            for ref in (ds_ref, ck_ref, cv_ref, kprev, vprev, *held):
                ref[...] = jnp.zeros_like(ref)

        def finish(ref, t, bs, s_, rb, val):
            cs = slice(s_ * SLAB, (s_ + 1) * SLAB)
            if rbs == 1:
                _st(ref, bs, s_, val)
            elif rb == 0:
                for j in range(rbs - 1):
                    _st(ref, bs, s_, held[t][j * BLK:(j + 1) * BLK, cs], j)
                _st(ref, bs, s_, val, rbs - 1)
            else:
                held[t][(rb - 1) * BLK:rb * BLK, cs] = val

        @pl.when(n < nbq)
        def _():
            col = lax.broadcasted_iota(jnp.int32, (HS * BLK, 2 * BLK), 1)
            first = (col >= BLK) | (n > 0)
            for s_, rb, (i, bs) in ((s_, rb, sub) for s_ in range(sps) for rb in range(rbs) for sub in enumerate(subs)):
                cs = slice(s_ * SLAB, (s_ + 1) * SLAB)
                st = _ld(st_refs, bs, s_, STAT_W, rb)
                kc, vc = _ld(k, bs, s_, SLAB, rb).astype(BF16), _ld(v, bs, s_, SLAB, rb).astype(BF16)
                kb = jnp.concatenate([kprev[i, :, cs], kc], axis=0)
                vb = jnp.concatenate([vprev[i, :, cs], vc], axis=0)
                kprev[i, :, cs], vprev[i, :, cs] = kc, vc
                lse = jnp.concatenate([st[:, h:h + 1] for h in range(HS)], axis=0)
                delta = jnp.concatenate([st[:, HS + h:HS + h + 1] for h in range(HS)], axis=0)
                qs = _head_stack(_ld(q, bs, s_, SLAB, rb).astype(BF16))
                dos = _head_stack(_ld(da, bs, s_, SLAB, rb).astype(BF16))
                s = _dot_nt(qs, kb) * (HD ** -0.5) + b_ref[pl.ds(s_ * HS, HS)].reshape(HS * BLK, 2 * BLK)
                if rb == 0:
                    s = jnp.where(first, s, NEG)
                p = jnp.exp(s - lse)
                ds = p * (_dot_nt(dos, vb) - delta)
                ds_ref[pl.ds(s_ * HS, HS)] += ds.reshape(HS, BLK, 2 * BLK)
                ds_b = (ds * (HD ** -0.5)).astype(BF16)
                _st(dq_ref, bs, s_, _head_unstack(_dot(ds_b, kb)), rb)
                dkb = _dot_tn(ds_b, qs)
                dvb = _dot_tn(p.astype(BF16), dos)
                finish(dk_ref, 0, bs, s_, rb, ck_ref[i, :, cs] + dkb[:BLK])
                finish(dv_ref, 1, bs, s_, rb, cv_ref[i, :, cs] + dvb[:BLK])
                ck_ref[i, :, cs] = dkb[BLK:]
                cv_ref[i, :, cs] = dvb[BLK:]

        @pl.when(n == nbq)
        def _():
            for s_ in range(sps):
                for i, bs in enumerate(subs):
                    finish(dk_ref, 0, bs, s_, 0, ck_ref[i, :, s_ * SLAB:(s_ + 1) * SLAB])
                    finish(dv_ref, 1, bs, s_, 0, cv_ref[i, :, s_ * SLAB:(s_ + 1) * SLAB])

    out = _sds(shape(AW), BF16 if GROUPS[g][1] == 1 else F32)
    nsg = N_SLAB // sps
    (dq, dk, dv, ds_acc), rode = _call_with_ride(
        body, ride, lambda: (pl.program_id(0) == 0) & (pl.program_id(1) == 0),
        lambda: (pl.program_id(0) == nsg - 1) & (pl.program_id(1) == nbq),
        name=f"attn_bwd{g}", grid=(nsg, nbq + 1),
        in_specs=sum(in_specs, []) + [pl.BlockSpec((sps * HS, BLK, 2 * BLK), lambda sg, n: (g * nsg + sg, 0, 0))],
        out_specs=[ospec(cur), ospec(late), ospec(late),
                   pl.BlockSpec((sps * HS, BLK, 2 * BLK), lambda sg, n: (sg, 0, 0))],
        out_shape=[out] * 3 + [_sds((NH, BLK, 2 * BLK))],
        scratch_shapes=[pltpu.VMEM((len(subs), BLK, sps * SLAB), F32)] * 2
        + [pltpu.VMEM((len(subs), BLK, sps * SLAB), BF16)] * 2 + [pltpu.VMEM(((rbs - 1) * BLK, sps * SLAB), F32)] * (2 if rbs > 1 else 0),
        compiler_params=_params("arbitrary", "arbitrary"),
    )(*([qkv_g] * (3 * nl)), *([dattn] * nl), *([stats] * len(in_specs[4])), bias_tab)
    return [dq.reshape(S, AW), dk.reshape(S, AW), dv.reshape(S, AW)], ds_acc, rode


TM_MIX = 256


def _mix_specs(tm):
    row512 = pl.BlockSpec((tm, AW), lambda i: (i, 0))
    return ([row512] * 6 + [
        pl.BlockSpec((tm, REST_W), lambda i: (i, 0)),
        pl.BlockSpec((HALO, AW), lambda i: (jnp.maximum(i * (tm // HALO) - 1, 0), 1)),
        pl.BlockSpec((AW, D), lambda i: (0, 0)), pl.BlockSpec((AW, D), lambda i: (0, 0)),
        pl.BlockSpec((4, PGW, PGW), lambda i: (0, 0, 0)), pl.BlockSpec((1, AW), lambda i: (0, 0))])


def _mix_forward(i, tm, o_refs, l_refs, rest_ref, halo_ref, wab_ref, wpb_ref, pw_ref, ps_ref):
    l0, l1, l2 = (r[...] for r in l_refs)
    mx = jnp.maximum(jnp.maximum(l0, l1), l2)
    e0, e1, e2 = jnp.exp(l0 - mx), jnp.exp(l1 - mx), jnp.exp(l2 - mx)
    den = e0 + e1 + e2
    lj = mx + jnp.log(den)
    attn = (e0 * o_refs[0][...] + e1 * o_refs[1][...] + e2 * o_refs[2][...]) / den

    z_attn = rest_ref[:, 0:AW]
    u = rest_ref[:, AW:2 * AW]
    z_pool = rest_ref[:, 2 * AW:3 * AW]
    g_attn = rest_ref[:, 3 * AW:3 * AW + D]
    g_pool = rest_ref[:, 3 * AW + D:3 * AW + 2 * D]

    sg_a = _sigmoid(z_attn)
    sil_a = z_attn * sg_a
    a_g = (attn * sil_a).astype(BF16)
    y_attn = _dot(a_g, wab_ref[...])

    halo = jnp.where(i > 0, halo_ref[...], 0.0)
    ext = jnp.concatenate([halo, u], axis=0)
    t = i * tm + lax.broadcasted_iota(jnp.int32, (tm, 1), 0)
    pooled, mixed_raw = [], []
    for gi, win in enumerate(POOL_WINDOWS):
        s = ext[:, gi * PGW:(gi + 1) * PGW]
        sh = 1
        while sh < win:
            s = s + pltpu.roll(s, sh, 0)
            sh *= 2
        cnt = jnp.minimum(t + 1, win).astype(F32)
        pg = s[HALO:] / cnt - u[:, gi * PGW:(gi + 1) * PGW]
        pooled.append(pg.astype(BF16))
        mixed_raw.append(_dot(pooled[-1], pw_ref[gi].astype(BF16)))
    mixed_raw = jnp.concatenate(mixed_raw, axis=1)
    mixed = mixed_raw * ps_ref[...]
    sg_p = _sigmoid(z_pool)
    sil_p = z_pool * sg_p
    m_g = (mixed * sil_p).astype(BF16)
    y_pool = _dot(m_g, wpb_ref[...])

    sa = _sigmoid(g_attn)
    sp = _sigmoid(g_pool)
    merged = sa * y_attn + sp * y_pool
    return dict(lj=lj, attn=attn, z_attn=z_attn, z_pool=z_pool, sg_a=sg_a, sil_a=sil_a, a_g=a_g, y_attn=y_attn,
                pooled=pooled, mixed_raw=mixed_raw, mixed=mixed, sg_p=sg_p, sil_p=sil_p, m_g=m_g, y_pool=y_pool,
                sa=sa, sp=sp, merged=merged)


def _mix_step(x, target, os_, ls_, rest, wab, wpb, pool_w, pool_scale, wout, mod, final_g):
    S = x.shape[0]
    tm = TM_MIX
    nt = S // tm
    sw = D // N_SHARD

    def body(o0, o1, o2, l0, l1, l2, rest_ref, halo_ref, wab_ref, wpb_ref, pw_ref, ps_ref,
             x_ref, t_ref, wo_ref, mod_ref, fg_ref, dx2_ref, loss_ref, dfg_ref, dgate_ref,
             dattn_ref, stats_ref, dpooled_ref, dproj_hbm, dwo_hbm, dwab_hbm, dwpb_hbm, dpw_ref, dps_ref,
             awo, awab, awpb, stage, stage_sem):
        i = pl.program_id(0)
        slot = i % 2

        def staged(step, sl):
            return pltpu.make_async_copy(stage.at[sl], dproj_hbm.at[pl.ds(step * tm, tm), pl.ds(QKV_W, REST_W)],
                                         stage_sem.at[sl])

        @pl.when(i == 0)
        def _():
            for ref in (loss_ref, dfg_ref, dgate_ref, awo, awab, awpb, dpw_ref, dps_ref):
                ref[...] = jnp.zeros_like(ref)

        f = _mix_forward(i, tm, (o0, o1, o2), (l0, l1, l2), rest_ref, halo_ref, wab_ref, wpb_ref, pw_ref, ps_ref)
        mo = _dot(f["merged"].astype(BF16), wo_ref[...])
        gate = mod_ref[:, 2 * D:3 * D]
        fg = fg_ref[...]
        x2 = x_ref[...] + gate * mo
        r2 = lax.rsqrt(jnp.mean(x2 * x2, axis=-1, keepdims=True) + EPS)
        n2 = x2 * r2
        err = n2 * fg - t_ref[...]
        loss_ref[...] += 0.5 * jnp.sum(jnp.mean(err * err, axis=-1, keepdims=True))
        dy = err * (1.0 / D)
        dfg_ref[...] += jnp.sum(dy * n2, axis=0, keepdims=True)
        dn = dy * fg
        dx2 = r2 * (dn - n2 * jnp.mean(dn * n2, axis=-1, keepdims=True))
        dgate_ref[...] += jnp.sum(dx2 * mo, axis=0, keepdims=True)
        dx2_ref[...] = dx2

        dmo_b = (dx2 * gate).astype(BF16)
        dmerged = _dot_nt(dmo_b, wo_ref[...])
        awo[...] += _dot_tn(f["merged"].astype(BF16), dmo_b)
        sa, sp = f["sa"], f["sp"]
        dya = (dmerged * sa).astype(BF16)
        dyp = (dmerged * sp).astype(BF16)
        dg_attn = dmerged * f["y_attn"] * sa * (1.0 - sa)
        dg_pool = dmerged * f["y_pool"] * sp * (1.0 - sp)
        dag = _dot_nt(dya, wab_ref[...])
        awab[...] += _dot_tn(f["a_g"], dya)
        dmg = _dot_nt(dyp, wpb_ref[...])
        awpb[...] += _dot_tn(f["m_g"], dyp)
        dattn = dag * f["sil_a"]
        dattn_ref[...] = dattn
        prod = dattn * f["attn"]
        lane = lax.broadcasted_iota(jnp.int32, (tm, STAT_W), 1)
        for sb in range(N_SLAB):
            st = jnp.zeros((tm, STAT_W), F32)
            for h in range(HS):
                hs = slice((sb * HS + h) * HD, (sb * HS + h + 1) * HD)
                st = jnp.where(lane == h, f["lj"][:, hs.start:hs.start + 1], st)
                st = jnp.where(lane == HS + h, jnp.sum(prod[:, hs], axis=-1, keepdims=True), st)
            stats_ref[:, sb * STAT_W:(sb + 1) * STAT_W] = st
        dz_attn = dag * f["attn"] * (f["sg_a"] * (1.0 + f["z_attn"] * (1.0 - f["sg_a"])))
        dmixed = dmg * f["sil_p"]
        dz_pool = dmg * f["mixed"] * (f["sg_p"] * (1.0 + f["z_pool"] * (1.0 - f["sg_p"])))
        dps_ref[...] += jnp.sum(dmixed * f["mixed_raw"], axis=0, keepdims=True)
        dpm = (dmixed * ps_ref[...]).astype(BF16)
        for gi in range(len(POOL_WINDOWS)):
            cs = slice(gi * PGW, (gi + 1) * PGW)
            dpw_ref[gi] += _dot_tn(f["pooled"][gi], dpm[:, cs])
            dpooled_ref[:, cs] = _dot_nt(dpm[:, cs], pw_ref[gi].astype(BF16))
        @pl.when(i >= 2)
        def _():
            staged(i - 2, slot).wait()

        stage[slot, :, 0:AW] = dz_attn.astype(BF16)
        stage[slot, :, AW:2 * AW] = jnp.zeros((tm, AW), BF16)
        stage[slot, :, 2 * AW:3 * AW] = dz_pool.astype(BF16)
        stage[slot, :, 3 * AW:3 * AW + D] = dg_attn.astype(BF16)
        stage[slot, :, 3 * AW + D:3 * AW + 2 * D] = dg_pool.astype(BF16)
        staged(i, slot).start()

        @pl.when(i == nt - 1)
        def _():
            staged(i - 1, 1 - slot).wait()
            staged(i, slot).wait()
            pltpu.sync_copy(awo, dwo_hbm)
            for k in range(N_SHARD):
                pltpu.sync_copy(awab.at[:, pl.ds(k * sw, sw)], dwab_hbm.at[k])
                pltpu.sync_copy(awpb.at[:, pl.ds(k * sw, sw)], dwpb_hbm.at[k])

    row = pl.BlockSpec((tm, D), lambda i: (i, 0))
    vec = pl.BlockSpec((1, D), lambda i: (0, 0))
    row512 = pl.BlockSpec((tm, AW), lambda i: (i, 0))
    outs = pl.pallas_call(
        body, name="mix_step", grid=(nt,),
        in_specs=_mix_specs(tm) + [row, row, pl.BlockSpec((D, D), lambda i: (0, 0)),
                                   pl.BlockSpec((1, 3 * D), lambda i: (0, 0)), vec],
        out_specs=[row, pl.BlockSpec((8, 128), lambda i: (0, 0)), vec, vec,
                   row512, pl.BlockSpec((tm, N_SLAB * STAT_W), lambda i: (i, 0)), row512, ANY, ANY, ANY, ANY,
                   pl.BlockSpec((4, PGW, PGW), lambda i: (0, 0, 0)), pl.BlockSpec((1, AW), lambda i: (0, 0))],
        out_shape=[_sds((S, D)), _sds((8, 128)), _sds((1, D)), _sds((1, D)),
                   _sds((S, AW)), _sds((S, N_SLAB * STAT_W)), _sds((S, AW)), _sds((S, IN_W), BF16),
                   _sds((D, D)), _sds((N_SHARD, AW, sw)), _sds((N_SHARD, AW, sw)), _sds((4, PGW, PGW)), _sds((1, AW))],
        scratch_shapes=[pltpu.VMEM((D, D), F32), pltpu.VMEM((AW, D), F32), pltpu.VMEM((AW, D), F32),
                        pltpu.VMEM((2, tm, REST_W), BF16), _dma_sems(2)],
        compiler_params=_params("arbitrary"),
    )(*os_, *ls_, rest, rest, wab, wpb, pool_w, pool_scale, x, target, wout, mod, final_g)
    dx2, loss, dfg, dgate, dattn, stats, dpooled, dproj, dwo, dwab, dwpb, dpw, dps = outs
    return (dx2, loss, dfg, dgate, dattn, stats, dpooled, dproj, dwo.reshape(N_SHARD, D // N_SHARD, D), dwab, dwpb,
            dpw, dps)


def _pool_bwd(dpooled):
    S = dpooled.shape[0]
    tm = 1024
    nt = S // tm

    def body(dp_ref, nxt_ref, du_ref):
        i = pl.program_id(0)
        t = i * tm + lax.broadcasted_iota(jnp.int32, (tm + HALO, 1), 0)
        nxt = jnp.where(i < nt - 1, nxt_ref[...], 0.0)
        ext = jnp.concatenate([dp_ref[...], nxt], axis=0)
        for gi, win in enumerate(POOL_WINDOWS):
            cs = slice(gi * PGW, (gi + 1) * PGW)
            s = ext[:, cs] / jnp.minimum(t + 1, win).astype(F32)
            sh = 1
            while sh < win:
                s = s + pltpu.roll(s, tm + HALO - sh, 0)
                sh *= 2
            du_ref[:, cs] = (s[:tm] - dp_ref[:, cs]).astype(BF16)

    return pl.pallas_call(
        body, name="pool_bwd", grid=(nt,),
        in_specs=[pl.BlockSpec((tm, AW), lambda i: (i, 0)),
                  pl.BlockSpec((HALO, AW), lambda i: (jnp.minimum((i + 1) * (tm // HALO), S // HALO - 1), 0))],
        out_specs=pl.BlockSpec((tm, AW), lambda i: (i, 0)),
        out_shape=_sds((S, AW), BF16), compiler_params=_params("parallel"),
    )(dpooled, dpooled)


TB = 1024


def _dh(dproj, wg_in, ride):
    S = dproj.shape[0]
    per = wg_in.shape[2] // TB
    nm, nk = S // TB, IN_W // TB

    def body(dp_ref, w_ref, out_ref):
        @pl.when(pl.program_id(1) == 0)
        def _():
            out_ref[...] = jnp.zeros_like(out_ref)

        out_ref[...] += _dot_nt(dp_ref[...], w_ref[...])

    (dh,), rode = _call_with_ride(
        body, ride, lambda: (pl.program_id(0) == 0) & (pl.program_id(1) == 0),
        lambda: (pl.program_id(0) == nm - 1) & (pl.program_id(1) == nk - 1),
        name="dh", grid=(nm, nk),
        in_specs=[pl.BlockSpec((TB, TB), lambda m, kk: (m, kk)),
                  pl.BlockSpec((None, D, TB), lambda m, kk: (kk // per, 0, kk % per))],
        out_specs=[pl.BlockSpec((TB, D), lambda m, kk: (m, 0))],
        out_shape=[_sds((S, D))], compiler_params=_params("arbitrary", "arbitrary"),
    )(dproj, wg_in)
    return dh, rode


def _dw_in(h, dproj):
    S = dproj.shape[0]
    per = IN_W // N_SHARD // TB

    def body(h_ref, dp_ref, out_ref):
        out_ref[...] = _dot_tn(h_ref[...], dp_ref[...])

    return pl.pallas_call(
        body, name="dw_in", grid=(IN_W // TB,),
        in_specs=[pl.BlockSpec((S, D), lambda j: (0, 0)), pl.BlockSpec((S, TB), lambda j: (0, j))],
        out_specs=pl.BlockSpec((None, D, TB), lambda j: (j // per, 0, j % per)),
        out_shape=_sds((N_SHARD, D, IN_W // N_SHARD)), compiler_params=_params("parallel"),
    )(h, dproj)


def _prenorm_bwd(x, dh, dx2, norm_g, mod):
    S = x.shape[0]
    tm = 1024

    def body(x_ref, dh_ref, dx2_ref, g_ref, mod_ref, gx_ref, dg_ref, dshift_ref, dscale_ref):
        i = pl.program_id(0)

        @pl.when(i == 0)
        def _():
            dg_ref[...] = jnp.zeros_like(dg_ref)
            dshift_ref[...] = jnp.zeros_like(dshift_ref)
            dscale_ref[...] = jnp.zeros_like(dscale_ref)

        xv = x_ref[...]
        dhv = dh_ref[...]
        g = g_ref[...]
        r = lax.rsqrt(jnp.mean(xv * xv, axis=-1, keepdims=True) + EPS)
        xh = xv * r
        dshift_ref[...] += jnp.sum(dhv, axis=0, keepdims=True)
        dscale_ref[...] += jnp.sum(dhv * (xh * g), axis=0, keepdims=True)
        dn1 = dhv * (1.0 + mod_ref[:, D:2 * D])
        dg_ref[...] += jnp.sum(dn1 * xh, axis=0, keepdims=True)
        dxh = dn1 * g
        gx_ref[...] = dx2_ref[...] + r * (dxh - xh * jnp.mean(dxh * xh, axis=-1, keepdims=True))

    row = pl.BlockSpec((tm, D), lambda i: (i, 0))
    vec = pl.BlockSpec((1, D), lambda i: (0, 0))
    return pl.pallas_call(
        body, name="prenorm_bwd", grid=(S // tm,),
        in_specs=[row, row, row, vec, pl.BlockSpec((1, 3 * D), lambda i: (0, 0))],
        out_specs=[row, vec, vec, vec],
        out_shape=[_sds((S, D)), _sds((1, D)), _sds((1, D)), _sds((1, D))],
        compiler_params=_params("arbitrary"),
    )(x, dh, dx2, norm_g, mod)


def _local_step(x, target, mod, wg_in, wab, wpb, wout, pool_w, pool_scale, rel_bias, norm_g, final_g, chip_half):
    buckets = jnp.asarray(_bucket_tables())
    bias_tab = _bias_table(rel_bias, buckets)
    h = _prenorm(x, norm_g, mod)
    qkv = [_proj(h, wg_in, 3 * g, 3, BF16 if GROUPS[g][1] == 1 else F32, f"proj_qkv{g}") for g in range(NG)]
    rest = _proj(h, wg_in, NCB_QKV, REST_W // CB, F32, "proj_rest")
    os_, ls_ = zip(*[_attn_fwd(qkv[g], bias_tab, g) for g in range(NG)])
    (dx2, loss, dfinal_g, dgate, dattn, stats, dpooled, dproj, dw_out, dw_ab, dw_pb, dpool_w,
     dpool_scale) = _mix_step(x, target, os_, ls_, rest, wab, wpb, pool_w, pool_scale, wout, mod, final_g)
    du = _pool_bwd(dpooled)

    small = [dw_ab, dw_pb, dw_out]
    dqkv0, ds0, sib_small = _attn_bwd(qkv[0], dattn, stats, bias_tab, 0, _ride_sibling_halves(small))
    p_small = _pair_sum_small(small, sib_small, chip_half)
    dqkv1, ds1, u_small = _attn_bwd(qkv[1], dattn, stats, bias_tab, 1,
                                    _ride_chip_exchange([p16 for _, p16 in p_small]))
    rs_ab, rs_pb, rs_out = _chip_sum_small([p32 for p32, _ in p_small], u_small, chip_half)
    dqkv2, ds2, _ = _attn_bwd(qkv[2], dattn, stats, bias_tab, 2, None)

    for j, piece in enumerate(dqkv0 + dqkv1 + dqkv2):
        dproj = lax.dynamic_update_slice(dproj, piece.astype(BF16), (0, j * AW))
    dproj = lax.dynamic_update_slice(dproj, du, (0, QKV_W + AW))
    dw_in = _dw_in(h, dproj)
    drel_rows, (sib_in,) = _bias_grad(jnp.concatenate([ds0, ds1, ds2], axis=0), buckets,
                                      _ride_sibling_halves([dw_in]))
    drel = drel_rows[:, 0, :NUM_BUCKETS].T
    p32_in, p16_in = _pair_sum(dw_in, sib_in, chip_half, "rs_pair_sum_in")
    dh, (u_in,) = _dh(dproj, wg_in, _ride_chip_exchange([p16_in]))
    rs_in = _chip_sum(p32_in, u_in, chip_half, "rs_chip_sum_in")

    grad_x, dnorm_g, dshift, dscale = _prenorm_bwd(x, dh, dx2, norm_g, mod)
    dmod = jnp.concatenate([dshift, dscale, dgate], axis=1)
    return dict(loss=loss[0, 0], grad_x=grad_x, dmod=dmod, dnorm_g=dnorm_g, dfinal_g=dfinal_g, dpool_w=dpool_w,
                dpool_scale=dpool_scale, drel_bias=drel, dw_in=dw_in, dw_attn_br=dw_ab, dw_pool_br=dw_pb,
                dw_out=dw_out, rs_in=rs_in, rs_attn_br=rs_ab, rs_pool_br=rs_pb, rs_out=rs_out)


def _allgather8(blocks, name, relay=None):
    nb = len(blocks)
    relay = [False] * nb if relay is None else list(relay)

    def body(*refs):
        ins, outs = refs[:nb], refs[nb:2 * nb]
        send_sems, recv_sems = refs[2 * nb:]
        x, y, c = lax.axis_index("x"), lax.axis_index("y"), lax.axis_index("c")
        me, sibling = (x, y, c), (x, y, 1 - c)
        here, xn, yn, dg = (x, y), (1 - x, y), (x, 1 - y), (1 - x, 1 - y)

        def slot(a, chip, core, half=None):
            ref = outs[a].at[4 * chip[0] + 2 * chip[1] + core]
            if half is None:
                return ref
            r2 = ref.shape[0] // 2
            return ref.at[pl.ds(half * r2, r2)]

        def copy(a, k, dst, to, src=None):
            return pltpu.make_async_remote_copy(src_ref=dst if src is None else src, dst_ref=dst,
                                                send_sem=send_sems.at[a, k], recv_sem=recv_sems.at[a, k],
                                                device_id=to, device_id_type=MESH)

        def start(cps):
            for cp in cps:
                cp.start()
            return cps

        sent = []
        for a in range(nb):
            own = slot(a, here, c)
            sent += [copy(a, 0, own, sibling, src=ins[a]), copy(a, 1, own, (*xn, c), src=ins[a]),
                     copy(a, 2, own, (*yn, c), src=ins[a])]
            if not relay[a]:
                sent.append(copy(a, 3, own, (*dg, c), src=ins[a]))
        start(sent)
        for a in range(nb):
            copy(a, 2, slot(a, yn, c), me).wait_recv()
            sent += start([copy(a, 6, slot(a, yn, c), sibling)]
                          + ([copy(a, 3, slot(a, yn, c, 0), (*xn, c))] if relay[a] else []))
        for a in range(nb):
            copy(a, 1, slot(a, xn, c), me).wait_recv()
            sent += start([copy(a, 5, slot(a, xn, c), sibling)]
                          + ([copy(a, 4, slot(a, xn, c, 1), (*yn, c))] if relay[a] else []))
        for a in range(nb):
            for k, half in ((3, 0), (4, 1)) if relay[a] else ((3, None),):
                copy(a, k, slot(a, dg, c, half), me).wait_recv()
                sent += start([copy(a, 4 + k, slot(a, dg, c, half), sibling)])
        for a in range(nb):
            copy(a, 0, slot(a, here, 1 - c), me).wait_recv()
            copy(a, 5, slot(a, xn, 1 - c), me).wait_recv()
            copy(a, 6, slot(a, yn, 1 - c), me).wait_recv()
            for k, half in ((7, 0), (8, 1)) if relay[a] else ((7, None),):
                copy(a, k, slot(a, dg, 1 - c, half), me).wait_recv()
        for cp in sent:
            cp.wait_send()

    outs = pl.pallas_call(
        body, name=name, in_specs=[ANY] * nb, out_specs=[ANY] * nb,
        out_shape=[_sds((8,) + b.shape, b.dtype) for b in blocks],
        scratch_shapes=[_dma_sems(nb, 9), _dma_sems(nb, 9)],
    )(*blocks)
    return [_place_own(buf, b) for buf, b in zip(outs, blocks)]


def _place_own(buf, block):
    dev = 4 * lax.axis_index("x") + 2 * lax.axis_index("y") + lax.axis_index("c")
    return lax.dynamic_update_index_in_dim(buf, block, dev, 0)


def _ride_sibling_halves(gs):
    def copies(ins, outs, send_sems, recv_sems):
        x, y, c = lax.axis_index("x"), lax.axis_index("y"), lax.axis_index("c")
        cps = []
        for a in range(len(gs)):
            r2 = ins[a].shape[1] // 2
            other = ins[a].at[:, pl.ds((1 - c) * r2, r2), :]
            cps.append(pltpu.make_async_remote_copy(src_ref=other, dst_ref=outs[a], send_sem=send_sems.at[a],
                                                    recv_sem=recv_sems.at[a], device_id=(x, y, 1 - c),
                                                    device_id_type=MESH))
        return cps

    return _Ride(gs, [_sds((g.shape[0], g.shape[1] // 2, g.shape[2]), g.dtype) for g in gs], len(gs), copies)


def _pair_sum(g, t, chip_half, name):
    nsh, rows, cols = g.shape
    r2 = rows // 2
    tr = _row_tile(r2, cols)
    nt = r2 // tr

    def body(ch_ref, g_ref, t_ref, p32_ref, p16_ref):
        p = g_ref[...] + t_ref[...]
        p16_ref[...] = p.astype(BF16)

        @pl.when(pl.program_id(1) == ch_ref[0])
        def _():
            p32_ref[...] = p

    blk = pl.BlockSpec((None, tr, cols), lambda i, k, ch_ref: (k, i, 0))
    return pl.pallas_call(
        body, name=name,
        grid_spec=pltpu.PrefetchScalarGridSpec(
            num_scalar_prefetch=1, grid=(nt, nsh),
            in_specs=[pl.BlockSpec((None, tr, cols), lambda i, k, ch_ref: (k, ch_ref[1] * nt + i, 0)), blk],
            out_specs=[pl.BlockSpec((tr, cols), lambda i, k, ch_ref: (i, 0)), blk]),
        out_shape=[_sds((r2, cols)), _sds((nsh, r2, cols), BF16)],
        compiler_params=_params("parallel", "arbitrary"),
    )(chip_half, g, t)


def _pair_sum_small(gs, ts, chip_half):
    na = len(gs)

    def body(ch_ref, *refs):
        g_refs, t_refs, outs = refs[:na], refs[na:2 * na], refs[2 * na:]
        for a in range(na):
            r2 = t_refs[a].shape[1]
            own = pl.ds(pl.multiple_of(ch_ref[1] * r2, 8), r2)
            outs[2 * a + 1][...] = (g_refs[a][:, own, :] + t_refs[a][...]).astype(BF16)
            outs[2 * a][...] = g_refs[a][ch_ref[0], own, :] + t_refs[a][ch_ref[0]]

    res = pl.pallas_call(
        body, name="rs_pair_sum_small",
        in_specs=[pl.BlockSpec(memory_space=pltpu.SMEM)] + [pl.BlockSpec(memory_space=pltpu.VMEM)] * (2 * na),
        out_shape=[s for t in ts for s in (_sds(t.shape[1:]), _sds(t.shape, BF16))], compiler_params=_params(),
    )(chip_half, *gs, *ts)
    return [(res[2 * a], res[2 * a + 1]) for a in range(na)]


def _chip_sum_small(p32s, us, chip_half):
    na = len(p32s)

    def body(ch_ref, *refs):
        p_refs, u_refs, outs = refs[:na], refs[na:2 * na], refs[2 * na:]
        for a in range(na):
            r2 = p_refs[a].shape[0]
            acc = p_refs[a][...]
            for j in range(3):
                acc = acc + u_refs[a][j].astype(F32)
            outs[a][pl.ds(pl.multiple_of(ch_ref[1] * r2, 8), r2), :] = acc

    return pl.pallas_call(
        body, name="rs_chip_sum_small",
        in_specs=[pl.BlockSpec(memory_space=pltpu.SMEM)] + [pl.BlockSpec(memory_space=pltpu.VMEM)] * (2 * na),
        out_shape=[_sds((2 * p.shape[0], p.shape[1])) for p in p32s], compiler_params=_params(),
    )(chip_half, *p32s, *us)


def _ride_chip_exchange(ps):
    def copies(ins, outs, send_sems, recv_sems):
        x, y, c = lax.axis_index("x"), lax.axis_index("y"), lax.axis_index("c")
        chips = [(1 - x, y), (x, 1 - y), (1 - x, 1 - y)]
        cps = []
        for a in range(len(ps)):
            for j, (ox, oy) in enumerate(chips):
                cps.append(pltpu.make_async_remote_copy(src_ref=ins[a].at[2 * ox + oy], dst_ref=outs[a].at[j],
                                                        send_sem=send_sems.at[3 * a + j],
                                                        recv_sem=recv_sems.at[3 * a + j],
                                                        device_id=(ox, oy, c), device_id_type=MESH))
        return cps

    return _Ride(ps, [_sds((3,) + p.shape[1:], p.dtype) for p in ps], 3 * len(ps), copies)


def _chip_sum(p32, u, chip_half, name):
    r2, cols = p32.shape
    tr = _row_tile(r2, cols)
    nt = r2 // tr

    def body(ch_ref, p_ref, u_ref, o_ref):
        acc = p_ref[...]
        for j in range(3):
            acc = acc + u_ref[j].astype(F32)
        o_ref[...] = acc

    return pl.pallas_call(
        body, name=name,
        grid_spec=pltpu.PrefetchScalarGridSpec(
            num_scalar_prefetch=1, grid=(nt,),
            in_specs=[pl.BlockSpec((tr, cols), lambda i, ch_ref: (i, 0)),
                      pl.BlockSpec((3, tr, cols), lambda i, ch_ref: (0, i, 0))],
            out_specs=pl.BlockSpec((tr, cols), lambda i, ch_ref: (ch_ref[1] * nt + i, 0))),
        out_shape=_sds((2 * r2, cols)), compiler_params=_params("parallel"),
    )(chip_half, p32, u)


def _sibling_join(fs, name):
    nb = len(fs)

    def body(*refs):
        outs = refs[nb:2 * nb]
        send_sems, recv_sems = refs[2 * nb:]
        x, y, c = lax.axis_index("x"), lax.axis_index("y"), lax.axis_index("c")
        cps = []
        for a in range(nb):
            r2 = outs[a].shape[0] // 2
            rows = outs[a].at[pl.ds(c * r2, r2), :]
            cps.append(pltpu.make_async_remote_copy(src_ref=rows, dst_ref=rows, send_sem=send_sems.at[a],
                                                    recv_sem=recv_sems.at[a], device_id=(x, y, 1 - c),
                                                    device_id_type=MESH))
        for cp in cps:
            cp.start()
        for cp in cps:
            cp.wait()

    return pl.pallas_call(
        body, name=name, in_specs=[ANY] * nb, out_specs=[ANY] * nb,
        out_shape=[_sds(f.shape, f.dtype) for f in fs],
        input_output_aliases={a: a for a in range(nb)},
        scratch_shapes=[_dma_sems(nb), _dma_sems(nb)],
    )(*fs)


def _row_tile(rows, cols):
    tile = rows
    while tile * cols * 4 > (1 << 20) and tile % 16 == 0:
        tile //= 2
    return tile


def _w_ada_grad(c_all, dmod_cols):
    def body(c_ref, d_ref, o_ref):
        o_ref[...] = _dot_tn(c_ref[...].astype(BF16), d_ref[...].astype(BF16))

    return pl.pallas_call(body, name="w_ada_grad", out_shape=_sds((c_all.shape[1], dmod_cols.shape[1])),
                          compiler_params=_params())(c_all, dmod_cols)


def _adam_math(w, g, m, v):
    nm = ADAM_B1 * m + (1.0 - ADAM_B1) * g
    nv = ADAM_B2 * v + (1.0 - ADAM_B2) * (g * g)
    m_hat = nm / (1.0 - ADAM_B1 ** ADAM_STEP)
    v_hat = nv / (1.0 - ADAM_B2 ** ADAM_STEP)
    return -ADAM_LR * (m_hat / (jnp.sqrt(v_hat) + ADAM_EPS) + ADAM_WD * w), nm, nv


def _adamw(w, g, m, v, name):
    rows, cols = w.shape
    tr = _row_tile(rows, cols)

    def body(w_ref, g_ref, m_ref, v_ref, go_ref, d_ref, nm_ref, nv_ref):
        gv = g_ref[...]
        go_ref[...] = gv
        d_ref[...], nm_ref[...], nv_ref[...] = _adam_math(w_ref[...], gv, m_ref[...], v_ref[...])

    spec = pl.BlockSpec((tr, cols), lambda i: (i, 0))
    return pl.pallas_call(
        body, name=name, grid=(rows // tr,), in_specs=[spec] * 4, out_specs=[spec] * 4,
        out_shape=[_sds((rows, cols))] * 4, compiler_params=_params("parallel"),
    )(w, g, m, v)


def _pack_small(dmod, dnorm_g, dfinal_g, dpool_scale, drel_bias, loss, dpool_w):
    return jnp.concatenate([dmod.reshape(-1, 128), dnorm_g.reshape(-1, 128), dfinal_g.reshape(-1, 128),
                            jnp.pad(dpool_scale.reshape(-1, 128), ((0, PK_RELB - PK_PSCALE - AW // 128), (0, 0))),
                            jnp.pad(drel_bias, ((0, 0), (0, 128 - NG * NH))),
                            jnp.full((PK_POOLW - PK_LOSS, 128), loss, F32), dpool_w.reshape(-1, 128)], axis=0)


def _small_update(small_all, ws, ms, vs):
    lane_rows = [(r0, r0 + w.shape[1] // 128) for r0, w in zip((PK_BADA, PK_NORMG, PK_FINALG, PK_PSCALE), ws)]
    nw = len(ws)

    def body(all_ref, *refs):
        w_refs, m_refs, v_refs = refs[:nw], refs[nw:2 * nw], refs[2 * nw:3 * nw]
        loss_ref, outs = refs[3 * nw], refs[3 * nw + 1:]
        g = all_ref[0]
        for s in range(1, all_ref.shape[0]):
            g = g + all_ref[s]
        loss_ref[...] = jnp.broadcast_to(g[PK_LOSS:PK_LOSS + 1, :], loss_ref.shape)

        def put(p, at, gv):
            d, nm, nv = _adam_math(w_refs[p][at], gv, m_refs[p][at], v_refs[p][at])
            for o_ref, val in zip(outs[4 * p:4 * p + 4], (gv, d, nm, nv)):
                o_ref[at] = val

        for p, (r0, r1) in enumerate(lane_rows):
            for i in range(r1 - r0):
                put(p, (slice(None), slice(128 * i, 128 * (i + 1))), g[r0 + i:r0 + i + 1, :])
        put(4, (slice(None), slice(None)), g[PK_RELB:PK_LOSS, 0:NG * NH])
        put(5, (slice(None), slice(None)), g[PK_POOLW:PK_ROWS, :])

    res = pl.pallas_call(
        body, name="small_update",
        out_shape=[_sds((8, 128))] + [_sds(w.shape) for w in ws for _ in range(4)], compiler_params=_params(),
    )(small_all, *ws, *ms, *vs)
    return res[0], [res[1 + 4 * p:5 + 4 * p] for p in range(nw)]


def kernel(x, c, norm_g, w_ada, b_ada, w_in, pool_w, pool_scale, w_attn_br, w_pool_br, w_out, rel_bias, final_g, loss_target, m_norm_g, m_w_ada, m_b_ada, m_w_in, m_pool_w, m_pool_scale, m_w_attn_br, m_w_pool_br, m_w_out, m_rel_bias, m_final_g, v_norm_g, v_w_ada, v_b_ada, v_w_in, v_pool_w, v_pool_scale, v_w_attn_br, v_w_pool_br, v_w_out, v_rel_bias, v_final_g):
    ix, iy, ic = lax.axis_index("x"), lax.axis_index("y"), lax.axis_index("c")
    dev = 4 * ix + 2 * iy + ic
    chip = 2 * ix + iy

    def half(w):
        r2 = w.shape[0] // 2
        return lax.dynamic_slice_in_dim(w, ic * r2, r2, axis=0).astype(BF16)

    gathered = _allgather8([jnp.broadcast_to(c, (8, D)), half(w_in[0]), half(w_attn_br[0]), half(w_pool_br[0]),
                            half(w_out[0])], "gather_weights", relay=[False, True, True, True, True])
    c_all = gathered[0][:, 0, :]
    wg_in = gathered[1].reshape(N_SHARD, D, IN_W // N_SHARD)
    wab = gathered[2].reshape(N_SHARD, AW, D // N_SHARD).transpose(1, 0, 2).reshape(AW, D)
    wpb = gathered[3].reshape(N_SHARD, AW, D // N_SHARD).transpose(1, 0, 2).reshape(AW, D)
    wout = gathered[4].reshape(D, D)

    mw = 3 * D // N_SHARD
    modp = _mod_partial(c_all, w_ada[0], lax.dynamic_slice_in_dim(b_ada, chip * mw, mw, axis=1))
    mod_all = _allgather8([modp], "gather_mod")[0]
    mod_full = mod_all[::2].transpose(1, 0, 2).reshape(8, 3 * D)
    mod = lax.dynamic_slice_in_dim(mod_full, dev, 1, axis=0)

    chip_half = jnp.stack([chip, ic]).astype(jnp.int32)
    r = _local_step(x[0], loss_target[0], mod, wg_in, wab, wpb, wout, pool_w[0], pool_scale, rel_bias, norm_g,
                    final_g.reshape(1, D), chip_half)

    packed = _pack_small(r["dmod"], r["dnorm_g"], r["dfinal_g"], r["dpool_scale"], r["drel_bias"], r["loss"],
                         r["dpool_w"])
    small_all = _allgather8([packed], "gather_small")[0]
    small = ["b_ada", "norm_g", "final_g", "pool_scale", "rel_bias", "pool_w"]
    shaped = lambda b, n, f, ps, rb, pw: [b, n, f.reshape(1, D), ps, rb, pw.reshape(4 * PGW, PGW)]
    loss, small_out = _small_update(small_all, shaped(b_ada, norm_g, final_g, pool_scale, rel_bias, pool_w),
                                    shaped(m_b_ada, m_norm_g, m_final_g, m_pool_scale, m_rel_bias, m_pool_w),
                                    shaped(v_b_ada, v_norm_g, v_final_g, v_pool_scale, v_rel_bias, v_pool_w))
    dmod_all = small_all[:, PK_BADA:PK_NORMG, :].reshape(8, 3 * D)
    g_w_ada = _w_ada_grad(c_all, lax.dynamic_slice_in_dim(dmod_all, chip * mw, mw, axis=1))

    g_w_in, g_w_ab, g_w_pb, g_w_out = _sibling_join([r["rs_in"], r["rs_attn_br"], r["rs_pool_br"], r["rs_out"]],
                                                    "rs_sibling_join")
    upd = dict(zip(small, small_out))
    upd["final_g"] = [a.reshape(D) for a in upd["final_g"]]
    upd["pool_w"] = [a.reshape(1, 4, PGW, PGW) for a in upd["pool_w"]]
    for nme, w, g, m, v in (("w_ada", w_ada, g_w_ada, m_w_ada, v_w_ada), ("w_in", w_in, g_w_in, m_w_in, v_w_in),
                            ("w_attn_br", w_attn_br, g_w_ab, m_w_attn_br, v_w_attn_br),
                            ("w_pool_br", w_pool_br, g_w_pb, m_w_pool_br, v_w_pool_br),
                            ("w_out", w_out, g_w_out, m_w_out, v_w_out)):
        upd[nme] = [a[None] for a in _adamw(w[0], g, m[0], v[0], "adamw_" + nme)]
    names = ["norm_g", "w_ada", "b_ada", "w_in", "pool_w", "pool_scale", "w_attn_br", "w_pool_br", "w_out",
             "rel_bias", "final_g"]
    return (loss[0, 0], r["grad_x"][None]) + tuple(upd[nme][kind] for kind in range(4) for nme in names)
```

```python
import math

import numpy as np
import jax
import jax.numpy as jnp
from jax import lax
from jax.experimental import pallas as pl
from jax.experimental.pallas import tpu as pltpu

F32 = jnp.float32
BF16 = jnp.bfloat16

D = 1024
HD = 64
NH = 8
AW = NH * HD
GROUPS = ((128, 1), (512, 4), (2048, 16))
NG = len(GROUPS)
BLK = 128
GW = 3 * AW
QKV_W = NG * GW
REST_W = 3584
IN_W = QKV_W + REST_W
CB = 512
NCB_QKV = QKV_W // CB
POOL_WINDOWS = (2, 4, 8, 16)
PGW = 128
HALO = 16
NUM_BUCKETS = 32
MAX_DISTANCE = 2048
EPS = 1e-6
NEG = -1e30
N_SHARD = 4
VMEM_LIMIT = 56 * 1024 * 1024

ADAM_LR = 0.001
ADAM_B1 = 0.9
ADAM_B2 = 0.999
ADAM_EPS = 1e-08
ADAM_WD = 0.01
ADAM_STEP = 10

PK_BADA, PK_NORMG, PK_FINALG, PK_PSCALE, PK_RELB, PK_LOSS, PK_POOLW, PK_ROWS = 0, 24, 32, 40, 48, 80, 88, 600

ANY = pl.BlockSpec(memory_space=pl.ANY)
MESH = pl.DeviceIdType.MESH


def _params(*sem):
    return pltpu.CompilerParams(dimension_semantics=sem, vmem_limit_bytes=VMEM_LIMIT)


def _sds(shape, dtype=F32):
    return jax.ShapeDtypeStruct(shape, dtype)


def _dot(a, b):
    return jnp.dot(a, b, preferred_element_type=F32)


def _dot_nt(a, b):
    return lax.dot_general(a, b, (((1,), (1,)), ((), ())), preferred_element_type=F32)


def _dot_tn(a, b):
    return lax.dot_general(a, b, (((0,), (0,)), ((), ())), preferred_element_type=F32)


def _sigmoid(z):
    return 0.5 * jnp.tanh(0.5 * z) + 0.5


def _dma_sems(*shape):
    return pltpu.SemaphoreType.DMA(shape)


class _Ride:
    def __init__(self, arrays, out_shapes, n_copies, copies):
        self.arrays, self.out_shapes, self.n_copies, self.copies = list(arrays), list(out_shapes), n_copies, copies


def _call_with_ride(body, ride, first, last, *, in_specs, out_specs, out_shape, scratch_shapes=(), **kw):
    in_specs, out_specs, out_shape, scratch_shapes = list(in_specs), list(out_specs), list(out_shape), list(scratch_shapes)
    n_in, n_out, n_sc = len(in_specs), len(out_specs), len(scratch_shapes)
    if ride is None:
        def run_plain(*operands):
            return pl.pallas_call(body, in_specs=in_specs, out_specs=out_specs, out_shape=out_shape,
                                  scratch_shapes=scratch_shapes, **kw)(*operands), []
        return run_plain
    n_ri, n_ro = len(ride.arrays), len(ride.out_shapes)

    def wrapped(*refs):
        ins, rest = refs[:n_in], refs[n_in:]
        r_ins, rest = rest[:n_ri], rest[n_ri:]
        outs, rest = rest[:n_out], rest[n_out:]
        r_outs, rest = rest[:n_ro], rest[n_ro:]
        scratch, (send_sems, recv_sems) = rest[:n_sc], rest[n_sc:]

        @pl.when(first())
        def _():
            for cp in ride.copies(r_ins, r_outs, send_sems, recv_sems):
                cp.start()

        body(*ins, *outs, *scratch)

        @pl.when(last())
        def _():
            for cp in ride.copies(r_ins, r_outs, send_sems, recv_sems):
                cp.wait()

    def run(*operands):
        res = pl.pallas_call(
            wrapped, in_specs=in_specs + [ANY] * n_ri, out_specs=out_specs + [ANY] * n_ro,
            out_shape=out_shape + ride.out_shapes,
            scratch_shapes=scratch_shapes + [_dma_sems(ride.n_copies), _dma_sems(ride.n_copies)], **kw,
        )(*operands, *ride.arrays)
        return res[:n_out], res[n_out:]
    return run


def _bucket_tables():
    i = np.arange(BLK)[:, None]
    j = np.arange(2 * BLK)[None, :]
    dist = BLK + i - j
    valid = (dist >= 0) & (dist <= BLK)
    tabs = []
    for _, dil in GROUPS:
        n = (np.clip(dist, 0, BLK) * dil).astype(np.int32)
        max_exact = NUM_BUCKETS // 2
        nf = np.maximum(n, 1).astype(np.float32)
        large = max_exact + (np.log(nf / np.float32(max_exact)) / np.float32(math.log(MAX_DISTANCE / max_exact))
                             * np.float32(NUM_BUCKETS - max_exact)).astype(np.int32)
        large = np.minimum(large, NUM_BUCKETS - 1)
        bucket = np.where(n < max_exact, n, large)
        tab = np.where(valid, bucket, -1).astype(np.int32)
        perm = _block_perm(dil)
        tabs.append(tab[perm][:, np.concatenate([perm, BLK + perm])])
    return np.stack(tabs)


def _bias_table(rel_bias, buckets):
    def body(rb_ref, bk_ref, out_ref):
        g = pl.program_id(0)
        bk = bk_ref[...]
        for h in range(NH):
            acc = jnp.full((BLK, 2 * BLK), NEG, F32)
            for b in range(NUM_BUCKETS):
                acc = jnp.where(bk == b, rb_ref[b, g * NH + h], acc)
            out_ref[h] = acc

    return pl.pallas_call(
        body, name="bias_table", grid=(NG,),
        in_specs=[pl.BlockSpec(memory_space=pltpu.SMEM),
                  pl.BlockSpec((None, BLK, 2 * BLK), lambda g: (g, 0, 0))],
        out_specs=pl.BlockSpec((NH, BLK, 2 * BLK), lambda g: (g, 0, 0)),
        out_shape=_sds((NG * NH, BLK, 2 * BLK)),
        compiler_params=_params("arbitrary"),
    )(rel_bias, buckets)


def _bias_grad(ds_acc, buckets, ride):
    def body(acc_ref, bk_ref, out_ref):
        bk = bk_ref[...]
        acc = acc_ref[...]
        lane = lax.broadcasted_iota(jnp.int32, (8, 128), 1)
        out = jnp.zeros((8, 128), F32)
        for b in range(NUM_BUCKETS):
            val = jnp.sum(jnp.where(bk == b, acc, 0.0))
            out = jnp.where(lane == b, val, out)
        out_ref[...] = out

    (out,), rode = _call_with_ride(
        body, ride, lambda: pl.program_id(0) == 0, lambda: pl.program_id(0) == NG * NH - 1,
        name="bias_grad", grid=(NG * NH,),
        in_specs=[pl.BlockSpec((None, BLK, 2 * BLK), lambda gh: (gh, 0, 0)),
                  pl.BlockSpec((None, BLK, 2 * BLK), lambda gh: (gh // NH, 0, 0))],
        out_specs=[pl.BlockSpec((None, 8, 128), lambda gh: (gh, 0, 0))],
        out_shape=[_sds((NG * NH, 8, 128))],
        compiler_params=_params("arbitrary"),
    )(ds_acc, buckets)
    return out, rode


def _mod_partial(c_all, w_ada_s, b_ada_s):
    def body(c_ref, w_ref, b_ref, o_ref):
        o_ref[...] = _dot(c_ref[...].astype(BF16), w_ref[...].astype(BF16)) + b_ref[...]

    return pl.pallas_call(body, name="mod_partial", out_shape=_sds((8, w_ada_s.shape[1])),
                          compiler_params=_params())(c_all, w_ada_s, b_ada_s)


def _prenorm(x, norm_g, mod):
    S = x.shape[0]
    tm = 1024

    def body(x_ref, g_ref, mod_ref, h_ref):
        xv = x_ref[...]
        r = lax.rsqrt(jnp.mean(xv * xv, axis=-1, keepdims=True) + EPS)
        n1 = xv * r * g_ref[...]
        h_ref[...] = (n1 * (1.0 + mod_ref[:, D:2 * D]) + mod_ref[:, 0:D]).astype(BF16)

    return pl.pallas_call(
        body, name="prenorm", grid=(S // tm,),
        in_specs=[pl.BlockSpec((tm, D), lambda i: (i, 0)), pl.BlockSpec((1, D), lambda i: (0, 0)),
                  pl.BlockSpec((1, 3 * D), lambda i: (0, 0))],
        out_specs=pl.BlockSpec((tm, D), lambda i: (i, 0)),
        out_shape=_sds((S, D), BF16), compiler_params=_params("parallel"),
    )(x, norm_g, mod)


def _proj(h, wg_in, j0, nj, dtype, name):
    S = h.shape[0]
    tm = S
    per = wg_in.shape[2] // CB

    def body(h_ref, w_ref, o_ref):
        o_ref[...] = _dot(h_ref[...], w_ref[...]).astype(dtype)

    return pl.pallas_call(
        body, name=name, grid=(S // tm, nj),
        in_specs=[pl.BlockSpec((tm, D), lambda m, j: (m, 0)),
                  pl.BlockSpec((None, D, CB), lambda m, j: ((j0 + j) // per, 0, (j0 + j) % per))],
        out_specs=pl.BlockSpec((tm, CB), lambda m, j: (m, j)),
        out_shape=_sds((S, nj * CB), dtype), compiler_params=_params("parallel", "parallel"),
    )(h, wg_in)


HS = 4
SLAB = HS * HD


def _lane_head(rows):
    return lax.broadcasted_iota(jnp.int32, (rows, SLAB), 1) // HD


def _head_stack(a):
    head = _lane_head(a.shape[0])
    return jnp.concatenate([jnp.where(head == h, a, jnp.zeros_like(a)) for h in range(HS)], axis=0)


def _head_unstack(a):
    rows = a.shape[0] // HS
    head = _lane_head(rows)
    out = a[:rows]
    for h in range(1, HS):
        out = jnp.where(head == h, a[h * rows:(h + 1) * rows], out)
    return out


STAT_W = 128
VIEW = 16


def _sub_layout(dil):
    if dil == 1:
        return BLK, [None]
    return BLK * dil // VIEW, [[r + dil * u for u in range(VIEW // dil)] for r in range(dil)]


def _block_perm(dil):
    a_rows, _ = _sub_layout(dil)
    p = np.arange(BLK)
    return p if dil == 1 else (VIEW // dil) * (p % a_rows) + p // a_rows


LB = 128
N_SLAB = NH // HS


RBS = 4


def _ld(refs, bs, s, w, rb=0):
    if bs is None:
        return refs[0][rb * BLK:(rb + 1) * BLK, s * w:(s + 1) * w]
    a_rows = refs[0].shape[0] // VIEW
    return jnp.concatenate([jnp.concatenate([ref[pl.ds(b, a_rows, stride=VIEW), :] for b in bs], axis=0)
                            for ref in refs[s * (w // LB):(s + 1) * (w // LB)]], axis=1)


def _st(ref, bs, s, val, rb=0):
    if bs is None:
        ref[rb * BLK:(rb + 1) * BLK, s * SLAB:(s + 1) * SLAB] = val.astype(ref.dtype)
        return
    a_rows = val.shape[0] // len(bs)
    for u, b in enumerate(bs):
        ref[:, b, s * SLAB:(s + 1) * SLAB] = val[u * a_rows:(u + 1) * a_rows]


def _attn_views(dil, S):
    a_rows, subs = _sub_layout(dil)
    if dil == 1:
        def ispecs(base, w, f):
            return [pl.BlockSpec((RBS * BLK, N_SLAB * w), lambda sg, n: (f(n), base // (N_SLAB * w)))]
        return subs, S // (RBS * BLK), N_SLAB, RBS, ispecs, (lambda w: (S, w)), (
            lambda f: pl.BlockSpec((RBS * BLK, AW), lambda sg, n: (f(n), 0)))

    sps = N_SLAB if dil < VIEW else 1

    def ispecs(base, w, f):
        return [pl.BlockSpec((a_rows * VIEW, LB), lambda sg, n, k=k: (f(n), (base + sg * sps * w) // LB + k))
                for k in range(sps * w // LB)]
    return subs, S // (a_rows * VIEW), sps, 1, ispecs, (lambda w: (S // VIEW, VIEW, w)), (
        lambda f: pl.BlockSpec((a_rows, VIEW, sps * SLAB), lambda sg, n: (f(n), 0, sg)))


def _attn_fwd(qkv_g, bias_tab, g):
    S = qkv_g.shape[0]
    subs, nbq, sps, rbs, ispecs, shape, ospec = _attn_views(GROUPS[g][1], S)
    cur = lambda n: n
    in_specs = [ispecs(0, SLAB, cur), ispecs(AW, SLAB, cur), ispecs(2 * AW, SLAB, cur)]
    nl = len(in_specs[0])

    def body(*refs):
        q, k, v = (refs[t * nl:(t + 1) * nl] for t in range(3))
        b_ref, o_ref, l_ref, kprev, vprev = refs[3 * nl:]
        n = pl.program_id(1)

        @pl.when(n == 0)
        def _():
            kprev[...] = jnp.zeros_like(kprev)
            vprev[...] = jnp.zeros_like(vprev)

        col = lax.broadcasted_iota(jnp.int32, (HS * BLK, 2 * BLK), 1)
        first = (col >= BLK) | (n > 0)
        for s_, rb, (i, bs) in ((s_, rb, sub) for s_ in range(sps) for rb in range(rbs) for sub in enumerate(subs)):
            cs = slice(s_ * SLAB, (s_ + 1) * SLAB)
            kc, vc = _ld(k, bs, s_, SLAB, rb).astype(BF16), _ld(v, bs, s_, SLAB, rb).astype(BF16)
            kb = jnp.concatenate([kprev[i, :, cs], kc], axis=0)
            vb = jnp.concatenate([vprev[i, :, cs], vc], axis=0)
            kprev[i, :, cs], vprev[i, :, cs] = kc, vc
            s = _dot_nt(_head_stack(_ld(q, bs, s_, SLAB, rb).astype(BF16)), kb) * (HD ** -0.5)
            s = s + b_ref[pl.ds(s_ * HS, HS)].reshape(HS * BLK, 2 * BLK)
            if rb == 0:
                s = jnp.where(first, s, NEG)
            m = jnp.max(s, axis=-1, keepdims=True)
            p = jnp.exp(s - m)
            den = jnp.sum(p, axis=-1, keepdims=True)
            _st(o_ref, bs, s_, _head_unstack(_dot(p.astype(BF16), vb) / den), rb)
            _st(l_ref, bs, s_, _head_unstack(jnp.broadcast_to(m + jnp.log(den), (HS * BLK, SLAB))), rb)

    out = _sds(shape(AW))
    nsg = N_SLAB // sps
    o, l = pl.pallas_call(
        body, name=f"attn_fwd{g}", grid=(nsg, nbq),
        in_specs=sum(in_specs, []) + [pl.BlockSpec((sps * HS, BLK, 2 * BLK), lambda sg, n: (g * nsg + sg, 0, 0))],
        out_specs=[ospec(cur), ospec(cur)],
        out_shape=[out, out],
        scratch_shapes=[pltpu.VMEM((len(subs), BLK, sps * SLAB), BF16)] * 2,
        compiler_params=_params("parallel", "arbitrary"),
    )(*([qkv_g] * (3 * nl)), bias_tab)
    return o.reshape(S, AW), l.reshape(S, AW)


def _attn_bwd(qkv_g, dattn, stats, bias_tab, g, ride):
    S = qkv_g.shape[0]
    subs, nbq, sps, rbs, ispecs, shape, ospec = _attn_views(GROUPS[g][1], S)
    cur = lambda n: jnp.minimum(n, nbq - 1)
    late = lambda n: jnp.maximum(n - 1, 0)
    in_specs = [ispecs(0, SLAB, cur), ispecs(AW, SLAB, cur), ispecs(2 * AW, SLAB, cur), ispecs(0, SLAB, cur),
                ispecs(0, STAT_W, cur)]
    nl = len(in_specs[0])

    def body(*refs):
        q, k, v, da = (refs[t * nl:(t + 1) * nl] for t in range(4))
        nst = len(in_specs[4])
        st_refs = refs[4 * nl:4 * nl + nst]
        b_ref, dq_ref, dk_ref, dv_ref, ds_ref, ck_ref, cv_ref, kprev, vprev, *held = refs[4 * nl + nst:]
        n = pl.program_id(1)

        @pl.when(n == 0)
        def _():
            for ref in (ds_ref, ck_ref, cv_ref, kprev, vprev, *held):
                ref[...] = jnp.zeros_like(ref)

        def finish(ref, t, bs, s_, rb, val):
            cs = slice(s_ * SLAB, (s_ + 1) * SLAB)
            if rbs == 1:
                _st(ref, bs, s_, val)
            elif rb == 0:
                for j in range(rbs - 1):
                    _st(ref, bs, s_, held[t][j * BLK:(j + 1) * BLK, cs], j)
                _st(ref, bs, s_, val, rbs - 1)
            else:
                held[t][(rb - 1) * BLK:rb * BLK, cs] = val

        @pl.when(n < nbq)
        def _():
            col = lax.broadcasted_iota(jnp.int32, (HS * BLK, 2 * BLK), 1)
            first = (col >= BLK) | (n > 0)
            for s_, rb, (i, bs) in ((s_, rb, sub) for s_ in range(sps) for rb in range(rbs) for sub in enumerate(subs)):
                cs = slice(s_ * SLAB, (s_ + 1) * SLAB)
                st = _ld(st_refs, bs, s_, STAT_W, rb)
                kc, vc = _ld(k, bs, s_, SLAB, rb).astype(BF16), _ld(v, bs, s_, SLAB, rb).astype(BF16)
                kb = jnp.concatenate([kprev[i, :, cs], kc], axis=0)
                vb = jnp.concatenate([vprev[i, :, cs], vc], axis=0)
                kprev[i, :, cs], vprev[i, :, cs] = kc, vc
                lse = jnp.concatenate([st[:, h:h + 1] for h in range(HS)], axis=0)
                delta = jnp.concatenate([st[:, HS + h:HS + h + 1] for h in range(HS)], axis=0)
                qs = _head_stack(_ld(q, bs, s_, SLAB, rb).astype(BF16))
                dos = _head_stack(_ld(da, bs, s_, SLAB, rb).astype(BF16))
                s = _dot_nt(qs, kb) * (HD ** -0.5) + b_ref[pl.ds(s_ * HS, HS)].reshape(HS * BLK, 2 * BLK)
                if rb == 0:
                    s = jnp.where(first, s, NEG)
                p = jnp.exp(s - lse)
                ds = p * (_dot_nt(dos, vb) - delta)
                ds_ref[pl.ds(s_ * HS, HS)] += ds.reshape(HS, BLK, 2 * BLK)
                ds_b = (ds * (HD ** -0.5)).astype(BF16)
                _st(dq_ref, bs, s_, _head_unstack(_dot(ds_b, kb)), rb)
                dkb = _dot_tn(ds_b, qs)
                dvb = _dot_tn(p.astype(BF16), dos)
                finish(dk_ref, 0, bs, s_, rb, ck_ref[i, :, cs] + dkb[:BLK])
                finish(dv_ref, 1, bs, s_, rb, cv_ref[i, :, cs] + dvb[:BLK])
                ck_ref[i, :, cs] = dkb[BLK:]
                cv_ref[i, :, cs] = dvb[BLK:]

        @pl.when(n == nbq)
        def _():
            for s_ in range(sps):
                for i, bs in enumerate(subs):
                    finish(dk_ref, 0, bs, s_, 0, ck_ref[i, :, s_ * SLAB:(s_ + 1) * SLAB])
                    finish(dv_ref, 1, bs, s_, 0, cv_ref[i, :, s_ * SLAB:(s_ + 1) * SLAB])

    out = _sds(shape(AW), BF16 if GROUPS[g][1] == 1 else F32)
    nsg = N_SLAB // sps
    (dq, dk, dv, ds_acc), rode = _call_with_ride(
        body, ride, lambda: (pl.program_id(0) == 0) & (pl.program_id(1) == 0),
        lambda: (pl.program_id(0) == nsg - 1) & (pl.program_id(1) == nbq),
        name=f"attn_bwd{g}", grid=(nsg, nbq + 1),
        in_specs=sum(in_specs, []) + [pl.BlockSpec((sps * HS, BLK, 2 * BLK), lambda sg, n: (g * nsg + sg, 0, 0))],
        out_specs=[ospec(cur), ospec(late), ospec(late),
                   pl.BlockSpec((sps * HS, BLK, 2 * BLK), lambda sg, n: (sg, 0, 0))],
        out_shape=[out] * 3 + [_sds((NH, BLK, 2 * BLK))],
        scratch_shapes=[pltpu.VMEM((len(subs), BLK, sps * SLAB), F32)] * 2
        + [pltpu.VMEM((len(subs), BLK, sps * SLAB), BF16)] * 2 + [pltpu.VMEM(((rbs - 1) * BLK, sps * SLAB), F32)] * (2 if rbs > 1 else 0),
        compiler_params=_params("arbitrary", "arbitrary"),
    )(*([qkv_g] * (3 * nl)), *([dattn] * nl), *([stats] * len(in_specs[4])), bias_tab)
    return [dq.reshape(S, AW), dk.reshape(S, AW), dv.reshape(S, AW)], ds_acc, rode


TM_MIX = 256


def _mix_specs(tm):
    row512 = pl.BlockSpec((tm, AW), lambda i: (i, 0))
    return ([row512] * 6 + [
        pl.BlockSpec((tm, REST_W), lambda i: (i, 0)),
        pl.BlockSpec((HALO, AW), lambda i: (jnp.maximum(i * (tm // HALO) - 1, 0), 1)),
        pl.BlockSpec((AW, D), lambda i: (0, 0)), pl.BlockSpec((AW, D), lambda i: (0, 0)),
        pl.BlockSpec((4, PGW, PGW), lambda i: (0, 0, 0)), pl.BlockSpec((1, AW), lambda i: (0, 0))])


def _mix_forward(i, tm, o_refs, l_refs, rest_ref, halo_ref, wab_ref, wpb_ref, pw_ref, ps_ref):
    l0, l1, l2 = (r[...] for r in l_refs)
    mx = jnp.maximum(jnp.maximum(l0, l1), l2)
    e0, e1, e2 = jnp.exp(l0 - mx), jnp.exp(l1 - mx), jnp.exp(l2 - mx)
    den = e0 + e1 + e2
    lj = mx + jnp.log(den)
    attn = (e0 * o_refs[0][...] + e1 * o_refs[1][...] + e2 * o_refs[2][...]) / den

    z_attn = rest_ref[:, 0:AW]
    u = rest_ref[:, AW:2 * AW]
    z_pool = rest_ref[:, 2 * AW:3 * AW]
    g_attn = rest_ref[:, 3 * AW:3 * AW + D]
    g_pool = rest_ref[:, 3 * AW + D:3 * AW + 2 * D]

    sg_a = _sigmoid(z_attn)
    sil_a = z_attn * sg_a
    a_g = (attn * sil_a).astype(BF16)
    y_attn = _dot(a_g, wab_ref[...])

    halo = jnp.where(i > 0, halo_ref[...], 0.0)
    ext = jnp.concatenate([halo, u], axis=0)
    t = i * tm + lax.broadcasted_iota(jnp.int32, (tm, 1), 0)
    pooled, mixed_raw = [], []
    for gi, win in enumerate(POOL_WINDOWS):
        s = ext[:, gi * PGW:(gi + 1) * PGW]
        sh = 1
        while sh < win:
            s = s + pltpu.roll(s, sh, 0)
            sh *= 2
        cnt = jnp.minimum(t + 1, win).astype(F32)
        pg = s[HALO:] / cnt - u[:, gi * PGW:(gi + 1) * PGW]
        pooled.append(pg.astype(BF16))
        mixed_raw.append(_dot(pooled[-1], pw_ref[gi].astype(BF16)))
    mixed_raw = jnp.concatenate(mixed_raw, axis=1)
    mixed = mixed_raw * ps_ref[...]
    sg_p = _sigmoid(z_pool)
    sil_p = z_pool * sg_p
    m_g = (mixed * sil_p).astype(BF16)
    y_pool = _dot(m_g, wpb_ref[...])

    sa = _sigmoid(g_attn)
    sp = _sigmoid(g_pool)
    merged = sa * y_attn + sp * y_pool
    return dict(lj=lj, attn=attn, z_attn=z_attn, z_pool=z_pool, sg_a=sg_a, sil_a=sil_a, a_g=a_g, y_attn=y_attn,
                pooled=pooled, mixed_raw=mixed_raw, mixed=mixed, sg_p=sg_p, sil_p=sil_p, m_g=m_g, y_pool=y_pool,
                sa=sa, sp=sp, merged=merged)


def _mix_step(x, target, os_, ls_, rest, wab, wpb, pool_w, pool_scale, wout, mod, final_g):
    S = x.shape[0]
    tm = TM_MIX
    nt = S // tm
    sw = D // N_SHARD

    def body(o0, o1, o2, l0, l1, l2, rest_ref, halo_ref, wab_ref, wpb_ref, pw_ref, ps_ref,
             x_ref, t_ref, wo_ref, mod_ref, fg_ref, dx2_ref, loss_ref, dfg_ref, dgate_ref,
             dattn_ref, stats_ref, dpooled_ref, dproj_hbm, dwo_hbm, dwab_hbm, dwpb_hbm, dpw_ref, dps_ref,
             awo, awab, awpb, stage, stage_sem):
        i = pl.program_id(0)
        slot = i % 2

        def staged(step, sl):
            return pltpu.make_async_copy(stage.at[sl], dproj_hbm.at[pl.ds(step * tm, tm), pl.ds(QKV_W, REST_W)],
                                         stage_sem.at[sl])

        @pl.when(i == 0)
        def _():
            for ref in (loss_ref, dfg_ref, dgate_ref, awo, awab, awpb, dpw_ref, dps_ref):
                ref[...] = jnp.zeros_like(ref)

        f = _mix_forward(i, tm, (o0, o1, o2), (l0, l1, l2), rest_ref, halo_ref, wab_ref, wpb_ref, pw_ref, ps_ref)
        mo = _dot(f["merged"].astype(BF16), wo_ref[...])
        gate = mod_ref[:, 2 * D:3 * D]
        fg = fg_ref[...]
        x2 = x_ref[...] + gate * mo
        r2 = lax.rsqrt(jnp.mean(x2 * x2, axis=-1, keepdims=True) + EPS)
        n2 = x2 * r2
        err = n2 * fg - t_ref[...]
        loss_ref[...] += 0.5 * jnp.sum(jnp.mean(err * err, axis=-1, keepdims=True))
        dy = err * (1.0 / D)
        dfg_ref[...] += jnp.sum(dy * n2, axis=0, keepdims=True)
        dn = dy * fg
        dx2 = r2 * (dn - n2 * jnp.mean(dn * n2, axis=-1, keepdims=True))
        dgate_ref[...] += jnp.sum(dx2 * mo, axis=0, keepdims=True)
        dx2_ref[...] = dx2

        dmo_b = (dx2 * gate).astype(BF16)
        dmerged = _dot_nt(dmo_b, wo_ref[...])
        awo[...] += _dot_tn(f["merged"].astype(BF16), dmo_b)
        sa, sp = f["sa"], f["sp"]
        dya = (dmerged * sa).astype(BF16)
        dyp = (dmerged * sp).astype(BF16)
        dg_attn = dmerged * f["y_attn"] * sa * (1.0 - sa)
        dg_pool = dmerged * f["y_pool"] * sp * (1.0 - sp)
        dag = _dot_nt(dya, wab_ref[...])
        awab[...] += _dot_tn(f["a_g"], dya)
        dmg = _dot_nt(dyp, wpb_ref[...])
        awpb[...] += _dot_tn(f["m_g"], dyp)
        dattn = dag * f["sil_a"]
        dattn_ref[...] = dattn
        prod = dattn * f["attn"]
        lane = lax.broadcasted_iota(jnp.int32, (tm, STAT_W), 1)
        for sb in range(N_SLAB):
            st = jnp.zeros((tm, STAT_W), F32)
            for h in range(HS):
                hs = slice((sb * HS + h) * HD, (sb * HS + h + 1) * HD)
                st = jnp.where(lane == h, f["lj"][:, hs.start:hs.start + 1], st)
                st = jnp.where(lane == HS + h, jnp.sum(prod[:, hs], axis=-1, keepdims=True), st)
            stats_ref[:, sb * STAT_W:(sb + 1) * STAT_W] = st
        dz_attn = dag * f["attn"] * (f["sg_a"] * (1.0 + f["z_attn"] * (1.0 - f["sg_a"])))
        dmixed = dmg * f["sil_p"]
        dz_pool = dmg * f["mixed"] * (f["sg_p"] * (1.0 + f["z_pool"] * (1.0 - f["sg_p"])))
        dps_ref[...] += jnp.sum(dmixed * f["mixed_raw"], axis=0, keepdims=True)
        dpm = (dmixed * ps_ref[...]).astype(BF16)
        for gi in range(len(POOL_WINDOWS)):
            cs = slice(gi * PGW, (gi + 1) * PGW)
            dpw_ref[gi] += _dot_tn(f["pooled"][gi], dpm[:, cs])
            dpooled_ref[:, cs] = _dot_nt(dpm[:, cs], pw_ref[gi].astype(BF16))
        @pl.when(i >= 2)
        def _():
            staged(i - 2, slot).wait()

        stage[slot, :, 0:AW] = dz_attn.astype(BF16)
        stage[slot, :, AW:2 * AW] = jnp.zeros((tm, AW), BF16)
        stage[slot, :, 2 * AW:3 * AW] = dz_pool.astype(BF16)
        stage[slot, :, 3 * AW:3 * AW + D] = dg_attn.astype(BF16)
        stage[slot, :, 3 * AW + D:3 * AW + 2 * D] = dg_pool.astype(BF16)
        staged(i, slot).start()

        @pl.when(i == nt - 1)
        def _():
            staged(i - 1, 1 - slot).wait()
            staged(i, slot).wait()
            pltpu.sync_copy(awo, dwo_hbm)
            for k in range(N_SHARD):
                pltpu.sync_copy(awab.at[:, pl.ds(k * sw, sw)], dwab_hbm.at[k])
                pltpu.sync_copy(awpb.at[:, pl.ds(k * sw, sw)], dwpb_hbm.at[k])

    row = pl.BlockSpec((tm, D), lambda i: (i, 0))
    vec = pl.BlockSpec((1, D), lambda i: (0, 0))
    row512 = pl.BlockSpec((tm, AW), lambda i: (i, 0))
    outs = pl.pallas_call(
        body, name="mix_step", grid=(nt,),
        in_specs=_mix_specs(tm) + [row, row, pl.BlockSpec((D, D), lambda i: (0, 0)),
                                   pl.BlockSpec((1, 3 * D), lambda i: (0, 0)), vec],
        out_specs=[row, pl.BlockSpec((8, 128), lambda i: (0, 0)), vec, vec,
                   row512, pl.BlockSpec((tm, N_SLAB * STAT_W), lambda i: (i, 0)), row512, ANY, ANY, ANY, ANY,
                   pl.BlockSpec((4, PGW, PGW), lambda i: (0, 0, 0)), pl.BlockSpec((1, AW), lambda i: (0, 0))],
        out_shape=[_sds((S, D)), _sds((8, 128)), _sds((1, D)), _sds((1, D)),
                   _sds((S, AW)), _sds((S, N_SLAB * STAT_W)), _sds((S, AW)), _sds((S, IN_W), BF16),
                   _sds((D, D)), _sds((N_SHARD, AW, sw)), _sds((N_SHARD, AW, sw)), _sds((4, PGW, PGW)), _sds((1, AW))],
        scratch_shapes=[pltpu.VMEM((D, D), F32), pltpu.VMEM((AW, D), F32), pltpu.VMEM((AW, D), F32),
                        pltpu.VMEM((2, tm, REST_W), BF16), _dma_sems(2)],
        compiler_params=_params("arbitrary"),
    )(*os_, *ls_, rest, rest, wab, wpb, pool_w, pool_scale, x, target, wout, mod, final_g)
    dx2, loss, dfg, dgate, dattn, stats, dpooled, dproj, dwo, dwab, dwpb, dpw, dps = outs
    return (dx2, loss, dfg, dgate, dattn, stats, dpooled, dproj, dwo.reshape(N_SHARD, D // N_SHARD, D), dwab, dwpb,
            dpw, dps)


def _pool_bwd(dpooled):
    S = dpooled.shape[0]
    tm = 1024
    nt = S // tm

    def body(dp_ref, nxt_ref, du_ref):
        i = pl.program_id(0)
        t = i * tm + lax.broadcasted_iota(jnp.int32, (tm + HALO, 1), 0)
        nxt = jnp.where(i < nt - 1, nxt_ref[...], 0.0)
        ext = jnp.concatenate([dp_ref[...], nxt], axis=0)
        for gi, win in enumerate(POOL_WINDOWS):
            cs = slice(gi * PGW, (gi + 1) * PGW)
            s = ext[:, cs] / jnp.minimum(t + 1, win).astype(F32)
            sh = 1
            while sh < win:
                s = s + pltpu.roll(s, tm + HALO - sh, 0)
                sh *= 2
            du_ref[:, cs] = (s[:tm] - dp_ref[:, cs]).astype(BF16)

    return pl.pallas_call(
        body, name="pool_bwd", grid=(nt,),
        in_specs=[pl.BlockSpec((tm, AW), lambda i: (i, 0)),
                  pl.BlockSpec((HALO, AW), lambda i: (jnp.minimum((i + 1) * (tm // HALO), S // HALO - 1), 0))],
        out_specs=pl.BlockSpec((tm, AW), lambda i: (i, 0)),
        out_shape=_sds((S, AW), BF16), compiler_params=_params("parallel"),
    )(dpooled, dpooled)


TB = 1024


def _dh(dproj, wg_in, ride):
    S = dproj.shape[0]
    per = wg_in.shape[2] // TB
    nm, nk = S // TB, IN_W // TB

    def body(dp_ref, w_ref, out_ref):
        @pl.when(pl.program_id(1) == 0)
        def _():
            out_ref[...] = jnp.zeros_like(out_ref)

        out_ref[...] += _dot_nt(dp_ref[...], w_ref[...])

    (dh,), rode = _call_with_ride(
        body, ride, lambda: (pl.program_id(0) == 0) & (pl.program_id(1) == 0),
        lambda: (pl.program_id(0) == nm - 1) & (pl.program_id(1) == nk - 1),
        name="dh", grid=(nm, nk),
        in_specs=[pl.BlockSpec((TB, TB), lambda m, kk: (m, kk)),
                  pl.BlockSpec((None, D, TB), lambda m, kk: (kk // per, 0, kk % per))],
        out_specs=[pl.BlockSpec((TB, D), lambda m, kk: (m, 0))],
        out_shape=[_sds((S, D))], compiler_params=_params("arbitrary", "arbitrary"),
    )(dproj, wg_in)
    return dh, rode


def _dw_in(h, dproj):
    S = dproj.shape[0]
    per = IN_W // N_SHARD // TB

    def body(h_ref, dp_ref, out_ref):
        out_ref[...] = _dot_tn(h_ref[...], dp_ref[...])

    return pl.pallas_call(
        body, name="dw_in", grid=(IN_W // TB,),
        in_specs=[pl.BlockSpec((S, D), lambda j: (0, 0)), pl.BlockSpec((S, TB), lambda j: (0, j))],
        out_specs=pl.BlockSpec((None, D, TB), lambda j: (j // per, 0, j % per)),
        out_shape=_sds((N_SHARD, D, IN_W // N_SHARD)), compiler_params=_params("parallel"),
    )(h, dproj)


def _prenorm_bwd(x, dh, dx2, norm_g, mod):
    S = x.shape[0]
    tm = 1024

    def body(x_ref, dh_ref, dx2_ref, g_ref, mod_ref, gx_ref, dg_ref, dshift_ref, dscale_ref):
        i = pl.program_id(0)

        @pl.when(i == 0)
        def _():
            dg_ref[...] = jnp.zeros_like(dg_ref)
            dshift_ref[...] = jnp.zeros_like(dshift_ref)
            dscale_ref[...] = jnp.zeros_like(dscale_ref)

        xv = x_ref[...]
        dhv = dh_ref[...]
        g = g_ref[...]
        r = lax.rsqrt(jnp.mean(xv * xv, axis=-1, keepdims=True) + EPS)
        xh = xv * r
        dshift_ref[...] += jnp.sum(dhv, axis=0, keepdims=True)
        dscale_ref[...] += jnp.sum(dhv * (xh * g), axis=0, keepdims=True)
        dn1 = dhv * (1.0 + mod_ref[:, D:2 * D])
        dg_ref[...] += jnp.sum(dn1 * xh, axis=0, keepdims=True)
        dxh = dn1 * g
        gx_ref[...] = dx2_ref[...] + r * (dxh - xh * jnp.mean(dxh * xh, axis=-1, keepdims=True))

    row = pl.BlockSpec((tm, D), lambda i: (i, 0))
    vec = pl.BlockSpec((1, D), lambda i: (0, 0))
    return pl.pallas_call(
        body, name="prenorm_bwd", grid=(S // tm,),
        in_specs=[row, row, row, vec, pl.BlockSpec((1, 3 * D), lambda i: (0, 0))],
        out_specs=[row, vec, vec, vec],
        out_shape=[_sds((S, D)), _sds((1, D)), _sds((1, D)), _sds((1, D))],
        compiler_params=_params("arbitrary"),
    )(x, dh, dx2, norm_g, mod)


def _local_step(x, target, mod, wg_in, wab, wpb, wout, pool_w, pool_scale, rel_bias, norm_g, final_g, chip_half):
    buckets = jnp.asarray(_bucket_tables())
    bias_tab = _bias_table(rel_bias, buckets)
    h = _prenorm(x, norm_g, mod)
    qkv = [_proj(h, wg_in, 3 * g, 3, BF16 if GROUPS[g][1] == 1 else F32, f"proj_qkv{g}") for g in range(NG)]
    rest = _proj(h, wg_in, NCB_QKV, REST_W // CB, F32, "proj_rest")
    os_, ls_ = zip(*[_attn_fwd(qkv[g], bias_tab, g) for g in range(NG)])
    (dx2, loss, dfinal_g, dgate, dattn, stats, dpooled, dproj, dw_out, dw_ab, dw_pb, dpool_w,
     dpool_scale) = _mix_step(x, target, os_, ls_, rest, wab, wpb, pool_w, pool_scale, wout, mod, final_g)
    du = _pool_bwd(dpooled)

    small = [dw_ab, dw_pb, dw_out]
    dqkv0, ds0, sib_small = _attn_bwd(qkv[0], dattn, stats, bias_tab, 0, _ride_sibling_halves(small))
    p_small = _pair_sum_small(small, sib_small, chip_half)
    dqkv1, ds1, u_small = _attn_bwd(qkv[1], dattn, stats, bias_tab, 1,
                                    _ride_chip_exchange([p16 for _, p16 in p_small]))
    rs_ab, rs_pb, rs_out = _chip_sum_small([p32 for p32, _ in p_small], u_small, chip_half)
    dqkv2, ds2, _ = _attn_bwd(qkv[2], dattn, stats, bias_tab, 2, None)

    for j, piece in enumerate(dqkv0 + dqkv1 + dqkv2):
        dproj = lax.dynamic_update_slice(dproj, piece.astype(BF16), (0, j * AW))
    dproj = lax.dynamic_update_slice(dproj, du, (0, QKV_W + AW))
    dw_in = _dw_in(h, dproj)
    drel_rows, (sib_in,) = _bias_grad(jnp.concatenate([ds0, ds1, ds2], axis=0), buckets,
                                      _ride_sibling_halves([dw_in]))
    drel = drel_rows[:, 0, :NUM_BUCKETS].T
    p32_in, p16_in = _pair_sum(dw_in, sib_in, chip_half, "rs_pair_sum_in")
    dh, (u_in,) = _dh(dproj, wg_in, _ride_chip_exchange([p16_in]))
    rs_in = _chip_sum(p32_in, u_in, chip_half, "rs_chip_sum_in")

    grad_x, dnorm_g, dshift, dscale = _prenorm_bwd(x, dh, dx2, norm_g, mod)
    dmod = jnp.concatenate([dshift, dscale, dgate], axis=1)
    return dict(loss=loss[0, 0], grad_x=grad_x, dmod=dmod, dnorm_g=dnorm_g, dfinal_g=dfinal_g, dpool_w=dpool_w,
                dpool_scale=dpool_scale, drel_bias=drel, dw_in=dw_in, dw_attn_br=dw_ab, dw_pool_br=dw_pb,
                dw_out=dw_out, rs_in=rs_in, rs_attn_br=rs_ab, rs_pool_br=rs_pb, rs_out=rs_out)


def _allgather8(blocks, name, relay=None):
    nb = len(blocks)
    relay = [False] * nb if relay is None else list(relay)

    def body(*refs):
        ins, outs = refs[:nb], refs[nb:2 * nb]
        send_sems, recv_sems = refs[2 * nb:]
        x, y, c = lax.axis_index("x"), lax.axis_index("y"), lax.axis_index("c")
        me, sibling = (x, y, c), (x, y, 1 - c)
        here, xn, yn, dg = (x, y), (1 - x, y), (x, 1 - y), (1 - x, 1 - y)

        def slot(a, chip, core, half=None):
            ref = outs[a].at[4 * chip[0] + 2 * chip[1] + core]
            if half is None:
                return ref
            r2 = ref.shape[0] // 2
            return ref.at[pl.ds(half * r2, r2)]

        def copy(a, k, dst, to, src=None):
            return pltpu.make_async_remote_copy(src_ref=dst if src is None else src, dst_ref=dst,
                                                send_sem=send_sems.at[a, k], recv_sem=recv_sems.at[a, k],
                                                device_id=to, device_id_type=MESH)

        def start(cps):
            for cp in cps:
                cp.start()
            return cps

        sent = []
        for a in range(nb):
            own = slot(a, here, c)
            sent += [copy(a, 0, own, sibling, src=ins[a]), copy(a, 1, own, (*xn, c), src=ins[a]),
                     copy(a, 2, own, (*yn, c), src=ins[a])]
            if not relay[a]:
                sent.append(copy(a, 3, own, (*dg, c), src=ins[a]))
        start(sent)
        for a in range(nb):
            copy(a, 2, slot(a, yn, c), me).wait_recv()
            sent += start([copy(a, 6, slot(a, yn, c), sibling)]
                          + ([copy(a, 3, slot(a, yn, c, 0), (*xn, c))] if relay[a] else []))
        for a in range(nb):
            copy(a, 1, slot(a, xn, c), me).wait_recv()
            sent += start([copy(a, 5, slot(a, xn, c), sibling)]
                          + ([copy(a, 4, slot(a, xn, c, 1), (*yn, c))] if relay[a] else []))
        for a in range(nb):
            for k, half in ((3, 0), (4, 1)) if relay[a] else ((3, None),):
                copy(a, k, slot(a, dg, c, half), me).wait_recv()
                sent += start([copy(a, 4 + k, slot(a, dg, c, half), sibling)])
        for a in range(nb):
            copy(a, 0, slot(a, here, 1 - c), me).wait_recv()
            copy(a, 5, slot(a, xn, 1 - c), me).wait_recv()
            copy(a, 6, slot(a, yn, 1 - c), me).wait_recv()
            for k, half in ((7, 0), (8, 1)) if relay[a] else ((7, None),):
                copy(a, k, slot(a, dg, 1 - c, half), me).wait_recv()
        for cp in sent:
            cp.wait_send()

    outs = pl.pallas_call(
        body, name=name, in_specs=[ANY] * nb, out_specs=[ANY] * nb,
        out_shape=[_sds((8,) + b.shape, b.dtype) for b in blocks],
        scratch_shapes=[_dma_sems(nb, 9), _dma_sems(nb, 9)],
    )(*blocks)
    return [_place_own(buf, b) for buf, b in zip(outs, blocks)]


def _place_own(buf, block):
    dev = 4 * lax.axis_index("x") + 2 * lax.axis_index("y") + lax.axis_index("c")
    return lax.dynamic_update_index_in_dim(buf, block, dev, 0)


def _ride_sibling_halves(gs):
    def copies(ins, outs, send_sems, recv_sems):
        x, y, c = lax.axis_index("x"), lax.axis_index("y"), lax.axis_index("c")
        cps = []
        for a in range(len(gs)):
            r2 = ins[a].shape[1] // 2
            other = ins[a].at[:, pl.ds((1 - c) * r2, r2), :]
            cps.append(pltpu.make_async_remote_copy(src_ref=other, dst_ref=outs[a], send_sem=send_sems.at[a],
                                                    recv_sem=recv_sems.at[a], device_id=(x, y, 1 - c),
                                                    device_id_type=MESH))
        return cps

    return _Ride(gs, [_sds((g.shape[0], g.shape[1] // 2, g.shape[2]), g.dtype) for g in gs], len(gs), copies)


def _pair_sum(g, t, chip_half, name):
    nsh, rows, cols = g.shape
    r2 = rows // 2
    tr = _row_tile(r2, cols)
    nt = r2 // tr

    def body(ch_ref, g_ref, t_ref, p32_ref, p16_ref):
        p = g_ref[...] + t_ref[...]
        p16_ref[...] = p.astype(BF16)

        @pl.when(pl.program_id(1) == ch_ref[0])
        def _():
            p32_ref[...] = p

    blk = pl.BlockSpec((None, tr, cols), lambda i, k, ch_ref: (k, i, 0))
    return pl.pallas_call(
        body, name=name,
        grid_spec=pltpu.PrefetchScalarGridSpec(
            num_scalar_prefetch=1, grid=(nt, nsh),
            in_specs=[pl.BlockSpec((None, tr, cols), lambda i, k, ch_ref: (k, ch_ref[1] * nt + i, 0)), blk],
            out_specs=[pl.BlockSpec((tr, cols), lambda i, k, ch_ref: (i, 0)), blk]),
        out_shape=[_sds((r2, cols)), _sds((nsh, r2, cols), BF16)],
        compiler_params=_params("parallel", "arbitrary"),
    )(chip_half, g, t)


def _pair_sum_small(gs, ts, chip_half):
    na = len(gs)

    def body(ch_ref, *refs):
        g_refs, t_refs, outs = refs[:na], refs[na:2 * na], refs[2 * na:]
        for a in range(na):
            r2 = t_refs[a].shape[1]
            own = pl.ds(pl.multiple_of(ch_ref[1] * r2, 8), r2)
            outs[2 * a + 1][...] = (g_refs[a][:, own, :] + t_refs[a][...]).astype(BF16)
            outs[2 * a][...] = g_refs[a][ch_ref[0], own, :] + t_refs[a][ch_ref[0]]

    res = pl.pallas_call(
        body, name="rs_pair_sum_small",
        in_specs=[pl.BlockSpec(memory_space=pltpu.SMEM)] + [pl.BlockSpec(memory_space=pltpu.VMEM)] * (2 * na),
        out_shape=[s for t in ts for s in (_sds(t.shape[1:]), _sds(t.shape, BF16))], compiler_params=_params(),
    )(chip_half, *gs, *ts)
    return [(res[2 * a], res[2 * a + 1]) for a in range(na)]


def _chip_sum_small(p32s, us, chip_half):
    na = len(p32s)

    def body(ch_ref, *refs):
        p_refs, u_refs, outs = refs[:na], refs[na:2 * na], refs[2 * na:]
        for a in range(na):
            r2 = p_refs[a].shape[0]
            acc = p_refs[a][...]
            for j in range(3):
                acc = acc + u_refs[a][j].astype(F32)
            outs[a][pl.ds(pl.multiple_of(ch_ref[1] * r2, 8), r2), :] = acc

    return pl.pallas_call(
        body, name="rs_chip_sum_small",
        in_specs=[pl.BlockSpec(memory_space=pltpu.SMEM)] + [pl.BlockSpec(memory_space=pltpu.VMEM)] * (2 * na),
        out_shape=[_sds((2 * p.shape[0], p.shape[1])) for p in p32s], compiler_params=_params(),
    )(chip_half, *p32s, *us)


def _ride_chip_exchange(ps):
    def copies(ins, outs, send_sems, recv_sems):
        x, y, c = lax.axis_index("x"), lax.axis_index("y"), lax.axis_index("c")
        chips = [(1 - x, y), (x, 1 - y), (1 - x, 1 - y)]
        cps = []
        for a in range(len(ps)):
            for j, (ox, oy) in enumerate(chips):
                cps.append(pltpu.make_async_remote_copy(src_ref=ins[a].at[2 * ox + oy], dst_ref=outs[a].at[j],
                                                        send_sem=send_sems.at[3 * a + j],
                                                        recv_sem=recv_sems.at[3 * a + j],
                                                        device_id=(ox, oy, c), device_id_type=MESH))
        return cps

    return _Ride(ps, [_sds((3,) + p.shape[1:], p.dtype) for p in ps], 3 * len(ps), copies)


def _chip_sum(p32, u, chip_half, name):
    r2, cols = p32.shape
    tr = _row_tile(r2, cols)
    nt = r2 // tr

    def body(ch_ref, p_ref, u_ref, o_ref):
        acc = p_ref[...]
        for j in range(3):
            acc = acc + u_ref[j].astype(F32)
        o_ref[...] = acc

    return pl.pallas_call(
        body, name=name,
        grid_spec=pltpu.PrefetchScalarGridSpec(
            num_scalar_prefetch=1, grid=(nt,),
            in_specs=[pl.BlockSpec((tr, cols), lambda i, ch_ref: (i, 0)),
                      pl.BlockSpec((3, tr, cols), lambda i, ch_ref: (0, i, 0))],
            out_specs=pl.BlockSpec((tr, cols), lambda i, ch_ref: (ch_ref[1] * nt + i, 0))),
        out_shape=_sds((2 * r2, cols)), compiler_params=_params("parallel"),
    )(chip_half, p32, u)


def _sibling_join(fs, name):
    nb = len(fs)

    def body(*refs):
        outs = refs[nb:2 * nb]
        send_sems, recv_sems = refs[2 * nb:]
        x, y, c = lax.axis_index("x"), lax.axis_index("y"), lax.axis_index("c")
        cps = []
        for a in range(nb):
            r2 = outs[a].shape[0] // 2
            rows = outs[a].at[pl.ds(c * r2, r2), :]
            cps.append(pltpu.make_async_remote_copy(src_ref=rows, dst_ref=rows, send_sem=send_sems.at[a],
                                                    recv_sem=recv_sems.at[a], device_id=(x, y, 1 - c),
                                                    device_id_type=MESH))
        for cp in cps:
            cp.start()
        for cp in cps:
            cp.wait()

    return pl.pallas_call(
        body, name=name, in_specs=[ANY] * nb, out_specs=[ANY] * nb,
        out_shape=[_sds(f.shape, f.dtype) for f in fs],
        input_output_aliases={a: a for a in range(nb)},
        scratch_shapes=[_dma_sems(nb), _dma_sems(nb)],
    )(*fs)


def _row_tile(rows, cols):
    tile = rows
    while tile * cols * 4 > (2 << 20) and tile % 16 == 0:
        tile //= 2
    return tile


def _w_ada_grad(c_all, dmod_cols):
    def body(c_ref, d_ref, o_ref):
        o_ref[...] = _dot_tn(c_ref[...].astype(BF16), d_ref[...].astype(BF16))

    return pl.pallas_call(body, name="w_ada_grad", out_shape=_sds((c_all.shape[1], dmod_cols.shape[1])),
                          compiler_params=_params())(c_all, dmod_cols)


def _adam_math(w, g, m, v):
    nm = ADAM_B1 * m + (1.0 - ADAM_B1) * g
    nv = ADAM_B2 * v + (1.0 - ADAM_B2) * (g * g)
    m_hat = nm / (1.0 - ADAM_B1 ** ADAM_STEP)
    v_hat = nv / (1.0 - ADAM_B2 ** ADAM_STEP)
    return -ADAM_LR * (m_hat / (jnp.sqrt(v_hat) + ADAM_EPS) + ADAM_WD * w), nm, nv


def _adamw(w, g, m, v, name):
    rows, cols = w.shape
    tr = _row_tile(rows, cols)

    def body(w_ref, g_ref, m_ref, v_ref, go_ref, d_ref, nm_ref, nv_ref):
        gv = g_ref[...]
        go_ref[...] = gv
        d_ref[...], nm_ref[...], nv_ref[...] = _adam_math(w_ref[...], gv, m_ref[...], v_ref[...])

    spec = pl.BlockSpec((tr, cols), lambda i: (i, 0))
    return pl.pallas_call(
        body, name=name, grid=(rows // tr,), in_specs=[spec] * 4, out_specs=[spec] * 4,
        out_shape=[_sds((rows, cols))] * 4, compiler_params=_params("parallel"),
    )(w, g, m, v)


def _pack_small(dmod, dnorm_g, dfinal_g, dpool_scale, drel_bias, loss, dpool_w):
    return jnp.concatenate([dmod.reshape(-1, 128), dnorm_g.reshape(-1, 128), dfinal_g.reshape(-1, 128),
                            jnp.pad(dpool_scale.reshape(-1, 128), ((0, PK_RELB - PK_PSCALE - AW // 128), (0, 0))),
                            jnp.pad(drel_bias, ((0, 0), (0, 128 - NG * NH))),
                            jnp.full((PK_POOLW - PK_LOSS, 128), loss, F32), dpool_w.reshape(-1, 128)], axis=0)


def _small_update(small_all, ws, ms, vs):
    lane_rows = [(r0, r0 + w.shape[1] // 128) for r0, w in zip((PK_BADA, PK_NORMG, PK_FINALG, PK_PSCALE), ws)]
    nw = len(ws)

    def body(all_ref, *refs):
        w_refs, m_refs, v_refs = refs[:nw], refs[nw:2 * nw], refs[2 * nw:3 * nw]
        loss_ref, outs = refs[3 * nw], refs[3 * nw + 1:]
        g = all_ref[0]
        for s in range(1, all_ref.shape[0]):
            g = g + all_ref[s]
        loss_ref[...] = jnp.broadcast_to(g[PK_LOSS:PK_LOSS + 1, :], loss_ref.shape)

        def put(p, at, gv):
            d, nm, nv = _adam_math(w_refs[p][at], gv, m_refs[p][at], v_refs[p][at])
            for o_ref, val in zip(outs[4 * p:4 * p + 4], (gv, d, nm, nv)):
                o_ref[at] = val

        for p, (r0, r1) in enumerate(lane_rows):
            for i in range(r1 - r0):
                put(p, (slice(None), slice(128 * i, 128 * (i + 1))), g[r0 + i:r0 + i + 1, :])
        put(4, (slice(None), slice(None)), g[PK_RELB:PK_LOSS, 0:NG * NH])
        put(5, (slice(None), slice(None)), g[PK_POOLW:PK_ROWS, :])

    res = pl.pallas_call(
        body, name="small_update",
        out_shape=[_sds((8, 128))] + [_sds(w.shape) for w in ws for _ in range(4)], compiler_params=_params(),
    )(small_all, *ws, *ms, *vs)
    return res[0], [res[1 + 4 * p:5 + 4 * p] for p in range(nw)]


def kernel(x, c, norm_g, w_ada, b_ada, w_in, pool_w, pool_scale, w_attn_br, w_pool_br, w_out, rel_bias, final_g, loss_target, m_norm_g, m_w_ada, m_b_ada, m_w_in, m_pool_w, m_pool_scale, m_w_attn_br, m_w_pool_br, m_w_out, m_rel_bias, m_final_g, v_norm_g, v_w_ada, v_b_ada, v_w_in, v_pool_w, v_pool_scale, v_w_attn_br, v_w_pool_br, v_w_out, v_rel_bias, v_final_g):
    ix, iy, ic = lax.axis_index("x"), lax.axis_index("y"), lax.axis_index("c")
    dev = 4 * ix + 2 * iy + ic
    chip = 2 * ix + iy

    def half(w):
        r2 = w.shape[0] // 2
        return lax.dynamic_slice_in_dim(w, ic * r2, r2, axis=0).astype(BF16)

    gathered = _allgather8([jnp.broadcast_to(c, (8, D)), half(w_in[0]), half(w_attn_br[0]), half(w_pool_br[0]),
                            half(w_out[0])], "gather_weights", relay=[False, True, True, True, True])
    c_all = gathered[0][:, 0, :]
    wg_in = gathered[1].reshape(N_SHARD, D, IN_W // N_SHARD)
    wab = gathered[2].reshape(N_SHARD, AW, D // N_SHARD).transpose(1, 0, 2).reshape(AW, D)
    wpb = gathered[3].reshape(N_SHARD, AW, D // N_SHARD).transpose(1, 0, 2).reshape(AW, D)
    wout = gathered[4].reshape(D, D)

    mw = 3 * D // N_SHARD
    modp = _mod_partial(c_all, w_ada[0], lax.dynamic_slice_in_dim(b_ada, chip * mw, mw, axis=1))
    mod_all = _allgather8([modp], "gather_mod")[0]
    mod_full = mod_all[::2].transpose(1, 0, 2).reshape(8, 3 * D)
    mod = lax.dynamic_slice_in_dim(mod_full, dev, 1, axis=0)

    chip_half = jnp.stack([chip, ic]).astype(jnp.int32)
    r = _local_step(x[0], loss_target[0], mod, wg_in, wab, wpb, wout, pool_w[0], pool_scale, rel_bias, norm_g,
                    final_g.reshape(1, D), chip_half)

    packed = _pack_small(r["dmod"], r["dnorm_g"], r["dfinal_g"], r["dpool_scale"], r["drel_bias"], r["loss"],
                         r["dpool_w"])
    small_all = _allgather8([packed], "gather_small")[0]
    small = ["b_ada", "norm_g", "final_g", "pool_scale", "rel_bias", "pool_w"]
    shaped = lambda b, n, f, ps, rb, pw: [b, n, f.reshape(1, D), ps, rb, pw.reshape(4 * PGW, PGW)]
    loss, small_out = _small_update(small_all, shaped(b_ada, norm_g, final_g, pool_scale, rel_bias, pool_w),
                                    shaped(m_b_ada, m_norm_g, m_final_g, m_pool_scale, m_rel_bias, m_pool_w),
                                    shaped(v_b_ada, v_norm_g, v_final_g, v_pool_scale, v_rel_bias, v_pool_w))
    dmod_all = small_all[:, PK_BADA:PK_NORMG, :].reshape(8, 3 * D)
    g_w_ada = _w_ada_grad(c_all, lax.dynamic_slice_in_dim(dmod_all, chip * mw, mw, axis=1))

    g_w_in, g_w_ab, g_w_pb, g_w_out = _sibling_join([r["rs_in"], r["rs_attn_br"], r["rs_pool_br"], r["rs_out"]],
                                                    "rs_sibling_join")
    upd = dict(zip(small, small_out))
    upd["final_g"] = [a.reshape(D) for a in upd["final_g"]]
    upd["pool_w"] = [a.reshape(1, 4, PGW, PGW) for a in upd["pool_w"]]
    for nme, w, g, m, v in (("w_ada", w_ada, g_w_ada, m_w_ada, v_w_ada), ("w_in", w_in, g_w_in, m_w_in, v_w_in),
                            ("w_attn_br", w_attn_br, g_w_ab, m_w_attn_br, v_w_attn_br),
                            ("w_pool_br", w_pool_br, g_w_pb, m_w_pool_br, v_w_pool_br),
                            ("w_out", w_out, g_w_out, m_w_out, v_w_out)):
        upd[nme] = [a[None] for a in _adamw(w[0], g, m[0], v[0], "adamw_" + nme)]
    names = ["norm_g", "w_ada", "b_ada", "w_in", "pool_w", "pool_scale", "w_attn_br", "w_pool_br", "w_out",
             "rel_bias", "final_g"]
    return (loss[0, 0], r["grad_x"][None]) + tuple(upd[nme][kind] for kind in range(4) for nme in names)
```

```python
import math

import numpy as np
import jax
import jax.numpy as jnp
from jax import lax
from jax.experimental import pallas as pl
from jax.experimental.pallas import tpu as pltpu

F32 = jnp.float32
BF16 = jnp.bfloat16

D = 1024
HD = 64
NH = 8
AW = NH * HD
GROUPS = ((128, 1), (512, 4), (2048, 16))
NG = len(GROUPS)
BLK = 128
GW = 3 * AW
QKV_W = NG * GW
REST_W = 3584
IN_W = QKV_W + REST_W
CB = 512
NCB_QKV = QKV_W // CB
POOL_WINDOWS = (2, 4, 8, 16)
PGW = 128
HALO = 16
NUM_BUCKETS = 32
MAX_DISTANCE = 2048
EPS = 1e-6
NEG = -1e30
N_SHARD = 4
VMEM_LIMIT = 56 * 1024 * 1024

ADAM_LR = 0.001
ADAM_B1 = 0.9
ADAM_B2 = 0.999
ADAM_EPS = 1e-08
ADAM_WD = 0.01
ADAM_STEP = 10

PK_BADA, PK_NORMG, PK_FINALG, PK_PSCALE, PK_RELB, PK_LOSS, PK_POOLW, PK_ROWS = 0, 24, 32, 40, 48, 80, 88, 600

ANY = pl.BlockSpec(memory_space=pl.ANY)
MESH = pl.DeviceIdType.MESH


def _params(*sem):
    return pltpu.CompilerParams(dimension_semantics=sem, vmem_limit_bytes=VMEM_LIMIT)


def _sds(shape, dtype=F32):
    return jax.ShapeDtypeStruct(shape, dtype)


def _dot(a, b):
    return jnp.dot(a, b, preferred_element_type=F32)


def _dot_nt(a, b):
    return lax.dot_general(a, b, (((1,), (1,)), ((), ())), preferred_element_type=F32)


def _dot_tn(a, b):
    return lax.dot_general(a, b, (((0,), (0,)), ((), ())), preferred_element_type=F32)


def _sigmoid(z):
    return 0.5 * jnp.tanh(0.5 * z) + 0.5


def _dma_sems(*shape):
    return pltpu.SemaphoreType.DMA(shape)


class _Ride:
    def __init__(self, arrays, out_shapes, n_copies, copies):
        self.arrays, self.out_shapes, self.n_copies, self.copies = list(arrays), list(out_shapes), n_copies, copies


def _call_with_ride(body, ride, first, last, *, in_specs, out_specs, out_shape, scratch_shapes=(), **kw):
    in_specs, out_specs, out_shape, scratch_shapes = list(in_specs), list(out_specs), list(out_shape), list(scratch_shapes)
    n_in, n_out, n_sc = len(in_specs), len(out_specs), len(scratch_shapes)
    if ride is None:
        def run_plain(*operands):
            return pl.pallas_call(body, in_specs=in_specs, out_specs=out_specs, out_shape=out_shape,
                                  scratch_shapes=scratch_shapes, **kw)(*operands), []
        return run_plain
    n_ri, n_ro = len(ride.arrays), len(ride.out_shapes)

    def wrapped(*refs):
        ins, rest = refs[:n_in], refs[n_in:]
        r_ins, rest = rest[:n_ri], rest[n_ri:]
        outs, rest = rest[:n_out], rest[n_out:]
        r_outs, rest = rest[:n_ro], rest[n_ro:]
        scratch, (send_sems, recv_sems) = rest[:n_sc], rest[n_sc:]

        @pl.when(first())
        def _():
            for cp in ride.copies(r_ins, r_outs, send_sems, recv_sems):
                cp.start()

        body(*ins, *outs, *scratch)

        @pl.when(last())
        def _():
            for cp in ride.copies(r_ins, r_outs, send_sems, recv_sems):
                cp.wait()

    def run(*operands):
        res = pl.pallas_call(
            wrapped, in_specs=in_specs + [ANY] * n_ri, out_specs=out_specs + [ANY] * n_ro,
            out_shape=out_shape + ride.out_shapes,
            scratch_shapes=scratch_shapes + [_dma_sems(ride.n_copies), _dma_sems(ride.n_copies)], **kw,
        )(*operands, *ride.arrays)
        return res[:n_out], res[n_out:]
    return run


def _bucket_tables():
    i = np.arange(BLK)[:, None]
    j = np.arange(2 * BLK)[None, :]
    dist = BLK + i - j
    valid = (dist >= 0) & (dist <= BLK)
    tabs = []
    for _, dil in GROUPS:
        n = (np.clip(dist, 0, BLK) * dil).astype(np.int32)
        max_exact = NUM_BUCKETS // 2
        nf = np.maximum(n, 1).astype(np.float32)
        large = max_exact + (np.log(nf / np.float32(max_exact)) / np.float32(math.log(MAX_DISTANCE / max_exact))
                             * np.float32(NUM_BUCKETS - max_exact)).astype(np.int32)
        large = np.minimum(large, NUM_BUCKETS - 1)
        bucket = np.where(n < max_exact, n, large)
        tab = np.where(valid, bucket, -1).astype(np.int32)
        perm = _block_perm(dil)
        tabs.append(tab[perm][:, np.concatenate([perm, BLK + perm])])
    return np.stack(tabs)


def _bias_table(rel_bias, buckets):
    def body(rb_ref, bk_ref, out_ref):
        g = pl.program_id(0)
        bk = bk_ref[...]
        for h in range(NH):
            acc = jnp.full((BLK, 2 * BLK), NEG, F32)
            for b in range(NUM_BUCKETS):
                acc = jnp.where(bk == b, rb_ref[b, g * NH + h], acc)
            out_ref[h] = acc

    return pl.pallas_call(
        body, name="bias_table", grid=(NG,),
        in_specs=[pl.BlockSpec(memory_space=pltpu.SMEM),
                  pl.BlockSpec((None, BLK, 2 * BLK), lambda g: (g, 0, 0))],
        out_specs=pl.BlockSpec((NH, BLK, 2 * BLK), lambda g: (g, 0, 0)),
        out_shape=_sds((NG * NH, BLK, 2 * BLK)),
        compiler_params=_params("arbitrary"),
    )(rel_bias, buckets)


def _bias_grad(ds_acc, buckets, ride):
    def body(acc_ref, bk_ref, out_ref):
        bk = bk_ref[...]
        acc = acc_ref[...]
        lane = lax.broadcasted_iota(jnp.int32, (8, 128), 1)
        out = jnp.zeros((8, 128), F32)
        for b in range(NUM_BUCKETS):
            val = jnp.sum(jnp.where(bk == b, acc, 0.0))
            out = jnp.where(lane == b, val, out)
        out_ref[...] = out

    (out,), rode = _call_with_ride(
        body, ride, lambda: pl.program_id(0) == 0, lambda: pl.program_id(0) == NG * NH - 1,
        name="bias_grad", grid=(NG * NH,),
        in_specs=[pl.BlockSpec((None, BLK, 2 * BLK), lambda gh: (gh, 0, 0)),
                  pl.BlockSpec((None, BLK, 2 * BLK), lambda gh: (gh // NH, 0, 0))],
        out_specs=[pl.BlockSpec((None, 8, 128), lambda gh: (gh, 0, 0))],
        out_shape=[_sds((NG * NH, 8, 128))],
        compiler_params=_params("arbitrary"),
    )(ds_acc, buckets)
    return out, rode


def _mod_partial(c_all, w_ada_s, b_ada_s):
    def body(c_ref, w_ref, b_ref, o_ref):
        o_ref[...] = _dot(c_ref[...].astype(BF16), w_ref[...].astype(BF16)) + b_ref[...]

    return pl.pallas_call(body, name="mod_partial", out_shape=_sds((8, w_ada_s.shape[1])),
                          compiler_params=_params())(c_all, w_ada_s, b_ada_s)


def _prenorm(x, norm_g, mod):
    S = x.shape[0]
    tm = 1024

    def body(x_ref, g_ref, mod_ref, h_ref):
        xv = x_ref[...]
        r = lax.rsqrt(jnp.mean(xv * xv, axis=-1, keepdims=True) + EPS)
        n1 = xv * r * g_ref[...]
        h_ref[...] = (n1 * (1.0 + mod_ref[:, D:2 * D]) + mod_ref[:, 0:D]).astype(BF16)

    return pl.pallas_call(
        body, name="prenorm", grid=(S // tm,),
        in_specs=[pl.BlockSpec((tm, D), lambda i: (i, 0)), pl.BlockSpec((1, D), lambda i: (0, 0)),
                  pl.BlockSpec((1, 3 * D), lambda i: (0, 0))],
        out_specs=pl.BlockSpec((tm, D), lambda i: (i, 0)),
        out_shape=_sds((S, D), BF16), compiler_params=_params("parallel"),
    )(x, norm_g, mod)


def _proj(h, wg_in, j0, nj, dtype, name):
    S = h.shape[0]
    tm = S
    per = wg_in.shape[2] // CB

    def body(h_ref, w_ref, o_ref):
        o_ref[...] = _dot(h_ref[...], w_ref[...]).astype(dtype)

    return pl.pallas_call(
        body, name=name, grid=(S // tm, nj),
        in_specs=[pl.BlockSpec((tm, D), lambda m, j: (m, 0)),
                  pl.BlockSpec((None, D, CB), lambda m, j: ((j0 + j) // per, 0, (j0 + j) % per))],
        out_specs=pl.BlockSpec((tm, CB), lambda m, j: (m, j)),
        out_shape=_sds((S, nj * CB), dtype), compiler_params=_params("parallel", "parallel"),
    )(h, wg_in)


HS = 4
SLAB = HS * HD


def _lane_head(rows):
    return lax.broadcasted_iota(jnp.int32, (rows, SLAB), 1) // HD


def _head_stack(a):
    head = _lane_head(a.shape[0])
    return jnp.concatenate([jnp.where(head == h, a, jnp.zeros_like(a)) for h in range(HS)], axis=0)


def _head_unstack(a):
    rows = a.shape[0] // HS
    head = _lane_head(rows)
    out = a[:rows]
    for h in range(1, HS):
        out = jnp.where(head == h, a[h * rows:(h + 1) * rows], out)
    return out


STAT_W = 128
VIEW = 16


def _sub_layout(dil):
    if dil == 1:
        return BLK, [None]
    return BLK * dil // VIEW, [[r + dil * u for u in range(VIEW // dil)] for r in range(dil)]


def _block_perm(dil):
    a_rows, _ = _sub_layout(dil)
    p = np.arange(BLK)
    return p if dil == 1 else (VIEW // dil) * (p % a_rows) + p // a_rows


LB = 128
N_SLAB = NH // HS


RBS = 4


def _ld(refs, bs, s, w, rb=0):
    if bs is None:
        return refs[0][rb * BLK:(rb + 1) * BLK, s * w:(s + 1) * w]
    a_rows = refs[0].shape[0] // VIEW
    return jnp.concatenate([jnp.concatenate([ref[pl.ds(b, a_rows, stride=VIEW), :] for b in bs], axis=0)
                            for ref in refs[s * (w // LB):(s + 1) * (w // LB)]], axis=1)


def _st(ref, bs, s, val, rb=0):
    if bs is None:
        ref[rb * BLK:(rb + 1) * BLK, s * SLAB:(s + 1) * SLAB] = val.astype(ref.dtype)
        return
    a_rows = val.shape[0] // len(bs)
    for u, b in enumerate(bs):
        ref[:, b, s * SLAB:(s + 1) * SLAB] = val[u * a_rows:(u + 1) * a_rows]


def _attn_views(dil, S):
    a_rows, subs = _sub_layout(dil)
    if dil == 1:
        def ispecs(base, w, f):
            return [pl.BlockSpec((RBS * BLK, N_SLAB * w), lambda sg, n: (f(n), base // (N_SLAB * w)))]
        return subs, S // (RBS * BLK), N_SLAB, RBS, ispecs, (lambda w: (S, w)), (
            lambda f: pl.BlockSpec((RBS * BLK, AW), lambda sg, n: (f(n), 0)))

    sps = N_SLAB if dil < VIEW else 1

    def ispecs(base, w, f):
        return [pl.BlockSpec((a_rows * VIEW, LB), lambda sg, n, k=k: (f(n), (base + sg * sps * w) // LB + k))
                for k in range(sps * w // LB)]
    return subs, S // (a_rows * VIEW), sps, 1, ispecs, (lambda w: (S // VIEW, VIEW, w)), (
        lambda f: pl.BlockSpec((a_rows, VIEW, sps * SLAB), lambda sg, n: (f(n), 0, sg)))


def _attn_fwd(qkv_g, bias_tab, g):
    S = qkv_g.shape[0]
    subs, nbq, sps, rbs, ispecs, shape, ospec = _attn_views(GROUPS[g][1], S)
    cur = lambda n: n
    in_specs = [ispecs(0, SLAB, cur), ispecs(AW, SLAB, cur), ispecs(2 * AW, SLAB, cur)]
    nl = len(in_specs[0])

    def body(*refs):
        q, k, v = (refs[t * nl:(t + 1) * nl] for t in range(3))
        b_ref, o_ref, l_ref, kprev, vprev = refs[3 * nl:]
        n = pl.program_id(1)

        @pl.when(n == 0)
        def _():
            kprev[...] = jnp.zeros_like(kprev)
            vprev[...] = jnp.zeros_like(vprev)

        col = lax.broadcasted_iota(jnp.int32, (HS * BLK, 2 * BLK), 1)
        first = (col >= BLK) | (n > 0)
        for s_, rb, (i, bs) in ((s_, rb, sub) for s_ in range(sps) for rb in range(rbs) for sub in enumerate(subs)):
            cs = slice(s_ * SLAB, (s_ + 1) * SLAB)
            kc, vc = _ld(k, bs, s_, SLAB, rb).astype(BF16), _ld(v, bs, s_, SLAB, rb).astype(BF16)
            kb = jnp.concatenate([kprev[i, :, cs], kc], axis=0)
            vb = jnp.concatenate([vprev[i, :, cs], vc], axis=0)
            kprev[i, :, cs], vprev[i, :, cs] = kc, vc
            s = _dot_nt(_head_stack(_ld(q, bs, s_, SLAB, rb).astype(BF16)), kb) * (HD ** -0.5)
            s = s + b_ref[pl.ds(s_ * HS, HS)].reshape(HS * BLK, 2 * BLK)
            if rb == 0:
                s = jnp.where(first, s, NEG)
            m = jnp.max(s, axis=-1, keepdims=True)
            p = jnp.exp(s - m)
            den = jnp.sum(p, axis=-1, keepdims=True)
            _st(o_ref, bs, s_, _head_unstack(_dot(p.astype(BF16), vb) / den), rb)
            _st(l_ref, bs, s_, _head_unstack(jnp.broadcast_to(m + jnp.log(den), (HS * BLK, SLAB))), rb)

    out = _sds(shape(AW))
    nsg = N_SLAB // sps
    o, l = pl.pallas_call(
        body, name=f"attn_fwd{g}", grid=(nsg, nbq),
        in_specs=sum(in_specs, []) + [pl.BlockSpec((sps * HS, BLK, 2 * BLK), lambda sg, n: (g * nsg + sg, 0, 0))],
        out_specs=[ospec(cur), ospec(cur)],
        out_shape=[out, out],
        scratch_shapes=[pltpu.VMEM((len(subs), BLK, sps * SLAB), BF16)] * 2,
        compiler_params=_params("parallel", "arbitrary"),
    )(*([qkv_g] * (3 * nl)), bias_tab)
    return o.reshape(S, AW), l.reshape(S, AW)


def _attn_bwd(qkv_g, dattn, stats, bias_tab, g, ride):
    S = qkv_g.shape[0]
    subs, nbq, sps, rbs, ispecs, shape, ospec = _attn_views(GROUPS[g][1], S)
    cur = lambda n: jnp.minimum(n, nbq - 1)
    late = lambda n: jnp.maximum(n - 1, 0)
    in_specs = [ispecs(0, SLAB, cur), ispecs(AW, SLAB, cur), ispecs(2 * AW, SLAB, cur), ispecs(0, SLAB, cur),
                ispecs(0, STAT_W, cur)]
    nl = len(in_specs[0])

    def body(*refs):
        q, k, v, da = (refs[t * nl:(t + 1) * nl] for t in range(4))
        nst = len(in_specs[4])
        st_refs = refs[4 * nl:4 * nl + nst]
        b_ref, dq_ref, dk_ref, dv_ref, ds_ref, ck_ref, cv_ref, kprev, vprev, *held = refs[4 * nl + nst:]
        n = pl.program_id(1)

        @pl.when(n == 0)
        def _():
            for ref in (ds_ref, ck_ref, cv_ref, kprev, vprev, *held):
                ref[...] = jnp.zeros_like(ref)

        def finish(ref, t, bs, s_, rb, val):
            cs = slice(s_ * SLAB, (s_ + 1) * SLAB)
            if rbs == 1:
                _st(ref, bs, s_, val)
            elif rb == 0:
                for j in range(rbs - 1):
                    _st(ref, bs, s_, held[t][j * BLK:(j + 1) * BLK, cs], j)
                _st(ref, bs, s_, val, rbs - 1)
            else:
                held[t][(rb - 1) * BLK:rb * BLK, cs] = val

        @pl.when(n < nbq)
        def _():
            col = lax.broadcasted_iota(jnp.int32, (HS * BLK, 2 * BLK), 1)
            first = (col >= BLK) | (n > 0)
            for s_, rb, (i, bs) in ((s_, rb, sub) for s_ in range(sps) for rb in range(rbs) for sub in enumerate(subs)):
                cs = slice(s_ * SLAB, (s_ + 1) * SLAB)
                st = _ld(st_refs, bs, s_, STAT_W, rb)
                kc, vc = _ld(k, bs, s_, SLAB, rb).astype(BF16), _ld(v, bs, s_, SLAB, rb).astype(BF16)
                kb = jnp.concatenate([kprev[i, :, cs], kc], axis=0)
                vb = jnp.concatenate([vprev[i, :, cs], vc], axis=0)
                kprev[i, :, cs], vprev[i, :, cs] = kc, vc
                lse = jnp.concatenate([st[:, h:h + 1] for h in range(HS)], axis=0)
                delta = jnp.concatenate([st[:, HS + h:HS + h + 1] for h in range(HS)], axis=0)
                qs = _head_stack(_ld(q, bs, s_, SLAB, rb).astype(BF16))
                dos = _head_stack(_ld(da, bs, s_, SLAB, rb).astype(BF16))
                s = _dot_nt(qs, kb) * (HD ** -0.5) + b_ref[pl.ds(s_ * HS, HS)].reshape(HS * BLK, 2 * BLK)
                if rb == 0:
                    s = jnp.where(first, s, NEG)
                p = jnp.exp(s - lse)
                ds = p * (_dot_nt(dos, vb) - delta)
                ds_ref[pl.ds(s_ * HS, HS)] += ds.reshape(HS, BLK, 2 * BLK)
                ds_b = (ds * (HD ** -0.5)).astype(BF16)
                _st(dq_ref, bs, s_, _head_unstack(_dot(ds_b, kb)), rb)
                dkb = _dot_tn(ds_b, qs)
                dvb = _dot_tn(p.astype(BF16), dos)
                finish(dk_ref, 0, bs, s_, rb, ck_ref[i, :, cs] + dkb[:BLK])
                finish(dv_ref, 1, bs, s_, rb, cv_ref[i, :, cs] + dvb[:BLK])
                ck_ref[i, :, cs] = dkb[BLK:]
                cv_ref[i, :, cs] = dvb[BLK:]

        @pl.when(n == nbq)
        def _():
            for s_ in range(sps):
                for i, bs in enumerate(subs):
                    finish(dk_ref, 0, bs, s_, 0, ck_ref[i, :, s_ * SLAB:(s_ + 1) * SLAB])
                    finish(dv_ref, 1, bs, s_, 0, cv_ref[i, :, s_ * SLAB:(s_ + 1) * SLAB])

    out = _sds(shape(AW), BF16 if GROUPS[g][1] == 1 else F32)
    nsg = N_SLAB // sps
    (dq, dk, dv, ds_acc), rode = _call_with_ride(
        body, ride, lambda: (pl.program_id(0) == 0) & (pl.program_id(1) == 0),
        lambda: (pl.program_id(0) == nsg - 1) & (pl.program_id(1) == nbq),
        name=f"attn_bwd{g}", grid=(nsg, nbq + 1),
        in_specs=sum(in_specs, []) + [pl.BlockSpec((sps * HS, BLK, 2 * BLK), lambda sg, n: (g * nsg + sg, 0, 0))],
        out_specs=[ospec(cur), ospec(late), ospec(late),
                   pl.BlockSpec((sps * HS, BLK, 2 * BLK), lambda sg, n: (sg, 0, 0))],
        out_shape=[out] * 3 + [_sds((NH, BLK, 2 * BLK))],
        scratch_shapes=[pltpu.VMEM((len(subs), BLK, sps * SLAB), F32)] * 2
        + [pltpu.VMEM((len(subs), BLK, sps * SLAB), BF16)] * 2 + [pltpu.VMEM(((rbs - 1) * BLK, sps * SLAB), F32)] * (2 if rbs > 1 else 0),
        compiler_params=_params("arbitrary", "arbitrary"),
    )(*([qkv_g] * (3 * nl)), *([dattn] * nl), *([stats] * len(in_specs[4])), bias_tab)
    return [dq.reshape(S, AW), dk.reshape(S, AW), dv.reshape(S, AW)], ds_acc, rode


TM_MIX = 256


def _mix_specs(tm):
    row512 = pl.BlockSpec((tm, AW), lambda i: (i, 0))
    return ([row512] * 6 + [
        pl.BlockSpec((tm, REST_W), lambda i: (i, 0)),
        pl.BlockSpec((HALO, AW), lambda i: (jnp.maximum(i * (tm // HALO) - 1, 0), 1)),
        pl.BlockSpec((AW, D), lambda i: (0, 0)), pl.BlockSpec((AW, D), lambda i: (0, 0)),
        pl.BlockSpec((4, PGW, PGW), lambda i: (0, 0, 0)), pl.BlockSpec((1, AW), lambda i: (0, 0))])


def _mix_forward(i, tm, o_refs, l_refs, rest_ref, halo_ref, wab_ref, wpb_ref, pw_ref, ps_ref):
    l0, l1, l2 = (r[...] for r in l_refs)
    mx = jnp.maximum(jnp.maximum(l0, l1), l2)
    e0, e1, e2 = jnp.exp(l0 - mx), jnp.exp(l1 - mx), jnp.exp(l2 - mx)
    den = e0 + e1 + e2
    lj = mx + jnp.log(den)
    attn = (e0 * o_refs[0][...] + e1 * o_refs[1][...] + e2 * o_refs[2][...]) / den

    z_attn = rest_ref[:, 0:AW]
    u = rest_ref[:, AW:2 * AW]
    z_pool = rest_ref[:, 2 * AW:3 * AW]
    g_attn = rest_ref[:, 3 * AW:3 * AW + D]
    g_pool = rest_ref[:, 3 * AW + D:3 * AW + 2 * D]

    sg_a = _sigmoid(z_attn)
    sil_a = z_attn * sg_a
    a_g = (attn * sil_a).astype(BF16)
    y_attn = _dot(a_g, wab_ref[...])

    halo = jnp.where(i > 0, halo_ref[...], 0.0)
    ext = jnp.concatenate([halo, u], axis=0)
    t = i * tm + lax.broadcasted_iota(jnp.int32, (tm, 1), 0)
    pooled, mixed_raw = [], []
    for gi, win in enumerate(POOL_WINDOWS):
        s = ext[:, gi * PGW:(gi + 1) * PGW]
        sh = 1
        while sh < win:
            s = s + pltpu.roll(s, sh, 0)
            sh *= 2
        cnt = jnp.minimum(t + 1, win).astype(F32)
        pg = s[HALO:] / cnt - u[:, gi * PGW:(gi + 1) * PGW]
        pooled.append(pg.astype(BF16))
        mixed_raw.append(_dot(pooled[-1], pw_ref[gi].astype(BF16)))
    mixed_raw = jnp.concatenate(mixed_raw, axis=1)
    mixed = mixed_raw * ps_ref[...]
    sg_p = _sigmoid(z_pool)
    sil_p = z_pool * sg_p
    m_g = (mixed * sil_p).astype(BF16)
    y_pool = _dot(m_g, wpb_ref[...])

    sa = _sigmoid(g_attn)
    sp = _sigmoid(g_pool)
    merged = sa * y_attn + sp * y_pool
    return dict(lj=lj, attn=attn, z_attn=z_attn, z_pool=z_pool, sg_a=sg_a, sil_a=sil_a, a_g=a_g, y_attn=y_attn,
                pooled=pooled, mixed_raw=mixed_raw, mixed=mixed, sg_p=sg_p, sil_p=sil_p, m_g=m_g, y_pool=y_pool,
                sa=sa, sp=sp, merged=merged)


def _mix_step(x, target, os_, ls_, rest, wab, wpb, pool_w, pool_scale, wout, mod, final_g):
    S = x.shape[0]
    tm = TM_MIX
    nt = S // tm
    sw = D // N_SHARD

    def body(o0, o1, o2, l0, l1, l2, rest_ref, halo_ref, wab_ref, wpb_ref, pw_ref, ps_ref,
             x_ref, t_ref, wo_ref, mod_ref, fg_ref, dx2_ref, loss_ref, dfg_ref, dgate_ref,
             dattn_ref, stats_ref, dpooled_ref, dproj_hbm, dwo_hbm, dwab_hbm, dwpb_hbm, dpw_ref, dps_ref,
             awo, awab, awpb, stage, stage_sem):
        i = pl.program_id(0)
        slot = i % 2

        def staged(step, sl):
            return pltpu.make_async_copy(stage.at[sl], dproj_hbm.at[pl.ds(step * tm, tm), pl.ds(QKV_W, REST_W)],
                                         stage_sem.at[sl])

        @pl.when(i == 0)
        def _():
            for ref in (loss_ref, dfg_ref, dgate_ref, awo, awab, awpb, dpw_ref, dps_ref):
                ref[...] = jnp.zeros_like(ref)

        f = _mix_forward(i, tm, (o0, o1, o2), (l0, l1, l2), rest_ref, halo_ref, wab_ref, wpb_ref, pw_ref, ps_ref)
        mo = _dot(f["merged"].astype(BF16), wo_ref[...])
        gate = mod_ref[:, 2 * D:3 * D]
        fg = fg_ref[...]
        x2 = x_ref[...] + gate * mo
        r2 = lax.rsqrt(jnp.mean(x2 * x2, axis=-1, keepdims=True) + EPS)
        n2 = x2 * r2
        err = n2 * fg - t_ref[...]
        loss_ref[...] += 0.5 * jnp.sum(jnp.mean(err * err, axis=-1, keepdims=True))
        dy = err * (1.0 / D)
        dfg_ref[...] += jnp.sum(dy * n2, axis=0, keepdims=True)
        dn = dy * fg
        dx2 = r2 * (dn - n2 * jnp.mean(dn * n2, axis=-1, keepdims=True))
        dgate_ref[...] += jnp.sum(dx2 * mo, axis=0, keepdims=True)
        dx2_ref[...] = dx2

        dmo_b = (dx2 * gate).astype(BF16)
        dmerged = _dot_nt(dmo_b, wo_ref[...])
        awo[...] += _dot_tn(f["merged"].astype(BF16), dmo_b)
        sa, sp = f["sa"], f["sp"]
        dya = (dmerged * sa).astype(BF16)
        dyp = (dmerged * sp).astype(BF16)
        dg_attn = dmerged * f["y_attn"] * sa * (1.0 - sa)
        dg_pool = dmerged * f["y_pool"] * sp * (1.0 - sp)
        dag = _dot_nt(dya, wab_ref[...])
        awab[...] += _dot_tn(f["a_g"], dya)
        dmg = _dot_nt(dyp, wpb_ref[...])
        awpb[...] += _dot_tn(f["m_g"], dyp)
        dattn = dag * f["sil_a"]
        dattn_ref[...] = dattn
        prod = dattn * f["attn"]
        lane = lax.broadcasted_iota(jnp.int32, (tm, STAT_W), 1)
        for sb in range(N_SLAB):
            st = jnp.zeros((tm, STAT_W), F32)
            for h in range(HS):
                hs = slice((sb * HS + h) * HD, (sb * HS + h + 1) * HD)
                st = jnp.where(lane == h, f["lj"][:, hs.start:hs.start + 1], st)
                st = jnp.where(lane == HS + h, jnp.sum(prod[:, hs], axis=-1, keepdims=True), st)
            stats_ref[:, sb * STAT_W:(sb + 1) * STAT_W] = st
        dz_attn = dag * f["attn"] * (f["sg_a"] * (1.0 + f["z_attn"] * (1.0 - f["sg_a"])))
        dmixed = dmg * f["sil_p"]
        dz_pool = dmg * f["mixed"] * (f["sg_p"] * (1.0 + f["z_pool"] * (1.0 - f["sg_p"])))
        dps_ref[...] += jnp.sum(dmixed * f["mixed_raw"], axis=0, keepdims=True)
        dpm = (dmixed * ps_ref[...]).astype(BF16)
        for gi in range(len(POOL_WINDOWS)):
            cs = slice(gi * PGW, (gi + 1) * PGW)
            dpw_ref[gi] += _dot_tn(f["pooled"][gi], dpm[:, cs])
            dpooled_ref[:, cs] = _dot_nt(dpm[:, cs], pw_ref[gi].astype(BF16))
        @pl.when(i >= 2)
        def _():
            staged(i - 2, slot).wait()

        stage[slot, :, 0:AW] = dz_attn.astype(BF16)
        stage[slot, :, AW:2 * AW] = jnp.zeros((tm, AW), BF16)
        stage[slot, :, 2 * AW:3 * AW] = dz_pool.astype(BF16)
        stage[slot, :, 3 * AW:3 * AW + D] = dg_attn.astype(BF16)
        stage[slot, :, 3 * AW + D:3 * AW + 2 * D] = dg_pool.astype(BF16)
        staged(i, slot).start()

        @pl.when(i == nt - 1)
        def _():
            staged(i - 1, 1 - slot).wait()
            staged(i, slot).wait()
            pltpu.sync_copy(awo, dwo_hbm)
            for k in range(N_SHARD):
                pltpu.sync_copy(awab.at[:, pl.ds(k * sw, sw)], dwab_hbm.at[k])
                pltpu.sync_copy(awpb.at[:, pl.ds(k * sw, sw)], dwpb_hbm.at[k])

    row = pl.BlockSpec((tm, D), lambda i: (i, 0))
    vec = pl.BlockSpec((1, D), lambda i: (0, 0))
    row512 = pl.BlockSpec((tm, AW), lambda i: (i, 0))
    outs = pl.pallas_call(
        body, name="mix_step", grid=(nt,),
        in_specs=_mix_specs(tm) + [row, row, pl.BlockSpec((D, D), lambda i: (0, 0)),
                                   pl.BlockSpec((1, 3 * D), lambda i: (0, 0)), vec],
        out_specs=[row, pl.BlockSpec((8, 128), lambda i: (0, 0)), vec, vec,
                   row512, pl.BlockSpec((tm, N_SLAB * STAT_W), lambda i: (i, 0)), row512, ANY, ANY, ANY, ANY,
                   pl.BlockSpec((4, PGW, PGW), lambda i: (0, 0, 0)), pl.BlockSpec((1, AW), lambda i: (0, 0))],
        out_shape=[_sds((S, D)), _sds((8, 128)), _sds((1, D)), _sds((1, D)),
                   _sds((S, AW)), _sds((S, N_SLAB * STAT_W)), _sds((S, AW)), _sds((S, IN_W), BF16),
                   _sds((D, D)), _sds((N_SHARD, AW, sw)), _sds((N_SHARD, AW, sw)), _sds((4, PGW, PGW)), _sds((1, AW))],
        scratch_shapes=[pltpu.VMEM((D, D), F32), pltpu.VMEM((AW, D), F32), pltpu.VMEM((AW, D), F32),
                        pltpu.VMEM((2, tm, REST_W), BF16), _dma_sems(2)],
        compiler_params=_params("arbitrary"),
    )(*os_, *ls_, rest, rest, wab, wpb, pool_w, pool_scale, x, target, wout, mod, final_g)
    dx2, loss, dfg, dgate, dattn, stats, dpooled, dproj, dwo, dwab, dwpb, dpw, dps = outs
    return (dx2, loss, dfg, dgate, dattn, stats, dpooled, dproj, dwo.reshape(N_SHARD, D // N_SHARD, D), dwab, dwpb,
            dpw, dps)


def _pool_bwd(dpooled):
    S = dpooled.shape[0]
    tm = 1024
    nt = S // tm

    def body(dp_ref, nxt_ref, du_ref):
        i = pl.program_id(0)
        t = i * tm + lax.broadcasted_iota(jnp.int32, (tm + HALO, 1), 0)
        nxt = jnp.where(i < nt - 1, nxt_ref[...], 0.0)
        ext = jnp.concatenate([dp_ref[...], nxt], axis=0)
        for gi, win in enumerate(POOL_WINDOWS):
            cs = slice(gi * PGW, (gi + 1) * PGW)
            s = ext[:, cs] / jnp.minimum(t + 1, win).astype(F32)
            sh = 1
            while sh < win:
                s = s + pltpu.roll(s, tm + HALO - sh, 0)
                sh *= 2
            du_ref[:, cs] = (s[:tm] - dp_ref[:, cs]).astype(BF16)

    return pl.pallas_call(
        body, name="pool_bwd", grid=(nt,),
        in_specs=[pl.BlockSpec((tm, AW), lambda i: (i, 0)),
                  pl.BlockSpec((HALO, AW), lambda i: (jnp.minimum((i + 1) * (tm // HALO), S // HALO - 1), 0))],
        out_specs=pl.BlockSpec((tm, AW), lambda i: (i, 0)),
        out_shape=_sds((S, AW), BF16), compiler_params=_params("parallel"),
    )(dpooled, dpooled)


TB = 1024


def _dh_prenorm_bwd(dproj, wg_in, x, dx2, norm_g, mod, ride):
    S = dproj.shape[0]
    per = wg_in.shape[2] // TB
    nm, nk = S // TB, IN_W // TB
    rows = 256

    def body(dp_ref, w_ref, x_ref, dx2_ref, g_ref, mod_ref, gx_ref, dg_ref, dshift_ref, dscale_ref, dh_ref):
        m, kk = pl.program_id(0), pl.program_id(1)

        @pl.when(kk == 0)
        def _():
            dh_ref[...] = jnp.zeros_like(dh_ref)

        @pl.when((m == 0) & (kk == 0))
        def _():
            dg_ref[...] = jnp.zeros_like(dg_ref)
            dshift_ref[...] = jnp.zeros_like(dshift_ref)
            dscale_ref[...] = jnp.zeros_like(dscale_ref)

        dh_ref[...] += _dot_nt(dp_ref[...], w_ref[...])

        @pl.when(kk == nk - 1)
        def _():
            g = g_ref[...]
            for c in range(TB // rows):
                sl = pl.ds(c * rows, rows)
                xv = x_ref[sl, :]
                dhv = dh_ref[sl, :]
                r = lax.rsqrt(jnp.mean(xv * xv, axis=-1, keepdims=True) + EPS)
                xh = xv * r
                dshift_ref[...] += jnp.sum(dhv, axis=0, keepdims=True)
                dscale_ref[...] += jnp.sum(dhv * (xh * g), axis=0, keepdims=True)
                dn1 = dhv * (1.0 + mod_ref[:, D:2 * D])
                dg_ref[...] += jnp.sum(dn1 * xh, axis=0, keepdims=True)
                dxh = dn1 * g
                gx_ref[sl, :] = dx2_ref[sl, :] + r * (dxh - xh * jnp.mean(dxh * xh, axis=-1, keepdims=True))

    row = pl.BlockSpec((TB, D), lambda m, kk: (m, 0))
    vec = pl.BlockSpec((1, D), lambda m, kk: (0, 0))
    outs, rode = _call_with_ride(
        body, ride, lambda: (pl.program_id(0) == 0) & (pl.program_id(1) == 0),
        lambda: (pl.program_id(0) == nm - 1) & (pl.program_id(1) == nk - 1),
        name="dh", grid=(nm, nk),
        in_specs=[pl.BlockSpec((TB, TB), lambda m, kk: (m, kk)),
                  pl.BlockSpec((None, D, TB), lambda m, kk: (kk // per, 0, kk % per)),
                  row, row, vec, pl.BlockSpec((1, 3 * D), lambda m, kk: (0, 0))],
        out_specs=[row, vec, vec, vec],
        out_shape=[_sds((S, D)), _sds((1, D)), _sds((1, D)), _sds((1, D))],
        scratch_shapes=[pltpu.VMEM((TB, D), F32)], compiler_params=_params("arbitrary", "arbitrary"),
    )(dproj, wg_in, x, dx2, norm_g, mod)
    return outs, rode


def _dw_in(h, dproj):
    S = dproj.shape[0]
    per = IN_W // N_SHARD // TB

    def body(h_ref, dp_ref, out_ref):
        out_ref[...] = _dot_tn(h_ref[...], dp_ref[...])

    return pl.pallas_call(
        body, name="dw_in", grid=(IN_W // TB,),
        in_specs=[pl.BlockSpec((S, D), lambda j: (0, 0)), pl.BlockSpec((S, TB), lambda j: (0, j))],
        out_specs=pl.BlockSpec((None, D, TB), lambda j: (j // per, 0, j % per)),
        out_shape=_sds((N_SHARD, D, IN_W // N_SHARD)), compiler_params=_params("parallel"),
    )(h, dproj)


def _local_step(x, target, mod, wg_in, wab, wpb, wout, pool_w, pool_scale, rel_bias, norm_g, final_g, chip_half):
    buckets = jnp.asarray(_bucket_tables())
    bias_tab = _bias_table(rel_bias, buckets)
    h = _prenorm(x, norm_g, mod)
    qkv = [_proj(h, wg_in, 3 * g, 3, BF16 if GROUPS[g][1] == 1 else F32, f"proj_qkv{g}") for g in range(NG)]
    rest = _proj(h, wg_in, NCB_QKV, REST_W // CB, F32, "proj_rest")
    os_, ls_ = zip(*[_attn_fwd(qkv[g], bias_tab, g) for g in range(NG)])
    (dx2, loss, dfinal_g, dgate, dattn, stats, dpooled, dproj, dw_out, dw_ab, dw_pb, dpool_w,
     dpool_scale) = _mix_step(x, target, os_, ls_, rest, wab, wpb, pool_w, pool_scale, wout, mod, final_g)
    du = _pool_bwd(dpooled)

    small = [dw_ab, dw_pb, dw_out]
    dqkv0, ds0, sib_small = _attn_bwd(qkv[0], dattn, stats, bias_tab, 0, _ride_sibling_halves(small))
    p_small = _pair_sum_small(small, sib_small, chip_half)
    dqkv1, ds1, u_small = _attn_bwd(qkv[1], dattn, stats, bias_tab, 1,
                                    _ride_chip_exchange([p16 for _, p16 in p_small]))
    rs_ab, rs_pb, rs_out = _chip_sum_small([p32 for p32, _ in p_small], u_small, chip_half)
    dqkv2, ds2, _ = _attn_bwd(qkv[2], dattn, stats, bias_tab, 2, None)

    for j, piece in enumerate(dqkv0 + dqkv1 + dqkv2):
        dproj = lax.dynamic_update_slice(dproj, piece.astype(BF16), (0, j * AW))
    dproj = lax.dynamic_update_slice(dproj, du, (0, QKV_W + AW))
    dw_in = _dw_in(h, dproj)
    drel_rows, (sib_in,) = _bias_grad(jnp.concatenate([ds0, ds1, ds2], axis=0), buckets,
                                      _ride_sibling_halves([dw_in]))
    drel = drel_rows[:, 0, :NUM_BUCKETS].T
    p32_in, p16_in = _pair_sum(dw_in, sib_in, chip_half, "rs_pair_sum_in")
    (grad_x, dnorm_g, dshift, dscale), (u_in,) = _dh_prenorm_bwd(dproj, wg_in, x, dx2, norm_g, mod,
                                                                 _ride_chip_exchange([p16_in]))
    rs_in = _chip_sum(p32_in, u_in, chip_half, "rs_chip_sum_in")
    dmod = jnp.concatenate([dshift, dscale, dgate], axis=1)
    return dict(loss=loss[0, 0], grad_x=grad_x, dmod=dmod, dnorm_g=dnorm_g, dfinal_g=dfinal_g, dpool_w=dpool_w,
                dpool_scale=dpool_scale, drel_bias=drel, dw_in=dw_in, dw_attn_br=dw_ab, dw_pool_br=dw_pb,
                dw_out=dw_out, rs_in=rs_in, rs_attn_br=rs_ab, rs_pool_br=rs_pb, rs_out=rs_out)


def _allgather8(blocks, name, relay=None):
    nb = len(blocks)
    relay = [False] * nb if relay is None else list(relay)

    def body(*refs):
        ins, outs = refs[:nb], refs[nb:2 * nb]
        send_sems, recv_sems = refs[2 * nb:]
        x, y, c = lax.axis_index("x"), lax.axis_index("y"), lax.axis_index("c")
        me, sibling = (x, y, c), (x, y, 1 - c)
        here, xn, yn, dg = (x, y), (1 - x, y), (x, 1 - y), (1 - x, 1 - y)

        def slot(a, chip, core, half=None):
            ref = outs[a].at[4 * chip[0] + 2 * chip[1] + core]
            if half is None:
                return ref
            r2 = ref.shape[0] // 2
            return ref.at[pl.ds(half * r2, r2)]

        def copy(a, k, dst, to, src=None):
            return pltpu.make_async_remote_copy(src_ref=dst if src is None else src, dst_ref=dst,
                                                send_sem=send_sems.at[a, k], recv_sem=recv_sems.at[a, k],
                                                device_id=to, device_id_type=MESH)

        def start(cps):
            for cp in cps:
                cp.start()
            return cps

        sent = []
        for a in range(nb):
            own = slot(a, here, c)
            sent += [copy(a, 0, own, sibling, src=ins[a]), copy(a, 1, own, (*xn, c), src=ins[a]),
                     copy(a, 2, own, (*yn, c), src=ins[a])]
            if not relay[a]:
                sent.append(copy(a, 3, own, (*dg, c), src=ins[a]))
        start(sent)
        for a in range(nb):
            copy(a, 2, slot(a, yn, c), me).wait_recv()
            sent += start([copy(a, 6, slot(a, yn, c), sibling)]
                          + ([copy(a, 3, slot(a, yn, c, 0), (*xn, c))] if relay[a] else []))
        for a in range(nb):
            copy(a, 1, slot(a, xn, c), me).wait_recv()
            sent += start([copy(a, 5, slot(a, xn, c), sibling)]
                          + ([copy(a, 4, slot(a, xn, c, 1), (*yn, c))] if relay[a] else []))
        for a in range(nb):
            for k, half in ((3, 0), (4, 1)) if relay[a] else ((3, None),):
                copy(a, k, slot(a, dg, c, half), me).wait_recv()
                sent += start([copy(a, 4 + k, slot(a, dg, c, half), sibling)])
        for a in range(nb):
            copy(a, 0, slot(a, here, 1 - c), me).wait_recv()
            copy(a, 5, slot(a, xn, 1 - c), me).wait_recv()
            copy(a, 6, slot(a, yn, 1 - c), me).wait_recv()
            for k, half in ((7, 0), (8, 1)) if relay[a] else ((7, None),):
                copy(a, k, slot(a, dg, 1 - c, half), me).wait_recv()
        for cp in sent:
            cp.wait_send()

    outs = pl.pallas_call(
        body, name=name, in_specs=[ANY] * nb, out_specs=[ANY] * nb,
        out_shape=[_sds((8,) + b.shape, b.dtype) for b in blocks],
        scratch_shapes=[_dma_sems(nb, 9), _dma_sems(nb, 9)],
    )(*blocks)
    return [_place_own(buf, b) for buf, b in zip(outs, blocks)]


def _place_own(buf, block):
    dev = 4 * lax.axis_index("x") + 2 * lax.axis_index("y") + lax.axis_index("c")
    return lax.dynamic_update_index_in_dim(buf, block, dev, 0)


def _ride_sibling_halves(gs):
    def copies(ins, outs, send_sems, recv_sems):
        x, y, c = lax.axis_index("x"), lax.axis_index("y"), lax.axis_index("c")
        cps = []
        for a in range(len(gs)):
            r2 = ins[a].shape[1] // 2
            other = ins[a].at[:, pl.ds((1 - c) * r2, r2), :]
            cps.append(pltpu.make_async_remote_copy(src_ref=other, dst_ref=outs[a], send_sem=send_sems.at[a],
                                                    recv_sem=recv_sems.at[a], device_id=(x, y, 1 - c),
                                                    device_id_type=MESH))
        return cps

    return _Ride(gs, [_sds((g.shape[0], g.shape[1] // 2, g.shape[2]), g.dtype) for g in gs], len(gs), copies)


def _pair_sum(g, t, chip_half, name):
    nsh, rows, cols = g.shape
    r2 = rows // 2
    tr = _row_tile(r2, cols)
    nt = r2 // tr

    def body(ch_ref, g_ref, t_ref, p32_ref, p16_ref):
        p = g_ref[...] + t_ref[...]
        p16_ref[...] = p.astype(BF16)

        @pl.when(pl.program_id(1) == ch_ref[0])
        def _():
            p32_ref[...] = p

    blk = pl.BlockSpec((None, tr, cols), lambda i, k, ch_ref: (k, i, 0))
    return pl.pallas_call(
        body, name=name,
        grid_spec=pltpu.PrefetchScalarGridSpec(
            num_scalar_prefetch=1, grid=(nt, nsh),
            in_specs=[pl.BlockSpec((None, tr, cols), lambda i, k, ch_ref: (k, ch_ref[1] * nt + i, 0)), blk],
            out_specs=[pl.BlockSpec((tr, cols), lambda i, k, ch_ref: (i, 0)), blk]),
        out_shape=[_sds((r2, cols)), _sds((nsh, r2, cols), BF16)],
        compiler_params=_params("parallel", "arbitrary"),
    )(chip_half, g, t)


def _pair_sum_small(gs, ts, chip_half):
    na = len(gs)

    def body(ch_ref, *refs):
        g_refs, t_refs, outs = refs[:na], refs[na:2 * na], refs[2 * na:]
        for a in range(na):
            r2 = t_refs[a].shape[1]
            own = pl.ds(pl.multiple_of(ch_ref[1] * r2, 8), r2)
            outs[2 * a + 1][...] = (g_refs[a][:, own, :] + t_refs[a][...]).astype(BF16)
            outs[2 * a][...] = g_refs[a][ch_ref[0], own, :] + t_refs[a][ch_ref[0]]

    res = pl.pallas_call(
        body, name="rs_pair_sum_small",
        in_specs=[pl.BlockSpec(memory_space=pltpu.SMEM)] + [pl.BlockSpec(memory_space=pltpu.VMEM)] * (2 * na),
        out_shape=[s for t in ts for s in (_sds(t.shape[1:]), _sds(t.shape, BF16))], compiler_params=_params(),
    )(chip_half, *gs, *ts)
    return [(res[2 * a], res[2 * a + 1]) for a in range(na)]


def _chip_sum_small(p32s, us, chip_half):
    na = len(p32s)

    def body(ch_ref, *refs):
        p_refs, u_refs, outs = refs[:na], refs[na:2 * na], refs[2 * na:]
        for a in range(na):
            r2 = p_refs[a].shape[0]
            acc = p_refs[a][...]
            for j in range(3):
                acc = acc + u_refs[a][j].astype(F32)
            outs[a][pl.ds(pl.multiple_of(ch_ref[1] * r2, 8), r2), :] = acc

    return pl.pallas_call(
        body, name="rs_chip_sum_small",
        in_specs=[pl.BlockSpec(memory_space=pltpu.SMEM)] + [pl.BlockSpec(memory_space=pltpu.VMEM)] * (2 * na),
        out_shape=[_sds((2 * p.shape[0], p.shape[1])) for p in p32s], compiler_params=_params(),
    )(chip_half, *p32s, *us)


def _ride_chip_exchange(ps):
    def copies(ins, outs, send_sems, recv_sems):
        x, y, c = lax.axis_index("x"), lax.axis_index("y"), lax.axis_index("c")
        chips = [(1 - x, y), (x, 1 - y), (1 - x, 1 - y)]
        cps = []
        for a in range(len(ps)):
            for j, (ox, oy) in enumerate(chips):
                cps.append(pltpu.make_async_remote_copy(src_ref=ins[a].at[2 * ox + oy], dst_ref=outs[a].at[j],
                                                        send_sem=send_sems.at[3 * a + j],
                                                        recv_sem=recv_sems.at[3 * a + j],
                                                        device_id=(ox, oy, c), device_id_type=MESH))
        return cps

    return _Ride(ps, [_sds((3,) + p.shape[1:], p.dtype) for p in ps], 3 * len(ps), copies)


def _chip_sum(p32, u, chip_half, name):
    r2, cols = p32.shape
    tr = _row_tile(r2, cols)
    nt = r2 // tr

    def body(ch_ref, p_ref, u_ref, o_ref):
        acc = p_ref[...]
        for j in range(3):
            acc = acc + u_ref[j].astype(F32)
        o_ref[...] = acc

    return pl.pallas_call(
        body, name=name,
        grid_spec=pltpu.PrefetchScalarGridSpec(
            num_scalar_prefetch=1, grid=(nt,),
            in_specs=[pl.BlockSpec((tr, cols), lambda i, ch_ref: (i, 0)),
                      pl.BlockSpec((3, tr, cols), lambda i, ch_ref: (0, i, 0))],
            out_specs=pl.BlockSpec((tr, cols), lambda i, ch_ref: (ch_ref[1] * nt + i, 0))),
        out_shape=_sds((2 * r2, cols)), compiler_params=_params("parallel"),
    )(chip_half, p32, u)


def _sibling_join(fs, name):
    nb = len(fs)

    def body(*refs):
        outs = refs[nb:2 * nb]
        send_sems, recv_sems = refs[2 * nb:]
        x, y, c = lax.axis_index("x"), lax.axis_index("y"), lax.axis_index("c")
        cps = []
        for a in range(nb):
            r2 = outs[a].shape[0] // 2
            rows = outs[a].at[pl.ds(c * r2, r2), :]
            cps.append(pltpu.make_async_remote_copy(src_ref=rows, dst_ref=rows, send_sem=send_sems.at[a],
                                                    recv_sem=recv_sems.at[a], device_id=(x, y, 1 - c),
                                                    device_id_type=MESH))
        for cp in cps:
            cp.start()
        for cp in cps:
            cp.wait()

    return pl.pallas_call(
        body, name=name, in_specs=[ANY] * nb, out_specs=[ANY] * nb,
        out_shape=[_sds(f.shape, f.dtype) for f in fs],
        input_output_aliases={a: a for a in range(nb)},
        scratch_shapes=[_dma_sems(nb), _dma_sems(nb)],
    )(*fs)


def _row_tile(rows, cols):
    tile = rows
    while tile * cols * 4 > (2 << 20) and tile % 16 == 0:
        tile //= 2
    return tile


def _w_ada_grad(c_all, dmod_cols):
    def body(c_ref, d_ref, o_ref):
        o_ref[...] = _dot_tn(c_ref[...].astype(BF16), d_ref[...].astype(BF16))

    return pl.pallas_call(body, name="w_ada_grad", out_shape=_sds((c_all.shape[1], dmod_cols.shape[1])),
                          compiler_params=_params())(c_all, dmod_cols)


def _adam_math(w, g, m, v):
    nm = ADAM_B1 * m + (1.0 - ADAM_B1) * g
    nv = ADAM_B2 * v + (1.0 - ADAM_B2) * (g * g)
    m_hat = nm / (1.0 - ADAM_B1 ** ADAM_STEP)
    v_hat = nv / (1.0 - ADAM_B2 ** ADAM_STEP)
    return -ADAM_LR * (m_hat / (jnp.sqrt(v_hat) + ADAM_EPS) + ADAM_WD * w), nm, nv


def _adamw(w, g, m, v, name):
    rows, cols = w.shape
    tr = _row_tile(rows, cols)

    def body(w_ref, g_ref, m_ref, v_ref, go_ref, d_ref, nm_ref, nv_ref):
        gv = g_ref[...]
        go_ref[...] = gv
        d_ref[...], nm_ref[...], nv_ref[...] = _adam_math(w_ref[...], gv, m_ref[...], v_ref[...])

    spec = pl.BlockSpec((tr, cols), lambda i: (i, 0))
    return pl.pallas_call(
        body, name=name, grid=(rows // tr,), in_specs=[spec] * 4, out_specs=[spec] * 4,
        out_shape=[_sds((rows, cols))] * 4, compiler_params=_params("parallel"),
    )(w, g, m, v)


def _pack_small(dmod, dnorm_g, dfinal_g, dpool_scale, drel_bias, loss, dpool_w):
    return jnp.concatenate([dmod.reshape(-1, 128), dnorm_g.reshape(-1, 128), dfinal_g.reshape(-1, 128),
                            jnp.pad(dpool_scale.reshape(-1, 128), ((0, PK_RELB - PK_PSCALE - AW // 128), (0, 0))),
                            jnp.pad(drel_bias, ((0, 0), (0, 128 - NG * NH))),
                            jnp.full((PK_POOLW - PK_LOSS, 128), loss, F32), dpool_w.reshape(-1, 128)], axis=0)


def _small_update(small_all, ws, ms, vs):
    lane_rows = [(r0, r0 + w.shape[1] // 128) for r0, w in zip((PK_BADA, PK_NORMG, PK_FINALG, PK_PSCALE), ws)]
    nw = len(ws)

    def body(all_ref, *refs):
        w_refs, m_refs, v_refs = refs[:nw], refs[nw:2 * nw], refs[2 * nw:3 * nw]
        loss_ref, outs = refs[3 * nw], refs[3 * nw + 1:]
        g = all_ref[0]
        for s in range(1, all_ref.shape[0]):
            g = g + all_ref[s]
        loss_ref[...] = jnp.broadcast_to(g[PK_LOSS:PK_LOSS + 1, :], loss_ref.shape)

        def put(p, at, gv):
            d, nm, nv = _adam_math(w_refs[p][at], gv, m_refs[p][at], v_refs[p][at])
            for o_ref, val in zip(outs[4 * p:4 * p + 4], (gv, d, nm, nv)):
                o_ref[at] = val

        for p, (r0, r1) in enumerate(lane_rows):
            for i in range(r1 - r0):
                put(p, (slice(None), slice(128 * i, 128 * (i + 1))), g[r0 + i:r0 + i + 1, :])
        put(4, (slice(None), slice(None)), g[PK_RELB:PK_LOSS, 0:NG * NH])
        put(5, (slice(None), slice(None)), g[PK_POOLW:PK_ROWS, :])

    res = pl.pallas_call(
        body, name="small_update",
        out_shape=[_sds((8, 128))] + [_sds(w.shape) for w in ws for _ in range(4)], compiler_params=_params(),
    )(small_all, *ws, *ms, *vs)
    return res[0], [res[1 + 4 * p:5 + 4 * p] for p in range(nw)]


def kernel(x, c, norm_g, w_ada, b_ada, w_in, pool_w, pool_scale, w_attn_br, w_pool_br, w_out, rel_bias, final_g, loss_target, m_norm_g, m_w_ada, m_b_ada, m_w_in, m_pool_w, m_pool_scale, m_w_attn_br, m_w_pool_br, m_w_out, m_rel_bias, m_final_g, v_norm_g, v_w_ada, v_b_ada, v_w_in, v_pool_w, v_pool_scale, v_w_attn_br, v_w_pool_br, v_w_out, v_rel_bias, v_final_g):
    ix, iy, ic = lax.axis_index("x"), lax.axis_index("y"), lax.axis_index("c")
    dev = 4 * ix + 2 * iy + ic
    chip = 2 * ix + iy

    def half(w):
        r2 = w.shape[0] // 2
        return lax.dynamic_slice_in_dim(w, ic * r2, r2, axis=0).astype(BF16)

    gathered = _allgather8([jnp.broadcast_to(c, (8, D)), half(w_in[0]), half(w_attn_br[0]), half(w_pool_br[0]),
                            half(w_out[0])], "gather_weights", relay=[False, True, True, True, True])
    c_all = gathered[0][:, 0, :]
    wg_in = gathered[1].reshape(N_SHARD, D, IN_W // N_SHARD)
    wab = gathered[2].reshape(N_SHARD, AW, D // N_SHARD).transpose(1, 0, 2).reshape(AW, D)
    wpb = gathered[3].reshape(N_SHARD, AW, D // N_SHARD).transpose(1, 0, 2).reshape(AW, D)
    wout = gathered[4].reshape(D, D)

    mw = 3 * D // N_SHARD
    modp = _mod_partial(c_all, w_ada[0], lax.dynamic_slice_in_dim(b_ada, chip * mw, mw, axis=1))
    mod_all = _allgather8([modp], "gather_mod")[0]
    mod_full = mod_all[::2].transpose(1, 0, 2).reshape(8, 3 * D)
    mod = lax.dynamic_slice_in_dim(mod_full, dev, 1, axis=0)

    chip_half = jnp.stack([chip, ic]).astype(jnp.int32)
    r = _local_step(x[0], loss_target[0], mod, wg_in, wab, wpb, wout, pool_w[0], pool_scale, rel_bias, norm_g,
                    final_g.reshape(1, D), chip_half)

    packed = _pack_small(r["dmod"], r["dnorm_g"], r["dfinal_g"], r["dpool_scale"], r["drel_bias"], r["loss"],
                         r["dpool_w"])
    small_all = _allgather8([packed], "gather_small")[0]
    small = ["b_ada", "norm_g", "final_g", "pool_scale", "rel_bias", "pool_w"]
    shaped = lambda b, n, f, ps, rb, pw: [b, n, f.reshape(1, D), ps, rb, pw.reshape(4 * PGW, PGW)]
    loss, small_out = _small_update(small_all, shaped(b_ada, norm_g, final_g, pool_scale, rel_bias, pool_w),
                                    shaped(m_b_ada, m_norm_g, m_final_g, m_pool_scale, m_rel_bias, m_pool_w),
                                    shaped(v_b_ada, v_norm_g, v_final_g, v_pool_scale, v_rel_bias, v_pool_w))
    dmod_all = small_all[:, PK_BADA:PK_NORMG, :].reshape(8, 3 * D)
    g_w_ada = _w_ada_grad(c_all, lax.dynamic_slice_in_dim(dmod_all, chip * mw, mw, axis=1))

    g_w_in, g_w_ab, g_w_pb, g_w_out = _sibling_join([r["rs_in"], r["rs_attn_br"], r["rs_pool_br"], r["rs_out"]],
                                                    "rs_sibling_join")
    upd = dict(zip(small, small_out))
    upd["final_g"] = [a.reshape(D) for a in upd["final_g"]]
    upd["pool_w"] = [a.reshape(1, 4, PGW, PGW) for a in upd["pool_w"]]
    for nme, w, g, m, v in (("w_ada", w_ada, g_w_ada, m_w_ada, v_w_ada), ("w_in", w_in, g_w_in, m_w_in, v_w_in),
                            ("w_attn_br", w_attn_br, g_w_ab, m_w_attn_br, v_w_attn_br),
                            ("w_pool_br", w_pool_br, g_w_pb, m_w_pool_br, v_w_pool_br),
                            ("w_out", w_out, g_w_out, m_w_out, v_w_out)):
        upd[nme] = [a[None] for a in _adamw(w[0], g, m[0], v[0], "adamw_" + nme)]
    names = ["norm_g", "w_ada", "b_ada", "w_in", "pool_w", "pool_scale", "w_attn_br", "w_pool_br", "w_out",
             "rel_bias", "final_g"]
    return (loss[0, 0], r["grad_x"][None]) + tuple(upd[nme][kind] for kind in range(4) for nme in names)
```

```python
import math

import numpy as np
import jax
import jax.numpy as jnp
from jax import lax
from jax.experimental import pallas as pl
from jax.experimental.pallas import tpu as pltpu

F32 = jnp.float32
BF16 = jnp.bfloat16

D = 1024
HD = 64
NH = 8
AW = NH * HD
GROUPS = ((128, 1), (512, 4), (2048, 16))
NG = len(GROUPS)
BLK = 128
GW = 3 * AW
QKV_W = NG * GW
REST_W = 3584
IN_W = QKV_W + REST_W
CB = 512
NCB_QKV = QKV_W // CB
POOL_WINDOWS = (2, 4, 8, 16)
PGW = 128
HALO = 16
NUM_BUCKETS = 32
MAX_DISTANCE = 2048
EPS = 1e-6
NEG = -1e30
N_SHARD = 4
VMEM_LIMIT = 56 * 1024 * 1024

ADAM_LR = 0.001
ADAM_B1 = 0.9
ADAM_B2 = 0.999
ADAM_EPS = 1e-08
ADAM_WD = 0.01
ADAM_STEP = 10

PK_BADA, PK_NORMG, PK_FINALG, PK_PSCALE, PK_RELB, PK_LOSS, PK_POOLW, PK_ROWS = 0, 24, 32, 40, 48, 80, 88, 600

ANY = pl.BlockSpec(memory_space=pl.ANY)
MESH = pl.DeviceIdType.MESH


def _params(*sem):
    return pltpu.CompilerParams(dimension_semantics=sem, vmem_limit_bytes=VMEM_LIMIT)


def _sds(shape, dtype=F32):
    return jax.ShapeDtypeStruct(shape, dtype)


def _dot(a, b):
    return jnp.dot(a, b, preferred_element_type=F32)


def _dot_nt(a, b):
    return lax.dot_general(a, b, (((1,), (1,)), ((), ())), preferred_element_type=F32)


def _dot_tn(a, b):
    return lax.dot_general(a, b, (((0,), (0,)), ((), ())), preferred_element_type=F32)


def _sigmoid(z):
    return 0.5 * jnp.tanh(0.5 * z) + 0.5


def _dma_sems(*shape):
    return pltpu.SemaphoreType.DMA(shape)


class _Ride:
    def __init__(self, arrays, out_shapes, n_copies, copies):
        self.arrays, self.out_shapes, self.n_copies, self.copies = list(arrays), list(out_shapes), n_copies, copies


def _call_with_ride(body, ride, first, last, *, in_specs, out_specs, out_shape, scratch_shapes=(), **kw):
    in_specs, out_specs, out_shape, scratch_shapes = list(in_specs), list(out_specs), list(out_shape), list(scratch_shapes)
    n_in, n_out, n_sc = len(in_specs), len(out_specs), len(scratch_shapes)
    if ride is None:
        def run_plain(*operands):
            return pl.pallas_call(body, in_specs=in_specs, out_specs=out_specs, out_shape=out_shape,
                                  scratch_shapes=scratch_shapes, **kw)(*operands), []
        return run_plain
    n_ri, n_ro = len(ride.arrays), len(ride.out_shapes)

    def wrapped(*refs):
        ins, rest = refs[:n_in], refs[n_in:]
        r_ins, rest = rest[:n_ri], rest[n_ri:]
        outs, rest = rest[:n_out], rest[n_out:]
        r_outs, rest = rest[:n_ro], rest[n_ro:]
        scratch, (send_sems, recv_sems) = rest[:n_sc], rest[n_sc:]

        @pl.when(first())
        def _():
            for cp in ride.copies(r_ins, r_outs, send_sems, recv_sems):
                cp.start()

        body(*ins, *outs, *scratch)

        @pl.when(last())
        def _():
            for cp in ride.copies(r_ins, r_outs, send_sems, recv_sems):
                cp.wait()

    def run(*operands):
        res = pl.pallas_call(
            wrapped, in_specs=in_specs + [ANY] * n_ri, out_specs=out_specs + [ANY] * n_ro,
            out_shape=out_shape + ride.out_shapes,
            scratch_shapes=scratch_shapes + [_dma_sems(ride.n_copies), _dma_sems(ride.n_copies)], **kw,
        )(*operands, *ride.arrays)
        return res[:n_out], res[n_out:]
    return run


def _bucket_tables():
    i = np.arange(BLK)[:, None]
    j = np.arange(2 * BLK)[None, :]
    dist = BLK + i - j
    valid = (dist >= 0) & (dist <= BLK)
    tabs = []
    for _, dil in GROUPS:
        n = (np.clip(dist, 0, BLK) * dil).astype(np.int32)
        max_exact = NUM_BUCKETS // 2
        nf = np.maximum(n, 1).astype(np.float32)
        large = max_exact + (np.log(nf / np.float32(max_exact)) / np.float32(math.log(MAX_DISTANCE / max_exact))
                             * np.float32(NUM_BUCKETS - max_exact)).astype(np.int32)
        large = np.minimum(large, NUM_BUCKETS - 1)
        bucket = np.where(n < max_exact, n, large)
        tab = np.where(valid, bucket, -1).astype(np.int32)
        perm = _block_perm(dil)
        tabs.append(tab[perm][:, np.concatenate([perm, BLK + perm])])
    return np.stack(tabs)


def _bias_table(rel_bias, buckets):
    def body(rb_ref, bk_ref, out_ref):
        g = pl.program_id(0)
        bk = bk_ref[...]
        for h in range(NH):
            acc = jnp.full((BLK, 2 * BLK), NEG, F32)
            for b in range(NUM_BUCKETS):
                acc = jnp.where(bk == b, rb_ref[b, g * NH + h], acc)
            out_ref[h] = acc

    return pl.pallas_call(
        body, name="bias_table", grid=(NG,),
        in_specs=[pl.BlockSpec(memory_space=pltpu.SMEM),
                  pl.BlockSpec((None, BLK, 2 * BLK), lambda g: (g, 0, 0))],
        out_specs=pl.BlockSpec((NH, BLK, 2 * BLK), lambda g: (g, 0, 0)),
        out_shape=_sds((NG * NH, BLK, 2 * BLK)),
        compiler_params=_params("arbitrary"),
    )(rel_bias, buckets)


def _bias_grad(ds_acc, buckets, ride):
    def body(acc_ref, bk_ref, out_ref):
        bk = bk_ref[...]
        acc = acc_ref[...]
        lane = lax.broadcasted_iota(jnp.int32, (8, 128), 1)
        out = jnp.zeros((8, 128), F32)
        for b in range(NUM_BUCKETS):
            val = jnp.sum(jnp.where(bk == b, acc, 0.0))
            out = jnp.where(lane == b, val, out)
        out_ref[...] = out

    (out,), rode = _call_with_ride(
        body, ride, lambda: pl.program_id(0) == 0, lambda: pl.program_id(0) == NG * NH - 1,
        name="bias_grad", grid=(NG * NH,),
        in_specs=[pl.BlockSpec((None, BLK, 2 * BLK), lambda gh: (gh, 0, 0)),
                  pl.BlockSpec((None, BLK, 2 * BLK), lambda gh: (gh // NH, 0, 0))],
        out_specs=[pl.BlockSpec((None, 8, 128), lambda gh: (gh, 0, 0))],
        out_shape=[_sds((NG * NH, 8, 128))],
        compiler_params=_params("arbitrary"),
    )(ds_acc, buckets)
    return out, rode


def _mod_partial(c_all, w_ada_s, b_ada_s):
    def body(c_ref, w_ref, b_ref, o_ref):
        o_ref[...] = _dot(c_ref[...].astype(BF16), w_ref[...].astype(BF16)) + b_ref[...]

    return pl.pallas_call(body, name="mod_partial", out_shape=_sds((8, w_ada_s.shape[1])),
                          compiler_params=_params())(c_all, w_ada_s, b_ada_s)


def _prenorm(x, norm_g, mod):
    S = x.shape[0]
    tm = 1024

    def body(x_ref, g_ref, mod_ref, h_ref):
        xv = x_ref[...]
        r = lax.rsqrt(jnp.mean(xv * xv, axis=-1, keepdims=True) + EPS)
        n1 = xv * r * g_ref[...]
        h_ref[...] = (n1 * (1.0 + mod_ref[:, D:2 * D]) + mod_ref[:, 0:D]).astype(BF16)

    return pl.pallas_call(
        body, name="prenorm", grid=(S // tm,),
        in_specs=[pl.BlockSpec((tm, D), lambda i: (i, 0)), pl.BlockSpec((1, D), lambda i: (0, 0)),
                  pl.BlockSpec((1, 3 * D), lambda i: (0, 0))],
        out_specs=pl.BlockSpec((tm, D), lambda i: (i, 0)),
        out_shape=_sds((S, D), BF16), compiler_params=_params("parallel"),
    )(x, norm_g, mod)


def _proj(h, wg_in, j0, nj, dtype, name):
    S = h.shape[0]
    tm = S
    per = wg_in.shape[2] // CB

    def body(h_ref, w_ref, o_ref):
        o_ref[...] = _dot(h_ref[...], w_ref[...]).astype(dtype)

    return pl.pallas_call(
        body, name=name, grid=(S // tm, nj),
        in_specs=[pl.BlockSpec((tm, D), lambda m, j: (m, 0)),
                  pl.BlockSpec((None, D, CB), lambda m, j: ((j0 + j) // per, 0, (j0 + j) % per))],
        out_specs=pl.BlockSpec((tm, CB), lambda m, j: (m, j)),
        out_shape=_sds((S, nj * CB), dtype), compiler_params=_params("parallel", "parallel"),
    )(h, wg_in)


HS = 4
SLAB = HS * HD


def _lane_head(rows):
    return lax.broadcasted_iota(jnp.int32, (rows, SLAB), 1) // HD


def _head_stack(a):
    head = _lane_head(a.shape[0])
    return jnp.concatenate([jnp.where(head == h, a, jnp.zeros_like(a)) for h in range(HS)], axis=0)


def _head_unstack(a):
    rows = a.shape[0] // HS
    head = _lane_head(rows)
    out = a[:rows]
    for h in range(1, HS):
        out = jnp.where(head == h, a[h * rows:(h + 1) * rows], out)
    return out


STAT_W = 128
VIEW = 16


def _sub_layout(dil):
    if dil == 1:
        return BLK, [None]
    return BLK * dil // VIEW, [[r + dil * u for u in range(VIEW // dil)] for r in range(dil)]


def _block_perm(dil):
    a_rows, _ = _sub_layout(dil)
    p = np.arange(BLK)
    return p if dil == 1 else (VIEW // dil) * (p % a_rows) + p // a_rows


LB = 128
N_SLAB = NH // HS


RBS = 4


def _ld(refs, bs, s, w, rb=0):
    if bs is None:
        return refs[0][rb * BLK:(rb + 1) * BLK, s * w:(s + 1) * w]
    a_rows = refs[0].shape[0] // VIEW
    return jnp.concatenate([jnp.concatenate([ref[pl.ds(b, a_rows, stride=VIEW), :] for b in bs], axis=0)
                            for ref in refs[s * (w // LB):(s + 1) * (w // LB)]], axis=1)


def _st(ref, bs, s, val, rb=0):
    if bs is None:
        ref[rb * BLK:(rb + 1) * BLK, s * SLAB:(s + 1) * SLAB] = val.astype(ref.dtype)
        return
    a_rows = val.shape[0] // len(bs)
    for u, b in enumerate(bs):
        ref[:, b, s * SLAB:(s + 1) * SLAB] = val[u * a_rows:(u + 1) * a_rows]


def _attn_views(dil, S):
    a_rows, subs = _sub_layout(dil)
    if dil == 1:
        def ispecs(base, w, f):
            return [pl.BlockSpec((RBS * BLK, N_SLAB * w), lambda sg, n: (f(n), base // (N_SLAB * w)))]
        return subs, S // (RBS * BLK), N_SLAB, RBS, ispecs, (lambda w: (S, w)), (
            lambda f: pl.BlockSpec((RBS * BLK, AW), lambda sg, n: (f(n), 0)))

    sps = N_SLAB if dil < VIEW else 1

    def ispecs(base, w, f):
        return [pl.BlockSpec((a_rows * VIEW, LB), lambda sg, n, k=k: (f(n), (base + sg * sps * w) // LB + k))
                for k in range(sps * w // LB)]
    return subs, S // (a_rows * VIEW), sps, 1, ispecs, (lambda w: (S // VIEW, VIEW, w)), (
        lambda f: pl.BlockSpec((a_rows, VIEW, sps * SLAB), lambda sg, n: (f(n), 0, sg)))


def _attn_fwd(qkv_g, bias_tab, g):
    S = qkv_g.shape[0]
    subs, nbq, sps, rbs, ispecs, shape, ospec = _attn_views(GROUPS[g][1], S)
    cur = lambda n: n
    in_specs = [ispecs(0, SLAB, cur), ispecs(AW, SLAB, cur), ispecs(2 * AW, SLAB, cur)]
    nl = len(in_specs[0])

    def body(*refs):
        q, k, v = (refs[t * nl:(t + 1) * nl] for t in range(3))
        b_ref, o_ref, l_ref, kprev, vprev = refs[3 * nl:]
        n = pl.program_id(1)

        @pl.when(n == 0)
        def _():
            kprev[...] = jnp.zeros_like(kprev)
            vprev[...] = jnp.zeros_like(vprev)

        col = lax.broadcasted_iota(jnp.int32, (HS * BLK, 2 * BLK), 1)
        first = (col >= BLK) | (n > 0)
        for s_, rb, (i, bs) in ((s_, rb, sub) for s_ in range(sps) for rb in range(rbs) for sub in enumerate(subs)):
            cs = slice(s_ * SLAB, (s_ + 1) * SLAB)
            kc, vc = _ld(k, bs, s_, SLAB, rb).astype(BF16), _ld(v, bs, s_, SLAB, rb).astype(BF16)
            kb = jnp.concatenate([kprev[i, :, cs], kc], axis=0)
            vb = jnp.concatenate([vprev[i, :, cs], vc], axis=0)
            kprev[i, :, cs], vprev[i, :, cs] = kc, vc
            s = _dot_nt(_head_stack(_ld(q, bs, s_, SLAB, rb).astype(BF16)), kb) * (HD ** -0.5)
            s = s + b_ref[pl.ds(s_ * HS, HS)].reshape(HS * BLK, 2 * BLK)
            if rb == 0:
                s = jnp.where(first, s, NEG)
            m = jnp.max(s, axis=-1, keepdims=True)
            p = jnp.exp(s - m)
            den = jnp.sum(p, axis=-1, keepdims=True)
            _st(o_ref, bs, s_, _head_unstack(_dot(p.astype(BF16), vb) / den), rb)
            _st(l_ref, bs, s_, _head_unstack(jnp.broadcast_to(m + jnp.log(den), (HS * BLK, SLAB))), rb)

    out = _sds(shape(AW))
    nsg = N_SLAB // sps
    o, l = pl.pallas_call(
        body, name=f"attn_fwd{g}", grid=(nsg, nbq),
        in_specs=sum(in_specs, []) + [pl.BlockSpec((sps * HS, BLK, 2 * BLK), lambda sg, n: (g * nsg + sg, 0, 0))],
        out_specs=[ospec(cur), ospec(cur)],
        out_shape=[out, out],
        scratch_shapes=[pltpu.VMEM((len(subs), BLK, sps * SLAB), BF16)] * 2,
        compiler_params=_params("parallel", "arbitrary"),
    )(*([qkv_g] * (3 * nl)), bias_tab)
    return o.reshape(S, AW), l.reshape(S, AW)


def _attn_bwd(qkv_g, dattn, stats, bias_tab, g, ride):
    S = qkv_g.shape[0]
    subs, nbq, sps, rbs, ispecs, shape, ospec = _attn_views(GROUPS[g][1], S)
    cur = lambda n: jnp.minimum(n, nbq - 1)
    late = lambda n: jnp.maximum(n - 1, 0)
    in_specs = [ispecs(0, SLAB, cur), ispecs(AW, SLAB, cur), ispecs(2 * AW, SLAB, cur), ispecs(0, SLAB, cur),
                ispecs(0, STAT_W, cur)]
    nl = len(in_specs[0])

    def body(*refs):
        q, k, v, da = (refs[t * nl:(t + 1) * nl] for t in range(4))
        nst = len(in_specs[4])
        st_refs = refs[4 * nl:4 * nl + nst]
        b_ref, dq_ref, dk_ref, dv_ref, ds_ref, ck_ref, cv_ref, kprev, vprev, *held = refs[4 * nl + nst:]
        n = pl.program_id(1)

        @pl.when(n == 0)
        def _():
            for ref in (ds_ref, ck_ref, cv_ref, kprev, vprev, *held):
                ref[...] = jnp.zeros_like(ref)

        def finish(ref, t, bs, s_, rb, val):
            cs = slice(s_ * SLAB, (s_ + 1) * SLAB)
            if rbs == 1:
                _st(ref, bs, s_, val)
            elif rb == 0:
                for j in range(rbs - 1):
                    _st(ref, bs, s_, held[t][j * BLK:(j + 1) * BLK, cs], j)
                _st(ref, bs, s_, val, rbs - 1)
            else:
                held[t][(rb - 1) * BLK:rb * BLK, cs] = val

        @pl.when(n < nbq)
        def _():
            col = lax.broadcasted_iota(jnp.int32, (HS * BLK, 2 * BLK), 1)
            first = (col >= BLK) | (n > 0)
            for s_, rb, (i, bs) in ((s_, rb, sub) for s_ in range(sps) for rb in range(rbs) for sub in enumerate(subs)):
                cs = slice(s_ * SLAB, (s_ + 1) * SLAB)
                st = _ld(st_refs, bs, s_, STAT_W, rb)
                kc, vc = _ld(k, bs, s_, SLAB, rb).astype(BF16), _ld(v, bs, s_, SLAB, rb).astype(BF16)
                kb = jnp.concatenate([kprev[i, :, cs], kc], axis=0)
                vb = jnp.concatenate([vprev[i, :, cs], vc], axis=0)
                kprev[i, :, cs], vprev[i, :, cs] = kc, vc
                lse = jnp.concatenate([st[:, h:h + 1] for h in range(HS)], axis=0)
                delta = jnp.concatenate([st[:, HS + h:HS + h + 1] for h in range(HS)], axis=0)
                qs = _head_stack(_ld(q, bs, s_, SLAB, rb).astype(BF16))
                dos = _head_stack(_ld(da, bs, s_, SLAB, rb).astype(BF16))
                s = _dot_nt(qs, kb) * (HD ** -0.5) + b_ref[pl.ds(s_ * HS, HS)].reshape(HS * BLK, 2 * BLK)
                if rb == 0:
                    s = jnp.where(first, s, NEG)
                p = jnp.exp(s - lse)
                ds = p * (_dot_nt(dos, vb) - delta)
                ds_ref[pl.ds(s_ * HS, HS)] += ds.reshape(HS, BLK, 2 * BLK)
                ds_b = (ds * (HD ** -0.5)).astype(BF16)
                _st(dq_ref, bs, s_, _head_unstack(_dot(ds_b, kb)), rb)
                dkb = _dot_tn(ds_b, qs)
                dvb = _dot_tn(p.astype(BF16), dos)
                finish(dk_ref, 0, bs, s_, rb, ck_ref[i, :, cs] + dkb[:BLK])
                finish(dv_ref, 1, bs, s_, rb, cv_ref[i, :, cs] + dvb[:BLK])
                ck_ref[i, :, cs] = dkb[BLK:]
                cv_ref[i, :, cs] = dvb[BLK:]

        @pl.when(n == nbq)
        def _():
            for s_ in range(sps):
                for i, bs in enumerate(subs):
                    finish(dk_ref, 0, bs, s_, 0, ck_ref[i, :, s_ * SLAB:(s_ + 1) * SLAB])
                    finish(dv_ref, 1, bs, s_, 0, cv_ref[i, :, s_ * SLAB:(s_ + 1) * SLAB])

    out = _sds(shape(AW), BF16 if GROUPS[g][1] == 1 else F32)
    nsg = N_SLAB // sps
    (dq, dk, dv, ds_acc), rode = _call_with_ride(
        body, ride, lambda: (pl.program_id(0) == 0) & (pl.program_id(1) == 0),
        lambda: (pl.program_id(0) == nsg - 1) & (pl.program_id(1) == nbq),
        name=f"attn_bwd{g}", grid=(nsg, nbq + 1),
        in_specs=sum(in_specs, []) + [pl.BlockSpec((sps * HS, BLK, 2 * BLK), lambda sg, n: (g * nsg + sg, 0, 0))],
        out_specs=[ospec(cur), ospec(late), ospec(late),
                   pl.BlockSpec((sps * HS, BLK, 2 * BLK), lambda sg, n: (sg, 0, 0))],
        out_shape=[out] * 3 + [_sds((NH, BLK, 2 * BLK))],
        scratch_shapes=[pltpu.VMEM((len(subs), BLK, sps * SLAB), F32)] * 2
        + [pltpu.VMEM((len(subs), BLK, sps * SLAB), BF16)] * 2 + [pltpu.VMEM(((rbs - 1) * BLK, sps * SLAB), F32)] * (2 if rbs > 1 else 0),
        compiler_params=_params("arbitrary", "arbitrary"),
    )(*([qkv_g] * (3 * nl)), *([dattn] * nl), *([stats] * len(in_specs[4])), bias_tab)
    return [dq.reshape(S, AW), dk.reshape(S, AW), dv.reshape(S, AW)], ds_acc, rode


TM_MIX = 256


def _mix_specs(tm):
    row512 = pl.BlockSpec((tm, AW), lambda i: (i, 0))
    return ([row512] * 6 + [
        pl.BlockSpec((tm, REST_W), lambda i: (i, 0)),
        pl.BlockSpec((HALO, AW), lambda i: (jnp.maximum(i * (tm // HALO) - 1, 0), 1)),
        pl.BlockSpec((AW, D), lambda i: (0, 0)), pl.BlockSpec((AW, D), lambda i: (0, 0)),
        pl.BlockSpec((4, PGW, PGW), lambda i: (0, 0, 0)), pl.BlockSpec((1, AW), lambda i: (0, 0))])


def _mix_forward(i, tm, o_refs, l_refs, rest_ref, halo_ref, wab_ref, wpb_ref, pw_ref, ps_ref):
    l0, l1, l2 = (r[...] for r in l_refs)
    mx = jnp.maximum(jnp.maximum(l0, l1), l2)
    e0, e1, e2 = jnp.exp(l0 - mx), jnp.exp(l1 - mx), jnp.exp(l2 - mx)
    den = e0 + e1 + e2
    lj = mx + jnp.log(den)
    attn = (e0 * o_refs[0][...] + e1 * o_refs[1][...] + e2 * o_refs[2][...]) / den

    z_attn = rest_ref[:, 0:AW]
    u = rest_ref[:, AW:2 * AW]
    z_pool = rest_ref[:, 2 * AW:3 * AW]
    g_attn = rest_ref[:, 3 * AW:3 * AW + D]
    g_pool = rest_ref[:, 3 * AW + D:3 * AW + 2 * D]

    sg_a = _sigmoid(z_attn)
    sil_a = z_attn * sg_a
    a_g = (attn * sil_a).astype(BF16)
    y_attn = _dot(a_g, wab_ref[...])

    halo = jnp.where(i > 0, halo_ref[...], 0.0)
    ext = jnp.concatenate([halo, u], axis=0)
    t = i * tm + lax.broadcasted_iota(jnp.int32, (tm, 1), 0)
    pooled, mixed_raw = [], []
    for gi, win in enumerate(POOL_WINDOWS):
        s = ext[:, gi * PGW:(gi + 1) * PGW]
        sh = 1
        while sh < win:
            s = s + pltpu.roll(s, sh, 0)
            sh *= 2
        cnt = jnp.minimum(t + 1, win).astype(F32)
        pg = s[HALO:] / cnt - u[:, gi * PGW:(gi + 1) * PGW]
        pooled.append(pg.astype(BF16))
        mixed_raw.append(_dot(pooled[-1], pw_ref[gi].astype(BF16)))
    mixed_raw = jnp.concatenate(mixed_raw, axis=1)
    mixed = mixed_raw * ps_ref[...]
    sg_p = _sigmoid(z_pool)
    sil_p = z_pool * sg_p
    m_g = (mixed * sil_p).astype(BF16)
    y_pool = _dot(m_g, wpb_ref[...])

    sa = _sigmoid(g_attn)
    sp = _sigmoid(g_pool)
    merged = sa * y_attn + sp * y_pool
    return dict(lj=lj, attn=attn, z_attn=z_attn, z_pool=z_pool, sg_a=sg_a, sil_a=sil_a, a_g=a_g, y_attn=y_attn,
                pooled=pooled, mixed_raw=mixed_raw, mixed=mixed, sg_p=sg_p, sil_p=sil_p, m_g=m_g, y_pool=y_pool,
                sa=sa, sp=sp, merged=merged)


def _mix_step(x, target, os_, ls_, rest, wab, wpb, pool_w, pool_scale, wout, mod, final_g):
    S = x.shape[0]
    tm = TM_MIX
    nt = S // tm
    sw = D // N_SHARD

    def body(o0, o1, o2, l0, l1, l2, rest_ref, halo_ref, wab_ref, wpb_ref, pw_ref, ps_ref,
             x_ref, t_ref, wo_ref, mod_ref, fg_ref, dx2_ref, loss_ref, dfg_ref, dgate_ref,
             dattn_ref, stats_ref, dpooled_ref, dproj_hbm, dwo_hbm, dwab_hbm, dwpb_hbm, dpw_ref, dps_ref,
             awo, awab, awpb, stage, stage_sem):
        i = pl.program_id(0)
        slot = i % 2

        def staged(step, sl):
            return pltpu.make_async_copy(stage.at[sl], dproj_hbm.at[pl.ds(step * tm, tm), pl.ds(QKV_W, REST_W)],
                                         stage_sem.at[sl])

        @pl.when(i == 0)
        def _():
            for ref in (loss_ref, dfg_ref, dgate_ref, awo, awab, awpb, dpw_ref, dps_ref):
                ref[...] = jnp.zeros_like(ref)

        f = _mix_forward(i, tm, (o0, o1, o2), (l0, l1, l2), rest_ref, halo_ref, wab_ref, wpb_ref, pw_ref, ps_ref)
        mo = _dot(f["merged"].astype(BF16), wo_ref[...])
        gate = mod_ref[:, 2 * D:3 * D]
        fg = fg_ref[...]
        x2 = x_ref[...] + gate * mo
        r2 = lax.rsqrt(jnp.mean(x2 * x2, axis=-1, keepdims=True) + EPS)
        n2 = x2 * r2
        err = n2 * fg - t_ref[...]
        loss_ref[...] += 0.5 * jnp.sum(jnp.mean(err * err, axis=-1, keepdims=True))
        dy = err * (1.0 / D)
        dfg_ref[...] += jnp.sum(dy * n2, axis=0, keepdims=True)
        dn = dy * fg
        dx2 = r2 * (dn - n2 * jnp.mean(dn * n2, axis=-1, keepdims=True))
        dgate_ref[...] += jnp.sum(dx2 * mo, axis=0, keepdims=True)
        dx2_ref[...] = dx2

        dmo_b = (dx2 * gate).astype(BF16)
        dmerged = _dot_nt(dmo_b, wo_ref[...])
        awo[...] += _dot_tn(f["merged"].astype(BF16), dmo_b)
        sa, sp = f["sa"], f["sp"]
        dya = (dmerged * sa).astype(BF16)
        dyp = (dmerged * sp).astype(BF16)
        dg_attn = dmerged * f["y_attn"] * sa * (1.0 - sa)
        dg_pool = dmerged * f["y_pool"] * sp * (1.0 - sp)
        dag = _dot_nt(dya, wab_ref[...])
        awab[...] += _dot_tn(f["a_g"], dya)
        dmg = _dot_nt(dyp, wpb_ref[...])
        awpb[...] += _dot_tn(f["m_g"], dyp)
        dattn = dag * f["sil_a"]
        dattn_ref[...] = dattn
        prod = dattn * f["attn"]
        lane = lax.broadcasted_iota(jnp.int32, (tm, STAT_W), 1)
        for sb in range(N_SLAB):
            st = jnp.zeros((tm, STAT_W), F32)
            for h in range(HS):
                hs = slice((sb * HS + h) * HD, (sb * HS + h + 1) * HD)
                st = jnp.where(lane == h, f["lj"][:, hs.start:hs.start + 1], st)
                st = jnp.where(lane == HS + h, jnp.sum(prod[:, hs], axis=-1, keepdims=True), st)
            stats_ref[:, sb * STAT_W:(sb + 1) * STAT_W] = st
        dz_attn = dag * f["attn"] * (f["sg_a"] * (1.0 + f["z_attn"] * (1.0 - f["sg_a"])))
        dmixed = dmg * f["sil_p"]
        dz_pool = dmg * f["mixed"] * (f["sg_p"] * (1.0 + f["z_pool"] * (1.0 - f["sg_p"])))
        dps_ref[...] += jnp.sum(dmixed * f["mixed_raw"], axis=0, keepdims=True)
        dpm = (dmixed * ps_ref[...]).astype(BF16)
        for gi in range(len(POOL_WINDOWS)):
            cs = slice(gi * PGW, (gi + 1) * PGW)
            dpw_ref[gi] += _dot_tn(f["pooled"][gi], dpm[:, cs])
            dpooled_ref[:, cs] = _dot_nt(dpm[:, cs], pw_ref[gi].astype(BF16))
        @pl.when(i >= 2)
        def _():
            staged(i - 2, slot).wait()

        stage[slot, :, 0:AW] = dz_attn.astype(BF16)
        stage[slot, :, AW:2 * AW] = jnp.zeros((tm, AW), BF16)
        stage[slot, :, 2 * AW:3 * AW] = dz_pool.astype(BF16)
        stage[slot, :, 3 * AW:3 * AW + D] = dg_attn.astype(BF16)
        stage[slot, :, 3 * AW + D:3 * AW + 2 * D] = dg_pool.astype(BF16)
        staged(i, slot).start()

        @pl.when(i == nt - 1)
        def _():
            staged(i - 1, 1 - slot).wait()
            staged(i, slot).wait()
            pltpu.sync_copy(awo, dwo_hbm)
            for k in range(N_SHARD):
                pltpu.sync_copy(awab.at[:, pl.ds(k * sw, sw)], dwab_hbm.at[k])
                pltpu.sync_copy(awpb.at[:, pl.ds(k * sw, sw)], dwpb_hbm.at[k])

    row = pl.BlockSpec((tm, D), lambda i: (i, 0))
    vec = pl.BlockSpec((1, D), lambda i: (0, 0))
    row512 = pl.BlockSpec((tm, AW), lambda i: (i, 0))
    outs = pl.pallas_call(
        body, name="mix_step", grid=(nt,),
        in_specs=_mix_specs(tm) + [row, row, pl.BlockSpec((D, D), lambda i: (0, 0)),
                                   pl.BlockSpec((1, 3 * D), lambda i: (0, 0)), vec],
        out_specs=[row, pl.BlockSpec((8, 128), lambda i: (0, 0)), vec, vec,
                   row512, pl.BlockSpec((tm, N_SLAB * STAT_W), lambda i: (i, 0)), row512, ANY, ANY, ANY, ANY,
                   pl.BlockSpec((4, PGW, PGW), lambda i: (0, 0, 0)), pl.BlockSpec((1, AW), lambda i: (0, 0))],
        out_shape=[_sds((S, D)), _sds((8, 128)), _sds((1, D)), _sds((1, D)),
                   _sds((S, AW)), _sds((S, N_SLAB * STAT_W)), _sds((S, AW)), _sds((S, IN_W), BF16),
                   _sds((D, D)), _sds((N_SHARD, AW, sw)), _sds((N_SHARD, AW, sw)), _sds((4, PGW, PGW)), _sds((1, AW))],
        scratch_shapes=[pltpu.VMEM((D, D), F32), pltpu.VMEM((AW, D), F32), pltpu.VMEM((AW, D), F32),
                        pltpu.VMEM((2, tm, REST_W), BF16), _dma_sems(2)],
        compiler_params=_params("arbitrary"),
    )(*os_, *ls_, rest, rest, wab, wpb, pool_w, pool_scale, x, target, wout, mod, final_g)
    dx2, loss, dfg, dgate, dattn, stats, dpooled, dproj, dwo, dwab, dwpb, dpw, dps = outs
    return (dx2, loss, dfg, dgate, dattn, stats, dpooled, dproj, dwo.reshape(N_SHARD, D // N_SHARD, D), dwab, dwpb,
            dpw, dps)


def _pool_bwd(dpooled):
    S = dpooled.shape[0]
    tm = 1024
    nt = S // tm

    def body(dp_ref, nxt_ref, du_ref):
        i = pl.program_id(0)
        t = i * tm + lax.broadcasted_iota(jnp.int32, (tm + HALO, 1), 0)
        nxt = jnp.where(i < nt - 1, nxt_ref[...], 0.0)
        ext = jnp.concatenate([dp_ref[...], nxt], axis=0)
        for gi, win in enumerate(POOL_WINDOWS):
            cs = slice(gi * PGW, (gi + 1) * PGW)
            s = ext[:, cs] / jnp.minimum(t + 1, win).astype(F32)
            sh = 1
            while sh < win:
                s = s + pltpu.roll(s, tm + HALO - sh, 0)
                sh *= 2
            du_ref[:, cs] = (s[:tm] - dp_ref[:, cs]).astype(BF16)

    return pl.pallas_call(
        body, name="pool_bwd", grid=(nt,),
        in_specs=[pl.BlockSpec((tm, AW), lambda i: (i, 0)),
                  pl.BlockSpec((HALO, AW), lambda i: (jnp.minimum((i + 1) * (tm // HALO), S // HALO - 1), 0))],
        out_specs=pl.BlockSpec((tm, AW), lambda i: (i, 0)),
        out_shape=_sds((S, AW), BF16), compiler_params=_params("parallel"),
    )(dpooled, dpooled)


TB = 1024


def _dh_prenorm_bwd(dproj, wg_in, x, dx2, norm_g, mod, ride):
    S = dproj.shape[0]
    per = wg_in.shape[2] // TB
    nm, nk = S // TB, IN_W // TB
    rows = 256

    def body(dp_ref, w_ref, x_ref, dx2_ref, g_ref, mod_ref, gx_ref, dg_ref, dshift_ref, dscale_ref, dh_ref):
        m, kk = pl.program_id(0), pl.program_id(1)

        @pl.when(kk == 0)
        def _():
            dh_ref[...] = jnp.zeros_like(dh_ref)

        @pl.when((m == 0) & (kk == 0))
        def _():
            dg_ref[...] = jnp.zeros_like(dg_ref)
            dshift_ref[...] = jnp.zeros_like(dshift_ref)
            dscale_ref[...] = jnp.zeros_like(dscale_ref)

        dh_ref[...] += _dot_nt(dp_ref[...], w_ref[...])

        @pl.when(kk == nk - 1)
        def _():
            g = g_ref[...]
            for c in range(TB // rows):
                sl = pl.ds(c * rows, rows)
                xv = x_ref[sl, :]
                dhv = dh_ref[sl, :]
                r = lax.rsqrt(jnp.mean(xv * xv, axis=-1, keepdims=True) + EPS)
                xh = xv * r
                dshift_ref[...] += jnp.sum(dhv, axis=0, keepdims=True)
                dscale_ref[...] += jnp.sum(dhv * (xh * g), axis=0, keepdims=True)
                dn1 = dhv * (1.0 + mod_ref[:, D:2 * D])
                dg_ref[...] += jnp.sum(dn1 * xh, axis=0, keepdims=True)
                dxh = dn1 * g
                gx_ref[sl, :] = dx2_ref[sl, :] + r * (dxh - xh * jnp.mean(dxh * xh, axis=-1, keepdims=True))

    row = pl.BlockSpec((TB, D), lambda m, kk: (m, 0))
    vec = pl.BlockSpec((1, D), lambda m, kk: (0, 0))
    outs, rode = _call_with_ride(
        body, ride, lambda: (pl.program_id(0) == 0) & (pl.program_id(1) == 0),
        lambda: (pl.program_id(0) == nm - 1) & (pl.program_id(1) == nk - 1),
        name="dh", grid=(nm, nk),
        in_specs=[pl.BlockSpec((TB, TB), lambda m, kk: (m, kk)),
                  pl.BlockSpec((None, D, TB), lambda m, kk: (kk // per, 0, kk % per)),
                  row, row, vec, pl.BlockSpec((1, 3 * D), lambda m, kk: (0, 0))],
        out_specs=[row, vec, vec, vec],
        out_shape=[_sds((S, D)), _sds((1, D)), _sds((1, D)), _sds((1, D))],
        scratch_shapes=[pltpu.VMEM((TB, D), F32)], compiler_params=_params("arbitrary", "arbitrary"),
    )(dproj, wg_in, x, dx2, norm_g, mod)
    return outs, rode


def _dw_in(h, dproj):
    S = dproj.shape[0]
    per = IN_W // N_SHARD // TB

    def body(h_ref, dp_ref, out_ref):
        out_ref[...] = _dot_tn(h_ref[...], dp_ref[...])

    return pl.pallas_call(
        body, name="dw_in", grid=(IN_W // TB,),
        in_specs=[pl.BlockSpec((S, D), lambda j: (0, 0)), pl.BlockSpec((S, TB), lambda j: (0, j))],
        out_specs=pl.BlockSpec((None, D, TB), lambda j: (j // per, 0, j % per)),
        out_shape=_sds((N_SHARD, D, IN_W // N_SHARD)), compiler_params=_params("parallel"),
    )(h, dproj)


def _local_step(x, target, mod, wg_in, wab, wpb, wout, pool_w, pool_scale, rel_bias, norm_g, final_g, chip_half):
    buckets = jnp.asarray(_bucket_tables())
    bias_tab = _bias_table(rel_bias, buckets)
    h = _prenorm(x, norm_g, mod)
    qkv = [_proj(h, wg_in, 3 * g, 3, BF16 if GROUPS[g][1] == 1 else F32, f"proj_qkv{g}") for g in range(NG)]
    rest = _proj(h, wg_in, NCB_QKV, REST_W // CB, F32, "proj_rest")
    os_, ls_ = zip(*[_attn_fwd(qkv[g], bias_tab, g) for g in range(NG)])
    (dx2, loss, dfinal_g, dgate, dattn, stats, dpooled, dproj, dw_out, dw_ab, dw_pb, dpool_w,
     dpool_scale) = _mix_step(x, target, os_, ls_, rest, wab, wpb, pool_w, pool_scale, wout, mod, final_g)
    du = _pool_bwd(dpooled)

    small = [dw_ab, dw_pb, dw_out]
    dqkv0, ds0, sib_small = _attn_bwd(qkv[0], dattn, stats, bias_tab, 0, _ride_sibling_halves(small))
    p_small = _pair_sum_small(small, sib_small, chip_half)
    dqkv1, ds1, u_small = _attn_bwd(qkv[1], dattn, stats, bias_tab, 1,
                                    _ride_chip_exchange([p16 for _, p16 in p_small]))
    rs_ab, rs_pb, rs_out = _chip_sum_small([p32 for p32, _ in p_small], u_small, chip_half)
    dqkv2, ds2, _ = _attn_bwd(qkv[2], dattn, stats, bias_tab, 2, None)

    for j, piece in enumerate(dqkv0 + dqkv1 + dqkv2):
        dproj = lax.dynamic_update_slice(dproj, piece.astype(BF16), (0, j * AW))
    dproj = lax.dynamic_update_slice(dproj, du, (0, QKV_W + AW))
    dw_in = _dw_in(h, dproj)
    drel_rows, (sib_in,) = _bias_grad(jnp.concatenate([ds0, ds1, ds2], axis=0), buckets,
                                      _ride_sibling_halves([dw_in]))
    drel = drel_rows[:, 0, :NUM_BUCKETS].T
    p32_in, p16_in = _pair_sum(dw_in, sib_in, chip_half, "rs_pair_sum_in")
    (grad_x, dnorm_g, dshift, dscale), (u_in,) = _dh_prenorm_bwd(dproj, wg_in, x, dx2, norm_g, mod,
                                                                 _ride_chip_exchange([p16_in]))
    rs_in = _chip_sum(p32_in, u_in, chip_half, "rs_chip_sum_in")
    dmod = jnp.concatenate([dshift, dscale, dgate], axis=1)
    return dict(loss=loss[0, 0], grad_x=grad_x, dmod=dmod, dnorm_g=dnorm_g, dfinal_g=dfinal_g, dpool_w=dpool_w,
                dpool_scale=dpool_scale, drel_bias=drel, dw_in=dw_in, dw_attn_br=dw_ab, dw_pool_br=dw_pb,
                dw_out=dw_out, rs_in=rs_in, rs_attn_br=rs_ab, rs_pool_br=rs_pb, rs_out=rs_out)


def _allgather8(blocks, name, relay=None, join=()):
    nb, nj = len(blocks), len(join)
    relay = [False] * nb if relay is None else list(relay)

    def body(*refs):
        ins, outs, joined = refs[:nb], refs[nb + nj:2 * nb + nj], refs[2 * nb + nj:2 * nb + 2 * nj]
        send_sems, recv_sems, join_send, join_recv = refs[2 * nb + 2 * nj:]
        x, y, c = lax.axis_index("x"), lax.axis_index("y"), lax.axis_index("c")
        me, sibling = (x, y, c), (x, y, 1 - c)
        here, xn, yn, dg = (x, y), (1 - x, y), (x, 1 - y), (1 - x, 1 - y)

        def slot(a, chip, core, half=None):
            ref = outs[a].at[4 * chip[0] + 2 * chip[1] + core]
            if half is None:
                return ref
            r2 = ref.shape[0] // 2
            return ref.at[pl.ds(half * r2, r2)]

        def copy(a, k, dst, to, src=None):
            return pltpu.make_async_remote_copy(src_ref=dst if src is None else src, dst_ref=dst,
                                                send_sem=send_sems.at[a, k], recv_sem=recv_sems.at[a, k],
                                                device_id=to, device_id_type=MESH)

        def start(cps):
            for cp in cps:
                cp.start()
            return cps

        swaps = []
        for a in range(nj):
            r2 = joined[a].shape[0] // 2
            rows = joined[a].at[pl.ds(c * r2, r2), :]
            swaps.append(pltpu.make_async_remote_copy(src_ref=rows, dst_ref=rows, send_sem=join_send.at[a],
                                                      recv_sem=join_recv.at[a], device_id=sibling,
                                                      device_id_type=MESH))
        start(swaps)
        sent = []
        for a in range(nb):
            own = slot(a, here, c)
            sent += [copy(a, 0, own, sibling, src=ins[a]), copy(a, 1, own, (*xn, c), src=ins[a]),
                     copy(a, 2, own, (*yn, c), src=ins[a])]
            if not relay[a]:
                sent.append(copy(a, 3, own, (*dg, c), src=ins[a]))
        start(sent)
        for a in range(nb):
            copy(a, 2, slot(a, yn, c), me).wait_recv()
            sent += start([copy(a, 6, slot(a, yn, c), sibling)]
                          + ([copy(a, 3, slot(a, yn, c, 0), (*xn, c))] if relay[a] else []))
        for a in range(nb):
            copy(a, 1, slot(a, xn, c), me).wait_recv()
            sent += start([copy(a, 5, slot(a, xn, c), sibling)]
                          + ([copy(a, 4, slot(a, xn, c, 1), (*yn, c))] if relay[a] else []))
        for a in range(nb):
            for k, half in ((3, 0), (4, 1)) if relay[a] else ((3, None),):
                copy(a, k, slot(a, dg, c, half), me).wait_recv()
                sent += start([copy(a, 4 + k, slot(a, dg, c, half), sibling)])
        for a in range(nb):
            copy(a, 0, slot(a, here, 1 - c), me).wait_recv()
            copy(a, 5, slot(a, xn, 1 - c), me).wait_recv()
            copy(a, 6, slot(a, yn, 1 - c), me).wait_recv()
            for k, half in ((7, 0), (8, 1)) if relay[a] else ((7, None),):
                copy(a, k, slot(a, dg, 1 - c, half), me).wait_recv()
        for cp in sent:
            cp.wait_send()
        for cp in swaps:
            cp.wait()

    outs = pl.pallas_call(
        body, name=name, in_specs=[ANY] * (nb + nj), out_specs=[ANY] * (nb + nj),
        out_shape=[_sds((8,) + b.shape, b.dtype) for b in blocks] + [_sds(f.shape, f.dtype) for f in join],
        input_output_aliases={nb + a: nb + a for a in range(nj)},
        scratch_shapes=[_dma_sems(nb, 9), _dma_sems(nb, 9), _dma_sems(max(nj, 1)), _dma_sems(max(nj, 1))],
    )(*blocks, *join)
    gathered = [_place_own(buf, b) for buf, b in zip(outs[:nb], blocks)]
    return (gathered, list(outs[nb:])) if nj else gathered


def _place_own(buf, block):
    dev = 4 * lax.axis_index("x") + 2 * lax.axis_index("y") + lax.axis_index("c")
    return lax.dynamic_update_index_in_dim(buf, block, dev, 0)


def _ride_sibling_halves(gs):
    def copies(ins, outs, send_sems, recv_sems):
        x, y, c = lax.axis_index("x"), lax.axis_index("y"), lax.axis_index("c")
        cps = []
        for a in range(len(gs)):
            r2 = ins[a].shape[1] // 2
            other = ins[a].at[:, pl.ds((1 - c) * r2, r2), :]
            cps.append(pltpu.make_async_remote_copy(src_ref=other, dst_ref=outs[a], send_sem=send_sems.at[a],
                                                    recv_sem=recv_sems.at[a], device_id=(x, y, 1 - c),
                                                    device_id_type=MESH))
        return cps

    return _Ride(gs, [_sds((g.shape[0], g.shape[1] // 2, g.shape[2]), g.dtype) for g in gs], len(gs), copies)


def _pair_sum(g, t, chip_half, name):
    nsh, rows, cols = g.shape
    r2 = rows // 2
    tr = _row_tile(r2, cols)
    nt = r2 // tr

    def body(ch_ref, g_ref, t_ref, p32_ref, p16_ref):
        p = g_ref[...] + t_ref[...]
        p16_ref[...] = p.astype(BF16)

        @pl.when(pl.program_id(1) == ch_ref[0])
        def _():
            p32_ref[...] = p

    blk = pl.BlockSpec((None, tr, cols), lambda i, k, ch_ref: (k, i, 0))
    return pl.pallas_call(
        body, name=name,
        grid_spec=pltpu.PrefetchScalarGridSpec(
            num_scalar_prefetch=1, grid=(nt, nsh),
            in_specs=[pl.BlockSpec((None, tr, cols), lambda i, k, ch_ref: (k, ch_ref[1] * nt + i, 0)), blk],
            out_specs=[pl.BlockSpec((tr, cols), lambda i, k, ch_ref: (i, 0)), blk]),
        out_shape=[_sds((r2, cols)), _sds((nsh, r2, cols), BF16)],
        compiler_params=_params("parallel", "arbitrary"),
    )(chip_half, g, t)


def _pair_sum_small(gs, ts, chip_half):
    na = len(gs)

    def body(ch_ref, *refs):
        g_refs, t_refs, outs = refs[:na], refs[na:2 * na], refs[2 * na:]
        for a in range(na):
            r2 = t_refs[a].shape[1]
            own = pl.ds(pl.multiple_of(ch_ref[1] * r2, 8), r2)
            outs[2 * a + 1][...] = (g_refs[a][:, own, :] + t_refs[a][...]).astype(BF16)
            outs[2 * a][...] = g_refs[a][ch_ref[0], own, :] + t_refs[a][ch_ref[0]]

    res = pl.pallas_call(
        body, name="rs_pair_sum_small",
        in_specs=[pl.BlockSpec(memory_space=pltpu.SMEM)] + [pl.BlockSpec(memory_space=pltpu.VMEM)] * (2 * na),
        out_shape=[s for t in ts for s in (_sds(t.shape[1:]), _sds(t.shape, BF16))], compiler_params=_params(),
    )(chip_half, *gs, *ts)
    return [(res[2 * a], res[2 * a + 1]) for a in range(na)]


def _chip_sum_small(p32s, us, chip_half):
    na = len(p32s)

    def body(ch_ref, *refs):
        p_refs, u_refs, outs = refs[:na], refs[na:2 * na], refs[2 * na:]
        for a in range(na):
            r2 = p_refs[a].shape[0]
            acc = p_refs[a][...]
            for j in range(3):
                acc = acc + u_refs[a][j].astype(F32)
            outs[a][pl.ds(pl.multiple_of(ch_ref[1] * r2, 8), r2), :] = acc

    return pl.pallas_call(
        body, name="rs_chip_sum_small",
        in_specs=[pl.BlockSpec(memory_space=pltpu.SMEM)] + [pl.BlockSpec(memory_space=pltpu.VMEM)] * (2 * na),
        out_shape=[_sds((2 * p.shape[0], p.shape[1])) for p in p32s], compiler_params=_params(),
    )(chip_half, *p32s, *us)


def _ride_chip_exchange(ps):
    def copies(ins, outs, send_sems, recv_sems):
        x, y, c = lax.axis_index("x"), lax.axis_index("y"), lax.axis_index("c")
        chips = [(1 - x, y), (x, 1 - y), (1 - x, 1 - y)]
        cps = []
        for a in range(len(ps)):
            for j, (ox, oy) in enumerate(chips):
                cps.append(pltpu.make_async_remote_copy(src_ref=ins[a].at[2 * ox + oy], dst_ref=outs[a].at[j],
                                                        send_sem=send_sems.at[3 * a + j],
                                                        recv_sem=recv_sems.at[3 * a + j],
                                                        device_id=(ox, oy, c), device_id_type=MESH))
        return cps

    return _Ride(ps, [_sds((3,) + p.shape[1:], p.dtype) for p in ps], 3 * len(ps), copies)


def _chip_sum(p32, u, chip_half, name):
    r2, cols = p32.shape
    tr = _row_tile(r2, cols)
    nt = r2 // tr

    def body(ch_ref, p_ref, u_ref, o_ref):
        acc = p_ref[...]
        for j in range(3):
            acc = acc + u_ref[j].astype(F32)
        o_ref[...] = acc

    return pl.pallas_call(
        body, name=name,
        grid_spec=pltpu.PrefetchScalarGridSpec(
            num_scalar_prefetch=1, grid=(nt,),
            in_specs=[pl.BlockSpec((tr, cols), lambda i, ch_ref: (i, 0)),
                      pl.BlockSpec((3, tr, cols), lambda i, ch_ref: (0, i, 0))],
            out_specs=pl.BlockSpec((tr, cols), lambda i, ch_ref: (ch_ref[1] * nt + i, 0))),
        out_shape=_sds((2 * r2, cols)), compiler_params=_params("parallel"),
    )(chip_half, p32, u)


def _row_tile(rows, cols):
    tile = rows
    while tile * cols * 4 > (2 << 20) and tile % 16 == 0:
        tile //= 2
    return tile


def _w_ada_grad(c_all, dmod_cols):
    def body(c_ref, d_ref, o_ref):
        o_ref[...] = _dot_tn(c_ref[...].astype(BF16), d_ref[...].astype(BF16))

    return pl.pallas_call(body, name="w_ada_grad", out_shape=_sds((c_all.shape[1], dmod_cols.shape[1])),
                          compiler_params=_params())(c_all, dmod_cols)


def _adam_math(w, g, m, v):
    nm = ADAM_B1 * m + (1.0 - ADAM_B1) * g
    nv = ADAM_B2 * v + (1.0 - ADAM_B2) * (g * g)
    m_hat = nm / (1.0 - ADAM_B1 ** ADAM_STEP)
    v_hat = nv / (1.0 - ADAM_B2 ** ADAM_STEP)
    return -ADAM_LR * (m_hat / (jnp.sqrt(v_hat) + ADAM_EPS) + ADAM_WD * w), nm, nv


def _adamw(w, g, m, v, name):
    rows, cols = w.shape
    tr = _row_tile(rows, cols)

    def body(w_ref, g_ref, m_ref, v_ref, go_ref, d_ref, nm_ref, nv_ref):
        gv = g_ref[...]
        go_ref[...] = gv
        d_ref[...], nm_ref[...], nv_ref[...] = _adam_math(w_ref[...], gv, m_ref[...], v_ref[...])

    spec = pl.BlockSpec((tr, cols), lambda i: (i, 0))
    return pl.pallas_call(
        body, name=name, grid=(rows // tr,), in_specs=[spec] * 4, out_specs=[spec] * 4,
        out_shape=[_sds((rows, cols))] * 4, compiler_params=_params("parallel"),
    )(w, g, m, v)


def _pack_small(dmod, dnorm_g, dfinal_g, dpool_scale, drel_bias, loss, dpool_w):
    return jnp.concatenate([dmod.reshape(-1, 128), dnorm_g.reshape(-1, 128), dfinal_g.reshape(-1, 128),
                            jnp.pad(dpool_scale.reshape(-1, 128), ((0, PK_RELB - PK_PSCALE - AW // 128), (0, 0))),
                            jnp.pad(drel_bias, ((0, 0), (0, 128 - NG * NH))),
                            jnp.full((PK_POOLW - PK_LOSS, 128), loss, F32), dpool_w.reshape(-1, 128)], axis=0)


def _small_update(small_all, ws, ms, vs):
    lane_rows = [(r0, r0 + w.shape[1] // 128) for r0, w in zip((PK_BADA, PK_NORMG, PK_FINALG, PK_PSCALE), ws)]
    nw = len(ws)

    def body(all_ref, *refs):
        w_refs, m_refs, v_refs = refs[:nw], refs[nw:2 * nw], refs[2 * nw:3 * nw]
        loss_ref, outs = refs[3 * nw], refs[3 * nw + 1:]
        g = all_ref[0]
        for s in range(1, all_ref.shape[0]):
            g = g + all_ref[s]
        loss_ref[...] = jnp.broadcast_to(g[PK_LOSS:PK_LOSS + 1, :], loss_ref.shape)

        def put(p, at, gv):
            d, nm, nv = _adam_math(w_refs[p][at], gv, m_refs[p][at], v_refs[p][at])
            for o_ref, val in zip(outs[4 * p:4 * p + 4], (gv, d, nm, nv)):
                o_ref[at] = val

        for p, (r0, r1) in enumerate(lane_rows):
            for i in range(r1 - r0):
                put(p, (slice(None), slice(128 * i, 128 * (i + 1))), g[r0 + i:r0 + i + 1, :])
        put(4, (slice(None), slice(None)), g[PK_RELB:PK_LOSS, 0:NG * NH])
        put(5, (slice(None), slice(None)), g[PK_POOLW:PK_ROWS, :])

    res = pl.pallas_call(
        body, name="small_update",
        out_shape=[_sds((8, 128))] + [_sds(w.shape) for w in ws for _ in range(4)], compiler_params=_params(),
    )(small_all, *ws, *ms, *vs)
    return res[0], [res[1 + 4 * p:5 + 4 * p] for p in range(nw)]


def kernel(x, c, norm_g, w_ada, b_ada, w_in, pool_w, pool_scale, w_attn_br, w_pool_br, w_out, rel_bias, final_g, loss_target, m_norm_g, m_w_ada, m_b_ada, m_w_in, m_pool_w, m_pool_scale, m_w_attn_br, m_w_pool_br, m_w_out, m_rel_bias, m_final_g, v_norm_g, v_w_ada, v_b_ada, v_w_in, v_pool_w, v_pool_scale, v_w_attn_br, v_w_pool_br, v_w_out, v_rel_bias, v_final_g):
    ix, iy, ic = lax.axis_index("x"), lax.axis_index("y"), lax.axis_index("c")
    dev = 4 * ix + 2 * iy + ic
    chip = 2 * ix + iy

    def half(w):
        r2 = w.shape[0] // 2
        return lax.dynamic_slice_in_dim(w, ic * r2, r2, axis=0).astype(BF16)

    gathered = _allgather8([jnp.broadcast_to(c, (8, D)), half(w_in[0]), half(w_attn_br[0]), half(w_pool_br[0]),
                            half(w_out[0])], "gather_weights", relay=[False, True, True, True, True])
    c_all = gathered[0][:, 0, :]
    wg_in = gathered[1].reshape(N_SHARD, D, IN_W // N_SHARD)
    wab = gathered[2].reshape(N_SHARD, AW, D // N_SHARD).transpose(1, 0, 2).reshape(AW, D)
    wpb = gathered[3].reshape(N_SHARD, AW, D // N_SHARD).transpose(1, 0, 2).reshape(AW, D)
    wout = gathered[4].reshape(D, D)

    mw = 3 * D // N_SHARD
    modp = _mod_partial(c_all, w_ada[0], lax.dynamic_slice_in_dim(b_ada, chip * mw, mw, axis=1))
    mod_all = _allgather8([modp], "gather_mod")[0]
    mod_full = mod_all[::2].transpose(1, 0, 2).reshape(8, 3 * D)
    mod = lax.dynamic_slice_in_dim(mod_full, dev, 1, axis=0)

    chip_half = jnp.stack([chip, ic]).astype(jnp.int32)
    r = _local_step(x[0], loss_target[0], mod, wg_in, wab, wpb, wout, pool_w[0], pool_scale, rel_bias, norm_g,
                    final_g.reshape(1, D), chip_half)

    packed = _pack_small(r["dmod"], r["dnorm_g"], r["dfinal_g"], r["dpool_scale"], r["drel_bias"], r["loss"],
                         r["dpool_w"])
    (small_all,), (g_w_in, g_w_ab, g_w_pb, g_w_out) = _allgather8(
        [packed], "gather_small", join=[r["rs_in"], r["rs_attn_br"], r["rs_pool_br"], r["rs_out"]])
    small = ["b_ada", "norm_g", "final_g", "pool_scale", "rel_bias", "pool_w"]
    shaped = lambda b, n, f, ps, rb, pw: [b, n, f.reshape(1, D), ps, rb, pw.reshape(4 * PGW, PGW)]
    loss, small_out = _small_update(small_all, shaped(b_ada, norm_g, final_g, pool_scale, rel_bias, pool_w),
                                    shaped(m_b_ada, m_norm_g, m_final_g, m_pool_scale, m_rel_bias, m_pool_w),
                                    shaped(v_b_ada, v_norm_g, v_final_g, v_pool_scale, v_rel_bias, v_pool_w))
    dmod_all = small_all[:, PK_BADA:PK_NORMG, :].reshape(8, 3 * D)
    g_w_ada = _w_ada_grad(c_all, lax.dynamic_slice_in_dim(dmod_all, chip * mw, mw, axis=1))

    upd = dict(zip(small, small_out))
    upd["final_g"] = [a.reshape(D) for a in upd["final_g"]]
    upd["pool_w"] = [a.reshape(1, 4, PGW, PGW) for a in upd["pool_w"]]
    for nme, w, g, m, v in (("w_ada", w_ada, g_w_ada, m_w_ada, v_w_ada), ("w_in", w_in, g_w_in, m_w_in, v_w_in),
                            ("w_attn_br", w_attn_br, g_w_ab, m_w_attn_br, v_w_attn_br),
                            ("w_pool_br", w_pool_br, g_w_pb, m_w_pool_br, v_w_pool_br),
                            ("w_out", w_out, g_w_out, m_w_out, v_w_out)):
        upd[nme] = [a[None] for a in _adamw(w[0], g, m[0], v[0], "adamw_" + nme)]
    names = ["norm_g", "w_ada", "b_ada", "w_in", "pool_w", "pool_scale", "w_attn_br", "w_pool_br", "w_out",
             "rel_bias", "final_g"]
    return (loss[0, 0], r["grad_x"][None]) + tuple(upd[nme][kind] for kind in range(4) for nme in names)
```

```python
import math

import numpy as np
import jax
import jax.numpy as jnp
from jax import lax
from jax.experimental import pallas as pl
from jax.experimental.pallas import tpu as pltpu

F32 = jnp.float32
BF16 = jnp.bfloat16

D = 1024
HD = 64
NH = 8
AW = NH * HD
GROUPS = ((128, 1), (512, 4), (2048, 16))
NG = len(GROUPS)
BLK = 128
GW = 3 * AW
QKV_W = NG * GW
REST_W = 3584
IN_W = QKV_W + REST_W
CB = 512
NCB_QKV = QKV_W // CB
POOL_WINDOWS = (2, 4, 8, 16)
PGW = 128
HALO = 16
NUM_BUCKETS = 32
MAX_DISTANCE = 2048
EPS = 1e-6
NEG = -1e30
N_SHARD = 4
VMEM_LIMIT = 56 * 1024 * 1024

ADAM_LR = 0.001
ADAM_B1 = 0.9
ADAM_B2 = 0.999
ADAM_EPS = 1e-08
ADAM_WD = 0.01
ADAM_STEP = 10

PK_BADA, PK_NORMG, PK_FINALG, PK_PSCALE, PK_RELB, PK_LOSS, PK_POOLW, PK_ROWS = 0, 24, 32, 40, 48, 80, 88, 600

ANY = pl.BlockSpec(memory_space=pl.ANY)
MESH = pl.DeviceIdType.MESH


def _params(*sem):
    return pltpu.CompilerParams(dimension_semantics=sem, vmem_limit_bytes=VMEM_LIMIT)


def _sds(shape, dtype=F32):
    return jax.ShapeDtypeStruct(shape, dtype)


def _dot(a, b):
    return jnp.dot(a, b, preferred_element_type=F32)


def _dot_nt(a, b):
    return lax.dot_general(a, b, (((1,), (1,)), ((), ())), preferred_element_type=F32)


def _dot_tn(a, b):
    return lax.dot_general(a, b, (((0,), (0,)), ((), ())), preferred_element_type=F32)


def _sigmoid(z):
    return 0.5 * jnp.tanh(0.5 * z) + 0.5


def _dma_sems(*shape):
    return pltpu.SemaphoreType.DMA(shape)


class _Ride:
    def __init__(self, arrays, out_shapes, n_copies, copies):
        self.arrays, self.out_shapes, self.n_copies, self.copies = list(arrays), list(out_shapes), n_copies, copies


def _call_with_ride(body, ride, first, last, *, in_specs, out_specs, out_shape, scratch_shapes=(), **kw):
    in_specs, out_specs, out_shape, scratch_shapes = list(in_specs), list(out_specs), list(out_shape), list(scratch_shapes)
    n_in, n_out, n_sc = len(in_specs), len(out_specs), len(scratch_shapes)
    if ride is None:
        def run_plain(*operands):
            return pl.pallas_call(body, in_specs=in_specs, out_specs=out_specs, out_shape=out_shape,
                                  scratch_shapes=scratch_shapes, **kw)(*operands), []
        return run_plain
    n_ri, n_ro = len(ride.arrays), len(ride.out_shapes)

    def wrapped(*refs):
        ins, rest = refs[:n_in], refs[n_in:]
        r_ins, rest = rest[:n_ri], rest[n_ri:]
        outs, rest = rest[:n_out], rest[n_out:]
        r_outs, rest = rest[:n_ro], rest[n_ro:]
        scratch, (send_sems, recv_sems) = rest[:n_sc], rest[n_sc:]

        @pl.when(first())
        def _():
            for cp in ride.copies(r_ins, r_outs, send_sems, recv_sems):
                cp.start()

        body(*ins, *outs, *scratch)

        @pl.when(last())
        def _():
            for cp in ride.copies(r_ins, r_outs, send_sems, recv_sems):
                cp.wait()

    def run(*operands):
        res = pl.pallas_call(
            wrapped, in_specs=in_specs + [ANY] * n_ri, out_specs=out_specs + [ANY] * n_ro,
            out_shape=out_shape + ride.out_shapes,
            scratch_shapes=scratch_shapes + [_dma_sems(ride.n_copies), _dma_sems(ride.n_copies)], **kw,
        )(*operands, *ride.arrays)
        return res[:n_out], res[n_out:]
    return run


def _bucket_tables():
    i = np.arange(BLK)[:, None]
    j = np.arange(2 * BLK)[None, :]
    dist = BLK + i - j
    valid = (dist >= 0) & (dist <= BLK)
    tabs = []
    for _, dil in GROUPS:
        n = (np.clip(dist, 0, BLK) * dil).astype(np.int32)
        max_exact = NUM_BUCKETS // 2
        nf = np.maximum(n, 1).astype(np.float32)
        large = max_exact + (np.log(nf / np.float32(max_exact)) / np.float32(math.log(MAX_DISTANCE / max_exact))
                             * np.float32(NUM_BUCKETS - max_exact)).astype(np.int32)
        large = np.minimum(large, NUM_BUCKETS - 1)
        bucket = np.where(n < max_exact, n, large)
        tab = np.where(valid, bucket, -1).astype(np.int32)
        perm = _block_perm(dil)
        tabs.append(tab[perm][:, np.concatenate([perm, BLK + perm])])
    return np.stack(tabs)


def _bias_table(rel_bias, buckets):
    def body(rb_ref, bk_ref, out_ref):
        g = pl.program_id(0)
        bk = bk_ref[...]
        for h in range(NH):
            acc = jnp.full((BLK, 2 * BLK), NEG, F32)
            for b in range(NUM_BUCKETS):
                acc = jnp.where(bk == b, rb_ref[b, g * NH + h], acc)
            out_ref[h] = acc

    return pl.pallas_call(
        body, name="bias_table", grid=(NG,),
        in_specs=[pl.BlockSpec(memory_space=pltpu.SMEM),
                  pl.BlockSpec((None, BLK, 2 * BLK), lambda g: (g, 0, 0))],
        out_specs=pl.BlockSpec((NH, BLK, 2 * BLK), lambda g: (g, 0, 0)),
        out_shape=_sds((NG * NH, BLK, 2 * BLK)),
        compiler_params=_params("arbitrary"),
    )(rel_bias, buckets)


def _bias_grad(ds_acc, buckets, ride):
    def body(acc_ref, bk_ref, out_ref):
        bk = bk_ref[...]
        acc = acc_ref[...]
        lane = lax.broadcasted_iota(jnp.int32, (8, 128), 1)
        out = jnp.zeros((8, 128), F32)
        for b in range(NUM_BUCKETS):
            val = jnp.sum(jnp.where(bk == b, acc, 0.0))
            out = jnp.where(lane == b, val, out)
        out_ref[...] = out

    (out,), rode = _call_with_ride(
        body, ride, lambda: pl.program_id(0) == 0, lambda: pl.program_id(0) == NG * NH - 1,
        name="bias_grad", grid=(NG * NH,),
        in_specs=[pl.BlockSpec((None, BLK, 2 * BLK), lambda gh: (gh, 0, 0)),
                  pl.BlockSpec((None, BLK, 2 * BLK), lambda gh: (gh // NH, 0, 0))],
        out_specs=[pl.BlockSpec((None, 8, 128), lambda gh: (gh, 0, 0))],
        out_shape=[_sds((NG * NH, 8, 128))],
        compiler_params=_params("arbitrary"),
    )(ds_acc, buckets)
    return out, rode


def _mod_partial(c_all, w_ada_s, b_ada_s):
    def body(c_ref, w_ref, b_ref, o_ref):
        o_ref[...] = _dot(c_ref[...].astype(BF16), w_ref[...].astype(BF16)) + b_ref[...]

    return pl.pallas_call(body, name="mod_partial", out_shape=_sds((8, w_ada_s.shape[1])),
                          compiler_params=_params())(c_all, w_ada_s, b_ada_s)


def _prenorm(x, norm_g, mod):
    S = x.shape[0]
    tm = 1024

    def body(x_ref, g_ref, mod_ref, h_ref):
        xv = x_ref[...]
        r = lax.rsqrt(jnp.mean(xv * xv, axis=-1, keepdims=True) + EPS)
        n1 = xv * r * g_ref[...]
        h_ref[...] = (n1 * (1.0 + mod_ref[:, D:2 * D]) + mod_ref[:, 0:D]).astype(BF16)

    return pl.pallas_call(
        body, name="prenorm", grid=(S // tm,),
        in_specs=[pl.BlockSpec((tm, D), lambda i: (i, 0)), pl.BlockSpec((1, D), lambda i: (0, 0)),
                  pl.BlockSpec((1, 3 * D), lambda i: (0, 0))],
        out_specs=pl.BlockSpec((tm, D), lambda i: (i, 0)),
        out_shape=_sds((S, D), BF16), compiler_params=_params("parallel"),
    )(x, norm_g, mod)


def _proj(h, wg_in, j0, nj, dtype, name):
    S = h.shape[0]
    tm = S
    per = wg_in.shape[2] // CB

    def body(h_ref, w_ref, o_ref):
        o_ref[...] = _dot(h_ref[...], w_ref[...]).astype(dtype)

    return pl.pallas_call(
        body, name=name, grid=(S // tm, nj),
        in_specs=[pl.BlockSpec((tm, D), lambda m, j: (m, 0)),
                  pl.BlockSpec((None, D, CB), lambda m, j: ((j0 + j) // per, 0, (j0 + j) % per))],
        out_specs=pl.BlockSpec((tm, CB), lambda m, j: (m, j)),
        out_shape=_sds((S, nj * CB), dtype), compiler_params=_params("parallel", "parallel"),
    )(h, wg_in)


HS = 4
SLAB = HS * HD


def _lane_head(rows):
    return lax.broadcasted_iota(jnp.int32, (rows, SLAB), 1) // HD


def _head_stack(a):
    head = _lane_head(a.shape[0])
    return jnp.concatenate([jnp.where(head == h, a, jnp.zeros_like(a)) for h in range(HS)], axis=0)


def _head_unstack(a):
    rows = a.shape[0] // HS
    head = _lane_head(rows)
    out = a[:rows]
    for h in range(1, HS):
        out = jnp.where(head == h, a[h * rows:(h + 1) * rows], out)
    return out


STAT_W = 128
VIEW = 16


def _sub_layout(dil):
    if dil == 1:
        return BLK, [None]
    return BLK * dil // VIEW, [[r + dil * u for u in range(VIEW // dil)] for r in range(dil)]


def _block_perm(dil):
    a_rows, _ = _sub_layout(dil)
    p = np.arange(BLK)
    return p if dil == 1 else (VIEW // dil) * (p % a_rows) + p // a_rows


LB = 128
N_SLAB = NH // HS


RBS = 4


def _ld(refs, bs, s, w, rb=0):
    if bs is None:
        return refs[0][rb * BLK:(rb + 1) * BLK, s * w:(s + 1) * w]
    a_rows = refs[0].shape[0] // VIEW
    return jnp.concatenate([jnp.concatenate([ref[pl.ds(b, a_rows, stride=VIEW), :] for b in bs], axis=0)
                            for ref in refs[s * (w // LB):(s + 1) * (w // LB)]], axis=1)


def _st(ref, bs, s, val, rb=0):
    if bs is None:
        ref[rb * BLK:(rb + 1) * BLK, s * SLAB:(s + 1) * SLAB] = val.astype(ref.dtype)
        return
    a_rows = val.shape[0] // len(bs)
    for u, b in enumerate(bs):
        ref[:, b, s * SLAB:(s + 1) * SLAB] = val[u * a_rows:(u + 1) * a_rows]


def _attn_views(dil, S):
    a_rows, subs = _sub_layout(dil)
    if dil == 1:
        def ispecs(base, w, f):
            return [pl.BlockSpec((RBS * BLK, N_SLAB * w), lambda sg, n: (f(n), base // (N_SLAB * w)))]
        return subs, S // (RBS * BLK), N_SLAB, RBS, ispecs, (lambda w: (S, w)), (
            lambda f: pl.BlockSpec((RBS * BLK, AW), lambda sg, n: (f(n), 0)))

    sps = N_SLAB if dil < VIEW else 1

    def ispecs(base, w, f):
        return [pl.BlockSpec((a_rows * VIEW, LB), lambda sg, n, k=k: (f(n), (base + sg * sps * w) // LB + k))
                for k in range(sps * w // LB)]
    return subs, S // (a_rows * VIEW), sps, 1, ispecs, (lambda w: (S // VIEW, VIEW, w)), (
        lambda f: pl.BlockSpec((a_rows, VIEW, sps * SLAB), lambda sg, n: (f(n), 0, sg)))


def _attn_fwd(qkv_g, bias_tab, g):
    S = qkv_g.shape[0]
    subs, nbq, sps, rbs, ispecs, shape, ospec = _attn_views(GROUPS[g][1], S)
    cur = lambda n: n
    in_specs = [ispecs(0, SLAB, cur), ispecs(AW, SLAB, cur), ispecs(2 * AW, SLAB, cur)]
    nl = len(in_specs[0])

    def body(*refs):
        q, k, v = (refs[t * nl:(t + 1) * nl] for t in range(3))
        b_ref, o_ref, l_ref, kprev, vprev = refs[3 * nl:]
        n = pl.program_id(1)

        @pl.when(n == 0)
        def _():
            kprev[...] = jnp.zeros_like(kprev)
            vprev[...] = jnp.zeros_like(vprev)

        col = lax.broadcasted_iota(jnp.int32, (HS * BLK, 2 * BLK), 1)
        first = (col >= BLK) | (n > 0)
        for s_, rb, (i, bs) in ((s_, rb, sub) for s_ in range(sps) for rb in range(rbs) for sub in enumerate(subs)):
            cs = slice(s_ * SLAB, (s_ + 1) * SLAB)
            kc, vc = _ld(k, bs, s_, SLAB, rb).astype(BF16), _ld(v, bs, s_, SLAB, rb).astype(BF16)
            kb = jnp.concatenate([kprev[i, :, cs], kc], axis=0)
            vb = jnp.concatenate([vprev[i, :, cs], vc], axis=0)
            kprev[i, :, cs], vprev[i, :, cs] = kc, vc
            s = _dot_nt(_head_stack(_ld(q, bs, s_, SLAB, rb).astype(BF16)), kb) * (HD ** -0.5)
            s = s + b_ref[pl.ds(s_ * HS, HS)].reshape(HS * BLK, 2 * BLK)
            if rb == 0:
                s = jnp.where(first, s, NEG)
            m = jnp.max(s, axis=-1, keepdims=True)
            p = jnp.exp(s - m)
            den = jnp.sum(p, axis=-1, keepdims=True)
            _st(o_ref, bs, s_, _head_unstack(_dot(p.astype(BF16), vb) / den), rb)
            _st(l_ref, bs, s_, _head_unstack(jnp.broadcast_to(m + jnp.log(den), (HS * BLK, SLAB))), rb)

    out = _sds(shape(AW))
    nsg = N_SLAB // sps
    o, l = pl.pallas_call(
        body, name=f"attn_fwd{g}", grid=(nsg, nbq),
        in_specs=sum(in_specs, []) + [pl.BlockSpec((sps * HS, BLK, 2 * BLK), lambda sg, n: (g * nsg + sg, 0, 0))],
        out_specs=[ospec(cur), ospec(cur)],
        out_shape=[out, out],
        scratch_shapes=[pltpu.VMEM((len(subs), BLK, sps * SLAB), BF16)] * 2,
        compiler_params=_params("parallel", "arbitrary"),
    )(*([qkv_g] * (3 * nl)), bias_tab)
    return o.reshape(S, AW), l.reshape(S, AW)


def _attn_bwd(qkv_g, dattn, stats, bias_tab, g, ride):
    S = qkv_g.shape[0]
    subs, nbq, sps, rbs, ispecs, shape, ospec = _attn_views(GROUPS[g][1], S)
    cur = lambda n: jnp.minimum(n, nbq - 1)
    late = lambda n: jnp.maximum(n - 1, 0)
    in_specs = [ispecs(0, SLAB, cur), ispecs(AW, SLAB, cur), ispecs(2 * AW, SLAB, cur), ispecs(0, SLAB, cur),
                ispecs(0, STAT_W, cur)]
    nl = len(in_specs[0])

    def body(*refs):
        q, k, v, da = (refs[t * nl:(t + 1) * nl] for t in range(4))
        nst = len(in_specs[4])
        st_refs = refs[4 * nl:4 * nl + nst]
        b_ref, dq_ref, dk_ref, dv_ref, ds_ref, ck_ref, cv_ref, kprev, vprev, *held = refs[4 * nl + nst:]
        n = pl.program_id(1)

        @pl.when(n == 0)
        def _():
            for ref in (ds_ref, ck_ref, cv_ref, kprev, vprev, *held):
                ref[...] = jnp.zeros_like(ref)

        def finish(ref, t, bs, s_, rb, val):
            cs = slice(s_ * SLAB, (s_ + 1) * SLAB)
            if rbs == 1:
                _st(ref, bs, s_, val)
            elif rb == 0:
                for j in range(rbs - 1):
                    _st(ref, bs, s_, held[t][j * BLK:(j + 1) * BLK, cs], j)
                _st(ref, bs, s_, val, rbs - 1)
            else:
                held[t][(rb - 1) * BLK:rb * BLK, cs] = val

        @pl.when(n < nbq)
        def _():
            col = lax.broadcasted_iota(jnp.int32, (HS * BLK, 2 * BLK), 1)
            first = (col >= BLK) | (n > 0)
            for s_, rb, (i, bs) in ((s_, rb, sub) for s_ in range(sps) for rb in range(rbs) for sub in enumerate(subs)):
                cs = slice(s_ * SLAB, (s_ + 1) * SLAB)
                st = _ld(st_refs, bs, s_, STAT_W, rb)
                kc, vc = _ld(k, bs, s_, SLAB, rb).astype(BF16), _ld(v, bs, s_, SLAB, rb).astype(BF16)
                kb = jnp.concatenate([kprev[i, :, cs], kc], axis=0)
                vb = jnp.concatenate([vprev[i, :, cs], vc], axis=0)
                kprev[i, :, cs], vprev[i, :, cs] = kc, vc
                lse = jnp.concatenate([st[:, h:h + 1] for h in range(HS)], axis=0)
                delta = jnp.concatenate([st[:, HS + h:HS + h + 1] for h in range(HS)], axis=0)
                qs = _head_stack(_ld(q, bs, s_, SLAB, rb).astype(BF16))
                dos = _head_stack(_ld(da, bs, s_, SLAB, rb).astype(BF16))
                s = _dot_nt(qs, kb) * (HD ** -0.5) + b_ref[pl.ds(s_ * HS, HS)].reshape(HS * BLK, 2 * BLK)
                if rb == 0:
                    s = jnp.where(first, s, NEG)
                p = jnp.exp(s - lse)
                ds = p * (_dot_nt(dos, vb) - delta)
                ds_ref[pl.ds(s_ * HS, HS)] += ds.reshape(HS, BLK, 2 * BLK)
                ds_b = (ds * (HD ** -0.5)).astype(BF16)
                _st(dq_ref, bs, s_, _head_unstack(_dot(ds_b, kb)), rb)
                dkb = _dot_tn(ds_b, qs)
                dvb = _dot_tn(p.astype(BF16), dos)
                finish(dk_ref, 0, bs, s_, rb, ck_ref[i, :, cs] + dkb[:BLK])
                finish(dv_ref, 1, bs, s_, rb, cv_ref[i, :, cs] + dvb[:BLK])
                ck_ref[i, :, cs] = dkb[BLK:]
                cv_ref[i, :, cs] = dvb[BLK:]

        @pl.when(n == nbq)
        def _():
            for s_ in range(sps):
                for i, bs in enumerate(subs):
                    finish(dk_ref, 0, bs, s_, 0, ck_ref[i, :, s_ * SLAB:(s_ + 1) * SLAB])
                    finish(dv_ref, 1, bs, s_, 0, cv_ref[i, :, s_ * SLAB:(s_ + 1) * SLAB])

    out = _sds(shape(AW), BF16 if GROUPS[g][1] == 1 else F32)
    nsg = N_SLAB // sps
    (dq, dk, dv, ds_acc), rode = _call_with_ride(
        body, ride, lambda: (pl.program_id(0) == 0) & (pl.program_id(1) == 0),
        lambda: (pl.program_id(0) == nsg - 1) & (pl.program_id(1) == nbq),
        name=f"attn_bwd{g}", grid=(nsg, nbq + 1),
        in_specs=sum(in_specs, []) + [pl.BlockSpec((sps * HS, BLK, 2 * BLK), lambda sg, n: (g * nsg + sg, 0, 0))],
        out_specs=[ospec(cur), ospec(late), ospec(late),
                   pl.BlockSpec((sps * HS, BLK, 2 * BLK), lambda sg, n: (sg, 0, 0))],
        out_shape=[out] * 3 + [_sds((NH, BLK, 2 * BLK))],
        scratch_shapes=[pltpu.VMEM((len(subs), BLK, sps * SLAB), F32)] * 2
        + [pltpu.VMEM((len(subs), BLK, sps * SLAB), BF16)] * 2 + [pltpu.VMEM(((rbs - 1) * BLK, sps * SLAB), F32)] * (2 if rbs > 1 else 0),
        compiler_params=_params("arbitrary", "arbitrary"),
    )(*([qkv_g] * (3 * nl)), *([dattn] * nl), *([stats] * len(in_specs[4])), bias_tab)
    return [dq.reshape(S, AW), dk.reshape(S, AW), dv.reshape(S, AW)], ds_acc, rode


TM_MIX = 256


def _mix_specs(tm):
    row512 = pl.BlockSpec((tm, AW), lambda i: (i, 0))
    return ([row512] * 6 + [
        pl.BlockSpec((tm, REST_W), lambda i: (i, 0)),
        pl.BlockSpec((HALO, AW), lambda i: (jnp.maximum(i * (tm // HALO) - 1, 0), 1)),
        pl.BlockSpec((AW, D), lambda i: (0, 0)), pl.BlockSpec((AW, D), lambda i: (0, 0)),
        pl.BlockSpec((4, PGW, PGW), lambda i: (0, 0, 0)), pl.BlockSpec((1, AW), lambda i: (0, 0))])


def _mix_forward(i, tm, o_refs, l_refs, rest_ref, halo_ref, wab_ref, wpb_ref, pw_ref, ps_ref):
    l0, l1, l2 = (r[...] for r in l_refs)
    mx = jnp.maximum(jnp.maximum(l0, l1), l2)
    e0, e1, e2 = jnp.exp(l0 - mx), jnp.exp(l1 - mx), jnp.exp(l2 - mx)
    den = e0 + e1 + e2
    lj = mx + jnp.log(den)
    attn = (e0 * o_refs[0][...] + e1 * o_refs[1][...] + e2 * o_refs[2][...]) / den

    z_attn = rest_ref[:, 0:AW]
    u = rest_ref[:, AW:2 * AW]
    z_pool = rest_ref[:, 2 * AW:3 * AW]
    g_attn = rest_ref[:, 3 * AW:3 * AW + D]
    g_pool = rest_ref[:, 3 * AW + D:3 * AW + 2 * D]

    sg_a = _sigmoid(z_attn)
    sil_a = z_attn * sg_a
    a_g = (attn * sil_a).astype(BF16)
    y_attn = _dot(a_g, wab_ref[...])

    halo = jnp.where(i > 0, halo_ref[...], 0.0)
    ext = jnp.concatenate([halo, u], axis=0)
    t = i * tm + lax.broadcasted_iota(jnp.int32, (tm, 1), 0)
    pooled, mixed_raw = [], []
    for gi, win in enumerate(POOL_WINDOWS):
        s = ext[:, gi * PGW:(gi + 1) * PGW]
        sh = 1
        while sh < win:
            s = s + pltpu.roll(s, sh, 0)
            sh *= 2
        cnt = jnp.minimum(t + 1, win).astype(F32)
        pg = s[HALO:] / cnt - u[:, gi * PGW:(gi + 1) * PGW]
        pooled.append(pg.astype(BF16))
        mixed_raw.append(_dot(pooled[-1], pw_ref[gi].astype(BF16)))
    mixed_raw = jnp.concatenate(mixed_raw, axis=1)
    mixed = mixed_raw * ps_ref[...]
    sg_p = _sigmoid(z_pool)
    sil_p = z_pool * sg_p
    m_g = (mixed * sil_p).astype(BF16)
    y_pool = _dot(m_g, wpb_ref[...])

    sa = _sigmoid(g_attn)
    sp = _sigmoid(g_pool)
    merged = sa * y_attn + sp * y_pool
    return dict(lj=lj, attn=attn, z_attn=z_attn, z_pool=z_pool, sg_a=sg_a, sil_a=sil_a, a_g=a_g, y_attn=y_attn,
                pooled=pooled, mixed_raw=mixed_raw, mixed=mixed, sg_p=sg_p, sil_p=sil_p, m_g=m_g, y_pool=y_pool,
                sa=sa, sp=sp, merged=merged)


def _mix_step(x, target, os_, ls_, rest, wab, wpb, pool_w, pool_scale, wout, mod, final_g):
    S = x.shape[0]
    tm = TM_MIX
    nt = S // tm
    sw = D // N_SHARD

    def body(o0, o1, o2, l0, l1, l2, rest_ref, halo_ref, wab_ref, wpb_ref, pw_ref, ps_ref,
             x_ref, t_ref, wo_ref, mod_ref, fg_ref, dx2_ref, loss_ref, dfg_ref, dgate_ref,
             dattn_ref, stats_ref, dpooled_ref, dproj_hbm, dwo_hbm, dwab_hbm, dwpb_hbm, dpw_ref, dps_ref,
             awo, awab, awpb, stage, stage_sem):
        i = pl.program_id(0)
        slot = i % 2

        def staged(step, sl):
            return pltpu.make_async_copy(stage.at[sl], dproj_hbm.at[pl.ds(step * tm, tm), pl.ds(QKV_W, REST_W)],
                                         stage_sem.at[sl])

        @pl.when(i == 0)
        def _():
            for ref in (loss_ref, dfg_ref, dgate_ref, awo, awab, awpb, dpw_ref, dps_ref):
                ref[...] = jnp.zeros_like(ref)

        f = _mix_forward(i, tm, (o0, o1, o2), (l0, l1, l2), rest_ref, halo_ref, wab_ref, wpb_ref, pw_ref, ps_ref)
        mo = _dot(f["merged"].astype(BF16), wo_ref[...])
        gate = mod_ref[:, 2 * D:3 * D]
        fg = fg_ref[...]
        x2 = x_ref[...] + gate * mo
        r2 = lax.rsqrt(jnp.mean(x2 * x2, axis=-1, keepdims=True) + EPS)
        n2 = x2 * r2
        err = n2 * fg - t_ref[...]
        loss_ref[...] += 0.5 * jnp.sum(jnp.mean(err * err, axis=-1, keepdims=True))
        dy = err * (1.0 / D)
        dfg_ref[...] += jnp.sum(dy * n2, axis=0, keepdims=True)
        dn = dy * fg
        dx2 = r2 * (dn - n2 * jnp.mean(dn * n2, axis=-1, keepdims=True))
        dgate_ref[...] += jnp.sum(dx2 * mo, axis=0, keepdims=True)
        dx2_ref[...] = dx2

        dmo_b = (dx2 * gate).astype(BF16)
        dmerged = _dot_nt(dmo_b, wo_ref[...])
        awo[...] += _dot_tn(f["merged"].astype(BF16), dmo_b)
        sa, sp = f["sa"], f["sp"]
        dya = (dmerged * sa).astype(BF16)
        dyp = (dmerged * sp).astype(BF16)
        dg_attn = dmerged * f["y_attn"] * sa * (1.0 - sa)
        dg_pool = dmerged * f["y_pool"] * sp * (1.0 - sp)
        dag = _dot_nt(dya, wab_ref[...])
        awab[...] += _dot_tn(f["a_g"], dya)
        dmg = _dot_nt(dyp, wpb_ref[...])
        awpb[...] += _dot_tn(f["m_g"], dyp)
        dattn = dag * f["sil_a"]
        dattn_ref[...] = dattn
        prod = dattn * f["attn"]
        lane = lax.broadcasted_iota(jnp.int32, (tm, STAT_W), 1)
        for sb in range(N_SLAB):
            st = jnp.zeros((tm, STAT_W), F32)
            for h in range(HS):
                hs = slice((sb * HS + h) * HD, (sb * HS + h + 1) * HD)
                st = jnp.where(lane == h, f["lj"][:, hs.start:hs.start + 1], st)
                st = jnp.where(lane == HS + h, jnp.sum(prod[:, hs], axis=-1, keepdims=True), st)
            stats_ref[:, sb * STAT_W:(sb + 1) * STAT_W] = st
        dz_attn = dag * f["attn"] * (f["sg_a"] * (1.0 + f["z_attn"] * (1.0 - f["sg_a"])))
        dmixed = dmg * f["sil_p"]
        dz_pool = dmg * f["mixed"] * (f["sg_p"] * (1.0 + f["z_pool"] * (1.0 - f["sg_p"])))
        dps_ref[...] += jnp.sum(dmixed * f["mixed_raw"], axis=0, keepdims=True)
        dpm = (dmixed * ps_ref[...]).astype(BF16)
        for gi in range(len(POOL_WINDOWS)):
            cs = slice(gi * PGW, (gi + 1) * PGW)
            dpw_ref[gi] += _dot_tn(f["pooled"][gi], dpm[:, cs])
            dpooled_ref[:, cs] = _dot_nt(dpm[:, cs], pw_ref[gi].astype(BF16))
        @pl.when(i >= 2)
        def _():
            staged(i - 2, slot).wait()

        stage[slot, :, 0:AW] = dz_attn.astype(BF16)
        stage[slot, :, AW:2 * AW] = jnp.zeros((tm, AW), BF16)
        stage[slot, :, 2 * AW:3 * AW] = dz_pool.astype(BF16)
        stage[slot, :, 3 * AW:3 * AW + D] = dg_attn.astype(BF16)
        stage[slot, :, 3 * AW + D:3 * AW + 2 * D] = dg_pool.astype(BF16)
        staged(i, slot).start()

        @pl.when(i == nt - 1)
        def _():
            staged(i - 1, 1 - slot).wait()
            staged(i, slot).wait()
            pltpu.sync_copy(awo, dwo_hbm)
            for k in range(N_SHARD):
                pltpu.sync_copy(awab.at[:, pl.ds(k * sw, sw)], dwab_hbm.at[k])
                pltpu.sync_copy(awpb.at[:, pl.ds(k * sw, sw)], dwpb_hbm.at[k])

    row = pl.BlockSpec((tm, D), lambda i: (i, 0))
    vec = pl.BlockSpec((1, D), lambda i: (0, 0))
    row512 = pl.BlockSpec((tm, AW), lambda i: (i, 0))
    outs = pl.pallas_call(
        body, name="mix_step", grid=(nt,),
        in_specs=_mix_specs(tm) + [row, row, pl.BlockSpec((D, D), lambda i: (0, 0)),
                                   pl.BlockSpec((1, 3 * D), lambda i: (0, 0)), vec],
        out_specs=[row, pl.BlockSpec((8, 128), lambda i: (0, 0)), vec, vec,
                   row512, pl.BlockSpec((tm, N_SLAB * STAT_W), lambda i: (i, 0)), row512, ANY, ANY, ANY, ANY,
                   pl.BlockSpec((4, PGW, PGW), lambda i: (0, 0, 0)), pl.BlockSpec((1, AW), lambda i: (0, 0))],
        out_shape=[_sds((S, D)), _sds((8, 128)), _sds((1, D)), _sds((1, D)),
                   _sds((S, AW)), _sds((S, N_SLAB * STAT_W)), _sds((S, AW)), _sds((S, IN_W), BF16),
                   _sds((D, D)), _sds((N_SHARD, AW, sw)), _sds((N_SHARD, AW, sw)), _sds((4, PGW, PGW)), _sds((1, AW))],
        scratch_shapes=[pltpu.VMEM((D, D), F32), pltpu.VMEM((AW, D), F32), pltpu.VMEM((AW, D), F32),
                        pltpu.VMEM((2, tm, REST_W), BF16), _dma_sems(2)],
        compiler_params=_params("arbitrary"),
    )(*os_, *ls_, rest, rest, wab, wpb, pool_w, pool_scale, x, target, wout, mod, final_g)
    dx2, loss, dfg, dgate, dattn, stats, dpooled, dproj, dwo, dwab, dwpb, dpw, dps = outs
    return (dx2, loss, dfg, dgate, dattn, stats, dpooled, dproj, dwo.reshape(N_SHARD, D // N_SHARD, D), dwab, dwpb,
            dpw, dps)


def _pool_bwd(dpooled):
    S = dpooled.shape[0]
    tm = 1024
    nt = S // tm

    def body(dp_ref, nxt_ref, du_ref):
        i = pl.program_id(0)
        t = i * tm + lax.broadcasted_iota(jnp.int32, (tm + HALO, 1), 0)
        nxt = jnp.where(i < nt - 1, nxt_ref[...], 0.0)
        ext = jnp.concatenate([dp_ref[...], nxt], axis=0)
        for gi, win in enumerate(POOL_WINDOWS):
            cs = slice(gi * PGW, (gi + 1) * PGW)
            s = ext[:, cs] / jnp.minimum(t + 1, win).astype(F32)
            sh = 1
            while sh < win:
                s = s + pltpu.roll(s, tm + HALO - sh, 0)
                sh *= 2
            du_ref[:, cs] = (s[:tm] - dp_ref[:, cs]).astype(BF16)

    return pl.pallas_call(
        body, name="pool_bwd", grid=(nt,),
        in_specs=[pl.BlockSpec((tm, AW), lambda i: (i, 0)),
                  pl.BlockSpec((HALO, AW), lambda i: (jnp.minimum((i + 1) * (tm // HALO), S // HALO - 1), 0))],
        out_specs=pl.BlockSpec((tm, AW), lambda i: (i, 0)),
        out_shape=_sds((S, AW), BF16), compiler_params=_params("parallel"),
    )(dpooled, dpooled)


TB = 1024


def _dh_prenorm_bwd(dproj, wg_in, x, dx2, norm_g, mod, ride):
    S = dproj.shape[0]
    per = wg_in.shape[2] // TB
    nm, nk = S // TB, IN_W // TB
    rows = 256

    def body(dp_ref, w_ref, x_ref, dx2_ref, g_ref, mod_ref, gx_ref, dg_ref, dshift_ref, dscale_ref, dh_ref):
        m, kk = pl.program_id(0), pl.program_id(1)

        @pl.when(kk == 0)
        def _():
            dh_ref[...] = jnp.zeros_like(dh_ref)

        @pl.when((m == 0) & (kk == 0))
        def _():
            dg_ref[...] = jnp.zeros_like(dg_ref)
            dshift_ref[...] = jnp.zeros_like(dshift_ref)
            dscale_ref[...] = jnp.zeros_like(dscale_ref)

        dh_ref[...] += _dot_nt(dp_ref[...], w_ref[...])

        @pl.when(kk == nk - 1)
        def _():
            g = g_ref[...]
            for c in range(TB // rows):
                sl = pl.ds(c * rows, rows)
                xv = x_ref[sl, :]
                dhv = dh_ref[sl, :]
                r = lax.rsqrt(jnp.mean(xv * xv, axis=-1, keepdims=True) + EPS)
                xh = xv * r
                dshift_ref[...] += jnp.sum(dhv, axis=0, keepdims=True)
                dscale_ref[...] += jnp.sum(dhv * (xh * g), axis=0, keepdims=True)
                dn1 = dhv * (1.0 + mod_ref[:, D:2 * D])
                dg_ref[...] += jnp.sum(dn1 * xh, axis=0, keepdims=True)
                dxh = dn1 * g
                gx_ref[sl, :] = dx2_ref[sl, :] + r * (dxh - xh * jnp.mean(dxh * xh, axis=-1, keepdims=True))

    row = pl.BlockSpec((TB, D), lambda m, kk: (m, 0))
    vec = pl.BlockSpec((1, D), lambda m, kk: (0, 0))
    outs, rode = _call_with_ride(
        body, ride, lambda: (pl.program_id(0) == 0) & (pl.program_id(1) == 0),
        lambda: (pl.program_id(0) == nm - 1) & (pl.program_id(1) == nk - 1),
        name="dh", grid=(nm, nk),
        in_specs=[pl.BlockSpec((TB, TB), lambda m, kk: (m, kk)),
                  pl.BlockSpec((None, D, TB), lambda m, kk: (kk // per, 0, kk % per)),
                  row, row, vec, pl.BlockSpec((1, 3 * D), lambda m, kk: (0, 0))],
        out_specs=[row, vec, vec, vec],
        out_shape=[_sds((S, D)), _sds((1, D)), _sds((1, D)), _sds((1, D))],
        scratch_shapes=[pltpu.VMEM((TB, D), F32)], compiler_params=_params("arbitrary", "arbitrary"),
        input_output_aliases={3: 0},
    )(dproj, wg_in, x, dx2, norm_g, mod)
    return outs, rode


def _dw_in(h, dproj):
    S = dproj.shape[0]
    per = IN_W // N_SHARD // TB

    def body(h_ref, dp_ref, out_ref):
        out_ref[...] = _dot_tn(h_ref[...], dp_ref[...])

    return pl.pallas_call(
        body, name="dw_in", grid=(IN_W // TB,),
        in_specs=[pl.BlockSpec((S, D), lambda j: (0, 0)), pl.BlockSpec((S, TB), lambda j: (0, j))],
        out_specs=pl.BlockSpec((None, D, TB), lambda j: (j // per, 0, j % per)),
        out_shape=_sds((N_SHARD, D, IN_W // N_SHARD)), compiler_params=_params("parallel"),
    )(h, dproj)


def _local_step(x, target, mod, wg_in, wab, wpb, wout, pool_w, pool_scale, rel_bias, norm_g, final_g, chip_half):
    buckets = jnp.asarray(_bucket_tables())
    bias_tab = _bias_table(rel_bias, buckets)
    h = _prenorm(x, norm_g, mod)
    qkv = [_proj(h, wg_in, 3 * g, 3, BF16 if GROUPS[g][1] == 1 else F32, f"proj_qkv{g}") for g in range(NG)]
    rest = _proj(h, wg_in, NCB_QKV, REST_W // CB, F32, "proj_rest")
    os_, ls_ = zip(*[_attn_fwd(qkv[g], bias_tab, g) for g in range(NG)])
    (dx2, loss, dfinal_g, dgate, dattn, stats, dpooled, dproj, dw_out, dw_ab, dw_pb, dpool_w,
     dpool_scale) = _mix_step(x, target, os_, ls_, rest, wab, wpb, pool_w, pool_scale, wout, mod, final_g)
    du = _pool_bwd(dpooled)

    small = [dw_ab, dw_pb, dw_out]
    dqkv0, ds0, sib_small = _attn_bwd(qkv[0], dattn, stats, bias_tab, 0, _ride_sibling_halves(small))
    p_small = _pair_sum_small(small, sib_small, chip_half)
    dqkv1, ds1, u_small = _attn_bwd(qkv[1], dattn, stats, bias_tab, 1,
                                    _ride_chip_exchange([p16 for _, p16 in p_small]))
    rs_ab, rs_pb, rs_out = _chip_sum_small([p32 for p32, _ in p_small], u_small, chip_half)
    dqkv2, ds2, _ = _attn_bwd(qkv[2], dattn, stats, bias_tab, 2, None)

    for j, piece in enumerate(dqkv0 + dqkv1 + dqkv2):
        dproj = lax.dynamic_update_slice(dproj, piece.astype(BF16), (0, j * AW))
    dproj = lax.dynamic_update_slice(dproj, du, (0, QKV_W + AW))
    dw_in = _dw_in(h, dproj)
    drel_rows, (sib_in,) = _bias_grad(jnp.concatenate([ds0, ds1, ds2], axis=0), buckets,
                                      _ride_sibling_halves([dw_in]))
    drel = drel_rows[:, 0, :NUM_BUCKETS].T
    p32_in, p16_in = _pair_sum(dw_in, sib_in, chip_half, "rs_pair_sum_in")
    (grad_x, dnorm_g, dshift, dscale), (u_in,) = _dh_prenorm_bwd(dproj, wg_in, x, dx2, norm_g, mod,
                                                                 _ride_chip_exchange([p16_in]))
    rs_in = _chip_sum(p32_in, u_in, chip_half, "rs_chip_sum_in")
    dmod = jnp.concatenate([dshift, dscale, dgate], axis=1)
    return dict(loss=loss[0, 0], grad_x=grad_x, dmod=dmod, dnorm_g=dnorm_g, dfinal_g=dfinal_g, dpool_w=dpool_w,
                dpool_scale=dpool_scale, drel_bias=drel, dw_in=dw_in, dw_attn_br=dw_ab, dw_pool_br=dw_pb,
                dw_out=dw_out, rs_in=rs_in, rs_attn_br=rs_ab, rs_pool_br=rs_pb, rs_out=rs_out)


def _allgather8(blocks, name, relay=None, join=()):
    nb, nj = len(blocks), len(join)
    relay = [False] * nb if relay is None else list(relay)

    def body(*refs):
        ins, outs, joined = refs[:nb], refs[nb + nj:2 * nb + nj], refs[2 * nb + nj:2 * nb + 2 * nj]
        send_sems, recv_sems, join_send, join_recv = refs[2 * nb + 2 * nj:]
        x, y, c = lax.axis_index("x"), lax.axis_index("y"), lax.axis_index("c")
        me, sibling = (x, y, c), (x, y, 1 - c)
        here, xn, yn, dg = (x, y), (1 - x, y), (x, 1 - y), (1 - x, 1 - y)

        def slot(a, chip, core, half=None):
            ref = outs[a].at[4 * chip[0] + 2 * chip[1] + core]
            if half is None:
                return ref
            r2 = ref.shape[0] // 2
            return ref.at[pl.ds(half * r2, r2)]

        def copy(a, k, dst, to, src=None):
            return pltpu.make_async_remote_copy(src_ref=dst if src is None else src, dst_ref=dst,
                                                send_sem=send_sems.at[a, k], recv_sem=recv_sems.at[a, k],
                                                device_id=to, device_id_type=MESH)

        def start(cps):
            for cp in cps:
                cp.start()
            return cps

        swaps = []
        for a in range(nj):
            r2 = joined[a].shape[0] // 2
            rows = joined[a].at[pl.ds(c * r2, r2), :]
            swaps.append(pltpu.make_async_remote_copy(src_ref=rows, dst_ref=rows, send_sem=join_send.at[a],
                                                      recv_sem=join_recv.at[a], device_id=sibling,
                                                      device_id_type=MESH))
        start(swaps)
        sent = []
        for a in range(nb):
            own = slot(a, here, c)
            sent += [copy(a, 0, own, sibling, src=ins[a]), copy(a, 1, own, (*xn, c), src=ins[a]),
                     copy(a, 2, own, (*yn, c), src=ins[a])]
            if not relay[a]:
                sent.append(copy(a, 3, own, (*dg, c), src=ins[a]))
        start(sent)
        for a in range(nb):
            copy(a, 2, slot(a, yn, c), me).wait_recv()
            sent += start([copy(a, 6, slot(a, yn, c), sibling)]
                          + ([copy(a, 3, slot(a, yn, c, 0), (*xn, c))] if relay[a] else []))
        for a in range(nb):
            copy(a, 1, slot(a, xn, c), me).wait_recv()
            sent += start([copy(a, 5, slot(a, xn, c), sibling)]
                          + ([copy(a, 4, slot(a, xn, c, 1), (*yn, c))] if relay[a] else []))
        for a in range(nb):
            for k, half in ((3, 0), (4, 1)) if relay[a] else ((3, None),):
                copy(a, k, slot(a, dg, c, half), me).wait_recv()
                sent += start([copy(a, 4 + k, slot(a, dg, c, half), sibling)])
        for a in range(nb):
            copy(a, 0, slot(a, here, 1 - c), me).wait_recv()
            copy(a, 5, slot(a, xn, 1 - c), me).wait_recv()
            copy(a, 6, slot(a, yn, 1 - c), me).wait_recv()
            for k, half in ((7, 0), (8, 1)) if relay[a] else ((7, None),):
                copy(a, k, slot(a, dg, 1 - c, half), me).wait_recv()
        for cp in sent:
            cp.wait_send()
        for cp in swaps:
            cp.wait()

    outs = pl.pallas_call(
        body, name=name, in_specs=[ANY] * (nb + nj), out_specs=[ANY] * (nb + nj),
        out_shape=[_sds((8,) + b.shape, b.dtype) for b in blocks] + [_sds(f.shape, f.dtype) for f in join],
        input_output_aliases={nb + a: nb + a for a in range(nj)},
        scratch_shapes=[_dma_sems(nb, 9), _dma_sems(nb, 9), _dma_sems(max(nj, 1)), _dma_sems(max(nj, 1))],
    )(*blocks, *join)
    gathered = [_place_own(buf, b) for buf, b in zip(outs[:nb], blocks)]
    return (gathered, list(outs[nb:])) if nj else gathered


def _place_own(buf, block):
    dev = 4 * lax.axis_index("x") + 2 * lax.axis_index("y") + lax.axis_index("c")
    return lax.dynamic_update_index_in_dim(buf, block, dev, 0)


def _ride_sibling_halves(gs):
    def copies(ins, outs, send_sems, recv_sems):
        x, y, c = lax.axis_index("x"), lax.axis_index("y"), lax.axis_index("c")
        cps = []
        for a in range(len(gs)):
            r2 = ins[a].shape[1] // 2
            other = ins[a].at[:, pl.ds((1 - c) * r2, r2), :]
            cps.append(pltpu.make_async_remote_copy(src_ref=other, dst_ref=outs[a], send_sem=send_sems.at[a],
                                                    recv_sem=recv_sems.at[a], device_id=(x, y, 1 - c),
                                                    device_id_type=MESH))
        return cps

    return _Ride(gs, [_sds((g.shape[0], g.shape[1] // 2, g.shape[2]), g.dtype) for g in gs], len(gs), copies)


def _pair_sum(g, t, chip_half, name):
    nsh, rows, cols = g.shape
    r2 = rows // 2
    tr = _row_tile(r2, cols)
    nt = r2 // tr

    def body(ch_ref, g_ref, t_ref, p32_ref, p16_ref):
        p = g_ref[...] + t_ref[...]
        p16_ref[...] = p.astype(BF16)

        @pl.when(pl.program_id(1) == ch_ref[0])
        def _():
            p32_ref[...] = p

    blk = pl.BlockSpec((None, tr, cols), lambda i, k, ch_ref: (k, i, 0))
    return pl.pallas_call(
        body, name=name,
        grid_spec=pltpu.PrefetchScalarGridSpec(
            num_scalar_prefetch=1, grid=(nt, nsh),
            in_specs=[pl.BlockSpec((None, tr, cols), lambda i, k, ch_ref: (k, ch_ref[1] * nt + i, 0)), blk],
            out_specs=[pl.BlockSpec((tr, cols), lambda i, k, ch_ref: (i, 0)), blk]),
        out_shape=[_sds((r2, cols)), _sds((nsh, r2, cols), BF16)],
        compiler_params=_params("parallel", "arbitrary"),
    )(chip_half, g, t)


def _pair_sum_small(gs, ts, chip_half):
    na = len(gs)

    def body(ch_ref, *refs):
        g_refs, t_refs, outs = refs[:na], refs[na:2 * na], refs[2 * na:]
        for a in range(na):
            r2 = t_refs[a].shape[1]
            own = pl.ds(pl.multiple_of(ch_ref[1] * r2, 8), r2)
            outs[2 * a + 1][...] = (g_refs[a][:, own, :] + t_refs[a][...]).astype(BF16)
            outs[2 * a][...] = g_refs[a][ch_ref[0], own, :] + t_refs[a][ch_ref[0]]

    res = pl.pallas_call(
        body, name="rs_pair_sum_small",
        in_specs=[pl.BlockSpec(memory_space=pltpu.SMEM)] + [pl.BlockSpec(memory_space=pltpu.VMEM)] * (2 * na),
        out_shape=[s for t in ts for s in (_sds(t.shape[1:]), _sds(t.shape, BF16))], compiler_params=_params(),
    )(chip_half, *gs, *ts)
    return [(res[2 * a], res[2 * a + 1]) for a in range(na)]


def _chip_sum_small(p32s, us, chip_half):
    na = len(p32s)

    def body(ch_ref, *refs):
        p_refs, u_refs, outs = refs[:na], refs[na:2 * na], refs[2 * na:]
        for a in range(na):
            r2 = p_refs[a].shape[0]
            acc = p_refs[a][...]
            for j in range(3):
                acc = acc + u_refs[a][j].astype(F32)
            outs[a][pl.ds(pl.multiple_of(ch_ref[1] * r2, 8), r2), :] = acc

    return pl.pallas_call(
        body, name="rs_chip_sum_small",
        in_specs=[pl.BlockSpec(memory_space=pltpu.SMEM)] + [pl.BlockSpec(memory_space=pltpu.VMEM)] * (2 * na),
        out_shape=[_sds((2 * p.shape[0], p.shape[1])) for p in p32s], compiler_params=_params(),
    )(chip_half, *p32s, *us)


def _ride_chip_exchange(ps):
    def copies(ins, outs, send_sems, recv_sems):
        x, y, c = lax.axis_index("x"), lax.axis_index("y"), lax.axis_index("c")
        chips = [(1 - x, y), (x, 1 - y), (1 - x, 1 - y)]
        cps = []
        for a in range(len(ps)):
            for j, (ox, oy) in enumerate(chips):
                cps.append(pltpu.make_async_remote_copy(src_ref=ins[a].at[2 * ox + oy], dst_ref=outs[a].at[j],
                                                        send_sem=send_sems.at[3 * a + j],
                                                        recv_sem=recv_sems.at[3 * a + j],
                                                        device_id=(ox, oy, c), device_id_type=MESH))
        return cps

    return _Ride(ps, [_sds((3,) + p.shape[1:], p.dtype) for p in ps], 3 * len(ps), copies)


def _chip_sum(p32, u, chip_half, name):
    r2, cols = p32.shape
    tr = _row_tile(r2, cols)
    nt = r2 // tr

    def body(ch_ref, p_ref, u_ref, o_ref):
        acc = p_ref[...]
        for j in range(3):
            acc = acc + u_ref[j].astype(F32)
        o_ref[...] = acc

    return pl.pallas_call(
        body, name=name,
        grid_spec=pltpu.PrefetchScalarGridSpec(
            num_scalar_prefetch=1, grid=(nt,),
            in_specs=[pl.BlockSpec((tr, cols), lambda i, ch_ref: (i, 0)),
                      pl.BlockSpec((3, tr, cols), lambda i, ch_ref: (0, i, 0))],
            out_specs=pl.BlockSpec((tr, cols), lambda i, ch_ref: (ch_ref[1] * nt + i, 0))),
        out_shape=_sds((2 * r2, cols)), compiler_params=_params("parallel"),
    )(chip_half, p32, u)


def _row_tile(rows, cols):
    tile = rows
    while tile * cols * 4 > (2 << 20) and tile % 16 == 0:
        tile //= 2
    return tile


def _w_ada_grad(c_all, dmod_cols):
    def body(c_ref, d_ref, o_ref):
        o_ref[...] = _dot_tn(c_ref[...].astype(BF16), d_ref[...].astype(BF16))

    return pl.pallas_call(body, name="w_ada_grad", out_shape=_sds((c_all.shape[1], dmod_cols.shape[1])),
                          compiler_params=_params())(c_all, dmod_cols)


def _adam_math(w, g, m, v):
    nm = ADAM_B1 * m + (1.0 - ADAM_B1) * g
    nv = ADAM_B2 * v + (1.0 - ADAM_B2) * (g * g)
    m_hat = nm / (1.0 - ADAM_B1 ** ADAM_STEP)
    v_hat = nv / (1.0 - ADAM_B2 ** ADAM_STEP)
    return -ADAM_LR * (m_hat / (jnp.sqrt(v_hat) + ADAM_EPS) + ADAM_WD * w), nm, nv


def _adamw(w, g, m, v, name):
    rows, cols = w.shape
    tr = _row_tile(rows, cols)

    def body(w_ref, g_ref, m_ref, v_ref, go_ref, d_ref, nm_ref, nv_ref):
        gv = g_ref[...]
        go_ref[...] = gv
        d_ref[...], nm_ref[...], nv_ref[...] = _adam_math(w_ref[...], gv, m_ref[...], v_ref[...])

    spec = pl.BlockSpec((tr, cols), lambda i: (i, 0))
    return pl.pallas_call(
        body, name=name, grid=(rows // tr,), in_specs=[spec] * 4, out_specs=[spec] * 4,
        out_shape=[_sds((rows, cols))] * 4, compiler_params=_params("parallel"),
    )(w, g, m, v)


def _pack_small(dmod, dnorm_g, dfinal_g, dpool_scale, drel_bias, loss, dpool_w):
    return jnp.concatenate([dmod.reshape(-1, 128), dnorm_g.reshape(-1, 128), dfinal_g.reshape(-1, 128),
                            jnp.pad(dpool_scale.reshape(-1, 128), ((0, PK_RELB - PK_PSCALE - AW // 128), (0, 0))),
                            jnp.pad(drel_bias, ((0, 0), (0, 128 - NG * NH))),
                            jnp.full((PK_POOLW - PK_LOSS, 128), loss, F32), dpool_w.reshape(-1, 128)], axis=0)


def _small_update(small_all, ws, ms, vs):
    lane_rows = [(r0, r0 + w.shape[1] // 128) for r0, w in zip((PK_BADA, PK_NORMG, PK_FINALG, PK_PSCALE), ws)]
    nw = len(ws)

    def body(all_ref, *refs):
        w_refs, m_refs, v_refs = refs[:nw], refs[nw:2 * nw], refs[2 * nw:3 * nw]
        loss_ref, outs = refs[3 * nw], refs[3 * nw + 1:]
        g = all_ref[0]
        for s in range(1, all_ref.shape[0]):
            g = g + all_ref[s]
        loss_ref[...] = jnp.broadcast_to(g[PK_LOSS:PK_LOSS + 1, :], loss_ref.shape)

        def put(p, at, gv):
            d, nm, nv = _adam_math(w_refs[p][at], gv, m_refs[p][at], v_refs[p][at])
            for o_ref, val in zip(outs[4 * p:4 * p + 4], (gv, d, nm, nv)):
                o_ref[at] = val

        for p, (r0, r1) in enumerate(lane_rows):
            for i in range(r1 - r0):
                put(p, (slice(None), slice(128 * i, 128 * (i + 1))), g[r0 + i:r0 + i + 1, :])
        put(4, (slice(None), slice(None)), g[PK_RELB:PK_LOSS, 0:NG * NH])
        put(5, (slice(None), slice(None)), g[PK_POOLW:PK_ROWS, :])

    res = pl.pallas_call(
        body, name="small_update",
        out_shape=[_sds((8, 128))] + [_sds(w.shape) for w in ws for _ in range(4)], compiler_params=_params(),
    )(small_all, *ws, *ms, *vs)
    return res[0], [res[1 + 4 * p:5 + 4 * p] for p in range(nw)]


def kernel(x, c, norm_g, w_ada, b_ada, w_in, pool_w, pool_scale, w_attn_br, w_pool_br, w_out, rel_bias, final_g, loss_target, m_norm_g, m_w_ada, m_b_ada, m_w_in, m_pool_w, m_pool_scale, m_w_attn_br, m_w_pool_br, m_w_out, m_rel_bias, m_final_g, v_norm_g, v_w_ada, v_b_ada, v_w_in, v_pool_w, v_pool_scale, v_w_attn_br, v_w_pool_br, v_w_out, v_rel_bias, v_final_g):
    ix, iy, ic = lax.axis_index("x"), lax.axis_index("y"), lax.axis_index("c")
    dev = 4 * ix + 2 * iy + ic
    chip = 2 * ix + iy

    def half(w):
        r2 = w.shape[0] // 2
        return lax.dynamic_slice_in_dim(w, ic * r2, r2, axis=0).astype(BF16)

    gathered = _allgather8([jnp.broadcast_to(c, (8, D)), half(w_in[0]), half(w_attn_br[0]), half(w_pool_br[0]),
                            half(w_out[0])], "gather_weights", relay=[False, True, True, True, True])
    c_all = gathered[0][:, 0, :]
    wg_in = gathered[1].reshape(N_SHARD, D, IN_W // N_SHARD)
    wab = gathered[2].reshape(N_SHARD, AW, D // N_SHARD).transpose(1, 0, 2).reshape(AW, D)
    wpb = gathered[3].reshape(N_SHARD, AW, D // N_SHARD).transpose(1, 0, 2).reshape(AW, D)
    wout = gathered[4].reshape(D, D)

    mw = 3 * D // N_SHARD
    modp = _mod_partial(c_all, w_ada[0], lax.dynamic_slice_in_dim(b_ada, chip * mw, mw, axis=1))
    mod_all = _allgather8([modp], "gather_mod")[0]
    mod_full = mod_all[::2].transpose(1, 0, 2).reshape(8, 3 * D)
    mod = lax.dynamic_slice_in_dim(mod_full, dev, 1, axis=0)

    chip_half = jnp.stack([chip, ic]).astype(jnp.int32)
    r = _local_step(x[0], loss_target[0], mod, wg_in, wab, wpb, wout, pool_w[0], pool_scale, rel_bias, norm_g,
                    final_g.reshape(1, D), chip_half)

    packed = _pack_small(r["dmod"], r["dnorm_g"], r["dfinal_g"], r["dpool_scale"], r["drel_bias"], r["loss"],
                         r["dpool_w"])
    (small_all,), (g_w_in, g_w_ab, g_w_pb, g_w_out) = _allgather8(
        [packed], "gather_small", join=[r["rs_in"], r["rs_attn_br"], r["rs_pool_br"], r["rs_out"]])
    small = ["b_ada", "norm_g", "final_g", "pool_scale", "rel_bias", "pool_w"]
    shaped = lambda b, n, f, ps, rb, pw: [b, n, f.reshape(1, D), ps, rb, pw.reshape(4 * PGW, PGW)]
    loss, small_out = _small_update(small_all, shaped(b_ada, norm_g, final_g, pool_scale, rel_bias, pool_w),
                                    shaped(m_b_ada, m_norm_g, m_final_g, m_pool_scale, m_rel_bias, m_pool_w),
                                    shaped(v_b_ada, v_norm_g, v_final_g, v_pool_scale, v_rel_bias, v_pool_w))
    dmod_all = small_all[:, PK_BADA:PK_NORMG, :].reshape(8, 3 * D)
    g_w_ada = _w_ada_grad(c_all, lax.dynamic_slice_in_dim(dmod_all, chip * mw, mw, axis=1))

    upd = dict(zip(small, small_out))
    upd["final_g"] = [a.reshape(D) for a in upd["final_g"]]
    upd["pool_w"] = [a.reshape(1, 4, PGW, PGW) for a in upd["pool_w"]]
    for nme, w, g, m, v in (("w_ada", w_ada, g_w_ada, m_w_ada, v_w_ada), ("w_in", w_in, g_w_in, m_w_in, v_w_in),
                            ("w_attn_br", w_attn_br, g_w_ab, m_w_attn_br, v_w_attn_br),
                            ("w_pool_br", w_pool_br, g_w_pb, m_w_pool_br, v_w_pool_br),
                            ("w_out", w_out, g_w_out, m_w_out, v_w_out)):
        upd[nme] = [a[None] for a in _adamw(w[0], g, m[0], v[0], "adamw_" + nme)]
    names = ["norm_g", "w_ada", "b_ada", "w_in", "pool_w", "pool_scale", "w_attn_br", "w_pool_br", "w_out",
             "rel_bias", "final_g"]
    return (loss[0, 0], r["grad_x"][None]) + tuple(upd[nme][kind] for kind in range(4) for nme in names)
```

```python
import math

import numpy as np
import jax
import jax.numpy as jnp
from jax import lax
from jax.experimental import pallas as pl
from jax.experimental.pallas import tpu as pltpu

F32 = jnp.float32
BF16 = jnp.bfloat16

D = 1024
HD = 64
NH = 8
AW = NH * HD
GROUPS = ((128, 1), (512, 4), (2048, 16))
NG = len(GROUPS)
BLK = 128
GW = 3 * AW
QKV_W = NG * GW
REST_W = 3584
IN_W = QKV_W + REST_W
CB = 512
NCB_QKV = QKV_W // CB
POOL_WINDOWS = (2, 4, 8, 16)
PGW = 128
HALO = 16
NUM_BUCKETS = 32
MAX_DISTANCE = 2048
EPS = 1e-6
NEG = -1e30
N_SHARD = 4
VMEM_LIMIT = 56 * 1024 * 1024

ADAM_LR = 0.001
ADAM_B1 = 0.9
ADAM_B2 = 0.999
ADAM_EPS = 1e-08
ADAM_WD = 0.01
ADAM_STEP = 10

PK_BADA, PK_NORMG, PK_FINALG, PK_PSCALE, PK_RELB, PK_LOSS, PK_POOLW, PK_ROWS = 0, 24, 32, 40, 48, 80, 88, 600

ANY = pl.BlockSpec(memory_space=pl.ANY)
MESH = pl.DeviceIdType.MESH


def _params(*sem):
    return pltpu.CompilerParams(dimension_semantics=sem, vmem_limit_bytes=VMEM_LIMIT)


def _sds(shape, dtype=F32):
    return jax.ShapeDtypeStruct(shape, dtype)


def _dot(a, b):
    return jnp.dot(a, b, preferred_element_type=F32)


def _dot_nt(a, b):
    return lax.dot_general(a, b, (((1,), (1,)), ((), ())), preferred_element_type=F32)


def _dot_tn(a, b):
    return lax.dot_general(a, b, (((0,), (0,)), ((), ())), preferred_element_type=F32)


def _sigmoid(z):
    return 0.5 * jnp.tanh(0.5 * z) + 0.5


def _dma_sems(*shape):
    return pltpu.SemaphoreType.DMA(shape)


class _Ride:
    def __init__(self, arrays, out_shapes, n_copies, copies):
        self.arrays, self.out_shapes, self.n_copies, self.copies = list(arrays), list(out_shapes), n_copies, copies


def _call_with_ride(body, ride, first, last, *, in_specs, out_specs, out_shape, scratch_shapes=(), **kw):
    in_specs, out_specs, out_shape, scratch_shapes = list(in_specs), list(out_specs), list(out_shape), list(scratch_shapes)
    n_in, n_out, n_sc = len(in_specs), len(out_specs), len(scratch_shapes)
    if ride is None:
        def run_plain(*operands):
            return pl.pallas_call(body, in_specs=in_specs, out_specs=out_specs, out_shape=out_shape,
                                  scratch_shapes=scratch_shapes, **kw)(*operands), []
        return run_plain
    n_ri, n_ro = len(ride.arrays), len(ride.out_shapes)

    def wrapped(*refs):
        ins, rest = refs[:n_in], refs[n_in:]
        r_ins, rest = rest[:n_ri], rest[n_ri:]
        outs, rest = rest[:n_out], rest[n_out:]
        r_outs, rest = rest[:n_ro], rest[n_ro:]
        scratch, (send_sems, recv_sems) = rest[:n_sc], rest[n_sc:]

        @pl.when(first())
        def _():
            for cp in ride.copies(r_ins, r_outs, send_sems, recv_sems):
                cp.start()

        body(*ins, *outs, *scratch)

        @pl.when(last())
        def _():
            for cp in ride.copies(r_ins, r_outs, send_sems, recv_sems):
                cp.wait()

    def run(*operands):
        res = pl.pallas_call(
            wrapped, in_specs=in_specs + [ANY] * n_ri, out_specs=out_specs + [ANY] * n_ro,
            out_shape=out_shape + ride.out_shapes,
            scratch_shapes=scratch_shapes + [_dma_sems(ride.n_copies), _dma_sems(ride.n_copies)], **kw,
        )(*operands, *ride.arrays)
        return res[:n_out], res[n_out:]
    return run


def _bucket_tables():
    i = np.arange(BLK)[:, None]
    j = np.arange(2 * BLK)[None, :]
    dist = BLK + i - j
    valid = (dist >= 0) & (dist <= BLK)
    tabs = []
    for _, dil in GROUPS:
        n = (np.clip(dist, 0, BLK) * dil).astype(np.int32)
        max_exact = NUM_BUCKETS // 2
        nf = np.maximum(n, 1).astype(np.float32)
        large = max_exact + (np.log(nf / np.float32(max_exact)) / np.float32(math.log(MAX_DISTANCE / max_exact))
                             * np.float32(NUM_BUCKETS - max_exact)).astype(np.int32)
        large = np.minimum(large, NUM_BUCKETS - 1)
        bucket = np.where(n < max_exact, n, large)
        tab = np.where(valid, bucket, -1).astype(np.int32)
        perm = _block_perm(dil)
        tabs.append(tab[perm][:, np.concatenate([perm, BLK + perm])])
    return np.stack(tabs)


def _bias_table(rel_bias, buckets):
    def body(rb_ref, bk_ref, out_ref):
        g = pl.program_id(0)
        bk = bk_ref[...]
        for h in range(NH):
            acc = jnp.full((BLK, 2 * BLK), NEG, F32)
            for b in range(NUM_BUCKETS):
                acc = jnp.where(bk == b, rb_ref[b, g * NH + h], acc)
            out_ref[h] = acc

    return pl.pallas_call(
        body, name="bias_table", grid=(NG,),
        in_specs=[pl.BlockSpec(memory_space=pltpu.SMEM),
                  pl.BlockSpec((None, BLK, 2 * BLK), lambda g: (g, 0, 0))],
        out_specs=pl.BlockSpec((NH, BLK, 2 * BLK), lambda g: (g, 0, 0)),
        out_shape=_sds((NG * NH, BLK, 2 * BLK)),
        compiler_params=_params("arbitrary"),
    )(rel_bias, buckets)


def _bias_grad(ds_acc, buckets, ride):
    def body(acc_ref, bk_ref, out_ref):
        bk = bk_ref[...]
        acc = acc_ref[...]
        lane = lax.broadcasted_iota(jnp.int32, (8, 128), 1)
        out = jnp.zeros((8, 128), F32)
        for b in range(NUM_BUCKETS):
            val = jnp.sum(jnp.where(bk == b, acc, 0.0))
            out = jnp.where(lane == b, val, out)
        out_ref[...] = out

    (out,), rode = _call_with_ride(
        body, ride, lambda: pl.program_id(0) == 0, lambda: pl.program_id(0) == NG * NH - 1,
        name="bias_grad", grid=(NG * NH,),
        in_specs=[pl.BlockSpec((None, BLK, 2 * BLK), lambda gh: (gh, 0, 0)),
                  pl.BlockSpec((None, BLK, 2 * BLK), lambda gh: (gh // NH, 0, 0))],
        out_specs=[pl.BlockSpec((None, 8, 128), lambda gh: (gh, 0, 0))],
        out_shape=[_sds((NG * NH, 8, 128))],
        compiler_params=_params("arbitrary"),
    )(ds_acc, buckets)
    return out, rode


def _mod_partial(c_all, w_ada_s, b_ada_s):
    def body(c_ref, w_ref, b_ref, o_ref):
        o_ref[...] = _dot(c_ref[...].astype(BF16), w_ref[...].astype(BF16)) + b_ref[...]

    return pl.pallas_call(body, name="mod_partial", out_shape=_sds((8, w_ada_s.shape[1])),
                          compiler_params=_params())(c_all, w_ada_s, b_ada_s)


def _prenorm(x, norm_g, mod):
    S = x.shape[0]
    tm = 1024

    def body(x_ref, g_ref, mod_ref, h_ref):
        xv = x_ref[...]
        r = lax.rsqrt(jnp.mean(xv * xv, axis=-1, keepdims=True) + EPS)
        n1 = xv * r * g_ref[...]
        h_ref[...] = (n1 * (1.0 + mod_ref[:, D:2 * D]) + mod_ref[:, 0:D]).astype(BF16)

    return pl.pallas_call(
        body, name="prenorm", grid=(S // tm,),
        in_specs=[pl.BlockSpec((tm, D), lambda i: (i, 0)), pl.BlockSpec((1, D), lambda i: (0, 0)),
                  pl.BlockSpec((1, 3 * D), lambda i: (0, 0))],
        out_specs=pl.BlockSpec((tm, D), lambda i: (i, 0)),
        out_shape=_sds((S, D), BF16), compiler_params=_params("parallel"),
    )(x, norm_g, mod)


def _proj(h, wg_in, j0, nj, dtype, name):
    S = h.shape[0]
    tm = S
    per = wg_in.shape[2] // CB

    def body(h_ref, w_ref, o_ref):
        o_ref[...] = _dot(h_ref[...], w_ref[...]).astype(dtype)

    return pl.pallas_call(
        body, name=name, grid=(S // tm, nj),
        in_specs=[pl.BlockSpec((tm, D), lambda m, j: (m, 0)),
                  pl.BlockSpec((None, D, CB), lambda m, j: ((j0 + j) // per, 0, (j0 + j) % per))],
        out_specs=pl.BlockSpec((tm, CB), lambda m, j: (m, j)),
        out_shape=_sds((S, nj * CB), dtype), compiler_params=_params("parallel", "parallel"),
    )(h, wg_in)


HS = 4
SLAB = HS * HD


def _lane_head(rows):
    return lax.broadcasted_iota(jnp.int32, (rows, SLAB), 1) // HD


def _head_stack(a):
    head = _lane_head(a.shape[0])
    return jnp.concatenate([jnp.where(head == h, a, jnp.zeros_like(a)) for h in range(HS)], axis=0)


def _head_unstack(a):
    rows = a.shape[0] // HS
    head = _lane_head(rows)
    out = a[:rows]
    for h in range(1, HS):
        out = jnp.where(head == h, a[h * rows:(h + 1) * rows], out)
    return out


STAT_W = 128
VIEW = 16


def _sub_layout(dil):
    if dil == 1:
        return BLK, [None]
    return BLK * dil // VIEW, [[r + dil * u for u in range(VIEW // dil)] for r in range(dil)]


def _block_perm(dil):
    a_rows, _ = _sub_layout(dil)
    p = np.arange(BLK)
    return p if dil == 1 else (VIEW // dil) * (p % a_rows) + p // a_rows


LB = 128
N_SLAB = NH // HS


RBS = 4


def _ld(refs, bs, s, w, rb=0):
    if bs is None:
        return refs[0][rb * BLK:(rb + 1) * BLK, s * w:(s + 1) * w]
    a_rows = refs[0].shape[0] // VIEW
    return jnp.concatenate([jnp.concatenate([ref[pl.ds(b, a_rows, stride=VIEW), :] for b in bs], axis=0)
                            for ref in refs[s * (w // LB):(s + 1) * (w // LB)]], axis=1)


def _st(ref, bs, s, val, rb=0):
    if bs is None:
        ref[rb * BLK:(rb + 1) * BLK, s * SLAB:(s + 1) * SLAB] = val.astype(ref.dtype)
        return
    a_rows = val.shape[0] // len(bs)
    for u, b in enumerate(bs):
        ref[:, b, s * SLAB:(s + 1) * SLAB] = val[u * a_rows:(u + 1) * a_rows]


def _attn_views(dil, S):
    a_rows, subs = _sub_layout(dil)
    if dil == 1:
        def ispecs(base, w, f):
            return [pl.BlockSpec((RBS * BLK, N_SLAB * w), lambda sg, n: (f(n), base // (N_SLAB * w)))]
        return subs, S // (RBS * BLK), N_SLAB, RBS, ispecs, (lambda w: (S, w)), (
            lambda f: pl.BlockSpec((RBS * BLK, AW), lambda sg, n: (f(n), 0)))

    sps = N_SLAB if dil < VIEW else 1

    def ispecs(base, w, f):
        return [pl.BlockSpec((a_rows * VIEW, LB), lambda sg, n, k=k: (f(n), (base + sg * sps * w) // LB + k))
                for k in range(sps * w // LB)]
    return subs, S // (a_rows * VIEW), sps, 1, ispecs, (lambda w: (S // VIEW, VIEW, w)), (
        lambda f: pl.BlockSpec((a_rows, VIEW, sps * SLAB), lambda sg, n: (f(n), 0, sg)))


def _attn_fwd(qkv_g, bias_tab, g):
    S = qkv_g.shape[0]
    subs, nbq, sps, rbs, ispecs, shape, ospec = _attn_views(GROUPS[g][1], S)
    cur = lambda n: n
    in_specs = [ispecs(0, SLAB, cur), ispecs(AW, SLAB, cur), ispecs(2 * AW, SLAB, cur)]
    nl = len(in_specs[0])

    def body(*refs):
        q, k, v = (refs[t * nl:(t + 1) * nl] for t in range(3))
        b_ref, o_ref, l_ref, kprev, vprev = refs[3 * nl:]
        n = pl.program_id(1)

        @pl.when(n == 0)
        def _():
            kprev[...] = jnp.zeros_like(kprev)
            vprev[...] = jnp.zeros_like(vprev)

        col = lax.broadcasted_iota(jnp.int32, (HS * BLK, 2 * BLK), 1)
        first = (col >= BLK) | (n > 0)
        for s_, rb, (i, bs) in ((s_, rb, sub) for s_ in range(sps) for rb in range(rbs) for sub in enumerate(subs)):
            cs = slice(s_ * SLAB, (s_ + 1) * SLAB)
            kc, vc = _ld(k, bs, s_, SLAB, rb).astype(BF16), _ld(v, bs, s_, SLAB, rb).astype(BF16)
            kb = jnp.concatenate([kprev[i, :, cs], kc], axis=0)
            vb = jnp.concatenate([vprev[i, :, cs], vc], axis=0)
            kprev[i, :, cs], vprev[i, :, cs] = kc, vc
            s = _dot_nt(_head_stack(_ld(q, bs, s_, SLAB, rb).astype(BF16)), kb) * (HD ** -0.5)
            s = s + b_ref[pl.ds(s_ * HS, HS)].reshape(HS * BLK, 2 * BLK)
            if rb == 0:
                s = jnp.where(first, s, NEG)
            m = jnp.max(s, axis=-1, keepdims=True)
            p = jnp.exp(s - m)
            den = jnp.sum(p, axis=-1, keepdims=True)
            _st(o_ref, bs, s_, _head_unstack(_dot(p.astype(BF16), vb) / den), rb)
            _st(l_ref, bs, s_, _head_unstack(jnp.broadcast_to(m + jnp.log(den), (HS * BLK, SLAB))), rb)

    out = _sds(shape(AW))
    nsg = N_SLAB // sps
    o, l = pl.pallas_call(
        body, name=f"attn_fwd{g}", grid=(nsg, nbq),
        in_specs=sum(in_specs, []) + [pl.BlockSpec((sps * HS, BLK, 2 * BLK), lambda sg, n: (g * nsg + sg, 0, 0))],
        out_specs=[ospec(cur), ospec(cur)],
        out_shape=[out, out],
        scratch_shapes=[pltpu.VMEM((len(subs), BLK, sps * SLAB), BF16)] * 2,
        compiler_params=_params("parallel", "arbitrary"),
    )(*([qkv_g] * (3 * nl)), bias_tab)
    return o.reshape(S, AW), l.reshape(S, AW)


def _attn_bwd(qkv_g, dattn, stats, bias_tab, g, ride):
    S = qkv_g.shape[0]
    subs, nbq, sps, rbs, ispecs, shape, ospec = _attn_views(GROUPS[g][1], S)
    cur = lambda n: jnp.minimum(n, nbq - 1)
    late = lambda n: jnp.maximum(n - 1, 0)
    in_specs = [ispecs(0, SLAB, cur), ispecs(AW, SLAB, cur), ispecs(2 * AW, SLAB, cur), ispecs(0, SLAB, cur),
                ispecs(0, STAT_W, cur)]
    nl = len(in_specs[0])

    def body(*refs):
        q, k, v, da = (refs[t * nl:(t + 1) * nl] for t in range(4))
        nst = len(in_specs[4])
        st_refs = refs[4 * nl:4 * nl + nst]
        b_ref, dq_ref, dk_ref, dv_ref, ds_ref, ck_ref, cv_ref, kprev, vprev, *held = refs[4 * nl + nst:]
        n = pl.program_id(1)

        @pl.when(n == 0)
        def _():
            for ref in (ds_ref, ck_ref, cv_ref, kprev, vprev, *held):
                ref[...] = jnp.zeros_like(ref)

        def finish(ref, t, bs, s_, rb, val):
            cs = slice(s_ * SLAB, (s_ + 1) * SLAB)
            if rbs == 1:
                _st(ref, bs, s_, val)
            elif rb == 0:
                for j in range(rbs - 1):
                    _st(ref, bs, s_, held[t][j * BLK:(j + 1) * BLK, cs], j)
                _st(ref, bs, s_, val, rbs - 1)
            else:
                held[t][(rb - 1) * BLK:rb * BLK, cs] = val

        @pl.when(n < nbq)
        def _():
            col = lax.broadcasted_iota(jnp.int32, (HS * BLK, 2 * BLK), 1)
            first = (col >= BLK) | (n > 0)
            for s_, rb, (i, bs) in ((s_, rb, sub) for s_ in range(sps) for rb in range(rbs) for sub in enumerate(subs)):
                cs = slice(s_ * SLAB, (s_ + 1) * SLAB)
                st = _ld(st_refs, bs, s_, STAT_W, rb)
                kc, vc = _ld(k, bs, s_, SLAB, rb).astype(BF16), _ld(v, bs, s_, SLAB, rb).astype(BF16)
                kb = jnp.concatenate([kprev[i, :, cs], kc], axis=0)
                vb = jnp.concatenate([vprev[i, :, cs], vc], axis=0)
                kprev[i, :, cs], vprev[i, :, cs] = kc, vc
                lse = jnp.concatenate([st[:, h:h + 1] for h in range(HS)], axis=0)
                delta = jnp.concatenate([st[:, HS + h:HS + h + 1] for h in range(HS)], axis=0)
                qs = _head_stack(_ld(q, bs, s_, SLAB, rb).astype(BF16))
                dos = _head_stack(_ld(da, bs, s_, SLAB, rb).astype(BF16))
                s = _dot_nt(qs, kb) * (HD ** -0.5) + b_ref[pl.ds(s_ * HS, HS)].reshape(HS * BLK, 2 * BLK)
                if rb == 0:
                    s = jnp.where(first, s, NEG)
                p = jnp.exp(s - lse)
                ds = p * (_dot_nt(dos, vb) - delta)
                ds_ref[pl.ds(s_ * HS, HS)] += ds.reshape(HS, BLK, 2 * BLK)
                ds_b = (ds * (HD ** -0.5)).astype(BF16)
                _st(dq_ref, bs, s_, _head_unstack(_dot(ds_b, kb)), rb)
                dkb = _dot_tn(ds_b, qs)
                dvb = _dot_tn(p.astype(BF16), dos)
                finish(dk_ref, 0, bs, s_, rb, ck_ref[i, :, cs] + dkb[:BLK])
                finish(dv_ref, 1, bs, s_, rb, cv_ref[i, :, cs] + dvb[:BLK])
                ck_ref[i, :, cs] = dkb[BLK:]
                cv_ref[i, :, cs] = dvb[BLK:]

        @pl.when(n == nbq)
        def _():
            for s_ in range(sps):
                for i, bs in enumerate(subs):
                    finish(dk_ref, 0, bs, s_, 0, ck_ref[i, :, s_ * SLAB:(s_ + 1) * SLAB])
                    finish(dv_ref, 1, bs, s_, 0, cv_ref[i, :, s_ * SLAB:(s_ + 1) * SLAB])

    out = _sds(shape(AW), BF16 if GROUPS[g][1] == 1 else F32)
    nsg = N_SLAB // sps
    (dq, dk, dv, ds_acc), rode = _call_with_ride(
        body, ride, lambda: (pl.program_id(0) == 0) & (pl.program_id(1) == 0),
        lambda: (pl.program_id(0) == nsg - 1) & (pl.program_id(1) == nbq),
        name=f"attn_bwd{g}", grid=(nsg, nbq + 1),
        in_specs=sum(in_specs, []) + [pl.BlockSpec((sps * HS, BLK, 2 * BLK), lambda sg, n: (g * nsg + sg, 0, 0))],
        out_specs=[ospec(cur), ospec(late), ospec(late),
                   pl.BlockSpec((sps * HS, BLK, 2 * BLK), lambda sg, n: (sg, 0, 0))],
        out_shape=[out] * 3 + [_sds((NH, BLK, 2 * BLK))],
        scratch_shapes=[pltpu.VMEM((len(subs), BLK, sps * SLAB), F32)] * 2
        + [pltpu.VMEM((len(subs), BLK, sps * SLAB), BF16)] * 2 + [pltpu.VMEM(((rbs - 1) * BLK, sps * SLAB), F32)] * (2 if rbs > 1 else 0),
        compiler_params=_params("arbitrary", "arbitrary"),
    )(*([qkv_g] * (3 * nl)), *([dattn] * nl), *([stats] * len(in_specs[4])), bias_tab)
    return [dq.reshape(S, AW), dk.reshape(S, AW), dv.reshape(S, AW)], ds_acc, rode


TM_MIX = 256


def _mix_specs(tm):
    row512 = pl.BlockSpec((tm, AW), lambda i: (i, 0))
    return ([row512] * 6 + [
        pl.BlockSpec((tm, REST_W), lambda i: (i, 0)),
        pl.BlockSpec((HALO, AW), lambda i: (jnp.maximum(i * (tm // HALO) - 1, 0), 1)),
        pl.BlockSpec((AW, D), lambda i: (0, 0)), pl.BlockSpec((AW, D), lambda i: (0, 0)),
        pl.BlockSpec((4, PGW, PGW), lambda i: (0, 0, 0)), pl.BlockSpec((1, AW), lambda i: (0, 0))])


def _mix_forward(i, tm, o_refs, l_refs, rest_ref, halo_ref, wab_ref, wpb_ref, pw_ref, ps_ref):
    l0, l1, l2 = (r[...] for r in l_refs)
    mx = jnp.maximum(jnp.maximum(l0, l1), l2)
    e0, e1, e2 = jnp.exp(l0 - mx), jnp.exp(l1 - mx), jnp.exp(l2 - mx)
    den = e0 + e1 + e2
    lj = mx + jnp.log(den)
    attn = (e0 * o_refs[0][...] + e1 * o_refs[1][...] + e2 * o_refs[2][...]) / den

    z_attn = rest_ref[:, 0:AW]
    u = rest_ref[:, AW:2 * AW]
    z_pool = rest_ref[:, 2 * AW:3 * AW]
    g_attn = rest_ref[:, 3 * AW:3 * AW + D]
    g_pool = rest_ref[:, 3 * AW + D:3 * AW + 2 * D]

    sg_a = _sigmoid(z_attn)
    sil_a = z_attn * sg_a
    a_g = (attn * sil_a).astype(BF16)
    y_attn = _dot(a_g, wab_ref[...])

    halo = jnp.where(i > 0, halo_ref[...], 0.0)
    ext = jnp.concatenate([halo, u], axis=0)
    t = i * tm + lax.broadcasted_iota(jnp.int32, (tm, 1), 0)
    pooled, mixed_raw = [], []
    for gi, win in enumerate(POOL_WINDOWS):
        s = ext[:, gi * PGW:(gi + 1) * PGW]
        sh = 1
        while sh < win:
            s = s + pltpu.roll(s, sh, 0)
            sh *= 2
        cnt = jnp.minimum(t + 1, win).astype(F32)
        pg = s[HALO:] / cnt - u[:, gi * PGW:(gi + 1) * PGW]
        pooled.append(pg.astype(BF16))
        mixed_raw.append(_dot(pooled[-1], pw_ref[gi].astype(BF16)))
    mixed_raw = jnp.concatenate(mixed_raw, axis=1)
    mixed = mixed_raw * ps_ref[...]
    sg_p = _sigmoid(z_pool)
    sil_p = z_pool * sg_p
    m_g = (mixed * sil_p).astype(BF16)
    y_pool = _dot(m_g, wpb_ref[...])

    sa = _sigmoid(g_attn)
    sp = _sigmoid(g_pool)
    merged = sa * y_attn + sp * y_pool
    return dict(lj=lj, attn=attn, z_attn=z_attn, z_pool=z_pool, sg_a=sg_a, sil_a=sil_a, a_g=a_g, y_attn=y_attn,
                pooled=pooled, mixed_raw=mixed_raw, mixed=mixed, sg_p=sg_p, sil_p=sil_p, m_g=m_g, y_pool=y_pool,
                sa=sa, sp=sp, merged=merged)


def _mix_step(x, target, os_, ls_, rest, wab, wpb, pool_w, pool_scale, wout, mod, final_g):
    S = x.shape[0]
    tm = TM_MIX
    nt = S // tm
    sw = D // N_SHARD

    def body(o0, o1, o2, l0, l1, l2, rest_ref, halo_ref, wab_ref, wpb_ref, pw_ref, ps_ref,
             x_ref, t_ref, wo_ref, mod_ref, fg_ref, dx2_ref, loss_ref, dfg_ref, dgate_ref,
             dattn_ref, stats_ref, dpooled_ref, dproj_hbm, dwo_hbm, dwab_hbm, dwpb_hbm, dpw_ref, dps_ref,
             awo, awab, awpb, stage, stage_sem):
        i = pl.program_id(0)
        slot = i % 2

        def staged(step, sl):
            return pltpu.make_async_copy(stage.at[sl], dproj_hbm.at[pl.ds(step * tm, tm), pl.ds(QKV_W, REST_W)],
                                         stage_sem.at[sl])

        @pl.when(i == 0)
        def _():
            for ref in (loss_ref, dfg_ref, dgate_ref, awo, awab, awpb, dpw_ref, dps_ref):
                ref[...] = jnp.zeros_like(ref)

        f = _mix_forward(i, tm, (o0, o1, o2), (l0, l1, l2), rest_ref, halo_ref, wab_ref, wpb_ref, pw_ref, ps_ref)
        mo = _dot(f["merged"].astype(BF16), wo_ref[...])
        gate = mod_ref[:, 2 * D:3 * D]
        fg = fg_ref[...]
        x2 = x_ref[...] + gate * mo
        r2 = lax.rsqrt(jnp.mean(x2 * x2, axis=-1, keepdims=True) + EPS)
        n2 = x2 * r2
        err = n2 * fg - t_ref[...]
        loss_ref[...] += 0.5 * jnp.sum(jnp.mean(err * err, axis=-1, keepdims=True))
        dy = err * (1.0 / D)
        dfg_ref[...] += jnp.sum(dy * n2, axis=0, keepdims=True)
        dn = dy * fg
        dx2 = r2 * (dn - n2 * jnp.mean(dn * n2, axis=-1, keepdims=True))
        dgate_ref[...] += jnp.sum(dx2 * mo, axis=0, keepdims=True)
        dx2_ref[...] = dx2

        dmo_b = (dx2 * gate).astype(BF16)
        dmerged = _dot_nt(dmo_b, wo_ref[...])
        awo[...] += _dot_tn(f["merged"].astype(BF16), dmo_b)
        sa, sp = f["sa"], f["sp"]
        dya = (dmerged * sa).astype(BF16)
        dyp = (dmerged * sp).astype(BF16)
        dg_attn = dmerged * f["y_attn"] * sa * (1.0 - sa)
        dg_pool = dmerged * f["y_pool"] * sp * (1.0 - sp)
        dag = _dot_nt(dya, wab_ref[...])
        awab[...] += _dot_tn(f["a_g"], dya)
        dmg = _dot_nt(dyp, wpb_ref[...])
        awpb[...] += _dot_tn(f["m_g"], dyp)
        dattn = dag * f["sil_a"]
        dattn_ref[...] = dattn
        prod = dattn * f["attn"]
        lane = lax.broadcasted_iota(jnp.int32, (tm, STAT_W), 1)
        for sb in range(N_SLAB):
            st = jnp.zeros((tm, STAT_W), F32)
            for h in range(HS):
                hs = slice((sb * HS + h) * HD, (sb * HS + h + 1) * HD)
                st = jnp.where(lane == h, f["lj"][:, hs.start:hs.start + 1], st)
                st = jnp.where(lane == HS + h, jnp.sum(prod[:, hs], axis=-1, keepdims=True), st)
            stats_ref[:, sb * STAT_W:(sb + 1) * STAT_W] = st
        dz_attn = dag * f["attn"] * (f["sg_a"] * (1.0 + f["z_attn"] * (1.0 - f["sg_a"])))
        dmixed = dmg * f["sil_p"]
        dz_pool = dmg * f["mixed"] * (f["sg_p"] * (1.0 + f["z_pool"] * (1.0 - f["sg_p"])))
        dps_ref[...] += jnp.sum(dmixed * f["mixed_raw"], axis=0, keepdims=True)
        dpm = (dmixed * ps_ref[...]).astype(BF16)
        for gi in range(len(POOL_WINDOWS)):
            cs = slice(gi * PGW, (gi + 1) * PGW)
            dpw_ref[gi] += _dot_tn(f["pooled"][gi], dpm[:, cs])
            dpooled_ref[:, cs] = _dot_nt(dpm[:, cs], pw_ref[gi].astype(BF16))
        @pl.when(i >= 2)
        def _():
            staged(i - 2, slot).wait()

        stage[slot, :, 0:AW] = dz_attn.astype(BF16)
        stage[slot, :, AW:2 * AW] = jnp.zeros((tm, AW), BF16)
        stage[slot, :, 2 * AW:3 * AW] = dz_pool.astype(BF16)
        stage[slot, :, 3 * AW:3 * AW + D] = dg_attn.astype(BF16)
        stage[slot, :, 3 * AW + D:3 * AW + 2 * D] = dg_pool.astype(BF16)
        staged(i, slot).start()

        @pl.when(i == nt - 1)
        def _():
            staged(i - 1, 1 - slot).wait()
            staged(i, slot).wait()
            pltpu.sync_copy(awo, dwo_hbm)
            for k in range(N_SHARD):
                pltpu.sync_copy(awab.at[:, pl.ds(k * sw, sw)], dwab_hbm.at[k])
                pltpu.sync_copy(awpb.at[:, pl.ds(k * sw, sw)], dwpb_hbm.at[k])

    row = pl.BlockSpec((tm, D), lambda i: (i, 0))
    vec = pl.BlockSpec((1, D), lambda i: (0, 0))
    row512 = pl.BlockSpec((tm, AW), lambda i: (i, 0))
    outs = pl.pallas_call(
        body, name="mix_step", grid=(nt,),
        in_specs=_mix_specs(tm) + [row, row, pl.BlockSpec((D, D), lambda i: (0, 0)),
                                   pl.BlockSpec((1, 3 * D), lambda i: (0, 0)), vec],
        out_specs=[row, pl.BlockSpec((8, 128), lambda i: (0, 0)), vec, vec,
                   row512, pl.BlockSpec((tm, N_SLAB * STAT_W), lambda i: (i, 0)), row512, ANY, ANY, ANY, ANY,
                   pl.BlockSpec((4, PGW, PGW), lambda i: (0, 0, 0)), pl.BlockSpec((1, AW), lambda i: (0, 0))],
        out_shape=[_sds((S, D)), _sds((8, 128)), _sds((1, D)), _sds((1, D)),
                   _sds((S, AW)), _sds((S, N_SLAB * STAT_W)), _sds((S, AW)), _sds((S, IN_W), BF16),
                   _sds((D, D)), _sds((N_SHARD, AW, sw)), _sds((N_SHARD, AW, sw)), _sds((4, PGW, PGW)), _sds((1, AW))],
        scratch_shapes=[pltpu.VMEM((D, D), F32), pltpu.VMEM((AW, D), F32), pltpu.VMEM((AW, D), F32),
                        pltpu.VMEM((2, tm, REST_W), BF16), _dma_sems(2)],
        compiler_params=_params("arbitrary"),
    )(*os_, *ls_, rest, rest, wab, wpb, pool_w, pool_scale, x, target, wout, mod, final_g)
    dx2, loss, dfg, dgate, dattn, stats, dpooled, dproj, dwo, dwab, dwpb, dpw, dps = outs
    return (dx2, loss, dfg, dgate, dattn, stats, dpooled, dproj, dwo.reshape(N_SHARD, D // N_SHARD, D), dwab, dwpb,
            dpw, dps)


def _pool_bwd(dpooled, dproj):
    S = dpooled.shape[0]
    tm = 1024
    nt = S // tm

    def body(dp_ref, nxt_ref, _, du_ref):
        i = pl.program_id(0)
        t = i * tm + lax.broadcasted_iota(jnp.int32, (tm + HALO, 1), 0)
        nxt = jnp.where(i < nt - 1, nxt_ref[...], 0.0)
        ext = jnp.concatenate([dp_ref[...], nxt], axis=0)
        for gi, win in enumerate(POOL_WINDOWS):
            cs = slice(gi * PGW, (gi + 1) * PGW)
            s = ext[:, cs] / jnp.minimum(t + 1, win).astype(F32)
            sh = 1
            while sh < win:
                s = s + pltpu.roll(s, tm + HALO - sh, 0)
                sh *= 2
            du_ref[:, cs] = (s[:tm] - dp_ref[:, cs]).astype(BF16)

    return pl.pallas_call(
        body, name="pool_bwd", grid=(nt,),
        in_specs=[pl.BlockSpec((tm, AW), lambda i: (i, 0)),
                  pl.BlockSpec((HALO, AW), lambda i: (jnp.minimum((i + 1) * (tm // HALO), S // HALO - 1), 0)), ANY],
        out_specs=pl.BlockSpec((tm, AW), lambda i: (i, (QKV_W + AW) // AW)),
        out_shape=_sds(dproj.shape, BF16), input_output_aliases={2: 0}, compiler_params=_params("parallel"),
    )(dpooled, dpooled, dproj)


TB = 1024


def _dh_prenorm_bwd(dproj, wg_in, x, dx2, norm_g, mod, ride):
    S = dproj.shape[0]
    per = wg_in.shape[2] // TB
    nm, nk = S // TB, IN_W // TB
    rows = 256

    def body(dp_ref, w_ref, x_ref, dx2_ref, g_ref, mod_ref, gx_ref, dg_ref, dshift_ref, dscale_ref, dh_ref):
        m, kk = pl.program_id(0), pl.program_id(1)

        @pl.when(kk == 0)
        def _():
            dh_ref[...] = jnp.zeros_like(dh_ref)

        @pl.when((m == 0) & (kk == 0))
        def _():
            dg_ref[...] = jnp.zeros_like(dg_ref)
            dshift_ref[...] = jnp.zeros_like(dshift_ref)
            dscale_ref[...] = jnp.zeros_like(dscale_ref)

        dh_ref[...] += _dot_nt(dp_ref[...], w_ref[...])

        @pl.when(kk == nk - 1)
        def _():
            g = g_ref[...]
            for c in range(TB // rows):
                sl = pl.ds(c * rows, rows)
                xv = x_ref[sl, :]
                dhv = dh_ref[sl, :]
                r = lax.rsqrt(jnp.mean(xv * xv, axis=-1, keepdims=True) + EPS)
                xh = xv * r
                dshift_ref[...] += jnp.sum(dhv, axis=0, keepdims=True)
                dscale_ref[...] += jnp.sum(dhv * (xh * g), axis=0, keepdims=True)
                dn1 = dhv * (1.0 + mod_ref[:, D:2 * D])
                dg_ref[...] += jnp.sum(dn1 * xh, axis=0, keepdims=True)
                dxh = dn1 * g
                gx_ref[sl, :] = dx2_ref[sl, :] + r * (dxh - xh * jnp.mean(dxh * xh, axis=-1, keepdims=True))

    row = pl.BlockSpec((TB, D), lambda m, kk: (m, 0))
    vec = pl.BlockSpec((1, D), lambda m, kk: (0, 0))
    outs, rode = _call_with_ride(
        body, ride, lambda: (pl.program_id(0) == 0) & (pl.program_id(1) == 0),
        lambda: (pl.program_id(0) == nm - 1) & (pl.program_id(1) == nk - 1),
        name="dh", grid=(nm, nk),
        in_specs=[pl.BlockSpec((TB, TB), lambda m, kk: (m, kk)),
                  pl.BlockSpec((None, D, TB), lambda m, kk: (kk // per, 0, kk % per)),
                  row, row, vec, pl.BlockSpec((1, 3 * D), lambda m, kk: (0, 0))],
        out_specs=[row, vec, vec, vec],
        out_shape=[_sds((S, D)), _sds((1, D)), _sds((1, D)), _sds((1, D))],
        scratch_shapes=[pltpu.VMEM((TB, D), F32)], compiler_params=_params("arbitrary", "arbitrary"),
    )(dproj, wg_in, x, dx2, norm_g, mod)
    return outs, rode


def _dw_in(h, dproj):
    S = dproj.shape[0]
    per = IN_W // N_SHARD // TB

    def body(h_ref, dp_ref, out_ref):
        out_ref[...] = _dot_tn(h_ref[...], dp_ref[...])

    return pl.pallas_call(
        body, name="dw_in", grid=(IN_W // TB,),
        in_specs=[pl.BlockSpec((S, D), lambda j: (0, 0)), pl.BlockSpec((S, TB), lambda j: (0, j))],
        out_specs=pl.BlockSpec((None, D, TB), lambda j: (j // per, 0, j % per)),
        out_shape=_sds((N_SHARD, D, IN_W // N_SHARD)), compiler_params=_params("parallel"),
    )(h, dproj)


def _local_step(x, target, mod, wg_in, wab, wpb, wout, pool_w, pool_scale, rel_bias, norm_g, final_g, chip_half):
    buckets = jnp.asarray(_bucket_tables())
    bias_tab = _bias_table(rel_bias, buckets)
    h = _prenorm(x, norm_g, mod)
    qkv = [_proj(h, wg_in, 3 * g, 3, BF16 if GROUPS[g][1] == 1 else F32, f"proj_qkv{g}") for g in range(NG)]
    rest = _proj(h, wg_in, NCB_QKV, REST_W // CB, F32, "proj_rest")
    os_, ls_ = zip(*[_attn_fwd(qkv[g], bias_tab, g) for g in range(NG)])
    (dx2, loss, dfinal_g, dgate, dattn, stats, dpooled, dproj, dw_out, dw_ab, dw_pb, dpool_w,
     dpool_scale) = _mix_step(x, target, os_, ls_, rest, wab, wpb, pool_w, pool_scale, wout, mod, final_g)
    dproj = _pool_bwd(dpooled, dproj)

    small = [dw_ab, dw_pb, dw_out]
    dqkv0, ds0, sib_small = _attn_bwd(qkv[0], dattn, stats, bias_tab, 0, _ride_sibling_halves(small))
    p_small = _pair_sum_small(small, sib_small, chip_half)
    dqkv1, ds1, u_small = _attn_bwd(qkv[1], dattn, stats, bias_tab, 1,
                                    _ride_chip_exchange([p16 for _, p16 in p_small]))
    rs_ab, rs_pb, rs_out = _chip_sum_small([p32 for p32, _ in p_small], u_small, chip_half)
    dqkv2, ds2, _ = _attn_bwd(qkv[2], dattn, stats, bias_tab, 2, None)

    for j, piece in enumerate(dqkv0 + dqkv1 + dqkv2):
        dproj = lax.dynamic_update_slice(dproj, piece.astype(BF16), (0, j * AW))
    dw_in = _dw_in(h, dproj)
    drel_rows, (sib_in,) = _bias_grad(jnp.concatenate([ds0, ds1, ds2], axis=0), buckets,
                                      _ride_sibling_halves([dw_in]))
    drel = drel_rows[:, 0, :NUM_BUCKETS].T
    p32_in, p16_in = _pair_sum(dw_in, sib_in, chip_half, "rs_pair_sum_in")
    (grad_x, dnorm_g, dshift, dscale), (u_in,) = _dh_prenorm_bwd(dproj, wg_in, x, dx2, norm_g, mod,
                                                                 _ride_chip_exchange([p16_in]))
    rs_in = _chip_sum(p32_in, u_in, chip_half, "rs_chip_sum_in")
    dmod = jnp.concatenate([dshift, dscale, dgate], axis=1)
    return dict(loss=loss[0, 0], grad_x=grad_x, dmod=dmod, dnorm_g=dnorm_g, dfinal_g=dfinal_g, dpool_w=dpool_w,
                dpool_scale=dpool_scale, drel_bias=drel, dw_in=dw_in, dw_attn_br=dw_ab, dw_pool_br=dw_pb,
                dw_out=dw_out, rs_in=rs_in, rs_attn_br=rs_ab, rs_pool_br=rs_pb, rs_out=rs_out)


def _allgather8(blocks, name, relay=None, join=()):
    nb, nj = len(blocks), len(join)
    relay = [False] * nb if relay is None else list(relay)

    def body(*refs):
        ins, outs, joined = refs[:nb], refs[nb + nj:2 * nb + nj], refs[2 * nb + nj:2 * nb + 2 * nj]
        send_sems, recv_sems, join_send, join_recv = refs[2 * nb + 2 * nj:]
        x, y, c = lax.axis_index("x"), lax.axis_index("y"), lax.axis_index("c")
        me, sibling = (x, y, c), (x, y, 1 - c)
        here, xn, yn, dg = (x, y), (1 - x, y), (x, 1 - y), (1 - x, 1 - y)

        def slot(a, chip, core, half=None):
            ref = outs[a].at[4 * chip[0] + 2 * chip[1] + core]
            if half is None:
                return ref
            r2 = ref.shape[0] // 2
            return ref.at[pl.ds(half * r2, r2)]

        def copy(a, k, dst, to, src=None):
            return pltpu.make_async_remote_copy(src_ref=dst if src is None else src, dst_ref=dst,
                                                send_sem=send_sems.at[a, k], recv_sem=recv_sems.at[a, k],
                                                device_id=to, device_id_type=MESH)

        def start(cps):
            for cp in cps:
                cp.start()
            return cps

        swaps = []
        for a in range(nj):
            r2 = joined[a].shape[0] // 2
            rows = joined[a].at[pl.ds(c * r2, r2), :]
            swaps.append(pltpu.make_async_remote_copy(src_ref=rows, dst_ref=rows, send_sem=join_send.at[a],
                                                      recv_sem=join_recv.at[a], device_id=sibling,
                                                      device_id_type=MESH))
        start(swaps)
        sent = []
        for a in range(nb):
            own = slot(a, here, c)
            sent += [copy(a, 0, own, sibling, src=ins[a]), copy(a, 1, own, (*xn, c), src=ins[a]),
                     copy(a, 2, own, (*yn, c), src=ins[a])]
            if not relay[a]:
                sent.append(copy(a, 3, own, (*dg, c), src=ins[a]))
        start(sent)
        for a in range(nb):
            copy(a, 2, slot(a, yn, c), me).wait_recv()
            sent += start([copy(a, 6, slot(a, yn, c), sibling)]
                          + ([copy(a, 3, slot(a, yn, c, 0), (*xn, c))] if relay[a] else []))
        for a in range(nb):
            copy(a, 1, slot(a, xn, c), me).wait_recv()
            sent += start([copy(a, 5, slot(a, xn, c), sibling)]
                          + ([copy(a, 4, slot(a, xn, c, 1), (*yn, c))] if relay[a] else []))
        for a in range(nb):
            for k, half in ((3, 0), (4, 1)) if relay[a] else ((3, None),):
                copy(a, k, slot(a, dg, c, half), me).wait_recv()
                sent += start([copy(a, 4 + k, slot(a, dg, c, half), sibling)])
        for a in range(nb):
            copy(a, 0, slot(a, here, 1 - c), me).wait_recv()
            copy(a, 5, slot(a, xn, 1 - c), me).wait_recv()
            copy(a, 6, slot(a, yn, 1 - c), me).wait_recv()
            for k, half in ((7, 0), (8, 1)) if relay[a] else ((7, None),):
                copy(a, k, slot(a, dg, 1 - c, half), me).wait_recv()
        for cp in sent:
            cp.wait_send()
        for cp in swaps:
            cp.wait()

    outs = pl.pallas_call(
        body, name=name, in_specs=[ANY] * (nb + nj), out_specs=[ANY] * (nb + nj),
        out_shape=[_sds((8,) + b.shape, b.dtype) for b in blocks] + [_sds(f.shape, f.dtype) for f in join],
        input_output_aliases={nb + a: nb + a for a in range(nj)},
        scratch_shapes=[_dma_sems(nb, 9), _dma_sems(nb, 9), _dma_sems(max(nj, 1)), _dma_sems(max(nj, 1))],
    )(*blocks, *join)
    gathered = [_place_own(buf, b) for buf, b in zip(outs[:nb], blocks)]
    return (gathered, list(outs[nb:])) if nj else gathered


def _place_own(buf, block):
    dev = 4 * lax.axis_index("x") + 2 * lax.axis_index("y") + lax.axis_index("c")
    return lax.dynamic_update_index_in_dim(buf, block, dev, 0)


def _ride_sibling_halves(gs):
    def copies(ins, outs, send_sems, recv_sems):
        x, y, c = lax.axis_index("x"), lax.axis_index("y"), lax.axis_index("c")
        cps = []
        for a in range(len(gs)):
            r2 = ins[a].shape[1] // 2
            other = ins[a].at[:, pl.ds((1 - c) * r2, r2), :]
            cps.append(pltpu.make_async_remote_copy(src_ref=other, dst_ref=outs[a], send_sem=send_sems.at[a],
                                                    recv_sem=recv_sems.at[a], device_id=(x, y, 1 - c),
                                                    device_id_type=MESH))
        return cps

    return _Ride(gs, [_sds((g.shape[0], g.shape[1] // 2, g.shape[2]), g.dtype) for g in gs], len(gs), copies)


def _pair_sum(g, t, chip_half, name):
    nsh, rows, cols = g.shape
    r2 = rows // 2
    tr = _row_tile(r2, cols)
    nt = r2 // tr

    def body(ch_ref, g_ref, t_ref, p32_ref, p16_ref):
        p = g_ref[...] + t_ref[...]
        p16_ref[...] = p.astype(BF16)

        @pl.when(pl.program_id(1) == ch_ref[0])
        def _():
            p32_ref[...] = p

    blk = pl.BlockSpec((None, tr, cols), lambda i, k, ch_ref: (k, i, 0))
    return pl.pallas_call(
        body, name=name,
        grid_spec=pltpu.PrefetchScalarGridSpec(
            num_scalar_prefetch=1, grid=(nt, nsh),
            in_specs=[pl.BlockSpec((None, tr, cols), lambda i, k, ch_ref: (k, ch_ref[1] * nt + i, 0)), blk],
            out_specs=[pl.BlockSpec((tr, cols), lambda i, k, ch_ref: (i, 0)), blk]),
        out_shape=[_sds((r2, cols)), _sds((nsh, r2, cols), BF16)],
        compiler_params=_params("parallel", "arbitrary"),
    )(chip_half, g, t)


def _pair_sum_small(gs, ts, chip_half):
    na = len(gs)

    def body(ch_ref, *refs):
        g_refs, t_refs, outs = refs[:na], refs[na:2 * na], refs[2 * na:]
        for a in range(na):
            r2 = t_refs[a].shape[1]
            own = pl.ds(pl.multiple_of(ch_ref[1] * r2, 8), r2)
            outs[2 * a + 1][...] = (g_refs[a][:, own, :] + t_refs[a][...]).astype(BF16)
            outs[2 * a][...] = g_refs[a][ch_ref[0], own, :] + t_refs[a][ch_ref[0]]

    res = pl.pallas_call(
        body, name="rs_pair_sum_small",
        in_specs=[pl.BlockSpec(memory_space=pltpu.SMEM)] + [pl.BlockSpec(memory_space=pltpu.VMEM)] * (2 * na),
        out_shape=[s for t in ts for s in (_sds(t.shape[1:]), _sds(t.shape, BF16))], compiler_params=_params(),
    )(chip_half, *gs, *ts)
    return [(res[2 * a], res[2 * a + 1]) for a in range(na)]


def _chip_sum_small(p32s, us, chip_half):
    na = len(p32s)

    def body(ch_ref, *refs):
        p_refs, u_refs, outs = refs[:na], refs[na:2 * na], refs[2 * na:]
        for a in range(na):
            r2 = p_refs[a].shape[0]
            acc = p_refs[a][...]
            for j in range(3):
                acc = acc + u_refs[a][j].astype(F32)
            outs[a][pl.ds(pl.multiple_of(ch_ref[1] * r2, 8), r2), :] = acc

    return pl.pallas_call(
        body, name="rs_chip_sum_small",
        in_specs=[pl.BlockSpec(memory_space=pltpu.SMEM)] + [pl.BlockSpec(memory_space=pltpu.VMEM)] * (2 * na),
        out_shape=[_sds((2 * p.shape[0], p.shape[1])) for p in p32s], compiler_params=_params(),
    )(chip_half, *p32s, *us)


def _ride_chip_exchange(ps):
    def copies(ins, outs, send_sems, recv_sems):
        x, y, c = lax.axis_index("x"), lax.axis_index("y"), lax.axis_index("c")
        chips = [(1 - x, y), (x, 1 - y), (1 - x, 1 - y)]
        cps = []
        for a in range(len(ps)):
            for j, (ox, oy) in enumerate(chips):
                cps.append(pltpu.make_async_remote_copy(src_ref=ins[a].at[2 * ox + oy], dst_ref=outs[a].at[j],
                                                        send_sem=send_sems.at[3 * a + j],
                                                        recv_sem=recv_sems.at[3 * a + j],
                                                        device_id=(ox, oy, c), device_id_type=MESH))
        return cps

    return _Ride(ps, [_sds((3,) + p.shape[1:], p.dtype) for p in ps], 3 * len(ps), copies)


def _chip_sum(p32, u, chip_half, name):
    r2, cols = p32.shape
    tr = _row_tile(r2, cols)
    nt = r2 // tr

    def body(ch_ref, p_ref, u_ref, o_ref):
        acc = p_ref[...]
        for j in range(3):
            acc = acc + u_ref[j].astype(F32)
        o_ref[...] = acc

    return pl.pallas_call(
        body, name=name,
        grid_spec=pltpu.PrefetchScalarGridSpec(
            num_scalar_prefetch=1, grid=(nt,),
            in_specs=[pl.BlockSpec((tr, cols), lambda i, ch_ref: (i, 0)),
                      pl.BlockSpec((3, tr, cols), lambda i, ch_ref: (0, i, 0))],
            out_specs=pl.BlockSpec((tr, cols), lambda i, ch_ref: (ch_ref[1] * nt + i, 0))),
        out_shape=_sds((2 * r2, cols)), compiler_params=_params("parallel"),
    )(chip_half, p32, u)


def _row_tile(rows, cols):
    tile = rows
    while tile * cols * 4 > (2 << 20) and tile % 16 == 0:
        tile //= 2
    return tile


def _w_ada_grad(c_all, dmod_cols):
    def body(c_ref, d_ref, o_ref):
        o_ref[...] = _dot_tn(c_ref[...].astype(BF16), d_ref[...].astype(BF16))

    return pl.pallas_call(body, name="w_ada_grad", out_shape=_sds((c_all.shape[1], dmod_cols.shape[1])),
                          compiler_params=_params())(c_all, dmod_cols)


def _adam_math(w, g, m, v):
    nm = ADAM_B1 * m + (1.0 - ADAM_B1) * g
    nv = ADAM_B2 * v + (1.0 - ADAM_B2) * (g * g)
    m_hat = nm / (1.0 - ADAM_B1 ** ADAM_STEP)
    v_hat = nv / (1.0 - ADAM_B2 ** ADAM_STEP)
    return -ADAM_LR * (m_hat / (jnp.sqrt(v_hat) + ADAM_EPS) + ADAM_WD * w), nm, nv


def _adamw(w, g, m, v, name):
    rows, cols = w.shape
    tr = _row_tile(rows, cols)

    def body(w_ref, g_ref, m_ref, v_ref, go_ref, d_ref, nm_ref, nv_ref):
        gv = g_ref[...]
        go_ref[...] = gv
        d_ref[...], nm_ref[...], nv_ref[...] = _adam_math(w_ref[...], gv, m_ref[...], v_ref[...])

    spec = pl.BlockSpec((tr, cols), lambda i: (i, 0))
    return pl.pallas_call(
        body, name=name, grid=(rows // tr,), in_specs=[spec] * 4, out_specs=[spec] * 4,
        out_shape=[_sds((rows, cols))] * 4, compiler_params=_params("parallel"),
    )(w, g, m, v)


def _pack_small(dmod, dnorm_g, dfinal_g, dpool_scale, drel_bias, loss, dpool_w):
    return jnp.concatenate([dmod.reshape(-1, 128), dnorm_g.reshape(-1, 128), dfinal_g.reshape(-1, 128),
                            jnp.pad(dpool_scale.reshape(-1, 128), ((0, PK_RELB - PK_PSCALE - AW // 128), (0, 0))),
                            jnp.pad(drel_bias, ((0, 0), (0, 128 - NG * NH))),
                            jnp.full((PK_POOLW - PK_LOSS, 128), loss, F32), dpool_w.reshape(-1, 128)], axis=0)


def _small_update(small_all, ws, ms, vs):
    lane_rows = [(r0, r0 + w.shape[1] // 128) for r0, w in zip((PK_BADA, PK_NORMG, PK_FINALG, PK_PSCALE), ws)]
    nw = len(ws)

    def body(all_ref, *refs):
        w_refs, m_refs, v_refs = refs[:nw], refs[nw:2 * nw], refs[2 * nw:3 * nw]
        loss_ref, outs = refs[3 * nw], refs[3 * nw + 1:]
        g = all_ref[0]
        for s in range(1, all_ref.shape[0]):
            g = g + all_ref[s]
        loss_ref[...] = jnp.broadcast_to(g[PK_LOSS:PK_LOSS + 1, :], loss_ref.shape)

        def put(p, at, gv):
            d, nm, nv = _adam_math(w_refs[p][at], gv, m_refs[p][at], v_refs[p][at])
            for o_ref, val in zip(outs[4 * p:4 * p + 4], (gv, d, nm, nv)):
                o_ref[at] = val

        for p, (r0, r1) in enumerate(lane_rows):
            for i in range(r1 - r0):
                put(p, (slice(None), slice(128 * i, 128 * (i + 1))), g[r0 + i:r0 + i + 1, :])
        put(4, (slice(None), slice(None)), g[PK_RELB:PK_LOSS, 0:NG * NH])
        put(5, (slice(None), slice(None)), g[PK_POOLW:PK_ROWS, :])

    res = pl.pallas_call(
        body, name="small_update",
        out_shape=[_sds((8, 128))] + [_sds(w.shape) for w in ws for _ in range(4)], compiler_params=_params(),
    )(small_all, *ws, *ms, *vs)
    return res[0], [res[1 + 4 * p:5 + 4 * p] for p in range(nw)]


def kernel(x, c, norm_g, w_ada, b_ada, w_in, pool_w, pool_scale, w_attn_br, w_pool_br, w_out, rel_bias, final_g, loss_target, m_norm_g, m_w_ada, m_b_ada, m_w_in, m_pool_w, m_pool_scale, m_w_attn_br, m_w_pool_br, m_w_out, m_rel_bias, m_final_g, v_norm_g, v_w_ada, v_b_ada, v_w_in, v_pool_w, v_pool_scale, v_w_attn_br, v_w_pool_br, v_w_out, v_rel_bias, v_final_g):
    ix, iy, ic = lax.axis_index("x"), lax.axis_index("y"), lax.axis_index("c")
    dev = 4 * ix + 2 * iy + ic
    chip = 2 * ix + iy

    def half(w):
        r2 = w.shape[0] // 2
        return lax.dynamic_slice_in_dim(w, ic * r2, r2, axis=0).astype(BF16)

    gathered = _allgather8([jnp.broadcast_to(c, (8, D)), half(w_in[0]), half(w_attn_br[0]), half(w_pool_br[0]),
                            half(w_out[0])], "gather_weights", relay=[False, True, True, True, True])
    c_all = gathered[0][:, 0, :]
    wg_in = gathered[1].reshape(N_SHARD, D, IN_W // N_SHARD)
    wab = gathered[2].reshape(N_SHARD, AW, D // N_SHARD).transpose(1, 0, 2).reshape(AW, D)
    wpb = gathered[3].reshape(N_SHARD, AW, D // N_SHARD).transpose(1, 0, 2).reshape(AW, D)
    wout = gathered[4].reshape(D, D)

    mw = 3 * D // N_SHARD
    modp = _mod_partial(c_all, w_ada[0], lax.dynamic_slice_in_dim(b_ada, chip * mw, mw, axis=1))
    mod_all = _allgather8([modp], "gather_mod")[0]
    mod_full = mod_all[::2].transpose(1, 0, 2).reshape(8, 3 * D)
    mod = lax.dynamic_slice_in_dim(mod_full, dev, 1, axis=0)

    chip_half = jnp.stack([chip, ic]).astype(jnp.int32)
    r = _local_step(x[0], loss_target[0], mod, wg_in, wab, wpb, wout, pool_w[0], pool_scale, rel_bias, norm_g,
                    final_g.reshape(1, D), chip_half)

    packed = _pack_small(r["dmod"], r["dnorm_g"], r["dfinal_g"], r["dpool_scale"], r["drel_bias"], r["loss"],
                         r["dpool_w"])
    (small_all,), (g_w_in, g_w_ab, g_w_pb, g_w_out) = _allgather8(
        [packed], "gather_small", join=[r["rs_in"], r["rs_attn_br"], r["rs_pool_br"], r["rs_out"]])
    small = ["b_ada", "norm_g", "final_g", "pool_scale", "rel_bias", "pool_w"]
    shaped = lambda b, n, f, ps, rb, pw: [b, n, f.reshape(1, D), ps, rb, pw.reshape(4 * PGW, PGW)]
    loss, small_out = _small_update(small_all, shaped(b_ada, norm_g, final_g, pool_scale, rel_bias, pool_w),
                                    shaped(m_b_ada, m_norm_g, m_final_g, m_pool_scale, m_rel_bias, m_pool_w),
                                    shaped(v_b_ada, v_norm_g, v_final_g, v_pool_scale, v_rel_bias, v_pool_w))
    dmod_all = small_all[:, PK_BADA:PK_NORMG, :].reshape(8, 3 * D)
    g_w_ada = _w_ada_grad(c_all, lax.dynamic_slice_in_dim(dmod_all, chip * mw, mw, axis=1))

    upd = dict(zip(small, small_out))
    upd["final_g"] = [a.reshape(D) for a in upd["final_g"]]
    upd["pool_w"] = [a.reshape(1, 4, PGW, PGW) for a in upd["pool_w"]]
    for nme, w, g, m, v in (("w_ada", w_ada, g_w_ada, m_w_ada, v_w_ada), ("w_in", w_in, g_w_in, m_w_in, v_w_in),
                            ("w_attn_br", w_attn_br, g_w_ab, m_w_attn_br, v_w_attn_br),
                            ("w_pool_br", w_pool_br, g_w_pb, m_w_pool_br, v_w_pool_br),
                            ("w_out", w_out, g_w_out, m_w_out, v_w_out)):
        upd[nme] = [a[None] for a in _adamw(w[0], g, m[0], v[0], "adamw_" + nme)]
    names = ["norm_g", "w_ada", "b_ada", "w_in", "pool_w", "pool_scale", "w_attn_br", "w_pool_br", "w_out",
             "rel_bias", "final_g"]
    return (loss[0, 0], r["grad_x"][None]) + tuple(upd[nme][kind] for kind in range(4) for nme in names)
```
